```python
import math
import jax, jax.numpy as jnp
from jax import lax
import numpy as np

D_MODEL = 2048
BATCH = 8
SEQ = 2048
DEPTH = 1

CHUNK = 64
Q_BLOCK = 128
D_SSM = D_MODEL // 2
SSD_HEADDIM = 64
SSD_HEADS = D_SSM // SSD_HEADDIM
SSD_GROUPS = 2
SSD_STATE = 128
SSD_CONV = 4
SSD_CONV_DIM = D_SSM + 2 * SSD_GROUPS * SSD_STATE
MLA_V = 128
MLA_HEADS = (D_MODEL - D_SSM) // MLA_V
MLA_NOPE = 128
MLA_ROPE = 64
MLA_Q_RANK = 512
MLA_KV_RANK = 256
ROPE_THETA = 10000.0
D_IN_PROJ = D_SSM + SSD_CONV_DIM + SSD_HEADS + MLA_Q_RANK + MLA_KV_RANK + MLA_ROPE
D_FF = 5632
FFN_CONV = 3
PLE_DIM = 256
NORM_EPS = 1e-6

kernel_name = "hybrid_ssd_mla_convffn_ple_block"


def rms_norm(x, w):
    xf = x.astype(jnp.float32)
    y = xf * lax.rsqrt(jnp.mean(xf * xf, axis=-1, keepdims=True) + NORM_EPS)
    return y.astype(x.dtype) * w


def causal_dwconv(u, w, b):
    k = w.shape[0]
    out = lax.conv_general_dilated(
        u, w[:, None, :].astype(u.dtype), window_strides=(1,), padding=[(k - 1, 0)],
        dimension_numbers=("NWC", "WIO", "NWC"), feature_group_count=u.shape[-1])
    return out + b


def ssd_chunked(xh, dt, a, bm, cm):
    bsz, s, g, r, pdim = xh.shape
    n = bm.shape[-1]
    nc = s // CHUNK
    xdt = (xh * dt[..., None]).reshape(bsz, nc, CHUNK, g, r, pdim)
    adt = jnp.moveaxis((dt * a).reshape(bsz, nc, CHUNK, g, r), (3, 4), (1, 2))
    a_cs = jnp.cumsum(adt, axis=-1)
    bc = bm.reshape(bsz, nc, CHUNK, g, n)
    cc = cm.reshape(bsz, nc, CHUNK, g, n)
    seg = a_cs[..., :, None] - a_cs[..., None, :]
    tri = jnp.tril(jnp.ones((CHUNK, CHUNK), dtype=bool))
    lmat = jnp.exp(jnp.where(tri, seg, -jnp.inf))
    y_diag = jnp.einsum("bclgn,bcsgn,bgrcls,bcsgrp->bclgrp", cc, bc, lmat, xdt)
    decay_states = jnp.exp(a_cs[..., -1:] - a_cs)
    states = jnp.einsum("bclgn,bgrcl,bclgrp->bcgrpn", bc, decay_states, xdt)
    chunk_decay = jnp.moveaxis(jnp.exp(a_cs[..., -1]), -1, 0)

    def step(h, inp):
        st, dec = inp
        return h * dec[..., None, None] + st, h

    h0 = jnp.zeros((bsz, g, r, pdim, n), jnp.float32)
    _, prev = lax.scan(step, h0, (jnp.moveaxis(states, 1, 0), chunk_decay))
    prev = jnp.moveaxis(prev, 0, 1)
    y_off = jnp.einsum("bclgn,bcgrpn,bgrcl->bclgrp", cc, prev, jnp.exp(a_cs))
    return (y_diag + y_off).reshape(bsz, s, g, r, pdim)


def ssd_mixer(z, xbc, dt_raw, conv_w, conv_b, dt_bias, a_log, d_skip, norm_w):
    bsz, s = z.shape[:2]
    r = SSD_HEADS // SSD_GROUPS
    xbc = jax.nn.silu(causal_dwconv(xbc, conv_w, conv_b))
    xs, bm, cm = jnp.split(xbc, [D_SSM, D_SSM + SSD_GROUPS * SSD_STATE], axis=-1)
    xh = xs.reshape(bsz, s, SSD_GROUPS, r, SSD_HEADDIM).astype(jnp.float32)
    dt = jax.nn.softplus(dt_raw.astype(jnp.float32) + dt_bias.astype(jnp.float32))
    dt = dt.reshape(bsz, s, SSD_GROUPS, r)
    a = -jnp.exp(a_log.astype(jnp.float32)).reshape(SSD_GROUPS, r)
    y = ssd_chunked(xh, dt, a,
                    bm.reshape(bsz, s, SSD_GROUPS, SSD_STATE).astype(jnp.float32),
                    cm.reshape(bsz, s, SSD_GROUPS, SSD_STATE).astype(jnp.float32))
    y = y + d_skip.astype(jnp.float32).reshape(SSD_GROUPS, r)[..., None] * xh
    y = y.reshape(bsz, s, D_SSM) * jax.nn.silu(z.astype(jnp.float32))
    yg = y.reshape(bsz, s, SSD_GROUPS, D_SSM // SSD_GROUPS)
    yg = yg * lax.rsqrt(jnp.mean(yg * yg, axis=-1, keepdims=True) + NORM_EPS)
    return yg.reshape(bsz, s, D_SSM).astype(z.dtype) * norm_w


def apply_rope(t, cos, sin):
    half = t.shape[-1] // 2
    t1, t2 = t[..., :half], t[..., half:]
    return jnp.concatenate([t1 * cos - t2 * sin, t1 * sin + t2 * cos], axis=-1).astype(t.dtype)


def mla_mixer(q_a, kv_a, cos, sin, q_a_norm_w, w_q_b, kv_a_norm_w, w_kv_b):
    bsz, s = q_a.shape[:2]
    q = (rms_norm(q_a, q_a_norm_w) @ w_q_b).reshape(bsz, s, MLA_HEADS, MLA_NOPE + MLA_ROPE)
    q_nope, q_rope = q[..., :MLA_NOPE], q[..., MLA_NOPE:]
    q_rope = apply_rope(q_rope, cos[:, :, None, :], sin[:, :, None, :])
    c_kv, k_rope = kv_a[..., :MLA_KV_RANK], kv_a[..., MLA_KV_RANK:]
    k_rope = apply_rope(k_rope, cos, sin)
    kv = (rms_norm(c_kv, kv_a_norm_w) @ w_kv_b).reshape(bsz, s, MLA_HEADS, MLA_NOPE + MLA_V)
    k_nope, v = kv[..., :MLA_NOPE], kv[..., MLA_NOPE:]
    scale = 1.0 / math.sqrt(MLA_NOPE + MLA_ROPE)
    nb = s // Q_BLOCK
    k_chunk = jnp.arange(s) // CHUNK

    def attend_block(args):
        qn, qr, start = args
        sc = (jnp.einsum("bqhd,bkhd->bhqk", qn, k_nope)
              + jnp.einsum("bqhr,bkr->bhqk", qr, k_rope)).astype(jnp.float32) * scale
        q_chunk = (start + jnp.arange(Q_BLOCK)) // CHUNK
        mask = k_chunk[None, :] <= q_chunk[:, None]
        sc = jnp.where(mask, sc, -jnp.inf)
        prob = jax.nn.softmax(sc, axis=-1).astype(v.dtype)
        return jnp.einsum("bhqk,bkhd->bqhd", prob, v)

    qn_b = jnp.moveaxis(q_nope.reshape(bsz, nb, Q_BLOCK, MLA_HEADS, MLA_NOPE), 1, 0)
    qr_b = jnp.moveaxis(q_rope.reshape(bsz, nb, Q_BLOCK, MLA_HEADS, MLA_ROPE), 1, 0)
    starts = jnp.arange(nb) * Q_BLOCK
    out = lax.map(attend_block, (qn_b, qr_b, starts))
    return jnp.moveaxis(out, 0, 1).reshape(bsz, s, MLA_HEADS * MLA_V)


def conv_ffn(h, w_up, conv_w, conv_b, w_down):
    u = causal_dwconv(h @ w_up, conv_w, conv_b)
    gate, up = u[..., :D_FF], u[..., D_FF:]
    return (jax.nn.silu(gate) * up) @ w_down


def _fwd_setup_inputs(seed: int = 0) -> dict:
    key = jax.random.key(seed)
    ks = jax.random.split(key, 32)
    f32 = jnp.float32

    def nrm(k, shape, scale):
        return jax.random.normal(k, shape, f32) * scale

    def gain(k, n):
        return 1.0 + 0.02 * jax.random.normal(k, (DEPTH, n), f32)

    x = jax.random.normal(ks[0], (BATCH, SEQ, D_MODEL), f32)
    p = jax.random.normal(ks[1], (DEPTH, BATCH, SEQ, PLE_DIM), f32)
    offsets = jax.random.randint(ks[2], (BATCH, 1), 0, 64) * CHUNK
    positions = (offsets + jnp.arange(SEQ)[None, :]).astype(jnp.int32)
    dt0 = jnp.exp(jax.random.uniform(ks[3], (DEPTH, SSD_HEADS), f32)
                  * (math.log(0.1) - math.log(0.001)) + math.log(0.001))
    dt_bias = dt0 + jnp.log(-jnp.expm1(-dt0))
    a_log = jnp.log(jax.random.uniform(ks[4], (DEPTH, SSD_HEADS), f32, 1.0, 16.0))
    return {
        "x": x,
        "p": p,
        "positions": positions,
        "mix_norm_w": gain(ks[5], D_MODEL),
        "w_in": nrm(ks[6], (DEPTH, D_MODEL, D_IN_PROJ), D_MODEL ** -0.5),
        "conv_w": nrm(ks[7], (DEPTH, SSD_CONV, SSD_CONV_DIM), SSD_CONV ** -0.5),
        "conv_b": nrm(ks[8], (DEPTH, SSD_CONV_DIM), 0.01),
        "dt_bias": dt_bias,
        "a_log": a_log,
        "d_skip": 1.0 + 0.1 * jax.random.normal(ks[9], (DEPTH, SSD_HEADS), f32),
        "ssd_norm_w": gain(ks[10], D_SSM),
        "q_a_norm_w": gain(ks[11], MLA_Q_RANK),
        "w_q_b": nrm(ks[12], (DEPTH, MLA_Q_RANK, MLA_HEADS * (MLA_NOPE + MLA_ROPE)), MLA_Q_RANK ** -0.5),
        "kv_a_norm_w": gain(ks[13], MLA_KV_RANK),
        "w_kv_b": nrm(ks[14], (DEPTH, MLA_KV_RANK, MLA_HEADS * (MLA_NOPE + MLA_V)), MLA_KV_RANK ** -0.5),
        "w_out": nrm(ks[15], (DEPTH, D_SSM + MLA_HEADS * MLA_V, D_MODEL), D_MODEL ** -0.5),
        "ffn_norm_w": gain(ks[16], D_MODEL),
        "w_ffn_up": nrm(ks[17], (DEPTH, D_MODEL, 2 * D_FF), D_MODEL ** -0.5),
        "ffn_conv_w": nrm(ks[18], (DEPTH, FFN_CONV, 2 * D_FF), FFN_CONV ** -0.5),
        "ffn_conv_b": nrm(ks[19], (DEPTH, 2 * D_FF), 0.01),
        "w_ffn_down": nrm(ks[20], (DEPTH, D_FF, D_MODEL), D_FF ** -0.5),
        "ple_norm_w": gain(ks[21], D_MODEL),
        "w_ple_gate": nrm(ks[22], (DEPTH, D_MODEL, D_MODEL), D_MODEL ** -0.5),
        "b_ple_gate": nrm(ks[23], (DEPTH, D_MODEL), 0.01),
        "w_ple_proj": nrm(ks[24], (DEPTH, PLE_DIM, D_MODEL), PLE_DIM ** -0.5),
        "ple_post_norm_w": gain(ks[25], D_MODEL),
        "final_norm_w": 1.0 + 0.02 * jax.random.normal(ks[26], (D_MODEL,), f32),
    }


def _fwd_reference(x, p, positions, mix_norm_w, w_in, conv_w, conv_b, dt_bias, a_log, d_skip,
              ssd_norm_w, q_a_norm_w, w_q_b, kv_a_norm_w, w_kv_b, w_out, ffn_norm_w,
              w_ffn_up, ffn_conv_w, ffn_conv_b, w_ffn_down, ple_norm_w, w_ple_gate,
              b_ple_gate, w_ple_proj, ple_post_norm_w, final_norm_w):
    inv_freq = ROPE_THETA ** (-jnp.arange(0, MLA_ROPE, 2, dtype=jnp.float32) / MLA_ROPE)
    ang = positions.astype(jnp.float32)[..., None] * inv_freq
    cos, sin = jnp.cos(ang), jnp.sin(ang)
    splits = [D_SSM, D_SSM + SSD_CONV_DIM, D_SSM + SSD_CONV_DIM + SSD_HEADS,
              D_SSM + SSD_CONV_DIM + SSD_HEADS + MLA_Q_RANK]
    for i in range(DEPTH):
        h = rms_norm(x, mix_norm_w[i])
        z, xbc, dt_raw, q_a, kv_a = jnp.split(h @ w_in[i], splits, axis=-1)
        y_ssd = ssd_mixer(z, xbc, dt_raw, conv_w[i], conv_b[i], dt_bias[i], a_log[i],
                          d_skip[i], ssd_norm_w[i])
        y_mla = mla_mixer(q_a, kv_a, cos, sin, q_a_norm_w[i], w_q_b[i],
                          kv_a_norm_w[i], w_kv_b[i])
        x = x + jnp.concatenate([y_ssd, y_mla], axis=-1) @ w_out[i]
        x = x + conv_ffn(rms_norm(x, ffn_norm_w[i]), w_ffn_up[i], ffn_conv_w[i],
                         ffn_conv_b[i], w_ffn_down[i])
        gate = jax.nn.sigmoid(rms_norm(x, ple_norm_w[i]) @ w_ple_gate[i] + b_ple_gate[i])
        x = x + gate * rms_norm(p[i] @ w_ple_proj[i], ple_post_norm_w[i])
    return rms_norm(x, final_norm_w)


import jax as _jax
import jax.numpy as _jnp

TWIN_FORMAT = 'train_step'
FWD_PARAMS = ['x', 'p', 'positions', 'mix_norm_w', 'w_in', 'conv_w', 'conv_b', 'dt_bias', 'a_log', 'd_skip', 'ssd_norm_w', 'q_a_norm_w', 'w_q_b', 'kv_a_norm_w', 'w_kv_b', 'w_out', 'ffn_norm_w', 'w_ffn_up', 'ffn_conv_w', 'ffn_conv_b', 'w_ffn_down', 'ple_norm_w', 'w_ple_gate', 'b_ple_gate', 'w_ple_proj', 'ple_post_norm_w', 'final_norm_w']
TWIN_WEIGHTS = ['mix_norm_w', 'w_in', 'conv_w', 'conv_b', 'dt_bias', 'a_log', 'd_skip', 'ssd_norm_w', 'q_a_norm_w', 'w_q_b', 'kv_a_norm_w', 'w_kv_b', 'w_out', 'ffn_norm_w', 'w_ffn_up', 'ffn_conv_w', 'ffn_conv_b', 'w_ffn_down', 'ple_norm_w', 'w_ple_gate', 'b_ple_gate', 'w_ple_proj', 'ple_post_norm_w', 'final_norm_w']
TWIN_DIFF_INPUT = 'x'
TWIN_INPUTS = ['x', 'p', 'positions', 'mix_norm_w', 'w_in', 'conv_w', 'conv_b', 'dt_bias', 'a_log', 'd_skip', 'ssd_norm_w', 'q_a_norm_w', 'w_q_b', 'kv_a_norm_w', 'w_kv_b', 'w_out', 'ffn_norm_w', 'w_ffn_up', 'ffn_conv_w', 'ffn_conv_b', 'w_ffn_down', 'ple_norm_w', 'w_ple_gate', 'b_ple_gate', 'w_ple_proj', 'ple_post_norm_w', 'final_norm_w', 'loss_target', 'm_mix_norm_w', 'm_w_in', 'm_conv_w', 'm_conv_b', 'm_dt_bias', 'm_a_log', 'm_d_skip', 'm_ssd_norm_w', 'm_q_a_norm_w', 'm_w_q_b', 'm_kv_a_norm_w', 'm_w_kv_b', 'm_w_out', 'm_ffn_norm_w', 'm_w_ffn_up', 'm_ffn_conv_w', 'm_ffn_conv_b', 'm_w_ffn_down', 'm_ple_norm_w', 'm_w_ple_gate', 'm_b_ple_gate', 'm_w_ple_proj', 'm_ple_post_norm_w', 'm_final_norm_w', 'v_mix_norm_w', 'v_w_in', 'v_conv_w', 'v_conv_b', 'v_dt_bias', 'v_a_log', 'v_d_skip', 'v_ssd_norm_w', 'v_q_a_norm_w', 'v_w_q_b', 'v_kv_a_norm_w', 'v_w_kv_b', 'v_w_out', 'v_ffn_norm_w', 'v_w_ffn_up', 'v_ffn_conv_w', 'v_ffn_conv_b', 'v_w_ffn_down', 'v_ple_norm_w', 'v_w_ple_gate', 'v_b_ple_gate', 'v_w_ple_proj', 'v_ple_post_norm_w', 'v_final_norm_w']
TWIN_OUTPUTS = ['loss', 'grad_x', 'grad_mix_norm_w', 'grad_w_in', 'grad_conv_w', 'grad_conv_b', 'grad_dt_bias', 'grad_a_log', 'grad_d_skip', 'grad_ssd_norm_w', 'grad_q_a_norm_w', 'grad_w_q_b', 'grad_kv_a_norm_w', 'grad_w_kv_b', 'grad_w_out', 'grad_ffn_norm_w', 'grad_w_ffn_up', 'grad_ffn_conv_w', 'grad_ffn_conv_b', 'grad_w_ffn_down', 'grad_ple_norm_w', 'grad_w_ple_gate', 'grad_b_ple_gate', 'grad_w_ple_proj', 'grad_ple_post_norm_w', 'grad_final_norm_w', 'delta_mix_norm_w', 'delta_w_in', 'delta_conv_w', 'delta_conv_b', 'delta_dt_bias', 'delta_a_log', 'delta_d_skip', 'delta_ssd_norm_w', 'delta_q_a_norm_w', 'delta_w_q_b', 'delta_kv_a_norm_w', 'delta_w_kv_b', 'delta_w_out', 'delta_ffn_norm_w', 'delta_w_ffn_up', 'delta_ffn_conv_w', 'delta_ffn_conv_b', 'delta_w_ffn_down', 'delta_ple_norm_w', 'delta_w_ple_gate', 'delta_b_ple_gate', 'delta_w_ple_proj', 'delta_ple_post_norm_w', 'delta_final_norm_w', 'new_m_mix_norm_w', 'new_m_w_in', 'new_m_conv_w', 'new_m_conv_b', 'new_m_dt_bias', 'new_m_a_log', 'new_m_d_skip', 'new_m_ssd_norm_w', 'new_m_q_a_norm_w', 'new_m_w_q_b', 'new_m_kv_a_norm_w', 'new_m_w_kv_b', 'new_m_w_out', 'new_m_ffn_norm_w', 'new_m_w_ffn_up', 'new_m_ffn_conv_w', 'new_m_ffn_conv_b', 'new_m_w_ffn_down', 'new_m_ple_norm_w', 'new_m_w_ple_gate', 'new_m_b_ple_gate', 'new_m_w_ple_proj', 'new_m_ple_post_norm_w', 'new_m_final_norm_w', 'new_v_mix_norm_w', 'new_v_w_in', 'new_v_conv_w', 'new_v_conv_b', 'new_v_dt_bias', 'new_v_a_log', 'new_v_d_skip', 'new_v_ssd_norm_w', 'new_v_q_a_norm_w', 'new_v_w_q_b', 'new_v_kv_a_norm_w', 'new_v_w_kv_b', 'new_v_w_out', 'new_v_ffn_norm_w', 'new_v_w_ffn_up', 'new_v_ffn_conv_w', 'new_v_ffn_conv_b', 'new_v_w_ffn_down', 'new_v_ple_norm_w', 'new_v_w_ple_gate', 'new_v_b_ple_gate', 'new_v_w_ple_proj', 'new_v_ple_post_norm_w', 'new_v_final_norm_w']
TWIN_LEAF_KINDS = {'loss': 'loss', 'grad_x': 'grad_x', 'grad_mix_norm_w': 'grad_w', 'grad_w_in': 'grad_w', 'grad_conv_w': 'grad_w', 'grad_conv_b': 'grad_w', 'grad_dt_bias': 'grad_w', 'grad_a_log': 'grad_w', 'grad_d_skip': 'grad_w', 'grad_ssd_norm_w': 'grad_w', 'grad_q_a_norm_w': 'grad_w', 'grad_w_q_b': 'grad_w', 'grad_kv_a_norm_w': 'grad_w', 'grad_w_kv_b': 'grad_w', 'grad_w_out': 'grad_w', 'grad_ffn_norm_w': 'grad_w', 'grad_w_ffn_up': 'grad_w', 'grad_ffn_conv_w': 'grad_w', 'grad_ffn_conv_b': 'grad_w', 'grad_w_ffn_down': 'grad_w', 'grad_ple_norm_w': 'grad_w', 'grad_w_ple_gate': 'grad_w', 'grad_b_ple_gate': 'grad_w', 'grad_w_ple_proj': 'grad_w', 'grad_ple_post_norm_w': 'grad_w', 'grad_final_norm_w': 'grad_w', 'delta_mix_norm_w': 'delta_w', 'delta_w_in': 'delta_w', 'delta_conv_w': 'delta_w', 'delta_conv_b': 'delta_w', 'delta_dt_bias': 'delta_w', 'delta_a_log': 'delta_w', 'delta_d_skip': 'delta_w', 'delta_ssd_norm_w': 'delta_w', 'delta_q_a_norm_w': 'delta_w', 'delta_w_q_b': 'delta_w', 'delta_kv_a_norm_w': 'delta_w', 'delta_w_kv_b': 'delta_w', 'delta_w_out': 'delta_w', 'delta_ffn_norm_w': 'delta_w', 'delta_w_ffn_up': 'delta_w', 'delta_ffn_conv_w': 'delta_w', 'delta_ffn_conv_b': 'delta_w', 'delta_w_ffn_down': 'delta_w', 'delta_ple_norm_w': 'delta_w', 'delta_w_ple_gate': 'delta_w', 'delta_b_ple_gate': 'delta_w', 'delta_w_ple_proj': 'delta_w', 'delta_ple_post_norm_w': 'delta_w', 'delta_final_norm_w': 'delta_w', 'new_m_mix_norm_w': 'new_m', 'new_m_w_in': 'new_m', 'new_m_conv_w': 'new_m', 'new_m_conv_b': 'new_m', 'new_m_dt_bias': 'new_m', 'new_m_a_log': 'new_m', 'new_m_d_skip': 'new_m', 'new_m_ssd_norm_w': 'new_m', 'new_m_q_a_norm_w': 'new_m', 'new_m_w_q_b': 'new_m', 'new_m_kv_a_norm_w': 'new_m', 'new_m_w_kv_b': 'new_m', 'new_m_w_out': 'new_m', 'new_m_ffn_norm_w': 'new_m', 'new_m_w_ffn_up': 'new_m', 'new_m_ffn_conv_w': 'new_m', 'new_m_ffn_conv_b': 'new_m', 'new_m_w_ffn_down': 'new_m', 'new_m_ple_norm_w': 'new_m', 'new_m_w_ple_gate': 'new_m', 'new_m_b_ple_gate': 'new_m', 'new_m_w_ple_proj': 'new_m', 'new_m_ple_post_norm_w': 'new_m', 'new_m_final_norm_w': 'new_m', 'new_v_mix_norm_w': 'new_v', 'new_v_w_in': 'new_v', 'new_v_conv_w': 'new_v', 'new_v_conv_b': 'new_v', 'new_v_dt_bias': 'new_v', 'new_v_a_log': 'new_v', 'new_v_d_skip': 'new_v', 'new_v_ssd_norm_w': 'new_v', 'new_v_q_a_norm_w': 'new_v', 'new_v_w_q_b': 'new_v', 'new_v_kv_a_norm_w': 'new_v', 'new_v_w_kv_b': 'new_v', 'new_v_w_out': 'new_v', 'new_v_ffn_norm_w': 'new_v', 'new_v_w_ffn_up': 'new_v', 'new_v_ffn_conv_w': 'new_v', 'new_v_ffn_conv_b': 'new_v', 'new_v_w_ffn_down': 'new_v', 'new_v_ple_norm_w': 'new_v', 'new_v_w_ple_gate': 'new_v', 'new_v_b_ple_gate': 'new_v', 'new_v_w_ple_proj': 'new_v', 'new_v_ple_post_norm_w': 'new_v', 'new_v_final_norm_w': 'new_v'}


def _forward(args):
    return _fwd_reference(*[args[k] for k in FWD_PARAMS])


def _output_shape():
    out = _jax.eval_shape(lambda: _forward(_fwd_setup_inputs(0)))
    return out.shape, out.dtype

N_MICROBATCH = 1
ADAM_LR = 0.001
ADAM_B1 = 0.9
ADAM_B2 = 0.999
ADAM_EPS = 1e-08
ADAM_WD = 0.01
ADAM_STEP = 10
PER_EXAMPLE_BATCH_AXIS = {'x': 0, 'p': 1, 'positions': 0, 'loss_target': 0}
SHARED_INPUTS = []
_WEIGHT_DTYPES = {'mix_norm_w': _jnp.float32, 'w_in': _jnp.float32, 'conv_w': _jnp.float32, 'conv_b': _jnp.float32, 'dt_bias': _jnp.float32, 'a_log': _jnp.float32, 'd_skip': _jnp.float32, 'ssd_norm_w': _jnp.float32, 'q_a_norm_w': _jnp.float32, 'w_q_b': _jnp.float32, 'kv_a_norm_w': _jnp.float32, 'w_kv_b': _jnp.float32, 'w_out': _jnp.float32, 'ffn_norm_w': _jnp.float32, 'w_ffn_up': _jnp.float32, 'ffn_conv_w': _jnp.float32, 'ffn_conv_b': _jnp.float32, 'w_ffn_down': _jnp.float32, 'ple_norm_w': _jnp.float32, 'w_ple_gate': _jnp.float32, 'b_ple_gate': _jnp.float32, 'w_ple_proj': _jnp.float32, 'ple_post_norm_w': _jnp.float32, 'final_norm_w': _jnp.float32}
MOMENT_SCALE = {'mix_norm_w': 5.823653e-02, 'w_in': 4.450401e-02, 'conv_w': 4.625928e-02, 'conv_b': 6.513182e-02, 'dt_bias': 1.378507e-01, 'a_log': 1.548440e-01, 'd_skip': 2.193069e-01, 'ssd_norm_w': 5.202449e-02, 'q_a_norm_w': 1.042983e-02, 'w_q_b': 6.271720e-03, 'kv_a_norm_w': 2.142669e-02, 'w_kv_b': 7.280092e-03, 'w_out': 3.751347e-02, 'ffn_norm_w': 3.689686e-02, 'w_ffn_up': 1.584555e-02, 'ffn_conv_w': 1.608220e-02, 'ffn_conv_b': 1.615849e-02, 'w_ffn_down': 2.590723e-02, 'ple_norm_w': 9.290623e-03, 'w_ple_gate': 9.075498e-03, 'b_ple_gate': 1.365145e-02, 'w_ple_proj': 2.318164e-02, 'ple_post_norm_w': 3.770640e-02, 'final_norm_w': 8.005315e+00}


def _to_microbatches(a, axis):
    t = _jnp.moveaxis(a, axis, 0)
    t = t.reshape((N_MICROBATCH, t.shape[0] // N_MICROBATCH) + t.shape[1:])
    return _jnp.moveaxis(t, 1, axis + 1)


def setup_inputs(seed: int = 0) -> dict:
    inp = _fwd_setup_inputs(seed)
    key = _jax.random.fold_in(_jax.random.key(seed), 7919)
    shape, _ = _output_shape()
    out = dict(inp)
    out["loss_target"] = _jax.random.normal(_jax.random.fold_in(key, 0), shape, _jnp.float32)
    for i, name in enumerate(TWIN_WEIGHTS):
        w = inp[name].astype(_jnp.float32)
        if MOMENT_SCALE is None:
            s = _jnp.sqrt(_jnp.mean(_jnp.square(w)) + 1e-30)
        else:
            s = MOMENT_SCALE[name]
        km, kv = _jax.random.split(_jax.random.fold_in(key, i + 1))
        out[name] = w
        out["m_" + name] = s * _jax.random.normal(km, w.shape, _jnp.float32)
        out["v_" + name] = (s * s) * _jax.random.uniform(kv, w.shape, _jnp.float32, 0.5, 1.5)
    if N_MICROBATCH > 1:
        for name, axis in PER_EXAMPLE_BATCH_AXIS.items():
            out[name] = _to_microbatches(out[name], axis)
    return {'x': out['x'], 'p': out['p'], 'positions': out['positions'], 'mix_norm_w': out['mix_norm_w'], 'w_in': out['w_in'], 'conv_w': out['conv_w'], 'conv_b': out['conv_b'], 'dt_bias': out['dt_bias'], 'a_log': out['a_log'], 'd_skip': out['d_skip'], 'ssd_norm_w': out['ssd_norm_w'], 'q_a_norm_w': out['q_a_norm_w'], 'w_q_b': out['w_q_b'], 'kv_a_norm_w': out['kv_a_norm_w'], 'w_kv_b': out['w_kv_b'], 'w_out': out['w_out'], 'ffn_norm_w': out['ffn_norm_w'], 'w_ffn_up': out['w_ffn_up'], 'ffn_conv_w': out['ffn_conv_w'], 'ffn_conv_b': out['ffn_conv_b'], 'w_ffn_down': out['w_ffn_down'], 'ple_norm_w': out['ple_norm_w'], 'w_ple_gate': out['w_ple_gate'], 'b_ple_gate': out['b_ple_gate'], 'w_ple_proj': out['w_ple_proj'], 'ple_post_norm_w': out['ple_post_norm_w'], 'final_norm_w': out['final_norm_w'], 'loss_target': out['loss_target'], 'm_mix_norm_w': out['m_mix_norm_w'], 'm_w_in': out['m_w_in'], 'm_conv_w': out['m_conv_w'], 'm_conv_b': out['m_conv_b'], 'm_dt_bias': out['m_dt_bias'], 'm_a_log': out['m_a_log'], 'm_d_skip': out['m_d_skip'], 'm_ssd_norm_w': out['m_ssd_norm_w'], 'm_q_a_norm_w': out['m_q_a_norm_w'], 'm_w_q_b': out['m_w_q_b'], 'm_kv_a_norm_w': out['m_kv_a_norm_w'], 'm_w_kv_b': out['m_w_kv_b'], 'm_w_out': out['m_w_out'], 'm_ffn_norm_w': out['m_ffn_norm_w'], 'm_w_ffn_up': out['m_w_ffn_up'], 'm_ffn_conv_w': out['m_ffn_conv_w'], 'm_ffn_conv_b': out['m_ffn_conv_b'], 'm_w_ffn_down': out['m_w_ffn_down'], 'm_ple_norm_w': out['m_ple_norm_w'], 'm_w_ple_gate': out['m_w_ple_gate'], 'm_b_ple_gate': out['m_b_ple_gate'], 'm_w_ple_proj': out['m_w_ple_proj'], 'm_ple_post_norm_w': out['m_ple_post_norm_w'], 'm_final_norm_w': out['m_final_norm_w'], 'v_mix_norm_w': out['v_mix_norm_w'], 'v_w_in': out['v_w_in'], 'v_conv_w': out['v_conv_w'], 'v_conv_b': out['v_conv_b'], 'v_dt_bias': out['v_dt_bias'], 'v_a_log': out['v_a_log'], 'v_d_skip': out['v_d_skip'], 'v_ssd_norm_w': out['v_ssd_norm_w'], 'v_q_a_norm_w': out['v_q_a_norm_w'], 'v_w_q_b': out['v_w_q_b'], 'v_kv_a_norm_w': out['v_kv_a_norm_w'], 'v_w_kv_b': out['v_w_kv_b'], 'v_w_out': out['v_w_out'], 'v_ffn_norm_w': out['v_ffn_norm_w'], 'v_w_ffn_up': out['v_w_ffn_up'], 'v_ffn_conv_w': out['v_ffn_conv_w'], 'v_ffn_conv_b': out['v_ffn_conv_b'], 'v_w_ffn_down': out['v_w_ffn_down'], 'v_ple_norm_w': out['v_ple_norm_w'], 'v_w_ple_gate': out['v_w_ple_gate'], 'v_b_ple_gate': out['v_b_ple_gate'], 'v_w_ple_proj': out['v_w_ple_proj'], 'v_ple_post_norm_w': out['v_ple_post_norm_w'], 'v_final_norm_w': out['v_final_norm_w']}


def _loss(weights, diff, rest, loss_target):
    with _jax.named_scope("forward"):
        args = {**rest, TWIN_DIFF_INPUT: diff, **{k: w.astype(_WEIGHT_DTYPES[k]) for k, w in weights.items()}}
        y = _forward(args)
    with _jax.named_scope("loss_head"):
        err = _jnp.square(y.astype(_jnp.float32) - loss_target)
        return 0.5 * _jnp.sum(_jnp.mean(err, axis=-1)) if err.ndim else 0.5 * err


def _adamw(w, g, m, v):
    m = ADAM_B1 * m + (1.0 - ADAM_B1) * g
    v = ADAM_B2 * v + (1.0 - ADAM_B2) * _jnp.square(g)
    m_hat = m / (1.0 - ADAM_B1 ** ADAM_STEP)
    v_hat = v / (1.0 - ADAM_B2 ** ADAM_STEP)
    delta = -ADAM_LR * (m_hat / (_jnp.sqrt(v_hat) + ADAM_EPS) + ADAM_WD * w)
    return delta, m, v


def reference(x, p, positions, mix_norm_w, w_in, conv_w, conv_b, dt_bias, a_log, d_skip, ssd_norm_w, q_a_norm_w, w_q_b, kv_a_norm_w, w_kv_b, w_out, ffn_norm_w, w_ffn_up, ffn_conv_w, ffn_conv_b, w_ffn_down, ple_norm_w, w_ple_gate, b_ple_gate, w_ple_proj, ple_post_norm_w, final_norm_w, loss_target, m_mix_norm_w, m_w_in, m_conv_w, m_conv_b, m_dt_bias, m_a_log, m_d_skip, m_ssd_norm_w, m_q_a_norm_w, m_w_q_b, m_kv_a_norm_w, m_w_kv_b, m_w_out, m_ffn_norm_w, m_w_ffn_up, m_ffn_conv_w, m_ffn_conv_b, m_w_ffn_down, m_ple_norm_w, m_w_ple_gate, m_b_ple_gate, m_w_ple_proj, m_ple_post_norm_w, m_final_norm_w, v_mix_norm_w, v_w_in, v_conv_w, v_conv_b, v_dt_bias, v_a_log, v_d_skip, v_ssd_norm_w, v_q_a_norm_w, v_w_q_b, v_kv_a_norm_w, v_w_kv_b, v_w_out, v_ffn_norm_w, v_w_ffn_up, v_ffn_conv_w, v_ffn_conv_b, v_w_ffn_down, v_ple_norm_w, v_w_ple_gate, v_b_ple_gate, v_w_ple_proj, v_ple_post_norm_w, v_final_norm_w):
    given = dict(x=x, p=p, positions=positions, mix_norm_w=mix_norm_w, w_in=w_in, conv_w=conv_w, conv_b=conv_b, dt_bias=dt_bias, a_log=a_log, d_skip=d_skip, ssd_norm_w=ssd_norm_w, q_a_norm_w=q_a_norm_w, w_q_b=w_q_b, kv_a_norm_w=kv_a_norm_w, w_kv_b=w_kv_b, w_out=w_out, ffn_norm_w=ffn_norm_w, w_ffn_up=w_ffn_up, ffn_conv_w=ffn_conv_w, ffn_conv_b=ffn_conv_b, w_ffn_down=w_ffn_down, ple_norm_w=ple_norm_w, w_ple_gate=w_ple_gate, b_ple_gate=b_ple_gate, w_ple_proj=w_ple_proj, ple_post_norm_w=ple_post_norm_w, final_norm_w=final_norm_w, loss_target=loss_target, m_mix_norm_w=m_mix_norm_w, m_w_in=m_w_in, m_conv_w=m_conv_w, m_conv_b=m_conv_b, m_dt_bias=m_dt_bias, m_a_log=m_a_log, m_d_skip=m_d_skip, m_ssd_norm_w=m_ssd_norm_w, m_q_a_norm_w=m_q_a_norm_w, m_w_q_b=m_w_q_b, m_kv_a_norm_w=m_kv_a_norm_w, m_w_kv_b=m_w_kv_b, m_w_out=m_w_out, m_ffn_norm_w=m_ffn_norm_w, m_w_ffn_up=m_w_ffn_up, m_ffn_conv_w=m_ffn_conv_w, m_ffn_conv_b=m_ffn_conv_b, m_w_ffn_down=m_w_ffn_down, m_ple_norm_w=m_ple_norm_w, m_w_ple_gate=m_w_ple_gate, m_b_ple_gate=m_b_ple_gate, m_w_ple_proj=m_w_ple_proj, m_ple_post_norm_w=m_ple_post_norm_w, m_final_norm_w=m_final_norm_w, v_mix_norm_w=v_mix_norm_w, v_w_in=v_w_in, v_conv_w=v_conv_w, v_conv_b=v_conv_b, v_dt_bias=v_dt_bias, v_a_log=v_a_log, v_d_skip=v_d_skip, v_ssd_norm_w=v_ssd_norm_w, v_q_a_norm_w=v_q_a_norm_w, v_w_q_b=v_w_q_b, v_kv_a_norm_w=v_kv_a_norm_w, v_w_kv_b=v_w_kv_b, v_w_out=v_w_out, v_ffn_norm_w=v_ffn_norm_w, v_w_ffn_up=v_w_ffn_up, v_ffn_conv_w=v_ffn_conv_w, v_ffn_conv_b=v_ffn_conv_b, v_w_ffn_down=v_w_ffn_down, v_ple_norm_w=v_ple_norm_w, v_w_ple_gate=v_w_ple_gate, v_b_ple_gate=v_b_ple_gate, v_w_ple_proj=v_w_ple_proj, v_ple_post_norm_w=v_ple_post_norm_w, v_final_norm_w=v_final_norm_w)
    weights = {n: given[n] for n in TWIN_WEIGHTS}
    shared = {n: given[n] for n in SHARED_INPUTS}
    per_example = {n: given[n] for n in ['x', 'p', 'positions']}
    grad_fn = _jax.value_and_grad(_loss, argnums=(0, 1))

    def one_microbatch(ex, loss_target):
        ex = dict(ex)
        diff = ex.pop(TWIN_DIFF_INPUT)
        return grad_fn(weights, diff, {**shared, **ex}, loss_target)

    if N_MICROBATCH == 1:
        loss, (grad_w, grad_x) = one_microbatch(per_example, given["loss_target"])
    else:
        def body(carry, xs):
            loss_sum, grad_sum = carry
            l_k, (gw_k, gx_k) = one_microbatch(xs[0], xs[1])
            with _jax.named_scope("update"):
                return (loss_sum + l_k, _jax.tree.map(_jnp.add, grad_sum, gw_k)), gx_k

        init = (_jnp.zeros((), _jnp.float32), _jax.tree.map(_jnp.zeros_like, weights))
        (loss, grad_w), grad_x = _jax.lax.scan(body, init, (per_example, given["loss_target"]))
    with _jax.named_scope("update"):
        delta_w, new_m, new_v = {}, {}, {}
        for n in TWIN_WEIGHTS:
            delta_w[n], new_m[n], new_v[n] = _adamw(weights[n], grad_w[n], given["m_" + n], given["v_" + n])
    return (loss, grad_x, *[grad_w[n] for n in TWIN_WEIGHTS], *[delta_w[n] for n in TWIN_WEIGHTS],
            *[new_m[n] for n in TWIN_WEIGHTS], *[new_v[n] for n in TWIN_WEIGHTS])
```

```python
import functools
import math

import numpy as np
import jax
import jax.numpy as jnp
from jax import lax
from jax.experimental import pallas as pl
from jax.experimental.pallas import tpu as pltpu

F32 = jnp.float32
BF16 = jnp.bfloat16
HI = lax.Precision.HIGHEST

D_MODEL = 2048
CHUNK = 64
D_SSM = 1024
SSD_P = 64
SSD_HEADS = 16
SSD_GROUPS = 2
SSD_N = 128
SSD_CONV = 4
SSD_CONV_DIM = D_SSM + 2 * SSD_GROUPS * SSD_N
MLA_HEADS = 8
MLA_NOPE = 128
MLA_ROPE = 64
MLA_V = 128
MLA_Q_RANK = 512
MLA_KV_RANK = 256
MLA_QK_PAD = 256
ROPE_THETA = 10000.0
D_FF = 5632
FFN_CONV = 3
PLE_DIM = 256
NORM_EPS = 1e-6
ADAM_LR, ADAM_B1, ADAM_B2, ADAM_EPS, ADAM_WD, ADAM_STEP = 0.001, 0.9, 0.999, 1e-08, 0.01, 10
N_DEV = 8

OFF_Z, OFF_XBC, OFF_QA, OFF_CKV, OFF_KR, OFF_DT, D_IN_PAD = 0, 1024, 2560, 3072, 3328, 3456, 3584
D_IN = 3408
LANES = 128
HALO = 8
VMEM_LIMIT = 56 * 1024 * 1024
FFN_TC = D_FF * 2 // N_DEV
FFN_PERM = (0, 4, 1, 5, 2, 6, 3, 7)
NEG = -1e30


def _cp(*sem):
    return pltpu.CompilerParams(dimension_semantics=tuple(sem), vmem_limit_bytes=VMEM_LIMIT)


def _tile(n, want):
    if n <= want:
        return n
    best = max(d for d in range(LANES, want + 1, LANES) if n % d == 0)
    return best


def _sigmoid(x):
    return 1.0 / (1.0 + jnp.exp(-x))


def _silu(x):
    return x * _sigmoid(x)


def _dsilu(x):
    s = _sigmoid(x)
    return s * (1.0 + x * (1.0 - s))


def _matmul(a, b, *, ta=False, tb=False, out_dtype=F32, add=None, bias=None, tm=1024, tn=1024, tk=512, name,
            mnk=None, a_spec=None, b_spec=None, o_spec=None, o_shape=None):
    if mnk is None:
        m, k = (a.shape[1], a.shape[0]) if ta else a.shape
        n = b.shape[0] if tb else b.shape[1]
        assert k == (b.shape[1] if tb else b.shape[0])
    else:
        m, n, k = mnk
    tm, tn, tk = _tile(m, tm), _tile(n, tn), _tile(k, tk)
    nk = k // tk
    dims = (((0 if ta else 1,), (1 if tb else 0,)), ((), ()))

    def body(*refs):
        a_ref, b_ref = refs[0], refs[1]
        pos = 2
        add_ref = bias_ref = None
        if add is not None:
            add_ref = refs[pos]
            pos += 1
        if bias is not None:
            bias_ref = refs[pos]
            pos += 1
        o_ref, acc_ref = refs[pos], refs[pos + 1]
        kk = pl.program_id(2)

        @pl.when(kk == 0)
        def _():
            acc_ref[...] = jnp.zeros_like(acc_ref)

        av = a_ref[...]
        bv = b_ref[...]
        av = av.reshape(av.shape[-2:]).astype(BF16)
        bv = bv.reshape(bv.shape[-2:]).astype(BF16)
        acc_ref[...] += lax.dot_general(av, bv, dims, preferred_element_type=F32)

        @pl.when(kk == nk - 1)
        def _():
            r = acc_ref[...]
            if bias_ref is not None:
                r = r + bias_ref[...]
            if add_ref is not None:
                r = r + add_ref[...].astype(F32)
            o_ref[...] = r.astype(out_dtype).reshape(o_ref.shape)

    if a_spec is None:
        a_spec = (pl.BlockSpec((tk, tm), lambda i, j, kk: (kk, i)) if ta
                  else pl.BlockSpec((tm, tk), lambda i, j, kk: (i, kk)))
    if b_spec is None:
        b_spec = (pl.BlockSpec((tn, tk), lambda i, j, kk: (j, kk)) if tb
                  else pl.BlockSpec((tk, tn), lambda i, j, kk: (kk, j)))
    if o_spec is None:
        o_spec = pl.BlockSpec((tm, tn), lambda i, j, kk: (i, j))
    if o_shape is None:
        o_shape = (m, n)
    in_specs = [a_spec, b_spec]
    args = [a, b]
    if add is not None:
        in_specs.append(pl.BlockSpec((tm, tn), lambda i, j, kk: (i, j)))
        args.append(add)
    if bias is not None:
        in_specs.append(pl.BlockSpec((1, tn), lambda i, j, kk: (0, j)))
        args.append(bias)
    return pl.pallas_call(
        body, name=name, grid=(m // tm, n // tn, nk), in_specs=in_specs, out_specs=o_spec,
        out_shape=jax.ShapeDtypeStruct(o_shape, out_dtype),
        scratch_shapes=[pltpu.VMEM((tm, tn), F32)],
        compiler_params=_cp("parallel", "parallel", "arbitrary"),
    )(*args)


def _rmsnorm_fwd(x, w, *, width, cblk=0, out_dtype=BF16, tr=256, name):
    t = x.shape[0]

    def body(x_ref, w_ref, o_ref):
        xv = x_ref[...].astype(F32)
        r = lax.rsqrt(jnp.mean(xv * xv, axis=-1, keepdims=True) + NORM_EPS)
        o_ref[...] = (xv * r * w_ref[...]).astype(out_dtype)

    return pl.pallas_call(
        body, name=name, grid=(t // tr,),
        in_specs=[pl.BlockSpec((tr, width), lambda i: (i, cblk)), pl.BlockSpec((1, width), lambda i: (0, 0))],
        out_specs=pl.BlockSpec((tr, width), lambda i: (i, 0)),
        out_shape=jax.ShapeDtypeStruct((t, width), out_dtype),
        compiler_params=_cp("parallel"),
    )(x, w)


def _rmsnorm_bwd(x, w, dy, add=None, *, width, cblk=0, out_dtype=F32, tr=256, name):
    t = x.shape[0]

    def body(*refs):
        if add is None:
            x_ref, w_ref, dy_ref, dx_ref, dw_ref = refs
            add_ref = None
        else:
            x_ref, w_ref, dy_ref, add_ref, dx_ref, dw_ref = refs
        xv = x_ref[...].astype(F32)
        dyv = dy_ref[...].astype(F32)
        r = lax.rsqrt(jnp.mean(xv * xv, axis=-1, keepdims=True) + NORM_EPS)
        xh = xv * r
        g = dyv * w_ref[...]
        dx = r * (g - xh * jnp.mean(g * xh, axis=-1, keepdims=True))
        if add_ref is not None:
            dx = dx + add_ref[...].astype(F32)
        dx_ref[...] = dx.astype(out_dtype)

        @pl.when(pl.program_id(0) == 0)
        def _():
            dw_ref[...] = jnp.zeros_like(dw_ref)

        dw_ref[...] += jnp.sum(dyv * xh, axis=0, keepdims=True)

    in_specs = [pl.BlockSpec((tr, width), lambda i: (i, cblk)), pl.BlockSpec((1, width), lambda i: (0, 0)),
                pl.BlockSpec((tr, width), lambda i: (i, 0))]
    args = [x, w, dy]
    if add is not None:
        in_specs.append(pl.BlockSpec((tr, width), lambda i: (i, 0)))
        args.append(add)
    return pl.pallas_call(
        body, name=name, grid=(t // tr,), in_specs=in_specs,
        out_specs=[pl.BlockSpec((tr, width), lambda i: (i, 0)), pl.BlockSpec((1, width), lambda i: (0, 0))],
        out_shape=[jax.ShapeDtypeStruct((t, width), out_dtype), jax.ShapeDtypeStruct((1, width), F32)],
        compiler_params=_cp("arbitrary"),
    )(*args)


def _shift_down(prev_halo, cur, j):
    if j == 0:
        return cur
    ext = jnp.concatenate([prev_halo, cur], axis=0)
    return pltpu.roll(ext, j, axis=0)[HALO:]


def _shift_up(cur, next_halo, j):
    if j == 0:
        return cur
    ext = jnp.concatenate([cur, next_halo], axis=0)
    return pltpu.roll(ext, ext.shape[0] - j, axis=0)[:cur.shape[0]]


def _conv_rows(prev, cur, w, b, kw):
    shifted = [cur]
    out = b + w[kw - 1:kw] * cur
    for j in range(1, kw):
        sh = _shift_down(prev, cur, j)
        shifted.append(sh)
        out = out + w[kw - 1 - j:kw - j] * sh
    return out, shifted


def _act_fwd(c, glu):
    if glu:
        half = c.shape[1] // 2
        return _silu(c[:, :half]) * c[:, half:]
    return _silu(c)


def _act_bwd(c, dout, glu):
    if glu:
        half = c.shape[1] // 2
        g, up = c[:, :half], c[:, half:]
        return jnp.concatenate([dout * up * _dsilu(g), dout * _silu(g)], axis=1)
    return dout * _dsilu(c)


def _conv_act_fwd(u, w, b, *, kw, glu, tc, coff, ncols, out_dtype, tr=256, name):
    t = u.shape[0]
    nb = ncols // tc
    oc = tc // 2 if glu else tc

    def body(u_ref, uh_ref, w_ref, b_ref, o_ref):
        prev = jnp.where(pl.program_id(0) == 0, 0.0, uh_ref[...])
        c, _ = _conv_rows(prev, u_ref[...], w_ref[...], b_ref[...], kw)
        o_ref[...] = _act_fwd(c, glu).astype(out_dtype)

    return pl.pallas_call(
        body, name=name, grid=(t // tr, nb),
        in_specs=[pl.BlockSpec((tr, tc), lambda i, j: (i, j + coff)),
                  pl.BlockSpec((HALO, tc), lambda i, j: (jnp.maximum(i * (tr // HALO) - 1, 0), j + coff)),
                  pl.BlockSpec((kw, tc), lambda i, j: (0, j)), pl.BlockSpec((1, tc), lambda i, j: (0, j))],
        out_specs=pl.BlockSpec((tr, oc), lambda i, j: (i, j)),
        out_shape=jax.ShapeDtypeStruct((t, nb * oc), out_dtype),
        compiler_params=_cp("parallel", "parallel"),
    )(u, u, w, b)


def _conv_act_bwd(u, w, b, dout, *, kw, glu, tc, coff, ncols, tr=256, name):
    t = u.shape[0]
    nb = ncols // tc
    nt = t // tr
    oc = tc // 2 if glu else tc

    def body(u_ref, up_ref, un_ref, d_ref, dn_ref, w_ref, b_ref, du_ref, dw_ref, db_ref):
        i = pl.program_id(1)
        cur, nxt, wv, bv = u_ref[...], un_ref[...], w_ref[...], b_ref[...]
        prev = jnp.where(i == 0, 0.0, up_ref[...])
        c_cur, shifted = _conv_rows(prev, cur, wv, bv, kw)
        c_nxt, _ = _conv_rows(cur[tr - HALO:], nxt, wv, bv, kw)
        d_cur = _act_bwd(c_cur, d_ref[...].astype(F32), glu)
        d_nxt = _act_bwd(c_nxt, jnp.where(i == nt - 1, 0.0, dn_ref[...].astype(F32)), glu)
        du = wv[kw - 1:kw] * d_cur
        for j in range(1, kw):
            du = du + wv[kw - 1 - j:kw - j] * _shift_up(d_cur, d_nxt, j)
        du_ref[...] = du.astype(BF16)

        @pl.when(i == 0)
        def _():
            dw_ref[...] = jnp.zeros_like(dw_ref)
            db_ref[...] = jnp.zeros_like(db_ref)

        db_ref[...] += jnp.sum(d_cur, axis=0, keepdims=True)
        dw_ref[...] += jnp.concatenate(
            [jnp.sum(d_cur * shifted[kw - 1 - k], axis=0, keepdims=True) for k in range(kw)], axis=0)

    nh = tr // HALO
    return pl.pallas_call(
        body, name=name, grid=(nb, nt),
        in_specs=[pl.BlockSpec((tr, tc), lambda j, i: (i, j + coff)),
                  pl.BlockSpec((HALO, tc), lambda j, i: (jnp.maximum(i * nh - 1, 0), j + coff)),
                  pl.BlockSpec((HALO, tc), lambda j, i: (jnp.minimum((i + 1) * nh, t // HALO - 1), j + coff)),
                  pl.BlockSpec((tr, oc), lambda j, i: (i, j)),
                  pl.BlockSpec((HALO, oc), lambda j, i: (jnp.minimum((i + 1) * nh, t // HALO - 1), j)),
                  pl.BlockSpec((kw, tc), lambda j, i: (0, j)), pl.BlockSpec((1, tc), lambda j, i: (0, j))],
        out_specs=[pl.BlockSpec((tr, tc), lambda j, i: (i, j)), pl.BlockSpec((kw, tc), lambda j, i: (0, j)),
                   pl.BlockSpec((1, tc), lambda j, i: (0, j))],
        out_shape=[jax.ShapeDtypeStruct((t, ncols), BF16), jax.ShapeDtypeStruct((kw, ncols), F32),
                   jax.ShapeDtypeStruct((1, ncols), F32)],
        compiler_params=_cp("parallel", "arbitrary"),
    )(u, u, u, dout, dout, w, b)


def _ple_fwd(x2, gl, pe, pw, *, tr=256, name):
    t, d = x2.shape

    def body(x_ref, gl_ref, pe_ref, pw_ref, o_ref):
        pv = pe_ref[...]
        r = lax.rsqrt(jnp.mean(pv * pv, axis=-1, keepdims=True) + NORM_EPS)
        o_ref[...] = x_ref[...] + _sigmoid(gl_ref[...]) * (pv * r * pw_ref[...])

    blk = pl.BlockSpec((tr, d), lambda i: (i, 0))
    return pl.pallas_call(
        body, name=name, grid=(t // tr,), in_specs=[blk, blk, blk, pl.BlockSpec((1, d), lambda i: (0, 0))],
        out_specs=blk, out_shape=jax.ShapeDtypeStruct((t, d), F32), compiler_params=_cp("parallel"),
    )(x2, gl, pe, pw)


def _ple_bwd(dx3, gl, pe, pw, *, tr=256, name):
    t, d = dx3.shape

    def body(dx_ref, gl_ref, pe_ref, pw_ref, dgl_ref, db_ref, dpe_ref, dpw_ref):
        dx, pv, pwv = dx_ref[...], pe_ref[...], pw_ref[...]
        gate = _sigmoid(gl_ref[...])
        r = lax.rsqrt(jnp.mean(pv * pv, axis=-1, keepdims=True) + NORM_EPS)
        ph = pv * r
        dgl = dx * (ph * pwv) * gate * (1.0 - gate)
        de = dx * gate
        g = de * pwv
        dgl_ref[...] = dgl.astype(BF16)
        dpe_ref[...] = (r * (g - ph * jnp.mean(g * ph, axis=-1, keepdims=True))).astype(BF16)

        @pl.when(pl.program_id(0) == 0)
        def _():
            db_ref[...] = jnp.zeros_like(db_ref)
            dpw_ref[...] = jnp.zeros_like(dpw_ref)

        db_ref[...] += jnp.sum(dgl, axis=0, keepdims=True)
        dpw_ref[...] += jnp.sum(de * ph, axis=0, keepdims=True)

    blk = pl.BlockSpec((tr, d), lambda i: (i, 0))
    row = pl.BlockSpec((1, d), lambda i: (0, 0))
    return pl.pallas_call(
        body, name=name, grid=(t // tr,), in_specs=[blk, blk, blk, row], out_specs=[blk, row, blk, row],
        out_shape=[jax.ShapeDtypeStruct((t, d), BF16), jax.ShapeDtypeStruct((1, d), F32),
                   jax.ShapeDtypeStruct((t, d), BF16), jax.ShapeDtypeStruct((1, d), F32)],
        compiler_params=_cp("arbitrary"),
    )(dx3, gl, pe, pw)


def _loss_head(x3, fw, target, *, tr=256, name):
    t, d = x3.shape

    def body(x_ref, w_ref, t_ref, l_ref, dx_ref, dw_ref):
        xv, wv = x_ref[...], w_ref[...]
        r = lax.rsqrt(jnp.mean(xv * xv, axis=-1, keepdims=True) + NORM_EPS)
        xh = xv * r
        err = xh * wv - t_ref[...]
        dy = err * (1.0 / d)
        g = dy * wv
        dx_ref[...] = r * (g - xh * jnp.mean(g * xh, axis=-1, keepdims=True))

        @pl.when(pl.program_id(0) == 0)
        def _():
            l_ref[...] = jnp.zeros_like(l_ref)
            dw_ref[...] = jnp.zeros_like(dw_ref)

        l_ref[...] += 0.5 * jnp.sum(jnp.mean(err * err, axis=-1, keepdims=True), axis=0, keepdims=True)
        dw_ref[...] += jnp.sum(dy * xh, axis=0, keepdims=True)

    blk = pl.BlockSpec((tr, d), lambda i: (i, 0))
    row = pl.BlockSpec((1, d), lambda i: (0, 0))
    return pl.pallas_call(
        body, name=name, grid=(t // tr,), in_specs=[blk, row, blk],
        out_specs=[pl.BlockSpec((1, 1), lambda i: (0, 0)), blk, row],
        out_shape=[jax.ShapeDtypeStruct((1, 1), F32), jax.ShapeDtypeStruct((t, d), F32),
                   jax.ShapeDtypeStruct((1, d), F32)],
        compiler_params=_cp("arbitrary"),
    )(x3, fw, target)


def _rope(blk, tab_ref):
    return blk * tab_ref[0] + pltpu.roll(blk, 96, axis=1) * tab_ref[1] + pltpu.roll(blk, 32, axis=1) * tab_ref[2]


def _unrope(g, tab_ref):
    return g * tab_ref[0] + pltpu.roll(g * tab_ref[1], 32, axis=1) + pltpu.roll(g * tab_ref[2], 96, axis=1)


def _mla_prep(q, kv, proj, tabs, *, tr=512, name):
    t = q.shape[0]

    def body(q_ref, kv_ref, kr_ref, tab_ref, qo_ref, ko_ref, vo_ref):
        qv, kvv = q_ref[...], kv_ref[...]
        qo_ref[0, :, :MLA_NOPE] = qv[:, :MLA_NOPE].astype(BF16)
        qo_ref[0, :, MLA_NOPE:] = _rope(qv[:, MLA_NOPE:], tab_ref).astype(BF16)
        ko_ref[0, :, :MLA_NOPE] = kvv[:, :MLA_NOPE].astype(BF16)
        ko_ref[0, :, MLA_NOPE:] = _rope(kr_ref[...], tab_ref).astype(BF16)
        vo_ref[0] = kvv[:, MLA_NOPE:].astype(BF16)

    return pl.pallas_call(
        body, name=name, grid=(t // tr, MLA_HEADS),
        in_specs=[pl.BlockSpec((tr, MLA_QK_PAD), lambda i, h: (i, h)),
                  pl.BlockSpec((tr, MLA_NOPE + MLA_V), lambda i, h: (i, h)),
                  pl.BlockSpec((tr, LANES), lambda i, h: (i, OFF_KR // LANES)),
                  pl.BlockSpec((3, tr, LANES), lambda i, h: (0, i, 0))],
        out_specs=[pl.BlockSpec((1, tr, MLA_QK_PAD), lambda i, h: (h, i, 0)),
                   pl.BlockSpec((1, tr, MLA_QK_PAD), lambda i, h: (h, i, 0)),
                   pl.BlockSpec((1, tr, MLA_V), lambda i, h: (h, i, 0))],
        out_shape=[jax.ShapeDtypeStruct((MLA_HEADS, t, MLA_QK_PAD), BF16),
                   jax.ShapeDtypeStruct((MLA_HEADS, t, MLA_QK_PAD), BF16),
                   jax.ShapeDtypeStruct((MLA_HEADS, t, MLA_V), BF16)],
        compiler_params=_cp("parallel", "parallel"),
    )(q, kv, proj, tabs)


def _mla_unprep(dq3, dk3, dv3, tabs, *, tr=256, name):
    t = dq3.shape[1]

    def body(dq_ref, dk_ref, dv_ref, tab_ref, qo_ref, kvo_ref, kro_ref):
        kr = jnp.zeros((tr, LANES), F32)
        for h in range(MLA_HEADS):
            c0 = h * MLA_QK_PAD
            qo_ref[:, c0:c0 + MLA_NOPE] = dq_ref[h, :, :MLA_NOPE].astype(BF16)
            qo_ref[:, c0 + MLA_NOPE:c0 + MLA_QK_PAD] = _unrope(dq_ref[h, :, MLA_NOPE:], tab_ref).astype(BF16)
            kvo_ref[:, c0:c0 + MLA_NOPE] = dk_ref[h, :, :MLA_NOPE].astype(BF16)
            kvo_ref[:, c0 + MLA_NOPE:c0 + MLA_QK_PAD] = dv_ref[h].astype(BF16)
            kr = kr + dk_ref[h, :, MLA_NOPE:]
        kro_ref[...] = _unrope(kr, tab_ref).astype(BF16)

    return pl.pallas_call(
        body, name=name, grid=(t // tr,),
        in_specs=[pl.BlockSpec((MLA_HEADS, tr, MLA_QK_PAD), lambda i: (0, i, 0)),
                  pl.BlockSpec((MLA_HEADS, tr, MLA_QK_PAD), lambda i: (0, i, 0)),
                  pl.BlockSpec((MLA_HEADS, tr, MLA_V), lambda i: (0, i, 0)),
                  pl.BlockSpec((3, tr, LANES), lambda i: (0, i, 0))],
        out_specs=[pl.BlockSpec((tr, MLA_HEADS * MLA_QK_PAD), lambda i: (i, 0)),
                   pl.BlockSpec((tr, MLA_HEADS * MLA_QK_PAD), lambda i: (i, 0)),
                   pl.BlockSpec((tr, LANES), lambda i: (i, 0))],
        out_shape=[jax.ShapeDtypeStruct((t, MLA_HEADS * MLA_QK_PAD), BF16),
                   jax.ShapeDtypeStruct((t, MLA_HEADS * MLA_QK_PAD), BF16),
                   jax.ShapeDtypeStruct((t, LANES), BF16)],
        compiler_params=_cp("parallel"),
    )(dq3, dk3, dv3, tabs)


ATT_BLK = 256
ATT_SCALE = 1.0 / math.sqrt(MLA_NOPE + MLA_ROPE)
_NT = (((1,), (1,)), ((), ()))
_TN = (((0,), (0,)), ((), ()))


def _att_scores(q, k, qi, kj):
    s = lax.dot_general(q, k, _NT, preferred_element_type=F32) * ATT_SCALE
    row = qi * ATT_BLK + lax.broadcasted_iota(jnp.int32, s.shape, 0)
    col = kj * ATT_BLK + lax.broadcasted_iota(jnp.int32, s.shape, 1)
    return jnp.where((col >> 6) <= (row >> 6), s, NEG)


def _attn_fwd(q3, k3, v3, *, name):
    t = q3.shape[1]
    nq = t // ATT_BLK

    def body(q_ref, k_ref, v_ref, o_ref, lse_ref):
        qi = pl.program_id(1)
        q = q_ref[0]

        def step(j, carry):
            m, l, acc = carry
            rows = pl.ds(pl.multiple_of(j * ATT_BLK, ATT_BLK), ATT_BLK)
            s = _att_scores(q, k_ref[0, rows, :], qi, j)
            m_new = jnp.maximum(m, jnp.max(s, axis=-1, keepdims=True))
            p = jnp.exp(s - m_new)
            alpha = jnp.exp(m - m_new)
            l = alpha * l + jnp.sum(p, axis=-1, keepdims=True)
            acc = alpha * acc + jnp.dot(p.astype(BF16), v_ref[0, rows, :], preferred_element_type=F32)
            return m_new, l, acc

        init = (jnp.full((ATT_BLK, 1), NEG, F32), jnp.zeros((ATT_BLK, 1), F32), jnp.zeros((ATT_BLK, MLA_V), F32))
        m, l, acc = lax.fori_loop(0, qi + 1, step, init)
        o_ref[...] = acc / l
        lse_ref[0] = m + jnp.log(l)

    return pl.pallas_call(
        body, name=name, grid=(MLA_HEADS, nq),
        in_specs=[pl.BlockSpec((1, ATT_BLK, MLA_QK_PAD), lambda h, i: (h, i, 0)),
                  pl.BlockSpec((1, t, MLA_QK_PAD), lambda h, i: (h, 0, 0)),
                  pl.BlockSpec((1, t, MLA_V), lambda h, i: (h, 0, 0))],
        out_specs=[pl.BlockSpec((ATT_BLK, MLA_V), lambda h, i: (i, h)),
                   pl.BlockSpec((1, ATT_BLK, 1), lambda h, i: (h, i, 0))],
        out_shape=[jax.ShapeDtypeStruct((t, MLA_HEADS * MLA_V), F32), jax.ShapeDtypeStruct((MLA_HEADS, t, 1), F32)],
        compiler_params=_cp("parallel", "parallel"),
    )(q3, k3, v3)


def _attn_bwd_dq(q3, k3, v3, o, dcat, lse, *, name):
    t = q3.shape[1]
    nq = t // ATT_BLK

    def body(q_ref, k_ref, v_ref, o_ref, do_ref, lse_ref, dq_ref, dl_ref):
        qi = pl.program_id(1)
        q, do, lse = q_ref[0], do_ref[...], lse_ref[0]
        delta = jnp.sum(o_ref[...] * do, axis=-1, keepdims=True)
        dob = do.astype(BF16)

        def step(j, dq):
            rows = pl.ds(pl.multiple_of(j * ATT_BLK, ATT_BLK), ATT_BLK)
            k = k_ref[0, rows, :]
            p = jnp.exp(_att_scores(q, k, qi, j) - lse)
            dp = lax.dot_general(dob, v_ref[0, rows, :], _NT, preferred_element_type=F32)
            ds = (p * (dp - delta) * ATT_SCALE).astype(BF16)
            return dq + jnp.dot(ds, k, preferred_element_type=F32)

        dq_ref[0] = lax.fori_loop(0, qi + 1, step, jnp.zeros((ATT_BLK, MLA_QK_PAD), F32))
        dl_ref[0] = delta

    return pl.pallas_call(
        body, name=name, grid=(MLA_HEADS, nq),
        in_specs=[pl.BlockSpec((1, ATT_BLK, MLA_QK_PAD), lambda h, i: (h, i, 0)),
                  pl.BlockSpec((1, t, MLA_QK_PAD), lambda h, i: (h, 0, 0)),
                  pl.BlockSpec((1, t, MLA_V), lambda h, i: (h, 0, 0)),
                  pl.BlockSpec((ATT_BLK, MLA_V), lambda h, i: (i, h)),
                  pl.BlockSpec((ATT_BLK, MLA_V), lambda h, i: (i, MLA_HEADS + h)),
                  pl.BlockSpec((1, ATT_BLK, 1), lambda h, i: (h, i, 0))],
        out_specs=[pl.BlockSpec((1, ATT_BLK, MLA_QK_PAD), lambda h, i: (h, i, 0)),
                   pl.BlockSpec((1, ATT_BLK, 1), lambda h, i: (h, i, 0))],
        out_shape=[jax.ShapeDtypeStruct((MLA_HEADS, t, MLA_QK_PAD), F32),
                   jax.ShapeDtypeStruct((MLA_HEADS, t, 1), F32)],
        compiler_params=_cp("parallel", "parallel"),
    )(q3, k3, v3, o, dcat, lse)


def _attn_bwd_dkv(q3, k3, v3, dcat, lse, delta, *, name):
    t = q3.shape[1]
    nq = t // ATT_BLK

    def body(q_ref, k_ref, v_ref, do_ref, lse_ref, dl_ref, dk_ref, dv_ref):
        kj = pl.program_id(1)
        k, v = k_ref[0], v_ref[0]

        def step(i, carry):
            dk, dv = carry
            rows = pl.ds(pl.multiple_of(i * ATT_BLK, ATT_BLK), ATT_BLK)
            q = q_ref[0, rows, :]
            dob = do_ref[rows, :].astype(BF16)
            p = jnp.exp(_att_scores(q, k, i, kj) - lse_ref[0, rows, :])
            dv = dv + lax.dot_general(p.astype(BF16), dob, _TN, preferred_element_type=F32)
            dp = lax.dot_general(dob, v, _NT, preferred_element_type=F32)
            ds = (p * (dp - dl_ref[0, rows, :]) * ATT_SCALE).astype(BF16)
            dk = dk + lax.dot_general(ds, q, _TN, preferred_element_type=F32)
            return dk, dv

        init = (jnp.zeros((ATT_BLK, MLA_QK_PAD), F32), jnp.zeros((ATT_BLK, MLA_V), F32))
        dk, dv = lax.fori_loop(kj, nq, step, init)
        dk_ref[0] = dk
        dv_ref[0] = dv

    return pl.pallas_call(
        body, name=name, grid=(MLA_HEADS, nq),
        in_specs=[pl.BlockSpec((1, t, MLA_QK_PAD), lambda h, j: (h, 0, 0)),
                  pl.BlockSpec((1, ATT_BLK, MLA_QK_PAD), lambda h, j: (h, j, 0)),
                  pl.BlockSpec((1, ATT_BLK, MLA_V), lambda h, j: (h, j, 0)),
                  pl.BlockSpec((t, MLA_V), lambda h, j: (0, MLA_HEADS + h)),
                  pl.BlockSpec((1, t, 1), lambda h, j: (h, 0, 0)),
                  pl.BlockSpec((1, t, 1), lambda h, j: (h, 0, 0))],
        out_specs=[pl.BlockSpec((1, ATT_BLK, MLA_QK_PAD), lambda h, j: (h, j, 0)),
                   pl.BlockSpec((1, ATT_BLK, MLA_V), lambda h, j: (h, j, 0))],
        out_shape=[jax.ShapeDtypeStruct((MLA_HEADS, t, MLA_QK_PAD), F32),
                   jax.ShapeDtypeStruct((MLA_HEADS, t, MLA_V), F32)],
        compiler_params=_cp("parallel", "parallel"),
    )(q3, k3, v3, dcat, lse, delta)


def _ssd_prep(proj, bias128, alog128, *, name):
    t = proj.shape[0]
    nc = t // CHUNK

    def body(raw_ref, b_ref, al_ref, dt_ref, cs_ref, a_ref):
        xv = raw_ref[...] + b_ref[...]
        dt = jnp.maximum(xv, 0.0) + jnp.log(1.0 + jnp.exp(-jnp.abs(xv)))
        a = -jnp.exp(al_ref[...])
        adt = (dt * a).reshape(nc, CHUNK, LANES)
        li = lax.broadcasted_iota(jnp.int32, (nc, CHUNK, CHUNK), 1)
        si = lax.broadcasted_iota(jnp.int32, (nc, CHUNK, CHUNK), 2)
        tril = jnp.where(si <= li, 1.0, 0.0).astype(F32)
        cs = lax.dot_general(tril, adt, (((2,), (1,)), ((0,), (0,))), precision=HI, preferred_element_type=F32)
        dt_ref[...] = dt
        cs_ref[...] = cs.reshape(t, LANES)
        a_ref[...] = a

    blk = pl.BlockSpec((t, LANES), lambda i: (0, 0))
    row = pl.BlockSpec((1, LANES), lambda i: (0, 0))
    return pl.pallas_call(
        body, name=name, grid=(1,),
        in_specs=[pl.BlockSpec((t, LANES), lambda i: (0, OFF_DT // LANES)), row, row],
        out_specs=[blk, blk, row],
        out_shape=[jax.ShapeDtypeStruct((t, LANES), F32), jax.ShapeDtypeStruct((t, LANES), F32),
                   jax.ShapeDtypeStruct((1, LANES), F32)],
        compiler_params=_cp("arbitrary"),
    )(proj, bias128, alog128)


def _ssd_prep_bwd(ddt128, dadt128, proj, bias128, dt128, a128, dd_h, *, name):
    t = proj.shape[0]

    def body(ddt_ref, dadt_ref, raw_ref, b_ref, dt_ref, a_ref, dd_ref, draw_ref, db_ref, dal_ref, dds_ref):
        draw = ddt_ref[...] * _sigmoid(raw_ref[...] + b_ref[...])
        draw_ref[...] = draw.astype(BF16)
        db_ref[...] = jnp.sum(draw, axis=0, keepdims=True)
        dal_ref[...] = jnp.sum(dadt_ref[...] * dt_ref[...], axis=0, keepdims=True) * a_ref[...]
        dds_ref[...] = jnp.sum(dd_ref[...], axis=-1, keepdims=True)

    blk = pl.BlockSpec((t, LANES), lambda i: (0, 0))
    row = pl.BlockSpec((1, LANES), lambda i: (0, 0))
    return pl.pallas_call(
        body, name=name, grid=(1,),
        in_specs=[blk, blk, pl.BlockSpec((t, LANES), lambda i: (0, OFF_DT // LANES)), row, blk, row,
                  pl.BlockSpec((SSD_HEADS, SSD_P), lambda i: (0, 0))],
        out_specs=[blk, row, row, pl.BlockSpec((SSD_HEADS, 1), lambda i: (0, 0))],
        out_shape=[jax.ShapeDtypeStruct((t, LANES), BF16), jax.ShapeDtypeStruct((1, LANES), F32),
                   jax.ShapeDtypeStruct((1, LANES), F32), jax.ShapeDtypeStruct((SSD_HEADS, 1), F32)],
        compiler_params=_cp("arbitrary"),
    )(ddt128, dadt128, proj, bias128, dt128, a128, dd_h)


def _bdot(a, b, ca, cb, precision=None):
    return lax.dot_general(a, b, (((ca,), (cb,)), ((0,), (0,))), precision=precision, preferred_element_type=F32)


def _ssd_common(xs_ref, dt_ref, cs_ref, csr_ref, b_ref, c_ref, nc):
    x = xs_ref[0].reshape(nc, CHUNK, SSD_P)
    dt = dt_ref[0].reshape(nc, CHUNK, SSD_P)
    cs = cs_ref[0].reshape(nc, CHUNK, SSD_P)
    csr = csr_ref[0]
    bm = b_ref[0].reshape(nc, CHUNK, SSD_N).astype(BF16)
    cm = c_ref[0].reshape(nc, CHUNK, SSD_N).astype(BF16)
    li = lax.broadcasted_iota(jnp.int32, (nc, CHUNK, CHUNK), 1)
    si = lax.broadcasted_iota(jnp.int32, (nc, CHUNK, CHUNK), 2)
    lmat = jnp.exp(jnp.where(si <= li, cs - csr, NEG))
    g = _bdot(cm, bm, 2, 2)
    cs_last = jnp.sum(jnp.where(li == CHUNK - 1, cs, 0.0), axis=1, keepdims=True)
    xdt = x * dt
    dec = jnp.exp(cs_last - cs)
    return x, dt, cs, bm, cm, li, si, lmat, g, cs_last, xdt, dec


def _ssd_fwd(xs_h, dt_h, cs_h, cs_row, b_g, c_g, dskip_h, *, name):
    t = xs_h.shape[1]
    nc = t // CHUNK
    hpg = SSD_HEADS // SSD_GROUPS

    def body(xs_ref, dt_ref, cs_ref, csr_ref, b_ref, c_ref, dk_ref, y_ref, st_ref, sc_ref, cd_ref):
        x, dt, cs, bm, cm, li, si, lmat, g, cs_last, xdt, dec = _ssd_common(xs_ref, dt_ref, cs_ref, csr_ref, b_ref,
                                                                           c_ref, nc)
        yd = _bdot((g * lmat).astype(BF16), xdt.astype(BF16), 2, 1)
        sc_ref[...] = _bdot(bm, (dec * xdt).astype(BF16), 1, 1)
        cd_ref[...] = jnp.exp(cs_last)

        def step(c, s):
            st_ref[0, c] = s
            return s * cd_ref[c] + sc_ref[c]

        lax.fori_loop(0, nc, step, jnp.zeros((SSD_N, SSD_P), F32))
        yo = _bdot(cm, st_ref[0].astype(BF16), 2, 1) * jnp.exp(cs)
        y_ref[0] = (yd + yo + dk_ref[0] * x).reshape(t, SSD_P)

    head = pl.BlockSpec((1, t, SSD_P), lambda h: (h, 0, 0))
    grp = pl.BlockSpec((1, t, SSD_N), lambda h: (h // hpg, 0, 0))
    return pl.pallas_call(
        body, name=name, grid=(SSD_HEADS,),
        in_specs=[head, head, head, pl.BlockSpec((1, nc, 1, CHUNK), lambda h: (h, 0, 0, 0)), grp, grp,
                  pl.BlockSpec((1, 1, SSD_P), lambda h: (h, 0, 0))],
        out_specs=[head, pl.BlockSpec((1, nc, SSD_N, SSD_P), lambda h: (h, 0, 0, 0))],
        out_shape=[jax.ShapeDtypeStruct((SSD_HEADS, t, SSD_P), F32),
                   jax.ShapeDtypeStruct((SSD_HEADS, nc, SSD_N, SSD_P), F32)],
        scratch_shapes=[pltpu.VMEM((nc, SSD_N, SSD_P), F32), pltpu.VMEM((nc, 1, SSD_P), F32)],
        compiler_params=_cp("parallel"),
    )(xs_h, dt_h, cs_h, cs_row, b_g, c_g, dskip_h)


def _ssd_bwd(xs_h, dt_h, cs_h, cs_row, b_g, c_g, dskip_h, a_h, states, dy_h, *, name):
    t = xs_h.shape[1]
    nc = t // CHUNK
    hpg = SSD_HEADS // SSD_GROUPS

    def body(xs_ref, dt_ref, cs_ref, csr_ref, b_ref, c_ref, dk_ref, a_ref, st_ref, dy_ref,
             dxs_ref, ddt_ref, dadt_ref, db_ref, dc_ref, dd_ref, dsl_ref, dsc_ref, cd_ref):
        x, dt, cs, bm, cm, li, si, lmat, g, cs_last, xdt, dec = _ssd_common(xs_ref, dt_ref, cs_ref, csr_ref, b_ref,
                                                                           c_ref, nc)
        dy = dy_ref[0].reshape(nc, CHUNK, SSD_P)
        dyb = dy.astype(BF16)
        xdtb = xdt.astype(BF16)
        sprev = st_ref[0]
        sprevb = sprev.astype(BF16)
        cdec = jnp.exp(cs_last)
        ecs = jnp.exp(cs)
        dw = (ecs * dy).astype(BF16)
        wmat = _bdot(cm, sprevb, 2, 1)
        dcs = jnp.sum(dy * ecs * wmat, axis=2, keepdims=True)
        dcm = _bdot(dw, sprevb, 2, 2)
        dsl_ref[...] = _bdot(cm, dw, 1, 1)
        cd_ref[...] = cdec

        def step(k, ds):
            c = nc - 1 - k
            dsc_ref[c] = ds
            return ds * cd_ref[c] + dsl_ref[c]

        lax.fori_loop(0, nc, step, jnp.zeros((SSD_N, SSD_P), F32))
        dsc = dsc_ref[...]
        dscb = dsc.astype(BF16)
        d_last = jnp.sum(jnp.sum(dsc * sprev, axis=1, keepdims=True) * cdec, axis=2, keepdims=True)
        z = dec * xdt
        dbm = _bdot(z.astype(BF16), dscb, 2, 2)
        dz = _bdot(bm, dscb, 2, 1)
        dxdt = dec * dz
        t2 = jnp.sum(dz * z, axis=2, keepdims=True)
        dcs = dcs - t2
        d_last = d_last + jnp.sum(t2, axis=1, keepdims=True)
        m = g * lmat
        mb = m.astype(BF16)
        dm = _bdot(dyb, xdtb, 2, 2)
        dxdt = dxdt + _bdot(mb, dyb, 1, 1)
        dseg = dm * m
        dcs = dcs + jnp.sum(dseg, axis=2, keepdims=True)
        ones = jnp.ones((nc, CHUNK, SSD_P), F32)
        dcs = dcs - _bdot(dseg, ones, 1, 1, precision=HI)
        dg = (dm * lmat).astype(BF16)
        dcm = dcm + _bdot(dg, bm, 2, 1)
        dbm = dbm + _bdot(dg, cm, 1, 1)
        dcs = dcs + jnp.where(li[:, :, :SSD_P] == CHUNK - 1, d_last, 0.0)
        triu = jnp.where(li <= si, 1.0, 0.0).astype(F32)
        dadt = _bdot(triu, dcs, 2, 1, precision=HI)
        dk = dk_ref[0]
        dxs_ref[0] = (dxdt * dt + dk * dy).reshape(t, SSD_P)
        ddt_ref[0] = (jnp.sum(dxdt * x, axis=2, keepdims=True) + dadt * a_ref[0]).reshape(t, SSD_P)
        dadt_ref[0] = dadt.reshape(t, SSD_P)
        dd_ref[0] = jnp.sum(jnp.sum(dy * x, axis=1, keepdims=True), axis=0)

        @pl.when(pl.program_id(1) == 0)
        def _():
            db_ref[...] = jnp.zeros_like(db_ref)
            dc_ref[...] = jnp.zeros_like(dc_ref)

        db_ref[0] += dbm.reshape(t, SSD_N)
        dc_ref[0] += dcm.reshape(t, SSD_N)

    head = pl.BlockSpec((1, t, SSD_P), lambda gi, hi: (gi * hpg + hi, 0, 0))
    grp = pl.BlockSpec((1, t, SSD_N), lambda gi, hi: (gi, 0, 0))
    lane = pl.BlockSpec((1, 1, SSD_P), lambda gi, hi: (gi * hpg + hi, 0, 0))
    return pl.pallas_call(
        body, name=name, grid=(SSD_GROUPS, hpg),
        in_specs=[head, head, head, pl.BlockSpec((1, nc, 1, CHUNK), lambda gi, hi: (gi * hpg + hi, 0, 0, 0)),
                  grp, grp, lane, lane, pl.BlockSpec((1, nc, SSD_N, SSD_P), lambda gi, hi: (gi * hpg + hi, 0, 0, 0)),
                  head],
        out_specs=[head, head, head, grp, grp, lane],
        out_shape=[jax.ShapeDtypeStruct((SSD_HEADS, t, SSD_P), F32)] * 3
        + [jax.ShapeDtypeStruct((SSD_GROUPS, t, SSD_N), F32)] * 2
        + [jax.ShapeDtypeStruct((SSD_HEADS, 1, SSD_P), F32)],
        scratch_shapes=[pltpu.VMEM((nc, SSD_N, SSD_P), F32), pltpu.VMEM((nc, SSD_N, SSD_P), F32),
                        pltpu.VMEM((nc, 1, SSD_P), F32)],
        compiler_params=_cp("parallel", "arbitrary"),
    )(xs_h, dt_h, cs_h, cs_row, b_g, c_g, dskip_h, a_h, states, dy_h)


def _ssd_gate_fwd(y, proj, w, *, tr=256, name):
    t = y.shape[0]
    gw = D_SSM // SSD_GROUPS

    def body(y_ref, z_ref, w_ref, o_ref):
        v = y_ref[...] * _silu(z_ref[...])
        for gi in range(SSD_GROUPS):
            vg = v[:, gi * gw:(gi + 1) * gw]
            r = lax.rsqrt(jnp.mean(vg * vg, axis=-1, keepdims=True) + NORM_EPS)
            o_ref[:, gi * gw:(gi + 1) * gw] = (vg * r * w_ref[:, gi * gw:(gi + 1) * gw]).astype(BF16)

    blk = pl.BlockSpec((tr, D_SSM), lambda i: (i, 0))
    return pl.pallas_call(
        body, name=name, grid=(t // tr,), in_specs=[blk, blk, pl.BlockSpec((1, D_SSM), lambda i: (0, 0))],
        out_specs=blk, out_shape=jax.ShapeDtypeStruct((t, D_SSM), BF16), compiler_params=_cp("parallel"),
    )(y, proj, w)


def _ssd_gate_bwd(y, proj, w, dcat, *, tr=256, name):
    t = y.shape[0]
    gw = D_SSM // SSD_GROUPS

    def body(y_ref, z_ref, w_ref, d_ref, dy_ref, dz_ref, dw_ref):
        yv, zv, dv = y_ref[...], z_ref[...], d_ref[...].astype(F32)
        sz = _silu(zv)
        v = yv * sz

        @pl.when(pl.program_id(0) == 0)
        def _():
            dw_ref[...] = jnp.zeros_like(dw_ref)

        for gi in range(SSD_GROUPS):
            sl = slice(gi * gw, (gi + 1) * gw)
            vg, dg = v[:, sl], dv[:, sl]
            r = lax.rsqrt(jnp.mean(vg * vg, axis=-1, keepdims=True) + NORM_EPS)
            vh = vg * r
            gg = dg * w_ref[:, sl]
            dvg = r * (gg - vh * jnp.mean(gg * vh, axis=-1, keepdims=True))
            dy_ref[:, sl] = dvg * sz[:, sl]
            dz_ref[:, sl] = (dvg * yv[:, sl] * _dsilu(zv[:, sl])).astype(BF16)
            dw_ref[:, sl] += jnp.sum(dg * vh, axis=0, keepdims=True)

    blk = pl.BlockSpec((tr, D_SSM), lambda i: (i, 0))
    row = pl.BlockSpec((1, D_SSM), lambda i: (0, 0))
    return pl.pallas_call(
        body, name=name, grid=(t // tr,), in_specs=[blk, blk, row, blk], out_specs=[blk, blk, row],
        out_shape=[jax.ShapeDtypeStruct((t, D_SSM), F32), jax.ShapeDtypeStruct((t, D_SSM), BF16),
                   jax.ShapeDtypeStruct((1, D_SSM), F32)],
        compiler_params=_cp("arbitrary"),
    )(y, proj, w, dcat)


def _pad_lanes(v):
    return jnp.pad(v, ((0, 0), (0, LANES - v.shape[1])))


def _to_heads(v):
    return v.reshape(v.shape[0], SSD_HEADS, SSD_P).transpose(1, 0, 2)


def _from_heads(v):
    return v.transpose(1, 0, 2).reshape(v.shape[1], SSD_HEADS * SSD_P)


def _per_head(v128, t):
    return jnp.broadcast_to(v128[:, :SSD_HEADS].T[:, :, None], (SSD_HEADS, t, SSD_P))


def _ssd_forward(proj, conv_w, conv_b, dt_bias, a_log, d_skip, ssd_norm_w):
    t = proj.shape[0]
    nc = t // CHUNK
    xbc = _conv_act_fwd(proj, conv_w, conv_b, kw=SSD_CONV, glu=False, tc=512, coff=OFF_XBC // 512,
                        ncols=SSD_CONV_DIM, out_dtype=F32, name="ssd_conv_fwd")
    bias128, alog128 = _pad_lanes(dt_bias), _pad_lanes(a_log)
    dt128, cs128, a128 = _ssd_prep(proj, bias128, alog128, name="ssd_prep")
    dt_h, cs_h = _per_head(dt128, t), _per_head(cs128, t)
    cs_row = cs128[:, :SSD_HEADS].T.reshape(SSD_HEADS, nc, 1, CHUNK)
    xs_h = _to_heads(xbc[:, :D_SSM])
    gn = SSD_GROUPS * SSD_N
    b_g = xbc[:, D_SSM:D_SSM + gn].reshape(t, SSD_GROUPS, SSD_N).transpose(1, 0, 2)
    c_g = xbc[:, D_SSM + gn:].reshape(t, SSD_GROUPS, SSD_N).transpose(1, 0, 2)
    dskip_h = jnp.broadcast_to(d_skip[0][:, None, None], (SSD_HEADS, 1, SSD_P))
    a_h = jnp.broadcast_to(a128[0, :SSD_HEADS][:, None, None], (SSD_HEADS, 1, SSD_P))
    y_h, states = _ssd_fwd(xs_h, dt_h, cs_h, cs_row, b_g, c_g, dskip_h, name="ssd_scan_fwd")
    y = _from_heads(y_h)
    y_ssd = _ssd_gate_fwd(y, proj, ssd_norm_w, name="ssd_gate_fwd")
    saved = (proj, conv_w, conv_b, ssd_norm_w, bias128, dt128, a128, dt_h, cs_h, cs_row, xs_h, b_g, c_g, dskip_h, a_h,
             states, y)
    return y_ssd, saved


def _ssd_backward(saved, dcat):
    (proj, conv_w, conv_b, ssd_norm_w, bias128, dt128, a128, dt_h, cs_h, cs_row, xs_h, b_g, c_g, dskip_h, a_h, states,
     y) = saved
    t = proj.shape[0]
    dy, dz, d_norm_w = _ssd_gate_bwd(y, proj, ssd_norm_w, dcat, name="ssd_gate_bwd")
    dxs_h, ddt_h, dadt_h, db_g, dc_g, dd_h = _ssd_bwd(xs_h, dt_h, cs_h, cs_row, b_g, c_g, dskip_h, a_h, states,
                                                      _to_heads(dy), name="ssd_scan_bwd")
    gn = SSD_GROUPS * SSD_N
    dxc = jnp.concatenate([_from_heads(dxs_h), db_g.transpose(1, 0, 2).reshape(t, gn),
                           dc_g.transpose(1, 0, 2).reshape(t, gn)], axis=1)
    dxbc, d_conv_w, d_conv_b = _conv_act_bwd(proj, conv_w, conv_b, dxc, kw=SSD_CONV, glu=False, tc=512,
                                             coff=OFF_XBC // 512, ncols=SSD_CONV_DIM, name="ssd_conv_bwd")
    ddt128 = _pad_lanes(ddt_h[:, :, 0].T)
    dadt128 = _pad_lanes(dadt_h[:, :, 0].T)
    d_raw, d_bias, d_alog, d_dskip = _ssd_prep_bwd(ddt128, dadt128, proj, bias128, dt128, a128,
                                                   dd_h.reshape(SSD_HEADS, SSD_P), name="ssd_prep_bwd")
    return (dz, dxbc, d_raw, d_norm_w, d_conv_w, d_conv_b, d_bias[:, :SSD_HEADS], d_alog[:, :SSD_HEADS],
            d_dskip.reshape(1, SSD_HEADS))


def _rope_tables(positions):
    inv_freq = ROPE_THETA ** (-jnp.arange(0, MLA_ROPE, 2, dtype=F32) / MLA_ROPE)
    ang = positions[0].astype(F32)[:, None] * inv_freq
    cos, sin = jnp.cos(ang), jnp.sin(ang)
    z = jnp.zeros_like(cos)
    return jnp.stack([jnp.concatenate([cos, cos, z, z], axis=1), jnp.concatenate([-sin, z, z, z], axis=1),
                      jnp.concatenate([z, sin, z, z], axis=1)])


def _mla_forward(proj, tabs, q_a_norm_w, wq_pad, kv_a_norm_w, wkv):
    qn = _rmsnorm_fwd(proj, q_a_norm_w, width=MLA_Q_RANK, cblk=OFF_QA // MLA_Q_RANK, name="q_a_norm")
    q = _matmul(qn, wq_pad, name="q_b_proj")
    kvn = _rmsnorm_fwd(proj, kv_a_norm_w, width=MLA_KV_RANK, cblk=OFF_CKV // MLA_KV_RANK, name="kv_a_norm")
    kv = _matmul(kvn, wkv, name="kv_b_proj")
    q3, k3, v3 = _mla_prep(q, kv, proj, tabs, name="mla_prep")
    o, lse = _attn_fwd(q3, k3, v3, name="attn_fwd")
    return o, (proj, tabs, q_a_norm_w, wq_pad, kv_a_norm_w, wkv, qn, kvn, q3, k3, v3, o, lse)


def _mla_backward(saved, dcat):
    proj, tabs, q_a_norm_w, wq_pad, kv_a_norm_w, wkv, qn, kvn, q3, k3, v3, o, lse = saved
    dq3, delta = _attn_bwd_dq(q3, k3, v3, o, dcat, lse, name="attn_bwd_dq")
    dk3, dv3 = _attn_bwd_dkv(q3, k3, v3, dcat, lse, delta, name="attn_bwd_dkv")
    dq, dkv, dkr = _mla_unprep(dq3, dk3, dv3, tabs, name="mla_unprep")
    d_wq = _matmul(qn, dq, ta=True, out_dtype=BF16, name="d_w_q_b")
    dqn = _matmul(dq, wq_pad, tb=True, name="d_qn")
    dq_a, d_qnw = _rmsnorm_bwd(proj, q_a_norm_w, dqn, width=MLA_Q_RANK, cblk=OFF_QA // MLA_Q_RANK, out_dtype=BF16,
                               name="q_a_norm_bwd")
    d_wkv = _matmul(kvn, dkv, ta=True, out_dtype=BF16, name="d_w_kv_b")
    dkvn = _matmul(dkv, wkv, tb=True, name="d_kvn")
    dckv, d_kvnw = _rmsnorm_bwd(proj, kv_a_norm_w, dkvn, width=MLA_KV_RANK, cblk=OFF_CKV // MLA_KV_RANK,
                                out_dtype=BF16, name="kv_a_norm_bwd")
    return dq_a, dckv, dkr, d_wq, d_wkv, d_qnw, d_kvnw


def _pad_w_q(w):
    r = w.shape[0]
    w3 = w.reshape(r, MLA_HEADS, MLA_NOPE + MLA_ROPE)
    return jnp.pad(w3, ((0, 0), (0, 0), (0, MLA_QK_PAD - MLA_NOPE - MLA_ROPE))).reshape(r, MLA_HEADS * MLA_QK_PAD)


def _unpad_w_q(w):
    r = w.shape[0]
    return w.reshape(r, MLA_HEADS, MLA_QK_PAD)[:, :, :MLA_NOPE + MLA_ROPE].reshape(r, MLA_HEADS * (MLA_NOPE + MLA_ROPE))


def _pad_w_in(w):
    r = w.shape[0]
    o_dt = D_SSM + SSD_CONV_DIM
    o_qa = o_dt + SSD_HEADS
    o_kr = o_qa + MLA_Q_RANK + MLA_KV_RANK
    zeros = lambda n: jnp.zeros((r, n), w.dtype)
    return jnp.concatenate([w[:, :o_dt], w[:, o_qa:o_kr], w[:, o_kr:], zeros(LANES - MLA_ROPE),
                            w[:, o_dt:o_qa], zeros(LANES - SSD_HEADS)], axis=1)


def _unpad_w_in(w):
    return jnp.concatenate([w[:, :OFF_QA], w[:, OFF_DT:OFF_DT + SSD_HEADS], w[:, OFF_QA:OFF_KR + MLA_ROPE]], axis=1)


WEIGHTS = ['mix_norm_w', 'w_in', 'conv_w', 'conv_b', 'dt_bias', 'a_log', 'd_skip', 'ssd_norm_w', 'q_a_norm_w', 'w_q_b',
           'kv_a_norm_w', 'w_kv_b', 'w_out', 'ffn_norm_w', 'w_ffn_up', 'ffn_conv_w', 'ffn_conv_b', 'w_ffn_down',
           'ple_norm_w', 'w_ple_gate', 'b_ple_gate', 'w_ple_proj', 'ple_post_norm_w', 'final_norm_w']
BIG = ['w_in', 'w_q_b', 'w_kv_b', 'w_out', 'w_ffn_up', 'w_ffn_down', 'w_ple_gate', 'w_ple_proj']
COL_SHARDED = ('w_in', 'w_q_b', 'w_kv_b', 'w_ffn_up', 'w_ple_proj')
CONV = ['conv_w', 'ffn_conv_w']
REPL = [n for n in WEIGHTS if n not in BIG and n not in CONV]
FFN_INV = tuple(int(i) for i in np.argsort(FFN_PERM))


def _cat_cols(g):
    return g.transpose(1, 0, 2).reshape(g.shape[1], N_DEV * g.shape[2])


def _split_cols(w):
    return w.reshape(w.shape[0], N_DEV, w.shape[1] // N_DEV).transpose(1, 0, 2)


def _interleave(v):
    r = v.shape[0]
    return v.reshape(r, N_DEV, FFN_TC)[:, jnp.array(FFN_PERM)].reshape(r, N_DEV * FFN_TC)


def _deinterleave(v):
    r = v.shape[0]
    return v.reshape(r, N_DEV, FFN_TC)[:, jnp.array(FFN_INV)].reshape(r, N_DEV * FFN_TC)


def _assemble_weights(g):
    return {
        'w_in': _pad_w_in(_cat_cols(g['w_in'])),
        'w_q_b': _pad_w_q(_cat_cols(g['w_q_b'])),
        'w_kv_b': _cat_cols(g['w_kv_b']),
        'w_out': g['w_out'].reshape(D_MODEL, D_MODEL),
        'w_ffn_up': g['w_ffn_up'],
        'w_ffn_down': g['w_ffn_down'].reshape(D_FF, D_MODEL),
        'w_ple_gate': g['w_ple_gate'].reshape(D_MODEL, D_MODEL),
        'w_ple_proj': _cat_cols(g['w_ple_proj']),
        'conv_w': _cat_cols(g['conv_w']),
        'ffn_conv_w': _interleave(_cat_cols(g['ffn_conv_w'])),
    }


def _ffn_perm(j):
    return (j % 2) * (N_DEV // 2) + j // 2


def _local_step(x, p, tabs, w, s, target):
    t = x.shape[0]
    half = D_MODEL // 2
    up_cols = 2 * D_FF
    ffn_conv_b = _interleave(s['ffn_conv_b'])
    h = _rmsnorm_fwd(x, s['mix_norm_w'], width=D_MODEL, name="mix_norm")
    proj = _matmul(h, w['w_in'], name="in_proj")
    y_ssd, ssd_saved = _ssd_forward(proj, w['conv_w'], s['conv_b'], s['dt_bias'], s['a_log'], s['d_skip'],
                                    s['ssd_norm_w'])
    o, mla_saved = _mla_forward(proj, tabs, s['q_a_norm_w'], w['w_q_b'], s['kv_a_norm_w'], w['w_kv_b'])
    tk_o, tn_o = _tile(half, 512), _tile(D_MODEL, 1024)
    x1 = _matmul(y_ssd, w['w_out'], add=x, mnk=(t, D_MODEL, half), name="out_proj_ssd")
    x1 = _matmul(o, w['w_out'], add=x1, mnk=(t, D_MODEL, half), name="out_proj_mla",
                 b_spec=pl.BlockSpec((tk_o, tn_o), lambda i, j, kk: (kk + half // tk_o, j)))
    hf = _rmsnorm_fwd(x1, s['ffn_norm_w'], width=D_MODEL, name="ffn_norm")
    tk_u = _tile(D_MODEL, 512)
    u = _matmul(hf, w['w_ffn_up'], mnk=(t, up_cols, D_MODEL), tn=FFN_TC, name="ffn_up",
                b_spec=pl.BlockSpec((1, tk_u, FFN_TC), lambda i, j, kk: (_ffn_perm(j), kk, 0)))
    act = _conv_act_fwd(u, w['ffn_conv_w'], ffn_conv_b, kw=FFN_CONV, glu=True, tc=2 * FFN_TC, coff=0, ncols=up_cols,
                        out_dtype=BF16, name="ffn_act")
    x2 = _matmul(act, w['w_ffn_down'], add=x1, name="ffn_down")
    hp = _rmsnorm_fwd(x2, s['ple_norm_w'], width=D_MODEL, name="ple_norm")
    gl = _matmul(hp, w['w_ple_gate'], bias=s['b_ple_gate'], name="ple_gate")
    pe = _matmul(p, w['w_ple_proj'], name="ple_proj")
    x3 = _ple_fwd(x2, gl, pe, s['ple_post_norm_w'], name="ple_mix")
    loss, dx3, d_final = _loss_head(x3, s['final_norm_w'], target, name="loss_head")
    dgl, d_bgate, dpe, d_post = _ple_bwd(dx3, gl, pe, s['ple_post_norm_w'], name="ple_mix_bwd")
    d_wproj = _matmul(p, dpe, ta=True, out_dtype=BF16, name="d_w_ple_proj")
    d_wgate = _matmul(hp, dgl, ta=True, out_dtype=BF16, name="d_w_ple_gate")
    dhp = _matmul(dgl, w['w_ple_gate'], tb=True, name="d_ple_normed")
    dx2, d_plenorm = _rmsnorm_bwd(x2, s['ple_norm_w'], dhp, dx3, width=D_MODEL, name="ple_norm_bwd")
    dact = _matmul(dx2, w['w_ffn_down'], tb=True, name="d_ffn_act")
    d_wdown = _matmul(act, dx2, ta=True, out_dtype=BF16, name="d_w_ffn_down")
    du, d_fconv_w, d_fconv_b = _conv_act_bwd(u, w['ffn_conv_w'], ffn_conv_b, dact, kw=FFN_CONV, glu=True,
                                             tc=2 * FFN_TC, coff=0, ncols=up_cols, name="ffn_act_bwd")
    tm_u = _tile(D_MODEL, 1024)
    d_wup = _matmul(hf, du, ta=True, out_dtype=BF16, mnk=(D_MODEL, up_cols, t), tn=FFN_TC, name="d_w_ffn_up",
                    o_spec=pl.BlockSpec((1, tm_u, FFN_TC), lambda i, j, kk: (_ffn_perm(j), i, 0)),
                    o_shape=(N_DEV, D_MODEL, FFN_TC))
    dhf = _matmul(du, w['w_ffn_up'], tb=True, mnk=(t, D_MODEL, up_cols), tk=FFN_TC, name="d_ffn_normed",
                  b_spec=pl.BlockSpec((1, tn_o, FFN_TC), lambda i, j, kk: (_ffn_perm(kk), j, 0)))
    dx1, d_ffnnorm = _rmsnorm_bwd(x1, s['ffn_norm_w'], dhf, dx2, width=D_MODEL, name="ffn_norm_bwd")
    dcat = _matmul(dx1, w['w_out'], tb=True, name="d_mixed")
    d_wout = jnp.concatenate([_matmul(y_ssd, dx1, ta=True, out_dtype=BF16, name="d_w_out_ssd"),
                              _matmul(o, dx1, ta=True, out_dtype=BF16, name="d_w_out_mla")], axis=0)
    dz, dxbc, d_raw, d_ssdnorm, d_conv_w, d_conv_b, d_dtb, d_alog, d_dskip = _ssd_backward(ssd_saved, dcat)
    dq_a, dckv, dkr, d_wq, d_wkv, d_qnorm, d_kvnorm = _mla_backward(mla_saved, dcat)
    dproj = jnp.concatenate([dz, dxbc, dq_a, dckv, dkr, d_raw], axis=1)
    d_win = _matmul(h, dproj, ta=True, out_dtype=BF16, name="d_w_in")
    dh = _matmul(dproj, w['w_in'], tb=True, name="d_in_normed")
    dx, d_mixnorm = _rmsnorm_bwd(x, s['mix_norm_w'], dh, dx1, width=D_MODEL, name="mix_norm_bwd")
    big = {
        'w_in': _split_cols(_unpad_w_in(d_win)), 'w_q_b': _split_cols(_unpad_w_q(d_wq)), 'w_kv_b': _split_cols(d_wkv),
        'w_out': d_wout.reshape(N_DEV, D_MODEL // N_DEV, D_MODEL), 'w_ffn_up': d_wup,
        'w_ffn_down': d_wdown.reshape(N_DEV, D_FF // N_DEV, D_MODEL),
        'w_ple_gate': d_wgate.reshape(N_DEV, D_MODEL // N_DEV, D_MODEL), 'w_ple_proj': _split_cols(d_wproj),
    }
    conv = {'conv_w': d_conv_w, 'ffn_conv_w': _deinterleave(d_fconv_w)}
    vec = {
        'mix_norm_w': d_mixnorm, 'conv_b': d_conv_b, 'dt_bias': d_dtb, 'a_log': d_alog, 'd_skip': d_dskip,
        'ssd_norm_w': d_ssdnorm, 'q_a_norm_w': d_qnorm, 'kv_a_norm_w': d_kvnorm, 'ffn_norm_w': d_ffnnorm,
        'ffn_conv_b': _deinterleave(d_fconv_b), 'ple_norm_w': d_plenorm, 'b_ple_gate': d_bgate,
        'ple_post_norm_w': d_post, 'final_norm_w': d_final,
    }
    return loss, dx, big, conv, vec


MESH = pl.DeviceIdType.MESH
FLIPS = ((0, 0, 1), (1, 0, 0), (0, 1, 0), (1, 1, 0), (1, 0, 1), (0, 1, 1), (1, 1, 1))


def _exchange(items, *, gather, name):
    n = len(items)

    def body(*refs):
        ins, outs = refs[:n], refs[n:2 * n]
        send_sems, recv_sems, local_sems = refs[2 * n:]
        x, y, c = lax.axis_index("x"), lax.axis_index("y"), lax.axis_index("c")
        me = 4 * x + 2 * y + c
        peers = [(jnp.where(fx, 1 - x, x), jnp.where(fy, 1 - y, y), jnp.where(fc, 1 - c, c)) for fx, fy, fc in FLIPS]
        slot = [4 * px + 2 * py + pc for px, py, pc in peers]
        local, sends = [], []
        for wi in range(n):
            cp = pltpu.make_async_copy(ins[wi] if gather else ins[wi].at[me], outs[wi].at[me], local_sems.at[wi])
            cp.start()
            local.append(cp)
            for k, peer in enumerate(peers):
                cp = pltpu.make_async_remote_copy(
                    src_ref=ins[wi] if gather else ins[wi].at[slot[k]], dst_ref=outs[wi].at[me],
                    send_sem=send_sems.at[k, wi], recv_sem=recv_sems.at[k, wi], device_id=peer, device_id_type=MESH)
                cp.start()
                sends.append(cp)
        for wi in range(n):
            for k, peer in enumerate(peers):
                pltpu.make_async_remote_copy(
                    src_ref=outs[wi].at[slot[k]], dst_ref=outs[wi].at[slot[k]], send_sem=send_sems.at[k, wi],
                    recv_sem=recv_sems.at[k, wi], device_id=peer, device_id_type=MESH).wait_recv()
        for cp in sends:
            cp.wait_send()
        for cp in local:
            cp.wait()

    hbm = pl.BlockSpec(memory_space=pltpu.HBM)
    out_shape = [jax.ShapeDtypeStruct(((N_DEV,) + v.shape) if gather else v.shape, v.dtype) for v in items]
    return pl.pallas_call(
        body, name=name, in_specs=[hbm] * n, out_specs=[hbm] * n, out_shape=out_shape,
        scratch_shapes=[pltpu.SemaphoreType.DMA((len(FLIPS), n)), pltpu.SemaphoreType.DMA((len(FLIPS), n)),
                        pltpu.SemaphoreType.DMA((n,))],
    )(*items)


def _adamw(parts, w, m, v, *, name):
    r, c = w.shape
    tr = max(d for d in range(HALO, 129, HALO) if r % d == 0) if r > 128 else r

    def body(p_ref, w_ref, m_ref, v_ref, g_ref, d_ref, mo_ref, vo_ref):
        g = p_ref[0].astype(F32)
        for k in range(1, N_DEV):
            g = g + p_ref[k].astype(F32)
        mn = ADAM_B1 * m_ref[...] + (1.0 - ADAM_B1) * g
        vn = ADAM_B2 * v_ref[...] + (1.0 - ADAM_B2) * (g * g)
        m_hat = mn / (1.0 - ADAM_B1 ** ADAM_STEP)
        v_hat = vn / (1.0 - ADAM_B2 ** ADAM_STEP)
        g_ref[...] = g
        d_ref[...] = -ADAM_LR * (m_hat / (jnp.sqrt(v_hat) + ADAM_EPS) + ADAM_WD * w_ref[...])
        mo_ref[...] = mn
        vo_ref[...] = vn

    blk = pl.BlockSpec((tr, c), lambda i: (i, 0))
    return pl.pallas_call(
        body, name=name, grid=(r // tr,), in_specs=[pl.BlockSpec((N_DEV, tr, c), lambda i: (0, i, 0)), blk, blk, blk],
        out_specs=[blk] * 4, out_shape=[jax.ShapeDtypeStruct((r, c), F32)] * 4, compiler_params=_cp("parallel"),
    )(parts, w, m, v)


def _pack_rows(vs, rows):
    lead = vs[0].shape[:-1] if vs[0].ndim > 1 else ()
    flat = jnp.concatenate(vs, axis=-1)
    pad = rows * LANES - flat.shape[-1]
    flat = jnp.pad(flat, [(0, 0)] * len(lead) + [(0, pad)])
    return flat.reshape(lead + (rows, LANES))


def kernel(x, p, positions, mix_norm_w, w_in, conv_w, conv_b, dt_bias, a_log, d_skip, ssd_norm_w, q_a_norm_w, w_q_b, kv_a_norm_w, w_kv_b, w_out, ffn_norm_w, w_ffn_up, ffn_conv_w, ffn_conv_b, w_ffn_down, ple_norm_w, w_ple_gate, b_ple_gate, w_ple_proj, ple_post_norm_w, final_norm_w, loss_target, m_mix_norm_w, m_w_in, m_conv_w, m_conv_b, m_dt_bias, m_a_log, m_d_skip, m_ssd_norm_w, m_q_a_norm_w, m_w_q_b, m_kv_a_norm_w, m_w_kv_b, m_w_out, m_ffn_norm_w, m_w_ffn_up, m_ffn_conv_w, m_ffn_conv_b, m_w_ffn_down, m_ple_norm_w, m_w_ple_gate, m_b_ple_gate, m_w_ple_proj, m_ple_post_norm_w, m_final_norm_w, v_mix_norm_w, v_w_in, v_conv_w, v_conv_b, v_dt_bias, v_a_log, v_d_skip, v_ssd_norm_w, v_q_a_norm_w, v_w_q_b, v_kv_a_norm_w, v_w_kv_b, v_w_out, v_ffn_norm_w, v_w_ffn_up, v_ffn_conv_w, v_ffn_conv_b, v_w_ffn_down, v_ple_norm_w, v_w_ple_gate, v_b_ple_gate, v_w_ple_proj, v_ple_post_norm_w, v_final_norm_w):
    given = dict(locals())
    shapes = {n: given[n].shape for n in WEIGHTS}
    w2 = {n: given[n].reshape(given[n].shape[-2:] if n in BIG or n in CONV else (1, -1)) for n in WEIGHTS}
    m2 = {n: given['m_' + n].reshape(w2[n].shape) for n in WEIGHTS}
    v2 = {n: given['v_' + n].reshape(w2[n].shape) for n in WEIGHTS}
    me = 4 * lax.axis_index("x") + 2 * lax.axis_index("y") + lax.axis_index("c")

    gathered = _exchange([w2[n].astype(BF16) for n in BIG] + [w2[n] for n in CONV], gather=True, name="gather_weights")
    wts = _assemble_weights(dict(zip(BIG + CONV, gathered)))
    vecs = {n: w2[n] for n in REPL}
    loss, dx, g_big, g_conv, g_vec = _local_step(x[0], p[0, 0], _rope_tables(positions), wts, vecs, loss_target[0])

    received = _exchange([g_big[n] for n in BIG], gather=False, name="scatter_grads")
    n_small = sum(g_vec[n].shape[1] for n in REPL) + sum(g_conv[n].size for n in CONV) + 1
    rows_small = -(-n_small // (LANES * HALO)) * HALO
    small = _pack_rows([g_vec[n] for n in REPL] + [g_conv[n].reshape(1, -1) for n in CONV] + [loss], rows_small)
    all_small = _exchange([small], gather=True, name="gather_small_grads")[0].reshape(N_DEV, rows_small * LANES)

    out_g, out_d, out_m, out_v = {}, {}, {}, {}
    for n, parts in zip(BIG, received):
        out_g[n], out_d[n], out_m[n], out_v[n] = _adamw(parts, w2[n], m2[n], v2[n], name="adamw_" + n)
    pieces, off = [], 0
    for n in REPL:
        k = g_vec[n].shape[1]
        pieces.append(all_small[:, off:off + k])
        off += k
    for n in CONV:
        kw, cols = g_conv[n].shape
        full = all_small[:, off:off + kw * cols].reshape(N_DEV, kw, cols)
        mine = lax.dynamic_slice_in_dim(full, me * (cols // N_DEV), cols // N_DEV, axis=2)
        pieces.append(mine.reshape(N_DEV, kw * (cols // N_DEV)))
        off += kw * cols
    pieces.append(all_small[:, off:off + 1])
    small_names = REPL + CONV
    n_mine = sum(q.shape[1] for q in pieces)
    rows_mine = -(-n_mine // (LANES * HALO)) * HALO
    zero = jnp.zeros((1, 1), F32)
    packed = [_pack_rows([src[n].reshape(1, -1) for n in small_names] + [zero], rows_mine).reshape(rows_mine, LANES)
              for src in (w2, m2, v2)]
    sg, sd, sm, sv = _adamw(_pack_rows(pieces, rows_mine), *packed, name="adamw_small")
    off = 0
    for n in small_names:
        k = w2[n].size
        for dst, src in ((out_g, sg), (out_d, sd), (out_m, sm), (out_v, sv)):
            dst[n] = src.reshape(-1)[off:off + k].reshape(w2[n].shape)
        off += k
    total_loss = sg.reshape(-1)[off]

    outs = [total_loss, dx[None]]
    for res in (out_g, out_d, out_m, out_v):
        outs += [res[n].reshape(shapes[n]) for n in WEIGHTS]
    return tuple(outs)
```

```python
import functools
import math

import numpy as np
import jax
import jax.numpy as jnp
from jax import lax
from jax.experimental import pallas as pl
from jax.experimental.pallas import tpu as pltpu

F32 = jnp.float32
BF16 = jnp.bfloat16
HI = lax.Precision.HIGHEST

D_MODEL = 2048
CHUNK = 64
D_SSM = 1024
SSD_P = 64
SSD_HEADS = 16
SSD_GROUPS = 2
SSD_N = 128
SSD_CONV = 4
SSD_CONV_DIM = D_SSM + 2 * SSD_GROUPS * SSD_N
MLA_HEADS = 8
MLA_NOPE = 128
MLA_ROPE = 64
MLA_V = 128
MLA_Q_RANK = 512
MLA_KV_RANK = 256
MLA_QK_PAD = 256
ROPE_THETA = 10000.0
D_FF = 5632
FFN_CONV = 3
PLE_DIM = 256
NORM_EPS = 1e-6
ADAM_LR, ADAM_B1, ADAM_B2, ADAM_EPS, ADAM_WD, ADAM_STEP = 0.001, 0.9, 0.999, 1e-08, 0.01, 10
N_DEV = 8

OFF_Z, OFF_XBC, OFF_QA, OFF_CKV, OFF_KR, OFF_DT, D_IN_PAD = 0, 1024, 2560, 3072, 3328, 3456, 3584
D_IN = 3408
LANES = 128
HALO = 8
VMEM_LIMIT = 56 * 1024 * 1024
FFN_TC = D_FF * 2 // N_DEV
FFN_PERM = (0, 4, 1, 5, 2, 6, 3, 7)
NEG = -1e30


def _cp(*sem):
    return pltpu.CompilerParams(dimension_semantics=tuple(sem), vmem_limit_bytes=VMEM_LIMIT)


def _tile(n, want):
    if n <= want:
        return n
    best = max(d for d in range(LANES, want + 1, LANES) if n % d == 0)
    return best


def _sigmoid(x):
    return 1.0 / (1.0 + jnp.exp(-x))


def _silu(x):
    return x * _sigmoid(x)


def _dsilu(x):
    s = _sigmoid(x)
    return s * (1.0 + x * (1.0 - s))


def _matmul(a, b, *, ta=False, tb=False, out_dtype=F32, add=None, bias=None, tm=1024, tn=1024, tk=512, name,
            mnk=None, a_spec=None, b_spec=None, o_spec=None, o_shape=None):
    if mnk is None:
        m, k = (a.shape[1], a.shape[0]) if ta else a.shape
        n = b.shape[0] if tb else b.shape[1]
        assert k == (b.shape[1] if tb else b.shape[0])
    else:
        m, n, k = mnk
    tm, tn, tk = _tile(m, tm), _tile(n, tn), _tile(k, tk)
    nk = k // tk
    dims = (((0 if ta else 1,), (1 if tb else 0,)), ((), ()))

    def body(*refs):
        a_ref, b_ref = refs[0], refs[1]
        pos = 2
        add_ref = bias_ref = None
        if add is not None:
            add_ref = refs[pos]
            pos += 1
        if bias is not None:
            bias_ref = refs[pos]
            pos += 1
        o_ref, acc_ref = refs[pos], refs[pos + 1]
        kk = pl.program_id(2)

        @pl.when(kk == 0)
        def _():
            acc_ref[...] = jnp.zeros_like(acc_ref)

        av = a_ref[...]
        bv = b_ref[...]
        av = av.reshape(av.shape[-2:]).astype(BF16)
        bv = bv.reshape(bv.shape[-2:]).astype(BF16)
        acc_ref[...] += lax.dot_general(av, bv, dims, preferred_element_type=F32)

        @pl.when(kk == nk - 1)
        def _():
            r = acc_ref[...]
            if bias_ref is not None:
                r = r + bias_ref[...]
            if add_ref is not None:
                r = r + add_ref[...].astype(F32)
            o_ref[...] = r.astype(out_dtype).reshape(o_ref.shape)

    if a_spec is None:
        a_spec = (pl.BlockSpec((tk, tm), lambda i, j, kk: (kk, i)) if ta
                  else pl.BlockSpec((tm, tk), lambda i, j, kk: (i, kk)))
    if b_spec is None:
        b_spec = (pl.BlockSpec((tn, tk), lambda i, j, kk: (j, kk)) if tb
                  else pl.BlockSpec((tk, tn), lambda i, j, kk: (kk, j)))
    if o_spec is None:
        o_spec = pl.BlockSpec((tm, tn), lambda i, j, kk: (i, j))
    if o_shape is None:
        o_shape = (m, n)
    in_specs = [a_spec, b_spec]
    args = [a, b]
    if add is not None:
        in_specs.append(pl.BlockSpec((tm, tn), lambda i, j, kk: (i, j)))
        args.append(add)
    if bias is not None:
        in_specs.append(pl.BlockSpec((1, tn), lambda i, j, kk: (0, j)))
        args.append(bias)
    return pl.pallas_call(
        body, name=name, grid=(m // tm, n // tn, nk), in_specs=in_specs, out_specs=o_spec,
        out_shape=jax.ShapeDtypeStruct(o_shape, out_dtype),
        scratch_shapes=[pltpu.VMEM((tm, tn), F32)],
        compiler_params=_cp("parallel", "parallel", "arbitrary"),
    )(*args)


def _rmsnorm_fwd(x, w, *, width, cblk=0, out_dtype=BF16, tr=256, name):
    t = x.shape[0]

    def body(x_ref, w_ref, o_ref):
        xv = x_ref[...].astype(F32)
        r = lax.rsqrt(jnp.mean(xv * xv, axis=-1, keepdims=True) + NORM_EPS)
        o_ref[...] = (xv * r * w_ref[...]).astype(out_dtype)

    return pl.pallas_call(
        body, name=name, grid=(t // tr,),
        in_specs=[pl.BlockSpec((tr, width), lambda i: (i, cblk)), pl.BlockSpec((1, width), lambda i: (0, 0))],
        out_specs=pl.BlockSpec((tr, width), lambda i: (i, 0)),
        out_shape=jax.ShapeDtypeStruct((t, width), out_dtype),
        compiler_params=_cp("parallel"),
    )(x, w)


def _rmsnorm_bwd(x, w, dy, add=None, *, width, cblk=0, out_dtype=F32, tr=256, name):
    t = x.shape[0]

    def body(*refs):
        if add is None:
            x_ref, w_ref, dy_ref, dx_ref, dw_ref = refs
            add_ref = None
        else:
            x_ref, w_ref, dy_ref, add_ref, dx_ref, dw_ref = refs
        xv = x_ref[...].astype(F32)
        dyv = dy_ref[...].astype(F32)
        r = lax.rsqrt(jnp.mean(xv * xv, axis=-1, keepdims=True) + NORM_EPS)
        xh = xv * r
        g = dyv * w_ref[...]
        dx = r * (g - xh * jnp.mean(g * xh, axis=-1, keepdims=True))
        if add_ref is not None:
            dx = dx + add_ref[...].astype(F32)
        dx_ref[...] = dx.astype(out_dtype)

        @pl.when(pl.program_id(0) == 0)
        def _():
            dw_ref[...] = jnp.zeros_like(dw_ref)

        dw_ref[...] += jnp.sum(dyv * xh, axis=0, keepdims=True)

    in_specs = [pl.BlockSpec((tr, width), lambda i: (i, cblk)), pl.BlockSpec((1, width), lambda i: (0, 0)),
                pl.BlockSpec((tr, width), lambda i: (i, 0))]
    args = [x, w, dy]
    if add is not None:
        in_specs.append(pl.BlockSpec((tr, width), lambda i: (i, 0)))
        args.append(add)
    return pl.pallas_call(
        body, name=name, grid=(t // tr,), in_specs=in_specs,
        out_specs=[pl.BlockSpec((tr, width), lambda i: (i, 0)), pl.BlockSpec((1, width), lambda i: (0, 0))],
        out_shape=[jax.ShapeDtypeStruct((t, width), out_dtype), jax.ShapeDtypeStruct((1, width), F32)],
        compiler_params=_cp("arbitrary"),
    )(*args)


def _shift_down(prev_halo, cur, j):
    if j == 0:
        return cur
    ext = jnp.concatenate([prev_halo, cur], axis=0)
    return pltpu.roll(ext, j, axis=0)[HALO:]


def _shift_up(cur, next_halo, j):
    if j == 0:
        return cur
    ext = jnp.concatenate([cur, next_halo], axis=0)
    return pltpu.roll(ext, ext.shape[0] - j, axis=0)[:cur.shape[0]]


def _conv_rows(prev, cur, w, b, kw):
    shifted = [cur]
    out = b + w[kw - 1:kw] * cur
    for j in range(1, kw):
        sh = _shift_down(prev, cur, j)
        shifted.append(sh)
        out = out + w[kw - 1 - j:kw - j] * sh
    return out, shifted


def _act_fwd(c, glu):
    if glu:
        half = c.shape[1] // 2
        return _silu(c[:, :half]) * c[:, half:]
    return _silu(c)


def _act_bwd(c, dout, glu):
    if glu:
        half = c.shape[1] // 2
        g, up = c[:, :half], c[:, half:]
        return jnp.concatenate([dout * up * _dsilu(g), dout * _silu(g)], axis=1)
    return dout * _dsilu(c)


def _conv_act_fwd(u, w, b, *, kw, glu, tc, coff, ncols, out_dtype, tr=256, name):
    t = u.shape[0]
    nb = ncols // tc
    oc = tc // 2 if glu else tc

    def body(u_ref, uh_ref, w_ref, b_ref, o_ref):
        prev = jnp.where(pl.program_id(0) == 0, 0.0, uh_ref[...])
        c, _ = _conv_rows(prev, u_ref[...], w_ref[...], b_ref[...], kw)
        o_ref[...] = _act_fwd(c, glu).astype(out_dtype)

    return pl.pallas_call(
        body, name=name, grid=(t // tr, nb),
        in_specs=[pl.BlockSpec((tr, tc), lambda i, j: (i, j + coff)),
                  pl.BlockSpec((HALO, tc), lambda i, j: (jnp.maximum(i * (tr // HALO) - 1, 0), j + coff)),
                  pl.BlockSpec((kw, tc), lambda i, j: (0, j)), pl.BlockSpec((1, tc), lambda i, j: (0, j))],
        out_specs=pl.BlockSpec((tr, oc), lambda i, j: (i, j)),
        out_shape=jax.ShapeDtypeStruct((t, nb * oc), out_dtype),
        compiler_params=_cp("parallel", "parallel"),
    )(u, u, w, b)


def _conv_act_bwd(u, w, b, dout, *, kw, glu, tc, coff, ncols, tr=256, name):
    t = u.shape[0]
    nb = ncols // tc
    nt = t // tr
    oc = tc // 2 if glu else tc

    def body(u_ref, up_ref, un_ref, d_ref, dn_ref, w_ref, b_ref, du_ref, dw_ref, db_ref):
        i = pl.program_id(1)
        cur, nxt, wv, bv = u_ref[...], un_ref[...], w_ref[...], b_ref[...]
        prev = jnp.where(i == 0, 0.0, up_ref[...])
        c_cur, shifted = _conv_rows(prev, cur, wv, bv, kw)
        c_nxt, _ = _conv_rows(cur[tr - HALO:], nxt, wv, bv, kw)
        d_cur = _act_bwd(c_cur, d_ref[...].astype(F32), glu)
        d_nxt = _act_bwd(c_nxt, jnp.where(i == nt - 1, 0.0, dn_ref[...].astype(F32)), glu)
        du = wv[kw - 1:kw] * d_cur
        for j in range(1, kw):
            du = du + wv[kw - 1 - j:kw - j] * _shift_up(d_cur, d_nxt, j)
        du_ref[...] = du.astype(BF16)

        @pl.when(i == 0)
        def _():
            dw_ref[...] = jnp.zeros_like(dw_ref)
            db_ref[...] = jnp.zeros_like(db_ref)

        db_ref[...] += jnp.sum(d_cur, axis=0, keepdims=True)
        dw_ref[...] += jnp.concatenate(
            [jnp.sum(d_cur * shifted[kw - 1 - k], axis=0, keepdims=True) for k in range(kw)], axis=0)

    nh = tr // HALO
    return pl.pallas_call(
        body, name=name, grid=(nb, nt),
        in_specs=[pl.BlockSpec((tr, tc), lambda j, i: (i, j + coff)),
                  pl.BlockSpec((HALO, tc), lambda j, i: (jnp.maximum(i * nh - 1, 0), j + coff)),
                  pl.BlockSpec((HALO, tc), lambda j, i: (jnp.minimum((i + 1) * nh, t // HALO - 1), j + coff)),
                  pl.BlockSpec((tr, oc), lambda j, i: (i, j)),
                  pl.BlockSpec((HALO, oc), lambda j, i: (jnp.minimum((i + 1) * nh, t // HALO - 1), j)),
                  pl.BlockSpec((kw, tc), lambda j, i: (0, j)), pl.BlockSpec((1, tc), lambda j, i: (0, j))],
        out_specs=[pl.BlockSpec((tr, tc), lambda j, i: (i, j)), pl.BlockSpec((kw, tc), lambda j, i: (0, j)),
                   pl.BlockSpec((1, tc), lambda j, i: (0, j))],
        out_shape=[jax.ShapeDtypeStruct((t, ncols), BF16), jax.ShapeDtypeStruct((kw, ncols), F32),
                   jax.ShapeDtypeStruct((1, ncols), F32)],
        compiler_params=_cp("parallel", "arbitrary"),
    )(u, u, u, dout, dout, w, b)


def _ple_fwd(x2, gl, pe, pw, *, tr=256, name):
    t, d = x2.shape

    def body(x_ref, gl_ref, pe_ref, pw_ref, o_ref):
        pv = pe_ref[...]
        r = lax.rsqrt(jnp.mean(pv * pv, axis=-1, keepdims=True) + NORM_EPS)
        o_ref[...] = x_ref[...] + _sigmoid(gl_ref[...]) * (pv * r * pw_ref[...])

    blk = pl.BlockSpec((tr, d), lambda i: (i, 0))
    return pl.pallas_call(
        body, name=name, grid=(t // tr,), in_specs=[blk, blk, blk, pl.BlockSpec((1, d), lambda i: (0, 0))],
        out_specs=blk, out_shape=jax.ShapeDtypeStruct((t, d), F32), compiler_params=_cp("parallel"),
    )(x2, gl, pe, pw)


def _ple_bwd(dx3, gl, pe, pw, *, tr=256, name):
    t, d = dx3.shape

    def body(dx_ref, gl_ref, pe_ref, pw_ref, dgl_ref, db_ref, dpe_ref, dpw_ref):
        dx, pv, pwv = dx_ref[...], pe_ref[...], pw_ref[...]
        gate = _sigmoid(gl_ref[...])
        r = lax.rsqrt(jnp.mean(pv * pv, axis=-1, keepdims=True) + NORM_EPS)
        ph = pv * r
        dgl = dx * (ph * pwv) * gate * (1.0 - gate)
        de = dx * gate
        g = de * pwv
        dgl_ref[...] = dgl.astype(BF16)
        dpe_ref[...] = (r * (g - ph * jnp.mean(g * ph, axis=-1, keepdims=True))).astype(BF16)

        @pl.when(pl.program_id(0) == 0)
        def _():
            db_ref[...] = jnp.zeros_like(db_ref)
            dpw_ref[...] = jnp.zeros_like(dpw_ref)

        db_ref[...] += jnp.sum(dgl, axis=0, keepdims=True)
        dpw_ref[...] += jnp.sum(de * ph, axis=0, keepdims=True)

    blk = pl.BlockSpec((tr, d), lambda i: (i, 0))
    row = pl.BlockSpec((1, d), lambda i: (0, 0))
    return pl.pallas_call(
        body, name=name, grid=(t // tr,), in_specs=[blk, blk, blk, row], out_specs=[blk, row, blk, row],
        out_shape=[jax.ShapeDtypeStruct((t, d), BF16), jax.ShapeDtypeStruct((1, d), F32),
                   jax.ShapeDtypeStruct((t, d), BF16), jax.ShapeDtypeStruct((1, d), F32)],
        compiler_params=_cp("arbitrary"),
    )(dx3, gl, pe, pw)


def _loss_head(x3, fw, target, *, tr=256, name):
    t, d = x3.shape

    def body(x_ref, w_ref, t_ref, l_ref, dx_ref, dw_ref):
        xv, wv = x_ref[...], w_ref[...]
        r = lax.rsqrt(jnp.mean(xv * xv, axis=-1, keepdims=True) + NORM_EPS)
        xh = xv * r
        err = xh * wv - t_ref[...]
        dy = err * (1.0 / d)
        g = dy * wv
        dx_ref[...] = r * (g - xh * jnp.mean(g * xh, axis=-1, keepdims=True))

        @pl.when(pl.program_id(0) == 0)
        def _():
            l_ref[...] = jnp.zeros_like(l_ref)
            dw_ref[...] = jnp.zeros_like(dw_ref)

        l_ref[...] += 0.5 * jnp.sum(jnp.mean(err * err, axis=-1, keepdims=True), axis=0, keepdims=True)
        dw_ref[...] += jnp.sum(dy * xh, axis=0, keepdims=True)

    blk = pl.BlockSpec((tr, d), lambda i: (i, 0))
    row = pl.BlockSpec((1, d), lambda i: (0, 0))
    return pl.pallas_call(
        body, name=name, grid=(t // tr,), in_specs=[blk, row, blk],
        out_specs=[pl.BlockSpec((1, 1), lambda i: (0, 0)), blk, row],
        out_shape=[jax.ShapeDtypeStruct((1, 1), F32), jax.ShapeDtypeStruct((t, d), F32),
                   jax.ShapeDtypeStruct((1, d), F32)],
        compiler_params=_cp("arbitrary"),
    )(x3, fw, target)


def _rope(blk, tab_ref):
    return blk * tab_ref[0] + pltpu.roll(blk, 96, axis=1) * tab_ref[1] + pltpu.roll(blk, 32, axis=1) * tab_ref[2]


def _unrope(g, tab_ref):
    return g * tab_ref[0] + pltpu.roll(g * tab_ref[1], 32, axis=1) + pltpu.roll(g * tab_ref[2], 96, axis=1)


def _mla_prep(q, kv, proj, tabs, *, tr=512, name):
    t = q.shape[0]

    def body(q_ref, kv_ref, kr_ref, tab_ref, qo_ref, ko_ref, vo_ref):
        qv, kvv = q_ref[...], kv_ref[...]
        qo_ref[0, :, :MLA_NOPE] = qv[:, :MLA_NOPE].astype(BF16)
        qo_ref[0, :, MLA_NOPE:] = _rope(qv[:, MLA_NOPE:], tab_ref).astype(BF16)
        ko_ref[0, :, :MLA_NOPE] = kvv[:, :MLA_NOPE].astype(BF16)
        ko_ref[0, :, MLA_NOPE:] = _rope(kr_ref[...], tab_ref).astype(BF16)
        vo_ref[0] = kvv[:, MLA_NOPE:].astype(BF16)

    return pl.pallas_call(
        body, name=name, grid=(t // tr, MLA_HEADS),
        in_specs=[pl.BlockSpec((tr, MLA_QK_PAD), lambda i, h: (i, h)),
                  pl.BlockSpec((tr, MLA_NOPE + MLA_V), lambda i, h: (i, h)),
                  pl.BlockSpec((tr, LANES), lambda i, h: (i, OFF_KR // LANES)),
                  pl.BlockSpec((3, tr, LANES), lambda i, h: (0, i, 0))],
        out_specs=[pl.BlockSpec((1, tr, MLA_QK_PAD), lambda i, h: (h, i, 0)),
                   pl.BlockSpec((1, tr, MLA_QK_PAD), lambda i, h: (h, i, 0)),
                   pl.BlockSpec((1, tr, MLA_V), lambda i, h: (h, i, 0))],
        out_shape=[jax.ShapeDtypeStruct((MLA_HEADS, t, MLA_QK_PAD), BF16),
                   jax.ShapeDtypeStruct((MLA_HEADS, t, MLA_QK_PAD), BF16),
                   jax.ShapeDtypeStruct((MLA_HEADS, t, MLA_V), BF16)],
        compiler_params=_cp("parallel", "parallel"),
    )(q, kv, proj, tabs)


def _mla_unprep(dq3, dk3, dv3, tabs, *, tr=256, name):
    t = dq3.shape[1]

    def body(dq_ref, dk_ref, dv_ref, tab_ref, qo_ref, kvo_ref, kro_ref):
        kr = jnp.zeros((tr, LANES), F32)
        for h in range(MLA_HEADS):
            c0 = h * MLA_QK_PAD
            qo_ref[:, c0:c0 + MLA_NOPE] = dq_ref[h, :, :MLA_NOPE].astype(BF16)
            qo_ref[:, c0 + MLA_NOPE:c0 + MLA_QK_PAD] = _unrope(dq_ref[h, :, MLA_NOPE:], tab_ref).astype(BF16)
            kvo_ref[:, c0:c0 + MLA_NOPE] = dk_ref[h, :, :MLA_NOPE].astype(BF16)
            kvo_ref[:, c0 + MLA_NOPE:c0 + MLA_QK_PAD] = dv_ref[h].astype(BF16)
            kr = kr + dk_ref[h, :, MLA_NOPE:]
        kro_ref[...] = _unrope(kr, tab_ref).astype(BF16)

    return pl.pallas_call(
        body, name=name, grid=(t // tr,),
        in_specs=[pl.BlockSpec((MLA_HEADS, tr, MLA_QK_PAD), lambda i: (0, i, 0)),
                  pl.BlockSpec((MLA_HEADS, tr, MLA_QK_PAD), lambda i: (0, i, 0)),
                  pl.BlockSpec((MLA_HEADS, tr, MLA_V), lambda i: (0, i, 0)),
                  pl.BlockSpec((3, tr, LANES), lambda i: (0, i, 0))],
        out_specs=[pl.BlockSpec((tr, MLA_HEADS * MLA_QK_PAD), lambda i: (i, 0)),
                   pl.BlockSpec((tr, MLA_HEADS * MLA_QK_PAD), lambda i: (i, 0)),
                   pl.BlockSpec((tr, LANES), lambda i: (i, 0))],
        out_shape=[jax.ShapeDtypeStruct((t, MLA_HEADS * MLA_QK_PAD), BF16),
                   jax.ShapeDtypeStruct((t, MLA_HEADS * MLA_QK_PAD), BF16),
                   jax.ShapeDtypeStruct((t, LANES), BF16)],
        compiler_params=_cp("parallel"),
    )(dq3, dk3, dv3, tabs)


ATT_BLK = 256
ATT_SCALE = 1.0 / math.sqrt(MLA_NOPE + MLA_ROPE)
_NT = (((1,), (1,)), ((), ()))
_TN = (((0,), (0,)), ((), ()))


def _att_scores(q, k, qi, kj):
    s = lax.dot_general(q, k, _NT, preferred_element_type=F32) * ATT_SCALE
    row = qi * ATT_BLK + lax.broadcasted_iota(jnp.int32, s.shape, 0)
    col = kj * ATT_BLK + lax.broadcasted_iota(jnp.int32, s.shape, 1)
    return jnp.where((col >> 6) <= (row >> 6), s, NEG)


def _attn_fwd(q3, k3, v3, *, name):
    t = q3.shape[1]
    nq = t // ATT_BLK

    def body(q_ref, k_ref, v_ref, o_ref, lse_ref):
        qi = pl.program_id(1)
        q = q_ref[0]

        def step(j, carry):
            m, l, acc = carry
            rows = pl.ds(pl.multiple_of(j * ATT_BLK, ATT_BLK), ATT_BLK)
            s = _att_scores(q, k_ref[0, rows, :], qi, j)
            m_new = jnp.maximum(m, jnp.max(s, axis=-1, keepdims=True))
            p = jnp.exp(s - m_new)
            alpha = jnp.exp(m - m_new)
            l = alpha * l + jnp.sum(p, axis=-1, keepdims=True)
            acc = alpha * acc + jnp.dot(p.astype(BF16), v_ref[0, rows, :], preferred_element_type=F32)
            return m_new, l, acc

        init = (jnp.full((ATT_BLK, 1), NEG, F32), jnp.zeros((ATT_BLK, 1), F32), jnp.zeros((ATT_BLK, MLA_V), F32))
        m, l, acc = lax.fori_loop(0, qi + 1, step, init)
        o_ref[...] = acc / l
        lse_ref[0] = m + jnp.log(l)

    return pl.pallas_call(
        body, name=name, grid=(MLA_HEADS, nq),
        in_specs=[pl.BlockSpec((1, ATT_BLK, MLA_QK_PAD), lambda h, i: (h, i, 0)),
                  pl.BlockSpec((1, t, MLA_QK_PAD), lambda h, i: (h, 0, 0)),
                  pl.BlockSpec((1, t, MLA_V), lambda h, i: (h, 0, 0))],
        out_specs=[pl.BlockSpec((ATT_BLK, MLA_V), lambda h, i: (i, h)),
                   pl.BlockSpec((1, ATT_BLK, 1), lambda h, i: (h, i, 0))],
        out_shape=[jax.ShapeDtypeStruct((t, MLA_HEADS * MLA_V), F32), jax.ShapeDtypeStruct((MLA_HEADS, t, 1), F32)],
        compiler_params=_cp("parallel", "parallel"),
    )(q3, k3, v3)


def _attn_bwd_dq(q3, k3, v3, o, dcat, lse, *, name):
    t = q3.shape[1]
    nq = t // ATT_BLK

    def body(q_ref, k_ref, v_ref, o_ref, do_ref, lse_ref, dq_ref, dl_ref):
        qi = pl.program_id(1)
        q, do, lse = q_ref[0], do_ref[...], lse_ref[0]
        delta = jnp.sum(o_ref[...] * do, axis=-1, keepdims=True)
        dob = do.astype(BF16)

        def step(j, dq):
            rows = pl.ds(pl.multiple_of(j * ATT_BLK, ATT_BLK), ATT_BLK)
            k = k_ref[0, rows, :]
            p = jnp.exp(_att_scores(q, k, qi, j) - lse)
            dp = lax.dot_general(dob, v_ref[0, rows, :], _NT, preferred_element_type=F32)
            ds = (p * (dp - delta) * ATT_SCALE).astype(BF16)
            return dq + jnp.dot(ds, k, preferred_element_type=F32)

        dq_ref[0] = lax.fori_loop(0, qi + 1, step, jnp.zeros((ATT_BLK, MLA_QK_PAD), F32))
        dl_ref[0] = delta

    return pl.pallas_call(
        body, name=name, grid=(MLA_HEADS, nq),
        in_specs=[pl.BlockSpec((1, ATT_BLK, MLA_QK_PAD), lambda h, i: (h, i, 0)),
                  pl.BlockSpec((1, t, MLA_QK_PAD), lambda h, i: (h, 0, 0)),
                  pl.BlockSpec((1, t, MLA_V), lambda h, i: (h, 0, 0)),
                  pl.BlockSpec((ATT_BLK, MLA_V), lambda h, i: (i, h)),
                  pl.BlockSpec((ATT_BLK, MLA_V), lambda h, i: (i, MLA_HEADS + h)),
                  pl.BlockSpec((1, ATT_BLK, 1), lambda h, i: (h, i, 0))],
        out_specs=[pl.BlockSpec((1, ATT_BLK, MLA_QK_PAD), lambda h, i: (h, i, 0)),
                   pl.BlockSpec((1, ATT_BLK, 1), lambda h, i: (h, i, 0))],
        out_shape=[jax.ShapeDtypeStruct((MLA_HEADS, t, MLA_QK_PAD), F32),
                   jax.ShapeDtypeStruct((MLA_HEADS, t, 1), F32)],
        compiler_params=_cp("parallel", "parallel"),
    )(q3, k3, v3, o, dcat, lse)


def _attn_bwd_dkv(q3, k3, v3, dcat, lse, delta, *, name):
    t = q3.shape[1]
    nq = t // ATT_BLK

    def body(q_ref, k_ref, v_ref, do_ref, lse_ref, dl_ref, dk_ref, dv_ref):
        kj = pl.program_id(1)
        k, v = k_ref[0], v_ref[0]

        def step(i, carry):
            dk, dv = carry
            rows = pl.ds(pl.multiple_of(i * ATT_BLK, ATT_BLK), ATT_BLK)
            q = q_ref[0, rows, :]
            dob = do_ref[rows, :].astype(BF16)
            p = jnp.exp(_att_scores(q, k, i, kj) - lse_ref[0, rows, :])
            dv = dv + lax.dot_general(p.astype(BF16), dob, _TN, preferred_element_type=F32)
            dp = lax.dot_general(dob, v, _NT, preferred_element_type=F32)
            ds = (p * (dp - dl_ref[0, rows, :]) * ATT_SCALE).astype(BF16)
            dk = dk + lax.dot_general(ds, q, _TN, preferred_element_type=F32)
            return dk, dv

        init = (jnp.zeros((ATT_BLK, MLA_QK_PAD), F32), jnp.zeros((ATT_BLK, MLA_V), F32))
        dk, dv = lax.fori_loop(kj, nq, step, init)
        dk_ref[0] = dk
        dv_ref[0] = dv

    return pl.pallas_call(
        body, name=name, grid=(MLA_HEADS, nq),
        in_specs=[pl.BlockSpec((1, t, MLA_QK_PAD), lambda h, j: (h, 0, 0)),
                  pl.BlockSpec((1, ATT_BLK, MLA_QK_PAD), lambda h, j: (h, j, 0)),
                  pl.BlockSpec((1, ATT_BLK, MLA_V), lambda h, j: (h, j, 0)),
                  pl.BlockSpec((t, MLA_V), lambda h, j: (0, MLA_HEADS + h)),
                  pl.BlockSpec((1, t, 1), lambda h, j: (h, 0, 0)),
                  pl.BlockSpec((1, t, 1), lambda h, j: (h, 0, 0))],
        out_specs=[pl.BlockSpec((1, ATT_BLK, MLA_QK_PAD), lambda h, j: (h, j, 0)),
                   pl.BlockSpec((1, ATT_BLK, MLA_V), lambda h, j: (h, j, 0))],
        out_shape=[jax.ShapeDtypeStruct((MLA_HEADS, t, MLA_QK_PAD), F32),
                   jax.ShapeDtypeStruct((MLA_HEADS, t, MLA_V), F32)],
        compiler_params=_cp("parallel", "parallel"),
    )(q3, k3, v3, dcat, lse, delta)


def _ssd_prep(proj, bias128, alog128, *, name):
    t = proj.shape[0]
    nc = t // CHUNK

    def body(raw_ref, b_ref, al_ref, dt_ref, cs_ref, a_ref):
        xv = raw_ref[...] + b_ref[...]
        dt = jnp.maximum(xv, 0.0) + jnp.log(1.0 + jnp.exp(-jnp.abs(xv)))
        a = -jnp.exp(al_ref[...])
        adt = (dt * a).reshape(nc, CHUNK, LANES)
        li = lax.broadcasted_iota(jnp.int32, (nc, CHUNK, CHUNK), 1)
        si = lax.broadcasted_iota(jnp.int32, (nc, CHUNK, CHUNK), 2)
        tril = jnp.where(si <= li, 1.0, 0.0).astype(F32)
        cs = lax.dot_general(tril, adt, (((2,), (1,)), ((0,), (0,))), precision=HI, preferred_element_type=F32)
        dt_ref[...] = dt
        cs_ref[...] = cs.reshape(t, LANES)
        a_ref[...] = a

    blk = pl.BlockSpec((t, LANES), lambda i: (0, 0))
    row = pl.BlockSpec((1, LANES), lambda i: (0, 0))
    return pl.pallas_call(
        body, name=name, grid=(1,),
        in_specs=[pl.BlockSpec((t, LANES), lambda i: (0, OFF_DT // LANES)), row, row],
        out_specs=[blk, blk, row],
        out_shape=[jax.ShapeDtypeStruct((t, LANES), F32), jax.ShapeDtypeStruct((t, LANES), F32),
                   jax.ShapeDtypeStruct((1, LANES), F32)],
        compiler_params=_cp("arbitrary"),
    )(proj, bias128, alog128)


def _ssd_prep_bwd(ddt128, dadt128, proj, bias128, dt128, a128, dd_h, *, name):
    t = proj.shape[0]

    def body(ddt_ref, dadt_ref, raw_ref, b_ref, dt_ref, a_ref, dd_ref, draw_ref, db_ref, dal_ref, dds_ref):
        draw = ddt_ref[...] * _sigmoid(raw_ref[...] + b_ref[...])
        draw_ref[...] = draw.astype(BF16)
        db_ref[...] = jnp.sum(draw, axis=0, keepdims=True)
        dal_ref[...] = jnp.sum(dadt_ref[...] * dt_ref[...], axis=0, keepdims=True) * a_ref[...]
        dds_ref[...] = jnp.sum(dd_ref[...], axis=-1, keepdims=True)

    blk = pl.BlockSpec((t, LANES), lambda i: (0, 0))
    row = pl.BlockSpec((1, LANES), lambda i: (0, 0))
    return pl.pallas_call(
        body, name=name, grid=(1,),
        in_specs=[blk, blk, pl.BlockSpec((t, LANES), lambda i: (0, OFF_DT // LANES)), row, blk, row,
                  pl.BlockSpec((SSD_HEADS, SSD_P), lambda i: (0, 0))],
        out_specs=[blk, row, row, pl.BlockSpec((SSD_HEADS, 1), lambda i: (0, 0))],
        out_shape=[jax.ShapeDtypeStruct((t, LANES), BF16), jax.ShapeDtypeStruct((1, LANES), F32),
                   jax.ShapeDtypeStruct((1, LANES), F32), jax.ShapeDtypeStruct((SSD_HEADS, 1), F32)],
        compiler_params=_cp("arbitrary"),
    )(ddt128, dadt128, proj, bias128, dt128, a128, dd_h)


def _bdot(a, b, ca, cb, precision=None):
    return lax.dot_general(a, b, (((ca,), (cb,)), ((0,), (0,))), precision=precision, preferred_element_type=F32)


def _ssd_common(xs_ref, dt_ref, cs_ref, csr_ref, b_ref, c_ref, nc):
    x = xs_ref[0].reshape(nc, CHUNK, SSD_P)
    dt = dt_ref[0].reshape(nc, CHUNK, SSD_P)
    cs = cs_ref[0].reshape(nc, CHUNK, SSD_P)
    csr = csr_ref[0]
    bm = b_ref[0].reshape(nc, CHUNK, SSD_N).astype(BF16)
    cm = c_ref[0].reshape(nc, CHUNK, SSD_N).astype(BF16)
    li = lax.broadcasted_iota(jnp.int32, (nc, CHUNK, CHUNK), 1)
    si = lax.broadcasted_iota(jnp.int32, (nc, CHUNK, CHUNK), 2)
    lmat = jnp.exp(jnp.where(si <= li, cs - csr, NEG))
    g = _bdot(cm, bm, 2, 2)
    cs_last = jnp.sum(jnp.where(li == CHUNK - 1, cs, 0.0), axis=1, keepdims=True)
    xdt = x * dt
    dec = jnp.exp(cs_last - cs)
    return x, dt, cs, bm, cm, li, si, lmat, g, cs_last, xdt, dec


def _ssd_fwd(xs_h, dt_h, cs_h, cs_row, b_g, c_g, dskip_h, *, name):
    t = xs_h.shape[1]
    nc = t // CHUNK
    hpg = SSD_HEADS // SSD_GROUPS

    def body(xs_ref, dt_ref, cs_ref, csr_ref, b_ref, c_ref, dk_ref, y_ref, st_ref, sc_ref, cd_ref):
        x, dt, cs, bm, cm, li, si, lmat, g, cs_last, xdt, dec = _ssd_common(xs_ref, dt_ref, cs_ref, csr_ref, b_ref,
                                                                           c_ref, nc)
        yd = _bdot((g * lmat).astype(BF16), xdt.astype(BF16), 2, 1)
        sc_ref[...] = _bdot(bm, (dec * xdt).astype(BF16), 1, 1)
        cd_ref[...] = jnp.exp(cs_last)

        def step(c, s):
            st_ref[0, c] = s
            return s * cd_ref[c] + sc_ref[c]

        lax.fori_loop(0, nc, step, jnp.zeros((SSD_N, SSD_P), F32))
        yo = _bdot(cm, st_ref[0].astype(BF16), 2, 1) * jnp.exp(cs)
        y_ref[0] = (yd + yo + dk_ref[0] * x).reshape(t, SSD_P)

    head = pl.BlockSpec((1, t, SSD_P), lambda h: (h, 0, 0))
    grp = pl.BlockSpec((1, t, SSD_N), lambda h: (h // hpg, 0, 0))
    return pl.pallas_call(
        body, name=name, grid=(SSD_HEADS,),
        in_specs=[head, head, head, pl.BlockSpec((1, nc, 1, CHUNK), lambda h: (h, 0, 0, 0)), grp, grp,
                  pl.BlockSpec((1, 1, SSD_P), lambda h: (h, 0, 0))],
        out_specs=[head, pl.BlockSpec((1, nc, SSD_N, SSD_P), lambda h: (h, 0, 0, 0))],
        out_shape=[jax.ShapeDtypeStruct((SSD_HEADS, t, SSD_P), F32),
                   jax.ShapeDtypeStruct((SSD_HEADS, nc, SSD_N, SSD_P), F32)],
        scratch_shapes=[pltpu.VMEM((nc, SSD_N, SSD_P), F32), pltpu.VMEM((nc, 1, SSD_P), F32)],
        compiler_params=_cp("parallel"),
    )(xs_h, dt_h, cs_h, cs_row, b_g, c_g, dskip_h)


def _ssd_bwd(xs_h, dt_h, cs_h, cs_row, b_g, c_g, dskip_h, a_h, states, dy_h, *, name):
    t = xs_h.shape[1]
    nc = t // CHUNK
    hpg = SSD_HEADS // SSD_GROUPS

    def body(xs_ref, dt_ref, cs_ref, csr_ref, b_ref, c_ref, dk_ref, a_ref, st_ref, dy_ref,
             dxs_ref, ddt_ref, dadt_ref, db_ref, dc_ref, dd_ref, dsl_ref, dsc_ref, cd_ref):
        x, dt, cs, bm, cm, li, si, lmat, g, cs_last, xdt, dec = _ssd_common(xs_ref, dt_ref, cs_ref, csr_ref, b_ref,
                                                                           c_ref, nc)
        dy = dy_ref[0].reshape(nc, CHUNK, SSD_P)
        dyb = dy.astype(BF16)
        xdtb = xdt.astype(BF16)
        sprev = st_ref[0]
        sprevb = sprev.astype(BF16)
        cdec = jnp.exp(cs_last)
        ecs = jnp.exp(cs)
        dw = (ecs * dy).astype(BF16)
        wmat = _bdot(cm, sprevb, 2, 1)
        dcs = jnp.sum(dy * ecs * wmat, axis=2, keepdims=True)
        dcm = _bdot(dw, sprevb, 2, 2)
        dsl_ref[...] = _bdot(cm, dw, 1, 1)
        cd_ref[...] = cdec

        def step(k, ds):
            c = nc - 1 - k
            dsc_ref[c] = ds
            return ds * cd_ref[c] + dsl_ref[c]

        lax.fori_loop(0, nc, step, jnp.zeros((SSD_N, SSD_P), F32))
        dsc = dsc_ref[...]
        dscb = dsc.astype(BF16)
        d_last = jnp.sum(jnp.sum(dsc * sprev, axis=1, keepdims=True) * cdec, axis=2, keepdims=True)
        z = dec * xdt
        dbm = _bdot(z.astype(BF16), dscb, 2, 2)
        dz = _bdot(bm, dscb, 2, 1)
        dxdt = dec * dz
        t2 = jnp.sum(dz * z, axis=2, keepdims=True)
        dcs = dcs - t2
        d_last = d_last + jnp.sum(t2, axis=1, keepdims=True)
        m = g * lmat
        mb = m.astype(BF16)
        dm = _bdot(dyb, xdtb, 2, 2)
        dxdt = dxdt + _bdot(mb, dyb, 1, 1)
        dseg = dm * m
        dcs = dcs + jnp.sum(dseg, axis=2, keepdims=True)
        ones = jnp.ones((nc, CHUNK, SSD_P), F32)
        dcs = dcs - _bdot(dseg, ones, 1, 1, precision=HI)
        dg = (dm * lmat).astype(BF16)
        dcm = dcm + _bdot(dg, bm, 2, 1)
        dbm = dbm + _bdot(dg, cm, 1, 1)
        dcs = dcs + jnp.where(li[:, :, :SSD_P] == CHUNK - 1, d_last, 0.0)
        triu = jnp.where(li <= si, 1.0, 0.0).astype(F32)
        dadt = _bdot(triu, dcs, 2, 1, precision=HI)
        dk = dk_ref[0]
        dxs_ref[0] = (dxdt * dt + dk * dy).reshape(t, SSD_P)
        ddt_ref[0] = (jnp.sum(dxdt * x, axis=2, keepdims=True) + dadt * a_ref[0]).reshape(t, SSD_P)
        dadt_ref[0] = dadt.reshape(t, SSD_P)
        dd_ref[0] = jnp.sum(jnp.sum(dy * x, axis=1, keepdims=True), axis=0)

        @pl.when(pl.program_id(1) == 0)
        def _():
            db_ref[...] = jnp.zeros_like(db_ref)
            dc_ref[...] = jnp.zeros_like(dc_ref)

        db_ref[0] += dbm.reshape(t, SSD_N)
        dc_ref[0] += dcm.reshape(t, SSD_N)

    head = pl.BlockSpec((1, t, SSD_P), lambda gi, hi: (gi * hpg + hi, 0, 0))
    grp = pl.BlockSpec((1, t, SSD_N), lambda gi, hi: (gi, 0, 0))
    lane = pl.BlockSpec((1, 1, SSD_P), lambda gi, hi: (gi * hpg + hi, 0, 0))
    return pl.pallas_call(
        body, name=name, grid=(SSD_GROUPS, hpg),
        in_specs=[head, head, head, pl.BlockSpec((1, nc, 1, CHUNK), lambda gi, hi: (gi * hpg + hi, 0, 0, 0)),
                  grp, grp, lane, lane, pl.BlockSpec((1, nc, SSD_N, SSD_P), lambda gi, hi: (gi * hpg + hi, 0, 0, 0)),
                  head],
        out_specs=[head, head, head, grp, grp, lane],
        out_shape=[jax.ShapeDtypeStruct((SSD_HEADS, t, SSD_P), F32)] * 3
        + [jax.ShapeDtypeStruct((SSD_GROUPS, t, SSD_N), F32)] * 2
        + [jax.ShapeDtypeStruct((SSD_HEADS, 1, SSD_P), F32)],
        scratch_shapes=[pltpu.VMEM((nc, SSD_N, SSD_P), F32), pltpu.VMEM((nc, SSD_N, SSD_P), F32),
                        pltpu.VMEM((nc, 1, SSD_P), F32)],
        compiler_params=_cp("parallel", "arbitrary"),
    )(xs_h, dt_h, cs_h, cs_row, b_g, c_g, dskip_h, a_h, states, dy_h)


def _ssd_gate_fwd(y, proj, w, *, tr=256, name):
    t = y.shape[0]
    gw = D_SSM // SSD_GROUPS

    def body(y_ref, z_ref, w_ref, o_ref):
        v = y_ref[...] * _silu(z_ref[...])
        for gi in range(SSD_GROUPS):
            vg = v[:, gi * gw:(gi + 1) * gw]
            r = lax.rsqrt(jnp.mean(vg * vg, axis=-1, keepdims=True) + NORM_EPS)
            o_ref[:, gi * gw:(gi + 1) * gw] = (vg * r * w_ref[:, gi * gw:(gi + 1) * gw]).astype(BF16)

    blk = pl.BlockSpec((tr, D_SSM), lambda i: (i, 0))
    return pl.pallas_call(
        body, name=name, grid=(t // tr,), in_specs=[blk, blk, pl.BlockSpec((1, D_SSM), lambda i: (0, 0))],
        out_specs=blk, out_shape=jax.ShapeDtypeStruct((t, D_SSM), BF16), compiler_params=_cp("parallel"),
    )(y, proj, w)


def _ssd_gate_bwd(y, proj, w, dcat, *, tr=256, name):
    t = y.shape[0]
    gw = D_SSM // SSD_GROUPS

    def body(y_ref, z_ref, w_ref, d_ref, dy_ref, dz_ref, dw_ref):
        yv, zv, dv = y_ref[...], z_ref[...], d_ref[...].astype(F32)
        sz = _silu(zv)
        v = yv * sz

        @pl.when(pl.program_id(0) == 0)
        def _():
            dw_ref[...] = jnp.zeros_like(dw_ref)

        for gi in range(SSD_GROUPS):
            sl = slice(gi * gw, (gi + 1) * gw)
            vg, dg = v[:, sl], dv[:, sl]
            r = lax.rsqrt(jnp.mean(vg * vg, axis=-1, keepdims=True) + NORM_EPS)
            vh = vg * r
            gg = dg * w_ref[:, sl]
            dvg = r * (gg - vh * jnp.mean(gg * vh, axis=-1, keepdims=True))
            dy_ref[:, sl] = dvg * sz[:, sl]
            dz_ref[:, sl] = (dvg * yv[:, sl] * _dsilu(zv[:, sl])).astype(BF16)
            dw_ref[:, sl] += jnp.sum(dg * vh, axis=0, keepdims=True)

    blk = pl.BlockSpec((tr, D_SSM), lambda i: (i, 0))
    row = pl.BlockSpec((1, D_SSM), lambda i: (0, 0))
    return pl.pallas_call(
        body, name=name, grid=(t // tr,), in_specs=[blk, blk, row, blk], out_specs=[blk, blk, row],
        out_shape=[jax.ShapeDtypeStruct((t, D_SSM), F32), jax.ShapeDtypeStruct((t, D_SSM), BF16),
                   jax.ShapeDtypeStruct((1, D_SSM), F32)],
        compiler_params=_cp("arbitrary"),
    )(y, proj, w, dcat)


def _pad_lanes(v):
    return jnp.pad(v, ((0, 0), (0, LANES - v.shape[1])))


def _to_heads(v):
    return v.reshape(v.shape[0], SSD_HEADS, SSD_P).transpose(1, 0, 2)


def _from_heads(v):
    return v.transpose(1, 0, 2).reshape(v.shape[1], SSD_HEADS * SSD_P)


def _per_head(v128, t):
    return jnp.broadcast_to(v128[:, :SSD_HEADS].T[:, :, None], (SSD_HEADS, t, SSD_P))


def _ssd_forward(proj, conv_w, conv_b, dt_bias, a_log, d_skip, ssd_norm_w):
    t = proj.shape[0]
    nc = t // CHUNK
    xbc = _conv_act_fwd(proj, conv_w, conv_b, kw=SSD_CONV, glu=False, tc=512, coff=OFF_XBC // 512,
                        ncols=SSD_CONV_DIM, out_dtype=F32, name="ssd_conv_fwd")
    bias128, alog128 = _pad_lanes(dt_bias), _pad_lanes(a_log)
    dt128, cs128, a128 = _ssd_prep(proj, bias128, alog128, name="ssd_prep")
    dt_h, cs_h = _per_head(dt128, t), _per_head(cs128, t)
    cs_row = cs128[:, :SSD_HEADS].T.reshape(SSD_HEADS, nc, 1, CHUNK)
    xs_h = _to_heads(xbc[:, :D_SSM])
    gn = SSD_GROUPS * SSD_N
    b_g = xbc[:, D_SSM:D_SSM + gn].reshape(t, SSD_GROUPS, SSD_N).transpose(1, 0, 2)
    c_g = xbc[:, D_SSM + gn:].reshape(t, SSD_GROUPS, SSD_N).transpose(1, 0, 2)
    dskip_h = jnp.broadcast_to(d_skip[0][:, None, None], (SSD_HEADS, 1, SSD_P))
    a_h = jnp.broadcast_to(a128[0, :SSD_HEADS][:, None, None], (SSD_HEADS, 1, SSD_P))
    y_h, states = _ssd_fwd(xs_h, dt_h, cs_h, cs_row, b_g, c_g, dskip_h, name="ssd_scan_fwd")
    y = _from_heads(y_h)
    y_ssd = _ssd_gate_fwd(y, proj, ssd_norm_w, name="ssd_gate_fwd")
    saved = (proj, conv_w, conv_b, ssd_norm_w, bias128, dt128, a128, dt_h, cs_h, cs_row, xs_h, b_g, c_g, dskip_h, a_h,
             states, y)
    return y_ssd, saved


def _ssd_backward(saved, dcat):
    (proj, conv_w, conv_b, ssd_norm_w, bias128, dt128, a128, dt_h, cs_h, cs_row, xs_h, b_g, c_g, dskip_h, a_h, states,
     y) = saved
    t = proj.shape[0]
    dy, dz, d_norm_w = _ssd_gate_bwd(y, proj, ssd_norm_w, dcat, name="ssd_gate_bwd")
    dxs_h, ddt_h, dadt_h, db_g, dc_g, dd_h = _ssd_bwd(xs_h, dt_h, cs_h, cs_row, b_g, c_g, dskip_h, a_h, states,
                                                      _to_heads(dy), name="ssd_scan_bwd")
    gn = SSD_GROUPS * SSD_N
    dxc = jnp.concatenate([_from_heads(dxs_h), db_g.transpose(1, 0, 2).reshape(t, gn),
                           dc_g.transpose(1, 0, 2).reshape(t, gn)], axis=1)
    dxbc, d_conv_w, d_conv_b = _conv_act_bwd(proj, conv_w, conv_b, dxc, kw=SSD_CONV, glu=False, tc=512,
                                             coff=OFF_XBC // 512, ncols=SSD_CONV_DIM, name="ssd_conv_bwd")
    ddt128 = _pad_lanes(ddt_h[:, :, 0].T)
    dadt128 = _pad_lanes(dadt_h[:, :, 0].T)
    d_raw, d_bias, d_alog, d_dskip = _ssd_prep_bwd(ddt128, dadt128, proj, bias128, dt128, a128,
                                                   dd_h.reshape(SSD_HEADS, SSD_P), name="ssd_prep_bwd")
    return (dz, dxbc, d_raw, d_norm_w, d_conv_w, d_conv_b, d_bias[:, :SSD_HEADS], d_alog[:, :SSD_HEADS],
            d_dskip.reshape(1, SSD_HEADS))


def _rope_tables(positions):
    inv_freq = ROPE_THETA ** (-jnp.arange(0, MLA_ROPE, 2, dtype=F32) / MLA_ROPE)
    ang = positions[0].astype(F32)[:, None] * inv_freq
    cos, sin = jnp.cos(ang), jnp.sin(ang)
    z = jnp.zeros_like(cos)
    return jnp.stack([jnp.concatenate([cos, cos, z, z], axis=1), jnp.concatenate([-sin, z, z, z], axis=1),
                      jnp.concatenate([z, sin, z, z], axis=1)])


def _mla_forward(proj, tabs, q_a_norm_w, wq_pad, kv_a_norm_w, wkv):
    qn = _rmsnorm_fwd(proj, q_a_norm_w, width=MLA_Q_RANK, cblk=OFF_QA // MLA_Q_RANK, name="q_a_norm")
    q = _matmul(qn, wq_pad, name="q_b_proj")
    kvn = _rmsnorm_fwd(proj, kv_a_norm_w, width=MLA_KV_RANK, cblk=OFF_CKV // MLA_KV_RANK, name="kv_a_norm")
    kv = _matmul(kvn, wkv, name="kv_b_proj")
    q3, k3, v3 = _mla_prep(q, kv, proj, tabs, name="mla_prep")
    o, lse = _attn_fwd(q3, k3, v3, name="attn_fwd")
    return o, (proj, tabs, q_a_norm_w, wq_pad, kv_a_norm_w, wkv, qn, kvn, q3, k3, v3, o, lse)


def _mla_backward(saved, dcat):
    proj, tabs, q_a_norm_w, wq_pad, kv_a_norm_w, wkv, qn, kvn, q3, k3, v3, o, lse = saved
    dq3, delta = _attn_bwd_dq(q3, k3, v3, o, dcat, lse, name="attn_bwd_dq")
    dk3, dv3 = _attn_bwd_dkv(q3, k3, v3, dcat, lse, delta, name="attn_bwd_dkv")
    dq, dkv, dkr = _mla_unprep(dq3, dk3, dv3, tabs, name="mla_unprep")
    d_wq = _matmul(qn, dq, ta=True, out_dtype=BF16, name="d_w_q_b")
    dqn = _matmul(dq, wq_pad, tb=True, name="d_qn")
    dq_a, d_qnw = _rmsnorm_bwd(proj, q_a_norm_w, dqn, width=MLA_Q_RANK, cblk=OFF_QA // MLA_Q_RANK, out_dtype=BF16,
                               name="q_a_norm_bwd")
    d_wkv = _matmul(kvn, dkv, ta=True, out_dtype=BF16, name="d_w_kv_b")
    dkvn = _matmul(dkv, wkv, tb=True, name="d_kvn")
    dckv, d_kvnw = _rmsnorm_bwd(proj, kv_a_norm_w, dkvn, width=MLA_KV_RANK, cblk=OFF_CKV // MLA_KV_RANK,
                                out_dtype=BF16, name="kv_a_norm_bwd")
    return dq_a, dckv, dkr, d_wq, d_wkv, d_qnw, d_kvnw


def _pad_w_q(w):
    r = w.shape[0]
    w3 = w.reshape(r, MLA_HEADS, MLA_NOPE + MLA_ROPE)
    return jnp.pad(w3, ((0, 0), (0, 0), (0, MLA_QK_PAD - MLA_NOPE - MLA_ROPE))).reshape(r, MLA_HEADS * MLA_QK_PAD)


def _unpad_w_q(w):
    r = w.shape[0]
    return w.reshape(r, MLA_HEADS, MLA_QK_PAD)[:, :, :MLA_NOPE + MLA_ROPE].reshape(r, MLA_HEADS * (MLA_NOPE + MLA_ROPE))


def _pad_w_in(w):
    r = w.shape[0]
    o_dt = D_SSM + SSD_CONV_DIM
    o_qa = o_dt + SSD_HEADS
    o_kr = o_qa + MLA_Q_RANK + MLA_KV_RANK
    zeros = lambda n: jnp.zeros((r, n), w.dtype)
    return jnp.concatenate([w[:, :o_dt], w[:, o_qa:o_kr], w[:, o_kr:], zeros(LANES - MLA_ROPE),
                            w[:, o_dt:o_qa], zeros(LANES - SSD_HEADS)], axis=1)


def _unpad_w_in(w):
    return jnp.concatenate([w[:, :OFF_QA], w[:, OFF_DT:OFF_DT + SSD_HEADS], w[:, OFF_QA:OFF_KR + MLA_ROPE]], axis=1)


WEIGHTS = ['mix_norm_w', 'w_in', 'conv_w', 'conv_b', 'dt_bias', 'a_log', 'd_skip', 'ssd_norm_w', 'q_a_norm_w', 'w_q_b',
           'kv_a_norm_w', 'w_kv_b', 'w_out', 'ffn_norm_w', 'w_ffn_up', 'ffn_conv_w', 'ffn_conv_b', 'w_ffn_down',
           'ple_norm_w', 'w_ple_gate', 'b_ple_gate', 'w_ple_proj', 'ple_post_norm_w', 'final_norm_w']
BIG = ['w_in', 'w_q_b', 'w_kv_b', 'w_out', 'w_ffn_up', 'w_ffn_down', 'w_ple_gate', 'w_ple_proj']
COL_SHARDED = ('w_in', 'w_q_b', 'w_kv_b', 'w_ffn_up', 'w_ple_proj')
CONV = ['conv_w', 'ffn_conv_w']
REPL = [n for n in WEIGHTS if n not in BIG and n not in CONV]
FFN_INV = tuple(int(i) for i in np.argsort(FFN_PERM))


def _cat_cols(g):
    return g.transpose(1, 0, 2).reshape(g.shape[1], N_DEV * g.shape[2])


def _split_cols(w):
    return w.reshape(w.shape[0], N_DEV, w.shape[1] // N_DEV).transpose(1, 0, 2)


def _interleave(v):
    r = v.shape[0]
    return v.reshape(r, N_DEV, FFN_TC)[:, jnp.array(FFN_PERM)].reshape(r, N_DEV * FFN_TC)


def _deinterleave(v):
    r = v.shape[0]
    return v.reshape(r, N_DEV, FFN_TC)[:, jnp.array(FFN_INV)].reshape(r, N_DEV * FFN_TC)


def _assemble_weights(g):
    layout = {
        'w_in': lambda v: _pad_w_in(_cat_cols(v)),
        'w_q_b': lambda v: _pad_w_q(_cat_cols(v)),
        'w_kv_b': _cat_cols,
        'w_out': lambda v: v.reshape(D_MODEL, D_MODEL),
        'w_ffn_up': lambda v: v,
        'w_ffn_down': lambda v: v.reshape(D_FF, D_MODEL),
        'w_ple_gate': lambda v: v.reshape(D_MODEL, D_MODEL),
        'w_ple_proj': _cat_cols,
        'conv_w': _cat_cols,
        'ffn_conv_w': lambda v: _interleave(_cat_cols(v)),
    }
    return {n: layout[n](v) for n, v in g.items()}


WEIGHT_GROUPS = {'a': ['w_in', 'w_q_b', 'w_kv_b', 'conv_w'], 'b': ['w_out'], 'c': ['w_ffn_up', 'ffn_conv_w'],
                 'd': ['w_ffn_down', 'w_ple_gate', 'w_ple_proj']}
GRAD_GROUPS = {'p': ['w_ple_proj', 'w_ple_gate'], 'q': ['w_ffn_down'], 'r': ['w_ffn_up'], 's': ['w_out'],
               't': ['w_q_b', 'w_kv_b', 'w_in']}


def _ffn_perm(j):
    return (j % 2) * (N_DEV // 2) + j // 2


def _local_step(x, p, tabs, get_w, s, target, emit):
    t = x.shape[0]
    s = dict(s)
    half = D_MODEL // 2
    up_cols = 2 * D_FF
    ffn_conv_b = _interleave(s['ffn_conv_b'])
    w = dict(get_w('a', None))
    h = _rmsnorm_fwd(x, s['mix_norm_w'], width=D_MODEL, name="mix_norm")
    proj = _matmul(h, w['w_in'], name="in_proj")
    y_ssd, ssd_saved = _ssd_forward(proj, w['conv_w'], s['conv_b'], s['dt_bias'], s['a_log'], s['d_skip'],
                                    s['ssd_norm_w'])
    o, mla_saved = _mla_forward(proj, tabs, s['q_a_norm_w'], w['w_q_b'], s['kv_a_norm_w'], w['w_kv_b'])
    tk_o, tn_o = _tile(half, 512), _tile(D_MODEL, 1024)
    w.update(get_w('b', o))
    x1 = _matmul(y_ssd, w['w_out'], add=x, mnk=(t, D_MODEL, half), name="out_proj_ssd")
    x1 = _matmul(o, w['w_out'], add=x1, mnk=(t, D_MODEL, half), name="out_proj_mla",
                 b_spec=pl.BlockSpec((tk_o, tn_o), lambda i, j, kk: (kk + half // tk_o, j)))
    hf = _rmsnorm_fwd(x1, s['ffn_norm_w'], width=D_MODEL, name="ffn_norm")
    w.update(get_w('c', hf))
    tk_u = _tile(D_MODEL, 512)
    u = _matmul(hf, w['w_ffn_up'], mnk=(t, up_cols, D_MODEL), tn=FFN_TC, name="ffn_up",
                b_spec=pl.BlockSpec((1, tk_u, FFN_TC), lambda i, j, kk: (_ffn_perm(j), kk, 0)))
    act = _conv_act_fwd(u, w['ffn_conv_w'], ffn_conv_b, kw=FFN_CONV, glu=True, tc=2 * FFN_TC, coff=0, ncols=up_cols,
                        out_dtype=BF16, name="ffn_act")
    w.update(get_w('d', act))
    x2 = _matmul(act, w['w_ffn_down'], add=x1, name="ffn_down")
    hp = _rmsnorm_fwd(x2, s['ple_norm_w'], width=D_MODEL, name="ple_norm")
    gl = _matmul(hp, w['w_ple_gate'], bias=s['b_ple_gate'], name="ple_gate")
    pe = _matmul(p, w['w_ple_proj'], name="ple_proj")
    x3 = _ple_fwd(x2, gl, pe, s['ple_post_norm_w'], name="ple_mix")
    loss, dx3, d_final = _loss_head(x3, s['final_norm_w'], target, name="loss_head")
    dgl, d_bgate, dpe, d_post = _ple_bwd(dx3, gl, pe, s['ple_post_norm_w'], name="ple_mix_bwd")
    d_wproj = _matmul(p, dpe, ta=True, out_dtype=BF16, name="d_w_ple_proj")
    d_wgate = _matmul(hp, dgl, ta=True, out_dtype=BF16, name="d_w_ple_gate")
    tok = emit('p', {'w_ple_proj': _split_cols(d_wproj),
                     'w_ple_gate': d_wgate.reshape(N_DEV, D_MODEL // N_DEV, D_MODEL)})
    dhp = _matmul(dgl, w['w_ple_gate'], tb=True, name="d_ple_normed")
    dx2, d_plenorm = _rmsnorm_bwd(x2, s['ple_norm_w'] + tok, dhp, dx3, width=D_MODEL, name="ple_norm_bwd")
    dact = _matmul(dx2, w['w_ffn_down'], tb=True, name="d_ffn_act")
    d_wdown = _matmul(act, dx2, ta=True, out_dtype=BF16, name="d_w_ffn_down")
    tok = emit('q', {'w_ffn_down': d_wdown.reshape(N_DEV, D_FF // N_DEV, D_MODEL)})
    du, d_fconv_w, d_fconv_b = _conv_act_bwd(u, w['ffn_conv_w'], ffn_conv_b + tok, dact, kw=FFN_CONV, glu=True,
                                             tc=2 * FFN_TC, coff=0, ncols=up_cols, name="ffn_act_bwd")
    tm_u = _tile(D_MODEL, 1024)
    d_wup = _matmul(hf, du, ta=True, out_dtype=BF16, mnk=(D_MODEL, up_cols, t), tn=FFN_TC, name="d_w_ffn_up",
                    o_spec=pl.BlockSpec((1, tm_u, FFN_TC), lambda i, j, kk: (_ffn_perm(j), i, 0)),
                    o_shape=(N_DEV, D_MODEL, FFN_TC))
    tok = emit('r', {'w_ffn_up': d_wup})
    dhf = _matmul(du, w['w_ffn_up'], tb=True, mnk=(t, D_MODEL, up_cols), tk=FFN_TC, name="d_ffn_normed",
                  b_spec=pl.BlockSpec((1, tn_o, FFN_TC), lambda i, j, kk: (_ffn_perm(kk), j, 0)))
    dx1, d_ffnnorm = _rmsnorm_bwd(x1, s['ffn_norm_w'] + tok, dhf, dx2, width=D_MODEL, name="ffn_norm_bwd")
    dcat = _matmul(dx1, w['w_out'], tb=True, name="d_mixed")
    d_wout = jnp.concatenate([_matmul(y_ssd, dx1, ta=True, out_dtype=BF16, name="d_w_out_ssd"),
                              _matmul(o, dx1, ta=True, out_dtype=BF16, name="d_w_out_mla")], axis=0)
    tok = emit('s', {'w_out': d_wout.reshape(N_DEV, D_MODEL // N_DEV, D_MODEL)})
    ssd_saved = ssd_saved[:3] + (ssd_saved[3] + tok,) + ssd_saved[4:]
    dz, dxbc, d_raw, d_ssdnorm, d_conv_w, d_conv_b, d_dtb, d_alog, d_dskip = _ssd_backward(ssd_saved, dcat)
    dq_a, dckv, dkr, d_wq, d_wkv, d_qnorm, d_kvnorm = _mla_backward(mla_saved, dcat)
    dproj = jnp.concatenate([dz, dxbc, dq_a, dckv, dkr, d_raw], axis=1)
    d_win = _matmul(h, dproj, ta=True, out_dtype=BF16, name="d_w_in")
    dh = _matmul(dproj, w['w_in'], tb=True, name="d_in_normed")
    dx, d_mixnorm = _rmsnorm_bwd(x, s['mix_norm_w'], dh, dx1, width=D_MODEL, name="mix_norm_bwd")
    emit('t', {'w_in': _split_cols(_unpad_w_in(d_win)), 'w_q_b': _split_cols(_unpad_w_q(d_wq)),
               'w_kv_b': _split_cols(d_wkv)})
    conv = {'conv_w': d_conv_w, 'ffn_conv_w': _deinterleave(d_fconv_w)}
    vec = {
        'mix_norm_w': d_mixnorm, 'conv_b': d_conv_b, 'dt_bias': d_dtb, 'a_log': d_alog, 'd_skip': d_dskip,
        'ssd_norm_w': d_ssdnorm, 'q_a_norm_w': d_qnorm, 'kv_a_norm_w': d_kvnorm, 'ffn_norm_w': d_ffnnorm,
        'ffn_conv_b': _deinterleave(d_fconv_b), 'ple_norm_w': d_plenorm, 'b_ple_gate': d_bgate,
        'ple_post_norm_w': d_post, 'final_norm_w': d_final,
    }
    return loss, dx, conv, vec


MESH = pl.DeviceIdType.MESH
FLIPS = ((0, 0, 1), (1, 0, 0), (0, 1, 0), (1, 1, 0), (1, 0, 1), (0, 1, 1), (1, 1, 1))


def _exchange(items, *, gather, name):
    n = len(items)

    def body(*refs):
        ins, outs = refs[:n], refs[n:2 * n]
        send_sems, recv_sems, local_sems = refs[2 * n:]
        x, y, c = lax.axis_index("x"), lax.axis_index("y"), lax.axis_index("c")
        me = 4 * x + 2 * y + c
        peers = [(jnp.where(fx, 1 - x, x), jnp.where(fy, 1 - y, y), jnp.where(fc, 1 - c, c)) for fx, fy, fc in FLIPS]
        slot = [4 * px + 2 * py + pc for px, py, pc in peers]
        local, sends = [], []
        for wi in range(n):
            cp = pltpu.make_async_copy(ins[wi] if gather else ins[wi].at[me], outs[wi].at[me], local_sems.at[wi])
            cp.start()
            local.append(cp)
            for k, peer in enumerate(peers):
                cp = pltpu.make_async_remote_copy(
                    src_ref=ins[wi] if gather else ins[wi].at[slot[k]], dst_ref=outs[wi].at[me],
                    send_sem=send_sems.at[k, wi], recv_sem=recv_sems.at[k, wi], device_id=peer, device_id_type=MESH)
                cp.start()
                sends.append(cp)
        for wi in range(n):
            for k, peer in enumerate(peers):
                pltpu.make_async_remote_copy(
                    src_ref=outs[wi].at[slot[k]], dst_ref=outs[wi].at[slot[k]], send_sem=send_sems.at[k, wi],
                    recv_sem=recv_sems.at[k, wi], device_id=peer, device_id_type=MESH).wait_recv()
        for cp in sends:
            cp.wait_send()
        for cp in local:
            cp.wait()

    hbm = pl.BlockSpec(memory_space=pltpu.HBM)
    out_shape = [jax.ShapeDtypeStruct(((N_DEV,) + v.shape) if gather else v.shape, v.dtype) for v in items]
    return pl.pallas_call(
        body, name=name, in_specs=[hbm] * n, out_specs=[hbm] * n, out_shape=out_shape,
        scratch_shapes=[pltpu.SemaphoreType.DMA((len(FLIPS), n)), pltpu.SemaphoreType.DMA((len(FLIPS), n)),
                        pltpu.SemaphoreType.DMA((n,))],
    )(*items)


HBM_SPEC = pl.BlockSpec(memory_space=pltpu.HBM)
SEM_SPEC = pl.BlockSpec(memory_space=pltpu.SEMAPHORE)
EFFECT = pltpu.SideEffectType.DATAFLOW_SIDE_EFFECTING


def _peers():
    x, y, c = lax.axis_index("x"), lax.axis_index("y"), lax.axis_index("c")
    peers = [(jnp.where(fx, 1 - x, x), jnp.where(fy, 1 - y, y), jnp.where(fc, 1 - c, c)) for fx, fy, fc in FLIPS]
    return 4 * x + 2 * y + c, peers, [4 * px + 2 * py + pc for px, py, pc in peers]


def _exchange_start(items, *, gather, name):
    n = len(items)
    lands = [lax.empty(((N_DEV,) + v.shape) if gather else v.shape, v.dtype) for v in items]

    def body(*refs):
        ins, land = refs[:n], refs[n:2 * n]
        send_sems, recv_sems, token = refs[2 * n], refs[2 * n + 1], refs[4 * n + 2]
        me, peers, slot = _peers()
        for wi in range(n):
            for k, peer in enumerate(peers):
                pltpu.make_async_remote_copy(
                    src_ref=ins[wi] if gather else ins[wi].at[slot[k]], dst_ref=land[wi].at[me],
                    send_sem=send_sems.at[k * n + wi], recv_sem=recv_sems.at[k * n + wi], device_id=peer,
                    device_id_type=MESH).start()
        token[...] = jnp.zeros_like(token)

    res = pl.pallas_call(
        body, name=name, in_specs=[HBM_SPEC] * (2 * n),
        out_specs=[SEM_SPEC, SEM_SPEC] + [HBM_SPEC] * (2 * n) + [pl.BlockSpec(memory_space=pltpu.VMEM)],
        out_shape=[pltpu.SemaphoreType.DMA((len(FLIPS) * n,)), pltpu.SemaphoreType.DMA((len(FLIPS) * n,))]
        + [pltpu.HBM(v.shape, v.dtype) for v in items] + [pltpu.HBM(v.shape, v.dtype) for v in lands]
        + [jax.ShapeDtypeStruct((HALO, LANES), F32)],
        input_output_aliases={i: 2 + i for i in range(2 * n)},
        compiler_params=pltpu.CompilerParams(has_side_effects=EFFECT),
    )(*[pltpu.with_memory_space_constraint(v, pltpu.HBM) for v in items + lands])
    return (res[0], res[1], list(res[2:2 + n]), list(res[2 + n:2 + 2 * n])), res[2 + 2 * n]


def _exchange_wait(started, after, *, gather, name):
    send_sems, recv_sems, items, lands = started
    n = len(items)

    def body(*refs):
        ins, land = refs[:n], refs[n:2 * n]
        send_sems, recv_sems = refs[2 * n], refs[2 * n + 1]
        local_sems = refs[4 * n + 3]
        me, peers, slot = _peers()
        local = []
        for wi in range(n):
            cp = pltpu.make_async_copy(ins[wi] if gather else ins[wi].at[me], land[wi].at[me], local_sems.at[wi])
            cp.start()
            local.append(cp)
        for wi in range(n):
            for k, peer in enumerate(peers):
                cp = pltpu.make_async_remote_copy(
                    src_ref=ins[wi] if gather else ins[wi].at[slot[k]], dst_ref=land[wi].at[slot[k]],
                    send_sem=send_sems.at[k * n + wi], recv_sem=recv_sems.at[k * n + wi], device_id=peer,
                    device_id_type=MESH)
                cp.wait_send()
                cp.wait_recv()
        for cp in local:
            cp.wait()

    res = pl.pallas_call(
        body, name=name, in_specs=[HBM_SPEC] * (2 * n) + [SEM_SPEC, SEM_SPEC, pl.BlockSpec(memory_space=pl.ANY)],
        out_specs=[HBM_SPEC] * (2 * n),
        out_shape=[pltpu.HBM(v.shape, v.dtype) for v in items] + [pltpu.HBM(v.shape, v.dtype) for v in lands],
        input_output_aliases={i: i for i in range(2 * n)},
        scratch_shapes=[pltpu.SemaphoreType.DMA((n,))],
        compiler_params=pltpu.CompilerParams(has_side_effects=EFFECT),
    )(*items, *lands, send_sems, recv_sems, after)
    return list(res[n:])


def _adamw(parts, w, m, v, *, name):
    r, c = w.shape
    tr = max(d for d in range(HALO, 129, HALO) if r % d == 0) if r > 128 else r

    def body(p_ref, w_ref, m_ref, v_ref, g_ref, d_ref, mo_ref, vo_ref):
        g = p_ref[0].astype(F32)
        for k in range(1, N_DEV):
            g = g + p_ref[k].astype(F32)
        mn = ADAM_B1 * m_ref[...] + (1.0 - ADAM_B1) * g
        vn = ADAM_B2 * v_ref[...] + (1.0 - ADAM_B2) * (g * g)
        m_hat = mn / (1.0 - ADAM_B1 ** ADAM_STEP)
        v_hat = vn / (1.0 - ADAM_B2 ** ADAM_STEP)
        g_ref[...] = g
        d_ref[...] = -ADAM_LR * (m_hat / (jnp.sqrt(v_hat) + ADAM_EPS) + ADAM_WD * w_ref[...])
        mo_ref[...] = mn
        vo_ref[...] = vn

    blk = pl.BlockSpec((tr, c), lambda i: (i, 0))
    return pl.pallas_call(
        body, name=name, grid=(r // tr,), in_specs=[pl.BlockSpec((N_DEV, tr, c), lambda i: (0, i, 0)), blk, blk, blk],
        out_specs=[blk] * 4, out_shape=[jax.ShapeDtypeStruct((r, c), F32)] * 4, compiler_params=_cp("parallel"),
    )(parts, w, m, v)


def _pack_rows(vs, rows):
    lead = vs[0].shape[:-1] if vs[0].ndim > 1 else ()
    flat = jnp.concatenate(vs, axis=-1)
    pad = rows * LANES - flat.shape[-1]
    flat = jnp.pad(flat, [(0, 0)] * len(lead) + [(0, pad)])
    return flat.reshape(lead + (rows, LANES))


def kernel(x, p, positions, mix_norm_w, w_in, conv_w, conv_b, dt_bias, a_log, d_skip, ssd_norm_w, q_a_norm_w, w_q_b, kv_a_norm_w, w_kv_b, w_out, ffn_norm_w, w_ffn_up, ffn_conv_w, ffn_conv_b, w_ffn_down, ple_norm_w, w_ple_gate, b_ple_gate, w_ple_proj, ple_post_norm_w, final_norm_w, loss_target, m_mix_norm_w, m_w_in, m_conv_w, m_conv_b, m_dt_bias, m_a_log, m_d_skip, m_ssd_norm_w, m_q_a_norm_w, m_w_q_b, m_kv_a_norm_w, m_w_kv_b, m_w_out, m_ffn_norm_w, m_w_ffn_up, m_ffn_conv_w, m_ffn_conv_b, m_w_ffn_down, m_ple_norm_w, m_w_ple_gate, m_b_ple_gate, m_w_ple_proj, m_ple_post_norm_w, m_final_norm_w, v_mix_norm_w, v_w_in, v_conv_w, v_conv_b, v_dt_bias, v_a_log, v_d_skip, v_ssd_norm_w, v_q_a_norm_w, v_w_q_b, v_kv_a_norm_w, v_w_kv_b, v_w_out, v_ffn_norm_w, v_w_ffn_up, v_ffn_conv_w, v_ffn_conv_b, v_w_ffn_down, v_ple_norm_w, v_w_ple_gate, v_b_ple_gate, v_w_ple_proj, v_ple_post_norm_w, v_final_norm_w):
    given = dict(locals())
    shapes = {n: given[n].shape for n in WEIGHTS}
    w2 = {n: given[n].reshape(given[n].shape[-2:] if n in BIG or n in CONV else (1, -1)) for n in WEIGHTS}
    m2 = {n: given['m_' + n].reshape(w2[n].shape) for n in WEIGHTS}
    v2 = {n: given['v_' + n].reshape(w2[n].shape) for n in WEIGHTS}
    me = 4 * lax.axis_index("x") + 2 * lax.axis_index("y") + lax.axis_index("c")

    gathers, zero = {}, 0.0
    for grp, names in WEIGHT_GROUPS.items():
        gathers[grp], token = _exchange_start([w2[n].astype(BF16) if n in BIG else w2[n] for n in names], gather=True,
                                              name="gather_" + grp + "_start")
        zero = zero + token[0, 0]

    def get_w(grp, after):
        lands = _exchange_wait(gathers[grp], token if after is None else after, gather=True,
                               name="gather_" + grp + "_wait")
        return _assemble_weights(dict(zip(WEIGHT_GROUPS[grp], lands)))

    scatters = {}

    def emit(grp, grads):
        scatters[grp], token = _exchange_start([grads[n] for n in GRAD_GROUPS[grp]], gather=False,
                                               name="scatter_" + grp + "_start")
        return token[0, 0]

    vecs = {n: w2[n] for n in REPL}
    vecs['mix_norm_w'] = vecs['mix_norm_w'] + zero
    loss, dx, g_conv, g_vec = _local_step(x[0], p[0, 0], _rope_tables(positions), get_w, vecs, loss_target[0], emit)
    n_small = sum(g_vec[n].shape[1] for n in REPL) + sum(g_conv[n].size for n in CONV) + 1
    rows_small = -(-n_small // (LANES * HALO)) * HALO
    small = _pack_rows([g_vec[n] for n in REPL] + [g_conv[n].reshape(1, -1) for n in CONV] + [loss], rows_small)
    all_small = _exchange([small], gather=True, name="gather_small_grads")[0].reshape(N_DEV, rows_small * LANES)

    out_g, out_d, out_m, out_v = {}, {}, {}, {}
    for grp, names in GRAD_GROUPS.items():
        received = _exchange_wait(scatters[grp], dx, gather=False, name="scatter_" + grp + "_wait")
        for n, parts in zip(names, received):
            out_g[n], out_d[n], out_m[n], out_v[n] = _adamw(parts, w2[n], m2[n], v2[n], name="adamw_" + n)
    pieces, off = [], 0
    for n in REPL:
        k = g_vec[n].shape[1]
        pieces.append(all_small[:, off:off + k])
        off += k
    for n in CONV:
        kw, cols = g_conv[n].shape
        full = all_small[:, off:off + kw * cols].reshape(N_DEV, kw, cols)
        mine = lax.dynamic_slice_in_dim(full, me * (cols // N_DEV), cols // N_DEV, axis=2)
        pieces.append(mine.reshape(N_DEV, kw * (cols // N_DEV)))
        off += kw * cols
    pieces.append(all_small[:, off:off + 1])
    small_names = REPL + CONV
    n_mine = sum(q.shape[1] for q in pieces)
    rows_mine = -(-n_mine // (LANES * HALO)) * HALO
    zero = jnp.zeros((1, 1), F32)
    packed = [_pack_rows([src[n].reshape(1, -1) for n in small_names] + [zero], rows_mine).reshape(rows_mine, LANES)
              for src in (w2, m2, v2)]
    sg, sd, sm, sv = _adamw(_pack_rows(pieces, rows_mine), *packed, name="adamw_small")
    off = 0
    for n in small_names:
        k = w2[n].size
        for dst, src in ((out_g, sg), (out_d, sd), (out_m, sm), (out_v, sv)):
            dst[n] = src.reshape(-1)[off:off + k].reshape(w2[n].shape)
        off += k
    total_loss = sg.reshape(-1)[off]

    outs = [total_loss, dx[None]]
    for res in (out_g, out_d, out_m, out_v):
        outs += [res[n].reshape(shapes[n]) for n in WEIGHTS]
    return tuple(outs)
```

```python
import functools
import math

import numpy as np
import jax
import jax.numpy as jnp
from jax import lax
from jax.experimental import pallas as pl
from jax.experimental.pallas import tpu as pltpu

F32 = jnp.float32
BF16 = jnp.bfloat16
HI = lax.Precision.HIGHEST

D_MODEL = 2048
CHUNK = 64
D_SSM = 1024
SSD_P = 64
SSD_HEADS = 16
SSD_GROUPS = 2
SSD_N = 128
SSD_CONV = 4
SSD_CONV_DIM = D_SSM + 2 * SSD_GROUPS * SSD_N
MLA_HEADS = 8
MLA_NOPE = 128
MLA_ROPE = 64
MLA_V = 128
MLA_Q_RANK = 512
MLA_KV_RANK = 256
MLA_QK_PAD = 256
ROPE_THETA = 10000.0
D_FF = 5632
FFN_CONV = 3
PLE_DIM = 256
NORM_EPS = 1e-6
ADAM_LR, ADAM_B1, ADAM_B2, ADAM_EPS, ADAM_WD, ADAM_STEP = 0.001, 0.9, 0.999, 1e-08, 0.01, 10
N_DEV = 8

OFF_Z, OFF_XBC, OFF_QA, OFF_CKV, OFF_KR, OFF_DT, D_IN_PAD = 0, 1024, 2560, 3072, 3328, 3456, 3584
D_IN = 3408
LANES = 128
HALO = 8
VMEM_LIMIT = 56 * 1024 * 1024
FFN_TC = D_FF * 2 // N_DEV
FFN_PERM = (0, 4, 1, 5, 2, 6, 3, 7)
NEG = -1e30


def _cp(*sem):
    return pltpu.CompilerParams(dimension_semantics=tuple(sem), vmem_limit_bytes=VMEM_LIMIT)


def _tile(n, want):
    if n <= want:
        return n
    best = max(d for d in range(LANES, want + 1, LANES) if n % d == 0)
    return best


def _sigmoid(x):
    return 1.0 / (1.0 + jnp.exp(-x))


def _silu(x):
    return x * _sigmoid(x)


def _dsilu(x):
    s = _sigmoid(x)
    return s * (1.0 + x * (1.0 - s))


def _matmul(a, b, *, ta=False, tb=False, out_dtype=F32, add=None, bias=None, tm=1024, tn=1024, tk=512, name,
            mnk=None, a_spec=None, b_spec=None, o_spec=None, o_shape=None):
    if mnk is None:
        m, k = (a.shape[1], a.shape[0]) if ta else a.shape
        n = b.shape[0] if tb else b.shape[1]
        assert k == (b.shape[1] if tb else b.shape[0])
    else:
        m, n, k = mnk
    tm, tn, tk = _tile(m, tm), _tile(n, tn), _tile(k, tk)
    nk = k // tk
    dims = (((0 if ta else 1,), (1 if tb else 0,)), ((), ()))

    def body(*refs):
        a_ref, b_ref = refs[0], refs[1]
        pos = 2
        add_ref = bias_ref = None
        if add is not None:
            add_ref = refs[pos]
            pos += 1
        if bias is not None:
            bias_ref = refs[pos]
            pos += 1
        o_ref, acc_ref = refs[pos], refs[pos + 1]
        kk = pl.program_id(2)

        @pl.when(kk == 0)
        def _():
            acc_ref[...] = jnp.zeros_like(acc_ref)

        av = a_ref[...]
        bv = b_ref[...]
        av = av.reshape(av.shape[-2:]).astype(BF16)
        bv = bv.reshape(bv.shape[-2:]).astype(BF16)
        acc_ref[...] += lax.dot_general(av, bv, dims, preferred_element_type=F32)

        @pl.when(kk == nk - 1)
        def _():
            r = acc_ref[...]
            if bias_ref is not None:
                r = r + bias_ref[...]
            if add_ref is not None:
                r = r + add_ref[...].astype(F32)
            o_ref[...] = r.astype(out_dtype).reshape(o_ref.shape)

    if a_spec is None:
        a_spec = (pl.BlockSpec((tk, tm), lambda i, j, kk: (kk, i)) if ta
                  else pl.BlockSpec((tm, tk), lambda i, j, kk: (i, kk)))
    if b_spec is None:
        b_spec = (pl.BlockSpec((tn, tk), lambda i, j, kk: (j, kk)) if tb
                  else pl.BlockSpec((tk, tn), lambda i, j, kk: (kk, j)))
    if o_spec is None:
        o_spec = pl.BlockSpec((tm, tn), lambda i, j, kk: (i, j))
    if o_shape is None:
        o_shape = (m, n)
    in_specs = [a_spec, b_spec]
    args = [a, b]
    if add is not None:
        in_specs.append(pl.BlockSpec((tm, tn), lambda i, j, kk: (i, j)))
        args.append(add)
    if bias is not None:
        in_specs.append(pl.BlockSpec((1, tn), lambda i, j, kk: (0, j)))
        args.append(bias)
    return pl.pallas_call(
        body, name=name, grid=(m // tm, n // tn, nk), in_specs=in_specs, out_specs=o_spec,
        out_shape=jax.ShapeDtypeStruct(o_shape, out_dtype),
        scratch_shapes=[pltpu.VMEM((tm, tn), F32)],
        compiler_params=_cp("parallel", "parallel", "arbitrary"),
    )(*args)


def _rmsnorm_fwd(x, w, *, width, cblk=0, out_dtype=BF16, tr=256, name):
    t = x.shape[0]

    def body(x_ref, w_ref, o_ref):
        xv = x_ref[...].astype(F32)
        r = lax.rsqrt(jnp.mean(xv * xv, axis=-1, keepdims=True) + NORM_EPS)
        o_ref[...] = (xv * r * w_ref[...]).astype(out_dtype)

    return pl.pallas_call(
        body, name=name, grid=(t // tr,),
        in_specs=[pl.BlockSpec((tr, width), lambda i: (i, cblk)), pl.BlockSpec((1, width), lambda i: (0, 0))],
        out_specs=pl.BlockSpec((tr, width), lambda i: (i, 0)),
        out_shape=jax.ShapeDtypeStruct((t, width), out_dtype),
        compiler_params=_cp("parallel"),
    )(x, w)


def _rmsnorm_bwd(x, w, dy, add=None, *, width, cblk=0, out_dtype=F32, tr=256, name):
    t = x.shape[0]

    def body(*refs):
        if add is None:
            x_ref, w_ref, dy_ref, dx_ref, dw_ref = refs
            add_ref = None
        else:
            x_ref, w_ref, dy_ref, add_ref, dx_ref, dw_ref = refs
        xv = x_ref[...].astype(F32)
        dyv = dy_ref[...].astype(F32)
        r = lax.rsqrt(jnp.mean(xv * xv, axis=-1, keepdims=True) + NORM_EPS)
        xh = xv * r
        g = dyv * w_ref[...]
        dx = r * (g - xh * jnp.mean(g * xh, axis=-1, keepdims=True))
        if add_ref is not None:
            dx = dx + add_ref[...].astype(F32)
        dx_ref[...] = dx.astype(out_dtype)

        @pl.when(pl.program_id(0) == 0)
        def _():
            dw_ref[...] = jnp.zeros_like(dw_ref)

        dw_ref[...] += jnp.sum(dyv * xh, axis=0, keepdims=True)

    in_specs = [pl.BlockSpec((tr, width), lambda i: (i, cblk)), pl.BlockSpec((1, width), lambda i: (0, 0)),
                pl.BlockSpec((tr, width), lambda i: (i, 0))]
    args = [x, w, dy]
    if add is not None:
        in_specs.append(pl.BlockSpec((tr, width), lambda i: (i, 0)))
        args.append(add)
    return pl.pallas_call(
        body, name=name, grid=(t // tr,), in_specs=in_specs,
        out_specs=[pl.BlockSpec((tr, width), lambda i: (i, 0)), pl.BlockSpec((1, width), lambda i: (0, 0))],
        out_shape=[jax.ShapeDtypeStruct((t, width), out_dtype), jax.ShapeDtypeStruct((1, width), F32)],
        compiler_params=_cp("arbitrary"),
    )(*args)


def _shift_down(prev_halo, cur, j):
    if j == 0:
        return cur
    ext = jnp.concatenate([prev_halo, cur], axis=0)
    return pltpu.roll(ext, j, axis=0)[HALO:]


def _shift_up(cur, next_halo, j):
    if j == 0:
        return cur
    ext = jnp.concatenate([cur, next_halo], axis=0)
    return pltpu.roll(ext, ext.shape[0] - j, axis=0)[:cur.shape[0]]


def _conv_rows(prev, cur, w, b, kw):
    shifted = [cur]
    out = b + w[kw - 1:kw] * cur
    for j in range(1, kw):
        sh = _shift_down(prev, cur, j)
        shifted.append(sh)
        out = out + w[kw - 1 - j:kw - j] * sh
    return out, shifted


def _act_fwd(c, glu):
    if glu:
        half = c.shape[1] // 2
        return _silu(c[:, :half]) * c[:, half:]
    return _silu(c)


def _act_bwd(c, dout, glu):
    if glu:
        half = c.shape[1] // 2
        g, up = c[:, :half], c[:, half:]
        return jnp.concatenate([dout * up * _dsilu(g), dout * _silu(g)], axis=1)
    return dout * _dsilu(c)


def _conv_act_fwd(u, w, b, *, kw, glu, tc, coff, ncols, out_dtype, tr=256, name):
    t = u.shape[0]
    nb = ncols // tc
    oc = tc // 2 if glu else tc

    def body(u_ref, uh_ref, w_ref, b_ref, o_ref):
        prev = jnp.where(pl.program_id(0) == 0, 0.0, uh_ref[...])
        c, _ = _conv_rows(prev, u_ref[...], w_ref[...], b_ref[...], kw)
        o_ref[...] = _act_fwd(c, glu).astype(out_dtype)

    return pl.pallas_call(
        body, name=name, grid=(t // tr, nb),
        in_specs=[pl.BlockSpec((tr, tc), lambda i, j: (i, j + coff)),
                  pl.BlockSpec((HALO, tc), lambda i, j: (jnp.maximum(i * (tr // HALO) - 1, 0), j + coff)),
                  pl.BlockSpec((kw, tc), lambda i, j: (0, j)), pl.BlockSpec((1, tc), lambda i, j: (0, j))],
        out_specs=pl.BlockSpec((tr, oc), lambda i, j: (i, j)),
        out_shape=jax.ShapeDtypeStruct((t, nb * oc), out_dtype),
        compiler_params=_cp("parallel", "parallel"),
    )(u, u, w, b)


def _conv_act_bwd(u, w, b, dout, *, kw, glu, tc, coff, ncols, tr=256, name):
    t = u.shape[0]
    nb = ncols // tc
    nt = t // tr
    oc = tc // 2 if glu else tc

    def body(u_ref, up_ref, un_ref, d_ref, dn_ref, w_ref, b_ref, du_ref, dw_ref, db_ref):
        i = pl.program_id(1)
        cur, nxt, wv, bv = u_ref[...], un_ref[...], w_ref[...], b_ref[...]
        prev = jnp.where(i == 0, 0.0, up_ref[...])
        c_cur, shifted = _conv_rows(prev, cur, wv, bv, kw)
        c_nxt, _ = _conv_rows(cur[tr - HALO:], nxt, wv, bv, kw)
        d_cur = _act_bwd(c_cur, d_ref[...].astype(F32), glu)
        d_nxt = _act_bwd(c_nxt, jnp.where(i == nt - 1, 0.0, dn_ref[...].astype(F32)), glu)
        du = wv[kw - 1:kw] * d_cur
        for j in range(1, kw):
            du = du + wv[kw - 1 - j:kw - j] * _shift_up(d_cur, d_nxt, j)
        du_ref[...] = du.astype(BF16)

        @pl.when(i == 0)
        def _():
            dw_ref[...] = jnp.zeros_like(dw_ref)
            db_ref[...] = jnp.zeros_like(db_ref)

        db_ref[...] += jnp.sum(d_cur, axis=0, keepdims=True)
        dw_ref[...] += jnp.concatenate(
            [jnp.sum(d_cur * shifted[kw - 1 - k], axis=0, keepdims=True) for k in range(kw)], axis=0)

    nh = tr // HALO
    return pl.pallas_call(
        body, name=name, grid=(nb, nt),
        in_specs=[pl.BlockSpec((tr, tc), lambda j, i: (i, j + coff)),
                  pl.BlockSpec((HALO, tc), lambda j, i: (jnp.maximum(i * nh - 1, 0), j + coff)),
                  pl.BlockSpec((HALO, tc), lambda j, i: (jnp.minimum((i + 1) * nh, t // HALO - 1), j + coff)),
                  pl.BlockSpec((tr, oc), lambda j, i: (i, j)),
                  pl.BlockSpec((HALO, oc), lambda j, i: (jnp.minimum((i + 1) * nh, t // HALO - 1), j)),
                  pl.BlockSpec((kw, tc), lambda j, i: (0, j)), pl.BlockSpec((1, tc), lambda j, i: (0, j))],
        out_specs=[pl.BlockSpec((tr, tc), lambda j, i: (i, j)), pl.BlockSpec((kw, tc), lambda j, i: (0, j)),
                   pl.BlockSpec((1, tc), lambda j, i: (0, j))],
        out_shape=[jax.ShapeDtypeStruct((t, ncols), BF16), jax.ShapeDtypeStruct((kw, ncols), F32),
                   jax.ShapeDtypeStruct((1, ncols), F32)],
        compiler_params=_cp("parallel", "arbitrary"),
    )(u, u, u, dout, dout, w, b)


def _ple_fwd(x2, gl, pe, pw, *, tr=256, name):
    t, d = x2.shape

    def body(x_ref, gl_ref, pe_ref, pw_ref, o_ref):
        pv = pe_ref[...]
        r = lax.rsqrt(jnp.mean(pv * pv, axis=-1, keepdims=True) + NORM_EPS)
        o_ref[...] = x_ref[...] + _sigmoid(gl_ref[...]) * (pv * r * pw_ref[...])

    blk = pl.BlockSpec((tr, d), lambda i: (i, 0))
    return pl.pallas_call(
        body, name=name, grid=(t // tr,), in_specs=[blk, blk, blk, pl.BlockSpec((1, d), lambda i: (0, 0))],
        out_specs=blk, out_shape=jax.ShapeDtypeStruct((t, d), F32), compiler_params=_cp("parallel"),
    )(x2, gl, pe, pw)


def _ple_bwd(dx3, gl, pe, pw, *, tr=256, name):
    t, d = dx3.shape

    def body(dx_ref, gl_ref, pe_ref, pw_ref, dgl_ref, db_ref, dpe_ref, dpw_ref):
        dx, pv, pwv = dx_ref[...], pe_ref[...], pw_ref[...]
        gate = _sigmoid(gl_ref[...])
        r = lax.rsqrt(jnp.mean(pv * pv, axis=-1, keepdims=True) + NORM_EPS)
        ph = pv * r
        dgl = dx * (ph * pwv) * gate * (1.0 - gate)
        de = dx * gate
        g = de * pwv
        dgl_ref[...] = dgl.astype(BF16)
        dpe_ref[...] = (r * (g - ph * jnp.mean(g * ph, axis=-1, keepdims=True))).astype(BF16)

        @pl.when(pl.program_id(0) == 0)
        def _():
            db_ref[...] = jnp.zeros_like(db_ref)
            dpw_ref[...] = jnp.zeros_like(dpw_ref)

        db_ref[...] += jnp.sum(dgl, axis=0, keepdims=True)
        dpw_ref[...] += jnp.sum(de * ph, axis=0, keepdims=True)

    blk = pl.BlockSpec((tr, d), lambda i: (i, 0))
    row = pl.BlockSpec((1, d), lambda i: (0, 0))
    return pl.pallas_call(
        body, name=name, grid=(t // tr,), in_specs=[blk, blk, blk, row], out_specs=[blk, row, blk, row],
        out_shape=[jax.ShapeDtypeStruct((t, d), BF16), jax.ShapeDtypeStruct((1, d), F32),
                   jax.ShapeDtypeStruct((t, d), BF16), jax.ShapeDtypeStruct((1, d), F32)],
        compiler_params=_cp("arbitrary"),
    )(dx3, gl, pe, pw)


def _loss_head(x3, fw, target, *, tr=256, name):
    t, d = x3.shape

    def body(x_ref, w_ref, t_ref, l_ref, dx_ref, dw_ref):
        xv, wv = x_ref[...], w_ref[...]
        r = lax.rsqrt(jnp.mean(xv * xv, axis=-1, keepdims=True) + NORM_EPS)
        xh = xv * r
        err = xh * wv - t_ref[...]
        dy = err * (1.0 / d)
        g = dy * wv
        dx_ref[...] = r * (g - xh * jnp.mean(g * xh, axis=-1, keepdims=True))

        @pl.when(pl.program_id(0) == 0)
        def _():
            l_ref[...] = jnp.zeros_like(l_ref)
            dw_ref[...] = jnp.zeros_like(dw_ref)

        l_ref[...] += 0.5 * jnp.sum(jnp.mean(err * err, axis=-1, keepdims=True), axis=0, keepdims=True)
        dw_ref[...] += jnp.sum(dy * xh, axis=0, keepdims=True)

    blk = pl.BlockSpec((tr, d), lambda i: (i, 0))
    row = pl.BlockSpec((1, d), lambda i: (0, 0))
    return pl.pallas_call(
        body, name=name, grid=(t // tr,), in_specs=[blk, row, blk],
        out_specs=[pl.BlockSpec((1, 1), lambda i: (0, 0)), blk, row],
        out_shape=[jax.ShapeDtypeStruct((1, 1), F32), jax.ShapeDtypeStruct((t, d), F32),
                   jax.ShapeDtypeStruct((1, d), F32)],
        compiler_params=_cp("arbitrary"),
    )(x3, fw, target)


def _rope(blk, tab_ref):
    return blk * tab_ref[0] + pltpu.roll(blk, 96, axis=1) * tab_ref[1] + pltpu.roll(blk, 32, axis=1) * tab_ref[2]


def _unrope(g, tab_ref):
    return g * tab_ref[0] + pltpu.roll(g * tab_ref[1], 32, axis=1) + pltpu.roll(g * tab_ref[2], 96, axis=1)


def _mla_prep(q, kv, proj, tabs, *, tr=512, name):
    t = q.shape[0]

    def body(q_ref, kv_ref, kr_ref, tab_ref, qo_ref, ko_ref, vo_ref):
        qv, kvv = q_ref[...], kv_ref[...]
        qo_ref[0, :, :MLA_NOPE] = qv[:, :MLA_NOPE].astype(BF16)
        qo_ref[0, :, MLA_NOPE:] = _rope(qv[:, MLA_NOPE:], tab_ref).astype(BF16)
        ko_ref[0, :, :MLA_NOPE] = kvv[:, :MLA_NOPE].astype(BF16)
        ko_ref[0, :, MLA_NOPE:] = _rope(kr_ref[...], tab_ref).astype(BF16)
        vo_ref[0] = kvv[:, MLA_NOPE:].astype(BF16)

    return pl.pallas_call(
        body, name=name, grid=(t // tr, MLA_HEADS),
        in_specs=[pl.BlockSpec((tr, MLA_QK_PAD), lambda i, h: (i, h)),
                  pl.BlockSpec((tr, MLA_NOPE + MLA_V), lambda i, h: (i, h)),
                  pl.BlockSpec((tr, LANES), lambda i, h: (i, OFF_KR // LANES)),
                  pl.BlockSpec((3, tr, LANES), lambda i, h: (0, i, 0))],
        out_specs=[pl.BlockSpec((1, tr, MLA_QK_PAD), lambda i, h: (h, i, 0)),
                   pl.BlockSpec((1, tr, MLA_QK_PAD), lambda i, h: (h, i, 0)),
                   pl.BlockSpec((1, tr, MLA_V), lambda i, h: (h, i, 0))],
        out_shape=[jax.ShapeDtypeStruct((MLA_HEADS, t, MLA_QK_PAD), BF16),
                   jax.ShapeDtypeStruct((MLA_HEADS, t, MLA_QK_PAD), BF16),
                   jax.ShapeDtypeStruct((MLA_HEADS, t, MLA_V), BF16)],
        compiler_params=_cp("parallel", "parallel"),
    )(q, kv, proj, tabs)


def _mla_unprep(dq3, dk3, dv3, tabs, *, tr=256, name):
    t = dq3.shape[1]

    def body(dq_ref, dk_ref, dv_ref, tab_ref, qo_ref, kvo_ref, kro_ref):
        kr = jnp.zeros((tr, LANES), F32)
        for h in range(MLA_HEADS):
            c0 = h * MLA_QK_PAD
            qo_ref[:, c0:c0 + MLA_NOPE] = dq_ref[h, :, :MLA_NOPE].astype(BF16)
            qo_ref[:, c0 + MLA_NOPE:c0 + MLA_QK_PAD] = _unrope(dq_ref[h, :, MLA_NOPE:], tab_ref).astype(BF16)
            kvo_ref[:, c0:c0 + MLA_NOPE] = dk_ref[h, :, :MLA_NOPE].astype(BF16)
            kvo_ref[:, c0 + MLA_NOPE:c0 + MLA_QK_PAD] = dv_ref[h].astype(BF16)
            kr = kr + dk_ref[h, :, MLA_NOPE:]
        kro_ref[...] = _unrope(kr, tab_ref).astype(BF16)

    return pl.pallas_call(
        body, name=name, grid=(t // tr,),
        in_specs=[pl.BlockSpec((MLA_HEADS, tr, MLA_QK_PAD), lambda i: (0, i, 0)),
                  pl.BlockSpec((MLA_HEADS, tr, MLA_QK_PAD), lambda i: (0, i, 0)),
                  pl.BlockSpec((MLA_HEADS, tr, MLA_V), lambda i: (0, i, 0)),
                  pl.BlockSpec((3, tr, LANES), lambda i: (0, i, 0))],
        out_specs=[pl.BlockSpec((tr, MLA_HEADS * MLA_QK_PAD), lambda i: (i, 0)),
                   pl.BlockSpec((tr, MLA_HEADS * MLA_QK_PAD), lambda i: (i, 0)),
                   pl.BlockSpec((tr, LANES), lambda i: (i, 0))],
        out_shape=[jax.ShapeDtypeStruct((t, MLA_HEADS * MLA_QK_PAD), BF16),
                   jax.ShapeDtypeStruct((t, MLA_HEADS * MLA_QK_PAD), BF16),
                   jax.ShapeDtypeStruct((t, LANES), BF16)],
        compiler_params=_cp("parallel"),
    )(dq3, dk3, dv3, tabs)


ATT_BLK = 256
ATT_SCALE = 1.0 / math.sqrt(MLA_NOPE + MLA_ROPE)
_NT = (((1,), (1,)), ((), ()))
_TN = (((0,), (0,)), ((), ()))


def _att_scores(q, k, qi, kj):
    s = lax.dot_general(q, k, _NT, preferred_element_type=F32) * ATT_SCALE
    row = qi * ATT_BLK + lax.broadcasted_iota(jnp.int32, s.shape, 0)
    col = kj * ATT_BLK + lax.broadcasted_iota(jnp.int32, s.shape, 1)
    return jnp.where((col >> 6) <= (row >> 6), s, NEG)


def _attn_fwd(q3, k3, v3, *, name):
    t = q3.shape[1]
    nq = t // ATT_BLK

    def body(q_ref, k_ref, v_ref, o_ref, lse_ref):
        qi = pl.program_id(1)
        q = q_ref[0]

        def step(j, carry):
            m, l, acc = carry
            rows = pl.ds(pl.multiple_of(j * ATT_BLK, ATT_BLK), ATT_BLK)
            s = _att_scores(q, k_ref[0, rows, :], qi, j)
            m_new = jnp.maximum(m, jnp.max(s, axis=-1, keepdims=True))
            p = jnp.exp(s - m_new)
            alpha = jnp.exp(m - m_new)
            l = alpha * l + jnp.sum(p, axis=-1, keepdims=True)
            acc = alpha * acc + jnp.dot(p.astype(BF16), v_ref[0, rows, :], preferred_element_type=F32)
            return m_new, l, acc

        init = (jnp.full((ATT_BLK, 1), NEG, F32), jnp.zeros((ATT_BLK, 1), F32), jnp.zeros((ATT_BLK, MLA_V), F32))
        m, l, acc = lax.fori_loop(0, qi + 1, step, init)
        o_ref[...] = acc / l
        lse_ref[0] = m + jnp.log(l)

    return pl.pallas_call(
        body, name=name, grid=(MLA_HEADS, nq),
        in_specs=[pl.BlockSpec((1, ATT_BLK, MLA_QK_PAD), lambda h, i: (h, i, 0)),
                  pl.BlockSpec((1, t, MLA_QK_PAD), lambda h, i: (h, 0, 0)),
                  pl.BlockSpec((1, t, MLA_V), lambda h, i: (h, 0, 0))],
        out_specs=[pl.BlockSpec((ATT_BLK, MLA_V), lambda h, i: (i, h)),
                   pl.BlockSpec((1, ATT_BLK, 1), lambda h, i: (h, i, 0))],
        out_shape=[jax.ShapeDtypeStruct((t, MLA_HEADS * MLA_V), F32), jax.ShapeDtypeStruct((MLA_HEADS, t, 1), F32)],
        compiler_params=_cp("parallel", "parallel"),
    )(q3, k3, v3)


def _attn_bwd_dq(q3, k3, v3, o, dcat, lse, *, name):
    t = q3.shape[1]
    nq = t // ATT_BLK

    def body(q_ref, k_ref, v_ref, o_ref, do_ref, lse_ref, dq_ref, dl_ref):
        qi = pl.program_id(1)
        q, do, lse = q_ref[0], do_ref[...], lse_ref[0]
        delta = jnp.sum(o_ref[...] * do, axis=-1, keepdims=True)
        dob = do.astype(BF16)

        def step(j, dq):
            rows = pl.ds(pl.multiple_of(j * ATT_BLK, ATT_BLK), ATT_BLK)
            k = k_ref[0, rows, :]
            p = jnp.exp(_att_scores(q, k, qi, j) - lse)
            dp = lax.dot_general(dob, v_ref[0, rows, :], _NT, preferred_element_type=F32)
            ds = (p * (dp - delta) * ATT_SCALE).astype(BF16)
            return dq + jnp.dot(ds, k, preferred_element_type=F32)

        dq_ref[0] = lax.fori_loop(0, qi + 1, step, jnp.zeros((ATT_BLK, MLA_QK_PAD), F32))
        dl_ref[0] = delta

    return pl.pallas_call(
        body, name=name, grid=(MLA_HEADS, nq),
        in_specs=[pl.BlockSpec((1, ATT_BLK, MLA_QK_PAD), lambda h, i: (h, i, 0)),
                  pl.BlockSpec((1, t, MLA_QK_PAD), lambda h, i: (h, 0, 0)),
                  pl.BlockSpec((1, t, MLA_V), lambda h, i: (h, 0, 0)),
                  pl.BlockSpec((ATT_BLK, MLA_V), lambda h, i: (i, h)),
                  pl.BlockSpec((ATT_BLK, MLA_V), lambda h, i: (i, MLA_HEADS + h)),
                  pl.BlockSpec((1, ATT_BLK, 1), lambda h, i: (h, i, 0))],
        out_specs=[pl.BlockSpec((1, ATT_BLK, MLA_QK_PAD), lambda h, i: (h, i, 0)),
                   pl.BlockSpec((1, ATT_BLK, 1), lambda h, i: (h, i, 0))],
        out_shape=[jax.ShapeDtypeStruct((MLA_HEADS, t, MLA_QK_PAD), F32),
                   jax.ShapeDtypeStruct((MLA_HEADS, t, 1), F32)],
        compiler_params=_cp("parallel", "parallel"),
    )(q3, k3, v3, o, dcat, lse)


def _attn_bwd_dkv(q3, k3, v3, dcat, lse, delta, *, name):
    t = q3.shape[1]
    nq = t // ATT_BLK

    def body(q_ref, k_ref, v_ref, do_ref, lse_ref, dl_ref, dk_ref, dv_ref):
        kj = pl.program_id(1)
        k, v = k_ref[0], v_ref[0]

        def step(i, carry):
            dk, dv = carry
            rows = pl.ds(pl.multiple_of(i * ATT_BLK, ATT_BLK), ATT_BLK)
            q = q_ref[0, rows, :]
            dob = do_ref[rows, :].astype(BF16)
            p = jnp.exp(_att_scores(q, k, i, kj) - lse_ref[0, rows, :])
            dv = dv + lax.dot_general(p.astype(BF16), dob, _TN, preferred_element_type=F32)
            dp = lax.dot_general(dob, v, _NT, preferred_element_type=F32)
            ds = (p * (dp - dl_ref[0, rows, :]) * ATT_SCALE).astype(BF16)
            dk = dk + lax.dot_general(ds, q, _TN, preferred_element_type=F32)
            return dk, dv

        init = (jnp.zeros((ATT_BLK, MLA_QK_PAD), F32), jnp.zeros((ATT_BLK, MLA_V), F32))
        dk, dv = lax.fori_loop(kj, nq, step, init)
        dk_ref[0] = dk
        dv_ref[0] = dv

    return pl.pallas_call(
        body, name=name, grid=(MLA_HEADS, nq),
        in_specs=[pl.BlockSpec((1, t, MLA_QK_PAD), lambda h, j: (h, 0, 0)),
                  pl.BlockSpec((1, ATT_BLK, MLA_QK_PAD), lambda h, j: (h, j, 0)),
                  pl.BlockSpec((1, ATT_BLK, MLA_V), lambda h, j: (h, j, 0)),
                  pl.BlockSpec((t, MLA_V), lambda h, j: (0, MLA_HEADS + h)),
                  pl.BlockSpec((1, t, 1), lambda h, j: (h, 0, 0)),
                  pl.BlockSpec((1, t, 1), lambda h, j: (h, 0, 0))],
        out_specs=[pl.BlockSpec((1, ATT_BLK, MLA_QK_PAD), lambda h, j: (h, j, 0)),
                   pl.BlockSpec((1, ATT_BLK, MLA_V), lambda h, j: (h, j, 0))],
        out_shape=[jax.ShapeDtypeStruct((MLA_HEADS, t, MLA_QK_PAD), F32),
                   jax.ShapeDtypeStruct((MLA_HEADS, t, MLA_V), F32)],
        compiler_params=_cp("parallel", "parallel"),
    )(q3, k3, v3, dcat, lse, delta)


def _ssd_prep(proj, bias128, alog128, *, name):
    t = proj.shape[0]
    nc = t // CHUNK

    def body(raw_ref, b_ref, al_ref, dt_ref, cs_ref, a_ref):
        xv = raw_ref[...] + b_ref[...]
        dt = jnp.maximum(xv, 0.0) + jnp.log(1.0 + jnp.exp(-jnp.abs(xv)))
        a = -jnp.exp(al_ref[...])
        adt = (dt * a).reshape(nc, CHUNK, LANES)
        li = lax.broadcasted_iota(jnp.int32, (nc, CHUNK, CHUNK), 1)
        si = lax.broadcasted_iota(jnp.int32, (nc, CHUNK, CHUNK), 2)
        tril = jnp.where(si <= li, 1.0, 0.0).astype(F32)
        cs = lax.dot_general(tril, adt, (((2,), (1,)), ((0,), (0,))), precision=HI, preferred_element_type=F32)
        dt_ref[...] = dt
        cs_ref[...] = cs.reshape(t, LANES)
        a_ref[...] = a

    blk = pl.BlockSpec((t, LANES), lambda i: (0, 0))
    row = pl.BlockSpec((1, LANES), lambda i: (0, 0))
    return pl.pallas_call(
        body, name=name, grid=(1,),
        in_specs=[pl.BlockSpec((t, LANES), lambda i: (0, OFF_DT // LANES)), row, row],
        out_specs=[blk, blk, row],
        out_shape=[jax.ShapeDtypeStruct((t, LANES), F32), jax.ShapeDtypeStruct((t, LANES), F32),
                   jax.ShapeDtypeStruct((1, LANES), F32)],
        compiler_params=_cp("arbitrary"),
    )(proj, bias128, alog128)


def _ssd_prep_bwd(ddt128, dadt128, proj, bias128, dt128, a128, dd_h, *, name):
    t = proj.shape[0]

    def body(ddt_ref, dadt_ref, raw_ref, b_ref, dt_ref, a_ref, dd_ref, draw_ref, db_ref, dal_ref, dds_ref):
        draw = ddt_ref[...] * _sigmoid(raw_ref[...] + b_ref[...])
        draw_ref[...] = draw.astype(BF16)
        db_ref[...] = jnp.sum(draw, axis=0, keepdims=True)
        dal_ref[...] = jnp.sum(dadt_ref[...] * dt_ref[...], axis=0, keepdims=True) * a_ref[...]
        dds_ref[...] = jnp.sum(dd_ref[...], axis=-1, keepdims=True)

    blk = pl.BlockSpec((t, LANES), lambda i: (0, 0))
    row = pl.BlockSpec((1, LANES), lambda i: (0, 0))
    return pl.pallas_call(
        body, name=name, grid=(1,),
        in_specs=[blk, blk, pl.BlockSpec((t, LANES), lambda i: (0, OFF_DT // LANES)), row, blk, row,
                  pl.BlockSpec((SSD_HEADS, SSD_P), lambda i: (0, 0))],
        out_specs=[blk, row, row, pl.BlockSpec((SSD_HEADS, 1), lambda i: (0, 0))],
        out_shape=[jax.ShapeDtypeStruct((t, LANES), BF16), jax.ShapeDtypeStruct((1, LANES), F32),
                   jax.ShapeDtypeStruct((1, LANES), F32), jax.ShapeDtypeStruct((SSD_HEADS, 1), F32)],
        compiler_params=_cp("arbitrary"),
    )(ddt128, dadt128, proj, bias128, dt128, a128, dd_h)


def _bdot(a, b, ca, cb, precision=None):
    return lax.dot_general(a, b, (((ca,), (cb,)), ((0,), (0,))), precision=precision, preferred_element_type=F32)


def _ssd_common(xs_ref, dt_ref, cs_ref, csr_ref, b_ref, c_ref, nc):
    x = xs_ref[0].reshape(nc, CHUNK, SSD_P)
    dt = dt_ref[0].reshape(nc, CHUNK, SSD_P)
    cs = cs_ref[0].reshape(nc, CHUNK, SSD_P)
    csr = csr_ref[0]
    bm = b_ref[0].reshape(nc, CHUNK, SSD_N).astype(BF16)
    cm = c_ref[0].reshape(nc, CHUNK, SSD_N).astype(BF16)
    li = lax.broadcasted_iota(jnp.int32, (nc, CHUNK, CHUNK), 1)
    si = lax.broadcasted_iota(jnp.int32, (nc, CHUNK, CHUNK), 2)
    lmat = jnp.exp(jnp.where(si <= li, cs - csr, NEG))
    g = _bdot(cm, bm, 2, 2)
    cs_last = jnp.sum(jnp.where(li == CHUNK - 1, cs, 0.0), axis=1, keepdims=True)
    xdt = x * dt
    dec = jnp.exp(cs_last - cs)
    return x, dt, cs, bm, cm, li, si, lmat, g, cs_last, xdt, dec


def _ssd_fwd(xs_h, dt_h, cs_h, cs_row, b_g, c_g, dskip_h, *, name):
    t = xs_h.shape[1]
    nc = t // CHUNK
    hpg = SSD_HEADS // SSD_GROUPS

    def body(xs_ref, dt_ref, cs_ref, csr_ref, b_ref, c_ref, dk_ref, y_ref, st_ref, sc_ref, cd_ref):
        x, dt, cs, bm, cm, li, si, lmat, g, cs_last, xdt, dec = _ssd_common(xs_ref, dt_ref, cs_ref, csr_ref, b_ref,
                                                                           c_ref, nc)
        yd = _bdot((g * lmat).astype(BF16), xdt.astype(BF16), 2, 1)
        sc_ref[...] = _bdot(bm, (dec * xdt).astype(BF16), 1, 1)
        cd_ref[...] = jnp.exp(cs_last)

        def step(c, s):
            st_ref[0, c] = s
            return s * cd_ref[c] + sc_ref[c]

        lax.fori_loop(0, nc, step, jnp.zeros((SSD_N, SSD_P), F32))
        yo = _bdot(cm, st_ref[0].astype(BF16), 2, 1) * jnp.exp(cs)
        y_ref[0] = (yd + yo + dk_ref[0] * x).reshape(t, SSD_P)

    head = pl.BlockSpec((1, t, SSD_P), lambda h: (h, 0, 0))
    grp = pl.BlockSpec((1, t, SSD_N), lambda h: (h // hpg, 0, 0))
    return pl.pallas_call(
        body, name=name, grid=(SSD_HEADS,),
        in_specs=[head, head, head, pl.BlockSpec((1, nc, 1, CHUNK), lambda h: (h, 0, 0, 0)), grp, grp,
                  pl.BlockSpec((1, 1, SSD_P), lambda h: (h, 0, 0))],
        out_specs=[head, pl.BlockSpec((1, nc, SSD_N, SSD_P), lambda h: (h, 0, 0, 0))],
        out_shape=[jax.ShapeDtypeStruct((SSD_HEADS, t, SSD_P), F32),
                   jax.ShapeDtypeStruct((SSD_HEADS, nc, SSD_N, SSD_P), F32)],
        scratch_shapes=[pltpu.VMEM((nc, SSD_N, SSD_P), F32), pltpu.VMEM((nc, 1, SSD_P), F32)],
        compiler_params=_cp("parallel"),
    )(xs_h, dt_h, cs_h, cs_row, b_g, c_g, dskip_h)


def _ssd_bwd(xs_h, dt_h, cs_h, cs_row, b_g, c_g, dskip_h, a_h, states, dy_h, *, name):
    t = xs_h.shape[1]
    nc = t // CHUNK
    hpg = SSD_HEADS // SSD_GROUPS

    def body(xs_ref, dt_ref, cs_ref, csr_ref, b_ref, c_ref, dk_ref, a_ref, st_ref, dy_ref,
             dxs_ref, ddt_ref, dadt_ref, db_ref, dc_ref, dd_ref, dsl_ref, dsc_ref, cd_ref):
        x, dt, cs, bm, cm, li, si, lmat, g, cs_last, xdt, dec = _ssd_common(xs_ref, dt_ref, cs_ref, csr_ref, b_ref,
                                                                           c_ref, nc)
        dy = dy_ref[0].reshape(nc, CHUNK, SSD_P)
        dyb = dy.astype(BF16)
        xdtb = xdt.astype(BF16)
        sprev = st_ref[0]
        sprevb = sprev.astype(BF16)
        cdec = jnp.exp(cs_last)
        ecs = jnp.exp(cs)
        dw = (ecs * dy).astype(BF16)
        wmat = _bdot(cm, sprevb, 2, 1)
        dcs = jnp.sum(dy * ecs * wmat, axis=2, keepdims=True)
        dcm = _bdot(dw, sprevb, 2, 2)
        dsl_ref[...] = _bdot(cm, dw, 1, 1)
        cd_ref[...] = cdec

        def step(k, ds):
            c = nc - 1 - k
            dsc_ref[c] = ds
            return ds * cd_ref[c] + dsl_ref[c]

        lax.fori_loop(0, nc, step, jnp.zeros((SSD_N, SSD_P), F32))
        dsc = dsc_ref[...]
        dscb = dsc.astype(BF16)
        d_last = jnp.sum(jnp.sum(dsc * sprev, axis=1, keepdims=True) * cdec, axis=2, keepdims=True)
        z = dec * xdt
        dbm = _bdot(z.astype(BF16), dscb, 2, 2)
        dz = _bdot(bm, dscb, 2, 1)
        dxdt = dec * dz
        t2 = jnp.sum(dz * z, axis=2, keepdims=True)
        dcs = dcs - t2
        d_last = d_last + jnp.sum(t2, axis=1, keepdims=True)
        m = g * lmat
        mb = m.astype(BF16)
        dm = _bdot(dyb, xdtb, 2, 2)
        dxdt = dxdt + _bdot(mb, dyb, 1, 1)
        dseg = dm * m
        dcs = dcs + jnp.sum(dseg, axis=2, keepdims=True)
        ones = jnp.ones((nc, CHUNK, SSD_P), F32)
        dcs = dcs - _bdot(dseg, ones, 1, 1, precision=HI)
        dg = (dm * lmat).astype(BF16)
        dcm = dcm + _bdot(dg, bm, 2, 1)
        dbm = dbm + _bdot(dg, cm, 1, 1)
        dcs = dcs + jnp.where(li[:, :, :SSD_P] == CHUNK - 1, d_last, 0.0)
        triu = jnp.where(li <= si, 1.0, 0.0).astype(F32)
        dadt = _bdot(triu, dcs, 2, 1, precision=HI)
        dk = dk_ref[0]
        dxs_ref[0] = (dxdt * dt + dk * dy).reshape(t, SSD_P)
        ddt_ref[0] = (jnp.sum(dxdt * x, axis=2, keepdims=True) + dadt * a_ref[0]).reshape(t, SSD_P)
        dadt_ref[0] = dadt.reshape(t, SSD_P)
        dd_ref[0] = jnp.sum(jnp.sum(dy * x, axis=1, keepdims=True), axis=0)

        @pl.when(pl.program_id(1) == 0)
        def _():
            db_ref[...] = jnp.zeros_like(db_ref)
            dc_ref[...] = jnp.zeros_like(dc_ref)

        db_ref[0] += dbm.reshape(t, SSD_N)
        dc_ref[0] += dcm.reshape(t, SSD_N)

    head = pl.BlockSpec((1, t, SSD_P), lambda gi, hi: (gi * hpg + hi, 0, 0))
    grp = pl.BlockSpec((1, t, SSD_N), lambda gi, hi: (gi, 0, 0))
    lane = pl.BlockSpec((1, 1, SSD_P), lambda gi, hi: (gi * hpg + hi, 0, 0))
    return pl.pallas_call(
        body, name=name, grid=(SSD_GROUPS, hpg),
        in_specs=[head, head, head, pl.BlockSpec((1, nc, 1, CHUNK), lambda gi, hi: (gi * hpg + hi, 0, 0, 0)),
                  grp, grp, lane, lane, pl.BlockSpec((1, nc, SSD_N, SSD_P), lambda gi, hi: (gi * hpg + hi, 0, 0, 0)),
                  head],
        out_specs=[head, head, head, grp, grp, lane],
        out_shape=[jax.ShapeDtypeStruct((SSD_HEADS, t, SSD_P), F32)] * 3
        + [jax.ShapeDtypeStruct((SSD_GROUPS, t, SSD_N), F32)] * 2
        + [jax.ShapeDtypeStruct((SSD_HEADS, 1, SSD_P), F32)],
        scratch_shapes=[pltpu.VMEM((nc, SSD_N, SSD_P), F32), pltpu.VMEM((nc, SSD_N, SSD_P), F32),
                        pltpu.VMEM((nc, 1, SSD_P), F32)],
        compiler_params=_cp("parallel", "arbitrary"),
    )(xs_h, dt_h, cs_h, cs_row, b_g, c_g, dskip_h, a_h, states, dy_h)


def _ssd_gate_fwd(y, proj, w, *, tr=256, name):
    t = y.shape[0]
    gw = D_SSM // SSD_GROUPS

    def body(y_ref, z_ref, w_ref, o_ref):
        v = y_ref[...] * _silu(z_ref[...])
        for gi in range(SSD_GROUPS):
            vg = v[:, gi * gw:(gi + 1) * gw]
            r = lax.rsqrt(jnp.mean(vg * vg, axis=-1, keepdims=True) + NORM_EPS)
            o_ref[:, gi * gw:(gi + 1) * gw] = (vg * r * w_ref[:, gi * gw:(gi + 1) * gw]).astype(BF16)

    blk = pl.BlockSpec((tr, D_SSM), lambda i: (i, 0))
    return pl.pallas_call(
        body, name=name, grid=(t // tr,), in_specs=[blk, blk, pl.BlockSpec((1, D_SSM), lambda i: (0, 0))],
        out_specs=blk, out_shape=jax.ShapeDtypeStruct((t, D_SSM), BF16), compiler_params=_cp("parallel"),
    )(y, proj, w)


def _ssd_gate_bwd(y, proj, w, dcat, *, tr=256, name):
    t = y.shape[0]
    gw = D_SSM // SSD_GROUPS

    def body(y_ref, z_ref, w_ref, d_ref, dy_ref, dz_ref, dw_ref):
        yv, zv, dv = y_ref[...], z_ref[...], d_ref[...].astype(F32)
        sz = _silu(zv)
        v = yv * sz

        @pl.when(pl.program_id(0) == 0)
        def _():
            dw_ref[...] = jnp.zeros_like(dw_ref)

        for gi in range(SSD_GROUPS):
            sl = slice(gi * gw, (gi + 1) * gw)
            vg, dg = v[:, sl], dv[:, sl]
            r = lax.rsqrt(jnp.mean(vg * vg, axis=-1, keepdims=True) + NORM_EPS)
            vh = vg * r
            gg = dg * w_ref[:, sl]
            dvg = r * (gg - vh * jnp.mean(gg * vh, axis=-1, keepdims=True))
            dy_ref[:, sl] = dvg * sz[:, sl]
            dz_ref[:, sl] = (dvg * yv[:, sl] * _dsilu(zv[:, sl])).astype(BF16)
            dw_ref[:, sl] += jnp.sum(dg * vh, axis=0, keepdims=True)

    blk = pl.BlockSpec((tr, D_SSM), lambda i: (i, 0))
    row = pl.BlockSpec((1, D_SSM), lambda i: (0, 0))
    return pl.pallas_call(
        body, name=name, grid=(t // tr,), in_specs=[blk, blk, row, blk], out_specs=[blk, blk, row],
        out_shape=[jax.ShapeDtypeStruct((t, D_SSM), F32), jax.ShapeDtypeStruct((t, D_SSM), BF16),
                   jax.ShapeDtypeStruct((1, D_SSM), F32)],
        compiler_params=_cp("arbitrary"),
    )(y, proj, w, dcat)


def _pad_lanes(v):
    return jnp.pad(v, ((0, 0), (0, LANES - v.shape[1])))


def _to_heads(v):
    return v.reshape(v.shape[0], SSD_HEADS, SSD_P).transpose(1, 0, 2)


def _from_heads(v):
    return v.transpose(1, 0, 2).reshape(v.shape[1], SSD_HEADS * SSD_P)


def _per_head(v128, t):
    return jnp.broadcast_to(v128[:, :SSD_HEADS].T[:, :, None], (SSD_HEADS, t, SSD_P))


def _ssd_forward(proj, conv_w, conv_b, dt_bias, a_log, d_skip, ssd_norm_w):
    t = proj.shape[0]
    nc = t // CHUNK
    xbc = _conv_act_fwd(proj, conv_w, conv_b, kw=SSD_CONV, glu=False, tc=512, coff=OFF_XBC // 512,
                        ncols=SSD_CONV_DIM, out_dtype=F32, name="ssd_conv_fwd")
    bias128, alog128 = _pad_lanes(dt_bias), _pad_lanes(a_log)
    dt128, cs128, a128 = _ssd_prep(proj, bias128, alog128, name="ssd_prep")
    dt_h, cs_h = _per_head(dt128, t), _per_head(cs128, t)
    cs_row = cs128[:, :SSD_HEADS].T.reshape(SSD_HEADS, nc, 1, CHUNK)
    xs_h = _to_heads(xbc[:, :D_SSM])
    gn = SSD_GROUPS * SSD_N
    b_g = xbc[:, D_SSM:D_SSM + gn].reshape(t, SSD_GROUPS, SSD_N).transpose(1, 0, 2)
    c_g = xbc[:, D_SSM + gn:].reshape(t, SSD_GROUPS, SSD_N).transpose(1, 0, 2)
    dskip_h = jnp.broadcast_to(d_skip[0][:, None, None], (SSD_HEADS, 1, SSD_P))
    a_h = jnp.broadcast_to(a128[0, :SSD_HEADS][:, None, None], (SSD_HEADS, 1, SSD_P))
    y_h, states = _ssd_fwd(xs_h, dt_h, cs_h, cs_row, b_g, c_g, dskip_h, name="ssd_scan_fwd")
    y = _from_heads(y_h)
    y_ssd = _ssd_gate_fwd(y, proj, ssd_norm_w, name="ssd_gate_fwd")
    saved = (proj, conv_w, conv_b, ssd_norm_w, bias128, dt128, a128, dt_h, cs_h, cs_row, xs_h, b_g, c_g, dskip_h, a_h,
             states, y)
    return y_ssd, saved


def _ssd_backward(saved, dcat):
    (proj, conv_w, conv_b, ssd_norm_w, bias128, dt128, a128, dt_h, cs_h, cs_row, xs_h, b_g, c_g, dskip_h, a_h, states,
     y) = saved
    t = proj.shape[0]
    dy, dz, d_norm_w = _ssd_gate_bwd(y, proj, ssd_norm_w, dcat, name="ssd_gate_bwd")
    dxs_h, ddt_h, dadt_h, db_g, dc_g, dd_h = _ssd_bwd(xs_h, dt_h, cs_h, cs_row, b_g, c_g, dskip_h, a_h, states,
                                                      _to_heads(dy), name="ssd_scan_bwd")
    gn = SSD_GROUPS * SSD_N
    dxc = jnp.concatenate([_from_heads(dxs_h), db_g.transpose(1, 0, 2).reshape(t, gn),
                           dc_g.transpose(1, 0, 2).reshape(t, gn)], axis=1)
    dxbc, d_conv_w, d_conv_b = _conv_act_bwd(proj, conv_w, conv_b, dxc, kw=SSD_CONV, glu=False, tc=512,
                                             coff=OFF_XBC // 512, ncols=SSD_CONV_DIM, name="ssd_conv_bwd")
    ddt128 = _pad_lanes(ddt_h[:, :, 0].T)
    dadt128 = _pad_lanes(dadt_h[:, :, 0].T)
    d_raw, d_bias, d_alog, d_dskip = _ssd_prep_bwd(ddt128, dadt128, proj, bias128, dt128, a128,
                                                   dd_h.reshape(SSD_HEADS, SSD_P), name="ssd_prep_bwd")
    return (dz, dxbc, d_raw, d_norm_w, d_conv_w, d_conv_b, d_bias[:, :SSD_HEADS], d_alog[:, :SSD_HEADS],
            d_dskip.reshape(1, SSD_HEADS))


def _rope_tables(positions):
    inv_freq = ROPE_THETA ** (-jnp.arange(0, MLA_ROPE, 2, dtype=F32) / MLA_ROPE)
    ang = positions[0].astype(F32)[:, None] * inv_freq
    cos, sin = jnp.cos(ang), jnp.sin(ang)
    z = jnp.zeros_like(cos)
    return jnp.stack([jnp.concatenate([cos, cos, z, z], axis=1), jnp.concatenate([-sin, z, z, z], axis=1),
                      jnp.concatenate([z, sin, z, z], axis=1)])


def _mla_forward(proj, tabs, q_a_norm_w, wq_pad, kv_a_norm_w, wkv):
    qn = _rmsnorm_fwd(proj, q_a_norm_w, width=MLA_Q_RANK, cblk=OFF_QA // MLA_Q_RANK, name="q_a_norm")
    q = _matmul(qn, wq_pad, name="q_b_proj")
    kvn = _rmsnorm_fwd(proj, kv_a_norm_w, width=MLA_KV_RANK, cblk=OFF_CKV // MLA_KV_RANK, name="kv_a_norm")
    kv = _matmul(kvn, wkv, name="kv_b_proj")
    q3, k3, v3 = _mla_prep(q, kv, proj, tabs, name="mla_prep")
    o, lse = _attn_fwd(q3, k3, v3, name="attn_fwd")
    return o, (proj, tabs, q_a_norm_w, wq_pad, kv_a_norm_w, wkv, qn, kvn, q3, k3, v3, o, lse)


def _mla_backward(saved, dcat):
    proj, tabs, q_a_norm_w, wq_pad, kv_a_norm_w, wkv, qn, kvn, q3, k3, v3, o, lse = saved
    dq3, delta = _attn_bwd_dq(q3, k3, v3, o, dcat, lse, name="attn_bwd_dq")
    dk3, dv3 = _attn_bwd_dkv(q3, k3, v3, dcat, lse, delta, name="attn_bwd_dkv")
    dq, dkv, dkr = _mla_unprep(dq3, dk3, dv3, tabs, name="mla_unprep")
    d_wq = _matmul(qn, dq, ta=True, out_dtype=BF16, name="d_w_q_b")
    dqn = _matmul(dq, wq_pad, tb=True, name="d_qn")
    dq_a, d_qnw = _rmsnorm_bwd(proj, q_a_norm_w, dqn, width=MLA_Q_RANK, cblk=OFF_QA // MLA_Q_RANK, out_dtype=BF16,
                               name="q_a_norm_bwd")
    d_wkv = _matmul(kvn, dkv, ta=True, out_dtype=BF16, name="d_w_kv_b")
    dkvn = _matmul(dkv, wkv, tb=True, name="d_kvn")
    dckv, d_kvnw = _rmsnorm_bwd(proj, kv_a_norm_w, dkvn, width=MLA_KV_RANK, cblk=OFF_CKV // MLA_KV_RANK,
                                out_dtype=BF16, name="kv_a_norm_bwd")
    return dq_a, dckv, dkr, d_wq, d_wkv, d_qnw, d_kvnw


def _pad_w_q(w):
    r = w.shape[0]
    w3 = w.reshape(r, MLA_HEADS, MLA_NOPE + MLA_ROPE)
    return jnp.pad(w3, ((0, 0), (0, 0), (0, MLA_QK_PAD - MLA_NOPE - MLA_ROPE))).reshape(r, MLA_HEADS * MLA_QK_PAD)


def _unpad_w_q(w):
    r = w.shape[0]
    return w.reshape(r, MLA_HEADS, MLA_QK_PAD)[:, :, :MLA_NOPE + MLA_ROPE].reshape(r, MLA_HEADS * (MLA_NOPE + MLA_ROPE))


def _pad_w_in(w):
    r = w.shape[0]
    o_dt = D_SSM + SSD_CONV_DIM
    o_qa = o_dt + SSD_HEADS
    o_kr = o_qa + MLA_Q_RANK + MLA_KV_RANK
    zeros = lambda n: jnp.zeros((r, n), w.dtype)
    return jnp.concatenate([w[:, :o_dt], w[:, o_qa:o_kr], w[:, o_kr:], zeros(LANES - MLA_ROPE),
                            w[:, o_dt:o_qa], zeros(LANES - SSD_HEADS)], axis=1)


def _unpad_w_in(w):
    return jnp.concatenate([w[:, :OFF_QA], w[:, OFF_DT:OFF_DT + SSD_HEADS], w[:, OFF_QA:OFF_KR + MLA_ROPE]], axis=1)


WEIGHTS = ['mix_norm_w', 'w_in', 'conv_w', 'conv_b', 'dt_bias', 'a_log', 'd_skip', 'ssd_norm_w', 'q_a_norm_w', 'w_q_b',
           'kv_a_norm_w', 'w_kv_b', 'w_out', 'ffn_norm_w', 'w_ffn_up', 'ffn_conv_w', 'ffn_conv_b', 'w_ffn_down',
           'ple_norm_w', 'w_ple_gate', 'b_ple_gate', 'w_ple_proj', 'ple_post_norm_w', 'final_norm_w']
BIG = ['w_in', 'w_q_b', 'w_kv_b', 'w_out', 'w_ffn_up', 'w_ffn_down', 'w_ple_gate', 'w_ple_proj']
COL_SHARDED = ('w_in', 'w_q_b', 'w_kv_b', 'w_ffn_up', 'w_ple_proj')
CONV = ['conv_w', 'ffn_conv_w']
REPL = [n for n in WEIGHTS if n not in BIG and n not in CONV]
FFN_INV = tuple(int(i) for i in np.argsort(FFN_PERM))


def _cat_cols(g):
    return g.transpose(1, 0, 2).reshape(g.shape[1], N_DEV * g.shape[2])


def _split_cols(w):
    return w.reshape(w.shape[0], N_DEV, w.shape[1] // N_DEV).transpose(1, 0, 2)


def _interleave(v):
    r = v.shape[0]
    return v.reshape(r, N_DEV, FFN_TC)[:, jnp.array(FFN_PERM)].reshape(r, N_DEV * FFN_TC)


def _deinterleave(v):
    r = v.shape[0]
    return v.reshape(r, N_DEV, FFN_TC)[:, jnp.array(FFN_INV)].reshape(r, N_DEV * FFN_TC)


def _assemble_weights(g):
    layout = {
        'w_in': lambda v: _pad_w_in(_cat_cols(v)),
        'w_q_b': lambda v: _pad_w_q(_cat_cols(v)),
        'w_kv_b': _cat_cols,
        'w_out': lambda v: v.reshape(D_MODEL, D_MODEL),
        'w_ffn_up': lambda v: v,
        'w_ffn_down': lambda v: v.reshape(D_FF, D_MODEL),
        'w_ple_gate': lambda v: v.reshape(D_MODEL, D_MODEL),
        'w_ple_proj': _cat_cols,
        'conv_w': _cat_cols,
        'ffn_conv_w': lambda v: _interleave(_cat_cols(v)),
    }
    return {n: layout[n](v) for n, v in g.items()}


WEIGHT_GROUPS = {'a': ['w_in', 'w_q_b', 'w_kv_b', 'conv_w'], 'b': ['w_out'],
                 'c': ['w_ffn_up', 'ffn_conv_w', 'w_ffn_down', 'w_ple_gate', 'w_ple_proj']}
GRAD_GROUPS = {'p': ['w_ple_proj', 'w_ple_gate', 'w_ffn_down'], 'r': ['w_ffn_up'], 's': ['w_out'],
               't': ['w_q_b', 'w_kv_b', 'w_in']}


def _ffn_perm(j):
    return (j % 2) * (N_DEV // 2) + j // 2


def _local_step(x, p, tabs, get_w, s, target, emit, relay):
    t = x.shape[0]
    s = dict(s)
    half = D_MODEL // 2
    up_cols = 2 * D_FF
    ffn_conv_b = _interleave(s['ffn_conv_b'])
    w = dict(get_w('a', None))
    h = _rmsnorm_fwd(x, s['mix_norm_w'], width=D_MODEL, name="mix_norm")
    proj = _matmul(h, w['w_in'], name="in_proj")
    y_ssd, ssd_saved = _ssd_forward(proj, w['conv_w'], s['conv_b'], s['dt_bias'], s['a_log'], s['d_skip'],
                                    s['ssd_norm_w'])
    o, mla_saved = _mla_forward(proj, tabs, s['q_a_norm_w'], w['w_q_b'], s['kv_a_norm_w'], w['w_kv_b'])
    tk_o, tn_o = _tile(half, 512), _tile(D_MODEL, 1024)
    w.update(get_w('b', o))
    x1 = _matmul(y_ssd, w['w_out'], add=x, mnk=(t, D_MODEL, half), name="out_proj_ssd")
    x1 = _matmul(o, w['w_out'], add=x1, mnk=(t, D_MODEL, half), name="out_proj_mla",
                 b_spec=pl.BlockSpec((tk_o, tn_o), lambda i, j, kk: (kk + half // tk_o, j)))
    hf = _rmsnorm_fwd(x1, s['ffn_norm_w'], width=D_MODEL, name="ffn_norm")
    w.update(get_w('c', hf))
    tk_u = _tile(D_MODEL, 512)
    u = _matmul(hf, w['w_ffn_up'], mnk=(t, up_cols, D_MODEL), tn=FFN_TC, name="ffn_up",
                b_spec=pl.BlockSpec((1, tk_u, FFN_TC), lambda i, j, kk: (_ffn_perm(j), kk, 0)))
    act = _conv_act_fwd(u, w['ffn_conv_w'], ffn_conv_b, kw=FFN_CONV, glu=True, tc=2 * FFN_TC, coff=0, ncols=up_cols,
                        out_dtype=BF16, name="ffn_act")
    x2 = _matmul(act, w['w_ffn_down'], add=x1, name="ffn_down")
    hp = _rmsnorm_fwd(x2, s['ple_norm_w'], width=D_MODEL, name="ple_norm")
    gl = _matmul(hp, w['w_ple_gate'], bias=s['b_ple_gate'], name="ple_gate")
    pe = _matmul(p, w['w_ple_proj'], name="ple_proj")
    x3 = _ple_fwd(x2, gl, pe, s['ple_post_norm_w'], name="ple_mix")
    loss, dx3, d_final = _loss_head(x3, s['final_norm_w'], target, name="loss_head")
    dgl, d_bgate, dpe, d_post = _ple_bwd(dx3, gl, pe, s['ple_post_norm_w'], name="ple_mix_bwd")
    d_wproj = _matmul(p, dpe, ta=True, out_dtype=BF16, name="d_w_ple_proj")
    d_wgate = _matmul(hp, dgl, ta=True, out_dtype=BF16, name="d_w_ple_gate")
    dhp = _matmul(dgl, w['w_ple_gate'], tb=True, name="d_ple_normed")
    dx2, d_plenorm = _rmsnorm_bwd(x2, s['ple_norm_w'], dhp, dx3, width=D_MODEL, name="ple_norm_bwd")
    dact = _matmul(dx2, w['w_ffn_down'], tb=True, name="d_ffn_act")
    d_wdown = _matmul(act, dx2, ta=True, out_dtype=BF16, name="d_w_ffn_down")
    zz = emit('p', {'w_ple_proj': _split_cols(d_wproj), 'w_ple_gate': d_wgate.reshape(N_DEV, D_MODEL // N_DEV, D_MODEL),
                    'w_ffn_down': d_wdown.reshape(N_DEV, D_FF // N_DEV, D_MODEL)})
    du, d_fconv_w, d_fconv_b = _conv_act_bwd(u, w['ffn_conv_w'], ffn_conv_b + zz, dact, kw=FFN_CONV, glu=True,
                                             tc=2 * FFN_TC, coff=0, ncols=up_cols, name="ffn_act_bwd")
    zz = zz + relay('p', du)
    tm_u = _tile(D_MODEL, 1024)
    d_wup = _matmul(hf, du, ta=True, out_dtype=BF16, mnk=(D_MODEL, up_cols, t), tn=FFN_TC, name="d_w_ffn_up",
                    o_spec=pl.BlockSpec((1, tm_u, FFN_TC), lambda i, j, kk: (_ffn_perm(j), i, 0)),
                    o_shape=(N_DEV, D_MODEL, FFN_TC))
    zz = zz + emit('r', {'w_ffn_up': d_wup})
    dhf = _matmul(du, w['w_ffn_up'], tb=True, mnk=(t, D_MODEL, up_cols), tk=FFN_TC, name="d_ffn_normed",
                  b_spec=pl.BlockSpec((1, tn_o, FFN_TC), lambda i, j, kk: (_ffn_perm(kk), j, 0)))
    zz = zz + relay('r', dhf)
    dx1, d_ffnnorm = _rmsnorm_bwd(x1, s['ffn_norm_w'] + zz, dhf, dx2, width=D_MODEL, name="ffn_norm_bwd")
    dcat = _matmul(dx1, w['w_out'], tb=True, name="d_mixed")
    d_wout = jnp.concatenate([_matmul(y_ssd, dx1, ta=True, out_dtype=BF16, name="d_w_out_ssd"),
                              _matmul(o, dx1, ta=True, out_dtype=BF16, name="d_w_out_mla")], axis=0)
    zz = zz + emit('s', {'w_out': d_wout.reshape(N_DEV, D_MODEL // N_DEV, D_MODEL)})
    ssd_saved = ssd_saved[:3] + (ssd_saved[3] + zz,) + ssd_saved[4:]
    dz, dxbc, d_raw, d_ssdnorm, d_conv_w, d_conv_b, d_dtb, d_alog, d_dskip = _ssd_backward(ssd_saved, dcat)
    zz = zz + relay('s', dz)
    mla_saved = mla_saved[:-1] + (mla_saved[-1] + zz,)
    dq_a, dckv, dkr, d_wq, d_wkv, d_qnorm, d_kvnorm = _mla_backward(mla_saved, dcat)
    dproj = jnp.concatenate([dz, dxbc, dq_a, dckv, dkr, d_raw], axis=1)
    d_win = _matmul(h, dproj, ta=True, out_dtype=BF16, name="d_w_in")
    dh = _matmul(dproj, w['w_in'], tb=True, name="d_in_normed")
    dx, d_mixnorm = _rmsnorm_bwd(x, s['mix_norm_w'], dh, dx1, width=D_MODEL, name="mix_norm_bwd")
    emit('t', {'w_in': _split_cols(_unpad_w_in(d_win)), 'w_q_b': _split_cols(_unpad_w_q(d_wq)),
               'w_kv_b': _split_cols(d_wkv)})
    relay('t', dx)
    conv = {'conv_w': d_conv_w, 'ffn_conv_w': _deinterleave(d_fconv_w)}
    vec = {
        'mix_norm_w': d_mixnorm, 'conv_b': d_conv_b, 'dt_bias': d_dtb, 'a_log': d_alog, 'd_skip': d_dskip,
        'ssd_norm_w': d_ssdnorm, 'q_a_norm_w': d_qnorm, 'kv_a_norm_w': d_kvnorm, 'ffn_norm_w': d_ffnnorm,
        'ffn_conv_b': _deinterleave(d_fconv_b), 'ple_norm_w': d_plenorm, 'b_ple_gate': d_bgate,
        'ple_post_norm_w': d_post, 'final_norm_w': d_final,
    }
    return loss, dx, conv, vec


MESH = pl.DeviceIdType.MESH
FLIPS = ((0, 0, 1), (1, 0, 0), (0, 1, 0), (1, 1, 0), (1, 0, 1), (0, 1, 1), (1, 1, 1))


def _exchange(items, *, gather, name):
    n = len(items)

    def body(*refs):
        ins, outs = refs[:n], refs[n:2 * n]
        send_sems, recv_sems, local_sems = refs[2 * n:]
        x, y, c = lax.axis_index("x"), lax.axis_index("y"), lax.axis_index("c")
        me = 4 * x + 2 * y + c
        peers = [(jnp.where(fx, 1 - x, x), jnp.where(fy, 1 - y, y), jnp.where(fc, 1 - c, c)) for fx, fy, fc in FLIPS]
        slot = [4 * px + 2 * py + pc for px, py, pc in peers]
        local, sends = [], []
        for wi in range(n):
            cp = pltpu.make_async_copy(ins[wi] if gather else ins[wi].at[me], outs[wi].at[me], local_sems.at[wi])
            cp.start()
            local.append(cp)
            for k, peer in enumerate(peers):
                cp = pltpu.make_async_remote_copy(
                    src_ref=ins[wi] if gather else ins[wi].at[slot[k]], dst_ref=outs[wi].at[me],
                    send_sem=send_sems.at[k, wi], recv_sem=recv_sems.at[k, wi], device_id=peer, device_id_type=MESH)
                cp.start()
                sends.append(cp)
        for wi in range(n):
            for k, peer in enumerate(peers):
                pltpu.make_async_remote_copy(
                    src_ref=outs[wi].at[slot[k]], dst_ref=outs[wi].at[slot[k]], send_sem=send_sems.at[k, wi],
                    recv_sem=recv_sems.at[k, wi], device_id=peer, device_id_type=MESH).wait_recv()
        for cp in sends:
            cp.wait_send()
        for cp in local:
            cp.wait()

    hbm = pl.BlockSpec(memory_space=pltpu.HBM)
    out_shape = [jax.ShapeDtypeStruct(((N_DEV,) + v.shape) if gather else v.shape, v.dtype) for v in items]
    return pl.pallas_call(
        body, name=name, in_specs=[hbm] * n, out_specs=[hbm] * n, out_shape=out_shape,
        scratch_shapes=[pltpu.SemaphoreType.DMA((len(FLIPS), n)), pltpu.SemaphoreType.DMA((len(FLIPS), n)),
                        pltpu.SemaphoreType.DMA((n,))],
    )(*items)


HBM_SPEC = pl.BlockSpec(memory_space=pltpu.HBM)
SEM_SPEC = pl.BlockSpec(memory_space=pltpu.SEMAPHORE)
EFFECT = pltpu.SideEffectType.DATAFLOW_SIDE_EFFECTING


def _peers():
    x, y, c = lax.axis_index("x"), lax.axis_index("y"), lax.axis_index("c")
    peers = [(jnp.where(fx, 1 - x, x), jnp.where(fy, 1 - y, y), jnp.where(fc, 1 - c, c)) for fx, fy, fc in FLIPS]
    return 4 * x + 2 * y + c, peers, [4 * px + 2 * py + pc for px, py, pc in peers]


def _split_start(bufs, ncopies, plan, *, name):
    nb = len(bufs)

    def body(*refs):
        send_sems, recv_sems, token = refs[nb], refs[nb + 1], refs[2 * nb + 2]
        for i, (src, dst, peer, _) in enumerate(plan(refs[:nb])):
            pltpu.make_async_remote_copy(src_ref=src, dst_ref=dst, send_sem=send_sems.at[i], recv_sem=recv_sems.at[i],
                                         device_id=peer, device_id_type=MESH).start()
        token[...] = jnp.zeros_like(token)

    res = pl.pallas_call(
        body, name=name, in_specs=[HBM_SPEC] * nb,
        out_specs=[SEM_SPEC, SEM_SPEC] + [HBM_SPEC] * nb + [pl.BlockSpec(memory_space=pltpu.VMEM)],
        out_shape=[pltpu.SemaphoreType.DMA((ncopies,)), pltpu.SemaphoreType.DMA((ncopies,))]
        + [pltpu.HBM(v.shape, v.dtype) for v in bufs] + [jax.ShapeDtypeStruct((HALO, LANES), F32)],
        input_output_aliases={i: 2 + i for i in range(nb)},
        compiler_params=pltpu.CompilerParams(has_side_effects=EFFECT),
    )(*[pltpu.with_memory_space_constraint(v, pltpu.HBM) for v in bufs])
    return (res[0], res[1], list(res[2:2 + nb])), res[2 + nb]


def _split_wait(started, after, plan, local_plan, *, name):
    send_sems, recv_sems, bufs = started
    nb = len(bufs)
    nlocal = len(local_plan(bufs))

    def body(*refs):
        send_sems, recv_sems = refs[nb], refs[nb + 1]
        local_sems = refs[2 * nb + 3]
        local = []
        for j, (src, dst) in enumerate(local_plan(refs[:nb])):
            cp = pltpu.make_async_copy(src, dst, local_sems.at[j])
            cp.start()
            local.append(cp)
        for i, (src, _, peer, incoming) in enumerate(plan(refs[:nb])):
            cp = pltpu.make_async_remote_copy(src_ref=src, dst_ref=incoming, send_sem=send_sems.at[i],
                                              recv_sem=recv_sems.at[i], device_id=peer, device_id_type=MESH)
            cp.wait_send()
            cp.wait_recv()
        for cp in local:
            cp.wait()

    res = pl.pallas_call(
        body, name=name, in_specs=[HBM_SPEC] * nb + [SEM_SPEC, SEM_SPEC, pl.BlockSpec(memory_space=pl.ANY)],
        out_specs=[HBM_SPEC] * nb, out_shape=[pltpu.HBM(v.shape, v.dtype) for v in bufs],
        input_output_aliases={i: i for i in range(nb)},
        scratch_shapes=[pltpu.SemaphoreType.DMA((max(nlocal, 1),))],
        compiler_params=pltpu.CompilerParams(has_side_effects=EFFECT),
    )(*bufs, send_sems, recv_sems, after)
    return list(res)


def _place():
    x, y, c = lax.axis_index("x"), lax.axis_index("y"), lax.axis_index("c")
    others = [((1 - x, y, c), 2 * (1 - x) + y), ((x, 1 - y, c), 2 * x + 1 - y), ((1 - x, 1 - y, c), 2 * (1 - x) + 1 - y)]
    return 4 * x + 2 * y + c, 2 * x + y, c, (x, y, 1 - c), others


def _gather1_plan(n):
    def plan(refs):
        me, _, _, sibling, others = _place()
        out = []
        for wi in range(n):
            item, land = refs[wi], refs[n + wi]
            out.append((item, land.at[me], sibling, land.at[me + 1 - 2 * lax.axis_index("c")]))
            for peer, chip in others:
                out.append((item, land.at[me], peer, land.at[2 * chip + lax.axis_index("c")]))
        return out

    return plan


def _gather1_local(n):
    def plan(refs):
        me = _place()[0]
        return [(refs[wi], refs[n + wi].at[me]) for wi in range(n)]

    return plan


def _gather2_plan(n):
    def plan(refs):
        _, _, c, sibling, others = _place()
        out = []
        for wi in range(n):
            land = refs[wi]
            for _, chip in others:
                out.append((land.at[2 * chip + c], land.at[2 * chip + c], sibling, land.at[2 * chip + 1 - c]))
        return out

    return plan


def _gather_start(items, *, name):
    lands = [lax.empty((N_DEV,) + v.shape, v.dtype) for v in items]
    return _split_start(items + lands, 4 * len(items), _gather1_plan(len(items)), name=name)


def _gather_forward(started, after, *, name):
    n = len(started[2]) // 2
    bufs = _split_wait(started, after, _gather1_plan(n), _gather1_local(n), name=name + "_wait")
    return _split_start(bufs[n:], 3 * n, _gather2_plan(n), name=name + "_start")


def _gather_finish(started, after, *, name):
    n = len(started[2])
    return _split_wait(started, after, _gather2_plan(n), lambda refs: [], name=name)


N_CHIP = N_DEV // 2


def _scatter1_plan(n):
    def plan(refs):
        _, _, c, sibling, _ = _place()
        out = []
        for wi in range(n):
            parts, half = refs[wi], refs[n + wi]
            for chip in range(N_CHIP):
                out.append((parts.at[2 * chip + 1 - c], half.at[chip], sibling, half.at[chip]))
        return out

    return plan


def _scatter2_plan(n):
    def plan(refs):
        _, my_chip, _, _, others = _place()
        out = []
        for wi in range(n):
            sums, recv = refs[wi], refs[n + wi]
            for peer, chip in others:
                out.append((sums.at[chip], recv.at[my_chip], peer, recv.at[chip]))
        return out

    return plan


def _scatter2_local(n):
    def plan(refs):
        my_chip = _place()[1]
        return [(refs[wi].at[my_chip], refs[n + wi].at[my_chip]) for wi in range(n)]

    return plan


def _pair_add(parts, half, core, *, name):
    _, r, c = parts.shape
    tr = max(d for d in range(HALO, 257, HALO) if r % d == 0) if r > 256 else r
    parts4 = parts.reshape(N_CHIP, 2, r, c)

    def body(core_ref, p_ref, h_ref, o_ref):
        o_ref[...] = (p_ref[:, 0].astype(F32) + h_ref[...].astype(F32)).astype(o_ref.dtype)

    return pl.pallas_call(
        body, name=name,
        grid_spec=pltpu.PrefetchScalarGridSpec(
            num_scalar_prefetch=1, grid=(r // tr,),
            in_specs=[pl.BlockSpec((N_CHIP, 1, tr, c), lambda i, core_ref: (0, core_ref[0], i, 0)),
                      pl.BlockSpec((N_CHIP, tr, c), lambda i, core_ref: (0, i, 0))],
            out_specs=pl.BlockSpec((N_CHIP, tr, c), lambda i, core_ref: (0, i, 0))),
        out_shape=jax.ShapeDtypeStruct((N_CHIP, r, c), parts.dtype), compiler_params=_cp("parallel"),
    )(core, parts4, half)


def _scatter_start(parts, *, name):
    halves = [lax.empty((N_CHIP,) + v.shape[1:], v.dtype) for v in parts]
    return _split_start(parts + halves, N_CHIP * len(parts), _scatter1_plan(len(parts)), name=name)


def _scatter_forward(started, after, core, *, name):
    n = len(started[2]) // 2
    bufs = _split_wait(started, after, _scatter1_plan(n), lambda refs: [], name=name + "_wait")
    sums = [_pair_add(bufs[wi], bufs[n + wi], core, name=name + "_add%d" % wi) for wi in range(n)]
    recvs = [lax.empty(v.shape, v.dtype) for v in sums]
    return _split_start(sums + recvs, 3 * n, _scatter2_plan(n), name=name + "_start")


def _scatter_finish(started, after, *, name):
    n = len(started[2]) // 2
    return _split_wait(started, after, _scatter2_plan(n), _scatter2_local(n), name=name)[n:]


def _adamw(parts, w, m, v, *, name):
    r, c = w.shape
    nparts = parts.shape[0]
    tr = max(d for d in range(HALO, 129, HALO) if r % d == 0) if r > 128 else r

    def body(p_ref, w_ref, m_ref, v_ref, g_ref, d_ref, mo_ref, vo_ref):
        g = p_ref[0].astype(F32)
        for k in range(1, nparts):
            g = g + p_ref[k].astype(F32)
        mn = ADAM_B1 * m_ref[...] + (1.0 - ADAM_B1) * g
        vn = ADAM_B2 * v_ref[...] + (1.0 - ADAM_B2) * (g * g)
        m_hat = mn / (1.0 - ADAM_B1 ** ADAM_STEP)
        v_hat = vn / (1.0 - ADAM_B2 ** ADAM_STEP)
        g_ref[...] = g
        d_ref[...] = -ADAM_LR * (m_hat / (jnp.sqrt(v_hat) + ADAM_EPS) + ADAM_WD * w_ref[...])
        mo_ref[...] = mn
        vo_ref[...] = vn

    blk = pl.BlockSpec((tr, c), lambda i: (i, 0))
    return pl.pallas_call(
        body, name=name, grid=(r // tr,), in_specs=[pl.BlockSpec((nparts, tr, c), lambda i: (0, i, 0)), blk, blk, blk],
        out_specs=[blk] * 4, out_shape=[jax.ShapeDtypeStruct((r, c), F32)] * 4, compiler_params=_cp("parallel"),
    )(parts, w, m, v)


def _pack_rows(vs, rows):
    lead = vs[0].shape[:-1] if vs[0].ndim > 1 else ()
    flat = jnp.concatenate(vs, axis=-1)
    pad = rows * LANES - flat.shape[-1]
    flat = jnp.pad(flat, [(0, 0)] * len(lead) + [(0, pad)])
    return flat.reshape(lead + (rows, LANES))


def kernel(x, p, positions, mix_norm_w, w_in, conv_w, conv_b, dt_bias, a_log, d_skip, ssd_norm_w, q_a_norm_w, w_q_b, kv_a_norm_w, w_kv_b, w_out, ffn_norm_w, w_ffn_up, ffn_conv_w, ffn_conv_b, w_ffn_down, ple_norm_w, w_ple_gate, b_ple_gate, w_ple_proj, ple_post_norm_w, final_norm_w, loss_target, m_mix_norm_w, m_w_in, m_conv_w, m_conv_b, m_dt_bias, m_a_log, m_d_skip, m_ssd_norm_w, m_q_a_norm_w, m_w_q_b, m_kv_a_norm_w, m_w_kv_b, m_w_out, m_ffn_norm_w, m_w_ffn_up, m_ffn_conv_w, m_ffn_conv_b, m_w_ffn_down, m_ple_norm_w, m_w_ple_gate, m_b_ple_gate, m_w_ple_proj, m_ple_post_norm_w, m_final_norm_w, v_mix_norm_w, v_w_in, v_conv_w, v_conv_b, v_dt_bias, v_a_log, v_d_skip, v_ssd_norm_w, v_q_a_norm_w, v_w_q_b, v_kv_a_norm_w, v_w_kv_b, v_w_out, v_ffn_norm_w, v_w_ffn_up, v_ffn_conv_w, v_ffn_conv_b, v_w_ffn_down, v_ple_norm_w, v_w_ple_gate, v_b_ple_gate, v_w_ple_proj, v_ple_post_norm_w, v_final_norm_w):
    given = dict(locals())
    shapes = {n: given[n].shape for n in WEIGHTS}
    w2 = {n: given[n].reshape(given[n].shape[-2:] if n in BIG or n in CONV else (1, -1)) for n in WEIGHTS}
    m2 = {n: given['m_' + n].reshape(w2[n].shape) for n in WEIGHTS}
    v2 = {n: given['v_' + n].reshape(w2[n].shape) for n in WEIGHTS}
    me = 4 * lax.axis_index("x") + 2 * lax.axis_index("y") + lax.axis_index("c")

    core = lax.axis_index("c").astype(jnp.int32).reshape(1)

    def shards(grp, zero):
        return [(w2[n] + zero).astype(BF16) if n in BIG else w2[n] + zero for n in WEIGHT_GROUPS[grp]]

    first, token = _gather_start(shards('a', 0.0), name="gather_a_hop1")
    first, token = _gather_forward(first, token, name="gather_a_hop2")
    zero = token[0, 0]
    gathers = {}
    for grp in ('b', 'c'):
        gathers[grp], token = _gather_start(shards(grp, zero), name="gather_" + grp + "_hop1")
        zero = zero + token[0, 0]

    def get_w(grp, after):
        if grp == 'a':
            lands = _gather_finish(first, token, name="gather_a_done")
        else:
            if grp == 'b':
                for g in ('b', 'c'):
                    gathers[g], after = _gather_forward(gathers[g], after, name="gather_" + g + "_hop2")
            lands = _gather_finish(gathers[grp], after, name="gather_" + grp + "_done")
        return _assemble_weights(dict(zip(WEIGHT_GROUPS[grp], lands)))

    scatters = {}

    def emit(grp, grads):
        scatters[grp], tok = _scatter_start([grads[n] for n in GRAD_GROUPS[grp]], name="scatter_" + grp + "_hop1")
        return tok[0, 0]

    def relay(grp, after):
        scatters[grp], tok = _scatter_forward(scatters[grp], after, core, name="scatter_" + grp + "_hop2")
        return tok[0, 0]

    vecs = {n: w2[n] for n in REPL}
    vecs['mix_norm_w'] = vecs['mix_norm_w'] + zero
    loss, dx, g_conv, g_vec = _local_step(x[0], p[0, 0], _rope_tables(positions), get_w, vecs, loss_target[0], emit,
                                          relay)
    n_small = sum(g_vec[n].shape[1] for n in REPL) + sum(g_conv[n].size for n in CONV) + 1
    rows_small = -(-n_small // (LANES * HALO)) * HALO
    small = _pack_rows([g_vec[n] for n in REPL] + [g_conv[n].reshape(1, -1) for n in CONV] + [loss], rows_small)
    all_small = _exchange([small], gather=True, name="gather_small_grads")[0].reshape(N_DEV, rows_small * LANES)

    out_g, out_d, out_m, out_v = {}, {}, {}, {}
    for grp, names in GRAD_GROUPS.items():
        received = _scatter_finish(scatters[grp], dx, name="scatter_" + grp + "_done")
        for n, parts in zip(names, received):
            out_g[n], out_d[n], out_m[n], out_v[n] = _adamw(parts, w2[n], m2[n], v2[n], name="adamw_" + n)
    pieces, off = [], 0
    for n in REPL:
        k = g_vec[n].shape[1]
        pieces.append(all_small[:, off:off + k])
        off += k
    for n in CONV:
        kw, cols = g_conv[n].shape
        full = all_small[:, off:off + kw * cols].reshape(N_DEV, kw, cols)
        mine = lax.dynamic_slice_in_dim(full, me * (cols // N_DEV), cols // N_DEV, axis=2)
        pieces.append(mine.reshape(N_DEV, kw * (cols // N_DEV)))
        off += kw * cols
    pieces.append(all_small[:, off:off + 1])
    small_names = REPL + CONV
    n_mine = sum(q.shape[1] for q in pieces)
    rows_mine = -(-n_mine // (LANES * HALO)) * HALO
    zero = jnp.zeros((1, 1), F32)
    packed = [_pack_rows([src[n].reshape(1, -1) for n in small_names] + [zero], rows_mine).reshape(rows_mine, LANES)
              for src in (w2, m2, v2)]
    sg, sd, sm, sv = _adamw(_pack_rows(pieces, rows_mine), *packed, name="adamw_small")
    off = 0
    for n in small_names:
        k = w2[n].size
        for dst, src in ((out_g, sg), (out_d, sd), (out_m, sm), (out_v, sv)):
            dst[n] = src.reshape(-1)[off:off + k].reshape(w2[n].shape)
        off += k
    total_loss = sg.reshape(-1)[off]

    outs = [total_loss, dx[None]]
    for res in (out_g, out_d, out_m, out_v):
        outs += [res[n].reshape(shapes[n]) for n in WEIGHTS]
    return tuple(outs)
```

```python
import functools
import math

import numpy as np
import jax
import jax.numpy as jnp
from jax import lax
from jax.experimental import pallas as pl
from jax.experimental.pallas import tpu as pltpu

F32 = jnp.float32
BF16 = jnp.bfloat16
HI = lax.Precision.HIGHEST

D_MODEL = 2048
CHUNK = 64
D_SSM = 1024
SSD_P = 64
SSD_HEADS = 16
SSD_GROUPS = 2
SSD_N = 128
SSD_CONV = 4
SSD_CONV_DIM = D_SSM + 2 * SSD_GROUPS * SSD_N
MLA_HEADS = 8
MLA_NOPE = 128
MLA_ROPE = 64
MLA_V = 128
MLA_Q_RANK = 512
MLA_KV_RANK = 256
MLA_QK_PAD = 256
ROPE_THETA = 10000.0
D_FF = 5632
FFN_CONV = 3
PLE_DIM = 256
NORM_EPS = 1e-6
ADAM_LR, ADAM_B1, ADAM_B2, ADAM_EPS, ADAM_WD, ADAM_STEP = 0.001, 0.9, 0.999, 1e-08, 0.01, 10
N_DEV = 8

OFF_Z, OFF_XBC, OFF_QA, OFF_CKV, OFF_KR, OFF_DT, D_IN_PAD = 0, 1024, 2560, 3072, 3328, 3456, 3584
D_IN = 3408
LANES = 128
HALO = 8
VMEM_LIMIT = 56 * 1024 * 1024
FFN_TC = D_FF * 2 // N_DEV
FFN_PERM = (0, 4, 1, 5, 2, 6, 3, 7)
NEG = -1e30


def _cp(*sem):
    return pltpu.CompilerParams(dimension_semantics=tuple(sem), vmem_limit_bytes=VMEM_LIMIT)


def _tile(n, want):
    if n <= want:
        return n
    best = max(d for d in range(LANES, want + 1, LANES) if n % d == 0)
    return best


def _sigmoid(x):
    return 1.0 / (1.0 + jnp.exp(-x))


def _silu(x):
    return x * _sigmoid(x)


def _dsilu(x):
    s = _sigmoid(x)
    return s * (1.0 + x * (1.0 - s))


MM_TILE = 1408
MM_TK = 2816


def _matmul(a, b, *, ta=False, tb=False, out_dtype=F32, add=None, bias=None, tm=MM_TILE, tn=MM_TILE, tk=MM_TK, name,
            mnk=None, a_spec=None, b_spec=None, o_spec=None, o_shape=None):
    if mnk is None:
        m, k = (a.shape[1], a.shape[0]) if ta else a.shape
        n = b.shape[0] if tb else b.shape[1]
        assert k == (b.shape[1] if tb else b.shape[0])
    else:
        m, n, k = mnk
    tm, tn, tk = _tile(m, tm), _tile(n, tn), _tile(k, tk)
    nk = k // tk
    dims = (((0 if ta else 1,), (1 if tb else 0,)), ((), ()))

    def body(*refs):
        a_ref, b_ref = refs[0], refs[1]
        pos = 2
        add_ref = bias_ref = None
        if add is not None:
            add_ref = refs[pos]
            pos += 1
        if bias is not None:
            bias_ref = refs[pos]
            pos += 1
        o_ref = refs[pos]
        kk = pl.program_id(2)
        av = a_ref[...]
        bv = b_ref[...]
        av = av.reshape(av.shape[-2:]).astype(BF16)
        bv = bv.reshape(bv.shape[-2:]).astype(BF16)
        prod = lax.dot_general(av, bv, dims, preferred_element_type=F32)

        def finish(r):
            if bias_ref is not None:
                r = r + bias_ref[...]
            if add_ref is not None:
                r = r + add_ref[...].astype(F32)
            o_ref[...] = r.astype(out_dtype).reshape(o_ref.shape)

        if nk == 1:
            finish(prod)
        else:
            acc_ref = refs[pos + 1]

            @pl.when(kk == 0)
            def _():
                acc_ref[...] = prod

            @pl.when(kk > 0)
            def _():
                acc_ref[...] += prod

            @pl.when(kk == nk - 1)
            def _():
                finish(acc_ref[...])

    if a_spec is None:
        a_spec = (pl.BlockSpec((tk, tm), lambda i, j, kk: (kk, i)) if ta
                  else pl.BlockSpec((tm, tk), lambda i, j, kk: (i, kk)))
    if b_spec is None:
        b_spec = (pl.BlockSpec((tn, tk), lambda i, j, kk: (j, kk)) if tb
                  else pl.BlockSpec((tk, tn), lambda i, j, kk: (kk, j)))
    if o_spec is None:
        o_spec = pl.BlockSpec((tm, tn), lambda i, j, kk: (i, j))
    if o_shape is None:
        o_shape = (m, n)
    in_specs = [a_spec, b_spec]
    args = [a, b]
    if add is not None:
        in_specs.append(pl.BlockSpec((tm, tn), lambda i, j, kk: (i, j)))
        args.append(add)
    if bias is not None:
        in_specs.append(pl.BlockSpec((1, tn), lambda i, j, kk: (0, j)))
        args.append(bias)
    return pl.pallas_call(
        body, name=name, grid=(m // tm, n // tn, nk), in_specs=in_specs, out_specs=o_spec,
        out_shape=jax.ShapeDtypeStruct(o_shape, out_dtype),
        scratch_shapes=[pltpu.VMEM((tm, tn), F32)] if nk > 1 else [],
        compiler_params=_cp("parallel", "parallel", "arbitrary"),
    )(*args)


def _rmsnorm_fwd(x, w, *, width, cblk=0, out_dtype=BF16, tr=256, name):
    t = x.shape[0]

    def body(x_ref, w_ref, o_ref):
        xv = x_ref[...].astype(F32)
        r = lax.rsqrt(jnp.mean(xv * xv, axis=-1, keepdims=True) + NORM_EPS)
        o_ref[...] = (xv * r * w_ref[...]).astype(out_dtype)

    return pl.pallas_call(
        body, name=name, grid=(t // tr,),
        in_specs=[pl.BlockSpec((tr, width), lambda i: (i, cblk)), pl.BlockSpec((1, width), lambda i: (0, 0))],
        out_specs=pl.BlockSpec((tr, width), lambda i: (i, 0)),
        out_shape=jax.ShapeDtypeStruct((t, width), out_dtype),
        compiler_params=_cp("parallel"),
    )(x, w)


def _rmsnorm_bwd(x, w, dy, add=None, *, width, cblk=0, out_dtype=F32, tr=256, name):
    t = x.shape[0]

    def body(*refs):
        if add is None:
            x_ref, w_ref, dy_ref, dx_ref, dw_ref = refs
            add_ref = None
        else:
            x_ref, w_ref, dy_ref, add_ref, dx_ref, dw_ref = refs
        xv = x_ref[...].astype(F32)
        dyv = dy_ref[...].astype(F32)
        r = lax.rsqrt(jnp.mean(xv * xv, axis=-1, keepdims=True) + NORM_EPS)
        xh = xv * r
        g = dyv * w_ref[...]
        dx = r * (g - xh * jnp.mean(g * xh, axis=-1, keepdims=True))
        if add_ref is not None:
            dx = dx + add_ref[...].astype(F32)
        dx_ref[...] = dx.astype(out_dtype)

        @pl.when(pl.program_id(0) == 0)
        def _():
            dw_ref[...] = jnp.zeros_like(dw_ref)

        dw_ref[...] += jnp.sum(dyv * xh, axis=0, keepdims=True)

    in_specs = [pl.BlockSpec((tr, width), lambda i: (i, cblk)), pl.BlockSpec((1, width), lambda i: (0, 0)),
                pl.BlockSpec((tr, width), lambda i: (i, 0))]
    args = [x, w, dy]
    if add is not None:
        in_specs.append(pl.BlockSpec((tr, width), lambda i: (i, 0)))
        args.append(add)
    return pl.pallas_call(
        body, name=name, grid=(t // tr,), in_specs=in_specs,
        out_specs=[pl.BlockSpec((tr, width), lambda i: (i, 0)), pl.BlockSpec((1, width), lambda i: (0, 0))],
        out_shape=[jax.ShapeDtypeStruct((t, width), out_dtype), jax.ShapeDtypeStruct((1, width), F32)],
        compiler_params=_cp("arbitrary"),
    )(*args)


def _shift_down(prev_halo, cur, j):
    if j == 0:
        return cur
    ext = jnp.concatenate([prev_halo, cur], axis=0)
    return pltpu.roll(ext, j, axis=0)[HALO:]


def _shift_up(cur, next_halo, j):
    if j == 0:
        return cur
    ext = jnp.concatenate([cur, next_halo], axis=0)
    return pltpu.roll(ext, ext.shape[0] - j, axis=0)[:cur.shape[0]]


def _conv_rows(prev, cur, w, b, kw):
    shifted = [cur]
    out = b + w[kw - 1:kw] * cur
    for j in range(1, kw):
        sh = _shift_down(prev, cur, j)
        shifted.append(sh)
        out = out + w[kw - 1 - j:kw - j] * sh
    return out, shifted


def _act_fwd(c, glu):
    if glu:
        half = c.shape[1] // 2
        return _silu(c[:, :half]) * c[:, half:]
    return _silu(c)


def _act_bwd(c, dout, glu):
    if glu:
        half = c.shape[1] // 2
        g, up = c[:, :half], c[:, half:]
        return jnp.concatenate([dout * up * _dsilu(g), dout * _silu(g)], axis=1)
    return dout * _dsilu(c)


def _conv_act_fwd(u, w, b, *, kw, glu, tc, coff, ncols, out_dtype, tr=256, name):
    t = u.shape[0]
    nb = ncols // tc
    oc = tc // 2 if glu else tc

    def body(u_ref, uh_ref, w_ref, b_ref, o_ref):
        prev = jnp.where(pl.program_id(0) == 0, 0.0, uh_ref[...])
        c, _ = _conv_rows(prev, u_ref[...], w_ref[...], b_ref[...], kw)
        o_ref[...] = _act_fwd(c, glu).astype(out_dtype)

    return pl.pallas_call(
        body, name=name, grid=(t // tr, nb),
        in_specs=[pl.BlockSpec((tr, tc), lambda i, j: (i, j + coff)),
                  pl.BlockSpec((HALO, tc), lambda i, j: (jnp.maximum(i * (tr // HALO) - 1, 0), j + coff)),
                  pl.BlockSpec((kw, tc), lambda i, j: (0, j)), pl.BlockSpec((1, tc), lambda i, j: (0, j))],
        out_specs=pl.BlockSpec((tr, oc), lambda i, j: (i, j)),
        out_shape=jax.ShapeDtypeStruct((t, nb * oc), out_dtype),
        compiler_params=_cp("parallel", "parallel"),
    )(u, u, w, b)


def _conv_act_bwd(u, w, b, dout, *, kw, glu, tc, coff, ncols, tr=256, name):
    t = u.shape[0]
    nb = ncols // tc
    nt = t // tr
    oc = tc // 2 if glu else tc

    def body(u_ref, up_ref, un_ref, d_ref, dn_ref, w_ref, b_ref, du_ref, dw_ref, db_ref):
        i = pl.program_id(1)
        cur, nxt, wv, bv = u_ref[...], un_ref[...], w_ref[...], b_ref[...]
        prev = jnp.where(i == 0, 0.0, up_ref[...])
        c_cur, shifted = _conv_rows(prev, cur, wv, bv, kw)
        c_nxt, _ = _conv_rows(cur[tr - HALO:], nxt, wv, bv, kw)
        d_cur = _act_bwd(c_cur, d_ref[...].astype(F32), glu)
        d_nxt = _act_bwd(c_nxt, jnp.where(i == nt - 1, 0.0, dn_ref[...].astype(F32)), glu)
        du = wv[kw - 1:kw] * d_cur
        for j in range(1, kw):
            du = du + wv[kw - 1 - j:kw - j] * _shift_up(d_cur, d_nxt, j)
        du_ref[...] = du.astype(BF16)

        @pl.when(i == 0)
        def _():
            dw_ref[...] = jnp.zeros_like(dw_ref)
            db_ref[...] = jnp.zeros_like(db_ref)

        db_ref[...] += jnp.sum(d_cur, axis=0, keepdims=True)
        dw_ref[...] += jnp.concatenate(
            [jnp.sum(d_cur * shifted[kw - 1 - k], axis=0, keepdims=True) for k in range(kw)], axis=0)

    nh = tr // HALO
    return pl.pallas_call(
        body, name=name, grid=(nb, nt),
        in_specs=[pl.BlockSpec((tr, tc), lambda j, i: (i, j + coff)),
                  pl.BlockSpec((HALO, tc), lambda j, i: (jnp.maximum(i * nh - 1, 0), j + coff)),
                  pl.BlockSpec((HALO, tc), lambda j, i: (jnp.minimum((i + 1) * nh, t // HALO - 1), j + coff)),
                  pl.BlockSpec((tr, oc), lambda j, i: (i, j)),
                  pl.BlockSpec((HALO, oc), lambda j, i: (jnp.minimum((i + 1) * nh, t // HALO - 1), j)),
                  pl.BlockSpec((kw, tc), lambda j, i: (0, j)), pl.BlockSpec((1, tc), lambda j, i: (0, j))],
        out_specs=[pl.BlockSpec((tr, tc), lambda j, i: (i, j)), pl.BlockSpec((kw, tc), lambda j, i: (0, j)),
                   pl.BlockSpec((1, tc), lambda j, i: (0, j))],
        out_shape=[jax.ShapeDtypeStruct((t, ncols), BF16), jax.ShapeDtypeStruct((kw, ncols), F32),
                   jax.ShapeDtypeStruct((1, ncols), F32)],
        compiler_params=_cp("parallel", "arbitrary"),
    )(u, u, u, dout, dout, w, b)


def _ple_fwd(x2, gl, pe, pw, *, tr=256, name):
    t, d = x2.shape

    def body(x_ref, gl_ref, pe_ref, pw_ref, o_ref):
        pv = pe_ref[...]
        r = lax.rsqrt(jnp.mean(pv * pv, axis=-1, keepdims=True) + NORM_EPS)
        o_ref[...] = x_ref[...] + _sigmoid(gl_ref[...]) * (pv * r * pw_ref[...])

    blk = pl.BlockSpec((tr, d), lambda i: (i, 0))
    return pl.pallas_call(
        body, name=name, grid=(t // tr,), in_specs=[blk, blk, blk, pl.BlockSpec((1, d), lambda i: (0, 0))],
        out_specs=blk, out_shape=jax.ShapeDtypeStruct((t, d), F32), compiler_params=_cp("parallel"),
    )(x2, gl, pe, pw)


def _ple_bwd(dx3, gl, pe, pw, *, tr=256, name):
    t, d = dx3.shape

    def body(dx_ref, gl_ref, pe_ref, pw_ref, dgl_ref, db_ref, dpe_ref, dpw_ref):
        dx, pv, pwv = dx_ref[...], pe_ref[...], pw_ref[...]
        gate = _sigmoid(gl_ref[...])
        r = lax.rsqrt(jnp.mean(pv * pv, axis=-1, keepdims=True) + NORM_EPS)
        ph = pv * r
        dgl = dx * (ph * pwv) * gate * (1.0 - gate)
        de = dx * gate
        g = de * pwv
        dgl_ref[...] = dgl.astype(BF16)
        dpe_ref[...] = (r * (g - ph * jnp.mean(g * ph, axis=-1, keepdims=True))).astype(BF16)

        @pl.when(pl.program_id(0) == 0)
        def _():
            db_ref[...] = jnp.zeros_like(db_ref)
            dpw_ref[...] = jnp.zeros_like(dpw_ref)

        db_ref[...] += jnp.sum(dgl, axis=0, keepdims=True)
        dpw_ref[...] += jnp.sum(de * ph, axis=0, keepdims=True)

    blk = pl.BlockSpec((tr, d), lambda i: (i, 0))
    row = pl.BlockSpec((1, d), lambda i: (0, 0))
    return pl.pallas_call(
        body, name=name, grid=(t // tr,), in_specs=[blk, blk, blk, row], out_specs=[blk, row, blk, row],
        out_shape=[jax.ShapeDtypeStruct((t, d), BF16), jax.ShapeDtypeStruct((1, d), F32),
                   jax.ShapeDtypeStruct((t, d), BF16), jax.ShapeDtypeStruct((1, d), F32)],
        compiler_params=_cp("arbitrary"),
    )(dx3, gl, pe, pw)


def _loss_head(x3, fw, target, *, tr=256, name):
    t, d = x3.shape

    def body(x_ref, w_ref, t_ref, l_ref, dx_ref, dw_ref):
        xv, wv = x_ref[...], w_ref[...]
        r = lax.rsqrt(jnp.mean(xv * xv, axis=-1, keepdims=True) + NORM_EPS)
        xh = xv * r
        err = xh * wv - t_ref[...]
        dy = err * (1.0 / d)
        g = dy * wv
        dx_ref[...] = r * (g - xh * jnp.mean(g * xh, axis=-1, keepdims=True))

        @pl.when(pl.program_id(0) == 0)
        def _():
            l_ref[...] = jnp.zeros_like(l_ref)
            dw_ref[...] = jnp.zeros_like(dw_ref)

        l_ref[...] += 0.5 * jnp.sum(jnp.mean(err * err, axis=-1, keepdims=True), axis=0, keepdims=True)
        dw_ref[...] += jnp.sum(dy * xh, axis=0, keepdims=True)

    blk = pl.BlockSpec((tr, d), lambda i: (i, 0))
    row = pl.BlockSpec((1, d), lambda i: (0, 0))
    return pl.pallas_call(
        body, name=name, grid=(t // tr,), in_specs=[blk, row, blk],
        out_specs=[pl.BlockSpec((1, 1), lambda i: (0, 0)), blk, row],
        out_shape=[jax.ShapeDtypeStruct((1, 1), F32), jax.ShapeDtypeStruct((t, d), F32),
                   jax.ShapeDtypeStruct((1, d), F32)],
        compiler_params=_cp("arbitrary"),
    )(x3, fw, target)


def _rope(blk, tab_ref):
    return blk * tab_ref[0] + pltpu.roll(blk, 96, axis=1) * tab_ref[1] + pltpu.roll(blk, 32, axis=1) * tab_ref[2]


def _unrope(g, tab_ref):
    return g * tab_ref[0] + pltpu.roll(g * tab_ref[1], 32, axis=1) + pltpu.roll(g * tab_ref[2], 96, axis=1)


def _mla_prep(q, kv, proj, tabs, *, tr=512, name):
    t = q.shape[0]

    def body(q_ref, kv_ref, kr_ref, tab_ref, qo_ref, ko_ref, vo_ref):
        qv, kvv = q_ref[...], kv_ref[...]
        qo_ref[0, :, :MLA_NOPE] = qv[:, :MLA_NOPE].astype(BF16)
        qo_ref[0, :, MLA_NOPE:] = _rope(qv[:, MLA_NOPE:], tab_ref).astype(BF16)
        ko_ref[0, :, :MLA_NOPE] = kvv[:, :MLA_NOPE].astype(BF16)
        ko_ref[0, :, MLA_NOPE:] = _rope(kr_ref[...], tab_ref).astype(BF16)
        vo_ref[0] = kvv[:, MLA_NOPE:].astype(BF16)

    return pl.pallas_call(
        body, name=name, grid=(t // tr, MLA_HEADS),
        in_specs=[pl.BlockSpec((tr, MLA_QK_PAD), lambda i, h: (i, h)),
                  pl.BlockSpec((tr, MLA_NOPE + MLA_V), lambda i, h: (i, h)),
                  pl.BlockSpec((tr, LANES), lambda i, h: (i, OFF_KR // LANES)),
                  pl.BlockSpec((3, tr, LANES), lambda i, h: (0, i, 0))],
        out_specs=[pl.BlockSpec((1, tr, MLA_QK_PAD), lambda i, h: (h, i, 0)),
                   pl.BlockSpec((1, tr, MLA_QK_PAD), lambda i, h: (h, i, 0)),
                   pl.BlockSpec((1, tr, MLA_V), lambda i, h: (h, i, 0))],
        out_shape=[jax.ShapeDtypeStruct((MLA_HEADS, t, MLA_QK_PAD), BF16),
                   jax.ShapeDtypeStruct((MLA_HEADS, t, MLA_QK_PAD), BF16),
                   jax.ShapeDtypeStruct((MLA_HEADS, t, MLA_V), BF16)],
        compiler_params=_cp("parallel", "parallel"),
    )(q, kv, proj, tabs)


def _mla_unprep(dq3, dk3, dv3, tabs, *, tr=256, name):
    t = dq3.shape[1]

    def body(dq_ref, dk_ref, dv_ref, tab_ref, qo_ref, kvo_ref, kro_ref):
        kr = jnp.zeros((tr, LANES), F32)
        for h in range(MLA_HEADS):
            c0 = h * MLA_QK_PAD
            qo_ref[:, c0:c0 + MLA_NOPE] = dq_ref[h, :, :MLA_NOPE].astype(BF16)
            qo_ref[:, c0 + MLA_NOPE:c0 + MLA_QK_PAD] = _unrope(dq_ref[h, :, MLA_NOPE:], tab_ref).astype(BF16)
            kvo_ref[:, c0:c0 + MLA_NOPE] = dk_ref[h, :, :MLA_NOPE].astype(BF16)
            kvo_ref[:, c0 + MLA_NOPE:c0 + MLA_QK_PAD] = dv_ref[h].astype(BF16)
            kr = kr + dk_ref[h, :, MLA_NOPE:]
        kro_ref[...] = _unrope(kr, tab_ref).astype(BF16)

    return pl.pallas_call(
        body, name=name, grid=(t // tr,),
        in_specs=[pl.BlockSpec((MLA_HEADS, tr, MLA_QK_PAD), lambda i: (0, i, 0)),
                  pl.BlockSpec((MLA_HEADS, tr, MLA_QK_PAD), lambda i: (0, i, 0)),
                  pl.BlockSpec((MLA_HEADS, tr, MLA_V), lambda i: (0, i, 0)),
                  pl.BlockSpec((3, tr, LANES), lambda i: (0, i, 0))],
        out_specs=[pl.BlockSpec((tr, MLA_HEADS * MLA_QK_PAD), lambda i: (i, 0)),
                   pl.BlockSpec((tr, MLA_HEADS * MLA_QK_PAD), lambda i: (i, 0)),
                   pl.BlockSpec((tr, LANES), lambda i: (i, 0))],
        out_shape=[jax.ShapeDtypeStruct((t, MLA_HEADS * MLA_QK_PAD), BF16),
                   jax.ShapeDtypeStruct((t, MLA_HEADS * MLA_QK_PAD), BF16),
                   jax.ShapeDtypeStruct((t, LANES), BF16)],
        compiler_params=_cp("parallel"),
    )(dq3, dk3, dv3, tabs)


ATT_BLK = 256
ATT_SCALE = 1.0 / math.sqrt(MLA_NOPE + MLA_ROPE)
_NT = (((1,), (1,)), ((), ()))
_TN = (((0,), (0,)), ((), ()))


def _att_scores(q, k, diagonal):
    s = lax.dot_general(q, k, _NT, preferred_element_type=F32) * ATT_SCALE
    if not diagonal:
        return s
    row = lax.broadcasted_iota(jnp.int32, s.shape, 0)
    col = lax.broadcasted_iota(jnp.int32, s.shape, 1)
    return jnp.where((col >> 6) <= (row >> 6), s, NEG)


def _att_rows(i):
    return pl.ds(pl.multiple_of(i * ATT_BLK, ATT_BLK), ATT_BLK)


def _attn_fwd(q3, k3, v3, *, name):
    t = q3.shape[1]
    nq = t // ATT_BLK

    def body(q_ref, k_ref, v_ref, o_ref, lse_ref):
        qi = pl.program_id(1)
        q = q_ref[0]

        def step(j, carry, diagonal=False):
            m, l, acc = carry
            s = _att_scores(q, k_ref[0, _att_rows(j), :], diagonal)
            m_new = jnp.maximum(m, jnp.max(s, axis=-1, keepdims=True))
            p = jnp.exp(s - m_new)
            alpha = jnp.exp(m - m_new)
            l = alpha * l + jnp.sum(p, axis=-1, keepdims=True)
            acc = alpha * acc + jnp.dot(p.astype(BF16), v_ref[0, _att_rows(j), :], preferred_element_type=F32)
            return m_new, l, acc

        init = (jnp.full((ATT_BLK, 1), NEG, F32), jnp.zeros((ATT_BLK, 1), F32), jnp.zeros((ATT_BLK, MLA_V), F32))
        m, l, acc = step(qi, lax.fori_loop(0, qi, step, init), diagonal=True)
        o_ref[...] = acc / l
        lse_ref[0] = m + jnp.log(l)

    return pl.pallas_call(
        body, name=name, grid=(MLA_HEADS, nq),
        in_specs=[pl.BlockSpec((1, ATT_BLK, MLA_QK_PAD), lambda h, i: (h, i, 0)),
                  pl.BlockSpec((1, t, MLA_QK_PAD), lambda h, i: (h, 0, 0)),
                  pl.BlockSpec((1, t, MLA_V), lambda h, i: (h, 0, 0))],
        out_specs=[pl.BlockSpec((ATT_BLK, MLA_V), lambda h, i: (i, h)),
                   pl.BlockSpec((1, ATT_BLK, 1), lambda h, i: (h, i, 0))],
        out_shape=[jax.ShapeDtypeStruct((t, MLA_HEADS * MLA_V), F32), jax.ShapeDtypeStruct((MLA_HEADS, t, 1), F32)],
        compiler_params=_cp("parallel", "parallel"),
    )(q3, k3, v3)


def _attn_bwd(q3, k3, v3, o, dcat, lse, *, name):
    t = q3.shape[1]
    nq = t // ATT_BLK

    def body(q_ref, k_ref, v_ref, o_ref, do_ref, lse_ref, dq_ref, dk_ref, dv_ref, delta_ref):
        kj = pl.program_id(1)
        k, v = k_ref[0], v_ref[0]

        @pl.when(kj == 0)
        def _():
            dq_ref[...] = jnp.zeros_like(dq_ref)
            delta_ref[...] = jnp.sum(o_ref[...] * do_ref[...], axis=-1, keepdims=True)

        def step(i, carry, diagonal=False):
            dk, dv = carry
            rows = _att_rows(i)
            q = q_ref[0, rows, :]
            dob = do_ref[rows, :].astype(BF16)
            p = jnp.exp(_att_scores(q, k, diagonal) - lse_ref[0, rows, :])
            dv = dv + lax.dot_general(p.astype(BF16), dob, _TN, preferred_element_type=F32)
            dp = lax.dot_general(dob, v, _NT, preferred_element_type=F32)
            ds = (p * (dp - delta_ref[rows, :]) * ATT_SCALE).astype(BF16)
            dk = dk + lax.dot_general(ds, q, _TN, preferred_element_type=F32)
            dq_ref[0, rows, :] += jnp.dot(ds, k, preferred_element_type=F32)
            return dk, dv

        init = (jnp.zeros((ATT_BLK, MLA_QK_PAD), F32), jnp.zeros((ATT_BLK, MLA_V), F32))
        dk, dv = lax.fori_loop(kj + 1, nq, step, step(kj, init, diagonal=True))
        dk_ref[0] = dk
        dv_ref[0] = dv

    return pl.pallas_call(
        body, name=name, grid=(MLA_HEADS, nq),
        in_specs=[pl.BlockSpec((1, t, MLA_QK_PAD), lambda h, j: (h, 0, 0)),
                  pl.BlockSpec((1, ATT_BLK, MLA_QK_PAD), lambda h, j: (h, j, 0)),
                  pl.BlockSpec((1, ATT_BLK, MLA_V), lambda h, j: (h, j, 0)),
                  pl.BlockSpec((t, MLA_V), lambda h, j: (0, h)),
                  pl.BlockSpec((t, MLA_V), lambda h, j: (0, MLA_HEADS + h)),
                  pl.BlockSpec((1, t, 1), lambda h, j: (h, 0, 0))],
        out_specs=[pl.BlockSpec((1, t, MLA_QK_PAD), lambda h, j: (h, 0, 0)),
                   pl.BlockSpec((1, ATT_BLK, MLA_QK_PAD), lambda h, j: (h, j, 0)),
                   pl.BlockSpec((1, ATT_BLK, MLA_V), lambda h, j: (h, j, 0))],
        out_shape=[jax.ShapeDtypeStruct((MLA_HEADS, t, MLA_QK_PAD), F32),
                   jax.ShapeDtypeStruct((MLA_HEADS, t, MLA_QK_PAD), F32),
                   jax.ShapeDtypeStruct((MLA_HEADS, t, MLA_V), F32)],
        scratch_shapes=[pltpu.VMEM((t, 1), F32)],
        compiler_params=_cp("parallel", "arbitrary"),
    )(q3, k3, v3, o, dcat, lse)


def _ssd_prep(proj, bias128, alog128, *, name):
    t = proj.shape[0]
    nc = t // CHUNK

    def body(raw_ref, b_ref, al_ref, dt_ref, cs_ref, a_ref):
        xv = raw_ref[...] + b_ref[...]
        dt = jnp.maximum(xv, 0.0) + jnp.log(1.0 + jnp.exp(-jnp.abs(xv)))
        a = -jnp.exp(al_ref[...])
        adt = (dt * a).reshape(nc, CHUNK, LANES)
        li = lax.broadcasted_iota(jnp.int32, (nc, CHUNK, CHUNK), 1)
        si = lax.broadcasted_iota(jnp.int32, (nc, CHUNK, CHUNK), 2)
        tril = jnp.where(si <= li, 1.0, 0.0).astype(F32)
        cs = lax.dot_general(tril, adt, (((2,), (1,)), ((0,), (0,))), precision=HI, preferred_element_type=F32)
        dt_ref[...] = dt
        cs_ref[...] = cs.reshape(t, LANES)
        a_ref[...] = a

    blk = pl.BlockSpec((t, LANES), lambda i: (0, 0))
    row = pl.BlockSpec((1, LANES), lambda i: (0, 0))
    return pl.pallas_call(
        body, name=name, grid=(1,),
        in_specs=[pl.BlockSpec((t, LANES), lambda i: (0, OFF_DT // LANES)), row, row],
        out_specs=[blk, blk, row],
        out_shape=[jax.ShapeDtypeStruct((t, LANES), F32), jax.ShapeDtypeStruct((t, LANES), F32),
                   jax.ShapeDtypeStruct((1, LANES), F32)],
        compiler_params=_cp("arbitrary"),
    )(proj, bias128, alog128)


def _ssd_prep_bwd(ddt128, dadt128, proj, bias128, dt128, a128, dd_h, *, name):
    t = proj.shape[0]

    def body(ddt_ref, dadt_ref, raw_ref, b_ref, dt_ref, a_ref, dd_ref, draw_ref, db_ref, dal_ref, dds_ref):
        draw = ddt_ref[...] * _sigmoid(raw_ref[...] + b_ref[...])
        draw_ref[...] = draw.astype(BF16)
        db_ref[...] = jnp.sum(draw, axis=0, keepdims=True)
        dal_ref[...] = jnp.sum(dadt_ref[...] * dt_ref[...], axis=0, keepdims=True) * a_ref[...]
        dds_ref[...] = jnp.sum(dd_ref[...], axis=-1, keepdims=True)

    blk = pl.BlockSpec((t, LANES), lambda i: (0, 0))
    row = pl.BlockSpec((1, LANES), lambda i: (0, 0))
    return pl.pallas_call(
        body, name=name, grid=(1,),
        in_specs=[blk, blk, pl.BlockSpec((t, LANES), lambda i: (0, OFF_DT // LANES)), row, blk, row,
                  pl.BlockSpec((SSD_HEADS, SSD_P), lambda i: (0, 0))],
        out_specs=[blk, row, row, pl.BlockSpec((SSD_HEADS, 1), lambda i: (0, 0))],
        out_shape=[jax.ShapeDtypeStruct((t, LANES), BF16), jax.ShapeDtypeStruct((1, LANES), F32),
                   jax.ShapeDtypeStruct((1, LANES), F32), jax.ShapeDtypeStruct((SSD_HEADS, 1), F32)],
        compiler_params=_cp("arbitrary"),
    )(ddt128, dadt128, proj, bias128, dt128, a128, dd_h)


def _bdot(a, b, ca, cb, precision=None):
    return lax.dot_general(a, b, (((ca,), (cb,)), ((0,), (0,))), precision=precision, preferred_element_type=F32)


def _ssd_common(xs_ref, dt_ref, cs_ref, csr_ref, b_ref, c_ref, nc):
    x = xs_ref[0].reshape(nc, CHUNK, SSD_P)
    dt = dt_ref[0].reshape(nc, CHUNK, SSD_P)
    cs = cs_ref[0].reshape(nc, CHUNK, SSD_P)
    csr = csr_ref[0]
    bm = b_ref[0].reshape(nc, CHUNK, SSD_N).astype(BF16)
    cm = c_ref[0].reshape(nc, CHUNK, SSD_N).astype(BF16)
    li = lax.broadcasted_iota(jnp.int32, (nc, CHUNK, CHUNK), 1)
    si = lax.broadcasted_iota(jnp.int32, (nc, CHUNK, CHUNK), 2)
    lmat = jnp.exp(jnp.where(si <= li, cs - csr, NEG))
    g = _bdot(cm, bm, 2, 2)
    cs_last = jnp.sum(jnp.where(li == CHUNK - 1, cs, 0.0), axis=1, keepdims=True)
    xdt = x * dt
    dec = jnp.exp(cs_last - cs)
    return x, dt, cs, bm, cm, li, si, lmat, g, cs_last, xdt, dec


def _ssd_fwd(xs_h, dt_h, cs_h, cs_row, b_g, c_g, dskip_h, *, name):
    t = xs_h.shape[1]
    nc = t // CHUNK
    hpg = SSD_HEADS // SSD_GROUPS

    def body(xs_ref, dt_ref, cs_ref, csr_ref, b_ref, c_ref, dk_ref, y_ref, st_ref, sc_ref, cd_ref):
        x, dt, cs, bm, cm, li, si, lmat, g, cs_last, xdt, dec = _ssd_common(xs_ref, dt_ref, cs_ref, csr_ref, b_ref,
                                                                           c_ref, nc)
        yd = _bdot((g * lmat).astype(BF16), xdt.astype(BF16), 2, 1)
        sc_ref[...] = _bdot(bm, (dec * xdt).astype(BF16), 1, 1)
        cd_ref[...] = jnp.exp(cs_last)

        def step(c, s):
            st_ref[0, c] = s
            return s * cd_ref[c] + sc_ref[c]

        lax.fori_loop(0, nc, step, jnp.zeros((SSD_N, SSD_P), F32))
        yo = _bdot(cm, st_ref[0].astype(BF16), 2, 1) * jnp.exp(cs)
        y_ref[0] = (yd + yo + dk_ref[0] * x).reshape(t, SSD_P)

    head = pl.BlockSpec((1, t, SSD_P), lambda h: (h, 0, 0))
    grp = pl.BlockSpec((1, t, SSD_N), lambda h: (h // hpg, 0, 0))
    return pl.pallas_call(
        body, name=name, grid=(SSD_HEADS,),
        in_specs=[head, head, head, pl.BlockSpec((1, nc, 1, CHUNK), lambda h: (h, 0, 0, 0)), grp, grp,
                  pl.BlockSpec((1, 1, SSD_P), lambda h: (h, 0, 0))],
        out_specs=[head, pl.BlockSpec((1, nc, SSD_N, SSD_P), lambda h: (h, 0, 0, 0))],
        out_shape=[jax.ShapeDtypeStruct((SSD_HEADS, t, SSD_P), F32),
                   jax.ShapeDtypeStruct((SSD_HEADS, nc, SSD_N, SSD_P), F32)],
        scratch_shapes=[pltpu.VMEM((nc, SSD_N, SSD_P), F32), pltpu.VMEM((nc, 1, SSD_P), F32)],
        compiler_params=_cp("parallel"),
    )(xs_h, dt_h, cs_h, cs_row, b_g, c_g, dskip_h)


def _ssd_bwd(xs_h, dt_h, cs_h, cs_row, b_g, c_g, dskip_h, a_h, states, dy_h, *, name):
    t = xs_h.shape[1]
    nc = t // CHUNK
    hpg = SSD_HEADS // SSD_GROUPS

    def body(xs_ref, dt_ref, cs_ref, csr_ref, b_ref, c_ref, dk_ref, a_ref, st_ref, dy_ref,
             dxs_ref, ddt_ref, dadt_ref, db_ref, dc_ref, dd_ref, dsl_ref, dsc_ref, cd_ref):
        x, dt, cs, bm, cm, li, si, lmat, g, cs_last, xdt, dec = _ssd_common(xs_ref, dt_ref, cs_ref, csr_ref, b_ref,
                                                                           c_ref, nc)
        dy = dy_ref[0].reshape(nc, CHUNK, SSD_P)
        dyb = dy.astype(BF16)
        xdtb = xdt.astype(BF16)
        sprev = st_ref[0]
        sprevb = sprev.astype(BF16)
        cdec = jnp.exp(cs_last)
        ecs = jnp.exp(cs)
        dw = (ecs * dy).astype(BF16)
        wmat = _bdot(cm, sprevb, 2, 1)
        dcs = jnp.sum(dy * ecs * wmat, axis=2, keepdims=True)
        dcm = _bdot(dw, sprevb, 2, 2)
        dsl_ref[...] = _bdot(cm, dw, 1, 1)
        cd_ref[...] = cdec

        def step(k, ds):
            c = nc - 1 - k
            dsc_ref[c] = ds
            return ds * cd_ref[c] + dsl_ref[c]

        lax.fori_loop(0, nc, step, jnp.zeros((SSD_N, SSD_P), F32))
        dsc = dsc_ref[...]
        dscb = dsc.astype(BF16)
        d_last = jnp.sum(jnp.sum(dsc * sprev, axis=1, keepdims=True) * cdec, axis=2, keepdims=True)
        z = dec * xdt
        dbm = _bdot(z.astype(BF16), dscb, 2, 2)
        dz = _bdot(bm, dscb, 2, 1)
        dxdt = dec * dz
        t2 = jnp.sum(dz * z, axis=2, keepdims=True)
        dcs = dcs - t2
        d_last = d_last + jnp.sum(t2, axis=1, keepdims=True)
        m = g * lmat
        mb = m.astype(BF16)
        dm = _bdot(dyb, xdtb, 2, 2)
        dxdt = dxdt + _bdot(mb, dyb, 1, 1)
        dseg = dm * m
        dcs = dcs + jnp.sum(dseg, axis=2, keepdims=True)
        ones = jnp.ones((nc, CHUNK, SSD_P), F32)
        dcs = dcs - _bdot(dseg, ones, 1, 1, precision=HI)
        dg = (dm * lmat).astype(BF16)
        dcm = dcm + _bdot(dg, bm, 2, 1)
        dbm = dbm + _bdot(dg, cm, 1, 1)
        dcs = dcs + jnp.where(li[:, :, :SSD_P] == CHUNK - 1, d_last, 0.0)
        triu = jnp.where(li <= si, 1.0, 0.0).astype(F32)
        dadt = _bdot(triu, dcs, 2, 1, precision=HI)
        dk = dk_ref[0]
        dxs_ref[0] = (dxdt * dt + dk * dy).reshape(t, SSD_P)
        ddt_ref[0] = (jnp.sum(dxdt * x, axis=2, keepdims=True) + dadt * a_ref[0]).reshape(t, SSD_P)
        dadt_ref[0] = dadt.reshape(t, SSD_P)
        dd_ref[0] = jnp.sum(jnp.sum(dy * x, axis=1, keepdims=True), axis=0)

        @pl.when(pl.program_id(1) == 0)
        def _():
            db_ref[...] = jnp.zeros_like(db_ref)
            dc_ref[...] = jnp.zeros_like(dc_ref)

        db_ref[0] += dbm.reshape(t, SSD_N)
        dc_ref[0] += dcm.reshape(t, SSD_N)

    head = pl.BlockSpec((1, t, SSD_P), lambda gi, hi: (gi * hpg + hi, 0, 0))
    grp = pl.BlockSpec((1, t, SSD_N), lambda gi, hi: (gi, 0, 0))
    lane = pl.BlockSpec((1, 1, SSD_P), lambda gi, hi: (gi * hpg + hi, 0, 0))
    return pl.pallas_call(
        body, name=name, grid=(SSD_GROUPS, hpg),
        in_specs=[head, head, head, pl.BlockSpec((1, nc, 1, CHUNK), lambda gi, hi: (gi * hpg + hi, 0, 0, 0)),
                  grp, grp, lane, lane, pl.BlockSpec((1, nc, SSD_N, SSD_P), lambda gi, hi: (gi * hpg + hi, 0, 0, 0)),
                  head],
        out_specs=[head, head, head, grp, grp, lane],
        out_shape=[jax.ShapeDtypeStruct((SSD_HEADS, t, SSD_P), F32)] * 3
        + [jax.ShapeDtypeStruct((SSD_GROUPS, t, SSD_N), F32)] * 2
        + [jax.ShapeDtypeStruct((SSD_HEADS, 1, SSD_P), F32)],
        scratch_shapes=[pltpu.VMEM((nc, SSD_N, SSD_P), F32), pltpu.VMEM((nc, SSD_N, SSD_P), F32),
                        pltpu.VMEM((nc, 1, SSD_P), F32)],
        compiler_params=_cp("parallel", "arbitrary"),
    )(xs_h, dt_h, cs_h, cs_row, b_g, c_g, dskip_h, a_h, states, dy_h)


def _ssd_gate_fwd(y, proj, w, *, tr=256, name):
    t = y.shape[0]
    gw = D_SSM // SSD_GROUPS

    def body(y_ref, z_ref, w_ref, o_ref):
        v = y_ref[...] * _silu(z_ref[...])
        for gi in range(SSD_GROUPS):
            vg = v[:, gi * gw:(gi + 1) * gw]
            r = lax.rsqrt(jnp.mean(vg * vg, axis=-1, keepdims=True) + NORM_EPS)
            o_ref[:, gi * gw:(gi + 1) * gw] = (vg * r * w_ref[:, gi * gw:(gi + 1) * gw]).astype(BF16)

    blk = pl.BlockSpec((tr, D_SSM), lambda i: (i, 0))
    return pl.pallas_call(
        body, name=name, grid=(t // tr,), in_specs=[blk, blk, pl.BlockSpec((1, D_SSM), lambda i: (0, 0))],
        out_specs=blk, out_shape=jax.ShapeDtypeStruct((t, D_SSM), BF16), compiler_params=_cp("parallel"),
    )(y, proj, w)


def _ssd_gate_bwd(y, proj, w, dcat, *, tr=256, name):
    t = y.shape[0]
    gw = D_SSM // SSD_GROUPS

    def body(y_ref, z_ref, w_ref, d_ref, dy_ref, dz_ref, dw_ref):
        yv, zv, dv = y_ref[...], z_ref[...], d_ref[...].astype(F32)
        sz = _silu(zv)
        v = yv * sz

        @pl.when(pl.program_id(0) == 0)
        def _():
            dw_ref[...] = jnp.zeros_like(dw_ref)

        for gi in range(SSD_GROUPS):
            sl = slice(gi * gw, (gi + 1) * gw)
            vg, dg = v[:, sl], dv[:, sl]
            r = lax.rsqrt(jnp.mean(vg * vg, axis=-1, keepdims=True) + NORM_EPS)
            vh = vg * r
            gg = dg * w_ref[:, sl]
            dvg = r * (gg - vh * jnp.mean(gg * vh, axis=-1, keepdims=True))
            dy_ref[:, sl] = dvg * sz[:, sl]
            dz_ref[:, sl] = (dvg * yv[:, sl] * _dsilu(zv[:, sl])).astype(BF16)
            dw_ref[:, sl] += jnp.sum(dg * vh, axis=0, keepdims=True)

    blk = pl.BlockSpec((tr, D_SSM), lambda i: (i, 0))
    row = pl.BlockSpec((1, D_SSM), lambda i: (0, 0))
    return pl.pallas_call(
        body, name=name, grid=(t // tr,), in_specs=[blk, blk, row, blk], out_specs=[blk, blk, row],
        out_shape=[jax.ShapeDtypeStruct((t, D_SSM), F32), jax.ShapeDtypeStruct((t, D_SSM), BF16),
                   jax.ShapeDtypeStruct((1, D_SSM), F32)],
        compiler_params=_cp("arbitrary"),
    )(y, proj, w, dcat)


def _pad_lanes(v):
    return jnp.pad(v, ((0, 0), (0, LANES - v.shape[1])))


def _to_heads(v):
    return v.reshape(v.shape[0], SSD_HEADS, SSD_P).transpose(1, 0, 2)


def _from_heads(v):
    return v.transpose(1, 0, 2).reshape(v.shape[1], SSD_HEADS * SSD_P)


def _per_head(v128, t):
    return jnp.broadcast_to(v128[:, :SSD_HEADS].T[:, :, None], (SSD_HEADS, t, SSD_P))


def _ssd_forward(proj, conv_w, conv_b, dt_bias, a_log, d_skip, ssd_norm_w):
    t = proj.shape[0]
    nc = t // CHUNK
    xbc = _conv_act_fwd(proj, conv_w, conv_b, kw=SSD_CONV, glu=False, tc=512, coff=OFF_XBC // 512,
                        ncols=SSD_CONV_DIM, out_dtype=F32, name="ssd_conv_fwd")
    bias128, alog128 = _pad_lanes(dt_bias), _pad_lanes(a_log)
    dt128, cs128, a128 = _ssd_prep(proj, bias128, alog128, name="ssd_prep")
    dt_h, cs_h = _per_head(dt128, t), _per_head(cs128, t)
    cs_row = cs128[:, :SSD_HEADS].T.reshape(SSD_HEADS, nc, 1, CHUNK)
    xs_h = _to_heads(xbc[:, :D_SSM])
    gn = SSD_GROUPS * SSD_N
    b_g = xbc[:, D_SSM:D_SSM + gn].reshape(t, SSD_GROUPS, SSD_N).transpose(1, 0, 2)
    c_g = xbc[:, D_SSM + gn:].reshape(t, SSD_GROUPS, SSD_N).transpose(1, 0, 2)
    dskip_h = jnp.broadcast_to(d_skip[0][:, None, None], (SSD_HEADS, 1, SSD_P))
    a_h = jnp.broadcast_to(a128[0, :SSD_HEADS][:, None, None], (SSD_HEADS, 1, SSD_P))
    y_h, states = _ssd_fwd(xs_h, dt_h, cs_h, cs_row, b_g, c_g, dskip_h, name="ssd_scan_fwd")
    y = _from_heads(y_h)
    y_ssd = _ssd_gate_fwd(y, proj, ssd_norm_w, name="ssd_gate_fwd")
    saved = (proj, conv_w, conv_b, ssd_norm_w, bias128, dt128, a128, dt_h, cs_h, cs_row, xs_h, b_g, c_g, dskip_h, a_h,
             states, y)
    return y_ssd, saved


def _ssd_backward(saved, dcat):
    (proj, conv_w, conv_b, ssd_norm_w, bias128, dt128, a128, dt_h, cs_h, cs_row, xs_h, b_g, c_g, dskip_h, a_h, states,
     y) = saved
    t = proj.shape[0]
    dy, dz, d_norm_w = _ssd_gate_bwd(y, proj, ssd_norm_w, dcat, name="ssd_gate_bwd")
    dxs_h, ddt_h, dadt_h, db_g, dc_g, dd_h = _ssd_bwd(xs_h, dt_h, cs_h, cs_row, b_g, c_g, dskip_h, a_h, states,
                                                      _to_heads(dy), name="ssd_scan_bwd")
    gn = SSD_GROUPS * SSD_N
    dxc = jnp.concatenate([_from_heads(dxs_h), db_g.transpose(1, 0, 2).reshape(t, gn),
                           dc_g.transpose(1, 0, 2).reshape(t, gn)], axis=1)
    dxbc, d_conv_w, d_conv_b = _conv_act_bwd(proj, conv_w, conv_b, dxc, kw=SSD_CONV, glu=False, tc=512,
                                             coff=OFF_XBC // 512, ncols=SSD_CONV_DIM, name="ssd_conv_bwd")
    ddt128 = _pad_lanes(ddt_h[:, :, 0].T)
    dadt128 = _pad_lanes(dadt_h[:, :, 0].T)
    d_raw, d_bias, d_alog, d_dskip = _ssd_prep_bwd(ddt128, dadt128, proj, bias128, dt128, a128,
                                                   dd_h.reshape(SSD_HEADS, SSD_P), name="ssd_prep_bwd")
    return (dz, dxbc, d_raw, d_norm_w, d_conv_w, d_conv_b, d_bias[:, :SSD_HEADS], d_alog[:, :SSD_HEADS],
            d_dskip.reshape(1, SSD_HEADS))


def _rope_tables(positions):
    inv_freq = ROPE_THETA ** (-jnp.arange(0, MLA_ROPE, 2, dtype=F32) / MLA_ROPE)
    ang = positions[0].astype(F32)[:, None] * inv_freq
    cos, sin = jnp.cos(ang), jnp.sin(ang)
    z = jnp.zeros_like(cos)
    return jnp.stack([jnp.concatenate([cos, cos, z, z], axis=1), jnp.concatenate([-sin, z, z, z], axis=1),
                      jnp.concatenate([z, sin, z, z], axis=1)])


def _mla_forward(proj, tabs, q_a_norm_w, wq_pad, kv_a_norm_w, wkv):
    qn = _rmsnorm_fwd(proj, q_a_norm_w, width=MLA_Q_RANK, cblk=OFF_QA // MLA_Q_RANK, name="q_a_norm")
    q = _matmul(qn, wq_pad, name="q_b_proj")
    kvn = _rmsnorm_fwd(proj, kv_a_norm_w, width=MLA_KV_RANK, cblk=OFF_CKV // MLA_KV_RANK, name="kv_a_norm")
    kv = _matmul(kvn, wkv, name="kv_b_proj")
    q3, k3, v3 = _mla_prep(q, kv, proj, tabs, name="mla_prep")
    o, lse = _attn_fwd(q3, k3, v3, name="attn_fwd")
    return o, (proj, tabs, q_a_norm_w, wq_pad, kv_a_norm_w, wkv, qn, kvn, q3, k3, v3, o, lse)


def _mla_backward(saved, dcat):
    proj, tabs, q_a_norm_w, wq_pad, kv_a_norm_w, wkv, qn, kvn, q3, k3, v3, o, lse = saved
    dq3, dk3, dv3 = _attn_bwd(q3, k3, v3, o, dcat, lse, name="attn_bwd")
    dq, dkv, dkr = _mla_unprep(dq3, dk3, dv3, tabs, name="mla_unprep")
    d_wq = _matmul(qn, dq, ta=True, out_dtype=BF16, name="d_w_q_b")
    dqn = _matmul(dq, wq_pad, tb=True, name="d_qn")
    dq_a, d_qnw = _rmsnorm_bwd(proj, q_a_norm_w, dqn, width=MLA_Q_RANK, cblk=OFF_QA // MLA_Q_RANK, out_dtype=BF16,
                               name="q_a_norm_bwd")
    d_wkv = _matmul(kvn, dkv, ta=True, out_dtype=BF16, name="d_w_kv_b")
    dkvn = _matmul(dkv, wkv, tb=True, name="d_kvn")
    dckv, d_kvnw = _rmsnorm_bwd(proj, kv_a_norm_w, dkvn, width=MLA_KV_RANK, cblk=OFF_CKV // MLA_KV_RANK,
                                out_dtype=BF16, name="kv_a_norm_bwd")
    return dq_a, dckv, dkr, d_wq, d_wkv, d_qnw, d_kvnw


def _pad_w_q(w):
    r = w.shape[0]
    w3 = w.reshape(r, MLA_HEADS, MLA_NOPE + MLA_ROPE)
    return jnp.pad(w3, ((0, 0), (0, 0), (0, MLA_QK_PAD - MLA_NOPE - MLA_ROPE))).reshape(r, MLA_HEADS * MLA_QK_PAD)


def _unpad_w_q(w):
    r = w.shape[0]
    return w.reshape(r, MLA_HEADS, MLA_QK_PAD)[:, :, :MLA_NOPE + MLA_ROPE].reshape(r, MLA_HEADS * (MLA_NOPE + MLA_ROPE))


def _pad_w_in(w):
    r = w.shape[0]
    o_dt = D_SSM + SSD_CONV_DIM
    o_qa = o_dt + SSD_HEADS
    o_kr = o_qa + MLA_Q_RANK + MLA_KV_RANK
    zeros = lambda n: jnp.zeros((r, n), w.dtype)
    return jnp.concatenate([w[:, :o_dt], w[:, o_qa:o_kr], w[:, o_kr:], zeros(LANES - MLA_ROPE),
                            w[:, o_dt:o_qa], zeros(LANES - SSD_HEADS)], axis=1)


def _unpad_w_in(w):
    return jnp.concatenate([w[:, :OFF_QA], w[:, OFF_DT:OFF_DT + SSD_HEADS], w[:, OFF_QA:OFF_KR + MLA_ROPE]], axis=1)


WEIGHTS = ['mix_norm_w', 'w_in', 'conv_w', 'conv_b', 'dt_bias', 'a_log', 'd_skip', 'ssd_norm_w', 'q_a_norm_w', 'w_q_b',
           'kv_a_norm_w', 'w_kv_b', 'w_out', 'ffn_norm_w', 'w_ffn_up', 'ffn_conv_w', 'ffn_conv_b', 'w_ffn_down',
           'ple_norm_w', 'w_ple_gate', 'b_ple_gate', 'w_ple_proj', 'ple_post_norm_w', 'final_norm_w']
BIG = ['w_in', 'w_q_b', 'w_kv_b', 'w_out', 'w_ffn_up', 'w_ffn_down', 'w_ple_gate', 'w_ple_proj']
COL_SHARDED = ('w_in', 'w_q_b', 'w_kv_b', 'w_ffn_up', 'w_ple_proj')
CONV = ['conv_w', 'ffn_conv_w']
REPL = [n for n in WEIGHTS if n not in BIG and n not in CONV]
FFN_INV = tuple(int(i) for i in np.argsort(FFN_PERM))


def _cat_cols(g):
    return g.transpose(1, 0, 2).reshape(g.shape[1], N_DEV * g.shape[2])


def _split_cols(w):
    return w.reshape(w.shape[0], N_DEV, w.shape[1] // N_DEV).transpose(1, 0, 2)


def _interleave(v):
    r = v.shape[0]
    return v.reshape(r, N_DEV, FFN_TC)[:, jnp.array(FFN_PERM)].reshape(r, N_DEV * FFN_TC)


def _deinterleave(v):
    r = v.shape[0]
    return v.reshape(r, N_DEV, FFN_TC)[:, jnp.array(FFN_INV)].reshape(r, N_DEV * FFN_TC)


def _assemble_weights(g):
    layout = {
        'w_in': lambda v: _pad_w_in(_cat_cols(v)),
        'w_q_b': lambda v: _pad_w_q(_cat_cols(v)),
        'w_kv_b': _cat_cols,
        'w_out': lambda v: v.reshape(D_MODEL, D_MODEL),
        'w_ffn_up': lambda v: v,
        'w_ffn_down': lambda v: v.reshape(D_FF, D_MODEL),
        'w_ple_gate': lambda v: v.reshape(D_MODEL, D_MODEL),
        'w_ple_proj': _cat_cols,
        'conv_w': _cat_cols,
        'ffn_conv_w': lambda v: _interleave(_cat_cols(v)),
    }
    return {n: layout[n](v) for n, v in g.items()}


WEIGHT_GROUPS = {'a': ['w_in', 'w_q_b', 'w_kv_b', 'conv_w'], 'b': ['w_out'],
                 'c': ['w_ffn_up', 'ffn_conv_w', 'w_ffn_down', 'w_ple_gate', 'w_ple_proj']}
GRAD_GROUPS = {'p': ['w_ple_proj', 'w_ple_gate', 'w_ffn_down'], 'r': ['w_ffn_up'], 's': ['w_out'],
               't': ['w_q_b', 'w_kv_b', 'w_in']}


def _ffn_perm(j):
    return (j % 2) * (N_DEV // 2) + j // 2


def _local_step(x, p, tabs, get_w, s, target, emit, relay):
    t = x.shape[0]
    s = dict(s)
    half = D_MODEL // 2
    up_cols = 2 * D_FF
    ffn_conv_b = _interleave(s['ffn_conv_b'])
    w = dict(get_w('a', None))
    h = _rmsnorm_fwd(x, s['mix_norm_w'], width=D_MODEL, name="mix_norm")
    proj = _matmul(h, w['w_in'], name="in_proj")
    y_ssd, ssd_saved = _ssd_forward(proj, w['conv_w'], s['conv_b'], s['dt_bias'], s['a_log'], s['d_skip'],
                                    s['ssd_norm_w'])
    o, mla_saved = _mla_forward(proj, tabs, s['q_a_norm_w'], w['w_q_b'], s['kv_a_norm_w'], w['w_kv_b'])
    tk_o, tn_o = _tile(half, MM_TK), _tile(D_MODEL, MM_TILE)
    w.update(get_w('b', o))
    x1 = _matmul(y_ssd, w['w_out'], add=x, mnk=(t, D_MODEL, half), name="out_proj_ssd")
    x1 = _matmul(o, w['w_out'], add=x1, mnk=(t, D_MODEL, half), name="out_proj_mla",
                 b_spec=pl.BlockSpec((tk_o, tn_o), lambda i, j, kk: (kk + half // tk_o, j)))
    hf = _rmsnorm_fwd(x1, s['ffn_norm_w'], width=D_MODEL, name="ffn_norm")
    w.update(get_w('c', hf))
    tk_u = _tile(D_MODEL, MM_TK)
    u = _matmul(hf, w['w_ffn_up'], mnk=(t, up_cols, D_MODEL), tn=FFN_TC, name="ffn_up",
                b_spec=pl.BlockSpec((1, tk_u, FFN_TC), lambda i, j, kk: (_ffn_perm(j), kk, 0)))
    act = _conv_act_fwd(u, w['ffn_conv_w'], ffn_conv_b, kw=FFN_CONV, glu=True, tc=2 * FFN_TC, coff=0, ncols=up_cols,
                        out_dtype=BF16, name="ffn_act")
    x2 = _matmul(act, w['w_ffn_down'], add=x1, name="ffn_down")
    hp = _rmsnorm_fwd(x2, s['ple_norm_w'], width=D_MODEL, name="ple_norm")
    gl = _matmul(hp, w['w_ple_gate'], bias=s['b_ple_gate'], name="ple_gate")
    pe = _matmul(p, w['w_ple_proj'], name="ple_proj")
    x3 = _ple_fwd(x2, gl, pe, s['ple_post_norm_w'], name="ple_mix")
    loss, dx3, d_final = _loss_head(x3, s['final_norm_w'], target, name="loss_head")
    dgl, d_bgate, dpe, d_post = _ple_bwd(dx3, gl, pe, s['ple_post_norm_w'], name="ple_mix_bwd")
    d_wproj = _matmul(p, dpe, ta=True, out_dtype=BF16, name="d_w_ple_proj")
    d_wgate = _matmul(hp, dgl, ta=True, out_dtype=BF16, name="d_w_ple_gate")
    dhp = _matmul(dgl, w['w_ple_gate'], tb=True, name="d_ple_normed")
    dx2, d_plenorm = _rmsnorm_bwd(x2, s['ple_norm_w'], dhp, dx3, width=D_MODEL, name="ple_norm_bwd")
    dact = _matmul(dx2, w['w_ffn_down'], tb=True, name="d_ffn_act")
    d_wdown = _matmul(act, dx2, ta=True, out_dtype=BF16, name="d_w_ffn_down")
    zz = emit('p', {'w_ple_proj': _split_cols(d_wproj), 'w_ple_gate': d_wgate.reshape(N_DEV, D_MODEL // N_DEV, D_MODEL),
                    'w_ffn_down': d_wdown.reshape(N_DEV, D_FF // N_DEV, D_MODEL)})
    du, d_fconv_w, d_fconv_b = _conv_act_bwd(u, w['ffn_conv_w'], ffn_conv_b + zz, dact, kw=FFN_CONV, glu=True,
                                             tc=2 * FFN_TC, coff=0, ncols=up_cols, name="ffn_act_bwd")
    zz = zz + relay('p', du)
    tm_u = _tile(D_MODEL, MM_TILE)
    d_wup = _matmul(hf, du, ta=True, out_dtype=BF16, mnk=(D_MODEL, up_cols, t), tn=FFN_TC, name="d_w_ffn_up",
                    o_spec=pl.BlockSpec((1, tm_u, FFN_TC), lambda i, j, kk: (_ffn_perm(j), i, 0)),
                    o_shape=(N_DEV, D_MODEL, FFN_TC))
    zz = zz + emit('r', {'w_ffn_up': d_wup})
    dhf = _matmul(du, w['w_ffn_up'], tb=True, mnk=(t, D_MODEL, up_cols), tk=FFN_TC, name="d_ffn_normed",
                  b_spec=pl.BlockSpec((1, tn_o, FFN_TC), lambda i, j, kk: (_ffn_perm(kk), j, 0)))
    zz = zz + relay('r', dhf)
    dx1, d_ffnnorm = _rmsnorm_bwd(x1, s['ffn_norm_w'] + zz, dhf, dx2, width=D_MODEL, name="ffn_norm_bwd")
    dcat = _matmul(dx1, w['w_out'], tb=True, name="d_mixed")
    d_wout = jnp.concatenate([_matmul(y_ssd, dx1, ta=True, out_dtype=BF16, name="d_w_out_ssd"),
                              _matmul(o, dx1, ta=True, out_dtype=BF16, name="d_w_out_mla")], axis=0)
    zz = zz + emit('s', {'w_out': d_wout.reshape(N_DEV, D_MODEL // N_DEV, D_MODEL)})
    ssd_saved = ssd_saved[:3] + (ssd_saved[3] + zz,) + ssd_saved[4:]
    dz, dxbc, d_raw, d_ssdnorm, d_conv_w, d_conv_b, d_dtb, d_alog, d_dskip = _ssd_backward(ssd_saved, dcat)
    zz = zz + relay('s', dz)
    mla_saved = mla_saved[:-1] + (mla_saved[-1] + zz,)
    dq_a, dckv, dkr, d_wq, d_wkv, d_qnorm, d_kvnorm = _mla_backward(mla_saved, dcat)
    dproj = jnp.concatenate([dz, dxbc, dq_a, dckv, dkr, d_raw], axis=1)
    d_win = _matmul(h, dproj, ta=True, out_dtype=BF16, name="d_w_in")
    dh = _matmul(dproj, w['w_in'], tb=True, name="d_in_normed")
    dx, d_mixnorm = _rmsnorm_bwd(x, s['mix_norm_w'], dh, dx1, width=D_MODEL, name="mix_norm_bwd")
    emit('t', {'w_in': _split_cols(_unpad_w_in(d_win)), 'w_q_b': _split_cols(_unpad_w_q(d_wq)),
               'w_kv_b': _split_cols(d_wkv)})
    relay('t', dx)
    conv = {'conv_w': d_conv_w, 'ffn_conv_w': _deinterleave(d_fconv_w)}
    vec = {
        'mix_norm_w': d_mixnorm, 'conv_b': d_conv_b, 'dt_bias': d_dtb, 'a_log': d_alog, 'd_skip': d_dskip,
        'ssd_norm_w': d_ssdnorm, 'q_a_norm_w': d_qnorm, 'kv_a_norm_w': d_kvnorm, 'ffn_norm_w': d_ffnnorm,
        'ffn_conv_b': _deinterleave(d_fconv_b), 'ple_norm_w': d_plenorm, 'b_ple_gate': d_bgate,
        'ple_post_norm_w': d_post, 'final_norm_w': d_final,
    }
    return loss, dx, conv, vec


MESH = pl.DeviceIdType.MESH
FLIPS = ((0, 0, 1), (1, 0, 0), (0, 1, 0), (1, 1, 0), (1, 0, 1), (0, 1, 1), (1, 1, 1))


def _exchange(items, *, gather, name):
    n = len(items)

    def body(*refs):
        ins, outs = refs[:n], refs[n:2 * n]
        send_sems, recv_sems, local_sems = refs[2 * n:]
        x, y, c = lax.axis_index("x"), lax.axis_index("y"), lax.axis_index("c")
        me = 4 * x + 2 * y + c
        peers = [(jnp.where(fx, 1 - x, x), jnp.where(fy, 1 - y, y), jnp.where(fc, 1 - c, c)) for fx, fy, fc in FLIPS]
        slot = [4 * px + 2 * py + pc for px, py, pc in peers]
        local, sends = [], []
        for wi in range(n):
            cp = pltpu.make_async_copy(ins[wi] if gather else ins[wi].at[me], outs[wi].at[me], local_sems.at[wi])
            cp.start()
            local.append(cp)
            for k, peer in enumerate(peers):
                cp = pltpu.make_async_remote_copy(
                    src_ref=ins[wi] if gather else ins[wi].at[slot[k]], dst_ref=outs[wi].at[me],
                    send_sem=send_sems.at[k, wi], recv_sem=recv_sems.at[k, wi], device_id=peer, device_id_type=MESH)
                cp.start()
                sends.append(cp)
        for wi in range(n):
            for k, peer in enumerate(peers):
                pltpu.make_async_remote_copy(
                    src_ref=outs[wi].at[slot[k]], dst_ref=outs[wi].at[slot[k]], send_sem=send_sems.at[k, wi],
                    recv_sem=recv_sems.at[k, wi], device_id=peer, device_id_type=MESH).wait_recv()
        for cp in sends:
            cp.wait_send()
        for cp in local:
            cp.wait()

    hbm = pl.BlockSpec(memory_space=pltpu.HBM)
    out_shape = [jax.ShapeDtypeStruct(((N_DEV,) + v.shape) if gather else v.shape, v.dtype) for v in items]
    return pl.pallas_call(
        body, name=name, in_specs=[hbm] * n, out_specs=[hbm] * n, out_shape=out_shape,
        scratch_shapes=[pltpu.SemaphoreType.DMA((len(FLIPS), n)), pltpu.SemaphoreType.DMA((len(FLIPS), n)),
                        pltpu.SemaphoreType.DMA((n,))],
    )(*items)


HBM_SPEC = pl.BlockSpec(memory_space=pltpu.HBM)
SEM_SPEC = pl.BlockSpec(memory_space=pltpu.SEMAPHORE)
EFFECT = pltpu.SideEffectType.DATAFLOW_SIDE_EFFECTING


def _peers():
    x, y, c = lax.axis_index("x"), lax.axis_index("y"), lax.axis_index("c")
    peers = [(jnp.where(fx, 1 - x, x), jnp.where(fy, 1 - y, y), jnp.where(fc, 1 - c, c)) for fx, fy, fc in FLIPS]
    return 4 * x + 2 * y + c, peers, [4 * px + 2 * py + pc for px, py, pc in peers]


def _split_start(bufs, ncopies, plan, *, name):
    nb = len(bufs)

    def body(*refs):
        send_sems, recv_sems, token = refs[nb], refs[nb + 1], refs[2 * nb + 2]
        for i, (src, dst, peer, _) in enumerate(plan(refs[:nb])):
            pltpu.make_async_remote_copy(src_ref=src, dst_ref=dst, send_sem=send_sems.at[i], recv_sem=recv_sems.at[i],
                                         device_id=peer, device_id_type=MESH).start()
        token[...] = jnp.zeros_like(token)

    res = pl.pallas_call(
        body, name=name, in_specs=[HBM_SPEC] * nb,
        out_specs=[SEM_SPEC, SEM_SPEC] + [HBM_SPEC] * nb + [pl.BlockSpec(memory_space=pltpu.VMEM)],
        out_shape=[pltpu.SemaphoreType.DMA((ncopies,)), pltpu.SemaphoreType.DMA((ncopies,))]
        + [pltpu.HBM(v.shape, v.dtype) for v in bufs] + [jax.ShapeDtypeStruct((HALO, LANES), F32)],
        input_output_aliases={i: 2 + i for i in range(nb)},
        compiler_params=pltpu.CompilerParams(has_side_effects=EFFECT),
    )(*[pltpu.with_memory_space_constraint(v, pltpu.HBM) for v in bufs])
    return (res[0], res[1], list(res[2:2 + nb])), res[2 + nb]


def _split_wait(started, after, plan, local_plan, *, name):
    send_sems, recv_sems, bufs = started
    nb = len(bufs)
    nlocal = len(local_plan(bufs))

    def body(*refs):
        send_sems, recv_sems = refs[nb], refs[nb + 1]
        local_sems = refs[2 * nb + 3]
        local = []
        for j, (src, dst) in enumerate(local_plan(refs[:nb])):
            cp = pltpu.make_async_copy(src, dst, local_sems.at[j])
            cp.start()
            local.append(cp)
        for i, (src, _, peer, incoming) in enumerate(plan(refs[:nb])):
            cp = pltpu.make_async_remote_copy(src_ref=src, dst_ref=incoming, send_sem=send_sems.at[i],
                                              recv_sem=recv_sems.at[i], device_id=peer, device_id_type=MESH)
            cp.wait_send()
            cp.wait_recv()
        for cp in local:
            cp.wait()

    res = pl.pallas_call(
        body, name=name, in_specs=[HBM_SPEC] * nb + [SEM_SPEC, SEM_SPEC, pl.BlockSpec(memory_space=pl.ANY)],
        out_specs=[HBM_SPEC] * nb, out_shape=[pltpu.HBM(v.shape, v.dtype) for v in bufs],
        input_output_aliases={i: i for i in range(nb)},
        scratch_shapes=[pltpu.SemaphoreType.DMA((max(nlocal, 1),))],
        compiler_params=pltpu.CompilerParams(has_side_effects=EFFECT),
    )(*bufs, send_sems, recv_sems, after)
    return list(res)


def _place():
    x, y, c = lax.axis_index("x"), lax.axis_index("y"), lax.axis_index("c")
    others = [((1 - x, y, c), 2 * (1 - x) + y), ((x, 1 - y, c), 2 * x + 1 - y), ((1 - x, 1 - y, c), 2 * (1 - x) + 1 - y)]
    return 4 * x + 2 * y + c, 2 * x + y, c, (x, y, 1 - c), others


def _gather1_plan(n):
    def plan(refs):
        me, _, _, sibling, others = _place()
        out = []
        for wi in range(n):
            item, land = refs[wi], refs[n + wi]
            out.append((item, land.at[me], sibling, land.at[me + 1 - 2 * lax.axis_index("c")]))
            for peer, chip in others:
                out.append((item, land.at[me], peer, land.at[2 * chip + lax.axis_index("c")]))
        return out

    return plan


def _gather1_local(n):
    def plan(refs):
        me = _place()[0]
        return [(refs[wi], refs[n + wi].at[me]) for wi in range(n)]

    return plan


def _gather2_plan(n):
    def plan(refs):
        _, _, c, sibling, others = _place()
        out = []
        for wi in range(n):
            land = refs[wi]
            for _, chip in others:
                out.append((land.at[2 * chip + c], land.at[2 * chip + c], sibling, land.at[2 * chip + 1 - c]))
        return out

    return plan


def _gather_start(items, *, name):
    lands = [lax.empty((N_DEV,) + v.shape, v.dtype) for v in items]
    return _split_start(items + lands, 4 * len(items), _gather1_plan(len(items)), name=name)


def _gather_forward(started, after, *, name):
    n = len(started[2]) // 2
    bufs = _split_wait(started, after, _gather1_plan(n), _gather1_local(n), name=name + "_wait")
    return _split_start(bufs[n:], 3 * n, _gather2_plan(n), name=name + "_start")


def _gather_finish(started, after, *, name):
    n = len(started[2])
    return _split_wait(started, after, _gather2_plan(n), lambda refs: [], name=name)


N_CHIP = N_DEV // 2


def _scatter1_plan(n):
    def plan(refs):
        _, _, c, sibling, _ = _place()
        out = []
        for wi in range(n):
            parts, half = refs[wi], refs[n + wi]
            for chip in range(N_CHIP):
                out.append((parts.at[2 * chip + 1 - c], half.at[chip], sibling, half.at[chip]))
        return out

    return plan


def _scatter2_plan(n):
    def plan(refs):
        _, my_chip, _, _, others = _place()
        out = []
        for wi in range(n):
            sums, recv = refs[wi], refs[n + wi]
            for peer, chip in others:
                out.append((sums.at[chip], recv.at[my_chip], peer, recv.at[chip]))
        return out

    return plan


def _scatter2_local(n):
    def plan(refs):
        my_chip = _place()[1]
        return [(refs[wi].at[my_chip], refs[n + wi].at[my_chip]) for wi in range(n)]

    return plan


def _pair_add(parts, half, core, *, name):
    _, r, c = parts.shape
    tr = max(d for d in range(HALO, 257, HALO) if r % d == 0) if r > 256 else r
    parts4 = parts.reshape(N_CHIP, 2, r, c)

    def body(core_ref, p_ref, h_ref, o_ref):
        o_ref[...] = (p_ref[:, 0].astype(F32) + h_ref[...].astype(F32)).astype(o_ref.dtype)

    return pl.pallas_call(
        body, name=name,
        grid_spec=pltpu.PrefetchScalarGridSpec(
            num_scalar_prefetch=1, grid=(r // tr,),
            in_specs=[pl.BlockSpec((N_CHIP, 1, tr, c), lambda i, core_ref: (0, core_ref[0], i, 0)),
                      pl.BlockSpec((N_CHIP, tr, c), lambda i, core_ref: (0, i, 0))],
            out_specs=pl.BlockSpec((N_CHIP, tr, c), lambda i, core_ref: (0, i, 0))),
        out_shape=jax.ShapeDtypeStruct((N_CHIP, r, c), parts.dtype), compiler_params=_cp("parallel"),
    )(core, parts4, half)


def _scatter_start(parts, *, name):
    halves = [lax.empty((N_CHIP,) + v.shape[1:], v.dtype) for v in parts]
    return _split_start(parts + halves, N_CHIP * len(parts), _scatter1_plan(len(parts)), name=name)


def _scatter_forward(started, after, core, *, name):
    n = len(started[2]) // 2
    bufs = _split_wait(started, after, _scatter1_plan(n), lambda refs: [], name=name + "_wait")
    sums = [_pair_add(bufs[wi], bufs[n + wi], core, name=name + "_add%d" % wi) for wi in range(n)]
    recvs = [lax.empty(v.shape, v.dtype) for v in sums]
    return _split_start(sums + recvs, 3 * n, _scatter2_plan(n), name=name + "_start")


def _scatter_finish(started, after, *, name):
    n = len(started[2]) // 2
    return _split_wait(started, after, _scatter2_plan(n), _scatter2_local(n), name=name)[n:]


def _adamw(parts, w, m, v, *, name):
    r, c = w.shape
    nparts = parts.shape[0]
    tr = max(d for d in range(HALO, 129, HALO) if r % d == 0) if r > 128 else r

    def body(p_ref, w_ref, m_ref, v_ref, g_ref, d_ref, mo_ref, vo_ref):
        g = p_ref[0].astype(F32)
        for k in range(1, nparts):
            g = g + p_ref[k].astype(F32)
        mn = ADAM_B1 * m_ref[...] + (1.0 - ADAM_B1) * g
        vn = ADAM_B2 * v_ref[...] + (1.0 - ADAM_B2) * (g * g)
        m_hat = mn / (1.0 - ADAM_B1 ** ADAM_STEP)
        v_hat = vn / (1.0 - ADAM_B2 ** ADAM_STEP)
        g_ref[...] = g
        d_ref[...] = -ADAM_LR * (m_hat / (jnp.sqrt(v_hat) + ADAM_EPS) + ADAM_WD * w_ref[...])
        mo_ref[...] = mn
        vo_ref[...] = vn

    blk = pl.BlockSpec((tr, c), lambda i: (i, 0))
    return pl.pallas_call(
        body, name=name, grid=(r // tr,), in_specs=[pl.BlockSpec((nparts, tr, c), lambda i: (0, i, 0)), blk, blk, blk],
        out_specs=[blk] * 4, out_shape=[jax.ShapeDtypeStruct((r, c), F32)] * 4, compiler_params=_cp("parallel"),
    )(parts, w, m, v)


def _pack_rows(vs, rows):
    lead = vs[0].shape[:-1] if vs[0].ndim > 1 else ()
    flat = jnp.concatenate(vs, axis=-1)
    pad = rows * LANES - flat.shape[-1]
    flat = jnp.pad(flat, [(0, 0)] * len(lead) + [(0, pad)])
    return flat.reshape(lead + (rows, LANES))


def kernel(x, p, positions, mix_norm_w, w_in, conv_w, conv_b, dt_bias, a_log, d_skip, ssd_norm_w, q_a_norm_w, w_q_b, kv_a_norm_w, w_kv_b, w_out, ffn_norm_w, w_ffn_up, ffn_conv_w, ffn_conv_b, w_ffn_down, ple_norm_w, w_ple_gate, b_ple_gate, w_ple_proj, ple_post_norm_w, final_norm_w, loss_target, m_mix_norm_w, m_w_in, m_conv_w, m_conv_b, m_dt_bias, m_a_log, m_d_skip, m_ssd_norm_w, m_q_a_norm_w, m_w_q_b, m_kv_a_norm_w, m_w_kv_b, m_w_out, m_ffn_norm_w, m_w_ffn_up, m_ffn_conv_w, m_ffn_conv_b, m_w_ffn_down, m_ple_norm_w, m_w_ple_gate, m_b_ple_gate, m_w_ple_proj, m_ple_post_norm_w, m_final_norm_w, v_mix_norm_w, v_w_in, v_conv_w, v_conv_b, v_dt_bias, v_a_log, v_d_skip, v_ssd_norm_w, v_q_a_norm_w, v_w_q_b, v_kv_a_norm_w, v_w_kv_b, v_w_out, v_ffn_norm_w, v_w_ffn_up, v_ffn_conv_w, v_ffn_conv_b, v_w_ffn_down, v_ple_norm_w, v_w_ple_gate, v_b_ple_gate, v_w_ple_proj, v_ple_post_norm_w, v_final_norm_w):
    given = dict(locals())
    shapes = {n: given[n].shape for n in WEIGHTS}
    w2 = {n: given[n].reshape(given[n].shape[-2:] if n in BIG or n in CONV else (1, -1)) for n in WEIGHTS}
    m2 = {n: given['m_' + n].reshape(w2[n].shape) for n in WEIGHTS}
    v2 = {n: given['v_' + n].reshape(w2[n].shape) for n in WEIGHTS}
    me = 4 * lax.axis_index("x") + 2 * lax.axis_index("y") + lax.axis_index("c")

    core = lax.axis_index("c").astype(jnp.int32).reshape(1)

    def shards(grp, zero):
        return [(w2[n] + zero).astype(BF16) if n in BIG else w2[n] + zero for n in WEIGHT_GROUPS[grp]]

    first, token = _gather_start(shards('a', 0.0), name="gather_a_hop1")
    first, token = _gather_forward(first, token, name="gather_a_hop2")
    zero = token[0, 0]
    gathers = {}
    for grp in ('b', 'c'):
        gathers[grp], token = _gather_start(shards(grp, zero), name="gather_" + grp + "_hop1")
        zero = zero + token[0, 0]

    def get_w(grp, after):
        if grp == 'a':
            lands = _gather_finish(first, token, name="gather_a_done")
        else:
            if grp == 'b':
                for g in ('b', 'c'):
                    gathers[g], after = _gather_forward(gathers[g], after, name="gather_" + g + "_hop2")
            lands = _gather_finish(gathers[grp], after, name="gather_" + grp + "_done")
        return _assemble_weights(dict(zip(WEIGHT_GROUPS[grp], lands)))

    scatters = {}

    def emit(grp, grads):
        scatters[grp], tok = _scatter_start([grads[n] for n in GRAD_GROUPS[grp]], name="scatter_" + grp + "_hop1")
        return tok[0, 0]

    def relay(grp, after):
        scatters[grp], tok = _scatter_forward(scatters[grp], after, core, name="scatter_" + grp + "_hop2")
        return tok[0, 0]

    vecs = {n: w2[n] for n in REPL}
    vecs['mix_norm_w'] = vecs['mix_norm_w'] + zero
    loss, dx, g_conv, g_vec = _local_step(x[0], p[0, 0], _rope_tables(positions), get_w, vecs, loss_target[0], emit,
                                          relay)
    n_small = sum(g_vec[n].shape[1] for n in REPL) + sum(g_conv[n].size for n in CONV) + 1
    rows_small = -(-n_small // (LANES * HALO)) * HALO
    small = _pack_rows([g_vec[n] for n in REPL] + [g_conv[n].reshape(1, -1) for n in CONV] + [loss], rows_small)
    all_small = _exchange([small], gather=True, name="gather_small_grads")[0].reshape(N_DEV, rows_small * LANES)

    out_g, out_d, out_m, out_v = {}, {}, {}, {}
    for grp, names in GRAD_GROUPS.items():
        received = _scatter_finish(scatters[grp], dx, name="scatter_" + grp + "_done")
        for n, parts in zip(names, received):
            out_g[n], out_d[n], out_m[n], out_v[n] = _adamw(parts, w2[n], m2[n], v2[n], name="adamw_" + n)
    pieces, off = [], 0
    for n in REPL:
        k = g_vec[n].shape[1]
        pieces.append(all_small[:, off:off + k])
        off += k
    for n in CONV:
        kw, cols = g_conv[n].shape
        full = all_small[:, off:off + kw * cols].reshape(N_DEV, kw, cols)
        mine = lax.dynamic_slice_in_dim(full, me * (cols // N_DEV), cols // N_DEV, axis=2)
        pieces.append(mine.reshape(N_DEV, kw * (cols // N_DEV)))
        off += kw * cols
    pieces.append(all_small[:, off:off + 1])
    small_names = REPL + CONV
    n_mine = sum(q.shape[1] for q in pieces)
    rows_mine = -(-n_mine // (LANES * HALO)) * HALO
    zero = jnp.zeros((1, 1), F32)
    packed = [_pack_rows([src[n].reshape(1, -1) for n in small_names] + [zero], rows_mine).reshape(rows_mine, LANES)
              for src in (w2, m2, v2)]
    sg, sd, sm, sv = _adamw(_pack_rows(pieces, rows_mine), *packed, name="adamw_small")
    off = 0
    for n in small_names:
        k = w2[n].size
        for dst, src in ((out_g, sg), (out_d, sd), (out_m, sm), (out_v, sv)):
            dst[n] = src.reshape(-1)[off:off + k].reshape(w2[n].shape)
        off += k
    total_loss = sg.reshape(-1)[off]

    outs = [total_loss, dx[None]]
    for res in (out_g, out_d, out_m, out_v):
        outs += [res[n].reshape(shapes[n]) for n in WEIGHTS]
    return tuple(outs)
```

```python
import functools
import math

import numpy as np
import jax
import jax.numpy as jnp
from jax import lax
from jax.experimental import pallas as pl
from jax.experimental.pallas import tpu as pltpu
from jax.experimental.pallas import tpu_sc as plsc

F32 = jnp.float32
BF16 = jnp.bfloat16
HI = lax.Precision.HIGHEST

D_MODEL = 2048
CHUNK = 64
D_SSM = 1024
SSD_P = 64
SSD_HEADS = 16
SSD_GROUPS = 2
SSD_N = 128
SSD_CONV = 4
SSD_CONV_DIM = D_SSM + 2 * SSD_GROUPS * SSD_N
MLA_HEADS = 8
MLA_NOPE = 128
MLA_ROPE = 64
MLA_V = 128
MLA_Q_RANK = 512
MLA_KV_RANK = 256
MLA_QK_PAD = 256
ROPE_THETA = 10000.0
D_FF = 5632
FFN_CONV = 3
PLE_DIM = 256
NORM_EPS = 1e-6
ADAM_LR, ADAM_B1, ADAM_B2, ADAM_EPS, ADAM_WD, ADAM_STEP = 0.001, 0.9, 0.999, 1e-08, 0.01, 10
N_DEV = 8

OFF_Z, OFF_XBC, OFF_QA, OFF_CKV, OFF_KR, OFF_DT, D_IN_PAD = 0, 1024, 2560, 3072, 3328, 3456, 3584
D_IN = 3408
LANES = 128
HALO = 8
VMEM_LIMIT = 56 * 1024 * 1024
FFN_TC = D_FF * 2 // N_DEV
FFN_PERM = (0, 4, 1, 5, 2, 6, 3, 7)
NEG = -1e30


def _cp(*sem):
    return pltpu.CompilerParams(dimension_semantics=tuple(sem), vmem_limit_bytes=VMEM_LIMIT)


def _tile(n, want):
    if n <= want:
        return n
    best = max(d for d in range(LANES, want + 1, LANES) if n % d == 0)
    return best


def _sigmoid(x):
    return 1.0 / (1.0 + jnp.exp(-x))


def _silu(x):
    return x * _sigmoid(x)


def _dsilu(x):
    s = _sigmoid(x)
    return s * (1.0 + x * (1.0 - s))


MM_TILE = 1408
MM_TK = 2816


def _matmul(a, b, *, ta=False, tb=False, out_dtype=F32, add=None, bias=None, tm=MM_TILE, tn=MM_TILE, tk=MM_TK, name,
            mnk=None, a_spec=None, b_spec=None, o_spec=None, o_shape=None):
    if mnk is None:
        m, k = (a.shape[1], a.shape[0]) if ta else a.shape
        n = b.shape[0] if tb else b.shape[1]
        assert k == (b.shape[1] if tb else b.shape[0])
    else:
        m, n, k = mnk
    tm, tn, tk = _tile(m, tm), _tile(n, tn), _tile(k, tk)
    nk = k // tk
    dims = (((0 if ta else 1,), (1 if tb else 0,)), ((), ()))

    def body(*refs):
        a_ref, b_ref = refs[0], refs[1]
        pos = 2
        add_ref = bias_ref = None
        if add is not None:
            add_ref = refs[pos]
            pos += 1
        if bias is not None:
            bias_ref = refs[pos]
            pos += 1
        o_ref = refs[pos]
        kk = pl.program_id(2)
        av = a_ref[...]
        bv = b_ref[...]
        av = av.reshape(av.shape[-2:]).astype(BF16)
        bv = bv.reshape(bv.shape[-2:]).astype(BF16)
        prod = lax.dot_general(av, bv, dims, preferred_element_type=F32)

        def finish(r):
            if bias_ref is not None:
                r = r + bias_ref[...]
            if add_ref is not None:
                r = r + add_ref[...].astype(F32)
            o_ref[...] = r.astype(out_dtype).reshape(o_ref.shape)

        if nk == 1:
            finish(prod)
        else:
            acc_ref = refs[pos + 1]

            @pl.when(kk == 0)
            def _():
                acc_ref[...] = prod

            @pl.when(kk > 0)
            def _():
                acc_ref[...] += prod

            @pl.when(kk == nk - 1)
            def _():
                finish(acc_ref[...])

    if a_spec is None:
        a_spec = (pl.BlockSpec((tk, tm), lambda i, j, kk: (kk, i)) if ta
                  else pl.BlockSpec((tm, tk), lambda i, j, kk: (i, kk)))
    if b_spec is None:
        b_spec = (pl.BlockSpec((tn, tk), lambda i, j, kk: (j, kk)) if tb
                  else pl.BlockSpec((tk, tn), lambda i, j, kk: (kk, j)))
    if o_spec is None:
        o_spec = pl.BlockSpec((tm, tn), lambda i, j, kk: (i, j))
    if o_shape is None:
        o_shape = (m, n)
    in_specs = [a_spec, b_spec]
    args = [a, b]
    if add is not None:
        in_specs.append(pl.BlockSpec((tm, tn), lambda i, j, kk: (i, j)))
        args.append(add)
    if bias is not None:
        in_specs.append(pl.BlockSpec((1, tn), lambda i, j, kk: (0, j)))
        args.append(bias)
    return pl.pallas_call(
        body, name=name, grid=(m // tm, n // tn, nk), in_specs=in_specs, out_specs=o_spec,
        out_shape=jax.ShapeDtypeStruct(o_shape, out_dtype),
        scratch_shapes=[pltpu.VMEM((tm, tn), F32)] if nk > 1 else [],
        compiler_params=_cp("parallel", "parallel", "arbitrary"),
    )(*args)


def _rmsnorm_fwd(x, w, *, width, cblk=0, out_dtype=BF16, tr=256, name):
    t = x.shape[0]

    def body(x_ref, w_ref, o_ref):
        xv = x_ref[...].astype(F32)
        r = lax.rsqrt(jnp.mean(xv * xv, axis=-1, keepdims=True) + NORM_EPS)
        o_ref[...] = (xv * r * w_ref[...]).astype(out_dtype)

    return pl.pallas_call(
        body, name=name, grid=(t // tr,),
        in_specs=[pl.BlockSpec((tr, width), lambda i: (i, cblk)), pl.BlockSpec((1, width), lambda i: (0, 0))],
        out_specs=pl.BlockSpec((tr, width), lambda i: (i, 0)),
        out_shape=jax.ShapeDtypeStruct((t, width), out_dtype),
        compiler_params=_cp("parallel"),
    )(x, w)


def _rmsnorm_bwd(x, w, dy, add=None, *, width, cblk=0, out_dtype=F32, tr=256, name):
    t = x.shape[0]

    def body(*refs):
        if add is None:
            x_ref, w_ref, dy_ref, dx_ref, dw_ref = refs
            add_ref = None
        else:
            x_ref, w_ref, dy_ref, add_ref, dx_ref, dw_ref = refs
        xv = x_ref[...].astype(F32)
        dyv = dy_ref[...].astype(F32)
        r = lax.rsqrt(jnp.mean(xv * xv, axis=-1, keepdims=True) + NORM_EPS)
        xh = xv * r
        g = dyv * w_ref[...]
        dx = r * (g - xh * jnp.mean(g * xh, axis=-1, keepdims=True))
        if add_ref is not None:
            dx = dx + add_ref[...].astype(F32)
        dx_ref[...] = dx.astype(out_dtype)

        @pl.when(pl.program_id(0) == 0)
        def _():
            dw_ref[...] = jnp.zeros_like(dw_ref)

        dw_ref[...] += jnp.sum(dyv * xh, axis=0, keepdims=True)

    in_specs = [pl.BlockSpec((tr, width), lambda i: (i, cblk)), pl.BlockSpec((1, width), lambda i: (0, 0)),
                pl.BlockSpec((tr, width), lambda i: (i, 0))]
    args = [x, w, dy]
    if add is not None:
        in_specs.append(pl.BlockSpec((tr, width), lambda i: (i, 0)))
        args.append(add)
    return pl.pallas_call(
        body, name=name, grid=(t // tr,), in_specs=in_specs,
        out_specs=[pl.BlockSpec((tr, width), lambda i: (i, 0)), pl.BlockSpec((1, width), lambda i: (0, 0))],
        out_shape=[jax.ShapeDtypeStruct((t, width), out_dtype), jax.ShapeDtypeStruct((1, width), F32)],
        compiler_params=_cp("arbitrary"),
    )(*args)


def _shift_down(prev_halo, cur, j):
    if j == 0:
        return cur
    ext = jnp.concatenate([prev_halo, cur], axis=0)
    return pltpu.roll(ext, j, axis=0)[HALO:]


def _shift_up(cur, next_halo, j):
    if j == 0:
        return cur
    ext = jnp.concatenate([cur, next_halo], axis=0)
    return pltpu.roll(ext, ext.shape[0] - j, axis=0)[:cur.shape[0]]


def _conv_rows(prev, cur, w, b, kw):
    shifted = [cur]
    out = b + w[kw - 1:kw] * cur
    for j in range(1, kw):
        sh = _shift_down(prev, cur, j)
        shifted.append(sh)
        out = out + w[kw - 1 - j:kw - j] * sh
    return out, shifted


def _act_fwd(c, glu):
    if glu:
        half = c.shape[1] // 2
        return _silu(c[:, :half]) * c[:, half:]
    return _silu(c)


def _act_bwd(c, dout, glu):
    if glu:
        half = c.shape[1] // 2
        g, up = c[:, :half], c[:, half:]
        return jnp.concatenate([dout * up * _dsilu(g), dout * _silu(g)], axis=1)
    return dout * _dsilu(c)


def _conv_act_fwd(u, w, b, *, kw, glu, tc, coff, ncols, out_dtype, tr=256, name):
    t = u.shape[0]
    nb = ncols // tc
    oc = tc // 2 if glu else tc

    def body(u_ref, uh_ref, w_ref, b_ref, o_ref):
        prev = jnp.where(pl.program_id(0) == 0, 0.0, uh_ref[...])
        c, _ = _conv_rows(prev, u_ref[...], w_ref[...], b_ref[...], kw)
        o_ref[...] = _act_fwd(c, glu).astype(out_dtype)

    return pl.pallas_call(
        body, name=name, grid=(t // tr, nb),
        in_specs=[pl.BlockSpec((tr, tc), lambda i, j: (i, j + coff)),
                  pl.BlockSpec((HALO, tc), lambda i, j: (jnp.maximum(i * (tr // HALO) - 1, 0), j + coff)),
                  pl.BlockSpec((kw, tc), lambda i, j: (0, j)), pl.BlockSpec((1, tc), lambda i, j: (0, j))],
        out_specs=pl.BlockSpec((tr, oc), lambda i, j: (i, j)),
        out_shape=jax.ShapeDtypeStruct((t, nb * oc), out_dtype),
        compiler_params=_cp("parallel", "parallel"),
    )(u, u, w, b)


def _conv_act_bwd(u, w, b, dout, *, kw, glu, tc, coff, ncols, tr=256, name):
    t = u.shape[0]
    nb = ncols // tc
    nt = t // tr
    oc = tc // 2 if glu else tc

    def body(u_ref, up_ref, un_ref, d_ref, dn_ref, w_ref, b_ref, du_ref, dw_ref, db_ref):
        i = pl.program_id(1)
        cur, nxt, wv, bv = u_ref[...], un_ref[...], w_ref[...], b_ref[...]
        prev = jnp.where(i == 0, 0.0, up_ref[...])
        c_cur, shifted = _conv_rows(prev, cur, wv, bv, kw)
        c_nxt, _ = _conv_rows(cur[tr - HALO:], nxt, wv, bv, kw)
        d_cur = _act_bwd(c_cur, d_ref[...].astype(F32), glu)
        d_nxt = _act_bwd(c_nxt, jnp.where(i == nt - 1, 0.0, dn_ref[...].astype(F32)), glu)
        du = wv[kw - 1:kw] * d_cur
        for j in range(1, kw):
            du = du + wv[kw - 1 - j:kw - j] * _shift_up(d_cur, d_nxt, j)
        du_ref[...] = du.astype(BF16)

        @pl.when(i == 0)
        def _():
            dw_ref[...] = jnp.zeros_like(dw_ref)
            db_ref[...] = jnp.zeros_like(db_ref)

        db_ref[...] += jnp.sum(d_cur, axis=0, keepdims=True)
        dw_ref[...] += jnp.concatenate(
            [jnp.sum(d_cur * shifted[kw - 1 - k], axis=0, keepdims=True) for k in range(kw)], axis=0)

    nh = tr // HALO
    return pl.pallas_call(
        body, name=name, grid=(nb, nt),
        in_specs=[pl.BlockSpec((tr, tc), lambda j, i: (i, j + coff)),
                  pl.BlockSpec((HALO, tc), lambda j, i: (jnp.maximum(i * nh - 1, 0), j + coff)),
                  pl.BlockSpec((HALO, tc), lambda j, i: (jnp.minimum((i + 1) * nh, t // HALO - 1), j + coff)),
                  pl.BlockSpec((tr, oc), lambda j, i: (i, j)),
                  pl.BlockSpec((HALO, oc), lambda j, i: (jnp.minimum((i + 1) * nh, t // HALO - 1), j)),
                  pl.BlockSpec((kw, tc), lambda j, i: (0, j)), pl.BlockSpec((1, tc), lambda j, i: (0, j))],
        out_specs=[pl.BlockSpec((tr, tc), lambda j, i: (i, j)), pl.BlockSpec((kw, tc), lambda j, i: (0, j)),
                   pl.BlockSpec((1, tc), lambda j, i: (0, j))],
        out_shape=[jax.ShapeDtypeStruct((t, ncols), BF16), jax.ShapeDtypeStruct((kw, ncols), F32),
                   jax.ShapeDtypeStruct((1, ncols), F32)],
        compiler_params=_cp("parallel", "arbitrary"),
    )(u, u, u, dout, dout, w, b)


def _ple_fwd(x2, gl, pe, pw, *, tr=256, name):
    t, d = x2.shape

    def body(x_ref, gl_ref, pe_ref, pw_ref, o_ref):
        pv = pe_ref[...]
        r = lax.rsqrt(jnp.mean(pv * pv, axis=-1, keepdims=True) + NORM_EPS)
        o_ref[...] = x_ref[...] + _sigmoid(gl_ref[...]) * (pv * r * pw_ref[...])

    blk = pl.BlockSpec((tr, d), lambda i: (i, 0))
    return pl.pallas_call(
        body, name=name, grid=(t // tr,), in_specs=[blk, blk, blk, pl.BlockSpec((1, d), lambda i: (0, 0))],
        out_specs=blk, out_shape=jax.ShapeDtypeStruct((t, d), F32), compiler_params=_cp("parallel"),
    )(x2, gl, pe, pw)


def _ple_bwd(dx3, gl, pe, pw, *, tr=256, name):
    t, d = dx3.shape

    def body(dx_ref, gl_ref, pe_ref, pw_ref, dgl_ref, db_ref, dpe_ref, dpw_ref):
        dx, pv, pwv = dx_ref[...], pe_ref[...], pw_ref[...]
        gate = _sigmoid(gl_ref[...])
        r = lax.rsqrt(jnp.mean(pv * pv, axis=-1, keepdims=True) + NORM_EPS)
        ph = pv * r
        dgl = dx * (ph * pwv) * gate * (1.0 - gate)
        de = dx * gate
        g = de * pwv
        dgl_ref[...] = dgl.astype(BF16)
        dpe_ref[...] = (r * (g - ph * jnp.mean(g * ph, axis=-1, keepdims=True))).astype(BF16)

        @pl.when(pl.program_id(0) == 0)
        def _():
            db_ref[...] = jnp.zeros_like(db_ref)
            dpw_ref[...] = jnp.zeros_like(dpw_ref)

        db_ref[...] += jnp.sum(dgl, axis=0, keepdims=True)
        dpw_ref[...] += jnp.sum(de * ph, axis=0, keepdims=True)

    blk = pl.BlockSpec((tr, d), lambda i: (i, 0))
    row = pl.BlockSpec((1, d), lambda i: (0, 0))
    return pl.pallas_call(
        body, name=name, grid=(t // tr,), in_specs=[blk, blk, blk, row], out_specs=[blk, row, blk, row],
        out_shape=[jax.ShapeDtypeStruct((t, d), BF16), jax.ShapeDtypeStruct((1, d), F32),
                   jax.ShapeDtypeStruct((t, d), BF16), jax.ShapeDtypeStruct((1, d), F32)],
        compiler_params=_cp("arbitrary"),
    )(dx3, gl, pe, pw)


def _loss_head(x3, fw, target, *, tr=256, name):
    t, d = x3.shape

    def body(x_ref, w_ref, t_ref, l_ref, dx_ref, dw_ref):
        xv, wv = x_ref[...], w_ref[...]
        r = lax.rsqrt(jnp.mean(xv * xv, axis=-1, keepdims=True) + NORM_EPS)
        xh = xv * r
        err = xh * wv - t_ref[...]
        dy = err * (1.0 / d)
        g = dy * wv
        dx_ref[...] = r * (g - xh * jnp.mean(g * xh, axis=-1, keepdims=True))

        @pl.when(pl.program_id(0) == 0)
        def _():
            l_ref[...] = jnp.zeros_like(l_ref)
            dw_ref[...] = jnp.zeros_like(dw_ref)

        l_ref[...] += 0.5 * jnp.sum(jnp.mean(err * err, axis=-1, keepdims=True), axis=0, keepdims=True)
        dw_ref[...] += jnp.sum(dy * xh, axis=0, keepdims=True)

    blk = pl.BlockSpec((tr, d), lambda i: (i, 0))
    row = pl.BlockSpec((1, d), lambda i: (0, 0))
    return pl.pallas_call(
        body, name=name, grid=(t // tr,), in_specs=[blk, row, blk],
        out_specs=[pl.BlockSpec((1, 1), lambda i: (0, 0)), blk, row],
        out_shape=[jax.ShapeDtypeStruct((1, 1), F32), jax.ShapeDtypeStruct((t, d), F32),
                   jax.ShapeDtypeStruct((1, d), F32)],
        compiler_params=_cp("arbitrary"),
    )(x3, fw, target)


def _rope(blk, tab_ref):
    return blk * tab_ref[0] + pltpu.roll(blk, 96, axis=1) * tab_ref[1] + pltpu.roll(blk, 32, axis=1) * tab_ref[2]


def _unrope(g, tab_ref):
    return g * tab_ref[0] + pltpu.roll(g * tab_ref[1], 32, axis=1) + pltpu.roll(g * tab_ref[2], 96, axis=1)


def _mla_prep(q, kv, proj, tabs, *, tr=512, name):
    t = q.shape[0]

    def body(q_ref, kv_ref, kr_ref, tab_ref, qo_ref, ko_ref, vo_ref):
        qv, kvv = q_ref[...], kv_ref[...]
        qo_ref[0, :, :MLA_NOPE] = qv[:, :MLA_NOPE].astype(BF16)
        qo_ref[0, :, MLA_NOPE:] = _rope(qv[:, MLA_NOPE:], tab_ref).astype(BF16)
        ko_ref[0, :, :MLA_NOPE] = kvv[:, :MLA_NOPE].astype(BF16)
        ko_ref[0, :, MLA_NOPE:] = _rope(kr_ref[...], tab_ref).astype(BF16)
        vo_ref[0] = kvv[:, MLA_NOPE:].astype(BF16)

    return pl.pallas_call(
        body, name=name, grid=(t // tr, MLA_HEADS),
        in_specs=[pl.BlockSpec((tr, MLA_QK_PAD), lambda i, h: (i, h)),
                  pl.BlockSpec((tr, MLA_NOPE + MLA_V), lambda i, h: (i, h)),
                  pl.BlockSpec((tr, LANES), lambda i, h: (i, OFF_KR // LANES)),
                  pl.BlockSpec((3, tr, LANES), lambda i, h: (0, i, 0))],
        out_specs=[pl.BlockSpec((1, tr, MLA_QK_PAD), lambda i, h: (h, i, 0)),
                   pl.BlockSpec((1, tr, MLA_QK_PAD), lambda i, h: (h, i, 0)),
                   pl.BlockSpec((1, tr, MLA_V), lambda i, h: (h, i, 0))],
        out_shape=[jax.ShapeDtypeStruct((MLA_HEADS, t, MLA_QK_PAD), BF16),
                   jax.ShapeDtypeStruct((MLA_HEADS, t, MLA_QK_PAD), BF16),
                   jax.ShapeDtypeStruct((MLA_HEADS, t, MLA_V), BF16)],
        compiler_params=_cp("parallel", "parallel"),
    )(q, kv, proj, tabs)


def _mla_unprep(dq3, dk3, dv3, tabs, *, tr=256, name):
    t = dq3.shape[1]

    def body(dq_ref, dk_ref, dv_ref, tab_ref, qo_ref, kvo_ref, kro_ref):
        kr = jnp.zeros((tr, LANES), F32)
        for h in range(MLA_HEADS):
            c0 = h * MLA_QK_PAD
            qo_ref[:, c0:c0 + MLA_NOPE] = dq_ref[h, :, :MLA_NOPE].astype(BF16)
            qo_ref[:, c0 + MLA_NOPE:c0 + MLA_QK_PAD] = _unrope(dq_ref[h, :, MLA_NOPE:], tab_ref).astype(BF16)
            kvo_ref[:, c0:c0 + MLA_NOPE] = dk_ref[h, :, :MLA_NOPE].astype(BF16)
            kvo_ref[:, c0 + MLA_NOPE:c0 + MLA_QK_PAD] = dv_ref[h].astype(BF16)
            kr = kr + dk_ref[h, :, MLA_NOPE:]
        kro_ref[...] = _unrope(kr, tab_ref).astype(BF16)

    return pl.pallas_call(
        body, name=name, grid=(t // tr,),
        in_specs=[pl.BlockSpec((MLA_HEADS, tr, MLA_QK_PAD), lambda i: (0, i, 0)),
                  pl.BlockSpec((MLA_HEADS, tr, MLA_QK_PAD), lambda i: (0, i, 0)),
                  pl.BlockSpec((MLA_HEADS, tr, MLA_V), lambda i: (0, i, 0)),
                  pl.BlockSpec((3, tr, LANES), lambda i: (0, i, 0))],
        out_specs=[pl.BlockSpec((tr, MLA_HEADS * MLA_QK_PAD), lambda i: (i, 0)),
                   pl.BlockSpec((tr, MLA_HEADS * MLA_QK_PAD), lambda i: (i, 0)),
                   pl.BlockSpec((tr, LANES), lambda i: (i, 0))],
        out_shape=[jax.ShapeDtypeStruct((t, MLA_HEADS * MLA_QK_PAD), BF16),
                   jax.ShapeDtypeStruct((t, MLA_HEADS * MLA_QK_PAD), BF16),
                   jax.ShapeDtypeStruct((t, LANES), BF16)],
        compiler_params=_cp("parallel"),
    )(dq3, dk3, dv3, tabs)


ATT_BLK = 256
ATT_SCALE = 1.0 / math.sqrt(MLA_NOPE + MLA_ROPE)
_NT = (((1,), (1,)), ((), ()))
_TN = (((0,), (0,)), ((), ()))


def _att_scores(q, k, diagonal):
    s = lax.dot_general(q, k, _NT, preferred_element_type=F32) * ATT_SCALE
    if not diagonal:
        return s
    row = lax.broadcasted_iota(jnp.int32, s.shape, 0)
    col = lax.broadcasted_iota(jnp.int32, s.shape, 1)
    return jnp.where((col >> 6) <= (row >> 6), s, NEG)


def _att_rows(i):
    return pl.ds(pl.multiple_of(i * ATT_BLK, ATT_BLK), ATT_BLK)


def _attn_fwd(q3, k3, v3, *, name):
    t = q3.shape[1]
    nq = t // ATT_BLK

    def body(q_ref, k_ref, v_ref, o_ref, lse_ref):
        qi = pl.program_id(1)
        q = q_ref[0]

        def step(j, carry, diagonal=False):
            m, l, acc = carry
            s = _att_scores(q, k_ref[0, _att_rows(j), :], diagonal)
            m_new = jnp.maximum(m, jnp.max(s, axis=-1, keepdims=True))
            p = jnp.exp(s - m_new)
            alpha = jnp.exp(m - m_new)
            l = alpha * l + jnp.sum(p, axis=-1, keepdims=True)
            acc = alpha * acc + jnp.dot(p.astype(BF16), v_ref[0, _att_rows(j), :], preferred_element_type=F32)
            return m_new, l, acc

        init = (jnp.full((ATT_BLK, 1), NEG, F32), jnp.zeros((ATT_BLK, 1), F32), jnp.zeros((ATT_BLK, MLA_V), F32))
        m, l, acc = step(qi, lax.fori_loop(0, qi, step, init), diagonal=True)
        o_ref[...] = acc / l
        lse_ref[0] = m + jnp.log(l)

    return pl.pallas_call(
        body, name=name, grid=(MLA_HEADS, nq),
        in_specs=[pl.BlockSpec((1, ATT_BLK, MLA_QK_PAD), lambda h, i: (h, i, 0)),
                  pl.BlockSpec((1, t, MLA_QK_PAD), lambda h, i: (h, 0, 0)),
                  pl.BlockSpec((1, t, MLA_V), lambda h, i: (h, 0, 0))],
        out_specs=[pl.BlockSpec((ATT_BLK, MLA_V), lambda h, i: (i, h)),
                   pl.BlockSpec((1, ATT_BLK, 1), lambda h, i: (h, i, 0))],
        out_shape=[jax.ShapeDtypeStruct((t, MLA_HEADS * MLA_V), F32), jax.ShapeDtypeStruct((MLA_HEADS, t, 1), F32)],
        compiler_params=_cp("parallel", "parallel"),
    )(q3, k3, v3)


def _attn_bwd(q3, k3, v3, o, dcat, lse, *, name):
    t = q3.shape[1]
    nq = t // ATT_BLK

    def body(q_ref, k_ref, v_ref, o_ref, do_ref, lse_ref, dq_ref, dk_ref, dv_ref, delta_ref):
        kj = pl.program_id(1)
        k, v = k_ref[0], v_ref[0]

        @pl.when(kj == 0)
        def _():
            dq_ref[...] = jnp.zeros_like(dq_ref)
            delta_ref[...] = jnp.sum(o_ref[...] * do_ref[...], axis=-1, keepdims=True)

        def step(i, carry, diagonal=False):
            dk, dv = carry
            rows = _att_rows(i)
            q = q_ref[0, rows, :]
            dob = do_ref[rows, :].astype(BF16)
            p = jnp.exp(_att_scores(q, k, diagonal) - lse_ref[0, rows, :])
            dv = dv + lax.dot_general(p.astype(BF16), dob, _TN, preferred_element_type=F32)
            dp = lax.dot_general(dob, v, _NT, preferred_element_type=F32)
            ds = (p * (dp - delta_ref[rows, :]) * ATT_SCALE).astype(BF16)
            dk = dk + lax.dot_general(ds, q, _TN, preferred_element_type=F32)
            dq_ref[0, rows, :] += jnp.dot(ds, k, preferred_element_type=F32)
            return dk, dv

        init = (jnp.zeros((ATT_BLK, MLA_QK_PAD), F32), jnp.zeros((ATT_BLK, MLA_V), F32))
        dk, dv = lax.fori_loop(kj + 1, nq, step, step(kj, init, diagonal=True))
        dk_ref[0] = dk
        dv_ref[0] = dv

    return pl.pallas_call(
        body, name=name, grid=(MLA_HEADS, nq),
        in_specs=[pl.BlockSpec((1, t, MLA_QK_PAD), lambda h, j: (h, 0, 0)),
                  pl.BlockSpec((1, ATT_BLK, MLA_QK_PAD), lambda h, j: (h, j, 0)),
                  pl.BlockSpec((1, ATT_BLK, MLA_V), lambda h, j: (h, j, 0)),
                  pl.BlockSpec((t, MLA_V), lambda h, j: (0, h)),
                  pl.BlockSpec((t, MLA_V), lambda h, j: (0, MLA_HEADS + h)),
                  pl.BlockSpec((1, t, 1), lambda h, j: (h, 0, 0))],
        out_specs=[pl.BlockSpec((1, t, MLA_QK_PAD), lambda h, j: (h, 0, 0)),
                   pl.BlockSpec((1, ATT_BLK, MLA_QK_PAD), lambda h, j: (h, j, 0)),
                   pl.BlockSpec((1, ATT_BLK, MLA_V), lambda h, j: (h, j, 0))],
        out_shape=[jax.ShapeDtypeStruct((MLA_HEADS, t, MLA_QK_PAD), F32),
                   jax.ShapeDtypeStruct((MLA_HEADS, t, MLA_QK_PAD), F32),
                   jax.ShapeDtypeStruct((MLA_HEADS, t, MLA_V), F32)],
        scratch_shapes=[pltpu.VMEM((t, 1), F32)],
        compiler_params=_cp("parallel", "arbitrary"),
    )(q3, k3, v3, o, dcat, lse)


def _ssd_prep(proj, bias128, alog128, *, name):
    t = proj.shape[0]
    nc = t // CHUNK

    def body(raw_ref, b_ref, al_ref, dt_ref, cs_ref, a_ref):
        xv = raw_ref[...] + b_ref[...]
        dt = jnp.maximum(xv, 0.0) + jnp.log(1.0 + jnp.exp(-jnp.abs(xv)))
        a = -jnp.exp(al_ref[...])
        adt = (dt * a).reshape(nc, CHUNK, LANES)
        li = lax.broadcasted_iota(jnp.int32, (nc, CHUNK, CHUNK), 1)
        si = lax.broadcasted_iota(jnp.int32, (nc, CHUNK, CHUNK), 2)
        tril = jnp.where(si <= li, 1.0, 0.0).astype(F32)
        cs = lax.dot_general(tril, adt, (((2,), (1,)), ((0,), (0,))), precision=HI, preferred_element_type=F32)
        dt_ref[...] = dt
        cs_ref[...] = cs.reshape(t, LANES)
        a_ref[...] = a

    blk = pl.BlockSpec((t, LANES), lambda i: (0, 0))
    row = pl.BlockSpec((1, LANES), lambda i: (0, 0))
    return pl.pallas_call(
        body, name=name, grid=(1,),
        in_specs=[pl.BlockSpec((t, LANES), lambda i: (0, OFF_DT // LANES)), row, row],
        out_specs=[blk, blk, row],
        out_shape=[jax.ShapeDtypeStruct((t, LANES), F32), jax.ShapeDtypeStruct((t, LANES), F32),
                   jax.ShapeDtypeStruct((1, LANES), F32)],
        compiler_params=_cp("arbitrary"),
    )(proj, bias128, alog128)


def _ssd_prep_bwd(ddt128, dadt128, proj, bias128, dt128, a128, dd_h, *, name):
    t = proj.shape[0]

    def body(ddt_ref, dadt_ref, raw_ref, b_ref, dt_ref, a_ref, dd_ref, draw_ref, db_ref, dal_ref, dds_ref):
        draw = ddt_ref[...] * _sigmoid(raw_ref[...] + b_ref[...])
        draw_ref[...] = draw.astype(BF16)
        db_ref[...] = jnp.sum(draw, axis=0, keepdims=True)
        dal_ref[...] = jnp.sum(dadt_ref[...] * dt_ref[...], axis=0, keepdims=True) * a_ref[...]
        dds_ref[...] = jnp.sum(dd_ref[...], axis=-1, keepdims=True)

    blk = pl.BlockSpec((t, LANES), lambda i: (0, 0))
    row = pl.BlockSpec((1, LANES), lambda i: (0, 0))
    return pl.pallas_call(
        body, name=name, grid=(1,),
        in_specs=[blk, blk, pl.BlockSpec((t, LANES), lambda i: (0, OFF_DT // LANES)), row, blk, row,
                  pl.BlockSpec((SSD_HEADS, SSD_P), lambda i: (0, 0))],
        out_specs=[blk, row, row, pl.BlockSpec((SSD_HEADS, 1), lambda i: (0, 0))],
        out_shape=[jax.ShapeDtypeStruct((t, LANES), BF16), jax.ShapeDtypeStruct((1, LANES), F32),
                   jax.ShapeDtypeStruct((1, LANES), F32), jax.ShapeDtypeStruct((SSD_HEADS, 1), F32)],
        compiler_params=_cp("arbitrary"),
    )(ddt128, dadt128, proj, bias128, dt128, a128, dd_h)


def _bdot(a, b, ca, cb, precision=None):
    return lax.dot_general(a, b, (((ca,), (cb,)), ((0,), (0,))), precision=precision, preferred_element_type=F32)


def _ssd_common(xs_ref, dt_ref, cs_ref, csr_ref, b_ref, c_ref, nc):
    x = xs_ref[0].reshape(nc, CHUNK, SSD_P)
    dt = dt_ref[0].reshape(nc, CHUNK, SSD_P)
    cs = cs_ref[0].reshape(nc, CHUNK, SSD_P)
    csr = csr_ref[0]
    bm = b_ref[0].reshape(nc, CHUNK, SSD_N).astype(BF16)
    cm = c_ref[0].reshape(nc, CHUNK, SSD_N).astype(BF16)
    li = lax.broadcasted_iota(jnp.int32, (nc, CHUNK, CHUNK), 1)
    si = lax.broadcasted_iota(jnp.int32, (nc, CHUNK, CHUNK), 2)
    lmat = jnp.exp(jnp.where(si <= li, cs - csr, NEG))
    g = _bdot(cm, bm, 2, 2)
    cs_last = jnp.sum(jnp.where(li == CHUNK - 1, cs, 0.0), axis=1, keepdims=True)
    xdt = x * dt
    dec = jnp.exp(cs_last - cs)
    return x, dt, cs, bm, cm, li, si, lmat, g, cs_last, xdt, dec


def _ssd_fwd(xs_h, dt_h, cs_h, cs_row, b_g, c_g, dskip_h, *, name):
    t = xs_h.shape[1]
    nc = t // CHUNK
    hpg = SSD_HEADS // SSD_GROUPS

    def body(xs_ref, dt_ref, cs_ref, csr_ref, b_ref, c_ref, dk_ref, y_ref, st_ref, sc_ref, cd_ref):
        x, dt, cs, bm, cm, li, si, lmat, g, cs_last, xdt, dec = _ssd_common(xs_ref, dt_ref, cs_ref, csr_ref, b_ref,
                                                                           c_ref, nc)
        yd = _bdot((g * lmat).astype(BF16), xdt.astype(BF16), 2, 1)
        sc_ref[...] = _bdot(bm, (dec * xdt).astype(BF16), 1, 1)
        cd_ref[...] = jnp.exp(cs_last)

        def step(c, s):
            st_ref[0, c] = s
            return s * cd_ref[c] + sc_ref[c]

        lax.fori_loop(0, nc, step, jnp.zeros((SSD_N, SSD_P), F32))
        yo = _bdot(cm, st_ref[0].astype(BF16), 2, 1) * jnp.exp(cs)
        y_ref[0] = (yd + yo + dk_ref[0] * x).reshape(t, SSD_P)

    head = pl.BlockSpec((1, t, SSD_P), lambda h: (h, 0, 0))
    grp = pl.BlockSpec((1, t, SSD_N), lambda h: (h // hpg, 0, 0))
    return pl.pallas_call(
        body, name=name, grid=(SSD_HEADS,),
        in_specs=[head, head, head, pl.BlockSpec((1, nc, 1, CHUNK), lambda h: (h, 0, 0, 0)), grp, grp,
                  pl.BlockSpec((1, 1, SSD_P), lambda h: (h, 0, 0))],
        out_specs=[head, pl.BlockSpec((1, nc, SSD_N, SSD_P), lambda h: (h, 0, 0, 0))],
        out_shape=[jax.ShapeDtypeStruct((SSD_HEADS, t, SSD_P), F32),
                   jax.ShapeDtypeStruct((SSD_HEADS, nc, SSD_N, SSD_P), F32)],
        scratch_shapes=[pltpu.VMEM((nc, SSD_N, SSD_P), F32), pltpu.VMEM((nc, 1, SSD_P), F32)],
        compiler_params=_cp("parallel"),
    )(xs_h, dt_h, cs_h, cs_row, b_g, c_g, dskip_h)


def _ssd_bwd(xs_h, dt_h, cs_h, cs_row, b_g, c_g, dskip_h, a_h, states, dy_h, *, name):
    t = xs_h.shape[1]
    nc = t // CHUNK
    hpg = SSD_HEADS // SSD_GROUPS

    def body(xs_ref, dt_ref, cs_ref, csr_ref, b_ref, c_ref, dk_ref, a_ref, st_ref, dy_ref,
             dxs_ref, ddt_ref, dadt_ref, db_ref, dc_ref, dd_ref, dsl_ref, dsc_ref, cd_ref):
        x, dt, cs, bm, cm, li, si, lmat, g, cs_last, xdt, dec = _ssd_common(xs_ref, dt_ref, cs_ref, csr_ref, b_ref,
                                                                           c_ref, nc)
        dy = dy_ref[0].reshape(nc, CHUNK, SSD_P)
        dyb = dy.astype(BF16)
        xdtb = xdt.astype(BF16)
        sprev = st_ref[0]
        sprevb = sprev.astype(BF16)
        cdec = jnp.exp(cs_last)
        ecs = jnp.exp(cs)
        dw = (ecs * dy).astype(BF16)
        wmat = _bdot(cm, sprevb, 2, 1)
        dcs = jnp.sum(dy * ecs * wmat, axis=2, keepdims=True)
        dcm = _bdot(dw, sprevb, 2, 2)
        dsl_ref[...] = _bdot(cm, dw, 1, 1)
        cd_ref[...] = cdec

        def step(k, ds):
            c = nc - 1 - k
            dsc_ref[c] = ds
            return ds * cd_ref[c] + dsl_ref[c]

        lax.fori_loop(0, nc, step, jnp.zeros((SSD_N, SSD_P), F32))
        dsc = dsc_ref[...]
        dscb = dsc.astype(BF16)
        d_last = jnp.sum(jnp.sum(dsc * sprev, axis=1, keepdims=True) * cdec, axis=2, keepdims=True)
        z = dec * xdt
        dbm = _bdot(z.astype(BF16), dscb, 2, 2)
        dz = _bdot(bm, dscb, 2, 1)
        dxdt = dec * dz
        t2 = jnp.sum(dz * z, axis=2, keepdims=True)
        dcs = dcs - t2
        d_last = d_last + jnp.sum(t2, axis=1, keepdims=True)
        m = g * lmat
        mb = m.astype(BF16)
        dm = _bdot(dyb, xdtb, 2, 2)
        dxdt = dxdt + _bdot(mb, dyb, 1, 1)
        dseg = dm * m
        dcs = dcs + jnp.sum(dseg, axis=2, keepdims=True)
        ones = jnp.ones((nc, CHUNK, SSD_P), F32)
        dcs = dcs - _bdot(dseg, ones, 1, 1, precision=HI)
        dg = (dm * lmat).astype(BF16)
        dcm = dcm + _bdot(dg, bm, 2, 1)
        dbm = dbm + _bdot(dg, cm, 1, 1)
        dcs = dcs + jnp.where(li[:, :, :SSD_P] == CHUNK - 1, d_last, 0.0)
        triu = jnp.where(li <= si, 1.0, 0.0).astype(F32)
        dadt = _bdot(triu, dcs, 2, 1, precision=HI)
        dk = dk_ref[0]
        dxs_ref[0] = (dxdt * dt + dk * dy).reshape(t, SSD_P)
        ddt_ref[0] = (jnp.sum(dxdt * x, axis=2, keepdims=True) + dadt * a_ref[0]).reshape(t, SSD_P)
        dadt_ref[0] = dadt.reshape(t, SSD_P)
        dd_ref[0] = jnp.sum(jnp.sum(dy * x, axis=1, keepdims=True), axis=0)

        @pl.when(pl.program_id(1) == 0)
        def _():
            db_ref[...] = jnp.zeros_like(db_ref)
            dc_ref[...] = jnp.zeros_like(dc_ref)

        db_ref[0] += dbm.reshape(t, SSD_N)
        dc_ref[0] += dcm.reshape(t, SSD_N)

    head = pl.BlockSpec((1, t, SSD_P), lambda gi, hi: (gi * hpg + hi, 0, 0))
    grp = pl.BlockSpec((1, t, SSD_N), lambda gi, hi: (gi, 0, 0))
    lane = pl.BlockSpec((1, 1, SSD_P), lambda gi, hi: (gi * hpg + hi, 0, 0))
    return pl.pallas_call(
        body, name=name, grid=(SSD_GROUPS, hpg),
        in_specs=[head, head, head, pl.BlockSpec((1, nc, 1, CHUNK), lambda gi, hi: (gi * hpg + hi, 0, 0, 0)),
                  grp, grp, lane, lane, pl.BlockSpec((1, nc, SSD_N, SSD_P), lambda gi, hi: (gi * hpg + hi, 0, 0, 0)),
                  head],
        out_specs=[head, head, head, grp, grp, lane],
        out_shape=[jax.ShapeDtypeStruct((SSD_HEADS, t, SSD_P), F32)] * 3
        + [jax.ShapeDtypeStruct((SSD_GROUPS, t, SSD_N), F32)] * 2
        + [jax.ShapeDtypeStruct((SSD_HEADS, 1, SSD_P), F32)],
        scratch_shapes=[pltpu.VMEM((nc, SSD_N, SSD_P), F32), pltpu.VMEM((nc, SSD_N, SSD_P), F32),
                        pltpu.VMEM((nc, 1, SSD_P), F32)],
        compiler_params=_cp("parallel", "arbitrary"),
    )(xs_h, dt_h, cs_h, cs_row, b_g, c_g, dskip_h, a_h, states, dy_h)


def _ssd_gate_fwd(y, proj, w, *, tr=256, name):
    t = y.shape[0]
    gw = D_SSM // SSD_GROUPS

    def body(y_ref, z_ref, w_ref, o_ref):
        v = y_ref[...] * _silu(z_ref[...])
        for gi in range(SSD_GROUPS):
            vg = v[:, gi * gw:(gi + 1) * gw]
            r = lax.rsqrt(jnp.mean(vg * vg, axis=-1, keepdims=True) + NORM_EPS)
            o_ref[:, gi * gw:(gi + 1) * gw] = (vg * r * w_ref[:, gi * gw:(gi + 1) * gw]).astype(BF16)

    blk = pl.BlockSpec((tr, D_SSM), lambda i: (i, 0))
    return pl.pallas_call(
        body, name=name, grid=(t // tr,), in_specs=[blk, blk, pl.BlockSpec((1, D_SSM), lambda i: (0, 0))],
        out_specs=blk, out_shape=jax.ShapeDtypeStruct((t, D_SSM), BF16), compiler_params=_cp("parallel"),
    )(y, proj, w)


def _ssd_gate_bwd(y, proj, w, dcat, *, tr=256, name):
    t = y.shape[0]
    gw = D_SSM // SSD_GROUPS

    def body(y_ref, z_ref, w_ref, d_ref, dy_ref, dz_ref, dw_ref):
        yv, zv, dv = y_ref[...], z_ref[...], d_ref[...].astype(F32)
        sz = _silu(zv)
        v = yv * sz

        @pl.when(pl.program_id(0) == 0)
        def _():
            dw_ref[...] = jnp.zeros_like(dw_ref)

        for gi in range(SSD_GROUPS):
            sl = slice(gi * gw, (gi + 1) * gw)
            vg, dg = v[:, sl], dv[:, sl]
            r = lax.rsqrt(jnp.mean(vg * vg, axis=-1, keepdims=True) + NORM_EPS)
            vh = vg * r
            gg = dg * w_ref[:, sl]
            dvg = r * (gg - vh * jnp.mean(gg * vh, axis=-1, keepdims=True))
            dy_ref[:, sl] = dvg * sz[:, sl]
            dz_ref[:, sl] = (dvg * yv[:, sl] * _dsilu(zv[:, sl])).astype(BF16)
            dw_ref[:, sl] += jnp.sum(dg * vh, axis=0, keepdims=True)

    blk = pl.BlockSpec((tr, D_SSM), lambda i: (i, 0))
    row = pl.BlockSpec((1, D_SSM), lambda i: (0, 0))
    return pl.pallas_call(
        body, name=name, grid=(t // tr,), in_specs=[blk, blk, row, blk], out_specs=[blk, blk, row],
        out_shape=[jax.ShapeDtypeStruct((t, D_SSM), F32), jax.ShapeDtypeStruct((t, D_SSM), BF16),
                   jax.ShapeDtypeStruct((1, D_SSM), F32)],
        compiler_params=_cp("arbitrary"),
    )(y, proj, w, dcat)


def _pad_lanes(v):
    return jnp.pad(v, ((0, 0), (0, LANES - v.shape[1])))


def _to_heads(v):
    return v.reshape(v.shape[0], SSD_HEADS, SSD_P).transpose(1, 0, 2)


def _from_heads(v):
    return v.transpose(1, 0, 2).reshape(v.shape[1], SSD_HEADS * SSD_P)


def _per_head(v128, t):
    return jnp.broadcast_to(v128[:, :SSD_HEADS].T[:, :, None], (SSD_HEADS, t, SSD_P))


def _ssd_forward(proj, conv_w, conv_b, dt_bias, a_log, d_skip, ssd_norm_w):
    t = proj.shape[0]
    nc = t // CHUNK
    xbc = _conv_act_fwd(proj, conv_w, conv_b, kw=SSD_CONV, glu=False, tc=512, coff=OFF_XBC // 512,
                        ncols=SSD_CONV_DIM, out_dtype=F32, name="ssd_conv_fwd")
    bias128, alog128 = _pad_lanes(dt_bias), _pad_lanes(a_log)
    dt128, cs128, a128 = _ssd_prep(proj, bias128, alog128, name="ssd_prep")
    dt_h, cs_h = _per_head(dt128, t), _per_head(cs128, t)
    cs_row = cs128[:, :SSD_HEADS].T.reshape(SSD_HEADS, nc, 1, CHUNK)
    xs_h = _to_heads(xbc[:, :D_SSM])
    gn = SSD_GROUPS * SSD_N
    b_g = xbc[:, D_SSM:D_SSM + gn].reshape(t, SSD_GROUPS, SSD_N).transpose(1, 0, 2)
    c_g = xbc[:, D_SSM + gn:].reshape(t, SSD_GROUPS, SSD_N).transpose(1, 0, 2)
    dskip_h = jnp.broadcast_to(d_skip[0][:, None, None], (SSD_HEADS, 1, SSD_P))
    a_h = jnp.broadcast_to(a128[0, :SSD_HEADS][:, None, None], (SSD_HEADS, 1, SSD_P))
    y_h, states = _ssd_fwd(xs_h, dt_h, cs_h, cs_row, b_g, c_g, dskip_h, name="ssd_scan_fwd")
    y = _from_heads(y_h)
    y_ssd = _ssd_gate_fwd(y, proj, ssd_norm_w, name="ssd_gate_fwd")
    saved = (proj, conv_w, conv_b, ssd_norm_w, bias128, dt128, a128, dt_h, cs_h, cs_row, xs_h, b_g, c_g, dskip_h, a_h,
             states, y)
    return y_ssd, saved


def _ssd_backward(saved, dcat):
    (proj, conv_w, conv_b, ssd_norm_w, bias128, dt128, a128, dt_h, cs_h, cs_row, xs_h, b_g, c_g, dskip_h, a_h, states,
     y) = saved
    t = proj.shape[0]
    dy, dz, d_norm_w = _ssd_gate_bwd(y, proj, ssd_norm_w, dcat, name="ssd_gate_bwd")
    dxs_h, ddt_h, dadt_h, db_g, dc_g, dd_h = _ssd_bwd(xs_h, dt_h, cs_h, cs_row, b_g, c_g, dskip_h, a_h, states,
                                                      _to_heads(dy), name="ssd_scan_bwd")
    gn = SSD_GROUPS * SSD_N
    dxc = jnp.concatenate([_from_heads(dxs_h), db_g.transpose(1, 0, 2).reshape(t, gn),
                           dc_g.transpose(1, 0, 2).reshape(t, gn)], axis=1)
    dxbc, d_conv_w, d_conv_b = _conv_act_bwd(proj, conv_w, conv_b, dxc, kw=SSD_CONV, glu=False, tc=512,
                                             coff=OFF_XBC // 512, ncols=SSD_CONV_DIM, name="ssd_conv_bwd")
    ddt128 = _pad_lanes(ddt_h[:, :, 0].T)
    dadt128 = _pad_lanes(dadt_h[:, :, 0].T)
    d_raw, d_bias, d_alog, d_dskip = _ssd_prep_bwd(ddt128, dadt128, proj, bias128, dt128, a128,
                                                   dd_h.reshape(SSD_HEADS, SSD_P), name="ssd_prep_bwd")
    return (dz, dxbc, d_raw, d_norm_w, d_conv_w, d_conv_b, d_bias[:, :SSD_HEADS], d_alog[:, :SSD_HEADS],
            d_dskip.reshape(1, SSD_HEADS))


def _rope_tables(positions):
    inv_freq = ROPE_THETA ** (-jnp.arange(0, MLA_ROPE, 2, dtype=F32) / MLA_ROPE)
    ang = positions[0].astype(F32)[:, None] * inv_freq
    cos, sin = jnp.cos(ang), jnp.sin(ang)
    z = jnp.zeros_like(cos)
    return jnp.stack([jnp.concatenate([cos, cos, z, z], axis=1), jnp.concatenate([-sin, z, z, z], axis=1),
                      jnp.concatenate([z, sin, z, z], axis=1)])


def _mla_forward(proj, tabs, q_a_norm_w, wq_pad, kv_a_norm_w, wkv):
    qn = _rmsnorm_fwd(proj, q_a_norm_w, width=MLA_Q_RANK, cblk=OFF_QA // MLA_Q_RANK, name="q_a_norm")
    q = _matmul(qn, wq_pad, name="q_b_proj")
    kvn = _rmsnorm_fwd(proj, kv_a_norm_w, width=MLA_KV_RANK, cblk=OFF_CKV // MLA_KV_RANK, name="kv_a_norm")
    kv = _matmul(kvn, wkv, name="kv_b_proj")
    q3, k3, v3 = _mla_prep(q, kv, proj, tabs, name="mla_prep")
    o, lse = _attn_fwd(q3, k3, v3, name="attn_fwd")
    return o, (proj, tabs, q_a_norm_w, wq_pad, kv_a_norm_w, wkv, qn, kvn, q3, k3, v3, o, lse)


def _mla_backward(saved, dcat):
    proj, tabs, q_a_norm_w, wq_pad, kv_a_norm_w, wkv, qn, kvn, q3, k3, v3, o, lse = saved
    dq3, dk3, dv3 = _attn_bwd(q3, k3, v3, o, dcat, lse, name="attn_bwd")
    dq, dkv, dkr = _mla_unprep(dq3, dk3, dv3, tabs, name="mla_unprep")
    d_wq = _matmul(qn, dq, ta=True, out_dtype=BF16, name="d_w_q_b")
    dqn = _matmul(dq, wq_pad, tb=True, name="d_qn")
    dq_a, d_qnw = _rmsnorm_bwd(proj, q_a_norm_w, dqn, width=MLA_Q_RANK, cblk=OFF_QA // MLA_Q_RANK, out_dtype=BF16,
                               name="q_a_norm_bwd")
    d_wkv = _matmul(kvn, dkv, ta=True, out_dtype=BF16, name="d_w_kv_b")
    dkvn = _matmul(dkv, wkv, tb=True, name="d_kvn")
    dckv, d_kvnw = _rmsnorm_bwd(proj, kv_a_norm_w, dkvn, width=MLA_KV_RANK, cblk=OFF_CKV // MLA_KV_RANK,
                                out_dtype=BF16, name="kv_a_norm_bwd")
    return dq_a, dckv, dkr, d_wq, d_wkv, d_qnw, d_kvnw


def _pad_w_q(w):
    r = w.shape[0]
    w3 = w.reshape(r, MLA_HEADS, MLA_NOPE + MLA_ROPE)
    return jnp.pad(w3, ((0, 0), (0, 0), (0, MLA_QK_PAD - MLA_NOPE - MLA_ROPE))).reshape(r, MLA_HEADS * MLA_QK_PAD)


def _unpad_w_q(w):
    r = w.shape[0]
    return w.reshape(r, MLA_HEADS, MLA_QK_PAD)[:, :, :MLA_NOPE + MLA_ROPE].reshape(r, MLA_HEADS * (MLA_NOPE + MLA_ROPE))


def _pad_w_in(w):
    r = w.shape[0]
    o_dt = D_SSM + SSD_CONV_DIM
    o_qa = o_dt + SSD_HEADS
    o_kr = o_qa + MLA_Q_RANK + MLA_KV_RANK
    zeros = lambda n: jnp.zeros((r, n), w.dtype)
    return jnp.concatenate([w[:, :o_dt], w[:, o_qa:o_kr], w[:, o_kr:], zeros(LANES - MLA_ROPE),
                            w[:, o_dt:o_qa], zeros(LANES - SSD_HEADS)], axis=1)


def _unpad_w_in(w):
    return jnp.concatenate([w[:, :OFF_QA], w[:, OFF_DT:OFF_DT + SSD_HEADS], w[:, OFF_QA:OFF_KR + MLA_ROPE]], axis=1)


WEIGHTS = ['mix_norm_w', 'w_in', 'conv_w', 'conv_b', 'dt_bias', 'a_log', 'd_skip', 'ssd_norm_w', 'q_a_norm_w', 'w_q_b',
           'kv_a_norm_w', 'w_kv_b', 'w_out', 'ffn_norm_w', 'w_ffn_up', 'ffn_conv_w', 'ffn_conv_b', 'w_ffn_down',
           'ple_norm_w', 'w_ple_gate', 'b_ple_gate', 'w_ple_proj', 'ple_post_norm_w', 'final_norm_w']
BIG = ['w_in', 'w_q_b', 'w_kv_b', 'w_out', 'w_ffn_up', 'w_ffn_down', 'w_ple_gate', 'w_ple_proj']
COL_SHARDED = ('w_in', 'w_q_b', 'w_kv_b', 'w_ffn_up', 'w_ple_proj')
CONV = ['conv_w', 'ffn_conv_w']
REPL = [n for n in WEIGHTS if n not in BIG and n not in CONV]
FFN_INV = tuple(int(i) for i in np.argsort(FFN_PERM))


def _cat_cols(g):
    return g.transpose(1, 0, 2).reshape(g.shape[1], N_DEV * g.shape[2])


def _split_cols(w):
    return w.reshape(w.shape[0], N_DEV, w.shape[1] // N_DEV).transpose(1, 0, 2)


def _interleave(v):
    r = v.shape[0]
    return v.reshape(r, N_DEV, FFN_TC)[:, jnp.array(FFN_PERM)].reshape(r, N_DEV * FFN_TC)


def _deinterleave(v):
    r = v.shape[0]
    return v.reshape(r, N_DEV, FFN_TC)[:, jnp.array(FFN_INV)].reshape(r, N_DEV * FFN_TC)


def _assemble_weights(g):
    layout = {
        'w_in': lambda v: _pad_w_in(_cat_cols(v)),
        'w_q_b': lambda v: _pad_w_q(_cat_cols(v)),
        'w_kv_b': _cat_cols,
        'w_out': lambda v: v.reshape(D_MODEL, D_MODEL),
        'w_ffn_up': lambda v: v,
        'w_ffn_down': lambda v: v.reshape(D_FF, D_MODEL),
        'w_ple_gate': lambda v: v.reshape(D_MODEL, D_MODEL),
        'w_ple_proj': _cat_cols,
        'conv_w': _cat_cols,
        'ffn_conv_w': lambda v: _interleave(_cat_cols(v)),
    }
    return {n: layout[n](v) for n, v in g.items()}


WEIGHT_GROUPS = {'a': ['w_in', 'w_q_b', 'w_kv_b', 'conv_w'], 'b': ['w_out'],
                 'c': ['w_ffn_up', 'ffn_conv_w', 'w_ffn_down', 'w_ple_gate', 'w_ple_proj']}
GRAD_GROUPS = {'p': ['w_ple_proj', 'w_ple_gate', 'w_ffn_down'], 'r': ['w_ffn_up'], 's': ['w_out'],
               't': ['w_q_b', 'w_kv_b', 'w_in']}


def _ffn_perm(j):
    return (j % 2) * (N_DEV // 2) + j // 2


def _local_step(x, p, tabs, get_w, s, target, emit, relay):
    t = x.shape[0]
    s = dict(s)
    half = D_MODEL // 2
    up_cols = 2 * D_FF
    ffn_conv_b = _interleave(s['ffn_conv_b'])
    w = dict(get_w('a', None))
    h = _rmsnorm_fwd(x, s['mix_norm_w'], width=D_MODEL, name="mix_norm")
    proj = _matmul(h, w['w_in'], name="in_proj")
    y_ssd, ssd_saved = _ssd_forward(proj, w['conv_w'], s['conv_b'], s['dt_bias'], s['a_log'], s['d_skip'],
                                    s['ssd_norm_w'])
    o, mla_saved = _mla_forward(proj, tabs, s['q_a_norm_w'], w['w_q_b'], s['kv_a_norm_w'], w['w_kv_b'])
    tk_o, tn_o = _tile(half, MM_TK), _tile(D_MODEL, MM_TILE)
    w.update(get_w('b', o))
    x1 = _matmul(y_ssd, w['w_out'], add=x, mnk=(t, D_MODEL, half), name="out_proj_ssd")
    x1 = _matmul(o, w['w_out'], add=x1, mnk=(t, D_MODEL, half), name="out_proj_mla",
                 b_spec=pl.BlockSpec((tk_o, tn_o), lambda i, j, kk: (kk + half // tk_o, j)))
    hf = _rmsnorm_fwd(x1, s['ffn_norm_w'], width=D_MODEL, name="ffn_norm")
    w.update(get_w('c', hf))
    tk_u = _tile(D_MODEL, MM_TK)
    u = _matmul(hf, w['w_ffn_up'], mnk=(t, up_cols, D_MODEL), tn=FFN_TC, name="ffn_up",
                b_spec=pl.BlockSpec((1, tk_u, FFN_TC), lambda i, j, kk: (_ffn_perm(j), kk, 0)))
    act = _conv_act_fwd(u, w['ffn_conv_w'], ffn_conv_b, kw=FFN_CONV, glu=True, tc=2 * FFN_TC, coff=0, ncols=up_cols,
                        out_dtype=BF16, name="ffn_act")
    x2 = _matmul(act, w['w_ffn_down'], add=x1, name="ffn_down")
    hp = _rmsnorm_fwd(x2, s['ple_norm_w'], width=D_MODEL, name="ple_norm")
    gl = _matmul(hp, w['w_ple_gate'], bias=s['b_ple_gate'], name="ple_gate")
    pe = _matmul(p, w['w_ple_proj'], name="ple_proj")
    x3 = _ple_fwd(x2, gl, pe, s['ple_post_norm_w'], name="ple_mix")
    loss, dx3, d_final = _loss_head(x3, s['final_norm_w'], target, name="loss_head")
    dgl, d_bgate, dpe, d_post = _ple_bwd(dx3, gl, pe, s['ple_post_norm_w'], name="ple_mix_bwd")
    d_wproj = _matmul(p, dpe, ta=True, out_dtype=BF16, name="d_w_ple_proj")
    d_wgate = _matmul(hp, dgl, ta=True, out_dtype=BF16, name="d_w_ple_gate")
    dhp = _matmul(dgl, w['w_ple_gate'], tb=True, name="d_ple_normed")
    dx2, d_plenorm = _rmsnorm_bwd(x2, s['ple_norm_w'], dhp, dx3, width=D_MODEL, name="ple_norm_bwd")
    dact = _matmul(dx2, w['w_ffn_down'], tb=True, name="d_ffn_act")
    d_wdown = _matmul(act, dx2, ta=True, out_dtype=BF16, name="d_w_ffn_down")
    zz = emit('p', {'w_ple_proj': _split_cols(d_wproj), 'w_ple_gate': d_wgate.reshape(N_DEV, D_MODEL // N_DEV, D_MODEL),
                    'w_ffn_down': d_wdown.reshape(N_DEV, D_FF // N_DEV, D_MODEL)})
    du, d_fconv_w, d_fconv_b = _conv_act_bwd(u, w['ffn_conv_w'], ffn_conv_b + zz, dact, kw=FFN_CONV, glu=True,
                                             tc=2 * FFN_TC, coff=0, ncols=up_cols, name="ffn_act_bwd")
    zz = zz + relay('p', du)
    tm_u = _tile(D_MODEL, MM_TILE)
    d_wup = _matmul(hf, du, ta=True, out_dtype=BF16, mnk=(D_MODEL, up_cols, t), tn=FFN_TC, name="d_w_ffn_up",
                    o_spec=pl.BlockSpec((1, tm_u, FFN_TC), lambda i, j, kk: (_ffn_perm(j), i, 0)),
                    o_shape=(N_DEV, D_MODEL, FFN_TC))
    zz = zz + emit('r', {'w_ffn_up': d_wup})
    dhf = _matmul(du, w['w_ffn_up'], tb=True, mnk=(t, D_MODEL, up_cols), tk=FFN_TC, name="d_ffn_normed",
                  b_spec=pl.BlockSpec((1, tn_o, FFN_TC), lambda i, j, kk: (_ffn_perm(kk), j, 0)))
    zz = zz + relay('r', dhf)
    dx1, d_ffnnorm = _rmsnorm_bwd(x1, s['ffn_norm_w'] + zz, dhf, dx2, width=D_MODEL, name="ffn_norm_bwd")
    dcat = _matmul(dx1, w['w_out'], tb=True, name="d_mixed")
    d_wout = jnp.concatenate([_matmul(y_ssd, dx1, ta=True, out_dtype=BF16, name="d_w_out_ssd"),
                              _matmul(o, dx1, ta=True, out_dtype=BF16, name="d_w_out_mla")], axis=0)
    zz = zz + emit('s', {'w_out': d_wout.reshape(N_DEV, D_MODEL // N_DEV, D_MODEL)})
    ssd_saved = ssd_saved[:3] + (ssd_saved[3] + zz,) + ssd_saved[4:]
    dz, dxbc, d_raw, d_ssdnorm, d_conv_w, d_conv_b, d_dtb, d_alog, d_dskip = _ssd_backward(ssd_saved, dcat)
    zz = zz + relay('s', dz)
    mla_saved = mla_saved[:-1] + (mla_saved[-1] + zz,)
    dq_a, dckv, dkr, d_wq, d_wkv, d_qnorm, d_kvnorm = _mla_backward(mla_saved, dcat)
    dproj = jnp.concatenate([dz, dxbc, dq_a, dckv, dkr, d_raw], axis=1)
    d_win = _matmul(h, dproj, ta=True, out_dtype=BF16, name="d_w_in")
    dh = _matmul(dproj, w['w_in'], tb=True, name="d_in_normed")
    dx, d_mixnorm = _rmsnorm_bwd(x, s['mix_norm_w'], dh, dx1, width=D_MODEL, name="mix_norm_bwd")
    emit('t', {'w_in': _split_cols(_unpad_w_in(d_win)), 'w_q_b': _split_cols(_unpad_w_q(d_wq)),
               'w_kv_b': _split_cols(d_wkv)})
    relay('t', dx)
    conv = {'conv_w': d_conv_w, 'ffn_conv_w': _deinterleave(d_fconv_w)}
    vec = {
        'mix_norm_w': d_mixnorm, 'conv_b': d_conv_b, 'dt_bias': d_dtb, 'a_log': d_alog, 'd_skip': d_dskip,
        'ssd_norm_w': d_ssdnorm, 'q_a_norm_w': d_qnorm, 'kv_a_norm_w': d_kvnorm, 'ffn_norm_w': d_ffnnorm,
        'ffn_conv_b': _deinterleave(d_fconv_b), 'ple_norm_w': d_plenorm, 'b_ple_gate': d_bgate,
        'ple_post_norm_w': d_post, 'final_norm_w': d_final,
    }
    return loss, dx, conv, vec


MESH = pl.DeviceIdType.MESH
FLIPS = ((0, 0, 1), (1, 0, 0), (0, 1, 0), (1, 1, 0), (1, 0, 1), (0, 1, 1), (1, 1, 1))


def _exchange(items, *, gather, name):
    n = len(items)

    def body(*refs):
        ins, outs = refs[:n], refs[n:2 * n]
        send_sems, recv_sems, local_sems = refs[2 * n:]
        x, y, c = lax.axis_index("x"), lax.axis_index("y"), lax.axis_index("c")
        me = 4 * x + 2 * y + c
        peers = [(jnp.where(fx, 1 - x, x), jnp.where(fy, 1 - y, y), jnp.where(fc, 1 - c, c)) for fx, fy, fc in FLIPS]
        slot = [4 * px + 2 * py + pc for px, py, pc in peers]
        local, sends = [], []
        for wi in range(n):
            cp = pltpu.make_async_copy(ins[wi] if gather else ins[wi].at[me], outs[wi].at[me], local_sems.at[wi])
            cp.start()
            local.append(cp)
            for k, peer in enumerate(peers):
                cp = pltpu.make_async_remote_copy(
                    src_ref=ins[wi] if gather else ins[wi].at[slot[k]], dst_ref=outs[wi].at[me],
                    send_sem=send_sems.at[k, wi], recv_sem=recv_sems.at[k, wi], device_id=peer, device_id_type=MESH)
                cp.start()
                sends.append(cp)
        for wi in range(n):
            for k, peer in enumerate(peers):
                pltpu.make_async_remote_copy(
                    src_ref=outs[wi].at[slot[k]], dst_ref=outs[wi].at[slot[k]], send_sem=send_sems.at[k, wi],
                    recv_sem=recv_sems.at[k, wi], device_id=peer, device_id_type=MESH).wait_recv()
        for cp in sends:
            cp.wait_send()
        for cp in local:
            cp.wait()

    hbm = pl.BlockSpec(memory_space=pltpu.HBM)
    out_shape = [jax.ShapeDtypeStruct(((N_DEV,) + v.shape) if gather else v.shape, v.dtype) for v in items]
    return pl.pallas_call(
        body, name=name, in_specs=[hbm] * n, out_specs=[hbm] * n, out_shape=out_shape,
        scratch_shapes=[pltpu.SemaphoreType.DMA((len(FLIPS), n)), pltpu.SemaphoreType.DMA((len(FLIPS), n)),
                        pltpu.SemaphoreType.DMA((n,))],
    )(*items)


HBM_SPEC = pl.BlockSpec(memory_space=pltpu.HBM)
SEM_SPEC = pl.BlockSpec(memory_space=pltpu.SEMAPHORE)
EFFECT = pltpu.SideEffectType.DATAFLOW_SIDE_EFFECTING


def _peers():
    x, y, c = lax.axis_index("x"), lax.axis_index("y"), lax.axis_index("c")
    peers = [(jnp.where(fx, 1 - x, x), jnp.where(fy, 1 - y, y), jnp.where(fc, 1 - c, c)) for fx, fy, fc in FLIPS]
    return 4 * x + 2 * y + c, peers, [4 * px + 2 * py + pc for px, py, pc in peers]


def _split_start(bufs, ncopies, plan, *, name):
    nb = len(bufs)

    def body(*refs):
        send_sems, recv_sems, token = refs[nb], refs[nb + 1], refs[2 * nb + 2]
        for i, (src, dst, peer, _) in enumerate(plan(refs[:nb])):
            pltpu.make_async_remote_copy(src_ref=src, dst_ref=dst, send_sem=send_sems.at[i], recv_sem=recv_sems.at[i],
                                         device_id=peer, device_id_type=MESH).start()
        token[...] = jnp.zeros_like(token)

    res = pl.pallas_call(
        body, name=name, in_specs=[HBM_SPEC] * nb,
        out_specs=[SEM_SPEC, SEM_SPEC] + [HBM_SPEC] * nb + [pl.BlockSpec(memory_space=pltpu.VMEM)],
        out_shape=[pltpu.SemaphoreType.DMA((ncopies,)), pltpu.SemaphoreType.DMA((ncopies,))]
        + [pltpu.HBM(v.shape, v.dtype) for v in bufs] + [jax.ShapeDtypeStruct((HALO, LANES), F32)],
        input_output_aliases={i: 2 + i for i in range(nb)},
        compiler_params=pltpu.CompilerParams(has_side_effects=EFFECT),
    )(*[pltpu.with_memory_space_constraint(v, pltpu.HBM) for v in bufs])
    return (res[0], res[1], list(res[2:2 + nb])), res[2 + nb]


def _split_wait(started, after, plan, local_plan, *, name):
    send_sems, recv_sems, bufs = started
    nb = len(bufs)
    nlocal = len(local_plan(bufs))

    def body(*refs):
        send_sems, recv_sems = refs[nb], refs[nb + 1]
        local_sems = refs[2 * nb + 3]
        local = []
        for j, (src, dst) in enumerate(local_plan(refs[:nb])):
            cp = pltpu.make_async_copy(src, dst, local_sems.at[j])
            cp.start()
            local.append(cp)
        for i, (src, _, peer, incoming) in enumerate(plan(refs[:nb])):
            cp = pltpu.make_async_remote_copy(src_ref=src, dst_ref=incoming, send_sem=send_sems.at[i],
                                              recv_sem=recv_sems.at[i], device_id=peer, device_id_type=MESH)
            cp.wait_send()
            cp.wait_recv()
        for cp in local:
            cp.wait()

    res = pl.pallas_call(
        body, name=name, in_specs=[HBM_SPEC] * nb + [SEM_SPEC, SEM_SPEC, pl.BlockSpec(memory_space=pl.ANY)],
        out_specs=[HBM_SPEC] * nb, out_shape=[pltpu.HBM(v.shape, v.dtype) for v in bufs],
        input_output_aliases={i: i for i in range(nb)},
        scratch_shapes=[pltpu.SemaphoreType.DMA((max(nlocal, 1),))],
        compiler_params=pltpu.CompilerParams(has_side_effects=EFFECT),
    )(*bufs, send_sems, recv_sems, after)
    return list(res)


def _place():
    x, y, c = lax.axis_index("x"), lax.axis_index("y"), lax.axis_index("c")
    others = [((1 - x, y, c), 2 * (1 - x) + y), ((x, 1 - y, c), 2 * x + 1 - y), ((1 - x, 1 - y, c), 2 * (1 - x) + 1 - y)]
    return 4 * x + 2 * y + c, 2 * x + y, c, (x, y, 1 - c), others


def _gather1_plan(n):
    def plan(refs):
        me, _, _, sibling, others = _place()
        out = []
        for wi in range(n):
            item, land = refs[wi], refs[n + wi]
            out.append((item, land.at[me], sibling, land.at[me + 1 - 2 * lax.axis_index("c")]))
            for peer, chip in others:
                out.append((item, land.at[me], peer, land.at[2 * chip + lax.axis_index("c")]))
        return out

    return plan


def _gather1_local(n):
    def plan(refs):
        me = _place()[0]
        return [(refs[wi], refs[n + wi].at[me]) for wi in range(n)]

    return plan


def _gather2_plan(n):
    def plan(refs):
        _, _, c, sibling, others = _place()
        out = []
        for wi in range(n):
            land = refs[wi]
            for _, chip in others:
                out.append((land.at[2 * chip + c], land.at[2 * chip + c], sibling, land.at[2 * chip + 1 - c]))
        return out

    return plan


def _gather_start(items, *, name):
    lands = [lax.empty((N_DEV,) + v.shape, v.dtype) for v in items]
    return _split_start(items + lands, 4 * len(items), _gather1_plan(len(items)), name=name)


def _gather_forward(started, after, *, name):
    n = len(started[2]) // 2
    bufs = _split_wait(started, after, _gather1_plan(n), _gather1_local(n), name=name + "_wait")
    return _split_start(bufs[n:], 3 * n, _gather2_plan(n), name=name + "_start")


def _gather_finish(started, after, *, name):
    n = len(started[2])
    return _split_wait(started, after, _gather2_plan(n), lambda refs: [], name=name)


def _handshake(peers):
    barrier = pltpu.get_barrier_semaphore()
    for peer in peers:
        pl.semaphore_signal(barrier, inc=1, device_id=peer, device_id_type=MESH)
    pl.semaphore_wait(barrier, len(peers))


def _remote(src, dst, send_sem, recv_sem, peer):
    return pltpu.make_async_remote_copy(src_ref=src, dst_ref=dst, send_sem=send_sem, recv_sem=recv_sem, device_id=peer,
                                        device_id_type=MESH)


def _sequencer_gather(items, *, collective_id, name):
    n = len(items)
    srcs = [jax.new_ref(v, memory_space=pltpu.MemorySpace.HBM) for v in items]
    lands = [jax.empty_ref(jax.ShapeDtypeStruct((N_DEV,) + v.shape, v.dtype), memory_space=pltpu.MemorySpace.HBM)
             for v in items]
    dma = pltpu.SemaphoreType.DMA

    @pl.kernel(mesh=plsc.ScalarSubcoreMesh(axis_name="sequencer", num_cores=1), name=name,
               scratch_types=(dma((4 * n,)), dma((4 * n,)), dma((3 * n,)), dma((3 * n,)), dma((n,))),
               compiler_params=pltpu.CompilerParams(collective_id=collective_id))
    def launch(send1, recv1, send2, recv2, local_sems):
        _, _, _, sibling, others = _place()
        _handshake([sibling] + [peer for peer, _ in others])
        hop1 = _gather1_plan(n)(srcs + lands)
        hop2 = _gather2_plan(n)(lands)
        local = [pltpu.make_async_copy(src, dst, local_sems.at[j])
                 for j, (src, dst) in enumerate(_gather1_local(n)(srcs + lands))]
        for cp in local:
            cp.start()
        for i, (src, dst, peer, _) in enumerate(hop1):
            _remote(src, dst, send1.at[i], recv1.at[i], peer).start()
        for wi in range(n):
            for j in range(3):
                i1, i2 = 4 * wi + 1 + j, 3 * wi + j
                src, _, peer, incoming = hop1[i1]
                _remote(src, incoming, send1.at[i1], recv1.at[i1], peer).wait_recv()
                src, dst, peer, _ = hop2[i2]
                _remote(src, dst, send2.at[i2], recv2.at[i2], peer).start()
        for wi in range(n):
            src, _, peer, incoming = hop1[4 * wi]
            _remote(src, incoming, send1.at[4 * wi], recv1.at[4 * wi], peer).wait_recv()
        for i, (src, _, peer, incoming) in enumerate(hop2):
            cp = _remote(src, incoming, send2.at[i], recv2.at[i], peer)
            cp.wait_send()
            cp.wait_recv()
        for i, (src, dst, peer, _) in enumerate(hop1):
            _remote(src, dst, send1.at[i], recv1.at[i], peer).wait_send()
        for cp in local:
            cp.wait()

    launch()
    return [land[...] for land in lands]


N_CHIP = N_DEV // 2


def _scatter1_plan(n):
    def plan(refs):
        _, _, c, sibling, _ = _place()
        out = []
        for wi in range(n):
            parts, half = refs[wi], refs[n + wi]
            for chip in range(N_CHIP):
                out.append((parts.at[2 * chip + 1 - c], half.at[chip], sibling, half.at[chip]))
        return out

    return plan


def _scatter2_plan(n):
    def plan(refs):
        _, my_chip, _, _, others = _place()
        out = []
        for wi in range(n):
            sums, recv = refs[wi], refs[n + wi]
            for peer, chip in others:
                out.append((sums.at[chip], recv.at[my_chip], peer, recv.at[chip]))
        return out

    return plan


def _scatter2_local(n):
    def plan(refs):
        my_chip = _place()[1]
        return [(refs[wi].at[my_chip], refs[n + wi].at[my_chip]) for wi in range(n)]

    return plan


def _pair_add(parts, half, core, *, name):
    _, r, c = parts.shape
    tr = max(d for d in range(HALO, 257, HALO) if r % d == 0) if r > 256 else r
    parts4 = parts.reshape(N_CHIP, 2, r, c)

    def body(core_ref, p_ref, h_ref, o_ref):
        o_ref[...] = (p_ref[:, 0].astype(F32) + h_ref[...].astype(F32)).astype(o_ref.dtype)

    return pl.pallas_call(
        body, name=name,
        grid_spec=pltpu.PrefetchScalarGridSpec(
            num_scalar_prefetch=1, grid=(r // tr,),
            in_specs=[pl.BlockSpec((N_CHIP, 1, tr, c), lambda i, core_ref: (0, core_ref[0], i, 0)),
                      pl.BlockSpec((N_CHIP, tr, c), lambda i, core_ref: (0, i, 0))],
            out_specs=pl.BlockSpec((N_CHIP, tr, c), lambda i, core_ref: (0, i, 0))),
        out_shape=jax.ShapeDtypeStruct((N_CHIP, r, c), parts.dtype), compiler_params=_cp("parallel"),
    )(core, parts4, half)


def _scatter_start(parts, *, name):
    halves = [lax.empty((N_CHIP,) + v.shape[1:], v.dtype) for v in parts]
    return _split_start(parts + halves, N_CHIP * len(parts), _scatter1_plan(len(parts)), name=name)


def _scatter_forward(started, after, core, *, name):
    n = len(started[2]) // 2
    bufs = _split_wait(started, after, _scatter1_plan(n), lambda refs: [], name=name + "_wait")
    sums = [_pair_add(bufs[wi], bufs[n + wi], core, name=name + "_add%d" % wi) for wi in range(n)]
    recvs = [lax.empty(v.shape, v.dtype) for v in sums]
    return _split_start(sums + recvs, 3 * n, _scatter2_plan(n), name=name + "_start")


def _scatter_finish(started, after, *, name):
    n = len(started[2]) // 2
    return _split_wait(started, after, _scatter2_plan(n), _scatter2_local(n), name=name)[n:]


def _adamw(parts, w, m, v, *, name):
    r, c = w.shape
    nparts = parts.shape[0]
    tr = max(d for d in range(HALO, 129, HALO) if r % d == 0) if r > 128 else r

    def body(p_ref, w_ref, m_ref, v_ref, g_ref, d_ref, mo_ref, vo_ref):
        g = p_ref[0].astype(F32)
        for k in range(1, nparts):
            g = g + p_ref[k].astype(F32)
        mn = ADAM_B1 * m_ref[...] + (1.0 - ADAM_B1) * g
        vn = ADAM_B2 * v_ref[...] + (1.0 - ADAM_B2) * (g * g)
        m_hat = mn / (1.0 - ADAM_B1 ** ADAM_STEP)
        v_hat = vn / (1.0 - ADAM_B2 ** ADAM_STEP)
        g_ref[...] = g
        d_ref[...] = -ADAM_LR * (m_hat / (jnp.sqrt(v_hat) + ADAM_EPS) + ADAM_WD * w_ref[...])
        mo_ref[...] = mn
        vo_ref[...] = vn

    blk = pl.BlockSpec((tr, c), lambda i: (i, 0))
    return pl.pallas_call(
        body, name=name, grid=(r // tr,), in_specs=[pl.BlockSpec((nparts, tr, c), lambda i: (0, i, 0)), blk, blk, blk],
        out_specs=[blk] * 4, out_shape=[jax.ShapeDtypeStruct((r, c), F32)] * 4, compiler_params=_cp("parallel"),
    )(parts, w, m, v)


def _pack_rows(vs, rows):
    lead = vs[0].shape[:-1] if vs[0].ndim > 1 else ()
    flat = jnp.concatenate(vs, axis=-1)
    pad = rows * LANES - flat.shape[-1]
    flat = jnp.pad(flat, [(0, 0)] * len(lead) + [(0, pad)])
    return flat.reshape(lead + (rows, LANES))


def kernel(x, p, positions, mix_norm_w, w_in, conv_w, conv_b, dt_bias, a_log, d_skip, ssd_norm_w, q_a_norm_w, w_q_b, kv_a_norm_w, w_kv_b, w_out, ffn_norm_w, w_ffn_up, ffn_conv_w, ffn_conv_b, w_ffn_down, ple_norm_w, w_ple_gate, b_ple_gate, w_ple_proj, ple_post_norm_w, final_norm_w, loss_target, m_mix_norm_w, m_w_in, m_conv_w, m_conv_b, m_dt_bias, m_a_log, m_d_skip, m_ssd_norm_w, m_q_a_norm_w, m_w_q_b, m_kv_a_norm_w, m_w_kv_b, m_w_out, m_ffn_norm_w, m_w_ffn_up, m_ffn_conv_w, m_ffn_conv_b, m_w_ffn_down, m_ple_norm_w, m_w_ple_gate, m_b_ple_gate, m_w_ple_proj, m_ple_post_norm_w, m_final_norm_w, v_mix_norm_w, v_w_in, v_conv_w, v_conv_b, v_dt_bias, v_a_log, v_d_skip, v_ssd_norm_w, v_q_a_norm_w, v_w_q_b, v_kv_a_norm_w, v_w_kv_b, v_w_out, v_ffn_norm_w, v_w_ffn_up, v_ffn_conv_w, v_ffn_conv_b, v_w_ffn_down, v_ple_norm_w, v_w_ple_gate, v_b_ple_gate, v_w_ple_proj, v_ple_post_norm_w, v_final_norm_w):
    given = dict(locals())
    shapes = {n: given[n].shape for n in WEIGHTS}
    w2 = {n: given[n].reshape(given[n].shape[-2:] if n in BIG or n in CONV else (1, -1)) for n in WEIGHTS}
    m2 = {n: given['m_' + n].reshape(w2[n].shape) for n in WEIGHTS}
    v2 = {n: given['v_' + n].reshape(w2[n].shape) for n in WEIGHTS}
    me = 4 * lax.axis_index("x") + 2 * lax.axis_index("y") + lax.axis_index("c")

    core = lax.axis_index("c").astype(jnp.int32).reshape(1)

    def shards(grp, zero):
        return [(w2[n] + zero).astype(BF16) if n in BIG else w2[n] + zero for n in WEIGHT_GROUPS[grp]]

    first, token = _gather_start(shards('a', 0.0), name="gather_a_hop1")
    first, token = _gather_forward(first, token, name="gather_a_hop2")
    zero = token[0, 0]
    later = _sequencer_gather(shards('b', zero) + shards('c', zero), collective_id=1, name="gather_later")
    later = dict(zip(WEIGHT_GROUPS['b'] + WEIGHT_GROUPS['c'], later))

    def get_w(grp, after):
        if grp == 'a':
            lands = dict(zip(WEIGHT_GROUPS[grp], _gather_finish(first, token, name="gather_a_done")))
        else:
            lands = {n: later[n] for n in WEIGHT_GROUPS[grp]}
        return _assemble_weights(lands)

    scatters = {}

    def emit(grp, grads):
        scatters[grp], tok = _scatter_start([grads[n] for n in GRAD_GROUPS[grp]], name="scatter_" + grp + "_hop1")
        return tok[0, 0]

    def relay(grp, after):
        scatters[grp], tok = _scatter_forward(scatters[grp], after, core, name="scatter_" + grp + "_hop2")
        return tok[0, 0]

    vecs = {n: w2[n] for n in REPL}
    vecs['mix_norm_w'] = vecs['mix_norm_w'] + zero
    loss, dx, g_conv, g_vec = _local_step(x[0], p[0, 0], _rope_tables(positions), get_w, vecs, loss_target[0], emit,
                                          relay)
    n_small = sum(g_vec[n].shape[1] for n in REPL) + sum(g_conv[n].size for n in CONV) + 1
    rows_small = -(-n_small // (LANES * HALO)) * HALO
    small = _pack_rows([g_vec[n] for n in REPL] + [g_conv[n].reshape(1, -1) for n in CONV] + [loss], rows_small)
    all_small = _exchange([small], gather=True, name="gather_small_grads")[0].reshape(N_DEV, rows_small * LANES)

    out_g, out_d, out_m, out_v = {}, {}, {}, {}
    for grp, names in GRAD_GROUPS.items():
        received = _scatter_finish(scatters[grp], dx, name="scatter_" + grp + "_done")
        for n, parts in zip(names, received):
            out_g[n], out_d[n], out_m[n], out_v[n] = _adamw(parts, w2[n], m2[n], v2[n], name="adamw_" + n)
    pieces, off = [], 0
    for n in REPL:
        k = g_vec[n].shape[1]
        pieces.append(all_small[:, off:off + k])
        off += k
    for n in CONV:
        kw, cols = g_conv[n].shape
        full = all_small[:, off:off + kw * cols].reshape(N_DEV, kw, cols)
        mine = lax.dynamic_slice_in_dim(full, me * (cols // N_DEV), cols // N_DEV, axis=2)
        pieces.append(mine.reshape(N_DEV, kw * (cols // N_DEV)))
        off += kw * cols
    pieces.append(all_small[:, off:off + 1])
    small_names = REPL + CONV
    n_mine = sum(q.shape[1] for q in pieces)
    rows_mine = -(-n_mine // (LANES * HALO)) * HALO
    zero = jnp.zeros((1, 1), F32)
    packed = [_pack_rows([src[n].reshape(1, -1) for n in small_names] + [zero], rows_mine).reshape(rows_mine, LANES)
              for src in (w2, m2, v2)]
    sg, sd, sm, sv = _adamw(_pack_rows(pieces, rows_mine), *packed, name="adamw_small")
    off = 0
    for n in small_names:
        k = w2[n].size
        for dst, src in ((out_g, sg), (out_d, sd), (out_m, sm), (out_v, sv)):
            dst[n] = src.reshape(-1)[off:off + k].reshape(w2[n].shape)
        off += k
    total_loss = sg.reshape(-1)[off]

    outs = [total_loss, dx[None]]
    for res in (out_g, out_d, out_m, out_v):
        outs += [res[n].reshape(shapes[n]) for n in WEIGHTS]
    return tuple(outs)
```

```python
import functools
import math

import numpy as np
import jax
import jax.numpy as jnp
from jax import lax
from jax.experimental import pallas as pl
from jax.experimental.pallas import tpu as pltpu
from jax.experimental.pallas import tpu_sc as plsc

F32 = jnp.float32
BF16 = jnp.bfloat16
HI = lax.Precision.HIGHEST

D_MODEL = 2048
CHUNK = 64
D_SSM = 1024
SSD_P = 64
SSD_HEADS = 16
SSD_GROUPS = 2
SSD_N = 128
SSD_CONV = 4
SSD_CONV_DIM = D_SSM + 2 * SSD_GROUPS * SSD_N
MLA_HEADS = 8
MLA_NOPE = 128
MLA_ROPE = 64
MLA_V = 128
MLA_Q_RANK = 512
MLA_KV_RANK = 256
MLA_QK_PAD = 256
ROPE_THETA = 10000.0
D_FF = 5632
FFN_CONV = 3
PLE_DIM = 256
NORM_EPS = 1e-6
ADAM_LR, ADAM_B1, ADAM_B2, ADAM_EPS, ADAM_WD, ADAM_STEP = 0.001, 0.9, 0.999, 1e-08, 0.01, 10
N_DEV = 8

OFF_Z, OFF_XBC, OFF_QA, OFF_CKV, OFF_KR, OFF_DT, D_IN_PAD = 0, 1024, 2560, 3072, 3328, 3456, 3584
D_IN = 3408
LANES = 128
HALO = 8
VMEM_LIMIT = 56 * 1024 * 1024
FFN_TC = D_FF * 2 // N_DEV
FFN_PERM = (0, 4, 1, 5, 2, 6, 3, 7)
NEG = -1e30


def _cp(*sem):
    return pltpu.CompilerParams(dimension_semantics=tuple(sem), vmem_limit_bytes=VMEM_LIMIT)


def _tile(n, want):
    if n <= want:
        return n
    best = max(d for d in range(LANES, want + 1, LANES) if n % d == 0)
    return best


def _sigmoid(x):
    return 1.0 / (1.0 + jnp.exp(-x))


def _silu(x):
    return x * _sigmoid(x)


def _dsilu(x):
    s = _sigmoid(x)
    return s * (1.0 + x * (1.0 - s))


MM_TILE = 1408
MM_TK = 2816


def _matmul(a, b, *, ta=False, tb=False, out_dtype=F32, add=None, bias=None, tm=MM_TILE, tn=MM_TILE, tk=MM_TK, name,
            mnk=None, a_spec=None, b_spec=None, o_spec=None, o_shape=None):
    if mnk is None:
        m, k = (a.shape[1], a.shape[0]) if ta else a.shape
        n = b.shape[0] if tb else b.shape[1]
        assert k == (b.shape[1] if tb else b.shape[0])
    else:
        m, n, k = mnk
    tm, tn, tk = _tile(m, tm), _tile(n, tn), _tile(k, tk)
    nk = k // tk
    dims = (((0 if ta else 1,), (1 if tb else 0,)), ((), ()))

    def body(*refs):
        a_ref, b_ref = refs[0], refs[1]
        pos = 2
        add_ref = bias_ref = None
        if add is not None:
            add_ref = refs[pos]
            pos += 1
        if bias is not None:
            bias_ref = refs[pos]
            pos += 1
        o_ref = refs[pos]
        kk = pl.program_id(2)
        av = a_ref[...]
        bv = b_ref[...]
        av = av.reshape(av.shape[-2:]).astype(BF16)
        bv = bv.reshape(bv.shape[-2:]).astype(BF16)
        prod = lax.dot_general(av, bv, dims, preferred_element_type=F32)

        def finish(r):
            if bias_ref is not None:
                r = r + bias_ref[...]
            if add_ref is not None:
                r = r + add_ref[...].astype(F32)
            o_ref[...] = r.astype(out_dtype).reshape(o_ref.shape)

        if nk == 1:
            finish(prod)
        else:
            acc_ref = refs[pos + 1]

            @pl.when(kk == 0)
            def _():
                acc_ref[...] = prod

            @pl.when(kk > 0)
            def _():
                acc_ref[...] += prod

            @pl.when(kk == nk - 1)
            def _():
                finish(acc_ref[...])

    if a_spec is None:
        a_spec = (pl.BlockSpec((tk, tm), lambda i, j, kk: (kk, i)) if ta
                  else pl.BlockSpec((tm, tk), lambda i, j, kk: (i, kk)))
    if b_spec is None:
        b_spec = (pl.BlockSpec((tn, tk), lambda i, j, kk: (j, kk)) if tb
                  else pl.BlockSpec((tk, tn), lambda i, j, kk: (kk, j)))
    if o_spec is None:
        o_spec = pl.BlockSpec((tm, tn), lambda i, j, kk: (i, j))
    if o_shape is None:
        o_shape = (m, n)
    in_specs = [a_spec, b_spec]
    args = [a, b]
    if add is not None:
        in_specs.append(pl.BlockSpec((tm, tn), lambda i, j, kk: (i, j)))
        args.append(add)
    if bias is not None:
        in_specs.append(pl.BlockSpec((1, tn), lambda i, j, kk: (0, j)))
        args.append(bias)
    return pl.pallas_call(
        body, name=name, grid=(m // tm, n // tn, nk), in_specs=in_specs, out_specs=o_spec,
        out_shape=jax.ShapeDtypeStruct(o_shape, out_dtype),
        scratch_shapes=[pltpu.VMEM((tm, tn), F32)] if nk > 1 else [],
        compiler_params=_cp("parallel", "parallel", "arbitrary"),
    )(*args)


def _rmsnorm_fwd(x, w, *, width, cblk=0, out_dtype=BF16, tr=256, name):
    t = x.shape[0]

    def body(x_ref, w_ref, o_ref):
        xv = x_ref[...].astype(F32)
        r = lax.rsqrt(jnp.mean(xv * xv, axis=-1, keepdims=True) + NORM_EPS)
        o_ref[...] = (xv * r * w_ref[...]).astype(out_dtype)

    return pl.pallas_call(
        body, name=name, grid=(t // tr,),
        in_specs=[pl.BlockSpec((tr, width), lambda i: (i, cblk)), pl.BlockSpec((1, width), lambda i: (0, 0))],
        out_specs=pl.BlockSpec((tr, width), lambda i: (i, 0)),
        out_shape=jax.ShapeDtypeStruct((t, width), out_dtype),
        compiler_params=_cp("parallel"),
    )(x, w)


def _rmsnorm_bwd(x, w, dy, add=None, *, width, cblk=0, out_dtype=F32, tr=256, name):
    t = x.shape[0]

    def body(*refs):
        if add is None:
            x_ref, w_ref, dy_ref, dx_ref, dw_ref = refs
            add_ref = None
        else:
            x_ref, w_ref, dy_ref, add_ref, dx_ref, dw_ref = refs
        xv = x_ref[...].astype(F32)
        dyv = dy_ref[...].astype(F32)
        r = lax.rsqrt(jnp.mean(xv * xv, axis=-1, keepdims=True) + NORM_EPS)
        xh = xv * r
        g = dyv * w_ref[...]
        dx = r * (g - xh * jnp.mean(g * xh, axis=-1, keepdims=True))
        if add_ref is not None:
            dx = dx + add_ref[...].astype(F32)
        dx_ref[...] = dx.astype(out_dtype)

        @pl.when(pl.program_id(0) == 0)
        def _():
            dw_ref[...] = jnp.zeros_like(dw_ref)

        dw_ref[...] += jnp.sum(dyv * xh, axis=0, keepdims=True)

    in_specs = [pl.BlockSpec((tr, width), lambda i: (i, cblk)), pl.BlockSpec((1, width), lambda i: (0, 0)),
                pl.BlockSpec((tr, width), lambda i: (i, 0))]
    args = [x, w, dy]
    if add is not None:
        in_specs.append(pl.BlockSpec((tr, width), lambda i: (i, 0)))
        args.append(add)
    return pl.pallas_call(
        body, name=name, grid=(t // tr,), in_specs=in_specs,
        out_specs=[pl.BlockSpec((tr, width), lambda i: (i, 0)), pl.BlockSpec((1, width), lambda i: (0, 0))],
        out_shape=[jax.ShapeDtypeStruct((t, width), out_dtype), jax.ShapeDtypeStruct((1, width), F32)],
        compiler_params=_cp("arbitrary"),
    )(*args)


def _shift_down(prev_halo, cur, j):
    if j == 0:
        return cur
    ext = jnp.concatenate([prev_halo, cur], axis=0)
    return pltpu.roll(ext, j, axis=0)[HALO:]


def _shift_up(cur, next_halo, j):
    if j == 0:
        return cur
    ext = jnp.concatenate([cur, next_halo], axis=0)
    return pltpu.roll(ext, ext.shape[0] - j, axis=0)[:cur.shape[0]]


def _conv_rows(prev, cur, w, b, kw):
    shifted = [cur]
    out = b + w[kw - 1:kw] * cur
    for j in range(1, kw):
        sh = _shift_down(prev, cur, j)
        shifted.append(sh)
        out = out + w[kw - 1 - j:kw - j] * sh
    return out, shifted


def _act_fwd(c, glu):
    if glu:
        half = c.shape[1] // 2
        return _silu(c[:, :half]) * c[:, half:]
    return _silu(c)


def _act_bwd(c, dout, glu):
    if glu:
        half = c.shape[1] // 2
        g, up = c[:, :half], c[:, half:]
        return jnp.concatenate([dout * up * _dsilu(g), dout * _silu(g)], axis=1)
    return dout * _dsilu(c)


def _conv_act_fwd(u, w, b, *, kw, glu, tc, coff, ncols, out_dtype, tr=256, name):
    t = u.shape[0]
    nb = ncols // tc
    oc = tc // 2 if glu else tc

    def body(u_ref, uh_ref, w_ref, b_ref, o_ref):
        prev = jnp.where(pl.program_id(0) == 0, 0.0, uh_ref[...])
        c, _ = _conv_rows(prev, u_ref[...], w_ref[...], b_ref[...], kw)
        o_ref[...] = _act_fwd(c, glu).astype(out_dtype)

    return pl.pallas_call(
        body, name=name, grid=(t // tr, nb),
        in_specs=[pl.BlockSpec((tr, tc), lambda i, j: (i, j + coff)),
                  pl.BlockSpec((HALO, tc), lambda i, j: (jnp.maximum(i * (tr // HALO) - 1, 0), j + coff)),
                  pl.BlockSpec((kw, tc), lambda i, j: (0, j)), pl.BlockSpec((1, tc), lambda i, j: (0, j))],
        out_specs=pl.BlockSpec((tr, oc), lambda i, j: (i, j)),
        out_shape=jax.ShapeDtypeStruct((t, nb * oc), out_dtype),
        compiler_params=_cp("parallel", "parallel"),
    )(u, u, w, b)


def _conv_act_bwd(u, w, b, dout, *, kw, glu, tc, coff, ncols, tr=256, name):
    t = u.shape[0]
    nb = ncols // tc
    nt = t // tr
    oc = tc // 2 if glu else tc

    def body(u_ref, up_ref, un_ref, d_ref, dn_ref, w_ref, b_ref, du_ref, dw_ref, db_ref):
        i = pl.program_id(1)
        cur, nxt, wv, bv = u_ref[...], un_ref[...], w_ref[...], b_ref[...]
        prev = jnp.where(i == 0, 0.0, up_ref[...])
        c_cur, shifted = _conv_rows(prev, cur, wv, bv, kw)
        c_nxt, _ = _conv_rows(cur[tr - HALO:], nxt, wv, bv, kw)
        d_cur = _act_bwd(c_cur, d_ref[...].astype(F32), glu)
        d_nxt = _act_bwd(c_nxt, jnp.where(i == nt - 1, 0.0, dn_ref[...].astype(F32)), glu)
        du = wv[kw - 1:kw] * d_cur
        for j in range(1, kw):
            du = du + wv[kw - 1 - j:kw - j] * _shift_up(d_cur, d_nxt, j)
        du_ref[...] = du.astype(BF16)

        @pl.when(i == 0)
        def _():
            dw_ref[...] = jnp.zeros_like(dw_ref)
            db_ref[...] = jnp.zeros_like(db_ref)

        db_ref[...] += jnp.sum(d_cur, axis=0, keepdims=True)
        dw_ref[...] += jnp.concatenate(
            [jnp.sum(d_cur * shifted[kw - 1 - k], axis=0, keepdims=True) for k in range(kw)], axis=0)

    nh = tr // HALO
    return pl.pallas_call(
        body, name=name, grid=(nb, nt),
        in_specs=[pl.BlockSpec((tr, tc), lambda j, i: (i, j + coff)),
                  pl.BlockSpec((HALO, tc), lambda j, i: (jnp.maximum(i * nh - 1, 0), j + coff)),
                  pl.BlockSpec((HALO, tc), lambda j, i: (jnp.minimum((i + 1) * nh, t // HALO - 1), j + coff)),
                  pl.BlockSpec((tr, oc), lambda j, i: (i, j)),
                  pl.BlockSpec((HALO, oc), lambda j, i: (jnp.minimum((i + 1) * nh, t // HALO - 1), j)),
                  pl.BlockSpec((kw, tc), lambda j, i: (0, j)), pl.BlockSpec((1, tc), lambda j, i: (0, j))],
        out_specs=[pl.BlockSpec((tr, tc), lambda j, i: (i, j)), pl.BlockSpec((kw, tc), lambda j, i: (0, j)),
                   pl.BlockSpec((1, tc), lambda j, i: (0, j))],
        out_shape=[jax.ShapeDtypeStruct((t, ncols), BF16), jax.ShapeDtypeStruct((kw, ncols), F32),
                   jax.ShapeDtypeStruct((1, ncols), F32)],
        compiler_params=_cp("parallel", "arbitrary"),
    )(u, u, u, dout, dout, w, b)


def _ple_fwd(x2, gl, pe, pw, *, tr=256, name):
    t, d = x2.shape

    def body(x_ref, gl_ref, pe_ref, pw_ref, o_ref):
        pv = pe_ref[...]
        r = lax.rsqrt(jnp.mean(pv * pv, axis=-1, keepdims=True) + NORM_EPS)
        o_ref[...] = x_ref[...] + _sigmoid(gl_ref[...]) * (pv * r * pw_ref[...])

    blk = pl.BlockSpec((tr, d), lambda i: (i, 0))
    return pl.pallas_call(
        body, name=name, grid=(t // tr,), in_specs=[blk, blk, blk, pl.BlockSpec((1, d), lambda i: (0, 0))],
        out_specs=blk, out_shape=jax.ShapeDtypeStruct((t, d), F32), compiler_params=_cp("parallel"),
    )(x2, gl, pe, pw)


def _ple_bwd(dx3, gl, pe, pw, *, tr=256, name):
    t, d = dx3.shape

    def body(dx_ref, gl_ref, pe_ref, pw_ref, dgl_ref, db_ref, dpe_ref, dpw_ref):
        dx, pv, pwv = dx_ref[...], pe_ref[...], pw_ref[...]
        gate = _sigmoid(gl_ref[...])
        r = lax.rsqrt(jnp.mean(pv * pv, axis=-1, keepdims=True) + NORM_EPS)
        ph = pv * r
        dgl = dx * (ph * pwv) * gate * (1.0 - gate)
        de = dx * gate
        g = de * pwv
        dgl_ref[...] = dgl.astype(BF16)
        dpe_ref[...] = (r * (g - ph * jnp.mean(g * ph, axis=-1, keepdims=True))).astype(BF16)

        @pl.when(pl.program_id(0) == 0)
        def _():
            db_ref[...] = jnp.zeros_like(db_ref)
            dpw_ref[...] = jnp.zeros_like(dpw_ref)

        db_ref[...] += jnp.sum(dgl, axis=0, keepdims=True)
        dpw_ref[...] += jnp.sum(de * ph, axis=0, keepdims=True)

    blk = pl.BlockSpec((tr, d), lambda i: (i, 0))
    row = pl.BlockSpec((1, d), lambda i: (0, 0))
    return pl.pallas_call(
        body, name=name, grid=(t // tr,), in_specs=[blk, blk, blk, row], out_specs=[blk, row, blk, row],
        out_shape=[jax.ShapeDtypeStruct((t, d), BF16), jax.ShapeDtypeStruct((1, d), F32),
                   jax.ShapeDtypeStruct((t, d), BF16), jax.ShapeDtypeStruct((1, d), F32)],
        compiler_params=_cp("arbitrary"),
    )(dx3, gl, pe, pw)


def _loss_head(x3, fw, target, *, tr=256, name):
    t, d = x3.shape

    def body(x_ref, w_ref, t_ref, l_ref, dx_ref, dw_ref):
        xv, wv = x_ref[...], w_ref[...]
        r = lax.rsqrt(jnp.mean(xv * xv, axis=-1, keepdims=True) + NORM_EPS)
        xh = xv * r
        err = xh * wv - t_ref[...]
        dy = err * (1.0 / d)
        g = dy * wv
        dx_ref[...] = r * (g - xh * jnp.mean(g * xh, axis=-1, keepdims=True))

        @pl.when(pl.program_id(0) == 0)
        def _():
            l_ref[...] = jnp.zeros_like(l_ref)
            dw_ref[...] = jnp.zeros_like(dw_ref)

        l_ref[...] += 0.5 * jnp.sum(jnp.mean(err * err, axis=-1, keepdims=True), axis=0, keepdims=True)
        dw_ref[...] += jnp.sum(dy * xh, axis=0, keepdims=True)

    blk = pl.BlockSpec((tr, d), lambda i: (i, 0))
    row = pl.BlockSpec((1, d), lambda i: (0, 0))
    return pl.pallas_call(
        body, name=name, grid=(t // tr,), in_specs=[blk, row, blk],
        out_specs=[pl.BlockSpec((1, 1), lambda i: (0, 0)), blk, row],
        out_shape=[jax.ShapeDtypeStruct((1, 1), F32), jax.ShapeDtypeStruct((t, d), F32),
                   jax.ShapeDtypeStruct((1, d), F32)],
        compiler_params=_cp("arbitrary"),
    )(x3, fw, target)


def _rope(blk, tab_ref):
    return blk * tab_ref[0] + pltpu.roll(blk, 96, axis=1) * tab_ref[1] + pltpu.roll(blk, 32, axis=1) * tab_ref[2]


def _unrope(g, tab_ref):
    return g * tab_ref[0] + pltpu.roll(g * tab_ref[1], 32, axis=1) + pltpu.roll(g * tab_ref[2], 96, axis=1)


def _mla_prep(q, kv, proj, tabs, *, tr=512, name):
    t = q.shape[0]

    def body(q_ref, kv_ref, kr_ref, tab_ref, qo_ref, ko_ref, vo_ref):
        qv, kvv = q_ref[...], kv_ref[...]
        qo_ref[0, :, :MLA_NOPE] = qv[:, :MLA_NOPE].astype(BF16)
        qo_ref[0, :, MLA_NOPE:] = _rope(qv[:, MLA_NOPE:], tab_ref).astype(BF16)
        ko_ref[0, :, :MLA_NOPE] = kvv[:, :MLA_NOPE].astype(BF16)
        ko_ref[0, :, MLA_NOPE:] = _rope(kr_ref[...], tab_ref).astype(BF16)
        vo_ref[0] = kvv[:, MLA_NOPE:].astype(BF16)

    return pl.pallas_call(
        body, name=name, grid=(t // tr, MLA_HEADS),
        in_specs=[pl.BlockSpec((tr, MLA_QK_PAD), lambda i, h: (i, h)),
                  pl.BlockSpec((tr, MLA_NOPE + MLA_V), lambda i, h: (i, h)),
                  pl.BlockSpec((tr, LANES), lambda i, h: (i, OFF_KR // LANES)),
                  pl.BlockSpec((3, tr, LANES), lambda i, h: (0, i, 0))],
        out_specs=[pl.BlockSpec((1, tr, MLA_QK_PAD), lambda i, h: (h, i, 0)),
                   pl.BlockSpec((1, tr, MLA_QK_PAD), lambda i, h: (h, i, 0)),
                   pl.BlockSpec((1, tr, MLA_V), lambda i, h: (h, i, 0))],
        out_shape=[jax.ShapeDtypeStruct((MLA_HEADS, t, MLA_QK_PAD), BF16),
                   jax.ShapeDtypeStruct((MLA_HEADS, t, MLA_QK_PAD), BF16),
                   jax.ShapeDtypeStruct((MLA_HEADS, t, MLA_V), BF16)],
        compiler_params=_cp("parallel", "parallel"),
    )(q, kv, proj, tabs)


def _mla_unprep(dq3, dk3, dv3, tabs, *, tr=256, name):
    t = dq3.shape[1]

    def body(dq_ref, dk_ref, dv_ref, tab_ref, qo_ref, kvo_ref, kro_ref):
        kr = jnp.zeros((tr, LANES), F32)
        for h in range(MLA_HEADS):
            c0 = h * MLA_QK_PAD
            qo_ref[:, c0:c0 + MLA_NOPE] = dq_ref[h, :, :MLA_NOPE].astype(BF16)
            qo_ref[:, c0 + MLA_NOPE:c0 + MLA_QK_PAD] = _unrope(dq_ref[h, :, MLA_NOPE:], tab_ref).astype(BF16)
            kvo_ref[:, c0:c0 + MLA_NOPE] = dk_ref[h, :, :MLA_NOPE].astype(BF16)
            kvo_ref[:, c0 + MLA_NOPE:c0 + MLA_QK_PAD] = dv_ref[h].astype(BF16)
            kr = kr + dk_ref[h, :, MLA_NOPE:]
        kro_ref[...] = _unrope(kr, tab_ref).astype(BF16)

    return pl.pallas_call(
        body, name=name, grid=(t // tr,),
        in_specs=[pl.BlockSpec((MLA_HEADS, tr, MLA_QK_PAD), lambda i: (0, i, 0)),
                  pl.BlockSpec((MLA_HEADS, tr, MLA_QK_PAD), lambda i: (0, i, 0)),
                  pl.BlockSpec((MLA_HEADS, tr, MLA_V), lambda i: (0, i, 0)),
                  pl.BlockSpec((3, tr, LANES), lambda i: (0, i, 0))],
        out_specs=[pl.BlockSpec((tr, MLA_HEADS * MLA_QK_PAD), lambda i: (i, 0)),
                   pl.BlockSpec((tr, MLA_HEADS * MLA_QK_PAD), lambda i: (i, 0)),
                   pl.BlockSpec((tr, LANES), lambda i: (i, 0))],
        out_shape=[jax.ShapeDtypeStruct((t, MLA_HEADS * MLA_QK_PAD), BF16),
                   jax.ShapeDtypeStruct((t, MLA_HEADS * MLA_QK_PAD), BF16),
                   jax.ShapeDtypeStruct((t, LANES), BF16)],
        compiler_params=_cp("parallel"),
    )(dq3, dk3, dv3, tabs)


ATT_BLK = 256
ATT_SCALE = 1.0 / math.sqrt(MLA_NOPE + MLA_ROPE)
_NT = (((1,), (1,)), ((), ()))
_TN = (((0,), (0,)), ((), ()))


def _att_scores(q, k, diagonal):
    s = lax.dot_general(q, k, _NT, preferred_element_type=F32) * ATT_SCALE
    if not diagonal:
        return s
    row = lax.broadcasted_iota(jnp.int32, s.shape, 0)
    col = lax.broadcasted_iota(jnp.int32, s.shape, 1)
    return jnp.where((col >> 6) <= (row >> 6), s, NEG)


def _att_rows(i):
    return pl.ds(pl.multiple_of(i * ATT_BLK, ATT_BLK), ATT_BLK)


def _attn_fwd(q3, k3, v3, *, name):
    t = q3.shape[1]
    nq = t // ATT_BLK

    def body(q_ref, k_ref, v_ref, o_ref, lse_ref):
        qi = pl.program_id(1)
        q = q_ref[0]

        def step(j, carry, diagonal=False):
            m, l, acc = carry
            s = _att_scores(q, k_ref[0, _att_rows(j), :], diagonal)
            m_new = jnp.maximum(m, jnp.max(s, axis=-1, keepdims=True))
            p = jnp.exp(s - m_new)
            alpha = jnp.exp(m - m_new)
            l = alpha * l + jnp.sum(p, axis=-1, keepdims=True)
            acc = alpha * acc + jnp.dot(p.astype(BF16), v_ref[0, _att_rows(j), :], preferred_element_type=F32)
            return m_new, l, acc

        init = (jnp.full((ATT_BLK, 1), NEG, F32), jnp.zeros((ATT_BLK, 1), F32), jnp.zeros((ATT_BLK, MLA_V), F32))
        m, l, acc = step(qi, lax.fori_loop(0, qi, step, init), diagonal=True)
        o_ref[...] = acc / l
        lse_ref[0] = m + jnp.log(l)

    return pl.pallas_call(
        body, name=name, grid=(MLA_HEADS, nq),
        in_specs=[pl.BlockSpec((1, ATT_BLK, MLA_QK_PAD), lambda h, i: (h, i, 0)),
                  pl.BlockSpec((1, t, MLA_QK_PAD), lambda h, i: (h, 0, 0)),
                  pl.BlockSpec((1, t, MLA_V), lambda h, i: (h, 0, 0))],
        out_specs=[pl.BlockSpec((ATT_BLK, MLA_V), lambda h, i: (i, h)),
                   pl.BlockSpec((1, ATT_BLK, 1), lambda h, i: (h, i, 0))],
        out_shape=[jax.ShapeDtypeStruct((t, MLA_HEADS * MLA_V), F32), jax.ShapeDtypeStruct((MLA_HEADS, t, 1), F32)],
        compiler_params=_cp("parallel", "parallel"),
    )(q3, k3, v3)


def _attn_bwd(q3, k3, v3, o, dcat, lse, *, name):
    t = q3.shape[1]
    nq = t // ATT_BLK

    def body(q_ref, k_ref, v_ref, o_ref, do_ref, lse_ref, dq_ref, dk_ref, dv_ref, delta_ref):
        kj = pl.program_id(1)
        k, v = k_ref[0], v_ref[0]

        @pl.when(kj == 0)
        def _():
            dq_ref[...] = jnp.zeros_like(dq_ref)
            delta_ref[...] = jnp.sum(o_ref[...] * do_ref[...], axis=-1, keepdims=True)

        def step(i, carry, diagonal=False):
            dk, dv = carry
            rows = _att_rows(i)
            q = q_ref[0, rows, :]
            dob = do_ref[rows, :].astype(BF16)
            p = jnp.exp(_att_scores(q, k, diagonal) - lse_ref[0, rows, :])
            dv = dv + lax.dot_general(p.astype(BF16), dob, _TN, preferred_element_type=F32)
            dp = lax.dot_general(dob, v, _NT, preferred_element_type=F32)
            ds = (p * (dp - delta_ref[rows, :]) * ATT_SCALE).astype(BF16)
            dk = dk + lax.dot_general(ds, q, _TN, preferred_element_type=F32)
            dq_ref[0, rows, :] += jnp.dot(ds, k, preferred_element_type=F32)
            return dk, dv

        init = (jnp.zeros((ATT_BLK, MLA_QK_PAD), F32), jnp.zeros((ATT_BLK, MLA_V), F32))
        dk, dv = lax.fori_loop(kj + 1, nq, step, step(kj, init, diagonal=True))
        dk_ref[0] = dk
        dv_ref[0] = dv

    return pl.pallas_call(
        body, name=name, grid=(MLA_HEADS, nq),
        in_specs=[pl.BlockSpec((1, t, MLA_QK_PAD), lambda h, j: (h, 0, 0)),
                  pl.BlockSpec((1, ATT_BLK, MLA_QK_PAD), lambda h, j: (h, j, 0)),
                  pl.BlockSpec((1, ATT_BLK, MLA_V), lambda h, j: (h, j, 0)),
                  pl.BlockSpec((t, MLA_V), lambda h, j: (0, h)),
                  pl.BlockSpec((t, MLA_V), lambda h, j: (0, MLA_HEADS + h)),
                  pl.BlockSpec((1, t, 1), lambda h, j: (h, 0, 0))],
        out_specs=[pl.BlockSpec((1, t, MLA_QK_PAD), lambda h, j: (h, 0, 0)),
                   pl.BlockSpec((1, ATT_BLK, MLA_QK_PAD), lambda h, j: (h, j, 0)),
                   pl.BlockSpec((1, ATT_BLK, MLA_V), lambda h, j: (h, j, 0))],
        out_shape=[jax.ShapeDtypeStruct((MLA_HEADS, t, MLA_QK_PAD), F32),
                   jax.ShapeDtypeStruct((MLA_HEADS, t, MLA_QK_PAD), F32),
                   jax.ShapeDtypeStruct((MLA_HEADS, t, MLA_V), F32)],
        scratch_shapes=[pltpu.VMEM((t, 1), F32)],
        compiler_params=_cp("parallel", "arbitrary"),
    )(q3, k3, v3, o, dcat, lse)


def _ssd_prep(proj, bias128, alog128, *, name):
    t = proj.shape[0]
    nc = t // CHUNK

    def body(raw_ref, b_ref, al_ref, dt_ref, cs_ref, a_ref):
        xv = raw_ref[...] + b_ref[...]
        dt = jnp.maximum(xv, 0.0) + jnp.log(1.0 + jnp.exp(-jnp.abs(xv)))
        a = -jnp.exp(al_ref[...])
        adt = (dt * a).reshape(nc, CHUNK, LANES)
        li = lax.broadcasted_iota(jnp.int32, (nc, CHUNK, CHUNK), 1)
        si = lax.broadcasted_iota(jnp.int32, (nc, CHUNK, CHUNK), 2)
        tril = jnp.where(si <= li, 1.0, 0.0).astype(F32)
        cs = lax.dot_general(tril, adt, (((2,), (1,)), ((0,), (0,))), precision=HI, preferred_element_type=F32)
        dt_ref[...] = dt
        cs_ref[...] = cs.reshape(t, LANES)
        a_ref[...] = a

    blk = pl.BlockSpec((t, LANES), lambda i: (0, 0))
    row = pl.BlockSpec((1, LANES), lambda i: (0, 0))
    return pl.pallas_call(
        body, name=name, grid=(1,),
        in_specs=[pl.BlockSpec((t, LANES), lambda i: (0, OFF_DT // LANES)), row, row],
        out_specs=[blk, blk, row],
        out_shape=[jax.ShapeDtypeStruct((t, LANES), F32), jax.ShapeDtypeStruct((t, LANES), F32),
                   jax.ShapeDtypeStruct((1, LANES), F32)],
        compiler_params=_cp("arbitrary"),
    )(proj, bias128, alog128)


def _ssd_prep_bwd(ddt128, dadt128, proj, bias128, dt128, a128, dd_h, *, name):
    t = proj.shape[0]

    def body(ddt_ref, dadt_ref, raw_ref, b_ref, dt_ref, a_ref, dd_ref, draw_ref, db_ref, dal_ref, dds_ref):
        draw = ddt_ref[...] * _sigmoid(raw_ref[...] + b_ref[...])
        draw_ref[...] = draw.astype(BF16)
        db_ref[...] = jnp.sum(draw, axis=0, keepdims=True)
        dal_ref[...] = jnp.sum(dadt_ref[...] * dt_ref[...], axis=0, keepdims=True) * a_ref[...]
        dds_ref[...] = jnp.sum(dd_ref[...], axis=-1, keepdims=True)

    blk = pl.BlockSpec((t, LANES), lambda i: (0, 0))
    row = pl.BlockSpec((1, LANES), lambda i: (0, 0))
    return pl.pallas_call(
        body, name=name, grid=(1,),
        in_specs=[blk, blk, pl.BlockSpec((t, LANES), lambda i: (0, OFF_DT // LANES)), row, blk, row,
                  pl.BlockSpec((SSD_HEADS, SSD_P), lambda i: (0, 0))],
        out_specs=[blk, row, row, pl.BlockSpec((SSD_HEADS, 1), lambda i: (0, 0))],
        out_shape=[jax.ShapeDtypeStruct((t, LANES), BF16), jax.ShapeDtypeStruct((1, LANES), F32),
                   jax.ShapeDtypeStruct((1, LANES), F32), jax.ShapeDtypeStruct((SSD_HEADS, 1), F32)],
        compiler_params=_cp("arbitrary"),
    )(ddt128, dadt128, proj, bias128, dt128, a128, dd_h)


def _bdot(a, b, ca, cb, precision=None):
    return lax.dot_general(a, b, (((ca,), (cb,)), ((0,), (0,))), precision=precision, preferred_element_type=F32)


def _ssd_common(xs_ref, dt_ref, cs_ref, csr_ref, b_ref, c_ref, nc):
    x = xs_ref[0].reshape(nc, CHUNK, SSD_P)
    dt = dt_ref[0].reshape(nc, CHUNK, SSD_P)
    cs = cs_ref[0].reshape(nc, CHUNK, SSD_P)
    csr = csr_ref[0]
    bm = b_ref[0].reshape(nc, CHUNK, SSD_N).astype(BF16)
    cm = c_ref[0].reshape(nc, CHUNK, SSD_N).astype(BF16)
    li = lax.broadcasted_iota(jnp.int32, (nc, CHUNK, CHUNK), 1)
    si = lax.broadcasted_iota(jnp.int32, (nc, CHUNK, CHUNK), 2)
    lmat = jnp.exp(jnp.where(si <= li, cs - csr, NEG))
    g = _bdot(cm, bm, 2, 2)
    cs_last = jnp.sum(jnp.where(li == CHUNK - 1, cs, 0.0), axis=1, keepdims=True)
    xdt = x * dt
    dec = jnp.exp(cs_last - cs)
    return x, dt, cs, bm, cm, li, si, lmat, g, cs_last, xdt, dec


def _ssd_fwd(xs_h, dt_h, cs_h, cs_row, b_g, c_g, dskip_h, *, name):
    t = xs_h.shape[1]
    nc = t // CHUNK
    hpg = SSD_HEADS // SSD_GROUPS

    def body(xs_ref, dt_ref, cs_ref, csr_ref, b_ref, c_ref, dk_ref, y_ref, st_ref, sc_ref, cd_ref):
        x, dt, cs, bm, cm, li, si, lmat, g, cs_last, xdt, dec = _ssd_common(xs_ref, dt_ref, cs_ref, csr_ref, b_ref,
                                                                           c_ref, nc)
        yd = _bdot((g * lmat).astype(BF16), xdt.astype(BF16), 2, 1)
        sc_ref[...] = _bdot(bm, (dec * xdt).astype(BF16), 1, 1)
        cd_ref[...] = jnp.exp(cs_last)

        def step(c, s):
            st_ref[0, c] = s
            return s * cd_ref[c] + sc_ref[c]

        lax.fori_loop(0, nc, step, jnp.zeros((SSD_N, SSD_P), F32))
        yo = _bdot(cm, st_ref[0].astype(BF16), 2, 1) * jnp.exp(cs)
        y_ref[0] = (yd + yo + dk_ref[0] * x).reshape(t, SSD_P)

    head = pl.BlockSpec((1, t, SSD_P), lambda h: (h, 0, 0))
    grp = pl.BlockSpec((1, t, SSD_N), lambda h: (h // hpg, 0, 0))
    return pl.pallas_call(
        body, name=name, grid=(SSD_HEADS,),
        in_specs=[head, head, head, pl.BlockSpec((1, nc, 1, CHUNK), lambda h: (h, 0, 0, 0)), grp, grp,
                  pl.BlockSpec((1, 1, SSD_P), lambda h: (h, 0, 0))],
        out_specs=[head, pl.BlockSpec((1, nc, SSD_N, SSD_P), lambda h: (h, 0, 0, 0))],
        out_shape=[jax.ShapeDtypeStruct((SSD_HEADS, t, SSD_P), F32),
                   jax.ShapeDtypeStruct((SSD_HEADS, nc, SSD_N, SSD_P), F32)],
        scratch_shapes=[pltpu.VMEM((nc, SSD_N, SSD_P), F32), pltpu.VMEM((nc, 1, SSD_P), F32)],
        compiler_params=_cp("parallel"),
    )(xs_h, dt_h, cs_h, cs_row, b_g, c_g, dskip_h)


def _ssd_bwd(xs_h, dt_h, cs_h, cs_row, b_g, c_g, dskip_h, a_h, states, dy_h, *, name):
    t = xs_h.shape[1]
    nc = t // CHUNK
    hpg = SSD_HEADS // SSD_GROUPS

    def body(xs_ref, dt_ref, cs_ref, csr_ref, b_ref, c_ref, dk_ref, a_ref, st_ref, dy_ref,
             dxs_ref, ddt_ref, dadt_ref, db_ref, dc_ref, dd_ref, dsl_ref, dsc_ref, cd_ref):
        x, dt, cs, bm, cm, li, si, lmat, g, cs_last, xdt, dec = _ssd_common(xs_ref, dt_ref, cs_ref, csr_ref, b_ref,
                                                                           c_ref, nc)
        dy = dy_ref[0].reshape(nc, CHUNK, SSD_P)
        dyb = dy.astype(BF16)
        xdtb = xdt.astype(BF16)
        sprev = st_ref[0]
        sprevb = sprev.astype(BF16)
        cdec = jnp.exp(cs_last)
        ecs = jnp.exp(cs)
        dw = (ecs * dy).astype(BF16)
        wmat = _bdot(cm, sprevb, 2, 1)
        dcs = jnp.sum(dy * ecs * wmat, axis=2, keepdims=True)
        dcm = _bdot(dw, sprevb, 2, 2)
        dsl_ref[...] = _bdot(cm, dw, 1, 1)
        cd_ref[...] = cdec

        def step(k, ds):
            c = nc - 1 - k
            dsc_ref[c] = ds
            return ds * cd_ref[c] + dsl_ref[c]

        lax.fori_loop(0, nc, step, jnp.zeros((SSD_N, SSD_P), F32))
        dsc = dsc_ref[...]
        dscb = dsc.astype(BF16)
        d_last = jnp.sum(jnp.sum(dsc * sprev, axis=1, keepdims=True) * cdec, axis=2, keepdims=True)
        z = dec * xdt
        dbm = _bdot(z.astype(BF16), dscb, 2, 2)
        dz = _bdot(bm, dscb, 2, 1)
        dxdt = dec * dz
        t2 = jnp.sum(dz * z, axis=2, keepdims=True)
        dcs = dcs - t2
        d_last = d_last + jnp.sum(t2, axis=1, keepdims=True)
        m = g * lmat
        mb = m.astype(BF16)
        dm = _bdot(dyb, xdtb, 2, 2)
        dxdt = dxdt + _bdot(mb, dyb, 1, 1)
        dseg = dm * m
        dcs = dcs + jnp.sum(dseg, axis=2, keepdims=True)
        ones = jnp.ones((nc, CHUNK, SSD_P), F32)
        dcs = dcs - _bdot(dseg, ones, 1, 1, precision=HI)
        dg = (dm * lmat).astype(BF16)
        dcm = dcm + _bdot(dg, bm, 2, 1)
        dbm = dbm + _bdot(dg, cm, 1, 1)
        dcs = dcs + jnp.where(li[:, :, :SSD_P] == CHUNK - 1, d_last, 0.0)
        triu = jnp.where(li <= si, 1.0, 0.0).astype(F32)
        dadt = _bdot(triu, dcs, 2, 1, precision=HI)
        dk = dk_ref[0]
        dxs_ref[0] = (dxdt * dt + dk * dy).reshape(t, SSD_P)
        ddt_ref[0] = (jnp.sum(dxdt * x, axis=2, keepdims=True) + dadt * a_ref[0]).reshape(t, SSD_P)
        dadt_ref[0] = dadt.reshape(t, SSD_P)
        dd_ref[0] = jnp.sum(jnp.sum(dy * x, axis=1, keepdims=True), axis=0)

        @pl.when(pl.program_id(1) == 0)
        def _():
            db_ref[...] = jnp.zeros_like(db_ref)
            dc_ref[...] = jnp.zeros_like(dc_ref)

        db_ref[0] += dbm.reshape(t, SSD_N)
        dc_ref[0] += dcm.reshape(t, SSD_N)

    head = pl.BlockSpec((1, t, SSD_P), lambda gi, hi: (gi * hpg + hi, 0, 0))
    grp = pl.BlockSpec((1, t, SSD_N), lambda gi, hi: (gi, 0, 0))
    lane = pl.BlockSpec((1, 1, SSD_P), lambda gi, hi: (gi * hpg + hi, 0, 0))
    return pl.pallas_call(
        body, name=name, grid=(SSD_GROUPS, hpg),
        in_specs=[head, head, head, pl.BlockSpec((1, nc, 1, CHUNK), lambda gi, hi: (gi * hpg + hi, 0, 0, 0)),
                  grp, grp, lane, lane, pl.BlockSpec((1, nc, SSD_N, SSD_P), lambda gi, hi: (gi * hpg + hi, 0, 0, 0)),
                  head],
        out_specs=[head, head, head, grp, grp, lane],
        out_shape=[jax.ShapeDtypeStruct((SSD_HEADS, t, SSD_P), F32)] * 3
        + [jax.ShapeDtypeStruct((SSD_GROUPS, t, SSD_N), F32)] * 2
        + [jax.ShapeDtypeStruct((SSD_HEADS, 1, SSD_P), F32)],
        scratch_shapes=[pltpu.VMEM((nc, SSD_N, SSD_P), F32), pltpu.VMEM((nc, SSD_N, SSD_P), F32),
                        pltpu.VMEM((nc, 1, SSD_P), F32)],
        compiler_params=_cp("parallel", "arbitrary"),
    )(xs_h, dt_h, cs_h, cs_row, b_g, c_g, dskip_h, a_h, states, dy_h)


def _ssd_gate_fwd(y, proj, w, *, tr=256, name):
    t = y.shape[0]
    gw = D_SSM // SSD_GROUPS

    def body(y_ref, z_ref, w_ref, o_ref):
        v = y_ref[...] * _silu(z_ref[...])
        for gi in range(SSD_GROUPS):
            vg = v[:, gi * gw:(gi + 1) * gw]
            r = lax.rsqrt(jnp.mean(vg * vg, axis=-1, keepdims=True) + NORM_EPS)
            o_ref[:, gi * gw:(gi + 1) * gw] = (vg * r * w_ref[:, gi * gw:(gi + 1) * gw]).astype(BF16)

    blk = pl.BlockSpec((tr, D_SSM), lambda i: (i, 0))
    return pl.pallas_call(
        body, name=name, grid=(t // tr,), in_specs=[blk, blk, pl.BlockSpec((1, D_SSM), lambda i: (0, 0))],
        out_specs=blk, out_shape=jax.ShapeDtypeStruct((t, D_SSM), BF16), compiler_params=_cp("parallel"),
    )(y, proj, w)


def _ssd_gate_bwd(y, proj, w, dcat, *, tr=256, name):
    t = y.shape[0]
    gw = D_SSM // SSD_GROUPS

    def body(y_ref, z_ref, w_ref, d_ref, dy_ref, dz_ref, dw_ref):
        yv, zv, dv = y_ref[...], z_ref[...], d_ref[...].astype(F32)
        sz = _silu(zv)
        v = yv * sz

        @pl.when(pl.program_id(0) == 0)
        def _():
            dw_ref[...] = jnp.zeros_like(dw_ref)

        for gi in range(SSD_GROUPS):
            sl = slice(gi * gw, (gi + 1) * gw)
            vg, dg = v[:, sl], dv[:, sl]
            r = lax.rsqrt(jnp.mean(vg * vg, axis=-1, keepdims=True) + NORM_EPS)
            vh = vg * r
            gg = dg * w_ref[:, sl]
            dvg = r * (gg - vh * jnp.mean(gg * vh, axis=-1, keepdims=True))
            dy_ref[:, sl] = dvg * sz[:, sl]
            dz_ref[:, sl] = (dvg * yv[:, sl] * _dsilu(zv[:, sl])).astype(BF16)
            dw_ref[:, sl] += jnp.sum(dg * vh, axis=0, keepdims=True)

    blk = pl.BlockSpec((tr, D_SSM), lambda i: (i, 0))
    row = pl.BlockSpec((1, D_SSM), lambda i: (0, 0))
    return pl.pallas_call(
        body, name=name, grid=(t // tr,), in_specs=[blk, blk, row, blk], out_specs=[blk, blk, row],
        out_shape=[jax.ShapeDtypeStruct((t, D_SSM), F32), jax.ShapeDtypeStruct((t, D_SSM), BF16),
                   jax.ShapeDtypeStruct((1, D_SSM), F32)],
        compiler_params=_cp("arbitrary"),
    )(y, proj, w, dcat)


def _pad_lanes(v):
    return jnp.pad(v, ((0, 0), (0, LANES - v.shape[1])))


def _to_heads(v):
    return v.reshape(v.shape[0], SSD_HEADS, SSD_P).transpose(1, 0, 2)


def _from_heads(v):
    return v.transpose(1, 0, 2).reshape(v.shape[1], SSD_HEADS * SSD_P)


def _per_head(v128, t):
    return jnp.broadcast_to(v128[:, :SSD_HEADS].T[:, :, None], (SSD_HEADS, t, SSD_P))


def _ssd_forward(proj, conv_w, conv_b, dt_bias, a_log, d_skip, ssd_norm_w):
    t = proj.shape[0]
    nc = t // CHUNK
    xbc = _conv_act_fwd(proj, conv_w, conv_b, kw=SSD_CONV, glu=False, tc=512, coff=OFF_XBC // 512,
                        ncols=SSD_CONV_DIM, out_dtype=F32, name="ssd_conv_fwd")
    bias128, alog128 = _pad_lanes(dt_bias), _pad_lanes(a_log)
    dt128, cs128, a128 = _ssd_prep(proj, bias128, alog128, name="ssd_prep")
    dt_h, cs_h = _per_head(dt128, t), _per_head(cs128, t)
    cs_row = cs128[:, :SSD_HEADS].T.reshape(SSD_HEADS, nc, 1, CHUNK)
    xs_h = _to_heads(xbc[:, :D_SSM])
    gn = SSD_GROUPS * SSD_N
    b_g = xbc[:, D_SSM:D_SSM + gn].reshape(t, SSD_GROUPS, SSD_N).transpose(1, 0, 2)
    c_g = xbc[:, D_SSM + gn:].reshape(t, SSD_GROUPS, SSD_N).transpose(1, 0, 2)
    dskip_h = jnp.broadcast_to(d_skip[0][:, None, None], (SSD_HEADS, 1, SSD_P))
    a_h = jnp.broadcast_to(a128[0, :SSD_HEADS][:, None, None], (SSD_HEADS, 1, SSD_P))
    y_h, states = _ssd_fwd(xs_h, dt_h, cs_h, cs_row, b_g, c_g, dskip_h, name="ssd_scan_fwd")
    y = _from_heads(y_h)
    y_ssd = _ssd_gate_fwd(y, proj, ssd_norm_w, name="ssd_gate_fwd")
    saved = (proj, conv_w, conv_b, ssd_norm_w, bias128, dt128, a128, dt_h, cs_h, cs_row, xs_h, b_g, c_g, dskip_h, a_h,
             states, y)
    return y_ssd, saved


def _ssd_backward(saved, dcat):
    (proj, conv_w, conv_b, ssd_norm_w, bias128, dt128, a128, dt_h, cs_h, cs_row, xs_h, b_g, c_g, dskip_h, a_h, states,
     y) = saved
    t = proj.shape[0]
    dy, dz, d_norm_w = _ssd_gate_bwd(y, proj, ssd_norm_w, dcat, name="ssd_gate_bwd")
    dxs_h, ddt_h, dadt_h, db_g, dc_g, dd_h = _ssd_bwd(xs_h, dt_h, cs_h, cs_row, b_g, c_g, dskip_h, a_h, states,
                                                      _to_heads(dy), name="ssd_scan_bwd")
    gn = SSD_GROUPS * SSD_N
    dxc = jnp.concatenate([_from_heads(dxs_h), db_g.transpose(1, 0, 2).reshape(t, gn),
                           dc_g.transpose(1, 0, 2).reshape(t, gn)], axis=1)
    dxbc, d_conv_w, d_conv_b = _conv_act_bwd(proj, conv_w, conv_b, dxc, kw=SSD_CONV, glu=False, tc=512,
                                             coff=OFF_XBC // 512, ncols=SSD_CONV_DIM, name="ssd_conv_bwd")
    ddt128 = _pad_lanes(ddt_h[:, :, 0].T)
    dadt128 = _pad_lanes(dadt_h[:, :, 0].T)
    d_raw, d_bias, d_alog, d_dskip = _ssd_prep_bwd(ddt128, dadt128, proj, bias128, dt128, a128,
                                                   dd_h.reshape(SSD_HEADS, SSD_P), name="ssd_prep_bwd")
    return (dz, dxbc, d_raw, d_norm_w, d_conv_w, d_conv_b, d_bias[:, :SSD_HEADS], d_alog[:, :SSD_HEADS],
            d_dskip.reshape(1, SSD_HEADS))


def _rope_tables(positions):
    inv_freq = ROPE_THETA ** (-jnp.arange(0, MLA_ROPE, 2, dtype=F32) / MLA_ROPE)
    ang = positions[0].astype(F32)[:, None] * inv_freq
    cos, sin = jnp.cos(ang), jnp.sin(ang)
    z = jnp.zeros_like(cos)
    return jnp.stack([jnp.concatenate([cos, cos, z, z], axis=1), jnp.concatenate([-sin, z, z, z], axis=1),
                      jnp.concatenate([z, sin, z, z], axis=1)])


def _mla_forward(proj, tabs, q_a_norm_w, wq_pad, kv_a_norm_w, wkv):
    qn = _rmsnorm_fwd(proj, q_a_norm_w, width=MLA_Q_RANK, cblk=OFF_QA // MLA_Q_RANK, name="q_a_norm")
    q = _matmul(qn, wq_pad, name="q_b_proj")
    kvn = _rmsnorm_fwd(proj, kv_a_norm_w, width=MLA_KV_RANK, cblk=OFF_CKV // MLA_KV_RANK, name="kv_a_norm")
    kv = _matmul(kvn, wkv, name="kv_b_proj")
    q3, k3, v3 = _mla_prep(q, kv, proj, tabs, name="mla_prep")
    o, lse = _attn_fwd(q3, k3, v3, name="attn_fwd")
    return o, (proj, tabs, q_a_norm_w, wq_pad, kv_a_norm_w, wkv, qn, kvn, q3, k3, v3, o, lse)


def _mla_backward(saved, dcat):
    proj, tabs, q_a_norm_w, wq_pad, kv_a_norm_w, wkv, qn, kvn, q3, k3, v3, o, lse = saved
    dq3, dk3, dv3 = _attn_bwd(q3, k3, v3, o, dcat, lse, name="attn_bwd")
    dq, dkv, dkr = _mla_unprep(dq3, dk3, dv3, tabs, name="mla_unprep")
    d_wq = _matmul(qn, dq, ta=True, out_dtype=BF16, name="d_w_q_b")
    dqn = _matmul(dq, wq_pad, tb=True, name="d_qn")
    dq_a, d_qnw = _rmsnorm_bwd(proj, q_a_norm_w, dqn, width=MLA_Q_RANK, cblk=OFF_QA // MLA_Q_RANK, out_dtype=BF16,
                               name="q_a_norm_bwd")
    d_wkv = _matmul(kvn, dkv, ta=True, out_dtype=BF16, name="d_w_kv_b")
    dkvn = _matmul(dkv, wkv, tb=True, name="d_kvn")
    dckv, d_kvnw = _rmsnorm_bwd(proj, kv_a_norm_w, dkvn, width=MLA_KV_RANK, cblk=OFF_CKV // MLA_KV_RANK,
                                out_dtype=BF16, name="kv_a_norm_bwd")
    return dq_a, dckv, dkr, d_wq, d_wkv, d_qnw, d_kvnw


def _pad_w_q(w):
    r = w.shape[0]
    w3 = w.reshape(r, MLA_HEADS, MLA_NOPE + MLA_ROPE)
    return jnp.pad(w3, ((0, 0), (0, 0), (0, MLA_QK_PAD - MLA_NOPE - MLA_ROPE))).reshape(r, MLA_HEADS * MLA_QK_PAD)


def _unpad_w_q(w):
    r = w.shape[0]
    return w.reshape(r, MLA_HEADS, MLA_QK_PAD)[:, :, :MLA_NOPE + MLA_ROPE].reshape(r, MLA_HEADS * (MLA_NOPE + MLA_ROPE))


def _pad_w_in(w):
    r = w.shape[0]
    o_dt = D_SSM + SSD_CONV_DIM
    o_qa = o_dt + SSD_HEADS
    o_kr = o_qa + MLA_Q_RANK + MLA_KV_RANK
    zeros = lambda n: jnp.zeros((r, n), w.dtype)
    return jnp.concatenate([w[:, :o_dt], w[:, o_qa:o_kr], w[:, o_kr:], zeros(LANES - MLA_ROPE),
                            w[:, o_dt:o_qa], zeros(LANES - SSD_HEADS)], axis=1)


def _unpad_w_in(w):
    return jnp.concatenate([w[:, :OFF_QA], w[:, OFF_DT:OFF_DT + SSD_HEADS], w[:, OFF_QA:OFF_KR + MLA_ROPE]], axis=1)


WEIGHTS = ['mix_norm_w', 'w_in', 'conv_w', 'conv_b', 'dt_bias', 'a_log', 'd_skip', 'ssd_norm_w', 'q_a_norm_w', 'w_q_b',
           'kv_a_norm_w', 'w_kv_b', 'w_out', 'ffn_norm_w', 'w_ffn_up', 'ffn_conv_w', 'ffn_conv_b', 'w_ffn_down',
           'ple_norm_w', 'w_ple_gate', 'b_ple_gate', 'w_ple_proj', 'ple_post_norm_w', 'final_norm_w']
BIG = ['w_in', 'w_q_b', 'w_kv_b', 'w_out', 'w_ffn_up', 'w_ffn_down', 'w_ple_gate', 'w_ple_proj']
COL_SHARDED = ('w_in', 'w_q_b', 'w_kv_b', 'w_ffn_up', 'w_ple_proj')
CONV = ['conv_w', 'ffn_conv_w']
REPL = [n for n in WEIGHTS if n not in BIG and n not in CONV]
FFN_INV = tuple(int(i) for i in np.argsort(FFN_PERM))


def _cat_cols(g):
    return g.transpose(1, 0, 2).reshape(g.shape[1], N_DEV * g.shape[2])


def _split_cols(w):
    return w.reshape(w.shape[0], N_DEV, w.shape[1] // N_DEV).transpose(1, 0, 2)


def _interleave(v):
    r = v.shape[0]
    return v.reshape(r, N_DEV, FFN_TC)[:, jnp.array(FFN_PERM)].reshape(r, N_DEV * FFN_TC)


def _deinterleave(v):
    r = v.shape[0]
    return v.reshape(r, N_DEV, FFN_TC)[:, jnp.array(FFN_INV)].reshape(r, N_DEV * FFN_TC)


def _assemble_weights(g):
    layout = {
        'w_in': lambda v: _pad_w_in(_cat_cols(v)),
        'w_q_b': lambda v: _pad_w_q(_cat_cols(v)),
        'w_kv_b': _cat_cols,
        'w_out': lambda v: v.reshape(D_MODEL, D_MODEL),
        'w_ffn_up': lambda v: v,
        'w_ffn_down': lambda v: v.reshape(D_FF, D_MODEL),
        'w_ple_gate': lambda v: v.reshape(D_MODEL, D_MODEL),
        'w_ple_proj': _cat_cols,
        'conv_w': _cat_cols,
        'ffn_conv_w': lambda v: _interleave(_cat_cols(v)),
    }
    return {n: layout[n](v) for n, v in g.items()}


WEIGHT_GROUPS = {'a': ['w_in', 'w_q_b', 'w_kv_b', 'conv_w'], 'b': ['w_out'],
                 'c': ['w_ffn_up', 'ffn_conv_w', 'w_ffn_down', 'w_ple_gate', 'w_ple_proj']}
GRAD_GROUPS = {'p': ['w_ple_proj', 'w_ple_gate', 'w_ffn_down'], 'r': ['w_ffn_up'], 's': ['w_out'],
               't': ['w_q_b', 'w_kv_b', 'w_in']}


def _ffn_perm(j):
    return (j % 2) * (N_DEV // 2) + j // 2


def _local_step(x, p, tabs, get_w, s, target, emit, relay):
    t = x.shape[0]
    s = dict(s)
    half = D_MODEL // 2
    up_cols = 2 * D_FF
    ffn_conv_b = _interleave(s['ffn_conv_b'])
    w = dict(get_w('a', None))
    h = _rmsnorm_fwd(x, s['mix_norm_w'], width=D_MODEL, name="mix_norm")
    proj = _matmul(h, w['w_in'], name="in_proj")
    y_ssd, ssd_saved = _ssd_forward(proj, w['conv_w'], s['conv_b'], s['dt_bias'], s['a_log'], s['d_skip'],
                                    s['ssd_norm_w'])
    o, mla_saved = _mla_forward(proj, tabs, s['q_a_norm_w'], w['w_q_b'], s['kv_a_norm_w'], w['w_kv_b'])
    tk_o, tn_o = _tile(half, MM_TK), _tile(D_MODEL, MM_TILE)
    w.update(get_w('b', o))
    x1 = _matmul(y_ssd, w['w_out'], add=x, mnk=(t, D_MODEL, half), name="out_proj_ssd")
    x1 = _matmul(o, w['w_out'], add=x1, mnk=(t, D_MODEL, half), name="out_proj_mla",
                 b_spec=pl.BlockSpec((tk_o, tn_o), lambda i, j, kk: (kk + half // tk_o, j)))
    hf = _rmsnorm_fwd(x1, s['ffn_norm_w'], width=D_MODEL, name="ffn_norm")
    w.update(get_w('c', hf))
    tk_u = _tile(D_MODEL, MM_TK)
    u = _matmul(hf, w['w_ffn_up'], mnk=(t, up_cols, D_MODEL), tn=FFN_TC, name="ffn_up",
                b_spec=pl.BlockSpec((1, tk_u, FFN_TC), lambda i, j, kk: (_ffn_perm(j), kk, 0)))
    act = _conv_act_fwd(u, w['ffn_conv_w'], ffn_conv_b, kw=FFN_CONV, glu=True, tc=2 * FFN_TC, coff=0, ncols=up_cols,
                        out_dtype=BF16, name="ffn_act")
    x2 = _matmul(act, w['w_ffn_down'], add=x1, name="ffn_down")
    hp = _rmsnorm_fwd(x2, s['ple_norm_w'], width=D_MODEL, name="ple_norm")
    gl = _matmul(hp, w['w_ple_gate'], bias=s['b_ple_gate'], name="ple_gate")
    pe = _matmul(p, w['w_ple_proj'], name="ple_proj")
    x3 = _ple_fwd(x2, gl, pe, s['ple_post_norm_w'], name="ple_mix")
    loss, dx3, d_final = _loss_head(x3, s['final_norm_w'], target, name="loss_head")
    dgl, d_bgate, dpe, d_post = _ple_bwd(dx3, gl, pe, s['ple_post_norm_w'], name="ple_mix_bwd")
    d_wproj = _matmul(p, dpe, ta=True, out_dtype=BF16, name="d_w_ple_proj")
    d_wgate = _matmul(hp, dgl, ta=True, out_dtype=BF16, name="d_w_ple_gate")
    dhp = _matmul(dgl, w['w_ple_gate'], tb=True, name="d_ple_normed")
    dx2, d_plenorm = _rmsnorm_bwd(x2, s['ple_norm_w'], dhp, dx3, width=D_MODEL, name="ple_norm_bwd")
    dact = _matmul(dx2, w['w_ffn_down'], tb=True, name="d_ffn_act")
    d_wdown = _matmul(act, dx2, ta=True, out_dtype=BF16, name="d_w_ffn_down")
    zz = emit('p', {'w_ple_proj': _split_cols(d_wproj), 'w_ple_gate': d_wgate.reshape(N_DEV, D_MODEL // N_DEV, D_MODEL),
                    'w_ffn_down': d_wdown.reshape(N_DEV, D_FF // N_DEV, D_MODEL)})
    du, d_fconv_w, d_fconv_b = _conv_act_bwd(u, w['ffn_conv_w'], ffn_conv_b + zz, dact, kw=FFN_CONV, glu=True,
                                             tc=2 * FFN_TC, coff=0, ncols=up_cols, name="ffn_act_bwd")
    zz = zz + relay('p', du)
    tm_u = _tile(D_MODEL, MM_TILE)
    d_wup = _matmul(hf, du, ta=True, out_dtype=BF16, mnk=(D_MODEL, up_cols, t), tn=FFN_TC, name="d_w_ffn_up",
                    o_spec=pl.BlockSpec((1, tm_u, FFN_TC), lambda i, j, kk: (_ffn_perm(j), i, 0)),
                    o_shape=(N_DEV, D_MODEL, FFN_TC))
    zz = zz + emit('r', {'w_ffn_up': d_wup})
    dhf = _matmul(du, w['w_ffn_up'], tb=True, mnk=(t, D_MODEL, up_cols), tk=FFN_TC, name="d_ffn_normed",
                  b_spec=pl.BlockSpec((1, tn_o, FFN_TC), lambda i, j, kk: (_ffn_perm(kk), j, 0)))
    zz = zz + relay('r', dhf)
    dx1, d_ffnnorm = _rmsnorm_bwd(x1, s['ffn_norm_w'] + zz, dhf, dx2, width=D_MODEL, name="ffn_norm_bwd")
    dcat = _matmul(dx1, w['w_out'], tb=True, name="d_mixed")
    d_wout = jnp.concatenate([_matmul(y_ssd, dx1, ta=True, out_dtype=BF16, name="d_w_out_ssd"),
                              _matmul(o, dx1, ta=True, out_dtype=BF16, name="d_w_out_mla")], axis=0)
    zz = zz + emit('s', {'w_out': d_wout.reshape(N_DEV, D_MODEL // N_DEV, D_MODEL)})
    ssd_saved = ssd_saved[:3] + (ssd_saved[3] + zz,) + ssd_saved[4:]
    dz, dxbc, d_raw, d_ssdnorm, d_conv_w, d_conv_b, d_dtb, d_alog, d_dskip = _ssd_backward(ssd_saved, dcat)
    zz = zz + relay('s', dz)
    mla_saved = mla_saved[:-1] + (mla_saved[-1] + zz,)
    dq_a, dckv, dkr, d_wq, d_wkv, d_qnorm, d_kvnorm = _mla_backward(mla_saved, dcat)
    dproj = jnp.concatenate([dz, dxbc, dq_a, dckv, dkr, d_raw], axis=1)
    d_win = _matmul(h, dproj, ta=True, out_dtype=BF16, name="d_w_in")
    dh = _matmul(dproj, w['w_in'], tb=True, name="d_in_normed")
    dx, d_mixnorm = _rmsnorm_bwd(x, s['mix_norm_w'], dh, dx1, width=D_MODEL, name="mix_norm_bwd")
    emit('t', {'w_in': _split_cols(_unpad_w_in(d_win)), 'w_q_b': _split_cols(_unpad_w_q(d_wq)),
               'w_kv_b': _split_cols(d_wkv)})
    relay('t', dx)
    conv = {'conv_w': d_conv_w, 'ffn_conv_w': _deinterleave(d_fconv_w)}
    vec = {
        'mix_norm_w': d_mixnorm, 'conv_b': d_conv_b, 'dt_bias': d_dtb, 'a_log': d_alog, 'd_skip': d_dskip,
        'ssd_norm_w': d_ssdnorm, 'q_a_norm_w': d_qnorm, 'kv_a_norm_w': d_kvnorm, 'ffn_norm_w': d_ffnnorm,
        'ffn_conv_b': _deinterleave(d_fconv_b), 'ple_norm_w': d_plenorm, 'b_ple_gate': d_bgate,
        'ple_post_norm_w': d_post, 'final_norm_w': d_final,
    }
    return loss, dx, conv, vec


MESH = pl.DeviceIdType.MESH
FLIPS = ((0, 0, 1), (1, 0, 0), (0, 1, 0), (1, 1, 0), (1, 0, 1), (0, 1, 1), (1, 1, 1))


def _exchange(items, *, gather, name):
    n = len(items)

    def body(*refs):
        ins, outs = refs[:n], refs[n:2 * n]
        send_sems, recv_sems, local_sems = refs[2 * n:]
        x, y, c = lax.axis_index("x"), lax.axis_index("y"), lax.axis_index("c")
        me = 4 * x + 2 * y + c
        peers = [(jnp.where(fx, 1 - x, x), jnp.where(fy, 1 - y, y), jnp.where(fc, 1 - c, c)) for fx, fy, fc in FLIPS]
        slot = [4 * px + 2 * py + pc for px, py, pc in peers]
        local, sends = [], []
        for wi in range(n):
            cp = pltpu.make_async_copy(ins[wi] if gather else ins[wi].at[me], outs[wi].at[me], local_sems.at[wi])
            cp.start()
            local.append(cp)
            for k, peer in enumerate(peers):
                cp = pltpu.make_async_remote_copy(
                    src_ref=ins[wi] if gather else ins[wi].at[slot[k]], dst_ref=outs[wi].at[me],
                    send_sem=send_sems.at[k, wi], recv_sem=recv_sems.at[k, wi], device_id=peer, device_id_type=MESH)
                cp.start()
                sends.append(cp)
        for wi in range(n):
            for k, peer in enumerate(peers):
                pltpu.make_async_remote_copy(
                    src_ref=outs[wi].at[slot[k]], dst_ref=outs[wi].at[slot[k]], send_sem=send_sems.at[k, wi],
                    recv_sem=recv_sems.at[k, wi], device_id=peer, device_id_type=MESH).wait_recv()
        for cp in sends:
            cp.wait_send()
        for cp in local:
            cp.wait()

    hbm = pl.BlockSpec(memory_space=pltpu.HBM)
    out_shape = [jax.ShapeDtypeStruct(((N_DEV,) + v.shape) if gather else v.shape, v.dtype) for v in items]
    return pl.pallas_call(
        body, name=name, in_specs=[hbm] * n, out_specs=[hbm] * n, out_shape=out_shape,
        scratch_shapes=[pltpu.SemaphoreType.DMA((len(FLIPS), n)), pltpu.SemaphoreType.DMA((len(FLIPS), n)),
                        pltpu.SemaphoreType.DMA((n,))],
    )(*items)


HBM_SPEC = pl.BlockSpec(memory_space=pltpu.HBM)
SEM_SPEC = pl.BlockSpec(memory_space=pltpu.SEMAPHORE)
EFFECT = pltpu.SideEffectType.DATAFLOW_SIDE_EFFECTING


def _peers():
    x, y, c = lax.axis_index("x"), lax.axis_index("y"), lax.axis_index("c")
    peers = [(jnp.where(fx, 1 - x, x), jnp.where(fy, 1 - y, y), jnp.where(fc, 1 - c, c)) for fx, fy, fc in FLIPS]
    return 4 * x + 2 * y + c, peers, [4 * px + 2 * py + pc for px, py, pc in peers]


def _split_start(bufs, ncopies, plan, *, name):
    nb = len(bufs)

    def body(*refs):
        send_sems, recv_sems, token = refs[nb], refs[nb + 1], refs[2 * nb + 2]
        for i, (src, dst, peer, _) in enumerate(plan(refs[:nb])):
            pltpu.make_async_remote_copy(src_ref=src, dst_ref=dst, send_sem=send_sems.at[i], recv_sem=recv_sems.at[i],
                                         device_id=peer, device_id_type=MESH).start()
        token[...] = jnp.zeros_like(token)

    res = pl.pallas_call(
        body, name=name, in_specs=[HBM_SPEC] * nb,
        out_specs=[SEM_SPEC, SEM_SPEC] + [HBM_SPEC] * nb + [pl.BlockSpec(memory_space=pltpu.VMEM)],
        out_shape=[pltpu.SemaphoreType.DMA((ncopies,)), pltpu.SemaphoreType.DMA((ncopies,))]
        + [pltpu.HBM(v.shape, v.dtype) for v in bufs] + [jax.ShapeDtypeStruct((HALO, LANES), F32)],
        input_output_aliases={i: 2 + i for i in range(nb)},
        compiler_params=pltpu.CompilerParams(has_side_effects=EFFECT),
    )(*[pltpu.with_memory_space_constraint(v, pltpu.HBM) for v in bufs])
    return (res[0], res[1], list(res[2:2 + nb])), res[2 + nb]


def _split_wait(started, after, plan, local_plan, *, name):
    send_sems, recv_sems, bufs = started
    nb = len(bufs)
    nlocal = len(local_plan(bufs))

    def body(*refs):
        send_sems, recv_sems = refs[nb], refs[nb + 1]
        local_sems = refs[2 * nb + 3]
        local = []
        for j, (src, dst) in enumerate(local_plan(refs[:nb])):
            cp = pltpu.make_async_copy(src, dst, local_sems.at[j])
            cp.start()
            local.append(cp)
        for i, (src, _, peer, incoming) in enumerate(plan(refs[:nb])):
            cp = pltpu.make_async_remote_copy(src_ref=src, dst_ref=incoming, send_sem=send_sems.at[i],
                                              recv_sem=recv_sems.at[i], device_id=peer, device_id_type=MESH)
            cp.wait_send()
            cp.wait_recv()
        for cp in local:
            cp.wait()

    res = pl.pallas_call(
        body, name=name, in_specs=[HBM_SPEC] * nb + [SEM_SPEC, SEM_SPEC, pl.BlockSpec(memory_space=pl.ANY)],
        out_specs=[HBM_SPEC] * nb, out_shape=[pltpu.HBM(v.shape, v.dtype) for v in bufs],
        input_output_aliases={i: i for i in range(nb)},
        scratch_shapes=[pltpu.SemaphoreType.DMA((max(nlocal, 1),))],
        compiler_params=pltpu.CompilerParams(has_side_effects=EFFECT),
    )(*bufs, send_sems, recv_sems, after)
    return list(res)


def _place():
    x, y, c = lax.axis_index("x"), lax.axis_index("y"), lax.axis_index("c")
    others = [((1 - x, y, c), 2 * (1 - x) + y), ((x, 1 - y, c), 2 * x + 1 - y), ((1 - x, 1 - y, c), 2 * (1 - x) + 1 - y)]
    return 4 * x + 2 * y + c, 2 * x + y, c, (x, y, 1 - c), others


def _gather1_plan(n):
    def plan(refs):
        me, _, _, sibling, others = _place()
        out = []
        for wi in range(n):
            item, land = refs[wi], refs[n + wi]
            out.append((item, land.at[me], sibling, land.at[me + 1 - 2 * lax.axis_index("c")]))
            for peer, chip in others:
                out.append((item, land.at[me], peer, land.at[2 * chip + lax.axis_index("c")]))
        return out

    return plan


def _gather1_local(n):
    def plan(refs):
        me = _place()[0]
        return [(refs[wi], refs[n + wi].at[me]) for wi in range(n)]

    return plan


def _gather2_plan(n):
    def plan(refs):
        _, _, c, sibling, others = _place()
        out = []
        for wi in range(n):
            land = refs[wi]
            for _, chip in others:
                out.append((land.at[2 * chip + c], land.at[2 * chip + c], sibling, land.at[2 * chip + 1 - c]))
        return out

    return plan


def _gather_start(items, *, name):
    lands = [lax.empty((N_DEV,) + v.shape, v.dtype) for v in items]
    return _split_start(items + lands, 4 * len(items), _gather1_plan(len(items)), name=name)


def _gather_forward(started, after, *, name):
    n = len(started[2]) // 2
    bufs = _split_wait(started, after, _gather1_plan(n), _gather1_local(n), name=name + "_wait")
    return _split_start(bufs[n:], 3 * n, _gather2_plan(n), name=name + "_start")


def _gather_finish(started, after, *, name):
    n = len(started[2])
    return _split_wait(started, after, _gather2_plan(n), lambda refs: [], name=name)


def _handshake(peers):
    barrier = pltpu.get_barrier_semaphore()
    for peer in peers:
        pl.semaphore_signal(barrier, inc=1, device_id=peer, device_id_type=MESH)
    pl.semaphore_wait(barrier, len(peers))


def _remote(src, dst, send_sem, recv_sem, peer):
    return pltpu.make_async_remote_copy(src_ref=src, dst_ref=dst, send_sem=send_sem, recv_sem=recv_sem, device_id=peer,
                                        device_id_type=MESH)


def _sequencer_gather(items, *, collective_id, name):
    n = len(items)
    srcs = [jax.new_ref(v, memory_space=pltpu.MemorySpace.HBM) for v in items]
    lands = [jax.empty_ref(jax.ShapeDtypeStruct((N_DEV,) + v.shape, v.dtype), memory_space=pltpu.MemorySpace.HBM)
             for v in items]
    dma = pltpu.SemaphoreType.DMA

    @pl.kernel(mesh=plsc.ScalarSubcoreMesh(axis_name="sequencer", num_cores=1), name=name,
               scratch_types=(dma((4 * n,)), dma((4 * n,)), dma((3 * n,)), dma((3 * n,)), dma((n,))),
               compiler_params=pltpu.CompilerParams(collective_id=collective_id))
    def launch(send1, recv1, send2, recv2, local_sems):
        _, _, _, sibling, others = _place()
        _handshake([sibling] + [peer for peer, _ in others])
        hop1 = _gather1_plan(n)(srcs + lands)
        hop2 = _gather2_plan(n)(lands)
        local = [pltpu.make_async_copy(src, dst, local_sems.at[j])
                 for j, (src, dst) in enumerate(_gather1_local(n)(srcs + lands))]
        for cp in local:
            cp.start()
        for i, (src, dst, peer, _) in enumerate(hop1):
            _remote(src, dst, send1.at[i], recv1.at[i], peer).start()
        for wi in range(n):
            for j in range(3):
                i1, i2 = 4 * wi + 1 + j, 3 * wi + j
                src, _, peer, incoming = hop1[i1]
                _remote(src, incoming, send1.at[i1], recv1.at[i1], peer).wait_recv()
                src, dst, peer, _ = hop2[i2]
                _remote(src, dst, send2.at[i2], recv2.at[i2], peer).start()
        for wi in range(n):
            src, _, peer, incoming = hop1[4 * wi]
            _remote(src, incoming, send1.at[4 * wi], recv1.at[4 * wi], peer).wait_recv()
        for i, (src, _, peer, incoming) in enumerate(hop2):
            cp = _remote(src, incoming, send2.at[i], recv2.at[i], peer)
            cp.wait_send()
            cp.wait_recv()
        for i, (src, dst, peer, _) in enumerate(hop1):
            _remote(src, dst, send1.at[i], recv1.at[i], peer).wait_send()
        for cp in local:
            cp.wait()

    launch()
    return [land[...] for land in lands]


def _sequencer_exchange(sources, land_shapes, ncopies, plan, local_plan, peers, *, collective_id, name):
    srcs = [jax.new_ref(v, memory_space=pltpu.MemorySpace.HBM) for v in sources]
    lands = [jax.empty_ref(s, memory_space=pltpu.MemorySpace.HBM) for s in land_shapes]
    nlocal = len(local_plan(srcs + lands))
    dma = pltpu.SemaphoreType.DMA

    @pl.kernel(mesh=plsc.ScalarSubcoreMesh(axis_name="sequencer", num_cores=1), name=name,
               scratch_types=(dma((ncopies,)), dma((ncopies,)), dma((max(nlocal, 1),))),
               compiler_params=pltpu.CompilerParams(collective_id=collective_id))
    def launch(send_sems, recv_sems, local_sems):
        _handshake(peers(_place()))
        copies = plan(srcs + lands)
        local = [pltpu.make_async_copy(src, dst, local_sems.at[j])
                 for j, (src, dst) in enumerate(local_plan(srcs + lands))]
        for cp in local:
            cp.start()
        for i, (src, dst, peer, _) in enumerate(copies):
            _remote(src, dst, send_sems.at[i], recv_sems.at[i], peer).start()
        for i, (src, _, peer, incoming) in enumerate(copies):
            cp = _remote(src, incoming, send_sems.at[i], recv_sems.at[i], peer)
            cp.wait_send()
            cp.wait_recv()
        for cp in local:
            cp.wait()

    launch()
    return [land[...] for land in lands]


def _sequencer_scatter_hop1(parts, *, collective_id, name):
    n = len(parts)
    shapes = [jax.ShapeDtypeStruct((N_CHIP,) + v.shape[1:], v.dtype) for v in parts]
    return _sequencer_exchange(parts, shapes, N_CHIP * n, _scatter1_plan(n), lambda refs: [], lambda place: [place[3]],
                               collective_id=collective_id, name=name)


def _sequencer_scatter_hop2(sums, *, collective_id, name):
    n = len(sums)
    shapes = [jax.ShapeDtypeStruct(v.shape, v.dtype) for v in sums]
    return _sequencer_exchange(sums, shapes, 3 * n, _scatter2_plan(n), _scatter2_local(n),
                               lambda place: [peer for peer, _ in place[4]], collective_id=collective_id, name=name)


N_CHIP = N_DEV // 2


def _scatter1_plan(n):
    def plan(refs):
        _, _, c, sibling, _ = _place()
        out = []
        for wi in range(n):
            parts, half = refs[wi], refs[n + wi]
            for chip in range(N_CHIP):
                out.append((parts.at[2 * chip + 1 - c], half.at[chip], sibling, half.at[chip]))
        return out

    return plan


def _scatter2_plan(n):
    def plan(refs):
        _, my_chip, _, _, others = _place()
        out = []
        for wi in range(n):
            sums, recv = refs[wi], refs[n + wi]
            for peer, chip in others:
                out.append((sums.at[chip], recv.at[my_chip], peer, recv.at[chip]))
        return out

    return plan


def _scatter2_local(n):
    def plan(refs):
        my_chip = _place()[1]
        return [(refs[wi].at[my_chip], refs[n + wi].at[my_chip]) for wi in range(n)]

    return plan


def _pair_add(parts, half, core, *, name):
    _, r, c = parts.shape
    tr = max(d for d in range(HALO, 257, HALO) if r % d == 0) if r > 256 else r
    parts4 = parts.reshape(N_CHIP, 2, r, c)

    def body(core_ref, p_ref, h_ref, o_ref):
        o_ref[...] = (p_ref[:, 0].astype(F32) + h_ref[...].astype(F32)).astype(o_ref.dtype)

    return pl.pallas_call(
        body, name=name,
        grid_spec=pltpu.PrefetchScalarGridSpec(
            num_scalar_prefetch=1, grid=(r // tr,),
            in_specs=[pl.BlockSpec((N_CHIP, 1, tr, c), lambda i, core_ref: (0, core_ref[0], i, 0)),
                      pl.BlockSpec((N_CHIP, tr, c), lambda i, core_ref: (0, i, 0))],
            out_specs=pl.BlockSpec((N_CHIP, tr, c), lambda i, core_ref: (0, i, 0))),
        out_shape=jax.ShapeDtypeStruct((N_CHIP, r, c), parts.dtype), compiler_params=_cp("parallel"),
    )(core, parts4, half)


def _scatter_start(parts, *, name):
    halves = [lax.empty((N_CHIP,) + v.shape[1:], v.dtype) for v in parts]
    return _split_start(parts + halves, N_CHIP * len(parts), _scatter1_plan(len(parts)), name=name)


def _scatter_forward(started, after, core, *, name):
    n = len(started[2]) // 2
    bufs = _split_wait(started, after, _scatter1_plan(n), lambda refs: [], name=name + "_wait")
    sums = [_pair_add(bufs[wi], bufs[n + wi], core, name=name + "_add%d" % wi) for wi in range(n)]
    recvs = [lax.empty(v.shape, v.dtype) for v in sums]
    return _split_start(sums + recvs, 3 * n, _scatter2_plan(n), name=name + "_start")


def _scatter_finish(started, after, *, name):
    n = len(started[2]) // 2
    return _split_wait(started, after, _scatter2_plan(n), _scatter2_local(n), name=name)[n:]


def _adamw(parts, w, m, v, *, name):
    r, c = w.shape
    nparts = parts.shape[0]
    tr = max(d for d in range(HALO, 129, HALO) if r % d == 0) if r > 128 else r

    def body(p_ref, w_ref, m_ref, v_ref, g_ref, d_ref, mo_ref, vo_ref):
        g = p_ref[0].astype(F32)
        for k in range(1, nparts):
            g = g + p_ref[k].astype(F32)
        mn = ADAM_B1 * m_ref[...] + (1.0 - ADAM_B1) * g
        vn = ADAM_B2 * v_ref[...] + (1.0 - ADAM_B2) * (g * g)
        m_hat = mn / (1.0 - ADAM_B1 ** ADAM_STEP)
        v_hat = vn / (1.0 - ADAM_B2 ** ADAM_STEP)
        g_ref[...] = g
        d_ref[...] = -ADAM_LR * (m_hat / (jnp.sqrt(v_hat) + ADAM_EPS) + ADAM_WD * w_ref[...])
        mo_ref[...] = mn
        vo_ref[...] = vn

    blk = pl.BlockSpec((tr, c), lambda i: (i, 0))
    return pl.pallas_call(
        body, name=name, grid=(r // tr,), in_specs=[pl.BlockSpec((nparts, tr, c), lambda i: (0, i, 0)), blk, blk, blk],
        out_specs=[blk] * 4, out_shape=[jax.ShapeDtypeStruct((r, c), F32)] * 4, compiler_params=_cp("parallel"),
    )(parts, w, m, v)


def _pack_rows(vs, rows):
    lead = vs[0].shape[:-1] if vs[0].ndim > 1 else ()
    flat = jnp.concatenate(vs, axis=-1)
    pad = rows * LANES - flat.shape[-1]
    flat = jnp.pad(flat, [(0, 0)] * len(lead) + [(0, pad)])
    return flat.reshape(lead + (rows, LANES))


def kernel(x, p, positions, mix_norm_w, w_in, conv_w, conv_b, dt_bias, a_log, d_skip, ssd_norm_w, q_a_norm_w, w_q_b, kv_a_norm_w, w_kv_b, w_out, ffn_norm_w, w_ffn_up, ffn_conv_w, ffn_conv_b, w_ffn_down, ple_norm_w, w_ple_gate, b_ple_gate, w_ple_proj, ple_post_norm_w, final_norm_w, loss_target, m_mix_norm_w, m_w_in, m_conv_w, m_conv_b, m_dt_bias, m_a_log, m_d_skip, m_ssd_norm_w, m_q_a_norm_w, m_w_q_b, m_kv_a_norm_w, m_w_kv_b, m_w_out, m_ffn_norm_w, m_w_ffn_up, m_ffn_conv_w, m_ffn_conv_b, m_w_ffn_down, m_ple_norm_w, m_w_ple_gate, m_b_ple_gate, m_w_ple_proj, m_ple_post_norm_w, m_final_norm_w, v_mix_norm_w, v_w_in, v_conv_w, v_conv_b, v_dt_bias, v_a_log, v_d_skip, v_ssd_norm_w, v_q_a_norm_w, v_w_q_b, v_kv_a_norm_w, v_w_kv_b, v_w_out, v_ffn_norm_w, v_w_ffn_up, v_ffn_conv_w, v_ffn_conv_b, v_w_ffn_down, v_ple_norm_w, v_w_ple_gate, v_b_ple_gate, v_w_ple_proj, v_ple_post_norm_w, v_final_norm_w):
    given = dict(locals())
    shapes = {n: given[n].shape for n in WEIGHTS}
    w2 = {n: given[n].reshape(given[n].shape[-2:] if n in BIG or n in CONV else (1, -1)) for n in WEIGHTS}
    m2 = {n: given['m_' + n].reshape(w2[n].shape) for n in WEIGHTS}
    v2 = {n: given['v_' + n].reshape(w2[n].shape) for n in WEIGHTS}
    me = 4 * lax.axis_index("x") + 2 * lax.axis_index("y") + lax.axis_index("c")

    core = lax.axis_index("c").astype(jnp.int32).reshape(1)

    def shards(grp, zero):
        return [(w2[n] + zero).astype(BF16) if n in BIG else w2[n] + zero for n in WEIGHT_GROUPS[grp]]

    first, token = _gather_start(shards('a', 0.0), name="gather_a_hop1")
    first, token = _gather_forward(first, token, name="gather_a_hop2")
    zero = token[0, 0]
    later = _sequencer_gather(shards('b', zero) + shards('c', zero), collective_id=1, name="gather_later")
    later = dict(zip(WEIGHT_GROUPS['b'] + WEIGHT_GROUPS['c'], later))

    def get_w(grp, after):
        if grp == 'a':
            lands = dict(zip(WEIGHT_GROUPS[grp], _gather_finish(first, token, name="gather_a_done")))
        else:
            lands = {n: later[n] for n in WEIGHT_GROUPS[grp]}
        return _assemble_weights(lands)

    scatters = {}

    hop_ids = {grp: 2 + 2 * i for i, grp in enumerate(GRAD_GROUPS)}

    def emit(grp, grads):
        parts = [grads[n] for n in GRAD_GROUPS[grp]]
        scatters[grp] = (parts, _sequencer_scatter_hop1(parts, collective_id=hop_ids[grp],
                                                        name="scatter_" + grp + "_hop1"))
        return 0.0

    def relay(grp, after):
        parts, halves = scatters[grp]
        sums = [_pair_add(mine, theirs, core, name="scatter_%s_add%d" % (grp, i))
                for i, (mine, theirs) in enumerate(zip(parts, halves))]
        scatters[grp] = _sequencer_scatter_hop2(sums, collective_id=hop_ids[grp] + 1, name="scatter_" + grp + "_hop2")
        return 0.0

    vecs = {n: w2[n] for n in REPL}
    vecs['mix_norm_w'] = vecs['mix_norm_w'] + zero
    loss, dx, g_conv, g_vec = _local_step(x[0], p[0, 0], _rope_tables(positions), get_w, vecs, loss_target[0], emit,
                                          relay)
    n_small = sum(g_vec[n].shape[1] for n in REPL) + sum(g_conv[n].size for n in CONV) + 1
    rows_small = -(-n_small // (LANES * HALO)) * HALO
    small = _pack_rows([g_vec[n] for n in REPL] + [g_conv[n].reshape(1, -1) for n in CONV] + [loss], rows_small)
    all_small = _exchange([small], gather=True, name="gather_small_grads")[0].reshape(N_DEV, rows_small * LANES)

    out_g, out_d, out_m, out_v = {}, {}, {}, {}
    for grp, names in GRAD_GROUPS.items():
        received = scatters[grp]
        for n, parts in zip(names, received):
            out_g[n], out_d[n], out_m[n], out_v[n] = _adamw(parts, w2[n], m2[n], v2[n], name="adamw_" + n)
    pieces, off = [], 0
    for n in REPL:
        k = g_vec[n].shape[1]
        pieces.append(all_small[:, off:off + k])
        off += k
    for n in CONV:
        kw, cols = g_conv[n].shape
        full = all_small[:, off:off + kw * cols].reshape(N_DEV, kw, cols)
        mine = lax.dynamic_slice_in_dim(full, me * (cols // N_DEV), cols // N_DEV, axis=2)
        pieces.append(mine.reshape(N_DEV, kw * (cols // N_DEV)))
        off += kw * cols
    pieces.append(all_small[:, off:off + 1])
    small_names = REPL + CONV
    n_mine = sum(q.shape[1] for q in pieces)
    rows_mine = -(-n_mine // (LANES * HALO)) * HALO
    zero = jnp.zeros((1, 1), F32)
    packed = [_pack_rows([src[n].reshape(1, -1) for n in small_names] + [zero], rows_mine).reshape(rows_mine, LANES)
              for src in (w2, m2, v2)]
    sg, sd, sm, sv = _adamw(_pack_rows(pieces, rows_mine), *packed, name="adamw_small")
    off = 0
    for n in small_names:
        k = w2[n].size
        for dst, src in ((out_g, sg), (out_d, sd), (out_m, sm), (out_v, sv)):
            dst[n] = src.reshape(-1)[off:off + k].reshape(w2[n].shape)
        off += k
    total_loss = sg.reshape(-1)[off]

    outs = [total_loss, dx[None]]
    for res in (out_g, out_d, out_m, out_v):
        outs += [res[n].reshape(shapes[n]) for n in WEIGHTS]
    return tuple(outs)
```

```python
import functools
import math

import numpy as np
import jax
import jax.numpy as jnp
from jax import lax
from jax.experimental import pallas as pl
from jax.experimental.pallas import tpu as pltpu
from jax.experimental.pallas import tpu_sc as plsc

F32 = jnp.float32
BF16 = jnp.bfloat16
HI = lax.Precision.HIGHEST

D_MODEL = 2048
CHUNK = 64
D_SSM = 1024
SSD_P = 64
SSD_HEADS = 16
SSD_GROUPS = 2
SSD_N = 128
SSD_CONV = 4
SSD_CONV_DIM = D_SSM + 2 * SSD_GROUPS * SSD_N
MLA_HEADS = 8
MLA_NOPE = 128
MLA_ROPE = 64
MLA_V = 128
MLA_Q_RANK = 512
MLA_KV_RANK = 256
MLA_QK_PAD = 256
ROPE_THETA = 10000.0
D_FF = 5632
FFN_CONV = 3
PLE_DIM = 256
NORM_EPS = 1e-6
ADAM_LR, ADAM_B1, ADAM_B2, ADAM_EPS, ADAM_WD, ADAM_STEP = 0.001, 0.9, 0.999, 1e-08, 0.01, 10
N_DEV = 8

OFF_Z, OFF_XBC, OFF_QA, OFF_CKV, OFF_KR, OFF_DT, D_IN_PAD = 0, 1024, 2560, 3072, 3328, 3456, 3584
D_IN = 3408
LANES = 128
HALO = 8
VMEM_LIMIT = 56 * 1024 * 1024
FFN_TC = D_FF * 2 // N_DEV
FFN_PERM = (0, 4, 1, 5, 2, 6, 3, 7)
NEG = -1e30


def _cp(*sem):
    return pltpu.CompilerParams(dimension_semantics=tuple(sem), vmem_limit_bytes=VMEM_LIMIT)


def _tile(n, want):
    if n <= want:
        return n
    best = max(d for d in range(LANES, want + 1, LANES) if n % d == 0)
    return best


def _sigmoid(x):
    return 1.0 / (1.0 + jnp.exp(-x))


def _silu(x):
    return x * _sigmoid(x)


def _dsilu(x):
    s = _sigmoid(x)
    return s * (1.0 + x * (1.0 - s))


MM_TILE = 1408
MM_TK = 2816


def _matmul(a, b, *, ta=False, tb=False, out_dtype=F32, add=None, bias=None, tm=MM_TILE, tn=MM_TILE, tk=MM_TK, name,
            mnk=None, a_spec=None, b_spec=None, o_spec=None, o_shape=None):
    if mnk is None:
        m, k = (a.shape[1], a.shape[0]) if ta else a.shape
        n = b.shape[0] if tb else b.shape[1]
        assert k == (b.shape[1] if tb else b.shape[0])
    else:
        m, n, k = mnk
    tm, tn, tk = _tile(m, tm), _tile(n, tn), _tile(k, tk)
    nk = k // tk
    dims = (((0 if ta else 1,), (1 if tb else 0,)), ((), ()))

    def body(*refs):
        a_ref, b_ref = refs[0], refs[1]
        pos = 2
        add_ref = bias_ref = None
        if add is not None:
            add_ref = refs[pos]
            pos += 1
        if bias is not None:
            bias_ref = refs[pos]
            pos += 1
        o_ref = refs[pos]
        kk = pl.program_id(2)
        av = a_ref[...]
        bv = b_ref[...]
        av = av.reshape(av.shape[-2:]).astype(BF16)
        bv = bv.reshape(bv.shape[-2:]).astype(BF16)
        prod = lax.dot_general(av, bv, dims, preferred_element_type=F32)

        def finish(r):
            if bias_ref is not None:
                r = r + bias_ref[...]
            if add_ref is not None:
                r = r + add_ref[...].astype(F32)
            o_ref[...] = r.astype(out_dtype).reshape(o_ref.shape)

        if nk == 1:
            finish(prod)
        else:
            acc_ref = refs[pos + 1]

            @pl.when(kk == 0)
            def _():
                acc_ref[...] = prod

            @pl.when(kk > 0)
            def _():
                acc_ref[...] += prod

            @pl.when(kk == nk - 1)
            def _():
                finish(acc_ref[...])

    if a_spec is None:
        a_spec = (pl.BlockSpec((tk, tm), lambda i, j, kk: (kk, i)) if ta
                  else pl.BlockSpec((tm, tk), lambda i, j, kk: (i, kk)))
    if b_spec is None:
        b_spec = (pl.BlockSpec((tn, tk), lambda i, j, kk: (j, kk)) if tb
                  else pl.BlockSpec((tk, tn), lambda i, j, kk: (kk, j)))
    if o_spec is None:
        o_spec = pl.BlockSpec((tm, tn), lambda i, j, kk: (i, j))
    if o_shape is None:
        o_shape = (m, n)
    in_specs = [a_spec, b_spec]
    args = [a, b]
    if add is not None:
        in_specs.append(pl.BlockSpec((tm, tn), lambda i, j, kk: (i, j)))
        args.append(add)
    if bias is not None:
        in_specs.append(pl.BlockSpec((1, tn), lambda i, j, kk: (0, j)))
        args.append(bias)
    return pl.pallas_call(
        body, name=name, grid=(m // tm, n // tn, nk), in_specs=in_specs, out_specs=o_spec,
        out_shape=jax.ShapeDtypeStruct(o_shape, out_dtype),
        scratch_shapes=[pltpu.VMEM((tm, tn), F32)] if nk > 1 else [],
        compiler_params=_cp("parallel", "parallel", "arbitrary"),
    )(*args)


def _rmsnorm_fwd(x, w, *, width, cblk=0, out_dtype=BF16, tr=256, name):
    t = x.shape[0]

    def body(x_ref, w_ref, o_ref):
        xv = x_ref[...].astype(F32)
        r = lax.rsqrt(jnp.mean(xv * xv, axis=-1, keepdims=True) + NORM_EPS)
        o_ref[...] = (xv * r * w_ref[...]).astype(out_dtype)

    return pl.pallas_call(
        body, name=name, grid=(t // tr,),
        in_specs=[pl.BlockSpec((tr, width), lambda i: (i, cblk)), pl.BlockSpec((1, width), lambda i: (0, 0))],
        out_specs=pl.BlockSpec((tr, width), lambda i: (i, 0)),
        out_shape=jax.ShapeDtypeStruct((t, width), out_dtype),
        compiler_params=_cp("parallel"),
    )(x, w)


def _rmsnorm_bwd(x, w, dy, add=None, *, width, cblk=0, out_dtype=F32, tr=256, name):
    t = x.shape[0]

    def body(*refs):
        if add is None:
            x_ref, w_ref, dy_ref, dx_ref, dw_ref = refs
            add_ref = None
        else:
            x_ref, w_ref, dy_ref, add_ref, dx_ref, dw_ref = refs
        xv = x_ref[...].astype(F32)
        dyv = dy_ref[...].astype(F32)
        r = lax.rsqrt(jnp.mean(xv * xv, axis=-1, keepdims=True) + NORM_EPS)
        xh = xv * r
        g = dyv * w_ref[...]
        dx = r * (g - xh * jnp.mean(g * xh, axis=-1, keepdims=True))
        if add_ref is not None:
            dx = dx + add_ref[...].astype(F32)
        dx_ref[...] = dx.astype(out_dtype)

        @pl.when(pl.program_id(0) == 0)
        def _():
            dw_ref[...] = jnp.zeros_like(dw_ref)

        dw_ref[...] += jnp.sum(dyv * xh, axis=0, keepdims=True)

    in_specs = [pl.BlockSpec((tr, width), lambda i: (i, cblk)), pl.BlockSpec((1, width), lambda i: (0, 0)),
                pl.BlockSpec((tr, width), lambda i: (i, 0))]
    args = [x, w, dy]
    if add is not None:
        in_specs.append(pl.BlockSpec((tr, width), lambda i: (i, 0)))
        args.append(add)
    return pl.pallas_call(
        body, name=name, grid=(t // tr,), in_specs=in_specs,
        out_specs=[pl.BlockSpec((tr, width), lambda i: (i, 0)), pl.BlockSpec((1, width), lambda i: (0, 0))],
        out_shape=[jax.ShapeDtypeStruct((t, width), out_dtype), jax.ShapeDtypeStruct((1, width), F32)],
        compiler_params=_cp("arbitrary"),
    )(*args)


def _shift_down(prev_halo, cur, j):
    if j == 0:
        return cur
    ext = jnp.concatenate([prev_halo, cur], axis=0)
    return pltpu.roll(ext, j, axis=0)[HALO:]


def _shift_up(cur, next_halo, j):
    if j == 0:
        return cur
    ext = jnp.concatenate([cur, next_halo], axis=0)
    return pltpu.roll(ext, ext.shape[0] - j, axis=0)[:cur.shape[0]]


def _conv_rows(prev, cur, w, b, kw):
    shifted = [cur]
    out = b + w[kw - 1:kw] * cur
    for j in range(1, kw):
        sh = _shift_down(prev, cur, j)
        shifted.append(sh)
        out = out + w[kw - 1 - j:kw - j] * sh
    return out, shifted


def _act_fwd(c, glu):
    if glu:
        half = c.shape[1] // 2
        return _silu(c[:, :half]) * c[:, half:]
    return _silu(c)


def _act_bwd(c, dout, glu):
    if glu:
        half = c.shape[1] // 2
        g, up = c[:, :half], c[:, half:]
        return jnp.concatenate([dout * up * _dsilu(g), dout * _silu(g)], axis=1)
    return dout * _dsilu(c)


def _conv_act_fwd(u, w, b, *, kw, glu, tc, coff, ncols, out_dtype, tr=256, name):
    t = u.shape[0]
    nb = ncols // tc
    oc = tc // 2 if glu else tc

    def body(u_ref, uh_ref, w_ref, b_ref, o_ref):
        prev = jnp.where(pl.program_id(0) == 0, 0.0, uh_ref[...])
        c, _ = _conv_rows(prev, u_ref[...], w_ref[...], b_ref[...], kw)
        o_ref[...] = _act_fwd(c, glu).astype(out_dtype)

    return pl.pallas_call(
        body, name=name, grid=(t // tr, nb),
        in_specs=[pl.BlockSpec((tr, tc), lambda i, j: (i, j + coff)),
                  pl.BlockSpec((HALO, tc), lambda i, j: (jnp.maximum(i * (tr // HALO) - 1, 0), j + coff)),
                  pl.BlockSpec((kw, tc), lambda i, j: (0, j)), pl.BlockSpec((1, tc), lambda i, j: (0, j))],
        out_specs=pl.BlockSpec((tr, oc), lambda i, j: (i, j)),
        out_shape=jax.ShapeDtypeStruct((t, nb * oc), out_dtype),
        compiler_params=_cp("parallel", "parallel"),
    )(u, u, w, b)


def _conv_act_bwd(u, w, b, dout, *, kw, glu, tc, coff, ncols, tr=256, name):
    t = u.shape[0]
    nb = ncols // tc
    nt = t // tr
    oc = tc // 2 if glu else tc

    def body(u_ref, up_ref, un_ref, d_ref, dn_ref, w_ref, b_ref, du_ref, dw_ref, db_ref):
        i = pl.program_id(1)
        cur, nxt, wv, bv = u_ref[...], un_ref[...], w_ref[...], b_ref[...]
        prev = jnp.where(i == 0, 0.0, up_ref[...])
        c_cur, shifted = _conv_rows(prev, cur, wv, bv, kw)
        c_nxt, _ = _conv_rows(cur[tr - HALO:], nxt, wv, bv, kw)
        d_cur = _act_bwd(c_cur, d_ref[...].astype(F32), glu)
        d_nxt = _act_bwd(c_nxt, jnp.where(i == nt - 1, 0.0, dn_ref[...].astype(F32)), glu)
        du = wv[kw - 1:kw] * d_cur
        for j in range(1, kw):
            du = du + wv[kw - 1 - j:kw - j] * _shift_up(d_cur, d_nxt, j)
        du_ref[...] = du.astype(BF16)

        @pl.when(i == 0)
        def _():
            dw_ref[...] = jnp.zeros_like(dw_ref)
            db_ref[...] = jnp.zeros_like(db_ref)

        db_ref[...] += jnp.sum(d_cur, axis=0, keepdims=True)
        dw_ref[...] += jnp.concatenate(
            [jnp.sum(d_cur * shifted[kw - 1 - k], axis=0, keepdims=True) for k in range(kw)], axis=0)

    nh = tr // HALO
    return pl.pallas_call(
        body, name=name, grid=(nb, nt),
        in_specs=[pl.BlockSpec((tr, tc), lambda j, i: (i, j + coff)),
                  pl.BlockSpec((HALO, tc), lambda j, i: (jnp.maximum(i * nh - 1, 0), j + coff)),
                  pl.BlockSpec((HALO, tc), lambda j, i: (jnp.minimum((i + 1) * nh, t // HALO - 1), j + coff)),
                  pl.BlockSpec((tr, oc), lambda j, i: (i, j)),
                  pl.BlockSpec((HALO, oc), lambda j, i: (jnp.minimum((i + 1) * nh, t // HALO - 1), j)),
                  pl.BlockSpec((kw, tc), lambda j, i: (0, j)), pl.BlockSpec((1, tc), lambda j, i: (0, j))],
        out_specs=[pl.BlockSpec((tr, tc), lambda j, i: (i, j)), pl.BlockSpec((kw, tc), lambda j, i: (0, j)),
                   pl.BlockSpec((1, tc), lambda j, i: (0, j))],
        out_shape=[jax.ShapeDtypeStruct((t, ncols), BF16), jax.ShapeDtypeStruct((kw, ncols), F32),
                   jax.ShapeDtypeStruct((1, ncols), F32)],
        compiler_params=_cp("parallel", "arbitrary"),
    )(u, u, u, dout, dout, w, b)


def _ple_fwd(x2, gl, pe, pw, *, tr=256, name):
    t, d = x2.shape

    def body(x_ref, gl_ref, pe_ref, pw_ref, o_ref):
        pv = pe_ref[...]
        r = lax.rsqrt(jnp.mean(pv * pv, axis=-1, keepdims=True) + NORM_EPS)
        o_ref[...] = x_ref[...] + _sigmoid(gl_ref[...]) * (pv * r * pw_ref[...])

    blk = pl.BlockSpec((tr, d), lambda i: (i, 0))
    return pl.pallas_call(
        body, name=name, grid=(t // tr,), in_specs=[blk, blk, blk, pl.BlockSpec((1, d), lambda i: (0, 0))],
        out_specs=blk, out_shape=jax.ShapeDtypeStruct((t, d), F32), compiler_params=_cp("parallel"),
    )(x2, gl, pe, pw)


def _ple_bwd(dx3, gl, pe, pw, *, tr=256, name):
    t, d = dx3.shape

    def body(dx_ref, gl_ref, pe_ref, pw_ref, dgl_ref, db_ref, dpe_ref, dpw_ref):
        dx, pv, pwv = dx_ref[...], pe_ref[...], pw_ref[...]
        gate = _sigmoid(gl_ref[...])
        r = lax.rsqrt(jnp.mean(pv * pv, axis=-1, keepdims=True) + NORM_EPS)
        ph = pv * r
        dgl = dx * (ph * pwv) * gate * (1.0 - gate)
        de = dx * gate
        g = de * pwv
        dgl_ref[...] = dgl.astype(BF16)
        dpe_ref[...] = (r * (g - ph * jnp.mean(g * ph, axis=-1, keepdims=True))).astype(BF16)

        @pl.when(pl.program_id(0) == 0)
        def _():
            db_ref[...] = jnp.zeros_like(db_ref)
            dpw_ref[...] = jnp.zeros_like(dpw_ref)

        db_ref[...] += jnp.sum(dgl, axis=0, keepdims=True)
        dpw_ref[...] += jnp.sum(de * ph, axis=0, keepdims=True)

    blk = pl.BlockSpec((tr, d), lambda i: (i, 0))
    row = pl.BlockSpec((1, d), lambda i: (0, 0))
    return pl.pallas_call(
        body, name=name, grid=(t // tr,), in_specs=[blk, blk, blk, row], out_specs=[blk, row, blk, row],
        out_shape=[jax.ShapeDtypeStruct((t, d), BF16), jax.ShapeDtypeStruct((1, d), F32),
                   jax.ShapeDtypeStruct((t, d), BF16), jax.ShapeDtypeStruct((1, d), F32)],
        compiler_params=_cp("arbitrary"),
    )(dx3, gl, pe, pw)


def _loss_head(x3, fw, target, *, tr=256, name):
    t, d = x3.shape

    def body(x_ref, w_ref, t_ref, l_ref, dx_ref, dw_ref):
        xv, wv = x_ref[...], w_ref[...]
        r = lax.rsqrt(jnp.mean(xv * xv, axis=-1, keepdims=True) + NORM_EPS)
        xh = xv * r
        err = xh * wv - t_ref[...]
        dy = err * (1.0 / d)
        g = dy * wv
        dx_ref[...] = r * (g - xh * jnp.mean(g * xh, axis=-1, keepdims=True))

        @pl.when(pl.program_id(0) == 0)
        def _():
            l_ref[...] = jnp.zeros_like(l_ref)
            dw_ref[...] = jnp.zeros_like(dw_ref)

        l_ref[...] += 0.5 * jnp.sum(jnp.mean(err * err, axis=-1, keepdims=True), axis=0, keepdims=True)
        dw_ref[...] += jnp.sum(dy * xh, axis=0, keepdims=True)

    blk = pl.BlockSpec((tr, d), lambda i: (i, 0))
    row = pl.BlockSpec((1, d), lambda i: (0, 0))
    return pl.pallas_call(
        body, name=name, grid=(t // tr,), in_specs=[blk, row, blk],
        out_specs=[pl.BlockSpec((1, 1), lambda i: (0, 0)), blk, row],
        out_shape=[jax.ShapeDtypeStruct((1, 1), F32), jax.ShapeDtypeStruct((t, d), F32),
                   jax.ShapeDtypeStruct((1, d), F32)],
        compiler_params=_cp("arbitrary"),
    )(x3, fw, target)


def _rope(blk, tab_ref):
    return blk * tab_ref[0] + pltpu.roll(blk, 96, axis=1) * tab_ref[1] + pltpu.roll(blk, 32, axis=1) * tab_ref[2]


def _unrope(g, tab_ref):
    return g * tab_ref[0] + pltpu.roll(g * tab_ref[1], 32, axis=1) + pltpu.roll(g * tab_ref[2], 96, axis=1)


def _mla_prep(q, kv, proj, tabs, *, tr=512, name):
    t = q.shape[0]

    def body(q_ref, kv_ref, kr_ref, tab_ref, qo_ref, ko_ref, vo_ref):
        qv, kvv = q_ref[...], kv_ref[...]
        qo_ref[0, :, :MLA_NOPE] = qv[:, :MLA_NOPE].astype(BF16)
        qo_ref[0, :, MLA_NOPE:] = _rope(qv[:, MLA_NOPE:], tab_ref).astype(BF16)
        ko_ref[0, :, :MLA_NOPE] = kvv[:, :MLA_NOPE].astype(BF16)
        ko_ref[0, :, MLA_NOPE:] = _rope(kr_ref[...], tab_ref).astype(BF16)
        vo_ref[0] = kvv[:, MLA_NOPE:].astype(BF16)

    return pl.pallas_call(
        body, name=name, grid=(t // tr, MLA_HEADS),
        in_specs=[pl.BlockSpec((tr, MLA_QK_PAD), lambda i, h: (i, h)),
                  pl.BlockSpec((tr, MLA_NOPE + MLA_V), lambda i, h: (i, h)),
                  pl.BlockSpec((tr, LANES), lambda i, h: (i, OFF_KR // LANES)),
                  pl.BlockSpec((3, tr, LANES), lambda i, h: (0, i, 0))],
        out_specs=[pl.BlockSpec((1, tr, MLA_QK_PAD), lambda i, h: (h, i, 0)),
                   pl.BlockSpec((1, tr, MLA_QK_PAD), lambda i, h: (h, i, 0)),
                   pl.BlockSpec((1, tr, MLA_V), lambda i, h: (h, i, 0))],
        out_shape=[jax.ShapeDtypeStruct((MLA_HEADS, t, MLA_QK_PAD), BF16),
                   jax.ShapeDtypeStruct((MLA_HEADS, t, MLA_QK_PAD), BF16),
                   jax.ShapeDtypeStruct((MLA_HEADS, t, MLA_V), BF16)],
        compiler_params=_cp("parallel", "parallel"),
    )(q, kv, proj, tabs)


def _mla_unprep(dq3, dk3, dv3, tabs, *, tr=256, name):
    t = dq3.shape[1]

    def body(dq_ref, dk_ref, dv_ref, tab_ref, qo_ref, kvo_ref, kro_ref):
        kr = jnp.zeros((tr, LANES), F32)
        for h in range(MLA_HEADS):
            c0 = h * MLA_QK_PAD
            qo_ref[:, c0:c0 + MLA_NOPE] = dq_ref[h, :, :MLA_NOPE].astype(BF16)
            qo_ref[:, c0 + MLA_NOPE:c0 + MLA_QK_PAD] = _unrope(dq_ref[h, :, MLA_NOPE:], tab_ref).astype(BF16)
            kvo_ref[:, c0:c0 + MLA_NOPE] = dk_ref[h, :, :MLA_NOPE].astype(BF16)
            kvo_ref[:, c0 + MLA_NOPE:c0 + MLA_QK_PAD] = dv_ref[h].astype(BF16)
            kr = kr + dk_ref[h, :, MLA_NOPE:]
        kro_ref[...] = _unrope(kr, tab_ref).astype(BF16)

    return pl.pallas_call(
        body, name=name, grid=(t // tr,),
        in_specs=[pl.BlockSpec((MLA_HEADS, tr, MLA_QK_PAD), lambda i: (0, i, 0)),
                  pl.BlockSpec((MLA_HEADS, tr, MLA_QK_PAD), lambda i: (0, i, 0)),
                  pl.BlockSpec((MLA_HEADS, tr, MLA_V), lambda i: (0, i, 0)),
                  pl.BlockSpec((3, tr, LANES), lambda i: (0, i, 0))],
        out_specs=[pl.BlockSpec((tr, MLA_HEADS * MLA_QK_PAD), lambda i: (i, 0)),
                   pl.BlockSpec((tr, MLA_HEADS * MLA_QK_PAD), lambda i: (i, 0)),
                   pl.BlockSpec((tr, LANES), lambda i: (i, 0))],
        out_shape=[jax.ShapeDtypeStruct((t, MLA_HEADS * MLA_QK_PAD), BF16),
                   jax.ShapeDtypeStruct((t, MLA_HEADS * MLA_QK_PAD), BF16),
                   jax.ShapeDtypeStruct((t, LANES), BF16)],
        compiler_params=_cp("parallel"),
    )(dq3, dk3, dv3, tabs)


ATT_BLK = 256
ATT_SCALE = 1.0 / math.sqrt(MLA_NOPE + MLA_ROPE)
_NT = (((1,), (1,)), ((), ()))
_TN = (((0,), (0,)), ((), ()))


def _att_scores(q, k, diagonal):
    s = lax.dot_general(q, k, _NT, preferred_element_type=F32) * ATT_SCALE
    if not diagonal:
        return s
    row = lax.broadcasted_iota(jnp.int32, s.shape, 0)
    col = lax.broadcasted_iota(jnp.int32, s.shape, 1)
    return jnp.where((col >> 6) <= (row >> 6), s, NEG)


def _att_rows(i):
    return pl.ds(pl.multiple_of(i * ATT_BLK, ATT_BLK), ATT_BLK)


def _attn_fwd(q3, k3, v3, *, name):
    t = q3.shape[1]
    nq = t // ATT_BLK

    def body(q_ref, k_ref, v_ref, o_ref, lse_ref):
        qi = pl.program_id(1)
        q = q_ref[0]

        def step(j, carry, diagonal=False):
            m, l, acc = carry
            s = _att_scores(q, k_ref[0, _att_rows(j), :], diagonal)
            m_new = jnp.maximum(m, jnp.max(s, axis=-1, keepdims=True))
            p = jnp.exp(s - m_new)
            alpha = jnp.exp(m - m_new)
            l = alpha * l + jnp.sum(p, axis=-1, keepdims=True)
            acc = alpha * acc + jnp.dot(p.astype(BF16), v_ref[0, _att_rows(j), :], preferred_element_type=F32)
            return m_new, l, acc

        init = (jnp.full((ATT_BLK, 1), NEG, F32), jnp.zeros((ATT_BLK, 1), F32), jnp.zeros((ATT_BLK, MLA_V), F32))
        m, l, acc = step(qi, lax.fori_loop(0, qi, step, init), diagonal=True)
        o_ref[...] = acc / l
        lse_ref[0] = m + jnp.log(l)

    return pl.pallas_call(
        body, name=name, grid=(MLA_HEADS, nq),
        in_specs=[pl.BlockSpec((1, ATT_BLK, MLA_QK_PAD), lambda h, i: (h, i, 0)),
                  pl.BlockSpec((1, t, MLA_QK_PAD), lambda h, i: (h, 0, 0)),
                  pl.BlockSpec((1, t, MLA_V), lambda h, i: (h, 0, 0))],
        out_specs=[pl.BlockSpec((ATT_BLK, MLA_V), lambda h, i: (i, h)),
                   pl.BlockSpec((1, ATT_BLK, 1), lambda h, i: (h, i, 0))],
        out_shape=[jax.ShapeDtypeStruct((t, MLA_HEADS * MLA_V), F32), jax.ShapeDtypeStruct((MLA_HEADS, t, 1), F32)],
        compiler_params=_cp("parallel", "parallel"),
    )(q3, k3, v3)


def _attn_bwd(q3, k3, v3, o, dcat, lse, *, name):
    t = q3.shape[1]
    nq = t // ATT_BLK

    def body(q_ref, k_ref, v_ref, o_ref, do_ref, lse_ref, dq_ref, dk_ref, dv_ref, delta_ref):
        kj = pl.program_id(1)
        k, v = k_ref[0], v_ref[0]

        @pl.when(kj == 0)
        def _():
            dq_ref[...] = jnp.zeros_like(dq_ref)
            delta_ref[...] = jnp.sum(o_ref[...] * do_ref[...], axis=-1, keepdims=True)

        def step(i, carry, diagonal=False):
            dk, dv = carry
            rows = _att_rows(i)
            q = q_ref[0, rows, :]
            dob = do_ref[rows, :].astype(BF16)
            p = jnp.exp(_att_scores(q, k, diagonal) - lse_ref[0, rows, :])
            dv = dv + lax.dot_general(p.astype(BF16), dob, _TN, preferred_element_type=F32)
            dp = lax.dot_general(dob, v, _NT, preferred_element_type=F32)
            ds = (p * (dp - delta_ref[rows, :]) * ATT_SCALE).astype(BF16)
            dk = dk + lax.dot_general(ds, q, _TN, preferred_element_type=F32)
            dq_ref[0, rows, :] += jnp.dot(ds, k, preferred_element_type=F32)
            return dk, dv

        init = (jnp.zeros((ATT_BLK, MLA_QK_PAD), F32), jnp.zeros((ATT_BLK, MLA_V), F32))
        dk, dv = lax.fori_loop(kj + 1, nq, step, step(kj, init, diagonal=True))
        dk_ref[0] = dk
        dv_ref[0] = dv

    return pl.pallas_call(
        body, name=name, grid=(MLA_HEADS, nq),
        in_specs=[pl.BlockSpec((1, t, MLA_QK_PAD), lambda h, j: (h, 0, 0)),
                  pl.BlockSpec((1, ATT_BLK, MLA_QK_PAD), lambda h, j: (h, j, 0)),
                  pl.BlockSpec((1, ATT_BLK, MLA_V), lambda h, j: (h, j, 0)),
                  pl.BlockSpec((t, MLA_V), lambda h, j: (0, h)),
                  pl.BlockSpec((t, MLA_V), lambda h, j: (0, MLA_HEADS + h)),
                  pl.BlockSpec((1, t, 1), lambda h, j: (h, 0, 0))],
        out_specs=[pl.BlockSpec((1, t, MLA_QK_PAD), lambda h, j: (h, 0, 0)),
                   pl.BlockSpec((1, ATT_BLK, MLA_QK_PAD), lambda h, j: (h, j, 0)),
                   pl.BlockSpec((1, ATT_BLK, MLA_V), lambda h, j: (h, j, 0))],
        out_shape=[jax.ShapeDtypeStruct((MLA_HEADS, t, MLA_QK_PAD), F32),
                   jax.ShapeDtypeStruct((MLA_HEADS, t, MLA_QK_PAD), F32),
                   jax.ShapeDtypeStruct((MLA_HEADS, t, MLA_V), F32)],
        scratch_shapes=[pltpu.VMEM((t, 1), F32)],
        compiler_params=_cp("parallel", "arbitrary"),
    )(q3, k3, v3, o, dcat, lse)


def _ssd_prep(proj, bias128, alog128, *, name):
    t = proj.shape[0]
    nc = t // CHUNK

    def body(raw_ref, b_ref, al_ref, dt_ref, cs_ref, a_ref):
        xv = raw_ref[...] + b_ref[...]
        dt = jnp.maximum(xv, 0.0) + jnp.log(1.0 + jnp.exp(-jnp.abs(xv)))
        a = -jnp.exp(al_ref[...])
        adt = (dt * a).reshape(nc, CHUNK, LANES)
        li = lax.broadcasted_iota(jnp.int32, (nc, CHUNK, CHUNK), 1)
        si = lax.broadcasted_iota(jnp.int32, (nc, CHUNK, CHUNK), 2)
        tril = jnp.where(si <= li, 1.0, 0.0).astype(F32)
        cs = lax.dot_general(tril, adt, (((2,), (1,)), ((0,), (0,))), precision=HI, preferred_element_type=F32)
        dt_ref[...] = dt
        cs_ref[...] = cs.reshape(t, LANES)
        a_ref[...] = a

    blk = pl.BlockSpec((t, LANES), lambda i: (0, 0))
    row = pl.BlockSpec((1, LANES), lambda i: (0, 0))
    return pl.pallas_call(
        body, name=name, grid=(1,),
        in_specs=[pl.BlockSpec((t, LANES), lambda i: (0, OFF_DT // LANES)), row, row],
        out_specs=[blk, blk, row],
        out_shape=[jax.ShapeDtypeStruct((t, LANES), F32), jax.ShapeDtypeStruct((t, LANES), F32),
                   jax.ShapeDtypeStruct((1, LANES), F32)],
        compiler_params=_cp("arbitrary"),
    )(proj, bias128, alog128)


def _ssd_prep_bwd(ddt128, dadt128, proj, bias128, dt128, a128, dd_h, *, name):
    t = proj.shape[0]

    def body(ddt_ref, dadt_ref, raw_ref, b_ref, dt_ref, a_ref, dd_ref, draw_ref, db_ref, dal_ref, dds_ref):
        draw = ddt_ref[...] * _sigmoid(raw_ref[...] + b_ref[...])
        draw_ref[...] = draw.astype(BF16)
        db_ref[...] = jnp.sum(draw, axis=0, keepdims=True)
        dal_ref[...] = jnp.sum(dadt_ref[...] * dt_ref[...], axis=0, keepdims=True) * a_ref[...]
        dds_ref[...] = jnp.sum(dd_ref[...], axis=-1, keepdims=True)

    blk = pl.BlockSpec((t, LANES), lambda i: (0, 0))
    row = pl.BlockSpec((1, LANES), lambda i: (0, 0))
    return pl.pallas_call(
        body, name=name, grid=(1,),
        in_specs=[blk, blk, pl.BlockSpec((t, LANES), lambda i: (0, OFF_DT // LANES)), row, blk, row,
                  pl.BlockSpec((SSD_HEADS, SSD_P), lambda i: (0, 0))],
        out_specs=[blk, row, row, pl.BlockSpec((SSD_HEADS, 1), lambda i: (0, 0))],
        out_shape=[jax.ShapeDtypeStruct((t, LANES), BF16), jax.ShapeDtypeStruct((1, LANES), F32),
                   jax.ShapeDtypeStruct((1, LANES), F32), jax.ShapeDtypeStruct((SSD_HEADS, 1), F32)],
        compiler_params=_cp("arbitrary"),
    )(ddt128, dadt128, proj, bias128, dt128, a128, dd_h)


def _bdot(a, b, ca, cb, precision=None):
    return lax.dot_general(a, b, (((ca,), (cb,)), ((0,), (0,))), precision=precision, preferred_element_type=F32)


def _ssd_common(xs_ref, dt_ref, cs_ref, csr_ref, b_ref, c_ref, nc):
    x = xs_ref[0].reshape(nc, CHUNK, SSD_P)
    dt = dt_ref[0].reshape(nc, CHUNK, SSD_P)
    cs = cs_ref[0].reshape(nc, CHUNK, SSD_P)
    csr = csr_ref[0]
    bm = b_ref[0].reshape(nc, CHUNK, SSD_N).astype(BF16)
    cm = c_ref[0].reshape(nc, CHUNK, SSD_N).astype(BF16)
    li = lax.broadcasted_iota(jnp.int32, (nc, CHUNK, CHUNK), 1)
    si = lax.broadcasted_iota(jnp.int32, (nc, CHUNK, CHUNK), 2)
    lmat = jnp.exp(jnp.where(si <= li, cs - csr, NEG))
    g = _bdot(cm, bm, 2, 2)
    cs_last = jnp.sum(jnp.where(li == CHUNK - 1, cs, 0.0), axis=1, keepdims=True)
    xdt = x * dt
    dec = jnp.exp(cs_last - cs)
    return x, dt, cs, bm, cm, li, si, lmat, g, cs_last, xdt, dec


def _ssd_fwd(xs_h, dt_h, cs_h, cs_row, b_g, c_g, dskip_h, *, name):
    t = xs_h.shape[1]
    nc = t // CHUNK
    hpg = SSD_HEADS // SSD_GROUPS

    def body(xs_ref, dt_ref, cs_ref, csr_ref, b_ref, c_ref, dk_ref, y_ref, st_ref, sc_ref, cd_ref):
        x, dt, cs, bm, cm, li, si, lmat, g, cs_last, xdt, dec = _ssd_common(xs_ref, dt_ref, cs_ref, csr_ref, b_ref,
                                                                           c_ref, nc)
        yd = _bdot((g * lmat).astype(BF16), xdt.astype(BF16), 2, 1)
        sc_ref[...] = _bdot(bm, (dec * xdt).astype(BF16), 1, 1)
        cd_ref[...] = jnp.exp(cs_last)

        def step(c, s):
            st_ref[0, c] = s
            return s * cd_ref[c] + sc_ref[c]

        lax.fori_loop(0, nc, step, jnp.zeros((SSD_N, SSD_P), F32))
        yo = _bdot(cm, st_ref[0].astype(BF16), 2, 1) * jnp.exp(cs)
        y_ref[0] = (yd + yo + dk_ref[0] * x).reshape(t, SSD_P)

    head = pl.BlockSpec((1, t, SSD_P), lambda h: (h, 0, 0))
    grp = pl.BlockSpec((1, t, SSD_N), lambda h: (h // hpg, 0, 0))
    return pl.pallas_call(
        body, name=name, grid=(SSD_HEADS,),
        in_specs=[head, head, head, pl.BlockSpec((1, nc, 1, CHUNK), lambda h: (h, 0, 0, 0)), grp, grp,
                  pl.BlockSpec((1, 1, SSD_P), lambda h: (h, 0, 0))],
        out_specs=[head, pl.BlockSpec((1, nc, SSD_N, SSD_P), lambda h: (h, 0, 0, 0))],
        out_shape=[jax.ShapeDtypeStruct((SSD_HEADS, t, SSD_P), F32),
                   jax.ShapeDtypeStruct((SSD_HEADS, nc, SSD_N, SSD_P), F32)],
        scratch_shapes=[pltpu.VMEM((nc, SSD_N, SSD_P), F32), pltpu.VMEM((nc, 1, SSD_P), F32)],
        compiler_params=_cp("parallel"),
    )(xs_h, dt_h, cs_h, cs_row, b_g, c_g, dskip_h)


def _ssd_bwd(xs_h, dt_h, cs_h, cs_row, b_g, c_g, dskip_h, a_h, states, dy_h, *, name):
    t = xs_h.shape[1]
    nc = t // CHUNK
    hpg = SSD_HEADS // SSD_GROUPS

    def body(xs_ref, dt_ref, cs_ref, csr_ref, b_ref, c_ref, dk_ref, a_ref, st_ref, dy_ref,
             dxs_ref, ddt_ref, dadt_ref, db_ref, dc_ref, dd_ref, dsl_ref, dsc_ref, cd_ref):
        x, dt, cs, bm, cm, li, si, lmat, g, cs_last, xdt, dec = _ssd_common(xs_ref, dt_ref, cs_ref, csr_ref, b_ref,
                                                                           c_ref, nc)
        dy = dy_ref[0].reshape(nc, CHUNK, SSD_P)
        dyb = dy.astype(BF16)
        xdtb = xdt.astype(BF16)
        sprev = st_ref[0]
        sprevb = sprev.astype(BF16)
        cdec = jnp.exp(cs_last)
        ecs = jnp.exp(cs)
        dw = (ecs * dy).astype(BF16)
        wmat = _bdot(cm, sprevb, 2, 1)
        dcs = jnp.sum(dy * ecs * wmat, axis=2, keepdims=True)
        dcm = _bdot(dw, sprevb, 2, 2)
        dsl_ref[...] = _bdot(cm, dw, 1, 1)
        cd_ref[...] = cdec

        def step(k, ds):
            c = nc - 1 - k
            dsc_ref[c] = ds
            return ds * cd_ref[c] + dsl_ref[c]

        lax.fori_loop(0, nc, step, jnp.zeros((SSD_N, SSD_P), F32))
        dsc = dsc_ref[...]
        dscb = dsc.astype(BF16)
        d_last = jnp.sum(jnp.sum(dsc * sprev, axis=1, keepdims=True) * cdec, axis=2, keepdims=True)
        z = dec * xdt
        dbm = _bdot(z.astype(BF16), dscb, 2, 2)
        dz = _bdot(bm, dscb, 2, 1)
        dxdt = dec * dz
        t2 = jnp.sum(dz * z, axis=2, keepdims=True)
        dcs = dcs - t2
        d_last = d_last + jnp.sum(t2, axis=1, keepdims=True)
        m = g * lmat
        mb = m.astype(BF16)
        dm = _bdot(dyb, xdtb, 2, 2)
        dxdt = dxdt + _bdot(mb, dyb, 1, 1)
        dseg = dm * m
        dcs = dcs + jnp.sum(dseg, axis=2, keepdims=True)
        ones = jnp.ones((nc, CHUNK, SSD_P), F32)
        dcs = dcs - _bdot(dseg, ones, 1, 1, precision=HI)
        dg = (dm * lmat).astype(BF16)
        dcm = dcm + _bdot(dg, bm, 2, 1)
        dbm = dbm + _bdot(dg, cm, 1, 1)
        dcs = dcs + jnp.where(li[:, :, :SSD_P] == CHUNK - 1, d_last, 0.0)
        triu = jnp.where(li <= si, 1.0, 0.0).astype(F32)
        dadt = _bdot(triu, dcs, 2, 1, precision=HI)
        dk = dk_ref[0]
        dxs_ref[0] = (dxdt * dt + dk * dy).reshape(t, SSD_P)
        ddt_ref[0] = (jnp.sum(dxdt * x, axis=2, keepdims=True) + dadt * a_ref[0]).reshape(t, SSD_P)
        dadt_ref[0] = dadt.reshape(t, SSD_P)
        dd_ref[0] = jnp.sum(jnp.sum(dy * x, axis=1, keepdims=True), axis=0)

        @pl.when(pl.program_id(1) == 0)
        def _():
            db_ref[...] = jnp.zeros_like(db_ref)
            dc_ref[...] = jnp.zeros_like(dc_ref)

        db_ref[0] += dbm.reshape(t, SSD_N)
        dc_ref[0] += dcm.reshape(t, SSD_N)

    head = pl.BlockSpec((1, t, SSD_P), lambda gi, hi: (gi * hpg + hi, 0, 0))
    grp = pl.BlockSpec((1, t, SSD_N), lambda gi, hi: (gi, 0, 0))
    lane = pl.BlockSpec((1, 1, SSD_P), lambda gi, hi: (gi * hpg + hi, 0, 0))
    return pl.pallas_call(
        body, name=name, grid=(SSD_GROUPS, hpg),
        in_specs=[head, head, head, pl.BlockSpec((1, nc, 1, CHUNK), lambda gi, hi: (gi * hpg + hi, 0, 0, 0)),
                  grp, grp, lane, lane, pl.BlockSpec((1, nc, SSD_N, SSD_P), lambda gi, hi: (gi * hpg + hi, 0, 0, 0)),
                  head],
        out_specs=[head, head, head, grp, grp, lane],
        out_shape=[jax.ShapeDtypeStruct((SSD_HEADS, t, SSD_P), F32)] * 3
        + [jax.ShapeDtypeStruct((SSD_GROUPS, t, SSD_N), F32)] * 2
        + [jax.ShapeDtypeStruct((SSD_HEADS, 1, SSD_P), F32)],
        scratch_shapes=[pltpu.VMEM((nc, SSD_N, SSD_P), F32), pltpu.VMEM((nc, SSD_N, SSD_P), F32),
                        pltpu.VMEM((nc, 1, SSD_P), F32)],
        compiler_params=_cp("parallel", "arbitrary"),
    )(xs_h, dt_h, cs_h, cs_row, b_g, c_g, dskip_h, a_h, states, dy_h)


def _ssd_gate_fwd(y, proj, w, *, tr=256, name):
    t = y.shape[0]
    gw = D_SSM // SSD_GROUPS

    def body(y_ref, z_ref, w_ref, o_ref):
        v = y_ref[...] * _silu(z_ref[...])
        for gi in range(SSD_GROUPS):
            vg = v[:, gi * gw:(gi + 1) * gw]
            r = lax.rsqrt(jnp.mean(vg * vg, axis=-1, keepdims=True) + NORM_EPS)
            o_ref[:, gi * gw:(gi + 1) * gw] = (vg * r * w_ref[:, gi * gw:(gi + 1) * gw]).astype(BF16)

    blk = pl.BlockSpec((tr, D_SSM), lambda i: (i, 0))
    return pl.pallas_call(
        body, name=name, grid=(t // tr,), in_specs=[blk, blk, pl.BlockSpec((1, D_SSM), lambda i: (0, 0))],
        out_specs=blk, out_shape=jax.ShapeDtypeStruct((t, D_SSM), BF16), compiler_params=_cp("parallel"),
    )(y, proj, w)


def _ssd_gate_bwd(y, proj, w, dcat, *, tr=256, name):
    t = y.shape[0]
    gw = D_SSM // SSD_GROUPS

    def body(y_ref, z_ref, w_ref, d_ref, dy_ref, dz_ref, dw_ref):
        yv, zv, dv = y_ref[...], z_ref[...], d_ref[...].astype(F32)
        sz = _silu(zv)
        v = yv * sz

        @pl.when(pl.program_id(0) == 0)
        def _():
            dw_ref[...] = jnp.zeros_like(dw_ref)

        for gi in range(SSD_GROUPS):
            sl = slice(gi * gw, (gi + 1) * gw)
            vg, dg = v[:, sl], dv[:, sl]
            r = lax.rsqrt(jnp.mean(vg * vg, axis=-1, keepdims=True) + NORM_EPS)
            vh = vg * r
            gg = dg * w_ref[:, sl]
            dvg = r * (gg - vh * jnp.mean(gg * vh, axis=-1, keepdims=True))
            dy_ref[:, sl] = dvg * sz[:, sl]
            dz_ref[:, sl] = (dvg * yv[:, sl] * _dsilu(zv[:, sl])).astype(BF16)
            dw_ref[:, sl] += jnp.sum(dg * vh, axis=0, keepdims=True)

    blk = pl.BlockSpec((tr, D_SSM), lambda i: (i, 0))
    row = pl.BlockSpec((1, D_SSM), lambda i: (0, 0))
    return pl.pallas_call(
        body, name=name, grid=(t // tr,), in_specs=[blk, blk, row, blk], out_specs=[blk, blk, row],
        out_shape=[jax.ShapeDtypeStruct((t, D_SSM), F32), jax.ShapeDtypeStruct((t, D_SSM), BF16),
                   jax.ShapeDtypeStruct((1, D_SSM), F32)],
        compiler_params=_cp("arbitrary"),
    )(y, proj, w, dcat)


def _pad_lanes(v):
    return jnp.pad(v, ((0, 0), (0, LANES - v.shape[1])))


def _to_heads(v):
    return v.reshape(v.shape[0], SSD_HEADS, SSD_P).transpose(1, 0, 2)


def _from_heads(v):
    return v.transpose(1, 0, 2).reshape(v.shape[1], SSD_HEADS * SSD_P)


def _per_head(v128, t):
    return jnp.broadcast_to(v128[:, :SSD_HEADS].T[:, :, None], (SSD_HEADS, t, SSD_P))


def _ssd_forward(proj, conv_w, conv_b, dt_bias, a_log, d_skip, ssd_norm_w):
    t = proj.shape[0]
    nc = t // CHUNK
    xbc = _conv_act_fwd(proj, conv_w, conv_b, kw=SSD_CONV, glu=False, tc=512, coff=OFF_XBC // 512,
                        ncols=SSD_CONV_DIM, out_dtype=F32, name="ssd_conv_fwd")
    bias128, alog128 = _pad_lanes(dt_bias), _pad_lanes(a_log)
    dt128, cs128, a128 = _ssd_prep(proj, bias128, alog128, name="ssd_prep")
    dt_h, cs_h = _per_head(dt128, t), _per_head(cs128, t)
    cs_row = cs128[:, :SSD_HEADS].T.reshape(SSD_HEADS, nc, 1, CHUNK)
    xs_h = _to_heads(xbc[:, :D_SSM])
    gn = SSD_GROUPS * SSD_N
    b_g = xbc[:, D_SSM:D_SSM + gn].reshape(t, SSD_GROUPS, SSD_N).transpose(1, 0, 2)
    c_g = xbc[:, D_SSM + gn:].reshape(t, SSD_GROUPS, SSD_N).transpose(1, 0, 2)
    dskip_h = jnp.broadcast_to(d_skip[0][:, None, None], (SSD_HEADS, 1, SSD_P))
    a_h = jnp.broadcast_to(a128[0, :SSD_HEADS][:, None, None], (SSD_HEADS, 1, SSD_P))
    y_h, states = _ssd_fwd(xs_h, dt_h, cs_h, cs_row, b_g, c_g, dskip_h, name="ssd_scan_fwd")
    y = _from_heads(y_h)
    y_ssd = _ssd_gate_fwd(y, proj, ssd_norm_w, name="ssd_gate_fwd")
    saved = (proj, conv_w, conv_b, ssd_norm_w, bias128, dt128, a128, dt_h, cs_h, cs_row, xs_h, b_g, c_g, dskip_h, a_h,
             states, y)
    return y_ssd, saved


def _ssd_backward(saved, dcat):
    (proj, conv_w, conv_b, ssd_norm_w, bias128, dt128, a128, dt_h, cs_h, cs_row, xs_h, b_g, c_g, dskip_h, a_h, states,
     y) = saved
    t = proj.shape[0]
    dy, dz, d_norm_w = _ssd_gate_bwd(y, proj, ssd_norm_w, dcat, name="ssd_gate_bwd")
    dxs_h, ddt_h, dadt_h, db_g, dc_g, dd_h = _ssd_bwd(xs_h, dt_h, cs_h, cs_row, b_g, c_g, dskip_h, a_h, states,
                                                      _to_heads(dy), name="ssd_scan_bwd")
    gn = SSD_GROUPS * SSD_N
    dxc = jnp.concatenate([_from_heads(dxs_h), db_g.transpose(1, 0, 2).reshape(t, gn),
                           dc_g.transpose(1, 0, 2).reshape(t, gn)], axis=1)
    dxbc, d_conv_w, d_conv_b = _conv_act_bwd(proj, conv_w, conv_b, dxc, kw=SSD_CONV, glu=False, tc=512,
                                             coff=OFF_XBC // 512, ncols=SSD_CONV_DIM, name="ssd_conv_bwd")
    ddt128 = _pad_lanes(ddt_h[:, :, 0].T)
    dadt128 = _pad_lanes(dadt_h[:, :, 0].T)
    d_raw, d_bias, d_alog, d_dskip = _ssd_prep_bwd(ddt128, dadt128, proj, bias128, dt128, a128,
                                                   dd_h.reshape(SSD_HEADS, SSD_P), name="ssd_prep_bwd")
    return (dz, dxbc, d_raw, d_norm_w, d_conv_w, d_conv_b, d_bias[:, :SSD_HEADS], d_alog[:, :SSD_HEADS],
            d_dskip.reshape(1, SSD_HEADS))


def _rope_tables(positions):
    inv_freq = ROPE_THETA ** (-jnp.arange(0, MLA_ROPE, 2, dtype=F32) / MLA_ROPE)
    ang = positions[0].astype(F32)[:, None] * inv_freq
    cos, sin = jnp.cos(ang), jnp.sin(ang)
    z = jnp.zeros_like(cos)
    return jnp.stack([jnp.concatenate([cos, cos, z, z], axis=1), jnp.concatenate([-sin, z, z, z], axis=1),
                      jnp.concatenate([z, sin, z, z], axis=1)])


def _mla_forward(proj, tabs, q_a_norm_w, wq_pad, kv_a_norm_w, wkv):
    qn = _rmsnorm_fwd(proj, q_a_norm_w, width=MLA_Q_RANK, cblk=OFF_QA // MLA_Q_RANK, name="q_a_norm")
    q = _matmul(qn, wq_pad, name="q_b_proj")
    kvn = _rmsnorm_fwd(proj, kv_a_norm_w, width=MLA_KV_RANK, cblk=OFF_CKV // MLA_KV_RANK, name="kv_a_norm")
    kv = _matmul(kvn, wkv, name="kv_b_proj")
    q3, k3, v3 = _mla_prep(q, kv, proj, tabs, name="mla_prep")
    o, lse = _attn_fwd(q3, k3, v3, name="attn_fwd")
    return o, (proj, tabs, q_a_norm_w, wq_pad, kv_a_norm_w, wkv, qn, kvn, q3, k3, v3, o, lse)


def _mla_backward(saved, dcat):
    proj, tabs, q_a_norm_w, wq_pad, kv_a_norm_w, wkv, qn, kvn, q3, k3, v3, o, lse = saved
    dq3, dk3, dv3 = _attn_bwd(q3, k3, v3, o, dcat, lse, name="attn_bwd")
    dq, dkv, dkr = _mla_unprep(dq3, dk3, dv3, tabs, name="mla_unprep")
    d_wq = _matmul(qn, dq, ta=True, out_dtype=BF16, name="d_w_q_b")
    dqn = _matmul(dq, wq_pad, tb=True, name="d_qn")
    dq_a, d_qnw = _rmsnorm_bwd(proj, q_a_norm_w, dqn, width=MLA_Q_RANK, cblk=OFF_QA // MLA_Q_RANK, out_dtype=BF16,
                               name="q_a_norm_bwd")
    d_wkv = _matmul(kvn, dkv, ta=True, out_dtype=BF16, name="d_w_kv_b")
    dkvn = _matmul(dkv, wkv, tb=True, name="d_kvn")
    dckv, d_kvnw = _rmsnorm_bwd(proj, kv_a_norm_w, dkvn, width=MLA_KV_RANK, cblk=OFF_CKV // MLA_KV_RANK,
                                out_dtype=BF16, name="kv_a_norm_bwd")
    return dq_a, dckv, dkr, d_wq, d_wkv, d_qnw, d_kvnw


def _pad_w_q(w):
    r = w.shape[0]
    w3 = w.reshape(r, MLA_HEADS, MLA_NOPE + MLA_ROPE)
    return jnp.pad(w3, ((0, 0), (0, 0), (0, MLA_QK_PAD - MLA_NOPE - MLA_ROPE))).reshape(r, MLA_HEADS * MLA_QK_PAD)


def _unpad_w_q(w):
    r = w.shape[0]
    return w.reshape(r, MLA_HEADS, MLA_QK_PAD)[:, :, :MLA_NOPE + MLA_ROPE].reshape(r, MLA_HEADS * (MLA_NOPE + MLA_ROPE))


def _pad_w_in(w):
    r = w.shape[0]
    o_dt = D_SSM + SSD_CONV_DIM
    o_qa = o_dt + SSD_HEADS
    o_kr = o_qa + MLA_Q_RANK + MLA_KV_RANK
    zeros = lambda n: jnp.zeros((r, n), w.dtype)
    return jnp.concatenate([w[:, :o_dt], w[:, o_qa:o_kr], w[:, o_kr:], zeros(LANES - MLA_ROPE),
                            w[:, o_dt:o_qa], zeros(LANES - SSD_HEADS)], axis=1)


def _unpad_w_in(w):
    return jnp.concatenate([w[:, :OFF_QA], w[:, OFF_DT:OFF_DT + SSD_HEADS], w[:, OFF_QA:OFF_KR + MLA_ROPE]], axis=1)


WEIGHTS = ['mix_norm_w', 'w_in', 'conv_w', 'conv_b', 'dt_bias', 'a_log', 'd_skip', 'ssd_norm_w', 'q_a_norm_w', 'w_q_b',
           'kv_a_norm_w', 'w_kv_b', 'w_out', 'ffn_norm_w', 'w_ffn_up', 'ffn_conv_w', 'ffn_conv_b', 'w_ffn_down',
           'ple_norm_w', 'w_ple_gate', 'b_ple_gate', 'w_ple_proj', 'ple_post_norm_w', 'final_norm_w']
BIG = ['w_in', 'w_q_b', 'w_kv_b', 'w_out', 'w_ffn_up', 'w_ffn_down', 'w_ple_gate', 'w_ple_proj']
COL_SHARDED = ('w_in', 'w_q_b', 'w_kv_b', 'w_ffn_up', 'w_ple_proj')
CONV = ['conv_w', 'ffn_conv_w']
REPL = [n for n in WEIGHTS if n not in BIG and n not in CONV]
FFN_INV = tuple(int(i) for i in np.argsort(FFN_PERM))


def _cat_cols(g):
    return g.transpose(1, 0, 2).reshape(g.shape[1], N_DEV * g.shape[2])


def _split_cols(w):
    return w.reshape(w.shape[0], N_DEV, w.shape[1] // N_DEV).transpose(1, 0, 2)


def _interleave(v):
    r = v.shape[0]
    return v.reshape(r, N_DEV, FFN_TC)[:, jnp.array(FFN_PERM)].reshape(r, N_DEV * FFN_TC)


def _deinterleave(v):
    r = v.shape[0]
    return v.reshape(r, N_DEV, FFN_TC)[:, jnp.array(FFN_INV)].reshape(r, N_DEV * FFN_TC)


def _assemble_weights(g):
    layout = {
        'w_in': lambda v: _pad_w_in(_cat_cols(v)),
        'w_q_b': lambda v: _pad_w_q(_cat_cols(v)),
        'w_kv_b': _cat_cols,
        'w_out': lambda v: v.reshape(D_MODEL, D_MODEL),
        'w_ffn_up': lambda v: v,
        'w_ffn_down': lambda v: v.reshape(D_FF, D_MODEL),
        'w_ple_gate': lambda v: v.reshape(D_MODEL, D_MODEL),
        'w_ple_proj': _cat_cols,
        'conv_w': _cat_cols,
        'ffn_conv_w': lambda v: _interleave(_cat_cols(v)),
    }
    return {n: layout[n](v) for n, v in g.items()}


WEIGHT_GROUPS = {'a': ['w_in', 'w_q_b', 'w_kv_b', 'conv_w'], 'b': ['w_out'],
                 'c': ['w_ffn_up', 'ffn_conv_w', 'w_ffn_down', 'w_ple_gate', 'w_ple_proj']}
GRAD_GROUPS = {'p': ['w_ple_proj', 'w_ple_gate', 'w_ffn_down'], 'r': ['w_ffn_up'], 's': ['w_out'],
               't': ['w_q_b', 'w_kv_b', 'w_in']}


def _ffn_perm(j):
    return (j % 2) * (N_DEV // 2) + j // 2


def _local_step(x, p, tabs, get_w, s, target, emit, relay):
    t = x.shape[0]
    s = dict(s)
    half = D_MODEL // 2
    up_cols = 2 * D_FF
    ffn_conv_b = _interleave(s['ffn_conv_b'])
    w = dict(get_w('a', None))
    h = _rmsnorm_fwd(x, s['mix_norm_w'], width=D_MODEL, name="mix_norm")
    proj = _matmul(h, w['w_in'], name="in_proj")
    y_ssd, ssd_saved = _ssd_forward(proj, w['conv_w'], s['conv_b'], s['dt_bias'], s['a_log'], s['d_skip'],
                                    s['ssd_norm_w'])
    o, mla_saved = _mla_forward(proj, tabs, s['q_a_norm_w'], w['w_q_b'], s['kv_a_norm_w'], w['w_kv_b'])
    tk_o, tn_o = _tile(half, MM_TK), _tile(D_MODEL, MM_TILE)
    w.update(get_w('b', o))
    x1 = _matmul(y_ssd, w['w_out'], add=x, mnk=(t, D_MODEL, half), name="out_proj_ssd")
    x1 = _matmul(o, w['w_out'], add=x1, mnk=(t, D_MODEL, half), name="out_proj_mla",
                 b_spec=pl.BlockSpec((tk_o, tn_o), lambda i, j, kk: (kk + half // tk_o, j)))
    hf = _rmsnorm_fwd(x1, s['ffn_norm_w'], width=D_MODEL, name="ffn_norm")
    w.update(get_w('c', hf))
    tk_u = _tile(D_MODEL, MM_TK)
    u = _matmul(hf, w['w_ffn_up'], mnk=(t, up_cols, D_MODEL), tn=FFN_TC, name="ffn_up",
                b_spec=pl.BlockSpec((1, tk_u, FFN_TC), lambda i, j, kk: (_ffn_perm(j), kk, 0)))
    act = _conv_act_fwd(u, w['ffn_conv_w'], ffn_conv_b, kw=FFN_CONV, glu=True, tc=2 * FFN_TC, coff=0, ncols=up_cols,
                        out_dtype=BF16, name="ffn_act")
    x2 = _matmul(act, w['w_ffn_down'], add=x1, name="ffn_down")
    hp = _rmsnorm_fwd(x2, s['ple_norm_w'], width=D_MODEL, name="ple_norm")
    gl = _matmul(hp, w['w_ple_gate'], bias=s['b_ple_gate'], name="ple_gate")
    pe = _matmul(p, w['w_ple_proj'], name="ple_proj")
    x3 = _ple_fwd(x2, gl, pe, s['ple_post_norm_w'], name="ple_mix")
    loss, dx3, d_final = _loss_head(x3, s['final_norm_w'], target, name="loss_head")
    dgl, d_bgate, dpe, d_post = _ple_bwd(dx3, gl, pe, s['ple_post_norm_w'], name="ple_mix_bwd")
    d_wproj = _matmul(p, dpe, ta=True, out_dtype=BF16, name="d_w_ple_proj")
    d_wgate = _matmul(hp, dgl, ta=True, out_dtype=BF16, name="d_w_ple_gate")
    dhp = _matmul(dgl, w['w_ple_gate'], tb=True, name="d_ple_normed")
    dx2, d_plenorm = _rmsnorm_bwd(x2, s['ple_norm_w'], dhp, dx3, width=D_MODEL, name="ple_norm_bwd")
    dact = _matmul(dx2, w['w_ffn_down'], tb=True, name="d_ffn_act")
    d_wdown = _matmul(act, dx2, ta=True, out_dtype=BF16, name="d_w_ffn_down")
    zz = emit('p', {'w_ple_proj': _split_cols(d_wproj), 'w_ple_gate': d_wgate.reshape(N_DEV, D_MODEL // N_DEV, D_MODEL),
                    'w_ffn_down': d_wdown.reshape(N_DEV, D_FF // N_DEV, D_MODEL)})
    du, d_fconv_w, d_fconv_b = _conv_act_bwd(u, w['ffn_conv_w'], ffn_conv_b + zz, dact, kw=FFN_CONV, glu=True,
                                             tc=2 * FFN_TC, coff=0, ncols=up_cols, name="ffn_act_bwd")
    zz = zz + relay('p', du)
    tm_u = _tile(D_MODEL, MM_TILE)
    d_wup = _matmul(hf, du, ta=True, out_dtype=BF16, mnk=(D_MODEL, up_cols, t), tn=FFN_TC, name="d_w_ffn_up",
                    o_spec=pl.BlockSpec((1, tm_u, FFN_TC), lambda i, j, kk: (_ffn_perm(j), i, 0)),
                    o_shape=(N_DEV, D_MODEL, FFN_TC))
    zz = zz + emit('r', {'w_ffn_up': d_wup})
    dhf = _matmul(du, w['w_ffn_up'], tb=True, mnk=(t, D_MODEL, up_cols), tk=FFN_TC, name="d_ffn_normed",
                  b_spec=pl.BlockSpec((1, tn_o, FFN_TC), lambda i, j, kk: (_ffn_perm(kk), j, 0)))
    zz = zz + relay('r', dhf)
    dx1, d_ffnnorm = _rmsnorm_bwd(x1, s['ffn_norm_w'] + zz, dhf, dx2, width=D_MODEL, name="ffn_norm_bwd")
    dcat = _matmul(dx1, w['w_out'], tb=True, name="d_mixed")
    d_wout = jnp.concatenate([_matmul(y_ssd, dx1, ta=True, out_dtype=BF16, name="d_w_out_ssd"),
                              _matmul(o, dx1, ta=True, out_dtype=BF16, name="d_w_out_mla")], axis=0)
    zz = zz + emit('s', {'w_out': d_wout.reshape(N_DEV, D_MODEL // N_DEV, D_MODEL)})
    ssd_saved = ssd_saved[:3] + (ssd_saved[3] + zz,) + ssd_saved[4:]
    dz, dxbc, d_raw, d_ssdnorm, d_conv_w, d_conv_b, d_dtb, d_alog, d_dskip = _ssd_backward(ssd_saved, dcat)
    zz = zz + relay('s', dz)
    mla_saved = mla_saved[:-1] + (mla_saved[-1] + zz,)
    dq_a, dckv, dkr, d_wq, d_wkv, d_qnorm, d_kvnorm = _mla_backward(mla_saved, dcat)
    dproj = jnp.concatenate([dz, dxbc, dq_a, dckv, dkr, d_raw], axis=1)
    d_win = _matmul(h, dproj, ta=True, out_dtype=BF16, name="d_w_in")
    dh = _matmul(dproj, w['w_in'], tb=True, name="d_in_normed")
    dx, d_mixnorm = _rmsnorm_bwd(x, s['mix_norm_w'], dh, dx1, width=D_MODEL, name="mix_norm_bwd")
    emit('t', {'w_in': _split_cols(_unpad_w_in(d_win)), 'w_q_b': _split_cols(_unpad_w_q(d_wq)),
               'w_kv_b': _split_cols(d_wkv)})
    relay('t', dx)
    conv = {'conv_w': d_conv_w, 'ffn_conv_w': _deinterleave(d_fconv_w)}
    vec = {
        'mix_norm_w': d_mixnorm, 'conv_b': d_conv_b, 'dt_bias': d_dtb, 'a_log': d_alog, 'd_skip': d_dskip,
        'ssd_norm_w': d_ssdnorm, 'q_a_norm_w': d_qnorm, 'kv_a_norm_w': d_kvnorm, 'ffn_norm_w': d_ffnnorm,
        'ffn_conv_b': _deinterleave(d_fconv_b), 'ple_norm_w': d_plenorm, 'b_ple_gate': d_bgate,
        'ple_post_norm_w': d_post, 'final_norm_w': d_final,
    }
    return loss, dx, conv, vec


MESH = pl.DeviceIdType.MESH
FLIPS = ((0, 0, 1), (1, 0, 0), (0, 1, 0), (1, 1, 0), (1, 0, 1), (0, 1, 1), (1, 1, 1))


def _exchange(items, *, gather, name):
    n = len(items)

    def body(*refs):
        ins, outs = refs[:n], refs[n:2 * n]
        send_sems, recv_sems, local_sems = refs[2 * n:]
        x, y, c = lax.axis_index("x"), lax.axis_index("y"), lax.axis_index("c")
        me = 4 * x + 2 * y + c
        peers = [(jnp.where(fx, 1 - x, x), jnp.where(fy, 1 - y, y), jnp.where(fc, 1 - c, c)) for fx, fy, fc in FLIPS]
        slot = [4 * px + 2 * py + pc for px, py, pc in peers]
        local, sends = [], []
        for wi in range(n):
            cp = pltpu.make_async_copy(ins[wi] if gather else ins[wi].at[me], outs[wi].at[me], local_sems.at[wi])
            cp.start()
            local.append(cp)
            for k, peer in enumerate(peers):
                cp = pltpu.make_async_remote_copy(
                    src_ref=ins[wi] if gather else ins[wi].at[slot[k]], dst_ref=outs[wi].at[me],
                    send_sem=send_sems.at[k, wi], recv_sem=recv_sems.at[k, wi], device_id=peer, device_id_type=MESH)
                cp.start()
                sends.append(cp)
        for wi in range(n):
            for k, peer in enumerate(peers):
                pltpu.make_async_remote_copy(
                    src_ref=outs[wi].at[slot[k]], dst_ref=outs[wi].at[slot[k]], send_sem=send_sems.at[k, wi],
                    recv_sem=recv_sems.at[k, wi], device_id=peer, device_id_type=MESH).wait_recv()
        for cp in sends:
            cp.wait_send()
        for cp in local:
            cp.wait()

    hbm = pl.BlockSpec(memory_space=pltpu.HBM)
    out_shape = [jax.ShapeDtypeStruct(((N_DEV,) + v.shape) if gather else v.shape, v.dtype) for v in items]
    return pl.pallas_call(
        body, name=name, in_specs=[hbm] * n, out_specs=[hbm] * n, out_shape=out_shape,
        scratch_shapes=[pltpu.SemaphoreType.DMA((len(FLIPS), n)), pltpu.SemaphoreType.DMA((len(FLIPS), n)),
                        pltpu.SemaphoreType.DMA((n,))],
    )(*items)


HBM_SPEC = pl.BlockSpec(memory_space=pltpu.HBM)
SEM_SPEC = pl.BlockSpec(memory_space=pltpu.SEMAPHORE)
EFFECT = pltpu.SideEffectType.DATAFLOW_SIDE_EFFECTING


def _peers():
    x, y, c = lax.axis_index("x"), lax.axis_index("y"), lax.axis_index("c")
    peers = [(jnp.where(fx, 1 - x, x), jnp.where(fy, 1 - y, y), jnp.where(fc, 1 - c, c)) for fx, fy, fc in FLIPS]
    return 4 * x + 2 * y + c, peers, [4 * px + 2 * py + pc for px, py, pc in peers]


def _split_start(bufs, ncopies, plan, *, name):
    nb = len(bufs)

    def body(*refs):
        send_sems, recv_sems, token = refs[nb], refs[nb + 1], refs[2 * nb + 2]
        for i, (src, dst, peer, _) in enumerate(plan(refs[:nb])):
            pltpu.make_async_remote_copy(src_ref=src, dst_ref=dst, send_sem=send_sems.at[i], recv_sem=recv_sems.at[i],
                                         device_id=peer, device_id_type=MESH).start()
        token[...] = jnp.zeros_like(token)

    res = pl.pallas_call(
        body, name=name, in_specs=[HBM_SPEC] * nb,
        out_specs=[SEM_SPEC, SEM_SPEC] + [HBM_SPEC] * nb + [pl.BlockSpec(memory_space=pltpu.VMEM)],
        out_shape=[pltpu.SemaphoreType.DMA((ncopies,)), pltpu.SemaphoreType.DMA((ncopies,))]
        + [pltpu.HBM(v.shape, v.dtype) for v in bufs] + [jax.ShapeDtypeStruct((HALO, LANES), F32)],
        input_output_aliases={i: 2 + i for i in range(nb)},
        compiler_params=pltpu.CompilerParams(has_side_effects=EFFECT),
    )(*[pltpu.with_memory_space_constraint(v, pltpu.HBM) for v in bufs])
    return (res[0], res[1], list(res[2:2 + nb])), res[2 + nb]


def _split_wait(started, after, plan, local_plan, *, name):
    send_sems, recv_sems, bufs = started
    nb = len(bufs)
    nlocal = len(local_plan(bufs))

    def body(*refs):
        send_sems, recv_sems = refs[nb], refs[nb + 1]
        local_sems = refs[2 * nb + 3]
        local = []
        for j, (src, dst) in enumerate(local_plan(refs[:nb])):
            cp = pltpu.make_async_copy(src, dst, local_sems.at[j])
            cp.start()
            local.append(cp)
        for i, (src, _, peer, incoming) in enumerate(plan(refs[:nb])):
            cp = pltpu.make_async_remote_copy(src_ref=src, dst_ref=incoming, send_sem=send_sems.at[i],
                                              recv_sem=recv_sems.at[i], device_id=peer, device_id_type=MESH)
            cp.wait_send()
            cp.wait_recv()
        for cp in local:
            cp.wait()

    res = pl.pallas_call(
        body, name=name, in_specs=[HBM_SPEC] * nb + [SEM_SPEC, SEM_SPEC, pl.BlockSpec(memory_space=pl.ANY)],
        out_specs=[HBM_SPEC] * nb, out_shape=[pltpu.HBM(v.shape, v.dtype) for v in bufs],
        input_output_aliases={i: i for i in range(nb)},
        scratch_shapes=[pltpu.SemaphoreType.DMA((max(nlocal, 1),))],
        compiler_params=pltpu.CompilerParams(has_side_effects=EFFECT),
    )(*bufs, send_sems, recv_sems, after)
    return list(res)


def _place():
    x, y, c = lax.axis_index("x"), lax.axis_index("y"), lax.axis_index("c")
    others = [((1 - x, y, c), 2 * (1 - x) + y), ((x, 1 - y, c), 2 * x + 1 - y), ((1 - x, 1 - y, c), 2 * (1 - x) + 1 - y)]
    return 4 * x + 2 * y + c, 2 * x + y, c, (x, y, 1 - c), others


def _gather1_plan(n):
    def plan(refs):
        me, _, _, sibling, others = _place()
        out = []
        for wi in range(n):
            item, land = refs[wi], refs[n + wi]
            out.append((item, land.at[me], sibling, land.at[me + 1 - 2 * lax.axis_index("c")]))
            for peer, chip in others:
                out.append((item, land.at[me], peer, land.at[2 * chip + lax.axis_index("c")]))
        return out

    return plan


def _gather1_local(n):
    def plan(refs):
        me = _place()[0]
        return [(refs[wi], refs[n + wi].at[me]) for wi in range(n)]

    return plan


def _gather2_plan(n):
    def plan(refs):
        _, _, c, sibling, others = _place()
        out = []
        for wi in range(n):
            land = refs[wi]
            for _, chip in others:
                out.append((land.at[2 * chip + c], land.at[2 * chip + c], sibling, land.at[2 * chip + 1 - c]))
        return out

    return plan


def _gather_start(items, *, name):
    lands = [lax.empty((N_DEV,) + v.shape, v.dtype) for v in items]
    return _split_start(items + lands, 4 * len(items), _gather1_plan(len(items)), name=name)


def _gather_forward(started, after, *, name):
    n = len(started[2]) // 2
    bufs = _split_wait(started, after, _gather1_plan(n), _gather1_local(n), name=name + "_wait")
    return _split_start(bufs[n:], 3 * n, _gather2_plan(n), name=name + "_start")


def _gather_finish(started, after, *, name):
    n = len(started[2])
    return _split_wait(started, after, _gather2_plan(n), lambda refs: [], name=name)


def _handshake(peers):
    barrier = pltpu.get_barrier_semaphore()
    for peer in peers:
        pl.semaphore_signal(barrier, inc=1, device_id=peer, device_id_type=MESH)
    pl.semaphore_wait(barrier, len(peers))


def _remote(src, dst, send_sem, recv_sem, peer):
    return pltpu.make_async_remote_copy(src_ref=src, dst_ref=dst, send_sem=send_sem, recv_sem=recv_sem, device_id=peer,
                                        device_id_type=MESH)


def _sequencer_gather(items, *, collective_id, name):
    n = len(items)
    srcs = [jax.new_ref(v, memory_space=pltpu.MemorySpace.HBM) for v in items]
    lands = [jax.empty_ref(jax.ShapeDtypeStruct((N_DEV,) + v.shape, v.dtype), memory_space=pltpu.MemorySpace.HBM)
             for v in items]
    dma = pltpu.SemaphoreType.DMA

    @pl.kernel(mesh=plsc.ScalarSubcoreMesh(axis_name="sequencer", num_cores=1), name=name,
               scratch_types=(dma((4 * n,)), dma((4 * n,)), dma((3 * n,)), dma((3 * n,)), dma((n,))),
               compiler_params=pltpu.CompilerParams(collective_id=collective_id))
    def launch(send1, recv1, send2, recv2, local_sems):
        _, _, _, sibling, others = _place()
        _handshake([sibling] + [peer for peer, _ in others])
        hop1 = _gather1_plan(n)(srcs + lands)
        hop2 = _gather2_plan(n)(lands)
        local = [pltpu.make_async_copy(src, dst, local_sems.at[j])
                 for j, (src, dst) in enumerate(_gather1_local(n)(srcs + lands))]
        for cp in local:
            cp.start()
        for i, (src, dst, peer, _) in enumerate(hop1):
            _remote(src, dst, send1.at[i], recv1.at[i], peer).start()
        for wi in range(n):
            for j in range(3):
                i1, i2 = 4 * wi + 1 + j, 3 * wi + j
                src, _, peer, incoming = hop1[i1]
                _remote(src, incoming, send1.at[i1], recv1.at[i1], peer).wait_recv()
                src, dst, peer, _ = hop2[i2]
                _remote(src, dst, send2.at[i2], recv2.at[i2], peer).start()
        for wi in range(n):
            src, _, peer, incoming = hop1[4 * wi]
            _remote(src, incoming, send1.at[4 * wi], recv1.at[4 * wi], peer).wait_recv()
        for i, (src, _, peer, incoming) in enumerate(hop2):
            cp = _remote(src, incoming, send2.at[i], recv2.at[i], peer)
            cp.wait_send()
            cp.wait_recv()
        for i, (src, dst, peer, _) in enumerate(hop1):
            _remote(src, dst, send1.at[i], recv1.at[i], peer).wait_send()
        for cp in local:
            cp.wait()

    launch()
    return [land[...] for land in lands]


def _sequencer_exchange(sources, land_shapes, ncopies, plan, local_plan, peers, *, collective_id, name):
    srcs = [jax.new_ref(v, memory_space=pltpu.MemorySpace.HBM) for v in sources]
    lands = [jax.empty_ref(s, memory_space=pltpu.MemorySpace.HBM) for s in land_shapes]
    nlocal = len(local_plan(srcs + lands))
    dma = pltpu.SemaphoreType.DMA

    @pl.kernel(mesh=plsc.ScalarSubcoreMesh(axis_name="sequencer", num_cores=1), name=name,
               scratch_types=(dma((ncopies,)), dma((ncopies,)), dma((max(nlocal, 1),))),
               compiler_params=pltpu.CompilerParams(collective_id=collective_id))
    def launch(send_sems, recv_sems, local_sems):
        _handshake(peers(_place()))
        copies = plan(srcs + lands)
        local = [pltpu.make_async_copy(src, dst, local_sems.at[j])
                 for j, (src, dst) in enumerate(local_plan(srcs + lands))]
        for cp in local:
            cp.start()
        for i, (src, dst, peer, _) in enumerate(copies):
            _remote(src, dst, send_sems.at[i], recv_sems.at[i], peer).start()
        for i, (src, _, peer, incoming) in enumerate(copies):
            cp = _remote(src, incoming, send_sems.at[i], recv_sems.at[i], peer)
            cp.wait_send()
            cp.wait_recv()
        for cp in local:
            cp.wait()

    launch()
    return [land[...] for land in lands]


def _sequencer_scatter_hop1(parts, *, collective_id, name):
    n = len(parts)
    shapes = [jax.ShapeDtypeStruct((N_CHIP,) + v.shape[1:], v.dtype) for v in parts]
    return _sequencer_exchange(parts, shapes, N_CHIP * n, _scatter1_plan(n), lambda refs: [], lambda place: [place[3]],
                               collective_id=collective_id, name=name)


def _sequencer_scatter_hop2(sums, *, collective_id, name):
    n = len(sums)
    shapes = [jax.ShapeDtypeStruct(v.shape, v.dtype) for v in sums]
    return _sequencer_exchange(sums, shapes, 3 * n, _scatter2_plan(n), _scatter2_local(n),
                               lambda place: [peer for peer, _ in place[4]], collective_id=collective_id, name=name)


N_CHIP = N_DEV // 2


def _scatter1_plan(n):
    def plan(refs):
        _, _, c, sibling, _ = _place()
        out = []
        for wi in range(n):
            parts, half = refs[wi], refs[n + wi]
            for chip in range(N_CHIP):
                out.append((parts.at[2 * chip + 1 - c], half.at[chip], sibling, half.at[chip]))
        return out

    return plan


def _scatter2_plan(n):
    def plan(refs):
        _, my_chip, _, _, others = _place()
        out = []
        for wi in range(n):
            sums, recv = refs[wi], refs[n + wi]
            for peer, chip in others:
                out.append((sums.at[chip], recv.at[my_chip], peer, recv.at[chip]))
        return out

    return plan


def _scatter2_local(n):
    def plan(refs):
        my_chip = _place()[1]
        return [(refs[wi].at[my_chip], refs[n + wi].at[my_chip]) for wi in range(n)]

    return plan


def _pair_add(parts, half, core, *, name):
    _, r, c = parts.shape
    tr = max(d for d in range(HALO, 257, HALO) if r % d == 0) if r > 256 else r
    parts4 = parts.reshape(N_CHIP, 2, r, c)

    def body(core_ref, p_ref, h_ref, o_ref):
        o_ref[...] = (p_ref[:, 0].astype(F32) + h_ref[...].astype(F32)).astype(o_ref.dtype)

    return pl.pallas_call(
        body, name=name,
        grid_spec=pltpu.PrefetchScalarGridSpec(
            num_scalar_prefetch=1, grid=(r // tr,),
            in_specs=[pl.BlockSpec((N_CHIP, 1, tr, c), lambda i, core_ref: (0, core_ref[0], i, 0)),
                      pl.BlockSpec((N_CHIP, tr, c), lambda i, core_ref: (0, i, 0))],
            out_specs=pl.BlockSpec((N_CHIP, tr, c), lambda i, core_ref: (0, i, 0))),
        out_shape=jax.ShapeDtypeStruct((N_CHIP, r, c), parts.dtype), compiler_params=_cp("parallel"),
    )(core, parts4, half)


def _scatter_start(parts, *, name):
    halves = [lax.empty((N_CHIP,) + v.shape[1:], v.dtype) for v in parts]
    return _split_start(parts + halves, N_CHIP * len(parts), _scatter1_plan(len(parts)), name=name)


def _scatter_forward(started, after, core, *, name):
    n = len(started[2]) // 2
    bufs = _split_wait(started, after, _scatter1_plan(n), lambda refs: [], name=name + "_wait")
    sums = [_pair_add(bufs[wi], bufs[n + wi], core, name=name + "_add%d" % wi) for wi in range(n)]
    recvs = [lax.empty(v.shape, v.dtype) for v in sums]
    return _split_start(sums + recvs, 3 * n, _scatter2_plan(n), name=name + "_start")


def _scatter_finish(started, after, *, name):
    n = len(started[2]) // 2
    return _split_wait(started, after, _scatter2_plan(n), _scatter2_local(n), name=name)[n:]


def _adamw(parts, w, m, v, *, name):
    r, c = w.shape
    nparts = parts.shape[0]
    tr = max(d for d in range(HALO, 129, HALO) if r % d == 0) if r > 128 else r

    def body(p_ref, w_ref, m_ref, v_ref, g_ref, d_ref, mo_ref, vo_ref):
        g = p_ref[0].astype(F32)
        for k in range(1, nparts):
            g = g + p_ref[k].astype(F32)
        mn = ADAM_B1 * m_ref[...] + (1.0 - ADAM_B1) * g
        vn = ADAM_B2 * v_ref[...] + (1.0 - ADAM_B2) * (g * g)
        m_hat = mn / (1.0 - ADAM_B1 ** ADAM_STEP)
        v_hat = vn / (1.0 - ADAM_B2 ** ADAM_STEP)
        g_ref[...] = g
        d_ref[...] = -ADAM_LR * (m_hat / (jnp.sqrt(v_hat) + ADAM_EPS) + ADAM_WD * w_ref[...])
        mo_ref[...] = mn
        vo_ref[...] = vn

    blk = pl.BlockSpec((tr, c), lambda i: (i, 0))
    return pl.pallas_call(
        body, name=name, grid=(r // tr,), in_specs=[pl.BlockSpec((nparts, tr, c), lambda i: (0, i, 0)), blk, blk, blk],
        out_specs=[blk] * 4, out_shape=[jax.ShapeDtypeStruct((r, c), F32)] * 4, compiler_params=_cp("parallel"),
    )(parts, w, m, v)


def _pack_rows(vs, rows):
    lead = vs[0].shape[:-1] if vs[0].ndim > 1 else ()
    flat = jnp.concatenate(vs, axis=-1)
    pad = rows * LANES - flat.shape[-1]
    flat = jnp.pad(flat, [(0, 0)] * len(lead) + [(0, pad)])
    return flat.reshape(lead + (rows, LANES))


def kernel(x, p, positions, mix_norm_w, w_in, conv_w, conv_b, dt_bias, a_log, d_skip, ssd_norm_w, q_a_norm_w, w_q_b, kv_a_norm_w, w_kv_b, w_out, ffn_norm_w, w_ffn_up, ffn_conv_w, ffn_conv_b, w_ffn_down, ple_norm_w, w_ple_gate, b_ple_gate, w_ple_proj, ple_post_norm_w, final_norm_w, loss_target, m_mix_norm_w, m_w_in, m_conv_w, m_conv_b, m_dt_bias, m_a_log, m_d_skip, m_ssd_norm_w, m_q_a_norm_w, m_w_q_b, m_kv_a_norm_w, m_w_kv_b, m_w_out, m_ffn_norm_w, m_w_ffn_up, m_ffn_conv_w, m_ffn_conv_b, m_w_ffn_down, m_ple_norm_w, m_w_ple_gate, m_b_ple_gate, m_w_ple_proj, m_ple_post_norm_w, m_final_norm_w, v_mix_norm_w, v_w_in, v_conv_w, v_conv_b, v_dt_bias, v_a_log, v_d_skip, v_ssd_norm_w, v_q_a_norm_w, v_w_q_b, v_kv_a_norm_w, v_w_kv_b, v_w_out, v_ffn_norm_w, v_w_ffn_up, v_ffn_conv_w, v_ffn_conv_b, v_w_ffn_down, v_ple_norm_w, v_w_ple_gate, v_b_ple_gate, v_w_ple_proj, v_ple_post_norm_w, v_final_norm_w):
    given = dict(locals())
    shapes = {n: given[n].shape for n in WEIGHTS}
    w2 = {n: given[n].reshape(given[n].shape[-2:] if n in BIG or n in CONV else (1, -1)) for n in WEIGHTS}
    m2 = {n: given['m_' + n].reshape(w2[n].shape) for n in WEIGHTS}
    v2 = {n: given['v_' + n].reshape(w2[n].shape) for n in WEIGHTS}
    me = 4 * lax.axis_index("x") + 2 * lax.axis_index("y") + lax.axis_index("c")

    core = lax.axis_index("c").astype(jnp.int32).reshape(1)

    def shards(grp, zero):
        return [(w2[n] + zero).astype(BF16) if n in BIG else w2[n] + zero for n in WEIGHT_GROUPS[grp]]

    first, token = _gather_start(shards('a', 0.0), name="gather_a_hop1")
    first, token = _gather_forward(first, token, name="gather_a_hop2")
    zero = token[0, 0]
    later = _sequencer_gather(shards('b', zero) + shards('c', zero), collective_id=1, name="gather_later")
    later = dict(zip(WEIGHT_GROUPS['b'] + WEIGHT_GROUPS['c'], later))

    def get_w(grp, after):
        if grp == 'a':
            lands = dict(zip(WEIGHT_GROUPS[grp], _gather_finish(first, token, name="gather_a_done")))
        else:
            lands = {n: later[n] for n in WEIGHT_GROUPS[grp]}
        return _assemble_weights(lands)

    scatters = {}

    hop_ids = {grp: 2 + 2 * i for i, grp in enumerate(GRAD_GROUPS)}

    def zero_of(arrays):
        return sum(v.reshape(-1)[0].astype(F32) * 0.0 for v in arrays)

    def emit(grp, grads):
        parts = [grads[n] for n in GRAD_GROUPS[grp]]
        scatters[grp] = (parts, _sequencer_scatter_hop1(parts, collective_id=hop_ids[grp],
                                                        name="scatter_" + grp + "_hop1"))
        return zero_of(parts)

    def relay(grp, after):
        parts, halves = scatters[grp]
        sums = [_pair_add(mine, theirs, core, name="scatter_%s_add%d" % (grp, i))
                for i, (mine, theirs) in enumerate(zip(parts, halves))]
        scatters[grp] = _sequencer_scatter_hop2(sums, collective_id=hop_ids[grp] + 1, name="scatter_" + grp + "_hop2")
        return zero_of(sums)

    vecs = {n: w2[n] for n in REPL}
    vecs['mix_norm_w'] = vecs['mix_norm_w'] + zero
    loss, dx, g_conv, g_vec = _local_step(x[0], p[0, 0], _rope_tables(positions), get_w, vecs, loss_target[0], emit,
                                          relay)
    n_small = sum(g_vec[n].shape[1] for n in REPL) + sum(g_conv[n].size for n in CONV) + 1
    rows_small = -(-n_small // (LANES * HALO)) * HALO
    small = _pack_rows([g_vec[n] for n in REPL] + [g_conv[n].reshape(1, -1) for n in CONV] + [loss], rows_small)
    all_small = _exchange([small], gather=True, name="gather_small_grads")[0].reshape(N_DEV, rows_small * LANES)

    out_g, out_d, out_m, out_v = {}, {}, {}, {}
    for grp, names in GRAD_GROUPS.items():
        received = scatters[grp]
        for n, parts in zip(names, received):
            out_g[n], out_d[n], out_m[n], out_v[n] = _adamw(parts, w2[n], m2[n], v2[n], name="adamw_" + n)
    pieces, off = [], 0
    for n in REPL:
        k = g_vec[n].shape[1]
        pieces.append(all_small[:, off:off + k])
        off += k
    for n in CONV:
        kw, cols = g_conv[n].shape
        full = all_small[:, off:off + kw * cols].reshape(N_DEV, kw, cols)
        mine = lax.dynamic_slice_in_dim(full, me * (cols // N_DEV), cols // N_DEV, axis=2)
        pieces.append(mine.reshape(N_DEV, kw * (cols // N_DEV)))
        off += kw * cols
    pieces.append(all_small[:, off:off + 1])
    small_names = REPL + CONV
    n_mine = sum(q.shape[1] for q in pieces)
    rows_mine = -(-n_mine // (LANES * HALO)) * HALO
    zero = jnp.zeros((1, 1), F32)
    packed = [_pack_rows([src[n].reshape(1, -1) for n in small_names] + [zero], rows_mine).reshape(rows_mine, LANES)
              for src in (w2, m2, v2)]
    sg, sd, sm, sv = _adamw(_pack_rows(pieces, rows_mine), *packed, name="adamw_small")
    off = 0
    for n in small_names:
        k = w2[n].size
        for dst, src in ((out_g, sg), (out_d, sd), (out_m, sm), (out_v, sv)):
            dst[n] = src.reshape(-1)[off:off + k].reshape(w2[n].shape)
        off += k
    total_loss = sg.reshape(-1)[off]

    outs = [total_loss, dx[None]]
    for res in (out_g, out_d, out_m, out_v):
        outs += [res[n].reshape(shapes[n]) for n in WEIGHTS]
    return tuple(outs)
```

```python
import functools
import math

import numpy as np
import jax
import jax.numpy as jnp
from jax import lax
from jax.experimental import pallas as pl
from jax.experimental.pallas import tpu as pltpu
from jax.experimental.pallas import tpu_sc as plsc

F32 = jnp.float32
BF16 = jnp.bfloat16
HI = lax.Precision.HIGHEST

D_MODEL = 2048
CHUNK = 64
D_SSM = 1024
SSD_P = 64
SSD_HEADS = 16
SSD_GROUPS = 2
SSD_N = 128
SSD_CONV = 4
SSD_CONV_DIM = D_SSM + 2 * SSD_GROUPS * SSD_N
MLA_HEADS = 8
MLA_NOPE = 128
MLA_ROPE = 64
MLA_V = 128
MLA_Q_RANK = 512
MLA_KV_RANK = 256
MLA_QK_PAD = 256
ROPE_THETA = 10000.0
D_FF = 5632
FFN_CONV = 3
PLE_DIM = 256
NORM_EPS = 1e-6
ADAM_LR, ADAM_B1, ADAM_B2, ADAM_EPS, ADAM_WD, ADAM_STEP = 0.001, 0.9, 0.999, 1e-08, 0.01, 10
N_DEV = 8

OFF_Z, OFF_XBC, OFF_QA, OFF_CKV, OFF_KR, OFF_DT, D_IN_PAD = 0, 1024, 2560, 3072, 3328, 3456, 3584
D_IN = 3408
LANES = 128
HALO = 8
VMEM_LIMIT = 56 * 1024 * 1024
FFN_TC = D_FF * 2 // N_DEV
FFN_PERM = (0, 4, 1, 5, 2, 6, 3, 7)
NEG = -1e30


def _cp(*sem):
    return pltpu.CompilerParams(dimension_semantics=tuple(sem), vmem_limit_bytes=VMEM_LIMIT)


def _tile(n, want):
    if n <= want:
        return n
    best = max(d for d in range(LANES, want + 1, LANES) if n % d == 0)
    return best


def _sigmoid(x):
    return 1.0 / (1.0 + jnp.exp(-x))


def _silu(x):
    return x * _sigmoid(x)


def _dsilu(x):
    s = _sigmoid(x)
    return s * (1.0 + x * (1.0 - s))


MM_TILE = 1408
MM_TK = 2816


def _matmul(a, b, *, ta=False, tb=False, out_dtype=F32, add=None, bias=None, tm=MM_TILE, tn=MM_TILE, tk=MM_TK, name,
            mnk=None, a_spec=None, b_spec=None, o_spec=None, o_shape=None):
    if mnk is None:
        m, k = (a.shape[1], a.shape[0]) if ta else a.shape
        n = b.shape[0] if tb else b.shape[1]
        assert k == (b.shape[1] if tb else b.shape[0])
    else:
        m, n, k = mnk
    tm, tn, tk = _tile(m, tm), _tile(n, tn), _tile(k, tk)
    nk = k // tk
    dims = (((0 if ta else 1,), (1 if tb else 0,)), ((), ()))

    def body(*refs):
        a_ref, b_ref = refs[0], refs[1]
        pos = 2
        add_ref = bias_ref = None
        if add is not None:
            add_ref = refs[pos]
            pos += 1
        if bias is not None:
            bias_ref = refs[pos]
            pos += 1
        o_ref = refs[pos]
        kk = pl.program_id(2)
        av = a_ref[...]
        bv = b_ref[...]
        av = av.reshape(av.shape[-2:]).astype(BF16)
        bv = bv.reshape(bv.shape[-2:]).astype(BF16)
        prod = lax.dot_general(av, bv, dims, preferred_element_type=F32)

        def finish(r):
            if bias_ref is not None:
                r = r + bias_ref[...]
            if add_ref is not None:
                r = r + add_ref[...].astype(F32)
            o_ref[...] = r.astype(out_dtype).reshape(o_ref.shape)

        if nk == 1:
            finish(prod)
        else:
            acc_ref = refs[pos + 1]

            @pl.when(kk == 0)
            def _():
                acc_ref[...] = prod

            @pl.when(kk > 0)
            def _():
                acc_ref[...] += prod

            @pl.when(kk == nk - 1)
            def _():
                finish(acc_ref[...])

    if a_spec is None:
        a_spec = (pl.BlockSpec((tk, tm), lambda i, j, kk: (kk, i)) if ta
                  else pl.BlockSpec((tm, tk), lambda i, j, kk: (i, kk)))
    if b_spec is None:
        b_spec = (pl.BlockSpec((tn, tk), lambda i, j, kk: (j, kk)) if tb
                  else pl.BlockSpec((tk, tn), lambda i, j, kk: (kk, j)))
    if o_spec is None:
        o_spec = pl.BlockSpec((tm, tn), lambda i, j, kk: (i, j))
    if o_shape is None:
        o_shape = (m, n)
    in_specs = [a_spec, b_spec]
    args = [a, b]
    if add is not None:
        in_specs.append(pl.BlockSpec((tm, tn), lambda i, j, kk: (i, j)))
        args.append(add)
    if bias is not None:
        in_specs.append(pl.BlockSpec((1, tn), lambda i, j, kk: (0, j)))
        args.append(bias)
    return pl.pallas_call(
        body, name=name, grid=(m // tm, n // tn, nk), in_specs=in_specs, out_specs=o_spec,
        out_shape=jax.ShapeDtypeStruct(o_shape, out_dtype),
        scratch_shapes=[pltpu.VMEM((tm, tn), F32)] if nk > 1 else [],
        compiler_params=_cp("parallel", "parallel", "arbitrary"),
    )(*args)


def _rmsnorm_fwd(x, w, *, width, cblk=0, out_dtype=BF16, tr=256, name):
    t = x.shape[0]

    def body(x_ref, w_ref, o_ref):
        xv = x_ref[...].astype(F32)
        r = lax.rsqrt(jnp.mean(xv * xv, axis=-1, keepdims=True) + NORM_EPS)
        o_ref[...] = (xv * r * w_ref[...]).astype(out_dtype)

    return pl.pallas_call(
        body, name=name, grid=(t // tr,),
        in_specs=[pl.BlockSpec((tr, width), lambda i: (i, cblk)), pl.BlockSpec((1, width), lambda i: (0, 0))],
        out_specs=pl.BlockSpec((tr, width), lambda i: (i, 0)),
        out_shape=jax.ShapeDtypeStruct((t, width), out_dtype),
        compiler_params=_cp("parallel"),
    )(x, w)


def _rmsnorm_bwd(x, w, dy, add=None, *, width, cblk=0, out_dtype=F32, tr=256, name):
    t = x.shape[0]

    def body(*refs):
        if add is None:
            x_ref, w_ref, dy_ref, dx_ref, dw_ref = refs
            add_ref = None
        else:
            x_ref, w_ref, dy_ref, add_ref, dx_ref, dw_ref = refs
        xv = x_ref[...].astype(F32)
        dyv = dy_ref[...].astype(F32)
        r = lax.rsqrt(jnp.mean(xv * xv, axis=-1, keepdims=True) + NORM_EPS)
        xh = xv * r
        g = dyv * w_ref[...]
        dx = r * (g - xh * jnp.mean(g * xh, axis=-1, keepdims=True))
        if add_ref is not None:
            dx = dx + add_ref[...].astype(F32)
        dx_ref[...] = dx.astype(out_dtype)

        @pl.when(pl.program_id(0) == 0)
        def _():
            dw_ref[...] = jnp.zeros_like(dw_ref)

        dw_ref[...] += jnp.sum(dyv * xh, axis=0, keepdims=True)

    in_specs = [pl.BlockSpec((tr, width), lambda i: (i, cblk)), pl.BlockSpec((1, width), lambda i: (0, 0)),
                pl.BlockSpec((tr, width), lambda i: (i, 0))]
    args = [x, w, dy]
    if add is not None:
        in_specs.append(pl.BlockSpec((tr, width), lambda i: (i, 0)))
        args.append(add)
    return pl.pallas_call(
        body, name=name, grid=(t // tr,), in_specs=in_specs,
        out_specs=[pl.BlockSpec((tr, width), lambda i: (i, 0)), pl.BlockSpec((1, width), lambda i: (0, 0))],
        out_shape=[jax.ShapeDtypeStruct((t, width), out_dtype), jax.ShapeDtypeStruct((1, width), F32)],
        compiler_params=_cp("arbitrary"),
    )(*args)


def _shift_down(prev_halo, cur, j):
    if j == 0:
        return cur
    ext = jnp.concatenate([prev_halo, cur], axis=0)
    return pltpu.roll(ext, j, axis=0)[HALO:]


def _shift_up(cur, next_halo, j):
    if j == 0:
        return cur
    ext = jnp.concatenate([cur, next_halo], axis=0)
    return pltpu.roll(ext, ext.shape[0] - j, axis=0)[:cur.shape[0]]


def _conv_rows(prev, cur, w, b, kw):
    shifted = [cur]
    out = b + w[kw - 1:kw] * cur
    for j in range(1, kw):
        sh = _shift_down(prev, cur, j)
        shifted.append(sh)
        out = out + w[kw - 1 - j:kw - j] * sh
    return out, shifted


def _act_fwd(c, glu):
    if glu:
        half = c.shape[1] // 2
        return _silu(c[:, :half]) * c[:, half:]
    return _silu(c)


def _act_bwd(c, dout, glu):
    if glu:
        half = c.shape[1] // 2
        g, up = c[:, :half], c[:, half:]
        return jnp.concatenate([dout * up * _dsilu(g), dout * _silu(g)], axis=1)
    return dout * _dsilu(c)


def _conv_act_fwd(u, w, b, *, kw, glu, tc, coff, ncols, out_dtype, tr=256, name):
    t = u.shape[0]
    nb = ncols // tc
    oc = tc // 2 if glu else tc

    def body(u_ref, uh_ref, w_ref, b_ref, o_ref):
        prev = jnp.where(pl.program_id(0) == 0, 0.0, uh_ref[...])
        c, _ = _conv_rows(prev, u_ref[...], w_ref[...], b_ref[...], kw)
        o_ref[...] = _act_fwd(c, glu).astype(out_dtype)

    return pl.pallas_call(
        body, name=name, grid=(t // tr, nb),
        in_specs=[pl.BlockSpec((tr, tc), lambda i, j: (i, j + coff)),
                  pl.BlockSpec((HALO, tc), lambda i, j: (jnp.maximum(i * (tr // HALO) - 1, 0), j + coff)),
                  pl.BlockSpec((kw, tc), lambda i, j: (0, j)), pl.BlockSpec((1, tc), lambda i, j: (0, j))],
        out_specs=pl.BlockSpec((tr, oc), lambda i, j: (i, j)),
        out_shape=jax.ShapeDtypeStruct((t, nb * oc), out_dtype),
        compiler_params=_cp("parallel", "parallel"),
    )(u, u, w, b)


def _conv_act_bwd(u, w, b, dout, *, kw, glu, tc, coff, ncols, tr=256, name):
    t = u.shape[0]
    nb = ncols // tc
    nt = t // tr
    oc = tc // 2 if glu else tc

    def body(u_ref, up_ref, un_ref, d_ref, dn_ref, w_ref, b_ref, du_ref, dw_ref, db_ref):
        i = pl.program_id(1)
        cur, nxt, wv, bv = u_ref[...], un_ref[...], w_ref[...], b_ref[...]
        prev = jnp.where(i == 0, 0.0, up_ref[...])
        c_cur, shifted = _conv_rows(prev, cur, wv, bv, kw)
        c_nxt, _ = _conv_rows(cur[tr - HALO:], nxt, wv, bv, kw)
        d_cur = _act_bwd(c_cur, d_ref[...].astype(F32), glu)
        d_nxt = _act_bwd(c_nxt, jnp.where(i == nt - 1, 0.0, dn_ref[...].astype(F32)), glu)
        du = wv[kw - 1:kw] * d_cur
        for j in range(1, kw):
            du = du + wv[kw - 1 - j:kw - j] * _shift_up(d_cur, d_nxt, j)
        du_ref[...] = du.astype(BF16)

        @pl.when(i == 0)
        def _():
            dw_ref[...] = jnp.zeros_like(dw_ref)
            db_ref[...] = jnp.zeros_like(db_ref)

        db_ref[...] += jnp.sum(d_cur, axis=0, keepdims=True)
        dw_ref[...] += jnp.concatenate(
            [jnp.sum(d_cur * shifted[kw - 1 - k], axis=0, keepdims=True) for k in range(kw)], axis=0)

    nh = tr // HALO
    return pl.pallas_call(
        body, name=name, grid=(nb, nt),
        in_specs=[pl.BlockSpec((tr, tc), lambda j, i: (i, j + coff)),
                  pl.BlockSpec((HALO, tc), lambda j, i: (jnp.maximum(i * nh - 1, 0), j + coff)),
                  pl.BlockSpec((HALO, tc), lambda j, i: (jnp.minimum((i + 1) * nh, t // HALO - 1), j + coff)),
                  pl.BlockSpec((tr, oc), lambda j, i: (i, j)),
                  pl.BlockSpec((HALO, oc), lambda j, i: (jnp.minimum((i + 1) * nh, t // HALO - 1), j)),
                  pl.BlockSpec((kw, tc), lambda j, i: (0, j)), pl.BlockSpec((1, tc), lambda j, i: (0, j))],
        out_specs=[pl.BlockSpec((tr, tc), lambda j, i: (i, j)), pl.BlockSpec((kw, tc), lambda j, i: (0, j)),
                   pl.BlockSpec((1, tc), lambda j, i: (0, j))],
        out_shape=[jax.ShapeDtypeStruct((t, ncols), BF16), jax.ShapeDtypeStruct((kw, ncols), F32),
                   jax.ShapeDtypeStruct((1, ncols), F32)],
        compiler_params=_cp("parallel", "arbitrary"),
    )(u, u, u, dout, dout, w, b)


def _ple_fwd(x2, gl, pe, pw, *, tr=256, name):
    t, d = x2.shape

    def body(x_ref, gl_ref, pe_ref, pw_ref, o_ref):
        pv = pe_ref[...]
        r = lax.rsqrt(jnp.mean(pv * pv, axis=-1, keepdims=True) + NORM_EPS)
        o_ref[...] = x_ref[...] + _sigmoid(gl_ref[...]) * (pv * r * pw_ref[...])

    blk = pl.BlockSpec((tr, d), lambda i: (i, 0))
    return pl.pallas_call(
        body, name=name, grid=(t // tr,), in_specs=[blk, blk, blk, pl.BlockSpec((1, d), lambda i: (0, 0))],
        out_specs=blk, out_shape=jax.ShapeDtypeStruct((t, d), F32), compiler_params=_cp("parallel"),
    )(x2, gl, pe, pw)


def _ple_bwd(dx3, gl, pe, pw, *, tr=256, name):
    t, d = dx3.shape

    def body(dx_ref, gl_ref, pe_ref, pw_ref, dgl_ref, db_ref, dpe_ref, dpw_ref):
        dx, pv, pwv = dx_ref[...], pe_ref[...], pw_ref[...]
        gate = _sigmoid(gl_ref[...])
        r = lax.rsqrt(jnp.mean(pv * pv, axis=-1, keepdims=True) + NORM_EPS)
        ph = pv * r
        dgl = dx * (ph * pwv) * gate * (1.0 - gate)
        de = dx * gate
        g = de * pwv
        dgl_ref[...] = dgl.astype(BF16)
        dpe_ref[...] = (r * (g - ph * jnp.mean(g * ph, axis=-1, keepdims=True))).astype(BF16)

        @pl.when(pl.program_id(0) == 0)
        def _():
            db_ref[...] = jnp.zeros_like(db_ref)
            dpw_ref[...] = jnp.zeros_like(dpw_ref)

        db_ref[...] += jnp.sum(dgl, axis=0, keepdims=True)
        dpw_ref[...] += jnp.sum(de * ph, axis=0, keepdims=True)

    blk = pl.BlockSpec((tr, d), lambda i: (i, 0))
    row = pl.BlockSpec((1, d), lambda i: (0, 0))
    return pl.pallas_call(
        body, name=name, grid=(t // tr,), in_specs=[blk, blk, blk, row], out_specs=[blk, row, blk, row],
        out_shape=[jax.ShapeDtypeStruct((t, d), BF16), jax.ShapeDtypeStruct((1, d), F32),
                   jax.ShapeDtypeStruct((t, d), BF16), jax.ShapeDtypeStruct((1, d), F32)],
        compiler_params=_cp("arbitrary"),
    )(dx3, gl, pe, pw)


def _loss_head(x3, fw, target, *, tr=256, name):
    t, d = x3.shape

    def body(x_ref, w_ref, t_ref, l_ref, dx_ref, dw_ref):
        xv, wv = x_ref[...], w_ref[...]
        r = lax.rsqrt(jnp.mean(xv * xv, axis=-1, keepdims=True) + NORM_EPS)
        xh = xv * r
        err = xh * wv - t_ref[...]
        dy = err * (1.0 / d)
        g = dy * wv
        dx_ref[...] = r * (g - xh * jnp.mean(g * xh, axis=-1, keepdims=True))

        @pl.when(pl.program_id(0) == 0)
        def _():
            l_ref[...] = jnp.zeros_like(l_ref)
            dw_ref[...] = jnp.zeros_like(dw_ref)

        l_ref[...] += 0.5 * jnp.sum(jnp.mean(err * err, axis=-1, keepdims=True), axis=0, keepdims=True)
        dw_ref[...] += jnp.sum(dy * xh, axis=0, keepdims=True)

    blk = pl.BlockSpec((tr, d), lambda i: (i, 0))
    row = pl.BlockSpec((1, d), lambda i: (0, 0))
    return pl.pallas_call(
        body, name=name, grid=(t // tr,), in_specs=[blk, row, blk],
        out_specs=[pl.BlockSpec((1, 1), lambda i: (0, 0)), blk, row],
        out_shape=[jax.ShapeDtypeStruct((1, 1), F32), jax.ShapeDtypeStruct((t, d), F32),
                   jax.ShapeDtypeStruct((1, d), F32)],
        compiler_params=_cp("arbitrary"),
    )(x3, fw, target)


def _rope(blk, tab_ref):
    return blk * tab_ref[0] + pltpu.roll(blk, 96, axis=1) * tab_ref[1] + pltpu.roll(blk, 32, axis=1) * tab_ref[2]


def _unrope(g, tab_ref):
    return g * tab_ref[0] + pltpu.roll(g * tab_ref[1], 32, axis=1) + pltpu.roll(g * tab_ref[2], 96, axis=1)


def _mla_prep(q, kv, proj, tabs, *, tr=512, name):
    t = q.shape[0]

    def body(q_ref, kv_ref, kr_ref, tab_ref, qo_ref, ko_ref, vo_ref):
        qv, kvv = q_ref[...], kv_ref[...]
        qo_ref[0, :, :MLA_NOPE] = qv[:, :MLA_NOPE].astype(BF16)
        qo_ref[0, :, MLA_NOPE:] = _rope(qv[:, MLA_NOPE:], tab_ref).astype(BF16)
        ko_ref[0, :, :MLA_NOPE] = kvv[:, :MLA_NOPE].astype(BF16)
        ko_ref[0, :, MLA_NOPE:] = _rope(kr_ref[...], tab_ref).astype(BF16)
        vo_ref[0] = kvv[:, MLA_NOPE:].astype(BF16)

    return pl.pallas_call(
        body, name=name, grid=(t // tr, MLA_HEADS),
        in_specs=[pl.BlockSpec((tr, MLA_QK_PAD), lambda i, h: (i, h)),
                  pl.BlockSpec((tr, MLA_NOPE + MLA_V), lambda i, h: (i, h)),
                  pl.BlockSpec((tr, LANES), lambda i, h: (i, OFF_KR // LANES)),
                  pl.BlockSpec((3, tr, LANES), lambda i, h: (0, i, 0))],
        out_specs=[pl.BlockSpec((1, tr, MLA_QK_PAD), lambda i, h: (h, i, 0)),
                   pl.BlockSpec((1, tr, MLA_QK_PAD), lambda i, h: (h, i, 0)),
                   pl.BlockSpec((1, tr, MLA_V), lambda i, h: (h, i, 0))],
        out_shape=[jax.ShapeDtypeStruct((MLA_HEADS, t, MLA_QK_PAD), BF16),
                   jax.ShapeDtypeStruct((MLA_HEADS, t, MLA_QK_PAD), BF16),
                   jax.ShapeDtypeStruct((MLA_HEADS, t, MLA_V), BF16)],
        compiler_params=_cp("parallel", "parallel"),
    )(q, kv, proj, tabs)


def _mla_unprep(dq3, dk3, dv3, tabs, *, tr=256, name):
    t = dq3.shape[1]

    def body(dq_ref, dk_ref, dv_ref, tab_ref, qo_ref, kvo_ref, kro_ref):
        kr = jnp.zeros((tr, LANES), F32)
        for h in range(MLA_HEADS):
            c0 = h * MLA_QK_PAD
            qo_ref[:, c0:c0 + MLA_NOPE] = dq_ref[h, :, :MLA_NOPE].astype(BF16)
            qo_ref[:, c0 + MLA_NOPE:c0 + MLA_QK_PAD] = _unrope(dq_ref[h, :, MLA_NOPE:], tab_ref).astype(BF16)
            kvo_ref[:, c0:c0 + MLA_NOPE] = dk_ref[h, :, :MLA_NOPE].astype(BF16)
            kvo_ref[:, c0 + MLA_NOPE:c0 + MLA_QK_PAD] = dv_ref[h].astype(BF16)
            kr = kr + dk_ref[h, :, MLA_NOPE:]
        kro_ref[...] = _unrope(kr, tab_ref).astype(BF16)

    return pl.pallas_call(
        body, name=name, grid=(t // tr,),
        in_specs=[pl.BlockSpec((MLA_HEADS, tr, MLA_QK_PAD), lambda i: (0, i, 0)),
                  pl.BlockSpec((MLA_HEADS, tr, MLA_QK_PAD), lambda i: (0, i, 0)),
                  pl.BlockSpec((MLA_HEADS, tr, MLA_V), lambda i: (0, i, 0)),
                  pl.BlockSpec((3, tr, LANES), lambda i: (0, i, 0))],
        out_specs=[pl.BlockSpec((tr, MLA_HEADS * MLA_QK_PAD), lambda i: (i, 0)),
                   pl.BlockSpec((tr, MLA_HEADS * MLA_QK_PAD), lambda i: (i, 0)),
                   pl.BlockSpec((tr, LANES), lambda i: (i, 0))],
        out_shape=[jax.ShapeDtypeStruct((t, MLA_HEADS * MLA_QK_PAD), BF16),
                   jax.ShapeDtypeStruct((t, MLA_HEADS * MLA_QK_PAD), BF16),
                   jax.ShapeDtypeStruct((t, LANES), BF16)],
        compiler_params=_cp("parallel"),
    )(dq3, dk3, dv3, tabs)


ATT_BLK = 256
ATT_SCALE = 1.0 / math.sqrt(MLA_NOPE + MLA_ROPE)
_NT = (((1,), (1,)), ((), ()))
_TN = (((0,), (0,)), ((), ()))


def _att_scores(q, k, diagonal):
    s = lax.dot_general(q, k, _NT, preferred_element_type=F32) * ATT_SCALE
    if not diagonal:
        return s
    row = lax.broadcasted_iota(jnp.int32, s.shape, 0)
    col = lax.broadcasted_iota(jnp.int32, s.shape, 1)
    return jnp.where((col >> 6) <= (row >> 6), s, NEG)


def _att_rows(i):
    return pl.ds(pl.multiple_of(i * ATT_BLK, ATT_BLK), ATT_BLK)


def _attn_fwd(q3, k3, v3, *, name):
    t = q3.shape[1]
    nq = t // ATT_BLK

    def body(q_ref, k_ref, v_ref, o_ref, lse_ref):
        qi = pl.program_id(1)
        q = q_ref[0]

        def step(j, carry, diagonal=False):
            m, l, acc = carry
            s = _att_scores(q, k_ref[0, _att_rows(j), :], diagonal)
            m_new = jnp.maximum(m, jnp.max(s, axis=-1, keepdims=True))
            p = jnp.exp(s - m_new)
            alpha = jnp.exp(m - m_new)
            l = alpha * l + jnp.sum(p, axis=-1, keepdims=True)
            acc = alpha * acc + jnp.dot(p.astype(BF16), v_ref[0, _att_rows(j), :], preferred_element_type=F32)
            return m_new, l, acc

        init = (jnp.full((ATT_BLK, 1), NEG, F32), jnp.zeros((ATT_BLK, 1), F32), jnp.zeros((ATT_BLK, MLA_V), F32))
        m, l, acc = step(qi, lax.fori_loop(0, qi, step, init), diagonal=True)
        o_ref[...] = acc / l
        lse_ref[0] = m + jnp.log(l)

    return pl.pallas_call(
        body, name=name, grid=(MLA_HEADS, nq),
        in_specs=[pl.BlockSpec((1, ATT_BLK, MLA_QK_PAD), lambda h, i: (h, i, 0)),
                  pl.BlockSpec((1, t, MLA_QK_PAD), lambda h, i: (h, 0, 0)),
                  pl.BlockSpec((1, t, MLA_V), lambda h, i: (h, 0, 0))],
        out_specs=[pl.BlockSpec((ATT_BLK, MLA_V), lambda h, i: (i, h)),
                   pl.BlockSpec((1, ATT_BLK, 1), lambda h, i: (h, i, 0))],
        out_shape=[jax.ShapeDtypeStruct((t, MLA_HEADS * MLA_V), F32), jax.ShapeDtypeStruct((MLA_HEADS, t, 1), F32)],
        compiler_params=_cp("parallel", "parallel"),
    )(q3, k3, v3)


def _attn_bwd(q3, k3, v3, o, dcat, lse, *, name):
    t = q3.shape[1]
    nq = t // ATT_BLK

    def body(q_ref, k_ref, v_ref, o_ref, do_ref, lse_ref, dq_ref, dk_ref, dv_ref, delta_ref):
        kj = pl.program_id(1)
        k, v = k_ref[0], v_ref[0]

        @pl.when(kj == 0)
        def _():
            dq_ref[...] = jnp.zeros_like(dq_ref)
            delta_ref[...] = jnp.sum(o_ref[...] * do_ref[...], axis=-1, keepdims=True)

        def step(i, carry, diagonal=False):
            dk, dv = carry
            rows = _att_rows(i)
            q = q_ref[0, rows, :]
            dob = do_ref[rows, :].astype(BF16)
            p = jnp.exp(_att_scores(q, k, diagonal) - lse_ref[0, rows, :])
            dv = dv + lax.dot_general(p.astype(BF16), dob, _TN, preferred_element_type=F32)
            dp = lax.dot_general(dob, v, _NT, preferred_element_type=F32)
            ds = (p * (dp - delta_ref[rows, :]) * ATT_SCALE).astype(BF16)
            dk = dk + lax.dot_general(ds, q, _TN, preferred_element_type=F32)
            dq_ref[0, rows, :] += jnp.dot(ds, k, preferred_element_type=F32)
            return dk, dv

        init = (jnp.zeros((ATT_BLK, MLA_QK_PAD), F32), jnp.zeros((ATT_BLK, MLA_V), F32))
        dk, dv = lax.fori_loop(kj + 1, nq, step, step(kj, init, diagonal=True))
        dk_ref[0] = dk
        dv_ref[0] = dv

    return pl.pallas_call(
        body, name=name, grid=(MLA_HEADS, nq),
        in_specs=[pl.BlockSpec((1, t, MLA_QK_PAD), lambda h, j: (h, 0, 0)),
                  pl.BlockSpec((1, ATT_BLK, MLA_QK_PAD), lambda h, j: (h, j, 0)),
                  pl.BlockSpec((1, ATT_BLK, MLA_V), lambda h, j: (h, j, 0)),
                  pl.BlockSpec((t, MLA_V), lambda h, j: (0, h)),
                  pl.BlockSpec((t, MLA_V), lambda h, j: (0, MLA_HEADS + h)),
                  pl.BlockSpec((1, t, 1), lambda h, j: (h, 0, 0))],
        out_specs=[pl.BlockSpec((1, t, MLA_QK_PAD), lambda h, j: (h, 0, 0)),
                   pl.BlockSpec((1, ATT_BLK, MLA_QK_PAD), lambda h, j: (h, j, 0)),
                   pl.BlockSpec((1, ATT_BLK, MLA_V), lambda h, j: (h, j, 0))],
        out_shape=[jax.ShapeDtypeStruct((MLA_HEADS, t, MLA_QK_PAD), F32),
                   jax.ShapeDtypeStruct((MLA_HEADS, t, MLA_QK_PAD), F32),
                   jax.ShapeDtypeStruct((MLA_HEADS, t, MLA_V), F32)],
        scratch_shapes=[pltpu.VMEM((t, 1), F32)],
        compiler_params=_cp("parallel", "arbitrary"),
    )(q3, k3, v3, o, dcat, lse)


def _ssd_prep(proj, bias128, alog128, *, name):
    t = proj.shape[0]
    nc = t // CHUNK

    def body(raw_ref, b_ref, al_ref, dt_ref, cs_ref, a_ref):
        xv = raw_ref[...] + b_ref[...]
        dt = jnp.maximum(xv, 0.0) + jnp.log(1.0 + jnp.exp(-jnp.abs(xv)))
        a = -jnp.exp(al_ref[...])
        adt = (dt * a).reshape(nc, CHUNK, LANES)
        li = lax.broadcasted_iota(jnp.int32, (nc, CHUNK, CHUNK), 1)
        si = lax.broadcasted_iota(jnp.int32, (nc, CHUNK, CHUNK), 2)
        tril = jnp.where(si <= li, 1.0, 0.0).astype(F32)
        cs = lax.dot_general(tril, adt, (((2,), (1,)), ((0,), (0,))), precision=HI, preferred_element_type=F32)
        dt_ref[...] = dt
        cs_ref[...] = cs.reshape(t, LANES)
        a_ref[...] = a

    blk = pl.BlockSpec((t, LANES), lambda i: (0, 0))
    row = pl.BlockSpec((1, LANES), lambda i: (0, 0))
    return pl.pallas_call(
        body, name=name, grid=(1,),
        in_specs=[pl.BlockSpec((t, LANES), lambda i: (0, OFF_DT // LANES)), row, row],
        out_specs=[blk, blk, row],
        out_shape=[jax.ShapeDtypeStruct((t, LANES), F32), jax.ShapeDtypeStruct((t, LANES), F32),
                   jax.ShapeDtypeStruct((1, LANES), F32)],
        compiler_params=_cp("arbitrary"),
    )(proj, bias128, alog128)


def _ssd_prep_bwd(ddt128, dadt128, proj, bias128, dt128, a128, dd_h, *, name):
    t = proj.shape[0]

    def body(ddt_ref, dadt_ref, raw_ref, b_ref, dt_ref, a_ref, dd_ref, draw_ref, db_ref, dal_ref, dds_ref):
        draw = ddt_ref[...] * _sigmoid(raw_ref[...] + b_ref[...])
        draw_ref[...] = draw.astype(BF16)
        db_ref[...] = jnp.sum(draw, axis=0, keepdims=True)
        dal_ref[...] = jnp.sum(dadt_ref[...] * dt_ref[...], axis=0, keepdims=True) * a_ref[...]
        dds_ref[...] = jnp.sum(dd_ref[...], axis=-1, keepdims=True)

    blk = pl.BlockSpec((t, LANES), lambda i: (0, 0))
    row = pl.BlockSpec((1, LANES), lambda i: (0, 0))
    return pl.pallas_call(
        body, name=name, grid=(1,),
        in_specs=[blk, blk, pl.BlockSpec((t, LANES), lambda i: (0, OFF_DT // LANES)), row, blk, row,
                  pl.BlockSpec((SSD_HEADS, SSD_P), lambda i: (0, 0))],
        out_specs=[blk, row, row, pl.BlockSpec((SSD_HEADS, 1), lambda i: (0, 0))],
        out_shape=[jax.ShapeDtypeStruct((t, LANES), BF16), jax.ShapeDtypeStruct((1, LANES), F32),
                   jax.ShapeDtypeStruct((1, LANES), F32), jax.ShapeDtypeStruct((SSD_HEADS, 1), F32)],
        compiler_params=_cp("arbitrary"),
    )(ddt128, dadt128, proj, bias128, dt128, a128, dd_h)


def _bdot(a, b, ca, cb, precision=None):
    return lax.dot_general(a, b, (((ca,), (cb,)), ((0,), (0,))), precision=precision, preferred_element_type=F32)


def _ssd_common(xs_ref, dt_ref, cs_ref, csr_ref, b_ref, c_ref, nc):
    x = xs_ref[0].reshape(nc, CHUNK, SSD_P)
    dt = dt_ref[0].reshape(nc, CHUNK, SSD_P)
    cs = cs_ref[0].reshape(nc, CHUNK, SSD_P)
    csr = csr_ref[0]
    bm = b_ref[0].reshape(nc, CHUNK, SSD_N).astype(BF16)
    cm = c_ref[0].reshape(nc, CHUNK, SSD_N).astype(BF16)
    li = lax.broadcasted_iota(jnp.int32, (nc, CHUNK, CHUNK), 1)
    si = lax.broadcasted_iota(jnp.int32, (nc, CHUNK, CHUNK), 2)
    lmat = jnp.exp(jnp.where(si <= li, cs - csr, NEG))
    g = _bdot(cm, bm, 2, 2)
    cs_last = jnp.sum(jnp.where(li == CHUNK - 1, cs, 0.0), axis=1, keepdims=True)
    xdt = x * dt
    dec = jnp.exp(cs_last - cs)
    return x, dt, cs, bm, cm, li, si, lmat, g, cs_last, xdt, dec


def _ssd_fwd(xs_h, dt_h, cs_h, cs_row, b_g, c_g, dskip_h, *, name):
    t = xs_h.shape[1]
    nc = t // CHUNK
    hpg = SSD_HEADS // SSD_GROUPS

    def body(xs_ref, dt_ref, cs_ref, csr_ref, b_ref, c_ref, dk_ref, y_ref, st_ref, sc_ref, cd_ref):
        x, dt, cs, bm, cm, li, si, lmat, g, cs_last, xdt, dec = _ssd_common(xs_ref, dt_ref, cs_ref, csr_ref, b_ref,
                                                                           c_ref, nc)
        yd = _bdot((g * lmat).astype(BF16), xdt.astype(BF16), 2, 1)
        sc_ref[...] = _bdot(bm, (dec * xdt).astype(BF16), 1, 1)
        cd_ref[...] = jnp.exp(cs_last)

        def step(c, s):
            st_ref[0, c] = s
            return s * cd_ref[c] + sc_ref[c]

        lax.fori_loop(0, nc, step, jnp.zeros((SSD_N, SSD_P), F32))
        yo = _bdot(cm, st_ref[0].astype(BF16), 2, 1) * jnp.exp(cs)
        y_ref[0] = (yd + yo + dk_ref[0] * x).reshape(t, SSD_P)

    head = pl.BlockSpec((1, t, SSD_P), lambda h: (h, 0, 0))
    grp = pl.BlockSpec((1, t, SSD_N), lambda h: (h // hpg, 0, 0))
    return pl.pallas_call(
        body, name=name, grid=(SSD_HEADS,),
        in_specs=[head, head, head, pl.BlockSpec((1, nc, 1, CHUNK), lambda h: (h, 0, 0, 0)), grp, grp,
                  pl.BlockSpec((1, 1, SSD_P), lambda h: (h, 0, 0))],
        out_specs=[head, pl.BlockSpec((1, nc, SSD_N, SSD_P), lambda h: (h, 0, 0, 0))],
        out_shape=[jax.ShapeDtypeStruct((SSD_HEADS, t, SSD_P), F32),
                   jax.ShapeDtypeStruct((SSD_HEADS, nc, SSD_N, SSD_P), F32)],
        scratch_shapes=[pltpu.VMEM((nc, SSD_N, SSD_P), F32), pltpu.VMEM((nc, 1, SSD_P), F32)],
        compiler_params=_cp("parallel"),
    )(xs_h, dt_h, cs_h, cs_row, b_g, c_g, dskip_h)


def _ssd_bwd(xs_h, dt_h, cs_h, cs_row, b_g, c_g, dskip_h, a_h, states, dy_h, *, name):
    t = xs_h.shape[1]
    nc = t // CHUNK
    hpg = SSD_HEADS // SSD_GROUPS

    def body(xs_ref, dt_ref, cs_ref, csr_ref, b_ref, c_ref, dk_ref, a_ref, st_ref, dy_ref,
             dxs_ref, ddt_ref, dadt_ref, db_ref, dc_ref, dd_ref, dsl_ref, dsc_ref, cd_ref):
        x, dt, cs, bm, cm, li, si, lmat, g, cs_last, xdt, dec = _ssd_common(xs_ref, dt_ref, cs_ref, csr_ref, b_ref,
                                                                           c_ref, nc)
        dy = dy_ref[0].reshape(nc, CHUNK, SSD_P)
        dyb = dy.astype(BF16)
        xdtb = xdt.astype(BF16)
        sprev = st_ref[0]
        sprevb = sprev.astype(BF16)
        cdec = jnp.exp(cs_last)
        ecs = jnp.exp(cs)
        dw = (ecs * dy).astype(BF16)
        wmat = _bdot(cm, sprevb, 2, 1)
        dcs = jnp.sum(dy * ecs * wmat, axis=2, keepdims=True)
        dcm = _bdot(dw, sprevb, 2, 2)
        dsl_ref[...] = _bdot(cm, dw, 1, 1)
        cd_ref[...] = cdec

        def step(k, ds):
            c = nc - 1 - k
            dsc_ref[c] = ds
            return ds * cd_ref[c] + dsl_ref[c]

        lax.fori_loop(0, nc, step, jnp.zeros((SSD_N, SSD_P), F32))
        dsc = dsc_ref[...]
        dscb = dsc.astype(BF16)
        d_last = jnp.sum(jnp.sum(dsc * sprev, axis=1, keepdims=True) * cdec, axis=2, keepdims=True)
        z = dec * xdt
        dbm = _bdot(z.astype(BF16), dscb, 2, 2)
        dz = _bdot(bm, dscb, 2, 1)
        dxdt = dec * dz
        t2 = jnp.sum(dz * z, axis=2, keepdims=True)
        dcs = dcs - t2
        d_last = d_last + jnp.sum(t2, axis=1, keepdims=True)
        m = g * lmat
        mb = m.astype(BF16)
        dm = _bdot(dyb, xdtb, 2, 2)
        dxdt = dxdt + _bdot(mb, dyb, 1, 1)
        dseg = dm * m
        dcs = dcs + jnp.sum(dseg, axis=2, keepdims=True)
        ones = jnp.ones((nc, CHUNK, SSD_P), F32)
        dcs = dcs - _bdot(dseg, ones, 1, 1, precision=HI)
        dg = (dm * lmat).astype(BF16)
        dcm = dcm + _bdot(dg, bm, 2, 1)
        dbm = dbm + _bdot(dg, cm, 1, 1)
        dcs = dcs + jnp.where(li[:, :, :SSD_P] == CHUNK - 1, d_last, 0.0)
        triu = jnp.where(li <= si, 1.0, 0.0).astype(F32)
        dadt = _bdot(triu, dcs, 2, 1, precision=HI)
        dk = dk_ref[0]
        dxs_ref[0] = (dxdt * dt + dk * dy).reshape(t, SSD_P)
        ddt_ref[0] = (jnp.sum(dxdt * x, axis=2, keepdims=True) + dadt * a_ref[0]).reshape(t, SSD_P)
        dadt_ref[0] = dadt.reshape(t, SSD_P)
        dd_ref[0] = jnp.sum(jnp.sum(dy * x, axis=1, keepdims=True), axis=0)

        @pl.when(pl.program_id(1) == 0)
        def _():
            db_ref[...] = jnp.zeros_like(db_ref)
            dc_ref[...] = jnp.zeros_like(dc_ref)

        db_ref[0] += dbm.reshape(t, SSD_N)
        dc_ref[0] += dcm.reshape(t, SSD_N)

    head = pl.BlockSpec((1, t, SSD_P), lambda gi, hi: (gi * hpg + hi, 0, 0))
    grp = pl.BlockSpec((1, t, SSD_N), lambda gi, hi: (gi, 0, 0))
    lane = pl.BlockSpec((1, 1, SSD_P), lambda gi, hi: (gi * hpg + hi, 0, 0))
    return pl.pallas_call(
        body, name=name, grid=(SSD_GROUPS, hpg),
        in_specs=[head, head, head, pl.BlockSpec((1, nc, 1, CHUNK), lambda gi, hi: (gi * hpg + hi, 0, 0, 0)),
                  grp, grp, lane, lane, pl.BlockSpec((1, nc, SSD_N, SSD_P), lambda gi, hi: (gi * hpg + hi, 0, 0, 0)),
                  head],
        out_specs=[head, head, head, grp, grp, lane],
        out_shape=[jax.ShapeDtypeStruct((SSD_HEADS, t, SSD_P), F32)] * 3
        + [jax.ShapeDtypeStruct((SSD_GROUPS, t, SSD_N), F32)] * 2
        + [jax.ShapeDtypeStruct((SSD_HEADS, 1, SSD_P), F32)],
        scratch_shapes=[pltpu.VMEM((nc, SSD_N, SSD_P), F32), pltpu.VMEM((nc, SSD_N, SSD_P), F32),
                        pltpu.VMEM((nc, 1, SSD_P), F32)],
        compiler_params=_cp("parallel", "arbitrary"),
    )(xs_h, dt_h, cs_h, cs_row, b_g, c_g, dskip_h, a_h, states, dy_h)


def _ssd_gate_fwd(y, proj, w, *, tr=256, name):
    t = y.shape[0]
    gw = D_SSM // SSD_GROUPS

    def body(y_ref, z_ref, w_ref, o_ref):
        v = y_ref[...] * _silu(z_ref[...])
        for gi in range(SSD_GROUPS):
            vg = v[:, gi * gw:(gi + 1) * gw]
            r = lax.rsqrt(jnp.mean(vg * vg, axis=-1, keepdims=True) + NORM_EPS)
            o_ref[:, gi * gw:(gi + 1) * gw] = (vg * r * w_ref[:, gi * gw:(gi + 1) * gw]).astype(BF16)

    blk = pl.BlockSpec((tr, D_SSM), lambda i: (i, 0))
    return pl.pallas_call(
        body, name=name, grid=(t // tr,), in_specs=[blk, blk, pl.BlockSpec((1, D_SSM), lambda i: (0, 0))],
        out_specs=blk, out_shape=jax.ShapeDtypeStruct((t, D_SSM), BF16), compiler_params=_cp("parallel"),
    )(y, proj, w)


def _ssd_gate_bwd(y, proj, w, dcat, *, tr=256, name):
    t = y.shape[0]
    gw = D_SSM // SSD_GROUPS

    def body(y_ref, z_ref, w_ref, d_ref, dy_ref, dz_ref, dw_ref):
        yv, zv, dv = y_ref[...], z_ref[...], d_ref[...].astype(F32)
        sz = _silu(zv)
        v = yv * sz

        @pl.when(pl.program_id(0) == 0)
        def _():
            dw_ref[...] = jnp.zeros_like(dw_ref)

        for gi in range(SSD_GROUPS):
            sl = slice(gi * gw, (gi + 1) * gw)
            vg, dg = v[:, sl], dv[:, sl]
            r = lax.rsqrt(jnp.mean(vg * vg, axis=-1, keepdims=True) + NORM_EPS)
            vh = vg * r
            gg = dg * w_ref[:, sl]
            dvg = r * (gg - vh * jnp.mean(gg * vh, axis=-1, keepdims=True))
            dy_ref[:, sl] = dvg * sz[:, sl]
            dz_ref[:, sl] = (dvg * yv[:, sl] * _dsilu(zv[:, sl])).astype(BF16)
            dw_ref[:, sl] += jnp.sum(dg * vh, axis=0, keepdims=True)

    blk = pl.BlockSpec((tr, D_SSM), lambda i: (i, 0))
    row = pl.BlockSpec((1, D_SSM), lambda i: (0, 0))
    return pl.pallas_call(
        body, name=name, grid=(t // tr,), in_specs=[blk, blk, row, blk], out_specs=[blk, blk, row],
        out_shape=[jax.ShapeDtypeStruct((t, D_SSM), F32), jax.ShapeDtypeStruct((t, D_SSM), BF16),
                   jax.ShapeDtypeStruct((1, D_SSM), F32)],
        compiler_params=_cp("arbitrary"),
    )(y, proj, w, dcat)


def _pad_lanes(v):
    return jnp.pad(v, ((0, 0), (0, LANES - v.shape[1])))


def _to_heads(v):
    return v.reshape(v.shape[0], SSD_HEADS, SSD_P).transpose(1, 0, 2)


def _from_heads(v):
    return v.transpose(1, 0, 2).reshape(v.shape[1], SSD_HEADS * SSD_P)


def _per_head(v128, t):
    return jnp.broadcast_to(v128[:, :SSD_HEADS].T[:, :, None], (SSD_HEADS, t, SSD_P))


def _ssd_forward(proj, conv_w, conv_b, dt_bias, a_log, d_skip, ssd_norm_w):
    t = proj.shape[0]
    nc = t // CHUNK
    xbc = _conv_act_fwd(proj, conv_w, conv_b, kw=SSD_CONV, glu=False, tc=512, coff=OFF_XBC // 512,
                        ncols=SSD_CONV_DIM, out_dtype=F32, name="ssd_conv_fwd")
    bias128, alog128 = _pad_lanes(dt_bias), _pad_lanes(a_log)
    dt128, cs128, a128 = _ssd_prep(proj, bias128, alog128, name="ssd_prep")
    dt_h, cs_h = _per_head(dt128, t), _per_head(cs128, t)
    cs_row = cs128[:, :SSD_HEADS].T.reshape(SSD_HEADS, nc, 1, CHUNK)
    xs_h = _to_heads(xbc[:, :D_SSM])
    gn = SSD_GROUPS * SSD_N
    b_g = xbc[:, D_SSM:D_SSM + gn].reshape(t, SSD_GROUPS, SSD_N).transpose(1, 0, 2)
    c_g = xbc[:, D_SSM + gn:].reshape(t, SSD_GROUPS, SSD_N).transpose(1, 0, 2)
    dskip_h = jnp.broadcast_to(d_skip[0][:, None, None], (SSD_HEADS, 1, SSD_P))
    a_h = jnp.broadcast_to(a128[0, :SSD_HEADS][:, None, None], (SSD_HEADS, 1, SSD_P))
    y_h, states = _ssd_fwd(xs_h, dt_h, cs_h, cs_row, b_g, c_g, dskip_h, name="ssd_scan_fwd")
    y = _from_heads(y_h)
    y_ssd = _ssd_gate_fwd(y, proj, ssd_norm_w, name="ssd_gate_fwd")
    saved = (proj, conv_w, conv_b, ssd_norm_w, bias128, dt128, a128, dt_h, cs_h, cs_row, xs_h, b_g, c_g, dskip_h, a_h,
             states, y)
    return y_ssd, saved


def _ssd_backward(saved, dcat):
    (proj, conv_w, conv_b, ssd_norm_w, bias128, dt128, a128, dt_h, cs_h, cs_row, xs_h, b_g, c_g, dskip_h, a_h, states,
     y) = saved
    t = proj.shape[0]
    dy, dz, d_norm_w = _ssd_gate_bwd(y, proj, ssd_norm_w, dcat, name="ssd_gate_bwd")
    dxs_h, ddt_h, dadt_h, db_g, dc_g, dd_h = _ssd_bwd(xs_h, dt_h, cs_h, cs_row, b_g, c_g, dskip_h, a_h, states,
                                                      _to_heads(dy), name="ssd_scan_bwd")
    gn = SSD_GROUPS * SSD_N
    dxc = jnp.concatenate([_from_heads(dxs_h), db_g.transpose(1, 0, 2).reshape(t, gn),
                           dc_g.transpose(1, 0, 2).reshape(t, gn)], axis=1)
    dxbc, d_conv_w, d_conv_b = _conv_act_bwd(proj, conv_w, conv_b, dxc, kw=SSD_CONV, glu=False, tc=512,
                                             coff=OFF_XBC // 512, ncols=SSD_CONV_DIM, name="ssd_conv_bwd")
    ddt128 = _pad_lanes(ddt_h[:, :, 0].T)
    dadt128 = _pad_lanes(dadt_h[:, :, 0].T)
    d_raw, d_bias, d_alog, d_dskip = _ssd_prep_bwd(ddt128, dadt128, proj, bias128, dt128, a128,
                                                   dd_h.reshape(SSD_HEADS, SSD_P), name="ssd_prep_bwd")
    return (dz, dxbc, d_raw, d_norm_w, d_conv_w, d_conv_b, d_bias[:, :SSD_HEADS], d_alog[:, :SSD_HEADS],
            d_dskip.reshape(1, SSD_HEADS))


def _rope_tables(positions):
    inv_freq = ROPE_THETA ** (-jnp.arange(0, MLA_ROPE, 2, dtype=F32) / MLA_ROPE)
    ang = positions[0].astype(F32)[:, None] * inv_freq
    cos, sin = jnp.cos(ang), jnp.sin(ang)
    z = jnp.zeros_like(cos)
    return jnp.stack([jnp.concatenate([cos, cos, z, z], axis=1), jnp.concatenate([-sin, z, z, z], axis=1),
                      jnp.concatenate([z, sin, z, z], axis=1)])


def _mla_forward(proj, tabs, q_a_norm_w, wq_pad, kv_a_norm_w, wkv):
    qn = _rmsnorm_fwd(proj, q_a_norm_w, width=MLA_Q_RANK, cblk=OFF_QA // MLA_Q_RANK, name="q_a_norm")
    q = _matmul(qn, wq_pad, name="q_b_proj")
    kvn = _rmsnorm_fwd(proj, kv_a_norm_w, width=MLA_KV_RANK, cblk=OFF_CKV // MLA_KV_RANK, name="kv_a_norm")
    kv = _matmul(kvn, wkv, name="kv_b_proj")
    q3, k3, v3 = _mla_prep(q, kv, proj, tabs, name="mla_prep")
    o, lse = _attn_fwd(q3, k3, v3, name="attn_fwd")
    return o, (proj, tabs, q_a_norm_w, wq_pad, kv_a_norm_w, wkv, qn, kvn, q3, k3, v3, o, lse)


def _mla_backward(saved, dcat):
    proj, tabs, q_a_norm_w, wq_pad, kv_a_norm_w, wkv, qn, kvn, q3, k3, v3, o, lse = saved
    dq3, dk3, dv3 = _attn_bwd(q3, k3, v3, o, dcat, lse, name="attn_bwd")
    dq, dkv, dkr = _mla_unprep(dq3, dk3, dv3, tabs, name="mla_unprep")
    d_wq = _matmul(qn, dq, ta=True, out_dtype=BF16, name="d_w_q_b")
    dqn = _matmul(dq, wq_pad, tb=True, name="d_qn")
    dq_a, d_qnw = _rmsnorm_bwd(proj, q_a_norm_w, dqn, width=MLA_Q_RANK, cblk=OFF_QA // MLA_Q_RANK, out_dtype=BF16,
                               name="q_a_norm_bwd")
    d_wkv = _matmul(kvn, dkv, ta=True, out_dtype=BF16, name="d_w_kv_b")
    dkvn = _matmul(dkv, wkv, tb=True, name="d_kvn")
    dckv, d_kvnw = _rmsnorm_bwd(proj, kv_a_norm_w, dkvn, width=MLA_KV_RANK, cblk=OFF_CKV // MLA_KV_RANK,
                                out_dtype=BF16, name="kv_a_norm_bwd")
    return dq_a, dckv, dkr, d_wq, d_wkv, d_qnw, d_kvnw


def _pad_w_q(w):
    r = w.shape[0]
    w3 = w.reshape(r, MLA_HEADS, MLA_NOPE + MLA_ROPE)
    return jnp.pad(w3, ((0, 0), (0, 0), (0, MLA_QK_PAD - MLA_NOPE - MLA_ROPE))).reshape(r, MLA_HEADS * MLA_QK_PAD)


def _unpad_w_q(w):
    r = w.shape[0]
    return w.reshape(r, MLA_HEADS, MLA_QK_PAD)[:, :, :MLA_NOPE + MLA_ROPE].reshape(r, MLA_HEADS * (MLA_NOPE + MLA_ROPE))


def _pad_w_in(w):
    r = w.shape[0]
    o_dt = D_SSM + SSD_CONV_DIM
    o_qa = o_dt + SSD_HEADS
    o_kr = o_qa + MLA_Q_RANK + MLA_KV_RANK
    zeros = lambda n: jnp.zeros((r, n), w.dtype)
    return jnp.concatenate([w[:, :o_dt], w[:, o_qa:o_kr], w[:, o_kr:], zeros(LANES - MLA_ROPE),
                            w[:, o_dt:o_qa], zeros(LANES - SSD_HEADS)], axis=1)


def _unpad_w_in(w):
    return jnp.concatenate([w[:, :OFF_QA], w[:, OFF_DT:OFF_DT + SSD_HEADS], w[:, OFF_QA:OFF_KR + MLA_ROPE]], axis=1)


WEIGHTS = ['mix_norm_w', 'w_in', 'conv_w', 'conv_b', 'dt_bias', 'a_log', 'd_skip', 'ssd_norm_w', 'q_a_norm_w', 'w_q_b',
           'kv_a_norm_w', 'w_kv_b', 'w_out', 'ffn_norm_w', 'w_ffn_up', 'ffn_conv_w', 'ffn_conv_b', 'w_ffn_down',
           'ple_norm_w', 'w_ple_gate', 'b_ple_gate', 'w_ple_proj', 'ple_post_norm_w', 'final_norm_w']
BIG = ['w_in', 'w_q_b', 'w_kv_b', 'w_out', 'w_ffn_up', 'w_ffn_down', 'w_ple_gate', 'w_ple_proj']
COL_SHARDED = ('w_in', 'w_q_b', 'w_kv_b', 'w_ffn_up', 'w_ple_proj')
CONV = ['conv_w', 'ffn_conv_w']
REPL = [n for n in WEIGHTS if n not in BIG and n not in CONV]
FFN_INV = tuple(int(i) for i in np.argsort(FFN_PERM))


def _cat_cols(g):
    return g.transpose(1, 0, 2).reshape(g.shape[1], N_DEV * g.shape[2])


def _split_cols(w):
    return w.reshape(w.shape[0], N_DEV, w.shape[1] // N_DEV).transpose(1, 0, 2)


def _interleave(v):
    r = v.shape[0]
    return v.reshape(r, N_DEV, FFN_TC)[:, jnp.array(FFN_PERM)].reshape(r, N_DEV * FFN_TC)


def _deinterleave(v):
    r = v.shape[0]
    return v.reshape(r, N_DEV, FFN_TC)[:, jnp.array(FFN_INV)].reshape(r, N_DEV * FFN_TC)


def _assemble_weights(g):
    layout = {
        'w_in': lambda v: _pad_w_in(_cat_cols(v)),
        'w_q_b': lambda v: _pad_w_q(_cat_cols(v)),
        'w_kv_b': _cat_cols,
        'w_out': lambda v: v.reshape(D_MODEL, D_MODEL),
        'w_ffn_up': lambda v: v,
        'w_ffn_down': lambda v: v.reshape(D_FF, D_MODEL),
        'w_ple_gate': lambda v: v.reshape(D_MODEL, D_MODEL),
        'w_ple_proj': _cat_cols,
        'conv_w': _cat_cols,
        'ffn_conv_w': lambda v: _interleave(_cat_cols(v)),
    }
    return {n: layout[n](v) for n, v in g.items()}


WEIGHT_GROUPS = {'a': ['w_in', 'w_q_b', 'w_kv_b', 'conv_w'], 'b': ['w_out'],
                 'c': ['w_ffn_up', 'ffn_conv_w', 'w_ffn_down', 'w_ple_gate', 'w_ple_proj']}
GRAD_GROUPS = {'p': ['w_ple_proj', 'w_ple_gate', 'w_ffn_down'], 'r': ['w_ffn_up'], 's': ['w_out'],
               't': ['w_q_b', 'w_kv_b', 'w_in']}


def _ffn_perm(j):
    return (j % 2) * (N_DEV // 2) + j // 2


def _local_step(x, p, tabs, get_w, s, target, emit, relay):
    t = x.shape[0]
    s = dict(s)
    half = D_MODEL // 2
    up_cols = 2 * D_FF
    ffn_conv_b = _interleave(s['ffn_conv_b'])
    w = dict(get_w('a', None))
    h = _rmsnorm_fwd(x, s['mix_norm_w'], width=D_MODEL, name="mix_norm")
    proj = _matmul(h, w['w_in'], name="in_proj")
    y_ssd, ssd_saved = _ssd_forward(proj, w['conv_w'], s['conv_b'], s['dt_bias'], s['a_log'], s['d_skip'],
                                    s['ssd_norm_w'])
    o, mla_saved = _mla_forward(proj, tabs, s['q_a_norm_w'], w['w_q_b'], s['kv_a_norm_w'], w['w_kv_b'])
    tk_o, tn_o = _tile(half, MM_TK), _tile(D_MODEL, MM_TILE)
    w.update(get_w('b', o))
    x1 = _matmul(y_ssd, w['w_out'], add=x, mnk=(t, D_MODEL, half), name="out_proj_ssd")
    x1 = _matmul(o, w['w_out'], add=x1, mnk=(t, D_MODEL, half), name="out_proj_mla",
                 b_spec=pl.BlockSpec((tk_o, tn_o), lambda i, j, kk: (kk + half // tk_o, j)))
    hf = _rmsnorm_fwd(x1, s['ffn_norm_w'], width=D_MODEL, name="ffn_norm")
    w.update(get_w('c', hf))
    tk_u = _tile(D_MODEL, MM_TK)
    u = _matmul(hf, w['w_ffn_up'], mnk=(t, up_cols, D_MODEL), tn=FFN_TC, name="ffn_up",
                b_spec=pl.BlockSpec((1, tk_u, FFN_TC), lambda i, j, kk: (_ffn_perm(j), kk, 0)))
    act = _conv_act_fwd(u, w['ffn_conv_w'], ffn_conv_b, kw=FFN_CONV, glu=True, tc=2 * FFN_TC, coff=0, ncols=up_cols,
                        out_dtype=BF16, name="ffn_act")
    x2 = _matmul(act, w['w_ffn_down'], add=x1, name="ffn_down")
    hp = _rmsnorm_fwd(x2, s['ple_norm_w'], width=D_MODEL, name="ple_norm")
    gl = _matmul(hp, w['w_ple_gate'], bias=s['b_ple_gate'], name="ple_gate")
    pe = _matmul(p, w['w_ple_proj'], name="ple_proj")
    x3 = _ple_fwd(x2, gl, pe, s['ple_post_norm_w'], name="ple_mix")
    loss, dx3, d_final = _loss_head(x3, s['final_norm_w'], target, name="loss_head")
    dgl, d_bgate, dpe, d_post = _ple_bwd(dx3, gl, pe, s['ple_post_norm_w'], name="ple_mix_bwd")
    d_wproj = _matmul(p, dpe, ta=True, out_dtype=BF16, name="d_w_ple_proj")
    d_wgate = _matmul(hp, dgl, ta=True, out_dtype=BF16, name="d_w_ple_gate")
    dhp = _matmul(dgl, w['w_ple_gate'], tb=True, name="d_ple_normed")
    dx2, d_plenorm = _rmsnorm_bwd(x2, s['ple_norm_w'], dhp, dx3, width=D_MODEL, name="ple_norm_bwd")
    dact = _matmul(dx2, w['w_ffn_down'], tb=True, name="d_ffn_act")
    d_wdown = _matmul(act, dx2, ta=True, out_dtype=BF16, name="d_w_ffn_down")
    zz = emit('p', {'w_ple_proj': _split_cols(d_wproj), 'w_ple_gate': d_wgate.reshape(N_DEV, D_MODEL // N_DEV, D_MODEL),
                    'w_ffn_down': d_wdown.reshape(N_DEV, D_FF // N_DEV, D_MODEL)})
    du, d_fconv_w, d_fconv_b = _conv_act_bwd(u, w['ffn_conv_w'], ffn_conv_b + zz, dact, kw=FFN_CONV, glu=True,
                                             tc=2 * FFN_TC, coff=0, ncols=up_cols, name="ffn_act_bwd")
    zz = zz + relay('p', du)
    tm_u = _tile(D_MODEL, MM_TILE)
    d_wup = _matmul(hf, du, ta=True, out_dtype=BF16, mnk=(D_MODEL, up_cols, t), tn=FFN_TC, name="d_w_ffn_up",
                    o_spec=pl.BlockSpec((1, tm_u, FFN_TC), lambda i, j, kk: (_ffn_perm(j), i, 0)),
                    o_shape=(N_DEV, D_MODEL, FFN_TC))
    zz = zz + emit('r', {'w_ffn_up': d_wup})
    dhf = _matmul(du, w['w_ffn_up'], tb=True, mnk=(t, D_MODEL, up_cols), tk=FFN_TC, name="d_ffn_normed",
                  b_spec=pl.BlockSpec((1, tn_o, FFN_TC), lambda i, j, kk: (_ffn_perm(kk), j, 0)))
    zz = zz + relay('r', dhf)
    dx1, d_ffnnorm = _rmsnorm_bwd(x1, s['ffn_norm_w'] + zz, dhf, dx2, width=D_MODEL, name="ffn_norm_bwd")
    dcat = _matmul(dx1, w['w_out'], tb=True, name="d_mixed")
    d_wout = jnp.concatenate([_matmul(y_ssd, dx1, ta=True, out_dtype=BF16, name="d_w_out_ssd"),
                              _matmul(o, dx1, ta=True, out_dtype=BF16, name="d_w_out_mla")], axis=0)
    zz = zz + emit('s', {'w_out': d_wout.reshape(N_DEV, D_MODEL // N_DEV, D_MODEL)})
    ssd_saved = ssd_saved[:3] + (ssd_saved[3] + zz,) + ssd_saved[4:]
    dz, dxbc, d_raw, d_ssdnorm, d_conv_w, d_conv_b, d_dtb, d_alog, d_dskip = _ssd_backward(ssd_saved, dcat)
    zz = zz + relay('s', dz)
    mla_saved = mla_saved[:-1] + (mla_saved[-1] + zz,)
    dq_a, dckv, dkr, d_wq, d_wkv, d_qnorm, d_kvnorm = _mla_backward(mla_saved, dcat)
    dproj = jnp.concatenate([dz, dxbc, dq_a, dckv, dkr, d_raw], axis=1)
    d_win = _matmul(h, dproj, ta=True, out_dtype=BF16, name="d_w_in")
    dh = _matmul(dproj, w['w_in'], tb=True, name="d_in_normed")
    dx, d_mixnorm = _rmsnorm_bwd(x, s['mix_norm_w'], dh, dx1, width=D_MODEL, name="mix_norm_bwd")
    emit('t', {'w_in': _split_cols(_unpad_w_in(d_win)), 'w_q_b': _split_cols(_unpad_w_q(d_wq)),
               'w_kv_b': _split_cols(d_wkv)})
    relay('t', dx)
    conv = {'conv_w': d_conv_w, 'ffn_conv_w': _deinterleave(d_fconv_w)}
    vec = {
        'mix_norm_w': d_mixnorm, 'conv_b': d_conv_b, 'dt_bias': d_dtb, 'a_log': d_alog, 'd_skip': d_dskip,
        'ssd_norm_w': d_ssdnorm, 'q_a_norm_w': d_qnorm, 'kv_a_norm_w': d_kvnorm, 'ffn_norm_w': d_ffnnorm,
        'ffn_conv_b': _deinterleave(d_fconv_b), 'ple_norm_w': d_plenorm, 'b_ple_gate': d_bgate,
        'ple_post_norm_w': d_post, 'final_norm_w': d_final,
    }
    return loss, dx, conv, vec


MESH = pl.DeviceIdType.MESH
FLIPS = ((0, 0, 1), (1, 0, 0), (0, 1, 0), (1, 1, 0), (1, 0, 1), (0, 1, 1), (1, 1, 1))


def _exchange(items, *, gather, name):
    n = len(items)

    def body(*refs):
        ins, outs = refs[:n], refs[n:2 * n]
        send_sems, recv_sems, local_sems = refs[2 * n:]
        x, y, c = lax.axis_index("x"), lax.axis_index("y"), lax.axis_index("c")
        me = 4 * x + 2 * y + c
        peers = [(jnp.where(fx, 1 - x, x), jnp.where(fy, 1 - y, y), jnp.where(fc, 1 - c, c)) for fx, fy, fc in FLIPS]
        slot = [4 * px + 2 * py + pc for px, py, pc in peers]
        local, sends = [], []
        for wi in range(n):
            cp = pltpu.make_async_copy(ins[wi] if gather else ins[wi].at[me], outs[wi].at[me], local_sems.at[wi])
            cp.start()
            local.append(cp)
            for k, peer in enumerate(peers):
                cp = pltpu.make_async_remote_copy(
                    src_ref=ins[wi] if gather else ins[wi].at[slot[k]], dst_ref=outs[wi].at[me],
                    send_sem=send_sems.at[k, wi], recv_sem=recv_sems.at[k, wi], device_id=peer, device_id_type=MESH)
                cp.start()
                sends.append(cp)
        for wi in range(n):
            for k, peer in enumerate(peers):
                pltpu.make_async_remote_copy(
                    src_ref=outs[wi].at[slot[k]], dst_ref=outs[wi].at[slot[k]], send_sem=send_sems.at[k, wi],
                    recv_sem=recv_sems.at[k, wi], device_id=peer, device_id_type=MESH).wait_recv()
        for cp in sends:
            cp.wait_send()
        for cp in local:
            cp.wait()

    hbm = pl.BlockSpec(memory_space=pltpu.HBM)
    out_shape = [jax.ShapeDtypeStruct(((N_DEV,) + v.shape) if gather else v.shape, v.dtype) for v in items]
    return pl.pallas_call(
        body, name=name, in_specs=[hbm] * n, out_specs=[hbm] * n, out_shape=out_shape,
        scratch_shapes=[pltpu.SemaphoreType.DMA((len(FLIPS), n)), pltpu.SemaphoreType.DMA((len(FLIPS), n)),
                        pltpu.SemaphoreType.DMA((n,))],
    )(*items)


HBM_SPEC = pl.BlockSpec(memory_space=pltpu.HBM)
SEM_SPEC = pl.BlockSpec(memory_space=pltpu.SEMAPHORE)
EFFECT = pltpu.SideEffectType.DATAFLOW_SIDE_EFFECTING


def _peers():
    x, y, c = lax.axis_index("x"), lax.axis_index("y"), lax.axis_index("c")
    peers = [(jnp.where(fx, 1 - x, x), jnp.where(fy, 1 - y, y), jnp.where(fc, 1 - c, c)) for fx, fy, fc in FLIPS]
    return 4 * x + 2 * y + c, peers, [4 * px + 2 * py + pc for px, py, pc in peers]


def _split_start(bufs, ncopies, plan, *, name):
    nb = len(bufs)

    def body(*refs):
        send_sems, recv_sems, token = refs[nb], refs[nb + 1], refs[2 * nb + 2]
        for i, (src, dst, peer, _) in enumerate(plan(refs[:nb])):
            pltpu.make_async_remote_copy(src_ref=src, dst_ref=dst, send_sem=send_sems.at[i], recv_sem=recv_sems.at[i],
                                         device_id=peer, device_id_type=MESH).start()
        token[...] = jnp.zeros_like(token)

    res = pl.pallas_call(
        body, name=name, in_specs=[HBM_SPEC] * nb,
        out_specs=[SEM_SPEC, SEM_SPEC] + [HBM_SPEC] * nb + [pl.BlockSpec(memory_space=pltpu.VMEM)],
        out_shape=[pltpu.SemaphoreType.DMA((ncopies,)), pltpu.SemaphoreType.DMA((ncopies,))]
        + [pltpu.HBM(v.shape, v.dtype) for v in bufs] + [jax.ShapeDtypeStruct((HALO, LANES), F32)],
        input_output_aliases={i: 2 + i for i in range(nb)},
        compiler_params=pltpu.CompilerParams(has_side_effects=EFFECT),
    )(*[pltpu.with_memory_space_constraint(v, pltpu.HBM) for v in bufs])
    return (res[0], res[1], list(res[2:2 + nb])), res[2 + nb]


def _split_wait(started, after, plan, local_plan, *, name):
    send_sems, recv_sems, bufs = started
    nb = len(bufs)
    nlocal = len(local_plan(bufs))

    def body(*refs):
        send_sems, recv_sems = refs[nb], refs[nb + 1]
        local_sems = refs[2 * nb + 3]
        local = []
        for j, (src, dst) in enumerate(local_plan(refs[:nb])):
            cp = pltpu.make_async_copy(src, dst, local_sems.at[j])
            cp.start()
            local.append(cp)
        for i, (src, _, peer, incoming) in enumerate(plan(refs[:nb])):
            cp = pltpu.make_async_remote_copy(src_ref=src, dst_ref=incoming, send_sem=send_sems.at[i],
                                              recv_sem=recv_sems.at[i], device_id=peer, device_id_type=MESH)
            cp.wait_send()
            cp.wait_recv()
        for cp in local:
            cp.wait()

    res = pl.pallas_call(
        body, name=name, in_specs=[HBM_SPEC] * nb + [SEM_SPEC, SEM_SPEC, pl.BlockSpec(memory_space=pl.ANY)],
        out_specs=[HBM_SPEC] * nb, out_shape=[pltpu.HBM(v.shape, v.dtype) for v in bufs],
        input_output_aliases={i: i for i in range(nb)},
        scratch_shapes=[pltpu.SemaphoreType.DMA((max(nlocal, 1),))],
        compiler_params=pltpu.CompilerParams(has_side_effects=EFFECT),
    )(*bufs, send_sems, recv_sems, after)
    return list(res)


def _place():
    x, y, c = lax.axis_index("x"), lax.axis_index("y"), lax.axis_index("c")
    others = [((1 - x, y, c), 2 * (1 - x) + y), ((x, 1 - y, c), 2 * x + 1 - y), ((1 - x, 1 - y, c), 2 * (1 - x) + 1 - y)]
    return 4 * x + 2 * y + c, 2 * x + y, c, (x, y, 1 - c), others


def _gather1_plan(n):
    def plan(refs):
        me, _, _, sibling, others = _place()
        out = []
        for wi in range(n):
            item, land = refs[wi], refs[n + wi]
            out.append((item, land.at[me], sibling, land.at[me + 1 - 2 * lax.axis_index("c")]))
            for peer, chip in others:
                out.append((item, land.at[me], peer, land.at[2 * chip + lax.axis_index("c")]))
        return out

    return plan


def _gather1_local(n):
    def plan(refs):
        me = _place()[0]
        return [(refs[wi], refs[n + wi].at[me]) for wi in range(n)]

    return plan


def _gather2_plan(n):
    def plan(refs):
        _, _, c, sibling, others = _place()
        out = []
        for wi in range(n):
            land = refs[wi]
            for _, chip in others:
                out.append((land.at[2 * chip + c], land.at[2 * chip + c], sibling, land.at[2 * chip + 1 - c]))
        return out

    return plan


def _gather_start(items, *, name):
    lands = [lax.empty((N_DEV,) + v.shape, v.dtype) for v in items]
    return _split_start(items + lands, 4 * len(items), _gather1_plan(len(items)), name=name)


def _gather_forward(started, after, *, name):
    n = len(started[2]) // 2
    bufs = _split_wait(started, after, _gather1_plan(n), _gather1_local(n), name=name + "_wait")
    return _split_start(bufs[n:], 3 * n, _gather2_plan(n), name=name + "_start")


def _gather_finish(started, after, *, name):
    n = len(started[2])
    return _split_wait(started, after, _gather2_plan(n), lambda refs: [], name=name)


def _handshake(peers):
    barrier = pltpu.get_barrier_semaphore()
    for peer in peers:
        pl.semaphore_signal(barrier, inc=1, device_id=peer, device_id_type=MESH)
    pl.semaphore_wait(barrier, len(peers))


def _remote(src, dst, send_sem, recv_sem, peer):
    return pltpu.make_async_remote_copy(src_ref=src, dst_ref=dst, send_sem=send_sem, recv_sem=recv_sem, device_id=peer,
                                        device_id_type=MESH)


def _sequencer_gather(items, *, collective_id, name):
    n = len(items)
    srcs = [jax.new_ref(v, memory_space=pltpu.MemorySpace.HBM) for v in items]
    lands = [jax.empty_ref(jax.ShapeDtypeStruct((N_DEV,) + v.shape, v.dtype), memory_space=pltpu.MemorySpace.HBM)
             for v in items]
    dma = pltpu.SemaphoreType.DMA

    @pl.kernel(mesh=plsc.ScalarSubcoreMesh(axis_name="sequencer", num_cores=1), name=name,
               scratch_types=(dma((4 * n,)), dma((4 * n,)), dma((3 * n,)), dma((3 * n,)), dma((n,))),
               compiler_params=pltpu.CompilerParams(collective_id=collective_id))
    def launch(send1, recv1, send2, recv2, local_sems):
        _, _, _, sibling, others = _place()
        _handshake([sibling] + [peer for peer, _ in others])
        hop1 = _gather1_plan(n)(srcs + lands)
        hop2 = _gather2_plan(n)(lands)
        local = [pltpu.make_async_copy(src, dst, local_sems.at[j])
                 for j, (src, dst) in enumerate(_gather1_local(n)(srcs + lands))]
        for cp in local:
            cp.start()
        for i, (src, dst, peer, _) in enumerate(hop1):
            _remote(src, dst, send1.at[i], recv1.at[i], peer).start()
        for wi in range(n):
            for j in range(3):
                i1, i2 = 4 * wi + 1 + j, 3 * wi + j
                src, _, peer, incoming = hop1[i1]
                _remote(src, incoming, send1.at[i1], recv1.at[i1], peer).wait_recv()
                src, dst, peer, _ = hop2[i2]
                _remote(src, dst, send2.at[i2], recv2.at[i2], peer).start()
        for wi in range(n):
            src, _, peer, incoming = hop1[4 * wi]
            _remote(src, incoming, send1.at[4 * wi], recv1.at[4 * wi], peer).wait_recv()
        for i, (src, _, peer, incoming) in enumerate(hop2):
            cp = _remote(src, incoming, send2.at[i], recv2.at[i], peer)
            cp.wait_send()
            cp.wait_recv()
        for i, (src, dst, peer, _) in enumerate(hop1):
            _remote(src, dst, send1.at[i], recv1.at[i], peer).wait_send()
        for cp in local:
            cp.wait()

    launch()
    return [land[...] for land in lands]


def _sequencer_exchange(sources, land_shapes, ncopies, plan, local_plan, peers, *, collective_id, name):
    srcs = [jax.new_ref(v, memory_space=pltpu.MemorySpace.HBM) for v in sources]
    lands = [jax.empty_ref(s, memory_space=pltpu.MemorySpace.HBM) for s in land_shapes]
    nlocal = len(local_plan(srcs + lands))
    dma = pltpu.SemaphoreType.DMA

    @pl.kernel(mesh=plsc.ScalarSubcoreMesh(axis_name="sequencer", num_cores=1), name=name,
               scratch_types=(dma((ncopies,)), dma((ncopies,)), dma((max(nlocal, 1),))),
               compiler_params=pltpu.CompilerParams(collective_id=collective_id))
    def launch(send_sems, recv_sems, local_sems):
        _handshake(peers(_place()))
        copies = plan(srcs + lands)
        local = [pltpu.make_async_copy(src, dst, local_sems.at[j])
                 for j, (src, dst) in enumerate(local_plan(srcs + lands))]
        for cp in local:
            cp.start()
        for i, (src, dst, peer, _) in enumerate(copies):
            _remote(src, dst, send_sems.at[i], recv_sems.at[i], peer).start()
        for i, (src, _, peer, incoming) in enumerate(copies):
            cp = _remote(src, incoming, send_sems.at[i], recv_sems.at[i], peer)
            cp.wait_send()
            cp.wait_recv()
        for cp in local:
            cp.wait()

    launch()
    return [land[...] for land in lands]


def _sequencer_scatter_hop1(parts, *, collective_id, name):
    n = len(parts)
    shapes = [jax.ShapeDtypeStruct((N_CHIP,) + v.shape[1:], v.dtype) for v in parts]
    return _sequencer_exchange(parts, shapes, N_CHIP * n, _scatter1_plan(n), lambda refs: [], lambda place: [place[3]],
                               collective_id=collective_id, name=name)


def _sequencer_scatter_hop2(sums, *, collective_id, name):
    n = len(sums)
    shapes = [jax.ShapeDtypeStruct(v.shape, v.dtype) for v in sums]
    return _sequencer_exchange(sums, shapes, 3 * n, _scatter2_plan(n), _scatter2_local(n),
                               lambda place: [peer for peer, _ in place[4]], collective_id=collective_id, name=name)


N_CHIP = N_DEV // 2


def _scatter1_plan(n):
    def plan(refs):
        _, _, c, sibling, _ = _place()
        out = []
        for wi in range(n):
            parts, half = refs[wi], refs[n + wi]
            for chip in range(N_CHIP):
                out.append((parts.at[2 * chip + 1 - c], half.at[chip], sibling, half.at[chip]))
        return out

    return plan


def _scatter2_plan(n):
    def plan(refs):
        _, my_chip, _, _, others = _place()
        out = []
        for wi in range(n):
            sums, recv = refs[wi], refs[n + wi]
            for peer, chip in others:
                out.append((sums.at[chip], recv.at[my_chip], peer, recv.at[chip]))
        return out

    return plan


def _scatter2_local(n):
    def plan(refs):
        my_chip = _place()[1]
        return [(refs[wi].at[my_chip], refs[n + wi].at[my_chip]) for wi in range(n)]

    return plan


def _pair_add(parts, half, core, *, name):
    _, r, c = parts.shape
    tr = max(d for d in range(HALO, 257, HALO) if r % d == 0) if r > 256 else r
    parts4 = parts.reshape(N_CHIP, 2, r, c)

    def body(core_ref, p_ref, h_ref, o_ref):
        o_ref[...] = (p_ref[:, 0].astype(F32) + h_ref[...].astype(F32)).astype(o_ref.dtype)

    return pl.pallas_call(
        body, name=name,
        grid_spec=pltpu.PrefetchScalarGridSpec(
            num_scalar_prefetch=1, grid=(r // tr,),
            in_specs=[pl.BlockSpec((N_CHIP, 1, tr, c), lambda i, core_ref: (0, core_ref[0], i, 0)),
                      pl.BlockSpec((N_CHIP, tr, c), lambda i, core_ref: (0, i, 0))],
            out_specs=pl.BlockSpec((N_CHIP, tr, c), lambda i, core_ref: (0, i, 0))),
        out_shape=jax.ShapeDtypeStruct((N_CHIP, r, c), parts.dtype), compiler_params=_cp("parallel"),
    )(core, parts4, half)


def _scatter_start(parts, *, name):
    halves = [lax.empty((N_CHIP,) + v.shape[1:], v.dtype) for v in parts]
    return _split_start(parts + halves, N_CHIP * len(parts), _scatter1_plan(len(parts)), name=name)


def _scatter_forward(started, after, core, *, name):
    n = len(started[2]) // 2
    bufs = _split_wait(started, after, _scatter1_plan(n), lambda refs: [], name=name + "_wait")
    sums = [_pair_add(bufs[wi], bufs[n + wi], core, name=name + "_add%d" % wi) for wi in range(n)]
    recvs = [lax.empty(v.shape, v.dtype) for v in sums]
    return _split_start(sums + recvs, 3 * n, _scatter2_plan(n), name=name + "_start")


def _scatter_finish(started, after, *, name):
    n = len(started[2]) // 2
    return _split_wait(started, after, _scatter2_plan(n), _scatter2_local(n), name=name)[n:]


def _adamw(parts, w, m, v, *, name):
    r, c = w.shape
    nparts = parts.shape[0]
    tr = max(d for d in range(HALO, 129, HALO) if r % d == 0) if r > 128 else r

    def body(p_ref, w_ref, m_ref, v_ref, g_ref, d_ref, mo_ref, vo_ref):
        g = p_ref[0].astype(F32)
        for k in range(1, nparts):
            g = g + p_ref[k].astype(F32)
        mn = ADAM_B1 * m_ref[...] + (1.0 - ADAM_B1) * g
        vn = ADAM_B2 * v_ref[...] + (1.0 - ADAM_B2) * (g * g)
        m_hat = mn / (1.0 - ADAM_B1 ** ADAM_STEP)
        v_hat = vn / (1.0 - ADAM_B2 ** ADAM_STEP)
        g_ref[...] = g
        d_ref[...] = -ADAM_LR * (m_hat / (jnp.sqrt(v_hat) + ADAM_EPS) + ADAM_WD * w_ref[...])
        mo_ref[...] = mn
        vo_ref[...] = vn

    blk = pl.BlockSpec((tr, c), lambda i: (i, 0))
    return pl.pallas_call(
        body, name=name, grid=(r // tr,), in_specs=[pl.BlockSpec((nparts, tr, c), lambda i: (0, i, 0)), blk, blk, blk],
        out_specs=[blk] * 4, out_shape=[jax.ShapeDtypeStruct((r, c), F32)] * 4, compiler_params=_cp("parallel"),
    )(parts, w, m, v)


def _pack_rows(vs, rows):
    lead = vs[0].shape[:-1] if vs[0].ndim > 1 else ()
    flat = jnp.concatenate(vs, axis=-1)
    pad = rows * LANES - flat.shape[-1]
    flat = jnp.pad(flat, [(0, 0)] * len(lead) + [(0, pad)])
    return flat.reshape(lead + (rows, LANES))


def kernel(x, p, positions, mix_norm_w, w_in, conv_w, conv_b, dt_bias, a_log, d_skip, ssd_norm_w, q_a_norm_w, w_q_b, kv_a_norm_w, w_kv_b, w_out, ffn_norm_w, w_ffn_up, ffn_conv_w, ffn_conv_b, w_ffn_down, ple_norm_w, w_ple_gate, b_ple_gate, w_ple_proj, ple_post_norm_w, final_norm_w, loss_target, m_mix_norm_w, m_w_in, m_conv_w, m_conv_b, m_dt_bias, m_a_log, m_d_skip, m_ssd_norm_w, m_q_a_norm_w, m_w_q_b, m_kv_a_norm_w, m_w_kv_b, m_w_out, m_ffn_norm_w, m_w_ffn_up, m_ffn_conv_w, m_ffn_conv_b, m_w_ffn_down, m_ple_norm_w, m_w_ple_gate, m_b_ple_gate, m_w_ple_proj, m_ple_post_norm_w, m_final_norm_w, v_mix_norm_w, v_w_in, v_conv_w, v_conv_b, v_dt_bias, v_a_log, v_d_skip, v_ssd_norm_w, v_q_a_norm_w, v_w_q_b, v_kv_a_norm_w, v_w_kv_b, v_w_out, v_ffn_norm_w, v_w_ffn_up, v_ffn_conv_w, v_ffn_conv_b, v_w_ffn_down, v_ple_norm_w, v_w_ple_gate, v_b_ple_gate, v_w_ple_proj, v_ple_post_norm_w, v_final_norm_w):
    given = dict(locals())
    shapes = {n: given[n].shape for n in WEIGHTS}
    w2 = {n: given[n].reshape(given[n].shape[-2:] if n in BIG or n in CONV else (1, -1)) for n in WEIGHTS}
    m2 = {n: given['m_' + n].reshape(w2[n].shape) for n in WEIGHTS}
    v2 = {n: given['v_' + n].reshape(w2[n].shape) for n in WEIGHTS}
    me = 4 * lax.axis_index("x") + 2 * lax.axis_index("y") + lax.axis_index("c")

    core = lax.axis_index("c").astype(jnp.int32).reshape(1)

    def shards(grp, zero):
        return [(w2[n] + zero).astype(BF16) if n in BIG else w2[n] + zero for n in WEIGHT_GROUPS[grp]]

    first, token = _gather_start(shards('a', 0.0), name="gather_a_hop1")
    first, token = _gather_forward(first, token, name="gather_a_hop2")
    zero = token[0, 0]
    later = _sequencer_gather(shards('b', zero) + shards('c', zero), collective_id=1, name="gather_later")
    later = dict(zip(WEIGHT_GROUPS['b'] + WEIGHT_GROUPS['c'], later))

    def get_w(grp, after):
        if grp == 'a':
            lands = dict(zip(WEIGHT_GROUPS[grp], _gather_finish(first, token, name="gather_a_done")))
        else:
            lands = {n: later[n] for n in WEIGHT_GROUPS[grp]}
        return _assemble_weights(lands)

    scatters = {}

    hop_ids = {grp: 2 + 2 * i for i, grp in enumerate(GRAD_GROUPS)}

    def zero_of(arrays):
        return sum(v[(0,) * v.ndim].astype(F32) * 0.0 for v in arrays)

    def emit(grp, grads):
        scatters[grp], tok = _scatter_start([grads[n] for n in GRAD_GROUPS[grp]], name="scatter_" + grp + "_hop1")
        return tok[0, 0]

    def relay(grp, after):
        n = len(GRAD_GROUPS[grp])
        bufs = _split_wait(scatters[grp], after, _scatter1_plan(n), lambda refs: [], name="scatter_" + grp + "_hop1_wait")
        sums = [_pair_add(bufs[i], bufs[n + i], core, name="scatter_%s_add%d" % (grp, i)) for i in range(n)]
        scatters[grp] = _sequencer_scatter_hop2(sums, collective_id=hop_ids[grp] + 1, name="scatter_" + grp + "_hop2")
        return zero_of(sums)

    vecs = {n: w2[n] for n in REPL}
    vecs['mix_norm_w'] = vecs['mix_norm_w'] + zero
    loss, dx, g_conv, g_vec = _local_step(x[0], p[0, 0], _rope_tables(positions), get_w, vecs, loss_target[0], emit,
                                          relay)
    n_small = sum(g_vec[n].shape[1] for n in REPL) + sum(g_conv[n].size for n in CONV) + 1
    rows_small = -(-n_small // (LANES * HALO)) * HALO
    small = _pack_rows([g_vec[n] for n in REPL] + [g_conv[n].reshape(1, -1) for n in CONV] + [loss], rows_small)

    out_g, out_d, out_m, out_v = {}, {}, {}, {}
    for grp, names in GRAD_GROUPS.items():
        received = scatters[grp]
        for n, parts in zip(names, received):
            out_g[n], out_d[n], out_m[n], out_v[n] = _adamw(parts, w2[n], m2[n], v2[n], name="adamw_" + n)
    small = small + zero_of([out_g[GRAD_GROUPS['t'][-1]]])
    all_small = _exchange([small], gather=True, name="gather_small_grads")[0].reshape(N_DEV, rows_small * LANES)
    pieces, off = [], 0
    for n in REPL:
        k = g_vec[n].shape[1]
        pieces.append(all_small[:, off:off + k])
        off += k
    for n in CONV:
        kw, cols = g_conv[n].shape
        full = all_small[:, off:off + kw * cols].reshape(N_DEV, kw, cols)
        mine = lax.dynamic_slice_in_dim(full, me * (cols // N_DEV), cols // N_DEV, axis=2)
        pieces.append(mine.reshape(N_DEV, kw * (cols // N_DEV)))
        off += kw * cols
    pieces.append(all_small[:, off:off + 1])
    small_names = REPL + CONV
    n_mine = sum(q.shape[1] for q in pieces)
    rows_mine = -(-n_mine // (LANES * HALO)) * HALO
    zero = jnp.zeros((1, 1), F32)
    packed = [_pack_rows([src[n].reshape(1, -1) for n in small_names] + [zero], rows_mine).reshape(rows_mine, LANES)
              for src in (w2, m2, v2)]
    sg, sd, sm, sv = _adamw(_pack_rows(pieces, rows_mine), *packed, name="adamw_small")
    off = 0
    for n in small_names:
        k = w2[n].size
        for dst, src in ((out_g, sg), (out_d, sd), (out_m, sm), (out_v, sv)):
            dst[n] = src.reshape(-1)[off:off + k].reshape(w2[n].shape)
        off += k
    total_loss = sg.reshape(-1)[off]

    outs = [total_loss, dx[None]]
    for res in (out_g, out_d, out_m, out_v):
        outs += [res[n].reshape(shapes[n]) for n in WEIGHTS]
    return tuple(outs)
```

```python
import functools
import math

import numpy as np
import jax
import jax.numpy as jnp
from jax import lax
from jax.experimental import pallas as pl
from jax.experimental.pallas import tpu as pltpu
from jax.experimental.pallas import tpu_sc as plsc

F32 = jnp.float32
BF16 = jnp.bfloat16
HI = lax.Precision.HIGHEST

D_MODEL = 2048
CHUNK = 64
D_SSM = 1024
SSD_P = 64
SSD_HEADS = 16
SSD_GROUPS = 2
SSD_N = 128
SSD_CONV = 4
SSD_CONV_DIM = D_SSM + 2 * SSD_GROUPS * SSD_N
MLA_HEADS = 8
MLA_NOPE = 128
MLA_ROPE = 64
MLA_V = 128
MLA_Q_RANK = 512
MLA_KV_RANK = 256
MLA_QK_PAD = 256
ROPE_THETA = 10000.0
D_FF = 5632
FFN_CONV = 3
PLE_DIM = 256
NORM_EPS = 1e-6
ADAM_LR, ADAM_B1, ADAM_B2, ADAM_EPS, ADAM_WD, ADAM_STEP = 0.001, 0.9, 0.999, 1e-08, 0.01, 10
N_DEV = 8

OFF_Z, OFF_XBC, OFF_QA, OFF_CKV, OFF_KR, OFF_DT, D_IN_PAD = 0, 1024, 2560, 3072, 3328, 3456, 3584
D_IN = 3408
LANES = 128
HALO = 8
VMEM_LIMIT = 56 * 1024 * 1024
FFN_TC = D_FF * 2 // N_DEV
FFN_PERM = (0, 4, 1, 5, 2, 6, 3, 7)
NEG = -1e30


def _cp(*sem):
    return pltpu.CompilerParams(dimension_semantics=tuple(sem), vmem_limit_bytes=VMEM_LIMIT)


def _tile(n, want):
    if n <= want:
        return n
    best = max(d for d in range(LANES, want + 1, LANES) if n % d == 0)
    return best


def _sigmoid(x):
    return 1.0 / (1.0 + jnp.exp(-x))


def _silu(x):
    return x * _sigmoid(x)


def _dsilu(x):
    s = _sigmoid(x)
    return s * (1.0 + x * (1.0 - s))


MM_TILE = 1408
MM_TK = 2816


def _matmul(a, b, *, ta=False, tb=False, out_dtype=F32, add=None, bias=None, tm=MM_TILE, tn=MM_TILE, tk=MM_TK, name,
            mnk=None, a_spec=None, b_spec=None, o_spec=None, o_shape=None):
    if mnk is None:
        m, k = (a.shape[1], a.shape[0]) if ta else a.shape
        n = b.shape[0] if tb else b.shape[1]
        assert k == (b.shape[1] if tb else b.shape[0])
    else:
        m, n, k = mnk
    tm, tn, tk = _tile(m, tm), _tile(n, tn), _tile(k, tk)
    nk = k // tk
    dims = (((0 if ta else 1,), (1 if tb else 0,)), ((), ()))

    def body(*refs):
        a_ref, b_ref = refs[0], refs[1]
        pos = 2
        add_ref = bias_ref = None
        if add is not None:
            add_ref = refs[pos]
            pos += 1
        if bias is not None:
            bias_ref = refs[pos]
            pos += 1
        o_ref = refs[pos]
        kk = pl.program_id(2)
        av = a_ref[...]
        bv = b_ref[...]
        av = av.reshape(av.shape[-2:]).astype(BF16)
        bv = bv.reshape(bv.shape[-2:]).astype(BF16)
        prod = lax.dot_general(av, bv, dims, preferred_element_type=F32)

        def finish(r):
            if bias_ref is not None:
                r = r + bias_ref[...]
            if add_ref is not None:
                r = r + add_ref[...].astype(F32)
            o_ref[...] = r.astype(out_dtype).reshape(o_ref.shape)

        if nk == 1:
            finish(prod)
        else:
            acc_ref = refs[pos + 1]

            @pl.when(kk == 0)
            def _():
                acc_ref[...] = prod

            @pl.when(kk > 0)
            def _():
                acc_ref[...] += prod

            @pl.when(kk == nk - 1)
            def _():
                finish(acc_ref[...])

    if a_spec is None:
        a_spec = (pl.BlockSpec((tk, tm), lambda i, j, kk: (kk, i)) if ta
                  else pl.BlockSpec((tm, tk), lambda i, j, kk: (i, kk)))
    if b_spec is None:
        b_spec = (pl.BlockSpec((tn, tk), lambda i, j, kk: (j, kk)) if tb
                  else pl.BlockSpec((tk, tn), lambda i, j, kk: (kk, j)))
    if o_spec is None:
        o_spec = pl.BlockSpec((tm, tn), lambda i, j, kk: (i, j))
    if o_shape is None:
        o_shape = (m, n)
    in_specs = [a_spec, b_spec]
    args = [a, b]
    if add is not None:
        in_specs.append(pl.BlockSpec((tm, tn), lambda i, j, kk: (i, j)))
        args.append(add)
    if bias is not None:
        in_specs.append(pl.BlockSpec((1, tn), lambda i, j, kk: (0, j)))
        args.append(bias)
    return pl.pallas_call(
        body, name=name, grid=(m // tm, n // tn, nk), in_specs=in_specs, out_specs=o_spec,
        out_shape=jax.ShapeDtypeStruct(o_shape, out_dtype),
        scratch_shapes=[pltpu.VMEM((tm, tn), F32)] if nk > 1 else [],
        compiler_params=_cp("parallel", "parallel", "arbitrary"),
    )(*args)


def _rmsnorm_fwd(x, w, *, width, cblk=0, out_dtype=BF16, tr=256, name):
    t = x.shape[0]

    def body(x_ref, w_ref, o_ref):
        xv = x_ref[...].astype(F32)
        r = lax.rsqrt(jnp.mean(xv * xv, axis=-1, keepdims=True) + NORM_EPS)
        o_ref[...] = (xv * r * w_ref[...]).astype(out_dtype)

    return pl.pallas_call(
        body, name=name, grid=(t // tr,),
        in_specs=[pl.BlockSpec((tr, width), lambda i: (i, cblk)), pl.BlockSpec((1, width), lambda i: (0, 0))],
        out_specs=pl.BlockSpec((tr, width), lambda i: (i, 0)),
        out_shape=jax.ShapeDtypeStruct((t, width), out_dtype),
        compiler_params=_cp("parallel"),
    )(x, w)


def _rmsnorm_bwd(x, w, dy, add=None, *, width, cblk=0, out_dtype=F32, tr=256, name):
    t = x.shape[0]

    def body(*refs):
        if add is None:
            x_ref, w_ref, dy_ref, dx_ref, dw_ref = refs
            add_ref = None
        else:
            x_ref, w_ref, dy_ref, add_ref, dx_ref, dw_ref = refs
        xv = x_ref[...].astype(F32)
        dyv = dy_ref[...].astype(F32)
        r = lax.rsqrt(jnp.mean(xv * xv, axis=-1, keepdims=True) + NORM_EPS)
        xh = xv * r
        g = dyv * w_ref[...]
        dx = r * (g - xh * jnp.mean(g * xh, axis=-1, keepdims=True))
        if add_ref is not None:
            dx = dx + add_ref[...].astype(F32)
        dx_ref[...] = dx.astype(out_dtype)

        @pl.when(pl.program_id(0) == 0)
        def _():
            dw_ref[...] = jnp.zeros_like(dw_ref)

        dw_ref[...] += jnp.sum(dyv * xh, axis=0, keepdims=True)

    in_specs = [pl.BlockSpec((tr, width), lambda i: (i, cblk)), pl.BlockSpec((1, width), lambda i: (0, 0)),
                pl.BlockSpec((tr, width), lambda i: (i, 0))]
    args = [x, w, dy]
    if add is not None:
        in_specs.append(pl.BlockSpec((tr, width), lambda i: (i, 0)))
        args.append(add)
    return pl.pallas_call(
        body, name=name, grid=(t // tr,), in_specs=in_specs,
        out_specs=[pl.BlockSpec((tr, width), lambda i: (i, 0)), pl.BlockSpec((1, width), lambda i: (0, 0))],
        out_shape=[jax.ShapeDtypeStruct((t, width), out_dtype), jax.ShapeDtypeStruct((1, width), F32)],
        compiler_params=_cp("arbitrary"),
    )(*args)


def _shift_down(prev_halo, cur, j):
    if j == 0:
        return cur
    ext = jnp.concatenate([prev_halo, cur], axis=0)
    return pltpu.roll(ext, j, axis=0)[HALO:]


def _shift_up(cur, next_halo, j):
    if j == 0:
        return cur
    ext = jnp.concatenate([cur, next_halo], axis=0)
    return pltpu.roll(ext, ext.shape[0] - j, axis=0)[:cur.shape[0]]


def _conv_rows(prev, cur, w, b, kw):
    shifted = [cur]
    out = b + w[kw - 1:kw] * cur
    for j in range(1, kw):
        sh = _shift_down(prev, cur, j)
        shifted.append(sh)
        out = out + w[kw - 1 - j:kw - j] * sh
    return out, shifted


def _act_fwd(c, glu):
    if glu:
        half = c.shape[1] // 2
        return _silu(c[:, :half]) * c[:, half:]
    return _silu(c)


def _act_bwd(c, dout, glu):
    if glu:
        half = c.shape[1] // 2
        g, up = c[:, :half], c[:, half:]
        return jnp.concatenate([dout * up * _dsilu(g), dout * _silu(g)], axis=1)
    return dout * _dsilu(c)


def _conv_act_fwd(u, w, b, *, kw, glu, tc, coff, ncols, out_dtype, tr=256, name):
    t = u.shape[0]
    nb = ncols // tc
    oc = tc // 2 if glu else tc

    def body(u_ref, uh_ref, w_ref, b_ref, o_ref):
        prev = jnp.where(pl.program_id(0) == 0, 0.0, uh_ref[...])
        c, _ = _conv_rows(prev, u_ref[...], w_ref[...], b_ref[...], kw)
        o_ref[...] = _act_fwd(c, glu).astype(out_dtype)

    return pl.pallas_call(
        body, name=name, grid=(t // tr, nb),
        in_specs=[pl.BlockSpec((tr, tc), lambda i, j: (i, j + coff)),
                  pl.BlockSpec((HALO, tc), lambda i, j: (jnp.maximum(i * (tr // HALO) - 1, 0), j + coff)),
                  pl.BlockSpec((kw, tc), lambda i, j: (0, j)), pl.BlockSpec((1, tc), lambda i, j: (0, j))],
        out_specs=pl.BlockSpec((tr, oc), lambda i, j: (i, j)),
        out_shape=jax.ShapeDtypeStruct((t, nb * oc), out_dtype),
        compiler_params=_cp("parallel", "parallel"),
    )(u, u, w, b)


def _conv_act_bwd(u, w, b, dout, *, kw, glu, tc, coff, ncols, tr=256, name):
    t = u.shape[0]
    nb = ncols // tc
    nt = t // tr
    oc = tc // 2 if glu else tc

    def body(u_ref, up_ref, un_ref, d_ref, dn_ref, w_ref, b_ref, du_ref, dw_ref, db_ref):
        i = pl.program_id(1)
        cur, nxt, wv, bv = u_ref[...], un_ref[...], w_ref[...], b_ref[...]
        prev = jnp.where(i == 0, 0.0, up_ref[...])
        c_cur, shifted = _conv_rows(prev, cur, wv, bv, kw)
        c_nxt, _ = _conv_rows(cur[tr - HALO:], nxt, wv, bv, kw)
        d_cur = _act_bwd(c_cur, d_ref[...].astype(F32), glu)
        d_nxt = _act_bwd(c_nxt, jnp.where(i == nt - 1, 0.0, dn_ref[...].astype(F32)), glu)
        du = wv[kw - 1:kw] * d_cur
        for j in range(1, kw):
            du = du + wv[kw - 1 - j:kw - j] * _shift_up(d_cur, d_nxt, j)
        du_ref[...] = du.astype(BF16)

        @pl.when(i == 0)
        def _():
            dw_ref[...] = jnp.zeros_like(dw_ref)
            db_ref[...] = jnp.zeros_like(db_ref)

        db_ref[...] += jnp.sum(d_cur, axis=0, keepdims=True)
        dw_ref[...] += jnp.concatenate(
            [jnp.sum(d_cur * shifted[kw - 1 - k], axis=0, keepdims=True) for k in range(kw)], axis=0)

    nh = tr // HALO
    return pl.pallas_call(
        body, name=name, grid=(nb, nt),
        in_specs=[pl.BlockSpec((tr, tc), lambda j, i: (i, j + coff)),
                  pl.BlockSpec((HALO, tc), lambda j, i: (jnp.maximum(i * nh - 1, 0), j + coff)),
                  pl.BlockSpec((HALO, tc), lambda j, i: (jnp.minimum((i + 1) * nh, t // HALO - 1), j + coff)),
                  pl.BlockSpec((tr, oc), lambda j, i: (i, j)),
                  pl.BlockSpec((HALO, oc), lambda j, i: (jnp.minimum((i + 1) * nh, t // HALO - 1), j)),
                  pl.BlockSpec((kw, tc), lambda j, i: (0, j)), pl.BlockSpec((1, tc), lambda j, i: (0, j))],
        out_specs=[pl.BlockSpec((tr, tc), lambda j, i: (i, j)), pl.BlockSpec((kw, tc), lambda j, i: (0, j)),
                   pl.BlockSpec((1, tc), lambda j, i: (0, j))],
        out_shape=[jax.ShapeDtypeStruct((t, ncols), BF16), jax.ShapeDtypeStruct((kw, ncols), F32),
                   jax.ShapeDtypeStruct((1, ncols), F32)],
        compiler_params=_cp("parallel", "arbitrary"),
    )(u, u, u, dout, dout, w, b)


def _ple_fwd(x2, gl, pe, pw, *, tr=256, name):
    t, d = x2.shape

    def body(x_ref, gl_ref, pe_ref, pw_ref, o_ref):
        pv = pe_ref[...]
        r = lax.rsqrt(jnp.mean(pv * pv, axis=-1, keepdims=True) + NORM_EPS)
        o_ref[...] = x_ref[...] + _sigmoid(gl_ref[...]) * (pv * r * pw_ref[...])

    blk = pl.BlockSpec((tr, d), lambda i: (i, 0))
    return pl.pallas_call(
        body, name=name, grid=(t // tr,), in_specs=[blk, blk, blk, pl.BlockSpec((1, d), lambda i: (0, 0))],
        out_specs=blk, out_shape=jax.ShapeDtypeStruct((t, d), F32), compiler_params=_cp("parallel"),
    )(x2, gl, pe, pw)


def _ple_bwd(dx3, gl, pe, pw, *, tr=256, name):
    t, d = dx3.shape

    def body(dx_ref, gl_ref, pe_ref, pw_ref, dgl_ref, db_ref, dpe_ref, dpw_ref):
        dx, pv, pwv = dx_ref[...], pe_ref[...], pw_ref[...]
        gate = _sigmoid(gl_ref[...])
        r = lax.rsqrt(jnp.mean(pv * pv, axis=-1, keepdims=True) + NORM_EPS)
        ph = pv * r
        dgl = dx * (ph * pwv) * gate * (1.0 - gate)
        de = dx * gate
        g = de * pwv
        dgl_ref[...] = dgl.astype(BF16)
        dpe_ref[...] = (r * (g - ph * jnp.mean(g * ph, axis=-1, keepdims=True))).astype(BF16)

        @pl.when(pl.program_id(0) == 0)
        def _():
            db_ref[...] = jnp.zeros_like(db_ref)
            dpw_ref[...] = jnp.zeros_like(dpw_ref)

        db_ref[...] += jnp.sum(dgl, axis=0, keepdims=True)
        dpw_ref[...] += jnp.sum(de * ph, axis=0, keepdims=True)

    blk = pl.BlockSpec((tr, d), lambda i: (i, 0))
    row = pl.BlockSpec((1, d), lambda i: (0, 0))
    return pl.pallas_call(
        body, name=name, grid=(t // tr,), in_specs=[blk, blk, blk, row], out_specs=[blk, row, blk, row],
        out_shape=[jax.ShapeDtypeStruct((t, d), BF16), jax.ShapeDtypeStruct((1, d), F32),
                   jax.ShapeDtypeStruct((t, d), BF16), jax.ShapeDtypeStruct((1, d), F32)],
        compiler_params=_cp("arbitrary"),
    )(dx3, gl, pe, pw)


def _loss_head(x3, fw, target, *, tr=256, name):
    t, d = x3.shape

    def body(x_ref, w_ref, t_ref, l_ref, dx_ref, dw_ref):
        xv, wv = x_ref[...], w_ref[...]
        r = lax.rsqrt(jnp.mean(xv * xv, axis=-1, keepdims=True) + NORM_EPS)
        xh = xv * r
        err = xh * wv - t_ref[...]
        dy = err * (1.0 / d)
        g = dy * wv
        dx_ref[...] = r * (g - xh * jnp.mean(g * xh, axis=-1, keepdims=True))

        @pl.when(pl.program_id(0) == 0)
        def _():
            l_ref[...] = jnp.zeros_like(l_ref)
            dw_ref[...] = jnp.zeros_like(dw_ref)

        l_ref[...] += 0.5 * jnp.sum(jnp.mean(err * err, axis=-1, keepdims=True), axis=0, keepdims=True)
        dw_ref[...] += jnp.sum(dy * xh, axis=0, keepdims=True)

    blk = pl.BlockSpec((tr, d), lambda i: (i, 0))
    row = pl.BlockSpec((1, d), lambda i: (0, 0))
    return pl.pallas_call(
        body, name=name, grid=(t // tr,), in_specs=[blk, row, blk],
        out_specs=[pl.BlockSpec((1, 1), lambda i: (0, 0)), blk, row],
        out_shape=[jax.ShapeDtypeStruct((1, 1), F32), jax.ShapeDtypeStruct((t, d), F32),
                   jax.ShapeDtypeStruct((1, d), F32)],
        compiler_params=_cp("arbitrary"),
    )(x3, fw, target)


def _rope(blk, tab_ref):
    return blk * tab_ref[0] + pltpu.roll(blk, 96, axis=1) * tab_ref[1] + pltpu.roll(blk, 32, axis=1) * tab_ref[2]


def _unrope(g, tab_ref):
    return g * tab_ref[0] + pltpu.roll(g * tab_ref[1], 32, axis=1) + pltpu.roll(g * tab_ref[2], 96, axis=1)


def _mla_prep(q, kv, proj, tabs, *, tr=512, name):
    t = q.shape[0]

    def body(q_ref, kv_ref, kr_ref, tab_ref, qo_ref, ko_ref, vo_ref):
        qv, kvv = q_ref[...], kv_ref[...]
        qo_ref[0, :, :MLA_NOPE] = qv[:, :MLA_NOPE].astype(BF16)
        qo_ref[0, :, MLA_NOPE:] = _rope(qv[:, MLA_NOPE:], tab_ref).astype(BF16)
        ko_ref[0, :, :MLA_NOPE] = kvv[:, :MLA_NOPE].astype(BF16)
        ko_ref[0, :, MLA_NOPE:] = _rope(kr_ref[...], tab_ref).astype(BF16)
        vo_ref[0] = kvv[:, MLA_NOPE:].astype(BF16)

    return pl.pallas_call(
        body, name=name, grid=(t // tr, MLA_HEADS),
        in_specs=[pl.BlockSpec((tr, MLA_QK_PAD), lambda i, h: (i, h)),
                  pl.BlockSpec((tr, MLA_NOPE + MLA_V), lambda i, h: (i, h)),
                  pl.BlockSpec((tr, LANES), lambda i, h: (i, OFF_KR // LANES)),
                  pl.BlockSpec((3, tr, LANES), lambda i, h: (0, i, 0))],
        out_specs=[pl.BlockSpec((1, tr, MLA_QK_PAD), lambda i, h: (h, i, 0)),
                   pl.BlockSpec((1, tr, MLA_QK_PAD), lambda i, h: (h, i, 0)),
                   pl.BlockSpec((1, tr, MLA_V), lambda i, h: (h, i, 0))],
        out_shape=[jax.ShapeDtypeStruct((MLA_HEADS, t, MLA_QK_PAD), BF16),
                   jax.ShapeDtypeStruct((MLA_HEADS, t, MLA_QK_PAD), BF16),
                   jax.ShapeDtypeStruct((MLA_HEADS, t, MLA_V), BF16)],
        compiler_params=_cp("parallel", "parallel"),
    )(q, kv, proj, tabs)


def _mla_unprep(dq3, dk3, dv3, tabs, *, tr=256, name):
    t = dq3.shape[1]

    def body(dq_ref, dk_ref, dv_ref, tab_ref, qo_ref, kvo_ref, kro_ref):
        kr = jnp.zeros((tr, LANES), F32)
        for h in range(MLA_HEADS):
            c0 = h * MLA_QK_PAD
            qo_ref[:, c0:c0 + MLA_NOPE] = dq_ref[h, :, :MLA_NOPE].astype(BF16)
            qo_ref[:, c0 + MLA_NOPE:c0 + MLA_QK_PAD] = _unrope(dq_ref[h, :, MLA_NOPE:], tab_ref).astype(BF16)
            kvo_ref[:, c0:c0 + MLA_NOPE] = dk_ref[h, :, :MLA_NOPE].astype(BF16)
            kvo_ref[:, c0 + MLA_NOPE:c0 + MLA_QK_PAD] = dv_ref[h].astype(BF16)
            kr = kr + dk_ref[h, :, MLA_NOPE:]
        kro_ref[...] = _unrope(kr, tab_ref).astype(BF16)

    return pl.pallas_call(
        body, name=name, grid=(t // tr,),
        in_specs=[pl.BlockSpec((MLA_HEADS, tr, MLA_QK_PAD), lambda i: (0, i, 0)),
                  pl.BlockSpec((MLA_HEADS, tr, MLA_QK_PAD), lambda i: (0, i, 0)),
                  pl.BlockSpec((MLA_HEADS, tr, MLA_V), lambda i: (0, i, 0)),
                  pl.BlockSpec((3, tr, LANES), lambda i: (0, i, 0))],
        out_specs=[pl.BlockSpec((tr, MLA_HEADS * MLA_QK_PAD), lambda i: (i, 0)),
                   pl.BlockSpec((tr, MLA_HEADS * MLA_QK_PAD), lambda i: (i, 0)),
                   pl.BlockSpec((tr, LANES), lambda i: (i, 0))],
        out_shape=[jax.ShapeDtypeStruct((t, MLA_HEADS * MLA_QK_PAD), BF16),
                   jax.ShapeDtypeStruct((t, MLA_HEADS * MLA_QK_PAD), BF16),
                   jax.ShapeDtypeStruct((t, LANES), BF16)],
        compiler_params=_cp("parallel"),
    )(dq3, dk3, dv3, tabs)


ATT_BLK = 256
ATT_SCALE = 1.0 / math.sqrt(MLA_NOPE + MLA_ROPE)
_NT = (((1,), (1,)), ((), ()))
_TN = (((0,), (0,)), ((), ()))


def _att_scores(q, k, diagonal):
    s = lax.dot_general(q, k, _NT, preferred_element_type=F32) * ATT_SCALE
    if not diagonal:
        return s
    row = lax.broadcasted_iota(jnp.int32, s.shape, 0)
    col = lax.broadcasted_iota(jnp.int32, s.shape, 1)
    return jnp.where((col >> 6) <= (row >> 6), s, NEG)


def _att_rows(i):
    return pl.ds(pl.multiple_of(i * ATT_BLK, ATT_BLK), ATT_BLK)


def _attn_fwd(q3, k3, v3, *, name):
    t = q3.shape[1]
    nq = t // ATT_BLK

    def body(q_ref, k_ref, v_ref, o_ref, lse_ref):
        qi = pl.program_id(1)
        q = q_ref[0]

        def step(j, carry, diagonal=False):
            m, l, acc = carry
            s = _att_scores(q, k_ref[0, _att_rows(j), :], diagonal)
            m_new = jnp.maximum(m, jnp.max(s, axis=-1, keepdims=True))
            p = jnp.exp(s - m_new)
            alpha = jnp.exp(m - m_new)
            l = alpha * l + jnp.sum(p, axis=-1, keepdims=True)
            acc = alpha * acc + jnp.dot(p.astype(BF16), v_ref[0, _att_rows(j), :], preferred_element_type=F32)
            return m_new, l, acc

        init = (jnp.full((ATT_BLK, 1), NEG, F32), jnp.zeros((ATT_BLK, 1), F32), jnp.zeros((ATT_BLK, MLA_V), F32))
        m, l, acc = step(qi, lax.fori_loop(0, qi, step, init), diagonal=True)
        o_ref[...] = acc / l
        lse_ref[0] = m + jnp.log(l)

    return pl.pallas_call(
        body, name=name, grid=(MLA_HEADS, nq),
        in_specs=[pl.BlockSpec((1, ATT_BLK, MLA_QK_PAD), lambda h, i: (h, i, 0)),
                  pl.BlockSpec((1, t, MLA_QK_PAD), lambda h, i: (h, 0, 0)),
                  pl.BlockSpec((1, t, MLA_V), lambda h, i: (h, 0, 0))],
        out_specs=[pl.BlockSpec((ATT_BLK, MLA_V), lambda h, i: (i, h)),
                   pl.BlockSpec((1, ATT_BLK, 1), lambda h, i: (h, i, 0))],
        out_shape=[jax.ShapeDtypeStruct((t, MLA_HEADS * MLA_V), F32), jax.ShapeDtypeStruct((MLA_HEADS, t, 1), F32)],
        compiler_params=_cp("parallel", "parallel"),
    )(q3, k3, v3)


def _attn_bwd(q3, k3, v3, o, dcat, lse, *, name):
    t = q3.shape[1]
    nq = t // ATT_BLK

    def body(q_ref, k_ref, v_ref, o_ref, do_ref, lse_ref, dq_ref, dk_ref, dv_ref, delta_ref):
        kj = pl.program_id(1)
        k, v = k_ref[0], v_ref[0]

        @pl.when(kj == 0)
        def _():
            dq_ref[...] = jnp.zeros_like(dq_ref)
            delta_ref[...] = jnp.sum(o_ref[...] * do_ref[...], axis=-1, keepdims=True)

        def step(i, carry, diagonal=False):
            dk, dv = carry
            rows = _att_rows(i)
            q = q_ref[0, rows, :]
            dob = do_ref[rows, :].astype(BF16)
            p = jnp.exp(_att_scores(q, k, diagonal) - lse_ref[0, rows, :])
            dv = dv + lax.dot_general(p.astype(BF16), dob, _TN, preferred_element_type=F32)
            dp = lax.dot_general(dob, v, _NT, preferred_element_type=F32)
            ds = (p * (dp - delta_ref[rows, :]) * ATT_SCALE).astype(BF16)
            dk = dk + lax.dot_general(ds, q, _TN, preferred_element_type=F32)
            dq_ref[0, rows, :] += jnp.dot(ds, k, preferred_element_type=F32)
            return dk, dv

        init = (jnp.zeros((ATT_BLK, MLA_QK_PAD), F32), jnp.zeros((ATT_BLK, MLA_V), F32))
        dk, dv = lax.fori_loop(kj + 1, nq, step, step(kj, init, diagonal=True))
        dk_ref[0] = dk
        dv_ref[0] = dv

    return pl.pallas_call(
        body, name=name, grid=(MLA_HEADS, nq),
        in_specs=[pl.BlockSpec((1, t, MLA_QK_PAD), lambda h, j: (h, 0, 0)),
                  pl.BlockSpec((1, ATT_BLK, MLA_QK_PAD), lambda h, j: (h, j, 0)),
                  pl.BlockSpec((1, ATT_BLK, MLA_V), lambda h, j: (h, j, 0)),
                  pl.BlockSpec((t, MLA_V), lambda h, j: (0, h)),
                  pl.BlockSpec((t, MLA_V), lambda h, j: (0, MLA_HEADS + h)),
                  pl.BlockSpec((1, t, 1), lambda h, j: (h, 0, 0))],
        out_specs=[pl.BlockSpec((1, t, MLA_QK_PAD), lambda h, j: (h, 0, 0)),
                   pl.BlockSpec((1, ATT_BLK, MLA_QK_PAD), lambda h, j: (h, j, 0)),
                   pl.BlockSpec((1, ATT_BLK, MLA_V), lambda h, j: (h, j, 0))],
        out_shape=[jax.ShapeDtypeStruct((MLA_HEADS, t, MLA_QK_PAD), F32),
                   jax.ShapeDtypeStruct((MLA_HEADS, t, MLA_QK_PAD), F32),
                   jax.ShapeDtypeStruct((MLA_HEADS, t, MLA_V), F32)],
        scratch_shapes=[pltpu.VMEM((t, 1), F32)],
        compiler_params=_cp("parallel", "arbitrary"),
    )(q3, k3, v3, o, dcat, lse)


def _ssd_prep(proj, bias128, alog128, *, name):
    t = proj.shape[0]
    nc = t // CHUNK

    def body(raw_ref, b_ref, al_ref, dt_ref, cs_ref, a_ref):
        xv = raw_ref[...] + b_ref[...]
        dt = jnp.maximum(xv, 0.0) + jnp.log(1.0 + jnp.exp(-jnp.abs(xv)))
        a = -jnp.exp(al_ref[...])
        adt = (dt * a).reshape(nc, CHUNK, LANES)
        li = lax.broadcasted_iota(jnp.int32, (nc, CHUNK, CHUNK), 1)
        si = lax.broadcasted_iota(jnp.int32, (nc, CHUNK, CHUNK), 2)
        tril = jnp.where(si <= li, 1.0, 0.0).astype(F32)
        cs = lax.dot_general(tril, adt, (((2,), (1,)), ((0,), (0,))), precision=HI, preferred_element_type=F32)
        dt_ref[...] = dt
        cs_ref[...] = cs.reshape(t, LANES)
        a_ref[...] = a

    blk = pl.BlockSpec((t, LANES), lambda i: (0, 0))
    row = pl.BlockSpec((1, LANES), lambda i: (0, 0))
    return pl.pallas_call(
        body, name=name, grid=(1,),
        in_specs=[pl.BlockSpec((t, LANES), lambda i: (0, OFF_DT // LANES)), row, row],
        out_specs=[blk, blk, row],
        out_shape=[jax.ShapeDtypeStruct((t, LANES), F32), jax.ShapeDtypeStruct((t, LANES), F32),
                   jax.ShapeDtypeStruct((1, LANES), F32)],
        compiler_params=_cp("arbitrary"),
    )(proj, bias128, alog128)


def _ssd_prep_bwd(ddt128, dadt128, proj, bias128, dt128, a128, dd_h, *, name):
    t = proj.shape[0]

    def body(ddt_ref, dadt_ref, raw_ref, b_ref, dt_ref, a_ref, dd_ref, draw_ref, db_ref, dal_ref, dds_ref):
        draw = ddt_ref[...] * _sigmoid(raw_ref[...] + b_ref[...])
        draw_ref[...] = draw.astype(BF16)
        db_ref[...] = jnp.sum(draw, axis=0, keepdims=True)
        dal_ref[...] = jnp.sum(dadt_ref[...] * dt_ref[...], axis=0, keepdims=True) * a_ref[...]
        dds_ref[...] = jnp.sum(dd_ref[...], axis=-1, keepdims=True)

    blk = pl.BlockSpec((t, LANES), lambda i: (0, 0))
    row = pl.BlockSpec((1, LANES), lambda i: (0, 0))
    return pl.pallas_call(
        body, name=name, grid=(1,),
        in_specs=[blk, blk, pl.BlockSpec((t, LANES), lambda i: (0, OFF_DT // LANES)), row, blk, row,
                  pl.BlockSpec((SSD_HEADS, SSD_P), lambda i: (0, 0))],
        out_specs=[blk, row, row, pl.BlockSpec((SSD_HEADS, 1), lambda i: (0, 0))],
        out_shape=[jax.ShapeDtypeStruct((t, LANES), BF16), jax.ShapeDtypeStruct((1, LANES), F32),
                   jax.ShapeDtypeStruct((1, LANES), F32), jax.ShapeDtypeStruct((SSD_HEADS, 1), F32)],
        compiler_params=_cp("arbitrary"),
    )(ddt128, dadt128, proj, bias128, dt128, a128, dd_h)


def _bdot(a, b, ca, cb, precision=None):
    return lax.dot_general(a, b, (((ca,), (cb,)), ((0,), (0,))), precision=precision, preferred_element_type=F32)


def _ssd_common(xs_ref, dt_ref, cs_ref, csr_ref, b_ref, c_ref, nc):
    x = xs_ref[0].reshape(nc, CHUNK, SSD_P)
    dt = dt_ref[0].reshape(nc, CHUNK, SSD_P)
    cs = cs_ref[0].reshape(nc, CHUNK, SSD_P)
    csr = csr_ref[0]
    bm = b_ref[0].reshape(nc, CHUNK, SSD_N).astype(BF16)
    cm = c_ref[0].reshape(nc, CHUNK, SSD_N).astype(BF16)
    li = lax.broadcasted_iota(jnp.int32, (nc, CHUNK, CHUNK), 1)
    si = lax.broadcasted_iota(jnp.int32, (nc, CHUNK, CHUNK), 2)
    lmat = jnp.exp(jnp.where(si <= li, cs - csr, NEG))
    g = _bdot(cm, bm, 2, 2)
    cs_last = jnp.sum(jnp.where(li == CHUNK - 1, cs, 0.0), axis=1, keepdims=True)
    xdt = x * dt
    dec = jnp.exp(cs_last - cs)
    return x, dt, cs, bm, cm, li, si, lmat, g, cs_last, xdt, dec


def _ssd_fwd(xs_h, dt_h, cs_h, cs_row, b_g, c_g, dskip_h, *, name):
    t = xs_h.shape[1]
    nc = t // CHUNK
    hpg = SSD_HEADS // SSD_GROUPS

    def body(xs_ref, dt_ref, cs_ref, csr_ref, b_ref, c_ref, dk_ref, y_ref, st_ref, sc_ref, cd_ref):
        x, dt, cs, bm, cm, li, si, lmat, g, cs_last, xdt, dec = _ssd_common(xs_ref, dt_ref, cs_ref, csr_ref, b_ref,
                                                                           c_ref, nc)
        yd = _bdot((g * lmat).astype(BF16), xdt.astype(BF16), 2, 1)
        sc_ref[...] = _bdot(bm, (dec * xdt).astype(BF16), 1, 1)
        cd_ref[...] = jnp.exp(cs_last)

        def step(c, s):
            st_ref[0, c] = s
            return s * cd_ref[c] + sc_ref[c]

        lax.fori_loop(0, nc, step, jnp.zeros((SSD_N, SSD_P), F32))
        yo = _bdot(cm, st_ref[0].astype(BF16), 2, 1) * jnp.exp(cs)
        y_ref[0] = (yd + yo + dk_ref[0] * x).reshape(t, SSD_P)

    head = pl.BlockSpec((1, t, SSD_P), lambda h: (h, 0, 0))
    grp = pl.BlockSpec((1, t, SSD_N), lambda h: (h // hpg, 0, 0))
    return pl.pallas_call(
        body, name=name, grid=(SSD_HEADS,),
        in_specs=[head, head, head, pl.BlockSpec((1, nc, 1, CHUNK), lambda h: (h, 0, 0, 0)), grp, grp,
                  pl.BlockSpec((1, 1, SSD_P), lambda h: (h, 0, 0))],
        out_specs=[head, pl.BlockSpec((1, nc, SSD_N, SSD_P), lambda h: (h, 0, 0, 0))],
        out_shape=[jax.ShapeDtypeStruct((SSD_HEADS, t, SSD_P), F32),
                   jax.ShapeDtypeStruct((SSD_HEADS, nc, SSD_N, SSD_P), F32)],
        scratch_shapes=[pltpu.VMEM((nc, SSD_N, SSD_P), F32), pltpu.VMEM((nc, 1, SSD_P), F32)],
        compiler_params=_cp("parallel"),
    )(xs_h, dt_h, cs_h, cs_row, b_g, c_g, dskip_h)


def _ssd_bwd(xs_h, dt_h, cs_h, cs_row, b_g, c_g, dskip_h, a_h, states, dy_h, *, name):
    t = xs_h.shape[1]
    nc = t // CHUNK
    hpg = SSD_HEADS // SSD_GROUPS

    def body(xs_ref, dt_ref, cs_ref, csr_ref, b_ref, c_ref, dk_ref, a_ref, st_ref, dy_ref,
             dxs_ref, ddt_ref, dadt_ref, db_ref, dc_ref, dd_ref, dsl_ref, dsc_ref, cd_ref):
        x, dt, cs, bm, cm, li, si, lmat, g, cs_last, xdt, dec = _ssd_common(xs_ref, dt_ref, cs_ref, csr_ref, b_ref,
                                                                           c_ref, nc)
        dy = dy_ref[0].reshape(nc, CHUNK, SSD_P)
        dyb = dy.astype(BF16)
        xdtb = xdt.astype(BF16)
        sprev = st_ref[0]
        sprevb = sprev.astype(BF16)
        cdec = jnp.exp(cs_last)
        ecs = jnp.exp(cs)
        dw = (ecs * dy).astype(BF16)
        wmat = _bdot(cm, sprevb, 2, 1)
        dcs = jnp.sum(dy * ecs * wmat, axis=2, keepdims=True)
        dcm = _bdot(dw, sprevb, 2, 2)
        dsl_ref[...] = _bdot(cm, dw, 1, 1)
        cd_ref[...] = cdec

        def step(k, ds):
            c = nc - 1 - k
            dsc_ref[c] = ds
            return ds * cd_ref[c] + dsl_ref[c]

        lax.fori_loop(0, nc, step, jnp.zeros((SSD_N, SSD_P), F32))
        dsc = dsc_ref[...]
        dscb = dsc.astype(BF16)
        d_last = jnp.sum(jnp.sum(dsc * sprev, axis=1, keepdims=True) * cdec, axis=2, keepdims=True)
        z = dec * xdt
        dbm = _bdot(z.astype(BF16), dscb, 2, 2)
        dz = _bdot(bm, dscb, 2, 1)
        dxdt = dec * dz
        t2 = jnp.sum(dz * z, axis=2, keepdims=True)
        dcs = dcs - t2
        d_last = d_last + jnp.sum(t2, axis=1, keepdims=True)
        m = g * lmat
        mb = m.astype(BF16)
        dm = _bdot(dyb, xdtb, 2, 2)
        dxdt = dxdt + _bdot(mb, dyb, 1, 1)
        dseg = dm * m
        dcs = dcs + jnp.sum(dseg, axis=2, keepdims=True)
        ones = jnp.ones((nc, CHUNK, SSD_P), F32)
        dcs = dcs - _bdot(dseg, ones, 1, 1, precision=HI)
        dg = (dm * lmat).astype(BF16)
        dcm = dcm + _bdot(dg, bm, 2, 1)
        dbm = dbm + _bdot(dg, cm, 1, 1)
        dcs = dcs + jnp.where(li[:, :, :SSD_P] == CHUNK - 1, d_last, 0.0)
        triu = jnp.where(li <= si, 1.0, 0.0).astype(F32)
        dadt = _bdot(triu, dcs, 2, 1, precision=HI)
        dk = dk_ref[0]
        dxs_ref[0] = (dxdt * dt + dk * dy).reshape(t, SSD_P)
        ddt_ref[0] = (jnp.sum(dxdt * x, axis=2, keepdims=True) + dadt * a_ref[0]).reshape(t, SSD_P)
        dadt_ref[0] = dadt.reshape(t, SSD_P)
        dd_ref[0] = jnp.sum(jnp.sum(dy * x, axis=1, keepdims=True), axis=0)

        @pl.when(pl.program_id(1) == 0)
        def _():
            db_ref[...] = jnp.zeros_like(db_ref)
            dc_ref[...] = jnp.zeros_like(dc_ref)

        db_ref[0] += dbm.reshape(t, SSD_N)
        dc_ref[0] += dcm.reshape(t, SSD_N)

    head = pl.BlockSpec((1, t, SSD_P), lambda gi, hi: (gi * hpg + hi, 0, 0))
    grp = pl.BlockSpec((1, t, SSD_N), lambda gi, hi: (gi, 0, 0))
    lane = pl.BlockSpec((1, 1, SSD_P), lambda gi, hi: (gi * hpg + hi, 0, 0))
    return pl.pallas_call(
        body, name=name, grid=(SSD_GROUPS, hpg),
        in_specs=[head, head, head, pl.BlockSpec((1, nc, 1, CHUNK), lambda gi, hi: (gi * hpg + hi, 0, 0, 0)),
                  grp, grp, lane, lane, pl.BlockSpec((1, nc, SSD_N, SSD_P), lambda gi, hi: (gi * hpg + hi, 0, 0, 0)),
                  head],
        out_specs=[head, head, head, grp, grp, lane],
        out_shape=[jax.ShapeDtypeStruct((SSD_HEADS, t, SSD_P), F32)] * 3
        + [jax.ShapeDtypeStruct((SSD_GROUPS, t, SSD_N), F32)] * 2
        + [jax.ShapeDtypeStruct((SSD_HEADS, 1, SSD_P), F32)],
        scratch_shapes=[pltpu.VMEM((nc, SSD_N, SSD_P), F32), pltpu.VMEM((nc, SSD_N, SSD_P), F32),
                        pltpu.VMEM((nc, 1, SSD_P), F32)],
        compiler_params=_cp("parallel", "arbitrary"),
    )(xs_h, dt_h, cs_h, cs_row, b_g, c_g, dskip_h, a_h, states, dy_h)


def _ssd_gate_fwd(y, proj, w, *, tr=256, name):
    t = y.shape[0]
    gw = D_SSM // SSD_GROUPS

    def body(y_ref, z_ref, w_ref, o_ref):
        v = y_ref[...] * _silu(z_ref[...])
        for gi in range(SSD_GROUPS):
            vg = v[:, gi * gw:(gi + 1) * gw]
            r = lax.rsqrt(jnp.mean(vg * vg, axis=-1, keepdims=True) + NORM_EPS)
            o_ref[:, gi * gw:(gi + 1) * gw] = (vg * r * w_ref[:, gi * gw:(gi + 1) * gw]).astype(BF16)

    blk = pl.BlockSpec((tr, D_SSM), lambda i: (i, 0))
    return pl.pallas_call(
        body, name=name, grid=(t // tr,), in_specs=[blk, blk, pl.BlockSpec((1, D_SSM), lambda i: (0, 0))],
        out_specs=blk, out_shape=jax.ShapeDtypeStruct((t, D_SSM), BF16), compiler_params=_cp("parallel"),
    )(y, proj, w)


def _ssd_gate_bwd(y, proj, w, dcat, *, tr=256, name):
    t = y.shape[0]
    gw = D_SSM // SSD_GROUPS

    def body(y_ref, z_ref, w_ref, d_ref, dy_ref, dz_ref, dw_ref):
        yv, zv, dv = y_ref[...], z_ref[...], d_ref[...].astype(F32)
        sz = _silu(zv)
        v = yv * sz

        @pl.when(pl.program_id(0) == 0)
        def _():
            dw_ref[...] = jnp.zeros_like(dw_ref)

        for gi in range(SSD_GROUPS):
            sl = slice(gi * gw, (gi + 1) * gw)
            vg, dg = v[:, sl], dv[:, sl]
            r = lax.rsqrt(jnp.mean(vg * vg, axis=-1, keepdims=True) + NORM_EPS)
            vh = vg * r
            gg = dg * w_ref[:, sl]
            dvg = r * (gg - vh * jnp.mean(gg * vh, axis=-1, keepdims=True))
            dy_ref[:, sl] = dvg * sz[:, sl]
            dz_ref[:, sl] = (dvg * yv[:, sl] * _dsilu(zv[:, sl])).astype(BF16)
            dw_ref[:, sl] += jnp.sum(dg * vh, axis=0, keepdims=True)

    blk = pl.BlockSpec((tr, D_SSM), lambda i: (i, 0))
    row = pl.BlockSpec((1, D_SSM), lambda i: (0, 0))
    return pl.pallas_call(
        body, name=name, grid=(t // tr,), in_specs=[blk, blk, row, blk], out_specs=[blk, blk, row],
        out_shape=[jax.ShapeDtypeStruct((t, D_SSM), F32), jax.ShapeDtypeStruct((t, D_SSM), BF16),
                   jax.ShapeDtypeStruct((1, D_SSM), F32)],
        compiler_params=_cp("arbitrary"),
    )(y, proj, w, dcat)


def _pad_lanes(v):
    return jnp.pad(v, ((0, 0), (0, LANES - v.shape[1])))


def _to_heads(v):
    return v.reshape(v.shape[0], SSD_HEADS, SSD_P).transpose(1, 0, 2)


def _from_heads(v):
    return v.transpose(1, 0, 2).reshape(v.shape[1], SSD_HEADS * SSD_P)


def _per_head(v128, t):
    return jnp.broadcast_to(v128[:, :SSD_HEADS].T[:, :, None], (SSD_HEADS, t, SSD_P))


def _ssd_forward(proj, conv_w, conv_b, dt_bias, a_log, d_skip, ssd_norm_w):
    t = proj.shape[0]
    nc = t // CHUNK
    xbc = _conv_act_fwd(proj, conv_w, conv_b, kw=SSD_CONV, glu=False, tc=512, coff=OFF_XBC // 512,
                        ncols=SSD_CONV_DIM, out_dtype=F32, name="ssd_conv_fwd")
    bias128, alog128 = _pad_lanes(dt_bias), _pad_lanes(a_log)
    dt128, cs128, a128 = _ssd_prep(proj, bias128, alog128, name="ssd_prep")
    dt_h, cs_h = _per_head(dt128, t), _per_head(cs128, t)
    cs_row = cs128[:, :SSD_HEADS].T.reshape(SSD_HEADS, nc, 1, CHUNK)
    xs_h = _to_heads(xbc[:, :D_SSM])
    gn = SSD_GROUPS * SSD_N
    b_g = xbc[:, D_SSM:D_SSM + gn].reshape(t, SSD_GROUPS, SSD_N).transpose(1, 0, 2)
    c_g = xbc[:, D_SSM + gn:].reshape(t, SSD_GROUPS, SSD_N).transpose(1, 0, 2)
    dskip_h = jnp.broadcast_to(d_skip[0][:, None, None], (SSD_HEADS, 1, SSD_P))
    a_h = jnp.broadcast_to(a128[0, :SSD_HEADS][:, None, None], (SSD_HEADS, 1, SSD_P))
    y_h, states = _ssd_fwd(xs_h, dt_h, cs_h, cs_row, b_g, c_g, dskip_h, name="ssd_scan_fwd")
    y = _from_heads(y_h)
    y_ssd = _ssd_gate_fwd(y, proj, ssd_norm_w, name="ssd_gate_fwd")
    saved = (proj, conv_w, conv_b, ssd_norm_w, bias128, dt128, a128, dt_h, cs_h, cs_row, xs_h, b_g, c_g, dskip_h, a_h,
             states, y)
    return y_ssd, saved


def _ssd_backward(saved, dcat):
    (proj, conv_w, conv_b, ssd_norm_w, bias128, dt128, a128, dt_h, cs_h, cs_row, xs_h, b_g, c_g, dskip_h, a_h, states,
     y) = saved
    t = proj.shape[0]
    dy, dz, d_norm_w = _ssd_gate_bwd(y, proj, ssd_norm_w, dcat, name="ssd_gate_bwd")
    dxs_h, ddt_h, dadt_h, db_g, dc_g, dd_h = _ssd_bwd(xs_h, dt_h, cs_h, cs_row, b_g, c_g, dskip_h, a_h, states,
                                                      _to_heads(dy), name="ssd_scan_bwd")
    gn = SSD_GROUPS * SSD_N
    dxc = jnp.concatenate([_from_heads(dxs_h), db_g.transpose(1, 0, 2).reshape(t, gn),
                           dc_g.transpose(1, 0, 2).reshape(t, gn)], axis=1)
    dxbc, d_conv_w, d_conv_b = _conv_act_bwd(proj, conv_w, conv_b, dxc, kw=SSD_CONV, glu=False, tc=512,
                                             coff=OFF_XBC // 512, ncols=SSD_CONV_DIM, name="ssd_conv_bwd")
    ddt128 = _pad_lanes(ddt_h[:, :, 0].T)
    dadt128 = _pad_lanes(dadt_h[:, :, 0].T)
    d_raw, d_bias, d_alog, d_dskip = _ssd_prep_bwd(ddt128, dadt128, proj, bias128, dt128, a128,
                                                   dd_h.reshape(SSD_HEADS, SSD_P), name="ssd_prep_bwd")
    return (dz, dxbc, d_raw, d_norm_w, d_conv_w, d_conv_b, d_bias[:, :SSD_HEADS], d_alog[:, :SSD_HEADS],
            d_dskip.reshape(1, SSD_HEADS))


def _rope_tables(positions):
    inv_freq = ROPE_THETA ** (-jnp.arange(0, MLA_ROPE, 2, dtype=F32) / MLA_ROPE)
    ang = positions[0].astype(F32)[:, None] * inv_freq
    cos, sin = jnp.cos(ang), jnp.sin(ang)
    z = jnp.zeros_like(cos)
    return jnp.stack([jnp.concatenate([cos, cos, z, z], axis=1), jnp.concatenate([-sin, z, z, z], axis=1),
                      jnp.concatenate([z, sin, z, z], axis=1)])


def _mla_forward(proj, tabs, q_a_norm_w, wq_pad, kv_a_norm_w, wkv):
    qn = _rmsnorm_fwd(proj, q_a_norm_w, width=MLA_Q_RANK, cblk=OFF_QA // MLA_Q_RANK, name="q_a_norm")
    q = _matmul(qn, wq_pad, name="q_b_proj")
    kvn = _rmsnorm_fwd(proj, kv_a_norm_w, width=MLA_KV_RANK, cblk=OFF_CKV // MLA_KV_RANK, name="kv_a_norm")
    kv = _matmul(kvn, wkv, name="kv_b_proj")
    q3, k3, v3 = _mla_prep(q, kv, proj, tabs, name="mla_prep")
    o, lse = _attn_fwd(q3, k3, v3, name="attn_fwd")
    return o, (proj, tabs, q_a_norm_w, wq_pad, kv_a_norm_w, wkv, qn, kvn, q3, k3, v3, o, lse)


def _mla_backward(saved, dcat):
    proj, tabs, q_a_norm_w, wq_pad, kv_a_norm_w, wkv, qn, kvn, q3, k3, v3, o, lse = saved
    dq3, dk3, dv3 = _attn_bwd(q3, k3, v3, o, dcat, lse, name="attn_bwd")
    dq, dkv, dkr = _mla_unprep(dq3, dk3, dv3, tabs, name="mla_unprep")
    d_wq = _matmul(qn, dq, ta=True, out_dtype=BF16, name="d_w_q_b")
    dqn = _matmul(dq, wq_pad, tb=True, name="d_qn")
    dq_a, d_qnw = _rmsnorm_bwd(proj, q_a_norm_w, dqn, width=MLA_Q_RANK, cblk=OFF_QA // MLA_Q_RANK, out_dtype=BF16,
                               name="q_a_norm_bwd")
    d_wkv = _matmul(kvn, dkv, ta=True, out_dtype=BF16, name="d_w_kv_b")
    dkvn = _matmul(dkv, wkv, tb=True, name="d_kvn")
    dckv, d_kvnw = _rmsnorm_bwd(proj, kv_a_norm_w, dkvn, width=MLA_KV_RANK, cblk=OFF_CKV // MLA_KV_RANK,
                                out_dtype=BF16, name="kv_a_norm_bwd")
    return dq_a, dckv, dkr, d_wq, d_wkv, d_qnw, d_kvnw


def _pad_w_q(w):
    r = w.shape[0]
    w3 = w.reshape(r, MLA_HEADS, MLA_NOPE + MLA_ROPE)
    return jnp.pad(w3, ((0, 0), (0, 0), (0, MLA_QK_PAD - MLA_NOPE - MLA_ROPE))).reshape(r, MLA_HEADS * MLA_QK_PAD)


def _unpad_w_q(w):
    r = w.shape[0]
    return w.reshape(r, MLA_HEADS, MLA_QK_PAD)[:, :, :MLA_NOPE + MLA_ROPE].reshape(r, MLA_HEADS * (MLA_NOPE + MLA_ROPE))


def _pad_w_in(w):
    r = w.shape[0]
    o_dt = D_SSM + SSD_CONV_DIM
    o_qa = o_dt + SSD_HEADS
    o_kr = o_qa + MLA_Q_RANK + MLA_KV_RANK
    zeros = lambda n: jnp.zeros((r, n), w.dtype)
    return jnp.concatenate([w[:, :o_dt], w[:, o_qa:o_kr], w[:, o_kr:], zeros(LANES - MLA_ROPE),
                            w[:, o_dt:o_qa], zeros(LANES - SSD_HEADS)], axis=1)


def _unpad_w_in(w):
    return jnp.concatenate([w[:, :OFF_QA], w[:, OFF_DT:OFF_DT + SSD_HEADS], w[:, OFF_QA:OFF_KR + MLA_ROPE]], axis=1)


WEIGHTS = ['mix_norm_w', 'w_in', 'conv_w', 'conv_b', 'dt_bias', 'a_log', 'd_skip', 'ssd_norm_w', 'q_a_norm_w', 'w_q_b',
           'kv_a_norm_w', 'w_kv_b', 'w_out', 'ffn_norm_w', 'w_ffn_up', 'ffn_conv_w', 'ffn_conv_b', 'w_ffn_down',
           'ple_norm_w', 'w_ple_gate', 'b_ple_gate', 'w_ple_proj', 'ple_post_norm_w', 'final_norm_w']
BIG = ['w_in', 'w_q_b', 'w_kv_b', 'w_out', 'w_ffn_up', 'w_ffn_down', 'w_ple_gate', 'w_ple_proj']
COL_SHARDED = ('w_in', 'w_q_b', 'w_kv_b', 'w_ffn_up', 'w_ple_proj')
CONV = ['conv_w', 'ffn_conv_w']
REPL = [n for n in WEIGHTS if n not in BIG and n not in CONV]
FFN_INV = tuple(int(i) for i in np.argsort(FFN_PERM))


def _cat_cols(g):
    return g.transpose(1, 0, 2).reshape(g.shape[1], N_DEV * g.shape[2])


def _split_cols(w):
    return w.reshape(w.shape[0], N_DEV, w.shape[1] // N_DEV).transpose(1, 0, 2)


def _interleave(v):
    r = v.shape[0]
    return v.reshape(r, N_DEV, FFN_TC)[:, jnp.array(FFN_PERM)].reshape(r, N_DEV * FFN_TC)


def _deinterleave(v):
    r = v.shape[0]
    return v.reshape(r, N_DEV, FFN_TC)[:, jnp.array(FFN_INV)].reshape(r, N_DEV * FFN_TC)


def _assemble_weights(g):
    layout = {
        'w_in': lambda v: _pad_w_in(_cat_cols(v)),
        'w_q_b': lambda v: _pad_w_q(_cat_cols(v)),
        'w_kv_b': _cat_cols,
        'w_out': lambda v: v.reshape(D_MODEL, D_MODEL),
        'w_ffn_up': lambda v: v,
        'w_ffn_down': lambda v: v.reshape(D_FF, D_MODEL),
        'w_ple_gate': lambda v: v.reshape(D_MODEL, D_MODEL),
        'w_ple_proj': _cat_cols,
        'conv_w': _cat_cols,
        'ffn_conv_w': lambda v: _interleave(_cat_cols(v)),
    }
    return {n: layout[n](v) for n, v in g.items()}


WEIGHT_GROUPS = {'a': ['w_in', 'w_q_b', 'w_kv_b', 'conv_w'], 'b': ['w_out'],
                 'c': ['w_ffn_up', 'ffn_conv_w', 'w_ffn_down', 'w_ple_gate', 'w_ple_proj']}
GRAD_GROUPS = {'p': ['w_ple_proj', 'w_ple_gate', 'w_ffn_down'], 'r': ['w_ffn_up'], 's': ['w_out'],
               't': ['w_q_b', 'w_kv_b', 'w_in']}


def _ffn_perm(j):
    return (j % 2) * (N_DEV // 2) + j // 2


def _local_step(x, p, tabs, get_w, s, target, emit, relay, settle):
    t = x.shape[0]
    s = dict(s)
    half = D_MODEL // 2
    up_cols = 2 * D_FF
    ffn_conv_b = _interleave(s['ffn_conv_b'])
    w = dict(get_w('a', None))
    h = _rmsnorm_fwd(x, s['mix_norm_w'], width=D_MODEL, name="mix_norm")
    proj = _matmul(h, w['w_in'], name="in_proj")
    y_ssd, ssd_saved = _ssd_forward(proj, w['conv_w'], s['conv_b'], s['dt_bias'], s['a_log'], s['d_skip'],
                                    s['ssd_norm_w'])
    o, mla_saved = _mla_forward(proj, tabs, s['q_a_norm_w'], w['w_q_b'], s['kv_a_norm_w'], w['w_kv_b'])
    tk_o, tn_o = _tile(half, MM_TK), _tile(D_MODEL, MM_TILE)
    w.update(get_w('b', o))
    x1 = _matmul(y_ssd, w['w_out'], add=x, mnk=(t, D_MODEL, half), name="out_proj_ssd")
    x1 = _matmul(o, w['w_out'], add=x1, mnk=(t, D_MODEL, half), name="out_proj_mla",
                 b_spec=pl.BlockSpec((tk_o, tn_o), lambda i, j, kk: (kk + half // tk_o, j)))
    hf = _rmsnorm_fwd(x1, s['ffn_norm_w'], width=D_MODEL, name="ffn_norm")
    w.update(get_w('c', hf))
    tk_u = _tile(D_MODEL, MM_TK)
    u = _matmul(hf, w['w_ffn_up'], mnk=(t, up_cols, D_MODEL), tn=FFN_TC, name="ffn_up",
                b_spec=pl.BlockSpec((1, tk_u, FFN_TC), lambda i, j, kk: (_ffn_perm(j), kk, 0)))
    act = _conv_act_fwd(u, w['ffn_conv_w'], ffn_conv_b, kw=FFN_CONV, glu=True, tc=2 * FFN_TC, coff=0, ncols=up_cols,
                        out_dtype=BF16, name="ffn_act")
    x2 = _matmul(act, w['w_ffn_down'], add=x1, name="ffn_down")
    hp = _rmsnorm_fwd(x2, s['ple_norm_w'], width=D_MODEL, name="ple_norm")
    gl = _matmul(hp, w['w_ple_gate'], bias=s['b_ple_gate'], name="ple_gate")
    pe = _matmul(p, w['w_ple_proj'], name="ple_proj")
    x3 = _ple_fwd(x2, gl, pe, s['ple_post_norm_w'], name="ple_mix")
    loss, dx3, d_final = _loss_head(x3, s['final_norm_w'], target, name="loss_head")
    dgl, d_bgate, dpe, d_post = _ple_bwd(dx3, gl, pe, s['ple_post_norm_w'], name="ple_mix_bwd")
    d_wproj = _matmul(p, dpe, ta=True, out_dtype=BF16, name="d_w_ple_proj")
    d_wgate = _matmul(hp, dgl, ta=True, out_dtype=BF16, name="d_w_ple_gate")
    dhp = _matmul(dgl, w['w_ple_gate'], tb=True, name="d_ple_normed")
    dx2, d_plenorm = _rmsnorm_bwd(x2, s['ple_norm_w'], dhp, dx3, width=D_MODEL, name="ple_norm_bwd")
    dact = _matmul(dx2, w['w_ffn_down'], tb=True, name="d_ffn_act")
    d_wdown = _matmul(act, dx2, ta=True, out_dtype=BF16, name="d_w_ffn_down")
    zz = emit('p', {'w_ple_proj': _split_cols(d_wproj), 'w_ple_gate': d_wgate.reshape(N_DEV, D_MODEL // N_DEV, D_MODEL),
                    'w_ffn_down': d_wdown.reshape(N_DEV, D_FF // N_DEV, D_MODEL)})
    du, d_fconv_w, d_fconv_b = _conv_act_bwd(u, w['ffn_conv_w'], ffn_conv_b + zz, dact, kw=FFN_CONV, glu=True,
                                             tc=2 * FFN_TC, coff=0, ncols=up_cols, name="ffn_act_bwd")
    zz = zz + relay('p', du)
    tm_u = _tile(D_MODEL, MM_TILE)
    d_wup = _matmul(hf, du, ta=True, out_dtype=BF16, mnk=(D_MODEL, up_cols, t), tn=FFN_TC, name="d_w_ffn_up",
                    o_spec=pl.BlockSpec((1, tm_u, FFN_TC), lambda i, j, kk: (_ffn_perm(j), i, 0)),
                    o_shape=(N_DEV, D_MODEL, FFN_TC))
    zz = zz + emit('r', {'w_ffn_up': d_wup})
    dhf = _matmul(du, w['w_ffn_up'], tb=True, mnk=(t, D_MODEL, up_cols), tk=FFN_TC, name="d_ffn_normed",
                  b_spec=pl.BlockSpec((1, tn_o, FFN_TC), lambda i, j, kk: (_ffn_perm(kk), j, 0)))
    zz = zz + relay('r', dhf) + settle('p')
    dx1, d_ffnnorm = _rmsnorm_bwd(x1, s['ffn_norm_w'] + zz, dhf, dx2, width=D_MODEL, name="ffn_norm_bwd")
    dcat = _matmul(dx1, w['w_out'], tb=True, name="d_mixed")
    d_wout = jnp.concatenate([_matmul(y_ssd, dx1, ta=True, out_dtype=BF16, name="d_w_out_ssd"),
                              _matmul(o, dx1, ta=True, out_dtype=BF16, name="d_w_out_mla")], axis=0)
    zz = zz + emit('s', {'w_out': d_wout.reshape(N_DEV, D_MODEL // N_DEV, D_MODEL)})
    ssd_saved = ssd_saved[:3] + (ssd_saved[3] + zz,) + ssd_saved[4:]
    dz, dxbc, d_raw, d_ssdnorm, d_conv_w, d_conv_b, d_dtb, d_alog, d_dskip = _ssd_backward(ssd_saved, dcat)
    zz = zz + relay('s', dz) + settle('r')
    mla_saved = mla_saved[:-1] + (mla_saved[-1] + zz,)
    dq_a, dckv, dkr, d_wq, d_wkv, d_qnorm, d_kvnorm = _mla_backward(mla_saved, dcat)
    d_raw = (d_raw + settle('s')).astype(BF16)
    dproj = jnp.concatenate([dz, dxbc, dq_a, dckv, dkr, d_raw], axis=1)
    d_win = _matmul(h, dproj, ta=True, out_dtype=BF16, name="d_w_in")
    dh = _matmul(dproj, w['w_in'], tb=True, name="d_in_normed")
    dx, d_mixnorm = _rmsnorm_bwd(x, s['mix_norm_w'], dh, dx1, width=D_MODEL, name="mix_norm_bwd")
    emit('t', {'w_in': _split_cols(_unpad_w_in(d_win)), 'w_q_b': _split_cols(_unpad_w_q(d_wq)),
               'w_kv_b': _split_cols(d_wkv)})
    relay('t', dx)
    settle('t')
    conv = {'conv_w': d_conv_w, 'ffn_conv_w': _deinterleave(d_fconv_w)}
    vec = {
        'mix_norm_w': d_mixnorm, 'conv_b': d_conv_b, 'dt_bias': d_dtb, 'a_log': d_alog, 'd_skip': d_dskip,
        'ssd_norm_w': d_ssdnorm, 'q_a_norm_w': d_qnorm, 'kv_a_norm_w': d_kvnorm, 'ffn_norm_w': d_ffnnorm,
        'ffn_conv_b': _deinterleave(d_fconv_b), 'ple_norm_w': d_plenorm, 'b_ple_gate': d_bgate,
        'ple_post_norm_w': d_post, 'final_norm_w': d_final,
    }
    return loss, dx, conv, vec


MESH = pl.DeviceIdType.MESH
FLIPS = ((0, 0, 1), (1, 0, 0), (0, 1, 0), (1, 1, 0), (1, 0, 1), (0, 1, 1), (1, 1, 1))


def _exchange(items, *, gather, name):
    n = len(items)

    def body(*refs):
        ins, outs = refs[:n], refs[n:2 * n]
        send_sems, recv_sems, local_sems = refs[2 * n:]
        x, y, c = lax.axis_index("x"), lax.axis_index("y"), lax.axis_index("c")
        me = 4 * x + 2 * y + c
        peers = [(jnp.where(fx, 1 - x, x), jnp.where(fy, 1 - y, y), jnp.where(fc, 1 - c, c)) for fx, fy, fc in FLIPS]
        slot = [4 * px + 2 * py + pc for px, py, pc in peers]
        local, sends = [], []
        for wi in range(n):
            cp = pltpu.make_async_copy(ins[wi] if gather else ins[wi].at[me], outs[wi].at[me], local_sems.at[wi])
            cp.start()
            local.append(cp)
            for k, peer in enumerate(peers):
                cp = pltpu.make_async_remote_copy(
                    src_ref=ins[wi] if gather else ins[wi].at[slot[k]], dst_ref=outs[wi].at[me],
                    send_sem=send_sems.at[k, wi], recv_sem=recv_sems.at[k, wi], device_id=peer, device_id_type=MESH)
                cp.start()
                sends.append(cp)
        for wi in range(n):
            for k, peer in enumerate(peers):
                pltpu.make_async_remote_copy(
                    src_ref=outs[wi].at[slot[k]], dst_ref=outs[wi].at[slot[k]], send_sem=send_sems.at[k, wi],
                    recv_sem=recv_sems.at[k, wi], device_id=peer, device_id_type=MESH).wait_recv()
        for cp in sends:
            cp.wait_send()
        for cp in local:
            cp.wait()

    hbm = pl.BlockSpec(memory_space=pltpu.HBM)
    out_shape = [jax.ShapeDtypeStruct(((N_DEV,) + v.shape) if gather else v.shape, v.dtype) for v in items]
    return pl.pallas_call(
        body, name=name, in_specs=[hbm] * n, out_specs=[hbm] * n, out_shape=out_shape,
        scratch_shapes=[pltpu.SemaphoreType.DMA((len(FLIPS), n)), pltpu.SemaphoreType.DMA((len(FLIPS), n)),
                        pltpu.SemaphoreType.DMA((n,))],
    )(*items)


HBM_SPEC = pl.BlockSpec(memory_space=pltpu.HBM)
SEM_SPEC = pl.BlockSpec(memory_space=pltpu.SEMAPHORE)
EFFECT = pltpu.SideEffectType.DATAFLOW_SIDE_EFFECTING


def _peers():
    x, y, c = lax.axis_index("x"), lax.axis_index("y"), lax.axis_index("c")
    peers = [(jnp.where(fx, 1 - x, x), jnp.where(fy, 1 - y, y), jnp.where(fc, 1 - c, c)) for fx, fy, fc in FLIPS]
    return 4 * x + 2 * y + c, peers, [4 * px + 2 * py + pc for px, py, pc in peers]


def _split_start(bufs, ncopies, plan, *, name):
    nb = len(bufs)

    def body(*refs):
        send_sems, recv_sems, token = refs[nb], refs[nb + 1], refs[2 * nb + 2]
        for i, (src, dst, peer, _) in enumerate(plan(refs[:nb])):
            pltpu.make_async_remote_copy(src_ref=src, dst_ref=dst, send_sem=send_sems.at[i], recv_sem=recv_sems.at[i],
                                         device_id=peer, device_id_type=MESH).start()
        token[...] = jnp.zeros_like(token)

    res = pl.pallas_call(
        body, name=name, in_specs=[HBM_SPEC] * nb,
        out_specs=[SEM_SPEC, SEM_SPEC] + [HBM_SPEC] * nb + [pl.BlockSpec(memory_space=pltpu.VMEM)],
        out_shape=[pltpu.SemaphoreType.DMA((ncopies,)), pltpu.SemaphoreType.DMA((ncopies,))]
        + [pltpu.HBM(v.shape, v.dtype) for v in bufs] + [jax.ShapeDtypeStruct((HALO, LANES), F32)],
        input_output_aliases={i: 2 + i for i in range(nb)},
        compiler_params=pltpu.CompilerParams(has_side_effects=EFFECT),
    )(*[pltpu.with_memory_space_constraint(v, pltpu.HBM) for v in bufs])
    return (res[0], res[1], list(res[2:2 + nb])), res[2 + nb]


def _split_wait(started, after, plan, local_plan, *, name):
    send_sems, recv_sems, bufs = started
    nb = len(bufs)
    nlocal = len(local_plan(bufs))

    def body(*refs):
        send_sems, recv_sems = refs[nb], refs[nb + 1]
        local_sems = refs[2 * nb + 3]
        local = []
        for j, (src, dst) in enumerate(local_plan(refs[:nb])):
            cp = pltpu.make_async_copy(src, dst, local_sems.at[j])
            cp.start()
            local.append(cp)
        for i, (src, _, peer, incoming) in enumerate(plan(refs[:nb])):
            cp = pltpu.make_async_remote_copy(src_ref=src, dst_ref=incoming, send_sem=send_sems.at[i],
                                              recv_sem=recv_sems.at[i], device_id=peer, device_id_type=MESH)
            cp.wait_send()
            cp.wait_recv()
        for cp in local:
            cp.wait()

    res = pl.pallas_call(
        body, name=name, in_specs=[HBM_SPEC] * nb + [SEM_SPEC, SEM_SPEC, pl.BlockSpec(memory_space=pl.ANY)],
        out_specs=[HBM_SPEC] * nb, out_shape=[pltpu.HBM(v.shape, v.dtype) for v in bufs],
        input_output_aliases={i: i for i in range(nb)},
        scratch_shapes=[pltpu.SemaphoreType.DMA((max(nlocal, 1),))],
        compiler_params=pltpu.CompilerParams(has_side_effects=EFFECT),
    )(*bufs, send_sems, recv_sems, after)
    return list(res)


def _place():
    x, y, c = lax.axis_index("x"), lax.axis_index("y"), lax.axis_index("c")
    others = [((1 - x, y, c), 2 * (1 - x) + y), ((x, 1 - y, c), 2 * x + 1 - y), ((1 - x, 1 - y, c), 2 * (1 - x) + 1 - y)]
    return 4 * x + 2 * y + c, 2 * x + y, c, (x, y, 1 - c), others


def _gather1_plan(n):
    def plan(refs):
        me, _, _, sibling, others = _place()
        out = []
        for wi in range(n):
            item, land = refs[wi], refs[n + wi]
            out.append((item, land.at[me], sibling, land.at[me + 1 - 2 * lax.axis_index("c")]))
            for peer, chip in others:
                out.append((item, land.at[me], peer, land.at[2 * chip + lax.axis_index("c")]))
        return out

    return plan


def _gather1_local(n):
    def plan(refs):
        me = _place()[0]
        return [(refs[wi], refs[n + wi].at[me]) for wi in range(n)]

    return plan


def _gather2_plan(n):
    def plan(refs):
        _, _, c, sibling, others = _place()
        out = []
        for wi in range(n):
            land = refs[wi]
            for _, chip in others:
                out.append((land.at[2 * chip + c], land.at[2 * chip + c], sibling, land.at[2 * chip + 1 - c]))
        return out

    return plan


def _gather_start(items, *, name):
    lands = [lax.empty((N_DEV,) + v.shape, v.dtype) for v in items]
    return _split_start(items + lands, 4 * len(items), _gather1_plan(len(items)), name=name)


def _gather_forward(started, after, *, name):
    n = len(started[2]) // 2
    bufs = _split_wait(started, after, _gather1_plan(n), _gather1_local(n), name=name + "_wait")
    return _split_start(bufs[n:], 3 * n, _gather2_plan(n), name=name + "_start")


def _gather_finish(started, after, *, name):
    n = len(started[2])
    return _split_wait(started, after, _gather2_plan(n), lambda refs: [], name=name)


def _handshake(peers):
    barrier = pltpu.get_barrier_semaphore()
    for peer in peers:
        pl.semaphore_signal(barrier, inc=1, device_id=peer, device_id_type=MESH)
    pl.semaphore_wait(barrier, len(peers))


def _remote(src, dst, send_sem, recv_sem, peer):
    return pltpu.make_async_remote_copy(src_ref=src, dst_ref=dst, send_sem=send_sem, recv_sem=recv_sem, device_id=peer,
                                        device_id_type=MESH)


def _sequencer_gather(items, *, collective_id, name):
    n = len(items)
    srcs = [jax.new_ref(v, memory_space=pltpu.MemorySpace.HBM) for v in items]
    lands = [jax.empty_ref(jax.ShapeDtypeStruct((N_DEV,) + v.shape, v.dtype), memory_space=pltpu.MemorySpace.HBM)
             for v in items]
    dma = pltpu.SemaphoreType.DMA

    @pl.kernel(mesh=plsc.ScalarSubcoreMesh(axis_name="sequencer", num_cores=1), name=name,
               scratch_types=(dma((4 * n,)), dma((4 * n,)), dma((3 * n,)), dma((3 * n,)), dma((n,))),
               compiler_params=pltpu.CompilerParams(collective_id=collective_id))
    def launch(send1, recv1, send2, recv2, local_sems):
        _, _, _, sibling, others = _place()
        _handshake([sibling] + [peer for peer, _ in others])
        hop1 = _gather1_plan(n)(srcs + lands)
        hop2 = _gather2_plan(n)(lands)
        local = [pltpu.make_async_copy(src, dst, local_sems.at[j])
                 for j, (src, dst) in enumerate(_gather1_local(n)(srcs + lands))]
        for cp in local:
            cp.start()
        for i, (src, dst, peer, _) in enumerate(hop1):
            _remote(src, dst, send1.at[i], recv1.at[i], peer).start()
        for wi in range(n):
            for j in range(3):
                i1, i2 = 4 * wi + 1 + j, 3 * wi + j
                src, _, peer, incoming = hop1[i1]
                _remote(src, incoming, send1.at[i1], recv1.at[i1], peer).wait_recv()
                src, dst, peer, _ = hop2[i2]
                _remote(src, dst, send2.at[i2], recv2.at[i2], peer).start()
        for wi in range(n):
            src, _, peer, incoming = hop1[4 * wi]
            _remote(src, incoming, send1.at[4 * wi], recv1.at[4 * wi], peer).wait_recv()
        for i, (src, _, peer, incoming) in enumerate(hop2):
            cp = _remote(src, incoming, send2.at[i], recv2.at[i], peer)
            cp.wait_send()
            cp.wait_recv()
        for i, (src, dst, peer, _) in enumerate(hop1):
            _remote(src, dst, send1.at[i], recv1.at[i], peer).wait_send()
        for cp in local:
            cp.wait()

    launch()
    return [land[...] for land in lands]


def _sequencer_exchange(sources, land_shapes, ncopies, plan, local_plan, peers, *, collective_id, name):
    srcs = [jax.new_ref(v, memory_space=pltpu.MemorySpace.HBM) for v in sources]
    lands = [jax.empty_ref(s, memory_space=pltpu.MemorySpace.HBM) for s in land_shapes]
    nlocal = len(local_plan(srcs + lands))
    dma = pltpu.SemaphoreType.DMA

    @pl.kernel(mesh=plsc.ScalarSubcoreMesh(axis_name="sequencer", num_cores=1), name=name,
               scratch_types=(dma((ncopies,)), dma((ncopies,)), dma((max(nlocal, 1),))),
               compiler_params=pltpu.CompilerParams(collective_id=collective_id))
    def launch(send_sems, recv_sems, local_sems):
        _handshake(peers(_place()))
        copies = plan(srcs + lands)
        local = [pltpu.make_async_copy(src, dst, local_sems.at[j])
                 for j, (src, dst) in enumerate(local_plan(srcs + lands))]
        for cp in local:
            cp.start()
        for i, (src, dst, peer, _) in enumerate(copies):
            _remote(src, dst, send_sems.at[i], recv_sems.at[i], peer).start()
        for i, (src, _, peer, incoming) in enumerate(copies):
            cp = _remote(src, incoming, send_sems.at[i], recv_sems.at[i], peer)
            cp.wait_send()
            cp.wait_recv()
        for cp in local:
            cp.wait()

    launch()
    return [land[...] for land in lands]


def _sequencer_scatter_hop1(parts, *, collective_id, name):
    n = len(parts)
    shapes = [jax.ShapeDtypeStruct((N_CHIP,) + v.shape[1:], v.dtype) for v in parts]
    return _sequencer_exchange(parts, shapes, N_CHIP * n, _scatter1_plan(n), lambda refs: [], lambda place: [place[3]],
                               collective_id=collective_id, name=name)


def _sequencer_scatter_hop2(sums, *, collective_id, name):
    n = len(sums)
    shapes = [jax.ShapeDtypeStruct(v.shape, v.dtype) for v in sums]
    return _sequencer_exchange(sums, shapes, 3 * n, _scatter2_plan(n), _scatter2_local(n),
                               lambda place: [peer for peer, _ in place[4]], collective_id=collective_id, name=name)


N_CHIP = N_DEV // 2


def _scatter1_plan(n):
    def plan(refs):
        _, _, c, sibling, _ = _place()
        out = []
        for wi in range(n):
            parts, half = refs[wi], refs[n + wi]
            for chip in range(N_CHIP):
                out.append((parts.at[2 * chip + 1 - c], half.at[chip], sibling, half.at[chip]))
        return out

    return plan


def _scatter2_plan(n):
    def plan(refs):
        _, my_chip, _, _, others = _place()
        out = []
        for wi in range(n):
            sums, recv = refs[wi], refs[n + wi]
            for peer, chip in others:
                out.append((sums.at[chip], recv.at[my_chip], peer, recv.at[chip]))
        return out

    return plan


def _scatter2_local(n):
    def plan(refs):
        my_chip = _place()[1]
        return [(refs[wi].at[my_chip], refs[n + wi].at[my_chip]) for wi in range(n)]

    return plan


def _pair_add(parts, half, core, *, name):
    _, r, c = parts.shape
    tr = max(d for d in range(HALO, 257, HALO) if r % d == 0) if r > 256 else r
    parts4 = parts.reshape(N_CHIP, 2, r, c)

    def body(core_ref, p_ref, h_ref, o_ref):
        o_ref[...] = (p_ref[:, 0].astype(F32) + h_ref[...].astype(F32)).astype(o_ref.dtype)

    return pl.pallas_call(
        body, name=name,
        grid_spec=pltpu.PrefetchScalarGridSpec(
            num_scalar_prefetch=1, grid=(r // tr,),
            in_specs=[pl.BlockSpec((N_CHIP, 1, tr, c), lambda i, core_ref: (0, core_ref[0], i, 0)),
                      pl.BlockSpec((N_CHIP, tr, c), lambda i, core_ref: (0, i, 0))],
            out_specs=pl.BlockSpec((N_CHIP, tr, c), lambda i, core_ref: (0, i, 0))),
        out_shape=jax.ShapeDtypeStruct((N_CHIP, r, c), parts.dtype), compiler_params=_cp("parallel"),
    )(core, parts4, half)


def _scatter_start(parts, *, name):
    halves = [lax.empty((N_CHIP,) + v.shape[1:], v.dtype) for v in parts]
    return _split_start(parts + halves, N_CHIP * len(parts), _scatter1_plan(len(parts)), name=name)


def _scatter_forward(started, after, core, *, name):
    n = len(started[2]) // 2
    bufs = _split_wait(started, after, _scatter1_plan(n), lambda refs: [], name=name + "_wait")
    sums = [_pair_add(bufs[wi], bufs[n + wi], core, name=name + "_add%d" % wi) for wi in range(n)]
    recvs = [lax.empty(v.shape, v.dtype) for v in sums]
    return _split_start(sums + recvs, 3 * n, _scatter2_plan(n), name=name + "_start")


def _scatter_finish(started, after, *, name):
    n = len(started[2]) // 2
    return _split_wait(started, after, _scatter2_plan(n), _scatter2_local(n), name=name)[n:]


def _adamw(parts, w, m, v, *, name):
    r, c = w.shape
    nparts = parts.shape[0]
    tr = max(d for d in range(HALO, 129, HALO) if r % d == 0) if r > 128 else r

    def body(p_ref, w_ref, m_ref, v_ref, g_ref, d_ref, mo_ref, vo_ref):
        g = p_ref[0].astype(F32)
        for k in range(1, nparts):
            g = g + p_ref[k].astype(F32)
        mn = ADAM_B1 * m_ref[...] + (1.0 - ADAM_B1) * g
        vn = ADAM_B2 * v_ref[...] + (1.0 - ADAM_B2) * (g * g)
        m_hat = mn / (1.0 - ADAM_B1 ** ADAM_STEP)
        v_hat = vn / (1.0 - ADAM_B2 ** ADAM_STEP)
        g_ref[...] = g
        d_ref[...] = -ADAM_LR * (m_hat / (jnp.sqrt(v_hat) + ADAM_EPS) + ADAM_WD * w_ref[...])
        mo_ref[...] = mn
        vo_ref[...] = vn

    blk = pl.BlockSpec((tr, c), lambda i: (i, 0))
    return pl.pallas_call(
        body, name=name, grid=(r // tr,), in_specs=[pl.BlockSpec((nparts, tr, c), lambda i: (0, i, 0)), blk, blk, blk],
        out_specs=[blk] * 4, out_shape=[jax.ShapeDtypeStruct((r, c), F32)] * 4, compiler_params=_cp("parallel"),
    )(parts, w, m, v)


def _pack_rows(vs, rows):
    lead = vs[0].shape[:-1] if vs[0].ndim > 1 else ()
    flat = jnp.concatenate(vs, axis=-1)
    pad = rows * LANES - flat.shape[-1]
    flat = jnp.pad(flat, [(0, 0)] * len(lead) + [(0, pad)])
    return flat.reshape(lead + (rows, LANES))


def kernel(x, p, positions, mix_norm_w, w_in, conv_w, conv_b, dt_bias, a_log, d_skip, ssd_norm_w, q_a_norm_w, w_q_b, kv_a_norm_w, w_kv_b, w_out, ffn_norm_w, w_ffn_up, ffn_conv_w, ffn_conv_b, w_ffn_down, ple_norm_w, w_ple_gate, b_ple_gate, w_ple_proj, ple_post_norm_w, final_norm_w, loss_target, m_mix_norm_w, m_w_in, m_conv_w, m_conv_b, m_dt_bias, m_a_log, m_d_skip, m_ssd_norm_w, m_q_a_norm_w, m_w_q_b, m_kv_a_norm_w, m_w_kv_b, m_w_out, m_ffn_norm_w, m_w_ffn_up, m_ffn_conv_w, m_ffn_conv_b, m_w_ffn_down, m_ple_norm_w, m_w_ple_gate, m_b_ple_gate, m_w_ple_proj, m_ple_post_norm_w, m_final_norm_w, v_mix_norm_w, v_w_in, v_conv_w, v_conv_b, v_dt_bias, v_a_log, v_d_skip, v_ssd_norm_w, v_q_a_norm_w, v_w_q_b, v_kv_a_norm_w, v_w_kv_b, v_w_out, v_ffn_norm_w, v_w_ffn_up, v_ffn_conv_w, v_ffn_conv_b, v_w_ffn_down, v_ple_norm_w, v_w_ple_gate, v_b_ple_gate, v_w_ple_proj, v_ple_post_norm_w, v_final_norm_w):
    given = dict(locals())
    shapes = {n: given[n].shape for n in WEIGHTS}
    w2 = {n: given[n].reshape(given[n].shape[-2:] if n in BIG or n in CONV else (1, -1)) for n in WEIGHTS}
    m2 = {n: given['m_' + n].reshape(w2[n].shape) for n in WEIGHTS}
    v2 = {n: given['v_' + n].reshape(w2[n].shape) for n in WEIGHTS}
    me = 4 * lax.axis_index("x") + 2 * lax.axis_index("y") + lax.axis_index("c")

    core = lax.axis_index("c").astype(jnp.int32).reshape(1)

    def shards(grp, zero):
        return [(w2[n] + zero).astype(BF16) if n in BIG else w2[n] + zero for n in WEIGHT_GROUPS[grp]]

    first, token = _gather_start(shards('a', 0.0), name="gather_a_hop1")
    first, token = _gather_forward(first, token, name="gather_a_hop2")
    zero = token[0, 0]
    later = _sequencer_gather(shards('b', zero) + shards('c', zero), collective_id=1, name="gather_later")
    later = dict(zip(WEIGHT_GROUPS['b'] + WEIGHT_GROUPS['c'], later))

    def get_w(grp, after):
        if grp == 'a':
            lands = dict(zip(WEIGHT_GROUPS[grp], _gather_finish(first, token, name="gather_a_done")))
        else:
            lands = {n: later[n] for n in WEIGHT_GROUPS[grp]}
        return _assemble_weights(lands)

    scatters = {}

    hop_ids = {grp: 2 + 2 * i for i, grp in enumerate(GRAD_GROUPS)}

    def zero_of(arrays):
        return sum(v[(0,) * v.ndim].astype(F32) * 0.0 for v in arrays)

    def emit(grp, grads):
        scatters[grp], tok = _scatter_start([grads[n] for n in GRAD_GROUPS[grp]], name="scatter_" + grp + "_hop1")
        return tok[0, 0]

    def relay(grp, after):
        n = len(GRAD_GROUPS[grp])
        bufs = _split_wait(scatters[grp], after, _scatter1_plan(n), lambda refs: [], name="scatter_" + grp + "_hop1_wait")
        sums = [_pair_add(bufs[i], bufs[n + i], core, name="scatter_%s_add%d" % (grp, i)) for i in range(n)]
        scatters[grp] = _sequencer_scatter_hop2(sums, collective_id=hop_ids[grp] + 1, name="scatter_" + grp + "_hop2")
        return zero_of(sums)

    out_g, out_d, out_m, out_v = {}, {}, {}, {}

    def settle(grp):
        for n, parts in zip(GRAD_GROUPS[grp], scatters[grp]):
            out_g[n], out_d[n], out_m[n], out_v[n] = _adamw(parts, w2[n], m2[n], v2[n], name="adamw_" + n)
        return zero_of([out_g[n] for n in GRAD_GROUPS[grp]])

    vecs = {n: w2[n] for n in REPL}
    vecs['mix_norm_w'] = vecs['mix_norm_w'] + zero
    loss, dx, g_conv, g_vec = _local_step(x[0], p[0, 0], _rope_tables(positions), get_w, vecs, loss_target[0], emit,
                                          relay, settle)
    n_small = sum(g_vec[n].shape[1] for n in REPL) + sum(g_conv[n].size for n in CONV) + 1
    rows_small = -(-n_small // (LANES * HALO)) * HALO
    small = _pack_rows([g_vec[n] for n in REPL] + [g_conv[n].reshape(1, -1) for n in CONV] + [loss], rows_small)

    small = small + zero_of([out_g[GRAD_GROUPS['t'][-1]]])
    all_small = _exchange([small], gather=True, name="gather_small_grads")[0].reshape(N_DEV, rows_small * LANES)
    pieces, off = [], 0
    for n in REPL:
        k = g_vec[n].shape[1]
        pieces.append(all_small[:, off:off + k])
        off += k
    for n in CONV:
        kw, cols = g_conv[n].shape
        full = all_small[:, off:off + kw * cols].reshape(N_DEV, kw, cols)
        mine = lax.dynamic_slice_in_dim(full, me * (cols // N_DEV), cols // N_DEV, axis=2)
        pieces.append(mine.reshape(N_DEV, kw * (cols // N_DEV)))
        off += kw * cols
    pieces.append(all_small[:, off:off + 1])
    small_names = REPL + CONV
    n_mine = sum(q.shape[1] for q in pieces)
    rows_mine = -(-n_mine // (LANES * HALO)) * HALO
    zero = jnp.zeros((1, 1), F32)
    packed = [_pack_rows([src[n].reshape(1, -1) for n in small_names] + [zero], rows_mine).reshape(rows_mine, LANES)
              for src in (w2, m2, v2)]
    sg, sd, sm, sv = _adamw(_pack_rows(pieces, rows_mine), *packed, name="adamw_small")
    off = 0
    for n in small_names:
        k = w2[n].size
        for dst, src in ((out_g, sg), (out_d, sd), (out_m, sm), (out_v, sv)):
            dst[n] = src.reshape(-1)[off:off + k].reshape(w2[n].shape)
        off += k
    total_loss = sg.reshape(-1)[off]

    outs = [total_loss, dx[None]]
    for res in (out_g, out_d, out_m, out_v):
        outs += [res[n].reshape(shapes[n]) for n in WEIGHTS]
    return tuple(outs)
```

```python
import functools
import math

import numpy as np
import jax
import jax.numpy as jnp
from jax import lax
from jax.experimental import pallas as pl
from jax.experimental.pallas import tpu as pltpu
from jax.experimental.pallas import tpu_sc as plsc

F32 = jnp.float32
BF16 = jnp.bfloat16
HI = lax.Precision.HIGHEST

D_MODEL = 2048
CHUNK = 64
D_SSM = 1024
SSD_P = 64
SSD_HEADS = 16
SSD_GROUPS = 2
SSD_N = 128
SSD_CONV = 4
SSD_CONV_DIM = D_SSM + 2 * SSD_GROUPS * SSD_N
MLA_HEADS = 8
MLA_NOPE = 128
MLA_ROPE = 64
MLA_V = 128
MLA_Q_RANK = 512
MLA_KV_RANK = 256
MLA_QK_PAD = 256
ROPE_THETA = 10000.0
D_FF = 5632
FFN_CONV = 3
PLE_DIM = 256
NORM_EPS = 1e-6
ADAM_LR, ADAM_B1, ADAM_B2, ADAM_EPS, ADAM_WD, ADAM_STEP = 0.001, 0.9, 0.999, 1e-08, 0.01, 10
N_DEV = 8

OFF_Z, OFF_XBC, OFF_QA, OFF_CKV, OFF_KR, OFF_DT, D_IN_PAD = 0, 1024, 2560, 3072, 3328, 3456, 3584
D_IN = 3408
LANES = 128
HALO = 8
VMEM_LIMIT = 56 * 1024 * 1024
FFN_TC = D_FF * 2 // N_DEV
FFN_PERM = (0, 4, 1, 5, 2, 6, 3, 7)
NEG = -1e30


def _cp(*sem):
    return pltpu.CompilerParams(dimension_semantics=tuple(sem), vmem_limit_bytes=VMEM_LIMIT)


def _tile(n, want):
    if n <= want:
        return n
    best = max(d for d in range(LANES, want + 1, LANES) if n % d == 0)
    return best


def _sigmoid(x):
    return 1.0 / (1.0 + jnp.exp(-x))


def _silu(x):
    return x * _sigmoid(x)


def _dsilu(x):
    s = _sigmoid(x)
    return s * (1.0 + x * (1.0 - s))


MM_TILE = 1408
MM_TK = 2816


def _matmul(a, b, *, ta=False, tb=False, out_dtype=F32, add=None, bias=None, tm=MM_TILE, tn=MM_TILE, tk=MM_TK, name,
            mnk=None, a_spec=None, b_spec=None, o_spec=None, o_shape=None):
    if mnk is None:
        m, k = (a.shape[1], a.shape[0]) if ta else a.shape
        n = b.shape[0] if tb else b.shape[1]
        assert k == (b.shape[1] if tb else b.shape[0])
    else:
        m, n, k = mnk
    tm, tn, tk = _tile(m, tm), _tile(n, tn), _tile(k, tk)
    nk = k // tk
    dims = (((0 if ta else 1,), (1 if tb else 0,)), ((), ()))

    def body(*refs):
        a_ref, b_ref = refs[0], refs[1]
        pos = 2
        add_ref = bias_ref = None
        if add is not None:
            add_ref = refs[pos]
            pos += 1
        if bias is not None:
            bias_ref = refs[pos]
            pos += 1
        o_ref = refs[pos]
        kk = pl.program_id(2)
        av = a_ref[...]
        bv = b_ref[...]
        av = av.reshape(av.shape[-2:]).astype(BF16)
        bv = bv.reshape(bv.shape[-2:]).astype(BF16)
        prod = lax.dot_general(av, bv, dims, preferred_element_type=F32)

        def finish(r):
            if bias_ref is not None:
                r = r + bias_ref[...]
            if add_ref is not None:
                r = r + add_ref[...].astype(F32)
            o_ref[...] = r.astype(out_dtype).reshape(o_ref.shape)

        if nk == 1:
            finish(prod)
        else:
            acc_ref = refs[pos + 1]

            @pl.when(kk == 0)
            def _():
                acc_ref[...] = prod

            @pl.when(kk > 0)
            def _():
                acc_ref[...] += prod

            @pl.when(kk == nk - 1)
            def _():
                finish(acc_ref[...])

    if a_spec is None:
        a_spec = (pl.BlockSpec((tk, tm), lambda i, j, kk: (kk, i)) if ta
                  else pl.BlockSpec((tm, tk), lambda i, j, kk: (i, kk)))
    if b_spec is None:
        b_spec = (pl.BlockSpec((tn, tk), lambda i, j, kk: (j, kk)) if tb
                  else pl.BlockSpec((tk, tn), lambda i, j, kk: (kk, j)))
    if o_spec is None:
        o_spec = pl.BlockSpec((tm, tn), lambda i, j, kk: (i, j))
    if o_shape is None:
        o_shape = (m, n)
    in_specs = [a_spec, b_spec]
    args = [a, b]
    if add is not None:
        in_specs.append(pl.BlockSpec((tm, tn), lambda i, j, kk: (i, j)))
        args.append(add)
    if bias is not None:
        in_specs.append(pl.BlockSpec((1, tn), lambda i, j, kk: (0, j)))
        args.append(bias)
    return pl.pallas_call(
        body, name=name, grid=(m // tm, n // tn, nk), in_specs=in_specs, out_specs=o_spec,
        out_shape=jax.ShapeDtypeStruct(o_shape, out_dtype),
        scratch_shapes=[pltpu.VMEM((tm, tn), F32)] if nk > 1 else [],
        compiler_params=_cp("parallel", "parallel", "arbitrary"),
    )(*args)


def _rmsnorm_fwd(x, w, *, width, cblk=0, out_dtype=BF16, tr=256, name):
    t = x.shape[0]

    def body(x_ref, w_ref, o_ref):
        xv = x_ref[...].astype(F32)
        r = lax.rsqrt(jnp.mean(xv * xv, axis=-1, keepdims=True) + NORM_EPS)
        o_ref[...] = (xv * r * w_ref[...]).astype(out_dtype)

    return pl.pallas_call(
        body, name=name, grid=(t // tr,),
        in_specs=[pl.BlockSpec((tr, width), lambda i: (i, cblk)), pl.BlockSpec((1, width), lambda i: (0, 0))],
        out_specs=pl.BlockSpec((tr, width), lambda i: (i, 0)),
        out_shape=jax.ShapeDtypeStruct((t, width), out_dtype),
        compiler_params=_cp("parallel"),
    )(x, w)


def _rmsnorm_bwd(x, w, dy, add=None, *, width, cblk=0, out_dtype=F32, tr=256, name):
    t = x.shape[0]

    def body(*refs):
        if add is None:
            x_ref, w_ref, dy_ref, dx_ref, dw_ref = refs
            add_ref = None
        else:
            x_ref, w_ref, dy_ref, add_ref, dx_ref, dw_ref = refs
        xv = x_ref[...].astype(F32)
        dyv = dy_ref[...].astype(F32)
        r = lax.rsqrt(jnp.mean(xv * xv, axis=-1, keepdims=True) + NORM_EPS)
        xh = xv * r
        g = dyv * w_ref[...]
        dx = r * (g - xh * jnp.mean(g * xh, axis=-1, keepdims=True))
        if add_ref is not None:
            dx = dx + add_ref[...].astype(F32)
        dx_ref[...] = dx.astype(out_dtype)

        @pl.when(pl.program_id(0) == 0)
        def _():
            dw_ref[...] = jnp.zeros_like(dw_ref)

        dw_ref[...] += jnp.sum(dyv * xh, axis=0, keepdims=True)

    in_specs = [pl.BlockSpec((tr, width), lambda i: (i, cblk)), pl.BlockSpec((1, width), lambda i: (0, 0)),
                pl.BlockSpec((tr, width), lambda i: (i, 0))]
    args = [x, w, dy]
    if add is not None:
        in_specs.append(pl.BlockSpec((tr, width), lambda i: (i, 0)))
        args.append(add)
    return pl.pallas_call(
        body, name=name, grid=(t // tr,), in_specs=in_specs,
        out_specs=[pl.BlockSpec((tr, width), lambda i: (i, 0)), pl.BlockSpec((1, width), lambda i: (0, 0))],
        out_shape=[jax.ShapeDtypeStruct((t, width), out_dtype), jax.ShapeDtypeStruct((1, width), F32)],
        compiler_params=_cp("arbitrary"),
    )(*args)


def _shift_down(prev_halo, cur, j):
    if j == 0:
        return cur
    ext = jnp.concatenate([prev_halo, cur], axis=0)
    return pltpu.roll(ext, j, axis=0)[HALO:]


def _shift_up(cur, next_halo, j):
    if j == 0:
        return cur
    ext = jnp.concatenate([cur, next_halo], axis=0)
    return pltpu.roll(ext, ext.shape[0] - j, axis=0)[:cur.shape[0]]


def _conv_rows(prev, cur, w, b, kw):
    shifted = [cur]
    out = b + w[kw - 1:kw] * cur
    for j in range(1, kw):
        sh = _shift_down(prev, cur, j)
        shifted.append(sh)
        out = out + w[kw - 1 - j:kw - j] * sh
    return out, shifted


def _act_fwd(c, glu):
    if glu:
        half = c.shape[1] // 2
        return _silu(c[:, :half]) * c[:, half:]
    return _silu(c)


def _act_bwd(c, dout, glu):
    if glu:
        half = c.shape[1] // 2
        g, up = c[:, :half], c[:, half:]
        return jnp.concatenate([dout * up * _dsilu(g), dout * _silu(g)], axis=1)
    return dout * _dsilu(c)


def _conv_act_fwd(u, w, b, *, kw, glu, tc, coff, ncols, out_dtype, tr=256, name):
    t = u.shape[0]
    nb = ncols // tc
    oc = tc // 2 if glu else tc

    def body(u_ref, uh_ref, w_ref, b_ref, o_ref):
        prev = jnp.where(pl.program_id(0) == 0, 0.0, uh_ref[...])
        c, _ = _conv_rows(prev, u_ref[...], w_ref[...], b_ref[...], kw)
        o_ref[...] = _act_fwd(c, glu).astype(out_dtype)

    return pl.pallas_call(
        body, name=name, grid=(t // tr, nb),
        in_specs=[pl.BlockSpec((tr, tc), lambda i, j: (i, j + coff)),
                  pl.BlockSpec((HALO, tc), lambda i, j: (jnp.maximum(i * (tr // HALO) - 1, 0), j + coff)),
                  pl.BlockSpec((kw, tc), lambda i, j: (0, j)), pl.BlockSpec((1, tc), lambda i, j: (0, j))],
        out_specs=pl.BlockSpec((tr, oc), lambda i, j: (i, j)),
        out_shape=jax.ShapeDtypeStruct((t, nb * oc), out_dtype),
        compiler_params=_cp("parallel", "parallel"),
    )(u, u, w, b)


def _conv_act_bwd(u, w, b, dout, *, kw, glu, tc, coff, ncols, tr=256, name):
    t = u.shape[0]
    nb = ncols // tc
    nt = t // tr
    oc = tc // 2 if glu else tc

    def body(u_ref, up_ref, un_ref, d_ref, dn_ref, w_ref, b_ref, du_ref, dw_ref, db_ref):
        i = pl.program_id(1)
        cur, nxt, wv, bv = u_ref[...], un_ref[...], w_ref[...], b_ref[...]
        prev = jnp.where(i == 0, 0.0, up_ref[...])
        c_cur, shifted = _conv_rows(prev, cur, wv, bv, kw)
        c_nxt, _ = _conv_rows(cur[tr - HALO:], nxt, wv, bv, kw)
        d_cur = _act_bwd(c_cur, d_ref[...].astype(F32), glu)
        d_nxt = _act_bwd(c_nxt, jnp.where(i == nt - 1, 0.0, dn_ref[...].astype(F32)), glu)
        du = wv[kw - 1:kw] * d_cur
        for j in range(1, kw):
            du = du + wv[kw - 1 - j:kw - j] * _shift_up(d_cur, d_nxt, j)
        du_ref[...] = du.astype(BF16)

        @pl.when(i == 0)
        def _():
            dw_ref[...] = jnp.zeros_like(dw_ref)
            db_ref[...] = jnp.zeros_like(db_ref)

        db_ref[...] += jnp.sum(d_cur, axis=0, keepdims=True)
        dw_ref[...] += jnp.concatenate(
            [jnp.sum(d_cur * shifted[kw - 1 - k], axis=0, keepdims=True) for k in range(kw)], axis=0)

    nh = tr // HALO
    return pl.pallas_call(
        body, name=name, grid=(nb, nt),
        in_specs=[pl.BlockSpec((tr, tc), lambda j, i: (i, j + coff)),
                  pl.BlockSpec((HALO, tc), lambda j, i: (jnp.maximum(i * nh - 1, 0), j + coff)),
                  pl.BlockSpec((HALO, tc), lambda j, i: (jnp.minimum((i + 1) * nh, t // HALO - 1), j + coff)),
                  pl.BlockSpec((tr, oc), lambda j, i: (i, j)),
                  pl.BlockSpec((HALO, oc), lambda j, i: (jnp.minimum((i + 1) * nh, t // HALO - 1), j)),
                  pl.BlockSpec((kw, tc), lambda j, i: (0, j)), pl.BlockSpec((1, tc), lambda j, i: (0, j))],
        out_specs=[pl.BlockSpec((tr, tc), lambda j, i: (i, j)), pl.BlockSpec((kw, tc), lambda j, i: (0, j)),
                   pl.BlockSpec((1, tc), lambda j, i: (0, j))],
        out_shape=[jax.ShapeDtypeStruct((t, ncols), BF16), jax.ShapeDtypeStruct((kw, ncols), F32),
                   jax.ShapeDtypeStruct((1, ncols), F32)],
        compiler_params=_cp("parallel", "arbitrary"),
    )(u, u, u, dout, dout, w, b)


def _ple_fwd(x2, gl, pe, pw, *, tr=256, name):
    t, d = x2.shape

    def body(x_ref, gl_ref, pe_ref, pw_ref, o_ref):
        pv = pe_ref[...]
        r = lax.rsqrt(jnp.mean(pv * pv, axis=-1, keepdims=True) + NORM_EPS)
        o_ref[...] = x_ref[...] + _sigmoid(gl_ref[...]) * (pv * r * pw_ref[...])

    blk = pl.BlockSpec((tr, d), lambda i: (i, 0))
    return pl.pallas_call(
        body, name=name, grid=(t // tr,), in_specs=[blk, blk, blk, pl.BlockSpec((1, d), lambda i: (0, 0))],
        out_specs=blk, out_shape=jax.ShapeDtypeStruct((t, d), F32), compiler_params=_cp("parallel"),
    )(x2, gl, pe, pw)


def _ple_bwd(dx3, gl, pe, pw, *, tr=256, name):
    t, d = dx3.shape

    def body(dx_ref, gl_ref, pe_ref, pw_ref, dgl_ref, db_ref, dpe_ref, dpw_ref):
        dx, pv, pwv = dx_ref[...], pe_ref[...], pw_ref[...]
        gate = _sigmoid(gl_ref[...])
        r = lax.rsqrt(jnp.mean(pv * pv, axis=-1, keepdims=True) + NORM_EPS)
        ph = pv * r
        dgl = dx * (ph * pwv) * gate * (1.0 - gate)
        de = dx * gate
        g = de * pwv
        dgl_ref[...] = dgl.astype(BF16)
        dpe_ref[...] = (r * (g - ph * jnp.mean(g * ph, axis=-1, keepdims=True))).astype(BF16)

        @pl.when(pl.program_id(0) == 0)
        def _():
            db_ref[...] = jnp.zeros_like(db_ref)
            dpw_ref[...] = jnp.zeros_like(dpw_ref)

        db_ref[...] += jnp.sum(dgl, axis=0, keepdims=True)
        dpw_ref[...] += jnp.sum(de * ph, axis=0, keepdims=True)

    blk = pl.BlockSpec((tr, d), lambda i: (i, 0))
    row = pl.BlockSpec((1, d), lambda i: (0, 0))
    return pl.pallas_call(
        body, name=name, grid=(t // tr,), in_specs=[blk, blk, blk, row], out_specs=[blk, row, blk, row],
        out_shape=[jax.ShapeDtypeStruct((t, d), BF16), jax.ShapeDtypeStruct((1, d), F32),
                   jax.ShapeDtypeStruct((t, d), BF16), jax.ShapeDtypeStruct((1, d), F32)],
        compiler_params=_cp("arbitrary"),
    )(dx3, gl, pe, pw)


def _loss_head(x3, fw, target, *, tr=256, name):
    t, d = x3.shape

    def body(x_ref, w_ref, t_ref, l_ref, dx_ref, dw_ref):
        xv, wv = x_ref[...], w_ref[...]
        r = lax.rsqrt(jnp.mean(xv * xv, axis=-1, keepdims=True) + NORM_EPS)
        xh = xv * r
        err = xh * wv - t_ref[...]
        dy = err * (1.0 / d)
        g = dy * wv
        dx_ref[...] = r * (g - xh * jnp.mean(g * xh, axis=-1, keepdims=True))

        @pl.when(pl.program_id(0) == 0)
        def _():
            l_ref[...] = jnp.zeros_like(l_ref)
            dw_ref[...] = jnp.zeros_like(dw_ref)

        l_ref[...] += 0.5 * jnp.sum(jnp.mean(err * err, axis=-1, keepdims=True), axis=0, keepdims=True)
        dw_ref[...] += jnp.sum(dy * xh, axis=0, keepdims=True)

    blk = pl.BlockSpec((tr, d), lambda i: (i, 0))
    row = pl.BlockSpec((1, d), lambda i: (0, 0))
    return pl.pallas_call(
        body, name=name, grid=(t // tr,), in_specs=[blk, row, blk],
        out_specs=[pl.BlockSpec((1, 1), lambda i: (0, 0)), blk, row],
        out_shape=[jax.ShapeDtypeStruct((1, 1), F32), jax.ShapeDtypeStruct((t, d), F32),
                   jax.ShapeDtypeStruct((1, d), F32)],
        compiler_params=_cp("arbitrary"),
    )(x3, fw, target)


def _rope(blk, tab_ref):
    return blk * tab_ref[0] + pltpu.roll(blk, 96, axis=1) * tab_ref[1] + pltpu.roll(blk, 32, axis=1) * tab_ref[2]


def _unrope(g, tab_ref):
    return g * tab_ref[0] + pltpu.roll(g * tab_ref[1], 32, axis=1) + pltpu.roll(g * tab_ref[2], 96, axis=1)


def _mla_prep(q, kv, proj, tabs, *, tr=512, name):
    t = q.shape[0]

    def body(q_ref, kv_ref, kr_ref, tab_ref, qo_ref, ko_ref, vo_ref):
        qv, kvv = q_ref[...], kv_ref[...]
        qo_ref[0, :, :MLA_NOPE] = qv[:, :MLA_NOPE].astype(BF16)
        qo_ref[0, :, MLA_NOPE:] = _rope(qv[:, MLA_NOPE:], tab_ref).astype(BF16)
        ko_ref[0, :, :MLA_NOPE] = kvv[:, :MLA_NOPE].astype(BF16)
        ko_ref[0, :, MLA_NOPE:] = _rope(kr_ref[...], tab_ref).astype(BF16)
        vo_ref[0] = kvv[:, MLA_NOPE:].astype(BF16)

    return pl.pallas_call(
        body, name=name, grid=(t // tr, MLA_HEADS),
        in_specs=[pl.BlockSpec((tr, MLA_QK_PAD), lambda i, h: (i, h)),
                  pl.BlockSpec((tr, MLA_NOPE + MLA_V), lambda i, h: (i, h)),
                  pl.BlockSpec((tr, LANES), lambda i, h: (i, OFF_KR // LANES)),
                  pl.BlockSpec((3, tr, LANES), lambda i, h: (0, i, 0))],
        out_specs=[pl.BlockSpec((1, tr, MLA_QK_PAD), lambda i, h: (h, i, 0)),
                   pl.BlockSpec((1, tr, MLA_QK_PAD), lambda i, h: (h, i, 0)),
                   pl.BlockSpec((1, tr, MLA_V), lambda i, h: (h, i, 0))],
        out_shape=[jax.ShapeDtypeStruct((MLA_HEADS, t, MLA_QK_PAD), BF16),
                   jax.ShapeDtypeStruct((MLA_HEADS, t, MLA_QK_PAD), BF16),
                   jax.ShapeDtypeStruct((MLA_HEADS, t, MLA_V), BF16)],
        compiler_params=_cp("parallel", "parallel"),
    )(q, kv, proj, tabs)


def _mla_unprep(dq3, dk3, dv3, tabs, *, tr=256, name):
    t = dq3.shape[1]

    def body(dq_ref, dk_ref, dv_ref, tab_ref, qo_ref, kvo_ref, kro_ref):
        kr = jnp.zeros((tr, LANES), F32)
        for h in range(MLA_HEADS):
            c0 = h * MLA_QK_PAD
            qo_ref[:, c0:c0 + MLA_NOPE] = dq_ref[h, :, :MLA_NOPE].astype(BF16)
            qo_ref[:, c0 + MLA_NOPE:c0 + MLA_QK_PAD] = _unrope(dq_ref[h, :, MLA_NOPE:], tab_ref).astype(BF16)
            kvo_ref[:, c0:c0 + MLA_NOPE] = dk_ref[h, :, :MLA_NOPE].astype(BF16)
            kvo_ref[:, c0 + MLA_NOPE:c0 + MLA_QK_PAD] = dv_ref[h].astype(BF16)
            kr = kr + dk_ref[h, :, MLA_NOPE:]
        kro_ref[...] = _unrope(kr, tab_ref).astype(BF16)

    return pl.pallas_call(
        body, name=name, grid=(t // tr,),
        in_specs=[pl.BlockSpec((MLA_HEADS, tr, MLA_QK_PAD), lambda i: (0, i, 0)),
                  pl.BlockSpec((MLA_HEADS, tr, MLA_QK_PAD), lambda i: (0, i, 0)),
                  pl.BlockSpec((MLA_HEADS, tr, MLA_V), lambda i: (0, i, 0)),
                  pl.BlockSpec((3, tr, LANES), lambda i: (0, i, 0))],
        out_specs=[pl.BlockSpec((tr, MLA_HEADS * MLA_QK_PAD), lambda i: (i, 0)),
                   pl.BlockSpec((tr, MLA_HEADS * MLA_QK_PAD), lambda i: (i, 0)),
                   pl.BlockSpec((tr, LANES), lambda i: (i, 0))],
        out_shape=[jax.ShapeDtypeStruct((t, MLA_HEADS * MLA_QK_PAD), BF16),
                   jax.ShapeDtypeStruct((t, MLA_HEADS * MLA_QK_PAD), BF16),
                   jax.ShapeDtypeStruct((t, LANES), BF16)],
        compiler_params=_cp("parallel"),
    )(dq3, dk3, dv3, tabs)


ATT_BLK = 256
ATT_SCALE = 1.0 / math.sqrt(MLA_NOPE + MLA_ROPE)
_NT = (((1,), (1,)), ((), ()))
_TN = (((0,), (0,)), ((), ()))


def _att_scores(q, k, diagonal):
    s = lax.dot_general(q, k, _NT, preferred_element_type=F32) * ATT_SCALE
    if not diagonal:
        return s
    row = lax.broadcasted_iota(jnp.int32, s.shape, 0)
    col = lax.broadcasted_iota(jnp.int32, s.shape, 1)
    return jnp.where((col >> 6) <= (row >> 6), s, NEG)


def _att_rows(i):
    return pl.ds(pl.multiple_of(i * ATT_BLK, ATT_BLK), ATT_BLK)


def _attn_fwd(q3, k3, v3, *, name):
    t = q3.shape[1]
    nq = t // ATT_BLK

    def body(q_ref, k_ref, v_ref, o_ref, lse_ref):
        qi = pl.program_id(1)
        q = q_ref[0]

        def step(j, carry, diagonal=False):
            m, l, acc = carry
            s = _att_scores(q, k_ref[0, _att_rows(j), :], diagonal)
            m_new = jnp.maximum(m, jnp.max(s, axis=-1, keepdims=True))
            p = jnp.exp(s - m_new)
            alpha = jnp.exp(m - m_new)
            l = alpha * l + jnp.sum(p, axis=-1, keepdims=True)
            acc = alpha * acc + jnp.dot(p.astype(BF16), v_ref[0, _att_rows(j), :], preferred_element_type=F32)
            return m_new, l, acc

        init = (jnp.full((ATT_BLK, 1), NEG, F32), jnp.zeros((ATT_BLK, 1), F32), jnp.zeros((ATT_BLK, MLA_V), F32))
        m, l, acc = step(qi, lax.fori_loop(0, qi, step, init), diagonal=True)
        o_ref[...] = acc / l
        lse_ref[0] = m + jnp.log(l)

    return pl.pallas_call(
        body, name=name, grid=(MLA_HEADS, nq),
        in_specs=[pl.BlockSpec((1, ATT_BLK, MLA_QK_PAD), lambda h, i: (h, i, 0)),
                  pl.BlockSpec((1, t, MLA_QK_PAD), lambda h, i: (h, 0, 0)),
                  pl.BlockSpec((1, t, MLA_V), lambda h, i: (h, 0, 0))],
        out_specs=[pl.BlockSpec((ATT_BLK, MLA_V), lambda h, i: (i, h)),
                   pl.BlockSpec((1, ATT_BLK, 1), lambda h, i: (h, i, 0))],
        out_shape=[jax.ShapeDtypeStruct((t, MLA_HEADS * MLA_V), F32), jax.ShapeDtypeStruct((MLA_HEADS, t, 1), F32)],
        compiler_params=_cp("parallel", "parallel"),
    )(q3, k3, v3)


def _attn_bwd(q3, k3, v3, o, dcat, lse, *, name):
    t = q3.shape[1]
    nq = t // ATT_BLK

    def body(q_ref, k_ref, v_ref, o_ref, do_ref, lse_ref, dq_ref, dk_ref, dv_ref, delta_ref):
        kj = pl.program_id(1)
        k, v = k_ref[0], v_ref[0]

        @pl.when(kj == 0)
        def _():
            dq_ref[...] = jnp.zeros_like(dq_ref)
            delta_ref[...] = jnp.sum(o_ref[...] * do_ref[...], axis=-1, keepdims=True)

        def step(i, carry, diagonal=False):
            dk, dv = carry
            rows = _att_rows(i)
            q = q_ref[0, rows, :]
            dob = do_ref[rows, :].astype(BF16)
            p = jnp.exp(_att_scores(q, k, diagonal) - lse_ref[0, rows, :])
            dv = dv + lax.dot_general(p.astype(BF16), dob, _TN, preferred_element_type=F32)
            dp = lax.dot_general(dob, v, _NT, preferred_element_type=F32)
            ds = (p * (dp - delta_ref[rows, :]) * ATT_SCALE).astype(BF16)
            dk = dk + lax.dot_general(ds, q, _TN, preferred_element_type=F32)
            dq_ref[0, rows, :] += jnp.dot(ds, k, preferred_element_type=F32)
            return dk, dv

        init = (jnp.zeros((ATT_BLK, MLA_QK_PAD), F32), jnp.zeros((ATT_BLK, MLA_V), F32))
        dk, dv = lax.fori_loop(kj + 1, nq, step, step(kj, init, diagonal=True))
        dk_ref[0] = dk
        dv_ref[0] = dv

    return pl.pallas_call(
        body, name=name, grid=(MLA_HEADS, nq),
        in_specs=[pl.BlockSpec((1, t, MLA_QK_PAD), lambda h, j: (h, 0, 0)),
                  pl.BlockSpec((1, ATT_BLK, MLA_QK_PAD), lambda h, j: (h, j, 0)),
                  pl.BlockSpec((1, ATT_BLK, MLA_V), lambda h, j: (h, j, 0)),
                  pl.BlockSpec((t, MLA_V), lambda h, j: (0, h)),
                  pl.BlockSpec((t, MLA_V), lambda h, j: (0, MLA_HEADS + h)),
                  pl.BlockSpec((1, t, 1), lambda h, j: (h, 0, 0))],
        out_specs=[pl.BlockSpec((1, t, MLA_QK_PAD), lambda h, j: (h, 0, 0)),
                   pl.BlockSpec((1, ATT_BLK, MLA_QK_PAD), lambda h, j: (h, j, 0)),
                   pl.BlockSpec((1, ATT_BLK, MLA_V), lambda h, j: (h, j, 0))],
        out_shape=[jax.ShapeDtypeStruct((MLA_HEADS, t, MLA_QK_PAD), F32),
                   jax.ShapeDtypeStruct((MLA_HEADS, t, MLA_QK_PAD), F32),
                   jax.ShapeDtypeStruct((MLA_HEADS, t, MLA_V), F32)],
        scratch_shapes=[pltpu.VMEM((t, 1), F32)],
        compiler_params=_cp("parallel", "arbitrary"),
    )(q3, k3, v3, o, dcat, lse)


def _ssd_prep(proj, bias128, alog128, *, name):
    t = proj.shape[0]
    nc = t // CHUNK

    def body(raw_ref, b_ref, al_ref, dt_ref, cs_ref, a_ref):
        xv = raw_ref[...] + b_ref[...]
        dt = jnp.maximum(xv, 0.0) + jnp.log(1.0 + jnp.exp(-jnp.abs(xv)))
        a = -jnp.exp(al_ref[...])
        adt = (dt * a).reshape(nc, CHUNK, LANES)
        li = lax.broadcasted_iota(jnp.int32, (nc, CHUNK, CHUNK), 1)
        si = lax.broadcasted_iota(jnp.int32, (nc, CHUNK, CHUNK), 2)
        tril = jnp.where(si <= li, 1.0, 0.0).astype(F32)
        cs = lax.dot_general(tril, adt, (((2,), (1,)), ((0,), (0,))), precision=HI, preferred_element_type=F32)
        dt_ref[...] = dt
        cs_ref[...] = cs.reshape(t, LANES)
        a_ref[...] = a

    blk = pl.BlockSpec((t, LANES), lambda i: (0, 0))
    row = pl.BlockSpec((1, LANES), lambda i: (0, 0))
    return pl.pallas_call(
        body, name=name, grid=(1,),
        in_specs=[pl.BlockSpec((t, LANES), lambda i: (0, OFF_DT // LANES)), row, row],
        out_specs=[blk, blk, row],
        out_shape=[jax.ShapeDtypeStruct((t, LANES), F32), jax.ShapeDtypeStruct((t, LANES), F32),
                   jax.ShapeDtypeStruct((1, LANES), F32)],
        compiler_params=_cp("arbitrary"),
    )(proj, bias128, alog128)


def _ssd_prep_bwd(ddt128, dadt128, proj, bias128, dt128, a128, dd_h, *, name):
    t = proj.shape[0]

    def body(ddt_ref, dadt_ref, raw_ref, b_ref, dt_ref, a_ref, dd_ref, draw_ref, db_ref, dal_ref, dds_ref):
        draw = ddt_ref[...] * _sigmoid(raw_ref[...] + b_ref[...])
        draw_ref[...] = draw.astype(BF16)
        db_ref[...] = jnp.sum(draw, axis=0, keepdims=True)
        dal_ref[...] = jnp.sum(dadt_ref[...] * dt_ref[...], axis=0, keepdims=True) * a_ref[...]
        dds_ref[...] = jnp.sum(dd_ref[...], axis=-1, keepdims=True)

    blk = pl.BlockSpec((t, LANES), lambda i: (0, 0))
    row = pl.BlockSpec((1, LANES), lambda i: (0, 0))
    return pl.pallas_call(
        body, name=name, grid=(1,),
        in_specs=[blk, blk, pl.BlockSpec((t, LANES), lambda i: (0, OFF_DT // LANES)), row, blk, row,
                  pl.BlockSpec((SSD_HEADS, SSD_P), lambda i: (0, 0))],
        out_specs=[blk, row, row, pl.BlockSpec((SSD_HEADS, 1), lambda i: (0, 0))],
        out_shape=[jax.ShapeDtypeStruct((t, LANES), BF16), jax.ShapeDtypeStruct((1, LANES), F32),
                   jax.ShapeDtypeStruct((1, LANES), F32), jax.ShapeDtypeStruct((SSD_HEADS, 1), F32)],
        compiler_params=_cp("arbitrary"),
    )(ddt128, dadt128, proj, bias128, dt128, a128, dd_h)


def _bdot(a, b, ca, cb, precision=None):
    return lax.dot_general(a, b, (((ca,), (cb,)), ((0,), (0,))), precision=precision, preferred_element_type=F32)


def _head_matrices():
    eye, zero = jnp.eye(SSD_P, dtype=F32), jnp.zeros((SSD_P, SSD_P), F32)
    pick = jnp.stack([jnp.concatenate([eye, zero], axis=0), jnp.concatenate([zero, eye], axis=0)])
    return pick, pick.transpose(0, 2, 1)


def _pick_head(pair_ref, pick_ref, h):
    return jnp.dot(pair_ref[...], pick_ref[h % 2], precision=HI, preferred_element_type=F32)


def _place_head(out_ref, val, place_ref, h):
    wide = jnp.dot(val, place_ref[h % 2], precision=HI, preferred_element_type=F32)

    @pl.when(h % 2 == 0)
    def _():
        out_ref[...] = wide

    @pl.when(h % 2 == 1)
    def _():
        out_ref[...] += wide


def _ssd_common(x2, dt_ref, cs_ref, csr_ref, b_ref, c_ref, nc):
    x = x2.reshape(nc, CHUNK, SSD_P)
    dt = dt_ref[0].reshape(nc, CHUNK, SSD_P)
    cs = cs_ref[0].reshape(nc, CHUNK, SSD_P)
    csr = csr_ref[0]
    bm = b_ref[...].reshape(nc, CHUNK, SSD_N).astype(BF16)
    cm = c_ref[...].reshape(nc, CHUNK, SSD_N).astype(BF16)
    li = lax.broadcasted_iota(jnp.int32, (nc, CHUNK, CHUNK), 1)
    si = lax.broadcasted_iota(jnp.int32, (nc, CHUNK, CHUNK), 2)
    lmat = jnp.exp(jnp.where(si <= li, cs - csr, NEG))
    g = _bdot(cm, bm, 2, 2)
    cs_last = jnp.sum(jnp.where(li == CHUNK - 1, cs, 0.0), axis=1, keepdims=True)
    xdt = x * dt
    dec = jnp.exp(cs_last - cs)
    return x, dt, cs, bm, cm, li, si, lmat, g, cs_last, xdt, dec


def _ssd_fwd(xbc, dt_h, cs_h, cs_row, dskip_h, *, name):
    t = xbc.shape[0]
    nc = t // CHUNK
    hpg = SSD_HEADS // SSD_GROUPS
    pick, place = _head_matrices()

    def body(xs_ref, dt_ref, cs_ref, csr_ref, b_ref, c_ref, dk_ref, pick_ref, place_ref, y_ref, st_ref, sc_ref, cd_ref):
        h = pl.program_id(0)
        x, dt, cs, bm, cm, li, si, lmat, g, cs_last, xdt, dec = _ssd_common(_pick_head(xs_ref, pick_ref, h), dt_ref,
                                                                           cs_ref, csr_ref, b_ref, c_ref, nc)
        yd = _bdot((g * lmat).astype(BF16), xdt.astype(BF16), 2, 1)
        sc_ref[...] = _bdot(bm, (dec * xdt).astype(BF16), 1, 1)
        cd_ref[...] = jnp.exp(cs_last)

        def step(c, s):
            st_ref[0, c] = s
            return s * cd_ref[c] + sc_ref[c]

        lax.fori_loop(0, nc, step, jnp.zeros((SSD_N, SSD_P), F32))
        yo = _bdot(cm, st_ref[0].astype(BF16), 2, 1) * jnp.exp(cs)
        _place_head(y_ref, (yd + yo + dk_ref[0] * x).reshape(t, SSD_P), place_ref, h)

    head = pl.BlockSpec((1, t, SSD_P), lambda h: (h, 0, 0))
    pair = pl.BlockSpec((t, 2 * SSD_P), lambda h: (0, h // 2))
    nxb = D_SSM // SSD_N
    return pl.pallas_call(
        body, name=name, grid=(SSD_HEADS,),
        in_specs=[pair, head, head, pl.BlockSpec((1, nc, 1, CHUNK), lambda h: (h, 0, 0, 0)),
                  pl.BlockSpec((t, SSD_N), lambda h: (0, nxb + h // hpg)),
                  pl.BlockSpec((t, SSD_N), lambda h: (0, nxb + SSD_GROUPS + h // hpg)),
                  pl.BlockSpec((1, 1, SSD_P), lambda h: (h, 0, 0)),
                  pl.BlockSpec((2, 2 * SSD_P, SSD_P), lambda h: (0, 0, 0)),
                  pl.BlockSpec((2, SSD_P, 2 * SSD_P), lambda h: (0, 0, 0))],
        out_specs=[pair, pl.BlockSpec((1, nc, SSD_N, SSD_P), lambda h: (h, 0, 0, 0))],
        out_shape=[jax.ShapeDtypeStruct((t, D_SSM), F32),
                   jax.ShapeDtypeStruct((SSD_HEADS, nc, SSD_N, SSD_P), F32)],
        scratch_shapes=[pltpu.VMEM((nc, SSD_N, SSD_P), F32), pltpu.VMEM((nc, 1, SSD_P), F32)],
        compiler_params=_cp("arbitrary"),
    )(xbc, dt_h, cs_h, cs_row, xbc, xbc, dskip_h, pick, place)


def _ssd_bwd(xbc, dt_h, cs_h, cs_row, dskip_h, a_h, states, dy, *, name):
    t = xbc.shape[0]
    nc = t // CHUNK
    hpg = SSD_HEADS // SSD_GROUPS
    pick, place = _head_matrices()

    def body(xs_ref, dt_ref, cs_ref, csr_ref, b_ref, c_ref, dk_ref, a_ref, st_ref, dy_ref, pick_ref, place_ref,
             dxs_ref, ddt_ref, dadt_ref, db_ref, dc_ref, dd_ref, dsl_ref, dsc_ref, cd_ref):
        h = pl.program_id(0) * hpg + pl.program_id(1)
        x, dt, cs, bm, cm, li, si, lmat, g, cs_last, xdt, dec = _ssd_common(_pick_head(xs_ref, pick_ref, h), dt_ref,
                                                                           cs_ref, csr_ref, b_ref, c_ref, nc)
        dy = _pick_head(dy_ref, pick_ref, h).reshape(nc, CHUNK, SSD_P)
        dyb = dy.astype(BF16)
        xdtb = xdt.astype(BF16)
        sprev = st_ref[0]
        sprevb = sprev.astype(BF16)
        cdec = jnp.exp(cs_last)
        ecs = jnp.exp(cs)
        dw = (ecs * dy).astype(BF16)
        wmat = _bdot(cm, sprevb, 2, 1)
        dcs = jnp.sum(dy * ecs * wmat, axis=2, keepdims=True)
        dcm = _bdot(dw, sprevb, 2, 2)
        dsl_ref[...] = _bdot(cm, dw, 1, 1)
        cd_ref[...] = cdec

        def step(k, ds):
            c = nc - 1 - k
            dsc_ref[c] = ds
            return ds * cd_ref[c] + dsl_ref[c]

        lax.fori_loop(0, nc, step, jnp.zeros((SSD_N, SSD_P), F32))
        dsc = dsc_ref[...]
        dscb = dsc.astype(BF16)
        d_last = jnp.sum(jnp.sum(dsc * sprev, axis=1, keepdims=True) * cdec, axis=2, keepdims=True)
        z = dec * xdt
        dbm = _bdot(z.astype(BF16), dscb, 2, 2)
        dz = _bdot(bm, dscb, 2, 1)
        dxdt = dec * dz
        t2 = jnp.sum(dz * z, axis=2, keepdims=True)
        dcs = dcs - t2
        d_last = d_last + jnp.sum(t2, axis=1, keepdims=True)
        m = g * lmat
        mb = m.astype(BF16)
        dm = _bdot(dyb, xdtb, 2, 2)
        dxdt = dxdt + _bdot(mb, dyb, 1, 1)
        dseg = dm * m
        dcs = dcs + jnp.sum(dseg, axis=2, keepdims=True)
        ones = jnp.ones((nc, CHUNK, SSD_P), F32)
        dcs = dcs - _bdot(dseg, ones, 1, 1, precision=HI)
        dg = (dm * lmat).astype(BF16)
        dcm = dcm + _bdot(dg, bm, 2, 1)
        dbm = dbm + _bdot(dg, cm, 1, 1)
        dcs = dcs + jnp.where(li[:, :, :SSD_P] == CHUNK - 1, d_last, 0.0)
        triu = jnp.where(li <= si, 1.0, 0.0).astype(F32)
        dadt = _bdot(triu, dcs, 2, 1, precision=HI)
        dk = dk_ref[0]
        _place_head(dxs_ref, (dxdt * dt + dk * dy).reshape(t, SSD_P), place_ref, h)
        ddt_ref[0] = (jnp.sum(dxdt * x, axis=2, keepdims=True) + dadt * a_ref[0]).reshape(t, SSD_P)
        dadt_ref[0] = dadt.reshape(t, SSD_P)
        dd_ref[0] = jnp.sum(jnp.sum(dy * x, axis=1, keepdims=True), axis=0)

        @pl.when(pl.program_id(1) == 0)
        def _():
            db_ref[...] = jnp.zeros_like(db_ref)
            dc_ref[...] = jnp.zeros_like(dc_ref)

        db_ref[...] += dbm.reshape(t, SSD_N)
        dc_ref[...] += dcm.reshape(t, SSD_N)

    head = pl.BlockSpec((1, t, SSD_P), lambda gi, hi: (gi * hpg + hi, 0, 0))
    pair = pl.BlockSpec((t, 2 * SSD_P), lambda gi, hi: (0, (gi * hpg + hi) // 2))
    grp = pl.BlockSpec((t, SSD_N), lambda gi, hi: (0, gi))
    lane = pl.BlockSpec((1, 1, SSD_P), lambda gi, hi: (gi * hpg + hi, 0, 0))
    nxb = D_SSM // SSD_N
    dxs, ddt, dadt, db, dc, dd = pl.pallas_call(
        body, name=name, grid=(SSD_GROUPS, hpg),
        in_specs=[pair, head, head, pl.BlockSpec((1, nc, 1, CHUNK), lambda gi, hi: (gi * hpg + hi, 0, 0, 0)),
                  pl.BlockSpec((t, SSD_N), lambda gi, hi: (0, nxb + gi)),
                  pl.BlockSpec((t, SSD_N), lambda gi, hi: (0, nxb + SSD_GROUPS + gi)), lane, lane,
                  pl.BlockSpec((1, nc, SSD_N, SSD_P), lambda gi, hi: (gi * hpg + hi, 0, 0, 0)), pair,
                  pl.BlockSpec((2, 2 * SSD_P, SSD_P), lambda gi, hi: (0, 0, 0)),
                  pl.BlockSpec((2, SSD_P, 2 * SSD_P), lambda gi, hi: (0, 0, 0))],
        out_specs=[pair, head, head, grp, grp, lane],
        out_shape=[jax.ShapeDtypeStruct((t, D_SSM), F32)] + [jax.ShapeDtypeStruct((SSD_HEADS, t, SSD_P), F32)] * 2
        + [jax.ShapeDtypeStruct((t, SSD_GROUPS * SSD_N), F32)] * 2
        + [jax.ShapeDtypeStruct((SSD_HEADS, 1, SSD_P), F32)],
        scratch_shapes=[pltpu.VMEM((nc, SSD_N, SSD_P), F32), pltpu.VMEM((nc, SSD_N, SSD_P), F32),
                        pltpu.VMEM((nc, 1, SSD_P), F32)],
        compiler_params=_cp("arbitrary", "arbitrary"),
    )(xbc, dt_h, cs_h, cs_row, xbc, xbc, dskip_h, a_h, states, dy, pick, place)
    return jnp.concatenate([dxs, db, dc], axis=1), ddt, dadt, dd


def _ssd_gate_fwd(y, proj, w, *, tr=256, name):
    t = y.shape[0]
    gw = D_SSM // SSD_GROUPS

    def body(y_ref, z_ref, w_ref, o_ref):
        v = y_ref[...] * _silu(z_ref[...])
        for gi in range(SSD_GROUPS):
            vg = v[:, gi * gw:(gi + 1) * gw]
            r = lax.rsqrt(jnp.mean(vg * vg, axis=-1, keepdims=True) + NORM_EPS)
            o_ref[:, gi * gw:(gi + 1) * gw] = (vg * r * w_ref[:, gi * gw:(gi + 1) * gw]).astype(BF16)

    blk = pl.BlockSpec((tr, D_SSM), lambda i: (i, 0))
    return pl.pallas_call(
        body, name=name, grid=(t // tr,), in_specs=[blk, blk, pl.BlockSpec((1, D_SSM), lambda i: (0, 0))],
        out_specs=blk, out_shape=jax.ShapeDtypeStruct((t, D_SSM), BF16), compiler_params=_cp("parallel"),
    )(y, proj, w)


def _ssd_gate_bwd(y, proj, w, dcat, *, tr=256, name):
    t = y.shape[0]
    gw = D_SSM // SSD_GROUPS

    def body(y_ref, z_ref, w_ref, d_ref, dy_ref, dz_ref, dw_ref):
        yv, zv, dv = y_ref[...], z_ref[...], d_ref[...].astype(F32)
        sz = _silu(zv)
        v = yv * sz

        @pl.when(pl.program_id(0) == 0)
        def _():
            dw_ref[...] = jnp.zeros_like(dw_ref)

        for gi in range(SSD_GROUPS):
            sl = slice(gi * gw, (gi + 1) * gw)
            vg, dg = v[:, sl], dv[:, sl]
            r = lax.rsqrt(jnp.mean(vg * vg, axis=-1, keepdims=True) + NORM_EPS)
            vh = vg * r
            gg = dg * w_ref[:, sl]
            dvg = r * (gg - vh * jnp.mean(gg * vh, axis=-1, keepdims=True))
            dy_ref[:, sl] = dvg * sz[:, sl]
            dz_ref[:, sl] = (dvg * yv[:, sl] * _dsilu(zv[:, sl])).astype(BF16)
            dw_ref[:, sl] += jnp.sum(dg * vh, axis=0, keepdims=True)

    blk = pl.BlockSpec((tr, D_SSM), lambda i: (i, 0))
    row = pl.BlockSpec((1, D_SSM), lambda i: (0, 0))
    return pl.pallas_call(
        body, name=name, grid=(t // tr,), in_specs=[blk, blk, row, blk], out_specs=[blk, blk, row],
        out_shape=[jax.ShapeDtypeStruct((t, D_SSM), F32), jax.ShapeDtypeStruct((t, D_SSM), BF16),
                   jax.ShapeDtypeStruct((1, D_SSM), F32)],
        compiler_params=_cp("arbitrary"),
    )(y, proj, w, dcat)


def _pad_lanes(v):
    return jnp.pad(v, ((0, 0), (0, LANES - v.shape[1])))


def _per_head(v128, t):
    return jnp.broadcast_to(v128[:, :SSD_HEADS].T[:, :, None], (SSD_HEADS, t, SSD_P))


def _ssd_forward(proj, conv_w, conv_b, dt_bias, a_log, d_skip, ssd_norm_w):
    t = proj.shape[0]
    nc = t // CHUNK
    xbc = _conv_act_fwd(proj, conv_w, conv_b, kw=SSD_CONV, glu=False, tc=512, coff=OFF_XBC // 512,
                        ncols=SSD_CONV_DIM, out_dtype=F32, name="ssd_conv_fwd")
    bias128, alog128 = _pad_lanes(dt_bias), _pad_lanes(a_log)
    dt128, cs128, a128 = _ssd_prep(proj, bias128, alog128, name="ssd_prep")
    dt_h, cs_h = _per_head(dt128, t), _per_head(cs128, t)
    cs_row = cs128[:, :SSD_HEADS].T.reshape(SSD_HEADS, nc, 1, CHUNK)
    dskip_h = jnp.broadcast_to(d_skip[0][:, None, None], (SSD_HEADS, 1, SSD_P))
    a_h = jnp.broadcast_to(a128[0, :SSD_HEADS][:, None, None], (SSD_HEADS, 1, SSD_P))
    y, states = _ssd_fwd(xbc, dt_h, cs_h, cs_row, dskip_h, name="ssd_scan_fwd")
    y_ssd = _ssd_gate_fwd(y, proj, ssd_norm_w, name="ssd_gate_fwd")
    saved = (proj, conv_w, conv_b, ssd_norm_w, bias128, dt128, a128, dt_h, cs_h, cs_row, xbc, dskip_h, a_h, states, y)
    return y_ssd, saved


def _ssd_backward(saved, dcat):
    proj, conv_w, conv_b, ssd_norm_w, bias128, dt128, a128, dt_h, cs_h, cs_row, xbc, dskip_h, a_h, states, y = saved
    dy, dz, d_norm_w = _ssd_gate_bwd(y, proj, ssd_norm_w, dcat, name="ssd_gate_bwd")
    dxc, ddt_h, dadt_h, dd_h = _ssd_bwd(xbc, dt_h, cs_h, cs_row, dskip_h, a_h, states, dy, name="ssd_scan_bwd")
    dxbc, d_conv_w, d_conv_b = _conv_act_bwd(proj, conv_w, conv_b, dxc, kw=SSD_CONV, glu=False, tc=512,
                                             coff=OFF_XBC // 512, ncols=SSD_CONV_DIM, name="ssd_conv_bwd")
    ddt128 = _pad_lanes(ddt_h[:, :, 0].T)
    dadt128 = _pad_lanes(dadt_h[:, :, 0].T)
    d_raw, d_bias, d_alog, d_dskip = _ssd_prep_bwd(ddt128, dadt128, proj, bias128, dt128, a128,
                                                   dd_h.reshape(SSD_HEADS, SSD_P), name="ssd_prep_bwd")
    return (dz, dxbc, d_raw, d_norm_w, d_conv_w, d_conv_b, d_bias[:, :SSD_HEADS], d_alog[:, :SSD_HEADS],
            d_dskip.reshape(1, SSD_HEADS))


def _rope_tables(positions):
    inv_freq = ROPE_THETA ** (-jnp.arange(0, MLA_ROPE, 2, dtype=F32) / MLA_ROPE)
    ang = positions[0].astype(F32)[:, None] * inv_freq
    cos, sin = jnp.cos(ang), jnp.sin(ang)
    z = jnp.zeros_like(cos)
    return jnp.stack([jnp.concatenate([cos, cos, z, z], axis=1), jnp.concatenate([-sin, z, z, z], axis=1),
                      jnp.concatenate([z, sin, z, z], axis=1)])


def _mla_forward(proj, tabs, q_a_norm_w, wq_pad, kv_a_norm_w, wkv):
    qn = _rmsnorm_fwd(proj, q_a_norm_w, width=MLA_Q_RANK, cblk=OFF_QA // MLA_Q_RANK, name="q_a_norm")
    q = _matmul(qn, wq_pad, name="q_b_proj")
    kvn = _rmsnorm_fwd(proj, kv_a_norm_w, width=MLA_KV_RANK, cblk=OFF_CKV // MLA_KV_RANK, name="kv_a_norm")
    kv = _matmul(kvn, wkv, name="kv_b_proj")
    q3, k3, v3 = _mla_prep(q, kv, proj, tabs, name="mla_prep")
    o, lse = _attn_fwd(q3, k3, v3, name="attn_fwd")
    return o, (proj, tabs, q_a_norm_w, wq_pad, kv_a_norm_w, wkv, qn, kvn, q3, k3, v3, o, lse)


def _mla_backward(saved, dcat):
    proj, tabs, q_a_norm_w, wq_pad, kv_a_norm_w, wkv, qn, kvn, q3, k3, v3, o, lse = saved
    dq3, dk3, dv3 = _attn_bwd(q3, k3, v3, o, dcat, lse, name="attn_bwd")
    dq, dkv, dkr = _mla_unprep(dq3, dk3, dv3, tabs, name="mla_unprep")
    d_wq = _matmul(qn, dq, ta=True, out_dtype=BF16, name="d_w_q_b")
    dqn = _matmul(dq, wq_pad, tb=True, name="d_qn")
    dq_a, d_qnw = _rmsnorm_bwd(proj, q_a_norm_w, dqn, width=MLA_Q_RANK, cblk=OFF_QA // MLA_Q_RANK, out_dtype=BF16,
                               name="q_a_norm_bwd")
    d_wkv = _matmul(kvn, dkv, ta=True, out_dtype=BF16, name="d_w_kv_b")
    dkvn = _matmul(dkv, wkv, tb=True, name="d_kvn")
    dckv, d_kvnw = _rmsnorm_bwd(proj, kv_a_norm_w, dkvn, width=MLA_KV_RANK, cblk=OFF_CKV // MLA_KV_RANK,
                                out_dtype=BF16, name="kv_a_norm_bwd")
    return dq_a, dckv, dkr, d_wq, d_wkv, d_qnw, d_kvnw


def _pad_w_q(w):
    r = w.shape[0]
    w3 = w.reshape(r, MLA_HEADS, MLA_NOPE + MLA_ROPE)
    return jnp.pad(w3, ((0, 0), (0, 0), (0, MLA_QK_PAD - MLA_NOPE - MLA_ROPE))).reshape(r, MLA_HEADS * MLA_QK_PAD)


def _unpad_w_q(w):
    r = w.shape[0]
    return w.reshape(r, MLA_HEADS, MLA_QK_PAD)[:, :, :MLA_NOPE + MLA_ROPE].reshape(r, MLA_HEADS * (MLA_NOPE + MLA_ROPE))


def _pad_w_in(w):
    r = w.shape[0]
    o_dt = D_SSM + SSD_CONV_DIM
    o_qa = o_dt + SSD_HEADS
    o_kr = o_qa + MLA_Q_RANK + MLA_KV_RANK
    zeros = lambda n: jnp.zeros((r, n), w.dtype)
    return jnp.concatenate([w[:, :o_dt], w[:, o_qa:o_kr], w[:, o_kr:], zeros(LANES - MLA_ROPE),
                            w[:, o_dt:o_qa], zeros(LANES - SSD_HEADS)], axis=1)


def _unpad_w_in(w):
    return jnp.concatenate([w[:, :OFF_QA], w[:, OFF_DT:OFF_DT + SSD_HEADS], w[:, OFF_QA:OFF_KR + MLA_ROPE]], axis=1)


WEIGHTS = ['mix_norm_w', 'w_in', 'conv_w', 'conv_b', 'dt_bias', 'a_log', 'd_skip', 'ssd_norm_w', 'q_a_norm_w', 'w_q_b',
           'kv_a_norm_w', 'w_kv_b', 'w_out', 'ffn_norm_w', 'w_ffn_up', 'ffn_conv_w', 'ffn_conv_b', 'w_ffn_down',
           'ple_norm_w', 'w_ple_gate', 'b_ple_gate', 'w_ple_proj', 'ple_post_norm_w', 'final_norm_w']
BIG = ['w_in', 'w_q_b', 'w_kv_b', 'w_out', 'w_ffn_up', 'w_ffn_down', 'w_ple_gate', 'w_ple_proj']
COL_SHARDED = ('w_in', 'w_q_b', 'w_kv_b', 'w_ffn_up', 'w_ple_proj')
CONV = ['conv_w', 'ffn_conv_w']
REPL = [n for n in WEIGHTS if n not in BIG and n not in CONV]
FFN_INV = tuple(int(i) for i in np.argsort(FFN_PERM))


def _cat_cols(g):
    return jnp.concatenate([g[j] for j in range(N_DEV)], axis=1)


def _split_cols(w):
    n = w.shape[1] // N_DEV
    return jnp.stack([w[:, j * n:(j + 1) * n] for j in range(N_DEV)])


def _interleave(v):
    r = v.shape[0]
    return v.reshape(r, N_DEV, FFN_TC)[:, jnp.array(FFN_PERM)].reshape(r, N_DEV * FFN_TC)


def _deinterleave(v):
    r = v.shape[0]
    return v.reshape(r, N_DEV, FFN_TC)[:, jnp.array(FFN_INV)].reshape(r, N_DEV * FFN_TC)


def _assemble_weights(g):
    layout = {
        'w_in': lambda v: _pad_w_in(_cat_cols(v)),
        'w_q_b': lambda v: _pad_w_q(_cat_cols(v)),
        'w_kv_b': _cat_cols,
        'w_out': lambda v: v.reshape(D_MODEL, D_MODEL),
        'w_ffn_up': lambda v: v,
        'w_ffn_down': lambda v: v.reshape(D_FF, D_MODEL),
        'w_ple_gate': lambda v: v.reshape(D_MODEL, D_MODEL),
        'w_ple_proj': _cat_cols,
        'conv_w': _cat_cols,
        'ffn_conv_w': lambda v: _interleave(_cat_cols(v)),
    }
    return {n: layout[n](v) for n, v in g.items()}


WEIGHT_GROUPS = {'a': ['w_in', 'w_q_b', 'w_kv_b', 'conv_w'], 'b': ['w_out'],
                 'c': ['w_ffn_up', 'ffn_conv_w', 'w_ffn_down', 'w_ple_gate', 'w_ple_proj']}
GRAD_GROUPS = {'p': ['w_ple_proj', 'w_ple_gate', 'w_ffn_down'], 'r': ['w_ffn_up'], 's': ['w_out'],
               't': ['w_q_b', 'w_kv_b', 'w_in']}


def _ffn_perm(j):
    return (j % 2) * (N_DEV // 2) + j // 2


def _local_step(x, p, tabs, get_w, s, target, emit, relay, settle):
    t = x.shape[0]
    s = dict(s)
    half = D_MODEL // 2
    up_cols = 2 * D_FF
    ffn_conv_b = _interleave(s['ffn_conv_b'])
    w = dict(get_w('a', None))
    h = _rmsnorm_fwd(x, s['mix_norm_w'], width=D_MODEL, name="mix_norm")
    proj = _matmul(h, w['w_in'], name="in_proj")
    y_ssd, ssd_saved = _ssd_forward(proj, w['conv_w'], s['conv_b'], s['dt_bias'], s['a_log'], s['d_skip'],
                                    s['ssd_norm_w'])
    o, mla_saved = _mla_forward(proj, tabs, s['q_a_norm_w'], w['w_q_b'], s['kv_a_norm_w'], w['w_kv_b'])
    tk_o, tn_o = _tile(half, MM_TK), _tile(D_MODEL, MM_TILE)
    w.update(get_w('b', o))
    x1 = _matmul(y_ssd, w['w_out'], add=x, mnk=(t, D_MODEL, half), name="out_proj_ssd")
    x1 = _matmul(o, w['w_out'], add=x1, mnk=(t, D_MODEL, half), name="out_proj_mla",
                 b_spec=pl.BlockSpec((tk_o, tn_o), lambda i, j, kk: (kk + half // tk_o, j)))
    hf = _rmsnorm_fwd(x1, s['ffn_norm_w'], width=D_MODEL, name="ffn_norm")
    w.update(get_w('c', hf))
    tk_u = _tile(D_MODEL, MM_TK)
    u = _matmul(hf, w['w_ffn_up'], mnk=(t, up_cols, D_MODEL), tn=FFN_TC, name="ffn_up",
                b_spec=pl.BlockSpec((1, tk_u, FFN_TC), lambda i, j, kk: (_ffn_perm(j), kk, 0)))
    act = _conv_act_fwd(u, w['ffn_conv_w'], ffn_conv_b, kw=FFN_CONV, glu=True, tc=2 * FFN_TC, coff=0, ncols=up_cols,
                        out_dtype=BF16, name="ffn_act")
    x2 = _matmul(act, w['w_ffn_down'], add=x1, name="ffn_down")
    hp = _rmsnorm_fwd(x2, s['ple_norm_w'], width=D_MODEL, name="ple_norm")
    gl = _matmul(hp, w['w_ple_gate'], bias=s['b_ple_gate'], name="ple_gate")
    pe = _matmul(p, w['w_ple_proj'], name="ple_proj")
    x3 = _ple_fwd(x2, gl, pe, s['ple_post_norm_w'], name="ple_mix")
    loss, dx3, d_final = _loss_head(x3, s['final_norm_w'], target, name="loss_head")
    dgl, d_bgate, dpe, d_post = _ple_bwd(dx3, gl, pe, s['ple_post_norm_w'], name="ple_mix_bwd")
    d_wproj = _matmul(p, dpe, ta=True, out_dtype=BF16, name="d_w_ple_proj")
    d_wgate = _matmul(hp, dgl, ta=True, out_dtype=BF16, name="d_w_ple_gate")
    dhp = _matmul(dgl, w['w_ple_gate'], tb=True, name="d_ple_normed")
    dx2, d_plenorm = _rmsnorm_bwd(x2, s['ple_norm_w'], dhp, dx3, width=D_MODEL, name="ple_norm_bwd")
    dact = _matmul(dx2, w['w_ffn_down'], tb=True, name="d_ffn_act")
    d_wdown = _matmul(act, dx2, ta=True, out_dtype=BF16, name="d_w_ffn_down")
    zz = emit('p', {'w_ple_proj': _split_cols(d_wproj), 'w_ple_gate': d_wgate.reshape(N_DEV, D_MODEL // N_DEV, D_MODEL),
                    'w_ffn_down': d_wdown.reshape(N_DEV, D_FF // N_DEV, D_MODEL)})
    du, d_fconv_w, d_fconv_b = _conv_act_bwd(u, w['ffn_conv_w'], ffn_conv_b + zz, dact, kw=FFN_CONV, glu=True,
                                             tc=2 * FFN_TC, coff=0, ncols=up_cols, name="ffn_act_bwd")
    zz = zz + relay('p', du)
    tm_u = _tile(D_MODEL, MM_TILE)
    d_wup = _matmul(hf, du, ta=True, out_dtype=BF16, mnk=(D_MODEL, up_cols, t), tn=FFN_TC, name="d_w_ffn_up",
                    o_spec=pl.BlockSpec((1, tm_u, FFN_TC), lambda i, j, kk: (_ffn_perm(j), i, 0)),
                    o_shape=(N_DEV, D_MODEL, FFN_TC))
    zz = zz + emit('r', {'w_ffn_up': d_wup})
    dhf = _matmul(du, w['w_ffn_up'], tb=True, mnk=(t, D_MODEL, up_cols), tk=FFN_TC, name="d_ffn_normed",
                  b_spec=pl.BlockSpec((1, tn_o, FFN_TC), lambda i, j, kk: (_ffn_perm(kk), j, 0)))
    zz = zz + relay('r', dhf) + settle('p')
    dx1, d_ffnnorm = _rmsnorm_bwd(x1, s['ffn_norm_w'] + zz, dhf, dx2, width=D_MODEL, name="ffn_norm_bwd")
    dcat = _matmul(dx1, w['w_out'], tb=True, name="d_mixed")
    d_wout = jnp.concatenate([_matmul(y_ssd, dx1, ta=True, out_dtype=BF16, name="d_w_out_ssd"),
                              _matmul(o, dx1, ta=True, out_dtype=BF16, name="d_w_out_mla")], axis=0)
    zz = zz + emit('s', {'w_out': d_wout.reshape(N_DEV, D_MODEL // N_DEV, D_MODEL)})
    ssd_saved = ssd_saved[:3] + (ssd_saved[3] + zz,) + ssd_saved[4:]
    dz, dxbc, d_raw, d_ssdnorm, d_conv_w, d_conv_b, d_dtb, d_alog, d_dskip = _ssd_backward(ssd_saved, dcat)
    zz = zz + relay('s', dz) + settle('r')
    mla_saved = mla_saved[:-1] + (mla_saved[-1] + zz,)
    dq_a, dckv, dkr, d_wq, d_wkv, d_qnorm, d_kvnorm = _mla_backward(mla_saved, dcat)
    d_raw = (d_raw + settle('s')).astype(BF16)
    dproj = jnp.concatenate([dz, dxbc, dq_a, dckv, dkr, d_raw], axis=1)
    d_win = _matmul(h, dproj, ta=True, out_dtype=BF16, name="d_w_in")
    dh = _matmul(dproj, w['w_in'], tb=True, name="d_in_normed")
    dx, d_mixnorm = _rmsnorm_bwd(x, s['mix_norm_w'], dh, dx1, width=D_MODEL, name="mix_norm_bwd")
    emit('t', {'w_in': _split_cols(_unpad_w_in(d_win)), 'w_q_b': _split_cols(_unpad_w_q(d_wq)),
               'w_kv_b': _split_cols(d_wkv)})
    relay('t', dx)
    settle('t')
    conv = {'conv_w': d_conv_w, 'ffn_conv_w': _deinterleave(d_fconv_w)}
    vec = {
        'mix_norm_w': d_mixnorm, 'conv_b': d_conv_b, 'dt_bias': d_dtb, 'a_log': d_alog, 'd_skip': d_dskip,
        'ssd_norm_w': d_ssdnorm, 'q_a_norm_w': d_qnorm, 'kv_a_norm_w': d_kvnorm, 'ffn_norm_w': d_ffnnorm,
        'ffn_conv_b': _deinterleave(d_fconv_b), 'ple_norm_w': d_plenorm, 'b_ple_gate': d_bgate,
        'ple_post_norm_w': d_post, 'final_norm_w': d_final,
    }
    return loss, dx, conv, vec


MESH = pl.DeviceIdType.MESH
FLIPS = ((0, 0, 1), (1, 0, 0), (0, 1, 0), (1, 1, 0), (1, 0, 1), (0, 1, 1), (1, 1, 1))


def _exchange(items, *, gather, name):
    n = len(items)

    def body(*refs):
        ins, outs = refs[:n], refs[n:2 * n]
        send_sems, recv_sems, local_sems = refs[2 * n:]
        x, y, c = lax.axis_index("x"), lax.axis_index("y"), lax.axis_index("c")
        me = 4 * x + 2 * y + c
        peers = [(jnp.where(fx, 1 - x, x), jnp.where(fy, 1 - y, y), jnp.where(fc, 1 - c, c)) for fx, fy, fc in FLIPS]
        slot = [4 * px + 2 * py + pc for px, py, pc in peers]
        local, sends = [], []
        for wi in range(n):
            cp = pltpu.make_async_copy(ins[wi] if gather else ins[wi].at[me], outs[wi].at[me], local_sems.at[wi])
            cp.start()
            local.append(cp)
            for k, peer in enumerate(peers):
                cp = pltpu.make_async_remote_copy(
                    src_ref=ins[wi] if gather else ins[wi].at[slot[k]], dst_ref=outs[wi].at[me],
                    send_sem=send_sems.at[k, wi], recv_sem=recv_sems.at[k, wi], device_id=peer, device_id_type=MESH)
                cp.start()
                sends.append(cp)
        for wi in range(n):
            for k, peer in enumerate(peers):
                pltpu.make_async_remote_copy(
                    src_ref=outs[wi].at[slot[k]], dst_ref=outs[wi].at[slot[k]], send_sem=send_sems.at[k, wi],
                    recv_sem=recv_sems.at[k, wi], device_id=peer, device_id_type=MESH).wait_recv()
        for cp in sends:
            cp.wait_send()
        for cp in local:
            cp.wait()

    hbm = pl.BlockSpec(memory_space=pltpu.HBM)
    out_shape = [jax.ShapeDtypeStruct(((N_DEV,) + v.shape) if gather else v.shape, v.dtype) for v in items]
    return pl.pallas_call(
        body, name=name, in_specs=[hbm] * n, out_specs=[hbm] * n, out_shape=out_shape,
        scratch_shapes=[pltpu.SemaphoreType.DMA((len(FLIPS), n)), pltpu.SemaphoreType.DMA((len(FLIPS), n)),
                        pltpu.SemaphoreType.DMA((n,))],
    )(*items)


HBM_SPEC = pl.BlockSpec(memory_space=pltpu.HBM)
SEM_SPEC = pl.BlockSpec(memory_space=pltpu.SEMAPHORE)
EFFECT = pltpu.SideEffectType.DATAFLOW_SIDE_EFFECTING


def _peers():
    x, y, c = lax.axis_index("x"), lax.axis_index("y"), lax.axis_index("c")
    peers = [(jnp.where(fx, 1 - x, x), jnp.where(fy, 1 - y, y), jnp.where(fc, 1 - c, c)) for fx, fy, fc in FLIPS]
    return 4 * x + 2 * y + c, peers, [4 * px + 2 * py + pc for px, py, pc in peers]


def _split_start(bufs, ncopies, plan, *, name):
    nb = len(bufs)

    def body(*refs):
        send_sems, recv_sems, token = refs[nb], refs[nb + 1], refs[2 * nb + 2]
        for i, (src, dst, peer, _) in enumerate(plan(refs[:nb])):
            pltpu.make_async_remote_copy(src_ref=src, dst_ref=dst, send_sem=send_sems.at[i], recv_sem=recv_sems.at[i],
                                         device_id=peer, device_id_type=MESH).start()
        token[...] = jnp.zeros_like(token)

    res = pl.pallas_call(
        body, name=name, in_specs=[HBM_SPEC] * nb,
        out_specs=[SEM_SPEC, SEM_SPEC] + [HBM_SPEC] * nb + [pl.BlockSpec(memory_space=pltpu.VMEM)],
        out_shape=[pltpu.SemaphoreType.DMA((ncopies,)), pltpu.SemaphoreType.DMA((ncopies,))]
        + [pltpu.HBM(v.shape, v.dtype) for v in bufs] + [jax.ShapeDtypeStruct((HALO, LANES), F32)],
        input_output_aliases={i: 2 + i for i in range(nb)},
        compiler_params=pltpu.CompilerParams(has_side_effects=EFFECT),
    )(*[pltpu.with_memory_space_constraint(v, pltpu.HBM) for v in bufs])
    return (res[0], res[1], list(res[2:2 + nb])), res[2 + nb]


def _split_wait(started, after, plan, local_plan, *, name):
    send_sems, recv_sems, bufs = started
    nb = len(bufs)
    nlocal = len(local_plan(bufs))

    def body(*refs):
        send_sems, recv_sems = refs[nb], refs[nb + 1]
        local_sems = refs[2 * nb + 3]
        local = []
        for j, (src, dst) in enumerate(local_plan(refs[:nb])):
            cp = pltpu.make_async_copy(src, dst, local_sems.at[j])
            cp.start()
            local.append(cp)
        for i, (src, _, peer, incoming) in enumerate(plan(refs[:nb])):
            cp = pltpu.make_async_remote_copy(src_ref=src, dst_ref=incoming, send_sem=send_sems.at[i],
                                              recv_sem=recv_sems.at[i], device_id=peer, device_id_type=MESH)
            cp.wait_send()
            cp.wait_recv()
        for cp in local:
            cp.wait()

    res = pl.pallas_call(
        body, name=name, in_specs=[HBM_SPEC] * nb + [SEM_SPEC, SEM_SPEC, pl.BlockSpec(memory_space=pl.ANY)],
        out_specs=[HBM_SPEC] * nb, out_shape=[pltpu.HBM(v.shape, v.dtype) for v in bufs],
        input_output_aliases={i: i for i in range(nb)},
        scratch_shapes=[pltpu.SemaphoreType.DMA((max(nlocal, 1),))],
        compiler_params=pltpu.CompilerParams(has_side_effects=EFFECT),
    )(*bufs, send_sems, recv_sems, after)
    return list(res)


def _place():
    x, y, c = lax.axis_index("x"), lax.axis_index("y"), lax.axis_index("c")
    others = [((1 - x, y, c), 2 * (1 - x) + y), ((x, 1 - y, c), 2 * x + 1 - y), ((1 - x, 1 - y, c), 2 * (1 - x) + 1 - y)]
    return 4 * x + 2 * y + c, 2 * x + y, c, (x, y, 1 - c), others


def _gather1_plan(n):
    def plan(refs):
        me, _, _, sibling, others = _place()
        out = []
        for wi in range(n):
            item, land = refs[wi], refs[n + wi]
            out.append((item, land.at[me], sibling, land.at[me + 1 - 2 * lax.axis_index("c")]))
            for peer, chip in others:
                out.append((item, land.at[me], peer, land.at[2 * chip + lax.axis_index("c")]))
        return out

    return plan


def _gather1_local(n):
    def plan(refs):
        me = _place()[0]
        return [(refs[wi], refs[n + wi].at[me]) for wi in range(n)]

    return plan


def _gather2_plan(n):
    def plan(refs):
        _, _, c, sibling, others = _place()
        out = []
        for wi in range(n):
            land = refs[wi]
            for _, chip in others:
                out.append((land.at[2 * chip + c], land.at[2 * chip + c], sibling, land.at[2 * chip + 1 - c]))
        return out

    return plan


def _gather_start(items, *, name):
    lands = [lax.empty((N_DEV,) + v.shape, v.dtype) for v in items]
    return _split_start(items + lands, 4 * len(items), _gather1_plan(len(items)), name=name)


def _gather_forward(started, after, *, name):
    n = len(started[2]) // 2
    bufs = _split_wait(started, after, _gather1_plan(n), _gather1_local(n), name=name + "_wait")
    return _split_start(bufs[n:], 3 * n, _gather2_plan(n), name=name + "_start")


def _gather_finish(started, after, *, name):
    n = len(started[2])
    return _split_wait(started, after, _gather2_plan(n), lambda refs: [], name=name)


def _handshake(peers):
    barrier = pltpu.get_barrier_semaphore()
    for peer in peers:
        pl.semaphore_signal(barrier, inc=1, device_id=peer, device_id_type=MESH)
    pl.semaphore_wait(barrier, len(peers))


def _remote(src, dst, send_sem, recv_sem, peer):
    return pltpu.make_async_remote_copy(src_ref=src, dst_ref=dst, send_sem=send_sem, recv_sem=recv_sem, device_id=peer,
                                        device_id_type=MESH)


def _sequencer_gather(items, *, collective_id, name):
    n = len(items)
    srcs = [jax.new_ref(v, memory_space=pltpu.MemorySpace.HBM) for v in items]
    lands = [jax.empty_ref(jax.ShapeDtypeStruct((N_DEV,) + v.shape, v.dtype), memory_space=pltpu.MemorySpace.HBM)
             for v in items]
    dma = pltpu.SemaphoreType.DMA

    @pl.kernel(mesh=plsc.ScalarSubcoreMesh(axis_name="sequencer", num_cores=1), name=name,
               scratch_types=(dma((4 * n,)), dma((4 * n,)), dma((3 * n,)), dma((3 * n,)), dma((n,))),
               compiler_params=pltpu.CompilerParams(collective_id=collective_id))
    def launch(send1, recv1, send2, recv2, local_sems):
        _, _, _, sibling, others = _place()
        _handshake([sibling] + [peer for peer, _ in others])
        hop1 = _gather1_plan(n)(srcs + lands)
        hop2 = _gather2_plan(n)(lands)
        local = [pltpu.make_async_copy(src, dst, local_sems.at[j])
                 for j, (src, dst) in enumerate(_gather1_local(n)(srcs + lands))]
        for cp in local:
            cp.start()
        for i, (src, dst, peer, _) in enumerate(hop1):
            _remote(src, dst, send1.at[i], recv1.at[i], peer).start()
        for wi in range(n):
            for j in range(3):
                i1, i2 = 4 * wi + 1 + j, 3 * wi + j
                src, _, peer, incoming = hop1[i1]
                _remote(src, incoming, send1.at[i1], recv1.at[i1], peer).wait_recv()
                src, dst, peer, _ = hop2[i2]
                _remote(src, dst, send2.at[i2], recv2.at[i2], peer).start()
        for wi in range(n):
            src, _, peer, incoming = hop1[4 * wi]
            _remote(src, incoming, send1.at[4 * wi], recv1.at[4 * wi], peer).wait_recv()
        for i, (src, _, peer, incoming) in enumerate(hop2):
            cp = _remote(src, incoming, send2.at[i], recv2.at[i], peer)
            cp.wait_send()
            cp.wait_recv()
        for i, (src, dst, peer, _) in enumerate(hop1):
            _remote(src, dst, send1.at[i], recv1.at[i], peer).wait_send()
        for cp in local:
            cp.wait()

    launch()
    return [land[...] for land in lands]


def _sequencer_exchange(sources, land_shapes, ncopies, plan, local_plan, peers, *, collective_id, name):
    srcs = [jax.new_ref(v, memory_space=pltpu.MemorySpace.HBM) for v in sources]
    lands = [jax.empty_ref(s, memory_space=pltpu.MemorySpace.HBM) for s in land_shapes]
    nlocal = len(local_plan(srcs + lands))
    dma = pltpu.SemaphoreType.DMA

    @pl.kernel(mesh=plsc.ScalarSubcoreMesh(axis_name="sequencer", num_cores=1), name=name,
               scratch_types=(dma((ncopies,)), dma((ncopies,)), dma((max(nlocal, 1),))),
               compiler_params=pltpu.CompilerParams(collective_id=collective_id))
    def launch(send_sems, recv_sems, local_sems):
        _handshake(peers(_place()))
        copies = plan(srcs + lands)
        local = [pltpu.make_async_copy(src, dst, local_sems.at[j])
                 for j, (src, dst) in enumerate(local_plan(srcs + lands))]
        for cp in local:
            cp.start()
        for i, (src, dst, peer, _) in enumerate(copies):
            _remote(src, dst, send_sems.at[i], recv_sems.at[i], peer).start()
        for i, (src, _, peer, incoming) in enumerate(copies):
            cp = _remote(src, incoming, send_sems.at[i], recv_sems.at[i], peer)
            cp.wait_send()
            cp.wait_recv()
        for cp in local:
            cp.wait()

    launch()
    return [land[...] for land in lands]


def _sequencer_scatter_hop1(parts, *, collective_id, name):
    n = len(parts)
    shapes = [jax.ShapeDtypeStruct((N_CHIP,) + v.shape[1:], v.dtype) for v in parts]
    return _sequencer_exchange(parts, shapes, N_CHIP * n, _scatter1_plan(n), lambda refs: [], lambda place: [place[3]],
                               collective_id=collective_id, name=name)


def _sequencer_scatter_hop2(sums, *, collective_id, name):
    n = len(sums)
    shapes = [jax.ShapeDtypeStruct(v.shape, v.dtype) for v in sums]
    return _sequencer_exchange(sums, shapes, 3 * n, _scatter2_plan(n), _scatter2_local(n),
                               lambda place: [peer for peer, _ in place[4]], collective_id=collective_id, name=name)


N_CHIP = N_DEV // 2


def _scatter1_plan(n):
    def plan(refs):
        _, _, c, sibling, _ = _place()
        out = []
        for wi in range(n):
            parts, half = refs[wi], refs[n + wi]
            for chip in range(N_CHIP):
                out.append((parts.at[2 * chip + 1 - c], half.at[chip], sibling, half.at[chip]))
        return out

    return plan


def _scatter2_plan(n):
    def plan(refs):
        _, my_chip, _, _, others = _place()
        out = []
        for wi in range(n):
            sums, recv = refs[wi], refs[n + wi]
            for peer, chip in others:
                out.append((sums.at[chip], recv.at[my_chip], peer, recv.at[chip]))
        return out

    return plan


def _scatter2_local(n):
    def plan(refs):
        my_chip = _place()[1]
        return [(refs[wi].at[my_chip], refs[n + wi].at[my_chip]) for wi in range(n)]

    return plan


def _pair_add(parts, half, core, *, name):
    _, r, c = parts.shape
    tr = max(d for d in range(HALO, 257, HALO) if r % d == 0) if r > 256 else r
    parts4 = parts.reshape(N_CHIP, 2, r, c)

    def body(core_ref, p_ref, h_ref, o_ref):
        o_ref[...] = (p_ref[:, 0].astype(F32) + h_ref[...].astype(F32)).astype(o_ref.dtype)

    return pl.pallas_call(
        body, name=name,
        grid_spec=pltpu.PrefetchScalarGridSpec(
            num_scalar_prefetch=1, grid=(r // tr,),
            in_specs=[pl.BlockSpec((N_CHIP, 1, tr, c), lambda i, core_ref: (0, core_ref[0], i, 0)),
                      pl.BlockSpec((N_CHIP, tr, c), lambda i, core_ref: (0, i, 0))],
            out_specs=pl.BlockSpec((N_CHIP, tr, c), lambda i, core_ref: (0, i, 0))),
        out_shape=jax.ShapeDtypeStruct((N_CHIP, r, c), parts.dtype), compiler_params=_cp("parallel"),
    )(core, parts4, half)


def _scatter_start(parts, *, name):
    halves = [lax.empty((N_CHIP,) + v.shape[1:], v.dtype) for v in parts]
    return _split_start(parts + halves, N_CHIP * len(parts), _scatter1_plan(len(parts)), name=name)


def _scatter_forward(started, after, core, *, name):
    n = len(started[2]) // 2
    bufs = _split_wait(started, after, _scatter1_plan(n), lambda refs: [], name=name + "_wait")
    sums = [_pair_add(bufs[wi], bufs[n + wi], core, name=name + "_add%d" % wi) for wi in range(n)]
    recvs = [lax.empty(v.shape, v.dtype) for v in sums]
    return _split_start(sums + recvs, 3 * n, _scatter2_plan(n), name=name + "_start")


def _scatter_finish(started, after, *, name):
    n = len(started[2]) // 2
    return _split_wait(started, after, _scatter2_plan(n), _scatter2_local(n), name=name)[n:]


def _adamw(parts, w, m, v, *, name):
    r, c = w.shape
    nparts = parts.shape[0]
    tr = max(d for d in range(HALO, 129, HALO) if r % d == 0) if r > 128 else r

    def body(p_ref, w_ref, m_ref, v_ref, g_ref, d_ref, mo_ref, vo_ref):
        g = p_ref[0].astype(F32)
        for k in range(1, nparts):
            g = g + p_ref[k].astype(F32)
        mn = ADAM_B1 * m_ref[...] + (1.0 - ADAM_B1) * g
        vn = ADAM_B2 * v_ref[...] + (1.0 - ADAM_B2) * (g * g)
        m_hat = mn / (1.0 - ADAM_B1 ** ADAM_STEP)
        v_hat = vn / (1.0 - ADAM_B2 ** ADAM_STEP)
        g_ref[...] = g
        d_ref[...] = -ADAM_LR * (m_hat / (jnp.sqrt(v_hat) + ADAM_EPS) + ADAM_WD * w_ref[...])
        mo_ref[...] = mn
        vo_ref[...] = vn

    blk = pl.BlockSpec((tr, c), lambda i: (i, 0))
    return pl.pallas_call(
        body, name=name, grid=(r // tr,), in_specs=[pl.BlockSpec((nparts, tr, c), lambda i: (0, i, 0)), blk, blk, blk],
        out_specs=[blk] * 4, out_shape=[jax.ShapeDtypeStruct((r, c), F32)] * 4, compiler_params=_cp("parallel"),
    )(parts, w, m, v)


def _pack_rows(vs, rows):
    lead = vs[0].shape[:-1] if vs[0].ndim > 1 else ()
    flat = jnp.concatenate(vs, axis=-1)
    pad = rows * LANES - flat.shape[-1]
    flat = jnp.pad(flat, [(0, 0)] * len(lead) + [(0, pad)])
    return flat.reshape(lead + (rows, LANES))


def kernel(x, p, positions, mix_norm_w, w_in, conv_w, conv_b, dt_bias, a_log, d_skip, ssd_norm_w, q_a_norm_w, w_q_b, kv_a_norm_w, w_kv_b, w_out, ffn_norm_w, w_ffn_up, ffn_conv_w, ffn_conv_b, w_ffn_down, ple_norm_w, w_ple_gate, b_ple_gate, w_ple_proj, ple_post_norm_w, final_norm_w, loss_target, m_mix_norm_w, m_w_in, m_conv_w, m_conv_b, m_dt_bias, m_a_log, m_d_skip, m_ssd_norm_w, m_q_a_norm_w, m_w_q_b, m_kv_a_norm_w, m_w_kv_b, m_w_out, m_ffn_norm_w, m_w_ffn_up, m_ffn_conv_w, m_ffn_conv_b, m_w_ffn_down, m_ple_norm_w, m_w_ple_gate, m_b_ple_gate, m_w_ple_proj, m_ple_post_norm_w, m_final_norm_w, v_mix_norm_w, v_w_in, v_conv_w, v_conv_b, v_dt_bias, v_a_log, v_d_skip, v_ssd_norm_w, v_q_a_norm_w, v_w_q_b, v_kv_a_norm_w, v_w_kv_b, v_w_out, v_ffn_norm_w, v_w_ffn_up, v_ffn_conv_w, v_ffn_conv_b, v_w_ffn_down, v_ple_norm_w, v_w_ple_gate, v_b_ple_gate, v_w_ple_proj, v_ple_post_norm_w, v_final_norm_w):
    given = dict(locals())
    shapes = {n: given[n].shape for n in WEIGHTS}
    w2 = {n: given[n].reshape(given[n].shape[-2:] if n in BIG or n in CONV else (1, -1)) for n in WEIGHTS}
    m2 = {n: given['m_' + n].reshape(w2[n].shape) for n in WEIGHTS}
    v2 = {n: given['v_' + n].reshape(w2[n].shape) for n in WEIGHTS}
    me = 4 * lax.axis_index("x") + 2 * lax.axis_index("y") + lax.axis_index("c")

    core = lax.axis_index("c").astype(jnp.int32).reshape(1)

    def shards(grp, zero):
        return [(w2[n] + zero).astype(BF16) if n in BIG else w2[n] + zero for n in WEIGHT_GROUPS[grp]]

    first, token = _gather_start(shards('a', 0.0), name="gather_a_hop1")
    first, token = _gather_forward(first, token, name="gather_a_hop2")
    zero = token[0, 0]
    later = _sequencer_gather(shards('b', zero) + shards('c', zero), collective_id=1, name="gather_later")
    later = dict(zip(WEIGHT_GROUPS['b'] + WEIGHT_GROUPS['c'], later))

    def get_w(grp, after):
        if grp == 'a':
            lands = dict(zip(WEIGHT_GROUPS[grp], _gather_finish(first, token, name="gather_a_done")))
        else:
            lands = {n: later[n] for n in WEIGHT_GROUPS[grp]}
        return _assemble_weights(lands)

    scatters = {}

    hop_ids = {grp: 2 + 2 * i for i, grp in enumerate(GRAD_GROUPS)}

    def zero_of(arrays):
        return sum(v[(0,) * v.ndim].astype(F32) * 0.0 for v in arrays)

    def emit(grp, grads):
        scatters[grp], tok = _scatter_start([grads[n] for n in GRAD_GROUPS[grp]], name="scatter_" + grp + "_hop1")
        return tok[0, 0]

    def relay(grp, after):
        n = len(GRAD_GROUPS[grp])
        bufs = _split_wait(scatters[grp], after, _scatter1_plan(n), lambda refs: [], name="scatter_" + grp + "_hop1_wait")
        sums = [_pair_add(bufs[i], bufs[n + i], core, name="scatter_%s_add%d" % (grp, i)) for i in range(n)]
        scatters[grp] = _sequencer_scatter_hop2(sums, collective_id=hop_ids[grp] + 1, name="scatter_" + grp + "_hop2")
        return zero_of(sums)

    out_g, out_d, out_m, out_v = {}, {}, {}, {}

    def settle(grp):
        for n, parts in zip(GRAD_GROUPS[grp], scatters[grp]):
            out_g[n], out_d[n], out_m[n], out_v[n] = _adamw(parts, w2[n], m2[n], v2[n], name="adamw_" + n)
        return zero_of([out_g[n] for n in GRAD_GROUPS[grp]])

    vecs = {n: w2[n] for n in REPL}
    vecs['mix_norm_w'] = vecs['mix_norm_w'] + zero
    loss, dx, g_conv, g_vec = _local_step(x[0], p[0, 0], _rope_tables(positions), get_w, vecs, loss_target[0], emit,
                                          relay, settle)
    n_small = sum(g_vec[n].shape[1] for n in REPL) + sum(g_conv[n].size for n in CONV) + 1
    rows_small = -(-n_small // (LANES * HALO)) * HALO
    small = _pack_rows([g_vec[n] for n in REPL] + [g_conv[n].reshape(1, -1) for n in CONV] + [loss], rows_small)

    small = small + zero_of([out_g[GRAD_GROUPS['t'][-1]]])
    all_small = _exchange([small], gather=True, name="gather_small_grads")[0].reshape(N_DEV, rows_small * LANES)
    pieces, off = [], 0
    for n in REPL:
        k = g_vec[n].shape[1]
        pieces.append(all_small[:, off:off + k])
        off += k
    for n in CONV:
        kw, cols = g_conv[n].shape
        full = all_small[:, off:off + kw * cols].reshape(N_DEV, kw, cols)
        mine = lax.dynamic_slice_in_dim(full, me * (cols // N_DEV), cols // N_DEV, axis=2)
        pieces.append(mine.reshape(N_DEV, kw * (cols // N_DEV)))
        off += kw * cols
    pieces.append(all_small[:, off:off + 1])
    small_names = REPL + CONV
    n_mine = sum(q.shape[1] for q in pieces)
    rows_mine = -(-n_mine // (LANES * HALO)) * HALO
    zero = jnp.zeros((1, 1), F32)
    packed = [_pack_rows([src[n].reshape(1, -1) for n in small_names] + [zero], rows_mine).reshape(rows_mine, LANES)
              for src in (w2, m2, v2)]
    sg, sd, sm, sv = _adamw(_pack_rows(pieces, rows_mine), *packed, name="adamw_small")
    off = 0
    for n in small_names:
        k = w2[n].size
        for dst, src in ((out_g, sg), (out_d, sd), (out_m, sm), (out_v, sv)):
            dst[n] = src.reshape(-1)[off:off + k].reshape(w2[n].shape)
        off += k
    total_loss = sg.reshape(-1)[off]

    outs = [total_loss, dx[None]]
    for res in (out_g, out_d, out_m, out_v):
        outs += [res[n].reshape(shapes[n]) for n in WEIGHTS]
    return tuple(outs)
```

```python
import functools
import math

import numpy as np
import jax
import jax.numpy as jnp
from jax import lax
from jax.experimental import pallas as pl
from jax.experimental.pallas import tpu as pltpu
from jax.experimental.pallas import tpu_sc as plsc

F32 = jnp.float32
BF16 = jnp.bfloat16
HI = lax.Precision.HIGHEST

D_MODEL = 2048
CHUNK = 64
D_SSM = 1024
SSD_P = 64
SSD_HEADS = 16
SSD_GROUPS = 2
SSD_N = 128
SSD_CONV = 4
SSD_CONV_DIM = D_SSM + 2 * SSD_GROUPS * SSD_N
MLA_HEADS = 8
MLA_NOPE = 128
MLA_ROPE = 64
MLA_V = 128
MLA_Q_RANK = 512
MLA_KV_RANK = 256
MLA_QK_PAD = 256
ROPE_THETA = 10000.0
D_FF = 5632
FFN_CONV = 3
PLE_DIM = 256
NORM_EPS = 1e-6
ADAM_LR, ADAM_B1, ADAM_B2, ADAM_EPS, ADAM_WD, ADAM_STEP = 0.001, 0.9, 0.999, 1e-08, 0.01, 10
N_DEV = 8

OFF_Z, OFF_XBC, OFF_QA, OFF_CKV, OFF_KR, OFF_DT, D_IN_PAD = 0, 1024, 2560, 3072, 3328, 3456, 3584
D_IN = 3408
LANES = 128
HALO = 8
VMEM_LIMIT = 56 * 1024 * 1024
FFN_TC = D_FF * 2 // N_DEV
FFN_PERM = (0, 4, 1, 5, 2, 6, 3, 7)
NEG = -1e30


def _cp(*sem):
    return pltpu.CompilerParams(dimension_semantics=tuple(sem), vmem_limit_bytes=VMEM_LIMIT)


def _tile(n, want):
    if n <= want:
        return n
    best = max(d for d in range(LANES, want + 1, LANES) if n % d == 0)
    return best


def _sigmoid(x):
    return 0.5 * (jnp.tanh(0.5 * x) + 1.0)


def _silu(x):
    return x * _sigmoid(x)


def _dsilu(x):
    s = _sigmoid(x)
    return s * (1.0 + x * (1.0 - s))


MM_TILE = 1408
MM_TK = 2816


def _matmul(a, b, *, ta=False, tb=False, out_dtype=F32, add=None, bias=None, tm=MM_TILE, tn=MM_TILE, tk=MM_TK, name,
            mnk=None, a_spec=None, b_spec=None, o_spec=None, o_shape=None):
    if mnk is None:
        m, k = (a.shape[1], a.shape[0]) if ta else a.shape
        n = b.shape[0] if tb else b.shape[1]
        assert k == (b.shape[1] if tb else b.shape[0])
    else:
        m, n, k = mnk
    tm, tn, tk = _tile(m, tm), _tile(n, tn), _tile(k, tk)
    nk = k // tk
    dims = (((0 if ta else 1,), (1 if tb else 0,)), ((), ()))

    def body(*refs):
        a_ref, b_ref = refs[0], refs[1]
        pos = 2
        add_ref = bias_ref = None
        if add is not None:
            add_ref = refs[pos]
            pos += 1
        if bias is not None:
            bias_ref = refs[pos]
            pos += 1
        o_ref = refs[pos]
        kk = pl.program_id(2)
        av = a_ref[...]
        bv = b_ref[...]
        av = av.reshape(av.shape[-2:]).astype(BF16)
        bv = bv.reshape(bv.shape[-2:]).astype(BF16)
        prod = lax.dot_general(av, bv, dims, preferred_element_type=F32)

        def finish(r):
            if bias_ref is not None:
                r = r + bias_ref[...]
            if add_ref is not None:
                r = r + add_ref[...].astype(F32)
            o_ref[...] = r.astype(out_dtype).reshape(o_ref.shape)

        if nk == 1:
            finish(prod)
        else:
            acc_ref = refs[pos + 1]

            @pl.when(kk == 0)
            def _():
                acc_ref[...] = prod

            @pl.when(kk > 0)
            def _():
                acc_ref[...] += prod

            @pl.when(kk == nk - 1)
            def _():
                finish(acc_ref[...])

    if a_spec is None:
        a_spec = (pl.BlockSpec((tk, tm), lambda i, j, kk: (kk, i)) if ta
                  else pl.BlockSpec((tm, tk), lambda i, j, kk: (i, kk)))
    if b_spec is None:
        b_spec = (pl.BlockSpec((tn, tk), lambda i, j, kk: (j, kk)) if tb
                  else pl.BlockSpec((tk, tn), lambda i, j, kk: (kk, j)))
    if o_spec is None:
        o_spec = pl.BlockSpec((tm, tn), lambda i, j, kk: (i, j))
    if o_shape is None:
        o_shape = (m, n)
    in_specs = [a_spec, b_spec]
    args = [a, b]
    if add is not None:
        in_specs.append(pl.BlockSpec((tm, tn), lambda i, j, kk: (i, j)))
        args.append(add)
    if bias is not None:
        in_specs.append(pl.BlockSpec((1, tn), lambda i, j, kk: (0, j)))
        args.append(bias)
    return pl.pallas_call(
        body, name=name, grid=(m // tm, n // tn, nk), in_specs=in_specs, out_specs=o_spec,
        out_shape=jax.ShapeDtypeStruct(o_shape, out_dtype),
        scratch_shapes=[pltpu.VMEM((tm, tn), F32)] if nk > 1 else [],
        compiler_params=_cp("parallel", "parallel", "arbitrary"),
    )(*args)


def _rmsnorm_fwd(x, w, *, width, cblk=0, out_dtype=BF16, tr=256, name):
    t = x.shape[0]

    def body(x_ref, w_ref, o_ref):
        xv = x_ref[...].astype(F32)
        r = lax.rsqrt(jnp.mean(xv * xv, axis=-1, keepdims=True) + NORM_EPS)
        o_ref[...] = (xv * r * w_ref[...]).astype(out_dtype)

    return pl.pallas_call(
        body, name=name, grid=(t // tr,),
        in_specs=[pl.BlockSpec((tr, width), lambda i: (i, cblk)), pl.BlockSpec((1, width), lambda i: (0, 0))],
        out_specs=pl.BlockSpec((tr, width), lambda i: (i, 0)),
        out_shape=jax.ShapeDtypeStruct((t, width), out_dtype),
        compiler_params=_cp("parallel"),
    )(x, w)


def _rmsnorm_bwd(x, w, dy, add=None, *, width, cblk=0, out_dtype=F32, tr=256, name):
    t = x.shape[0]

    def body(*refs):
        if add is None:
            x_ref, w_ref, dy_ref, dx_ref, dw_ref = refs
            add_ref = None
        else:
            x_ref, w_ref, dy_ref, add_ref, dx_ref, dw_ref = refs
        xv = x_ref[...].astype(F32)
        dyv = dy_ref[...].astype(F32)
        r = lax.rsqrt(jnp.mean(xv * xv, axis=-1, keepdims=True) + NORM_EPS)
        xh = xv * r
        g = dyv * w_ref[...]
        dx = r * (g - xh * jnp.mean(g * xh, axis=-1, keepdims=True))
        if add_ref is not None:
            dx = dx + add_ref[...].astype(F32)
        dx_ref[...] = dx.astype(out_dtype)

        @pl.when(pl.program_id(0) == 0)
        def _():
            dw_ref[...] = jnp.zeros_like(dw_ref)

        dw_ref[...] += jnp.sum(dyv * xh, axis=0, keepdims=True)

    in_specs = [pl.BlockSpec((tr, width), lambda i: (i, cblk)), pl.BlockSpec((1, width), lambda i: (0, 0)),
                pl.BlockSpec((tr, width), lambda i: (i, 0))]
    args = [x, w, dy]
    if add is not None:
        in_specs.append(pl.BlockSpec((tr, width), lambda i: (i, 0)))
        args.append(add)
    return pl.pallas_call(
        body, name=name, grid=(t // tr,), in_specs=in_specs,
        out_specs=[pl.BlockSpec((tr, width), lambda i: (i, 0)), pl.BlockSpec((1, width), lambda i: (0, 0))],
        out_shape=[jax.ShapeDtypeStruct((t, width), out_dtype), jax.ShapeDtypeStruct((1, width), F32)],
        compiler_params=_cp("arbitrary"),
    )(*args)


def _shift_down(prev_halo, cur, j):
    if j == 0:
        return cur
    ext = jnp.concatenate([prev_halo, cur], axis=0)
    return pltpu.roll(ext, j, axis=0)[HALO:]


def _shift_up(cur, next_halo, j):
    if j == 0:
        return cur
    ext = jnp.concatenate([cur, next_halo], axis=0)
    return pltpu.roll(ext, ext.shape[0] - j, axis=0)[:cur.shape[0]]


def _conv_rows(prev, cur, w, b, kw):
    shifted = [cur]
    out = b + w[kw - 1:kw] * cur
    for j in range(1, kw):
        sh = _shift_down(prev, cur, j)
        shifted.append(sh)
        out = out + w[kw - 1 - j:kw - j] * sh
    return out, shifted


def _act_fwd(c, glu):
    if glu:
        half = c.shape[1] // 2
        return _silu(c[:, :half]) * c[:, half:]
    return _silu(c)


def _act_bwd(c, dout, glu):
    if glu:
        half = c.shape[1] // 2
        g, up = c[:, :half], c[:, half:]
        s = _sigmoid(g)
        gs = g * s
        return jnp.concatenate([dout * up * (s + gs * (1.0 - s)), dout * gs], axis=1)
    return dout * _dsilu(c)


def _conv_act_fwd(u, w, b, *, kw, glu, tc, coff, ncols, out_dtype, tr=256, name):
    t = u.shape[0]
    nb = ncols // tc
    oc = tc // 2 if glu else tc

    def body(u_ref, uh_ref, w_ref, b_ref, o_ref):
        prev = jnp.where(pl.program_id(0) == 0, 0.0, uh_ref[...])
        c, _ = _conv_rows(prev, u_ref[...], w_ref[...], b_ref[...], kw)
        o_ref[...] = _act_fwd(c, glu).astype(out_dtype)

    return pl.pallas_call(
        body, name=name, grid=(t // tr, nb),
        in_specs=[pl.BlockSpec((tr, tc), lambda i, j: (i, j + coff)),
                  pl.BlockSpec((HALO, tc), lambda i, j: (jnp.maximum(i * (tr // HALO) - 1, 0), j + coff)),
                  pl.BlockSpec((kw, tc), lambda i, j: (0, j)), pl.BlockSpec((1, tc), lambda i, j: (0, j))],
        out_specs=pl.BlockSpec((tr, oc), lambda i, j: (i, j)),
        out_shape=jax.ShapeDtypeStruct((t, nb * oc), out_dtype),
        compiler_params=_cp("parallel", "parallel"),
    )(u, u, w, b)


def _conv_act_bwd(u, w, b, dout, *, kw, glu, tc, coff, ncols, tr=256, name):
    t = u.shape[0]
    nb = ncols // tc
    nt = t // tr
    oc = tc // 2 if glu else tc

    def body(u_ref, up_ref, un_ref, d_ref, dn_ref, w_ref, b_ref, du_ref, dw_ref, db_ref):
        i = pl.program_id(1)
        cur, nxt, wv, bv = u_ref[...], un_ref[...], w_ref[...], b_ref[...]
        prev = jnp.where(i == 0, 0.0, up_ref[...])
        c_cur, shifted = _conv_rows(prev, cur, wv, bv, kw)
        c_nxt, _ = _conv_rows(cur[tr - HALO:], nxt, wv, bv, kw)
        d_cur = _act_bwd(c_cur, d_ref[...].astype(F32), glu)
        d_nxt = _act_bwd(c_nxt, jnp.where(i == nt - 1, 0.0, dn_ref[...].astype(F32)), glu)
        du = wv[kw - 1:kw] * d_cur
        for j in range(1, kw):
            du = du + wv[kw - 1 - j:kw - j] * _shift_up(d_cur, d_nxt, j)
        du_ref[...] = du.astype(BF16)

        @pl.when(i == 0)
        def _():
            dw_ref[...] = jnp.zeros_like(dw_ref)
            db_ref[...] = jnp.zeros_like(db_ref)

        db_ref[...] += jnp.sum(d_cur, axis=0, keepdims=True)
        dw_ref[...] += jnp.concatenate(
            [jnp.sum(d_cur * shifted[kw - 1 - k], axis=0, keepdims=True) for k in range(kw)], axis=0)

    nh = tr // HALO
    return pl.pallas_call(
        body, name=name, grid=(nb, nt),
        in_specs=[pl.BlockSpec((tr, tc), lambda j, i: (i, j + coff)),
                  pl.BlockSpec((HALO, tc), lambda j, i: (jnp.maximum(i * nh - 1, 0), j + coff)),
                  pl.BlockSpec((HALO, tc), lambda j, i: (jnp.minimum((i + 1) * nh, t // HALO - 1), j + coff)),
                  pl.BlockSpec((tr, oc), lambda j, i: (i, j)),
                  pl.BlockSpec((HALO, oc), lambda j, i: (jnp.minimum((i + 1) * nh, t // HALO - 1), j)),
                  pl.BlockSpec((kw, tc), lambda j, i: (0, j)), pl.BlockSpec((1, tc), lambda j, i: (0, j))],
        out_specs=[pl.BlockSpec((tr, tc), lambda j, i: (i, j)), pl.BlockSpec((kw, tc), lambda j, i: (0, j)),
                   pl.BlockSpec((1, tc), lambda j, i: (0, j))],
        out_shape=[jax.ShapeDtypeStruct((t, ncols), BF16), jax.ShapeDtypeStruct((kw, ncols), F32),
                   jax.ShapeDtypeStruct((1, ncols), F32)],
        compiler_params=_cp("parallel", "arbitrary"),
    )(u, u, u, dout, dout, w, b)


def _ple_fwd(x2, gl, pe, pw, *, tr=256, name):
    t, d = x2.shape

    def body(x_ref, gl_ref, pe_ref, pw_ref, o_ref):
        pv = pe_ref[...]
        r = lax.rsqrt(jnp.mean(pv * pv, axis=-1, keepdims=True) + NORM_EPS)
        o_ref[...] = x_ref[...] + _sigmoid(gl_ref[...]) * (pv * r * pw_ref[...])

    blk = pl.BlockSpec((tr, d), lambda i: (i, 0))
    return pl.pallas_call(
        body, name=name, grid=(t // tr,), in_specs=[blk, blk, blk, pl.BlockSpec((1, d), lambda i: (0, 0))],
        out_specs=blk, out_shape=jax.ShapeDtypeStruct((t, d), F32), compiler_params=_cp("parallel"),
    )(x2, gl, pe, pw)


def _ple_bwd(dx3, gl, pe, pw, *, tr=256, name):
    t, d = dx3.shape

    def body(dx_ref, gl_ref, pe_ref, pw_ref, dgl_ref, db_ref, dpe_ref, dpw_ref):
        dx, pv, pwv = dx_ref[...], pe_ref[...], pw_ref[...]
        gate = _sigmoid(gl_ref[...])
        r = lax.rsqrt(jnp.mean(pv * pv, axis=-1, keepdims=True) + NORM_EPS)
        ph = pv * r
        dgl = dx * (ph * pwv) * gate * (1.0 - gate)
        de = dx * gate
        g = de * pwv
        dgl_ref[...] = dgl.astype(BF16)
        dpe_ref[...] = (r * (g - ph * jnp.mean(g * ph, axis=-1, keepdims=True))).astype(BF16)

        @pl.when(pl.program_id(0) == 0)
        def _():
            db_ref[...] = jnp.zeros_like(db_ref)
            dpw_ref[...] = jnp.zeros_like(dpw_ref)

        db_ref[...] += jnp.sum(dgl, axis=0, keepdims=True)
        dpw_ref[...] += jnp.sum(de * ph, axis=0, keepdims=True)

    blk = pl.BlockSpec((tr, d), lambda i: (i, 0))
    row = pl.BlockSpec((1, d), lambda i: (0, 0))
    return pl.pallas_call(
        body, name=name, grid=(t // tr,), in_specs=[blk, blk, blk, row], out_specs=[blk, row, blk, row],
        out_shape=[jax.ShapeDtypeStruct((t, d), BF16), jax.ShapeDtypeStruct((1, d), F32),
                   jax.ShapeDtypeStruct((t, d), BF16), jax.ShapeDtypeStruct((1, d), F32)],
        compiler_params=_cp("arbitrary"),
    )(dx3, gl, pe, pw)


def _loss_head(x3, fw, target, *, tr=256, name):
    t, d = x3.shape

    def body(x_ref, w_ref, t_ref, l_ref, dx_ref, dw_ref):
        xv, wv = x_ref[...], w_ref[...]
        r = lax.rsqrt(jnp.mean(xv * xv, axis=-1, keepdims=True) + NORM_EPS)
        xh = xv * r
        err = xh * wv - t_ref[...]
        dy = err * (1.0 / d)
        g = dy * wv
        dx_ref[...] = r * (g - xh * jnp.mean(g * xh, axis=-1, keepdims=True))

        @pl.when(pl.program_id(0) == 0)
        def _():
            l_ref[...] = jnp.zeros_like(l_ref)
            dw_ref[...] = jnp.zeros_like(dw_ref)

        l_ref[...] += 0.5 * jnp.sum(jnp.mean(err * err, axis=-1, keepdims=True), axis=0, keepdims=True)
        dw_ref[...] += jnp.sum(dy * xh, axis=0, keepdims=True)

    blk = pl.BlockSpec((tr, d), lambda i: (i, 0))
    row = pl.BlockSpec((1, d), lambda i: (0, 0))
    return pl.pallas_call(
        body, name=name, grid=(t // tr,), in_specs=[blk, row, blk],
        out_specs=[pl.BlockSpec((1, 1), lambda i: (0, 0)), blk, row],
        out_shape=[jax.ShapeDtypeStruct((1, 1), F32), jax.ShapeDtypeStruct((t, d), F32),
                   jax.ShapeDtypeStruct((1, d), F32)],
        compiler_params=_cp("arbitrary"),
    )(x3, fw, target)


def _rope(blk, tab_ref):
    return blk * tab_ref[0] + pltpu.roll(blk, 96, axis=1) * tab_ref[1] + pltpu.roll(blk, 32, axis=1) * tab_ref[2]


def _unrope(g, tab_ref):
    return g * tab_ref[0] + pltpu.roll(g * tab_ref[1], 32, axis=1) + pltpu.roll(g * tab_ref[2], 96, axis=1)


def _mla_prep(q, kv, proj, tabs, *, tr=512, name):
    t = q.shape[0]

    def body(q_ref, kv_ref, kr_ref, tab_ref, qo_ref, ko_ref, vo_ref):
        qv, kvv = q_ref[...], kv_ref[...]
        qo_ref[0, :, :MLA_NOPE] = qv[:, :MLA_NOPE].astype(BF16)
        qo_ref[0, :, MLA_NOPE:] = _rope(qv[:, MLA_NOPE:], tab_ref).astype(BF16)
        ko_ref[0, :, :MLA_NOPE] = kvv[:, :MLA_NOPE].astype(BF16)
        ko_ref[0, :, MLA_NOPE:] = _rope(kr_ref[...], tab_ref).astype(BF16)
        vo_ref[0] = kvv[:, MLA_NOPE:].astype(BF16)

    return pl.pallas_call(
        body, name=name, grid=(t // tr, MLA_HEADS),
        in_specs=[pl.BlockSpec((tr, MLA_QK_PAD), lambda i, h: (i, h)),
                  pl.BlockSpec((tr, MLA_NOPE + MLA_V), lambda i, h: (i, h)),
                  pl.BlockSpec((tr, LANES), lambda i, h: (i, OFF_KR // LANES)),
                  pl.BlockSpec((3, tr, LANES), lambda i, h: (0, i, 0))],
        out_specs=[pl.BlockSpec((1, tr, MLA_QK_PAD), lambda i, h: (h, i, 0)),
                   pl.BlockSpec((1, tr, MLA_QK_PAD), lambda i, h: (h, i, 0)),
                   pl.BlockSpec((1, tr, MLA_V), lambda i, h: (h, i, 0))],
        out_shape=[jax.ShapeDtypeStruct((MLA_HEADS, t, MLA_QK_PAD), BF16),
                   jax.ShapeDtypeStruct((MLA_HEADS, t, MLA_QK_PAD), BF16),
                   jax.ShapeDtypeStruct((MLA_HEADS, t, MLA_V), BF16)],
        compiler_params=_cp("parallel", "parallel"),
    )(q, kv, proj, tabs)


def _mla_unprep(dq3, dk3, dv3, tabs, *, tr=256, name):
    t = dq3.shape[1]

    def body(dq_ref, dk_ref, dv_ref, tab_ref, qo_ref, kvo_ref, kro_ref):
        kr = jnp.zeros((tr, LANES), F32)
        for h in range(MLA_HEADS):
            c0 = h * MLA_QK_PAD
            qo_ref[:, c0:c0 + MLA_NOPE] = dq_ref[h, :, :MLA_NOPE].astype(BF16)
            qo_ref[:, c0 + MLA_NOPE:c0 + MLA_QK_PAD] = _unrope(dq_ref[h, :, MLA_NOPE:], tab_ref).astype(BF16)
            kvo_ref[:, c0:c0 + MLA_NOPE] = dk_ref[h, :, :MLA_NOPE].astype(BF16)
            kvo_ref[:, c0 + MLA_NOPE:c0 + MLA_QK_PAD] = dv_ref[h].astype(BF16)
            kr = kr + dk_ref[h, :, MLA_NOPE:]
        kro_ref[...] = _unrope(kr, tab_ref).astype(BF16)

    return pl.pallas_call(
        body, name=name, grid=(t // tr,),
        in_specs=[pl.BlockSpec((MLA_HEADS, tr, MLA_QK_PAD), lambda i: (0, i, 0)),
                  pl.BlockSpec((MLA_HEADS, tr, MLA_QK_PAD), lambda i: (0, i, 0)),
                  pl.BlockSpec((MLA_HEADS, tr, MLA_V), lambda i: (0, i, 0)),
                  pl.BlockSpec((3, tr, LANES), lambda i: (0, i, 0))],
        out_specs=[pl.BlockSpec((tr, MLA_HEADS * MLA_QK_PAD), lambda i: (i, 0)),
                   pl.BlockSpec((tr, MLA_HEADS * MLA_QK_PAD), lambda i: (i, 0)),
                   pl.BlockSpec((tr, LANES), lambda i: (i, 0))],
        out_shape=[jax.ShapeDtypeStruct((t, MLA_HEADS * MLA_QK_PAD), BF16),
                   jax.ShapeDtypeStruct((t, MLA_HEADS * MLA_QK_PAD), BF16),
                   jax.ShapeDtypeStruct((t, LANES), BF16)],
        compiler_params=_cp("parallel"),
    )(dq3, dk3, dv3, tabs)


ATT_BLK = 256
ATT_SCALE = 1.0 / math.sqrt(MLA_NOPE + MLA_ROPE)
_NT = (((1,), (1,)), ((), ()))
_TN = (((0,), (0,)), ((), ()))


def _att_scores(q, k, diagonal):
    s = lax.dot_general(q, k, _NT, preferred_element_type=F32) * ATT_SCALE
    if not diagonal:
        return s
    row = lax.broadcasted_iota(jnp.int32, s.shape, 0)
    col = lax.broadcasted_iota(jnp.int32, s.shape, 1)
    return jnp.where((col >> 6) <= (row >> 6), s, NEG)


def _att_rows(i):
    return pl.ds(pl.multiple_of(i * ATT_BLK, ATT_BLK), ATT_BLK)


def _attn_fwd(q3, k3, v3, *, name):
    t = q3.shape[1]
    nq = t // ATT_BLK

    def body(q_ref, k_ref, v_ref, o_ref, lse_ref):
        qi = pl.program_id(1)
        q = q_ref[0]

        def step(j, carry, diagonal=False):
            m, l, acc = carry
            s = _att_scores(q, k_ref[0, _att_rows(j), :], diagonal)
            m_new = jnp.maximum(m, jnp.max(s, axis=-1, keepdims=True))
            p = jnp.exp(s - m_new)
            alpha = jnp.exp(m - m_new)
            l = alpha * l + jnp.sum(p, axis=-1, keepdims=True)
            acc = alpha * acc + jnp.dot(p.astype(BF16), v_ref[0, _att_rows(j), :], preferred_element_type=F32)
            return m_new, l, acc

        init = (jnp.full((ATT_BLK, 1), NEG, F32), jnp.zeros((ATT_BLK, 1), F32), jnp.zeros((ATT_BLK, MLA_V), F32))
        m, l, acc = step(qi, lax.fori_loop(0, qi, step, init), diagonal=True)
        o_ref[...] = acc / l
        lse_ref[0] = m + jnp.log(l)

    return pl.pallas_call(
        body, name=name, grid=(MLA_HEADS, nq),
        in_specs=[pl.BlockSpec((1, ATT_BLK, MLA_QK_PAD), lambda h, i: (h, i, 0)),
                  pl.BlockSpec((1, t, MLA_QK_PAD), lambda h, i: (h, 0, 0)),
                  pl.BlockSpec((1, t, MLA_V), lambda h, i: (h, 0, 0))],
        out_specs=[pl.BlockSpec((ATT_BLK, MLA_V), lambda h, i: (i, h)),
                   pl.BlockSpec((1, ATT_BLK, 1), lambda h, i: (h, i, 0))],
        out_shape=[jax.ShapeDtypeStruct((t, MLA_HEADS * MLA_V), F32), jax.ShapeDtypeStruct((MLA_HEADS, t, 1), F32)],
        compiler_params=_cp("parallel", "parallel"),
    )(q3, k3, v3)


def _attn_bwd(q3, k3, v3, o, dcat, lse, *, name):
    t = q3.shape[1]
    nq = t // ATT_BLK

    def body(q_ref, k_ref, v_ref, o_ref, do_ref, lse_ref, dq_ref, dk_ref, dv_ref, delta_ref):
        kj = pl.program_id(1)
        k, v = k_ref[0], v_ref[0]

        @pl.when(kj == 0)
        def _():
            dq_ref[...] = jnp.zeros_like(dq_ref)
            delta_ref[...] = jnp.sum(o_ref[...] * do_ref[...], axis=-1, keepdims=True)

        def step(i, carry, diagonal=False):
            dk, dv = carry
            rows = _att_rows(i)
            q = q_ref[0, rows, :]
            dob = do_ref[rows, :].astype(BF16)
            p = jnp.exp(_att_scores(q, k, diagonal) - lse_ref[0, rows, :])
            dv = dv + lax.dot_general(p.astype(BF16), dob, _TN, preferred_element_type=F32)
            dp = lax.dot_general(dob, v, _NT, preferred_element_type=F32)
            ds = (p * (dp - delta_ref[rows, :]) * ATT_SCALE).astype(BF16)
            dk = dk + lax.dot_general(ds, q, _TN, preferred_element_type=F32)
            dq_ref[0, rows, :] += jnp.dot(ds, k, preferred_element_type=F32)
            return dk, dv

        init = (jnp.zeros((ATT_BLK, MLA_QK_PAD), F32), jnp.zeros((ATT_BLK, MLA_V), F32))
        dk, dv = lax.fori_loop(kj + 1, nq, step, step(kj, init, diagonal=True))
        dk_ref[0] = dk
        dv_ref[0] = dv

    return pl.pallas_call(
        body, name=name, grid=(MLA_HEADS, nq),
        in_specs=[pl.BlockSpec((1, t, MLA_QK_PAD), lambda h, j: (h, 0, 0)),
                  pl.BlockSpec((1, ATT_BLK, MLA_QK_PAD), lambda h, j: (h, j, 0)),
                  pl.BlockSpec((1, ATT_BLK, MLA_V), lambda h, j: (h, j, 0)),
                  pl.BlockSpec((t, MLA_V), lambda h, j: (0, h)),
                  pl.BlockSpec((t, MLA_V), lambda h, j: (0, MLA_HEADS + h)),
                  pl.BlockSpec((1, t, 1), lambda h, j: (h, 0, 0))],
        out_specs=[pl.BlockSpec((1, t, MLA_QK_PAD), lambda h, j: (h, 0, 0)),
                   pl.BlockSpec((1, ATT_BLK, MLA_QK_PAD), lambda h, j: (h, j, 0)),
                   pl.BlockSpec((1, ATT_BLK, MLA_V), lambda h, j: (h, j, 0))],
        out_shape=[jax.ShapeDtypeStruct((MLA_HEADS, t, MLA_QK_PAD), F32),
                   jax.ShapeDtypeStruct((MLA_HEADS, t, MLA_QK_PAD), F32),
                   jax.ShapeDtypeStruct((MLA_HEADS, t, MLA_V), F32)],
        scratch_shapes=[pltpu.VMEM((t, 1), F32)],
        compiler_params=_cp("parallel", "arbitrary"),
    )(q3, k3, v3, o, dcat, lse)


def _ssd_prep(proj, bias128, alog128, *, name):
    t = proj.shape[0]
    nc = t // CHUNK

    def body(raw_ref, b_ref, al_ref, dt_ref, cs_ref, a_ref):
        xv = raw_ref[...] + b_ref[...]
        dt = jnp.maximum(xv, 0.0) + jnp.log(1.0 + jnp.exp(-jnp.abs(xv)))
        a = -jnp.exp(al_ref[...])
        adt = (dt * a).reshape(nc, CHUNK, LANES)
        li = lax.broadcasted_iota(jnp.int32, (nc, CHUNK, CHUNK), 1)
        si = lax.broadcasted_iota(jnp.int32, (nc, CHUNK, CHUNK), 2)
        tril = jnp.where(si <= li, 1.0, 0.0).astype(F32)
        cs = lax.dot_general(tril, adt, (((2,), (1,)), ((0,), (0,))), precision=HI, preferred_element_type=F32)
        dt_ref[...] = dt
        cs_ref[...] = cs.reshape(t, LANES)
        a_ref[...] = a

    blk = pl.BlockSpec((t, LANES), lambda i: (0, 0))
    row = pl.BlockSpec((1, LANES), lambda i: (0, 0))
    return pl.pallas_call(
        body, name=name, grid=(1,),
        in_specs=[pl.BlockSpec((t, LANES), lambda i: (0, OFF_DT // LANES)), row, row],
        out_specs=[blk, blk, row],
        out_shape=[jax.ShapeDtypeStruct((t, LANES), F32), jax.ShapeDtypeStruct((t, LANES), F32),
                   jax.ShapeDtypeStruct((1, LANES), F32)],
        compiler_params=_cp("arbitrary"),
    )(proj, bias128, alog128)


def _ssd_prep_bwd(ddt128, dadt128, proj, bias128, dt128, a128, dd_h, *, name):
    t = proj.shape[0]

    def body(ddt_ref, dadt_ref, raw_ref, b_ref, dt_ref, a_ref, dd_ref, draw_ref, db_ref, dal_ref, dds_ref):
        draw = ddt_ref[...] * _sigmoid(raw_ref[...] + b_ref[...])
        draw_ref[...] = draw.astype(BF16)
        db_ref[...] = jnp.sum(draw, axis=0, keepdims=True)
        dal_ref[...] = jnp.sum(dadt_ref[...] * dt_ref[...], axis=0, keepdims=True) * a_ref[...]
        dds_ref[...] = jnp.sum(dd_ref[...], axis=-1, keepdims=True)

    blk = pl.BlockSpec((t, LANES), lambda i: (0, 0))
    row = pl.BlockSpec((1, LANES), lambda i: (0, 0))
    return pl.pallas_call(
        body, name=name, grid=(1,),
        in_specs=[blk, blk, pl.BlockSpec((t, LANES), lambda i: (0, OFF_DT // LANES)), row, blk, row,
                  pl.BlockSpec((SSD_HEADS, SSD_P), lambda i: (0, 0))],
        out_specs=[blk, row, row, pl.BlockSpec((SSD_HEADS, 1), lambda i: (0, 0))],
        out_shape=[jax.ShapeDtypeStruct((t, LANES), BF16), jax.ShapeDtypeStruct((1, LANES), F32),
                   jax.ShapeDtypeStruct((1, LANES), F32), jax.ShapeDtypeStruct((SSD_HEADS, 1), F32)],
        compiler_params=_cp("arbitrary"),
    )(ddt128, dadt128, proj, bias128, dt128, a128, dd_h)


def _bdot(a, b, ca, cb, precision=None):
    return lax.dot_general(a, b, (((ca,), (cb,)), ((0,), (0,))), precision=precision, preferred_element_type=F32)


def _head_matrices():
    eye, zero = jnp.eye(SSD_P, dtype=F32), jnp.zeros((SSD_P, SSD_P), F32)
    pick = jnp.stack([jnp.concatenate([eye, zero], axis=0), jnp.concatenate([zero, eye], axis=0)])
    return pick, pick.transpose(0, 2, 1)


def _pick_head(pair_ref, pick_ref, h):
    return jnp.dot(pair_ref[...], pick_ref[h % 2], precision=HI, preferred_element_type=F32)


def _place_head(out_ref, val, place_ref, h):
    wide = jnp.dot(val, place_ref[h % 2], precision=HI, preferred_element_type=F32)

    @pl.when(h % 2 == 0)
    def _():
        out_ref[...] = wide

    @pl.when(h % 2 == 1)
    def _():
        out_ref[...] += wide


def _ssd_common(x2, dt_ref, cs_ref, csr_ref, b_ref, c_ref, nc):
    x = x2.reshape(nc, CHUNK, SSD_P)
    dt = dt_ref[0].reshape(nc, CHUNK, SSD_P)
    cs = cs_ref[0].reshape(nc, CHUNK, SSD_P)
    csr = csr_ref[0]
    bm = b_ref[...].reshape(nc, CHUNK, SSD_N).astype(BF16)
    cm = c_ref[...].reshape(nc, CHUNK, SSD_N).astype(BF16)
    li = lax.broadcasted_iota(jnp.int32, (nc, CHUNK, CHUNK), 1)
    si = lax.broadcasted_iota(jnp.int32, (nc, CHUNK, CHUNK), 2)
    lmat = jnp.exp(jnp.where(si <= li, cs - csr, NEG))
    g = _bdot(cm, bm, 2, 2)
    cs_last = jnp.sum(jnp.where(li == CHUNK - 1, cs, 0.0), axis=1, keepdims=True)
    xdt = x * dt
    dec = jnp.exp(cs_last - cs)
    return x, dt, cs, bm, cm, li, si, lmat, g, cs_last, xdt, dec


def _ssd_fwd(xbc, dt_h, cs_h, cs_row, dskip_h, *, name):
    t = xbc.shape[0]
    nc = t // CHUNK
    hpg = SSD_HEADS // SSD_GROUPS
    pick, place = _head_matrices()

    def body(xs_ref, dt_ref, cs_ref, csr_ref, b_ref, c_ref, dk_ref, pick_ref, place_ref, y_ref, st_ref, sc_ref, cd_ref):
        h = pl.program_id(0)
        x, dt, cs, bm, cm, li, si, lmat, g, cs_last, xdt, dec = _ssd_common(_pick_head(xs_ref, pick_ref, h), dt_ref,
                                                                           cs_ref, csr_ref, b_ref, c_ref, nc)
        yd = _bdot((g * lmat).astype(BF16), xdt.astype(BF16), 2, 1)
        sc_ref[...] = _bdot(bm, (dec * xdt).astype(BF16), 1, 1)
        cd_ref[...] = jnp.exp(cs_last)

        def step(c, s):
            st_ref[0, c] = s
            return s * cd_ref[c] + sc_ref[c]

        lax.fori_loop(0, nc, step, jnp.zeros((SSD_N, SSD_P), F32))
        yo = _bdot(cm, st_ref[0].astype(BF16), 2, 1) * jnp.exp(cs)
        _place_head(y_ref, (yd + yo + dk_ref[0] * x).reshape(t, SSD_P), place_ref, h)

    head = pl.BlockSpec((1, t, SSD_P), lambda h: (h, 0, 0))
    pair = pl.BlockSpec((t, 2 * SSD_P), lambda h: (0, h // 2))
    nxb = D_SSM // SSD_N
    return pl.pallas_call(
        body, name=name, grid=(SSD_HEADS,),
        in_specs=[pair, head, head, pl.BlockSpec((1, nc, 1, CHUNK), lambda h: (h, 0, 0, 0)),
                  pl.BlockSpec((t, SSD_N), lambda h: (0, nxb + h // hpg)),
                  pl.BlockSpec((t, SSD_N), lambda h: (0, nxb + SSD_GROUPS + h // hpg)),
                  pl.BlockSpec((1, 1, SSD_P), lambda h: (h, 0, 0)),
                  pl.BlockSpec((2, 2 * SSD_P, SSD_P), lambda h: (0, 0, 0)),
                  pl.BlockSpec((2, SSD_P, 2 * SSD_P), lambda h: (0, 0, 0))],
        out_specs=[pair, pl.BlockSpec((1, nc, SSD_N, SSD_P), lambda h: (h, 0, 0, 0))],
        out_shape=[jax.ShapeDtypeStruct((t, D_SSM), F32),
                   jax.ShapeDtypeStruct((SSD_HEADS, nc, SSD_N, SSD_P), F32)],
        scratch_shapes=[pltpu.VMEM((nc, SSD_N, SSD_P), F32), pltpu.VMEM((nc, 1, SSD_P), F32)],
        compiler_params=_cp("arbitrary"),
    )(xbc, dt_h, cs_h, cs_row, xbc, xbc, dskip_h, pick, place)


def _ssd_bwd(xbc, dt_h, cs_h, cs_row, dskip_h, a_h, states, dy, *, name):
    t = xbc.shape[0]
    nc = t // CHUNK
    hpg = SSD_HEADS // SSD_GROUPS
    pick, place = _head_matrices()

    def body(xs_ref, dt_ref, cs_ref, csr_ref, b_ref, c_ref, dk_ref, a_ref, st_ref, dy_ref, pick_ref, place_ref,
             dxs_ref, ddt_ref, dadt_ref, db_ref, dc_ref, dd_ref, dsl_ref, dsc_ref, cd_ref):
        h = pl.program_id(0) * hpg + pl.program_id(1)
        x, dt, cs, bm, cm, li, si, lmat, g, cs_last, xdt, dec = _ssd_common(_pick_head(xs_ref, pick_ref, h), dt_ref,
                                                                           cs_ref, csr_ref, b_ref, c_ref, nc)
        dy = _pick_head(dy_ref, pick_ref, h).reshape(nc, CHUNK, SSD_P)
        dyb = dy.astype(BF16)
        xdtb = xdt.astype(BF16)
        sprev = st_ref[0]
        sprevb = sprev.astype(BF16)
        cdec = jnp.exp(cs_last)
        ecs = jnp.exp(cs)
        dw = (ecs * dy).astype(BF16)
        wmat = _bdot(cm, sprevb, 2, 1)
        dcs = jnp.sum(dy * ecs * wmat, axis=2, keepdims=True)
        dcm = _bdot(dw, sprevb, 2, 2)
        dsl_ref[...] = _bdot(cm, dw, 1, 1)
        cd_ref[...] = cdec

        def step(k, ds):
            c = nc - 1 - k
            dsc_ref[c] = ds
            return ds * cd_ref[c] + dsl_ref[c]

        lax.fori_loop(0, nc, step, jnp.zeros((SSD_N, SSD_P), F32))
        dsc = dsc_ref[...]
        dscb = dsc.astype(BF16)
        d_last = jnp.sum(jnp.sum(dsc * sprev, axis=1, keepdims=True) * cdec, axis=2, keepdims=True)
        z = dec * xdt
        dbm = _bdot(z.astype(BF16), dscb, 2, 2)
        dz = _bdot(bm, dscb, 2, 1)
        dxdt = dec * dz
        t2 = jnp.sum(dz * z, axis=2, keepdims=True)
        dcs = dcs - t2
        d_last = d_last + jnp.sum(t2, axis=1, keepdims=True)
        m = g * lmat
        mb = m.astype(BF16)
        dm = _bdot(dyb, xdtb, 2, 2)
        dxdt = dxdt + _bdot(mb, dyb, 1, 1)
        dseg = dm * m
        dcs = dcs + jnp.sum(dseg, axis=2, keepdims=True)
        ones = jnp.ones((nc, CHUNK, SSD_P), F32)
        dcs = dcs - _bdot(dseg, ones, 1, 1, precision=HI)
        dg = (dm * lmat).astype(BF16)
        dcm = dcm + _bdot(dg, bm, 2, 1)
        dbm = dbm + _bdot(dg, cm, 1, 1)
        dcs = dcs + jnp.where(li[:, :, :SSD_P] == CHUNK - 1, d_last, 0.0)
        triu = jnp.where(li <= si, 1.0, 0.0).astype(F32)
        dadt = _bdot(triu, dcs, 2, 1, precision=HI)
        dk = dk_ref[0]
        _place_head(dxs_ref, (dxdt * dt + dk * dy).reshape(t, SSD_P), place_ref, h)
        ddt = jnp.sum(dxdt * x, axis=2, keepdims=True) + dadt * a_ref[0]
        mine = lax.broadcasted_iota(jnp.int32, (t, LANES), 1) == h

        @pl.when(h == 0)
        def _():
            ddt_ref[...] = jnp.zeros_like(ddt_ref)
            dadt_ref[...] = jnp.zeros_like(dadt_ref)

        ddt_ref[...] += jnp.where(mine, jnp.max(ddt, axis=2, keepdims=True).reshape(t, 1), 0.0)
        dadt_ref[...] += jnp.where(mine, jnp.max(dadt, axis=2, keepdims=True).reshape(t, 1), 0.0)
        dd_ref[0] = jnp.sum(jnp.sum(dy * x, axis=1, keepdims=True), axis=0)

        @pl.when(pl.program_id(1) == 0)
        def _():
            db_ref[...] = jnp.zeros_like(db_ref)
            dc_ref[...] = jnp.zeros_like(dc_ref)

        db_ref[...] += dbm.reshape(t, SSD_N)
        dc_ref[...] += dcm.reshape(t, SSD_N)

    head = pl.BlockSpec((1, t, SSD_P), lambda gi, hi: (gi * hpg + hi, 0, 0))
    pair = pl.BlockSpec((t, 2 * SSD_P), lambda gi, hi: (0, (gi * hpg + hi) // 2))
    grp = pl.BlockSpec((t, SSD_N), lambda gi, hi: (0, gi))
    lane = pl.BlockSpec((1, 1, SSD_P), lambda gi, hi: (gi * hpg + hi, 0, 0))
    rows = pl.BlockSpec((t, LANES), lambda gi, hi: (0, 0))
    nxb = D_SSM // SSD_N
    dxs, ddt, dadt, db, dc, dd = pl.pallas_call(
        body, name=name, grid=(SSD_GROUPS, hpg),
        in_specs=[pair, head, head, pl.BlockSpec((1, nc, 1, CHUNK), lambda gi, hi: (gi * hpg + hi, 0, 0, 0)),
                  pl.BlockSpec((t, SSD_N), lambda gi, hi: (0, nxb + gi)),
                  pl.BlockSpec((t, SSD_N), lambda gi, hi: (0, nxb + SSD_GROUPS + gi)), lane, lane,
                  pl.BlockSpec((1, nc, SSD_N, SSD_P), lambda gi, hi: (gi * hpg + hi, 0, 0, 0)), pair,
                  pl.BlockSpec((2, 2 * SSD_P, SSD_P), lambda gi, hi: (0, 0, 0)),
                  pl.BlockSpec((2, SSD_P, 2 * SSD_P), lambda gi, hi: (0, 0, 0))],
        out_specs=[pair, rows, rows, grp, grp, lane],
        out_shape=[jax.ShapeDtypeStruct((t, D_SSM), F32)] + [jax.ShapeDtypeStruct((t, LANES), F32)] * 2
        + [jax.ShapeDtypeStruct((t, SSD_GROUPS * SSD_N), F32)] * 2
        + [jax.ShapeDtypeStruct((SSD_HEADS, 1, SSD_P), F32)],
        scratch_shapes=[pltpu.VMEM((nc, SSD_N, SSD_P), F32), pltpu.VMEM((nc, SSD_N, SSD_P), F32),
                        pltpu.VMEM((nc, 1, SSD_P), F32)],
        compiler_params=_cp("arbitrary", "arbitrary"),
    )(xbc, dt_h, cs_h, cs_row, xbc, xbc, dskip_h, a_h, states, dy, pick, place)
    return jnp.concatenate([dxs, db, dc], axis=1), ddt, dadt, dd


def _ssd_gate_fwd(y, proj, w, *, tr=256, name):
    t = y.shape[0]
    gw = D_SSM // SSD_GROUPS

    def body(y_ref, z_ref, w_ref, o_ref):
        v = y_ref[...] * _silu(z_ref[...])
        for gi in range(SSD_GROUPS):
            vg = v[:, gi * gw:(gi + 1) * gw]
            r = lax.rsqrt(jnp.mean(vg * vg, axis=-1, keepdims=True) + NORM_EPS)
            o_ref[:, gi * gw:(gi + 1) * gw] = (vg * r * w_ref[:, gi * gw:(gi + 1) * gw]).astype(BF16)

    blk = pl.BlockSpec((tr, D_SSM), lambda i: (i, 0))
    return pl.pallas_call(
        body, name=name, grid=(t // tr,), in_specs=[blk, blk, pl.BlockSpec((1, D_SSM), lambda i: (0, 0))],
        out_specs=blk, out_shape=jax.ShapeDtypeStruct((t, D_SSM), BF16), compiler_params=_cp("parallel"),
    )(y, proj, w)


def _ssd_gate_bwd(y, proj, w, dcat, *, tr=256, name):
    t = y.shape[0]
    gw = D_SSM // SSD_GROUPS

    def body(y_ref, z_ref, w_ref, d_ref, dy_ref, dz_ref, dw_ref):
        yv, zv, dv = y_ref[...], z_ref[...], d_ref[...].astype(F32)
        sz = _silu(zv)
        v = yv * sz

        @pl.when(pl.program_id(0) == 0)
        def _():
            dw_ref[...] = jnp.zeros_like(dw_ref)

        for gi in range(SSD_GROUPS):
            sl = slice(gi * gw, (gi + 1) * gw)
            vg, dg = v[:, sl], dv[:, sl]
            r = lax.rsqrt(jnp.mean(vg * vg, axis=-1, keepdims=True) + NORM_EPS)
            vh = vg * r
            gg = dg * w_ref[:, sl]
            dvg = r * (gg - vh * jnp.mean(gg * vh, axis=-1, keepdims=True))
            dy_ref[:, sl] = dvg * sz[:, sl]
            dz_ref[:, sl] = (dvg * yv[:, sl] * _dsilu(zv[:, sl])).astype(BF16)
            dw_ref[:, sl] += jnp.sum(dg * vh, axis=0, keepdims=True)

    blk = pl.BlockSpec((tr, D_SSM), lambda i: (i, 0))
    row = pl.BlockSpec((1, D_SSM), lambda i: (0, 0))
    return pl.pallas_call(
        body, name=name, grid=(t // tr,), in_specs=[blk, blk, row, blk], out_specs=[blk, blk, row],
        out_shape=[jax.ShapeDtypeStruct((t, D_SSM), F32), jax.ShapeDtypeStruct((t, D_SSM), BF16),
                   jax.ShapeDtypeStruct((1, D_SSM), F32)],
        compiler_params=_cp("arbitrary"),
    )(y, proj, w, dcat)


def _pad_lanes(v):
    return jnp.pad(v, ((0, 0), (0, LANES - v.shape[1])))


def _per_head(v128, t):
    return jnp.broadcast_to(v128[:, :SSD_HEADS].T[:, :, None], (SSD_HEADS, t, SSD_P))


def _ssd_forward(proj, conv_w, conv_b, dt_bias, a_log, d_skip, ssd_norm_w):
    t = proj.shape[0]
    nc = t // CHUNK
    xbc = _conv_act_fwd(proj, conv_w, conv_b, kw=SSD_CONV, glu=False, tc=512, coff=OFF_XBC // 512,
                        ncols=SSD_CONV_DIM, out_dtype=F32, name="ssd_conv_fwd")
    bias128, alog128 = _pad_lanes(dt_bias), _pad_lanes(a_log)
    dt128, cs128, a128 = _ssd_prep(proj, bias128, alog128, name="ssd_prep")
    dt_h, cs_h = _per_head(dt128, t), _per_head(cs128, t)
    cs_row = cs128[:, :SSD_HEADS].T.reshape(SSD_HEADS, nc, 1, CHUNK)
    dskip_h = jnp.broadcast_to(d_skip[0][:, None, None], (SSD_HEADS, 1, SSD_P))
    a_h = jnp.broadcast_to(a128[0, :SSD_HEADS][:, None, None], (SSD_HEADS, 1, SSD_P))
    y, states = _ssd_fwd(xbc, dt_h, cs_h, cs_row, dskip_h, name="ssd_scan_fwd")
    y_ssd = _ssd_gate_fwd(y, proj, ssd_norm_w, name="ssd_gate_fwd")
    saved = (proj, conv_w, conv_b, ssd_norm_w, bias128, dt128, a128, dt_h, cs_h, cs_row, xbc, dskip_h, a_h, states, y)
    return y_ssd, saved


def _ssd_backward(saved, dcat):
    proj, conv_w, conv_b, ssd_norm_w, bias128, dt128, a128, dt_h, cs_h, cs_row, xbc, dskip_h, a_h, states, y = saved
    dy, dz, d_norm_w = _ssd_gate_bwd(y, proj, ssd_norm_w, dcat, name="ssd_gate_bwd")
    dxc, ddt128, dadt128, dd_h = _ssd_bwd(xbc, dt_h, cs_h, cs_row, dskip_h, a_h, states, dy, name="ssd_scan_bwd")
    dxbc, d_conv_w, d_conv_b = _conv_act_bwd(proj, conv_w, conv_b, dxc, kw=SSD_CONV, glu=False, tc=512,
                                             coff=OFF_XBC // 512, ncols=SSD_CONV_DIM, name="ssd_conv_bwd")
    d_raw, d_bias, d_alog, d_dskip = _ssd_prep_bwd(ddt128, dadt128, proj, bias128, dt128, a128,
                                                   dd_h.reshape(SSD_HEADS, SSD_P), name="ssd_prep_bwd")
    return (dz, dxbc, d_raw, d_norm_w, d_conv_w, d_conv_b, d_bias[:, :SSD_HEADS], d_alog[:, :SSD_HEADS],
            d_dskip.reshape(1, SSD_HEADS))


def _rope_tables(positions):
    inv_freq = ROPE_THETA ** (-jnp.arange(0, MLA_ROPE, 2, dtype=F32) / MLA_ROPE)
    ang = positions[0].astype(F32)[:, None] * inv_freq
    cos, sin = jnp.cos(ang), jnp.sin(ang)
    z = jnp.zeros_like(cos)
    return jnp.stack([jnp.concatenate([cos, cos, z, z], axis=1), jnp.concatenate([-sin, z, z, z], axis=1),
                      jnp.concatenate([z, sin, z, z], axis=1)])


def _mla_forward(proj, tabs, q_a_norm_w, wq_pad, kv_a_norm_w, wkv):
    qn = _rmsnorm_fwd(proj, q_a_norm_w, width=MLA_Q_RANK, cblk=OFF_QA // MLA_Q_RANK, name="q_a_norm")
    q = _matmul(qn, wq_pad, name="q_b_proj")
    kvn = _rmsnorm_fwd(proj, kv_a_norm_w, width=MLA_KV_RANK, cblk=OFF_CKV // MLA_KV_RANK, name="kv_a_norm")
    kv = _matmul(kvn, wkv, name="kv_b_proj")
    q3, k3, v3 = _mla_prep(q, kv, proj, tabs, name="mla_prep")
    o, lse = _attn_fwd(q3, k3, v3, name="attn_fwd")
    return o, (proj, tabs, q_a_norm_w, wq_pad, kv_a_norm_w, wkv, qn, kvn, q3, k3, v3, o, lse)


def _mla_backward(saved, dcat):
    proj, tabs, q_a_norm_w, wq_pad, kv_a_norm_w, wkv, qn, kvn, q3, k3, v3, o, lse = saved
    dq3, dk3, dv3 = _attn_bwd(q3, k3, v3, o, dcat, lse, name="attn_bwd")
    dq, dkv, dkr = _mla_unprep(dq3, dk3, dv3, tabs, name="mla_unprep")
    d_wq = _matmul(qn, dq, ta=True, out_dtype=BF16, name="d_w_q_b")
    dqn = _matmul(dq, wq_pad, tb=True, name="d_qn")
    dq_a, d_qnw = _rmsnorm_bwd(proj, q_a_norm_w, dqn, width=MLA_Q_RANK, cblk=OFF_QA // MLA_Q_RANK, out_dtype=BF16,
                               name="q_a_norm_bwd")
    d_wkv = _matmul(kvn, dkv, ta=True, out_dtype=BF16, name="d_w_kv_b")
    dkvn = _matmul(dkv, wkv, tb=True, name="d_kvn")
    dckv, d_kvnw = _rmsnorm_bwd(proj, kv_a_norm_w, dkvn, width=MLA_KV_RANK, cblk=OFF_CKV // MLA_KV_RANK,
                                out_dtype=BF16, name="kv_a_norm_bwd")
    return dq_a, dckv, dkr, d_wq, d_wkv, d_qnw, d_kvnw


def _pad_w_q(w):
    r = w.shape[0]
    w3 = w.reshape(r, MLA_HEADS, MLA_NOPE + MLA_ROPE)
    return jnp.pad(w3, ((0, 0), (0, 0), (0, MLA_QK_PAD - MLA_NOPE - MLA_ROPE))).reshape(r, MLA_HEADS * MLA_QK_PAD)


def _unpad_w_q(w):
    r = w.shape[0]
    return w.reshape(r, MLA_HEADS, MLA_QK_PAD)[:, :, :MLA_NOPE + MLA_ROPE].reshape(r, MLA_HEADS * (MLA_NOPE + MLA_ROPE))


def _pad_w_in(w):
    r = w.shape[0]
    o_dt = D_SSM + SSD_CONV_DIM
    o_qa = o_dt + SSD_HEADS
    o_kr = o_qa + MLA_Q_RANK + MLA_KV_RANK
    zeros = lambda n: jnp.zeros((r, n), w.dtype)
    return jnp.concatenate([w[:, :o_dt], w[:, o_qa:o_kr], w[:, o_kr:], zeros(LANES - MLA_ROPE),
                            w[:, o_dt:o_qa], zeros(LANES - SSD_HEADS)], axis=1)


def _unpad_w_in(w):
    return jnp.concatenate([w[:, :OFF_QA], w[:, OFF_DT:OFF_DT + SSD_HEADS], w[:, OFF_QA:OFF_KR + MLA_ROPE]], axis=1)


W_IN_SEGMENTS = ((0, D_SSM + SSD_CONV_DIM, 0), (D_SSM + SSD_CONV_DIM, D_SSM + SSD_CONV_DIM + SSD_HEADS, OFF_DT),
                 (D_SSM + SSD_CONV_DIM + SSD_HEADS, D_IN - MLA_ROPE, OFF_QA), (D_IN - MLA_ROPE, D_IN, OFF_KR))


def _pad_w_in_shards(g):
    n = g.shape[2]
    pieces, at = [], 0
    for lo, hi, start in sorted(W_IN_SEGMENTS, key=lambda seg: seg[2]):
        if start > at:
            pieces.append(jnp.zeros((g.shape[1], start - at), g.dtype))
        for j in range(N_DEV):
            a, b = max(lo, j * n), min(hi, (j + 1) * n)
            if a < b:
                pieces.append(g[j][:, a - j * n:b - j * n])
        at = start + hi - lo
    pieces.append(jnp.zeros((g.shape[1], D_IN_PAD - at), g.dtype))
    return jnp.concatenate(pieces, axis=1)


def _unpad_w_in_shards(w):
    n = D_IN // N_DEV
    shards = []
    for j in range(N_DEV):
        pieces = []
        for lo, hi, start in W_IN_SEGMENTS:
            a, b = max(lo, j * n), min(hi, (j + 1) * n)
            if a < b:
                pieces.append(w[:, start + a - lo:start + b - lo])
        shards.append(jnp.concatenate(pieces, axis=1) if len(pieces) > 1 else pieces[0])
    return jnp.stack(shards)


WEIGHTS = ['mix_norm_w', 'w_in', 'conv_w', 'conv_b', 'dt_bias', 'a_log', 'd_skip', 'ssd_norm_w', 'q_a_norm_w', 'w_q_b',
           'kv_a_norm_w', 'w_kv_b', 'w_out', 'ffn_norm_w', 'w_ffn_up', 'ffn_conv_w', 'ffn_conv_b', 'w_ffn_down',
           'ple_norm_w', 'w_ple_gate', 'b_ple_gate', 'w_ple_proj', 'ple_post_norm_w', 'final_norm_w']
BIG = ['w_in', 'w_q_b', 'w_kv_b', 'w_out', 'w_ffn_up', 'w_ffn_down', 'w_ple_gate', 'w_ple_proj']
COL_SHARDED = ('w_in', 'w_q_b', 'w_kv_b', 'w_ffn_up', 'w_ple_proj')
CONV = ['conv_w', 'ffn_conv_w']
REPL = [n for n in WEIGHTS if n not in BIG and n not in CONV]
FFN_INV = tuple(int(i) for i in np.argsort(FFN_PERM))


def _cat_cols(g):
    return jnp.concatenate([g[j] for j in range(N_DEV)], axis=1)


def _split_cols(w):
    n = w.shape[1] // N_DEV
    return jnp.stack([w[:, j * n:(j + 1) * n] for j in range(N_DEV)])


def _interleave(v):
    r = v.shape[0]
    return v.reshape(r, N_DEV, FFN_TC)[:, jnp.array(FFN_PERM)].reshape(r, N_DEV * FFN_TC)


def _deinterleave(v):
    r = v.shape[0]
    return v.reshape(r, N_DEV, FFN_TC)[:, jnp.array(FFN_INV)].reshape(r, N_DEV * FFN_TC)


def _assemble_weights(g):
    layout = {
        'w_in': _pad_w_in_shards,
        'w_q_b': lambda v: _pad_w_q(_cat_cols(v)),
        'w_kv_b': _cat_cols,
        'w_out': lambda v: v.reshape(D_MODEL, D_MODEL),
        'w_ffn_up': lambda v: v,
        'w_ffn_down': lambda v: v.reshape(D_FF, D_MODEL),
        'w_ple_gate': lambda v: v.reshape(D_MODEL, D_MODEL),
        'w_ple_proj': _cat_cols,
        'conv_w': _cat_cols,
        'ffn_conv_w': lambda v: _interleave(_cat_cols(v)),
    }
    return {n: layout[n](v) for n, v in g.items()}


WEIGHT_GROUPS = {'a': ['w_in', 'w_q_b', 'w_kv_b', 'conv_w'], 'b': ['w_out'],
                 'c': ['w_ffn_up', 'ffn_conv_w', 'w_ffn_down', 'w_ple_gate', 'w_ple_proj']}
GRAD_GROUPS = {'p': ['w_ple_proj', 'w_ple_gate', 'w_ffn_down'], 'r': ['w_ffn_up'], 's': ['w_out'],
               't': ['w_q_b', 'w_kv_b', 'w_in']}


def _ffn_perm(j):
    return (j % 2) * (N_DEV // 2) + j // 2


def _local_step(x, p, tabs, get_w, s, target, emit, relay, settle):
    t = x.shape[0]
    s = dict(s)
    half = D_MODEL // 2
    up_cols = 2 * D_FF
    ffn_conv_b = _interleave(s['ffn_conv_b'])
    w = dict(get_w('a', None))
    h = _rmsnorm_fwd(x, s['mix_norm_w'], width=D_MODEL, name="mix_norm")
    proj = _matmul(h, w['w_in'], name="in_proj")
    y_ssd, ssd_saved = _ssd_forward(proj, w['conv_w'], s['conv_b'], s['dt_bias'], s['a_log'], s['d_skip'],
                                    s['ssd_norm_w'])
    o, mla_saved = _mla_forward(proj, tabs, s['q_a_norm_w'], w['w_q_b'], s['kv_a_norm_w'], w['w_kv_b'])
    tk_o, tn_o = _tile(half, MM_TK), _tile(D_MODEL, MM_TILE)
    w.update(get_w('b', o))
    x1 = _matmul(y_ssd, w['w_out'], add=x, mnk=(t, D_MODEL, half), name="out_proj_ssd")
    x1 = _matmul(o, w['w_out'], add=x1, mnk=(t, D_MODEL, half), name="out_proj_mla",
                 b_spec=pl.BlockSpec((tk_o, tn_o), lambda i, j, kk: (kk + half // tk_o, j)))
    hf = _rmsnorm_fwd(x1, s['ffn_norm_w'], width=D_MODEL, name="ffn_norm")
    w.update(get_w('c', hf))
    tk_u = _tile(D_MODEL, MM_TK)
    u = _matmul(hf, w['w_ffn_up'], mnk=(t, up_cols, D_MODEL), tn=FFN_TC, name="ffn_up",
                b_spec=pl.BlockSpec((1, tk_u, FFN_TC), lambda i, j, kk: (_ffn_perm(j), kk, 0)))
    act = _conv_act_fwd(u, w['ffn_conv_w'], ffn_conv_b, kw=FFN_CONV, glu=True, tc=2 * FFN_TC, coff=0, ncols=up_cols,
                        out_dtype=BF16, name="ffn_act")
    x2 = _matmul(act, w['w_ffn_down'], add=x1, name="ffn_down")
    hp = _rmsnorm_fwd(x2, s['ple_norm_w'], width=D_MODEL, name="ple_norm")
    gl = _matmul(hp, w['w_ple_gate'], bias=s['b_ple_gate'], name="ple_gate")
    pe = _matmul(p, w['w_ple_proj'], name="ple_proj")
    x3 = _ple_fwd(x2, gl, pe, s['ple_post_norm_w'], name="ple_mix")
    loss, dx3, d_final = _loss_head(x3, s['final_norm_w'], target, name="loss_head")
    dgl, d_bgate, dpe, d_post = _ple_bwd(dx3, gl, pe, s['ple_post_norm_w'], name="ple_mix_bwd")
    d_wproj = _matmul(p, dpe, ta=True, out_dtype=BF16, name="d_w_ple_proj")
    d_wgate = _matmul(hp, dgl, ta=True, out_dtype=BF16, name="d_w_ple_gate")
    dhp = _matmul(dgl, w['w_ple_gate'], tb=True, name="d_ple_normed")
    dx2, d_plenorm = _rmsnorm_bwd(x2, s['ple_norm_w'], dhp, dx3, width=D_MODEL, name="ple_norm_bwd")
    dact = _matmul(dx2, w['w_ffn_down'], tb=True, name="d_ffn_act")
    d_wdown = _matmul(act, dx2, ta=True, out_dtype=BF16, name="d_w_ffn_down")
    zz = emit('p', {'w_ple_proj': _split_cols(d_wproj), 'w_ple_gate': d_wgate.reshape(N_DEV, D_MODEL // N_DEV, D_MODEL),
                    'w_ffn_down': d_wdown.reshape(N_DEV, D_FF // N_DEV, D_MODEL)})
    du, d_fconv_w, d_fconv_b = _conv_act_bwd(u, w['ffn_conv_w'], ffn_conv_b + zz, dact, kw=FFN_CONV, glu=True,
                                             tc=2 * FFN_TC, coff=0, ncols=up_cols, name="ffn_act_bwd")
    zz = zz + relay('p', du)
    tm_u = _tile(D_MODEL, MM_TILE)
    d_wup = _matmul(hf, du, ta=True, out_dtype=BF16, mnk=(D_MODEL, up_cols, t), tn=FFN_TC, name="d_w_ffn_up",
                    o_spec=pl.BlockSpec((1, tm_u, FFN_TC), lambda i, j, kk: (_ffn_perm(j), i, 0)),
                    o_shape=(N_DEV, D_MODEL, FFN_TC))
    zz = zz + emit('r', {'w_ffn_up': d_wup})
    dhf = _matmul(du, w['w_ffn_up'], tb=True, mnk=(t, D_MODEL, up_cols), tk=FFN_TC, name="d_ffn_normed",
                  b_spec=pl.BlockSpec((1, tn_o, FFN_TC), lambda i, j, kk: (_ffn_perm(kk), j, 0)))
    zz = zz + relay('r', dhf) + settle('p')
    dx1, d_ffnnorm = _rmsnorm_bwd(x1, s['ffn_norm_w'] + zz, dhf, dx2, width=D_MODEL, name="ffn_norm_bwd")
    dcat = _matmul(dx1, w['w_out'], tb=True, name="d_mixed")
    d_wout = jnp.concatenate([_matmul(y_ssd, dx1, ta=True, out_dtype=BF16, name="d_w_out_ssd"),
                              _matmul(o, dx1, ta=True, out_dtype=BF16, name="d_w_out_mla")], axis=0)
    zz = zz + emit('s', {'w_out': d_wout.reshape(N_DEV, D_MODEL // N_DEV, D_MODEL)})
    ssd_saved = ssd_saved[:3] + (ssd_saved[3] + zz,) + ssd_saved[4:]
    dz, dxbc, d_raw, d_ssdnorm, d_conv_w, d_conv_b, d_dtb, d_alog, d_dskip = _ssd_backward(ssd_saved, dcat)
    zz = zz + relay('s', dz)
    mla_saved = mla_saved[:-1] + (mla_saved[-1] + zz,)
    dq_a, dckv, dkr, d_wq, d_wkv, d_qnorm, d_kvnorm = _mla_backward(mla_saved, dcat)
    d_raw = (d_raw + settle('r')).astype(BF16)
    dproj = jnp.concatenate([dz, dxbc, dq_a, dckv, dkr, d_raw], axis=1)
    d_win = _matmul(h, dproj, ta=True, out_dtype=BF16, name="d_w_in")
    dh = _matmul(dproj, w['w_in'], tb=True, name="d_in_normed")
    dx, d_mixnorm = _rmsnorm_bwd(x, s['mix_norm_w'] + settle('s'), dh, dx1, width=D_MODEL, name="mix_norm_bwd")
    emit('t', {'w_in': _unpad_w_in_shards(d_win), 'w_q_b': _split_cols(_unpad_w_q(d_wq)),
               'w_kv_b': _split_cols(d_wkv)})
    relay('t', dx)
    settle('t')
    conv = {'conv_w': d_conv_w, 'ffn_conv_w': _deinterleave(d_fconv_w)}
    vec = {
        'mix_norm_w': d_mixnorm, 'conv_b': d_conv_b, 'dt_bias': d_dtb, 'a_log': d_alog, 'd_skip': d_dskip,
        'ssd_norm_w': d_ssdnorm, 'q_a_norm_w': d_qnorm, 'kv_a_norm_w': d_kvnorm, 'ffn_norm_w': d_ffnnorm,
        'ffn_conv_b': _deinterleave(d_fconv_b), 'ple_norm_w': d_plenorm, 'b_ple_gate': d_bgate,
        'ple_post_norm_w': d_post, 'final_norm_w': d_final,
    }
    return loss, dx, conv, vec


MESH = pl.DeviceIdType.MESH
FLIPS = ((0, 0, 1), (1, 0, 0), (0, 1, 0), (1, 1, 0), (1, 0, 1), (0, 1, 1), (1, 1, 1))


def _exchange(items, *, gather, name):
    n = len(items)

    def body(*refs):
        ins, outs = refs[:n], refs[n:2 * n]
        send_sems, recv_sems, local_sems = refs[2 * n:]
        x, y, c = lax.axis_index("x"), lax.axis_index("y"), lax.axis_index("c")
        me = 4 * x + 2 * y + c
        peers = [(jnp.where(fx, 1 - x, x), jnp.where(fy, 1 - y, y), jnp.where(fc, 1 - c, c)) for fx, fy, fc in FLIPS]
        slot = [4 * px + 2 * py + pc for px, py, pc in peers]
        local, sends = [], []
        for wi in range(n):
            cp = pltpu.make_async_copy(ins[wi] if gather else ins[wi].at[me], outs[wi].at[me], local_sems.at[wi])
            cp.start()
            local.append(cp)
            for k, peer in enumerate(peers):
                cp = pltpu.make_async_remote_copy(
                    src_ref=ins[wi] if gather else ins[wi].at[slot[k]], dst_ref=outs[wi].at[me],
                    send_sem=send_sems.at[k, wi], recv_sem=recv_sems.at[k, wi], device_id=peer, device_id_type=MESH)
                cp.start()
                sends.append(cp)
        for wi in range(n):
            for k, peer in enumerate(peers):
                pltpu.make_async_remote_copy(
                    src_ref=outs[wi].at[slot[k]], dst_ref=outs[wi].at[slot[k]], send_sem=send_sems.at[k, wi],
                    recv_sem=recv_sems.at[k, wi], device_id=peer, device_id_type=MESH).wait_recv()
        for cp in sends:
            cp.wait_send()
        for cp in local:
            cp.wait()

    hbm = pl.BlockSpec(memory_space=pltpu.HBM)
    out_shape = [jax.ShapeDtypeStruct(((N_DEV,) + v.shape) if gather else v.shape, v.dtype) for v in items]
    return pl.pallas_call(
        body, name=name, in_specs=[hbm] * n, out_specs=[hbm] * n, out_shape=out_shape,
        scratch_shapes=[pltpu.SemaphoreType.DMA((len(FLIPS), n)), pltpu.SemaphoreType.DMA((len(FLIPS), n)),
                        pltpu.SemaphoreType.DMA((n,))],
    )(*items)


HBM_SPEC = pl.BlockSpec(memory_space=pltpu.HBM)
SEM_SPEC = pl.BlockSpec(memory_space=pltpu.SEMAPHORE)
EFFECT = pltpu.SideEffectType.DATAFLOW_SIDE_EFFECTING


def _peers():
    x, y, c = lax.axis_index("x"), lax.axis_index("y"), lax.axis_index("c")
    peers = [(jnp.where(fx, 1 - x, x), jnp.where(fy, 1 - y, y), jnp.where(fc, 1 - c, c)) for fx, fy, fc in FLIPS]
    return 4 * x + 2 * y + c, peers, [4 * px + 2 * py + pc for px, py, pc in peers]


def _split_start(bufs, ncopies, plan, *, name):
    nb = len(bufs)

    def body(*refs):
        send_sems, recv_sems, token = refs[nb], refs[nb + 1], refs[2 * nb + 2]
        for i, (src, dst, peer, _) in enumerate(plan(refs[:nb])):
            pltpu.make_async_remote_copy(src_ref=src, dst_ref=dst, send_sem=send_sems.at[i], recv_sem=recv_sems.at[i],
                                         device_id=peer, device_id_type=MESH).start()
        token[...] = jnp.zeros_like(token)

    res = pl.pallas_call(
        body, name=name, in_specs=[HBM_SPEC] * nb,
        out_specs=[SEM_SPEC, SEM_SPEC] + [HBM_SPEC] * nb + [pl.BlockSpec(memory_space=pltpu.VMEM)],
        out_shape=[pltpu.SemaphoreType.DMA((ncopies,)), pltpu.SemaphoreType.DMA((ncopies,))]
        + [pltpu.HBM(v.shape, v.dtype) for v in bufs] + [jax.ShapeDtypeStruct((HALO, LANES), F32)],
        input_output_aliases={i: 2 + i for i in range(nb)},
        compiler_params=pltpu.CompilerParams(has_side_effects=EFFECT),
    )(*[pltpu.with_memory_space_constraint(v, pltpu.HBM) for v in bufs])
    return (res[0], res[1], list(res[2:2 + nb])), res[2 + nb]


def _split_wait(started, after, plan, local_plan, *, name):
    send_sems, recv_sems, bufs = started
    nb = len(bufs)
    nlocal = len(local_plan(bufs))

    def body(*refs):
        send_sems, recv_sems = refs[nb], refs[nb + 1]
        local_sems = refs[2 * nb + 3]
        local = []
        for j, (src, dst) in enumerate(local_plan(refs[:nb])):
            cp = pltpu.make_async_copy(src, dst, local_sems.at[j])
            cp.start()
            local.append(cp)
        for i, (src, _, peer, incoming) in enumerate(plan(refs[:nb])):
            cp = pltpu.make_async_remote_copy(src_ref=src, dst_ref=incoming, send_sem=send_sems.at[i],
                                              recv_sem=recv_sems.at[i], device_id=peer, device_id_type=MESH)
            cp.wait_send()
            cp.wait_recv()
        for cp in local:
            cp.wait()

    res = pl.pallas_call(
        body, name=name, in_specs=[HBM_SPEC] * nb + [SEM_SPEC, SEM_SPEC, pl.BlockSpec(memory_space=pl.ANY)],
        out_specs=[HBM_SPEC] * nb, out_shape=[pltpu.HBM(v.shape, v.dtype) for v in bufs],
        input_output_aliases={i: i for i in range(nb)},
        scratch_shapes=[pltpu.SemaphoreType.DMA((max(nlocal, 1),))],
        compiler_params=pltpu.CompilerParams(has_side_effects=EFFECT),
    )(*bufs, send_sems, recv_sems, after)
    return list(res)


def _place():
    x, y, c = lax.axis_index("x"), lax.axis_index("y"), lax.axis_index("c")
    others = [((1 - x, y, c), 2 * (1 - x) + y), ((x, 1 - y, c), 2 * x + 1 - y), ((1 - x, 1 - y, c), 2 * (1 - x) + 1 - y)]
    return 4 * x + 2 * y + c, 2 * x + y, c, (x, y, 1 - c), others


def _gather1_plan(n):
    def plan(refs):
        me, _, _, sibling, others = _place()
        out = []
        for wi in range(n):
            item, land = refs[wi], refs[n + wi]
            out.append((item, land.at[me], sibling, land.at[me + 1 - 2 * lax.axis_index("c")]))
            for peer, chip in others:
                out.append((item, land.at[me], peer, land.at[2 * chip + lax.axis_index("c")]))
        return out

    return plan


def _gather1_local(n):
    def plan(refs):
        me = _place()[0]
        return [(refs[wi], refs[n + wi].at[me]) for wi in range(n)]

    return plan


def _gather2_plan(n):
    def plan(refs):
        _, _, c, sibling, others = _place()
        out = []
        for wi in range(n):
            land = refs[wi]
            for _, chip in others:
                out.append((land.at[2 * chip + c], land.at[2 * chip + c], sibling, land.at[2 * chip + 1 - c]))
        return out

    return plan


def _gather_start(items, *, name):
    lands = [lax.empty((N_DEV,) + v.shape, v.dtype) for v in items]
    return _split_start(items + lands, 4 * len(items), _gather1_plan(len(items)), name=name)


def _gather_forward(started, after, *, name):
    n = len(started[2]) // 2
    bufs = _split_wait(started, after, _gather1_plan(n), _gather1_local(n), name=name + "_wait")
    return _split_start(bufs[n:], 3 * n, _gather2_plan(n), name=name + "_start")


def _gather_finish(started, after, *, name):
    n = len(started[2])
    return _split_wait(started, after, _gather2_plan(n), lambda refs: [], name=name)


def _handshake(peers):
    barrier = pltpu.get_barrier_semaphore()
    for peer in peers:
        pl.semaphore_signal(barrier, inc=1, device_id=peer, device_id_type=MESH)
    pl.semaphore_wait(barrier, len(peers))


def _remote(src, dst, send_sem, recv_sem, peer):
    return pltpu.make_async_remote_copy(src_ref=src, dst_ref=dst, send_sem=send_sem, recv_sem=recv_sem, device_id=peer,
                                        device_id_type=MESH)


def _sequencer_gather(items, *, collective_id, name):
    n = len(items)
    srcs = [jax.new_ref(v, memory_space=pltpu.MemorySpace.HBM) for v in items]
    lands = [jax.empty_ref(jax.ShapeDtypeStruct((N_DEV,) + v.shape, v.dtype), memory_space=pltpu.MemorySpace.HBM)
             for v in items]
    dma = pltpu.SemaphoreType.DMA

    @pl.kernel(mesh=plsc.ScalarSubcoreMesh(axis_name="sequencer", num_cores=1), name=name,
               scratch_types=(dma((4 * n,)), dma((4 * n,)), dma((3 * n,)), dma((3 * n,)), dma((n,))),
               compiler_params=pltpu.CompilerParams(collective_id=collective_id))
    def launch(send1, recv1, send2, recv2, local_sems):
        _, _, _, sibling, others = _place()
        _handshake([sibling] + [peer for peer, _ in others])
        hop1 = _gather1_plan(n)(srcs + lands)
        hop2 = _gather2_plan(n)(lands)
        local = [pltpu.make_async_copy(src, dst, local_sems.at[j])
                 for j, (src, dst) in enumerate(_gather1_local(n)(srcs + lands))]
        for cp in local:
            cp.start()
        for i, (src, dst, peer, _) in enumerate(hop1):
            _remote(src, dst, send1.at[i], recv1.at[i], peer).start()
        for wi in range(n):
            for j in range(3):
                i1, i2 = 4 * wi + 1 + j, 3 * wi + j
                src, _, peer, incoming = hop1[i1]
                _remote(src, incoming, send1.at[i1], recv1.at[i1], peer).wait_recv()
                src, dst, peer, _ = hop2[i2]
                _remote(src, dst, send2.at[i2], recv2.at[i2], peer).start()
        for wi in range(n):
            src, _, peer, incoming = hop1[4 * wi]
            _remote(src, incoming, send1.at[4 * wi], recv1.at[4 * wi], peer).wait_recv()
        for i, (src, _, peer, incoming) in enumerate(hop2):
            cp = _remote(src, incoming, send2.at[i], recv2.at[i], peer)
            cp.wait_send()
            cp.wait_recv()
        for i, (src, dst, peer, _) in enumerate(hop1):
            _remote(src, dst, send1.at[i], recv1.at[i], peer).wait_send()
        for cp in local:
            cp.wait()

    launch()
    return [land[...] for land in lands]


def _sequencer_exchange(sources, land_shapes, ncopies, plan, local_plan, peers, *, collective_id, name):
    srcs = [jax.new_ref(v, memory_space=pltpu.MemorySpace.HBM) for v in sources]
    lands = [jax.empty_ref(s, memory_space=pltpu.MemorySpace.HBM) for s in land_shapes]
    nlocal = len(local_plan(srcs + lands))
    dma = pltpu.SemaphoreType.DMA

    @pl.kernel(mesh=plsc.ScalarSubcoreMesh(axis_name="sequencer", num_cores=1), name=name,
               scratch_types=(dma((ncopies,)), dma((ncopies,)), dma((max(nlocal, 1),))),
               compiler_params=pltpu.CompilerParams(collective_id=collective_id))
    def launch(send_sems, recv_sems, local_sems):
        _handshake(peers(_place()))
        copies = plan(srcs + lands)
        local = [pltpu.make_async_copy(src, dst, local_sems.at[j])
                 for j, (src, dst) in enumerate(local_plan(srcs + lands))]
        for cp in local:
            cp.start()
        for i, (src, dst, peer, _) in enumerate(copies):
            _remote(src, dst, send_sems.at[i], recv_sems.at[i], peer).start()
        for i, (src, _, peer, incoming) in enumerate(copies):
            cp = _remote(src, incoming, send_sems.at[i], recv_sems.at[i], peer)
            cp.wait_send()
            cp.wait_recv()
        for cp in local:
            cp.wait()

    launch()
    return [land[...] for land in lands]


def _sequencer_scatter_hop1(parts, *, collective_id, name):
    n = len(parts)
    shapes = [jax.ShapeDtypeStruct((N_CHIP,) + v.shape[1:], v.dtype) for v in parts]
    return _sequencer_exchange(parts, shapes, N_CHIP * n, _scatter1_plan(n), lambda refs: [], lambda place: [place[3]],
                               collective_id=collective_id, name=name)


def _sequencer_scatter_hop2(sums, *, collective_id, name):
    n = len(sums)
    shapes = [jax.ShapeDtypeStruct(v.shape, v.dtype) for v in sums]
    return _sequencer_exchange(sums, shapes, 3 * n, _scatter2_plan(n), _scatter2_local(n),
                               lambda place: [peer for peer, _ in place[4]], collective_id=collective_id, name=name)


N_CHIP = N_DEV // 2


def _scatter1_plan(n):
    def plan(refs):
        _, _, c, sibling, _ = _place()
        out = []
        for wi in range(n):
            parts, half = refs[wi], refs[n + wi]
            for chip in range(N_CHIP):
                out.append((parts.at[2 * chip + 1 - c], half.at[chip], sibling, half.at[chip]))
        return out

    return plan


def _scatter2_plan(n):
    def plan(refs):
        _, my_chip, _, _, others = _place()
        out = []
        for wi in range(n):
            sums, recv = refs[wi], refs[n + wi]
            for peer, chip in others:
                out.append((sums.at[chip], recv.at[my_chip], peer, recv.at[chip]))
        return out

    return plan


def _scatter2_local(n):
    def plan(refs):
        my_chip = _place()[1]
        return [(refs[wi].at[my_chip], refs[n + wi].at[my_chip]) for wi in range(n)]

    return plan


def _pair_add(parts, half, core, *, name):
    _, r, c = parts.shape
    tr = max(d for d in range(HALO, 257, HALO) if r % d == 0) if r > 256 else r
    parts4 = parts.reshape(N_CHIP, 2, r, c)

    def body(core_ref, p_ref, h_ref, o_ref):
        o_ref[...] = (p_ref[:, 0].astype(F32) + h_ref[...].astype(F32)).astype(o_ref.dtype)

    return pl.pallas_call(
        body, name=name,
        grid_spec=pltpu.PrefetchScalarGridSpec(
            num_scalar_prefetch=1, grid=(r // tr,),
            in_specs=[pl.BlockSpec((N_CHIP, 1, tr, c), lambda i, core_ref: (0, core_ref[0], i, 0)),
                      pl.BlockSpec((N_CHIP, tr, c), lambda i, core_ref: (0, i, 0))],
            out_specs=pl.BlockSpec((N_CHIP, tr, c), lambda i, core_ref: (0, i, 0))),
        out_shape=jax.ShapeDtypeStruct((N_CHIP, r, c), parts.dtype), compiler_params=_cp("parallel"),
    )(core, parts4, half)


def _scatter_start(parts, *, name):
    halves = [lax.empty((N_CHIP,) + v.shape[1:], v.dtype) for v in parts]
    return _split_start(parts + halves, N_CHIP * len(parts), _scatter1_plan(len(parts)), name=name)


def _scatter_forward(started, after, core, *, name):
    n = len(started[2]) // 2
    bufs = _split_wait(started, after, _scatter1_plan(n), lambda refs: [], name=name + "_wait")
    sums = [_pair_add(bufs[wi], bufs[n + wi], core, name=name + "_add%d" % wi) for wi in range(n)]
    recvs = [lax.empty(v.shape, v.dtype) for v in sums]
    return _split_start(sums + recvs, 3 * n, _scatter2_plan(n), name=name + "_start")


def _scatter_finish(started, after, *, name):
    n = len(started[2]) // 2
    return _split_wait(started, after, _scatter2_plan(n), _scatter2_local(n), name=name)[n:]


def _adamw(parts, w, m, v, *, name):
    r, c = w.shape
    nparts = parts.shape[0]
    tr = max(d for d in range(HALO, 129, HALO) if r % d == 0) if r > 128 else r

    def body(p_ref, w_ref, m_ref, v_ref, g_ref, d_ref, mo_ref, vo_ref):
        g = p_ref[0].astype(F32)
        for k in range(1, nparts):
            g = g + p_ref[k].astype(F32)
        mn = ADAM_B1 * m_ref[...] + (1.0 - ADAM_B1) * g
        vn = ADAM_B2 * v_ref[...] + (1.0 - ADAM_B2) * (g * g)
        m_hat = mn / (1.0 - ADAM_B1 ** ADAM_STEP)
        v_hat = vn / (1.0 - ADAM_B2 ** ADAM_STEP)
        g_ref[...] = g
        d_ref[...] = -ADAM_LR * (m_hat / (jnp.sqrt(v_hat) + ADAM_EPS) + ADAM_WD * w_ref[...])
        mo_ref[...] = mn
        vo_ref[...] = vn

    blk = pl.BlockSpec((tr, c), lambda i: (i, 0))
    return pl.pallas_call(
        body, name=name, grid=(r // tr,), in_specs=[pl.BlockSpec((nparts, tr, c), lambda i: (0, i, 0)), blk, blk, blk],
        out_specs=[blk] * 4, out_shape=[jax.ShapeDtypeStruct((r, c), F32)] * 4, compiler_params=_cp("parallel"),
    )(parts, w, m, v)


def _pack_rows(vs, rows):
    lead = vs[0].shape[:-1] if vs[0].ndim > 1 else ()
    flat = jnp.concatenate(vs, axis=-1)
    pad = rows * LANES - flat.shape[-1]
    flat = jnp.pad(flat, [(0, 0)] * len(lead) + [(0, pad)])
    return flat.reshape(lead + (rows, LANES))


def kernel(x, p, positions, mix_norm_w, w_in, conv_w, conv_b, dt_bias, a_log, d_skip, ssd_norm_w, q_a_norm_w, w_q_b, kv_a_norm_w, w_kv_b, w_out, ffn_norm_w, w_ffn_up, ffn_conv_w, ffn_conv_b, w_ffn_down, ple_norm_w, w_ple_gate, b_ple_gate, w_ple_proj, ple_post_norm_w, final_norm_w, loss_target, m_mix_norm_w, m_w_in, m_conv_w, m_conv_b, m_dt_bias, m_a_log, m_d_skip, m_ssd_norm_w, m_q_a_norm_w, m_w_q_b, m_kv_a_norm_w, m_w_kv_b, m_w_out, m_ffn_norm_w, m_w_ffn_up, m_ffn_conv_w, m_ffn_conv_b, m_w_ffn_down, m_ple_norm_w, m_w_ple_gate, m_b_ple_gate, m_w_ple_proj, m_ple_post_norm_w, m_final_norm_w, v_mix_norm_w, v_w_in, v_conv_w, v_conv_b, v_dt_bias, v_a_log, v_d_skip, v_ssd_norm_w, v_q_a_norm_w, v_w_q_b, v_kv_a_norm_w, v_w_kv_b, v_w_out, v_ffn_norm_w, v_w_ffn_up, v_ffn_conv_w, v_ffn_conv_b, v_w_ffn_down, v_ple_norm_w, v_w_ple_gate, v_b_ple_gate, v_w_ple_proj, v_ple_post_norm_w, v_final_norm_w):
    given = dict(locals())
    shapes = {n: given[n].shape for n in WEIGHTS}
    w2 = {n: given[n].reshape(given[n].shape[-2:] if n in BIG or n in CONV else (1, -1)) for n in WEIGHTS}
    m2 = {n: given['m_' + n].reshape(w2[n].shape) for n in WEIGHTS}
    v2 = {n: given['v_' + n].reshape(w2[n].shape) for n in WEIGHTS}
    me = 4 * lax.axis_index("x") + 2 * lax.axis_index("y") + lax.axis_index("c")

    core = lax.axis_index("c").astype(jnp.int32).reshape(1)

    def shards(grp, zero):
        return [(w2[n] + zero).astype(BF16) if n in BIG else w2[n] + zero for n in WEIGHT_GROUPS[grp]]

    first, token = _gather_start(shards('a', 0.0), name="gather_a_hop1")
    first, token = _gather_forward(first, token, name="gather_a_hop2")
    zero = token[0, 0]
    later = _sequencer_gather(shards('b', zero) + shards('c', zero), collective_id=1, name="gather_later")
    later = dict(zip(WEIGHT_GROUPS['b'] + WEIGHT_GROUPS['c'], later))

    def get_w(grp, after):
        if grp == 'a':
            lands = dict(zip(WEIGHT_GROUPS[grp], _gather_finish(first, token, name="gather_a_done")))
        else:
            lands = {n: later[n] for n in WEIGHT_GROUPS[grp]}
        return _assemble_weights(lands)

    scatters = {}

    hop_ids = {grp: 2 + 2 * i for i, grp in enumerate(GRAD_GROUPS)}

    def zero_of(arrays):
        return sum(v[(0,) * v.ndim].astype(F32) * 0.0 for v in arrays)

    def emit(grp, grads):
        scatters[grp], tok = _scatter_start([grads[n] for n in GRAD_GROUPS[grp]], name="scatter_" + grp + "_hop1")
        return tok[0, 0]

    def relay(grp, after):
        n = len(GRAD_GROUPS[grp])
        bufs = _split_wait(scatters[grp], after, _scatter1_plan(n), lambda refs: [], name="scatter_" + grp + "_hop1_wait")
        sums = [_pair_add(bufs[i], bufs[n + i], core, name="scatter_%s_add%d" % (grp, i)) for i in range(n)]
        scatters[grp] = _sequencer_scatter_hop2(sums, collective_id=hop_ids[grp] + 1, name="scatter_" + grp + "_hop2")
        return zero_of(sums)

    out_g, out_d, out_m, out_v = {}, {}, {}, {}

    def settle(grp):
        for n, parts in zip(GRAD_GROUPS[grp], scatters[grp]):
            out_g[n], out_d[n], out_m[n], out_v[n] = _adamw(parts, w2[n], m2[n], v2[n], name="adamw_" + n)
        return zero_of([out_g[n] for n in GRAD_GROUPS[grp]])

    vecs = {n: w2[n] for n in REPL}
    vecs['mix_norm_w'] = vecs['mix_norm_w'] + zero
    loss, dx, g_conv, g_vec = _local_step(x[0], p[0, 0], _rope_tables(positions), get_w, vecs, loss_target[0], emit,
                                          relay, settle)
    n_small = sum(g_vec[n].shape[1] for n in REPL) + sum(g_conv[n].size for n in CONV) + 1
    rows_small = -(-n_small // (LANES * HALO)) * HALO
    small = _pack_rows([g_vec[n] for n in REPL] + [g_conv[n].reshape(1, -1) for n in CONV] + [loss], rows_small)

    small = small + zero_of([out_g[GRAD_GROUPS['t'][-1]]])
    all_small = _exchange([small], gather=True, name="gather_small_grads")[0].reshape(N_DEV, rows_small * LANES)
    pieces, off = [], 0
    for n in REPL:
        k = g_vec[n].shape[1]
        pieces.append(all_small[:, off:off + k])
        off += k
    for n in CONV:
        kw, cols = g_conv[n].shape
        full = all_small[:, off:off + kw * cols].reshape(N_DEV, kw, cols)
        mine = lax.dynamic_slice_in_dim(full, me * (cols // N_DEV), cols // N_DEV, axis=2)
        pieces.append(mine.reshape(N_DEV, kw * (cols // N_DEV)))
        off += kw * cols
    pieces.append(all_small[:, off:off + 1])
    small_names = REPL + CONV
    n_mine = sum(q.shape[1] for q in pieces)
    rows_mine = -(-n_mine // (LANES * HALO)) * HALO
    zero = jnp.zeros((1, 1), F32)
    packed = [_pack_rows([src[n].reshape(1, -1) for n in small_names] + [zero], rows_mine).reshape(rows_mine, LANES)
              for src in (w2, m2, v2)]
    sg, sd, sm, sv = _adamw(_pack_rows(pieces, rows_mine), *packed, name="adamw_small")
    off = 0
    for n in small_names:
        k = w2[n].size
        for dst, src in ((out_g, sg), (out_d, sd), (out_m, sm), (out_v, sv)):
            dst[n] = src.reshape(-1)[off:off + k].reshape(w2[n].shape)
        off += k
    total_loss = sg.reshape(-1)[off]

    outs = [total_loss, dx[None]]
    for res in (out_g, out_d, out_m, out_v):
        outs += [res[n].reshape(shapes[n]) for n in WEIGHTS]
    return tuple(outs)
```

```python
import functools
import math

import numpy as np
import jax
import jax.numpy as jnp
from jax import lax
from jax.experimental import pallas as pl
from jax.experimental.pallas import tpu as pltpu
from jax.experimental.pallas import tpu_sc as plsc

F32 = jnp.float32
BF16 = jnp.bfloat16
HI = lax.Precision.HIGHEST

D_MODEL = 2048
CHUNK = 64
D_SSM = 1024
SSD_P = 64
SSD_HEADS = 16
SSD_GROUPS = 2
SSD_N = 128
SSD_CONV = 4
SSD_CONV_DIM = D_SSM + 2 * SSD_GROUPS * SSD_N
MLA_HEADS = 8
MLA_NOPE = 128
MLA_ROPE = 64
MLA_V = 128
MLA_Q_RANK = 512
MLA_KV_RANK = 256
MLA_QK_PAD = 256
ROPE_THETA = 10000.0
D_FF = 5632
FFN_CONV = 3
PLE_DIM = 256
NORM_EPS = 1e-6
ADAM_LR, ADAM_B1, ADAM_B2, ADAM_EPS, ADAM_WD, ADAM_STEP = 0.001, 0.9, 0.999, 1e-08, 0.01, 10
N_DEV = 8

OFF_Z, OFF_XBC, OFF_QA, OFF_CKV, OFF_KR, OFF_DT, D_IN_PAD = 0, 1024, 2560, 3072, 3328, 3456, 3584
D_IN = 3408
LANES = 128
HALO = 8
VMEM_LIMIT = 56 * 1024 * 1024
FFN_TC = D_FF * 2 // N_DEV
FFN_PERM = (0, 4, 1, 5, 2, 6, 3, 7)
NEG = -1e30


def _cp(*sem):
    return pltpu.CompilerParams(dimension_semantics=tuple(sem), vmem_limit_bytes=VMEM_LIMIT)


def _tile(n, want):
    if n <= want:
        return n
    best = max(d for d in range(LANES, want + 1, LANES) if n % d == 0)
    return best


def _sigmoid(x):
    return 0.5 * (jnp.tanh(0.5 * x) + 1.0)


def _silu(x):
    return x * _sigmoid(x)


def _dsilu(x):
    s = _sigmoid(x)
    return s * (1.0 + x * (1.0 - s))


MM_TILE = 1408
MM_TK = 2816


def _matmul(a, b, *, ta=False, tb=False, out_dtype=F32, add=None, bias=None, tm=MM_TILE, tn=MM_TILE, tk=MM_TK, name,
            mnk=None, a_spec=None, b_spec=None, o_spec=None, o_shape=None):
    if mnk is None:
        m, k = (a.shape[1], a.shape[0]) if ta else a.shape
        n = b.shape[0] if tb else b.shape[1]
        assert k == (b.shape[1] if tb else b.shape[0])
    else:
        m, n, k = mnk
    tm, tn, tk = _tile(m, tm), _tile(n, tn), _tile(k, tk)
    nk = k // tk
    dims = (((0 if ta else 1,), (1 if tb else 0,)), ((), ()))

    def body(*refs):
        a_ref, b_ref = refs[0], refs[1]
        pos = 2
        add_ref = bias_ref = None
        if add is not None:
            add_ref = refs[pos]
            pos += 1
        if bias is not None:
            bias_ref = refs[pos]
            pos += 1
        o_ref = refs[pos]
        kk = pl.program_id(2)
        av = a_ref[...]
        bv = b_ref[...]
        av = av.reshape(av.shape[-2:]).astype(BF16)
        bv = bv.reshape(bv.shape[-2:]).astype(BF16)
        prod = lax.dot_general(av, bv, dims, preferred_element_type=F32)

        def finish(r):
            if bias_ref is not None:
                r = r + bias_ref[...]
            if add_ref is not None:
                r = r + add_ref[...].astype(F32)
            o_ref[...] = r.astype(out_dtype).reshape(o_ref.shape)

        if nk == 1:
            finish(prod)
        else:
            acc_ref = refs[pos + 1]

            @pl.when(kk == 0)
            def _():
                acc_ref[...] = prod

            @pl.when(kk > 0)
            def _():
                acc_ref[...] += prod

            @pl.when(kk == nk - 1)
            def _():
                finish(acc_ref[...])

    if a_spec is None:
        a_spec = (pl.BlockSpec((tk, tm), lambda i, j, kk: (kk, i)) if ta
                  else pl.BlockSpec((tm, tk), lambda i, j, kk: (i, kk)))
    if b_spec is None:
        b_spec = (pl.BlockSpec((tn, tk), lambda i, j, kk: (j, kk)) if tb
                  else pl.BlockSpec((tk, tn), lambda i, j, kk: (kk, j)))
    if o_spec is None:
        o_spec = pl.BlockSpec((tm, tn), lambda i, j, kk: (i, j))
    if o_shape is None:
        o_shape = (m, n)
    in_specs = [a_spec, b_spec]
    args = [a, b]
    if add is not None:
        in_specs.append(pl.BlockSpec((tm, tn), lambda i, j, kk: (i, j)))
        args.append(add)
    if bias is not None:
        in_specs.append(pl.BlockSpec((1, tn), lambda i, j, kk: (0, j)))
        args.append(bias)
    return pl.pallas_call(
        body, name=name, grid=(m // tm, n // tn, nk), in_specs=in_specs, out_specs=o_spec,
        out_shape=jax.ShapeDtypeStruct(o_shape, out_dtype),
        scratch_shapes=[pltpu.VMEM((tm, tn), F32)] if nk > 1 else [],
        compiler_params=_cp("parallel", "parallel", "arbitrary"),
    )(*args)


def _rmsnorm_fwd(x, w, *, width, cblk=0, out_dtype=BF16, tr=256, name):
    t = x.shape[0]

    def body(x_ref, w_ref, o_ref):
        xv = x_ref[...].astype(F32)
        r = lax.rsqrt(jnp.mean(xv * xv, axis=-1, keepdims=True) + NORM_EPS)
        o_ref[...] = (xv * r * w_ref[...]).astype(out_dtype)

    return pl.pallas_call(
        body, name=name, grid=(t // tr,),
        in_specs=[pl.BlockSpec((tr, width), lambda i: (i, cblk)), pl.BlockSpec((1, width), lambda i: (0, 0))],
        out_specs=pl.BlockSpec((tr, width), lambda i: (i, 0)),
        out_shape=jax.ShapeDtypeStruct((t, width), out_dtype),
        compiler_params=_cp("parallel"),
    )(x, w)


def _rmsnorm_bwd(x, w, dy, add=None, *, width, cblk=0, out_dtype=F32, tr=256, name):
    t = x.shape[0]

    def body(*refs):
        if add is None:
            x_ref, w_ref, dy_ref, dx_ref, dw_ref = refs
            add_ref = None
        else:
            x_ref, w_ref, dy_ref, add_ref, dx_ref, dw_ref = refs
        xv = x_ref[...].astype(F32)
        dyv = dy_ref[...].astype(F32)
        r = lax.rsqrt(jnp.mean(xv * xv, axis=-1, keepdims=True) + NORM_EPS)
        xh = xv * r
        g = dyv * w_ref[...]
        dx = r * (g - xh * jnp.mean(g * xh, axis=-1, keepdims=True))
        if add_ref is not None:
            dx = dx + add_ref[...].astype(F32)
        dx_ref[...] = dx.astype(out_dtype)

        @pl.when(pl.program_id(0) == 0)
        def _():
            dw_ref[...] = jnp.zeros_like(dw_ref)

        dw_ref[...] += jnp.sum(dyv * xh, axis=0, keepdims=True)

    in_specs = [pl.BlockSpec((tr, width), lambda i: (i, cblk)), pl.BlockSpec((1, width), lambda i: (0, 0)),
                pl.BlockSpec((tr, width), lambda i: (i, 0))]
    args = [x, w, dy]
    if add is not None:
        in_specs.append(pl.BlockSpec((tr, width), lambda i: (i, 0)))
        args.append(add)
    return pl.pallas_call(
        body, name=name, grid=(t // tr,), in_specs=in_specs,
        out_specs=[pl.BlockSpec((tr, width), lambda i: (i, 0)), pl.BlockSpec((1, width), lambda i: (0, 0))],
        out_shape=[jax.ShapeDtypeStruct((t, width), out_dtype), jax.ShapeDtypeStruct((1, width), F32)],
        compiler_params=_cp("arbitrary"),
    )(*args)


def _shift_down(prev_halo, cur, j):
    if j == 0:
        return cur
    ext = jnp.concatenate([prev_halo, cur], axis=0)
    return pltpu.roll(ext, j, axis=0)[HALO:]


def _shift_up(cur, next_halo, j):
    if j == 0:
        return cur
    ext = jnp.concatenate([cur, next_halo], axis=0)
    return pltpu.roll(ext, ext.shape[0] - j, axis=0)[:cur.shape[0]]


def _conv_rows(prev, cur, w, b, kw):
    shifted = [cur]
    out = b + w[kw - 1:kw] * cur
    for j in range(1, kw):
        sh = _shift_down(prev, cur, j)
        shifted.append(sh)
        out = out + w[kw - 1 - j:kw - j] * sh
    return out, shifted


def _act_fwd(c, glu):
    if glu:
        half = c.shape[1] // 2
        return _silu(c[:, :half]) * c[:, half:]
    return _silu(c)


def _act_bwd(c, dout, glu):
    if glu:
        half = c.shape[1] // 2
        g, up = c[:, :half], c[:, half:]
        s = _sigmoid(g)
        gs = g * s
        return jnp.concatenate([dout * up * (s + gs * (1.0 - s)), dout * gs], axis=1)
    return dout * _dsilu(c)


def _conv_act_fwd(u, w, b, *, kw, glu, tc, coff, ncols, out_dtype, tr=256, name):
    t = u.shape[0]
    nb = ncols // tc
    oc = tc // 2 if glu else tc

    def body(u_ref, uh_ref, w_ref, b_ref, o_ref):
        prev = jnp.where(pl.program_id(0) == 0, 0.0, uh_ref[...])
        c, _ = _conv_rows(prev, u_ref[...], w_ref[...], b_ref[...], kw)
        o_ref[...] = _act_fwd(c, glu).astype(out_dtype)

    return pl.pallas_call(
        body, name=name, grid=(t // tr, nb),
        in_specs=[pl.BlockSpec((tr, tc), lambda i, j: (i, j + coff)),
                  pl.BlockSpec((HALO, tc), lambda i, j: (jnp.maximum(i * (tr // HALO) - 1, 0), j + coff)),
                  pl.BlockSpec((kw, tc), lambda i, j: (0, j)), pl.BlockSpec((1, tc), lambda i, j: (0, j))],
        out_specs=pl.BlockSpec((tr, oc), lambda i, j: (i, j)),
        out_shape=jax.ShapeDtypeStruct((t, nb * oc), out_dtype),
        compiler_params=_cp("parallel", "parallel"),
    )(u, u, w, b)


def _conv_act_bwd(u, w, b, dout, *, kw, glu, tc, coff, ncols, tr=256, name):
    t = u.shape[0]
    nb = ncols // tc
    nt = t // tr
    oc = tc // 2 if glu else tc

    def body(u_ref, up_ref, un_ref, d_ref, dn_ref, w_ref, b_ref, du_ref, dw_ref, db_ref):
        i = pl.program_id(1)
        cur, nxt, wv, bv = u_ref[...], un_ref[...], w_ref[...], b_ref[...]
        prev = jnp.where(i == 0, 0.0, up_ref[...])
        c_cur, shifted = _conv_rows(prev, cur, wv, bv, kw)
        c_nxt, _ = _conv_rows(cur[tr - HALO:], nxt, wv, bv, kw)
        d_cur = _act_bwd(c_cur, d_ref[...].astype(F32), glu)
        d_nxt = _act_bwd(c_nxt, jnp.where(i == nt - 1, 0.0, dn_ref[...].astype(F32)), glu)
        du = wv[kw - 1:kw] * d_cur
        for j in range(1, kw):
            du = du + wv[kw - 1 - j:kw - j] * _shift_up(d_cur, d_nxt, j)
        du_ref[...] = du.astype(BF16)

        @pl.when(i == 0)
        def _():
            dw_ref[...] = jnp.zeros_like(dw_ref)
            db_ref[...] = jnp.zeros_like(db_ref)

        db_ref[...] += jnp.sum(d_cur, axis=0, keepdims=True)
        dw_ref[...] += jnp.concatenate(
            [jnp.sum(d_cur * shifted[kw - 1 - k], axis=0, keepdims=True) for k in range(kw)], axis=0)

    nh = tr // HALO
    return pl.pallas_call(
        body, name=name, grid=(nb, nt),
        in_specs=[pl.BlockSpec((tr, tc), lambda j, i: (i, j + coff)),
                  pl.BlockSpec((HALO, tc), lambda j, i: (jnp.maximum(i * nh - 1, 0), j + coff)),
                  pl.BlockSpec((HALO, tc), lambda j, i: (jnp.minimum((i + 1) * nh, t // HALO - 1), j + coff)),
                  pl.BlockSpec((tr, oc), lambda j, i: (i, j)),
                  pl.BlockSpec((HALO, oc), lambda j, i: (jnp.minimum((i + 1) * nh, t // HALO - 1), j)),
                  pl.BlockSpec((kw, tc), lambda j, i: (0, j)), pl.BlockSpec((1, tc), lambda j, i: (0, j))],
        out_specs=[pl.BlockSpec((tr, tc), lambda j, i: (i, j)), pl.BlockSpec((kw, tc), lambda j, i: (0, j)),
                   pl.BlockSpec((1, tc), lambda j, i: (0, j))],
        out_shape=[jax.ShapeDtypeStruct((t, ncols), BF16), jax.ShapeDtypeStruct((kw, ncols), F32),
                   jax.ShapeDtypeStruct((1, ncols), F32)],
        compiler_params=_cp("parallel", "arbitrary"),
    )(u, u, u, dout, dout, w, b)


def _ple_fwd(x2, gl, pe, pw, *, tr=256, name):
    t, d = x2.shape

    def body(x_ref, gl_ref, pe_ref, pw_ref, o_ref):
        pv = pe_ref[...]
        r = lax.rsqrt(jnp.mean(pv * pv, axis=-1, keepdims=True) + NORM_EPS)
        o_ref[...] = x_ref[...] + _sigmoid(gl_ref[...]) * (pv * r * pw_ref[...])

    blk = pl.BlockSpec((tr, d), lambda i: (i, 0))
    return pl.pallas_call(
        body, name=name, grid=(t // tr,), in_specs=[blk, blk, blk, pl.BlockSpec((1, d), lambda i: (0, 0))],
        out_specs=blk, out_shape=jax.ShapeDtypeStruct((t, d), F32), compiler_params=_cp("parallel"),
    )(x2, gl, pe, pw)


def _ple_bwd(dx3, gl, pe, pw, *, tr=256, name):
    t, d = dx3.shape

    def body(dx_ref, gl_ref, pe_ref, pw_ref, dgl_ref, db_ref, dpe_ref, dpw_ref):
        dx, pv, pwv = dx_ref[...], pe_ref[...], pw_ref[...]
        gate = _sigmoid(gl_ref[...])
        r = lax.rsqrt(jnp.mean(pv * pv, axis=-1, keepdims=True) + NORM_EPS)
        ph = pv * r
        dgl = dx * (ph * pwv) * gate * (1.0 - gate)
        de = dx * gate
        g = de * pwv
        dgl_ref[...] = dgl.astype(BF16)
        dpe_ref[...] = (r * (g - ph * jnp.mean(g * ph, axis=-1, keepdims=True))).astype(BF16)

        @pl.when(pl.program_id(0) == 0)
        def _():
            db_ref[...] = jnp.zeros_like(db_ref)
            dpw_ref[...] = jnp.zeros_like(dpw_ref)

        db_ref[...] += jnp.sum(dgl, axis=0, keepdims=True)
        dpw_ref[...] += jnp.sum(de * ph, axis=0, keepdims=True)

    blk = pl.BlockSpec((tr, d), lambda i: (i, 0))
    row = pl.BlockSpec((1, d), lambda i: (0, 0))
    return pl.pallas_call(
        body, name=name, grid=(t // tr,), in_specs=[blk, blk, blk, row], out_specs=[blk, row, blk, row],
        out_shape=[jax.ShapeDtypeStruct((t, d), BF16), jax.ShapeDtypeStruct((1, d), F32),
                   jax.ShapeDtypeStruct((t, d), BF16), jax.ShapeDtypeStruct((1, d), F32)],
        compiler_params=_cp("arbitrary"),
    )(dx3, gl, pe, pw)


def _loss_head(x3, fw, target, *, tr=256, name):
    t, d = x3.shape

    def body(x_ref, w_ref, t_ref, l_ref, dx_ref, dw_ref):
        xv, wv = x_ref[...], w_ref[...]
        r = lax.rsqrt(jnp.mean(xv * xv, axis=-1, keepdims=True) + NORM_EPS)
        xh = xv * r
        err = xh * wv - t_ref[...]
        dy = err * (1.0 / d)
        g = dy * wv
        dx_ref[...] = r * (g - xh * jnp.mean(g * xh, axis=-1, keepdims=True))

        @pl.when(pl.program_id(0) == 0)
        def _():
            l_ref[...] = jnp.zeros_like(l_ref)
            dw_ref[...] = jnp.zeros_like(dw_ref)

        l_ref[...] += 0.5 * jnp.sum(jnp.mean(err * err, axis=-1, keepdims=True), axis=0, keepdims=True)
        dw_ref[...] += jnp.sum(dy * xh, axis=0, keepdims=True)

    blk = pl.BlockSpec((tr, d), lambda i: (i, 0))
    row = pl.BlockSpec((1, d), lambda i: (0, 0))
    return pl.pallas_call(
        body, name=name, grid=(t // tr,), in_specs=[blk, row, blk],
        out_specs=[pl.BlockSpec((1, 1), lambda i: (0, 0)), blk, row],
        out_shape=[jax.ShapeDtypeStruct((1, 1), F32), jax.ShapeDtypeStruct((t, d), F32),
                   jax.ShapeDtypeStruct((1, d), F32)],
        compiler_params=_cp("arbitrary"),
    )(x3, fw, target)


def _rope(blk, tab_ref):
    return blk * tab_ref[0] + pltpu.roll(blk, 96, axis=1) * tab_ref[1] + pltpu.roll(blk, 32, axis=1) * tab_ref[2]


def _unrope(g, tab_ref):
    return g * tab_ref[0] + pltpu.roll(g * tab_ref[1], 32, axis=1) + pltpu.roll(g * tab_ref[2], 96, axis=1)


def _mla_prep(q, kv, proj, tabs, *, tr=512, name):
    t = q.shape[0]

    def body(q_ref, kv_ref, kr_ref, tab_ref, qo_ref, ko_ref, vo_ref):
        qv, kvv = q_ref[...], kv_ref[...]
        qo_ref[0, :, :MLA_NOPE] = qv[:, :MLA_NOPE].astype(BF16)
        qo_ref[0, :, MLA_NOPE:] = _rope(qv[:, MLA_NOPE:], tab_ref).astype(BF16)
        ko_ref[0, :, :MLA_NOPE] = kvv[:, :MLA_NOPE].astype(BF16)
        ko_ref[0, :, MLA_NOPE:] = _rope(kr_ref[...], tab_ref).astype(BF16)
        vo_ref[0] = kvv[:, MLA_NOPE:].astype(BF16)

    return pl.pallas_call(
        body, name=name, grid=(t // tr, MLA_HEADS),
        in_specs=[pl.BlockSpec((tr, MLA_QK_PAD), lambda i, h: (i, h)),
                  pl.BlockSpec((tr, MLA_NOPE + MLA_V), lambda i, h: (i, h)),
                  pl.BlockSpec((tr, LANES), lambda i, h: (i, OFF_KR // LANES)),
                  pl.BlockSpec((3, tr, LANES), lambda i, h: (0, i, 0))],
        out_specs=[pl.BlockSpec((1, tr, MLA_QK_PAD), lambda i, h: (h, i, 0)),
                   pl.BlockSpec((1, tr, MLA_QK_PAD), lambda i, h: (h, i, 0)),
                   pl.BlockSpec((1, tr, MLA_V), lambda i, h: (h, i, 0))],
        out_shape=[jax.ShapeDtypeStruct((MLA_HEADS, t, MLA_QK_PAD), BF16),
                   jax.ShapeDtypeStruct((MLA_HEADS, t, MLA_QK_PAD), BF16),
                   jax.ShapeDtypeStruct((MLA_HEADS, t, MLA_V), BF16)],
        compiler_params=_cp("parallel", "parallel"),
    )(q, kv, proj, tabs)


def _mla_unprep(dq3, dk3, dv3, tabs, *, tr=256, name):
    t = dq3.shape[1]

    def body(dq_ref, dk_ref, dv_ref, tab_ref, qo_ref, kvo_ref, kro_ref):
        kr = jnp.zeros((tr, LANES), F32)
        for h in range(MLA_HEADS):
            c0 = h * MLA_QK_PAD
            qo_ref[:, c0:c0 + MLA_NOPE] = dq_ref[h, :, :MLA_NOPE].astype(BF16)
            qo_ref[:, c0 + MLA_NOPE:c0 + MLA_QK_PAD] = _unrope(dq_ref[h, :, MLA_NOPE:], tab_ref).astype(BF16)
            kvo_ref[:, c0:c0 + MLA_NOPE] = dk_ref[h, :, :MLA_NOPE].astype(BF16)
            kvo_ref[:, c0 + MLA_NOPE:c0 + MLA_QK_PAD] = dv_ref[h].astype(BF16)
            kr = kr + dk_ref[h, :, MLA_NOPE:]
        kro_ref[...] = _unrope(kr, tab_ref).astype(BF16)

    return pl.pallas_call(
        body, name=name, grid=(t // tr,),
        in_specs=[pl.BlockSpec((MLA_HEADS, tr, MLA_QK_PAD), lambda i: (0, i, 0)),
                  pl.BlockSpec((MLA_HEADS, tr, MLA_QK_PAD), lambda i: (0, i, 0)),
                  pl.BlockSpec((MLA_HEADS, tr, MLA_V), lambda i: (0, i, 0)),
                  pl.BlockSpec((3, tr, LANES), lambda i: (0, i, 0))],
        out_specs=[pl.BlockSpec((tr, MLA_HEADS * MLA_QK_PAD), lambda i: (i, 0)),
                   pl.BlockSpec((tr, MLA_HEADS * MLA_QK_PAD), lambda i: (i, 0)),
                   pl.BlockSpec((tr, LANES), lambda i: (i, 0))],
        out_shape=[jax.ShapeDtypeStruct((t, MLA_HEADS * MLA_QK_PAD), BF16),
                   jax.ShapeDtypeStruct((t, MLA_HEADS * MLA_QK_PAD), BF16),
                   jax.ShapeDtypeStruct((t, LANES), BF16)],
        compiler_params=_cp("parallel"),
    )(dq3, dk3, dv3, tabs)


ATT_BLK = 256
ATT_SCALE = 1.0 / math.sqrt(MLA_NOPE + MLA_ROPE)
_NT = (((1,), (1,)), ((), ()))
_TN = (((0,), (0,)), ((), ()))


def _att_scores(q, k, diagonal):
    s = lax.dot_general(q, k, _NT, preferred_element_type=F32) * ATT_SCALE
    if not diagonal:
        return s
    row = lax.broadcasted_iota(jnp.int32, s.shape, 0)
    col = lax.broadcasted_iota(jnp.int32, s.shape, 1)
    return jnp.where((col >> 6) <= (row >> 6), s, NEG)


def _att_rows(i):
    return pl.ds(pl.multiple_of(i * ATT_BLK, ATT_BLK), ATT_BLK)


ATT_HEADS = 2


def _attn_fwd(q3, k3, v3, *, name):
    t = q3.shape[1]
    nq = t // ATT_BLK

    def body(q_ref, k_ref, v_ref, o_ref, lse_ref):
        qi = pl.program_id(1)
        qs = [q_ref[hh] for hh in range(ATT_HEADS)]

        def step(j, carry, diagonal=False):
            out = []
            for hh, (m, l, acc) in enumerate(carry):
                s = _att_scores(qs[hh], k_ref[hh, _att_rows(j), :], diagonal)
                m_new = jnp.maximum(m, jnp.max(s, axis=-1, keepdims=True))
                p = jnp.exp(s - m_new)
                alpha = jnp.exp(m - m_new)
                l = alpha * l + jnp.sum(p, axis=-1, keepdims=True)
                acc = alpha * acc + jnp.dot(p.astype(BF16), v_ref[hh, _att_rows(j), :], preferred_element_type=F32)
                out.append((m_new, l, acc))
            return tuple(out)

        init = tuple((jnp.full((ATT_BLK, 1), NEG, F32), jnp.zeros((ATT_BLK, 1), F32),
                      jnp.zeros((ATT_BLK, MLA_V), F32)) for _ in range(ATT_HEADS))
        done = step(qi, lax.fori_loop(0, qi, step, init), diagonal=True)
        for hh, (m, l, acc) in enumerate(done):
            o_ref[:, hh * MLA_V:(hh + 1) * MLA_V] = acc / l
            lse_ref[hh] = m + jnp.log(l)

    return pl.pallas_call(
        body, name=name, grid=(MLA_HEADS // ATT_HEADS, nq),
        in_specs=[pl.BlockSpec((ATT_HEADS, ATT_BLK, MLA_QK_PAD), lambda h, i: (h, i, 0)),
                  pl.BlockSpec((ATT_HEADS, t, MLA_QK_PAD), lambda h, i: (h, 0, 0)),
                  pl.BlockSpec((ATT_HEADS, t, MLA_V), lambda h, i: (h, 0, 0))],
        out_specs=[pl.BlockSpec((ATT_BLK, ATT_HEADS * MLA_V), lambda h, i: (i, h)),
                   pl.BlockSpec((ATT_HEADS, ATT_BLK, 1), lambda h, i: (h, i, 0))],
        out_shape=[jax.ShapeDtypeStruct((t, MLA_HEADS * MLA_V), F32), jax.ShapeDtypeStruct((MLA_HEADS, t, 1), F32)],
        compiler_params=_cp("parallel", "parallel"),
    )(q3, k3, v3)


def _attn_bwd(q3, k3, v3, o, dcat, lse, *, name):
    t = q3.shape[1]
    nq = t // ATT_BLK
    wide = ATT_HEADS * MLA_V

    def body(q_ref, k_ref, v_ref, o_ref, do_ref, lse_ref, dq_ref, dk_ref, dv_ref, delta_ref):
        kj = pl.program_id(1)

        @pl.when(kj == 0)
        def _():
            dq_ref[...] = jnp.zeros_like(dq_ref)
            prod = o_ref[...] * do_ref[...]
            for hh in range(ATT_HEADS):
                delta_ref[hh] = jnp.sum(prod[:, hh * MLA_V:(hh + 1) * MLA_V], axis=-1, keepdims=True)

        def step(i, carry, diagonal=False):
            rows = _att_rows(i)
            out = []
            for hh, (dk, dv) in enumerate(carry):
                k, v = k_ref[hh], v_ref[hh]
                q = q_ref[hh, rows, :]
                dob = do_ref[rows, hh * MLA_V:(hh + 1) * MLA_V].astype(BF16)
                p = jnp.exp(_att_scores(q, k, diagonal) - lse_ref[hh, rows, :])
                dv = dv + lax.dot_general(p.astype(BF16), dob, _TN, preferred_element_type=F32)
                dp = lax.dot_general(dob, v, _NT, preferred_element_type=F32)
                ds = (p * (dp - delta_ref[hh, rows, :]) * ATT_SCALE).astype(BF16)
                dk = dk + lax.dot_general(ds, q, _TN, preferred_element_type=F32)
                dq_ref[hh, rows, :] += jnp.dot(ds, k, preferred_element_type=F32)
                out.append((dk, dv))
            return tuple(out)

        init = tuple((jnp.zeros((ATT_BLK, MLA_QK_PAD), F32), jnp.zeros((ATT_BLK, MLA_V), F32))
                     for _ in range(ATT_HEADS))
        done = lax.fori_loop(kj + 1, nq, step, step(kj, init, diagonal=True))
        for hh, (dk, dv) in enumerate(done):
            dk_ref[hh] = dk
            dv_ref[hh] = dv

    return pl.pallas_call(
        body, name=name, grid=(MLA_HEADS // ATT_HEADS, nq),
        in_specs=[pl.BlockSpec((ATT_HEADS, t, MLA_QK_PAD), lambda h, j: (h, 0, 0)),
                  pl.BlockSpec((ATT_HEADS, ATT_BLK, MLA_QK_PAD), lambda h, j: (h, j, 0)),
                  pl.BlockSpec((ATT_HEADS, ATT_BLK, MLA_V), lambda h, j: (h, j, 0)),
                  pl.BlockSpec((t, wide), lambda h, j: (0, h)),
                  pl.BlockSpec((t, wide), lambda h, j: (0, MLA_HEADS // ATT_HEADS + h)),
                  pl.BlockSpec((ATT_HEADS, t, 1), lambda h, j: (h, 0, 0))],
        out_specs=[pl.BlockSpec((ATT_HEADS, t, MLA_QK_PAD), lambda h, j: (h, 0, 0)),
                   pl.BlockSpec((ATT_HEADS, ATT_BLK, MLA_QK_PAD), lambda h, j: (h, j, 0)),
                   pl.BlockSpec((ATT_HEADS, ATT_BLK, MLA_V), lambda h, j: (h, j, 0))],
        out_shape=[jax.ShapeDtypeStruct((MLA_HEADS, t, MLA_QK_PAD), F32),
                   jax.ShapeDtypeStruct((MLA_HEADS, t, MLA_QK_PAD), F32),
                   jax.ShapeDtypeStruct((MLA_HEADS, t, MLA_V), F32)],
        scratch_shapes=[pltpu.VMEM((ATT_HEADS, t, 1), F32)],
        compiler_params=_cp("parallel", "arbitrary"),
    )(q3, k3, v3, o, dcat, lse)


def _ssd_prep(proj, bias128, alog128, *, name):
    t = proj.shape[0]
    nc = t // CHUNK

    def body(raw_ref, b_ref, al_ref, dt_ref, cs_ref, a_ref):
        xv = raw_ref[...] + b_ref[...]
        dt = jnp.maximum(xv, 0.0) + jnp.log(1.0 + jnp.exp(-jnp.abs(xv)))
        a = -jnp.exp(al_ref[...])
        adt = (dt * a).reshape(nc, CHUNK, LANES)
        li = lax.broadcasted_iota(jnp.int32, (nc, CHUNK, CHUNK), 1)
        si = lax.broadcasted_iota(jnp.int32, (nc, CHUNK, CHUNK), 2)
        tril = jnp.where(si <= li, 1.0, 0.0).astype(F32)
        cs = lax.dot_general(tril, adt, (((2,), (1,)), ((0,), (0,))), precision=HI, preferred_element_type=F32)
        dt_ref[...] = dt
        cs_ref[...] = cs.reshape(t, LANES)
        a_ref[...] = a

    blk = pl.BlockSpec((t, LANES), lambda i: (0, 0))
    row = pl.BlockSpec((1, LANES), lambda i: (0, 0))
    return pl.pallas_call(
        body, name=name, grid=(1,),
        in_specs=[pl.BlockSpec((t, LANES), lambda i: (0, OFF_DT // LANES)), row, row],
        out_specs=[blk, blk, row],
        out_shape=[jax.ShapeDtypeStruct((t, LANES), F32), jax.ShapeDtypeStruct((t, LANES), F32),
                   jax.ShapeDtypeStruct((1, LANES), F32)],
        compiler_params=_cp("arbitrary"),
    )(proj, bias128, alog128)


def _ssd_prep_bwd(ddt128, dadt128, proj, bias128, dt128, a128, dd_h, *, name):
    t = proj.shape[0]

    def body(ddt_ref, dadt_ref, raw_ref, b_ref, dt_ref, a_ref, dd_ref, draw_ref, db_ref, dal_ref, dds_ref):
        draw = ddt_ref[...] * _sigmoid(raw_ref[...] + b_ref[...])
        draw_ref[...] = draw.astype(BF16)
        db_ref[...] = jnp.sum(draw, axis=0, keepdims=True)
        dal_ref[...] = jnp.sum(dadt_ref[...] * dt_ref[...], axis=0, keepdims=True) * a_ref[...]
        dds_ref[...] = jnp.sum(dd_ref[...], axis=-1, keepdims=True)

    blk = pl.BlockSpec((t, LANES), lambda i: (0, 0))
    row = pl.BlockSpec((1, LANES), lambda i: (0, 0))
    return pl.pallas_call(
        body, name=name, grid=(1,),
        in_specs=[blk, blk, pl.BlockSpec((t, LANES), lambda i: (0, OFF_DT // LANES)), row, blk, row,
                  pl.BlockSpec((SSD_HEADS, SSD_P), lambda i: (0, 0))],
        out_specs=[blk, row, row, pl.BlockSpec((SSD_HEADS, 1), lambda i: (0, 0))],
        out_shape=[jax.ShapeDtypeStruct((t, LANES), BF16), jax.ShapeDtypeStruct((1, LANES), F32),
                   jax.ShapeDtypeStruct((1, LANES), F32), jax.ShapeDtypeStruct((SSD_HEADS, 1), F32)],
        compiler_params=_cp("arbitrary"),
    )(ddt128, dadt128, proj, bias128, dt128, a128, dd_h)


def _bdot(a, b, ca, cb, precision=None):
    return lax.dot_general(a, b, (((ca,), (cb,)), ((0,), (0,))), precision=precision, preferred_element_type=F32)


def _head_matrices():
    eye, zero = jnp.eye(SSD_P, dtype=F32), jnp.zeros((SSD_P, SSD_P), F32)
    pick = jnp.stack([jnp.concatenate([eye, zero], axis=0), jnp.concatenate([zero, eye], axis=0)])
    return pick, pick.transpose(0, 2, 1)


def _pick_head(pair_ref, pick_ref, h):
    return jnp.dot(pair_ref[...], pick_ref[h % 2], precision=HI, preferred_element_type=F32)


def _place_head(out_ref, val, place_ref, h):
    wide = jnp.dot(val, place_ref[h % 2], precision=HI, preferred_element_type=F32)

    @pl.when(h % 2 == 0)
    def _():
        out_ref[...] = wide

    @pl.when(h % 2 == 1)
    def _():
        out_ref[...] += wide


def _ssd_common(x2, dt_ref, cs_ref, csr_ref, b_ref, c_ref, nc):
    x = x2.reshape(nc, CHUNK, SSD_P)
    dt = dt_ref[0].reshape(nc, CHUNK, SSD_P)
    cs = cs_ref[0].reshape(nc, CHUNK, SSD_P)
    csr = csr_ref[0]
    bm = b_ref[...].reshape(nc, CHUNK, SSD_N).astype(BF16)
    cm = c_ref[...].reshape(nc, CHUNK, SSD_N).astype(BF16)
    li = lax.broadcasted_iota(jnp.int32, (nc, CHUNK, CHUNK), 1)
    si = lax.broadcasted_iota(jnp.int32, (nc, CHUNK, CHUNK), 2)
    lmat = jnp.exp(jnp.where(si <= li, cs - csr, NEG))
    g = _bdot(cm, bm, 2, 2)
    cs_last = jnp.sum(jnp.where(li == CHUNK - 1, cs, 0.0), axis=1, keepdims=True)
    xdt = x * dt
    dec = jnp.exp(cs_last - cs)
    return x, dt, cs, bm, cm, li, si, lmat, g, cs_last, xdt, dec


def _ssd_fwd(xbc, dt_h, cs_h, cs_row, dskip_h, *, name):
    t = xbc.shape[0]
    nc = t // CHUNK
    hpg = SSD_HEADS // SSD_GROUPS
    pick, place = _head_matrices()

    def body(xs_ref, dt_ref, cs_ref, csr_ref, b_ref, c_ref, dk_ref, pick_ref, place_ref, y_ref, st_ref, sc_ref, cd_ref):
        h = pl.program_id(0)
        x, dt, cs, bm, cm, li, si, lmat, g, cs_last, xdt, dec = _ssd_common(_pick_head(xs_ref, pick_ref, h), dt_ref,
                                                                           cs_ref, csr_ref, b_ref, c_ref, nc)
        yd = _bdot((g * lmat).astype(BF16), xdt.astype(BF16), 2, 1)
        sc_ref[...] = _bdot(bm, (dec * xdt).astype(BF16), 1, 1)
        cd_ref[...] = jnp.exp(cs_last)

        def step(c, s):
            st_ref[0, c] = s
            return s * cd_ref[c] + sc_ref[c]

        lax.fori_loop(0, nc, step, jnp.zeros((SSD_N, SSD_P), F32))
        yo = _bdot(cm, st_ref[0].astype(BF16), 2, 1) * jnp.exp(cs)
        _place_head(y_ref, (yd + yo + dk_ref[0] * x).reshape(t, SSD_P), place_ref, h)

    head = pl.BlockSpec((1, t, SSD_P), lambda h: (h, 0, 0))
    pair = pl.BlockSpec((t, 2 * SSD_P), lambda h: (0, h // 2))
    nxb = D_SSM // SSD_N
    return pl.pallas_call(
        body, name=name, grid=(SSD_HEADS,),
        in_specs=[pair, head, head, pl.BlockSpec((1, nc, 1, CHUNK), lambda h: (h, 0, 0, 0)),
                  pl.BlockSpec((t, SSD_N), lambda h: (0, nxb + h // hpg)),
                  pl.BlockSpec((t, SSD_N), lambda h: (0, nxb + SSD_GROUPS + h // hpg)),
                  pl.BlockSpec((1, 1, SSD_P), lambda h: (h, 0, 0)),
                  pl.BlockSpec((2, 2 * SSD_P, SSD_P), lambda h: (0, 0, 0)),
                  pl.BlockSpec((2, SSD_P, 2 * SSD_P), lambda h: (0, 0, 0))],
        out_specs=[pair, pl.BlockSpec((1, nc, SSD_N, SSD_P), lambda h: (h, 0, 0, 0))],
        out_shape=[jax.ShapeDtypeStruct((t, D_SSM), F32),
                   jax.ShapeDtypeStruct((SSD_HEADS, nc, SSD_N, SSD_P), F32)],
        scratch_shapes=[pltpu.VMEM((nc, SSD_N, SSD_P), F32), pltpu.VMEM((nc, 1, SSD_P), F32)],
        compiler_params=_cp("arbitrary"),
    )(xbc, dt_h, cs_h, cs_row, xbc, xbc, dskip_h, pick, place)


def _ssd_bwd(xbc, dt_h, cs_h, cs_row, dskip_h, a_h, states, dy, *, name):
    t = xbc.shape[0]
    nc = t // CHUNK
    hpg = SSD_HEADS // SSD_GROUPS
    pick, place = _head_matrices()

    def body(xs_ref, dt_ref, cs_ref, csr_ref, b_ref, c_ref, dk_ref, a_ref, st_ref, dy_ref, pick_ref, place_ref,
             dxs_ref, ddt_ref, dadt_ref, db_ref, dc_ref, dd_ref, dsl_ref, dsc_ref, cd_ref):
        h = pl.program_id(0) * hpg + pl.program_id(1)
        x, dt, cs, bm, cm, li, si, lmat, g, cs_last, xdt, dec = _ssd_common(_pick_head(xs_ref, pick_ref, h), dt_ref,
                                                                           cs_ref, csr_ref, b_ref, c_ref, nc)
        dy = _pick_head(dy_ref, pick_ref, h).reshape(nc, CHUNK, SSD_P)
        dyb = dy.astype(BF16)
        xdtb = xdt.astype(BF16)
        sprev = st_ref[0]
        sprevb = sprev.astype(BF16)
        cdec = jnp.exp(cs_last)
        ecs = jnp.exp(cs)
        dw = (ecs * dy).astype(BF16)
        wmat = _bdot(cm, sprevb, 2, 1)
        dcs = jnp.sum(dy * ecs * wmat, axis=2, keepdims=True)
        dcm = _bdot(dw, sprevb, 2, 2)
        dsl_ref[...] = _bdot(cm, dw, 1, 1)
        cd_ref[...] = cdec

        def step(k, ds):
            c = nc - 1 - k
            dsc_ref[c] = ds
            return ds * cd_ref[c] + dsl_ref[c]

        lax.fori_loop(0, nc, step, jnp.zeros((SSD_N, SSD_P), F32))
        dsc = dsc_ref[...]
        dscb = dsc.astype(BF16)
        d_last = jnp.sum(jnp.sum(dsc * sprev, axis=1, keepdims=True) * cdec, axis=2, keepdims=True)
        z = dec * xdt
        dbm = _bdot(z.astype(BF16), dscb, 2, 2)
        dz = _bdot(bm, dscb, 2, 1)
        dxdt = dec * dz
        t2 = jnp.sum(dz * z, axis=2, keepdims=True)
        dcs = dcs - t2
        d_last = d_last + jnp.sum(t2, axis=1, keepdims=True)
        m = g * lmat
        mb = m.astype(BF16)
        dm = _bdot(dyb, xdtb, 2, 2)
        dxdt = dxdt + _bdot(mb, dyb, 1, 1)
        dseg = dm * m
        dcs = dcs + jnp.sum(dseg, axis=2, keepdims=True)
        ones = jnp.ones((nc, CHUNK, SSD_P), F32)
        dcs = dcs - _bdot(dseg, ones, 1, 1, precision=HI)
        dg = (dm * lmat).astype(BF16)
        dcm = dcm + _bdot(dg, bm, 2, 1)
        dbm = dbm + _bdot(dg, cm, 1, 1)
        dcs = dcs + jnp.where(li[:, :, :SSD_P] == CHUNK - 1, d_last, 0.0)
        triu = jnp.where(li <= si, 1.0, 0.0).astype(F32)
        dadt = _bdot(triu, dcs, 2, 1, precision=HI)
        dk = dk_ref[0]
        _place_head(dxs_ref, (dxdt * dt + dk * dy).reshape(t, SSD_P), place_ref, h)
        ddt = jnp.sum(dxdt * x, axis=2, keepdims=True) + dadt * a_ref[0]
        mine = lax.broadcasted_iota(jnp.int32, (t, LANES), 1) == h

        @pl.when(h == 0)
        def _():
            ddt_ref[...] = jnp.zeros_like(ddt_ref)
            dadt_ref[...] = jnp.zeros_like(dadt_ref)

        ddt_ref[...] += jnp.where(mine, jnp.max(ddt, axis=2, keepdims=True).reshape(t, 1), 0.0)
        dadt_ref[...] += jnp.where(mine, jnp.max(dadt, axis=2, keepdims=True).reshape(t, 1), 0.0)
        dd_ref[0] = jnp.sum(jnp.sum(dy * x, axis=1, keepdims=True), axis=0)

        @pl.when(pl.program_id(1) == 0)
        def _():
            db_ref[...] = jnp.zeros_like(db_ref)
            dc_ref[...] = jnp.zeros_like(dc_ref)

        db_ref[...] += dbm.reshape(t, SSD_N)
        dc_ref[...] += dcm.reshape(t, SSD_N)

    head = pl.BlockSpec((1, t, SSD_P), lambda gi, hi: (gi * hpg + hi, 0, 0))
    pair = pl.BlockSpec((t, 2 * SSD_P), lambda gi, hi: (0, (gi * hpg + hi) // 2))
    grp = pl.BlockSpec((t, SSD_N), lambda gi, hi: (0, gi))
    lane = pl.BlockSpec((1, 1, SSD_P), lambda gi, hi: (gi * hpg + hi, 0, 0))
    rows = pl.BlockSpec((t, LANES), lambda gi, hi: (0, 0))
    nxb = D_SSM // SSD_N
    dxs, ddt, dadt, db, dc, dd = pl.pallas_call(
        body, name=name, grid=(SSD_GROUPS, hpg),
        in_specs=[pair, head, head, pl.BlockSpec((1, nc, 1, CHUNK), lambda gi, hi: (gi * hpg + hi, 0, 0, 0)),
                  pl.BlockSpec((t, SSD_N), lambda gi, hi: (0, nxb + gi)),
                  pl.BlockSpec((t, SSD_N), lambda gi, hi: (0, nxb + SSD_GROUPS + gi)), lane, lane,
                  pl.BlockSpec((1, nc, SSD_N, SSD_P), lambda gi, hi: (gi * hpg + hi, 0, 0, 0)), pair,
                  pl.BlockSpec((2, 2 * SSD_P, SSD_P), lambda gi, hi: (0, 0, 0)),
                  pl.BlockSpec((2, SSD_P, 2 * SSD_P), lambda gi, hi: (0, 0, 0))],
        out_specs=[pair, rows, rows, grp, grp, lane],
        out_shape=[jax.ShapeDtypeStruct((t, D_SSM), F32)] + [jax.ShapeDtypeStruct((t, LANES), F32)] * 2
        + [jax.ShapeDtypeStruct((t, SSD_GROUPS * SSD_N), F32)] * 2
        + [jax.ShapeDtypeStruct((SSD_HEADS, 1, SSD_P), F32)],
        scratch_shapes=[pltpu.VMEM((nc, SSD_N, SSD_P), F32), pltpu.VMEM((nc, SSD_N, SSD_P), F32),
                        pltpu.VMEM((nc, 1, SSD_P), F32)],
        compiler_params=_cp("arbitrary", "arbitrary"),
    )(xbc, dt_h, cs_h, cs_row, xbc, xbc, dskip_h, a_h, states, dy, pick, place)
    return jnp.concatenate([dxs, db, dc], axis=1), ddt, dadt, dd


def _ssd_gate_fwd(y, proj, w, *, tr=256, name):
    t = y.shape[0]
    gw = D_SSM // SSD_GROUPS

    def body(y_ref, z_ref, w_ref, o_ref):
        v = y_ref[...] * _silu(z_ref[...])
        for gi in range(SSD_GROUPS):
            vg = v[:, gi * gw:(gi + 1) * gw]
            r = lax.rsqrt(jnp.mean(vg * vg, axis=-1, keepdims=True) + NORM_EPS)
            o_ref[:, gi * gw:(gi + 1) * gw] = (vg * r * w_ref[:, gi * gw:(gi + 1) * gw]).astype(BF16)

    blk = pl.BlockSpec((tr, D_SSM), lambda i: (i, 0))
    return pl.pallas_call(
        body, name=name, grid=(t // tr,), in_specs=[blk, blk, pl.BlockSpec((1, D_SSM), lambda i: (0, 0))],
        out_specs=blk, out_shape=jax.ShapeDtypeStruct((t, D_SSM), BF16), compiler_params=_cp("parallel"),
    )(y, proj, w)


def _ssd_gate_bwd(y, proj, w, dcat, *, tr=256, name):
    t = y.shape[0]
    gw = D_SSM // SSD_GROUPS

    def body(y_ref, z_ref, w_ref, d_ref, dy_ref, dz_ref, dw_ref):
        yv, zv, dv = y_ref[...], z_ref[...], d_ref[...].astype(F32)
        sz = _silu(zv)
        v = yv * sz

        @pl.when(pl.program_id(0) == 0)
        def _():
            dw_ref[...] = jnp.zeros_like(dw_ref)

        for gi in range(SSD_GROUPS):
            sl = slice(gi * gw, (gi + 1) * gw)
            vg, dg = v[:, sl], dv[:, sl]
            r = lax.rsqrt(jnp.mean(vg * vg, axis=-1, keepdims=True) + NORM_EPS)
            vh = vg * r
            gg = dg * w_ref[:, sl]
            dvg = r * (gg - vh * jnp.mean(gg * vh, axis=-1, keepdims=True))
            dy_ref[:, sl] = dvg * sz[:, sl]
            dz_ref[:, sl] = (dvg * yv[:, sl] * _dsilu(zv[:, sl])).astype(BF16)
            dw_ref[:, sl] += jnp.sum(dg * vh, axis=0, keepdims=True)

    blk = pl.BlockSpec((tr, D_SSM), lambda i: (i, 0))
    row = pl.BlockSpec((1, D_SSM), lambda i: (0, 0))
    return pl.pallas_call(
        body, name=name, grid=(t // tr,), in_specs=[blk, blk, row, blk], out_specs=[blk, blk, row],
        out_shape=[jax.ShapeDtypeStruct((t, D_SSM), F32), jax.ShapeDtypeStruct((t, D_SSM), BF16),
                   jax.ShapeDtypeStruct((1, D_SSM), F32)],
        compiler_params=_cp("arbitrary"),
    )(y, proj, w, dcat)


def _pad_lanes(v):
    return jnp.pad(v, ((0, 0), (0, LANES - v.shape[1])))


def _per_head(v128, t):
    return jnp.broadcast_to(v128[:, :SSD_HEADS].T[:, :, None], (SSD_HEADS, t, SSD_P))


def _ssd_forward(proj, conv_w, conv_b, dt_bias, a_log, d_skip, ssd_norm_w):
    t = proj.shape[0]
    nc = t // CHUNK
    xbc = _conv_act_fwd(proj, conv_w, conv_b, kw=SSD_CONV, glu=False, tc=512, coff=OFF_XBC // 512,
                        ncols=SSD_CONV_DIM, out_dtype=F32, name="ssd_conv_fwd")
    bias128, alog128 = _pad_lanes(dt_bias), _pad_lanes(a_log)
    dt128, cs128, a128 = _ssd_prep(proj, bias128, alog128, name="ssd_prep")
    dt_h, cs_h = _per_head(dt128, t), _per_head(cs128, t)
    cs_row = cs128[:, :SSD_HEADS].T.reshape(SSD_HEADS, nc, 1, CHUNK)
    dskip_h = jnp.broadcast_to(d_skip[0][:, None, None], (SSD_HEADS, 1, SSD_P))
    a_h = jnp.broadcast_to(a128[0, :SSD_HEADS][:, None, None], (SSD_HEADS, 1, SSD_P))
    y, states = _ssd_fwd(xbc, dt_h, cs_h, cs_row, dskip_h, name="ssd_scan_fwd")
    y_ssd = _ssd_gate_fwd(y, proj, ssd_norm_w, name="ssd_gate_fwd")
    saved = (proj, conv_w, conv_b, ssd_norm_w, bias128, dt128, a128, dt_h, cs_h, cs_row, xbc, dskip_h, a_h, states, y)
    return y_ssd, saved


def _ssd_backward(saved, dcat):
    proj, conv_w, conv_b, ssd_norm_w, bias128, dt128, a128, dt_h, cs_h, cs_row, xbc, dskip_h, a_h, states, y = saved
    dy, dz, d_norm_w = _ssd_gate_bwd(y, proj, ssd_norm_w, dcat, name="ssd_gate_bwd")
    dxc, ddt128, dadt128, dd_h = _ssd_bwd(xbc, dt_h, cs_h, cs_row, dskip_h, a_h, states, dy, name="ssd_scan_bwd")
    dxbc, d_conv_w, d_conv_b = _conv_act_bwd(proj, conv_w, conv_b, dxc, kw=SSD_CONV, glu=False, tc=512,
                                             coff=OFF_XBC // 512, ncols=SSD_CONV_DIM, name="ssd_conv_bwd")
    d_raw, d_bias, d_alog, d_dskip = _ssd_prep_bwd(ddt128, dadt128, proj, bias128, dt128, a128,
                                                   dd_h.reshape(SSD_HEADS, SSD_P), name="ssd_prep_bwd")
    return (dz, dxbc, d_raw, d_norm_w, d_conv_w, d_conv_b, d_bias[:, :SSD_HEADS], d_alog[:, :SSD_HEADS],
            d_dskip.reshape(1, SSD_HEADS))


def _rope_tables(positions):
    inv_freq = ROPE_THETA ** (-jnp.arange(0, MLA_ROPE, 2, dtype=F32) / MLA_ROPE)
    ang = positions[0].astype(F32)[:, None] * inv_freq
    cos, sin = jnp.cos(ang), jnp.sin(ang)
    z = jnp.zeros_like(cos)
    return jnp.stack([jnp.concatenate([cos, cos, z, z], axis=1), jnp.concatenate([-sin, z, z, z], axis=1),
                      jnp.concatenate([z, sin, z, z], axis=1)])


def _mla_forward(proj, tabs, q_a_norm_w, wq_pad, kv_a_norm_w, wkv):
    qn = _rmsnorm_fwd(proj, q_a_norm_w, width=MLA_Q_RANK, cblk=OFF_QA // MLA_Q_RANK, name="q_a_norm")
    q = _matmul(qn, wq_pad, name="q_b_proj")
    kvn = _rmsnorm_fwd(proj, kv_a_norm_w, width=MLA_KV_RANK, cblk=OFF_CKV // MLA_KV_RANK, name="kv_a_norm")
    kv = _matmul(kvn, wkv, name="kv_b_proj")
    q3, k3, v3 = _mla_prep(q, kv, proj, tabs, name="mla_prep")
    o, lse = _attn_fwd(q3, k3, v3, name="attn_fwd")
    return o, (proj, tabs, q_a_norm_w, wq_pad, kv_a_norm_w, wkv, qn, kvn, q3, k3, v3, o, lse)


def _mla_backward(saved, dcat):
    proj, tabs, q_a_norm_w, wq_pad, kv_a_norm_w, wkv, qn, kvn, q3, k3, v3, o, lse = saved
    dq3, dk3, dv3 = _attn_bwd(q3, k3, v3, o, dcat, lse, name="attn_bwd")
    dq, dkv, dkr = _mla_unprep(dq3, dk3, dv3, tabs, name="mla_unprep")
    d_wq = _matmul(qn, dq, ta=True, out_dtype=BF16, name="d_w_q_b")
    dqn = _matmul(dq, wq_pad, tb=True, name="d_qn")
    dq_a, d_qnw = _rmsnorm_bwd(proj, q_a_norm_w, dqn, width=MLA_Q_RANK, cblk=OFF_QA // MLA_Q_RANK, out_dtype=BF16,
                               name="q_a_norm_bwd")
    d_wkv = _matmul(kvn, dkv, ta=True, out_dtype=BF16, name="d_w_kv_b")
    dkvn = _matmul(dkv, wkv, tb=True, name="d_kvn")
    dckv, d_kvnw = _rmsnorm_bwd(proj, kv_a_norm_w, dkvn, width=MLA_KV_RANK, cblk=OFF_CKV // MLA_KV_RANK,
                                out_dtype=BF16, name="kv_a_norm_bwd")
    return dq_a, dckv, dkr, d_wq, d_wkv, d_qnw, d_kvnw


def _pad_w_q(w):
    r = w.shape[0]
    w3 = w.reshape(r, MLA_HEADS, MLA_NOPE + MLA_ROPE)
    return jnp.pad(w3, ((0, 0), (0, 0), (0, MLA_QK_PAD - MLA_NOPE - MLA_ROPE))).reshape(r, MLA_HEADS * MLA_QK_PAD)


def _unpad_w_q(w):
    r = w.shape[0]
    return w.reshape(r, MLA_HEADS, MLA_QK_PAD)[:, :, :MLA_NOPE + MLA_ROPE].reshape(r, MLA_HEADS * (MLA_NOPE + MLA_ROPE))


def _pad_w_in(w):
    r = w.shape[0]
    o_dt = D_SSM + SSD_CONV_DIM
    o_qa = o_dt + SSD_HEADS
    o_kr = o_qa + MLA_Q_RANK + MLA_KV_RANK
    zeros = lambda n: jnp.zeros((r, n), w.dtype)
    return jnp.concatenate([w[:, :o_dt], w[:, o_qa:o_kr], w[:, o_kr:], zeros(LANES - MLA_ROPE),
                            w[:, o_dt:o_qa], zeros(LANES - SSD_HEADS)], axis=1)


def _unpad_w_in(w):
    return jnp.concatenate([w[:, :OFF_QA], w[:, OFF_DT:OFF_DT + SSD_HEADS], w[:, OFF_QA:OFF_KR + MLA_ROPE]], axis=1)


W_IN_SEGMENTS = ((0, D_SSM + SSD_CONV_DIM, 0), (D_SSM + SSD_CONV_DIM, D_SSM + SSD_CONV_DIM + SSD_HEADS, OFF_DT),
                 (D_SSM + SSD_CONV_DIM + SSD_HEADS, D_IN - MLA_ROPE, OFF_QA), (D_IN - MLA_ROPE, D_IN, OFF_KR))


def _pad_w_in_shards(g):
    n = g.shape[2]
    pieces, at = [], 0
    for lo, hi, start in sorted(W_IN_SEGMENTS, key=lambda seg: seg[2]):
        if start > at:
            pieces.append(jnp.zeros((g.shape[1], start - at), g.dtype))
        for j in range(N_DEV):
            a, b = max(lo, j * n), min(hi, (j + 1) * n)
            if a < b:
                pieces.append(g[j][:, a - j * n:b - j * n])
        at = start + hi - lo
    pieces.append(jnp.zeros((g.shape[1], D_IN_PAD - at), g.dtype))
    return jnp.concatenate(pieces, axis=1)


def _unpad_w_in_shards(w):
    n = D_IN // N_DEV
    shards = []
    for j in range(N_DEV):
        pieces = []
        for lo, hi, start in W_IN_SEGMENTS:
            a, b = max(lo, j * n), min(hi, (j + 1) * n)
            if a < b:
                pieces.append(w[:, start + a - lo:start + b - lo])
        shards.append(jnp.concatenate(pieces, axis=1) if len(pieces) > 1 else pieces[0])
    return jnp.stack(shards)


WEIGHTS = ['mix_norm_w', 'w_in', 'conv_w', 'conv_b', 'dt_bias', 'a_log', 'd_skip', 'ssd_norm_w', 'q_a_norm_w', 'w_q_b',
           'kv_a_norm_w', 'w_kv_b', 'w_out', 'ffn_norm_w', 'w_ffn_up', 'ffn_conv_w', 'ffn_conv_b', 'w_ffn_down',
           'ple_norm_w', 'w_ple_gate', 'b_ple_gate', 'w_ple_proj', 'ple_post_norm_w', 'final_norm_w']
BIG = ['w_in', 'w_q_b', 'w_kv_b', 'w_out', 'w_ffn_up', 'w_ffn_down', 'w_ple_gate', 'w_ple_proj']
COL_SHARDED = ('w_in', 'w_q_b', 'w_kv_b', 'w_ffn_up', 'w_ple_proj')
CONV = ['conv_w', 'ffn_conv_w']
REPL = [n for n in WEIGHTS if n not in BIG and n not in CONV]
FFN_INV = tuple(int(i) for i in np.argsort(FFN_PERM))


def _cat_cols(g):
    return jnp.concatenate([g[j] for j in range(N_DEV)], axis=1)


def _split_cols(w):
    n = w.shape[1] // N_DEV
    return jnp.stack([w[:, j * n:(j + 1) * n] for j in range(N_DEV)])


def _interleave(v):
    r = v.shape[0]
    return v.reshape(r, N_DEV, FFN_TC)[:, jnp.array(FFN_PERM)].reshape(r, N_DEV * FFN_TC)


def _deinterleave(v):
    r = v.shape[0]
    return v.reshape(r, N_DEV, FFN_TC)[:, jnp.array(FFN_INV)].reshape(r, N_DEV * FFN_TC)


def _assemble_weights(g):
    layout = {
        'w_in': _pad_w_in_shards,
        'w_q_b': lambda v: _pad_w_q(_cat_cols(v)),
        'w_kv_b': _cat_cols,
        'w_out': lambda v: v.reshape(D_MODEL, D_MODEL),
        'w_ffn_up': lambda v: v,
        'w_ffn_down': lambda v: v.reshape(D_FF, D_MODEL),
        'w_ple_gate': lambda v: v.reshape(D_MODEL, D_MODEL),
        'w_ple_proj': _cat_cols,
        'conv_w': _cat_cols,
        'ffn_conv_w': lambda v: _interleave(_cat_cols(v)),
    }
    return {n: layout[n](v) for n, v in g.items()}


WEIGHT_GROUPS = {'a': ['w_in', 'w_q_b', 'w_kv_b', 'conv_w'], 'b': ['w_out'],
                 'c': ['w_ffn_up', 'ffn_conv_w', 'w_ffn_down', 'w_ple_gate', 'w_ple_proj']}
GRAD_GROUPS = {'p': ['w_ple_proj', 'w_ple_gate', 'w_ffn_down'], 'r': ['w_ffn_up'], 's': ['w_out'],
               't': ['w_q_b', 'w_kv_b', 'w_in']}


def _ffn_perm(j):
    return (j % 2) * (N_DEV // 2) + j // 2


def _local_step(x, p, tabs, get_w, s, target, emit, relay, settle):
    t = x.shape[0]
    s = dict(s)
    half = D_MODEL // 2
    up_cols = 2 * D_FF
    ffn_conv_b = _interleave(s['ffn_conv_b'])
    w = dict(get_w('a', None))
    h = _rmsnorm_fwd(x, s['mix_norm_w'], width=D_MODEL, name="mix_norm")
    proj = _matmul(h, w['w_in'], name="in_proj")
    y_ssd, ssd_saved = _ssd_forward(proj, w['conv_w'], s['conv_b'], s['dt_bias'], s['a_log'], s['d_skip'],
                                    s['ssd_norm_w'])
    o, mla_saved = _mla_forward(proj, tabs, s['q_a_norm_w'], w['w_q_b'], s['kv_a_norm_w'], w['w_kv_b'])
    tk_o, tn_o = _tile(half, MM_TK), _tile(D_MODEL, MM_TILE)
    w.update(get_w('b', o))
    x1 = _matmul(y_ssd, w['w_out'], add=x, mnk=(t, D_MODEL, half), name="out_proj_ssd")
    x1 = _matmul(o, w['w_out'], add=x1, mnk=(t, D_MODEL, half), name="out_proj_mla",
                 b_spec=pl.BlockSpec((tk_o, tn_o), lambda i, j, kk: (kk + half // tk_o, j)))
    hf = _rmsnorm_fwd(x1, s['ffn_norm_w'], width=D_MODEL, name="ffn_norm")
    w.update(get_w('c', hf))
    tk_u = _tile(D_MODEL, MM_TK)
    u = _matmul(hf, w['w_ffn_up'], mnk=(t, up_cols, D_MODEL), tn=FFN_TC, name="ffn_up",
                b_spec=pl.BlockSpec((1, tk_u, FFN_TC), lambda i, j, kk: (_ffn_perm(j), kk, 0)))
    act = _conv_act_fwd(u, w['ffn_conv_w'], ffn_conv_b, kw=FFN_CONV, glu=True, tc=2 * FFN_TC, coff=0, ncols=up_cols,
                        out_dtype=BF16, name="ffn_act")
    x2 = _matmul(act, w['w_ffn_down'], add=x1, name="ffn_down")
    hp = _rmsnorm_fwd(x2, s['ple_norm_w'], width=D_MODEL, name="ple_norm")
    gl = _matmul(hp, w['w_ple_gate'], bias=s['b_ple_gate'], name="ple_gate")
    pe = _matmul(p, w['w_ple_proj'], name="ple_proj")
    x3 = _ple_fwd(x2, gl, pe, s['ple_post_norm_w'], name="ple_mix")
    loss, dx3, d_final = _loss_head(x3, s['final_norm_w'], target, name="loss_head")
    dgl, d_bgate, dpe, d_post = _ple_bwd(dx3, gl, pe, s['ple_post_norm_w'], name="ple_mix_bwd")
    d_wproj = _matmul(p, dpe, ta=True, out_dtype=BF16, name="d_w_ple_proj")
    d_wgate = _matmul(hp, dgl, ta=True, out_dtype=BF16, name="d_w_ple_gate")
    dhp = _matmul(dgl, w['w_ple_gate'], tb=True, name="d_ple_normed")
    dx2, d_plenorm = _rmsnorm_bwd(x2, s['ple_norm_w'], dhp, dx3, width=D_MODEL, name="ple_norm_bwd")
    dact = _matmul(dx2, w['w_ffn_down'], tb=True, name="d_ffn_act")
    d_wdown = _matmul(act, dx2, ta=True, out_dtype=BF16, name="d_w_ffn_down")
    zz = emit('p', {'w_ple_proj': _split_cols(d_wproj), 'w_ple_gate': d_wgate.reshape(N_DEV, D_MODEL // N_DEV, D_MODEL),
                    'w_ffn_down': d_wdown.reshape(N_DEV, D_FF // N_DEV, D_MODEL)})
    du, d_fconv_w, d_fconv_b = _conv_act_bwd(u, w['ffn_conv_w'], ffn_conv_b + zz, dact, kw=FFN_CONV, glu=True,
                                             tc=2 * FFN_TC, coff=0, ncols=up_cols, name="ffn_act_bwd")
    zz = zz + relay('p', du)
    tm_u = _tile(D_MODEL, MM_TILE)
    d_wup = _matmul(hf, du, ta=True, out_dtype=BF16, mnk=(D_MODEL, up_cols, t), tn=FFN_TC, name="d_w_ffn_up",
                    o_spec=pl.BlockSpec((1, tm_u, FFN_TC), lambda i, j, kk: (_ffn_perm(j), i, 0)),
                    o_shape=(N_DEV, D_MODEL, FFN_TC))
    zz = zz + emit('r', {'w_ffn_up': d_wup})
    dhf = _matmul(du, w['w_ffn_up'], tb=True, mnk=(t, D_MODEL, up_cols), tk=FFN_TC, name="d_ffn_normed",
                  b_spec=pl.BlockSpec((1, tn_o, FFN_TC), lambda i, j, kk: (_ffn_perm(kk), j, 0)))
    zz = zz + relay('r', dhf) + settle('p')
    dx1, d_ffnnorm = _rmsnorm_bwd(x1, s['ffn_norm_w'] + zz, dhf, dx2, width=D_MODEL, name="ffn_norm_bwd")
    dcat = _matmul(dx1, w['w_out'], tb=True, name="d_mixed")
    d_wout = jnp.concatenate([_matmul(y_ssd, dx1, ta=True, out_dtype=BF16, name="d_w_out_ssd"),
                              _matmul(o, dx1, ta=True, out_dtype=BF16, name="d_w_out_mla")], axis=0)
    zz = zz + emit('s', {'w_out': d_wout.reshape(N_DEV, D_MODEL // N_DEV, D_MODEL)})
    ssd_saved = ssd_saved[:3] + (ssd_saved[3] + zz,) + ssd_saved[4:]
    dz, dxbc, d_raw, d_ssdnorm, d_conv_w, d_conv_b, d_dtb, d_alog, d_dskip = _ssd_backward(ssd_saved, dcat)
    zz = zz + relay('s', dz)
    mla_saved = mla_saved[:-1] + (mla_saved[-1] + zz,)
    dq_a, dckv, dkr, d_wq, d_wkv, d_qnorm, d_kvnorm = _mla_backward(mla_saved, dcat)
    d_raw = (d_raw + settle('r')).astype(BF16)
    dproj = jnp.concatenate([dz, dxbc, dq_a, dckv, dkr, d_raw], axis=1)
    d_win = _matmul(h, dproj, ta=True, out_dtype=BF16, name="d_w_in")
    dh = _matmul(dproj, w['w_in'], tb=True, name="d_in_normed")
    dx, d_mixnorm = _rmsnorm_bwd(x, s['mix_norm_w'] + settle('s'), dh, dx1, width=D_MODEL, name="mix_norm_bwd")
    emit('t', {'w_in': _unpad_w_in_shards(d_win), 'w_q_b': _split_cols(_unpad_w_q(d_wq)),
               'w_kv_b': _split_cols(d_wkv)})
    relay('t', dx)
    settle('t')
    conv = {'conv_w': d_conv_w, 'ffn_conv_w': _deinterleave(d_fconv_w)}
    vec = {
        'mix_norm_w': d_mixnorm, 'conv_b': d_conv_b, 'dt_bias': d_dtb, 'a_log': d_alog, 'd_skip': d_dskip,
        'ssd_norm_w': d_ssdnorm, 'q_a_norm_w': d_qnorm, 'kv_a_norm_w': d_kvnorm, 'ffn_norm_w': d_ffnnorm,
        'ffn_conv_b': _deinterleave(d_fconv_b), 'ple_norm_w': d_plenorm, 'b_ple_gate': d_bgate,
        'ple_post_norm_w': d_post, 'final_norm_w': d_final,
    }
    return loss, dx, conv, vec


MESH = pl.DeviceIdType.MESH
FLIPS = ((0, 0, 1), (1, 0, 0), (0, 1, 0), (1, 1, 0), (1, 0, 1), (0, 1, 1), (1, 1, 1))


def _exchange(items, *, gather, name):
    n = len(items)

    def body(*refs):
        ins, outs = refs[:n], refs[n:2 * n]
        send_sems, recv_sems, local_sems = refs[2 * n:]
        x, y, c = lax.axis_index("x"), lax.axis_index("y"), lax.axis_index("c")
        me = 4 * x + 2 * y + c
        peers = [(jnp.where(fx, 1 - x, x), jnp.where(fy, 1 - y, y), jnp.where(fc, 1 - c, c)) for fx, fy, fc in FLIPS]
        slot = [4 * px + 2 * py + pc for px, py, pc in peers]
        local, sends = [], []
        for wi in range(n):
            cp = pltpu.make_async_copy(ins[wi] if gather else ins[wi].at[me], outs[wi].at[me], local_sems.at[wi])
            cp.start()
            local.append(cp)
            for k, peer in enumerate(peers):
                cp = pltpu.make_async_remote_copy(
                    src_ref=ins[wi] if gather else ins[wi].at[slot[k]], dst_ref=outs[wi].at[me],
                    send_sem=send_sems.at[k, wi], recv_sem=recv_sems.at[k, wi], device_id=peer, device_id_type=MESH)
                cp.start()
                sends.append(cp)
        for wi in range(n):
            for k, peer in enumerate(peers):
                pltpu.make_async_remote_copy(
                    src_ref=outs[wi].at[slot[k]], dst_ref=outs[wi].at[slot[k]], send_sem=send_sems.at[k, wi],
                    recv_sem=recv_sems.at[k, wi], device_id=peer, device_id_type=MESH).wait_recv()
        for cp in sends:
            cp.wait_send()
        for cp in local:
            cp.wait()

    hbm = pl.BlockSpec(memory_space=pltpu.HBM)
    out_shape = [jax.ShapeDtypeStruct(((N_DEV,) + v.shape) if gather else v.shape, v.dtype) for v in items]
    return pl.pallas_call(
        body, name=name, in_specs=[hbm] * n, out_specs=[hbm] * n, out_shape=out_shape,
        scratch_shapes=[pltpu.SemaphoreType.DMA((len(FLIPS), n)), pltpu.SemaphoreType.DMA((len(FLIPS), n)),
                        pltpu.SemaphoreType.DMA((n,))],
    )(*items)


HBM_SPEC = pl.BlockSpec(memory_space=pltpu.HBM)
SEM_SPEC = pl.BlockSpec(memory_space=pltpu.SEMAPHORE)
EFFECT = pltpu.SideEffectType.DATAFLOW_SIDE_EFFECTING


def _peers():
    x, y, c = lax.axis_index("x"), lax.axis_index("y"), lax.axis_index("c")
    peers = [(jnp.where(fx, 1 - x, x), jnp.where(fy, 1 - y, y), jnp.where(fc, 1 - c, c)) for fx, fy, fc in FLIPS]
    return 4 * x + 2 * y + c, peers, [4 * px + 2 * py + pc for px, py, pc in peers]


def _split_start(bufs, ncopies, plan, *, name):
    nb = len(bufs)

    def body(*refs):
        send_sems, recv_sems, token = refs[nb], refs[nb + 1], refs[2 * nb + 2]
        for i, (src, dst, peer, _) in enumerate(plan(refs[:nb])):
            pltpu.make_async_remote_copy(src_ref=src, dst_ref=dst, send_sem=send_sems.at[i], recv_sem=recv_sems.at[i],
                                         device_id=peer, device_id_type=MESH).start()
        token[...] = jnp.zeros_like(token)

    res = pl.pallas_call(
        body, name=name, in_specs=[HBM_SPEC] * nb,
        out_specs=[SEM_SPEC, SEM_SPEC] + [HBM_SPEC] * nb + [pl.BlockSpec(memory_space=pltpu.VMEM)],
        out_shape=[pltpu.SemaphoreType.DMA((ncopies,)), pltpu.SemaphoreType.DMA((ncopies,))]
        + [pltpu.HBM(v.shape, v.dtype) for v in bufs] + [jax.ShapeDtypeStruct((HALO, LANES), F32)],
        input_output_aliases={i: 2 + i for i in range(nb)},
        compiler_params=pltpu.CompilerParams(has_side_effects=EFFECT),
    )(*[pltpu.with_memory_space_constraint(v, pltpu.HBM) for v in bufs])
    return (res[0], res[1], list(res[2:2 + nb])), res[2 + nb]


def _split_wait(started, after, plan, local_plan, *, name):
    send_sems, recv_sems, bufs = started
    nb = len(bufs)
    nlocal = len(local_plan(bufs))

    def body(*refs):
        send_sems, recv_sems = refs[nb], refs[nb + 1]
        local_sems = refs[2 * nb + 3]
        local = []
        for j, (src, dst) in enumerate(local_plan(refs[:nb])):
            cp = pltpu.make_async_copy(src, dst, local_sems.at[j])
            cp.start()
            local.append(cp)
        for i, (src, _, peer, incoming) in enumerate(plan(refs[:nb])):
            cp = pltpu.make_async_remote_copy(src_ref=src, dst_ref=incoming, send_sem=send_sems.at[i],
                                              recv_sem=recv_sems.at[i], device_id=peer, device_id_type=MESH)
            cp.wait_send()
            cp.wait_recv()
        for cp in local:
            cp.wait()

    res = pl.pallas_call(
        body, name=name, in_specs=[HBM_SPEC] * nb + [SEM_SPEC, SEM_SPEC, pl.BlockSpec(memory_space=pl.ANY)],
        out_specs=[HBM_SPEC] * nb, out_shape=[pltpu.HBM(v.shape, v.dtype) for v in bufs],
        input_output_aliases={i: i for i in range(nb)},
        scratch_shapes=[pltpu.SemaphoreType.DMA((max(nlocal, 1),))],
        compiler_params=pltpu.CompilerParams(has_side_effects=EFFECT),
    )(*bufs, send_sems, recv_sems, after)
    return list(res)


def _place():
    x, y, c = lax.axis_index("x"), lax.axis_index("y"), lax.axis_index("c")
    others = [((1 - x, y, c), 2 * (1 - x) + y), ((x, 1 - y, c), 2 * x + 1 - y), ((1 - x, 1 - y, c), 2 * (1 - x) + 1 - y)]
    return 4 * x + 2 * y + c, 2 * x + y, c, (x, y, 1 - c), others


def _gather1_plan(n):
    def plan(refs):
        me, _, _, sibling, others = _place()
        out = []
        for wi in range(n):
            item, land = refs[wi], refs[n + wi]
            out.append((item, land.at[me], sibling, land.at[me + 1 - 2 * lax.axis_index("c")]))
            for peer, chip in others:
                out.append((item, land.at[me], peer, land.at[2 * chip + lax.axis_index("c")]))
        return out

    return plan


def _gather1_local(n):
    def plan(refs):
        me = _place()[0]
        return [(refs[wi], refs[n + wi].at[me]) for wi in range(n)]

    return plan


def _gather2_plan(n):
    def plan(refs):
        _, _, c, sibling, others = _place()
        out = []
        for wi in range(n):
            land = refs[wi]
            for _, chip in others:
                out.append((land.at[2 * chip + c], land.at[2 * chip + c], sibling, land.at[2 * chip + 1 - c]))
        return out

    return plan


def _gather_start(items, *, name):
    lands = [lax.empty((N_DEV,) + v.shape, v.dtype) for v in items]
    return _split_start(items + lands, 4 * len(items), _gather1_plan(len(items)), name=name)


def _gather_forward(started, after, *, name):
    n = len(started[2]) // 2
    bufs = _split_wait(started, after, _gather1_plan(n), _gather1_local(n), name=name + "_wait")
    return _split_start(bufs[n:], 3 * n, _gather2_plan(n), name=name + "_start")


def _gather_finish(started, after, *, name):
    n = len(started[2])
    return _split_wait(started, after, _gather2_plan(n), lambda refs: [], name=name)


def _handshake(peers):
    barrier = pltpu.get_barrier_semaphore()
    for peer in peers:
        pl.semaphore_signal(barrier, inc=1, device_id=peer, device_id_type=MESH)
    pl.semaphore_wait(barrier, len(peers))


def _remote(src, dst, send_sem, recv_sem, peer):
    return pltpu.make_async_remote_copy(src_ref=src, dst_ref=dst, send_sem=send_sem, recv_sem=recv_sem, device_id=peer,
                                        device_id_type=MESH)


def _sequencer_gather(items, *, collective_id, name):
    n = len(items)
    srcs = [jax.new_ref(v, memory_space=pltpu.MemorySpace.HBM) for v in items]
    lands = [jax.empty_ref(jax.ShapeDtypeStruct((N_DEV,) + v.shape, v.dtype), memory_space=pltpu.MemorySpace.HBM)
             for v in items]
    dma = pltpu.SemaphoreType.DMA

    @pl.kernel(mesh=plsc.ScalarSubcoreMesh(axis_name="sequencer", num_cores=1), name=name,
               scratch_types=(dma((4 * n,)), dma((4 * n,)), dma((3 * n,)), dma((3 * n,)), dma((n,))),
               compiler_params=pltpu.CompilerParams(collective_id=collective_id))
    def launch(send1, recv1, send2, recv2, local_sems):
        _, _, _, sibling, others = _place()
        _handshake([sibling] + [peer for peer, _ in others])
        hop1 = _gather1_plan(n)(srcs + lands)
        hop2 = _gather2_plan(n)(lands)
        local = [pltpu.make_async_copy(src, dst, local_sems.at[j])
                 for j, (src, dst) in enumerate(_gather1_local(n)(srcs + lands))]
        for cp in local:
            cp.start()
        for i, (src, dst, peer, _) in enumerate(hop1):
            _remote(src, dst, send1.at[i], recv1.at[i], peer).start()
        for wi in range(n):
            for j in range(3):
                i1, i2 = 4 * wi + 1 + j, 3 * wi + j
                src, _, peer, incoming = hop1[i1]
                _remote(src, incoming, send1.at[i1], recv1.at[i1], peer).wait_recv()
                src, dst, peer, _ = hop2[i2]
                _remote(src, dst, send2.at[i2], recv2.at[i2], peer).start()
        for wi in range(n):
            src, _, peer, incoming = hop1[4 * wi]
            _remote(src, incoming, send1.at[4 * wi], recv1.at[4 * wi], peer).wait_recv()
        for i, (src, _, peer, incoming) in enumerate(hop2):
            cp = _remote(src, incoming, send2.at[i], recv2.at[i], peer)
            cp.wait_send()
            cp.wait_recv()
        for i, (src, dst, peer, _) in enumerate(hop1):
            _remote(src, dst, send1.at[i], recv1.at[i], peer).wait_send()
        for cp in local:
            cp.wait()

    launch()
    return [land[...] for land in lands]


def _sequencer_exchange(sources, land_shapes, ncopies, plan, local_plan, peers, *, collective_id, name):
    srcs = [jax.new_ref(v, memory_space=pltpu.MemorySpace.HBM) for v in sources]
    lands = [jax.empty_ref(s, memory_space=pltpu.MemorySpace.HBM) for s in land_shapes]
    nlocal = len(local_plan(srcs + lands))
    dma = pltpu.SemaphoreType.DMA

    @pl.kernel(mesh=plsc.ScalarSubcoreMesh(axis_name="sequencer", num_cores=1), name=name,
               scratch_types=(dma((ncopies,)), dma((ncopies,)), dma((max(nlocal, 1),))),
               compiler_params=pltpu.CompilerParams(collective_id=collective_id))
    def launch(send_sems, recv_sems, local_sems):
        _handshake(peers(_place()))
        copies = plan(srcs + lands)
        local = [pltpu.make_async_copy(src, dst, local_sems.at[j])
                 for j, (src, dst) in enumerate(local_plan(srcs + lands))]
        for cp in local:
            cp.start()
        for i, (src, dst, peer, _) in enumerate(copies):
            _remote(src, dst, send_sems.at[i], recv_sems.at[i], peer).start()
        for i, (src, _, peer, incoming) in enumerate(copies):
            cp = _remote(src, incoming, send_sems.at[i], recv_sems.at[i], peer)
            cp.wait_send()
            cp.wait_recv()
        for cp in local:
            cp.wait()

    launch()
    return [land[...] for land in lands]


def _sequencer_scatter_hop1(parts, *, collective_id, name):
    n = len(parts)
    shapes = [jax.ShapeDtypeStruct((N_CHIP,) + v.shape[1:], v.dtype) for v in parts]
    return _sequencer_exchange(parts, shapes, N_CHIP * n, _scatter1_plan(n), lambda refs: [], lambda place: [place[3]],
                               collective_id=collective_id, name=name)


def _sequencer_scatter_hop2(sums, *, collective_id, name):
    n = len(sums)
    shapes = [jax.ShapeDtypeStruct(v.shape, v.dtype) for v in sums]
    return _sequencer_exchange(sums, shapes, 3 * n, _scatter2_plan(n), _scatter2_local(n),
                               lambda place: [peer for peer, _ in place[4]], collective_id=collective_id, name=name)


N_CHIP = N_DEV // 2


def _scatter1_plan(n):
    def plan(refs):
        _, _, c, sibling, _ = _place()
        out = []
        for wi in range(n):
            parts, half = refs[wi], refs[n + wi]
            for chip in range(N_CHIP):
                out.append((parts.at[2 * chip + 1 - c], half.at[chip], sibling, half.at[chip]))
        return out

    return plan


def _scatter2_plan(n):
    def plan(refs):
        _, my_chip, _, _, others = _place()
        out = []
        for wi in range(n):
            sums, recv = refs[wi], refs[n + wi]
            for peer, chip in others:
                out.append((sums.at[chip], recv.at[my_chip], peer, recv.at[chip]))
        return out

    return plan


def _scatter2_local(n):
    def plan(refs):
        my_chip = _place()[1]
        return [(refs[wi].at[my_chip], refs[n + wi].at[my_chip]) for wi in range(n)]

    return plan


def _pair_add(parts, half, core, *, name):
    _, r, c = parts.shape
    tr = max(d for d in range(HALO, 257, HALO) if r % d == 0) if r > 256 else r
    parts4 = parts.reshape(N_CHIP, 2, r, c)

    def body(core_ref, p_ref, h_ref, o_ref):
        o_ref[...] = (p_ref[:, 0].astype(F32) + h_ref[...].astype(F32)).astype(o_ref.dtype)

    return pl.pallas_call(
        body, name=name,
        grid_spec=pltpu.PrefetchScalarGridSpec(
            num_scalar_prefetch=1, grid=(r // tr,),
            in_specs=[pl.BlockSpec((N_CHIP, 1, tr, c), lambda i, core_ref: (0, core_ref[0], i, 0)),
                      pl.BlockSpec((N_CHIP, tr, c), lambda i, core_ref: (0, i, 0))],
            out_specs=pl.BlockSpec((N_CHIP, tr, c), lambda i, core_ref: (0, i, 0))),
        out_shape=jax.ShapeDtypeStruct((N_CHIP, r, c), parts.dtype), compiler_params=_cp("parallel"),
    )(core, parts4, half)


def _scatter_start(parts, *, name):
    halves = [lax.empty((N_CHIP,) + v.shape[1:], v.dtype) for v in parts]
    return _split_start(parts + halves, N_CHIP * len(parts), _scatter1_plan(len(parts)), name=name)


def _scatter_forward(started, after, core, *, name):
    n = len(started[2]) // 2
    bufs = _split_wait(started, after, _scatter1_plan(n), lambda refs: [], name=name + "_wait")
    sums = [_pair_add(bufs[wi], bufs[n + wi], core, name=name + "_add%d" % wi) for wi in range(n)]
    recvs = [lax.empty(v.shape, v.dtype) for v in sums]
    return _split_start(sums + recvs, 3 * n, _scatter2_plan(n), name=name + "_start")


def _scatter_finish(started, after, *, name):
    n = len(started[2]) // 2
    return _split_wait(started, after, _scatter2_plan(n), _scatter2_local(n), name=name)[n:]


def _adamw(parts, w, m, v, *, name):
    r, c = w.shape
    nparts = parts.shape[0]
    tr = max(d for d in range(HALO, 129, HALO) if r % d == 0) if r > 128 else r

    def body(p_ref, w_ref, m_ref, v_ref, g_ref, d_ref, mo_ref, vo_ref):
        g = p_ref[0].astype(F32)
        for k in range(1, nparts):
            g = g + p_ref[k].astype(F32)
        mn = ADAM_B1 * m_ref[...] + (1.0 - ADAM_B1) * g
        vn = ADAM_B2 * v_ref[...] + (1.0 - ADAM_B2) * (g * g)
        m_hat = mn / (1.0 - ADAM_B1 ** ADAM_STEP)
        v_hat = vn / (1.0 - ADAM_B2 ** ADAM_STEP)
        g_ref[...] = g
        d_ref[...] = -ADAM_LR * (m_hat / (jnp.sqrt(v_hat) + ADAM_EPS) + ADAM_WD * w_ref[...])
        mo_ref[...] = mn
        vo_ref[...] = vn

    blk = pl.BlockSpec((tr, c), lambda i: (i, 0))
    return pl.pallas_call(
        body, name=name, grid=(r // tr,), in_specs=[pl.BlockSpec((nparts, tr, c), lambda i: (0, i, 0)), blk, blk, blk],
        out_specs=[blk] * 4, out_shape=[jax.ShapeDtypeStruct((r, c), F32)] * 4, compiler_params=_cp("parallel"),
    )(parts, w, m, v)


def _pack_rows(vs, rows):
    lead = vs[0].shape[:-1] if vs[0].ndim > 1 else ()
    flat = jnp.concatenate(vs, axis=-1)
    pad = rows * LANES - flat.shape[-1]
    flat = jnp.pad(flat, [(0, 0)] * len(lead) + [(0, pad)])
    return flat.reshape(lead + (rows, LANES))


def kernel(x, p, positions, mix_norm_w, w_in, conv_w, conv_b, dt_bias, a_log, d_skip, ssd_norm_w, q_a_norm_w, w_q_b, kv_a_norm_w, w_kv_b, w_out, ffn_norm_w, w_ffn_up, ffn_conv_w, ffn_conv_b, w_ffn_down, ple_norm_w, w_ple_gate, b_ple_gate, w_ple_proj, ple_post_norm_w, final_norm_w, loss_target, m_mix_norm_w, m_w_in, m_conv_w, m_conv_b, m_dt_bias, m_a_log, m_d_skip, m_ssd_norm_w, m_q_a_norm_w, m_w_q_b, m_kv_a_norm_w, m_w_kv_b, m_w_out, m_ffn_norm_w, m_w_ffn_up, m_ffn_conv_w, m_ffn_conv_b, m_w_ffn_down, m_ple_norm_w, m_w_ple_gate, m_b_ple_gate, m_w_ple_proj, m_ple_post_norm_w, m_final_norm_w, v_mix_norm_w, v_w_in, v_conv_w, v_conv_b, v_dt_bias, v_a_log, v_d_skip, v_ssd_norm_w, v_q_a_norm_w, v_w_q_b, v_kv_a_norm_w, v_w_kv_b, v_w_out, v_ffn_norm_w, v_w_ffn_up, v_ffn_conv_w, v_ffn_conv_b, v_w_ffn_down, v_ple_norm_w, v_w_ple_gate, v_b_ple_gate, v_w_ple_proj, v_ple_post_norm_w, v_final_norm_w):
    given = dict(locals())
    shapes = {n: given[n].shape for n in WEIGHTS}
    w2 = {n: given[n].reshape(given[n].shape[-2:] if n in BIG or n in CONV else (1, -1)) for n in WEIGHTS}
    m2 = {n: given['m_' + n].reshape(w2[n].shape) for n in WEIGHTS}
    v2 = {n: given['v_' + n].reshape(w2[n].shape) for n in WEIGHTS}
    me = 4 * lax.axis_index("x") + 2 * lax.axis_index("y") + lax.axis_index("c")

    core = lax.axis_index("c").astype(jnp.int32).reshape(1)

    def shards(grp, zero):
        return [(w2[n] + zero).astype(BF16) if n in BIG else w2[n] + zero for n in WEIGHT_GROUPS[grp]]

    first, token = _gather_start(shards('a', 0.0), name="gather_a_hop1")
    first, token = _gather_forward(first, token, name="gather_a_hop2")
    zero = token[0, 0]
    later = _sequencer_gather(shards('b', zero) + shards('c', zero), collective_id=1, name="gather_later")
    later = dict(zip(WEIGHT_GROUPS['b'] + WEIGHT_GROUPS['c'], later))

    def get_w(grp, after):
        if grp == 'a':
            lands = dict(zip(WEIGHT_GROUPS[grp], _gather_finish(first, token, name="gather_a_done")))
        else:
            lands = {n: later[n] for n in WEIGHT_GROUPS[grp]}
        return _assemble_weights(lands)

    scatters = {}

    hop_ids = {grp: 2 + 2 * i for i, grp in enumerate(GRAD_GROUPS)}

    def zero_of(arrays):
        return sum(v[(0,) * v.ndim].astype(F32) * 0.0 for v in arrays)

    def emit(grp, grads):
        scatters[grp], tok = _scatter_start([grads[n] for n in GRAD_GROUPS[grp]], name="scatter_" + grp + "_hop1")
        return tok[0, 0]

    def relay(grp, after):
        n = len(GRAD_GROUPS[grp])
        bufs = _split_wait(scatters[grp], after, _scatter1_plan(n), lambda refs: [], name="scatter_" + grp + "_hop1_wait")
        sums = [_pair_add(bufs[i], bufs[n + i], core, name="scatter_%s_add%d" % (grp, i)) for i in range(n)]
        scatters[grp] = _sequencer_scatter_hop2(sums, collective_id=hop_ids[grp] + 1, name="scatter_" + grp + "_hop2")
        return zero_of(sums)

    out_g, out_d, out_m, out_v = {}, {}, {}, {}

    def settle(grp):
        for n, parts in zip(GRAD_GROUPS[grp], scatters[grp]):
            out_g[n], out_d[n], out_m[n], out_v[n] = _adamw(parts, w2[n], m2[n], v2[n], name="adamw_" + n)
        return zero_of([out_g[n] for n in GRAD_GROUPS[grp]])

    vecs = {n: w2[n] for n in REPL}
    vecs['mix_norm_w'] = vecs['mix_norm_w'] + zero
    loss, dx, g_conv, g_vec = _local_step(x[0], p[0, 0], _rope_tables(positions), get_w, vecs, loss_target[0], emit,
                                          relay, settle)
    n_small = sum(g_vec[n].shape[1] for n in REPL) + sum(g_conv[n].size for n in CONV) + 1
    rows_small = -(-n_small // (LANES * HALO)) * HALO
    small = _pack_rows([g_vec[n] for n in REPL] + [g_conv[n].reshape(1, -1) for n in CONV] + [loss], rows_small)

    small = small + zero_of([out_g[GRAD_GROUPS['t'][-1]]])
    all_small = _exchange([small], gather=True, name="gather_small_grads")[0].reshape(N_DEV, rows_small * LANES)
    pieces, off = [], 0
    for n in REPL:
        k = g_vec[n].shape[1]
        pieces.append(all_small[:, off:off + k])
        off += k
    for n in CONV:
        kw, cols = g_conv[n].shape
        full = all_small[:, off:off + kw * cols].reshape(N_DEV, kw, cols)
        mine = lax.dynamic_slice_in_dim(full, me * (cols // N_DEV), cols // N_DEV, axis=2)
        pieces.append(mine.reshape(N_DEV, kw * (cols // N_DEV)))
        off += kw * cols
    pieces.append(all_small[:, off:off + 1])
    small_names = REPL + CONV
    n_mine = sum(q.shape[1] for q in pieces)
    rows_mine = -(-n_mine // (LANES * HALO)) * HALO
    zero = jnp.zeros((1, 1), F32)
    packed = [_pack_rows([src[n].reshape(1, -1) for n in small_names] + [zero], rows_mine).reshape(rows_mine, LANES)
              for src in (w2, m2, v2)]
    sg, sd, sm, sv = _adamw(_pack_rows(pieces, rows_mine), *packed, name="adamw_small")
    off = 0
    for n in small_names:
        k = w2[n].size
        for dst, src in ((out_g, sg), (out_d, sd), (out_m, sm), (out_v, sv)):
            dst[n] = src.reshape(-1)[off:off + k].reshape(w2[n].shape)
        off += k
    total_loss = sg.reshape(-1)[off]

    outs = [total_loss, dx[None]]
    for res in (out_g, out_d, out_m, out_v):
        outs += [res[n].reshape(shapes[n]) for n in WEIGHTS]
    return tuple(outs)
```

```python
import functools
import math

import numpy as np
import jax
import jax.numpy as jnp
from jax import lax
from jax.experimental import pallas as pl
from jax.experimental.pallas import tpu as pltpu
from jax.experimental.pallas import tpu_sc as plsc

F32 = jnp.float32
BF16 = jnp.bfloat16
HI = lax.Precision.HIGHEST

D_MODEL = 2048
CHUNK = 64
D_SSM = 1024
SSD_P = 64
SSD_HEADS = 16
SSD_GROUPS = 2
SSD_N = 128
SSD_CONV = 4
SSD_CONV_DIM = D_SSM + 2 * SSD_GROUPS * SSD_N
MLA_HEADS = 8
MLA_NOPE = 128
MLA_ROPE = 64
MLA_V = 128
MLA_Q_RANK = 512
MLA_KV_RANK = 256
MLA_QK_PAD = 256
ROPE_THETA = 10000.0
D_FF = 5632
FFN_CONV = 3
PLE_DIM = 256
NORM_EPS = 1e-6
ADAM_LR, ADAM_B1, ADAM_B2, ADAM_EPS, ADAM_WD, ADAM_STEP = 0.001, 0.9, 0.999, 1e-08, 0.01, 10
N_DEV = 8

OFF_Z, OFF_XBC, OFF_QA, OFF_CKV, OFF_KR, OFF_DT, D_IN_PAD = 0, 1024, 2560, 3072, 3328, 3456, 3584
D_IN = 3408
LANES = 128
HALO = 8
VMEM_LIMIT = 56 * 1024 * 1024
FFN_TC = D_FF * 2 // N_DEV
FFN_PERM = (0, 4, 1, 5, 2, 6, 3, 7)
NEG = -1e30


def _cp(*sem):
    return pltpu.CompilerParams(dimension_semantics=tuple(sem), vmem_limit_bytes=VMEM_LIMIT)


def _tile(n, want):
    if n <= want:
        return n
    best = max(d for d in range(LANES, want + 1, LANES) if n % d == 0)
    return best


def _sigmoid(x):
    return 0.5 * (jnp.tanh(0.5 * x) + 1.0)


def _silu(x):
    return x * _sigmoid(x)


def _dsilu(x):
    s = _sigmoid(x)
    return s * (1.0 + x * (1.0 - s))


MM_TILE = 1408
MM_TK = 2816


def _matmul(a, b, *, ta=False, tb=False, out_dtype=F32, add=None, bias=None, tm=MM_TILE, tn=MM_TILE, tk=MM_TK, name,
            mnk=None, a_spec=None, b_spec=None, o_spec=None, o_shape=None):
    if mnk is None:
        m, k = (a.shape[1], a.shape[0]) if ta else a.shape
        n = b.shape[0] if tb else b.shape[1]
        assert k == (b.shape[1] if tb else b.shape[0])
    else:
        m, n, k = mnk
    tm, tn, tk = _tile(m, tm), _tile(n, tn), _tile(k, tk)
    nk = k // tk
    dims = (((0 if ta else 1,), (1 if tb else 0,)), ((), ()))

    def body(*refs):
        a_ref, b_ref = refs[0], refs[1]
        pos = 2
        add_ref = bias_ref = None
        if add is not None:
            add_ref = refs[pos]
            pos += 1
        if bias is not None:
            bias_ref = refs[pos]
            pos += 1
        o_ref = refs[pos]
        kk = pl.program_id(2)
        av = a_ref[...]
        bv = b_ref[...]
        av = av.reshape(av.shape[-2:]).astype(BF16)
        bv = bv.reshape(bv.shape[-2:]).astype(BF16)
        prod = lax.dot_general(av, bv, dims, preferred_element_type=F32)

        def finish(r):
            if bias_ref is not None:
                r = r + bias_ref[...]
            if add_ref is not None:
                r = r + add_ref[...].astype(F32)
            o_ref[...] = r.astype(out_dtype).reshape(o_ref.shape)

        if nk == 1:
            finish(prod)
        else:
            acc_ref = refs[pos + 1]

            @pl.when(kk == 0)
            def _():
                acc_ref[...] = prod

            @pl.when(kk > 0)
            def _():
                acc_ref[...] += prod

            @pl.when(kk == nk - 1)
            def _():
                finish(acc_ref[...])

    if a_spec is None:
        a_spec = (pl.BlockSpec((tk, tm), lambda i, j, kk: (kk, i)) if ta
                  else pl.BlockSpec((tm, tk), lambda i, j, kk: (i, kk)))
    if b_spec is None:
        b_spec = (pl.BlockSpec((tn, tk), lambda i, j, kk: (j, kk)) if tb
                  else pl.BlockSpec((tk, tn), lambda i, j, kk: (kk, j)))
    if o_spec is None:
        o_spec = pl.BlockSpec((tm, tn), lambda i, j, kk: (i, j))
    if o_shape is None:
        o_shape = (m, n)
    in_specs = [a_spec, b_spec]
    args = [a, b]
    if add is not None:
        in_specs.append(pl.BlockSpec((tm, tn), lambda i, j, kk: (i, j)))
        args.append(add)
    if bias is not None:
        in_specs.append(pl.BlockSpec((1, tn), lambda i, j, kk: (0, j)))
        args.append(bias)
    return pl.pallas_call(
        body, name=name, grid=(m // tm, n // tn, nk), in_specs=in_specs, out_specs=o_spec,
        out_shape=jax.ShapeDtypeStruct(o_shape, out_dtype),
        scratch_shapes=[pltpu.VMEM((tm, tn), F32)] if nk > 1 else [],
        compiler_params=_cp("parallel", "parallel", "arbitrary"),
    )(*args)


def _rmsnorm_fwd(x, w, *, width, cblk=0, out_dtype=BF16, tr=256, name):
    t = x.shape[0]

    def body(x_ref, w_ref, o_ref):
        xv = x_ref[...].astype(F32)
        r = lax.rsqrt(jnp.mean(xv * xv, axis=-1, keepdims=True) + NORM_EPS)
        o_ref[...] = (xv * r * w_ref[...]).astype(out_dtype)

    return pl.pallas_call(
        body, name=name, grid=(t // tr,),
        in_specs=[pl.BlockSpec((tr, width), lambda i: (i, cblk)), pl.BlockSpec((1, width), lambda i: (0, 0))],
        out_specs=pl.BlockSpec((tr, width), lambda i: (i, 0)),
        out_shape=jax.ShapeDtypeStruct((t, width), out_dtype),
        compiler_params=_cp("parallel"),
    )(x, w)


def _rmsnorm_bwd(x, w, dy, add=None, *, width, cblk=0, out_dtype=F32, tr=256, name):
    t = x.shape[0]

    def body(*refs):
        if add is None:
            x_ref, w_ref, dy_ref, dx_ref, dw_ref = refs
            add_ref = None
        else:
            x_ref, w_ref, dy_ref, add_ref, dx_ref, dw_ref = refs
        xv = x_ref[...].astype(F32)
        dyv = dy_ref[...].astype(F32)
        r = lax.rsqrt(jnp.mean(xv * xv, axis=-1, keepdims=True) + NORM_EPS)
        xh = xv * r
        g = dyv * w_ref[...]
        dx = r * (g - xh * jnp.mean(g * xh, axis=-1, keepdims=True))
        if add_ref is not None:
            dx = dx + add_ref[...].astype(F32)
        dx_ref[...] = dx.astype(out_dtype)

        @pl.when(pl.program_id(0) == 0)
        def _():
            dw_ref[...] = jnp.zeros_like(dw_ref)

        dw_ref[...] += jnp.sum(dyv * xh, axis=0, keepdims=True)

    in_specs = [pl.BlockSpec((tr, width), lambda i: (i, cblk)), pl.BlockSpec((1, width), lambda i: (0, 0)),
                pl.BlockSpec((tr, width), lambda i: (i, 0))]
    args = [x, w, dy]
    if add is not None:
        in_specs.append(pl.BlockSpec((tr, width), lambda i: (i, 0)))
        args.append(add)
    return pl.pallas_call(
        body, name=name, grid=(t // tr,), in_specs=in_specs,
        out_specs=[pl.BlockSpec((tr, width), lambda i: (i, 0)), pl.BlockSpec((1, width), lambda i: (0, 0))],
        out_shape=[jax.ShapeDtypeStruct((t, width), out_dtype), jax.ShapeDtypeStruct((1, width), F32)],
        compiler_params=_cp("arbitrary"),
    )(*args)


def _shift_down(prev_halo, cur, j):
    if j == 0:
        return cur
    ext = jnp.concatenate([prev_halo, cur], axis=0)
    return pltpu.roll(ext, j, axis=0)[HALO:]


def _shift_up(cur, next_halo, j):
    if j == 0:
        return cur
    ext = jnp.concatenate([cur, next_halo], axis=0)
    return pltpu.roll(ext, ext.shape[0] - j, axis=0)[:cur.shape[0]]


def _conv_rows(prev, cur, w, b, kw):
    shifted = [cur]
    out = b + w[kw - 1:kw] * cur
    for j in range(1, kw):
        sh = _shift_down(prev, cur, j)
        shifted.append(sh)
        out = out + w[kw - 1 - j:kw - j] * sh
    return out, shifted


def _act_fwd(c, glu):
    if glu:
        half = c.shape[1] // 2
        return _silu(c[:, :half]) * c[:, half:]
    return _silu(c)


def _act_bwd(c, dout, glu):
    if glu:
        half = c.shape[1] // 2
        g, up = c[:, :half], c[:, half:]
        s = _sigmoid(g)
        gs = g * s
        return jnp.concatenate([dout * up * (s + gs * (1.0 - s)), dout * gs], axis=1)
    return dout * _dsilu(c)


def _conv_act_fwd(u, w, b, *, kw, glu, tc, coff, ncols, out_dtype, tr=256, name):
    t = u.shape[0]
    nb = ncols // tc
    oc = tc // 2 if glu else tc

    def body(u_ref, uh_ref, w_ref, b_ref, o_ref):
        prev = jnp.where(pl.program_id(0) == 0, 0.0, uh_ref[...])
        c, _ = _conv_rows(prev, u_ref[...], w_ref[...], b_ref[...], kw)
        o_ref[...] = _act_fwd(c, glu).astype(out_dtype)

    return pl.pallas_call(
        body, name=name, grid=(t // tr, nb),
        in_specs=[pl.BlockSpec((tr, tc), lambda i, j: (i, j + coff)),
                  pl.BlockSpec((HALO, tc), lambda i, j: (jnp.maximum(i * (tr // HALO) - 1, 0), j + coff)),
                  pl.BlockSpec((kw, tc), lambda i, j: (0, j)), pl.BlockSpec((1, tc), lambda i, j: (0, j))],
        out_specs=pl.BlockSpec((tr, oc), lambda i, j: (i, j)),
        out_shape=jax.ShapeDtypeStruct((t, nb * oc), out_dtype),
        compiler_params=_cp("parallel", "parallel"),
    )(u, u, w, b)


def _conv_act_bwd(u, w, b, dout, *, kw, glu, tc, coff, ncols, tr=256, name):
    t = u.shape[0]
    nb = ncols // tc
    nt = t // tr
    oc = tc // 2 if glu else tc

    def body(u_ref, up_ref, un_ref, d_ref, dn_ref, w_ref, b_ref, du_ref, dw_ref, db_ref):
        i = pl.program_id(1)
        cur, nxt, wv, bv = u_ref[...], un_ref[...], w_ref[...], b_ref[...]
        prev = jnp.where(i == 0, 0.0, up_ref[...])
        c_cur, shifted = _conv_rows(prev, cur, wv, bv, kw)
        c_nxt, _ = _conv_rows(cur[tr - HALO:], nxt, wv, bv, kw)
        d_cur = _act_bwd(c_cur, d_ref[...].astype(F32), glu)
        d_nxt = _act_bwd(c_nxt, jnp.where(i == nt - 1, 0.0, dn_ref[...].astype(F32)), glu)
        du = wv[kw - 1:kw] * d_cur
        for j in range(1, kw):
            du = du + wv[kw - 1 - j:kw - j] * _shift_up(d_cur, d_nxt, j)
        du_ref[...] = du.astype(BF16)

        @pl.when(i == 0)
        def _():
            dw_ref[...] = jnp.zeros_like(dw_ref)
            db_ref[...] = jnp.zeros_like(db_ref)

        db_ref[...] += jnp.sum(d_cur, axis=0, keepdims=True)
        dw_ref[...] += jnp.concatenate(
            [jnp.sum(d_cur * shifted[kw - 1 - k], axis=0, keepdims=True) for k in range(kw)], axis=0)

    nh = tr // HALO
    return pl.pallas_call(
        body, name=name, grid=(nb, nt),
        in_specs=[pl.BlockSpec((tr, tc), lambda j, i: (i, j + coff)),
                  pl.BlockSpec((HALO, tc), lambda j, i: (jnp.maximum(i * nh - 1, 0), j + coff)),
                  pl.BlockSpec((HALO, tc), lambda j, i: (jnp.minimum((i + 1) * nh, t // HALO - 1), j + coff)),
                  pl.BlockSpec((tr, oc), lambda j, i: (i, j)),
                  pl.BlockSpec((HALO, oc), lambda j, i: (jnp.minimum((i + 1) * nh, t // HALO - 1), j)),
                  pl.BlockSpec((kw, tc), lambda j, i: (0, j)), pl.BlockSpec((1, tc), lambda j, i: (0, j))],
        out_specs=[pl.BlockSpec((tr, tc), lambda j, i: (i, j)), pl.BlockSpec((kw, tc), lambda j, i: (0, j)),
                   pl.BlockSpec((1, tc), lambda j, i: (0, j))],
        out_shape=[jax.ShapeDtypeStruct((t, ncols), BF16), jax.ShapeDtypeStruct((kw, ncols), F32),
                   jax.ShapeDtypeStruct((1, ncols), F32)],
        compiler_params=_cp("parallel", "arbitrary"),
    )(u, u, u, dout, dout, w, b)


def _ple_fwd(x2, gl, pe, pw, *, tr=256, name):
    t, d = x2.shape

    def body(x_ref, gl_ref, pe_ref, pw_ref, o_ref):
        pv = pe_ref[...]
        r = lax.rsqrt(jnp.mean(pv * pv, axis=-1, keepdims=True) + NORM_EPS)
        o_ref[...] = x_ref[...] + _sigmoid(gl_ref[...]) * (pv * r * pw_ref[...])

    blk = pl.BlockSpec((tr, d), lambda i: (i, 0))
    return pl.pallas_call(
        body, name=name, grid=(t // tr,), in_specs=[blk, blk, blk, pl.BlockSpec((1, d), lambda i: (0, 0))],
        out_specs=blk, out_shape=jax.ShapeDtypeStruct((t, d), F32), compiler_params=_cp("parallel"),
    )(x2, gl, pe, pw)


def _ple_bwd(dx3, gl, pe, pw, *, tr=256, name):
    t, d = dx3.shape

    def body(dx_ref, gl_ref, pe_ref, pw_ref, dgl_ref, db_ref, dpe_ref, dpw_ref):
        dx, pv, pwv = dx_ref[...], pe_ref[...], pw_ref[...]
        gate = _sigmoid(gl_ref[...])
        r = lax.rsqrt(jnp.mean(pv * pv, axis=-1, keepdims=True) + NORM_EPS)
        ph = pv * r
        dgl = dx * (ph * pwv) * gate * (1.0 - gate)
        de = dx * gate
        g = de * pwv
        dgl_ref[...] = dgl.astype(BF16)
        dpe_ref[...] = (r * (g - ph * jnp.mean(g * ph, axis=-1, keepdims=True))).astype(BF16)

        @pl.when(pl.program_id(0) == 0)
        def _():
            db_ref[...] = jnp.zeros_like(db_ref)
            dpw_ref[...] = jnp.zeros_like(dpw_ref)

        db_ref[...] += jnp.sum(dgl, axis=0, keepdims=True)
        dpw_ref[...] += jnp.sum(de * ph, axis=0, keepdims=True)

    blk = pl.BlockSpec((tr, d), lambda i: (i, 0))
    row = pl.BlockSpec((1, d), lambda i: (0, 0))
    return pl.pallas_call(
        body, name=name, grid=(t // tr,), in_specs=[blk, blk, blk, row], out_specs=[blk, row, blk, row],
        out_shape=[jax.ShapeDtypeStruct((t, d), BF16), jax.ShapeDtypeStruct((1, d), F32),
                   jax.ShapeDtypeStruct((t, d), BF16), jax.ShapeDtypeStruct((1, d), F32)],
        compiler_params=_cp("arbitrary"),
    )(dx3, gl, pe, pw)


def _loss_head(x3, fw, target, *, tr=256, name):
    t, d = x3.shape

    def body(x_ref, w_ref, t_ref, l_ref, dx_ref, dw_ref):
        xv, wv = x_ref[...], w_ref[...]
        r = lax.rsqrt(jnp.mean(xv * xv, axis=-1, keepdims=True) + NORM_EPS)
        xh = xv * r
        err = xh * wv - t_ref[...]
        dy = err * (1.0 / d)
        g = dy * wv
        dx_ref[...] = r * (g - xh * jnp.mean(g * xh, axis=-1, keepdims=True))

        @pl.when(pl.program_id(0) == 0)
        def _():
            l_ref[...] = jnp.zeros_like(l_ref)
            dw_ref[...] = jnp.zeros_like(dw_ref)

        l_ref[...] += 0.5 * jnp.sum(jnp.mean(err * err, axis=-1, keepdims=True), axis=0, keepdims=True)
        dw_ref[...] += jnp.sum(dy * xh, axis=0, keepdims=True)

    blk = pl.BlockSpec((tr, d), lambda i: (i, 0))
    row = pl.BlockSpec((1, d), lambda i: (0, 0))
    return pl.pallas_call(
        body, name=name, grid=(t // tr,), in_specs=[blk, row, blk],
        out_specs=[pl.BlockSpec((1, 1), lambda i: (0, 0)), blk, row],
        out_shape=[jax.ShapeDtypeStruct((1, 1), F32), jax.ShapeDtypeStruct((t, d), F32),
                   jax.ShapeDtypeStruct((1, d), F32)],
        compiler_params=_cp("arbitrary"),
    )(x3, fw, target)


def _rope(blk, tab_ref):
    return blk * tab_ref[0] + pltpu.roll(blk, 96, axis=1) * tab_ref[1] + pltpu.roll(blk, 32, axis=1) * tab_ref[2]


def _unrope(g, tab_ref):
    return g * tab_ref[0] + pltpu.roll(g * tab_ref[1], 32, axis=1) + pltpu.roll(g * tab_ref[2], 96, axis=1)


def _mla_prep(q, kv, proj, tabs, *, tr=512, name):
    t = q.shape[0]

    def body(q_ref, kv_ref, kr_ref, tab_ref, qo_ref, ko_ref, vo_ref):
        qv, kvv = q_ref[...], kv_ref[...]
        qo_ref[0, :, :MLA_NOPE] = qv[:, :MLA_NOPE].astype(BF16)
        qo_ref[0, :, MLA_NOPE:] = _rope(qv[:, MLA_NOPE:], tab_ref).astype(BF16)
        ko_ref[0, :, :MLA_NOPE] = kvv[:, :MLA_NOPE].astype(BF16)
        ko_ref[0, :, MLA_NOPE:] = _rope(kr_ref[...], tab_ref).astype(BF16)
        vo_ref[0] = kvv[:, MLA_NOPE:].astype(BF16)

    return pl.pallas_call(
        body, name=name, grid=(t // tr, MLA_HEADS),
        in_specs=[pl.BlockSpec((tr, MLA_QK_PAD), lambda i, h: (i, h)),
                  pl.BlockSpec((tr, MLA_NOPE + MLA_V), lambda i, h: (i, h)),
                  pl.BlockSpec((tr, LANES), lambda i, h: (i, OFF_KR // LANES)),
                  pl.BlockSpec((3, tr, LANES), lambda i, h: (0, i, 0))],
        out_specs=[pl.BlockSpec((1, tr, MLA_QK_PAD), lambda i, h: (h, i, 0)),
                   pl.BlockSpec((1, tr, MLA_QK_PAD), lambda i, h: (h, i, 0)),
                   pl.BlockSpec((1, tr, MLA_V), lambda i, h: (h, i, 0))],
        out_shape=[jax.ShapeDtypeStruct((MLA_HEADS, t, MLA_QK_PAD), BF16),
                   jax.ShapeDtypeStruct((MLA_HEADS, t, MLA_QK_PAD), BF16),
                   jax.ShapeDtypeStruct((MLA_HEADS, t, MLA_V), BF16)],
        compiler_params=_cp("parallel", "parallel"),
    )(q, kv, proj, tabs)


def _mla_unprep(dq3, dk3, dv3, tabs, *, tr=256, name):
    t = dq3.shape[1]

    def body(dq_ref, dk_ref, dv_ref, tab_ref, qo_ref, kvo_ref, kro_ref):
        kr = jnp.zeros((tr, LANES), F32)
        for h in range(MLA_HEADS):
            c0 = h * MLA_QK_PAD
            qo_ref[:, c0:c0 + MLA_NOPE] = dq_ref[h, :, :MLA_NOPE].astype(BF16)
            qo_ref[:, c0 + MLA_NOPE:c0 + MLA_QK_PAD] = _unrope(dq_ref[h, :, MLA_NOPE:], tab_ref).astype(BF16)
            kvo_ref[:, c0:c0 + MLA_NOPE] = dk_ref[h, :, :MLA_NOPE].astype(BF16)
            kvo_ref[:, c0 + MLA_NOPE:c0 + MLA_QK_PAD] = dv_ref[h].astype(BF16)
            kr = kr + dk_ref[h, :, MLA_NOPE:]
        kro_ref[...] = _unrope(kr, tab_ref).astype(BF16)

    return pl.pallas_call(
        body, name=name, grid=(t // tr,),
        in_specs=[pl.BlockSpec((MLA_HEADS, tr, MLA_QK_PAD), lambda i: (0, i, 0)),
                  pl.BlockSpec((MLA_HEADS, tr, MLA_QK_PAD), lambda i: (0, i, 0)),
                  pl.BlockSpec((MLA_HEADS, tr, MLA_V), lambda i: (0, i, 0)),
                  pl.BlockSpec((3, tr, LANES), lambda i: (0, i, 0))],
        out_specs=[pl.BlockSpec((tr, MLA_HEADS * MLA_QK_PAD), lambda i: (i, 0)),
                   pl.BlockSpec((tr, MLA_HEADS * MLA_QK_PAD), lambda i: (i, 0)),
                   pl.BlockSpec((tr, LANES), lambda i: (i, 0))],
        out_shape=[jax.ShapeDtypeStruct((t, MLA_HEADS * MLA_QK_PAD), BF16),
                   jax.ShapeDtypeStruct((t, MLA_HEADS * MLA_QK_PAD), BF16),
                   jax.ShapeDtypeStruct((t, LANES), BF16)],
        compiler_params=_cp("parallel"),
    )(dq3, dk3, dv3, tabs)


ATT_BLK = 256
ATT_SCALE = 1.0 / math.sqrt(MLA_NOPE + MLA_ROPE)
_NT = (((1,), (1,)), ((), ()))
_TN = (((0,), (0,)), ((), ()))


def _att_scores(q, k, diagonal):
    s = lax.dot_general(q, k, _NT, preferred_element_type=F32) * ATT_SCALE
    if not diagonal:
        return s
    row = lax.broadcasted_iota(jnp.int32, s.shape, 0)
    col = lax.broadcasted_iota(jnp.int32, s.shape, 1)
    return jnp.where((col >> 6) <= (row >> 6), s, NEG)


def _att_rows(i):
    return pl.ds(pl.multiple_of(i * ATT_BLK, ATT_BLK), ATT_BLK)


ATT_HEADS = 2


def _attn_fwd(q3, k3, v3, *, name):
    t = q3.shape[1]
    nq = t // ATT_BLK

    def body(q_ref, k_ref, v_ref, o_ref, lse_ref):
        qi = pl.program_id(1)
        qs = [q_ref[hh] for hh in range(ATT_HEADS)]

        def step(j, carry, diagonal=False):
            out = []
            for hh, (m, l, acc) in enumerate(carry):
                s = _att_scores(qs[hh], k_ref[hh, _att_rows(j), :], diagonal)
                m_new = jnp.maximum(m, jnp.max(s, axis=-1, keepdims=True))
                p = jnp.exp(s - m_new)
                alpha = jnp.exp(m - m_new)
                l = alpha * l + jnp.sum(p, axis=-1, keepdims=True)
                acc = alpha * acc + jnp.dot(p.astype(BF16), v_ref[hh, _att_rows(j), :], preferred_element_type=F32)
                out.append((m_new, l, acc))
            return tuple(out)

        init = tuple((jnp.full((ATT_BLK, 1), NEG, F32), jnp.zeros((ATT_BLK, 1), F32),
                      jnp.zeros((ATT_BLK, MLA_V), F32)) for _ in range(ATT_HEADS))
        done = step(qi, lax.fori_loop(0, qi, step, init), diagonal=True)
        for hh, (m, l, acc) in enumerate(done):
            o_ref[:, hh * MLA_V:(hh + 1) * MLA_V] = acc / l
            lse_ref[hh] = m + jnp.log(l)

    return pl.pallas_call(
        body, name=name, grid=(MLA_HEADS // ATT_HEADS, nq),
        in_specs=[pl.BlockSpec((ATT_HEADS, ATT_BLK, MLA_QK_PAD), lambda h, i: (h, i, 0)),
                  pl.BlockSpec((ATT_HEADS, t, MLA_QK_PAD), lambda h, i: (h, 0, 0)),
                  pl.BlockSpec((ATT_HEADS, t, MLA_V), lambda h, i: (h, 0, 0))],
        out_specs=[pl.BlockSpec((ATT_BLK, ATT_HEADS * MLA_V), lambda h, i: (i, h)),
                   pl.BlockSpec((ATT_HEADS, ATT_BLK, 1), lambda h, i: (h, i, 0))],
        out_shape=[jax.ShapeDtypeStruct((t, MLA_HEADS * MLA_V), F32), jax.ShapeDtypeStruct((MLA_HEADS, t, 1), F32)],
        compiler_params=_cp("parallel", "parallel"),
    )(q3, k3, v3)


def _attn_bwd(q3, k3, v3, o, dcat, lse, *, name):
    t = q3.shape[1]
    nq = t // ATT_BLK
    wide = ATT_HEADS * MLA_V

    def body(q_ref, k_ref, v_ref, o_ref, do_ref, lse_ref, dq_ref, dk_ref, dv_ref, delta_ref):
        kj = pl.program_id(1)

        @pl.when(kj == 0)
        def _():
            dq_ref[...] = jnp.zeros_like(dq_ref)
            prod = o_ref[...] * do_ref[...]
            for hh in range(ATT_HEADS):
                delta_ref[hh] = jnp.sum(prod[:, hh * MLA_V:(hh + 1) * MLA_V], axis=-1, keepdims=True)

        def step(i, carry, diagonal=False):
            rows = _att_rows(i)
            out = []
            for hh, (dk, dv) in enumerate(carry):
                k, v = k_ref[hh], v_ref[hh]
                q = q_ref[hh, rows, :]
                dob = do_ref[rows, hh * MLA_V:(hh + 1) * MLA_V].astype(BF16)
                p = jnp.exp(_att_scores(q, k, diagonal) - lse_ref[hh, rows, :])
                dv = dv + lax.dot_general(p.astype(BF16), dob, _TN, preferred_element_type=F32)
                dp = lax.dot_general(dob, v, _NT, preferred_element_type=F32)
                ds = (p * (dp - delta_ref[hh, rows, :]) * ATT_SCALE).astype(BF16)
                dk = dk + lax.dot_general(ds, q, _TN, preferred_element_type=F32)
                dq_ref[hh, rows, :] += jnp.dot(ds, k, preferred_element_type=F32)
                out.append((dk, dv))
            return tuple(out)

        init = tuple((jnp.zeros((ATT_BLK, MLA_QK_PAD), F32), jnp.zeros((ATT_BLK, MLA_V), F32))
                     for _ in range(ATT_HEADS))
        done = lax.fori_loop(kj + 1, nq, step, step(kj, init, diagonal=True))
        for hh, (dk, dv) in enumerate(done):
            dk_ref[hh] = dk
            dv_ref[hh] = dv

    return pl.pallas_call(
        body, name=name, grid=(MLA_HEADS // ATT_HEADS, nq),
        in_specs=[pl.BlockSpec((ATT_HEADS, t, MLA_QK_PAD), lambda h, j: (h, 0, 0)),
                  pl.BlockSpec((ATT_HEADS, ATT_BLK, MLA_QK_PAD), lambda h, j: (h, j, 0)),
                  pl.BlockSpec((ATT_HEADS, ATT_BLK, MLA_V), lambda h, j: (h, j, 0)),
                  pl.BlockSpec((t, wide), lambda h, j: (0, h)),
                  pl.BlockSpec((t, wide), lambda h, j: (0, MLA_HEADS // ATT_HEADS + h)),
                  pl.BlockSpec((ATT_HEADS, t, 1), lambda h, j: (h, 0, 0))],
        out_specs=[pl.BlockSpec((ATT_HEADS, t, MLA_QK_PAD), lambda h, j: (h, 0, 0)),
                   pl.BlockSpec((ATT_HEADS, ATT_BLK, MLA_QK_PAD), lambda h, j: (h, j, 0)),
                   pl.BlockSpec((ATT_HEADS, ATT_BLK, MLA_V), lambda h, j: (h, j, 0))],
        out_shape=[jax.ShapeDtypeStruct((MLA_HEADS, t, MLA_QK_PAD), F32),
                   jax.ShapeDtypeStruct((MLA_HEADS, t, MLA_QK_PAD), F32),
                   jax.ShapeDtypeStruct((MLA_HEADS, t, MLA_V), F32)],
        scratch_shapes=[pltpu.VMEM((ATT_HEADS, t, 1), F32)],
        compiler_params=_cp("parallel", "arbitrary"),
    )(q3, k3, v3, o, dcat, lse)


def _ssd_prep(proj, bias128, alog128, *, name):
    t = proj.shape[0]
    nc = t // CHUNK

    def body(raw_ref, b_ref, al_ref, dt_ref, cs_ref, a_ref):
        xv = raw_ref[...] + b_ref[...]
        dt = jnp.maximum(xv, 0.0) + jnp.log(1.0 + jnp.exp(-jnp.abs(xv)))
        a = -jnp.exp(al_ref[...])
        adt = (dt * a).reshape(nc, CHUNK, LANES)
        li = lax.broadcasted_iota(jnp.int32, (nc, CHUNK, CHUNK), 1)
        si = lax.broadcasted_iota(jnp.int32, (nc, CHUNK, CHUNK), 2)
        tril = jnp.where(si <= li, 1.0, 0.0).astype(F32)
        cs = lax.dot_general(tril, adt, (((2,), (1,)), ((0,), (0,))), precision=HI, preferred_element_type=F32)
        dt_ref[...] = dt
        cs_ref[...] = cs.reshape(t, LANES)
        a_ref[...] = a

    blk = pl.BlockSpec((t, LANES), lambda i: (0, 0))
    row = pl.BlockSpec((1, LANES), lambda i: (0, 0))
    return pl.pallas_call(
        body, name=name, grid=(1,),
        in_specs=[pl.BlockSpec((t, LANES), lambda i: (0, OFF_DT // LANES)), row, row],
        out_specs=[blk, blk, row],
        out_shape=[jax.ShapeDtypeStruct((t, LANES), F32), jax.ShapeDtypeStruct((t, LANES), F32),
                   jax.ShapeDtypeStruct((1, LANES), F32)],
        compiler_params=_cp("arbitrary"),
    )(proj, bias128, alog128)


def _ssd_prep_bwd(ddt128, dadt128, proj, bias128, dt128, a128, dd_h, *, name):
    t = proj.shape[0]

    def body(ddt_ref, dadt_ref, raw_ref, b_ref, dt_ref, a_ref, dd_ref, draw_ref, db_ref, dal_ref, dds_ref):
        draw = ddt_ref[...] * _sigmoid(raw_ref[...] + b_ref[...])
        draw_ref[...] = draw.astype(BF16)
        db_ref[...] = jnp.sum(draw, axis=0, keepdims=True)
        dal_ref[...] = jnp.sum(dadt_ref[...] * dt_ref[...], axis=0, keepdims=True) * a_ref[...]
        dds_ref[...] = jnp.sum(dd_ref[...], axis=-1, keepdims=True)

    blk = pl.BlockSpec((t, LANES), lambda i: (0, 0))
    row = pl.BlockSpec((1, LANES), lambda i: (0, 0))
    return pl.pallas_call(
        body, name=name, grid=(1,),
        in_specs=[blk, blk, pl.BlockSpec((t, LANES), lambda i: (0, OFF_DT // LANES)), row, blk, row,
                  pl.BlockSpec((SSD_HEADS, SSD_P), lambda i: (0, 0))],
        out_specs=[blk, row, row, pl.BlockSpec((SSD_HEADS, 1), lambda i: (0, 0))],
        out_shape=[jax.ShapeDtypeStruct((t, LANES), BF16), jax.ShapeDtypeStruct((1, LANES), F32),
                   jax.ShapeDtypeStruct((1, LANES), F32), jax.ShapeDtypeStruct((SSD_HEADS, 1), F32)],
        compiler_params=_cp("arbitrary"),
    )(ddt128, dadt128, proj, bias128, dt128, a128, dd_h)


def _bdot(a, b, ca, cb, precision=None):
    return lax.dot_general(a, b, (((ca,), (cb,)), ((0,), (0,))), precision=precision, preferred_element_type=F32)


def _head_matrices():
    eye, zero = jnp.eye(SSD_P, dtype=F32), jnp.zeros((SSD_P, SSD_P), F32)
    pick = jnp.stack([jnp.concatenate([eye, zero], axis=0), jnp.concatenate([zero, eye], axis=0)])
    return pick, pick.transpose(0, 2, 1)


def _pick_head(pair_ref, pick_ref, h):
    return jnp.dot(pair_ref[...], pick_ref[h % 2], precision=HI, preferred_element_type=F32)


def _place_head(out_ref, val, place_ref, h):
    wide = jnp.dot(val, place_ref[h % 2], precision=HI, preferred_element_type=F32)

    @pl.when(h % 2 == 0)
    def _():
        out_ref[...] = wide

    @pl.when(h % 2 == 1)
    def _():
        out_ref[...] += wide


def _ssd_common(x2, dt_ref, cs_ref, csr_ref, b_ref, c_ref, nc):
    x = x2.reshape(nc, CHUNK, SSD_P)
    dt = dt_ref[0].reshape(nc, CHUNK, SSD_P)
    cs = cs_ref[0].reshape(nc, CHUNK, SSD_P)
    csr = csr_ref[0]
    bm = b_ref[...].reshape(nc, CHUNK, SSD_N).astype(BF16)
    cm = c_ref[...].reshape(nc, CHUNK, SSD_N).astype(BF16)
    li = lax.broadcasted_iota(jnp.int32, (nc, CHUNK, CHUNK), 1)
    si = lax.broadcasted_iota(jnp.int32, (nc, CHUNK, CHUNK), 2)
    lmat = jnp.exp(jnp.where(si <= li, cs - csr, NEG))
    g = _bdot(cm, bm, 2, 2)
    cs_last = jnp.sum(jnp.where(li == CHUNK - 1, cs, 0.0), axis=1, keepdims=True)
    xdt = x * dt
    dec = jnp.exp(cs_last - cs)
    return x, dt, cs, bm, cm, li, si, lmat, g, cs_last, xdt, dec


def _ssd_fwd(xbc, dt_h, cs_h, cs_row, dskip_h, *, name):
    t = xbc.shape[0]
    nc = t // CHUNK
    hpg = SSD_HEADS // SSD_GROUPS
    pick, place = _head_matrices()

    def body(xs_ref, dt_ref, cs_ref, csr_ref, b_ref, c_ref, dk_ref, pick_ref, place_ref, y_ref, st_ref, sc_ref, cd_ref):
        h = pl.program_id(0)
        x, dt, cs, bm, cm, li, si, lmat, g, cs_last, xdt, dec = _ssd_common(_pick_head(xs_ref, pick_ref, h), dt_ref,
                                                                           cs_ref, csr_ref, b_ref, c_ref, nc)
        yd = _bdot((g * lmat).astype(BF16), xdt.astype(BF16), 2, 1)
        sc_ref[...] = _bdot(bm, (dec * xdt).astype(BF16), 1, 1)
        cd_ref[...] = jnp.exp(cs_last)

        def step(c, s):
            st_ref[0, c] = s
            return s * cd_ref[c] + sc_ref[c]

        lax.fori_loop(0, nc, step, jnp.zeros((SSD_N, SSD_P), F32))
        yo = _bdot(cm, st_ref[0].astype(BF16), 2, 1) * jnp.exp(cs)
        _place_head(y_ref, (yd + yo + dk_ref[0] * x).reshape(t, SSD_P), place_ref, h)

    head = pl.BlockSpec((1, t, SSD_P), lambda h: (h, 0, 0))
    pair = pl.BlockSpec((t, 2 * SSD_P), lambda h: (0, h // 2))
    nxb = D_SSM // SSD_N
    return pl.pallas_call(
        body, name=name, grid=(SSD_HEADS,),
        in_specs=[pair, head, head, pl.BlockSpec((1, nc, 1, CHUNK), lambda h: (h, 0, 0, 0)),
                  pl.BlockSpec((t, SSD_N), lambda h: (0, nxb + h // hpg)),
                  pl.BlockSpec((t, SSD_N), lambda h: (0, nxb + SSD_GROUPS + h // hpg)),
                  pl.BlockSpec((1, 1, SSD_P), lambda h: (h, 0, 0)),
                  pl.BlockSpec((2, 2 * SSD_P, SSD_P), lambda h: (0, 0, 0)),
                  pl.BlockSpec((2, SSD_P, 2 * SSD_P), lambda h: (0, 0, 0))],
        out_specs=[pair, pl.BlockSpec((1, nc, SSD_N, SSD_P), lambda h: (h, 0, 0, 0))],
        out_shape=[jax.ShapeDtypeStruct((t, D_SSM), F32),
                   jax.ShapeDtypeStruct((SSD_HEADS, nc, SSD_N, SSD_P), F32)],
        scratch_shapes=[pltpu.VMEM((nc, SSD_N, SSD_P), F32), pltpu.VMEM((nc, 1, SSD_P), F32)],
        compiler_params=_cp("arbitrary"),
    )(xbc, dt_h, cs_h, cs_row, xbc, xbc, dskip_h, pick, place)


def _ssd_bwd(xbc, dt_h, cs_h, cs_row, dskip_h, a_h, states, dy, *, name):
    t = xbc.shape[0]
    nc = t // CHUNK
    hpg = SSD_HEADS // SSD_GROUPS
    pick, place = _head_matrices()

    def body(xs_ref, dt_ref, cs_ref, csr_ref, b_ref, c_ref, dk_ref, a_ref, st_ref, dy_ref, pick_ref, place_ref,
             dxs_ref, ddt_ref, dadt_ref, db_ref, dc_ref, dd_ref, dsl_ref, dsc_ref, cd_ref):
        h = pl.program_id(0) * hpg + pl.program_id(1)
        x, dt, cs, bm, cm, li, si, lmat, g, cs_last, xdt, dec = _ssd_common(_pick_head(xs_ref, pick_ref, h), dt_ref,
                                                                           cs_ref, csr_ref, b_ref, c_ref, nc)
        dy = _pick_head(dy_ref, pick_ref, h).reshape(nc, CHUNK, SSD_P)
        dyb = dy.astype(BF16)
        xdtb = xdt.astype(BF16)
        sprev = st_ref[0]
        sprevb = sprev.astype(BF16)
        cdec = jnp.exp(cs_last)
        ecs = jnp.exp(cs)
        dw = (ecs * dy).astype(BF16)
        wmat = _bdot(cm, sprevb, 2, 1)
        dcs = jnp.sum(dy * ecs * wmat, axis=2, keepdims=True)
        dcm = _bdot(dw, sprevb, 2, 2)
        dsl_ref[...] = _bdot(cm, dw, 1, 1)
        cd_ref[...] = cdec

        def step(k, ds):
            c = nc - 1 - k
            dsc_ref[c] = ds
            return ds * cd_ref[c] + dsl_ref[c]

        lax.fori_loop(0, nc, step, jnp.zeros((SSD_N, SSD_P), F32))
        dsc = dsc_ref[...]
        dscb = dsc.astype(BF16)
        d_last = jnp.sum(jnp.sum(dsc * sprev, axis=1, keepdims=True) * cdec, axis=2, keepdims=True)
        z = dec * xdt
        dbm = _bdot(z.astype(BF16), dscb, 2, 2)
        dz = _bdot(bm, dscb, 2, 1)
        dxdt = dec * dz
        t2 = jnp.sum(dz * z, axis=2, keepdims=True)
        dcs = dcs - t2
        d_last = d_last + jnp.sum(t2, axis=1, keepdims=True)
        m = g * lmat
        mb = m.astype(BF16)
        dm = _bdot(dyb, xdtb, 2, 2)
        dxdt = dxdt + _bdot(mb, dyb, 1, 1)
        dseg = dm * m
        dcs = dcs + jnp.sum(dseg, axis=2, keepdims=True)
        ones = jnp.ones((nc, CHUNK, SSD_P), F32)
        dcs = dcs - _bdot(dseg, ones, 1, 1, precision=HI)
        dg = (dm * lmat).astype(BF16)
        dcm = dcm + _bdot(dg, bm, 2, 1)
        dbm = dbm + _bdot(dg, cm, 1, 1)
        dcs = dcs + jnp.where(li[:, :, :SSD_P] == CHUNK - 1, d_last, 0.0)
        triu = jnp.where(li <= si, 1.0, 0.0).astype(F32)
        dadt = _bdot(triu, dcs, 2, 1, precision=HI)
        dk = dk_ref[0]
        _place_head(dxs_ref, (dxdt * dt + dk * dy).reshape(t, SSD_P), place_ref, h)
        ddt = jnp.sum(dxdt * x, axis=2, keepdims=True) + dadt * a_ref[0]
        mine = lax.broadcasted_iota(jnp.int32, (t, LANES), 1) == h

        @pl.when(h == 0)
        def _():
            ddt_ref[...] = jnp.zeros_like(ddt_ref)
            dadt_ref[...] = jnp.zeros_like(dadt_ref)

        ddt_ref[...] += jnp.where(mine, jnp.max(ddt, axis=2, keepdims=True).reshape(t, 1), 0.0)
        dadt_ref[...] += jnp.where(mine, jnp.max(dadt, axis=2, keepdims=True).reshape(t, 1), 0.0)
        dd_ref[0] = jnp.sum(jnp.sum(dy * x, axis=1, keepdims=True), axis=0)

        @pl.when(pl.program_id(1) == 0)
        def _():
            db_ref[...] = jnp.zeros_like(db_ref)
            dc_ref[...] = jnp.zeros_like(dc_ref)

        db_ref[...] += dbm.reshape(t, SSD_N)
        dc_ref[...] += dcm.reshape(t, SSD_N)

    head = pl.BlockSpec((1, t, SSD_P), lambda gi, hi: (gi * hpg + hi, 0, 0))
    pair = pl.BlockSpec((t, 2 * SSD_P), lambda gi, hi: (0, (gi * hpg + hi) // 2))
    grp = pl.BlockSpec((t, SSD_N), lambda gi, hi: (0, gi))
    lane = pl.BlockSpec((1, 1, SSD_P), lambda gi, hi: (gi * hpg + hi, 0, 0))
    rows = pl.BlockSpec((t, LANES), lambda gi, hi: (0, 0))
    nxb = D_SSM // SSD_N
    dxs, ddt, dadt, db, dc, dd = pl.pallas_call(
        body, name=name, grid=(SSD_GROUPS, hpg),
        in_specs=[pair, head, head, pl.BlockSpec((1, nc, 1, CHUNK), lambda gi, hi: (gi * hpg + hi, 0, 0, 0)),
                  pl.BlockSpec((t, SSD_N), lambda gi, hi: (0, nxb + gi)),
                  pl.BlockSpec((t, SSD_N), lambda gi, hi: (0, nxb + SSD_GROUPS + gi)), lane, lane,
                  pl.BlockSpec((1, nc, SSD_N, SSD_P), lambda gi, hi: (gi * hpg + hi, 0, 0, 0)), pair,
                  pl.BlockSpec((2, 2 * SSD_P, SSD_P), lambda gi, hi: (0, 0, 0)),
                  pl.BlockSpec((2, SSD_P, 2 * SSD_P), lambda gi, hi: (0, 0, 0))],
        out_specs=[pair, rows, rows, grp, grp, lane],
        out_shape=[jax.ShapeDtypeStruct((t, D_SSM), F32)] + [jax.ShapeDtypeStruct((t, LANES), F32)] * 2
        + [jax.ShapeDtypeStruct((t, SSD_GROUPS * SSD_N), F32)] * 2
        + [jax.ShapeDtypeStruct((SSD_HEADS, 1, SSD_P), F32)],
        scratch_shapes=[pltpu.VMEM((nc, SSD_N, SSD_P), F32), pltpu.VMEM((nc, SSD_N, SSD_P), F32),
                        pltpu.VMEM((nc, 1, SSD_P), F32)],
        compiler_params=_cp("arbitrary", "arbitrary"),
    )(xbc, dt_h, cs_h, cs_row, xbc, xbc, dskip_h, a_h, states, dy, pick, place)
    return jnp.concatenate([dxs, db, dc], axis=1), ddt, dadt, dd


def _ssd_gate_fwd(y, proj, w, *, tr=256, name):
    t = y.shape[0]
    gw = D_SSM // SSD_GROUPS

    def body(y_ref, z_ref, w_ref, o_ref):
        v = y_ref[...] * _silu(z_ref[...])
        for gi in range(SSD_GROUPS):
            vg = v[:, gi * gw:(gi + 1) * gw]
            r = lax.rsqrt(jnp.mean(vg * vg, axis=-1, keepdims=True) + NORM_EPS)
            o_ref[:, gi * gw:(gi + 1) * gw] = (vg * r * w_ref[:, gi * gw:(gi + 1) * gw]).astype(BF16)

    blk = pl.BlockSpec((tr, D_SSM), lambda i: (i, 0))
    return pl.pallas_call(
        body, name=name, grid=(t // tr,), in_specs=[blk, blk, pl.BlockSpec((1, D_SSM), lambda i: (0, 0))],
        out_specs=blk, out_shape=jax.ShapeDtypeStruct((t, D_SSM), BF16), compiler_params=_cp("parallel"),
    )(y, proj, w)


def _ssd_gate_bwd(y, proj, w, dcat, *, tr=256, name):
    t = y.shape[0]
    gw = D_SSM // SSD_GROUPS

    def body(y_ref, z_ref, w_ref, d_ref, dy_ref, dz_ref, dw_ref):
        yv, zv, dv = y_ref[...], z_ref[...], d_ref[...].astype(F32)
        sz = _silu(zv)
        v = yv * sz

        @pl.when(pl.program_id(0) == 0)
        def _():
            dw_ref[...] = jnp.zeros_like(dw_ref)

        for gi in range(SSD_GROUPS):
            sl = slice(gi * gw, (gi + 1) * gw)
            vg, dg = v[:, sl], dv[:, sl]
            r = lax.rsqrt(jnp.mean(vg * vg, axis=-1, keepdims=True) + NORM_EPS)
            vh = vg * r
            gg = dg * w_ref[:, sl]
            dvg = r * (gg - vh * jnp.mean(gg * vh, axis=-1, keepdims=True))
            dy_ref[:, sl] = dvg * sz[:, sl]
            dz_ref[:, sl] = (dvg * yv[:, sl] * _dsilu(zv[:, sl])).astype(BF16)
            dw_ref[:, sl] += jnp.sum(dg * vh, axis=0, keepdims=True)

    blk = pl.BlockSpec((tr, D_SSM), lambda i: (i, 0))
    row = pl.BlockSpec((1, D_SSM), lambda i: (0, 0))
    return pl.pallas_call(
        body, name=name, grid=(t // tr,), in_specs=[blk, blk, row, blk], out_specs=[blk, blk, row],
        out_shape=[jax.ShapeDtypeStruct((t, D_SSM), F32), jax.ShapeDtypeStruct((t, D_SSM), BF16),
                   jax.ShapeDtypeStruct((1, D_SSM), F32)],
        compiler_params=_cp("arbitrary"),
    )(y, proj, w, dcat)


def _pad_lanes(v):
    return jnp.pad(v, ((0, 0), (0, LANES - v.shape[1])))


def _per_head(v128, t):
    return jnp.broadcast_to(v128[:, :SSD_HEADS].T[:, :, None], (SSD_HEADS, t, SSD_P))


def _ssd_forward(proj, conv_w, conv_b, dt_bias, a_log, d_skip, ssd_norm_w):
    t = proj.shape[0]
    nc = t // CHUNK
    xbc = _conv_act_fwd(proj, conv_w, conv_b, kw=SSD_CONV, glu=False, tc=512, coff=OFF_XBC // 512,
                        ncols=SSD_CONV_DIM, out_dtype=F32, name="ssd_conv_fwd")
    bias128, alog128 = _pad_lanes(dt_bias), _pad_lanes(a_log)
    dt128, cs128, a128 = _ssd_prep(proj, bias128, alog128, name="ssd_prep")
    dt_h, cs_h = _per_head(dt128, t), _per_head(cs128, t)
    cs_row = cs128[:, :SSD_HEADS].T.reshape(SSD_HEADS, nc, 1, CHUNK)
    dskip_h = jnp.broadcast_to(d_skip[0][:, None, None], (SSD_HEADS, 1, SSD_P))
    a_h = jnp.broadcast_to(a128[0, :SSD_HEADS][:, None, None], (SSD_HEADS, 1, SSD_P))
    y, states = _ssd_fwd(xbc, dt_h, cs_h, cs_row, dskip_h, name="ssd_scan_fwd")
    y_ssd = _ssd_gate_fwd(y, proj, ssd_norm_w, name="ssd_gate_fwd")
    saved = (proj, conv_w, conv_b, ssd_norm_w, bias128, dt128, a128, dt_h, cs_h, cs_row, xbc, dskip_h, a_h, states, y)
    return y_ssd, saved


def _ssd_backward(saved, dcat):
    proj, conv_w, conv_b, ssd_norm_w, bias128, dt128, a128, dt_h, cs_h, cs_row, xbc, dskip_h, a_h, states, y = saved
    dy, dz, d_norm_w = _ssd_gate_bwd(y, proj, ssd_norm_w, dcat, name="ssd_gate_bwd")
    dxc, ddt128, dadt128, dd_h = _ssd_bwd(xbc, dt_h, cs_h, cs_row, dskip_h, a_h, states, dy, name="ssd_scan_bwd")
    dxbc, d_conv_w, d_conv_b = _conv_act_bwd(proj, conv_w, conv_b, dxc, kw=SSD_CONV, glu=False, tc=512,
                                             coff=OFF_XBC // 512, ncols=SSD_CONV_DIM, name="ssd_conv_bwd")
    d_raw, d_bias, d_alog, d_dskip = _ssd_prep_bwd(ddt128, dadt128, proj, bias128, dt128, a128,
                                                   dd_h.reshape(SSD_HEADS, SSD_P), name="ssd_prep_bwd")
    return (dz, dxbc, d_raw, d_norm_w, d_conv_w, d_conv_b, d_bias[:, :SSD_HEADS], d_alog[:, :SSD_HEADS],
            d_dskip.reshape(1, SSD_HEADS))


def _rope_tables(positions):
    inv_freq = ROPE_THETA ** (-jnp.arange(0, MLA_ROPE, 2, dtype=F32) / MLA_ROPE)
    ang = positions[0].astype(F32)[:, None] * inv_freq
    cos, sin = jnp.cos(ang), jnp.sin(ang)
    z = jnp.zeros_like(cos)
    return jnp.stack([jnp.concatenate([cos, cos, z, z], axis=1), jnp.concatenate([-sin, z, z, z], axis=1),
                      jnp.concatenate([z, sin, z, z], axis=1)])


def _mla_forward(proj, tabs, q_a_norm_w, wq_pad, kv_a_norm_w, wkv):
    qn = _rmsnorm_fwd(proj, q_a_norm_w, width=MLA_Q_RANK, cblk=OFF_QA // MLA_Q_RANK, name="q_a_norm")
    q = _matmul(qn, wq_pad, name="q_b_proj")
    kvn = _rmsnorm_fwd(proj, kv_a_norm_w, width=MLA_KV_RANK, cblk=OFF_CKV // MLA_KV_RANK, name="kv_a_norm")
    kv = _matmul(kvn, wkv, name="kv_b_proj")
    q3, k3, v3 = _mla_prep(q, kv, proj, tabs, name="mla_prep")
    o, lse = _attn_fwd(q3, k3, v3, name="attn_fwd")
    return o, (proj, tabs, q_a_norm_w, wq_pad, kv_a_norm_w, wkv, qn, kvn, q3, k3, v3, o, lse)


def _mla_backward(saved, dcat):
    proj, tabs, q_a_norm_w, wq_pad, kv_a_norm_w, wkv, qn, kvn, q3, k3, v3, o, lse = saved
    dq3, dk3, dv3 = _attn_bwd(q3, k3, v3, o, dcat, lse, name="attn_bwd")
    dq, dkv, dkr = _mla_unprep(dq3, dk3, dv3, tabs, name="mla_unprep")
    d_wq = _matmul(qn, dq, ta=True, out_dtype=BF16, name="d_w_q_b")
    dqn = _matmul(dq, wq_pad, tb=True, name="d_qn")
    dq_a, d_qnw = _rmsnorm_bwd(proj, q_a_norm_w, dqn, width=MLA_Q_RANK, cblk=OFF_QA // MLA_Q_RANK, out_dtype=BF16,
                               name="q_a_norm_bwd")
    d_wkv = _matmul(kvn, dkv, ta=True, out_dtype=BF16, name="d_w_kv_b")
    dkvn = _matmul(dkv, wkv, tb=True, name="d_kvn")
    dckv, d_kvnw = _rmsnorm_bwd(proj, kv_a_norm_w, dkvn, width=MLA_KV_RANK, cblk=OFF_CKV // MLA_KV_RANK,
                                out_dtype=BF16, name="kv_a_norm_bwd")
    return dq_a, dckv, dkr, d_wq, d_wkv, d_qnw, d_kvnw


def _pad_w_q(w):
    r = w.shape[0]
    w3 = w.reshape(r, MLA_HEADS, MLA_NOPE + MLA_ROPE)
    return jnp.pad(w3, ((0, 0), (0, 0), (0, MLA_QK_PAD - MLA_NOPE - MLA_ROPE))).reshape(r, MLA_HEADS * MLA_QK_PAD)


def _unpad_w_q(w):
    r = w.shape[0]
    return w.reshape(r, MLA_HEADS, MLA_QK_PAD)[:, :, :MLA_NOPE + MLA_ROPE].reshape(r, MLA_HEADS * (MLA_NOPE + MLA_ROPE))


def _pad_w_in(w):
    r = w.shape[0]
    o_dt = D_SSM + SSD_CONV_DIM
    o_qa = o_dt + SSD_HEADS
    o_kr = o_qa + MLA_Q_RANK + MLA_KV_RANK
    zeros = lambda n: jnp.zeros((r, n), w.dtype)
    return jnp.concatenate([w[:, :o_dt], w[:, o_qa:o_kr], w[:, o_kr:], zeros(LANES - MLA_ROPE),
                            w[:, o_dt:o_qa], zeros(LANES - SSD_HEADS)], axis=1)


def _unpad_w_in(w):
    return jnp.concatenate([w[:, :OFF_QA], w[:, OFF_DT:OFF_DT + SSD_HEADS], w[:, OFF_QA:OFF_KR + MLA_ROPE]], axis=1)


W_IN_SEGMENTS = ((0, D_SSM + SSD_CONV_DIM, 0), (D_SSM + SSD_CONV_DIM, D_SSM + SSD_CONV_DIM + SSD_HEADS, OFF_DT),
                 (D_SSM + SSD_CONV_DIM + SSD_HEADS, D_IN - MLA_ROPE, OFF_QA), (D_IN - MLA_ROPE, D_IN, OFF_KR))


def _pad_w_in_shards(g):
    n = g.shape[2]
    pieces, at = [], 0
    for lo, hi, start in sorted(W_IN_SEGMENTS, key=lambda seg: seg[2]):
        if start > at:
            pieces.append(jnp.zeros((g.shape[1], start - at), g.dtype))
        for j in range(N_DEV):
            a, b = max(lo, j * n), min(hi, (j + 1) * n)
            if a < b:
                pieces.append(g[j][:, a - j * n:b - j * n])
        at = start + hi - lo
    pieces.append(jnp.zeros((g.shape[1], D_IN_PAD - at), g.dtype))
    return jnp.concatenate(pieces, axis=1)


def _unpad_w_in_shards(w):
    n = D_IN // N_DEV
    shards = []
    for j in range(N_DEV):
        pieces = []
        for lo, hi, start in W_IN_SEGMENTS:
            a, b = max(lo, j * n), min(hi, (j + 1) * n)
            if a < b:
                pieces.append(w[:, start + a - lo:start + b - lo])
        shards.append(jnp.concatenate(pieces, axis=1) if len(pieces) > 1 else pieces[0])
    return jnp.stack(shards)


WEIGHTS = ['mix_norm_w', 'w_in', 'conv_w', 'conv_b', 'dt_bias', 'a_log', 'd_skip', 'ssd_norm_w', 'q_a_norm_w', 'w_q_b',
           'kv_a_norm_w', 'w_kv_b', 'w_out', 'ffn_norm_w', 'w_ffn_up', 'ffn_conv_w', 'ffn_conv_b', 'w_ffn_down',
           'ple_norm_w', 'w_ple_gate', 'b_ple_gate', 'w_ple_proj', 'ple_post_norm_w', 'final_norm_w']
BIG = ['w_in', 'w_q_b', 'w_kv_b', 'w_out', 'w_ffn_up', 'w_ffn_down', 'w_ple_gate', 'w_ple_proj']
COL_SHARDED = ('w_in', 'w_q_b', 'w_kv_b', 'w_ffn_up', 'w_ple_proj')
CONV = ['conv_w', 'ffn_conv_w']
REPL = [n for n in WEIGHTS if n not in BIG and n not in CONV]
FFN_INV = tuple(int(i) for i in np.argsort(FFN_PERM))


def _cat_cols(g):
    return jnp.concatenate([g[j] for j in range(N_DEV)], axis=1)


def _split_cols(w):
    n = w.shape[1] // N_DEV
    return jnp.stack([w[:, j * n:(j + 1) * n] for j in range(N_DEV)])


def _interleave(v):
    r = v.shape[0]
    return v.reshape(r, N_DEV, FFN_TC)[:, jnp.array(FFN_PERM)].reshape(r, N_DEV * FFN_TC)


def _deinterleave(v):
    r = v.shape[0]
    return v.reshape(r, N_DEV, FFN_TC)[:, jnp.array(FFN_INV)].reshape(r, N_DEV * FFN_TC)


def _assemble_weights(g):
    layout = {
        'w_in': _pad_w_in_shards,
        'w_q_b': lambda v: _pad_w_q(_cat_cols(v)),
        'w_kv_b': _cat_cols,
        'w_out': lambda v: v.reshape(D_MODEL, D_MODEL),
        'w_ffn_up': lambda v: v,
        'w_ffn_down': lambda v: v.reshape(D_FF, D_MODEL),
        'w_ple_gate': lambda v: v.reshape(D_MODEL, D_MODEL),
        'w_ple_proj': _cat_cols,
        'conv_w': _cat_cols,
        'ffn_conv_w': lambda v: _interleave(_cat_cols(v)),
    }
    return {n: layout[n](v) for n, v in g.items()}


WEIGHT_GROUPS = {'a': ['w_in', 'w_q_b', 'w_kv_b', 'conv_w'], 'b': ['w_out'],
                 'c': ['w_ffn_up', 'ffn_conv_w', 'w_ffn_down', 'w_ple_gate', 'w_ple_proj']}
GRAD_GROUPS = {'p': ['w_ple_proj', 'w_ple_gate', 'w_ffn_down'], 'r': ['w_ffn_up'], 's': ['w_out'],
               't': ['w_q_b', 'w_kv_b', 'w_in']}


def _ffn_perm(j):
    return (j % 2) * (N_DEV // 2) + j // 2


def _local_step(x, p, tabs, get_w, s, target, emit, relay, settle):
    t = x.shape[0]
    s = dict(s)
    half = D_MODEL // 2
    up_cols = 2 * D_FF
    ffn_conv_b = _interleave(s['ffn_conv_b'])
    w = dict(get_w('a', None))
    h = _rmsnorm_fwd(x, s['mix_norm_w'], width=D_MODEL, name="mix_norm")
    proj = _matmul(h, w['w_in'], name="in_proj")
    y_ssd, ssd_saved = _ssd_forward(proj, w['conv_w'], s['conv_b'], s['dt_bias'], s['a_log'], s['d_skip'],
                                    s['ssd_norm_w'])
    o, mla_saved = _mla_forward(proj, tabs, s['q_a_norm_w'], w['w_q_b'], s['kv_a_norm_w'], w['w_kv_b'])
    tk_o, tn_o = _tile(half, MM_TK), _tile(D_MODEL, MM_TILE)
    w.update(get_w('b', o))
    x1 = _matmul(y_ssd, w['w_out'], add=x, mnk=(t, D_MODEL, half), name="out_proj_ssd")
    x1 = _matmul(o, w['w_out'], add=x1, mnk=(t, D_MODEL, half), name="out_proj_mla",
                 b_spec=pl.BlockSpec((tk_o, tn_o), lambda i, j, kk: (kk + half // tk_o, j)))
    hf = _rmsnorm_fwd(x1, s['ffn_norm_w'], width=D_MODEL, name="ffn_norm")
    w.update(get_w('c', hf))
    tk_u = _tile(D_MODEL, MM_TK)
    u = _matmul(hf, w['w_ffn_up'], mnk=(t, up_cols, D_MODEL), tn=FFN_TC, name="ffn_up",
                b_spec=pl.BlockSpec((1, tk_u, FFN_TC), lambda i, j, kk: (_ffn_perm(j), kk, 0)))
    act = _conv_act_fwd(u, w['ffn_conv_w'], ffn_conv_b, kw=FFN_CONV, glu=True, tc=2 * FFN_TC, coff=0, ncols=up_cols,
                        out_dtype=BF16, name="ffn_act")
    x2 = _matmul(act, w['w_ffn_down'], add=x1, name="ffn_down")
    hp = _rmsnorm_fwd(x2, s['ple_norm_w'], width=D_MODEL, name="ple_norm")
    gl = _matmul(hp, w['w_ple_gate'], bias=s['b_ple_gate'], name="ple_gate")
    pe = _matmul(p, w['w_ple_proj'], name="ple_proj")
    x3 = _ple_fwd(x2, gl, pe, s['ple_post_norm_w'], name="ple_mix")
    loss, dx3, d_final = _loss_head(x3, s['final_norm_w'], target, name="loss_head")
    dgl, d_bgate, dpe, d_post = _ple_bwd(dx3, gl, pe, s['ple_post_norm_w'], name="ple_mix_bwd")
    d_wproj = _matmul(p, dpe, ta=True, out_dtype=BF16, name="d_w_ple_proj")
    d_wgate = _matmul(hp, dgl, ta=True, out_dtype=BF16, name="d_w_ple_gate")
    dhp = _matmul(dgl, w['w_ple_gate'], tb=True, name="d_ple_normed")
    dx2, d_plenorm = _rmsnorm_bwd(x2, s['ple_norm_w'], dhp, dx3, width=D_MODEL, name="ple_norm_bwd")
    dact = _matmul(dx2, w['w_ffn_down'], tb=True, name="d_ffn_act")
    d_wdown = _matmul(act, dx2, ta=True, out_dtype=BF16, name="d_w_ffn_down")
    zz = emit('p', {'w_ple_proj': _split_cols(d_wproj), 'w_ple_gate': d_wgate.reshape(N_DEV, D_MODEL // N_DEV, D_MODEL),
                    'w_ffn_down': d_wdown.reshape(N_DEV, D_FF // N_DEV, D_MODEL)})
    du, d_fconv_w, d_fconv_b = _conv_act_bwd(u, w['ffn_conv_w'], ffn_conv_b + zz, dact, kw=FFN_CONV, glu=True,
                                             tc=2 * FFN_TC, coff=0, ncols=up_cols, name="ffn_act_bwd")
    zz = zz + relay('p', du)
    tm_u = _tile(D_MODEL, MM_TILE)
    d_wup = _matmul(hf, du, ta=True, out_dtype=BF16, mnk=(D_MODEL, up_cols, t), tn=FFN_TC, name="d_w_ffn_up",
                    o_spec=pl.BlockSpec((1, tm_u, FFN_TC), lambda i, j, kk: (_ffn_perm(j), i, 0)),
                    o_shape=(N_DEV, D_MODEL, FFN_TC))
    zz = zz + emit('r', {'w_ffn_up': d_wup})
    zero_row = jnp.zeros((1, D_MODEL), F32)
    dhf = _matmul(du, w['w_ffn_up'], tb=True, mnk=(t, D_MODEL, up_cols), tk=FFN_TC, name="d_ffn_normed",
                  bias=zero_row + zz,
                  b_spec=pl.BlockSpec((1, tn_o, FFN_TC), lambda i, j, kk: (_ffn_perm(kk), j, 0)))
    zz = zz + relay('r', dhf) + settle('p')
    dx1, d_ffnnorm = _rmsnorm_bwd(x1, s['ffn_norm_w'] + zz, dhf, dx2, width=D_MODEL, name="ffn_norm_bwd")
    dcat = _matmul(dx1, w['w_out'], tb=True, name="d_mixed")
    d_wout = jnp.concatenate([_matmul(y_ssd, dx1, ta=True, out_dtype=BF16, name="d_w_out_ssd"),
                              _matmul(o, dx1, ta=True, out_dtype=BF16, name="d_w_out_mla")], axis=0)
    zz = zz + emit('s', {'w_out': d_wout.reshape(N_DEV, D_MODEL // N_DEV, D_MODEL)})
    ssd_saved = ssd_saved[:3] + (ssd_saved[3] + zz,) + ssd_saved[4:]
    dz, dxbc, d_raw, d_ssdnorm, d_conv_w, d_conv_b, d_dtb, d_alog, d_dskip = _ssd_backward(ssd_saved, dcat)
    zz = zz + relay('s', dz)
    mla_saved = mla_saved[:-1] + (mla_saved[-1] + zz,)
    dq_a, dckv, dkr, d_wq, d_wkv, d_qnorm, d_kvnorm = _mla_backward(mla_saved, dcat)
    d_raw = (d_raw + settle('r')).astype(BF16)
    dproj = jnp.concatenate([dz, dxbc, dq_a, dckv, dkr, d_raw], axis=1)
    d_win = _matmul(h, dproj, ta=True, out_dtype=BF16, name="d_w_in")
    zz = emit('t', {'w_in': _unpad_w_in_shards(d_win), 'w_q_b': _split_cols(_unpad_w_q(d_wq)),
                    'w_kv_b': _split_cols(d_wkv)})
    dh = _matmul(dproj, w['w_in'], tb=True, bias=zero_row + zz, name="d_in_normed")
    zz = relay('t', dh) + settle('s')
    dx, d_mixnorm = _rmsnorm_bwd(x, s['mix_norm_w'] + zz, dh, dx1, width=D_MODEL, name="mix_norm_bwd")
    conv = {'conv_w': d_conv_w, 'ffn_conv_w': _deinterleave(d_fconv_w)}
    vec = {
        'mix_norm_w': d_mixnorm, 'conv_b': d_conv_b, 'dt_bias': d_dtb, 'a_log': d_alog, 'd_skip': d_dskip,
        'ssd_norm_w': d_ssdnorm, 'q_a_norm_w': d_qnorm, 'kv_a_norm_w': d_kvnorm, 'ffn_norm_w': d_ffnnorm,
        'ffn_conv_b': _deinterleave(d_fconv_b), 'ple_norm_w': d_plenorm, 'b_ple_gate': d_bgate,
        'ple_post_norm_w': d_post, 'final_norm_w': d_final,
    }
    return loss, dx, conv, vec


MESH = pl.DeviceIdType.MESH
FLIPS = ((0, 0, 1), (1, 0, 0), (0, 1, 0), (1, 1, 0), (1, 0, 1), (0, 1, 1), (1, 1, 1))


def _exchange(items, *, gather, name):
    n = len(items)

    def body(*refs):
        ins, outs = refs[:n], refs[n:2 * n]
        send_sems, recv_sems, local_sems = refs[2 * n:]
        x, y, c = lax.axis_index("x"), lax.axis_index("y"), lax.axis_index("c")
        me = 4 * x + 2 * y + c
        peers = [(jnp.where(fx, 1 - x, x), jnp.where(fy, 1 - y, y), jnp.where(fc, 1 - c, c)) for fx, fy, fc in FLIPS]
        slot = [4 * px + 2 * py + pc for px, py, pc in peers]
        local, sends = [], []
        for wi in range(n):
            cp = pltpu.make_async_copy(ins[wi] if gather else ins[wi].at[me], outs[wi].at[me], local_sems.at[wi])
            cp.start()
            local.append(cp)
            for k, peer in enumerate(peers):
                cp = pltpu.make_async_remote_copy(
                    src_ref=ins[wi] if gather else ins[wi].at[slot[k]], dst_ref=outs[wi].at[me],
                    send_sem=send_sems.at[k, wi], recv_sem=recv_sems.at[k, wi], device_id=peer, device_id_type=MESH)
                cp.start()
                sends.append(cp)
        for wi in range(n):
            for k, peer in enumerate(peers):
                pltpu.make_async_remote_copy(
                    src_ref=outs[wi].at[slot[k]], dst_ref=outs[wi].at[slot[k]], send_sem=send_sems.at[k, wi],
                    recv_sem=recv_sems.at[k, wi], device_id=peer, device_id_type=MESH).wait_recv()
        for cp in sends:
            cp.wait_send()
        for cp in local:
            cp.wait()

    hbm = pl.BlockSpec(memory_space=pltpu.HBM)
    out_shape = [jax.ShapeDtypeStruct(((N_DEV,) + v.shape) if gather else v.shape, v.dtype) for v in items]
    return pl.pallas_call(
        body, name=name, in_specs=[hbm] * n, out_specs=[hbm] * n, out_shape=out_shape,
        scratch_shapes=[pltpu.SemaphoreType.DMA((len(FLIPS), n)), pltpu.SemaphoreType.DMA((len(FLIPS), n)),
                        pltpu.SemaphoreType.DMA((n,))],
    )(*items)


HBM_SPEC = pl.BlockSpec(memory_space=pltpu.HBM)
SEM_SPEC = pl.BlockSpec(memory_space=pltpu.SEMAPHORE)
EFFECT = pltpu.SideEffectType.DATAFLOW_SIDE_EFFECTING


def _peers():
    x, y, c = lax.axis_index("x"), lax.axis_index("y"), lax.axis_index("c")
    peers = [(jnp.where(fx, 1 - x, x), jnp.where(fy, 1 - y, y), jnp.where(fc, 1 - c, c)) for fx, fy, fc in FLIPS]
    return 4 * x + 2 * y + c, peers, [4 * px + 2 * py + pc for px, py, pc in peers]


def _split_start(bufs, ncopies, plan, *, name):
    nb = len(bufs)

    def body(*refs):
        send_sems, recv_sems, token = refs[nb], refs[nb + 1], refs[2 * nb + 2]
        for i, (src, dst, peer, _) in enumerate(plan(refs[:nb])):
            pltpu.make_async_remote_copy(src_ref=src, dst_ref=dst, send_sem=send_sems.at[i], recv_sem=recv_sems.at[i],
                                         device_id=peer, device_id_type=MESH).start()
        token[...] = jnp.zeros_like(token)

    res = pl.pallas_call(
        body, name=name, in_specs=[HBM_SPEC] * nb,
        out_specs=[SEM_SPEC, SEM_SPEC] + [HBM_SPEC] * nb + [pl.BlockSpec(memory_space=pltpu.VMEM)],
        out_shape=[pltpu.SemaphoreType.DMA((ncopies,)), pltpu.SemaphoreType.DMA((ncopies,))]
        + [pltpu.HBM(v.shape, v.dtype) for v in bufs] + [jax.ShapeDtypeStruct((HALO, LANES), F32)],
        input_output_aliases={i: 2 + i for i in range(nb)},
        compiler_params=pltpu.CompilerParams(has_side_effects=EFFECT),
    )(*[pltpu.with_memory_space_constraint(v, pltpu.HBM) for v in bufs])
    return (res[0], res[1], list(res[2:2 + nb])), res[2 + nb]


def _split_wait(started, after, plan, local_plan, *, name):
    send_sems, recv_sems, bufs = started
    nb = len(bufs)
    nlocal = len(local_plan(bufs))

    def body(*refs):
        send_sems, recv_sems = refs[nb], refs[nb + 1]
        local_sems = refs[2 * nb + 3]
        local = []
        for j, (src, dst) in enumerate(local_plan(refs[:nb])):
            cp = pltpu.make_async_copy(src, dst, local_sems.at[j])
            cp.start()
            local.append(cp)
        for i, (src, _, peer, incoming) in enumerate(plan(refs[:nb])):
            cp = pltpu.make_async_remote_copy(src_ref=src, dst_ref=incoming, send_sem=send_sems.at[i],
                                              recv_sem=recv_sems.at[i], device_id=peer, device_id_type=MESH)
            cp.wait_send()
            cp.wait_recv()
        for cp in local:
            cp.wait()

    res = pl.pallas_call(
        body, name=name, in_specs=[HBM_SPEC] * nb + [SEM_SPEC, SEM_SPEC, pl.BlockSpec(memory_space=pl.ANY)],
        out_specs=[HBM_SPEC] * nb, out_shape=[pltpu.HBM(v.shape, v.dtype) for v in bufs],
        input_output_aliases={i: i for i in range(nb)},
        scratch_shapes=[pltpu.SemaphoreType.DMA((max(nlocal, 1),))],
        compiler_params=pltpu.CompilerParams(has_side_effects=EFFECT),
    )(*bufs, send_sems, recv_sems, after)
    return list(res)


def _place():
    x, y, c = lax.axis_index("x"), lax.axis_index("y"), lax.axis_index("c")
    others = [((1 - x, y, c), 2 * (1 - x) + y), ((x, 1 - y, c), 2 * x + 1 - y), ((1 - x, 1 - y, c), 2 * (1 - x) + 1 - y)]
    return 4 * x + 2 * y + c, 2 * x + y, c, (x, y, 1 - c), others


def _gather1_plan(n):
    def plan(refs):
        me, _, _, sibling, others = _place()
        out = []
        for wi in range(n):
            item, land = refs[wi], refs[n + wi]
            out.append((item, land.at[me], sibling, land.at[me + 1 - 2 * lax.axis_index("c")]))
            for peer, chip in others:
                out.append((item, land.at[me], peer, land.at[2 * chip + lax.axis_index("c")]))
        return out

    return plan


def _gather1_local(n):
    def plan(refs):
        me = _place()[0]
        return [(refs[wi], refs[n + wi].at[me]) for wi in range(n)]

    return plan


def _gather2_plan(n):
    def plan(refs):
        _, _, c, sibling, others = _place()
        out = []
        for wi in range(n):
            land = refs[wi]
            for _, chip in others:
                out.append((land.at[2 * chip + c], land.at[2 * chip + c], sibling, land.at[2 * chip + 1 - c]))
        return out

    return plan


def _gather_start(items, *, name):
    lands = [lax.empty((N_DEV,) + v.shape, v.dtype) for v in items]
    return _split_start(items + lands, 4 * len(items), _gather1_plan(len(items)), name=name)


def _gather_forward(started, after, *, name):
    n = len(started[2]) // 2
    bufs = _split_wait(started, after, _gather1_plan(n), _gather1_local(n), name=name + "_wait")
    return _split_start(bufs[n:], 3 * n, _gather2_plan(n), name=name + "_start")


def _gather_finish(started, after, *, name):
    n = len(started[2])
    return _split_wait(started, after, _gather2_plan(n), lambda refs: [], name=name)


def _handshake(peers):
    barrier = pltpu.get_barrier_semaphore()
    for peer in peers:
        pl.semaphore_signal(barrier, inc=1, device_id=peer, device_id_type=MESH)
    pl.semaphore_wait(barrier, len(peers))


def _remote(src, dst, send_sem, recv_sem, peer):
    return pltpu.make_async_remote_copy(src_ref=src, dst_ref=dst, send_sem=send_sem, recv_sem=recv_sem, device_id=peer,
                                        device_id_type=MESH)


def _sequencer_gather(items, *, collective_id, name):
    n = len(items)
    srcs = [jax.new_ref(v, memory_space=pltpu.MemorySpace.HBM) for v in items]
    lands = [jax.empty_ref(jax.ShapeDtypeStruct((N_DEV,) + v.shape, v.dtype), memory_space=pltpu.MemorySpace.HBM)
             for v in items]
    dma = pltpu.SemaphoreType.DMA

    @pl.kernel(mesh=plsc.ScalarSubcoreMesh(axis_name="sequencer", num_cores=1), name=name,
               scratch_types=(dma((4 * n,)), dma((4 * n,)), dma((3 * n,)), dma((3 * n,)), dma((n,))),
               compiler_params=pltpu.CompilerParams(collective_id=collective_id))
    def launch(send1, recv1, send2, recv2, local_sems):
        _, _, _, sibling, others = _place()
        _handshake([sibling] + [peer for peer, _ in others])
        hop1 = _gather1_plan(n)(srcs + lands)
        hop2 = _gather2_plan(n)(lands)
        local = [pltpu.make_async_copy(src, dst, local_sems.at[j])
                 for j, (src, dst) in enumerate(_gather1_local(n)(srcs + lands))]
        for cp in local:
            cp.start()
        for i, (src, dst, peer, _) in enumerate(hop1):
            _remote(src, dst, send1.at[i], recv1.at[i], peer).start()
        for wi in range(n):
            for j in range(3):
                i1, i2 = 4 * wi + 1 + j, 3 * wi + j
                src, _, peer, incoming = hop1[i1]
                _remote(src, incoming, send1.at[i1], recv1.at[i1], peer).wait_recv()
                src, dst, peer, _ = hop2[i2]
                _remote(src, dst, send2.at[i2], recv2.at[i2], peer).start()
        for wi in range(n):
            src, _, peer, incoming = hop1[4 * wi]
            _remote(src, incoming, send1.at[4 * wi], recv1.at[4 * wi], peer).wait_recv()
        for i, (src, _, peer, incoming) in enumerate(hop2):
            cp = _remote(src, incoming, send2.at[i], recv2.at[i], peer)
            cp.wait_send()
            cp.wait_recv()
        for i, (src, dst, peer, _) in enumerate(hop1):
            _remote(src, dst, send1.at[i], recv1.at[i], peer).wait_send()
        for cp in local:
            cp.wait()

    launch()
    return [land[...] for land in lands]


def _sequencer_exchange(sources, land_shapes, ncopies, plan, local_plan, peers, *, collective_id, name):
    srcs = [jax.new_ref(v, memory_space=pltpu.MemorySpace.HBM) for v in sources]
    lands = [jax.empty_ref(s, memory_space=pltpu.MemorySpace.HBM) for s in land_shapes]
    nlocal = len(local_plan(srcs + lands))
    dma = pltpu.SemaphoreType.DMA

    @pl.kernel(mesh=plsc.ScalarSubcoreMesh(axis_name="sequencer", num_cores=1), name=name,
               scratch_types=(dma((ncopies,)), dma((ncopies,)), dma((max(nlocal, 1),))),
               compiler_params=pltpu.CompilerParams(collective_id=collective_id))
    def launch(send_sems, recv_sems, local_sems):
        _handshake(peers(_place()))
        copies = plan(srcs + lands)
        local = [pltpu.make_async_copy(src, dst, local_sems.at[j])
                 for j, (src, dst) in enumerate(local_plan(srcs + lands))]
        for cp in local:
            cp.start()
        for i, (src, dst, peer, _) in enumerate(copies):
            _remote(src, dst, send_sems.at[i], recv_sems.at[i], peer).start()
        for i, (src, _, peer, incoming) in enumerate(copies):
            cp = _remote(src, incoming, send_sems.at[i], recv_sems.at[i], peer)
            cp.wait_send()
            cp.wait_recv()
        for cp in local:
            cp.wait()

    launch()
    return [land[...] for land in lands]


def _sequencer_scatter_hop1(parts, *, collective_id, name):
    n = len(parts)
    shapes = [jax.ShapeDtypeStruct((N_CHIP,) + v.shape[1:], v.dtype) for v in parts]
    return _sequencer_exchange(parts, shapes, N_CHIP * n, _scatter1_plan(n), lambda refs: [], lambda place: [place[3]],
                               collective_id=collective_id, name=name)


def _sequencer_scatter_hop2(sums, *, collective_id, name):
    n = len(sums)
    shapes = [jax.ShapeDtypeStruct(v.shape, v.dtype) for v in sums]
    return _sequencer_exchange(sums, shapes, 3 * n, _scatter2_plan(n), _scatter2_local(n),
                               lambda place: [peer for peer, _ in place[4]], collective_id=collective_id, name=name)


N_CHIP = N_DEV // 2


def _scatter1_plan(n):
    def plan(refs):
        _, _, c, sibling, _ = _place()
        out = []
        for wi in range(n):
            parts, half = refs[wi], refs[n + wi]
            for chip in range(N_CHIP):
                out.append((parts.at[2 * chip + 1 - c], half.at[chip], sibling, half.at[chip]))
        return out

    return plan


def _scatter2_plan(n):
    def plan(refs):
        _, my_chip, _, _, others = _place()
        out = []
        for wi in range(n):
            sums, recv = refs[wi], refs[n + wi]
            for peer, chip in others:
                out.append((sums.at[chip], recv.at[my_chip], peer, recv.at[chip]))
        return out

    return plan


def _scatter2_local(n):
    def plan(refs):
        my_chip = _place()[1]
        return [(refs[wi].at[my_chip], refs[n + wi].at[my_chip]) for wi in range(n)]

    return plan


def _pair_add(parts, half, core, *, name):
    _, r, c = parts.shape
    tr = max(d for d in range(HALO, 257, HALO) if r % d == 0) if r > 256 else r
    parts4 = parts.reshape(N_CHIP, 2, r, c)

    def body(core_ref, p_ref, h_ref, o_ref):
        o_ref[...] = (p_ref[:, 0].astype(F32) + h_ref[...].astype(F32)).astype(o_ref.dtype)

    return pl.pallas_call(
        body, name=name,
        grid_spec=pltpu.PrefetchScalarGridSpec(
            num_scalar_prefetch=1, grid=(r // tr,),
            in_specs=[pl.BlockSpec((N_CHIP, 1, tr, c), lambda i, core_ref: (0, core_ref[0], i, 0)),
                      pl.BlockSpec((N_CHIP, tr, c), lambda i, core_ref: (0, i, 0))],
            out_specs=pl.BlockSpec((N_CHIP, tr, c), lambda i, core_ref: (0, i, 0))),
        out_shape=jax.ShapeDtypeStruct((N_CHIP, r, c), parts.dtype), compiler_params=_cp("parallel"),
    )(core, parts4, half)


def _scatter_start(parts, *, name):
    halves = [lax.empty((N_CHIP,) + v.shape[1:], v.dtype) for v in parts]
    return _split_start(parts + halves, N_CHIP * len(parts), _scatter1_plan(len(parts)), name=name)


def _scatter_forward(started, after, core, *, name):
    n = len(started[2]) // 2
    bufs = _split_wait(started, after, _scatter1_plan(n), lambda refs: [], name=name + "_wait")
    sums = [_pair_add(bufs[wi], bufs[n + wi], core, name=name + "_add%d" % wi) for wi in range(n)]
    recvs = [lax.empty(v.shape, v.dtype) for v in sums]
    return _split_start(sums + recvs, 3 * n, _scatter2_plan(n), name=name + "_start")


def _scatter_finish(started, after, *, name):
    n = len(started[2]) // 2
    return _split_wait(started, after, _scatter2_plan(n), _scatter2_local(n), name=name)[n:]


def _adamw(parts, w, m, v, *, name):
    r, c = w.shape
    nparts = parts.shape[0]
    tr = max(d for d in range(HALO, 129, HALO) if r % d == 0) if r > 128 else r

    def body(p_ref, w_ref, m_ref, v_ref, g_ref, d_ref, mo_ref, vo_ref):
        g = p_ref[0].astype(F32)
        for k in range(1, nparts):
            g = g + p_ref[k].astype(F32)
        mn = ADAM_B1 * m_ref[...] + (1.0 - ADAM_B1) * g
        vn = ADAM_B2 * v_ref[...] + (1.0 - ADAM_B2) * (g * g)
        m_hat = mn / (1.0 - ADAM_B1 ** ADAM_STEP)
        v_hat = vn / (1.0 - ADAM_B2 ** ADAM_STEP)
        g_ref[...] = g
        d_ref[...] = -ADAM_LR * (m_hat / (jnp.sqrt(v_hat) + ADAM_EPS) + ADAM_WD * w_ref[...])
        mo_ref[...] = mn
        vo_ref[...] = vn

    blk = pl.BlockSpec((tr, c), lambda i: (i, 0))
    return pl.pallas_call(
        body, name=name, grid=(r // tr,), in_specs=[pl.BlockSpec((nparts, tr, c), lambda i: (0, i, 0)), blk, blk, blk],
        out_specs=[blk] * 4, out_shape=[jax.ShapeDtypeStruct((r, c), F32)] * 4, compiler_params=_cp("parallel"),
    )(parts, w, m, v)


def _pack_rows(vs, rows):
    lead = vs[0].shape[:-1] if vs[0].ndim > 1 else ()
    flat = jnp.concatenate(vs, axis=-1)
    pad = rows * LANES - flat.shape[-1]
    flat = jnp.pad(flat, [(0, 0)] * len(lead) + [(0, pad)])
    return flat.reshape(lead + (rows, LANES))


def kernel(x, p, positions, mix_norm_w, w_in, conv_w, conv_b, dt_bias, a_log, d_skip, ssd_norm_w, q_a_norm_w, w_q_b, kv_a_norm_w, w_kv_b, w_out, ffn_norm_w, w_ffn_up, ffn_conv_w, ffn_conv_b, w_ffn_down, ple_norm_w, w_ple_gate, b_ple_gate, w_ple_proj, ple_post_norm_w, final_norm_w, loss_target, m_mix_norm_w, m_w_in, m_conv_w, m_conv_b, m_dt_bias, m_a_log, m_d_skip, m_ssd_norm_w, m_q_a_norm_w, m_w_q_b, m_kv_a_norm_w, m_w_kv_b, m_w_out, m_ffn_norm_w, m_w_ffn_up, m_ffn_conv_w, m_ffn_conv_b, m_w_ffn_down, m_ple_norm_w, m_w_ple_gate, m_b_ple_gate, m_w_ple_proj, m_ple_post_norm_w, m_final_norm_w, v_mix_norm_w, v_w_in, v_conv_w, v_conv_b, v_dt_bias, v_a_log, v_d_skip, v_ssd_norm_w, v_q_a_norm_w, v_w_q_b, v_kv_a_norm_w, v_w_kv_b, v_w_out, v_ffn_norm_w, v_w_ffn_up, v_ffn_conv_w, v_ffn_conv_b, v_w_ffn_down, v_ple_norm_w, v_w_ple_gate, v_b_ple_gate, v_w_ple_proj, v_ple_post_norm_w, v_final_norm_w):
    given = dict(locals())
    shapes = {n: given[n].shape for n in WEIGHTS}
    w2 = {n: given[n].reshape(given[n].shape[-2:] if n in BIG or n in CONV else (1, -1)) for n in WEIGHTS}
    m2 = {n: given['m_' + n].reshape(w2[n].shape) for n in WEIGHTS}
    v2 = {n: given['v_' + n].reshape(w2[n].shape) for n in WEIGHTS}
    me = 4 * lax.axis_index("x") + 2 * lax.axis_index("y") + lax.axis_index("c")

    core = lax.axis_index("c").astype(jnp.int32).reshape(1)

    def shards(grp, zero):
        return [(w2[n] + zero).astype(BF16) if n in BIG else w2[n] + zero for n in WEIGHT_GROUPS[grp]]

    first, token = _gather_start(shards('a', 0.0), name="gather_a_hop1")
    first, token = _gather_forward(first, token, name="gather_a_hop2")
    zero = token[0, 0]
    later = _sequencer_gather(shards('b', zero) + shards('c', zero), collective_id=1, name="gather_later")
    later = dict(zip(WEIGHT_GROUPS['b'] + WEIGHT_GROUPS['c'], later))

    def get_w(grp, after):
        if grp == 'a':
            lands = dict(zip(WEIGHT_GROUPS[grp], _gather_finish(first, token, name="gather_a_done")))
        else:
            lands = {n: later[n] for n in WEIGHT_GROUPS[grp]}
        return _assemble_weights(lands)

    scatters = {}

    hop_ids = {grp: 2 + 2 * i for i, grp in enumerate(GRAD_GROUPS)}

    def zero_of(arrays):
        return sum(v[(0,) * v.ndim].astype(F32) * 0.0 for v in arrays)

    def emit(grp, grads):
        scatters[grp], tok = _scatter_start([grads[n] for n in GRAD_GROUPS[grp]], name="scatter_" + grp + "_hop1")
        return tok[0, 0]

    def relay(grp, after):
        n = len(GRAD_GROUPS[grp])
        bufs = _split_wait(scatters[grp], after, _scatter1_plan(n), lambda refs: [], name="scatter_" + grp + "_hop1_wait")
        sums = [_pair_add(bufs[i], bufs[n + i], core, name="scatter_%s_add%d" % (grp, i)) for i in range(n)]
        scatters[grp] = _sequencer_scatter_hop2(sums, collective_id=hop_ids[grp] + 1, name="scatter_" + grp + "_hop2")
        return zero_of(sums)

    out_g, out_d, out_m, out_v = {}, {}, {}, {}

    def settle(grp, behind=None):
        for n, parts in zip(GRAD_GROUPS[grp], scatters[grp]):
            wn = w2[n] if behind is None else w2[n] + behind
            out_g[n], out_d[n], out_m[n], out_v[n] = _adamw(parts, wn, m2[n], v2[n], name="adamw_" + n)
        return zero_of([out_g[n] for n in GRAD_GROUPS[grp]])

    vecs = {n: w2[n] for n in REPL}
    vecs['mix_norm_w'] = vecs['mix_norm_w'] + zero
    loss, dx, g_conv, g_vec = _local_step(x[0], p[0, 0], _rope_tables(positions), get_w, vecs, loss_target[0], emit,
                                          relay, settle)
    n_small = sum(g_vec[n].shape[1] for n in REPL) + sum(g_conv[n].size for n in CONV) + 1
    rows_small = -(-n_small // (LANES * HALO)) * HALO
    small = _pack_rows([g_vec[n] for n in REPL] + [g_conv[n].reshape(1, -1) for n in CONV] + [loss], rows_small)

    all_small = _exchange([small], gather=True, name="gather_small_grads")[0].reshape(N_DEV, rows_small * LANES)
    settle('t', zero_of([all_small]))
    pieces, off = [], 0
    for n in REPL:
        k = g_vec[n].shape[1]
        pieces.append(all_small[:, off:off + k])
        off += k
    for n in CONV:
        kw, cols = g_conv[n].shape
        full = all_small[:, off:off + kw * cols].reshape(N_DEV, kw, cols)
        mine = lax.dynamic_slice_in_dim(full, me * (cols // N_DEV), cols // N_DEV, axis=2)
        pieces.append(mine.reshape(N_DEV, kw * (cols // N_DEV)))
        off += kw * cols
    pieces.append(all_small[:, off:off + 1])
    small_names = REPL + CONV
    n_mine = sum(q.shape[1] for q in pieces)
    rows_mine = -(-n_mine // (LANES * HALO)) * HALO
    zero = jnp.zeros((1, 1), F32)
    packed = [_pack_rows([src[n].reshape(1, -1) for n in small_names] + [zero], rows_mine).reshape(rows_mine, LANES)
              for src in (w2, m2, v2)]
    sg, sd, sm, sv = _adamw(_pack_rows(pieces, rows_mine), *packed, name="adamw_small")
    off = 0
    for n in small_names:
        k = w2[n].size
        for dst, src in ((out_g, sg), (out_d, sd), (out_m, sm), (out_v, sv)):
            dst[n] = src.reshape(-1)[off:off + k].reshape(w2[n].shape)
        off += k
    total_loss = sg.reshape(-1)[off]

    outs = [total_loss, dx[None]]
    for res in (out_g, out_d, out_m, out_v):
        outs += [res[n].reshape(shapes[n]) for n in WEIGHTS]
    return tuple(outs)
```

```python
import functools
import math

import numpy as np
import jax
import jax.numpy as jnp
from jax import lax
from jax.experimental import pallas as pl
from jax.experimental.pallas import tpu as pltpu
from jax.experimental.pallas import tpu_sc as plsc

F32 = jnp.float32
BF16 = jnp.bfloat16
HI = lax.Precision.HIGHEST

D_MODEL = 2048
CHUNK = 64
D_SSM = 1024
SSD_P = 64
SSD_HEADS = 16
SSD_GROUPS = 2
SSD_N = 128
SSD_CONV = 4
SSD_CONV_DIM = D_SSM + 2 * SSD_GROUPS * SSD_N
MLA_HEADS = 8
MLA_NOPE = 128
MLA_ROPE = 64
MLA_V = 128
MLA_Q_RANK = 512
MLA_KV_RANK = 256
MLA_QK_PAD = 256
ROPE_THETA = 10000.0
D_FF = 5632
FFN_CONV = 3
PLE_DIM = 256
NORM_EPS = 1e-6
ADAM_LR, ADAM_B1, ADAM_B2, ADAM_EPS, ADAM_WD, ADAM_STEP = 0.001, 0.9, 0.999, 1e-08, 0.01, 10
N_DEV = 8

OFF_Z, OFF_XBC, OFF_QA, OFF_CKV, OFF_KR, OFF_DT, D_IN_PAD = 0, 1024, 2560, 3072, 3328, 3456, 3584
D_IN = 3408
LANES = 128
HALO = 8
VMEM_LIMIT = 56 * 1024 * 1024
FFN_TC = D_FF * 2 // N_DEV
FFN_PERM = (0, 4, 1, 5, 2, 6, 3, 7)
NEG = -1e30


def _cp(*sem):
    return pltpu.CompilerParams(dimension_semantics=tuple(sem), vmem_limit_bytes=VMEM_LIMIT)


def _tile(n, want):
    if n <= want:
        return n
    best = max(d for d in range(LANES, want + 1, LANES) if n % d == 0)
    return best


def _sigmoid(x):
    return 0.5 * (jnp.tanh(0.5 * x) + 1.0)


def _silu(x):
    return x * _sigmoid(x)


def _dsilu(x):
    s = _sigmoid(x)
    return s * (1.0 + x * (1.0 - s))


MM_TILE = 1408
MM_TK = 2816


def _matmul(a, b, *, ta=False, tb=False, out_dtype=F32, add=None, bias=None, tm=MM_TILE, tn=MM_TILE, tk=MM_TK, name,
            mnk=None, a_spec=None, b_spec=None, o_spec=None, o_shape=None):
    if mnk is None:
        m, k = (a.shape[1], a.shape[0]) if ta else a.shape
        n = b.shape[0] if tb else b.shape[1]
        assert k == (b.shape[1] if tb else b.shape[0])
    else:
        m, n, k = mnk
    tm, tn, tk = _tile(m, tm), _tile(n, tn), _tile(k, tk)
    nk = k // tk
    dims = (((0 if ta else 1,), (1 if tb else 0,)), ((), ()))

    def body(*refs):
        a_ref, b_ref = refs[0], refs[1]
        pos = 2
        add_ref = bias_ref = None
        if add is not None:
            add_ref = refs[pos]
            pos += 1
        if bias is not None:
            bias_ref = refs[pos]
            pos += 1
        o_ref = refs[pos]
        kk = pl.program_id(2)
        av = a_ref[...]
        bv = b_ref[...]
        av = av.reshape(av.shape[-2:]).astype(BF16)
        bv = bv.reshape(bv.shape[-2:]).astype(BF16)
        prod = lax.dot_general(av, bv, dims, preferred_element_type=F32)

        def finish(r):
            if bias_ref is not None:
                r = r + bias_ref[...]
            if add_ref is not None:
                r = r + add_ref[...].astype(F32)
            o_ref[...] = r.astype(out_dtype).reshape(o_ref.shape)

        if nk == 1:
            finish(prod)
        else:
            acc_ref = refs[pos + 1]

            @pl.when(kk == 0)
            def _():
                acc_ref[...] = prod

            @pl.when(kk > 0)
            def _():
                acc_ref[...] += prod

            @pl.when(kk == nk - 1)
            def _():
                finish(acc_ref[...])

    if a_spec is None:
        a_spec = (pl.BlockSpec((tk, tm), lambda i, j, kk: (kk, i)) if ta
                  else pl.BlockSpec((tm, tk), lambda i, j, kk: (i, kk)))
    if b_spec is None:
        b_spec = (pl.BlockSpec((tn, tk), lambda i, j, kk: (j, kk)) if tb
                  else pl.BlockSpec((tk, tn), lambda i, j, kk: (kk, j)))
    if o_spec is None:
        o_spec = pl.BlockSpec((tm, tn), lambda i, j, kk: (i, j))
    if o_shape is None:
        o_shape = (m, n)
    in_specs = [a_spec, b_spec]
    args = [a, b]
    if add is not None:
        in_specs.append(pl.BlockSpec((tm, tn), lambda i, j, kk: (i, j)))
        args.append(add)
    if bias is not None:
        in_specs.append(pl.BlockSpec((1, tn), lambda i, j, kk: (0, j)))
        args.append(bias)
    return pl.pallas_call(
        body, name=name, grid=(m // tm, n // tn, nk), in_specs=in_specs, out_specs=o_spec,
        out_shape=jax.ShapeDtypeStruct(o_shape, out_dtype),
        scratch_shapes=[pltpu.VMEM((tm, tn), F32)] if nk > 1 else [],
        compiler_params=_cp("parallel", "parallel", "arbitrary"),
    )(*args)


def _rmsnorm_fwd(x, w, *, width, cblk=0, out_dtype=BF16, tr=256, name):
    t = x.shape[0]

    def body(x_ref, w_ref, o_ref):
        xv = x_ref[...].astype(F32)
        r = lax.rsqrt(jnp.mean(xv * xv, axis=-1, keepdims=True) + NORM_EPS)
        o_ref[...] = (xv * r * w_ref[...]).astype(out_dtype)

    return pl.pallas_call(
        body, name=name, grid=(t // tr,),
        in_specs=[pl.BlockSpec((tr, width), lambda i: (i, cblk)), pl.BlockSpec((1, width), lambda i: (0, 0))],
        out_specs=pl.BlockSpec((tr, width), lambda i: (i, 0)),
        out_shape=jax.ShapeDtypeStruct((t, width), out_dtype),
        compiler_params=_cp("parallel"),
    )(x, w)


def _rmsnorm_bwd(x, w, dy, add=None, *, width, cblk=0, out_dtype=F32, tr=256, name):
    t = x.shape[0]

    def body(*refs):
        if add is None:
            x_ref, w_ref, dy_ref, dx_ref, dw_ref = refs
            add_ref = None
        else:
            x_ref, w_ref, dy_ref, add_ref, dx_ref, dw_ref = refs
        xv = x_ref[...].astype(F32)
        dyv = dy_ref[...].astype(F32)
        r = lax.rsqrt(jnp.mean(xv * xv, axis=-1, keepdims=True) + NORM_EPS)
        xh = xv * r
        g = dyv * w_ref[...]
        dx = r * (g - xh * jnp.mean(g * xh, axis=-1, keepdims=True))
        if add_ref is not None:
            dx = dx + add_ref[...].astype(F32)
        dx_ref[...] = dx.astype(out_dtype)

        @pl.when(pl.program_id(0) == 0)
        def _():
            dw_ref[...] = jnp.zeros_like(dw_ref)

        dw_ref[...] += jnp.sum(dyv * xh, axis=0, keepdims=True)

    in_specs = [pl.BlockSpec((tr, width), lambda i: (i, cblk)), pl.BlockSpec((1, width), lambda i: (0, 0)),
                pl.BlockSpec((tr, width), lambda i: (i, 0))]
    args = [x, w, dy]
    if add is not None:
        in_specs.append(pl.BlockSpec((tr, width), lambda i: (i, 0)))
        args.append(add)
    return pl.pallas_call(
        body, name=name, grid=(t // tr,), in_specs=in_specs,
        out_specs=[pl.BlockSpec((tr, width), lambda i: (i, 0)), pl.BlockSpec((1, width), lambda i: (0, 0))],
        out_shape=[jax.ShapeDtypeStruct((t, width), out_dtype), jax.ShapeDtypeStruct((1, width), F32)],
        compiler_params=_cp("arbitrary"),
    )(*args)


def _shift_down(prev_halo, cur, j):
    if j == 0:
        return cur
    ext = jnp.concatenate([prev_halo, cur], axis=0)
    return pltpu.roll(ext, j, axis=0)[HALO:]


def _shift_up(cur, next_halo, j):
    if j == 0:
        return cur
    ext = jnp.concatenate([cur, next_halo], axis=0)
    return pltpu.roll(ext, ext.shape[0] - j, axis=0)[:cur.shape[0]]


def _conv_rows(prev, cur, w, b, kw):
    shifted = [cur]
    out = b + w[kw - 1:kw] * cur
    for j in range(1, kw):
        sh = _shift_down(prev, cur, j)
        shifted.append(sh)
        out = out + w[kw - 1 - j:kw - j] * sh
    return out, shifted


def _act_fwd(c, glu):
    if glu:
        half = c.shape[1] // 2
        return _silu(c[:, :half]) * c[:, half:]
    return _silu(c)


def _act_bwd(c, dout, glu):
    if glu:
        half = c.shape[1] // 2
        g, up = c[:, :half], c[:, half:]
        s = _sigmoid(g)
        gs = g * s
        return jnp.concatenate([dout * up * (s + gs * (1.0 - s)), dout * gs], axis=1)
    return dout * _dsilu(c)


def _conv_act_fwd(u, w, b, *, kw, glu, tc, coff, ncols, out_dtype, tr=256, name):
    t = u.shape[0]
    nb = ncols // tc
    oc = tc // 2 if glu else tc

    def body(u_ref, uh_ref, w_ref, b_ref, o_ref):
        prev = jnp.where(pl.program_id(0) == 0, 0.0, uh_ref[...])
        c, _ = _conv_rows(prev, u_ref[...], w_ref[...], b_ref[...], kw)
        o_ref[...] = _act_fwd(c, glu).astype(out_dtype)

    return pl.pallas_call(
        body, name=name, grid=(t // tr, nb),
        in_specs=[pl.BlockSpec((tr, tc), lambda i, j: (i, j + coff)),
                  pl.BlockSpec((HALO, tc), lambda i, j: (jnp.maximum(i * (tr // HALO) - 1, 0), j + coff)),
                  pl.BlockSpec((kw, tc), lambda i, j: (0, j)), pl.BlockSpec((1, tc), lambda i, j: (0, j))],
        out_specs=pl.BlockSpec((tr, oc), lambda i, j: (i, j)),
        out_shape=jax.ShapeDtypeStruct((t, nb * oc), out_dtype),
        compiler_params=_cp("parallel", "parallel"),
    )(u, u, w, b)


def _conv_act_bwd(u, w, b, dout, *, kw, glu, tc, coff, ncols, tr=256, name):
    t = u.shape[0]
    nb = ncols // tc
    nt = t // tr
    oc = tc // 2 if glu else tc

    def body(u_ref, up_ref, un_ref, d_ref, dn_ref, w_ref, b_ref, du_ref, dw_ref, db_ref):
        i = pl.program_id(1)
        cur, nxt, wv, bv = u_ref[...], un_ref[...], w_ref[...], b_ref[...]
        prev = jnp.where(i == 0, 0.0, up_ref[...])
        c_cur, shifted = _conv_rows(prev, cur, wv, bv, kw)
        c_nxt, _ = _conv_rows(cur[tr - HALO:], nxt, wv, bv, kw)
        d_cur = _act_bwd(c_cur, d_ref[...].astype(F32), glu)
        d_nxt = _act_bwd(c_nxt, jnp.where(i == nt - 1, 0.0, dn_ref[...].astype(F32)), glu)
        du = wv[kw - 1:kw] * d_cur
        for j in range(1, kw):
            du = du + wv[kw - 1 - j:kw - j] * _shift_up(d_cur, d_nxt, j)
        du_ref[...] = du.astype(BF16)

        @pl.when(i == 0)
        def _():
            dw_ref[...] = jnp.zeros_like(dw_ref)
            db_ref[...] = jnp.zeros_like(db_ref)

        db_ref[...] += jnp.sum(d_cur, axis=0, keepdims=True)
        dw_ref[...] += jnp.concatenate(
            [jnp.sum(d_cur * shifted[kw - 1 - k], axis=0, keepdims=True) for k in range(kw)], axis=0)

    nh = tr // HALO
    return pl.pallas_call(
        body, name=name, grid=(nb, nt),
        in_specs=[pl.BlockSpec((tr, tc), lambda j, i: (i, j + coff)),
                  pl.BlockSpec((HALO, tc), lambda j, i: (jnp.maximum(i * nh - 1, 0), j + coff)),
                  pl.BlockSpec((HALO, tc), lambda j, i: (jnp.minimum((i + 1) * nh, t // HALO - 1), j + coff)),
                  pl.BlockSpec((tr, oc), lambda j, i: (i, j)),
                  pl.BlockSpec((HALO, oc), lambda j, i: (jnp.minimum((i + 1) * nh, t // HALO - 1), j)),
                  pl.BlockSpec((kw, tc), lambda j, i: (0, j)), pl.BlockSpec((1, tc), lambda j, i: (0, j))],
        out_specs=[pl.BlockSpec((tr, tc), lambda j, i: (i, j)), pl.BlockSpec((kw, tc), lambda j, i: (0, j)),
                   pl.BlockSpec((1, tc), lambda j, i: (0, j))],
        out_shape=[jax.ShapeDtypeStruct((t, ncols), BF16), jax.ShapeDtypeStruct((kw, ncols), F32),
                   jax.ShapeDtypeStruct((1, ncols), F32)],
        compiler_params=_cp("parallel", "arbitrary"),
    )(u, u, u, dout, dout, w, b)


def _ple_fwd(x2, gl, pe, pw, *, tr=256, name):
    t, d = x2.shape

    def body(x_ref, gl_ref, pe_ref, pw_ref, o_ref):
        pv = pe_ref[...]
        r = lax.rsqrt(jnp.mean(pv * pv, axis=-1, keepdims=True) + NORM_EPS)
        o_ref[...] = x_ref[...] + _sigmoid(gl_ref[...]) * (pv * r * pw_ref[...])

    blk = pl.BlockSpec((tr, d), lambda i: (i, 0))
    return pl.pallas_call(
        body, name=name, grid=(t // tr,), in_specs=[blk, blk, blk, pl.BlockSpec((1, d), lambda i: (0, 0))],
        out_specs=blk, out_shape=jax.ShapeDtypeStruct((t, d), F32), compiler_params=_cp("parallel"),
    )(x2, gl, pe, pw)


def _ple_bwd(dx3, gl, pe, pw, *, tr=256, name):
    t, d = dx3.shape

    def body(dx_ref, gl_ref, pe_ref, pw_ref, dgl_ref, db_ref, dpe_ref, dpw_ref):
        dx, pv, pwv = dx_ref[...], pe_ref[...], pw_ref[...]
        gate = _sigmoid(gl_ref[...])
        r = lax.rsqrt(jnp.mean(pv * pv, axis=-1, keepdims=True) + NORM_EPS)
        ph = pv * r
        dgl = dx * (ph * pwv) * gate * (1.0 - gate)
        de = dx * gate
        g = de * pwv
        dgl_ref[...] = dgl.astype(BF16)
        dpe_ref[...] = (r * (g - ph * jnp.mean(g * ph, axis=-1, keepdims=True))).astype(BF16)

        @pl.when(pl.program_id(0) == 0)
        def _():
            db_ref[...] = jnp.zeros_like(db_ref)
            dpw_ref[...] = jnp.zeros_like(dpw_ref)

        db_ref[...] += jnp.sum(dgl, axis=0, keepdims=True)
        dpw_ref[...] += jnp.sum(de * ph, axis=0, keepdims=True)

    blk = pl.BlockSpec((tr, d), lambda i: (i, 0))
    row = pl.BlockSpec((1, d), lambda i: (0, 0))
    return pl.pallas_call(
        body, name=name, grid=(t // tr,), in_specs=[blk, blk, blk, row], out_specs=[blk, row, blk, row],
        out_shape=[jax.ShapeDtypeStruct((t, d), BF16), jax.ShapeDtypeStruct((1, d), F32),
                   jax.ShapeDtypeStruct((t, d), BF16), jax.ShapeDtypeStruct((1, d), F32)],
        compiler_params=_cp("arbitrary"),
    )(dx3, gl, pe, pw)


def _loss_head(x3, fw, target, *, tr=256, name):
    t, d = x3.shape

    def body(x_ref, w_ref, t_ref, l_ref, dx_ref, dw_ref):
        xv, wv = x_ref[...], w_ref[...]
        r = lax.rsqrt(jnp.mean(xv * xv, axis=-1, keepdims=True) + NORM_EPS)
        xh = xv * r
        err = xh * wv - t_ref[...]
        dy = err * (1.0 / d)
        g = dy * wv
        dx_ref[...] = r * (g - xh * jnp.mean(g * xh, axis=-1, keepdims=True))

        @pl.when(pl.program_id(0) == 0)
        def _():
            l_ref[...] = jnp.zeros_like(l_ref)
            dw_ref[...] = jnp.zeros_like(dw_ref)

        l_ref[...] += 0.5 * jnp.sum(jnp.mean(err * err, axis=-1, keepdims=True), axis=0, keepdims=True)
        dw_ref[...] += jnp.sum(dy * xh, axis=0, keepdims=True)

    blk = pl.BlockSpec((tr, d), lambda i: (i, 0))
    row = pl.BlockSpec((1, d), lambda i: (0, 0))
    return pl.pallas_call(
        body, name=name, grid=(t // tr,), in_specs=[blk, row, blk],
        out_specs=[pl.BlockSpec((1, 1), lambda i: (0, 0)), blk, row],
        out_shape=[jax.ShapeDtypeStruct((1, 1), F32), jax.ShapeDtypeStruct((t, d), F32),
                   jax.ShapeDtypeStruct((1, d), F32)],
        compiler_params=_cp("arbitrary"),
    )(x3, fw, target)


def _rope(blk, tab_ref):
    return blk * tab_ref[0] + pltpu.roll(blk, 96, axis=1) * tab_ref[1] + pltpu.roll(blk, 32, axis=1) * tab_ref[2]


def _unrope(g, tab_ref):
    return g * tab_ref[0] + pltpu.roll(g * tab_ref[1], 32, axis=1) + pltpu.roll(g * tab_ref[2], 96, axis=1)


def _mla_prep(q, kv, proj, tabs, *, tr=512, name):
    t = q.shape[0]

    def body(q_ref, kv_ref, kr_ref, tab_ref, qo_ref, ko_ref, vo_ref):
        qv, kvv = q_ref[...], kv_ref[...]
        qo_ref[0, :, :MLA_NOPE] = qv[:, :MLA_NOPE].astype(BF16)
        qo_ref[0, :, MLA_NOPE:] = _rope(qv[:, MLA_NOPE:], tab_ref).astype(BF16)
        ko_ref[0, :, :MLA_NOPE] = kvv[:, :MLA_NOPE].astype(BF16)
        ko_ref[0, :, MLA_NOPE:] = _rope(kr_ref[...], tab_ref).astype(BF16)
        vo_ref[0] = kvv[:, MLA_NOPE:].astype(BF16)

    return pl.pallas_call(
        body, name=name, grid=(t // tr, MLA_HEADS),
        in_specs=[pl.BlockSpec((tr, MLA_QK_PAD), lambda i, h: (i, h)),
                  pl.BlockSpec((tr, MLA_NOPE + MLA_V), lambda i, h: (i, h)),
                  pl.BlockSpec((tr, LANES), lambda i, h: (i, OFF_KR // LANES)),
                  pl.BlockSpec((3, tr, LANES), lambda i, h: (0, i, 0))],
        out_specs=[pl.BlockSpec((1, tr, MLA_QK_PAD), lambda i, h: (h, i, 0)),
                   pl.BlockSpec((1, tr, MLA_QK_PAD), lambda i, h: (h, i, 0)),
                   pl.BlockSpec((1, tr, MLA_V), lambda i, h: (h, i, 0))],
        out_shape=[jax.ShapeDtypeStruct((MLA_HEADS, t, MLA_QK_PAD), BF16),
                   jax.ShapeDtypeStruct((MLA_HEADS, t, MLA_QK_PAD), BF16),
                   jax.ShapeDtypeStruct((MLA_HEADS, t, MLA_V), BF16)],
        compiler_params=_cp("parallel", "parallel"),
    )(q, kv, proj, tabs)


def _mla_unprep(dq3, dk3, dv3, tabs, *, tr=256, name):
    t = dq3.shape[1]

    def body(dq_ref, dk_ref, dv_ref, tab_ref, qo_ref, kvo_ref, kro_ref):
        kr = jnp.zeros((tr, LANES), F32)
        for h in range(MLA_HEADS):
            c0 = h * MLA_QK_PAD
            qo_ref[:, c0:c0 + MLA_NOPE] = dq_ref[h, :, :MLA_NOPE].astype(BF16)
            qo_ref[:, c0 + MLA_NOPE:c0 + MLA_QK_PAD] = _unrope(dq_ref[h, :, MLA_NOPE:], tab_ref).astype(BF16)
            kvo_ref[:, c0:c0 + MLA_NOPE] = dk_ref[h, :, :MLA_NOPE].astype(BF16)
            kvo_ref[:, c0 + MLA_NOPE:c0 + MLA_QK_PAD] = dv_ref[h].astype(BF16)
            kr = kr + dk_ref[h, :, MLA_NOPE:]
        kro_ref[...] = _unrope(kr, tab_ref).astype(BF16)

    return pl.pallas_call(
        body, name=name, grid=(t // tr,),
        in_specs=[pl.BlockSpec((MLA_HEADS, tr, MLA_QK_PAD), lambda i: (0, i, 0)),
                  pl.BlockSpec((MLA_HEADS, tr, MLA_QK_PAD), lambda i: (0, i, 0)),
                  pl.BlockSpec((MLA_HEADS, tr, MLA_V), lambda i: (0, i, 0)),
                  pl.BlockSpec((3, tr, LANES), lambda i: (0, i, 0))],
        out_specs=[pl.BlockSpec((tr, MLA_HEADS * MLA_QK_PAD), lambda i: (i, 0)),
                   pl.BlockSpec((tr, MLA_HEADS * MLA_QK_PAD), lambda i: (i, 0)),
                   pl.BlockSpec((tr, LANES), lambda i: (i, 0))],
        out_shape=[jax.ShapeDtypeStruct((t, MLA_HEADS * MLA_QK_PAD), BF16),
                   jax.ShapeDtypeStruct((t, MLA_HEADS * MLA_QK_PAD), BF16),
                   jax.ShapeDtypeStruct((t, LANES), BF16)],
        compiler_params=_cp("parallel"),
    )(dq3, dk3, dv3, tabs)


ATT_BLK = 256
ATT_SCALE = 1.0 / math.sqrt(MLA_NOPE + MLA_ROPE)
_NT = (((1,), (1,)), ((), ()))
_TN = (((0,), (0,)), ((), ()))


def _att_scores(q, k, diagonal):
    s = lax.dot_general(q, k, _NT, preferred_element_type=F32) * ATT_SCALE
    if not diagonal:
        return s
    row = lax.broadcasted_iota(jnp.int32, s.shape, 0)
    col = lax.broadcasted_iota(jnp.int32, s.shape, 1)
    return jnp.where((col >> 6) <= (row >> 6), s, NEG)


def _att_rows(i):
    return pl.ds(pl.multiple_of(i * ATT_BLK, ATT_BLK), ATT_BLK)


ATT_HEADS = 2


def _attn_fwd(q3, k3, v3, *, name):
    t = q3.shape[1]
    nq = t // ATT_BLK

    def body(q_ref, k_ref, v_ref, o_ref, lse_ref):
        qi = pl.program_id(1)
        qs = [q_ref[hh] for hh in range(ATT_HEADS)]

        def step(j, carry, diagonal=False):
            out = []
            for hh, (m, l, acc) in enumerate(carry):
                s = _att_scores(qs[hh], k_ref[hh, _att_rows(j), :], diagonal)
                m_new = jnp.maximum(m, jnp.max(s, axis=-1, keepdims=True))
                p = jnp.exp(s - m_new)
                alpha = jnp.exp(m - m_new)
                l = alpha * l + jnp.sum(p, axis=-1, keepdims=True)
                acc = alpha * acc + jnp.dot(p.astype(BF16), v_ref[hh, _att_rows(j), :], preferred_element_type=F32)
                out.append((m_new, l, acc))
            return tuple(out)

        init = tuple((jnp.full((ATT_BLK, 1), NEG, F32), jnp.zeros((ATT_BLK, 1), F32),
                      jnp.zeros((ATT_BLK, MLA_V), F32)) for _ in range(ATT_HEADS))
        done = step(qi, lax.fori_loop(0, qi, step, init), diagonal=True)
        for hh, (m, l, acc) in enumerate(done):
            o_ref[:, hh * MLA_V:(hh + 1) * MLA_V] = acc / l
            lse_ref[hh] = m + jnp.log(l)

    return pl.pallas_call(
        body, name=name, grid=(MLA_HEADS // ATT_HEADS, nq),
        in_specs=[pl.BlockSpec((ATT_HEADS, ATT_BLK, MLA_QK_PAD), lambda h, i: (h, i, 0)),
                  pl.BlockSpec((ATT_HEADS, t, MLA_QK_PAD), lambda h, i: (h, 0, 0)),
                  pl.BlockSpec((ATT_HEADS, t, MLA_V), lambda h, i: (h, 0, 0))],
        out_specs=[pl.BlockSpec((ATT_BLK, ATT_HEADS * MLA_V), lambda h, i: (i, h)),
                   pl.BlockSpec((ATT_HEADS, ATT_BLK, 1), lambda h, i: (h, i, 0))],
        out_shape=[jax.ShapeDtypeStruct((t, MLA_HEADS * MLA_V), F32), jax.ShapeDtypeStruct((MLA_HEADS, t, 1), F32)],
        compiler_params=_cp("parallel", "parallel"),
    )(q3, k3, v3)


def _attn_bwd(q3, k3, v3, o, dcat, lse, *, name):
    t = q3.shape[1]
    nq = t // ATT_BLK
    wide = ATT_HEADS * MLA_V

    def body(q_ref, k_ref, v_ref, o_ref, do_ref, lse_ref, dq_ref, dk_ref, dv_ref, delta_ref):
        kj = pl.program_id(1)

        @pl.when(kj == 0)
        def _():
            dq_ref[...] = jnp.zeros_like(dq_ref)
            prod = o_ref[...] * do_ref[...]
            for hh in range(ATT_HEADS):
                delta_ref[hh] = jnp.sum(prod[:, hh * MLA_V:(hh + 1) * MLA_V], axis=-1, keepdims=True)

        def step(i, carry, diagonal=False):
            rows = _att_rows(i)
            out = []
            for hh, (dk, dv) in enumerate(carry):
                k, v = k_ref[hh], v_ref[hh]
                q = q_ref[hh, rows, :]
                dob = do_ref[rows, hh * MLA_V:(hh + 1) * MLA_V].astype(BF16)
                p = jnp.exp(_att_scores(q, k, diagonal) - lse_ref[hh, rows, :])
                dv = dv + lax.dot_general(p.astype(BF16), dob, _TN, preferred_element_type=F32)
                dp = lax.dot_general(dob, v, _NT, preferred_element_type=F32)
                ds = (p * (dp - delta_ref[hh, rows, :]) * ATT_SCALE).astype(BF16)
                dk = dk + lax.dot_general(ds, q, _TN, preferred_element_type=F32)
                dq_ref[hh, rows, :] += jnp.dot(ds, k, preferred_element_type=F32)
                out.append((dk, dv))
            return tuple(out)

        init = tuple((jnp.zeros((ATT_BLK, MLA_QK_PAD), F32), jnp.zeros((ATT_BLK, MLA_V), F32))
                     for _ in range(ATT_HEADS))
        done = lax.fori_loop(kj + 1, nq, step, step(kj, init, diagonal=True))
        for hh, (dk, dv) in enumerate(done):
            dk_ref[hh] = dk
            dv_ref[hh] = dv

    return pl.pallas_call(
        body, name=name, grid=(MLA_HEADS // ATT_HEADS, nq),
        in_specs=[pl.BlockSpec((ATT_HEADS, t, MLA_QK_PAD), lambda h, j: (h, 0, 0)),
                  pl.BlockSpec((ATT_HEADS, ATT_BLK, MLA_QK_PAD), lambda h, j: (h, j, 0)),
                  pl.BlockSpec((ATT_HEADS, ATT_BLK, MLA_V), lambda h, j: (h, j, 0)),
                  pl.BlockSpec((t, wide), lambda h, j: (0, h)),
                  pl.BlockSpec((t, wide), lambda h, j: (0, MLA_HEADS // ATT_HEADS + h)),
                  pl.BlockSpec((ATT_HEADS, t, 1), lambda h, j: (h, 0, 0))],
        out_specs=[pl.BlockSpec((ATT_HEADS, t, MLA_QK_PAD), lambda h, j: (h, 0, 0)),
                   pl.BlockSpec((ATT_HEADS, ATT_BLK, MLA_QK_PAD), lambda h, j: (h, j, 0)),
                   pl.BlockSpec((ATT_HEADS, ATT_BLK, MLA_V), lambda h, j: (h, j, 0))],
        out_shape=[jax.ShapeDtypeStruct((MLA_HEADS, t, MLA_QK_PAD), F32),
                   jax.ShapeDtypeStruct((MLA_HEADS, t, MLA_QK_PAD), F32),
                   jax.ShapeDtypeStruct((MLA_HEADS, t, MLA_V), F32)],
        scratch_shapes=[pltpu.VMEM((ATT_HEADS, t, 1), F32)],
        compiler_params=_cp("parallel", "arbitrary"),
    )(q3, k3, v3, o, dcat, lse)


def _ssd_prep(proj, bias128, alog128, *, name):
    t = proj.shape[0]
    nc = t // CHUNK

    def body(raw_ref, b_ref, al_ref, dt_ref, cs_ref, a_ref):
        xv = raw_ref[...] + b_ref[...]
        dt = jnp.maximum(xv, 0.0) + jnp.log(1.0 + jnp.exp(-jnp.abs(xv)))
        a = -jnp.exp(al_ref[...])
        adt = (dt * a).reshape(nc, CHUNK, LANES)
        li = lax.broadcasted_iota(jnp.int32, (nc, CHUNK, CHUNK), 1)
        si = lax.broadcasted_iota(jnp.int32, (nc, CHUNK, CHUNK), 2)
        tril = jnp.where(si <= li, 1.0, 0.0).astype(F32)
        cs = lax.dot_general(tril, adt, (((2,), (1,)), ((0,), (0,))), precision=HI, preferred_element_type=F32)
        dt_ref[...] = dt
        cs_ref[...] = cs.reshape(t, LANES)
        a_ref[...] = a

    blk = pl.BlockSpec((t, LANES), lambda i: (0, 0))
    row = pl.BlockSpec((1, LANES), lambda i: (0, 0))
    return pl.pallas_call(
        body, name=name, grid=(1,),
        in_specs=[pl.BlockSpec((t, LANES), lambda i: (0, OFF_DT // LANES)), row, row],
        out_specs=[blk, blk, row],
        out_shape=[jax.ShapeDtypeStruct((t, LANES), F32), jax.ShapeDtypeStruct((t, LANES), F32),
                   jax.ShapeDtypeStruct((1, LANES), F32)],
        compiler_params=_cp("arbitrary"),
    )(proj, bias128, alog128)


def _ssd_prep_bwd(ddt128, dadt128, proj, bias128, dt128, a128, dd_h, *, name):
    t = proj.shape[0]

    def body(ddt_ref, dadt_ref, raw_ref, b_ref, dt_ref, a_ref, dd_ref, draw_ref, db_ref, dal_ref, dds_ref):
        draw = ddt_ref[...] * _sigmoid(raw_ref[...] + b_ref[...])
        draw_ref[...] = draw.astype(BF16)
        db_ref[...] = jnp.sum(draw, axis=0, keepdims=True)
        dal_ref[...] = jnp.sum(dadt_ref[...] * dt_ref[...], axis=0, keepdims=True) * a_ref[...]
        dds_ref[...] = jnp.sum(dd_ref[...], axis=-1, keepdims=True)

    blk = pl.BlockSpec((t, LANES), lambda i: (0, 0))
    row = pl.BlockSpec((1, LANES), lambda i: (0, 0))
    return pl.pallas_call(
        body, name=name, grid=(1,),
        in_specs=[blk, blk, pl.BlockSpec((t, LANES), lambda i: (0, OFF_DT // LANES)), row, blk, row,
                  pl.BlockSpec((SSD_HEADS, SSD_P), lambda i: (0, 0))],
        out_specs=[blk, row, row, pl.BlockSpec((SSD_HEADS, 1), lambda i: (0, 0))],
        out_shape=[jax.ShapeDtypeStruct((t, LANES), BF16), jax.ShapeDtypeStruct((1, LANES), F32),
                   jax.ShapeDtypeStruct((1, LANES), F32), jax.ShapeDtypeStruct((SSD_HEADS, 1), F32)],
        compiler_params=_cp("arbitrary"),
    )(ddt128, dadt128, proj, bias128, dt128, a128, dd_h)


def _bdot(a, b, ca, cb, precision=None):
    return lax.dot_general(a, b, (((ca,), (cb,)), ((0,), (0,))), precision=precision, preferred_element_type=F32)


def _head_matrices():
    eye, zero = jnp.eye(SSD_P, dtype=F32), jnp.zeros((SSD_P, SSD_P), F32)
    pick = jnp.stack([jnp.concatenate([eye, zero], axis=0), jnp.concatenate([zero, eye], axis=0)])
    return pick, pick.transpose(0, 2, 1)


def _move(x, sel):
    selb = sel.astype(BF16)
    hi = x.astype(BF16)
    rest = x - hi.astype(F32)
    mid = rest.astype(BF16)
    low = (rest - mid.astype(F32)).astype(BF16)
    out = jnp.dot(hi, selb, preferred_element_type=F32)
    out = out + jnp.dot(mid, selb, preferred_element_type=F32)
    return out + jnp.dot(low, selb, preferred_element_type=F32)


def _pick_head(pair_ref, pick_ref, h):
    return _move(pair_ref[...], pick_ref[h % 2])


def _place_head(out_ref, val, place_ref, h):
    wide = _move(val, place_ref[h % 2])

    @pl.when(h % 2 == 0)
    def _():
        out_ref[...] = wide

    @pl.when(h % 2 == 1)
    def _():
        out_ref[...] += wide


def _ssd_common(x2, dt_ref, cs_ref, csr_ref, b_ref, c_ref, nc):
    x = x2.reshape(nc, CHUNK, SSD_P)
    dt = dt_ref[0].reshape(nc, CHUNK, SSD_P)
    cs = cs_ref[0].reshape(nc, CHUNK, SSD_P)
    csr = csr_ref[0]
    bm = b_ref[...].reshape(nc, CHUNK, SSD_N).astype(BF16)
    cm = c_ref[...].reshape(nc, CHUNK, SSD_N).astype(BF16)
    li = lax.broadcasted_iota(jnp.int32, (nc, CHUNK, CHUNK), 1)
    si = lax.broadcasted_iota(jnp.int32, (nc, CHUNK, CHUNK), 2)
    lmat = jnp.exp(jnp.where(si <= li, cs - csr, NEG))
    g = _bdot(cm, bm, 2, 2)
    cs_last = jnp.sum(jnp.where(li == CHUNK - 1, cs, 0.0), axis=1, keepdims=True)
    xdt = x * dt
    dec = jnp.exp(cs_last - cs)
    return x, dt, cs, bm, cm, li, si, lmat, g, cs_last, xdt, dec


def _ssd_fwd(xbc, dt_h, cs_h, cs_row, dskip_h, *, name):
    t = xbc.shape[0]
    nc = t // CHUNK
    hpg = SSD_HEADS // SSD_GROUPS
    pick, place = _head_matrices()

    def body(xs_ref, dt_ref, cs_ref, csr_ref, b_ref, c_ref, dk_ref, pick_ref, place_ref, y_ref, st_ref, sc_ref, cd_ref):
        h = pl.program_id(0)
        x, dt, cs, bm, cm, li, si, lmat, g, cs_last, xdt, dec = _ssd_common(_pick_head(xs_ref, pick_ref, h), dt_ref,
                                                                           cs_ref, csr_ref, b_ref, c_ref, nc)
        yd = _bdot((g * lmat).astype(BF16), xdt.astype(BF16), 2, 1)
        sc_ref[...] = _bdot(bm, (dec * xdt).astype(BF16), 1, 1)
        cd_ref[...] = jnp.exp(cs_last)

        def step(c, s):
            st_ref[0, c] = s
            return s * cd_ref[c] + sc_ref[c]

        lax.fori_loop(0, nc, step, jnp.zeros((SSD_N, SSD_P), F32))
        yo = _bdot(cm, st_ref[0].astype(BF16), 2, 1) * jnp.exp(cs)
        _place_head(y_ref, (yd + yo + dk_ref[0] * x).reshape(t, SSD_P), place_ref, h)

    head = pl.BlockSpec((1, t, SSD_P), lambda h: (h, 0, 0))
    pair = pl.BlockSpec((t, 2 * SSD_P), lambda h: (0, h // 2))
    nxb = D_SSM // SSD_N
    return pl.pallas_call(
        body, name=name, grid=(SSD_HEADS,),
        in_specs=[pair, head, head, pl.BlockSpec((1, nc, 1, CHUNK), lambda h: (h, 0, 0, 0)),
                  pl.BlockSpec((t, SSD_N), lambda h: (0, nxb + h // hpg)),
                  pl.BlockSpec((t, SSD_N), lambda h: (0, nxb + SSD_GROUPS + h // hpg)),
                  pl.BlockSpec((1, 1, SSD_P), lambda h: (h, 0, 0)),
                  pl.BlockSpec((2, 2 * SSD_P, SSD_P), lambda h: (0, 0, 0)),
                  pl.BlockSpec((2, SSD_P, 2 * SSD_P), lambda h: (0, 0, 0))],
        out_specs=[pair, pl.BlockSpec((1, nc, SSD_N, SSD_P), lambda h: (h, 0, 0, 0))],
        out_shape=[jax.ShapeDtypeStruct((t, D_SSM), F32),
                   jax.ShapeDtypeStruct((SSD_HEADS, nc, SSD_N, SSD_P), F32)],
        scratch_shapes=[pltpu.VMEM((nc, SSD_N, SSD_P), F32), pltpu.VMEM((nc, 1, SSD_P), F32)],
        compiler_params=_cp("arbitrary"),
    )(xbc, dt_h, cs_h, cs_row, xbc, xbc, dskip_h, pick, place)


def _ssd_bwd(xbc, dt_h, cs_h, cs_row, dskip_h, a_h, states, dy, *, name):
    t = xbc.shape[0]
    nc = t // CHUNK
    hpg = SSD_HEADS // SSD_GROUPS
    pick, place = _head_matrices()

    def body(xs_ref, dt_ref, cs_ref, csr_ref, b_ref, c_ref, dk_ref, a_ref, st_ref, dy_ref, pick_ref, place_ref,
             dxs_ref, ddt_ref, dadt_ref, db_ref, dc_ref, dd_ref, dsl_ref, dsc_ref, cd_ref):
        h = pl.program_id(0) * hpg + pl.program_id(1)
        x, dt, cs, bm, cm, li, si, lmat, g, cs_last, xdt, dec = _ssd_common(_pick_head(xs_ref, pick_ref, h), dt_ref,
                                                                           cs_ref, csr_ref, b_ref, c_ref, nc)
        dy = _pick_head(dy_ref, pick_ref, h).reshape(nc, CHUNK, SSD_P)
        dyb = dy.astype(BF16)
        xdtb = xdt.astype(BF16)
        sprev = st_ref[0]
        sprevb = sprev.astype(BF16)
        cdec = jnp.exp(cs_last)
        ecs = jnp.exp(cs)
        dw = (ecs * dy).astype(BF16)
        wmat = _bdot(cm, sprevb, 2, 1)
        dcs = jnp.sum(dy * ecs * wmat, axis=2, keepdims=True)
        dcm = _bdot(dw, sprevb, 2, 2)
        dsl_ref[...] = _bdot(cm, dw, 1, 1)
        cd_ref[...] = cdec

        def step(k, ds):
            c = nc - 1 - k
            dsc_ref[c] = ds
            return ds * cd_ref[c] + dsl_ref[c]

        lax.fori_loop(0, nc, step, jnp.zeros((SSD_N, SSD_P), F32))
        dsc = dsc_ref[...]
        dscb = dsc.astype(BF16)
        d_last = jnp.sum(jnp.sum(dsc * sprev, axis=1, keepdims=True) * cdec, axis=2, keepdims=True)
        z = dec * xdt
        dbm = _bdot(z.astype(BF16), dscb, 2, 2)
        dz = _bdot(bm, dscb, 2, 1)
        dxdt = dec * dz
        t2 = jnp.sum(dz * z, axis=2, keepdims=True)
        dcs = dcs - t2
        d_last = d_last + jnp.sum(t2, axis=1, keepdims=True)
        m = g * lmat
        mb = m.astype(BF16)
        dm = _bdot(dyb, xdtb, 2, 2)
        dxdt = dxdt + _bdot(mb, dyb, 1, 1)
        dseg = dm * m
        dcs = dcs + jnp.sum(dseg, axis=2, keepdims=True)
        ones = jnp.ones((nc, CHUNK, SSD_P), F32)
        dcs = dcs - _bdot(dseg, ones, 1, 1, precision=HI)
        dg = (dm * lmat).astype(BF16)
        dcm = dcm + _bdot(dg, bm, 2, 1)
        dbm = dbm + _bdot(dg, cm, 1, 1)
        dcs = dcs + jnp.where(li[:, :, :SSD_P] == CHUNK - 1, d_last, 0.0)
        triu = jnp.where(li <= si, 1.0, 0.0).astype(F32)
        dadt = _bdot(triu, dcs, 2, 1, precision=HI)
        dk = dk_ref[0]
        _place_head(dxs_ref, (dxdt * dt + dk * dy).reshape(t, SSD_P), place_ref, h)
        ddt = jnp.sum(dxdt * x, axis=2, keepdims=True) + dadt * a_ref[0]
        mine = lax.broadcasted_iota(jnp.int32, (t, LANES), 1) == h

        @pl.when(h == 0)
        def _():
            ddt_ref[...] = jnp.zeros_like(ddt_ref)
            dadt_ref[...] = jnp.zeros_like(dadt_ref)

        ddt_ref[...] += jnp.where(mine, jnp.max(ddt, axis=2, keepdims=True).reshape(t, 1), 0.0)
        dadt_ref[...] += jnp.where(mine, jnp.max(dadt, axis=2, keepdims=True).reshape(t, 1), 0.0)
        dd_ref[0] = jnp.sum(jnp.sum(dy * x, axis=1, keepdims=True), axis=0)

        @pl.when(pl.program_id(1) == 0)
        def _():
            db_ref[...] = jnp.zeros_like(db_ref)
            dc_ref[...] = jnp.zeros_like(dc_ref)

        db_ref[...] += dbm.reshape(t, SSD_N)
        dc_ref[...] += dcm.reshape(t, SSD_N)

    head = pl.BlockSpec((1, t, SSD_P), lambda gi, hi: (gi * hpg + hi, 0, 0))
    pair = pl.BlockSpec((t, 2 * SSD_P), lambda gi, hi: (0, (gi * hpg + hi) // 2))
    grp = pl.BlockSpec((t, SSD_N), lambda gi, hi: (0, gi))
    lane = pl.BlockSpec((1, 1, SSD_P), lambda gi, hi: (gi * hpg + hi, 0, 0))
    rows = pl.BlockSpec((t, LANES), lambda gi, hi: (0, 0))
    nxb = D_SSM // SSD_N
    dxs, ddt, dadt, db, dc, dd = pl.pallas_call(
        body, name=name, grid=(SSD_GROUPS, hpg),
        in_specs=[pair, head, head, pl.BlockSpec((1, nc, 1, CHUNK), lambda gi, hi: (gi * hpg + hi, 0, 0, 0)),
                  pl.BlockSpec((t, SSD_N), lambda gi, hi: (0, nxb + gi)),
                  pl.BlockSpec((t, SSD_N), lambda gi, hi: (0, nxb + SSD_GROUPS + gi)), lane, lane,
                  pl.BlockSpec((1, nc, SSD_N, SSD_P), lambda gi, hi: (gi * hpg + hi, 0, 0, 0)), pair,
                  pl.BlockSpec((2, 2 * SSD_P, SSD_P), lambda gi, hi: (0, 0, 0)),
                  pl.BlockSpec((2, SSD_P, 2 * SSD_P), lambda gi, hi: (0, 0, 0))],
        out_specs=[pair, rows, rows, grp, grp, lane],
        out_shape=[jax.ShapeDtypeStruct((t, D_SSM), F32)] + [jax.ShapeDtypeStruct((t, LANES), F32)] * 2
        + [jax.ShapeDtypeStruct((t, SSD_GROUPS * SSD_N), F32)] * 2
        + [jax.ShapeDtypeStruct((SSD_HEADS, 1, SSD_P), F32)],
        scratch_shapes=[pltpu.VMEM((nc, SSD_N, SSD_P), F32), pltpu.VMEM((nc, SSD_N, SSD_P), F32),
                        pltpu.VMEM((nc, 1, SSD_P), F32)],
        compiler_params=_cp("arbitrary", "arbitrary"),
    )(xbc, dt_h, cs_h, cs_row, xbc, xbc, dskip_h, a_h, states, dy, pick, place)
    return jnp.concatenate([dxs, db, dc], axis=1), ddt, dadt, dd


def _ssd_gate_fwd(y, proj, w, *, tr=256, name):
    t = y.shape[0]
    gw = D_SSM // SSD_GROUPS

    def body(y_ref, z_ref, w_ref, o_ref):
        v = y_ref[...] * _silu(z_ref[...])
        for gi in range(SSD_GROUPS):
            vg = v[:, gi * gw:(gi + 1) * gw]
            r = lax.rsqrt(jnp.mean(vg * vg, axis=-1, keepdims=True) + NORM_EPS)
            o_ref[:, gi * gw:(gi + 1) * gw] = (vg * r * w_ref[:, gi * gw:(gi + 1) * gw]).astype(BF16)

    blk = pl.BlockSpec((tr, D_SSM), lambda i: (i, 0))
    return pl.pallas_call(
        body, name=name, grid=(t // tr,), in_specs=[blk, blk, pl.BlockSpec((1, D_SSM), lambda i: (0, 0))],
        out_specs=blk, out_shape=jax.ShapeDtypeStruct((t, D_SSM), BF16), compiler_params=_cp("parallel"),
    )(y, proj, w)


def _ssd_gate_bwd(y, proj, w, dcat, *, tr=256, name):
    t = y.shape[0]
    gw = D_SSM // SSD_GROUPS

    def body(y_ref, z_ref, w_ref, d_ref, dy_ref, dz_ref, dw_ref):
        yv, zv, dv = y_ref[...], z_ref[...], d_ref[...].astype(F32)
        sz = _silu(zv)
        v = yv * sz

        @pl.when(pl.program_id(0) == 0)
        def _():
            dw_ref[...] = jnp.zeros_like(dw_ref)

        for gi in range(SSD_GROUPS):
            sl = slice(gi * gw, (gi + 1) * gw)
            vg, dg = v[:, sl], dv[:, sl]
            r = lax.rsqrt(jnp.mean(vg * vg, axis=-1, keepdims=True) + NORM_EPS)
            vh = vg * r
            gg = dg * w_ref[:, sl]
            dvg = r * (gg - vh * jnp.mean(gg * vh, axis=-1, keepdims=True))
            dy_ref[:, sl] = dvg * sz[:, sl]
            dz_ref[:, sl] = (dvg * yv[:, sl] * _dsilu(zv[:, sl])).astype(BF16)
            dw_ref[:, sl] += jnp.sum(dg * vh, axis=0, keepdims=True)

    blk = pl.BlockSpec((tr, D_SSM), lambda i: (i, 0))
    row = pl.BlockSpec((1, D_SSM), lambda i: (0, 0))
    return pl.pallas_call(
        body, name=name, grid=(t // tr,), in_specs=[blk, blk, row, blk], out_specs=[blk, blk, row],
        out_shape=[jax.ShapeDtypeStruct((t, D_SSM), F32), jax.ShapeDtypeStruct((t, D_SSM), BF16),
                   jax.ShapeDtypeStruct((1, D_SSM), F32)],
        compiler_params=_cp("arbitrary"),
    )(y, proj, w, dcat)


def _pad_lanes(v):
    return jnp.pad(v, ((0, 0), (0, LANES - v.shape[1])))


def _per_head(v128, t):
    return jnp.broadcast_to(v128[:, :SSD_HEADS].T[:, :, None], (SSD_HEADS, t, SSD_P))


def _ssd_forward(proj, conv_w, conv_b, dt_bias, a_log, d_skip, ssd_norm_w):
    t = proj.shape[0]
    nc = t // CHUNK
    xbc = _conv_act_fwd(proj, conv_w, conv_b, kw=SSD_CONV, glu=False, tc=512, coff=OFF_XBC // 512,
                        ncols=SSD_CONV_DIM, out_dtype=F32, name="ssd_conv_fwd")
    bias128, alog128 = _pad_lanes(dt_bias), _pad_lanes(a_log)
    dt128, cs128, a128 = _ssd_prep(proj, bias128, alog128, name="ssd_prep")
    dt_h, cs_h = _per_head(dt128, t), _per_head(cs128, t)
    cs_row = cs128[:, :SSD_HEADS].T.reshape(SSD_HEADS, nc, 1, CHUNK)
    dskip_h = jnp.broadcast_to(d_skip[0][:, None, None], (SSD_HEADS, 1, SSD_P))
    a_h = jnp.broadcast_to(a128[0, :SSD_HEADS][:, None, None], (SSD_HEADS, 1, SSD_P))
    y, states = _ssd_fwd(xbc, dt_h, cs_h, cs_row, dskip_h, name="ssd_scan_fwd")
    y_ssd = _ssd_gate_fwd(y, proj, ssd_norm_w, name="ssd_gate_fwd")
    saved = (proj, conv_w, conv_b, ssd_norm_w, bias128, dt128, a128, dt_h, cs_h, cs_row, xbc, dskip_h, a_h, states, y)
    return y_ssd, saved


def _ssd_backward(saved, dcat):
    proj, conv_w, conv_b, ssd_norm_w, bias128, dt128, a128, dt_h, cs_h, cs_row, xbc, dskip_h, a_h, states, y = saved
    dy, dz, d_norm_w = _ssd_gate_bwd(y, proj, ssd_norm_w, dcat, name="ssd_gate_bwd")
    dxc, ddt128, dadt128, dd_h = _ssd_bwd(xbc, dt_h, cs_h, cs_row, dskip_h, a_h, states, dy, name="ssd_scan_bwd")
    dxbc, d_conv_w, d_conv_b = _conv_act_bwd(proj, conv_w, conv_b, dxc, kw=SSD_CONV, glu=False, tc=512,
                                             coff=OFF_XBC // 512, ncols=SSD_CONV_DIM, name="ssd_conv_bwd")
    d_raw, d_bias, d_alog, d_dskip = _ssd_prep_bwd(ddt128, dadt128, proj, bias128, dt128, a128,
                                                   dd_h.reshape(SSD_HEADS, SSD_P), name="ssd_prep_bwd")
    return (dz, dxbc, d_raw, d_norm_w, d_conv_w, d_conv_b, d_bias[:, :SSD_HEADS], d_alog[:, :SSD_HEADS],
            d_dskip.reshape(1, SSD_HEADS))


def _rope_tables(positions):
    inv_freq = ROPE_THETA ** (-jnp.arange(0, MLA_ROPE, 2, dtype=F32) / MLA_ROPE)
    ang = positions[0].astype(F32)[:, None] * inv_freq
    cos, sin = jnp.cos(ang), jnp.sin(ang)
    z = jnp.zeros_like(cos)
    return jnp.stack([jnp.concatenate([cos, cos, z, z], axis=1), jnp.concatenate([-sin, z, z, z], axis=1),
                      jnp.concatenate([z, sin, z, z], axis=1)])


def _mla_forward(proj, tabs, q_a_norm_w, wq_pad, kv_a_norm_w, wkv):
    qn = _rmsnorm_fwd(proj, q_a_norm_w, width=MLA_Q_RANK, cblk=OFF_QA // MLA_Q_RANK, name="q_a_norm")
    q = _matmul(qn, wq_pad, name="q_b_proj")
    kvn = _rmsnorm_fwd(proj, kv_a_norm_w, width=MLA_KV_RANK, cblk=OFF_CKV // MLA_KV_RANK, name="kv_a_norm")
    kv = _matmul(kvn, wkv, name="kv_b_proj")
    q3, k3, v3 = _mla_prep(q, kv, proj, tabs, name="mla_prep")
    o, lse = _attn_fwd(q3, k3, v3, name="attn_fwd")
    return o, (proj, tabs, q_a_norm_w, wq_pad, kv_a_norm_w, wkv, qn, kvn, q3, k3, v3, o, lse)


def _mla_backward(saved, dcat):
    proj, tabs, q_a_norm_w, wq_pad, kv_a_norm_w, wkv, qn, kvn, q3, k3, v3, o, lse = saved
    dq3, dk3, dv3 = _attn_bwd(q3, k3, v3, o, dcat, lse, name="attn_bwd")
    dq, dkv, dkr = _mla_unprep(dq3, dk3, dv3, tabs, name="mla_unprep")
    d_wq = _matmul(qn, dq, ta=True, out_dtype=BF16, name="d_w_q_b")
    dqn = _matmul(dq, wq_pad, tb=True, name="d_qn")
    dq_a, d_qnw = _rmsnorm_bwd(proj, q_a_norm_w, dqn, width=MLA_Q_RANK, cblk=OFF_QA // MLA_Q_RANK, out_dtype=BF16,
                               name="q_a_norm_bwd")
    d_wkv = _matmul(kvn, dkv, ta=True, out_dtype=BF16, name="d_w_kv_b")
    dkvn = _matmul(dkv, wkv, tb=True, name="d_kvn")
    dckv, d_kvnw = _rmsnorm_bwd(proj, kv_a_norm_w, dkvn, width=MLA_KV_RANK, cblk=OFF_CKV // MLA_KV_RANK,
                                out_dtype=BF16, name="kv_a_norm_bwd")
    return dq_a, dckv, dkr, d_wq, d_wkv, d_qnw, d_kvnw


def _pad_w_q(w):
    r = w.shape[0]
    w3 = w.reshape(r, MLA_HEADS, MLA_NOPE + MLA_ROPE)
    return jnp.pad(w3, ((0, 0), (0, 0), (0, MLA_QK_PAD - MLA_NOPE - MLA_ROPE))).reshape(r, MLA_HEADS * MLA_QK_PAD)


def _unpad_w_q(w):
    r = w.shape[0]
    return w.reshape(r, MLA_HEADS, MLA_QK_PAD)[:, :, :MLA_NOPE + MLA_ROPE].reshape(r, MLA_HEADS * (MLA_NOPE + MLA_ROPE))


W_IN_SEGMENTS = ((0, D_SSM + SSD_CONV_DIM, 0), (D_SSM + SSD_CONV_DIM, D_SSM + SSD_CONV_DIM + SSD_HEADS, OFF_DT),
                 (D_SSM + SSD_CONV_DIM + SSD_HEADS, D_IN - MLA_ROPE, OFF_QA), (D_IN - MLA_ROPE, D_IN, OFF_KR))


def _pad_w_in_shards(g):
    n = g.shape[2]
    pieces, at = [], 0
    for lo, hi, start in sorted(W_IN_SEGMENTS, key=lambda seg: seg[2]):
        if start > at:
            pieces.append(jnp.zeros((g.shape[1], start - at), g.dtype))
        for j in range(N_DEV):
            a, b = max(lo, j * n), min(hi, (j + 1) * n)
            if a < b:
                pieces.append(g[j][:, a - j * n:b - j * n])
        at = start + hi - lo
    pieces.append(jnp.zeros((g.shape[1], D_IN_PAD - at), g.dtype))
    return jnp.concatenate(pieces, axis=1)


def _unpad_w_in_shards(w):
    n = D_IN // N_DEV
    shards = []
    for j in range(N_DEV):
        pieces = []
        for lo, hi, start in W_IN_SEGMENTS:
            a, b = max(lo, j * n), min(hi, (j + 1) * n)
            if a < b:
                pieces.append(w[:, start + a - lo:start + b - lo])
        shards.append(jnp.concatenate(pieces, axis=1) if len(pieces) > 1 else pieces[0])
    return jnp.stack(shards)


WEIGHTS = ['mix_norm_w', 'w_in', 'conv_w', 'conv_b', 'dt_bias', 'a_log', 'd_skip', 'ssd_norm_w', 'q_a_norm_w', 'w_q_b',
           'kv_a_norm_w', 'w_kv_b', 'w_out', 'ffn_norm_w', 'w_ffn_up', 'ffn_conv_w', 'ffn_conv_b', 'w_ffn_down',
           'ple_norm_w', 'w_ple_gate', 'b_ple_gate', 'w_ple_proj', 'ple_post_norm_w', 'final_norm_w']
BIG = ['w_in', 'w_q_b', 'w_kv_b', 'w_out', 'w_ffn_up', 'w_ffn_down', 'w_ple_gate', 'w_ple_proj']
COL_SHARDED = ('w_in', 'w_q_b', 'w_kv_b', 'w_ffn_up', 'w_ple_proj')
CONV = ['conv_w', 'ffn_conv_w']
REPL = [n for n in WEIGHTS if n not in BIG and n not in CONV]
FFN_INV = tuple(int(i) for i in np.argsort(FFN_PERM))


def _cat_cols(g):
    return jnp.concatenate([g[j] for j in range(N_DEV)], axis=1)


def _split_cols(w):
    n = w.shape[1] // N_DEV
    return jnp.stack([w[:, j * n:(j + 1) * n] for j in range(N_DEV)])


def _interleave(v):
    r = v.shape[0]
    return v.reshape(r, N_DEV, FFN_TC)[:, jnp.array(FFN_PERM)].reshape(r, N_DEV * FFN_TC)


def _deinterleave(v):
    r = v.shape[0]
    return v.reshape(r, N_DEV, FFN_TC)[:, jnp.array(FFN_INV)].reshape(r, N_DEV * FFN_TC)


def _assemble_weights(g):
    layout = {
        'w_in': _pad_w_in_shards,
        'w_q_b': lambda v: _pad_w_q(_cat_cols(v)),
        'w_kv_b': _cat_cols,
        'w_out': lambda v: v.reshape(D_MODEL, D_MODEL),
        'w_ffn_up': lambda v: v,
        'w_ffn_down': lambda v: v.reshape(D_FF, D_MODEL),
        'w_ple_gate': lambda v: v.reshape(D_MODEL, D_MODEL),
        'w_ple_proj': _cat_cols,
        'conv_w': _cat_cols,
        'ffn_conv_w': lambda v: _interleave(_cat_cols(v)),
    }
    return {n: layout[n](v) for n, v in g.items()}


WEIGHT_GROUPS = {'a': ['w_in', 'w_q_b', 'w_kv_b', 'conv_w'], 'b': ['w_out'],
                 'c': ['w_ffn_up', 'ffn_conv_w', 'w_ffn_down', 'w_ple_gate', 'w_ple_proj']}
GRAD_GROUPS = {'p': ['w_ple_proj', 'w_ple_gate', 'w_ffn_down'], 'r': ['w_ffn_up'], 's': ['w_out'],
               't': ['w_q_b', 'w_kv_b', 'w_in']}


def _ffn_perm(j):
    return (j % 2) * (N_DEV // 2) + j // 2


def _local_step(x, p, tabs, get_w, s, target, emit, relay, settle):
    t = x.shape[0]
    s = dict(s)
    half = D_MODEL // 2
    up_cols = 2 * D_FF
    ffn_conv_b = _interleave(s['ffn_conv_b'])
    w = dict(get_w('a', None))
    h = _rmsnorm_fwd(x, s['mix_norm_w'], width=D_MODEL, name="mix_norm")
    proj = _matmul(h, w['w_in'], name="in_proj")
    y_ssd, ssd_saved = _ssd_forward(proj, w['conv_w'], s['conv_b'], s['dt_bias'], s['a_log'], s['d_skip'],
                                    s['ssd_norm_w'])
    o, mla_saved = _mla_forward(proj, tabs, s['q_a_norm_w'], w['w_q_b'], s['kv_a_norm_w'], w['w_kv_b'])
    tk_o, tn_o = _tile(half, MM_TK), _tile(D_MODEL, MM_TILE)
    w.update(get_w('b', o))
    x1 = _matmul(y_ssd, w['w_out'], add=x, mnk=(t, D_MODEL, half), name="out_proj_ssd")
    x1 = _matmul(o, w['w_out'], add=x1, mnk=(t, D_MODEL, half), name="out_proj_mla",
                 b_spec=pl.BlockSpec((tk_o, tn_o), lambda i, j, kk: (kk + half // tk_o, j)))
    hf = _rmsnorm_fwd(x1, s['ffn_norm_w'], width=D_MODEL, name="ffn_norm")
    w.update(get_w('c', hf))
    tk_u = _tile(D_MODEL, MM_TK)
    u = _matmul(hf, w['w_ffn_up'], mnk=(t, up_cols, D_MODEL), tn=FFN_TC, name="ffn_up",
                b_spec=pl.BlockSpec((1, tk_u, FFN_TC), lambda i, j, kk: (_ffn_perm(j), kk, 0)))
    act = _conv_act_fwd(u, w['ffn_conv_w'], ffn_conv_b, kw=FFN_CONV, glu=True, tc=2 * FFN_TC, coff=0, ncols=up_cols,
                        out_dtype=BF16, name="ffn_act")
    x2 = _matmul(act, w['w_ffn_down'], add=x1, name="ffn_down")
    hp = _rmsnorm_fwd(x2, s['ple_norm_w'], width=D_MODEL, name="ple_norm")
    gl = _matmul(hp, w['w_ple_gate'], bias=s['b_ple_gate'], name="ple_gate")
    pe = _matmul(p, w['w_ple_proj'], name="ple_proj")
    x3 = _ple_fwd(x2, gl, pe, s['ple_post_norm_w'], name="ple_mix")
    loss, dx3, d_final = _loss_head(x3, s['final_norm_w'], target, name="loss_head")
    dgl, d_bgate, dpe, d_post = _ple_bwd(dx3, gl, pe, s['ple_post_norm_w'], name="ple_mix_bwd")
    d_wproj = _matmul(p, dpe, ta=True, out_dtype=BF16, name="d_w_ple_proj")
    d_wgate = _matmul(hp, dgl, ta=True, out_dtype=BF16, name="d_w_ple_gate")
    dhp = _matmul(dgl, w['w_ple_gate'], tb=True, name="d_ple_normed")
    dx2, d_plenorm = _rmsnorm_bwd(x2, s['ple_norm_w'], dhp, dx3, width=D_MODEL, name="ple_norm_bwd")
    dact = _matmul(dx2, w['w_ffn_down'], tb=True, name="d_ffn_act")
    d_wdown = _matmul(act, dx2, ta=True, out_dtype=BF16, name="d_w_ffn_down")
    zz = emit('p', {'w_ple_proj': _split_cols(d_wproj), 'w_ple_gate': d_wgate.reshape(N_DEV, D_MODEL // N_DEV, D_MODEL),
                    'w_ffn_down': d_wdown.reshape(N_DEV, D_FF // N_DEV, D_MODEL)})
    du, d_fconv_w, d_fconv_b = _conv_act_bwd(u, w['ffn_conv_w'], ffn_conv_b + zz, dact, kw=FFN_CONV, glu=True,
                                             tc=2 * FFN_TC, coff=0, ncols=up_cols, name="ffn_act_bwd")
    zz = zz + relay('p', du)
    tm_u = _tile(D_MODEL, MM_TILE)
    d_wup = _matmul(hf, du, ta=True, out_dtype=BF16, mnk=(D_MODEL, up_cols, t), tn=FFN_TC, name="d_w_ffn_up",
                    o_spec=pl.BlockSpec((1, tm_u, FFN_TC), lambda i, j, kk: (_ffn_perm(j), i, 0)),
                    o_shape=(N_DEV, D_MODEL, FFN_TC))
    zz = zz + emit('r', {'w_ffn_up': d_wup})
    zero_row = jnp.zeros((1, D_MODEL), F32)
    dhf = _matmul(du, w['w_ffn_up'], tb=True, mnk=(t, D_MODEL, up_cols), tk=FFN_TC, name="d_ffn_normed",
                  bias=zero_row + zz,
                  b_spec=pl.BlockSpec((1, tn_o, FFN_TC), lambda i, j, kk: (_ffn_perm(kk), j, 0)))
    zz = zz + relay('r', dhf) + settle('p')
    dx1, d_ffnnorm = _rmsnorm_bwd(x1, s['ffn_norm_w'] + zz, dhf, dx2, width=D_MODEL, name="ffn_norm_bwd")
    dcat = _matmul(dx1, w['w_out'], tb=True, name="d_mixed")
    d_wout = jnp.concatenate([_matmul(y_ssd, dx1, ta=True, out_dtype=BF16, name="d_w_out_ssd"),
                              _matmul(o, dx1, ta=True, out_dtype=BF16, name="d_w_out_mla")], axis=0)
    zz = zz + emit('s', {'w_out': d_wout.reshape(N_DEV, D_MODEL // N_DEV, D_MODEL)})
    ssd_saved = ssd_saved[:3] + (ssd_saved[3] + zz,) + ssd_saved[4:]
    dz, dxbc, d_raw, d_ssdnorm, d_conv_w, d_conv_b, d_dtb, d_alog, d_dskip = _ssd_backward(ssd_saved, dcat)
    zz = zz + relay('s', dz)
    mla_saved = mla_saved[:-1] + (mla_saved[-1] + zz,)
    dq_a, dckv, dkr, d_wq, d_wkv, d_qnorm, d_kvnorm = _mla_backward(mla_saved, dcat)
    d_raw = (d_raw + settle('r')).astype(BF16)
    dproj = jnp.concatenate([dz, dxbc, dq_a, dckv, dkr, d_raw], axis=1)
    d_win = _matmul(h, dproj, ta=True, out_dtype=BF16, name="d_w_in")
    zz = emit('t', {'w_in': _unpad_w_in_shards(d_win), 'w_q_b': _split_cols(_unpad_w_q(d_wq)),
                    'w_kv_b': _split_cols(d_wkv)})
    dh = _matmul(dproj, w['w_in'], tb=True, bias=zero_row + zz, name="d_in_normed")
    zz = relay('t', dh) + settle('s')
    dx, d_mixnorm = _rmsnorm_bwd(x, s['mix_norm_w'] + zz, dh, dx1, width=D_MODEL, name="mix_norm_bwd")
    conv = {'conv_w': d_conv_w, 'ffn_conv_w': _deinterleave(d_fconv_w)}
    vec = {
        'mix_norm_w': d_mixnorm, 'conv_b': d_conv_b, 'dt_bias': d_dtb, 'a_log': d_alog, 'd_skip': d_dskip,
        'ssd_norm_w': d_ssdnorm, 'q_a_norm_w': d_qnorm, 'kv_a_norm_w': d_kvnorm, 'ffn_norm_w': d_ffnnorm,
        'ffn_conv_b': _deinterleave(d_fconv_b), 'ple_norm_w': d_plenorm, 'b_ple_gate': d_bgate,
        'ple_post_norm_w': d_post, 'final_norm_w': d_final,
    }
    return loss, dx, conv, vec


MESH = pl.DeviceIdType.MESH
FLIPS = ((0, 0, 1), (1, 0, 0), (0, 1, 0), (1, 1, 0), (1, 0, 1), (0, 1, 1), (1, 1, 1))


def _exchange(items, *, gather, name):
    n = len(items)

    def body(*refs):
        ins, outs = refs[:n], refs[n:2 * n]
        send_sems, recv_sems, local_sems = refs[2 * n:]
        x, y, c = lax.axis_index("x"), lax.axis_index("y"), lax.axis_index("c")
        me = 4 * x + 2 * y + c
        peers = [(jnp.where(fx, 1 - x, x), jnp.where(fy, 1 - y, y), jnp.where(fc, 1 - c, c)) for fx, fy, fc in FLIPS]
        slot = [4 * px + 2 * py + pc for px, py, pc in peers]
        local, sends = [], []
        for wi in range(n):
            cp = pltpu.make_async_copy(ins[wi] if gather else ins[wi].at[me], outs[wi].at[me], local_sems.at[wi])
            cp.start()
            local.append(cp)
            for k, peer in enumerate(peers):
                cp = pltpu.make_async_remote_copy(
                    src_ref=ins[wi] if gather else ins[wi].at[slot[k]], dst_ref=outs[wi].at[me],
                    send_sem=send_sems.at[k, wi], recv_sem=recv_sems.at[k, wi], device_id=peer, device_id_type=MESH)
                cp.start()
                sends.append(cp)
        for wi in range(n):
            for k, peer in enumerate(peers):
                pltpu.make_async_remote_copy(
                    src_ref=outs[wi].at[slot[k]], dst_ref=outs[wi].at[slot[k]], send_sem=send_sems.at[k, wi],
                    recv_sem=recv_sems.at[k, wi], device_id=peer, device_id_type=MESH).wait_recv()
        for cp in sends:
            cp.wait_send()
        for cp in local:
            cp.wait()

    hbm = pl.BlockSpec(memory_space=pltpu.HBM)
    out_shape = [jax.ShapeDtypeStruct(((N_DEV,) + v.shape) if gather else v.shape, v.dtype) for v in items]
    return pl.pallas_call(
        body, name=name, in_specs=[hbm] * n, out_specs=[hbm] * n, out_shape=out_shape,
        scratch_shapes=[pltpu.SemaphoreType.DMA((len(FLIPS), n)), pltpu.SemaphoreType.DMA((len(FLIPS), n)),
                        pltpu.SemaphoreType.DMA((n,))],
    )(*items)


HBM_SPEC = pl.BlockSpec(memory_space=pltpu.HBM)
SEM_SPEC = pl.BlockSpec(memory_space=pltpu.SEMAPHORE)
EFFECT = pltpu.SideEffectType.DATAFLOW_SIDE_EFFECTING


def _peers():
    x, y, c = lax.axis_index("x"), lax.axis_index("y"), lax.axis_index("c")
    peers = [(jnp.where(fx, 1 - x, x), jnp.where(fy, 1 - y, y), jnp.where(fc, 1 - c, c)) for fx, fy, fc in FLIPS]
    return 4 * x + 2 * y + c, peers, [4 * px + 2 * py + pc for px, py, pc in peers]


def _split_start(bufs, ncopies, plan, *, name):
    nb = len(bufs)

    def body(*refs):
        send_sems, recv_sems, token = refs[nb], refs[nb + 1], refs[2 * nb + 2]
        for i, (src, dst, peer, _) in enumerate(plan(refs[:nb])):
            pltpu.make_async_remote_copy(src_ref=src, dst_ref=dst, send_sem=send_sems.at[i], recv_sem=recv_sems.at[i],
                                         device_id=peer, device_id_type=MESH).start()
        token[...] = jnp.zeros_like(token)

    res = pl.pallas_call(
        body, name=name, in_specs=[HBM_SPEC] * nb,
        out_specs=[SEM_SPEC, SEM_SPEC] + [HBM_SPEC] * nb + [pl.BlockSpec(memory_space=pltpu.VMEM)],
        out_shape=[pltpu.SemaphoreType.DMA((ncopies,)), pltpu.SemaphoreType.DMA((ncopies,))]
        + [pltpu.HBM(v.shape, v.dtype) for v in bufs] + [jax.ShapeDtypeStruct((HALO, LANES), F32)],
        input_output_aliases={i: 2 + i for i in range(nb)},
        compiler_params=pltpu.CompilerParams(has_side_effects=EFFECT),
    )(*[pltpu.with_memory_space_constraint(v, pltpu.HBM) for v in bufs])
    return (res[0], res[1], list(res[2:2 + nb])), res[2 + nb]


def _split_wait(started, after, plan, local_plan, *, name):
    send_sems, recv_sems, bufs = started
    nb = len(bufs)
    nlocal = len(local_plan(bufs))

    def body(*refs):
        send_sems, recv_sems = refs[nb], refs[nb + 1]
        local_sems = refs[2 * nb + 3]
        local = []
        for j, (src, dst) in enumerate(local_plan(refs[:nb])):
            cp = pltpu.make_async_copy(src, dst, local_sems.at[j])
            cp.start()
            local.append(cp)
        for i, (src, _, peer, incoming) in enumerate(plan(refs[:nb])):
            cp = pltpu.make_async_remote_copy(src_ref=src, dst_ref=incoming, send_sem=send_sems.at[i],
                                              recv_sem=recv_sems.at[i], device_id=peer, device_id_type=MESH)
            cp.wait_send()
            cp.wait_recv()
        for cp in local:
            cp.wait()

    res = pl.pallas_call(
        body, name=name, in_specs=[HBM_SPEC] * nb + [SEM_SPEC, SEM_SPEC, pl.BlockSpec(memory_space=pl.ANY)],
        out_specs=[HBM_SPEC] * nb, out_shape=[pltpu.HBM(v.shape, v.dtype) for v in bufs],
        input_output_aliases={i: i for i in range(nb)},
        scratch_shapes=[pltpu.SemaphoreType.DMA((max(nlocal, 1),))],
        compiler_params=pltpu.CompilerParams(has_side_effects=EFFECT),
    )(*bufs, send_sems, recv_sems, after)
    return list(res)


def _place():
    x, y, c = lax.axis_index("x"), lax.axis_index("y"), lax.axis_index("c")
    others = [((1 - x, y, c), 2 * (1 - x) + y), ((x, 1 - y, c), 2 * x + 1 - y), ((1 - x, 1 - y, c), 2 * (1 - x) + 1 - y)]
    return 4 * x + 2 * y + c, 2 * x + y, c, (x, y, 1 - c), others


def _gather1_plan(n):
    def plan(refs):
        me, _, _, sibling, others = _place()
        out = []
        for wi in range(n):
            item, land = refs[wi], refs[n + wi]
            out.append((item, land.at[me], sibling, land.at[me + 1 - 2 * lax.axis_index("c")]))
            for peer, chip in others:
                out.append((item, land.at[me], peer, land.at[2 * chip + lax.axis_index("c")]))
        return out

    return plan


def _gather1_local(n):
    def plan(refs):
        me = _place()[0]
        return [(refs[wi], refs[n + wi].at[me]) for wi in range(n)]

    return plan


def _gather2_plan(n):
    def plan(refs):
        _, _, c, sibling, others = _place()
        out = []
        for wi in range(n):
            land = refs[wi]
            for _, chip in others:
                out.append((land.at[2 * chip + c], land.at[2 * chip + c], sibling, land.at[2 * chip + 1 - c]))
        return out

    return plan


def _gather_start(items, *, name):
    lands = [lax.empty((N_DEV,) + v.shape, v.dtype) for v in items]
    return _split_start(items + lands, 4 * len(items), _gather1_plan(len(items)), name=name)


def _gather_forward(started, after, *, name):
    n = len(started[2]) // 2
    bufs = _split_wait(started, after, _gather1_plan(n), _gather1_local(n), name=name + "_wait")
    return _split_start(bufs[n:], 3 * n, _gather2_plan(n), name=name + "_start")


def _gather_finish(started, after, *, name):
    n = len(started[2])
    return _split_wait(started, after, _gather2_plan(n), lambda refs: [], name=name)


def _handshake(peers):
    barrier = pltpu.get_barrier_semaphore()
    for peer in peers:
        pl.semaphore_signal(barrier, inc=1, device_id=peer, device_id_type=MESH)
    pl.semaphore_wait(barrier, len(peers))


def _remote(src, dst, send_sem, recv_sem, peer):
    return pltpu.make_async_remote_copy(src_ref=src, dst_ref=dst, send_sem=send_sem, recv_sem=recv_sem, device_id=peer,
                                        device_id_type=MESH)


def _sequencer_gather(items, *, collective_id, name):
    n = len(items)
    srcs = [jax.new_ref(v, memory_space=pltpu.MemorySpace.HBM) for v in items]
    lands = [jax.empty_ref(jax.ShapeDtypeStruct((N_DEV,) + v.shape, v.dtype), memory_space=pltpu.MemorySpace.HBM)
             for v in items]
    dma = pltpu.SemaphoreType.DMA

    @pl.kernel(mesh=plsc.ScalarSubcoreMesh(axis_name="sequencer", num_cores=1), name=name,
               scratch_types=(dma((4 * n,)), dma((4 * n,)), dma((3 * n,)), dma((3 * n,)), dma((n,))),
               compiler_params=pltpu.CompilerParams(collective_id=collective_id))
    def launch(send1, recv1, send2, recv2, local_sems):
        _, _, _, sibling, others = _place()
        _handshake([sibling] + [peer for peer, _ in others])
        hop1 = _gather1_plan(n)(srcs + lands)
        hop2 = _gather2_plan(n)(lands)
        local = [pltpu.make_async_copy(src, dst, local_sems.at[j])
                 for j, (src, dst) in enumerate(_gather1_local(n)(srcs + lands))]
        for cp in local:
            cp.start()
        for i, (src, dst, peer, _) in enumerate(hop1):
            _remote(src, dst, send1.at[i], recv1.at[i], peer).start()
        for wi in range(n):
            for j in range(3):
                i1, i2 = 4 * wi + 1 + j, 3 * wi + j
                src, _, peer, incoming = hop1[i1]
                _remote(src, incoming, send1.at[i1], recv1.at[i1], peer).wait_recv()
                src, dst, peer, _ = hop2[i2]
                _remote(src, dst, send2.at[i2], recv2.at[i2], peer).start()
        for wi in range(n):
            src, _, peer, incoming = hop1[4 * wi]
            _remote(src, incoming, send1.at[4 * wi], recv1.at[4 * wi], peer).wait_recv()
        for i, (src, _, peer, incoming) in enumerate(hop2):
            cp = _remote(src, incoming, send2.at[i], recv2.at[i], peer)
            cp.wait_send()
            cp.wait_recv()
        for i, (src, dst, peer, _) in enumerate(hop1):
            _remote(src, dst, send1.at[i], recv1.at[i], peer).wait_send()
        for cp in local:
            cp.wait()

    launch()
    return [land[...] for land in lands]


def _sequencer_exchange(sources, land_shapes, ncopies, plan, local_plan, peers, *, collective_id, name):
    srcs = [jax.new_ref(v, memory_space=pltpu.MemorySpace.HBM) for v in sources]
    lands = [jax.empty_ref(s, memory_space=pltpu.MemorySpace.HBM) for s in land_shapes]
    nlocal = len(local_plan(srcs + lands))
    dma = pltpu.SemaphoreType.DMA

    @pl.kernel(mesh=plsc.ScalarSubcoreMesh(axis_name="sequencer", num_cores=1), name=name,
               scratch_types=(dma((ncopies,)), dma((ncopies,)), dma((max(nlocal, 1),))),
               compiler_params=pltpu.CompilerParams(collective_id=collective_id))
    def launch(send_sems, recv_sems, local_sems):
        _handshake(peers(_place()))
        copies = plan(srcs + lands)
        local = [pltpu.make_async_copy(src, dst, local_sems.at[j])
                 for j, (src, dst) in enumerate(local_plan(srcs + lands))]
        for cp in local:
            cp.start()
        for i, (src, dst, peer, _) in enumerate(copies):
            _remote(src, dst, send_sems.at[i], recv_sems.at[i], peer).start()
        for i, (src, _, peer, incoming) in enumerate(copies):
            cp = _remote(src, incoming, send_sems.at[i], recv_sems.at[i], peer)
            cp.wait_send()
            cp.wait_recv()
        for cp in local:
            cp.wait()

    launch()
    return [land[...] for land in lands]


def _sequencer_scatter_hop2(sums, *, collective_id, name):
    n = len(sums)
    shapes = [jax.ShapeDtypeStruct(v.shape, v.dtype) for v in sums]
    return _sequencer_exchange(sums, shapes, 3 * n, _scatter2_plan(n), _scatter2_local(n),
                               lambda place: [peer for peer, _ in place[4]], collective_id=collective_id, name=name)


N_CHIP = N_DEV // 2


def _scatter1_plan(n):
    def plan(refs):
        _, _, c, sibling, _ = _place()
        out = []
        for wi in range(n):
            parts, half = refs[wi], refs[n + wi]
            for chip in range(N_CHIP):
                out.append((parts.at[2 * chip + 1 - c], half.at[chip], sibling, half.at[chip]))
        return out

    return plan


def _scatter2_plan(n):
    def plan(refs):
        _, my_chip, _, _, others = _place()
        out = []
        for wi in range(n):
            sums, recv = refs[wi], refs[n + wi]
            for peer, chip in others:
                out.append((sums.at[chip], recv.at[my_chip], peer, recv.at[chip]))
        return out

    return plan


def _scatter2_local(n):
    def plan(refs):
        my_chip = _place()[1]
        return [(refs[wi].at[my_chip], refs[n + wi].at[my_chip]) for wi in range(n)]

    return plan


def _pair_add(parts, half, core, *, name):
    _, r, c = parts.shape
    tr = max(d for d in range(HALO, 257, HALO) if r % d == 0) if r > 256 else r
    parts4 = parts.reshape(N_CHIP, 2, r, c)

    def body(core_ref, p_ref, h_ref, o_ref):
        o_ref[...] = (p_ref[:, 0].astype(F32) + h_ref[...].astype(F32)).astype(o_ref.dtype)

    return pl.pallas_call(
        body, name=name,
        grid_spec=pltpu.PrefetchScalarGridSpec(
            num_scalar_prefetch=1, grid=(r // tr,),
            in_specs=[pl.BlockSpec((N_CHIP, 1, tr, c), lambda i, core_ref: (0, core_ref[0], i, 0)),
                      pl.BlockSpec((N_CHIP, tr, c), lambda i, core_ref: (0, i, 0))],
            out_specs=pl.BlockSpec((N_CHIP, tr, c), lambda i, core_ref: (0, i, 0))),
        out_shape=jax.ShapeDtypeStruct((N_CHIP, r, c), parts.dtype), compiler_params=_cp("parallel"),
    )(core, parts4, half)


def _scatter_start(parts, *, name):
    halves = [lax.empty((N_CHIP,) + v.shape[1:], v.dtype) for v in parts]
    return _split_start(parts + halves, N_CHIP * len(parts), _scatter1_plan(len(parts)), name=name)


def _adamw(parts, w, m, v, *, name):
    r, c = w.shape
    nparts = parts.shape[0]
    tr = max(d for d in range(HALO, 129, HALO) if r % d == 0) if r > 128 else r

    def body(p_ref, w_ref, m_ref, v_ref, g_ref, d_ref, mo_ref, vo_ref):
        g = p_ref[0].astype(F32)
        for k in range(1, nparts):
            g = g + p_ref[k].astype(F32)
        mn = ADAM_B1 * m_ref[...] + (1.0 - ADAM_B1) * g
        vn = ADAM_B2 * v_ref[...] + (1.0 - ADAM_B2) * (g * g)
        m_hat = mn / (1.0 - ADAM_B1 ** ADAM_STEP)
        v_hat = vn / (1.0 - ADAM_B2 ** ADAM_STEP)
        g_ref[...] = g
        d_ref[...] = -ADAM_LR * (m_hat / (jnp.sqrt(v_hat) + ADAM_EPS) + ADAM_WD * w_ref[...])
        mo_ref[...] = mn
        vo_ref[...] = vn

    blk = pl.BlockSpec((tr, c), lambda i: (i, 0))
    return pl.pallas_call(
        body, name=name, grid=(r // tr,), in_specs=[pl.BlockSpec((nparts, tr, c), lambda i: (0, i, 0)), blk, blk, blk],
        out_specs=[blk] * 4, out_shape=[jax.ShapeDtypeStruct((r, c), F32)] * 4, compiler_params=_cp("parallel"),
    )(parts, w, m, v)


def _pack_rows(vs, rows):
    lead = vs[0].shape[:-1] if vs[0].ndim > 1 else ()
    flat = jnp.concatenate(vs, axis=-1)
    pad = rows * LANES - flat.shape[-1]
    flat = jnp.pad(flat, [(0, 0)] * len(lead) + [(0, pad)])
    return flat.reshape(lead + (rows, LANES))


def kernel(x, p, positions, mix_norm_w, w_in, conv_w, conv_b, dt_bias, a_log, d_skip, ssd_norm_w, q_a_norm_w, w_q_b, kv_a_norm_w, w_kv_b, w_out, ffn_norm_w, w_ffn_up, ffn_conv_w, ffn_conv_b, w_ffn_down, ple_norm_w, w_ple_gate, b_ple_gate, w_ple_proj, ple_post_norm_w, final_norm_w, loss_target, m_mix_norm_w, m_w_in, m_conv_w, m_conv_b, m_dt_bias, m_a_log, m_d_skip, m_ssd_norm_w, m_q_a_norm_w, m_w_q_b, m_kv_a_norm_w, m_w_kv_b, m_w_out, m_ffn_norm_w, m_w_ffn_up, m_ffn_conv_w, m_ffn_conv_b, m_w_ffn_down, m_ple_norm_w, m_w_ple_gate, m_b_ple_gate, m_w_ple_proj, m_ple_post_norm_w, m_final_norm_w, v_mix_norm_w, v_w_in, v_conv_w, v_conv_b, v_dt_bias, v_a_log, v_d_skip, v_ssd_norm_w, v_q_a_norm_w, v_w_q_b, v_kv_a_norm_w, v_w_kv_b, v_w_out, v_ffn_norm_w, v_w_ffn_up, v_ffn_conv_w, v_ffn_conv_b, v_w_ffn_down, v_ple_norm_w, v_w_ple_gate, v_b_ple_gate, v_w_ple_proj, v_ple_post_norm_w, v_final_norm_w):
    given = dict(locals())
    shapes = {n: given[n].shape for n in WEIGHTS}
    w2 = {n: given[n].reshape(given[n].shape[-2:] if n in BIG or n in CONV else (1, -1)) for n in WEIGHTS}
    m2 = {n: given['m_' + n].reshape(w2[n].shape) for n in WEIGHTS}
    v2 = {n: given['v_' + n].reshape(w2[n].shape) for n in WEIGHTS}
    me = 4 * lax.axis_index("x") + 2 * lax.axis_index("y") + lax.axis_index("c")

    core = lax.axis_index("c").astype(jnp.int32).reshape(1)

    def shards(grp, zero):
        return [(w2[n] + zero).astype(BF16) if n in BIG else w2[n] + zero for n in WEIGHT_GROUPS[grp]]

    first, token = _gather_start(shards('a', 0.0), name="gather_a_hop1")
    first, token = _gather_forward(first, token, name="gather_a_hop2")
    zero = token[0, 0]
    later = _sequencer_gather(shards('b', zero) + shards('c', zero), collective_id=1, name="gather_later")
    later = dict(zip(WEIGHT_GROUPS['b'] + WEIGHT_GROUPS['c'], later))

    def get_w(grp, after):
        if grp == 'a':
            lands = dict(zip(WEIGHT_GROUPS[grp], _gather_finish(first, token, name="gather_a_done")))
        else:
            lands = {n: later[n] for n in WEIGHT_GROUPS[grp]}
        return _assemble_weights(lands)

    scatters = {}

    hop_ids = {grp: 2 + 2 * i for i, grp in enumerate(GRAD_GROUPS)}

    def zero_of(arrays):
        return sum(v[(0,) * v.ndim].astype(F32) * 0.0 for v in arrays)

    def emit(grp, grads):
        scatters[grp], tok = _scatter_start([grads[n] for n in GRAD_GROUPS[grp]], name="scatter_" + grp + "_hop1")
        return tok[0, 0]

    def relay(grp, after):
        n = len(GRAD_GROUPS[grp])
        bufs = _split_wait(scatters[grp], after, _scatter1_plan(n), lambda refs: [], name="scatter_" + grp + "_hop1_wait")
        sums = [_pair_add(bufs[i], bufs[n + i], core, name="scatter_%s_add%d" % (grp, i)) for i in range(n)]
        scatters[grp] = _sequencer_scatter_hop2(sums, collective_id=hop_ids[grp] + 1, name="scatter_" + grp + "_hop2")
        return zero_of(sums)

    out_g, out_d, out_m, out_v = {}, {}, {}, {}

    def settle(grp):
        return zero_of(scatters[grp])

    def update(grp, behind=None):
        for n, parts in zip(GRAD_GROUPS[grp], scatters[grp]):
            wn = w2[n] if behind is None else w2[n] + behind
            out_g[n], out_d[n], out_m[n], out_v[n] = _adamw(parts, wn, m2[n], v2[n], name="adamw_" + n)

    vecs = {n: w2[n] for n in REPL}
    vecs['mix_norm_w'] = vecs['mix_norm_w'] + zero
    loss, dx, g_conv, g_vec = _local_step(x[0], p[0, 0], _rope_tables(positions), get_w, vecs, loss_target[0], emit,
                                          relay, settle)
    n_small = sum(g_vec[n].shape[1] for n in REPL) + sum(g_conv[n].size for n in CONV) + 1
    rows_small = -(-n_small // (LANES * HALO)) * HALO
    small = _pack_rows([g_vec[n] for n in REPL] + [g_conv[n].reshape(1, -1) for n in CONV] + [loss], rows_small)

    for grp in list(GRAD_GROUPS)[:-1]:
        update(grp)
    all_small = _exchange([small], gather=True, name="gather_small_grads")[0].reshape(N_DEV, rows_small * LANES)
    update(list(GRAD_GROUPS)[-1], zero_of([all_small]))
    pieces, off = [], 0
    for n in REPL:
        k = g_vec[n].shape[1]
        pieces.append(all_small[:, off:off + k])
        off += k
    for n in CONV:
        kw, cols = g_conv[n].shape
        full = all_small[:, off:off + kw * cols].reshape(N_DEV, kw, cols)
        mine = lax.dynamic_slice_in_dim(full, me * (cols // N_DEV), cols // N_DEV, axis=2)
        pieces.append(mine.reshape(N_DEV, kw * (cols // N_DEV)))
        off += kw * cols
    pieces.append(all_small[:, off:off + 1])
    small_names = REPL + CONV
    n_mine = sum(q.shape[1] for q in pieces)
    rows_mine = -(-n_mine // (LANES * HALO)) * HALO
    zero = jnp.zeros((1, 1), F32)
    packed = [_pack_rows([src[n].reshape(1, -1) for n in small_names] + [zero], rows_mine).reshape(rows_mine, LANES)
              for src in (w2, m2, v2)]
    sg, sd, sm, sv = _adamw(_pack_rows(pieces, rows_mine), *packed, name="adamw_small")
    off = 0
    for n in small_names:
        k = w2[n].size
        for dst, src in ((out_g, sg), (out_d, sd), (out_m, sm), (out_v, sv)):
            dst[n] = src.reshape(-1)[off:off + k].reshape(w2[n].shape)
        off += k
    total_loss = sg.reshape(-1)[off]

    outs = [total_loss, dx[None]]
    for res in (out_g, out_d, out_m, out_v):
        outs += [res[n].reshape(shapes[n]) for n in WEIGHTS]
    return tuple(outs)
```

```python
import functools
import math

import numpy as np
import jax
import jax.numpy as jnp
from jax import lax
from jax.experimental import pallas as pl
from jax.experimental.pallas import tpu as pltpu
from jax.experimental.pallas import tpu_sc as plsc

F32 = jnp.float32
BF16 = jnp.bfloat16
HI = lax.Precision.HIGHEST

D_MODEL = 2048
CHUNK = 64
D_SSM = 1024
SSD_P = 64
SSD_HEADS = 16
SSD_GROUPS = 2
SSD_N = 128
SSD_CONV = 4
SSD_CONV_DIM = D_SSM + 2 * SSD_GROUPS * SSD_N
MLA_HEADS = 8
MLA_NOPE = 128
MLA_ROPE = 64
MLA_V = 128
MLA_Q_RANK = 512
MLA_KV_RANK = 256
MLA_QK_PAD = 256
ROPE_THETA = 10000.0
D_FF = 5632
FFN_CONV = 3
PLE_DIM = 256
NORM_EPS = 1e-6
ADAM_LR, ADAM_B1, ADAM_B2, ADAM_EPS, ADAM_WD, ADAM_STEP = 0.001, 0.9, 0.999, 1e-08, 0.01, 10
N_DEV = 8

OFF_Z, OFF_XBC, OFF_QA, OFF_CKV, OFF_KR, OFF_DT, D_IN_PAD = 0, 1024, 2560, 3072, 3328, 3456, 3584
D_IN = 3408
LANES = 128
HALO = 8
VMEM_LIMIT = 56 * 1024 * 1024
FFN_TC = D_FF * 2 // N_DEV
FFN_PERM = (0, 4, 1, 5, 2, 6, 3, 7)
NEG = -1e30


def _cp(*sem):
    return pltpu.CompilerParams(dimension_semantics=tuple(sem), vmem_limit_bytes=VMEM_LIMIT)


def _tile(n, want):
    if n <= want:
        return n
    best = max(d for d in range(LANES, want + 1, LANES) if n % d == 0)
    return best


def _sigmoid(x):
    return 0.5 * (jnp.tanh(0.5 * x) + 1.0)


def _silu(x):
    return x * _sigmoid(x)


def _dsilu(x):
    s = _sigmoid(x)
    return s * (1.0 + x * (1.0 - s))


MM_TILE = 1408
MM_TK = 2816


def _matmul(a, b, *, ta=False, tb=False, out_dtype=F32, add=None, bias=None, tm=MM_TILE, tn=MM_TILE, tk=MM_TK, name,
            mnk=None, a_spec=None, b_spec=None, o_spec=None, o_shape=None):
    if mnk is None:
        m, k = (a.shape[1], a.shape[0]) if ta else a.shape
        n = b.shape[0] if tb else b.shape[1]
        assert k == (b.shape[1] if tb else b.shape[0])
    else:
        m, n, k = mnk
    tm, tn, tk = _tile(m, tm), _tile(n, tn), _tile(k, tk)
    nk = k // tk
    dims = (((0 if ta else 1,), (1 if tb else 0,)), ((), ()))

    def body(*refs):
        a_ref, b_ref = refs[0], refs[1]
        pos = 2
        add_ref = bias_ref = None
        if add is not None:
            add_ref = refs[pos]
            pos += 1
        if bias is not None:
            bias_ref = refs[pos]
            pos += 1
        o_ref = refs[pos]
        kk = pl.program_id(2)
        av = a_ref[...]
        bv = b_ref[...]
        av = av.reshape(av.shape[-2:]).astype(BF16)
        bv = bv.reshape(bv.shape[-2:]).astype(BF16)
        prod = lax.dot_general(av, bv, dims, preferred_element_type=F32)

        def finish(r):
            if bias_ref is not None:
                r = r + bias_ref[...]
            if add_ref is not None:
                r = r + add_ref[...].astype(F32)
            o_ref[...] = r.astype(out_dtype).reshape(o_ref.shape)

        if nk == 1:
            finish(prod)
        else:
            acc_ref = refs[pos + 1]

            @pl.when(kk == 0)
            def _():
                acc_ref[...] = prod

            @pl.when(kk > 0)
            def _():
                acc_ref[...] += prod

            @pl.when(kk == nk - 1)
            def _():
                finish(acc_ref[...])

    if a_spec is None:
        a_spec = (pl.BlockSpec((tk, tm), lambda i, j, kk: (kk, i)) if ta
                  else pl.BlockSpec((tm, tk), lambda i, j, kk: (i, kk)))
    if b_spec is None:
        b_spec = (pl.BlockSpec((tn, tk), lambda i, j, kk: (j, kk)) if tb
                  else pl.BlockSpec((tk, tn), lambda i, j, kk: (kk, j)))
    if o_spec is None:
        o_spec = pl.BlockSpec((tm, tn), lambda i, j, kk: (i, j))
    if o_shape is None:
        o_shape = (m, n)
    in_specs = [a_spec, b_spec]
    args = [a, b]
    if add is not None:
        in_specs.append(pl.BlockSpec((tm, tn), lambda i, j, kk: (i, j)))
        args.append(add)
    if bias is not None:
        in_specs.append(pl.BlockSpec((1, tn), lambda i, j, kk: (0, j)))
        args.append(bias)
    return pl.pallas_call(
        body, name=name, grid=(m // tm, n // tn, nk), in_specs=in_specs, out_specs=o_spec,
        out_shape=jax.ShapeDtypeStruct(o_shape, out_dtype),
        scratch_shapes=[pltpu.VMEM((tm, tn), F32)] if nk > 1 else [],
        compiler_params=_cp("parallel", "parallel", "arbitrary"),
    )(*args)


def _rmsnorm_fwd(x, w, *, width, cblk=0, out_dtype=BF16, tr=256, name):
    t = x.shape[0]

    def body(x_ref, w_ref, o_ref):
        xv = x_ref[...].astype(F32)
        r = lax.rsqrt(jnp.mean(xv * xv, axis=-1, keepdims=True) + NORM_EPS)
        o_ref[...] = (xv * r * w_ref[...]).astype(out_dtype)

    return pl.pallas_call(
        body, name=name, grid=(t // tr,),
        in_specs=[pl.BlockSpec((tr, width), lambda i: (i, cblk)), pl.BlockSpec((1, width), lambda i: (0, 0))],
        out_specs=pl.BlockSpec((tr, width), lambda i: (i, 0)),
        out_shape=jax.ShapeDtypeStruct((t, width), out_dtype),
        compiler_params=_cp("parallel"),
    )(x, w)


def _rmsnorm_bwd(x, w, dy, add=None, *, width, cblk=0, out_dtype=F32, tr=256, name):
    t = x.shape[0]

    def body(*refs):
        if add is None:
            x_ref, w_ref, dy_ref, dx_ref, dw_ref = refs
            add_ref = None
        else:
            x_ref, w_ref, dy_ref, add_ref, dx_ref, dw_ref = refs
        xv = x_ref[...].astype(F32)
        dyv = dy_ref[...].astype(F32)
        r = lax.rsqrt(jnp.mean(xv * xv, axis=-1, keepdims=True) + NORM_EPS)
        xh = xv * r
        g = dyv * w_ref[...]
        dx = r * (g - xh * jnp.mean(g * xh, axis=-1, keepdims=True))
        if add_ref is not None:
            dx = dx + add_ref[...].astype(F32)
        dx_ref[...] = dx.astype(out_dtype)

        @pl.when(pl.program_id(0) == 0)
        def _():
            dw_ref[...] = jnp.zeros_like(dw_ref)

        dw_ref[...] += jnp.sum(dyv * xh, axis=0, keepdims=True)

    in_specs = [pl.BlockSpec((tr, width), lambda i: (i, cblk)), pl.BlockSpec((1, width), lambda i: (0, 0)),
                pl.BlockSpec((tr, width), lambda i: (i, 0))]
    args = [x, w, dy]
    if add is not None:
        in_specs.append(pl.BlockSpec((tr, width), lambda i: (i, 0)))
        args.append(add)
    return pl.pallas_call(
        body, name=name, grid=(t // tr,), in_specs=in_specs,
        out_specs=[pl.BlockSpec((tr, width), lambda i: (i, 0)), pl.BlockSpec((1, width), lambda i: (0, 0))],
        out_shape=[jax.ShapeDtypeStruct((t, width), out_dtype), jax.ShapeDtypeStruct((1, width), F32)],
        compiler_params=_cp("arbitrary"),
    )(*args)


def _shift_down(prev_halo, cur, j):
    if j == 0:
        return cur
    ext = jnp.concatenate([prev_halo, cur], axis=0)
    return pltpu.roll(ext, j, axis=0)[HALO:]


def _shift_up(cur, next_halo, j):
    if j == 0:
        return cur
    ext = jnp.concatenate([cur, next_halo], axis=0)
    return pltpu.roll(ext, ext.shape[0] - j, axis=0)[:cur.shape[0]]


def _conv_rows(prev, cur, w, b, kw):
    shifted = [cur]
    out = b + w[kw - 1:kw] * cur
    for j in range(1, kw):
        sh = _shift_down(prev, cur, j)
        shifted.append(sh)
        out = out + w[kw - 1 - j:kw - j] * sh
    return out, shifted


def _act_fwd(c, glu):
    if glu:
        half = c.shape[1] // 2
        return _silu(c[:, :half]) * c[:, half:]
    return _silu(c)


def _act_bwd(c, dout, glu):
    if glu:
        half = c.shape[1] // 2
        g, up = c[:, :half], c[:, half:]
        s = _sigmoid(g)
        gs = g * s
        return jnp.concatenate([dout * up * (s + gs * (1.0 - s)), dout * gs], axis=1)
    return dout * _dsilu(c)


def _conv_act_fwd(u, w, b, *, kw, glu, tc, coff, ncols, out_dtype, tr=256, name):
    t = u.shape[0]
    nb = ncols // tc
    oc = tc // 2 if glu else tc

    def body(u_ref, uh_ref, w_ref, b_ref, o_ref):
        prev = jnp.where(pl.program_id(0) == 0, 0.0, uh_ref[...])
        c, _ = _conv_rows(prev, u_ref[...], w_ref[...], b_ref[...], kw)
        o_ref[...] = _act_fwd(c, glu).astype(out_dtype)

    return pl.pallas_call(
        body, name=name, grid=(t // tr, nb),
        in_specs=[pl.BlockSpec((tr, tc), lambda i, j: (i, j + coff)),
                  pl.BlockSpec((HALO, tc), lambda i, j: (jnp.maximum(i * (tr // HALO) - 1, 0), j + coff)),
                  pl.BlockSpec((kw, tc), lambda i, j: (0, j)), pl.BlockSpec((1, tc), lambda i, j: (0, j))],
        out_specs=pl.BlockSpec((tr, oc), lambda i, j: (i, j)),
        out_shape=jax.ShapeDtypeStruct((t, nb * oc), out_dtype),
        compiler_params=_cp("parallel", "parallel"),
    )(u, u, w, b)


def _conv_act_bwd(u, w, b, dout, *, kw, glu, tc, coff, ncols, tr=256, name):
    t = u.shape[0]
    nb = ncols // tc
    nt = t // tr
    oc = tc // 2 if glu else tc

    def body(u_ref, up_ref, un_ref, d_ref, dn_ref, w_ref, b_ref, du_ref, dw_ref, db_ref):
        i = pl.program_id(1)
        cur, nxt, wv, bv = u_ref[...], un_ref[...], w_ref[...], b_ref[...]
        prev = jnp.where(i == 0, 0.0, up_ref[...])
        c_cur, shifted = _conv_rows(prev, cur, wv, bv, kw)
        c_nxt, _ = _conv_rows(cur[tr - HALO:], nxt, wv, bv, kw)
        d_cur = _act_bwd(c_cur, d_ref[...].astype(F32), glu)
        d_nxt = _act_bwd(c_nxt, jnp.where(i == nt - 1, 0.0, dn_ref[...].astype(F32)), glu)
        du = wv[kw - 1:kw] * d_cur
        for j in range(1, kw):
            du = du + wv[kw - 1 - j:kw - j] * _shift_up(d_cur, d_nxt, j)
        du_ref[...] = du.astype(BF16)

        @pl.when(i == 0)
        def _():
            dw_ref[...] = jnp.zeros_like(dw_ref)
            db_ref[...] = jnp.zeros_like(db_ref)

        db_ref[...] += jnp.sum(d_cur, axis=0, keepdims=True)
        dw_ref[...] += jnp.concatenate(
            [jnp.sum(d_cur * shifted[kw - 1 - k], axis=0, keepdims=True) for k in range(kw)], axis=0)

    nh = tr // HALO
    return pl.pallas_call(
        body, name=name, grid=(nb, nt),
        in_specs=[pl.BlockSpec((tr, tc), lambda j, i: (i, j + coff)),
                  pl.BlockSpec((HALO, tc), lambda j, i: (jnp.maximum(i * nh - 1, 0), j + coff)),
                  pl.BlockSpec((HALO, tc), lambda j, i: (jnp.minimum((i + 1) * nh, t // HALO - 1), j + coff)),
                  pl.BlockSpec((tr, oc), lambda j, i: (i, j)),
                  pl.BlockSpec((HALO, oc), lambda j, i: (jnp.minimum((i + 1) * nh, t // HALO - 1), j)),
                  pl.BlockSpec((kw, tc), lambda j, i: (0, j)), pl.BlockSpec((1, tc), lambda j, i: (0, j))],
        out_specs=[pl.BlockSpec((tr, tc), lambda j, i: (i, j)), pl.BlockSpec((kw, tc), lambda j, i: (0, j)),
                   pl.BlockSpec((1, tc), lambda j, i: (0, j))],
        out_shape=[jax.ShapeDtypeStruct((t, ncols), BF16), jax.ShapeDtypeStruct((kw, ncols), F32),
                   jax.ShapeDtypeStruct((1, ncols), F32)],
        compiler_params=_cp("parallel", "arbitrary"),
    )(u, u, u, dout, dout, w, b)


def _ple_fwd(x2, gl, pe, pw, *, tr=256, name):
    t, d = x2.shape

    def body(x_ref, gl_ref, pe_ref, pw_ref, o_ref):
        pv = pe_ref[...]
        r = lax.rsqrt(jnp.mean(pv * pv, axis=-1, keepdims=True) + NORM_EPS)
        o_ref[...] = x_ref[...] + _sigmoid(gl_ref[...]) * (pv * r * pw_ref[...])

    blk = pl.BlockSpec((tr, d), lambda i: (i, 0))
    return pl.pallas_call(
        body, name=name, grid=(t // tr,), in_specs=[blk, blk, blk, pl.BlockSpec((1, d), lambda i: (0, 0))],
        out_specs=blk, out_shape=jax.ShapeDtypeStruct((t, d), F32), compiler_params=_cp("parallel"),
    )(x2, gl, pe, pw)


def _ple_bwd(dx3, gl, pe, pw, *, tr=256, name):
    t, d = dx3.shape

    def body(dx_ref, gl_ref, pe_ref, pw_ref, dgl_ref, db_ref, dpe_ref, dpw_ref):
        dx, pv, pwv = dx_ref[...], pe_ref[...], pw_ref[...]
        gate = _sigmoid(gl_ref[...])
        r = lax.rsqrt(jnp.mean(pv * pv, axis=-1, keepdims=True) + NORM_EPS)
        ph = pv * r
        dgl = dx * (ph * pwv) * gate * (1.0 - gate)
        de = dx * gate
        g = de * pwv
        dgl_ref[...] = dgl.astype(BF16)
        dpe_ref[...] = (r * (g - ph * jnp.mean(g * ph, axis=-1, keepdims=True))).astype(BF16)

        @pl.when(pl.program_id(0) == 0)
        def _():
            db_ref[...] = jnp.zeros_like(db_ref)
            dpw_ref[...] = jnp.zeros_like(dpw_ref)

        db_ref[...] += jnp.sum(dgl, axis=0, keepdims=True)
        dpw_ref[...] += jnp.sum(de * ph, axis=0, keepdims=True)

    blk = pl.BlockSpec((tr, d), lambda i: (i, 0))
    row = pl.BlockSpec((1, d), lambda i: (0, 0))
    return pl.pallas_call(
        body, name=name, grid=(t // tr,), in_specs=[blk, blk, blk, row], out_specs=[blk, row, blk, row],
        out_shape=[jax.ShapeDtypeStruct((t, d), BF16), jax.ShapeDtypeStruct((1, d), F32),
                   jax.ShapeDtypeStruct((t, d), BF16), jax.ShapeDtypeStruct((1, d), F32)],
        compiler_params=_cp("arbitrary"),
    )(dx3, gl, pe, pw)


def _loss_head(x3, fw, target, *, tr=256, name):
    t, d = x3.shape

    def body(x_ref, w_ref, t_ref, l_ref, dx_ref, dw_ref):
        xv, wv = x_ref[...], w_ref[...]
        r = lax.rsqrt(jnp.mean(xv * xv, axis=-1, keepdims=True) + NORM_EPS)
        xh = xv * r
        err = xh * wv - t_ref[...]
        dy = err * (1.0 / d)
        g = dy * wv
        dx_ref[...] = r * (g - xh * jnp.mean(g * xh, axis=-1, keepdims=True))

        @pl.when(pl.program_id(0) == 0)
        def _():
            l_ref[...] = jnp.zeros_like(l_ref)
            dw_ref[...] = jnp.zeros_like(dw_ref)

        l_ref[...] += 0.5 * jnp.sum(jnp.mean(err * err, axis=-1, keepdims=True), axis=0, keepdims=True)
        dw_ref[...] += jnp.sum(dy * xh, axis=0, keepdims=True)

    blk = pl.BlockSpec((tr, d), lambda i: (i, 0))
    row = pl.BlockSpec((1, d), lambda i: (0, 0))
    return pl.pallas_call(
        body, name=name, grid=(t // tr,), in_specs=[blk, row, blk],
        out_specs=[pl.BlockSpec((1, 1), lambda i: (0, 0)), blk, row],
        out_shape=[jax.ShapeDtypeStruct((1, 1), F32), jax.ShapeDtypeStruct((t, d), F32),
                   jax.ShapeDtypeStruct((1, d), F32)],
        compiler_params=_cp("arbitrary"),
    )(x3, fw, target)


def _rope(blk, tab_ref):
    return blk * tab_ref[0] + pltpu.roll(blk, 96, axis=1) * tab_ref[1] + pltpu.roll(blk, 32, axis=1) * tab_ref[2]


def _unrope(g, tab_ref):
    return g * tab_ref[0] + pltpu.roll(g * tab_ref[1], 32, axis=1) + pltpu.roll(g * tab_ref[2], 96, axis=1)


def _mla_prep(q, kv, proj, tabs, *, tr=512, name):
    t = q.shape[0]

    def body(q_ref, kv_ref, kr_ref, tab_ref, qo_ref, ko_ref, vo_ref, vt_ref):
        qv, kvv = q_ref[...], kv_ref[...]
        qo_ref[0, :, :MLA_NOPE] = qv[:, :MLA_NOPE].astype(BF16)
        qo_ref[0, :, MLA_NOPE:] = _rope(qv[:, MLA_NOPE:], tab_ref).astype(BF16)
        ko_ref[0, :, :MLA_NOPE] = kvv[:, :MLA_NOPE].astype(BF16)
        ko_ref[0, :, MLA_NOPE:] = _rope(kr_ref[...], tab_ref).astype(BF16)
        vo_ref[0] = kvv[:, MLA_NOPE:].astype(BF16)
        for blk in range(tr // ATT_BLK):
            vt_ref[0, blk] = kvv[blk * ATT_BLK:(blk + 1) * ATT_BLK, MLA_NOPE:].T.astype(BF16)

    return pl.pallas_call(
        body, name=name, grid=(t // tr, MLA_HEADS),
        in_specs=[pl.BlockSpec((tr, MLA_QK_PAD), lambda i, h: (i, h)),
                  pl.BlockSpec((tr, MLA_NOPE + MLA_V), lambda i, h: (i, h)),
                  pl.BlockSpec((tr, LANES), lambda i, h: (i, OFF_KR // LANES)),
                  pl.BlockSpec((3, tr, LANES), lambda i, h: (0, i, 0))],
        out_specs=[pl.BlockSpec((1, tr, MLA_QK_PAD), lambda i, h: (h, i, 0)),
                   pl.BlockSpec((1, tr, MLA_QK_PAD), lambda i, h: (h, i, 0)),
                   pl.BlockSpec((1, tr, MLA_V), lambda i, h: (h, i, 0)),
                   pl.BlockSpec((1, tr // ATT_BLK, MLA_V, ATT_BLK), lambda i, h: (h, i, 0, 0))],
        out_shape=[jax.ShapeDtypeStruct((MLA_HEADS, t, MLA_QK_PAD), BF16),
                   jax.ShapeDtypeStruct((MLA_HEADS, t, MLA_QK_PAD), BF16),
                   jax.ShapeDtypeStruct((MLA_HEADS, t, MLA_V), BF16),
                   jax.ShapeDtypeStruct((MLA_HEADS, t // ATT_BLK, MLA_V, ATT_BLK), BF16)],
        compiler_params=_cp("parallel", "parallel"),
    )(q, kv, proj, tabs)


def _mla_unprep(dq3, dk3, dv3, tabs, *, tr=256, name):
    t = dq3.shape[1]

    def body(dq_ref, dk_ref, dv_ref, tab_ref, qo_ref, kvo_ref, kro_ref):
        kr = jnp.zeros((tr, LANES), F32)
        for h in range(MLA_HEADS):
            c0 = h * MLA_QK_PAD
            qo_ref[:, c0:c0 + MLA_NOPE] = dq_ref[h, :, :MLA_NOPE].astype(BF16)
            qo_ref[:, c0 + MLA_NOPE:c0 + MLA_QK_PAD] = _unrope(dq_ref[h, :, MLA_NOPE:], tab_ref).astype(BF16)
            kvo_ref[:, c0:c0 + MLA_NOPE] = dk_ref[h, :, :MLA_NOPE].astype(BF16)
            kvo_ref[:, c0 + MLA_NOPE:c0 + MLA_QK_PAD] = dv_ref[h].astype(BF16)
            kr = kr + dk_ref[h, :, MLA_NOPE:]
        kro_ref[...] = _unrope(kr, tab_ref).astype(BF16)

    return pl.pallas_call(
        body, name=name, grid=(t // tr,),
        in_specs=[pl.BlockSpec((MLA_HEADS, tr, MLA_QK_PAD), lambda i: (0, i, 0)),
                  pl.BlockSpec((MLA_HEADS, tr, MLA_QK_PAD), lambda i: (0, i, 0)),
                  pl.BlockSpec((MLA_HEADS, tr, MLA_V), lambda i: (0, i, 0)),
                  pl.BlockSpec((3, tr, LANES), lambda i: (0, i, 0))],
        out_specs=[pl.BlockSpec((tr, MLA_HEADS * MLA_QK_PAD), lambda i: (i, 0)),
                   pl.BlockSpec((tr, MLA_HEADS * MLA_QK_PAD), lambda i: (i, 0)),
                   pl.BlockSpec((tr, LANES), lambda i: (i, 0))],
        out_shape=[jax.ShapeDtypeStruct((t, MLA_HEADS * MLA_QK_PAD), BF16),
                   jax.ShapeDtypeStruct((t, MLA_HEADS * MLA_QK_PAD), BF16),
                   jax.ShapeDtypeStruct((t, LANES), BF16)],
        compiler_params=_cp("parallel"),
    )(dq3, dk3, dv3, tabs)


ATT_BLK = 256
ATT_SCALE = 1.0 / math.sqrt(MLA_NOPE + MLA_ROPE)
_NT = (((1,), (1,)), ((), ()))
_TN = (((0,), (0,)), ((), ()))


def _att_scores_t(k, q, diagonal):
    s = lax.dot_general(k, q, _NT, preferred_element_type=F32) * ATT_SCALE
    if not diagonal:
        return s
    key = lax.broadcasted_iota(jnp.int32, s.shape, 0)
    query = lax.broadcasted_iota(jnp.int32, s.shape, 1)
    return jnp.where((key >> 6) <= (query >> 6), s, NEG)


def _att_rows(i):
    return pl.ds(pl.multiple_of(i * ATT_BLK, ATT_BLK), ATT_BLK)


ATT_HEADS = 2


def _attn_fwd(q3, k3, vt4, *, name):
    t = q3.shape[1]
    nq = t // ATT_BLK

    def body(q_ref, k_ref, vt_ref, o_ref, lse_ref):
        qi = pl.program_id(1)
        qs = [q_ref[hh] for hh in range(ATT_HEADS)]

        def step(j, carry, diagonal=False):
            out = []
            for hh, (m, l, acc) in enumerate(carry):
                s = _att_scores_t(k_ref[hh, _att_rows(j), :], qs[hh], diagonal)
                m_new = jnp.maximum(m, jnp.max(s, axis=0, keepdims=True))
                p = jnp.exp(s - m_new)
                alpha = jnp.exp(m - m_new)
                l = alpha * l + jnp.sum(p, axis=0, keepdims=True)
                acc = alpha * acc + jnp.dot(vt_ref[hh, j], p.astype(BF16), preferred_element_type=F32)
                out.append((m_new, l, acc))
            return tuple(out)

        init = tuple((jnp.full((1, ATT_BLK), NEG, F32), jnp.zeros((1, ATT_BLK), F32),
                      jnp.zeros((MLA_V, ATT_BLK), F32)) for _ in range(ATT_HEADS))
        done = step(qi, lax.fori_loop(0, qi, step, init), diagonal=True)
        for hh, (m, l, acc) in enumerate(done):
            o_ref[:, hh * MLA_V:(hh + 1) * MLA_V] = (acc / l).T
            lse_ref[hh, 0] = m + jnp.log(l)

    return pl.pallas_call(
        body, name=name, grid=(MLA_HEADS // ATT_HEADS, nq),
        in_specs=[pl.BlockSpec((ATT_HEADS, ATT_BLK, MLA_QK_PAD), lambda h, i: (h, i, 0)),
                  pl.BlockSpec((ATT_HEADS, t, MLA_QK_PAD), lambda h, i: (h, 0, 0)),
                  pl.BlockSpec((ATT_HEADS, nq, MLA_V, ATT_BLK), lambda h, i: (h, 0, 0, 0))],
        out_specs=[pl.BlockSpec((ATT_BLK, ATT_HEADS * MLA_V), lambda h, i: (i, h)),
                   pl.BlockSpec((ATT_HEADS, 1, 1, ATT_BLK), lambda h, i: (h, i, 0, 0))],
        out_shape=[jax.ShapeDtypeStruct((t, MLA_HEADS * MLA_V), F32),
                   jax.ShapeDtypeStruct((MLA_HEADS, nq, 1, ATT_BLK), F32)],
        compiler_params=_cp("parallel", "parallel"),
    )(q3, k3, vt4)


def _attn_bwd(q3, k3, v3, o, dcat, lse, *, name):
    t = q3.shape[1]
    nq = t // ATT_BLK
    wide = ATT_HEADS * MLA_V

    def body(q_ref, k_ref, v_ref, o_ref, do_ref, lse_ref, dq_ref, dk_ref, dv_ref, delta_ref):
        kj = pl.program_id(1)

        @pl.when(kj == 0)
        def _():
            dq_ref[...] = jnp.zeros_like(dq_ref)
            ones = jnp.ones((HALO, MLA_V), F32)
            for i in range(nq):
                rows = pl.ds(i * ATT_BLK, ATT_BLK)
                prod = o_ref[rows, :] * do_ref[rows, :]
                for hh in range(ATT_HEADS):
                    delta_ref[hh, i] = lax.dot_general(ones, prod[:, hh * MLA_V:(hh + 1) * MLA_V], _NT, precision=HI,
                                                       preferred_element_type=F32)

        def step(i, carry, diagonal=False):
            rows = _att_rows(i)
            out = []
            for hh, (dk, dv) in enumerate(carry):
                k, v = k_ref[hh], v_ref[hh]
                q = q_ref[hh, rows, :]
                dob = do_ref[rows, hh * MLA_V:(hh + 1) * MLA_V].astype(BF16)
                p = jnp.exp(_att_scores_t(k, q, diagonal) - lse_ref[hh, i])
                dv = dv + jnp.dot(p.astype(BF16), dob, preferred_element_type=F32)
                dp = lax.dot_general(v, dob, _NT, preferred_element_type=F32)
                ds = (p * (dp - delta_ref[hh, i, 0:1, :]) * ATT_SCALE).astype(BF16)
                dk = dk + jnp.dot(ds, q, preferred_element_type=F32)
                dq_ref[hh, rows, :] += lax.dot_general(ds, k, _TN, preferred_element_type=F32)
                out.append((dk, dv))
            return tuple(out)

        init = tuple((jnp.zeros((ATT_BLK, MLA_QK_PAD), F32), jnp.zeros((ATT_BLK, MLA_V), F32))
                     for _ in range(ATT_HEADS))
        done = lax.fori_loop(kj + 1, nq, step, step(kj, init, diagonal=True))
        for hh, (dk, dv) in enumerate(done):
            dk_ref[hh] = dk
            dv_ref[hh] = dv

    return pl.pallas_call(
        body, name=name, grid=(MLA_HEADS // ATT_HEADS, nq),
        in_specs=[pl.BlockSpec((ATT_HEADS, t, MLA_QK_PAD), lambda h, j: (h, 0, 0)),
                  pl.BlockSpec((ATT_HEADS, ATT_BLK, MLA_QK_PAD), lambda h, j: (h, j, 0)),
                  pl.BlockSpec((ATT_HEADS, ATT_BLK, MLA_V), lambda h, j: (h, j, 0)),
                  pl.BlockSpec((t, wide), lambda h, j: (0, h)),
                  pl.BlockSpec((t, wide), lambda h, j: (0, MLA_HEADS // ATT_HEADS + h)),
                  pl.BlockSpec((ATT_HEADS, nq, 1, ATT_BLK), lambda h, j: (h, 0, 0, 0))],
        out_specs=[pl.BlockSpec((ATT_HEADS, t, MLA_QK_PAD), lambda h, j: (h, 0, 0)),
                   pl.BlockSpec((ATT_HEADS, ATT_BLK, MLA_QK_PAD), lambda h, j: (h, j, 0)),
                   pl.BlockSpec((ATT_HEADS, ATT_BLK, MLA_V), lambda h, j: (h, j, 0))],
        out_shape=[jax.ShapeDtypeStruct((MLA_HEADS, t, MLA_QK_PAD), F32),
                   jax.ShapeDtypeStruct((MLA_HEADS, t, MLA_QK_PAD), F32),
                   jax.ShapeDtypeStruct((MLA_HEADS, t, MLA_V), F32)],
        scratch_shapes=[pltpu.VMEM((ATT_HEADS, nq, HALO, ATT_BLK), F32)],
        compiler_params=_cp("parallel", "arbitrary"),
    )(q3, k3, v3, o, dcat, lse)


def _ssd_prep(proj, bias128, alog128, *, name):
    t = proj.shape[0]
    nc = t // CHUNK

    def body(raw_ref, b_ref, al_ref, dt_ref, cs_ref, a_ref):
        xv = raw_ref[...] + b_ref[...]
        dt = jnp.maximum(xv, 0.0) + jnp.log(1.0 + jnp.exp(-jnp.abs(xv)))
        a = -jnp.exp(al_ref[...])
        adt = (dt * a).reshape(nc, CHUNK, LANES)
        li = lax.broadcasted_iota(jnp.int32, (nc, CHUNK, CHUNK), 1)
        si = lax.broadcasted_iota(jnp.int32, (nc, CHUNK, CHUNK), 2)
        tril = jnp.where(si <= li, 1.0, 0.0).astype(F32)
        cs = lax.dot_general(tril, adt, (((2,), (1,)), ((0,), (0,))), precision=HI, preferred_element_type=F32)
        dt_ref[...] = dt
        cs_ref[...] = cs.reshape(t, LANES)
        a_ref[...] = a

    blk = pl.BlockSpec((t, LANES), lambda i: (0, 0))
    row = pl.BlockSpec((1, LANES), lambda i: (0, 0))
    return pl.pallas_call(
        body, name=name, grid=(1,),
        in_specs=[pl.BlockSpec((t, LANES), lambda i: (0, OFF_DT // LANES)), row, row],
        out_specs=[blk, blk, row],
        out_shape=[jax.ShapeDtypeStruct((t, LANES), F32), jax.ShapeDtypeStruct((t, LANES), F32),
                   jax.ShapeDtypeStruct((1, LANES), F32)],
        compiler_params=_cp("arbitrary"),
    )(proj, bias128, alog128)


def _ssd_prep_bwd(ddt128, dadt128, proj, bias128, dt128, a128, dd_h, *, name):
    t = proj.shape[0]

    def body(ddt_ref, dadt_ref, raw_ref, b_ref, dt_ref, a_ref, dd_ref, draw_ref, db_ref, dal_ref, dds_ref):
        draw = ddt_ref[...] * _sigmoid(raw_ref[...] + b_ref[...])
        draw_ref[...] = draw.astype(BF16)
        db_ref[...] = jnp.sum(draw, axis=0, keepdims=True)
        dal_ref[...] = jnp.sum(dadt_ref[...] * dt_ref[...], axis=0, keepdims=True) * a_ref[...]
        dds_ref[...] = jnp.sum(dd_ref[...], axis=-1, keepdims=True)

    blk = pl.BlockSpec((t, LANES), lambda i: (0, 0))
    row = pl.BlockSpec((1, LANES), lambda i: (0, 0))
    return pl.pallas_call(
        body, name=name, grid=(1,),
        in_specs=[blk, blk, pl.BlockSpec((t, LANES), lambda i: (0, OFF_DT // LANES)), row, blk, row,
                  pl.BlockSpec((SSD_HEADS, SSD_P), lambda i: (0, 0))],
        out_specs=[blk, row, row, pl.BlockSpec((SSD_HEADS, 1), lambda i: (0, 0))],
        out_shape=[jax.ShapeDtypeStruct((t, LANES), BF16), jax.ShapeDtypeStruct((1, LANES), F32),
                   jax.ShapeDtypeStruct((1, LANES), F32), jax.ShapeDtypeStruct((SSD_HEADS, 1), F32)],
        compiler_params=_cp("arbitrary"),
    )(ddt128, dadt128, proj, bias128, dt128, a128, dd_h)


def _bdot(a, b, ca, cb, precision=None):
    return lax.dot_general(a, b, (((ca,), (cb,)), ((0,), (0,))), precision=precision, preferred_element_type=F32)


def _head_matrices():
    eye, zero = jnp.eye(SSD_P, dtype=F32), jnp.zeros((SSD_P, SSD_P), F32)
    pick = jnp.stack([jnp.concatenate([eye, zero], axis=0), jnp.concatenate([zero, eye], axis=0)])
    return pick, pick.transpose(0, 2, 1)


def _move(x, sel):
    selb = sel.astype(BF16)
    hi = x.astype(BF16)
    rest = x - hi.astype(F32)
    mid = rest.astype(BF16)
    low = (rest - mid.astype(F32)).astype(BF16)
    out = jnp.dot(hi, selb, preferred_element_type=F32)
    out = out + jnp.dot(mid, selb, preferred_element_type=F32)
    return out + jnp.dot(low, selb, preferred_element_type=F32)


def _pick_head(pair_ref, pick_ref, h):
    return _move(pair_ref[...], pick_ref[h % 2])


def _place_head(out_ref, val, place_ref, h):
    wide = _move(val, place_ref[h % 2])

    @pl.when(h % 2 == 0)
    def _():
        out_ref[...] = wide

    @pl.when(h % 2 == 1)
    def _():
        out_ref[...] += wide


def _ssd_common(x2, dt_ref, cs_ref, csr_ref, b_ref, c_ref, nc):
    x = x2.reshape(nc, CHUNK, SSD_P)
    dt = dt_ref[0].reshape(nc, CHUNK, SSD_P)
    cs = cs_ref[0].reshape(nc, CHUNK, SSD_P)
    csr = csr_ref[0]
    bm = b_ref[...].reshape(nc, CHUNK, SSD_N).astype(BF16)
    cm = c_ref[...].reshape(nc, CHUNK, SSD_N).astype(BF16)
    li = lax.broadcasted_iota(jnp.int32, (nc, CHUNK, CHUNK), 1)
    si = lax.broadcasted_iota(jnp.int32, (nc, CHUNK, CHUNK), 2)
    lmat = jnp.exp(jnp.where(si <= li, cs - csr, NEG))
    g = _bdot(cm, bm, 2, 2)
    cs_last = jnp.sum(jnp.where(li == CHUNK - 1, cs, 0.0), axis=1, keepdims=True)
    xdt = x * dt
    dec = jnp.exp(cs_last - cs)
    return x, dt, cs, bm, cm, li, si, lmat, g, cs_last, xdt, dec


def _ssd_fwd(xbc, dt_h, cs_h, cs_row, dskip_h, *, name):
    t = xbc.shape[0]
    nc = t // CHUNK
    hpg = SSD_HEADS // SSD_GROUPS
    pick, place = _head_matrices()

    def body(xs_ref, dt_ref, cs_ref, csr_ref, b_ref, c_ref, dk_ref, pick_ref, place_ref, y_ref, st_ref, sc_ref, cd_ref):
        h = pl.program_id(0)
        x, dt, cs, bm, cm, li, si, lmat, g, cs_last, xdt, dec = _ssd_common(_pick_head(xs_ref, pick_ref, h), dt_ref,
                                                                           cs_ref, csr_ref, b_ref, c_ref, nc)
        yd = _bdot((g * lmat).astype(BF16), xdt.astype(BF16), 2, 1)
        sc_ref[...] = _bdot(bm, (dec * xdt).astype(BF16), 1, 1)
        cd_ref[...] = jnp.exp(cs_last)

        def step(c, s):
            st_ref[0, c] = s
            return s * cd_ref[c] + sc_ref[c]

        lax.fori_loop(0, nc, step, jnp.zeros((SSD_N, SSD_P), F32))
        yo = _bdot(cm, st_ref[0].astype(BF16), 2, 1) * jnp.exp(cs)
        _place_head(y_ref, (yd + yo + dk_ref[0] * x).reshape(t, SSD_P), place_ref, h)

    head = pl.BlockSpec((1, t, SSD_P), lambda h: (h, 0, 0))
    pair = pl.BlockSpec((t, 2 * SSD_P), lambda h: (0, h // 2))
    nxb = D_SSM // SSD_N
    return pl.pallas_call(
        body, name=name, grid=(SSD_HEADS,),
        in_specs=[pair, head, head, pl.BlockSpec((1, nc, 1, CHUNK), lambda h: (h, 0, 0, 0)),
                  pl.BlockSpec((t, SSD_N), lambda h: (0, nxb + h // hpg)),
                  pl.BlockSpec((t, SSD_N), lambda h: (0, nxb + SSD_GROUPS + h // hpg)),
                  pl.BlockSpec((1, 1, SSD_P), lambda h: (h, 0, 0)),
                  pl.BlockSpec((2, 2 * SSD_P, SSD_P), lambda h: (0, 0, 0)),
                  pl.BlockSpec((2, SSD_P, 2 * SSD_P), lambda h: (0, 0, 0))],
        out_specs=[pair, pl.BlockSpec((1, nc, SSD_N, SSD_P), lambda h: (h, 0, 0, 0))],
        out_shape=[jax.ShapeDtypeStruct((t, D_SSM), F32),
                   jax.ShapeDtypeStruct((SSD_HEADS, nc, SSD_N, SSD_P), F32)],
        scratch_shapes=[pltpu.VMEM((nc, SSD_N, SSD_P), F32), pltpu.VMEM((nc, 1, SSD_P), F32)],
        compiler_params=_cp("arbitrary"),
    )(xbc, dt_h, cs_h, cs_row, xbc, xbc, dskip_h, pick, place)


def _ssd_bwd(xbc, dt_h, cs_h, cs_row, dskip_h, a_h, states, dy, *, name):
    t = xbc.shape[0]
    nc = t // CHUNK
    hpg = SSD_HEADS // SSD_GROUPS
    pick, place = _head_matrices()

    def body(xs_ref, dt_ref, cs_ref, csr_ref, b_ref, c_ref, dk_ref, a_ref, st_ref, dy_ref, pick_ref, place_ref,
             dxs_ref, ddt_ref, dadt_ref, db_ref, dc_ref, dd_ref, dsl_ref, dsc_ref, cd_ref):
        h = pl.program_id(0) * hpg + pl.program_id(1)
        x, dt, cs, bm, cm, li, si, lmat, g, cs_last, xdt, dec = _ssd_common(_pick_head(xs_ref, pick_ref, h), dt_ref,
                                                                           cs_ref, csr_ref, b_ref, c_ref, nc)
        dy = _pick_head(dy_ref, pick_ref, h).reshape(nc, CHUNK, SSD_P)
        dyb = dy.astype(BF16)
        xdtb = xdt.astype(BF16)
        sprev = st_ref[0]
        sprevb = sprev.astype(BF16)
        cdec = jnp.exp(cs_last)
        ecs = jnp.exp(cs)
        dw = (ecs * dy).astype(BF16)
        wmat = _bdot(cm, sprevb, 2, 1)
        dcs = jnp.sum(dy * ecs * wmat, axis=2, keepdims=True)
        dcm = _bdot(dw, sprevb, 2, 2)
        dsl_ref[...] = _bdot(cm, dw, 1, 1)
        cd_ref[...] = cdec

        def step(k, ds):
            c = nc - 1 - k
            dsc_ref[c] = ds
            return ds * cd_ref[c] + dsl_ref[c]

        lax.fori_loop(0, nc, step, jnp.zeros((SSD_N, SSD_P), F32))
        dsc = dsc_ref[...]
        dscb = dsc.astype(BF16)
        d_last = jnp.sum(jnp.sum(dsc * sprev, axis=1, keepdims=True) * cdec, axis=2, keepdims=True)
        z = dec * xdt
        dbm = _bdot(z.astype(BF16), dscb, 2, 2)
        dz = _bdot(bm, dscb, 2, 1)
        dxdt = dec * dz
        t2 = jnp.sum(dz * z, axis=2, keepdims=True)
        dcs = dcs - t2
        d_last = d_last + jnp.sum(t2, axis=1, keepdims=True)
        m = g * lmat
        mb = m.astype(BF16)
        dm = _bdot(dyb, xdtb, 2, 2)
        dxdt = dxdt + _bdot(mb, dyb, 1, 1)
        dseg = dm * m
        dcs = dcs + jnp.sum(dseg, axis=2, keepdims=True)
        ones = jnp.ones((nc, CHUNK, SSD_P), F32)
        dcs = dcs - _bdot(dseg, ones, 1, 1, precision=HI)
        dg = (dm * lmat).astype(BF16)
        dcm = dcm + _bdot(dg, bm, 2, 1)
        dbm = dbm + _bdot(dg, cm, 1, 1)
        dcs = dcs + jnp.where(li[:, :, :SSD_P] == CHUNK - 1, d_last, 0.0)
        triu = jnp.where(li <= si, 1.0, 0.0).astype(F32)
        dadt = _bdot(triu, dcs, 2, 1, precision=HI)
        dk = dk_ref[0]
        _place_head(dxs_ref, (dxdt * dt + dk * dy).reshape(t, SSD_P), place_ref, h)
        ddt = jnp.sum(dxdt * x, axis=2, keepdims=True) + dadt * a_ref[0]
        mine = lax.broadcasted_iota(jnp.int32, (t, LANES), 1) == h

        @pl.when(h == 0)
        def _():
            ddt_ref[...] = jnp.zeros_like(ddt_ref)
            dadt_ref[...] = jnp.zeros_like(dadt_ref)

        ddt_ref[...] += jnp.where(mine, jnp.max(ddt, axis=2, keepdims=True).reshape(t, 1), 0.0)
        dadt_ref[...] += jnp.where(mine, jnp.max(dadt, axis=2, keepdims=True).reshape(t, 1), 0.0)
        dd_ref[0] = jnp.sum(jnp.sum(dy * x, axis=1, keepdims=True), axis=0)

        @pl.when(pl.program_id(1) == 0)
        def _():
            db_ref[...] = jnp.zeros_like(db_ref)
            dc_ref[...] = jnp.zeros_like(dc_ref)

        db_ref[...] += dbm.reshape(t, SSD_N)
        dc_ref[...] += dcm.reshape(t, SSD_N)

    head = pl.BlockSpec((1, t, SSD_P), lambda gi, hi: (gi * hpg + hi, 0, 0))
    pair = pl.BlockSpec((t, 2 * SSD_P), lambda gi, hi: (0, (gi * hpg + hi) // 2))
    grp = pl.BlockSpec((t, SSD_N), lambda gi, hi: (0, gi))
    lane = pl.BlockSpec((1, 1, SSD_P), lambda gi, hi: (gi * hpg + hi, 0, 0))
    rows = pl.BlockSpec((t, LANES), lambda gi, hi: (0, 0))
    nxb = D_SSM // SSD_N
    dxs, ddt, dadt, db, dc, dd = pl.pallas_call(
        body, name=name, grid=(SSD_GROUPS, hpg),
        in_specs=[pair, head, head, pl.BlockSpec((1, nc, 1, CHUNK), lambda gi, hi: (gi * hpg + hi, 0, 0, 0)),
                  pl.BlockSpec((t, SSD_N), lambda gi, hi: (0, nxb + gi)),
                  pl.BlockSpec((t, SSD_N), lambda gi, hi: (0, nxb + SSD_GROUPS + gi)), lane, lane,
                  pl.BlockSpec((1, nc, SSD_N, SSD_P), lambda gi, hi: (gi * hpg + hi, 0, 0, 0)), pair,
                  pl.BlockSpec((2, 2 * SSD_P, SSD_P), lambda gi, hi: (0, 0, 0)),
                  pl.BlockSpec((2, SSD_P, 2 * SSD_P), lambda gi, hi: (0, 0, 0))],
        out_specs=[pair, rows, rows, grp, grp, lane],
        out_shape=[jax.ShapeDtypeStruct((t, D_SSM), F32)] + [jax.ShapeDtypeStruct((t, LANES), F32)] * 2
        + [jax.ShapeDtypeStruct((t, SSD_GROUPS * SSD_N), F32)] * 2
        + [jax.ShapeDtypeStruct((SSD_HEADS, 1, SSD_P), F32)],
        scratch_shapes=[pltpu.VMEM((nc, SSD_N, SSD_P), F32), pltpu.VMEM((nc, SSD_N, SSD_P), F32),
                        pltpu.VMEM((nc, 1, SSD_P), F32)],
        compiler_params=_cp("arbitrary", "arbitrary"),
    )(xbc, dt_h, cs_h, cs_row, xbc, xbc, dskip_h, a_h, states, dy, pick, place)
    return jnp.concatenate([dxs, db, dc], axis=1), ddt, dadt, dd


def _ssd_gate_fwd(y, proj, w, *, tr=256, name):
    t = y.shape[0]
    gw = D_SSM // SSD_GROUPS

    def body(y_ref, z_ref, w_ref, o_ref):
        v = y_ref[...] * _silu(z_ref[...])
        for gi in range(SSD_GROUPS):
            vg = v[:, gi * gw:(gi + 1) * gw]
            r = lax.rsqrt(jnp.mean(vg * vg, axis=-1, keepdims=True) + NORM_EPS)
            o_ref[:, gi * gw:(gi + 1) * gw] = (vg * r * w_ref[:, gi * gw:(gi + 1) * gw]).astype(BF16)

    blk = pl.BlockSpec((tr, D_SSM), lambda i: (i, 0))
    return pl.pallas_call(
        body, name=name, grid=(t // tr,), in_specs=[blk, blk, pl.BlockSpec((1, D_SSM), lambda i: (0, 0))],
        out_specs=blk, out_shape=jax.ShapeDtypeStruct((t, D_SSM), BF16), compiler_params=_cp("parallel"),
    )(y, proj, w)


def _ssd_gate_bwd(y, proj, w, dcat, *, tr=256, name):
    t = y.shape[0]
    gw = D_SSM // SSD_GROUPS

    def body(y_ref, z_ref, w_ref, d_ref, dy_ref, dz_ref, dw_ref):
        yv, zv, dv = y_ref[...], z_ref[...], d_ref[...].astype(F32)
        sz = _silu(zv)
        v = yv * sz

        @pl.when(pl.program_id(0) == 0)
        def _():
            dw_ref[...] = jnp.zeros_like(dw_ref)

        for gi in range(SSD_GROUPS):
            sl = slice(gi * gw, (gi + 1) * gw)
            vg, dg = v[:, sl], dv[:, sl]
            r = lax.rsqrt(jnp.mean(vg * vg, axis=-1, keepdims=True) + NORM_EPS)
            vh = vg * r
            gg = dg * w_ref[:, sl]
            dvg = r * (gg - vh * jnp.mean(gg * vh, axis=-1, keepdims=True))
            dy_ref[:, sl] = dvg * sz[:, sl]
            dz_ref[:, sl] = (dvg * yv[:, sl] * _dsilu(zv[:, sl])).astype(BF16)
            dw_ref[:, sl] += jnp.sum(dg * vh, axis=0, keepdims=True)

    blk = pl.BlockSpec((tr, D_SSM), lambda i: (i, 0))
    row = pl.BlockSpec((1, D_SSM), lambda i: (0, 0))
    return pl.pallas_call(
        body, name=name, grid=(t // tr,), in_specs=[blk, blk, row, blk], out_specs=[blk, blk, row],
        out_shape=[jax.ShapeDtypeStruct((t, D_SSM), F32), jax.ShapeDtypeStruct((t, D_SSM), BF16),
                   jax.ShapeDtypeStruct((1, D_SSM), F32)],
        compiler_params=_cp("arbitrary"),
    )(y, proj, w, dcat)


def _pad_lanes(v):
    return jnp.pad(v, ((0, 0), (0, LANES - v.shape[1])))


def _per_head(v128, t):
    return jnp.broadcast_to(v128[:, :SSD_HEADS].T[:, :, None], (SSD_HEADS, t, SSD_P))


def _ssd_forward(proj, conv_w, conv_b, dt_bias, a_log, d_skip, ssd_norm_w):
    t = proj.shape[0]
    nc = t // CHUNK
    xbc = _conv_act_fwd(proj, conv_w, conv_b, kw=SSD_CONV, glu=False, tc=512, coff=OFF_XBC // 512,
                        ncols=SSD_CONV_DIM, out_dtype=F32, name="ssd_conv_fwd")
    bias128, alog128 = _pad_lanes(dt_bias), _pad_lanes(a_log)
    dt128, cs128, a128 = _ssd_prep(proj, bias128, alog128, name="ssd_prep")
    dt_h, cs_h = _per_head(dt128, t), _per_head(cs128, t)
    cs_row = cs128[:, :SSD_HEADS].T.reshape(SSD_HEADS, nc, 1, CHUNK)
    dskip_h = jnp.broadcast_to(d_skip[0][:, None, None], (SSD_HEADS, 1, SSD_P))
    a_h = jnp.broadcast_to(a128[0, :SSD_HEADS][:, None, None], (SSD_HEADS, 1, SSD_P))
    y, states = _ssd_fwd(xbc, dt_h, cs_h, cs_row, dskip_h, name="ssd_scan_fwd")
    y_ssd = _ssd_gate_fwd(y, proj, ssd_norm_w, name="ssd_gate_fwd")
    saved = (proj, conv_w, conv_b, ssd_norm_w, bias128, dt128, a128, dt_h, cs_h, cs_row, xbc, dskip_h, a_h, states, y)
    return y_ssd, saved


def _ssd_backward(saved, dcat):
    proj, conv_w, conv_b, ssd_norm_w, bias128, dt128, a128, dt_h, cs_h, cs_row, xbc, dskip_h, a_h, states, y = saved
    dy, dz, d_norm_w = _ssd_gate_bwd(y, proj, ssd_norm_w, dcat, name="ssd_gate_bwd")
    dxc, ddt128, dadt128, dd_h = _ssd_bwd(xbc, dt_h, cs_h, cs_row, dskip_h, a_h, states, dy, name="ssd_scan_bwd")
    dxbc, d_conv_w, d_conv_b = _conv_act_bwd(proj, conv_w, conv_b, dxc, kw=SSD_CONV, glu=False, tc=512,
                                             coff=OFF_XBC // 512, ncols=SSD_CONV_DIM, name="ssd_conv_bwd")
    d_raw, d_bias, d_alog, d_dskip = _ssd_prep_bwd(ddt128, dadt128, proj, bias128, dt128, a128,
                                                   dd_h.reshape(SSD_HEADS, SSD_P), name="ssd_prep_bwd")
    return (dz, dxbc, d_raw, d_norm_w, d_conv_w, d_conv_b, d_bias[:, :SSD_HEADS], d_alog[:, :SSD_HEADS],
            d_dskip.reshape(1, SSD_HEADS))


def _rope_tables(positions):
    inv_freq = ROPE_THETA ** (-jnp.arange(0, MLA_ROPE, 2, dtype=F32) / MLA_ROPE)
    ang = positions[0].astype(F32)[:, None] * inv_freq
    cos, sin = jnp.cos(ang), jnp.sin(ang)
    z = jnp.zeros_like(cos)
    return jnp.stack([jnp.concatenate([cos, cos, z, z], axis=1), jnp.concatenate([-sin, z, z, z], axis=1),
                      jnp.concatenate([z, sin, z, z], axis=1)])


def _mla_forward(proj, tabs, q_a_norm_w, wq_pad, kv_a_norm_w, wkv):
    qn = _rmsnorm_fwd(proj, q_a_norm_w, width=MLA_Q_RANK, cblk=OFF_QA // MLA_Q_RANK, name="q_a_norm")
    q = _matmul(qn, wq_pad, name="q_b_proj")
    kvn = _rmsnorm_fwd(proj, kv_a_norm_w, width=MLA_KV_RANK, cblk=OFF_CKV // MLA_KV_RANK, name="kv_a_norm")
    kv = _matmul(kvn, wkv, name="kv_b_proj")
    q3, k3, v3, vt4 = _mla_prep(q, kv, proj, tabs, name="mla_prep")
    o, lse = _attn_fwd(q3, k3, vt4, name="attn_fwd")
    return o, (proj, tabs, q_a_norm_w, wq_pad, kv_a_norm_w, wkv, qn, kvn, q3, k3, v3, o, lse)


def _mla_backward(saved, dcat):
    proj, tabs, q_a_norm_w, wq_pad, kv_a_norm_w, wkv, qn, kvn, q3, k3, v3, o, lse = saved
    dq3, dk3, dv3 = _attn_bwd(q3, k3, v3, o, dcat, lse, name="attn_bwd")
    dq, dkv, dkr = _mla_unprep(dq3, dk3, dv3, tabs, name="mla_unprep")
    d_wq = _matmul(qn, dq, ta=True, out_dtype=BF16, name="d_w_q_b")
    dqn = _matmul(dq, wq_pad, tb=True, name="d_qn")
    dq_a, d_qnw = _rmsnorm_bwd(proj, q_a_norm_w, dqn, width=MLA_Q_RANK, cblk=OFF_QA // MLA_Q_RANK, out_dtype=BF16,
                               name="q_a_norm_bwd")
    d_wkv = _matmul(kvn, dkv, ta=True, out_dtype=BF16, name="d_w_kv_b")
    dkvn = _matmul(dkv, wkv, tb=True, name="d_kvn")
    dckv, d_kvnw = _rmsnorm_bwd(proj, kv_a_norm_w, dkvn, width=MLA_KV_RANK, cblk=OFF_CKV // MLA_KV_RANK,
                                out_dtype=BF16, name="kv_a_norm_bwd")
    return dq_a, dckv, dkr, d_wq, d_wkv, d_qnw, d_kvnw


def _pad_w_q(w):
    r = w.shape[0]
    w3 = w.reshape(r, MLA_HEADS, MLA_NOPE + MLA_ROPE)
    return jnp.pad(w3, ((0, 0), (0, 0), (0, MLA_QK_PAD - MLA_NOPE - MLA_ROPE))).reshape(r, MLA_HEADS * MLA_QK_PAD)


def _unpad_w_q(w):
    r = w.shape[0]
    return w.reshape(r, MLA_HEADS, MLA_QK_PAD)[:, :, :MLA_NOPE + MLA_ROPE].reshape(r, MLA_HEADS * (MLA_NOPE + MLA_ROPE))


W_IN_SEGMENTS = ((0, D_SSM + SSD_CONV_DIM, 0), (D_SSM + SSD_CONV_DIM, D_SSM + SSD_CONV_DIM + SSD_HEADS, OFF_DT),
                 (D_SSM + SSD_CONV_DIM + SSD_HEADS, D_IN - MLA_ROPE, OFF_QA), (D_IN - MLA_ROPE, D_IN, OFF_KR))


def _pad_w_in_shards(g):
    n = g.shape[2]
    pieces, at = [], 0
    for lo, hi, start in sorted(W_IN_SEGMENTS, key=lambda seg: seg[2]):
        if start > at:
            pieces.append(jnp.zeros((g.shape[1], start - at), g.dtype))
        for j in range(N_DEV):
            a, b = max(lo, j * n), min(hi, (j + 1) * n)
            if a < b:
                pieces.append(g[j][:, a - j * n:b - j * n])
        at = start + hi - lo
    pieces.append(jnp.zeros((g.shape[1], D_IN_PAD - at), g.dtype))
    return jnp.concatenate(pieces, axis=1)


def _unpad_w_in_shards(w):
    n = D_IN // N_DEV
    shards = []
    for j in range(N_DEV):
        pieces = []
        for lo, hi, start in W_IN_SEGMENTS:
            a, b = max(lo, j * n), min(hi, (j + 1) * n)
            if a < b:
                pieces.append(w[:, start + a - lo:start + b - lo])
        shards.append(jnp.concatenate(pieces, axis=1) if len(pieces) > 1 else pieces[0])
    return jnp.stack(shards)


WEIGHTS = ['mix_norm_w', 'w_in', 'conv_w', 'conv_b', 'dt_bias', 'a_log', 'd_skip', 'ssd_norm_w', 'q_a_norm_w', 'w_q_b',
           'kv_a_norm_w', 'w_kv_b', 'w_out', 'ffn_norm_w', 'w_ffn_up', 'ffn_conv_w', 'ffn_conv_b', 'w_ffn_down',
           'ple_norm_w', 'w_ple_gate', 'b_ple_gate', 'w_ple_proj', 'ple_post_norm_w', 'final_norm_w']
BIG = ['w_in', 'w_q_b', 'w_kv_b', 'w_out', 'w_ffn_up', 'w_ffn_down', 'w_ple_gate', 'w_ple_proj']
COL_SHARDED = ('w_in', 'w_q_b', 'w_kv_b', 'w_ffn_up', 'w_ple_proj')
CONV = ['conv_w', 'ffn_conv_w']
REPL = [n for n in WEIGHTS if n not in BIG and n not in CONV]
FFN_INV = tuple(int(i) for i in np.argsort(FFN_PERM))


def _cat_cols(g):
    return jnp.concatenate([g[j] for j in range(N_DEV)], axis=1)


def _split_cols(w):
    n = w.shape[1] // N_DEV
    return jnp.stack([w[:, j * n:(j + 1) * n] for j in range(N_DEV)])


def _interleave(v):
    r = v.shape[0]
    return v.reshape(r, N_DEV, FFN_TC)[:, jnp.array(FFN_PERM)].reshape(r, N_DEV * FFN_TC)


def _deinterleave(v):
    r = v.shape[0]
    return v.reshape(r, N_DEV, FFN_TC)[:, jnp.array(FFN_INV)].reshape(r, N_DEV * FFN_TC)


def _assemble_weights(g):
    layout = {
        'w_in': _pad_w_in_shards,
        'w_q_b': lambda v: _pad_w_q(_cat_cols(v)),
        'w_kv_b': _cat_cols,
        'w_out': lambda v: v.reshape(D_MODEL, D_MODEL),
        'w_ffn_up': lambda v: v,
        'w_ffn_down': lambda v: v.reshape(D_FF, D_MODEL),
        'w_ple_gate': lambda v: v.reshape(D_MODEL, D_MODEL),
        'w_ple_proj': _cat_cols,
        'conv_w': _cat_cols,
        'ffn_conv_w': lambda v: _interleave(_cat_cols(v)),
    }
    return {n: layout[n](v) for n, v in g.items()}


WEIGHT_GROUPS = {'a': ['w_in', 'w_q_b', 'w_kv_b', 'conv_w'], 'b': ['w_out'],
                 'c': ['w_ffn_up', 'ffn_conv_w', 'w_ffn_down', 'w_ple_gate', 'w_ple_proj']}
GRAD_GROUPS = {'p': ['w_ple_proj', 'w_ple_gate', 'w_ffn_down'], 'r': ['w_ffn_up'], 's': ['w_out'],
               't': ['w_q_b', 'w_kv_b', 'w_in']}


def _ffn_perm(j):
    return (j % 2) * (N_DEV // 2) + j // 2


def _local_step(x, p, tabs, get_w, s, target, emit, relay, settle):
    t = x.shape[0]
    s = dict(s)
    half = D_MODEL // 2
    up_cols = 2 * D_FF
    ffn_conv_b = _interleave(s['ffn_conv_b'])
    w = dict(get_w('a', None))
    h = _rmsnorm_fwd(x, s['mix_norm_w'], width=D_MODEL, name="mix_norm")
    proj = _matmul(h, w['w_in'], name="in_proj")
    y_ssd, ssd_saved = _ssd_forward(proj, w['conv_w'], s['conv_b'], s['dt_bias'], s['a_log'], s['d_skip'],
                                    s['ssd_norm_w'])
    o, mla_saved = _mla_forward(proj, tabs, s['q_a_norm_w'], w['w_q_b'], s['kv_a_norm_w'], w['w_kv_b'])
    tk_o, tn_o = _tile(half, MM_TK), _tile(D_MODEL, MM_TILE)
    w.update(get_w('b', o))
    x1 = _matmul(y_ssd, w['w_out'], add=x, mnk=(t, D_MODEL, half), name="out_proj_ssd")
    x1 = _matmul(o, w['w_out'], add=x1, mnk=(t, D_MODEL, half), name="out_proj_mla",
                 b_spec=pl.BlockSpec((tk_o, tn_o), lambda i, j, kk: (kk + half // tk_o, j)))
    hf = _rmsnorm_fwd(x1, s['ffn_norm_w'], width=D_MODEL, name="ffn_norm")
    w.update(get_w('c', hf))
    tk_u = _tile(D_MODEL, MM_TK)
    u = _matmul(hf, w['w_ffn_up'], mnk=(t, up_cols, D_MODEL), tn=FFN_TC, name="ffn_up",
                b_spec=pl.BlockSpec((1, tk_u, FFN_TC), lambda i, j, kk: (_ffn_perm(j), kk, 0)))
    act = _conv_act_fwd(u, w['ffn_conv_w'], ffn_conv_b, kw=FFN_CONV, glu=True, tc=2 * FFN_TC, coff=0, ncols=up_cols,
                        out_dtype=BF16, name="ffn_act")
    x2 = _matmul(act, w['w_ffn_down'], add=x1, name="ffn_down")
    hp = _rmsnorm_fwd(x2, s['ple_norm_w'], width=D_MODEL, name="ple_norm")
    gl = _matmul(hp, w['w_ple_gate'], bias=s['b_ple_gate'], name="ple_gate")
    pe = _matmul(p, w['w_ple_proj'], name="ple_proj")
    x3 = _ple_fwd(x2, gl, pe, s['ple_post_norm_w'], name="ple_mix")
    loss, dx3, d_final = _loss_head(x3, s['final_norm_w'], target, name="loss_head")
    dgl, d_bgate, dpe, d_post = _ple_bwd(dx3, gl, pe, s['ple_post_norm_w'], name="ple_mix_bwd")
    d_wproj = _matmul(p, dpe, ta=True, out_dtype=BF16, name="d_w_ple_proj")
    d_wgate = _matmul(hp, dgl, ta=True, out_dtype=BF16, name="d_w_ple_gate")
    dhp = _matmul(dgl, w['w_ple_gate'], tb=True, name="d_ple_normed")
    dx2, d_plenorm = _rmsnorm_bwd(x2, s['ple_norm_w'], dhp, dx3, width=D_MODEL, name="ple_norm_bwd")
    dact = _matmul(dx2, w['w_ffn_down'], tb=True, name="d_ffn_act")
    d_wdown = _matmul(act, dx2, ta=True, out_dtype=BF16, name="d_w_ffn_down")
    zz = emit('p', {'w_ple_proj': _split_cols(d_wproj), 'w_ple_gate': d_wgate.reshape(N_DEV, D_MODEL // N_DEV, D_MODEL),
                    'w_ffn_down': d_wdown.reshape(N_DEV, D_FF // N_DEV, D_MODEL)})
    du, d_fconv_w, d_fconv_b = _conv_act_bwd(u, w['ffn_conv_w'], ffn_conv_b + zz, dact, kw=FFN_CONV, glu=True,
                                             tc=2 * FFN_TC, coff=0, ncols=up_cols, name="ffn_act_bwd")
    zz = zz + relay('p', du)
    tm_u = _tile(D_MODEL, MM_TILE)
    d_wup = _matmul(hf, du, ta=True, out_dtype=BF16, mnk=(D_MODEL, up_cols, t), tn=FFN_TC, name="d_w_ffn_up",
                    o_spec=pl.BlockSpec((1, tm_u, FFN_TC), lambda i, j, kk: (_ffn_perm(j), i, 0)),
                    o_shape=(N_DEV, D_MODEL, FFN_TC))
    zz = zz + emit('r', {'w_ffn_up': d_wup})
    zero_row = jnp.zeros((1, D_MODEL), F32)
    dhf = _matmul(du, w['w_ffn_up'], tb=True, mnk=(t, D_MODEL, up_cols), tk=FFN_TC, name="d_ffn_normed",
                  bias=zero_row + zz,
                  b_spec=pl.BlockSpec((1, tn_o, FFN_TC), lambda i, j, kk: (_ffn_perm(kk), j, 0)))
    zz = zz + relay('r', dhf) + settle('p')
    dx1, d_ffnnorm = _rmsnorm_bwd(x1, s['ffn_norm_w'] + zz, dhf, dx2, width=D_MODEL, name="ffn_norm_bwd")
    dcat = _matmul(dx1, w['w_out'], tb=True, name="d_mixed")
    d_wout = jnp.concatenate([_matmul(y_ssd, dx1, ta=True, out_dtype=BF16, name="d_w_out_ssd"),
                              _matmul(o, dx1, ta=True, out_dtype=BF16, name="d_w_out_mla")], axis=0)
    zz = zz + emit('s', {'w_out': d_wout.reshape(N_DEV, D_MODEL // N_DEV, D_MODEL)})
    ssd_saved = ssd_saved[:3] + (ssd_saved[3] + zz,) + ssd_saved[4:]
    dz, dxbc, d_raw, d_ssdnorm, d_conv_w, d_conv_b, d_dtb, d_alog, d_dskip = _ssd_backward(ssd_saved, dcat)
    zz = zz + relay('s', dz)
    mla_saved = mla_saved[:-1] + (mla_saved[-1] + zz,)
    dq_a, dckv, dkr, d_wq, d_wkv, d_qnorm, d_kvnorm = _mla_backward(mla_saved, dcat)
    d_raw = (d_raw + settle('r')).astype(BF16)
    dproj = jnp.concatenate([dz, dxbc, dq_a, dckv, dkr, d_raw], axis=1)
    d_win = _matmul(h, dproj, ta=True, out_dtype=BF16, name="d_w_in")
    zz = emit('t', {'w_in': _unpad_w_in_shards(d_win), 'w_q_b': _split_cols(_unpad_w_q(d_wq)),
                    'w_kv_b': _split_cols(d_wkv)})
    dh = _matmul(dproj, w['w_in'], tb=True, bias=zero_row + zz, name="d_in_normed")
    zz = relay('t', dh) + settle('s')
    dx, d_mixnorm = _rmsnorm_bwd(x, s['mix_norm_w'] + zz, dh, dx1, width=D_MODEL, name="mix_norm_bwd")
    conv = {'conv_w': d_conv_w, 'ffn_conv_w': _deinterleave(d_fconv_w)}
    vec = {
        'mix_norm_w': d_mixnorm, 'conv_b': d_conv_b, 'dt_bias': d_dtb, 'a_log': d_alog, 'd_skip': d_dskip,
        'ssd_norm_w': d_ssdnorm, 'q_a_norm_w': d_qnorm, 'kv_a_norm_w': d_kvnorm, 'ffn_norm_w': d_ffnnorm,
        'ffn_conv_b': _deinterleave(d_fconv_b), 'ple_norm_w': d_plenorm, 'b_ple_gate': d_bgate,
        'ple_post_norm_w': d_post, 'final_norm_w': d_final,
    }
    return loss, dx, conv, vec


MESH = pl.DeviceIdType.MESH
FLIPS = ((0, 0, 1), (1, 0, 0), (0, 1, 0), (1, 1, 0), (1, 0, 1), (0, 1, 1), (1, 1, 1))


def _exchange(items, *, gather, name):
    n = len(items)

    def body(*refs):
        ins, outs = refs[:n], refs[n:2 * n]
        send_sems, recv_sems, local_sems = refs[2 * n:]
        x, y, c = lax.axis_index("x"), lax.axis_index("y"), lax.axis_index("c")
        me = 4 * x + 2 * y + c
        peers = [(jnp.where(fx, 1 - x, x), jnp.where(fy, 1 - y, y), jnp.where(fc, 1 - c, c)) for fx, fy, fc in FLIPS]
        slot = [4 * px + 2 * py + pc for px, py, pc in peers]
        local, sends = [], []
        for wi in range(n):
            cp = pltpu.make_async_copy(ins[wi] if gather else ins[wi].at[me], outs[wi].at[me], local_sems.at[wi])
            cp.start()
            local.append(cp)
            for k, peer in enumerate(peers):
                cp = pltpu.make_async_remote_copy(
                    src_ref=ins[wi] if gather else ins[wi].at[slot[k]], dst_ref=outs[wi].at[me],
                    send_sem=send_sems.at[k, wi], recv_sem=recv_sems.at[k, wi], device_id=peer, device_id_type=MESH)
                cp.start()
                sends.append(cp)
        for wi in range(n):
            for k, peer in enumerate(peers):
                pltpu.make_async_remote_copy(
                    src_ref=outs[wi].at[slot[k]], dst_ref=outs[wi].at[slot[k]], send_sem=send_sems.at[k, wi],
                    recv_sem=recv_sems.at[k, wi], device_id=peer, device_id_type=MESH).wait_recv()
        for cp in sends:
            cp.wait_send()
        for cp in local:
            cp.wait()

    hbm = pl.BlockSpec(memory_space=pltpu.HBM)
    out_shape = [jax.ShapeDtypeStruct(((N_DEV,) + v.shape) if gather else v.shape, v.dtype) for v in items]
    return pl.pallas_call(
        body, name=name, in_specs=[hbm] * n, out_specs=[hbm] * n, out_shape=out_shape,
        scratch_shapes=[pltpu.SemaphoreType.DMA((len(FLIPS), n)), pltpu.SemaphoreType.DMA((len(FLIPS), n)),
                        pltpu.SemaphoreType.DMA((n,))],
    )(*items)


HBM_SPEC = pl.BlockSpec(memory_space=pltpu.HBM)
SEM_SPEC = pl.BlockSpec(memory_space=pltpu.SEMAPHORE)
EFFECT = pltpu.SideEffectType.DATAFLOW_SIDE_EFFECTING


def _peers():
    x, y, c = lax.axis_index("x"), lax.axis_index("y"), lax.axis_index("c")
    peers = [(jnp.where(fx, 1 - x, x), jnp.where(fy, 1 - y, y), jnp.where(fc, 1 - c, c)) for fx, fy, fc in FLIPS]
    return 4 * x + 2 * y + c, peers, [4 * px + 2 * py + pc for px, py, pc in peers]


def _split_start(bufs, ncopies, plan, *, name):
    nb = len(bufs)

    def body(*refs):
        send_sems, recv_sems, token = refs[nb], refs[nb + 1], refs[2 * nb + 2]
        for i, (src, dst, peer, _) in enumerate(plan(refs[:nb])):
            pltpu.make_async_remote_copy(src_ref=src, dst_ref=dst, send_sem=send_sems.at[i], recv_sem=recv_sems.at[i],
                                         device_id=peer, device_id_type=MESH).start()
        token[...] = jnp.zeros_like(token)

    res = pl.pallas_call(
        body, name=name, in_specs=[HBM_SPEC] * nb,
        out_specs=[SEM_SPEC, SEM_SPEC] + [HBM_SPEC] * nb + [pl.BlockSpec(memory_space=pltpu.VMEM)],
        out_shape=[pltpu.SemaphoreType.DMA((ncopies,)), pltpu.SemaphoreType.DMA((ncopies,))]
        + [pltpu.HBM(v.shape, v.dtype) for v in bufs] + [jax.ShapeDtypeStruct((HALO, LANES), F32)],
        input_output_aliases={i: 2 + i for i in range(nb)},
        compiler_params=pltpu.CompilerParams(has_side_effects=EFFECT),
    )(*[pltpu.with_memory_space_constraint(v, pltpu.HBM) for v in bufs])
    return (res[0], res[1], list(res[2:2 + nb])), res[2 + nb]


def _split_wait(started, after, plan, local_plan, *, name):
    send_sems, recv_sems, bufs = started
    nb = len(bufs)
    nlocal = len(local_plan(bufs))

    def body(*refs):
        send_sems, recv_sems = refs[nb], refs[nb + 1]
        local_sems = refs[2 * nb + 3]
        local = []
        for j, (src, dst) in enumerate(local_plan(refs[:nb])):
            cp = pltpu.make_async_copy(src, dst, local_sems.at[j])
            cp.start()
            local.append(cp)
        for i, (src, _, peer, incoming) in enumerate(plan(refs[:nb])):
            cp = pltpu.make_async_remote_copy(src_ref=src, dst_ref=incoming, send_sem=send_sems.at[i],
                                              recv_sem=recv_sems.at[i], device_id=peer, device_id_type=MESH)
            cp.wait_send()
            cp.wait_recv()
        for cp in local:
            cp.wait()

    res = pl.pallas_call(
        body, name=name, in_specs=[HBM_SPEC] * nb + [SEM_SPEC, SEM_SPEC, pl.BlockSpec(memory_space=pl.ANY)],
        out_specs=[HBM_SPEC] * nb, out_shape=[pltpu.HBM(v.shape, v.dtype) for v in bufs],
        input_output_aliases={i: i for i in range(nb)},
        scratch_shapes=[pltpu.SemaphoreType.DMA((max(nlocal, 1),))],
        compiler_params=pltpu.CompilerParams(has_side_effects=EFFECT),
    )(*bufs, send_sems, recv_sems, after)
    return list(res)


def _place():
    x, y, c = lax.axis_index("x"), lax.axis_index("y"), lax.axis_index("c")
    others = [((1 - x, y, c), 2 * (1 - x) + y), ((x, 1 - y, c), 2 * x + 1 - y), ((1 - x, 1 - y, c), 2 * (1 - x) + 1 - y)]
    return 4 * x + 2 * y + c, 2 * x + y, c, (x, y, 1 - c), others


def _gather1_plan(n):
    def plan(refs):
        me, _, _, sibling, others = _place()
        out = []
        for wi in range(n):
            item, land = refs[wi], refs[n + wi]
            out.append((item, land.at[me], sibling, land.at[me + 1 - 2 * lax.axis_index("c")]))
            for peer, chip in others:
                out.append((item, land.at[me], peer, land.at[2 * chip + lax.axis_index("c")]))
        return out

    return plan


def _gather1_local(n):
    def plan(refs):
        me = _place()[0]
        return [(refs[wi], refs[n + wi].at[me]) for wi in range(n)]

    return plan


def _gather2_plan(n):
    def plan(refs):
        _, _, c, sibling, others = _place()
        out = []
        for wi in range(n):
            land = refs[wi]
            for _, chip in others:
                out.append((land.at[2 * chip + c], land.at[2 * chip + c], sibling, land.at[2 * chip + 1 - c]))
        return out

    return plan


def _gather_start(items, *, name):
    lands = [lax.empty((N_DEV,) + v.shape, v.dtype) for v in items]
    return _split_start(items + lands, 4 * len(items), _gather1_plan(len(items)), name=name)


def _gather_forward(started, after, *, name):
    n = len(started[2]) // 2
    bufs = _split_wait(started, after, _gather1_plan(n), _gather1_local(n), name=name + "_wait")
    return _split_start(bufs[n:], 3 * n, _gather2_plan(n), name=name + "_start")


def _gather_finish(started, after, *, name):
    n = len(started[2])
    return _split_wait(started, after, _gather2_plan(n), lambda refs: [], name=name)


def _handshake(peers):
    barrier = pltpu.get_barrier_semaphore()
    for peer in peers:
        pl.semaphore_signal(barrier, inc=1, device_id=peer, device_id_type=MESH)
    pl.semaphore_wait(barrier, len(peers))


def _remote(src, dst, send_sem, recv_sem, peer):
    return pltpu.make_async_remote_copy(src_ref=src, dst_ref=dst, send_sem=send_sem, recv_sem=recv_sem, device_id=peer,
                                        device_id_type=MESH)


def _sequencer_gather(items, *, collective_id, name):
    n = len(items)
    srcs = [jax.new_ref(v, memory_space=pltpu.MemorySpace.HBM) for v in items]
    lands = [jax.empty_ref(jax.ShapeDtypeStruct((N_DEV,) + v.shape, v.dtype), memory_space=pltpu.MemorySpace.HBM)
             for v in items]
    dma = pltpu.SemaphoreType.DMA

    @pl.kernel(mesh=plsc.ScalarSubcoreMesh(axis_name="sequencer", num_cores=1), name=name,
               scratch_types=(dma((4 * n,)), dma((4 * n,)), dma((3 * n,)), dma((3 * n,)), dma((n,))),
               compiler_params=pltpu.CompilerParams(collective_id=collective_id))
    def launch(send1, recv1, send2, recv2, local_sems):
        _, _, _, sibling, others = _place()
        _handshake([sibling] + [peer for peer, _ in others])
        hop1 = _gather1_plan(n)(srcs + lands)
        hop2 = _gather2_plan(n)(lands)
        local = [pltpu.make_async_copy(src, dst, local_sems.at[j])
                 for j, (src, dst) in enumerate(_gather1_local(n)(srcs + lands))]
        for cp in local:
            cp.start()
        for i, (src, dst, peer, _) in enumerate(hop1):
            _remote(src, dst, send1.at[i], recv1.at[i], peer).start()
        for wi in range(n):
            for j in range(3):
                i1, i2 = 4 * wi + 1 + j, 3 * wi + j
                src, _, peer, incoming = hop1[i1]
                _remote(src, incoming, send1.at[i1], recv1.at[i1], peer).wait_recv()
                src, dst, peer, _ = hop2[i2]
                _remote(src, dst, send2.at[i2], recv2.at[i2], peer).start()
        for wi in range(n):
            src, _, peer, incoming = hop1[4 * wi]
            _remote(src, incoming, send1.at[4 * wi], recv1.at[4 * wi], peer).wait_recv()
        for i, (src, _, peer, incoming) in enumerate(hop2):
            cp = _remote(src, incoming, send2.at[i], recv2.at[i], peer)
            cp.wait_send()
            cp.wait_recv()
        for i, (src, dst, peer, _) in enumerate(hop1):
            _remote(src, dst, send1.at[i], recv1.at[i], peer).wait_send()
        for cp in local:
            cp.wait()

    launch()
    return [land[...] for land in lands]


def _sequencer_exchange(sources, land_shapes, ncopies, plan, local_plan, peers, *, collective_id, name):
    srcs = [jax.new_ref(v, memory_space=pltpu.MemorySpace.HBM) for v in sources]
    lands = [jax.empty_ref(s, memory_space=pltpu.MemorySpace.HBM) for s in land_shapes]
    nlocal = len(local_plan(srcs + lands))
    dma = pltpu.SemaphoreType.DMA

    @pl.kernel(mesh=plsc.ScalarSubcoreMesh(axis_name="sequencer", num_cores=1), name=name,
               scratch_types=(dma((ncopies,)), dma((ncopies,)), dma((max(nlocal, 1),))),
               compiler_params=pltpu.CompilerParams(collective_id=collective_id))
    def launch(send_sems, recv_sems, local_sems):
        _handshake(peers(_place()))
        copies = plan(srcs + lands)
        local = [pltpu.make_async_copy(src, dst, local_sems.at[j])
                 for j, (src, dst) in enumerate(local_plan(srcs + lands))]
        for cp in local:
            cp.start()
        for i, (src, dst, peer, _) in enumerate(copies):
            _remote(src, dst, send_sems.at[i], recv_sems.at[i], peer).start()
        for i, (src, _, peer, incoming) in enumerate(copies):
            cp = _remote(src, incoming, send_sems.at[i], recv_sems.at[i], peer)
            cp.wait_send()
            cp.wait_recv()
        for cp in local:
            cp.wait()

    launch()
    return [land[...] for land in lands]


def _sequencer_scatter_hop2(sums, *, collective_id, name):
    n = len(sums)
    shapes = [jax.ShapeDtypeStruct(v.shape, v.dtype) for v in sums]
    return _sequencer_exchange(sums, shapes, 3 * n, _scatter2_plan(n), _scatter2_local(n),
                               lambda place: [peer for peer, _ in place[4]], collective_id=collective_id, name=name)


N_CHIP = N_DEV // 2


def _scatter1_plan(n):
    def plan(refs):
        _, _, c, sibling, _ = _place()
        out = []
        for wi in range(n):
            parts, half = refs[wi], refs[n + wi]
            for chip in range(N_CHIP):
                out.append((parts.at[2 * chip + 1 - c], half.at[chip], sibling, half.at[chip]))
        return out

    return plan


def _scatter2_plan(n):
    def plan(refs):
        _, my_chip, _, _, others = _place()
        out = []
        for wi in range(n):
            sums, recv = refs[wi], refs[n + wi]
            for peer, chip in others:
                out.append((sums.at[chip], recv.at[my_chip], peer, recv.at[chip]))
        return out

    return plan


def _scatter2_local(n):
    def plan(refs):
        my_chip = _place()[1]
        return [(refs[wi].at[my_chip], refs[n + wi].at[my_chip]) for wi in range(n)]

    return plan


def _pair_add(parts, half, core, *, name):
    _, r, c = parts.shape
    tr = max(d for d in range(HALO, 257, HALO) if r % d == 0) if r > 256 else r
    parts4 = parts.reshape(N_CHIP, 2, r, c)

    def body(core_ref, p_ref, h_ref, o_ref):
        o_ref[...] = (p_ref[:, 0].astype(F32) + h_ref[...].astype(F32)).astype(o_ref.dtype)

    return pl.pallas_call(
        body, name=name,
        grid_spec=pltpu.PrefetchScalarGridSpec(
            num_scalar_prefetch=1, grid=(r // tr,),
            in_specs=[pl.BlockSpec((N_CHIP, 1, tr, c), lambda i, core_ref: (0, core_ref[0], i, 0)),
                      pl.BlockSpec((N_CHIP, tr, c), lambda i, core_ref: (0, i, 0))],
            out_specs=pl.BlockSpec((N_CHIP, tr, c), lambda i, core_ref: (0, i, 0))),
        out_shape=jax.ShapeDtypeStruct((N_CHIP, r, c), parts.dtype), compiler_params=_cp("parallel"),
    )(core, parts4, half)


def _scatter_start(parts, *, name):
    halves = [lax.empty((N_CHIP,) + v.shape[1:], v.dtype) for v in parts]
    return _split_start(parts + halves, N_CHIP * len(parts), _scatter1_plan(len(parts)), name=name)


def _adamw(parts, w, m, v, *, name):
    r, c = w.shape
    nparts = parts.shape[0]
    tr = max(d for d in range(HALO, 129, HALO) if r % d == 0) if r > 128 else r

    def body(p_ref, w_ref, m_ref, v_ref, g_ref, d_ref, mo_ref, vo_ref):
        g = p_ref[0].astype(F32)
        for k in range(1, nparts):
            g = g + p_ref[k].astype(F32)
        mn = ADAM_B1 * m_ref[...] + (1.0 - ADAM_B1) * g
        vn = ADAM_B2 * v_ref[...] + (1.0 - ADAM_B2) * (g * g)
        m_hat = mn / (1.0 - ADAM_B1 ** ADAM_STEP)
        v_hat = vn / (1.0 - ADAM_B2 ** ADAM_STEP)
        g_ref[...] = g
        d_ref[...] = -ADAM_LR * (m_hat / (jnp.sqrt(v_hat) + ADAM_EPS) + ADAM_WD * w_ref[...])
        mo_ref[...] = mn
        vo_ref[...] = vn

    blk = pl.BlockSpec((tr, c), lambda i: (i, 0))
    return pl.pallas_call(
        body, name=name, grid=(r // tr,), in_specs=[pl.BlockSpec((nparts, tr, c), lambda i: (0, i, 0)), blk, blk, blk],
        out_specs=[blk] * 4, out_shape=[jax.ShapeDtypeStruct((r, c), F32)] * 4, compiler_params=_cp("parallel"),
    )(parts, w, m, v)


def _pack_rows(vs, rows):
    lead = vs[0].shape[:-1] if vs[0].ndim > 1 else ()
    flat = jnp.concatenate(vs, axis=-1)
    pad = rows * LANES - flat.shape[-1]
    flat = jnp.pad(flat, [(0, 0)] * len(lead) + [(0, pad)])
    return flat.reshape(lead + (rows, LANES))


def kernel(x, p, positions, mix_norm_w, w_in, conv_w, conv_b, dt_bias, a_log, d_skip, ssd_norm_w, q_a_norm_w, w_q_b, kv_a_norm_w, w_kv_b, w_out, ffn_norm_w, w_ffn_up, ffn_conv_w, ffn_conv_b, w_ffn_down, ple_norm_w, w_ple_gate, b_ple_gate, w_ple_proj, ple_post_norm_w, final_norm_w, loss_target, m_mix_norm_w, m_w_in, m_conv_w, m_conv_b, m_dt_bias, m_a_log, m_d_skip, m_ssd_norm_w, m_q_a_norm_w, m_w_q_b, m_kv_a_norm_w, m_w_kv_b, m_w_out, m_ffn_norm_w, m_w_ffn_up, m_ffn_conv_w, m_ffn_conv_b, m_w_ffn_down, m_ple_norm_w, m_w_ple_gate, m_b_ple_gate, m_w_ple_proj, m_ple_post_norm_w, m_final_norm_w, v_mix_norm_w, v_w_in, v_conv_w, v_conv_b, v_dt_bias, v_a_log, v_d_skip, v_ssd_norm_w, v_q_a_norm_w, v_w_q_b, v_kv_a_norm_w, v_w_kv_b, v_w_out, v_ffn_norm_w, v_w_ffn_up, v_ffn_conv_w, v_ffn_conv_b, v_w_ffn_down, v_ple_norm_w, v_w_ple_gate, v_b_ple_gate, v_w_ple_proj, v_ple_post_norm_w, v_final_norm_w):
    given = dict(locals())
    shapes = {n: given[n].shape for n in WEIGHTS}
    w2 = {n: given[n].reshape(given[n].shape[-2:] if n in BIG or n in CONV else (1, -1)) for n in WEIGHTS}
    m2 = {n: given['m_' + n].reshape(w2[n].shape) for n in WEIGHTS}
    v2 = {n: given['v_' + n].reshape(w2[n].shape) for n in WEIGHTS}
    me = 4 * lax.axis_index("x") + 2 * lax.axis_index("y") + lax.axis_index("c")

    core = lax.axis_index("c").astype(jnp.int32).reshape(1)

    def shards(grp, zero):
        return [(w2[n] + zero).astype(BF16) if n in BIG else w2[n] + zero for n in WEIGHT_GROUPS[grp]]

    first, token = _gather_start(shards('a', 0.0), name="gather_a_hop1")
    first, token = _gather_forward(first, token, name="gather_a_hop2")
    zero = token[0, 0]
    later = _sequencer_gather(shards('b', zero) + shards('c', zero), collective_id=1, name="gather_later")
    later = dict(zip(WEIGHT_GROUPS['b'] + WEIGHT_GROUPS['c'], later))

    def get_w(grp, after):
        if grp == 'a':
            lands = dict(zip(WEIGHT_GROUPS[grp], _gather_finish(first, token, name="gather_a_done")))
        else:
            lands = {n: later[n] for n in WEIGHT_GROUPS[grp]}
        return _assemble_weights(lands)

    scatters = {}

    hop_ids = {grp: 2 + 2 * i for i, grp in enumerate(GRAD_GROUPS)}

    def zero_of(arrays):
        return sum(v[(0,) * v.ndim].astype(F32) * 0.0 for v in arrays)

    def emit(grp, grads):
        scatters[grp], tok = _scatter_start([grads[n] for n in GRAD_GROUPS[grp]], name="scatter_" + grp + "_hop1")
        return tok[0, 0]

    def relay(grp, after):
        n = len(GRAD_GROUPS[grp])
        bufs = _split_wait(scatters[grp], after, _scatter1_plan(n), lambda refs: [], name="scatter_" + grp + "_hop1_wait")
        sums = [_pair_add(bufs[i], bufs[n + i], core, name="scatter_%s_add%d" % (grp, i)) for i in range(n)]
        scatters[grp] = _sequencer_scatter_hop2(sums, collective_id=hop_ids[grp] + 1, name="scatter_" + grp + "_hop2")
        return zero_of(sums)

    out_g, out_d, out_m, out_v = {}, {}, {}, {}

    def settle(grp):
        return zero_of(scatters[grp])

    def update(grp, behind=None):
        for n, parts in zip(GRAD_GROUPS[grp], scatters[grp]):
            wn = w2[n] if behind is None else w2[n] + behind
            out_g[n], out_d[n], out_m[n], out_v[n] = _adamw(parts, wn, m2[n], v2[n], name="adamw_" + n)

    vecs = {n: w2[n] for n in REPL}
    vecs['mix_norm_w'] = vecs['mix_norm_w'] + zero
    loss, dx, g_conv, g_vec = _local_step(x[0], p[0, 0], _rope_tables(positions), get_w, vecs, loss_target[0], emit,
                                          relay, settle)
    n_small = sum(g_vec[n].shape[1] for n in REPL) + sum(g_conv[n].size for n in CONV) + 1
    rows_small = -(-n_small // (LANES * HALO)) * HALO
    small = _pack_rows([g_vec[n] for n in REPL] + [g_conv[n].reshape(1, -1) for n in CONV] + [loss], rows_small)

    for grp in list(GRAD_GROUPS)[:-1]:
        update(grp)
    all_small = _exchange([small], gather=True, name="gather_small_grads")[0].reshape(N_DEV, rows_small * LANES)
    update(list(GRAD_GROUPS)[-1], zero_of([all_small]))
    pieces, off = [], 0
    for n in REPL:
        k = g_vec[n].shape[1]
        pieces.append(all_small[:, off:off + k])
        off += k
    for n in CONV:
        kw, cols = g_conv[n].shape
        full = all_small[:, off:off + kw * cols].reshape(N_DEV, kw, cols)
        mine = lax.dynamic_slice_in_dim(full, me * (cols // N_DEV), cols // N_DEV, axis=2)
        pieces.append(mine.reshape(N_DEV, kw * (cols // N_DEV)))
        off += kw * cols
    pieces.append(all_small[:, off:off + 1])
    small_names = REPL + CONV
    n_mine = sum(q.shape[1] for q in pieces)
    rows_mine = -(-n_mine // (LANES * HALO)) * HALO
    zero = jnp.zeros((1, 1), F32)
    packed = [_pack_rows([src[n].reshape(1, -1) for n in small_names] + [zero], rows_mine).reshape(rows_mine, LANES)
              for src in (w2, m2, v2)]
    sg, sd, sm, sv = _adamw(_pack_rows(pieces, rows_mine), *packed, name="adamw_small")
    off = 0
    for n in small_names:
        k = w2[n].size
        for dst, src in ((out_g, sg), (out_d, sd), (out_m, sm), (out_v, sv)):
            dst[n] = src.reshape(-1)[off:off + k].reshape(w2[n].shape)
        off += k
    total_loss = sg.reshape(-1)[off]

    outs = [total_loss, dx[None]]
    for res in (out_g, out_d, out_m, out_v):
        outs += [res[n].reshape(shapes[n]) for n in WEIGHTS]
    return tuple(outs)
```

```python
import math

import numpy as np
import jax
import jax.numpy as jnp
from jax import lax
from jax.experimental import pallas as pl
from jax.experimental.pallas import tpu as pltpu
from jax.experimental.pallas import tpu_sc as plsc

F32 = jnp.float32
BF16 = jnp.bfloat16
HI = lax.Precision.HIGHEST

D_MODEL = 2048
CHUNK = 64
D_SSM = 1024
SSD_P = 64
SSD_HEADS = 16
SSD_GROUPS = 2
SSD_N = 128
SSD_CONV = 4
SSD_CONV_DIM = D_SSM + 2 * SSD_GROUPS * SSD_N
MLA_HEADS = 8
MLA_NOPE = 128
MLA_ROPE = 64
MLA_V = 128
MLA_Q_RANK = 512
MLA_KV_RANK = 256
MLA_QK_PAD = 256
ROPE_THETA = 10000.0
D_FF = 5632
FFN_CONV = 3
PLE_DIM = 256
NORM_EPS = 1e-6
ADAM_LR, ADAM_B1, ADAM_B2, ADAM_EPS, ADAM_WD, ADAM_STEP = 0.001, 0.9, 0.999, 1e-08, 0.01, 10
N_DEV = 8

OFF_Z, OFF_XBC, OFF_QA, OFF_CKV, OFF_KR, OFF_DT, D_IN_PAD = 0, 1024, 2560, 3072, 3328, 3456, 3584
D_IN = 3408
LANES = 128
HALO = 8
VMEM_LIMIT = 56 * 1024 * 1024
FFN_TC = D_FF * 2 // N_DEV
FFN_PERM = (0, 4, 1, 5, 2, 6, 3, 7)
NEG = -1e30


def _cp(*sem):
    return pltpu.CompilerParams(dimension_semantics=tuple(sem), vmem_limit_bytes=VMEM_LIMIT)


def _tile(n, want):
    if n <= want:
        return n
    best = max(d for d in range(LANES, want + 1, LANES) if n % d == 0)
    return best


def _sigmoid(x):
    return 0.5 * (jnp.tanh(0.5 * x) + 1.0)


def _silu(x):
    return x * _sigmoid(x)


def _dsilu(x):
    s = _sigmoid(x)
    return s * (1.0 + x * (1.0 - s))


MM_TILE = 1408
MM_TK = 2816


def _matmul(a, b, *, ta=False, tb=False, out_dtype=F32, add=None, bias=None, tm=MM_TILE, tn=MM_TILE, tk=MM_TK, name,
            mnk=None, a_spec=None, b_spec=None, o_spec=None, o_shape=None):
    if mnk is None:
        m, k = (a.shape[1], a.shape[0]) if ta else a.shape
        n = b.shape[0] if tb else b.shape[1]
        assert k == (b.shape[1] if tb else b.shape[0])
    else:
        m, n, k = mnk
    tm, tn, tk = _tile(m, tm), _tile(n, tn), _tile(k, tk)
    nk = k // tk
    dims = (((0 if ta else 1,), (1 if tb else 0,)), ((), ()))

    def body(*refs):
        a_ref, b_ref = refs[0], refs[1]
        pos = 2
        add_ref = bias_ref = None
        if add is not None:
            add_ref = refs[pos]
            pos += 1
        if bias is not None:
            bias_ref = refs[pos]
            pos += 1
        o_ref = refs[pos]
        kk = pl.program_id(2)
        av = a_ref[...]
        bv = b_ref[...]
        av = av.reshape(av.shape[-2:]).astype(BF16)
        bv = bv.reshape(bv.shape[-2:]).astype(BF16)
        prod = lax.dot_general(av, bv, dims, preferred_element_type=F32)

        def finish(r):
            if bias_ref is not None:
                r = r + bias_ref[...]
            if add_ref is not None:
                r = r + add_ref[...].astype(F32)
            o_ref[...] = r.astype(out_dtype).reshape(o_ref.shape)

        if nk == 1:
            finish(prod)
        else:
            acc_ref = refs[pos + 1]

            @pl.when(kk == 0)
            def _():
                acc_ref[...] = prod

            @pl.when(kk > 0)
            def _():
                acc_ref[...] += prod

            @pl.when(kk == nk - 1)
            def _():
                finish(acc_ref[...])

    if a_spec is None:
        a_spec = (pl.BlockSpec((tk, tm), lambda i, j, kk: (kk, i)) if ta
                  else pl.BlockSpec((tm, tk), lambda i, j, kk: (i, kk)))
    if b_spec is None:
        b_spec = (pl.BlockSpec((tn, tk), lambda i, j, kk: (j, kk)) if tb
                  else pl.BlockSpec((tk, tn), lambda i, j, kk: (kk, j)))
    if o_spec is None:
        o_spec = pl.BlockSpec((tm, tn), lambda i, j, kk: (i, j))
    if o_shape is None:
        o_shape = (m, n)
    in_specs = [a_spec, b_spec]
    args = [a, b]
    if add is not None:
        in_specs.append(pl.BlockSpec((tm, tn), lambda i, j, kk: (i, j)))
        args.append(add)
    if bias is not None:
        in_specs.append(pl.BlockSpec((1, tn), lambda i, j, kk: (0, j)))
        args.append(bias)
    return pl.pallas_call(
        body, name=name, grid=(m // tm, n // tn, nk), in_specs=in_specs, out_specs=o_spec,
        out_shape=jax.ShapeDtypeStruct(o_shape, out_dtype),
        scratch_shapes=[pltpu.VMEM((tm, tn), F32)] if nk > 1 else [],
        compiler_params=_cp("parallel", "parallel", "arbitrary"),
    )(*args)


def _rmsnorm_fwd(x, w, *, width, cblk=0, out_dtype=BF16, tr=256, name):
    t = x.shape[0]

    def body(x_ref, w_ref, o_ref):
        xv = x_ref[...].astype(F32)
        r = lax.rsqrt(jnp.mean(xv * xv, axis=-1, keepdims=True) + NORM_EPS)
        o_ref[...] = (xv * r * w_ref[...]).astype(out_dtype)

    return pl.pallas_call(
        body, name=name, grid=(t // tr,),
        in_specs=[pl.BlockSpec((tr, width), lambda i: (i, cblk)), pl.BlockSpec((1, width), lambda i: (0, 0))],
        out_specs=pl.BlockSpec((tr, width), lambda i: (i, 0)),
        out_shape=jax.ShapeDtypeStruct((t, width), out_dtype),
        compiler_params=_cp("parallel"),
    )(x, w)


def _rmsnorm_bwd(x, w, dy, add=None, *, width, cblk=0, out_dtype=F32, tr=256, name):
    t = x.shape[0]

    def body(*refs):
        if add is None:
            x_ref, w_ref, dy_ref, dx_ref, dw_ref = refs
            add_ref = None
        else:
            x_ref, w_ref, dy_ref, add_ref, dx_ref, dw_ref = refs
        xv = x_ref[...].astype(F32)
        dyv = dy_ref[...].astype(F32)
        r = lax.rsqrt(jnp.mean(xv * xv, axis=-1, keepdims=True) + NORM_EPS)
        xh = xv * r
        g = dyv * w_ref[...]
        dx = r * (g - xh * jnp.mean(g * xh, axis=-1, keepdims=True))
        if add_ref is not None:
            dx = dx + add_ref[...].astype(F32)
        dx_ref[...] = dx.astype(out_dtype)

        @pl.when(pl.program_id(0) == 0)
        def _():
            dw_ref[...] = jnp.zeros_like(dw_ref)

        dw_ref[...] += jnp.sum(dyv * xh, axis=0, keepdims=True)

    in_specs = [pl.BlockSpec((tr, width), lambda i: (i, cblk)), pl.BlockSpec((1, width), lambda i: (0, 0)),
                pl.BlockSpec((tr, width), lambda i: (i, 0))]
    args = [x, w, dy]
    if add is not None:
        in_specs.append(pl.BlockSpec((tr, width), lambda i: (i, 0)))
        args.append(add)
    return pl.pallas_call(
        body, name=name, grid=(t // tr,), in_specs=in_specs,
        out_specs=[pl.BlockSpec((tr, width), lambda i: (i, 0)), pl.BlockSpec((1, width), lambda i: (0, 0))],
        out_shape=[jax.ShapeDtypeStruct((t, width), out_dtype), jax.ShapeDtypeStruct((1, width), F32)],
        compiler_params=_cp("arbitrary"),
    )(*args)


def _shift_down(prev_halo, cur, j):
    if j == 0:
        return cur
    ext = jnp.concatenate([prev_halo, cur], axis=0)
    return pltpu.roll(ext, j, axis=0)[HALO:]


def _shift_up(cur, next_halo, j):
    if j == 0:
        return cur
    ext = jnp.concatenate([cur, next_halo], axis=0)
    return pltpu.roll(ext, ext.shape[0] - j, axis=0)[:cur.shape[0]]


def _conv_rows(prev, cur, w, b, kw):
    shifted = [cur]
    out = b + w[kw - 1:kw] * cur
    for j in range(1, kw):
        sh = _shift_down(prev, cur, j)
        shifted.append(sh)
        out = out + w[kw - 1 - j:kw - j] * sh
    return out, shifted


def _act_fwd(c, glu):
    if glu:
        half = c.shape[1] // 2
        return _silu(c[:, :half]) * c[:, half:]
    return _silu(c)


def _act_bwd(c, dout, glu):
    if glu:
        half = c.shape[1] // 2
        g, up = c[:, :half], c[:, half:]
        s = _sigmoid(g)
        gs = g * s
        return jnp.concatenate([dout * up * (s + gs * (1.0 - s)), dout * gs], axis=1)
    return dout * _dsilu(c)


def _conv_act_fwd(u, w, b, *, kw, glu, tc, coff, ncols, out_dtype, tr=256, name):
    t = u.shape[0]
    nb = ncols // tc
    oc = tc // 2 if glu else tc

    def body(u_ref, uh_ref, w_ref, b_ref, o_ref):
        prev = jnp.where(pl.program_id(0) == 0, 0.0, uh_ref[...])
        c, _ = _conv_rows(prev, u_ref[...], w_ref[...], b_ref[...], kw)
        o_ref[...] = _act_fwd(c, glu).astype(out_dtype)

    return pl.pallas_call(
        body, name=name, grid=(t // tr, nb),
        in_specs=[pl.BlockSpec((tr, tc), lambda i, j: (i, j + coff)),
                  pl.BlockSpec((HALO, tc), lambda i, j: (jnp.maximum(i * (tr // HALO) - 1, 0), j + coff)),
                  pl.BlockSpec((kw, tc), lambda i, j: (0, j)), pl.BlockSpec((1, tc), lambda i, j: (0, j))],
        out_specs=pl.BlockSpec((tr, oc), lambda i, j: (i, j)),
        out_shape=jax.ShapeDtypeStruct((t, nb * oc), out_dtype),
        compiler_params=_cp("parallel", "parallel"),
    )(u, u, w, b)


def _conv_act_bwd(u, w, b, dout, *, kw, glu, tc, coff, ncols, tr=256, name):
    t = u.shape[0]
    nb = ncols // tc
    nt = t // tr
    oc = tc // 2 if glu else tc

    def body(u_ref, up_ref, un_ref, d_ref, dn_ref, w_ref, b_ref, du_ref, dw_ref, db_ref):
        i = pl.program_id(1)
        cur, nxt, wv, bv = u_ref[...], un_ref[...], w_ref[...], b_ref[...]
        prev = jnp.where(i == 0, 0.0, up_ref[...])
        c_cur, shifted = _conv_rows(prev, cur, wv, bv, kw)
        c_nxt, _ = _conv_rows(cur[tr - HALO:], nxt, wv, bv, kw)
        d_cur = _act_bwd(c_cur, d_ref[...].astype(F32), glu)
        d_nxt = _act_bwd(c_nxt, jnp.where(i == nt - 1, 0.0, dn_ref[...].astype(F32)), glu)
        du = wv[kw - 1:kw] * d_cur
        for j in range(1, kw):
            du = du + wv[kw - 1 - j:kw - j] * _shift_up(d_cur, d_nxt, j)
        du_ref[...] = du.astype(BF16)

        @pl.when(i == 0)
        def _():
            dw_ref[...] = jnp.zeros_like(dw_ref)
            db_ref[...] = jnp.zeros_like(db_ref)

        db_ref[...] += jnp.sum(d_cur, axis=0, keepdims=True)
        dw_ref[...] += jnp.concatenate(
            [jnp.sum(d_cur * shifted[kw - 1 - k], axis=0, keepdims=True) for k in range(kw)], axis=0)

    nh = tr // HALO
    return pl.pallas_call(
        body, name=name, grid=(nb, nt),
        in_specs=[pl.BlockSpec((tr, tc), lambda j, i: (i, j + coff)),
                  pl.BlockSpec((HALO, tc), lambda j, i: (jnp.maximum(i * nh - 1, 0), j + coff)),
                  pl.BlockSpec((HALO, tc), lambda j, i: (jnp.minimum((i + 1) * nh, t // HALO - 1), j + coff)),
                  pl.BlockSpec((tr, oc), lambda j, i: (i, j)),
                  pl.BlockSpec((HALO, oc), lambda j, i: (jnp.minimum((i + 1) * nh, t // HALO - 1), j)),
                  pl.BlockSpec((kw, tc), lambda j, i: (0, j)), pl.BlockSpec((1, tc), lambda j, i: (0, j))],
        out_specs=[pl.BlockSpec((tr, tc), lambda j, i: (i, j)), pl.BlockSpec((kw, tc), lambda j, i: (0, j)),
                   pl.BlockSpec((1, tc), lambda j, i: (0, j))],
        out_shape=[jax.ShapeDtypeStruct((t, ncols), BF16), jax.ShapeDtypeStruct((kw, ncols), F32),
                   jax.ShapeDtypeStruct((1, ncols), F32)],
        compiler_params=_cp("parallel", "arbitrary"),
    )(u, u, u, dout, dout, w, b)


def _ple_fwd(x2, gl, pe, pw, *, tr=256, name):
    t, d = x2.shape

    def body(x_ref, gl_ref, pe_ref, pw_ref, o_ref):
        pv = pe_ref[...]
        r = lax.rsqrt(jnp.mean(pv * pv, axis=-1, keepdims=True) + NORM_EPS)
        o_ref[...] = x_ref[...] + _sigmoid(gl_ref[...]) * (pv * r * pw_ref[...])

    blk = pl.BlockSpec((tr, d), lambda i: (i, 0))
    return pl.pallas_call(
        body, name=name, grid=(t // tr,), in_specs=[blk, blk, blk, pl.BlockSpec((1, d), lambda i: (0, 0))],
        out_specs=blk, out_shape=jax.ShapeDtypeStruct((t, d), F32), compiler_params=_cp("parallel"),
    )(x2, gl, pe, pw)


def _ple_bwd(dx3, gl, pe, pw, *, tr=256, name):
    t, d = dx3.shape

    def body(dx_ref, gl_ref, pe_ref, pw_ref, dgl_ref, db_ref, dpe_ref, dpw_ref):
        dx, pv, pwv = dx_ref[...], pe_ref[...], pw_ref[...]
        gate = _sigmoid(gl_ref[...])
        r = lax.rsqrt(jnp.mean(pv * pv, axis=-1, keepdims=True) + NORM_EPS)
        ph = pv * r
        dgl = dx * (ph * pwv) * gate * (1.0 - gate)
        de = dx * gate
        g = de * pwv
        dgl_ref[...] = dgl.astype(BF16)
        dpe_ref[...] = (r * (g - ph * jnp.mean(g * ph, axis=-1, keepdims=True))).astype(BF16)

        @pl.when(pl.program_id(0) == 0)
        def _():
            db_ref[...] = jnp.zeros_like(db_ref)
            dpw_ref[...] = jnp.zeros_like(dpw_ref)

        db_ref[...] += jnp.sum(dgl, axis=0, keepdims=True)
        dpw_ref[...] += jnp.sum(de * ph, axis=0, keepdims=True)

    blk = pl.BlockSpec((tr, d), lambda i: (i, 0))
    row = pl.BlockSpec((1, d), lambda i: (0, 0))
    return pl.pallas_call(
        body, name=name, grid=(t // tr,), in_specs=[blk, blk, blk, row], out_specs=[blk, row, blk, row],
        out_shape=[jax.ShapeDtypeStruct((t, d), BF16), jax.ShapeDtypeStruct((1, d), F32),
                   jax.ShapeDtypeStruct((t, d), BF16), jax.ShapeDtypeStruct((1, d), F32)],
        compiler_params=_cp("arbitrary"),
    )(dx3, gl, pe, pw)


def _loss_head(x3, fw, target, *, tr=256, name):
    t, d = x3.shape

    def body(x_ref, w_ref, t_ref, l_ref, dx_ref, dw_ref):
        xv, wv = x_ref[...], w_ref[...]
        r = lax.rsqrt(jnp.mean(xv * xv, axis=-1, keepdims=True) + NORM_EPS)
        xh = xv * r
        err = xh * wv - t_ref[...]
        dy = err * (1.0 / d)
        g = dy * wv
        dx_ref[...] = r * (g - xh * jnp.mean(g * xh, axis=-1, keepdims=True))

        @pl.when(pl.program_id(0) == 0)
        def _():
            l_ref[...] = jnp.zeros_like(l_ref)
            dw_ref[...] = jnp.zeros_like(dw_ref)

        l_ref[...] += 0.5 * jnp.sum(jnp.mean(err * err, axis=-1, keepdims=True), axis=0, keepdims=True)
        dw_ref[...] += jnp.sum(dy * xh, axis=0, keepdims=True)

    blk = pl.BlockSpec((tr, d), lambda i: (i, 0))
    row = pl.BlockSpec((1, d), lambda i: (0, 0))
    return pl.pallas_call(
        body, name=name, grid=(t // tr,), in_specs=[blk, row, blk],
        out_specs=[pl.BlockSpec((1, 1), lambda i: (0, 0)), blk, row],
        out_shape=[jax.ShapeDtypeStruct((1, 1), F32), jax.ShapeDtypeStruct((t, d), F32),
                   jax.ShapeDtypeStruct((1, d), F32)],
        compiler_params=_cp("arbitrary"),
    )(x3, fw, target)


def _rope(blk, tab_ref):
    return blk * tab_ref[0] + pltpu.roll(blk, 96, axis=1) * tab_ref[1] + pltpu.roll(blk, 32, axis=1) * tab_ref[2]


def _unrope(g, tab_ref):
    return g * tab_ref[0] + pltpu.roll(g * tab_ref[1], 32, axis=1) + pltpu.roll(g * tab_ref[2], 96, axis=1)


def _mla_prep(q, kv, proj, tabs, *, tr=512, name):
    t = q.shape[0]

    def body(q_ref, kv_ref, kr_ref, tab_ref, qo_ref, ko_ref, vo_ref, vt_ref):
        qv, kvv = q_ref[...], kv_ref[...]
        qo_ref[0, :, :MLA_NOPE] = qv[:, :MLA_NOPE].astype(BF16)
        qo_ref[0, :, MLA_NOPE:] = _rope(qv[:, MLA_NOPE:], tab_ref).astype(BF16)
        ko_ref[0, :, :MLA_NOPE] = kvv[:, :MLA_NOPE].astype(BF16)
        ko_ref[0, :, MLA_NOPE:] = _rope(kr_ref[...], tab_ref).astype(BF16)
        vo_ref[0] = kvv[:, MLA_NOPE:].astype(BF16)
        for blk in range(tr // ATT_BLK):
            vt_ref[0, blk] = kvv[blk * ATT_BLK:(blk + 1) * ATT_BLK, MLA_NOPE:].T.astype(BF16)

    return pl.pallas_call(
        body, name=name, grid=(t // tr, MLA_HEADS),
        in_specs=[pl.BlockSpec((tr, MLA_QK_PAD), lambda i, h: (i, h)),
                  pl.BlockSpec((tr, MLA_NOPE + MLA_V), lambda i, h: (i, h)),
                  pl.BlockSpec((tr, LANES), lambda i, h: (i, OFF_KR // LANES)),
                  pl.BlockSpec((3, tr, LANES), lambda i, h: (0, i, 0))],
        out_specs=[pl.BlockSpec((1, tr, MLA_QK_PAD), lambda i, h: (h, i, 0)),
                   pl.BlockSpec((1, tr, MLA_QK_PAD), lambda i, h: (h, i, 0)),
                   pl.BlockSpec((1, tr, MLA_V), lambda i, h: (h, i, 0)),
                   pl.BlockSpec((1, tr // ATT_BLK, MLA_V, ATT_BLK), lambda i, h: (h, i, 0, 0))],
        out_shape=[jax.ShapeDtypeStruct((MLA_HEADS, t, MLA_QK_PAD), BF16),
                   jax.ShapeDtypeStruct((MLA_HEADS, t, MLA_QK_PAD), BF16),
                   jax.ShapeDtypeStruct((MLA_HEADS, t, MLA_V), BF16),
                   jax.ShapeDtypeStruct((MLA_HEADS, t // ATT_BLK, MLA_V, ATT_BLK), BF16)],
        compiler_params=_cp("parallel", "parallel"),
    )(q, kv, proj, tabs)


def _mla_unprep(dq3, dk3, dv3, tabs, *, tr=256, name):
    t = dq3.shape[1]

    def body(dq_ref, dk_ref, dv_ref, tab_ref, qo_ref, kvo_ref, kro_ref):
        kr = jnp.zeros((tr, LANES), F32)
        for h in range(MLA_HEADS):
            c0 = h * MLA_QK_PAD
            qo_ref[:, c0:c0 + MLA_NOPE] = dq_ref[h, :, :MLA_NOPE].astype(BF16)
            qo_ref[:, c0 + MLA_NOPE:c0 + MLA_QK_PAD] = _unrope(dq_ref[h, :, MLA_NOPE:], tab_ref).astype(BF16)
            kvo_ref[:, c0:c0 + MLA_NOPE] = dk_ref[h, :, :MLA_NOPE].astype(BF16)
            kvo_ref[:, c0 + MLA_NOPE:c0 + MLA_QK_PAD] = dv_ref[h].astype(BF16)
            kr = kr + dk_ref[h, :, MLA_NOPE:]
        kro_ref[...] = _unrope(kr, tab_ref).astype(BF16)

    return pl.pallas_call(
        body, name=name, grid=(t // tr,),
        in_specs=[pl.BlockSpec((MLA_HEADS, tr, MLA_QK_PAD), lambda i: (0, i, 0)),
                  pl.BlockSpec((MLA_HEADS, tr, MLA_QK_PAD), lambda i: (0, i, 0)),
                  pl.BlockSpec((MLA_HEADS, tr, MLA_V), lambda i: (0, i, 0)),
                  pl.BlockSpec((3, tr, LANES), lambda i: (0, i, 0))],
        out_specs=[pl.BlockSpec((tr, MLA_HEADS * MLA_QK_PAD), lambda i: (i, 0)),
                   pl.BlockSpec((tr, MLA_HEADS * MLA_QK_PAD), lambda i: (i, 0)),
                   pl.BlockSpec((tr, LANES), lambda i: (i, 0))],
        out_shape=[jax.ShapeDtypeStruct((t, MLA_HEADS * MLA_QK_PAD), BF16),
                   jax.ShapeDtypeStruct((t, MLA_HEADS * MLA_QK_PAD), BF16),
                   jax.ShapeDtypeStruct((t, LANES), BF16)],
        compiler_params=_cp("parallel"),
    )(dq3, dk3, dv3, tabs)


ATT_BLK = 256
ATT_SCALE = 1.0 / math.sqrt(MLA_NOPE + MLA_ROPE)
_NT = (((1,), (1,)), ((), ()))
_TN = (((0,), (0,)), ((), ()))


def _att_scores_t(k, q, diagonal):
    s = lax.dot_general(k, q, _NT, preferred_element_type=F32) * ATT_SCALE
    if not diagonal:
        return s
    key = lax.broadcasted_iota(jnp.int32, s.shape, 0)
    query = lax.broadcasted_iota(jnp.int32, s.shape, 1)
    return jnp.where((key >> 6) <= (query >> 6), s, NEG)


def _att_rows(i):
    return pl.ds(pl.multiple_of(i * ATT_BLK, ATT_BLK), ATT_BLK)


ATT_HEADS = 2


def _attn_fwd(q3, k3, vt4, *, name):
    t = q3.shape[1]
    nq = t // ATT_BLK

    def body(q_ref, k_ref, vt_ref, o_ref, lse_ref):
        qi = pl.program_id(1)
        qs = [q_ref[hh] for hh in range(ATT_HEADS)]

        def step(j, carry, diagonal=False):
            out = []
            for hh, (m, l, acc) in enumerate(carry):
                s = _att_scores_t(k_ref[hh, _att_rows(j), :], qs[hh], diagonal)
                m_new = jnp.maximum(m, jnp.max(s, axis=0, keepdims=True))
                p = jnp.exp(s - m_new)
                alpha = jnp.exp(m - m_new)
                l = alpha * l + jnp.sum(p, axis=0, keepdims=True)
                acc = alpha * acc + jnp.dot(vt_ref[hh, j], p.astype(BF16), preferred_element_type=F32)
                out.append((m_new, l, acc))
            return tuple(out)

        init = tuple((jnp.full((1, ATT_BLK), NEG, F32), jnp.zeros((1, ATT_BLK), F32),
                      jnp.zeros((MLA_V, ATT_BLK), F32)) for _ in range(ATT_HEADS))
        done = step(qi, lax.fori_loop(0, qi, step, init), diagonal=True)
        for hh, (m, l, acc) in enumerate(done):
            o_ref[:, hh * MLA_V:(hh + 1) * MLA_V] = (acc / l).T
            lse_ref[hh, 0] = m + jnp.log(l)

    return pl.pallas_call(
        body, name=name, grid=(MLA_HEADS // ATT_HEADS, nq),
        in_specs=[pl.BlockSpec((ATT_HEADS, ATT_BLK, MLA_QK_PAD), lambda h, i: (h, i, 0)),
                  pl.BlockSpec((ATT_HEADS, t, MLA_QK_PAD), lambda h, i: (h, 0, 0)),
                  pl.BlockSpec((ATT_HEADS, nq, MLA_V, ATT_BLK), lambda h, i: (h, 0, 0, 0))],
        out_specs=[pl.BlockSpec((ATT_BLK, ATT_HEADS * MLA_V), lambda h, i: (i, h)),
                   pl.BlockSpec((ATT_HEADS, 1, 1, ATT_BLK), lambda h, i: (h, i, 0, 0))],
        out_shape=[jax.ShapeDtypeStruct((t, MLA_HEADS * MLA_V), F32),
                   jax.ShapeDtypeStruct((MLA_HEADS, nq, 1, ATT_BLK), F32)],
        compiler_params=_cp("parallel", "parallel"),
    )(q3, k3, vt4)


def _attn_bwd(q3, k3, v3, o, dcat, lse, *, name):
    t = q3.shape[1]
    nq = t // ATT_BLK
    wide = ATT_HEADS * MLA_V

    def body(q_ref, k_ref, v_ref, o_ref, do_ref, lse_ref, dq_ref, dk_ref, dv_ref, delta_ref):
        kj = pl.program_id(1)

        @pl.when(kj == 0)
        def _():
            dq_ref[...] = jnp.zeros_like(dq_ref)
            ones = jnp.ones((HALO, MLA_V), F32)
            for i in range(nq):
                rows = pl.ds(i * ATT_BLK, ATT_BLK)
                prod = o_ref[rows, :] * do_ref[rows, :]
                for hh in range(ATT_HEADS):
                    delta_ref[hh, i] = lax.dot_general(ones, prod[:, hh * MLA_V:(hh + 1) * MLA_V], _NT, precision=HI,
                                                       preferred_element_type=F32)

        def step(i, carry, diagonal=False):
            rows = _att_rows(i)
            out = []
            for hh, (dk, dv) in enumerate(carry):
                k, v = k_ref[hh], v_ref[hh]
                q = q_ref[hh, rows, :]
                dob = do_ref[rows, hh * MLA_V:(hh + 1) * MLA_V].astype(BF16)
                p = jnp.exp(_att_scores_t(k, q, diagonal) - lse_ref[hh, i])
                dv = dv + jnp.dot(p.astype(BF16), dob, preferred_element_type=F32)
                dp = lax.dot_general(v, dob, _NT, preferred_element_type=F32)
                ds = (p * (dp - delta_ref[hh, i, 0:1, :]) * ATT_SCALE).astype(BF16)
                dk = dk + jnp.dot(ds, q, preferred_element_type=F32)
                dq_ref[hh, rows, :] += lax.dot_general(ds, k, _TN, preferred_element_type=F32)
                out.append((dk, dv))
            return tuple(out)

        init = tuple((jnp.zeros((ATT_BLK, MLA_QK_PAD), F32), jnp.zeros((ATT_BLK, MLA_V), F32))
                     for _ in range(ATT_HEADS))
        done = lax.fori_loop(kj + 1, nq, step, step(kj, init, diagonal=True))
        for hh, (dk, dv) in enumerate(done):
            dk_ref[hh] = dk
            dv_ref[hh] = dv

    return pl.pallas_call(
        body, name=name, grid=(MLA_HEADS // ATT_HEADS, nq),
        in_specs=[pl.BlockSpec((ATT_HEADS, t, MLA_QK_PAD), lambda h, j: (h, 0, 0)),
                  pl.BlockSpec((ATT_HEADS, ATT_BLK, MLA_QK_PAD), lambda h, j: (h, j, 0)),
                  pl.BlockSpec((ATT_HEADS, ATT_BLK, MLA_V), lambda h, j: (h, j, 0)),
                  pl.BlockSpec((t, wide), lambda h, j: (0, h)),
                  pl.BlockSpec((t, wide), lambda h, j: (0, MLA_HEADS // ATT_HEADS + h)),
                  pl.BlockSpec((ATT_HEADS, nq, 1, ATT_BLK), lambda h, j: (h, 0, 0, 0))],
        out_specs=[pl.BlockSpec((ATT_HEADS, t, MLA_QK_PAD), lambda h, j: (h, 0, 0)),
                   pl.BlockSpec((ATT_HEADS, ATT_BLK, MLA_QK_PAD), lambda h, j: (h, j, 0)),
                   pl.BlockSpec((ATT_HEADS, ATT_BLK, MLA_V), lambda h, j: (h, j, 0))],
        out_shape=[jax.ShapeDtypeStruct((MLA_HEADS, t, MLA_QK_PAD), F32),
                   jax.ShapeDtypeStruct((MLA_HEADS, t, MLA_QK_PAD), F32),
                   jax.ShapeDtypeStruct((MLA_HEADS, t, MLA_V), F32)],
        scratch_shapes=[pltpu.VMEM((ATT_HEADS, nq, HALO, ATT_BLK), F32)],
        compiler_params=_cp("parallel", "arbitrary"),
    )(q3, k3, v3, o, dcat, lse)


def _ssd_prep(proj, bias128, alog128, *, name):
    t = proj.shape[0]
    nc = t // CHUNK

    def body(raw_ref, b_ref, al_ref, dt_ref, cs_ref, a_ref):
        xv = raw_ref[...] + b_ref[...]
        dt = jnp.maximum(xv, 0.0) + jnp.log(1.0 + jnp.exp(-jnp.abs(xv)))
        a = -jnp.exp(al_ref[...])
        adt = (dt * a).reshape(nc, CHUNK, LANES)
        li = lax.broadcasted_iota(jnp.int32, (nc, CHUNK, CHUNK), 1)
        si = lax.broadcasted_iota(jnp.int32, (nc, CHUNK, CHUNK), 2)
        tril = jnp.where(si <= li, 1.0, 0.0).astype(F32)
        cs = lax.dot_general(tril, adt, (((2,), (1,)), ((0,), (0,))), precision=HI, preferred_element_type=F32)
        dt_ref[...] = dt
        cs_ref[...] = cs.reshape(t, LANES)
        a_ref[...] = a

    blk = pl.BlockSpec((t, LANES), lambda i: (0, 0))
    row = pl.BlockSpec((1, LANES), lambda i: (0, 0))
    return pl.pallas_call(
        body, name=name, grid=(1,),
        in_specs=[pl.BlockSpec((t, LANES), lambda i: (0, OFF_DT // LANES)), row, row],
        out_specs=[blk, blk, row],
        out_shape=[jax.ShapeDtypeStruct((t, LANES), F32), jax.ShapeDtypeStruct((t, LANES), F32),
                   jax.ShapeDtypeStruct((1, LANES), F32)],
        compiler_params=_cp("arbitrary"),
    )(proj, bias128, alog128)


def _ssd_prep_bwd(ddt128, dadt128, proj, bias128, dt128, a128, dd_h, *, name):
    t = proj.shape[0]

    def body(ddt_ref, dadt_ref, raw_ref, b_ref, dt_ref, a_ref, dd_ref, draw_ref, db_ref, dal_ref, dds_ref):
        draw = ddt_ref[...] * _sigmoid(raw_ref[...] + b_ref[...])
        draw_ref[...] = draw.astype(BF16)
        db_ref[...] = jnp.sum(draw, axis=0, keepdims=True)
        dal_ref[...] = jnp.sum(dadt_ref[...] * dt_ref[...], axis=0, keepdims=True) * a_ref[...]
        dds_ref[...] = jnp.sum(dd_ref[...], axis=-1, keepdims=True)

    blk = pl.BlockSpec((t, LANES), lambda i: (0, 0))
    row = pl.BlockSpec((1, LANES), lambda i: (0, 0))
    return pl.pallas_call(
        body, name=name, grid=(1,),
        in_specs=[blk, blk, pl.BlockSpec((t, LANES), lambda i: (0, OFF_DT // LANES)), row, blk, row,
                  pl.BlockSpec((SSD_HEADS, SSD_P), lambda i: (0, 0))],
        out_specs=[blk, row, row, pl.BlockSpec((SSD_HEADS, 1), lambda i: (0, 0))],
        out_shape=[jax.ShapeDtypeStruct((t, LANES), BF16), jax.ShapeDtypeStruct((1, LANES), F32),
                   jax.ShapeDtypeStruct((1, LANES), F32), jax.ShapeDtypeStruct((SSD_HEADS, 1), F32)],
        compiler_params=_cp("arbitrary"),
    )(ddt128, dadt128, proj, bias128, dt128, a128, dd_h)


def _bdot(a, b, ca, cb, precision=None):
    return lax.dot_general(a, b, (((ca,), (cb,)), ((0,), (0,))), precision=precision, preferred_element_type=F32)


def _head_matrices():
    eye, zero = jnp.eye(SSD_P, dtype=F32), jnp.zeros((SSD_P, SSD_P), F32)
    pick = jnp.stack([jnp.concatenate([eye, zero], axis=0), jnp.concatenate([zero, eye], axis=0)])
    return pick, pick.transpose(0, 2, 1)


def _move(x, sel):
    selb = sel.astype(BF16)
    hi = x.astype(BF16)
    rest = x - hi.astype(F32)
    mid = rest.astype(BF16)
    low = (rest - mid.astype(F32)).astype(BF16)
    out = jnp.dot(hi, selb, preferred_element_type=F32)
    out = out + jnp.dot(mid, selb, preferred_element_type=F32)
    return out + jnp.dot(low, selb, preferred_element_type=F32)


def _pick_head(pair_ref, pick_ref, h):
    return _move(pair_ref[...], pick_ref[h % 2])


def _place_head(out_ref, val, place_ref, h):
    wide = _move(val, place_ref[h % 2])

    @pl.when(h % 2 == 0)
    def _():
        out_ref[...] = wide

    @pl.when(h % 2 == 1)
    def _():
        out_ref[...] += wide


def _ssd_common(x2, dt_ref, cs_ref, csr_ref, b_ref, c_ref, nc):
    x = x2.reshape(nc, CHUNK, SSD_P)
    dt = dt_ref[0].reshape(nc, CHUNK, SSD_P)
    cs = cs_ref[0].reshape(nc, CHUNK, SSD_P)
    csr = csr_ref[0]
    bm = b_ref[...].reshape(nc, CHUNK, SSD_N).astype(BF16)
    cm = c_ref[...].reshape(nc, CHUNK, SSD_N).astype(BF16)
    li = lax.broadcasted_iota(jnp.int32, (nc, CHUNK, CHUNK), 1)
    si = lax.broadcasted_iota(jnp.int32, (nc, CHUNK, CHUNK), 2)
    lmat = jnp.exp(jnp.where(si <= li, cs - csr, NEG))
    g = _bdot(cm, bm, 2, 2)
    cs_last = jnp.sum(jnp.where(li == CHUNK - 1, cs, 0.0), axis=1, keepdims=True)
    xdt = x * dt
    dec = jnp.exp(cs_last - cs)
    return x, dt, cs, bm, cm, li, si, lmat, g, cs_last, xdt, dec


def _ssd_fwd(xbc, dt_h, cs_h, cs_row, dskip_h, *, name):
    t = xbc.shape[0]
    nc = t // CHUNK
    hpg = SSD_HEADS // SSD_GROUPS
    pick, place = _head_matrices()

    def body(xs_ref, dt_ref, cs_ref, csr_ref, b_ref, c_ref, dk_ref, pick_ref, place_ref, y_ref, st_ref, sc_ref, cd_ref):
        h = pl.program_id(0)
        x, dt, cs, bm, cm, li, si, lmat, g, cs_last, xdt, dec = _ssd_common(_pick_head(xs_ref, pick_ref, h), dt_ref,
                                                                           cs_ref, csr_ref, b_ref, c_ref, nc)
        yd = _bdot((g * lmat).astype(BF16), xdt.astype(BF16), 2, 1)
        sc_ref[...] = _bdot(bm, (dec * xdt).astype(BF16), 1, 1)
        cd_ref[...] = jnp.exp(cs_last)

        def step(c, s):
            st_ref[0, c] = s
            return s * cd_ref[c] + sc_ref[c]

        lax.fori_loop(0, nc, step, jnp.zeros((SSD_N, SSD_P), F32))
        yo = _bdot(cm, st_ref[0].astype(BF16), 2, 1) * jnp.exp(cs)
        _place_head(y_ref, (yd + yo + dk_ref[0] * x).reshape(t, SSD_P), place_ref, h)

    head = pl.BlockSpec((1, t, SSD_P), lambda h: (h, 0, 0))
    pair = pl.BlockSpec((t, 2 * SSD_P), lambda h: (0, h // 2))
    nxb = D_SSM // SSD_N
    return pl.pallas_call(
        body, name=name, grid=(SSD_HEADS,),
        in_specs=[pair, head, head, pl.BlockSpec((1, nc, 1, CHUNK), lambda h: (h, 0, 0, 0)),
                  pl.BlockSpec((t, SSD_N), lambda h: (0, nxb + h // hpg)),
                  pl.BlockSpec((t, SSD_N), lambda h: (0, nxb + SSD_GROUPS + h // hpg)),
                  pl.BlockSpec((1, 1, SSD_P), lambda h: (h, 0, 0)),
                  pl.BlockSpec((2, 2 * SSD_P, SSD_P), lambda h: (0, 0, 0)),
                  pl.BlockSpec((2, SSD_P, 2 * SSD_P), lambda h: (0, 0, 0))],
        out_specs=[pair, pl.BlockSpec((1, nc, SSD_N, SSD_P), lambda h: (h, 0, 0, 0))],
        out_shape=[jax.ShapeDtypeStruct((t, D_SSM), F32),
                   jax.ShapeDtypeStruct((SSD_HEADS, nc, SSD_N, SSD_P), F32)],
        scratch_shapes=[pltpu.VMEM((nc, SSD_N, SSD_P), F32), pltpu.VMEM((nc, 1, SSD_P), F32)],
        compiler_params=_cp("arbitrary"),
    )(xbc, dt_h, cs_h, cs_row, xbc, xbc, dskip_h, pick, place)


def _ssd_bwd(xbc, dt_h, cs_h, cs_row, dskip_h, a_h, states, dy, *, name):
    t = xbc.shape[0]
    nc = t // CHUNK
    hpg = SSD_HEADS // SSD_GROUPS
    pick, place = _head_matrices()

    def body(xs_ref, dt_ref, cs_ref, csr_ref, b_ref, c_ref, dk_ref, a_ref, st_ref, dy_ref, pick_ref, place_ref,
             dxs_ref, ddt_ref, dadt_ref, db_ref, dc_ref, dd_ref, dsl_ref, dsc_ref, cd_ref):
        h = pl.program_id(0) * hpg + pl.program_id(1)
        x, dt, cs, bm, cm, li, si, lmat, g, cs_last, xdt, dec = _ssd_common(_pick_head(xs_ref, pick_ref, h), dt_ref,
                                                                           cs_ref, csr_ref, b_ref, c_ref, nc)
        dy = _pick_head(dy_ref, pick_ref, h).reshape(nc, CHUNK, SSD_P)
        dyb = dy.astype(BF16)
        xdtb = xdt.astype(BF16)
        sprev = st_ref[0]
        sprevb = sprev.astype(BF16)
        cdec = jnp.exp(cs_last)
        ecs = jnp.exp(cs)
        dw = (ecs * dy).astype(BF16)
        wmat = _bdot(cm, sprevb, 2, 1)
        dcs = jnp.sum(dy * ecs * wmat, axis=2, keepdims=True)
        dcm = _bdot(dw, sprevb, 2, 2)
        dsl_ref[...] = _bdot(cm, dw, 1, 1)
        cd_ref[...] = cdec

        def step(k, ds):
            c = nc - 1 - k
            dsc_ref[c] = ds
            return ds * cd_ref[c] + dsl_ref[c]

        lax.fori_loop(0, nc, step, jnp.zeros((SSD_N, SSD_P), F32))
        dsc = dsc_ref[...]
        dscb = dsc.astype(BF16)
        d_last = jnp.sum(jnp.sum(dsc * sprev, axis=1, keepdims=True) * cdec, axis=2, keepdims=True)
        z = dec * xdt
        dbm = _bdot(z.astype(BF16), dscb, 2, 2)
        dz = _bdot(bm, dscb, 2, 1)
        dxdt = dec * dz
        t2 = jnp.sum(dz * z, axis=2, keepdims=True)
        dcs = dcs - t2
        d_last = d_last + jnp.sum(t2, axis=1, keepdims=True)
        m = g * lmat
        mb = m.astype(BF16)
        dm = _bdot(dyb, xdtb, 2, 2)
        dxdt = dxdt + _bdot(mb, dyb, 1, 1)
        dseg = dm * m
        dcs = dcs + jnp.sum(dseg, axis=2, keepdims=True)
        ones = jnp.ones((nc, CHUNK, SSD_P), F32)
        dcs = dcs - _bdot(dseg, ones, 1, 1, precision=HI)
        dg = (dm * lmat).astype(BF16)
        dcm = dcm + _bdot(dg, bm, 2, 1)
        dbm = dbm + _bdot(dg, cm, 1, 1)
        dcs = dcs + jnp.where(li[:, :, :SSD_P] == CHUNK - 1, d_last, 0.0)
        triu = jnp.where(li <= si, 1.0, 0.0).astype(F32)
        dadt = _bdot(triu, dcs, 2, 1, precision=HI)
        dk = dk_ref[0]
        _place_head(dxs_ref, (dxdt * dt + dk * dy).reshape(t, SSD_P), place_ref, h)
        ddt = jnp.sum(dxdt * x, axis=2, keepdims=True) + dadt * a_ref[0]
        mine = lax.broadcasted_iota(jnp.int32, (t, LANES), 1) == h

        @pl.when(h == 0)
        def _():
            ddt_ref[...] = jnp.zeros_like(ddt_ref)
            dadt_ref[...] = jnp.zeros_like(dadt_ref)

        ddt_ref[...] += jnp.where(mine, jnp.max(ddt, axis=2, keepdims=True).reshape(t, 1), 0.0)
        dadt_ref[...] += jnp.where(mine, jnp.max(dadt, axis=2, keepdims=True).reshape(t, 1), 0.0)
        dd_ref[0] = jnp.sum(jnp.sum(dy * x, axis=1, keepdims=True), axis=0)

        @pl.when(pl.program_id(1) == 0)
        def _():
            db_ref[...] = jnp.zeros_like(db_ref)
            dc_ref[...] = jnp.zeros_like(dc_ref)

        db_ref[...] += dbm.reshape(t, SSD_N)
        dc_ref[...] += dcm.reshape(t, SSD_N)

    head = pl.BlockSpec((1, t, SSD_P), lambda gi, hi: (gi * hpg + hi, 0, 0))
    pair = pl.BlockSpec((t, 2 * SSD_P), lambda gi, hi: (0, (gi * hpg + hi) // 2))
    grp = pl.BlockSpec((t, SSD_N), lambda gi, hi: (0, gi))
    lane = pl.BlockSpec((1, 1, SSD_P), lambda gi, hi: (gi * hpg + hi, 0, 0))
    rows = pl.BlockSpec((t, LANES), lambda gi, hi: (0, 0))
    nxb = D_SSM // SSD_N
    dxs, ddt, dadt, db, dc, dd = pl.pallas_call(
        body, name=name, grid=(SSD_GROUPS, hpg),
        in_specs=[pair, head, head, pl.BlockSpec((1, nc, 1, CHUNK), lambda gi, hi: (gi * hpg + hi, 0, 0, 0)),
                  pl.BlockSpec((t, SSD_N), lambda gi, hi: (0, nxb + gi)),
                  pl.BlockSpec((t, SSD_N), lambda gi, hi: (0, nxb + SSD_GROUPS + gi)), lane, lane,
                  pl.BlockSpec((1, nc, SSD_N, SSD_P), lambda gi, hi: (gi * hpg + hi, 0, 0, 0)), pair,
                  pl.BlockSpec((2, 2 * SSD_P, SSD_P), lambda gi, hi: (0, 0, 0)),
                  pl.BlockSpec((2, SSD_P, 2 * SSD_P), lambda gi, hi: (0, 0, 0))],
        out_specs=[pair, rows, rows, grp, grp, lane],
        out_shape=[jax.ShapeDtypeStruct((t, D_SSM), F32)] + [jax.ShapeDtypeStruct((t, LANES), F32)] * 2
        + [jax.ShapeDtypeStruct((t, SSD_GROUPS * SSD_N), F32)] * 2
        + [jax.ShapeDtypeStruct((SSD_HEADS, 1, SSD_P), F32)],
        scratch_shapes=[pltpu.VMEM((nc, SSD_N, SSD_P), F32), pltpu.VMEM((nc, SSD_N, SSD_P), F32),
                        pltpu.VMEM((nc, 1, SSD_P), F32)],
        compiler_params=_cp("arbitrary", "arbitrary"),
    )(xbc, dt_h, cs_h, cs_row, xbc, xbc, dskip_h, a_h, states, dy, pick, place)
    return jnp.concatenate([dxs, db, dc], axis=1), ddt, dadt, dd


def _ssd_gate_fwd(y, proj, w, *, tr=256, name):
    t = y.shape[0]
    gw = D_SSM // SSD_GROUPS

    def body(y_ref, z_ref, w_ref, o_ref):
        v = y_ref[...] * _silu(z_ref[...])
        for gi in range(SSD_GROUPS):
            vg = v[:, gi * gw:(gi + 1) * gw]
            r = lax.rsqrt(jnp.mean(vg * vg, axis=-1, keepdims=True) + NORM_EPS)
            o_ref[:, gi * gw:(gi + 1) * gw] = (vg * r * w_ref[:, gi * gw:(gi + 1) * gw]).astype(BF16)

    blk = pl.BlockSpec((tr, D_SSM), lambda i: (i, 0))
    return pl.pallas_call(
        body, name=name, grid=(t // tr,), in_specs=[blk, blk, pl.BlockSpec((1, D_SSM), lambda i: (0, 0))],
        out_specs=blk, out_shape=jax.ShapeDtypeStruct((t, D_SSM), BF16), compiler_params=_cp("parallel"),
    )(y, proj, w)


def _ssd_gate_bwd(y, proj, w, dcat, *, tr=256, name):
    t = y.shape[0]
    gw = D_SSM // SSD_GROUPS

    def body(y_ref, z_ref, w_ref, d_ref, dy_ref, dz_ref, dw_ref):
        yv, zv, dv = y_ref[...], z_ref[...], d_ref[...].astype(F32)
        sz = _silu(zv)
        v = yv * sz

        @pl.when(pl.program_id(0) == 0)
        def _():
            dw_ref[...] = jnp.zeros_like(dw_ref)

        for gi in range(SSD_GROUPS):
            sl = slice(gi * gw, (gi + 1) * gw)
            vg, dg = v[:, sl], dv[:, sl]
            r = lax.rsqrt(jnp.mean(vg * vg, axis=-1, keepdims=True) + NORM_EPS)
            vh = vg * r
            gg = dg * w_ref[:, sl]
            dvg = r * (gg - vh * jnp.mean(gg * vh, axis=-1, keepdims=True))
            dy_ref[:, sl] = dvg * sz[:, sl]
            dz_ref[:, sl] = (dvg * yv[:, sl] * _dsilu(zv[:, sl])).astype(BF16)
            dw_ref[:, sl] += jnp.sum(dg * vh, axis=0, keepdims=True)

    blk = pl.BlockSpec((tr, D_SSM), lambda i: (i, 0))
    row = pl.BlockSpec((1, D_SSM), lambda i: (0, 0))
    return pl.pallas_call(
        body, name=name, grid=(t // tr,), in_specs=[blk, blk, row, blk], out_specs=[blk, blk, row],
        out_shape=[jax.ShapeDtypeStruct((t, D_SSM), F32), jax.ShapeDtypeStruct((t, D_SSM), BF16),
                   jax.ShapeDtypeStruct((1, D_SSM), F32)],
        compiler_params=_cp("arbitrary"),
    )(y, proj, w, dcat)


def _pad_lanes(v):
    return jnp.pad(v, ((0, 0), (0, LANES - v.shape[1])))


def _per_head(v128, t):
    return jnp.broadcast_to(v128[:, :SSD_HEADS].T[:, :, None], (SSD_HEADS, t, SSD_P))


def _ssd_forward(proj, conv_w, conv_b, dt_bias, a_log, d_skip, ssd_norm_w):
    t = proj.shape[0]
    nc = t // CHUNK
    xbc = _conv_act_fwd(proj, conv_w, conv_b, kw=SSD_CONV, glu=False, tc=512, coff=OFF_XBC // 512,
                        ncols=SSD_CONV_DIM, out_dtype=F32, name="ssd_conv_fwd")
    bias128, alog128 = _pad_lanes(dt_bias), _pad_lanes(a_log)
    dt128, cs128, a128 = _ssd_prep(proj, bias128, alog128, name="ssd_prep")
    dt_h, cs_h = _per_head(dt128, t), _per_head(cs128, t)
    cs_row = cs128[:, :SSD_HEADS].T.reshape(SSD_HEADS, nc, 1, CHUNK)
    dskip_h = jnp.broadcast_to(d_skip[0][:, None, None], (SSD_HEADS, 1, SSD_P))
    a_h = jnp.broadcast_to(a128[0, :SSD_HEADS][:, None, None], (SSD_HEADS, 1, SSD_P))
    y, states = _ssd_fwd(xbc, dt_h, cs_h, cs_row, dskip_h, name="ssd_scan_fwd")
    y_ssd = _ssd_gate_fwd(y, proj, ssd_norm_w, name="ssd_gate_fwd")
    saved = (proj, conv_w, conv_b, ssd_norm_w, bias128, dt128, a128, dt_h, cs_h, cs_row, xbc, dskip_h, a_h, states, y)
    return y_ssd, saved


def _ssd_backward(saved, dcat):
    proj, conv_w, conv_b, ssd_norm_w, bias128, dt128, a128, dt_h, cs_h, cs_row, xbc, dskip_h, a_h, states, y = saved
    dy, dz, d_norm_w = _ssd_gate_bwd(y, proj, ssd_norm_w, dcat, name="ssd_gate_bwd")
    dxc, ddt128, dadt128, dd_h = _ssd_bwd(xbc, dt_h, cs_h, cs_row, dskip_h, a_h, states, dy, name="ssd_scan_bwd")
    dxbc, d_conv_w, d_conv_b = _conv_act_bwd(proj, conv_w, conv_b, dxc, kw=SSD_CONV, glu=False, tc=512,
                                             coff=OFF_XBC // 512, ncols=SSD_CONV_DIM, name="ssd_conv_bwd")
    d_raw, d_bias, d_alog, d_dskip = _ssd_prep_bwd(ddt128, dadt128, proj, bias128, dt128, a128,
                                                   dd_h.reshape(SSD_HEADS, SSD_P), name="ssd_prep_bwd")
    return (dz, dxbc, d_raw, d_norm_w, d_conv_w, d_conv_b, d_bias[:, :SSD_HEADS], d_alog[:, :SSD_HEADS],
            d_dskip.reshape(1, SSD_HEADS))


def _rope_tables(positions):
    inv_freq = ROPE_THETA ** (-jnp.arange(0, MLA_ROPE, 2, dtype=F32) / MLA_ROPE)
    ang = positions[0].astype(F32)[:, None] * inv_freq
    cos, sin = jnp.cos(ang), jnp.sin(ang)
    z = jnp.zeros_like(cos)
    return jnp.stack([jnp.concatenate([cos, cos, z, z], axis=1), jnp.concatenate([-sin, z, z, z], axis=1),
                      jnp.concatenate([z, sin, z, z], axis=1)])


def _mla_forward(proj, tabs, q_a_norm_w, wq_pad, kv_a_norm_w, wkv):
    qn = _rmsnorm_fwd(proj, q_a_norm_w, width=MLA_Q_RANK, cblk=OFF_QA // MLA_Q_RANK, name="q_a_norm")
    q = _matmul(qn, wq_pad, name="q_b_proj")
    kvn = _rmsnorm_fwd(proj, kv_a_norm_w, width=MLA_KV_RANK, cblk=OFF_CKV // MLA_KV_RANK, name="kv_a_norm")
    kv = _matmul(kvn, wkv, name="kv_b_proj")
    q3, k3, v3, vt4 = _mla_prep(q, kv, proj, tabs, name="mla_prep")
    o, lse = _attn_fwd(q3, k3, vt4, name="attn_fwd")
    return o, (proj, tabs, q_a_norm_w, wq_pad, kv_a_norm_w, wkv, qn, kvn, q3, k3, v3, o, lse)


def _mla_backward(saved, dcat):
    proj, tabs, q_a_norm_w, wq_pad, kv_a_norm_w, wkv, qn, kvn, q3, k3, v3, o, lse = saved
    dq3, dk3, dv3 = _attn_bwd(q3, k3, v3, o, dcat, lse, name="attn_bwd")
    dq, dkv, dkr = _mla_unprep(dq3, dk3, dv3, tabs, name="mla_unprep")
    d_wq = _matmul(qn, dq, ta=True, out_dtype=BF16, name="d_w_q_b")
    dqn = _matmul(dq, wq_pad, tb=True, name="d_qn")
    dq_a, d_qnw = _rmsnorm_bwd(proj, q_a_norm_w, dqn, width=MLA_Q_RANK, cblk=OFF_QA // MLA_Q_RANK, out_dtype=BF16,
                               name="q_a_norm_bwd")
    d_wkv = _matmul(kvn, dkv, ta=True, out_dtype=BF16, name="d_w_kv_b")
    dkvn = _matmul(dkv, wkv, tb=True, name="d_kvn")
    dckv, d_kvnw = _rmsnorm_bwd(proj, kv_a_norm_w, dkvn, width=MLA_KV_RANK, cblk=OFF_CKV // MLA_KV_RANK,
                                out_dtype=BF16, name="kv_a_norm_bwd")
    return dq_a, dckv, dkr, d_wq, d_wkv, d_qnw, d_kvnw


def _pad_w_q(w):
    r = w.shape[0]
    w3 = w.reshape(r, MLA_HEADS, MLA_NOPE + MLA_ROPE)
    return jnp.pad(w3, ((0, 0), (0, 0), (0, MLA_QK_PAD - MLA_NOPE - MLA_ROPE))).reshape(r, MLA_HEADS * MLA_QK_PAD)


def _unpad_w_q(w):
    r = w.shape[0]
    return w.reshape(r, MLA_HEADS, MLA_QK_PAD)[:, :, :MLA_NOPE + MLA_ROPE].reshape(r, MLA_HEADS * (MLA_NOPE + MLA_ROPE))


W_IN_SEGMENTS = ((0, D_SSM + SSD_CONV_DIM, 0), (D_SSM + SSD_CONV_DIM, D_SSM + SSD_CONV_DIM + SSD_HEADS, OFF_DT),
                 (D_SSM + SSD_CONV_DIM + SSD_HEADS, D_IN - MLA_ROPE, OFF_QA), (D_IN - MLA_ROPE, D_IN, OFF_KR))


def _pad_w_in_shards(g):
    n = g.shape[2]
    pieces, at = [], 0
    for lo, hi, start in sorted(W_IN_SEGMENTS, key=lambda seg: seg[2]):
        if start > at:
            pieces.append(jnp.zeros((g.shape[1], start - at), g.dtype))
        for j in range(N_DEV):
            a, b = max(lo, j * n), min(hi, (j + 1) * n)
            if a < b:
                pieces.append(g[j][:, a - j * n:b - j * n])
        at = start + hi - lo
    pieces.append(jnp.zeros((g.shape[1], D_IN_PAD - at), g.dtype))
    return jnp.concatenate(pieces, axis=1)


def _unpad_w_in_shards(w):
    n = D_IN // N_DEV
    shards = []
    for j in range(N_DEV):
        pieces = []
        for lo, hi, start in W_IN_SEGMENTS:
            a, b = max(lo, j * n), min(hi, (j + 1) * n)
            if a < b:
                pieces.append(w[:, start + a - lo:start + b - lo])
        shards.append(jnp.concatenate(pieces, axis=1) if len(pieces) > 1 else pieces[0])
    return jnp.stack(shards)


WEIGHTS = ['mix_norm_w', 'w_in', 'conv_w', 'conv_b', 'dt_bias', 'a_log', 'd_skip', 'ssd_norm_w', 'q_a_norm_w', 'w_q_b',
           'kv_a_norm_w', 'w_kv_b', 'w_out', 'ffn_norm_w', 'w_ffn_up', 'ffn_conv_w', 'ffn_conv_b', 'w_ffn_down',
           'ple_norm_w', 'w_ple_gate', 'b_ple_gate', 'w_ple_proj', 'ple_post_norm_w', 'final_norm_w']
BIG = ['w_in', 'w_q_b', 'w_kv_b', 'w_out', 'w_ffn_up', 'w_ffn_down', 'w_ple_gate', 'w_ple_proj']
COL_SHARDED = ('w_in', 'w_q_b', 'w_kv_b', 'w_ffn_up', 'w_ple_proj')
CONV = ['conv_w', 'ffn_conv_w']
REPL = [n for n in WEIGHTS if n not in BIG and n not in CONV]
FFN_INV = tuple(int(i) for i in np.argsort(FFN_PERM))


def _cat_cols(g):
    return jnp.concatenate([g[j] for j in range(N_DEV)], axis=1)


def _split_cols(w):
    n = w.shape[1] // N_DEV
    return jnp.stack([w[:, j * n:(j + 1) * n] for j in range(N_DEV)])


def _interleave(v):
    r = v.shape[0]
    return v.reshape(r, N_DEV, FFN_TC)[:, jnp.array(FFN_PERM)].reshape(r, N_DEV * FFN_TC)


def _deinterleave(v):
    r = v.shape[0]
    return v.reshape(r, N_DEV, FFN_TC)[:, jnp.array(FFN_INV)].reshape(r, N_DEV * FFN_TC)


def _assemble_weights(g):
    layout = {
        'w_in': _pad_w_in_shards,
        'w_q_b': lambda v: _pad_w_q(_cat_cols(v)),
        'w_kv_b': _cat_cols,
        'w_out': lambda v: v.reshape(D_MODEL, D_MODEL),
        'w_ffn_up': lambda v: v,
        'w_ffn_down': lambda v: v.reshape(D_FF, D_MODEL),
        'w_ple_gate': lambda v: v.reshape(D_MODEL, D_MODEL),
        'w_ple_proj': _cat_cols,
        'conv_w': _cat_cols,
        'ffn_conv_w': lambda v: _interleave(_cat_cols(v)),
    }
    return {n: layout[n](v) for n, v in g.items()}


WEIGHT_GROUPS = {'a': ['w_in', 'w_q_b', 'w_kv_b', 'conv_w'], 'b': ['w_out'],
                 'c': ['w_ffn_up', 'ffn_conv_w', 'w_ffn_down', 'w_ple_gate', 'w_ple_proj']}
GRAD_GROUPS = {'p': ['w_ple_proj', 'w_ple_gate', 'w_ffn_down'], 'r': ['w_ffn_up'], 's': ['w_out'],
               't': ['w_q_b', 'w_kv_b', 'w_in']}


def _ffn_perm(j):
    return (j % 2) * (N_DEV // 2) + j // 2


def _local_step(x, p, tabs, get_w, s, target, emit, relay, settle):
    t = x.shape[0]
    s = dict(s)
    half = D_MODEL // 2
    up_cols = 2 * D_FF
    ffn_conv_b = _interleave(s['ffn_conv_b'])
    w = dict(get_w('a', None))
    h = _rmsnorm_fwd(x, s['mix_norm_w'], width=D_MODEL, name="mix_norm")
    proj = _matmul(h, w['w_in'], name="in_proj")
    y_ssd, ssd_saved = _ssd_forward(proj, w['conv_w'], s['conv_b'], s['dt_bias'], s['a_log'], s['d_skip'],
                                    s['ssd_norm_w'])
    o, mla_saved = _mla_forward(proj, tabs, s['q_a_norm_w'], w['w_q_b'], s['kv_a_norm_w'], w['w_kv_b'])
    tk_o, tn_o = _tile(half, MM_TK), _tile(D_MODEL, MM_TILE)
    w.update(get_w('b', o))
    x1 = _matmul(y_ssd, w['w_out'], add=x, mnk=(t, D_MODEL, half), name="out_proj_ssd")
    x1 = _matmul(o, w['w_out'], add=x1, mnk=(t, D_MODEL, half), name="out_proj_mla",
                 b_spec=pl.BlockSpec((tk_o, tn_o), lambda i, j, kk: (kk + half // tk_o, j)))
    hf = _rmsnorm_fwd(x1, s['ffn_norm_w'], width=D_MODEL, name="ffn_norm")
    w.update(get_w('c', hf))
    tk_u = _tile(D_MODEL, MM_TK)
    u = _matmul(hf, w['w_ffn_up'], mnk=(t, up_cols, D_MODEL), tn=FFN_TC, name="ffn_up",
                b_spec=pl.BlockSpec((1, tk_u, FFN_TC), lambda i, j, kk: (_ffn_perm(j), kk, 0)))
    act = _conv_act_fwd(u, w['ffn_conv_w'], ffn_conv_b, kw=FFN_CONV, glu=True, tc=2 * FFN_TC, coff=0, ncols=up_cols,
                        out_dtype=BF16, name="ffn_act")
    x2 = _matmul(act, w['w_ffn_down'], add=x1, name="ffn_down")
    hp = _rmsnorm_fwd(x2, s['ple_norm_w'], width=D_MODEL, name="ple_norm")
    gl = _matmul(hp, w['w_ple_gate'], bias=s['b_ple_gate'], name="ple_gate")
    pe = _matmul(p, w['w_ple_proj'], name="ple_proj")
    x3 = _ple_fwd(x2, gl, pe, s['ple_post_norm_w'], name="ple_mix")
    loss, dx3, d_final = _loss_head(x3, s['final_norm_w'], target, name="loss_head")
    dgl, d_bgate, dpe, d_post = _ple_bwd(dx3, gl, pe, s['ple_post_norm_w'], name="ple_mix_bwd")
    d_wproj = _matmul(p, dpe, ta=True, out_dtype=BF16, name="d_w_ple_proj")
    d_wgate = _matmul(hp, dgl, ta=True, out_dtype=BF16, name="d_w_ple_gate")
    dhp = _matmul(dgl, w['w_ple_gate'], tb=True, name="d_ple_normed")
    dx2, d_plenorm = _rmsnorm_bwd(x2, s['ple_norm_w'], dhp, dx3, width=D_MODEL, name="ple_norm_bwd")
    dact = _matmul(dx2, w['w_ffn_down'], tb=True, name="d_ffn_act")
    d_wdown = _matmul(act, dx2, ta=True, out_dtype=BF16, name="d_w_ffn_down")
    zz = emit('p', {'w_ple_proj': _split_cols(d_wproj), 'w_ple_gate': d_wgate.reshape(N_DEV, D_MODEL // N_DEV, D_MODEL),
                    'w_ffn_down': d_wdown.reshape(N_DEV, D_FF // N_DEV, D_MODEL)})
    du, d_fconv_w, d_fconv_b = _conv_act_bwd(u, w['ffn_conv_w'], ffn_conv_b + zz, dact, kw=FFN_CONV, glu=True,
                                             tc=2 * FFN_TC, coff=0, ncols=up_cols, name="ffn_act_bwd")
    zz = zz + relay('p', du)
    tm_u = _tile(D_MODEL, MM_TILE)
    d_wup = _matmul(hf, du, ta=True, out_dtype=BF16, mnk=(D_MODEL, up_cols, t), tn=FFN_TC, name="d_w_ffn_up",
                    o_spec=pl.BlockSpec((1, tm_u, FFN_TC), lambda i, j, kk: (_ffn_perm(j), i, 0)),
                    o_shape=(N_DEV, D_MODEL, FFN_TC))
    zz = zz + emit('r', {'w_ffn_up': d_wup})
    zero_row = jnp.zeros((1, D_MODEL), F32)
    dhf = _matmul(du, w['w_ffn_up'], tb=True, mnk=(t, D_MODEL, up_cols), tk=FFN_TC, name="d_ffn_normed",
                  bias=zero_row + zz,
                  b_spec=pl.BlockSpec((1, tn_o, FFN_TC), lambda i, j, kk: (_ffn_perm(kk), j, 0)))
    zz = zz + relay('r', dhf) + settle('p')
    dx1, d_ffnnorm = _rmsnorm_bwd(x1, s['ffn_norm_w'] + zz, dhf, dx2, width=D_MODEL, name="ffn_norm_bwd")
    dcat = _matmul(dx1, w['w_out'], tb=True, name="d_mixed")
    d_wout = jnp.concatenate([_matmul(y_ssd, dx1, ta=True, out_dtype=BF16, name="d_w_out_ssd"),
                              _matmul(o, dx1, ta=True, out_dtype=BF16, name="d_w_out_mla")], axis=0)
    zz = zz + emit('s', {'w_out': d_wout.reshape(N_DEV, D_MODEL // N_DEV, D_MODEL)})
    ssd_saved = ssd_saved[:3] + (ssd_saved[3] + zz,) + ssd_saved[4:]
    dz, dxbc, d_raw, d_ssdnorm, d_conv_w, d_conv_b, d_dtb, d_alog, d_dskip = _ssd_backward(ssd_saved, dcat)
    zz = zz + relay('s', dz)
    mla_saved = mla_saved[:-1] + (mla_saved[-1] + zz,)
    dq_a, dckv, dkr, d_wq, d_wkv, d_qnorm, d_kvnorm = _mla_backward(mla_saved, dcat)
    d_raw = (d_raw + settle('r')).astype(BF16)
    dproj = jnp.concatenate([dz, dxbc, dq_a, dckv, dkr, d_raw], axis=1)
    d_win = _matmul(h, dproj, ta=True, out_dtype=BF16, name="d_w_in")
    zz = emit('t', {'w_in': _unpad_w_in_shards(d_win), 'w_q_b': _split_cols(_unpad_w_q(d_wq)),
                    'w_kv_b': _split_cols(d_wkv)}) + settle('s')
    dh = _matmul(dproj, w['w_in'], tb=True, bias=zero_row + zz, name="d_in_normed")
    zz = relay('t', dh)
    dx, d_mixnorm = _rmsnorm_bwd(x, s['mix_norm_w'] + zz, dh, dx1, width=D_MODEL, name="mix_norm_bwd")
    conv = {'conv_w': d_conv_w, 'ffn_conv_w': _deinterleave(d_fconv_w)}
    vec = {
        'mix_norm_w': d_mixnorm, 'conv_b': d_conv_b, 'dt_bias': d_dtb, 'a_log': d_alog, 'd_skip': d_dskip,
        'ssd_norm_w': d_ssdnorm, 'q_a_norm_w': d_qnorm, 'kv_a_norm_w': d_kvnorm, 'ffn_norm_w': d_ffnnorm,
        'ffn_conv_b': _deinterleave(d_fconv_b), 'ple_norm_w': d_plenorm, 'b_ple_gate': d_bgate,
        'ple_post_norm_w': d_post, 'final_norm_w': d_final,
    }
    return loss, dx, conv, vec


MESH = pl.DeviceIdType.MESH
FLIPS = ((0, 0, 1), (1, 0, 0), (0, 1, 0), (1, 1, 0), (1, 0, 1), (0, 1, 1), (1, 1, 1))


def _exchange(items, *, gather, name):
    n = len(items)

    def body(*refs):
        ins, outs = refs[:n], refs[n:2 * n]
        send_sems, recv_sems, local_sems = refs[2 * n:]
        x, y, c = lax.axis_index("x"), lax.axis_index("y"), lax.axis_index("c")
        me = 4 * x + 2 * y + c
        peers = [(jnp.where(fx, 1 - x, x), jnp.where(fy, 1 - y, y), jnp.where(fc, 1 - c, c)) for fx, fy, fc in FLIPS]
        slot = [4 * px + 2 * py + pc for px, py, pc in peers]
        local, sends = [], []
        for wi in range(n):
            cp = pltpu.make_async_copy(ins[wi] if gather else ins[wi].at[me], outs[wi].at[me], local_sems.at[wi])
            cp.start()
            local.append(cp)
            for k, peer in enumerate(peers):
                cp = pltpu.make_async_remote_copy(
                    src_ref=ins[wi] if gather else ins[wi].at[slot[k]], dst_ref=outs[wi].at[me],
                    send_sem=send_sems.at[k, wi], recv_sem=recv_sems.at[k, wi], device_id=peer, device_id_type=MESH)
                cp.start()
                sends.append(cp)
        for wi in range(n):
            for k, peer in enumerate(peers):
                pltpu.make_async_remote_copy(
                    src_ref=outs[wi].at[slot[k]], dst_ref=outs[wi].at[slot[k]], send_sem=send_sems.at[k, wi],
                    recv_sem=recv_sems.at[k, wi], device_id=peer, device_id_type=MESH).wait_recv()
        for cp in sends:
            cp.wait_send()
        for cp in local:
            cp.wait()

    hbm = pl.BlockSpec(memory_space=pltpu.HBM)
    out_shape = [jax.ShapeDtypeStruct(((N_DEV,) + v.shape) if gather else v.shape, v.dtype) for v in items]
    return pl.pallas_call(
        body, name=name, in_specs=[hbm] * n, out_specs=[hbm] * n, out_shape=out_shape,
        scratch_shapes=[pltpu.SemaphoreType.DMA((len(FLIPS), n)), pltpu.SemaphoreType.DMA((len(FLIPS), n)),
                        pltpu.SemaphoreType.DMA((n,))],
    )(*items)


HBM_SPEC = pl.BlockSpec(memory_space=pltpu.HBM)
SEM_SPEC = pl.BlockSpec(memory_space=pltpu.SEMAPHORE)
EFFECT = pltpu.SideEffectType.DATAFLOW_SIDE_EFFECTING


def _split_start(bufs, ncopies, plan, *, name):
    nb = len(bufs)

    def body(*refs):
        send_sems, recv_sems, token = refs[nb], refs[nb + 1], refs[2 * nb + 2]
        for i, (src, dst, peer, _) in enumerate(plan(refs[:nb])):
            pltpu.make_async_remote_copy(src_ref=src, dst_ref=dst, send_sem=send_sems.at[i], recv_sem=recv_sems.at[i],
                                         device_id=peer, device_id_type=MESH).start()
        token[...] = jnp.zeros_like(token)

    res = pl.pallas_call(
        body, name=name, in_specs=[HBM_SPEC] * nb,
        out_specs=[SEM_SPEC, SEM_SPEC] + [HBM_SPEC] * nb + [pl.BlockSpec(memory_space=pltpu.VMEM)],
        out_shape=[pltpu.SemaphoreType.DMA((ncopies,)), pltpu.SemaphoreType.DMA((ncopies,))]
        + [pltpu.HBM(v.shape, v.dtype) for v in bufs] + [jax.ShapeDtypeStruct((HALO, LANES), F32)],
        input_output_aliases={i: 2 + i for i in range(nb)},
        compiler_params=pltpu.CompilerParams(has_side_effects=EFFECT),
    )(*[pltpu.with_memory_space_constraint(v, pltpu.HBM) for v in bufs])
    return (res[0], res[1], list(res[2:2 + nb])), res[2 + nb]


def _split_wait(started, after, plan, local_plan, *, name):
    send_sems, recv_sems, bufs = started
    nb = len(bufs)
    nlocal = len(local_plan(bufs))

    def body(*refs):
        send_sems, recv_sems = refs[nb], refs[nb + 1]
        local_sems = refs[2 * nb + 3]
        local = []
        for j, (src, dst) in enumerate(local_plan(refs[:nb])):
            cp = pltpu.make_async_copy(src, dst, local_sems.at[j])
            cp.start()
            local.append(cp)
        for i, (src, _, peer, incoming) in enumerate(plan(refs[:nb])):
            cp = pltpu.make_async_remote_copy(src_ref=src, dst_ref=incoming, send_sem=send_sems.at[i],
                                              recv_sem=recv_sems.at[i], device_id=peer, device_id_type=MESH)
            cp.wait_send()
            cp.wait_recv()
        for cp in local:
            cp.wait()

    res = pl.pallas_call(
        body, name=name, in_specs=[HBM_SPEC] * nb + [SEM_SPEC, SEM_SPEC, pl.BlockSpec(memory_space=pl.ANY)],
        out_specs=[HBM_SPEC] * nb, out_shape=[pltpu.HBM(v.shape, v.dtype) for v in bufs],
        input_output_aliases={i: i for i in range(nb)},
        scratch_shapes=[pltpu.SemaphoreType.DMA((max(nlocal, 1),))],
        compiler_params=pltpu.CompilerParams(has_side_effects=EFFECT),
    )(*bufs, send_sems, recv_sems, after)
    return list(res)


def _place():
    x, y, c = lax.axis_index("x"), lax.axis_index("y"), lax.axis_index("c")
    others = [((1 - x, y, c), 2 * (1 - x) + y), ((x, 1 - y, c), 2 * x + 1 - y), ((1 - x, 1 - y, c), 2 * (1 - x) + 1 - y)]
    return 4 * x + 2 * y + c, 2 * x + y, c, (x, y, 1 - c), others


def _gather1_plan(n):
    def plan(refs):
        me, _, _, sibling, others = _place()
        out = []
        for wi in range(n):
            item, land = refs[wi], refs[n + wi]
            out.append((item, land.at[me], sibling, land.at[me + 1 - 2 * lax.axis_index("c")]))
            for peer, chip in others:
                out.append((item, land.at[me], peer, land.at[2 * chip + lax.axis_index("c")]))
        return out

    return plan


def _gather1_local(n):
    def plan(refs):
        me = _place()[0]
        return [(refs[wi], refs[n + wi].at[me]) for wi in range(n)]

    return plan


def _gather2_plan(n):
    def plan(refs):
        _, _, c, sibling, others = _place()
        out = []
        for wi in range(n):
            land = refs[wi]
            for _, chip in others:
                out.append((land.at[2 * chip + c], land.at[2 * chip + c], sibling, land.at[2 * chip + 1 - c]))
        return out

    return plan


def _gather_start(items, *, name):
    lands = [lax.empty((N_DEV,) + v.shape, v.dtype) for v in items]
    return _split_start(items + lands, 4 * len(items), _gather1_plan(len(items)), name=name)


def _gather_forward(started, after, *, name):
    n = len(started[2]) // 2
    bufs = _split_wait(started, after, _gather1_plan(n), _gather1_local(n), name=name + "_wait")
    return _split_start(bufs[n:], 3 * n, _gather2_plan(n), name=name + "_start")


def _gather_finish(started, after, *, name):
    n = len(started[2])
    return _split_wait(started, after, _gather2_plan(n), lambda refs: [], name=name)


def _handshake(peers):
    barrier = pltpu.get_barrier_semaphore()
    for peer in peers:
        pl.semaphore_signal(barrier, inc=1, device_id=peer, device_id_type=MESH)
    pl.semaphore_wait(barrier, len(peers))


def _remote(src, dst, send_sem, recv_sem, peer):
    return pltpu.make_async_remote_copy(src_ref=src, dst_ref=dst, send_sem=send_sem, recv_sem=recv_sem, device_id=peer,
                                        device_id_type=MESH)


def _sequencer_gather(items, *, collective_id, name):
    n = len(items)
    srcs = [jax.new_ref(v, memory_space=pltpu.MemorySpace.HBM) for v in items]
    lands = [jax.empty_ref(jax.ShapeDtypeStruct((N_DEV,) + v.shape, v.dtype), memory_space=pltpu.MemorySpace.HBM)
             for v in items]
    dma = pltpu.SemaphoreType.DMA

    @pl.kernel(mesh=plsc.ScalarSubcoreMesh(axis_name="sequencer", num_cores=1), name=name,
               scratch_types=(dma((4 * n,)), dma((4 * n,)), dma((3 * n,)), dma((3 * n,)), dma((n,))),
               compiler_params=pltpu.CompilerParams(collective_id=collective_id))
    def launch(send1, recv1, send2, recv2, local_sems):
        _, _, _, sibling, others = _place()
        _handshake([sibling] + [peer for peer, _ in others])
        hop1 = _gather1_plan(n)(srcs + lands)
        hop2 = _gather2_plan(n)(lands)
        local = [pltpu.make_async_copy(src, dst, local_sems.at[j])
                 for j, (src, dst) in enumerate(_gather1_local(n)(srcs + lands))]
        for cp in local:
            cp.start()
        for i, (src, dst, peer, _) in enumerate(hop1):
            _remote(src, dst, send1.at[i], recv1.at[i], peer).start()
        for wi in range(n):
            for j in range(3):
                i1, i2 = 4 * wi + 1 + j, 3 * wi + j
                src, _, peer, incoming = hop1[i1]
                _remote(src, incoming, send1.at[i1], recv1.at[i1], peer).wait_recv()
                src, dst, peer, _ = hop2[i2]
                _remote(src, dst, send2.at[i2], recv2.at[i2], peer).start()
        for wi in range(n):
            src, _, peer, incoming = hop1[4 * wi]
            _remote(src, incoming, send1.at[4 * wi], recv1.at[4 * wi], peer).wait_recv()
        for i, (src, _, peer, incoming) in enumerate(hop2):
            cp = _remote(src, incoming, send2.at[i], recv2.at[i], peer)
            cp.wait_send()
            cp.wait_recv()
        for i, (src, dst, peer, _) in enumerate(hop1):
            _remote(src, dst, send1.at[i], recv1.at[i], peer).wait_send()
        for cp in local:
            cp.wait()

    launch()
    return [land[...] for land in lands]


def _sequencer_exchange(sources, land_shapes, ncopies, plan, local_plan, peers, *, collective_id, name):
    srcs = [jax.new_ref(v, memory_space=pltpu.MemorySpace.HBM) for v in sources]
    lands = [jax.empty_ref(s, memory_space=pltpu.MemorySpace.HBM) for s in land_shapes]
    nlocal = len(local_plan(srcs + lands))
    dma = pltpu.SemaphoreType.DMA

    @pl.kernel(mesh=plsc.ScalarSubcoreMesh(axis_name="sequencer", num_cores=1), name=name,
               scratch_types=(dma((ncopies,)), dma((ncopies,)), dma((max(nlocal, 1),))),
               compiler_params=pltpu.CompilerParams(collective_id=collective_id))
    def launch(send_sems, recv_sems, local_sems):
        _handshake(peers(_place()))
        copies = plan(srcs + lands)
        local = [pltpu.make_async_copy(src, dst, local_sems.at[j])
                 for j, (src, dst) in enumerate(local_plan(srcs + lands))]
        for cp in local:
            cp.start()
        for i, (src, dst, peer, _) in enumerate(copies):
            _remote(src, dst, send_sems.at[i], recv_sems.at[i], peer).start()
        for i, (src, _, peer, incoming) in enumerate(copies):
            cp = _remote(src, incoming, send_sems.at[i], recv_sems.at[i], peer)
            cp.wait_send()
            cp.wait_recv()
        for cp in local:
            cp.wait()

    launch()
    return [land[...] for land in lands]


def _sequencer_scatter_hop2(sums, *, collective_id, name):
    n = len(sums)
    shapes = [jax.ShapeDtypeStruct(v.shape, v.dtype) for v in sums]
    return _sequencer_exchange(sums, shapes, 3 * n, _scatter2_plan(n), _scatter2_local(n),
                               lambda place: [peer for peer, _ in place[4]], collective_id=collective_id, name=name)


N_CHIP = N_DEV // 2


def _scatter1_plan(n):
    def plan(refs):
        _, _, c, sibling, _ = _place()
        out = []
        for wi in range(n):
            parts, half = refs[wi], refs[n + wi]
            for chip in range(N_CHIP):
                out.append((parts.at[2 * chip + 1 - c], half.at[chip], sibling, half.at[chip]))
        return out

    return plan


def _scatter2_plan(n):
    def plan(refs):
        _, my_chip, _, _, others = _place()
        out = []
        for wi in range(n):
            sums, recv = refs[wi], refs[n + wi]
            for peer, chip in others:
                out.append((sums.at[chip], recv.at[my_chip], peer, recv.at[chip]))
        return out

    return plan


def _scatter2_local(n):
    def plan(refs):
        my_chip = _place()[1]
        return [(refs[wi].at[my_chip], refs[n + wi].at[my_chip]) for wi in range(n)]

    return plan


def _pair_add(parts, half, core, *, name):
    _, r, c = parts.shape
    tr = max(d for d in range(HALO, 257, HALO) if r % d == 0) if r > 256 else r
    parts4 = parts.reshape(N_CHIP, 2, r, c)

    def body(core_ref, p_ref, h_ref, o_ref):
        o_ref[...] = (p_ref[:, 0].astype(F32) + h_ref[...].astype(F32)).astype(o_ref.dtype)

    return pl.pallas_call(
        body, name=name,
        grid_spec=pltpu.PrefetchScalarGridSpec(
            num_scalar_prefetch=1, grid=(r // tr,),
            in_specs=[pl.BlockSpec((N_CHIP, 1, tr, c), lambda i, core_ref: (0, core_ref[0], i, 0)),
                      pl.BlockSpec((N_CHIP, tr, c), lambda i, core_ref: (0, i, 0))],
            out_specs=pl.BlockSpec((N_CHIP, tr, c), lambda i, core_ref: (0, i, 0))),
        out_shape=jax.ShapeDtypeStruct((N_CHIP, r, c), parts.dtype), compiler_params=_cp("parallel"),
    )(core, parts4, half)


def _scatter_start(parts, *, name):
    halves = [lax.empty((N_CHIP,) + v.shape[1:], v.dtype) for v in parts]
    return _split_start(parts + halves, N_CHIP * len(parts), _scatter1_plan(len(parts)), name=name)


def _adamw(parts, w, m, v, *, name):
    r, c = w.shape
    nparts = parts.shape[0]
    tr = max(d for d in range(HALO, 129, HALO) if r % d == 0) if r > 128 else r

    def body(p_ref, w_ref, m_ref, v_ref, g_ref, d_ref, mo_ref, vo_ref):
        g = p_ref[0].astype(F32)
        for k in range(1, nparts):
            g = g + p_ref[k].astype(F32)
        mn = ADAM_B1 * m_ref[...] + (1.0 - ADAM_B1) * g
        vn = ADAM_B2 * v_ref[...] + (1.0 - ADAM_B2) * (g * g)
        m_hat = mn / (1.0 - ADAM_B1 ** ADAM_STEP)
        v_hat = vn / (1.0 - ADAM_B2 ** ADAM_STEP)
        g_ref[...] = g
        d_ref[...] = -ADAM_LR * (m_hat / (jnp.sqrt(v_hat) + ADAM_EPS) + ADAM_WD * w_ref[...])
        mo_ref[...] = mn
        vo_ref[...] = vn

    blk = pl.BlockSpec((tr, c), lambda i: (i, 0))
    return pl.pallas_call(
        body, name=name, grid=(r // tr,), in_specs=[pl.BlockSpec((nparts, tr, c), lambda i: (0, i, 0)), blk, blk, blk],
        out_specs=[blk] * 4, out_shape=[jax.ShapeDtypeStruct((r, c), F32)] * 4, compiler_params=_cp("parallel"),
    )(parts, w, m, v)


def _pack_rows(vs, rows):
    lead = vs[0].shape[:-1] if vs[0].ndim > 1 else ()
    flat = jnp.concatenate(vs, axis=-1)
    pad = rows * LANES - flat.shape[-1]
    flat = jnp.pad(flat, [(0, 0)] * len(lead) + [(0, pad)])
    return flat.reshape(lead + (rows, LANES))


def kernel(x, p, positions, mix_norm_w, w_in, conv_w, conv_b, dt_bias, a_log, d_skip, ssd_norm_w, q_a_norm_w, w_q_b, kv_a_norm_w, w_kv_b, w_out, ffn_norm_w, w_ffn_up, ffn_conv_w, ffn_conv_b, w_ffn_down, ple_norm_w, w_ple_gate, b_ple_gate, w_ple_proj, ple_post_norm_w, final_norm_w, loss_target, m_mix_norm_w, m_w_in, m_conv_w, m_conv_b, m_dt_bias, m_a_log, m_d_skip, m_ssd_norm_w, m_q_a_norm_w, m_w_q_b, m_kv_a_norm_w, m_w_kv_b, m_w_out, m_ffn_norm_w, m_w_ffn_up, m_ffn_conv_w, m_ffn_conv_b, m_w_ffn_down, m_ple_norm_w, m_w_ple_gate, m_b_ple_gate, m_w_ple_proj, m_ple_post_norm_w, m_final_norm_w, v_mix_norm_w, v_w_in, v_conv_w, v_conv_b, v_dt_bias, v_a_log, v_d_skip, v_ssd_norm_w, v_q_a_norm_w, v_w_q_b, v_kv_a_norm_w, v_w_kv_b, v_w_out, v_ffn_norm_w, v_w_ffn_up, v_ffn_conv_w, v_ffn_conv_b, v_w_ffn_down, v_ple_norm_w, v_w_ple_gate, v_b_ple_gate, v_w_ple_proj, v_ple_post_norm_w, v_final_norm_w):
    given = dict(locals())
    shapes = {n: given[n].shape for n in WEIGHTS}
    w2 = {n: given[n].reshape(given[n].shape[-2:] if n in BIG or n in CONV else (1, -1)) for n in WEIGHTS}
    m2 = {n: given['m_' + n].reshape(w2[n].shape) for n in WEIGHTS}
    v2 = {n: given['v_' + n].reshape(w2[n].shape) for n in WEIGHTS}
    me = 4 * lax.axis_index("x") + 2 * lax.axis_index("y") + lax.axis_index("c")

    core = lax.axis_index("c").astype(jnp.int32).reshape(1)

    def shards(grp, zero):
        return [(w2[n] + zero).astype(BF16) if n in BIG else w2[n] + zero for n in WEIGHT_GROUPS[grp]]

    first, token = _gather_start(shards('a', 0.0), name="gather_a_hop1")
    first, token = _gather_forward(first, token, name="gather_a_hop2")
    zero = token[0, 0]
    later = _sequencer_gather(shards('b', zero) + shards('c', zero), collective_id=1, name="gather_later")
    later = dict(zip(WEIGHT_GROUPS['b'] + WEIGHT_GROUPS['c'], later))

    def get_w(grp, after):
        if grp == 'a':
            lands = dict(zip(WEIGHT_GROUPS[grp], _gather_finish(first, token, name="gather_a_done")))
        else:
            lands = {n: later[n] for n in WEIGHT_GROUPS[grp]}
        return _assemble_weights(lands)

    scatters = {}

    hop_ids = {grp: 2 + 2 * i for i, grp in enumerate(GRAD_GROUPS)}

    def zero_of(arrays):
        return sum(v[(0,) * v.ndim].astype(F32) * 0.0 for v in arrays)

    def emit(grp, grads):
        scatters[grp], tok = _scatter_start([grads[n] for n in GRAD_GROUPS[grp]], name="scatter_" + grp + "_hop1")
        return tok[0, 0]

    def relay(grp, after):
        n = len(GRAD_GROUPS[grp])
        bufs = _split_wait(scatters[grp], after, _scatter1_plan(n), lambda refs: [], name="scatter_" + grp + "_hop1_wait")
        sums = [_pair_add(bufs[i], bufs[n + i], core, name="scatter_%s_add%d" % (grp, i)) for i in range(n)]
        scatters[grp] = _sequencer_scatter_hop2(sums, collective_id=hop_ids[grp] + 1, name="scatter_" + grp + "_hop2")
        return zero_of(sums)

    out_g, out_d, out_m, out_v = {}, {}, {}, {}

    def settle(grp):
        return zero_of(scatters[grp])

    def update(grp, behind=None):
        for n, parts in zip(GRAD_GROUPS[grp], scatters[grp]):
            wn = w2[n] if behind is None else w2[n] + behind
            out_g[n], out_d[n], out_m[n], out_v[n] = _adamw(parts, wn, m2[n], v2[n], name="adamw_" + n)

    vecs = {n: w2[n] for n in REPL}
    vecs['mix_norm_w'] = vecs['mix_norm_w'] + zero
    loss, dx, g_conv, g_vec = _local_step(x[0], p[0, 0], _rope_tables(positions), get_w, vecs, loss_target[0], emit,
                                          relay, settle)
    n_small = sum(g_vec[n].shape[1] for n in REPL) + sum(g_conv[n].size for n in CONV) + 1
    rows_small = -(-n_small // (LANES * HALO)) * HALO
    small = _pack_rows([g_vec[n] for n in REPL] + [g_conv[n].reshape(1, -1) for n in CONV] + [loss], rows_small)

    for grp in list(GRAD_GROUPS)[:-1]:
        update(grp)
    all_small = _exchange([small], gather=True, name="gather_small_grads")[0].reshape(N_DEV, rows_small * LANES)
    update(list(GRAD_GROUPS)[-1], zero_of([all_small]))
    pieces, off = [], 0
    for n in REPL:
        k = g_vec[n].shape[1]
        pieces.append(all_small[:, off:off + k])
        off += k
    for n in CONV:
        kw, cols = g_conv[n].shape
        full = all_small[:, off:off + kw * cols].reshape(N_DEV, kw, cols)
        mine = lax.dynamic_slice_in_dim(full, me * (cols // N_DEV), cols // N_DEV, axis=2)
        pieces.append(mine.reshape(N_DEV, kw * (cols // N_DEV)))
        off += kw * cols
    pieces.append(all_small[:, off:off + 1])
    small_names = REPL + CONV
    n_mine = sum(q.shape[1] for q in pieces)
    rows_mine = -(-n_mine // (LANES * HALO)) * HALO
    zero = jnp.zeros((1, 1), F32)
    packed = [_pack_rows([src[n].reshape(1, -1) for n in small_names] + [zero], rows_mine).reshape(rows_mine, LANES)
              for src in (w2, m2, v2)]
    sg, sd, sm, sv = _adamw(_pack_rows(pieces, rows_mine), *packed, name="adamw_small")
    off = 0
    for n in small_names:
        k = w2[n].size
        for dst, src in ((out_g, sg), (out_d, sd), (out_m, sm), (out_v, sv)):
            dst[n] = src.reshape(-1)[off:off + k].reshape(w2[n].shape)
        off += k
    total_loss = sg.reshape(-1)[off]

    outs = [total_loss, dx[None]]
    for res in (out_g, out_d, out_m, out_v):
        outs += [res[n].reshape(shapes[n]) for n in WEIGHTS]
    return tuple(outs)
```

```python
import math

import numpy as np
import jax
import jax.numpy as jnp
from jax import lax
from jax.experimental import pallas as pl
from jax.experimental.pallas import tpu as pltpu
from jax.experimental.pallas import tpu_sc as plsc

F32 = jnp.float32
BF16 = jnp.bfloat16
HI = lax.Precision.HIGHEST

D_MODEL = 2048
CHUNK = 64
D_SSM = 1024
SSD_P = 64
SSD_HEADS = 16
SSD_GROUPS = 2
SSD_N = 128
SSD_CONV = 4
SSD_CONV_DIM = D_SSM + 2 * SSD_GROUPS * SSD_N
MLA_HEADS = 8
MLA_NOPE = 128
MLA_ROPE = 64
MLA_V = 128
MLA_Q_RANK = 512
MLA_KV_RANK = 256
MLA_QK_PAD = 256
ROPE_THETA = 10000.0
D_FF = 5632
FFN_CONV = 3
PLE_DIM = 256
NORM_EPS = 1e-6
ADAM_LR, ADAM_B1, ADAM_B2, ADAM_EPS, ADAM_WD, ADAM_STEP = 0.001, 0.9, 0.999, 1e-08, 0.01, 10
N_DEV = 8

OFF_Z, OFF_XBC, OFF_QA, OFF_CKV, OFF_KR, OFF_DT, D_IN_PAD = 0, 1024, 2560, 3072, 3328, 3456, 3584
D_IN = 3408
LANES = 128
HALO = 8
VMEM_LIMIT = 56 * 1024 * 1024
FFN_TC = D_FF * 2 // N_DEV
FFN_PERM = (0, 4, 1, 5, 2, 6, 3, 7)
NEG = -1e30


def _cp(*sem):
    return pltpu.CompilerParams(dimension_semantics=tuple(sem), vmem_limit_bytes=VMEM_LIMIT)


def _tile(n, want):
    if n <= want:
        return n
    best = max(d for d in range(LANES, want + 1, LANES) if n % d == 0)
    return best


def _sigmoid(x):
    return 0.5 * (jnp.tanh(0.5 * x) + 1.0)


def _silu(x):
    return x * _sigmoid(x)


def _dsilu(x):
    s = _sigmoid(x)
    return s * (1.0 + x * (1.0 - s))


MM_TILE = 1408
MM_TK = 2816


def _matmul(a, b, *, ta=False, tb=False, out_dtype=F32, add=None, bias=None, tm=MM_TILE, tn=MM_TILE, tk=MM_TK, name,
            mnk=None, a_spec=None, b_spec=None, o_spec=None, o_shape=None):
    if mnk is None:
        m, k = (a.shape[1], a.shape[0]) if ta else a.shape
        n = b.shape[0] if tb else b.shape[1]
        assert k == (b.shape[1] if tb else b.shape[0])
    else:
        m, n, k = mnk
    tm, tn, tk = _tile(m, tm), _tile(n, tn), _tile(k, tk)
    nk = k // tk
    dims = (((0 if ta else 1,), (1 if tb else 0,)), ((), ()))

    def body(*refs):
        a_ref, b_ref = refs[0], refs[1]
        pos = 2
        add_ref = bias_ref = None
        if add is not None:
            add_ref = refs[pos]
            pos += 1
        if bias is not None:
            bias_ref = refs[pos]
            pos += 1
        o_ref = refs[pos]
        kk = pl.program_id(2)
        av = a_ref[...]
        bv = b_ref[...]
        av = av.reshape(av.shape[-2:]).astype(BF16)
        bv = bv.reshape(bv.shape[-2:]).astype(BF16)
        prod = lax.dot_general(av, bv, dims, preferred_element_type=F32)

        def finish(r):
            if bias_ref is not None:
                r = r + bias_ref[...]
            if add_ref is not None:
                r = r + add_ref[...].astype(F32)
            o_ref[...] = r.astype(out_dtype).reshape(o_ref.shape)

        if nk == 1:
            finish(prod)
        else:
            acc_ref = refs[pos + 1]

            @pl.when(kk == 0)
            def _():
                acc_ref[...] = prod

            @pl.when(kk > 0)
            def _():
                acc_ref[...] += prod

            @pl.when(kk == nk - 1)
            def _():
                finish(acc_ref[...])

    if a_spec is None:
        a_spec = (pl.BlockSpec((tk, tm), lambda i, j, kk: (kk, i)) if ta
                  else pl.BlockSpec((tm, tk), lambda i, j, kk: (i, kk)))
    if b_spec is None:
        b_spec = (pl.BlockSpec((tn, tk), lambda i, j, kk: (j, kk)) if tb
                  else pl.BlockSpec((tk, tn), lambda i, j, kk: (kk, j)))
    if o_spec is None:
        o_spec = pl.BlockSpec((tm, tn), lambda i, j, kk: (i, j))
    if o_shape is None:
        o_shape = (m, n)
    in_specs = [a_spec, b_spec]
    args = [a, b]
    if add is not None:
        in_specs.append(pl.BlockSpec((tm, tn), lambda i, j, kk: (i, j)))
        args.append(add)
    if bias is not None:
        in_specs.append(pl.BlockSpec((1, tn), lambda i, j, kk: (0, j)))
        args.append(bias)
    return pl.pallas_call(
        body, name=name, grid=(m // tm, n // tn, nk), in_specs=in_specs, out_specs=o_spec,
        out_shape=jax.ShapeDtypeStruct(o_shape, out_dtype),
        scratch_shapes=[pltpu.VMEM((tm, tn), F32)] if nk > 1 else [],
        compiler_params=_cp("parallel", "parallel", "arbitrary"),
    )(*args)


def _rmsnorm_fwd(x, w, *, width, cblk=0, out_dtype=BF16, tr=256, name):
    t = x.shape[0]

    def body(x_ref, w_ref, o_ref):
        xv = x_ref[...].astype(F32)
        r = lax.rsqrt(jnp.mean(xv * xv, axis=-1, keepdims=True) + NORM_EPS)
        o_ref[...] = (xv * r * w_ref[...]).astype(out_dtype)

    return pl.pallas_call(
        body, name=name, grid=(t // tr,),
        in_specs=[pl.BlockSpec((tr, width), lambda i: (i, cblk)), pl.BlockSpec((1, width), lambda i: (0, 0))],
        out_specs=pl.BlockSpec((tr, width), lambda i: (i, 0)),
        out_shape=jax.ShapeDtypeStruct((t, width), out_dtype),
        compiler_params=_cp("parallel"),
    )(x, w)


def _rmsnorm_bwd(x, w, dy, add=None, *, width, cblk=0, out_dtype=F32, tr=256, name):
    t = x.shape[0]

    def body(*refs):
        if add is None:
            x_ref, w_ref, dy_ref, dx_ref, dw_ref = refs
            add_ref = None
        else:
            x_ref, w_ref, dy_ref, add_ref, dx_ref, dw_ref = refs
        xv = x_ref[...].astype(F32)
        dyv = dy_ref[...].astype(F32)
        r = lax.rsqrt(jnp.mean(xv * xv, axis=-1, keepdims=True) + NORM_EPS)
        xh = xv * r
        g = dyv * w_ref[...]
        dx = r * (g - xh * jnp.mean(g * xh, axis=-1, keepdims=True))
        if add_ref is not None:
            dx = dx + add_ref[...].astype(F32)
        dx_ref[...] = dx.astype(out_dtype)

        @pl.when(pl.program_id(0) == 0)
        def _():
            dw_ref[...] = jnp.zeros_like(dw_ref)

        dw_ref[...] += jnp.sum(dyv * xh, axis=0, keepdims=True)

    in_specs = [pl.BlockSpec((tr, width), lambda i: (i, cblk)), pl.BlockSpec((1, width), lambda i: (0, 0)),
                pl.BlockSpec((tr, width), lambda i: (i, 0))]
    args = [x, w, dy]
    if add is not None:
        in_specs.append(pl.BlockSpec((tr, width), lambda i: (i, 0)))
        args.append(add)
    return pl.pallas_call(
        body, name=name, grid=(t // tr,), in_specs=in_specs,
        out_specs=[pl.BlockSpec((tr, width), lambda i: (i, 0)), pl.BlockSpec((1, width), lambda i: (0, 0))],
        out_shape=[jax.ShapeDtypeStruct((t, width), out_dtype), jax.ShapeDtypeStruct((1, width), F32)],
        compiler_params=_cp("arbitrary"),
    )(*args)


def _shift_down(prev_halo, cur, j):
    if j == 0:
        return cur
    ext = jnp.concatenate([prev_halo, cur], axis=0)
    return pltpu.roll(ext, j, axis=0)[HALO:]


def _shift_up(cur, next_halo, j):
    if j == 0:
        return cur
    ext = jnp.concatenate([cur, next_halo], axis=0)
    return pltpu.roll(ext, ext.shape[0] - j, axis=0)[:cur.shape[0]]


def _conv_rows(prev, cur, w, b, kw):
    shifted = [cur]
    out = b + w[kw - 1:kw] * cur
    for j in range(1, kw):
        sh = _shift_down(prev, cur, j)
        shifted.append(sh)
        out = out + w[kw - 1 - j:kw - j] * sh
    return out, shifted


def _act_fwd(c, glu):
    if glu:
        half = c.shape[1] // 2
        return _silu(c[:, :half]) * c[:, half:]
    return _silu(c)


def _act_bwd(c, dout, glu):
    if glu:
        half = c.shape[1] // 2
        g, up = c[:, :half], c[:, half:]
        s = _sigmoid(g)
        gs = g * s
        return jnp.concatenate([dout * up * (s + gs * (1.0 - s)), dout * gs], axis=1)
    return dout * _dsilu(c)


def _conv_act_fwd(u, w, b, *, kw, glu, tc, coff, ncols, out_dtype, tr=256, name):
    t = u.shape[0]
    nb = ncols // tc
    oc = tc // 2 if glu else tc

    def body(u_ref, uh_ref, w_ref, b_ref, o_ref):
        prev = jnp.where(pl.program_id(0) == 0, 0.0, uh_ref[...])
        c, _ = _conv_rows(prev, u_ref[...], w_ref[...], b_ref[...], kw)
        o_ref[...] = _act_fwd(c, glu).astype(out_dtype)

    return pl.pallas_call(
        body, name=name, grid=(t // tr, nb),
        in_specs=[pl.BlockSpec((tr, tc), lambda i, j: (i, j + coff)),
                  pl.BlockSpec((HALO, tc), lambda i, j: (jnp.maximum(i * (tr // HALO) - 1, 0), j + coff)),
                  pl.BlockSpec((kw, tc), lambda i, j: (0, j)), pl.BlockSpec((1, tc), lambda i, j: (0, j))],
        out_specs=pl.BlockSpec((tr, oc), lambda i, j: (i, j)),
        out_shape=jax.ShapeDtypeStruct((t, nb * oc), out_dtype),
        compiler_params=_cp("parallel", "parallel"),
    )(u, u, w, b)


def _conv_act_bwd(u, w, b, dout, *, kw, glu, tc, coff, ncols, tr=256, name):
    t = u.shape[0]
    nb = ncols // tc
    nt = t // tr
    oc = tc // 2 if glu else tc

    def body(u_ref, up_ref, un_ref, d_ref, dn_ref, w_ref, b_ref, du_ref, dw_ref, db_ref):
        i = pl.program_id(1)
        cur, nxt, wv, bv = u_ref[...], un_ref[...], w_ref[...], b_ref[...]
        prev = jnp.where(i == 0, 0.0, up_ref[...])
        c_cur, shifted = _conv_rows(prev, cur, wv, bv, kw)
        c_nxt, _ = _conv_rows(cur[tr - HALO:], nxt, wv, bv, kw)
        d_cur = _act_bwd(c_cur, d_ref[...].astype(F32), glu)
        d_nxt = _act_bwd(c_nxt, jnp.where(i == nt - 1, 0.0, dn_ref[...].astype(F32)), glu)
        du = wv[kw - 1:kw] * d_cur
        for j in range(1, kw):
            du = du + wv[kw - 1 - j:kw - j] * _shift_up(d_cur, d_nxt, j)
        du_ref[...] = du.astype(BF16)

        @pl.when(i == 0)
        def _():
            dw_ref[...] = jnp.zeros_like(dw_ref)
            db_ref[...] = jnp.zeros_like(db_ref)

        db_ref[...] += jnp.sum(d_cur, axis=0, keepdims=True)
        dw_ref[...] += jnp.concatenate(
            [jnp.sum(d_cur * shifted[kw - 1 - k], axis=0, keepdims=True) for k in range(kw)], axis=0)

    nh = tr // HALO
    return pl.pallas_call(
        body, name=name, grid=(nb, nt),
        in_specs=[pl.BlockSpec((tr, tc), lambda j, i: (i, j + coff)),
                  pl.BlockSpec((HALO, tc), lambda j, i: (jnp.maximum(i * nh - 1, 0), j + coff)),
                  pl.BlockSpec((HALO, tc), lambda j, i: (jnp.minimum((i + 1) * nh, t // HALO - 1), j + coff)),
                  pl.BlockSpec((tr, oc), lambda j, i: (i, j)),
                  pl.BlockSpec((HALO, oc), lambda j, i: (jnp.minimum((i + 1) * nh, t // HALO - 1), j)),
                  pl.BlockSpec((kw, tc), lambda j, i: (0, j)), pl.BlockSpec((1, tc), lambda j, i: (0, j))],
        out_specs=[pl.BlockSpec((tr, tc), lambda j, i: (i, j)), pl.BlockSpec((kw, tc), lambda j, i: (0, j)),
                   pl.BlockSpec((1, tc), lambda j, i: (0, j))],
        out_shape=[jax.ShapeDtypeStruct((t, ncols), BF16), jax.ShapeDtypeStruct((kw, ncols), F32),
                   jax.ShapeDtypeStruct((1, ncols), F32)],
        compiler_params=_cp("parallel", "arbitrary"),
    )(u, u, u, dout, dout, w, b)


def _ple_fwd(x2, gl, pe, pw, *, tr=256, name):
    t, d = x2.shape

    def body(x_ref, gl_ref, pe_ref, pw_ref, o_ref):
        pv = pe_ref[...]
        r = lax.rsqrt(jnp.mean(pv * pv, axis=-1, keepdims=True) + NORM_EPS)
        o_ref[...] = x_ref[...] + _sigmoid(gl_ref[...]) * (pv * r * pw_ref[...])

    blk = pl.BlockSpec((tr, d), lambda i: (i, 0))
    return pl.pallas_call(
        body, name=name, grid=(t // tr,), in_specs=[blk, blk, blk, pl.BlockSpec((1, d), lambda i: (0, 0))],
        out_specs=blk, out_shape=jax.ShapeDtypeStruct((t, d), F32), compiler_params=_cp("parallel"),
    )(x2, gl, pe, pw)


def _ple_bwd(dx3, gl, pe, pw, *, tr=256, name):
    t, d = dx3.shape

    def body(dx_ref, gl_ref, pe_ref, pw_ref, dgl_ref, db_ref, dpe_ref, dpw_ref):
        dx, pv, pwv = dx_ref[...], pe_ref[...], pw_ref[...]
        gate = _sigmoid(gl_ref[...])
        r = lax.rsqrt(jnp.mean(pv * pv, axis=-1, keepdims=True) + NORM_EPS)
        ph = pv * r
        dgl = dx * (ph * pwv) * gate * (1.0 - gate)
        de = dx * gate
        g = de * pwv
        dgl_ref[...] = dgl.astype(BF16)
        dpe_ref[...] = (r * (g - ph * jnp.mean(g * ph, axis=-1, keepdims=True))).astype(BF16)

        @pl.when(pl.program_id(0) == 0)
        def _():
            db_ref[...] = jnp.zeros_like(db_ref)
            dpw_ref[...] = jnp.zeros_like(dpw_ref)

        db_ref[...] += jnp.sum(dgl, axis=0, keepdims=True)
        dpw_ref[...] += jnp.sum(de * ph, axis=0, keepdims=True)

    blk = pl.BlockSpec((tr, d), lambda i: (i, 0))
    row = pl.BlockSpec((1, d), lambda i: (0, 0))
    return pl.pallas_call(
        body, name=name, grid=(t // tr,), in_specs=[blk, blk, blk, row], out_specs=[blk, row, blk, row],
        out_shape=[jax.ShapeDtypeStruct((t, d), BF16), jax.ShapeDtypeStruct((1, d), F32),
                   jax.ShapeDtypeStruct((t, d), BF16), jax.ShapeDtypeStruct((1, d), F32)],
        compiler_params=_cp("arbitrary"),
    )(dx3, gl, pe, pw)


def _loss_head(x3, fw, target, *, tr=256, name):
    t, d = x3.shape

    def body(x_ref, w_ref, t_ref, l_ref, dx_ref, dw_ref):
        xv, wv = x_ref[...], w_ref[...]
        r = lax.rsqrt(jnp.mean(xv * xv, axis=-1, keepdims=True) + NORM_EPS)
        xh = xv * r
        err = xh * wv - t_ref[...]
        dy = err * (1.0 / d)
        g = dy * wv
        dx_ref[...] = r * (g - xh * jnp.mean(g * xh, axis=-1, keepdims=True))

        @pl.when(pl.program_id(0) == 0)
        def _():
            l_ref[...] = jnp.zeros_like(l_ref)
            dw_ref[...] = jnp.zeros_like(dw_ref)

        l_ref[...] += 0.5 * jnp.sum(jnp.mean(err * err, axis=-1, keepdims=True), axis=0, keepdims=True)
        dw_ref[...] += jnp.sum(dy * xh, axis=0, keepdims=True)

    blk = pl.BlockSpec((tr, d), lambda i: (i, 0))
    row = pl.BlockSpec((1, d), lambda i: (0, 0))
    return pl.pallas_call(
        body, name=name, grid=(t // tr,), in_specs=[blk, row, blk],
        out_specs=[pl.BlockSpec((1, 1), lambda i: (0, 0)), blk, row],
        out_shape=[jax.ShapeDtypeStruct((1, 1), F32), jax.ShapeDtypeStruct((t, d), F32),
                   jax.ShapeDtypeStruct((1, d), F32)],
        compiler_params=_cp("arbitrary"),
    )(x3, fw, target)


def _rope(blk, tab_ref):
    return blk * tab_ref[0] + pltpu.roll(blk, 96, axis=1) * tab_ref[1] + pltpu.roll(blk, 32, axis=1) * tab_ref[2]


def _unrope(g, tab_ref):
    return g * tab_ref[0] + pltpu.roll(g * tab_ref[1], 32, axis=1) + pltpu.roll(g * tab_ref[2], 96, axis=1)


def _mla_prep(q, kv, proj, tabs, *, tr=512, name):
    t = q.shape[0]

    def body(q_ref, kv_ref, kr_ref, tab_ref, qo_ref, ko_ref, vo_ref, vt_ref):
        qv, kvv = q_ref[...], kv_ref[...]
        qo_ref[0, :, :MLA_NOPE] = qv[:, :MLA_NOPE].astype(BF16)
        qo_ref[0, :, MLA_NOPE:] = _rope(qv[:, MLA_NOPE:], tab_ref).astype(BF16)
        ko_ref[0, :, :MLA_NOPE] = kvv[:, :MLA_NOPE].astype(BF16)
        ko_ref[0, :, MLA_NOPE:] = _rope(kr_ref[...], tab_ref).astype(BF16)
        vo_ref[0] = kvv[:, MLA_NOPE:].astype(BF16)
        for blk in range(tr // ATT_BLK):
            vt_ref[0, blk] = kvv[blk * ATT_BLK:(blk + 1) * ATT_BLK, MLA_NOPE:].T.astype(BF16)

    return pl.pallas_call(
        body, name=name, grid=(t // tr, MLA_HEADS),
        in_specs=[pl.BlockSpec((tr, MLA_QK_PAD), lambda i, h: (i, h)),
                  pl.BlockSpec((tr, MLA_NOPE + MLA_V), lambda i, h: (i, h)),
                  pl.BlockSpec((tr, LANES), lambda i, h: (i, OFF_KR // LANES)),
                  pl.BlockSpec((3, tr, LANES), lambda i, h: (0, i, 0))],
        out_specs=[pl.BlockSpec((1, tr, MLA_QK_PAD), lambda i, h: (h, i, 0)),
                   pl.BlockSpec((1, tr, MLA_QK_PAD), lambda i, h: (h, i, 0)),
                   pl.BlockSpec((1, tr, MLA_V), lambda i, h: (h, i, 0)),
                   pl.BlockSpec((1, tr // ATT_BLK, MLA_V, ATT_BLK), lambda i, h: (h, i, 0, 0))],
        out_shape=[jax.ShapeDtypeStruct((MLA_HEADS, t, MLA_QK_PAD), BF16),
                   jax.ShapeDtypeStruct((MLA_HEADS, t, MLA_QK_PAD), BF16),
                   jax.ShapeDtypeStruct((MLA_HEADS, t, MLA_V), BF16),
                   jax.ShapeDtypeStruct((MLA_HEADS, t // ATT_BLK, MLA_V, ATT_BLK), BF16)],
        compiler_params=_cp("parallel", "parallel"),
    )(q, kv, proj, tabs)


def _mla_unprep(dq3, dk3, dv3, tabs, *, tr=256, name):
    t = dq3.shape[1]

    def body(dq_ref, dk_ref, dv_ref, tab_ref, qo_ref, kvo_ref, kro_ref):
        kr = jnp.zeros((tr, LANES), F32)
        for h in range(MLA_HEADS):
            c0 = h * MLA_QK_PAD
            qo_ref[:, c0:c0 + MLA_NOPE] = dq_ref[h, :, :MLA_NOPE].astype(BF16)
            qo_ref[:, c0 + MLA_NOPE:c0 + MLA_QK_PAD] = _unrope(dq_ref[h, :, MLA_NOPE:], tab_ref).astype(BF16)
            kvo_ref[:, c0:c0 + MLA_NOPE] = dk_ref[h, :, :MLA_NOPE].astype(BF16)
            kvo_ref[:, c0 + MLA_NOPE:c0 + MLA_QK_PAD] = dv_ref[h].astype(BF16)
            kr = kr + dk_ref[h, :, MLA_NOPE:]
        kro_ref[...] = _unrope(kr, tab_ref).astype(BF16)

    return pl.pallas_call(
        body, name=name, grid=(t // tr,),
        in_specs=[pl.BlockSpec((MLA_HEADS, tr, MLA_QK_PAD), lambda i: (0, i, 0)),
                  pl.BlockSpec((MLA_HEADS, tr, MLA_QK_PAD), lambda i: (0, i, 0)),
                  pl.BlockSpec((MLA_HEADS, tr, MLA_V), lambda i: (0, i, 0)),
                  pl.BlockSpec((3, tr, LANES), lambda i: (0, i, 0))],
        out_specs=[pl.BlockSpec((tr, MLA_HEADS * MLA_QK_PAD), lambda i: (i, 0)),
                   pl.BlockSpec((tr, MLA_HEADS * MLA_QK_PAD), lambda i: (i, 0)),
                   pl.BlockSpec((tr, LANES), lambda i: (i, 0))],
        out_shape=[jax.ShapeDtypeStruct((t, MLA_HEADS * MLA_QK_PAD), BF16),
                   jax.ShapeDtypeStruct((t, MLA_HEADS * MLA_QK_PAD), BF16),
                   jax.ShapeDtypeStruct((t, LANES), BF16)],
        compiler_params=_cp("parallel"),
    )(dq3, dk3, dv3, tabs)


ATT_BLK = 512
ATT_SCALE = 1.0 / math.sqrt(MLA_NOPE + MLA_ROPE)
_NT = (((1,), (1,)), ((), ()))
_TN = (((0,), (0,)), ((), ()))


def _att_scores_t(k, q, diagonal):
    s = lax.dot_general(k, q, _NT, preferred_element_type=F32) * ATT_SCALE
    if not diagonal:
        return s
    key = lax.broadcasted_iota(jnp.int32, s.shape, 0)
    query = lax.broadcasted_iota(jnp.int32, s.shape, 1)
    return jnp.where((key >> 6) <= (query >> 6), s, NEG)


def _att_rows(i):
    return pl.ds(pl.multiple_of(i * ATT_BLK, ATT_BLK), ATT_BLK)


ATT_HEADS = 1


def _attn_fwd(q3, k3, vt4, *, name):
    t = q3.shape[1]
    nq = t // ATT_BLK

    def body(q_ref, k_ref, vt_ref, o_ref, lse_ref):
        qi = pl.program_id(1)
        qs = [q_ref[hh] for hh in range(ATT_HEADS)]

        def step(j, carry, diagonal=False):
            out = []
            for hh, (m, l, acc) in enumerate(carry):
                s = _att_scores_t(k_ref[hh, _att_rows(j), :], qs[hh], diagonal)
                m_new = jnp.maximum(m, jnp.max(s, axis=0, keepdims=True))
                p = jnp.exp(s - m_new)
                alpha = jnp.exp(m - m_new)
                l = alpha * l + jnp.sum(p, axis=0, keepdims=True)
                acc = alpha * acc + jnp.dot(vt_ref[hh, j], p.astype(BF16), preferred_element_type=F32)
                out.append((m_new, l, acc))
            return tuple(out)

        init = tuple((jnp.full((1, ATT_BLK), NEG, F32), jnp.zeros((1, ATT_BLK), F32),
                      jnp.zeros((MLA_V, ATT_BLK), F32)) for _ in range(ATT_HEADS))
        done = step(qi, lax.fori_loop(0, qi, step, init), diagonal=True)
        for hh, (m, l, acc) in enumerate(done):
            o_ref[:, hh * MLA_V:(hh + 1) * MLA_V] = (acc / l).T
            lse_ref[hh, 0] = m + jnp.log(l)

    return pl.pallas_call(
        body, name=name, grid=(MLA_HEADS // ATT_HEADS, nq),
        in_specs=[pl.BlockSpec((ATT_HEADS, ATT_BLK, MLA_QK_PAD), lambda h, i: (h, i, 0)),
                  pl.BlockSpec((ATT_HEADS, t, MLA_QK_PAD), lambda h, i: (h, 0, 0)),
                  pl.BlockSpec((ATT_HEADS, nq, MLA_V, ATT_BLK), lambda h, i: (h, 0, 0, 0))],
        out_specs=[pl.BlockSpec((ATT_BLK, ATT_HEADS * MLA_V), lambda h, i: (i, h)),
                   pl.BlockSpec((ATT_HEADS, 1, 1, ATT_BLK), lambda h, i: (h, i, 0, 0))],
        out_shape=[jax.ShapeDtypeStruct((t, MLA_HEADS * MLA_V), F32),
                   jax.ShapeDtypeStruct((MLA_HEADS, nq, 1, ATT_BLK), F32)],
        compiler_params=_cp("parallel", "parallel"),
    )(q3, k3, vt4)


def _attn_bwd(q3, k3, v3, o, dcat, lse, *, name):
    t = q3.shape[1]
    nq = t // ATT_BLK
    wide = ATT_HEADS * MLA_V

    def body(q_ref, k_ref, v_ref, o_ref, do_ref, lse_ref, dq_ref, dk_ref, dv_ref, delta_ref):
        kj = pl.program_id(1)

        @pl.when(kj == 0)
        def _():
            dq_ref[...] = jnp.zeros_like(dq_ref)
            ones = jnp.ones((HALO, MLA_V), F32)
            for i in range(nq):
                rows = pl.ds(i * ATT_BLK, ATT_BLK)
                prod = o_ref[rows, :] * do_ref[rows, :]
                for hh in range(ATT_HEADS):
                    delta_ref[hh, i] = lax.dot_general(ones, prod[:, hh * MLA_V:(hh + 1) * MLA_V], _NT, precision=HI,
                                                       preferred_element_type=F32)

        def step(i, carry, diagonal=False):
            rows = _att_rows(i)
            out = []
            for hh, (dk, dv) in enumerate(carry):
                k, v = k_ref[hh], v_ref[hh]
                q = q_ref[hh, rows, :]
                dob = do_ref[rows, hh * MLA_V:(hh + 1) * MLA_V].astype(BF16)
                p = jnp.exp(_att_scores_t(k, q, diagonal) - lse_ref[hh, i])
                dv = dv + jnp.dot(p.astype(BF16), dob, preferred_element_type=F32)
                dp = lax.dot_general(v, dob, _NT, preferred_element_type=F32)
                ds = (p * (dp - delta_ref[hh, i, 0:1, :]) * ATT_SCALE).astype(BF16)
                dk = dk + jnp.dot(ds, q, preferred_element_type=F32)
                dq_ref[hh, rows, :] += lax.dot_general(ds, k, _TN, preferred_element_type=F32)
                out.append((dk, dv))
            return tuple(out)

        init = tuple((jnp.zeros((ATT_BLK, MLA_QK_PAD), F32), jnp.zeros((ATT_BLK, MLA_V), F32))
                     for _ in range(ATT_HEADS))
        done = lax.fori_loop(kj + 1, nq, step, step(kj, init, diagonal=True))
        for hh, (dk, dv) in enumerate(done):
            dk_ref[hh] = dk
            dv_ref[hh] = dv

    return pl.pallas_call(
        body, name=name, grid=(MLA_HEADS // ATT_HEADS, nq),
        in_specs=[pl.BlockSpec((ATT_HEADS, t, MLA_QK_PAD), lambda h, j: (h, 0, 0)),
                  pl.BlockSpec((ATT_HEADS, ATT_BLK, MLA_QK_PAD), lambda h, j: (h, j, 0)),
                  pl.BlockSpec((ATT_HEADS, ATT_BLK, MLA_V), lambda h, j: (h, j, 0)),
                  pl.BlockSpec((t, wide), lambda h, j: (0, h)),
                  pl.BlockSpec((t, wide), lambda h, j: (0, MLA_HEADS // ATT_HEADS + h)),
                  pl.BlockSpec((ATT_HEADS, nq, 1, ATT_BLK), lambda h, j: (h, 0, 0, 0))],
        out_specs=[pl.BlockSpec((ATT_HEADS, t, MLA_QK_PAD), lambda h, j: (h, 0, 0)),
                   pl.BlockSpec((ATT_HEADS, ATT_BLK, MLA_QK_PAD), lambda h, j: (h, j, 0)),
                   pl.BlockSpec((ATT_HEADS, ATT_BLK, MLA_V), lambda h, j: (h, j, 0))],
        out_shape=[jax.ShapeDtypeStruct((MLA_HEADS, t, MLA_QK_PAD), F32),
                   jax.ShapeDtypeStruct((MLA_HEADS, t, MLA_QK_PAD), F32),
                   jax.ShapeDtypeStruct((MLA_HEADS, t, MLA_V), F32)],
        scratch_shapes=[pltpu.VMEM((ATT_HEADS, nq, HALO, ATT_BLK), F32)],
        compiler_params=_cp("parallel", "arbitrary"),
    )(q3, k3, v3, o, dcat, lse)


def _ssd_prep(proj, bias128, alog128, *, name):
    t = proj.shape[0]
    nc = t // CHUNK

    def body(raw_ref, b_ref, al_ref, dt_ref, cs_ref, a_ref):
        xv = raw_ref[...] + b_ref[...]
        dt = jnp.maximum(xv, 0.0) + jnp.log(1.0 + jnp.exp(-jnp.abs(xv)))
        a = -jnp.exp(al_ref[...])
        adt = (dt * a).reshape(nc, CHUNK, LANES)
        li = lax.broadcasted_iota(jnp.int32, (nc, CHUNK, CHUNK), 1)
        si = lax.broadcasted_iota(jnp.int32, (nc, CHUNK, CHUNK), 2)
        tril = jnp.where(si <= li, 1.0, 0.0).astype(F32)
        cs = lax.dot_general(tril, adt, (((2,), (1,)), ((0,), (0,))), precision=HI, preferred_element_type=F32)
        dt_ref[...] = dt
        cs_ref[...] = cs.reshape(t, LANES)
        a_ref[...] = a

    blk = pl.BlockSpec((t, LANES), lambda i: (0, 0))
    row = pl.BlockSpec((1, LANES), lambda i: (0, 0))
    return pl.pallas_call(
        body, name=name, grid=(1,),
        in_specs=[pl.BlockSpec((t, LANES), lambda i: (0, OFF_DT // LANES)), row, row],
        out_specs=[blk, blk, row],
        out_shape=[jax.ShapeDtypeStruct((t, LANES), F32), jax.ShapeDtypeStruct((t, LANES), F32),
                   jax.ShapeDtypeStruct((1, LANES), F32)],
        compiler_params=_cp("arbitrary"),
    )(proj, bias128, alog128)


def _ssd_prep_bwd(ddt128, dadt128, proj, bias128, dt128, a128, dd_h, *, name):
    t = proj.shape[0]

    def body(ddt_ref, dadt_ref, raw_ref, b_ref, dt_ref, a_ref, dd_ref, draw_ref, db_ref, dal_ref, dds_ref):
        draw = ddt_ref[...] * _sigmoid(raw_ref[...] + b_ref[...])
        draw_ref[...] = draw.astype(BF16)
        db_ref[...] = jnp.sum(draw, axis=0, keepdims=True)
        dal_ref[...] = jnp.sum(dadt_ref[...] * dt_ref[...], axis=0, keepdims=True) * a_ref[...]
        dds_ref[...] = jnp.sum(dd_ref[...], axis=-1, keepdims=True)

    blk = pl.BlockSpec((t, LANES), lambda i: (0, 0))
    row = pl.BlockSpec((1, LANES), lambda i: (0, 0))
    return pl.pallas_call(
        body, name=name, grid=(1,),
        in_specs=[blk, blk, pl.BlockSpec((t, LANES), lambda i: (0, OFF_DT // LANES)), row, blk, row,
                  pl.BlockSpec((SSD_HEADS, SSD_P), lambda i: (0, 0))],
        out_specs=[blk, row, row, pl.BlockSpec((SSD_HEADS, 1), lambda i: (0, 0))],
        out_shape=[jax.ShapeDtypeStruct((t, LANES), BF16), jax.ShapeDtypeStruct((1, LANES), F32),
                   jax.ShapeDtypeStruct((1, LANES), F32), jax.ShapeDtypeStruct((SSD_HEADS, 1), F32)],
        compiler_params=_cp("arbitrary"),
    )(ddt128, dadt128, proj, bias128, dt128, a128, dd_h)


def _bdot(a, b, ca, cb, precision=None):
    return lax.dot_general(a, b, (((ca,), (cb,)), ((0,), (0,))), precision=precision, preferred_element_type=F32)


def _head_matrices():
    eye, zero = jnp.eye(SSD_P, dtype=F32), jnp.zeros((SSD_P, SSD_P), F32)
    pick = jnp.stack([jnp.concatenate([eye, zero], axis=0), jnp.concatenate([zero, eye], axis=0)])
    return pick, pick.transpose(0, 2, 1)


def _move(x, sel):
    selb = sel.astype(BF16)
    hi = x.astype(BF16)
    rest = x - hi.astype(F32)
    mid = rest.astype(BF16)
    low = (rest - mid.astype(F32)).astype(BF16)
    out = jnp.dot(hi, selb, preferred_element_type=F32)
    out = out + jnp.dot(mid, selb, preferred_element_type=F32)
    return out + jnp.dot(low, selb, preferred_element_type=F32)


def _pick_head(pair_ref, pick_ref, h):
    return _move(pair_ref[...], pick_ref[h % 2])


def _place_head(out_ref, val, place_ref, h):
    wide = _move(val, place_ref[h % 2])

    @pl.when(h % 2 == 0)
    def _():
        out_ref[...] = wide

    @pl.when(h % 2 == 1)
    def _():
        out_ref[...] += wide


def _ssd_common(x2, dt_ref, cs_ref, csr_ref, b_ref, c_ref, nc):
    x = x2.reshape(nc, CHUNK, SSD_P)
    dt = dt_ref[0].reshape(nc, CHUNK, SSD_P)
    cs = cs_ref[0].reshape(nc, CHUNK, SSD_P)
    csr = csr_ref[0]
    bm = b_ref[...].reshape(nc, CHUNK, SSD_N).astype(BF16)
    cm = c_ref[...].reshape(nc, CHUNK, SSD_N).astype(BF16)
    li = lax.broadcasted_iota(jnp.int32, (nc, CHUNK, CHUNK), 1)
    si = lax.broadcasted_iota(jnp.int32, (nc, CHUNK, CHUNK), 2)
    lmat = jnp.exp(jnp.where(si <= li, cs - csr, NEG))
    g = _bdot(cm, bm, 2, 2)
    cs_last = jnp.sum(jnp.where(li == CHUNK - 1, cs, 0.0), axis=1, keepdims=True)
    xdt = x * dt
    dec = jnp.exp(cs_last - cs)
    return x, dt, cs, bm, cm, li, si, lmat, g, cs_last, xdt, dec


def _ssd_fwd(xbc, dt_h, cs_h, cs_row, dskip_h, *, name):
    t = xbc.shape[0]
    nc = t // CHUNK
    hpg = SSD_HEADS // SSD_GROUPS
    pick, place = _head_matrices()

    def body(xs_ref, dt_ref, cs_ref, csr_ref, b_ref, c_ref, dk_ref, pick_ref, place_ref, y_ref, st_ref, sc_ref, cd_ref):
        h = pl.program_id(0)
        x, dt, cs, bm, cm, li, si, lmat, g, cs_last, xdt, dec = _ssd_common(_pick_head(xs_ref, pick_ref, h), dt_ref,
                                                                           cs_ref, csr_ref, b_ref, c_ref, nc)
        yd = _bdot((g * lmat).astype(BF16), xdt.astype(BF16), 2, 1)
        sc_ref[...] = _bdot(bm, (dec * xdt).astype(BF16), 1, 1)
        cd_ref[...] = jnp.exp(cs_last)

        def step(c, s):
            st_ref[0, c] = s
            return s * cd_ref[c] + sc_ref[c]

        lax.fori_loop(0, nc, step, jnp.zeros((SSD_N, SSD_P), F32))
        yo = _bdot(cm, st_ref[0].astype(BF16), 2, 1) * jnp.exp(cs)
        _place_head(y_ref, (yd + yo + dk_ref[0] * x).reshape(t, SSD_P), place_ref, h)

    head = pl.BlockSpec((1, t, SSD_P), lambda h: (h, 0, 0))
    pair = pl.BlockSpec((t, 2 * SSD_P), lambda h: (0, h // 2))
    nxb = D_SSM // SSD_N
    return pl.pallas_call(
        body, name=name, grid=(SSD_HEADS,),
        in_specs=[pair, head, head, pl.BlockSpec((1, nc, 1, CHUNK), lambda h: (h, 0, 0, 0)),
                  pl.BlockSpec((t, SSD_N), lambda h: (0, nxb + h // hpg)),
                  pl.BlockSpec((t, SSD_N), lambda h: (0, nxb + SSD_GROUPS + h // hpg)),
                  pl.BlockSpec((1, 1, SSD_P), lambda h: (h, 0, 0)),
                  pl.BlockSpec((2, 2 * SSD_P, SSD_P), lambda h: (0, 0, 0)),
                  pl.BlockSpec((2, SSD_P, 2 * SSD_P), lambda h: (0, 0, 0))],
        out_specs=[pair, pl.BlockSpec((1, nc, SSD_N, SSD_P), lambda h: (h, 0, 0, 0))],
        out_shape=[jax.ShapeDtypeStruct((t, D_SSM), F32),
                   jax.ShapeDtypeStruct((SSD_HEADS, nc, SSD_N, SSD_P), F32)],
        scratch_shapes=[pltpu.VMEM((nc, SSD_N, SSD_P), F32), pltpu.VMEM((nc, 1, SSD_P), F32)],
        compiler_params=_cp("arbitrary"),
    )(xbc, dt_h, cs_h, cs_row, xbc, xbc, dskip_h, pick, place)


def _ssd_bwd(xbc, dt_h, cs_h, cs_row, dskip_h, a_h, states, dy, *, name):
    t = xbc.shape[0]
    nc = t // CHUNK
    hpg = SSD_HEADS // SSD_GROUPS
    pick, place = _head_matrices()

    def body(xs_ref, dt_ref, cs_ref, csr_ref, b_ref, c_ref, dk_ref, a_ref, st_ref, dy_ref, pick_ref, place_ref,
             dxs_ref, ddt_ref, dadt_ref, db_ref, dc_ref, dd_ref, dsl_ref, dsc_ref, cd_ref):
        h = pl.program_id(0) * hpg + pl.program_id(1)
        x, dt, cs, bm, cm, li, si, lmat, g, cs_last, xdt, dec = _ssd_common(_pick_head(xs_ref, pick_ref, h), dt_ref,
                                                                           cs_ref, csr_ref, b_ref, c_ref, nc)
        dy = _pick_head(dy_ref, pick_ref, h).reshape(nc, CHUNK, SSD_P)
        dyb = dy.astype(BF16)
        xdtb = xdt.astype(BF16)
        sprev = st_ref[0]
        sprevb = sprev.astype(BF16)
        cdec = jnp.exp(cs_last)
        ecs = jnp.exp(cs)
        dw = (ecs * dy).astype(BF16)
        wmat = _bdot(cm, sprevb, 2, 1)
        dcs = jnp.sum(dy * ecs * wmat, axis=2, keepdims=True)
        dcm = _bdot(dw, sprevb, 2, 2)
        dsl_ref[...] = _bdot(cm, dw, 1, 1)
        cd_ref[...] = cdec

        def step(k, ds):
            c = nc - 1 - k
            dsc_ref[c] = ds
            return ds * cd_ref[c] + dsl_ref[c]

        lax.fori_loop(0, nc, step, jnp.zeros((SSD_N, SSD_P), F32))
        dsc = dsc_ref[...]
        dscb = dsc.astype(BF16)
        d_last = jnp.sum(jnp.sum(dsc * sprev, axis=1, keepdims=True) * cdec, axis=2, keepdims=True)
        z = dec * xdt
        dbm = _bdot(z.astype(BF16), dscb, 2, 2)
        dz = _bdot(bm, dscb, 2, 1)
        dxdt = dec * dz
        t2 = jnp.sum(dz * z, axis=2, keepdims=True)
        dcs = dcs - t2
        d_last = d_last + jnp.sum(t2, axis=1, keepdims=True)
        m = g * lmat
        mb = m.astype(BF16)
        dm = _bdot(dyb, xdtb, 2, 2)
        dxdt = dxdt + _bdot(mb, dyb, 1, 1)
        dseg = dm * m
        dcs = dcs + jnp.sum(dseg, axis=2, keepdims=True)
        ones = jnp.ones((nc, CHUNK, SSD_P), F32)
        dcs = dcs - _bdot(dseg, ones, 1, 1, precision=HI)
        dg = (dm * lmat).astype(BF16)
        dcm = dcm + _bdot(dg, bm, 2, 1)
        dbm = dbm + _bdot(dg, cm, 1, 1)
        dcs = dcs + jnp.where(li[:, :, :SSD_P] == CHUNK - 1, d_last, 0.0)
        triu = jnp.where(li <= si, 1.0, 0.0).astype(F32)
        dadt = _bdot(triu, dcs, 2, 1, precision=HI)
        dk = dk_ref[0]
        _place_head(dxs_ref, (dxdt * dt + dk * dy).reshape(t, SSD_P), place_ref, h)
        ddt = jnp.sum(dxdt * x, axis=2, keepdims=True) + dadt * a_ref[0]
        mine = lax.broadcasted_iota(jnp.int32, (t, LANES), 1) == h

        @pl.when(h == 0)
        def _():
            ddt_ref[...] = jnp.zeros_like(ddt_ref)
            dadt_ref[...] = jnp.zeros_like(dadt_ref)

        ddt_ref[...] += jnp.where(mine, jnp.max(ddt, axis=2, keepdims=True).reshape(t, 1), 0.0)
        dadt_ref[...] += jnp.where(mine, jnp.max(dadt, axis=2, keepdims=True).reshape(t, 1), 0.0)
        dd_ref[0] = jnp.sum(jnp.sum(dy * x, axis=1, keepdims=True), axis=0)

        @pl.when(pl.program_id(1) == 0)
        def _():
            db_ref[...] = jnp.zeros_like(db_ref)
            dc_ref[...] = jnp.zeros_like(dc_ref)

        db_ref[...] += dbm.reshape(t, SSD_N)
        dc_ref[...] += dcm.reshape(t, SSD_N)

    head = pl.BlockSpec((1, t, SSD_P), lambda gi, hi: (gi * hpg + hi, 0, 0))
    pair = pl.BlockSpec((t, 2 * SSD_P), lambda gi, hi: (0, (gi * hpg + hi) // 2))
    grp = pl.BlockSpec((t, SSD_N), lambda gi, hi: (0, gi))
    lane = pl.BlockSpec((1, 1, SSD_P), lambda gi, hi: (gi * hpg + hi, 0, 0))
    rows = pl.BlockSpec((t, LANES), lambda gi, hi: (0, 0))
    nxb = D_SSM // SSD_N
    dxs, ddt, dadt, db, dc, dd = pl.pallas_call(
        body, name=name, grid=(SSD_GROUPS, hpg),
        in_specs=[pair, head, head, pl.BlockSpec((1, nc, 1, CHUNK), lambda gi, hi: (gi * hpg + hi, 0, 0, 0)),
                  pl.BlockSpec((t, SSD_N), lambda gi, hi: (0, nxb + gi)),
                  pl.BlockSpec((t, SSD_N), lambda gi, hi: (0, nxb + SSD_GROUPS + gi)), lane, lane,
                  pl.BlockSpec((1, nc, SSD_N, SSD_P), lambda gi, hi: (gi * hpg + hi, 0, 0, 0)), pair,
                  pl.BlockSpec((2, 2 * SSD_P, SSD_P), lambda gi, hi: (0, 0, 0)),
                  pl.BlockSpec((2, SSD_P, 2 * SSD_P), lambda gi, hi: (0, 0, 0))],
        out_specs=[pair, rows, rows, grp, grp, lane],
        out_shape=[jax.ShapeDtypeStruct((t, D_SSM), F32)] + [jax.ShapeDtypeStruct((t, LANES), F32)] * 2
        + [jax.ShapeDtypeStruct((t, SSD_GROUPS * SSD_N), F32)] * 2
        + [jax.ShapeDtypeStruct((SSD_HEADS, 1, SSD_P), F32)],
        scratch_shapes=[pltpu.VMEM((nc, SSD_N, SSD_P), F32), pltpu.VMEM((nc, SSD_N, SSD_P), F32),
                        pltpu.VMEM((nc, 1, SSD_P), F32)],
        compiler_params=_cp("arbitrary", "arbitrary"),
    )(xbc, dt_h, cs_h, cs_row, xbc, xbc, dskip_h, a_h, states, dy, pick, place)
    return jnp.concatenate([dxs, db, dc], axis=1), ddt, dadt, dd


def _ssd_gate_fwd(y, proj, w, *, tr=256, name):
    t = y.shape[0]
    gw = D_SSM // SSD_GROUPS

    def body(y_ref, z_ref, w_ref, o_ref):
        v = y_ref[...] * _silu(z_ref[...])
        for gi in range(SSD_GROUPS):
            vg = v[:, gi * gw:(gi + 1) * gw]
            r = lax.rsqrt(jnp.mean(vg * vg, axis=-1, keepdims=True) + NORM_EPS)
            o_ref[:, gi * gw:(gi + 1) * gw] = (vg * r * w_ref[:, gi * gw:(gi + 1) * gw]).astype(BF16)

    blk = pl.BlockSpec((tr, D_SSM), lambda i: (i, 0))
    return pl.pallas_call(
        body, name=name, grid=(t // tr,), in_specs=[blk, blk, pl.BlockSpec((1, D_SSM), lambda i: (0, 0))],
        out_specs=blk, out_shape=jax.ShapeDtypeStruct((t, D_SSM), BF16), compiler_params=_cp("parallel"),
    )(y, proj, w)


def _ssd_gate_bwd(y, proj, w, dcat, *, tr=256, name):
    t = y.shape[0]
    gw = D_SSM // SSD_GROUPS

    def body(y_ref, z_ref, w_ref, d_ref, dy_ref, dz_ref, dw_ref):
        yv, zv, dv = y_ref[...], z_ref[...], d_ref[...].astype(F32)
        sz = _silu(zv)
        v = yv * sz

        @pl.when(pl.program_id(0) == 0)
        def _():
            dw_ref[...] = jnp.zeros_like(dw_ref)

        for gi in range(SSD_GROUPS):
            sl = slice(gi * gw, (gi + 1) * gw)
            vg, dg = v[:, sl], dv[:, sl]
            r = lax.rsqrt(jnp.mean(vg * vg, axis=-1, keepdims=True) + NORM_EPS)
            vh = vg * r
            gg = dg * w_ref[:, sl]
            dvg = r * (gg - vh * jnp.mean(gg * vh, axis=-1, keepdims=True))
            dy_ref[:, sl] = dvg * sz[:, sl]
            dz_ref[:, sl] = (dvg * yv[:, sl] * _dsilu(zv[:, sl])).astype(BF16)
            dw_ref[:, sl] += jnp.sum(dg * vh, axis=0, keepdims=True)

    blk = pl.BlockSpec((tr, D_SSM), lambda i: (i, 0))
    row = pl.BlockSpec((1, D_SSM), lambda i: (0, 0))
    return pl.pallas_call(
        body, name=name, grid=(t // tr,), in_specs=[blk, blk, row, blk], out_specs=[blk, blk, row],
        out_shape=[jax.ShapeDtypeStruct((t, D_SSM), F32), jax.ShapeDtypeStruct((t, D_SSM), BF16),
                   jax.ShapeDtypeStruct((1, D_SSM), F32)],
        compiler_params=_cp("arbitrary"),
    )(y, proj, w, dcat)


def _pad_lanes(v):
    return jnp.pad(v, ((0, 0), (0, LANES - v.shape[1])))


def _per_head(v128, t):
    return jnp.broadcast_to(v128[:, :SSD_HEADS].T[:, :, None], (SSD_HEADS, t, SSD_P))


def _ssd_forward(proj, conv_w, conv_b, dt_bias, a_log, d_skip, ssd_norm_w):
    t = proj.shape[0]
    nc = t // CHUNK
    xbc = _conv_act_fwd(proj, conv_w, conv_b, kw=SSD_CONV, glu=False, tc=512, coff=OFF_XBC // 512,
                        ncols=SSD_CONV_DIM, out_dtype=F32, name="ssd_conv_fwd")
    bias128, alog128 = _pad_lanes(dt_bias), _pad_lanes(a_log)
    dt128, cs128, a128 = _ssd_prep(proj, bias128, alog128, name="ssd_prep")
    dt_h, cs_h = _per_head(dt128, t), _per_head(cs128, t)
    cs_row = cs128[:, :SSD_HEADS].T.reshape(SSD_HEADS, nc, 1, CHUNK)
    dskip_h = jnp.broadcast_to(d_skip[0][:, None, None], (SSD_HEADS, 1, SSD_P))
    a_h = jnp.broadcast_to(a128[0, :SSD_HEADS][:, None, None], (SSD_HEADS, 1, SSD_P))
    y, states = _ssd_fwd(xbc, dt_h, cs_h, cs_row, dskip_h, name="ssd_scan_fwd")
    y_ssd = _ssd_gate_fwd(y, proj, ssd_norm_w, name="ssd_gate_fwd")
    saved = (proj, conv_w, conv_b, ssd_norm_w, bias128, dt128, a128, dt_h, cs_h, cs_row, xbc, dskip_h, a_h, states, y)
    return y_ssd, saved


def _ssd_backward(saved, dcat):
    proj, conv_w, conv_b, ssd_norm_w, bias128, dt128, a128, dt_h, cs_h, cs_row, xbc, dskip_h, a_h, states, y = saved
    dy, dz, d_norm_w = _ssd_gate_bwd(y, proj, ssd_norm_w, dcat, name="ssd_gate_bwd")
    dxc, ddt128, dadt128, dd_h = _ssd_bwd(xbc, dt_h, cs_h, cs_row, dskip_h, a_h, states, dy, name="ssd_scan_bwd")
    dxbc, d_conv_w, d_conv_b = _conv_act_bwd(proj, conv_w, conv_b, dxc, kw=SSD_CONV, glu=False, tc=512,
                                             coff=OFF_XBC // 512, ncols=SSD_CONV_DIM, name="ssd_conv_bwd")
    d_raw, d_bias, d_alog, d_dskip = _ssd_prep_bwd(ddt128, dadt128, proj, bias128, dt128, a128,
                                                   dd_h.reshape(SSD_HEADS, SSD_P), name="ssd_prep_bwd")
    return (dz, dxbc, d_raw, d_norm_w, d_conv_w, d_conv_b, d_bias[:, :SSD_HEADS], d_alog[:, :SSD_HEADS],
            d_dskip.reshape(1, SSD_HEADS))


def _rope_tables(positions):
    inv_freq = ROPE_THETA ** (-jnp.arange(0, MLA_ROPE, 2, dtype=F32) / MLA_ROPE)
    ang = positions[0].astype(F32)[:, None] * inv_freq
    cos, sin = jnp.cos(ang), jnp.sin(ang)
    z = jnp.zeros_like(cos)
    return jnp.stack([jnp.concatenate([cos, cos, z, z], axis=1), jnp.concatenate([-sin, z, z, z], axis=1),
                      jnp.concatenate([z, sin, z, z], axis=1)])


def _mla_forward(proj, tabs, q_a_norm_w, wq_pad, kv_a_norm_w, wkv):
    qn = _rmsnorm_fwd(proj, q_a_norm_w, width=MLA_Q_RANK, cblk=OFF_QA // MLA_Q_RANK, name="q_a_norm")
    q = _matmul(qn, wq_pad, name="q_b_proj")
    kvn = _rmsnorm_fwd(proj, kv_a_norm_w, width=MLA_KV_RANK, cblk=OFF_CKV // MLA_KV_RANK, name="kv_a_norm")
    kv = _matmul(kvn, wkv, name="kv_b_proj")
    q3, k3, v3, vt4 = _mla_prep(q, kv, proj, tabs, name="mla_prep")
    o, lse = _attn_fwd(q3, k3, vt4, name="attn_fwd")
    return o, (proj, tabs, q_a_norm_w, wq_pad, kv_a_norm_w, wkv, qn, kvn, q3, k3, v3, o, lse)


def _mla_backward(saved, dcat):
    proj, tabs, q_a_norm_w, wq_pad, kv_a_norm_w, wkv, qn, kvn, q3, k3, v3, o, lse = saved
    dq3, dk3, dv3 = _attn_bwd(q3, k3, v3, o, dcat, lse, name="attn_bwd")
    dq, dkv, dkr = _mla_unprep(dq3, dk3, dv3, tabs, name="mla_unprep")
    d_wq = _matmul(qn, dq, ta=True, out_dtype=BF16, name="d_w_q_b")
    dqn = _matmul(dq, wq_pad, tb=True, name="d_qn")
    dq_a, d_qnw = _rmsnorm_bwd(proj, q_a_norm_w, dqn, width=MLA_Q_RANK, cblk=OFF_QA // MLA_Q_RANK, out_dtype=BF16,
                               name="q_a_norm_bwd")
    d_wkv = _matmul(kvn, dkv, ta=True, out_dtype=BF16, name="d_w_kv_b")
    dkvn = _matmul(dkv, wkv, tb=True, name="d_kvn")
    dckv, d_kvnw = _rmsnorm_bwd(proj, kv_a_norm_w, dkvn, width=MLA_KV_RANK, cblk=OFF_CKV // MLA_KV_RANK,
                                out_dtype=BF16, name="kv_a_norm_bwd")
    return dq_a, dckv, dkr, d_wq, d_wkv, d_qnw, d_kvnw


def _pad_w_q(w):
    r = w.shape[0]
    w3 = w.reshape(r, MLA_HEADS, MLA_NOPE + MLA_ROPE)
    return jnp.pad(w3, ((0, 0), (0, 0), (0, MLA_QK_PAD - MLA_NOPE - MLA_ROPE))).reshape(r, MLA_HEADS * MLA_QK_PAD)


def _unpad_w_q(w):
    r = w.shape[0]
    return w.reshape(r, MLA_HEADS, MLA_QK_PAD)[:, :, :MLA_NOPE + MLA_ROPE].reshape(r, MLA_HEADS * (MLA_NOPE + MLA_ROPE))


W_IN_SEGMENTS = ((0, D_SSM + SSD_CONV_DIM, 0), (D_SSM + SSD_CONV_DIM, D_SSM + SSD_CONV_DIM + SSD_HEADS, OFF_DT),
                 (D_SSM + SSD_CONV_DIM + SSD_HEADS, D_IN - MLA_ROPE, OFF_QA), (D_IN - MLA_ROPE, D_IN, OFF_KR))


def _pad_w_in_shards(g):
    n = g.shape[2]
    pieces, at = [], 0
    for lo, hi, start in sorted(W_IN_SEGMENTS, key=lambda seg: seg[2]):
        if start > at:
            pieces.append(jnp.zeros((g.shape[1], start - at), g.dtype))
        for j in range(N_DEV):
            a, b = max(lo, j * n), min(hi, (j + 1) * n)
            if a < b:
                pieces.append(g[j][:, a - j * n:b - j * n])
        at = start + hi - lo
    pieces.append(jnp.zeros((g.shape[1], D_IN_PAD - at), g.dtype))
    return jnp.concatenate(pieces, axis=1)


def _unpad_w_in_shards(w):
    n = D_IN // N_DEV
    shards = []
    for j in range(N_DEV):
        pieces = []
        for lo, hi, start in W_IN_SEGMENTS:
            a, b = max(lo, j * n), min(hi, (j + 1) * n)
            if a < b:
                pieces.append(w[:, start + a - lo:start + b - lo])
        shards.append(jnp.concatenate(pieces, axis=1) if len(pieces) > 1 else pieces[0])
    return jnp.stack(shards)


WEIGHTS = ['mix_norm_w', 'w_in', 'conv_w', 'conv_b', 'dt_bias', 'a_log', 'd_skip', 'ssd_norm_w', 'q_a_norm_w', 'w_q_b',
           'kv_a_norm_w', 'w_kv_b', 'w_out', 'ffn_norm_w', 'w_ffn_up', 'ffn_conv_w', 'ffn_conv_b', 'w_ffn_down',
           'ple_norm_w', 'w_ple_gate', 'b_ple_gate', 'w_ple_proj', 'ple_post_norm_w', 'final_norm_w']
BIG = ['w_in', 'w_q_b', 'w_kv_b', 'w_out', 'w_ffn_up', 'w_ffn_down', 'w_ple_gate', 'w_ple_proj']
COL_SHARDED = ('w_in', 'w_q_b', 'w_kv_b', 'w_ffn_up', 'w_ple_proj')
CONV = ['conv_w', 'ffn_conv_w']
REPL = [n for n in WEIGHTS if n not in BIG and n not in CONV]
FFN_INV = tuple(int(i) for i in np.argsort(FFN_PERM))


def _cat_cols(g):
    return jnp.concatenate([g[j] for j in range(N_DEV)], axis=1)


def _split_cols(w):
    n = w.shape[1] // N_DEV
    return jnp.stack([w[:, j * n:(j + 1) * n] for j in range(N_DEV)])


def _interleave(v):
    r = v.shape[0]
    return v.reshape(r, N_DEV, FFN_TC)[:, jnp.array(FFN_PERM)].reshape(r, N_DEV * FFN_TC)


def _deinterleave(v):
    r = v.shape[0]
    return v.reshape(r, N_DEV, FFN_TC)[:, jnp.array(FFN_INV)].reshape(r, N_DEV * FFN_TC)


def _assemble_weights(g):
    layout = {
        'w_in': _pad_w_in_shards,
        'w_q_b': lambda v: _pad_w_q(_cat_cols(v)),
        'w_kv_b': _cat_cols,
        'w_out': lambda v: v.reshape(D_MODEL, D_MODEL),
        'w_ffn_up': lambda v: v,
        'w_ffn_down': lambda v: v.reshape(D_FF, D_MODEL),
        'w_ple_gate': lambda v: v.reshape(D_MODEL, D_MODEL),
        'w_ple_proj': _cat_cols,
        'conv_w': _cat_cols,
        'ffn_conv_w': lambda v: _interleave(_cat_cols(v)),
    }
    return {n: layout[n](v) for n, v in g.items()}


WEIGHT_GROUPS = {'a': ['w_in', 'w_q_b', 'w_kv_b', 'conv_w'], 'b': ['w_out'],
                 'c': ['w_ffn_up', 'ffn_conv_w', 'w_ffn_down', 'w_ple_gate', 'w_ple_proj']}
GRAD_GROUPS = {'p': ['w_ple_proj', 'w_ple_gate', 'w_ffn_down'], 'r': ['w_ffn_up'], 's': ['w_out'],
               't': ['w_q_b', 'w_kv_b', 'w_in']}


def _ffn_perm(j):
    return (j % 2) * (N_DEV // 2) + j // 2


def _local_step(x, p, tabs, get_w, s, target, emit, relay, settle):
    t = x.shape[0]
    s = dict(s)
    half = D_MODEL // 2
    up_cols = 2 * D_FF
    ffn_conv_b = _interleave(s['ffn_conv_b'])
    w = dict(get_w('a', None))
    h = _rmsnorm_fwd(x, s['mix_norm_w'], width=D_MODEL, name="mix_norm")
    proj = _matmul(h, w['w_in'], name="in_proj")
    y_ssd, ssd_saved = _ssd_forward(proj, w['conv_w'], s['conv_b'], s['dt_bias'], s['a_log'], s['d_skip'],
                                    s['ssd_norm_w'])
    o, mla_saved = _mla_forward(proj, tabs, s['q_a_norm_w'], w['w_q_b'], s['kv_a_norm_w'], w['w_kv_b'])
    tk_o, tn_o = _tile(half, MM_TK), _tile(D_MODEL, MM_TILE)
    w.update(get_w('b', o))
    x1 = _matmul(y_ssd, w['w_out'], add=x, mnk=(t, D_MODEL, half), name="out_proj_ssd")
    x1 = _matmul(o, w['w_out'], add=x1, mnk=(t, D_MODEL, half), name="out_proj_mla",
                 b_spec=pl.BlockSpec((tk_o, tn_o), lambda i, j, kk: (kk + half // tk_o, j)))
    hf = _rmsnorm_fwd(x1, s['ffn_norm_w'], width=D_MODEL, name="ffn_norm")
    w.update(get_w('c', hf))
    tk_u = _tile(D_MODEL, MM_TK)
    u = _matmul(hf, w['w_ffn_up'], mnk=(t, up_cols, D_MODEL), tn=FFN_TC, name="ffn_up",
                b_spec=pl.BlockSpec((1, tk_u, FFN_TC), lambda i, j, kk: (_ffn_perm(j), kk, 0)))
    act = _conv_act_fwd(u, w['ffn_conv_w'], ffn_conv_b, kw=FFN_CONV, glu=True, tc=2 * FFN_TC, coff=0, ncols=up_cols,
                        out_dtype=BF16, name="ffn_act")
    x2 = _matmul(act, w['w_ffn_down'], add=x1, name="ffn_down")
    hp = _rmsnorm_fwd(x2, s['ple_norm_w'], width=D_MODEL, name="ple_norm")
    gl = _matmul(hp, w['w_ple_gate'], bias=s['b_ple_gate'], name="ple_gate")
    pe = _matmul(p, w['w_ple_proj'], name="ple_proj")
    x3 = _ple_fwd(x2, gl, pe, s['ple_post_norm_w'], name="ple_mix")
    loss, dx3, d_final = _loss_head(x3, s['final_norm_w'], target, name="loss_head")
    dgl, d_bgate, dpe, d_post = _ple_bwd(dx3, gl, pe, s['ple_post_norm_w'], name="ple_mix_bwd")
    d_wproj = _matmul(p, dpe, ta=True, out_dtype=BF16, name="d_w_ple_proj")
    d_wgate = _matmul(hp, dgl, ta=True, out_dtype=BF16, name="d_w_ple_gate")
    dhp = _matmul(dgl, w['w_ple_gate'], tb=True, name="d_ple_normed")
    dx2, d_plenorm = _rmsnorm_bwd(x2, s['ple_norm_w'], dhp, dx3, width=D_MODEL, name="ple_norm_bwd")
    dact = _matmul(dx2, w['w_ffn_down'], tb=True, name="d_ffn_act")
    d_wdown = _matmul(act, dx2, ta=True, out_dtype=BF16, name="d_w_ffn_down")
    zz = emit('p', {'w_ple_proj': _split_cols(d_wproj), 'w_ple_gate': d_wgate.reshape(N_DEV, D_MODEL // N_DEV, D_MODEL),
                    'w_ffn_down': d_wdown.reshape(N_DEV, D_FF // N_DEV, D_MODEL)})
    du, d_fconv_w, d_fconv_b = _conv_act_bwd(u, w['ffn_conv_w'], ffn_conv_b + zz, dact, kw=FFN_CONV, glu=True,
                                             tc=2 * FFN_TC, coff=0, ncols=up_cols, name="ffn_act_bwd")
    zz = zz + relay('p', du)
    tm_u = _tile(D_MODEL, MM_TILE)
    d_wup = _matmul(hf, du, ta=True, out_dtype=BF16, mnk=(D_MODEL, up_cols, t), tn=FFN_TC, name="d_w_ffn_up",
                    o_spec=pl.BlockSpec((1, tm_u, FFN_TC), lambda i, j, kk: (_ffn_perm(j), i, 0)),
                    o_shape=(N_DEV, D_MODEL, FFN_TC))
    zz = zz + emit('r', {'w_ffn_up': d_wup})
    zero_row = jnp.zeros((1, D_MODEL), F32)
    dhf = _matmul(du, w['w_ffn_up'], tb=True, mnk=(t, D_MODEL, up_cols), tk=FFN_TC, name="d_ffn_normed",
                  bias=zero_row + zz,
                  b_spec=pl.BlockSpec((1, tn_o, FFN_TC), lambda i, j, kk: (_ffn_perm(kk), j, 0)))
    zz = zz + relay('r', dhf) + settle('p')
    dx1, d_ffnnorm = _rmsnorm_bwd(x1, s['ffn_norm_w'] + zz, dhf, dx2, width=D_MODEL, name="ffn_norm_bwd")
    dcat = _matmul(dx1, w['w_out'], tb=True, name="d_mixed")
    d_wout = jnp.concatenate([_matmul(y_ssd, dx1, ta=True, out_dtype=BF16, name="d_w_out_ssd"),
                              _matmul(o, dx1, ta=True, out_dtype=BF16, name="d_w_out_mla")], axis=0)
    zz = zz + emit('s', {'w_out': d_wout.reshape(N_DEV, D_MODEL // N_DEV, D_MODEL)})
    ssd_saved = ssd_saved[:3] + (ssd_saved[3] + zz,) + ssd_saved[4:]
    dz, dxbc, d_raw, d_ssdnorm, d_conv_w, d_conv_b, d_dtb, d_alog, d_dskip = _ssd_backward(ssd_saved, dcat)
    zz = zz + relay('s', dz)
    mla_saved = mla_saved[:-1] + (mla_saved[-1] + zz,)
    dq_a, dckv, dkr, d_wq, d_wkv, d_qnorm, d_kvnorm = _mla_backward(mla_saved, dcat)
    d_raw = (d_raw + settle('r')).astype(BF16)
    dproj = jnp.concatenate([dz, dxbc, dq_a, dckv, dkr, d_raw], axis=1)
    d_win = _matmul(h, dproj, ta=True, out_dtype=BF16, name="d_w_in")
    zz = emit('t', {'w_in': _unpad_w_in_shards(d_win), 'w_q_b': _split_cols(_unpad_w_q(d_wq)),
                    'w_kv_b': _split_cols(d_wkv)}) + settle('s')
    dh = _matmul(dproj, w['w_in'], tb=True, bias=zero_row + zz, name="d_in_normed")
    zz = relay('t', dh)
    dx, d_mixnorm = _rmsnorm_bwd(x, s['mix_norm_w'] + zz, dh, dx1, width=D_MODEL, name="mix_norm_bwd")
    conv = {'conv_w': d_conv_w, 'ffn_conv_w': _deinterleave(d_fconv_w)}
    vec = {
        'mix_norm_w': d_mixnorm, 'conv_b': d_conv_b, 'dt_bias': d_dtb, 'a_log': d_alog, 'd_skip': d_dskip,
        'ssd_norm_w': d_ssdnorm, 'q_a_norm_w': d_qnorm, 'kv_a_norm_w': d_kvnorm, 'ffn_norm_w': d_ffnnorm,
        'ffn_conv_b': _deinterleave(d_fconv_b), 'ple_norm_w': d_plenorm, 'b_ple_gate': d_bgate,
        'ple_post_norm_w': d_post, 'final_norm_w': d_final,
    }
    return loss, dx, conv, vec


MESH = pl.DeviceIdType.MESH
FLIPS = ((0, 0, 1), (1, 0, 0), (0, 1, 0), (1, 1, 0), (1, 0, 1), (0, 1, 1), (1, 1, 1))


def _exchange(items, *, gather, name):
    n = len(items)

    def body(*refs):
        ins, outs = refs[:n], refs[n:2 * n]
        send_sems, recv_sems, local_sems = refs[2 * n:]
        x, y, c = lax.axis_index("x"), lax.axis_index("y"), lax.axis_index("c")
        me = 4 * x + 2 * y + c
        peers = [(jnp.where(fx, 1 - x, x), jnp.where(fy, 1 - y, y), jnp.where(fc, 1 - c, c)) for fx, fy, fc in FLIPS]
        slot = [4 * px + 2 * py + pc for px, py, pc in peers]
        local, sends = [], []
        for wi in range(n):
            cp = pltpu.make_async_copy(ins[wi] if gather else ins[wi].at[me], outs[wi].at[me], local_sems.at[wi])
            cp.start()
            local.append(cp)
            for k, peer in enumerate(peers):
                cp = pltpu.make_async_remote_copy(
                    src_ref=ins[wi] if gather else ins[wi].at[slot[k]], dst_ref=outs[wi].at[me],
                    send_sem=send_sems.at[k, wi], recv_sem=recv_sems.at[k, wi], device_id=peer, device_id_type=MESH)
                cp.start()
                sends.append(cp)
        for wi in range(n):
            for k, peer in enumerate(peers):
                pltpu.make_async_remote_copy(
                    src_ref=outs[wi].at[slot[k]], dst_ref=outs[wi].at[slot[k]], send_sem=send_sems.at[k, wi],
                    recv_sem=recv_sems.at[k, wi], device_id=peer, device_id_type=MESH).wait_recv()
        for cp in sends:
            cp.wait_send()
        for cp in local:
            cp.wait()

    hbm = pl.BlockSpec(memory_space=pltpu.HBM)
    out_shape = [jax.ShapeDtypeStruct(((N_DEV,) + v.shape) if gather else v.shape, v.dtype) for v in items]
    return pl.pallas_call(
        body, name=name, in_specs=[hbm] * n, out_specs=[hbm] * n, out_shape=out_shape,
        scratch_shapes=[pltpu.SemaphoreType.DMA((len(FLIPS), n)), pltpu.SemaphoreType.DMA((len(FLIPS), n)),
                        pltpu.SemaphoreType.DMA((n,))],
    )(*items)


HBM_SPEC = pl.BlockSpec(memory_space=pltpu.HBM)
SEM_SPEC = pl.BlockSpec(memory_space=pltpu.SEMAPHORE)
EFFECT = pltpu.SideEffectType.DATAFLOW_SIDE_EFFECTING


def _split_start(bufs, ncopies, plan, *, name):
    nb = len(bufs)

    def body(*refs):
        send_sems, recv_sems, token = refs[nb], refs[nb + 1], refs[2 * nb + 2]
        for i, (src, dst, peer, _) in enumerate(plan(refs[:nb])):
            pltpu.make_async_remote_copy(src_ref=src, dst_ref=dst, send_sem=send_sems.at[i], recv_sem=recv_sems.at[i],
                                         device_id=peer, device_id_type=MESH).start()
        token[...] = jnp.zeros_like(token)

    res = pl.pallas_call(
        body, name=name, in_specs=[HBM_SPEC] * nb,
        out_specs=[SEM_SPEC, SEM_SPEC] + [HBM_SPEC] * nb + [pl.BlockSpec(memory_space=pltpu.VMEM)],
        out_shape=[pltpu.SemaphoreType.DMA((ncopies,)), pltpu.SemaphoreType.DMA((ncopies,))]
        + [pltpu.HBM(v.shape, v.dtype) for v in bufs] + [jax.ShapeDtypeStruct((HALO, LANES), F32)],
        input_output_aliases={i: 2 + i for i in range(nb)},
        compiler_params=pltpu.CompilerParams(has_side_effects=EFFECT),
    )(*[pltpu.with_memory_space_constraint(v, pltpu.HBM) for v in bufs])
    return (res[0], res[1], list(res[2:2 + nb])), res[2 + nb]


def _split_wait(started, after, plan, local_plan, *, name):
    send_sems, recv_sems, bufs = started
    nb = len(bufs)
    nlocal = len(local_plan(bufs))

    def body(*refs):
        send_sems, recv_sems = refs[nb], refs[nb + 1]
        local_sems = refs[2 * nb + 3]
        local = []
        for j, (src, dst) in enumerate(local_plan(refs[:nb])):
            cp = pltpu.make_async_copy(src, dst, local_sems.at[j])
            cp.start()
            local.append(cp)
        for i, (src, _, peer, incoming) in enumerate(plan(refs[:nb])):
            cp = pltpu.make_async_remote_copy(src_ref=src, dst_ref=incoming, send_sem=send_sems.at[i],
                                              recv_sem=recv_sems.at[i], device_id=peer, device_id_type=MESH)
            cp.wait_send()
            cp.wait_recv()
        for cp in local:
            cp.wait()

    res = pl.pallas_call(
        body, name=name, in_specs=[HBM_SPEC] * nb + [SEM_SPEC, SEM_SPEC, pl.BlockSpec(memory_space=pl.ANY)],
        out_specs=[HBM_SPEC] * nb, out_shape=[pltpu.HBM(v.shape, v.dtype) for v in bufs],
        input_output_aliases={i: i for i in range(nb)},
        scratch_shapes=[pltpu.SemaphoreType.DMA((max(nlocal, 1),))],
        compiler_params=pltpu.CompilerParams(has_side_effects=EFFECT),
    )(*bufs, send_sems, recv_sems, after)
    return list(res)


def _place():
    x, y, c = lax.axis_index("x"), lax.axis_index("y"), lax.axis_index("c")
    others = [((1 - x, y, c), 2 * (1 - x) + y), ((x, 1 - y, c), 2 * x + 1 - y), ((1 - x, 1 - y, c), 2 * (1 - x) + 1 - y)]
    return 4 * x + 2 * y + c, 2 * x + y, c, (x, y, 1 - c), others


def _gather1_plan(n):
    def plan(refs):
        me, _, _, sibling, others = _place()
        out = []
        for wi in range(n):
            item, land = refs[wi], refs[n + wi]
            out.append((item, land.at[me], sibling, land.at[me + 1 - 2 * lax.axis_index("c")]))
            for peer, chip in others:
                out.append((item, land.at[me], peer, land.at[2 * chip + lax.axis_index("c")]))
        return out

    return plan


def _gather1_local(n):
    def plan(refs):
        me = _place()[0]
        return [(refs[wi], refs[n + wi].at[me]) for wi in range(n)]

    return plan


def _gather2_plan(n):
    def plan(refs):
        _, _, c, sibling, others = _place()
        out = []
        for wi in range(n):
            land = refs[wi]
            for _, chip in others:
                out.append((land.at[2 * chip + c], land.at[2 * chip + c], sibling, land.at[2 * chip + 1 - c]))
        return out

    return plan


def _gather_start(items, *, name):
    lands = [lax.empty((N_DEV,) + v.shape, v.dtype) for v in items]
    return _split_start(items + lands, 4 * len(items), _gather1_plan(len(items)), name=name)


def _gather_forward(started, after, *, name):
    n = len(started[2]) // 2
    bufs = _split_wait(started, after, _gather1_plan(n), _gather1_local(n), name=name + "_wait")
    return _split_start(bufs[n:], 3 * n, _gather2_plan(n), name=name + "_start")


def _gather_finish(started, after, *, name):
    n = len(started[2])
    return _split_wait(started, after, _gather2_plan(n), lambda refs: [], name=name)


def _handshake(peers):
    barrier = pltpu.get_barrier_semaphore()
    for peer in peers:
        pl.semaphore_signal(barrier, inc=1, device_id=peer, device_id_type=MESH)
    pl.semaphore_wait(barrier, len(peers))


def _remote(src, dst, send_sem, recv_sem, peer):
    return pltpu.make_async_remote_copy(src_ref=src, dst_ref=dst, send_sem=send_sem, recv_sem=recv_sem, device_id=peer,
                                        device_id_type=MESH)


def _sequencer_gather(items, *, collective_id, name):
    n = len(items)
    srcs = [jax.new_ref(v, memory_space=pltpu.MemorySpace.HBM) for v in items]
    lands = [jax.empty_ref(jax.ShapeDtypeStruct((N_DEV,) + v.shape, v.dtype), memory_space=pltpu.MemorySpace.HBM)
             for v in items]
    dma = pltpu.SemaphoreType.DMA

    @pl.kernel(mesh=plsc.ScalarSubcoreMesh(axis_name="sequencer", num_cores=1), name=name,
               scratch_types=(dma((4 * n,)), dma((4 * n,)), dma((3 * n,)), dma((3 * n,)), dma((n,))),
               compiler_params=pltpu.CompilerParams(collective_id=collective_id))
    def launch(send1, recv1, send2, recv2, local_sems):
        _, _, _, sibling, others = _place()
        _handshake([sibling] + [peer for peer, _ in others])
        hop1 = _gather1_plan(n)(srcs + lands)
        hop2 = _gather2_plan(n)(lands)
        local = [pltpu.make_async_copy(src, dst, local_sems.at[j])
                 for j, (src, dst) in enumerate(_gather1_local(n)(srcs + lands))]
        for cp in local:
            cp.start()
        for i, (src, dst, peer, _) in enumerate(hop1):
            _remote(src, dst, send1.at[i], recv1.at[i], peer).start()
        for wi in range(n):
            for j in range(3):
                i1, i2 = 4 * wi + 1 + j, 3 * wi + j
                src, _, peer, incoming = hop1[i1]
                _remote(src, incoming, send1.at[i1], recv1.at[i1], peer).wait_recv()
                src, dst, peer, _ = hop2[i2]
                _remote(src, dst, send2.at[i2], recv2.at[i2], peer).start()
        for wi in range(n):
            src, _, peer, incoming = hop1[4 * wi]
            _remote(src, incoming, send1.at[4 * wi], recv1.at[4 * wi], peer).wait_recv()
        for i, (src, _, peer, incoming) in enumerate(hop2):
            cp = _remote(src, incoming, send2.at[i], recv2.at[i], peer)
            cp.wait_send()
            cp.wait_recv()
        for i, (src, dst, peer, _) in enumerate(hop1):
            _remote(src, dst, send1.at[i], recv1.at[i], peer).wait_send()
        for cp in local:
            cp.wait()

    launch()
    return [land[...] for land in lands]


def _sequencer_exchange(sources, land_shapes, ncopies, plan, local_plan, peers, *, collective_id, name):
    srcs = [jax.new_ref(v, memory_space=pltpu.MemorySpace.HBM) for v in sources]
    lands = [jax.empty_ref(s, memory_space=pltpu.MemorySpace.HBM) for s in land_shapes]
    nlocal = len(local_plan(srcs + lands))
    dma = pltpu.SemaphoreType.DMA

    @pl.kernel(mesh=plsc.ScalarSubcoreMesh(axis_name="sequencer", num_cores=1), name=name,
               scratch_types=(dma((ncopies,)), dma((ncopies,)), dma((max(nlocal, 1),))),
               compiler_params=pltpu.CompilerParams(collective_id=collective_id))
    def launch(send_sems, recv_sems, local_sems):
        _handshake(peers(_place()))
        copies = plan(srcs + lands)
        local = [pltpu.make_async_copy(src, dst, local_sems.at[j])
                 for j, (src, dst) in enumerate(local_plan(srcs + lands))]
        for cp in local:
            cp.start()
        for i, (src, dst, peer, _) in enumerate(copies):
            _remote(src, dst, send_sems.at[i], recv_sems.at[i], peer).start()
        for i, (src, _, peer, incoming) in enumerate(copies):
            cp = _remote(src, incoming, send_sems.at[i], recv_sems.at[i], peer)
            cp.wait_send()
            cp.wait_recv()
        for cp in local:
            cp.wait()

    launch()
    return [land[...] for land in lands]


def _sequencer_scatter_hop2(sums, *, collective_id, name):
    n = len(sums)
    shapes = [jax.ShapeDtypeStruct(v.shape, v.dtype) for v in sums]
    return _sequencer_exchange(sums, shapes, 3 * n, _scatter2_plan(n), _scatter2_local(n),
                               lambda place: [peer for peer, _ in place[4]], collective_id=collective_id, name=name)


N_CHIP = N_DEV // 2


def _scatter1_plan(n):
    def plan(refs):
        _, _, c, sibling, _ = _place()
        out = []
        for wi in range(n):
            parts, half = refs[wi], refs[n + wi]
            for chip in range(N_CHIP):
                out.append((parts.at[2 * chip + 1 - c], half.at[chip], sibling, half.at[chip]))
        return out

    return plan


def _scatter2_plan(n):
    def plan(refs):
        _, my_chip, _, _, others = _place()
        out = []
        for wi in range(n):
            sums, recv = refs[wi], refs[n + wi]
            for peer, chip in others:
                out.append((sums.at[chip], recv.at[my_chip], peer, recv.at[chip]))
        return out

    return plan


def _scatter2_local(n):
    def plan(refs):
        my_chip = _place()[1]
        return [(refs[wi].at[my_chip], refs[n + wi].at[my_chip]) for wi in range(n)]

    return plan


def _pair_add(parts, half, core, *, name):
    _, r, c = parts.shape
    tr = max(d for d in range(HALO, 257, HALO) if r % d == 0) if r > 256 else r
    parts4 = parts.reshape(N_CHIP, 2, r, c)

    def body(core_ref, p_ref, h_ref, o_ref):
        o_ref[...] = (p_ref[:, 0].astype(F32) + h_ref[...].astype(F32)).astype(o_ref.dtype)

    return pl.pallas_call(
        body, name=name,
        grid_spec=pltpu.PrefetchScalarGridSpec(
            num_scalar_prefetch=1, grid=(r // tr,),
            in_specs=[pl.BlockSpec((N_CHIP, 1, tr, c), lambda i, core_ref: (0, core_ref[0], i, 0)),
                      pl.BlockSpec((N_CHIP, tr, c), lambda i, core_ref: (0, i, 0))],
            out_specs=pl.BlockSpec((N_CHIP, tr, c), lambda i, core_ref: (0, i, 0))),
        out_shape=jax.ShapeDtypeStruct((N_CHIP, r, c), parts.dtype), compiler_params=_cp("parallel"),
    )(core, parts4, half)


def _scatter_start(parts, *, name):
    halves = [lax.empty((N_CHIP,) + v.shape[1:], v.dtype) for v in parts]
    return _split_start(parts + halves, N_CHIP * len(parts), _scatter1_plan(len(parts)), name=name)


def _adamw(parts, w, m, v, *, name):
    r, c = w.shape
    nparts = parts.shape[0]
    tr = max(d for d in range(HALO, 129, HALO) if r % d == 0) if r > 128 else r

    def body(p_ref, w_ref, m_ref, v_ref, g_ref, d_ref, mo_ref, vo_ref):
        g = p_ref[0].astype(F32)
        for k in range(1, nparts):
            g = g + p_ref[k].astype(F32)
        mn = ADAM_B1 * m_ref[...] + (1.0 - ADAM_B1) * g
        vn = ADAM_B2 * v_ref[...] + (1.0 - ADAM_B2) * (g * g)
        m_hat = mn / (1.0 - ADAM_B1 ** ADAM_STEP)
        v_hat = vn / (1.0 - ADAM_B2 ** ADAM_STEP)
        g_ref[...] = g
        d_ref[...] = -ADAM_LR * (m_hat / (jnp.sqrt(v_hat) + ADAM_EPS) + ADAM_WD * w_ref[...])
        mo_ref[...] = mn
        vo_ref[...] = vn

    blk = pl.BlockSpec((tr, c), lambda i: (i, 0))
    return pl.pallas_call(
        body, name=name, grid=(r // tr,), in_specs=[pl.BlockSpec((nparts, tr, c), lambda i: (0, i, 0)), blk, blk, blk],
        out_specs=[blk] * 4, out_shape=[jax.ShapeDtypeStruct((r, c), F32)] * 4, compiler_params=_cp("parallel"),
    )(parts, w, m, v)


def _pack_rows(vs, rows):
    lead = vs[0].shape[:-1] if vs[0].ndim > 1 else ()
    flat = jnp.concatenate(vs, axis=-1)
    pad = rows * LANES - flat.shape[-1]
    flat = jnp.pad(flat, [(0, 0)] * len(lead) + [(0, pad)])
    return flat.reshape(lead + (rows, LANES))


def kernel(x, p, positions, mix_norm_w, w_in, conv_w, conv_b, dt_bias, a_log, d_skip, ssd_norm_w, q_a_norm_w, w_q_b, kv_a_norm_w, w_kv_b, w_out, ffn_norm_w, w_ffn_up, ffn_conv_w, ffn_conv_b, w_ffn_down, ple_norm_w, w_ple_gate, b_ple_gate, w_ple_proj, ple_post_norm_w, final_norm_w, loss_target, m_mix_norm_w, m_w_in, m_conv_w, m_conv_b, m_dt_bias, m_a_log, m_d_skip, m_ssd_norm_w, m_q_a_norm_w, m_w_q_b, m_kv_a_norm_w, m_w_kv_b, m_w_out, m_ffn_norm_w, m_w_ffn_up, m_ffn_conv_w, m_ffn_conv_b, m_w_ffn_down, m_ple_norm_w, m_w_ple_gate, m_b_ple_gate, m_w_ple_proj, m_ple_post_norm_w, m_final_norm_w, v_mix_norm_w, v_w_in, v_conv_w, v_conv_b, v_dt_bias, v_a_log, v_d_skip, v_ssd_norm_w, v_q_a_norm_w, v_w_q_b, v_kv_a_norm_w, v_w_kv_b, v_w_out, v_ffn_norm_w, v_w_ffn_up, v_ffn_conv_w, v_ffn_conv_b, v_w_ffn_down, v_ple_norm_w, v_w_ple_gate, v_b_ple_gate, v_w_ple_proj, v_ple_post_norm_w, v_final_norm_w):
    given = dict(locals())
    shapes = {n: given[n].shape for n in WEIGHTS}
    w2 = {n: given[n].reshape(given[n].shape[-2:] if n in BIG or n in CONV else (1, -1)) for n in WEIGHTS}
    m2 = {n: given['m_' + n].reshape(w2[n].shape) for n in WEIGHTS}
    v2 = {n: given['v_' + n].reshape(w2[n].shape) for n in WEIGHTS}
    me = 4 * lax.axis_index("x") + 2 * lax.axis_index("y") + lax.axis_index("c")

    core = lax.axis_index("c").astype(jnp.int32).reshape(1)

    def shards(grp, zero):
        return [(w2[n] + zero).astype(BF16) if n in BIG else w2[n] + zero for n in WEIGHT_GROUPS[grp]]

    first, token = _gather_start(shards('a', 0.0), name="gather_a_hop1")
    first, token = _gather_forward(first, token, name="gather_a_hop2")
    zero = token[0, 0]
    later = _sequencer_gather(shards('b', zero) + shards('c', zero), collective_id=1, name="gather_later")
    later = dict(zip(WEIGHT_GROUPS['b'] + WEIGHT_GROUPS['c'], later))

    def get_w(grp, after):
        if grp == 'a':
            lands = dict(zip(WEIGHT_GROUPS[grp], _gather_finish(first, token, name="gather_a_done")))
        else:
            lands = {n: later[n] for n in WEIGHT_GROUPS[grp]}
        return _assemble_weights(lands)

    scatters = {}

    hop_ids = {grp: 2 + 2 * i for i, grp in enumerate(GRAD_GROUPS)}

    def zero_of(arrays):
        return sum(v[(0,) * v.ndim].astype(F32) * 0.0 for v in arrays)

    def emit(grp, grads):
        scatters[grp], tok = _scatter_start([grads[n] for n in GRAD_GROUPS[grp]], name="scatter_" + grp + "_hop1")
        return tok[0, 0]

    def relay(grp, after):
        n = len(GRAD_GROUPS[grp])
        bufs = _split_wait(scatters[grp], after, _scatter1_plan(n), lambda refs: [], name="scatter_" + grp + "_hop1_wait")
        sums = [_pair_add(bufs[i], bufs[n + i], core, name="scatter_%s_add%d" % (grp, i)) for i in range(n)]
        scatters[grp] = _sequencer_scatter_hop2(sums, collective_id=hop_ids[grp] + 1, name="scatter_" + grp + "_hop2")
        return zero_of(sums)

    out_g, out_d, out_m, out_v = {}, {}, {}, {}

    def settle(grp):
        return zero_of(scatters[grp])

    def update(grp, behind=None):
        for n, parts in zip(GRAD_GROUPS[grp], scatters[grp]):
            wn = w2[n] if behind is None else w2[n] + behind
            out_g[n], out_d[n], out_m[n], out_v[n] = _adamw(parts, wn, m2[n], v2[n], name="adamw_" + n)

    vecs = {n: w2[n] for n in REPL}
    vecs['mix_norm_w'] = vecs['mix_norm_w'] + zero
    loss, dx, g_conv, g_vec = _local_step(x[0], p[0, 0], _rope_tables(positions), get_w, vecs, loss_target[0], emit,
                                          relay, settle)
    n_small = sum(g_vec[n].shape[1] for n in REPL) + sum(g_conv[n].size for n in CONV) + 1
    rows_small = -(-n_small // (LANES * HALO)) * HALO
    small = _pack_rows([g_vec[n] for n in REPL] + [g_conv[n].reshape(1, -1) for n in CONV] + [loss], rows_small)

    for grp in list(GRAD_GROUPS)[:-1]:
        update(grp)
    all_small = _exchange([small], gather=True, name="gather_small_grads")[0].reshape(N_DEV, rows_small * LANES)
    update(list(GRAD_GROUPS)[-1], zero_of([all_small]))
    pieces, off = [], 0
    for n in REPL:
        k = g_vec[n].shape[1]
        pieces.append(all_small[:, off:off + k])
        off += k
    for n in CONV:
        kw, cols = g_conv[n].shape
        full = all_small[:, off:off + kw * cols].reshape(N_DEV, kw, cols)
        mine = lax.dynamic_slice_in_dim(full, me * (cols // N_DEV), cols // N_DEV, axis=2)
        pieces.append(mine.reshape(N_DEV, kw * (cols // N_DEV)))
        off += kw * cols
    pieces.append(all_small[:, off:off + 1])
    small_names = REPL + CONV
    n_mine = sum(q.shape[1] for q in pieces)
    rows_mine = -(-n_mine // (LANES * HALO)) * HALO
    zero = jnp.zeros((1, 1), F32)
    packed = [_pack_rows([src[n].reshape(1, -1) for n in small_names] + [zero], rows_mine).reshape(rows_mine, LANES)
              for src in (w2, m2, v2)]
    sg, sd, sm, sv = _adamw(_pack_rows(pieces, rows_mine), *packed, name="adamw_small")
    off = 0
    for n in small_names:
        k = w2[n].size
        for dst, src in ((out_g, sg), (out_d, sd), (out_m, sm), (out_v, sv)):
            dst[n] = src.reshape(-1)[off:off + k].reshape(w2[n].shape)
        off += k
    total_loss = sg.reshape(-1)[off]

    outs = [total_loss, dx[None]]
    for res in (out_g, out_d, out_m, out_v):
        outs += [res[n].reshape(shapes[n]) for n in WEIGHTS]
    return tuple(outs)
```

```python
import math

import numpy as np
import jax
import jax.numpy as jnp
from jax import lax
from jax.experimental import pallas as pl
from jax.experimental.pallas import tpu as pltpu
from jax.experimental.pallas import tpu_sc as plsc

F32 = jnp.float32
BF16 = jnp.bfloat16
HI = lax.Precision.HIGHEST

D_MODEL = 2048
CHUNK = 64
D_SSM = 1024
SSD_P = 64
SSD_HEADS = 16
SSD_GROUPS = 2
SSD_N = 128
SSD_CONV = 4
SSD_CONV_DIM = D_SSM + 2 * SSD_GROUPS * SSD_N
MLA_HEADS = 8
MLA_NOPE = 128
MLA_ROPE = 64
MLA_V = 128
MLA_Q_RANK = 512
MLA_KV_RANK = 256
MLA_QK_PAD = 256
ROPE_THETA = 10000.0
D_FF = 5632
FFN_CONV = 3
PLE_DIM = 256
NORM_EPS = 1e-6
ADAM_LR, ADAM_B1, ADAM_B2, ADAM_EPS, ADAM_WD, ADAM_STEP = 0.001, 0.9, 0.999, 1e-08, 0.01, 10
N_DEV = 8

OFF_Z, OFF_XBC, OFF_QA, OFF_CKV, OFF_KR, OFF_DT, D_IN_PAD = 0, 1024, 2560, 3072, 3328, 3456, 3584
D_IN = 3408
LANES = 128
HALO = 8
VMEM_LIMIT = 56 * 1024 * 1024
FFN_TC = D_FF * 2 // N_DEV
FFN_PERM = (0, 4, 1, 5, 2, 6, 3, 7)
NEG = -1e30


def _cp(*sem):
    return pltpu.CompilerParams(dimension_semantics=tuple(sem), vmem_limit_bytes=VMEM_LIMIT)


def _tile(n, want):
    if n <= want:
        return n
    best = max(d for d in range(LANES, want + 1, LANES) if n % d == 0)
    return best


def _sigmoid(x):
    return 0.5 * (jnp.tanh(0.5 * x) + 1.0)


def _silu(x):
    return x * _sigmoid(x)


def _dsilu(x):
    s = _sigmoid(x)
    return s * (1.0 + x * (1.0 - s))


MM_TILE = 1408
MM_TK = 2816


def _matmul(a, b, *, ta=False, tb=False, out_dtype=F32, add=None, bias=None, tm=MM_TILE, tn=MM_TILE, tk=MM_TK, name,
            mnk=None, a_spec=None, b_spec=None, o_spec=None, o_shape=None):
    if mnk is None:
        m, k = (a.shape[1], a.shape[0]) if ta else a.shape
        n = b.shape[0] if tb else b.shape[1]
        assert k == (b.shape[1] if tb else b.shape[0])
    else:
        m, n, k = mnk
    tm, tn, tk = _tile(m, tm), _tile(n, tn), _tile(k, tk)
    nk = k // tk
    dims = (((0 if ta else 1,), (1 if tb else 0,)), ((), ()))

    def body(*refs):
        a_ref, b_ref = refs[0], refs[1]
        pos = 2
        add_ref = bias_ref = None
        if add is not None:
            add_ref = refs[pos]
            pos += 1
        if bias is not None:
            bias_ref = refs[pos]
            pos += 1
        o_ref = refs[pos]
        kk = pl.program_id(2)
        av = a_ref[...]
        bv = b_ref[...]
        av = av.reshape(av.shape[-2:]).astype(BF16)
        bv = bv.reshape(bv.shape[-2:]).astype(BF16)
        prod = lax.dot_general(av, bv, dims, preferred_element_type=F32)

        def finish(r):
            if bias_ref is not None:
                r = r + bias_ref[...]
            if add_ref is not None:
                r = r + add_ref[...].astype(F32)
            o_ref[...] = r.astype(out_dtype).reshape(o_ref.shape)

        if nk == 1:
            finish(prod)
        else:
            acc_ref = refs[pos + 1]

            @pl.when(kk == 0)
            def _():
                acc_ref[...] = prod

            @pl.when(kk > 0)
            def _():
                acc_ref[...] += prod

            @pl.when(kk == nk - 1)
            def _():
                finish(acc_ref[...])

    if a_spec is None:
        a_spec = (pl.BlockSpec((tk, tm), lambda i, j, kk: (kk, i)) if ta
                  else pl.BlockSpec((tm, tk), lambda i, j, kk: (i, kk)))
    if b_spec is None:
        b_spec = (pl.BlockSpec((tn, tk), lambda i, j, kk: (j, kk)) if tb
                  else pl.BlockSpec((tk, tn), lambda i, j, kk: (kk, j)))
    if o_spec is None:
        o_spec = pl.BlockSpec((tm, tn), lambda i, j, kk: (i, j))
    if o_shape is None:
        o_shape = (m, n)
    in_specs = [a_spec, b_spec]
    args = [a, b]
    if add is not None:
        in_specs.append(pl.BlockSpec((tm, tn), lambda i, j, kk: (i, j)))
        args.append(add)
    if bias is not None:
        in_specs.append(pl.BlockSpec((1, tn), lambda i, j, kk: (0, j)))
        args.append(bias)
    return pl.pallas_call(
        body, name=name, grid=(m // tm, n // tn, nk), in_specs=in_specs, out_specs=o_spec,
        out_shape=jax.ShapeDtypeStruct(o_shape, out_dtype),
        scratch_shapes=[pltpu.VMEM((tm, tn), F32)] if nk > 1 else [],
        compiler_params=_cp("parallel", "parallel", "arbitrary"),
    )(*args)


def _rmsnorm_fwd(x, w, *, width, cblk=0, out_dtype=BF16, tr=256, name):
    t = x.shape[0]

    def body(x_ref, w_ref, o_ref):
        xv = x_ref[...].astype(F32)
        r = lax.rsqrt(jnp.mean(xv * xv, axis=-1, keepdims=True) + NORM_EPS)
        o_ref[...] = (xv * r * w_ref[...]).astype(out_dtype)

    return pl.pallas_call(
        body, name=name, grid=(t // tr,),
        in_specs=[pl.BlockSpec((tr, width), lambda i: (i, cblk)), pl.BlockSpec((1, width), lambda i: (0, 0))],
        out_specs=pl.BlockSpec((tr, width), lambda i: (i, 0)),
        out_shape=jax.ShapeDtypeStruct((t, width), out_dtype),
        compiler_params=_cp("parallel"),
    )(x, w)


def _rmsnorm_bwd(x, w, dy, add=None, *, width, cblk=0, out_dtype=F32, tr=256, name):
    t = x.shape[0]

    def body(*refs):
        if add is None:
            x_ref, w_ref, dy_ref, dx_ref, dw_ref = refs
            add_ref = None
        else:
            x_ref, w_ref, dy_ref, add_ref, dx_ref, dw_ref = refs
        xv = x_ref[...].astype(F32)
        dyv = dy_ref[...].astype(F32)
        r = lax.rsqrt(jnp.mean(xv * xv, axis=-1, keepdims=True) + NORM_EPS)
        xh = xv * r
        g = dyv * w_ref[...]
        dx = r * (g - xh * jnp.mean(g * xh, axis=-1, keepdims=True))
        if add_ref is not None:
            dx = dx + add_ref[...].astype(F32)
        dx_ref[...] = dx.astype(out_dtype)

        @pl.when(pl.program_id(0) == 0)
        def _():
            dw_ref[...] = jnp.zeros_like(dw_ref)

        dw_ref[...] += jnp.sum(dyv * xh, axis=0, keepdims=True)

    in_specs = [pl.BlockSpec((tr, width), lambda i: (i, cblk)), pl.BlockSpec((1, width), lambda i: (0, 0)),
                pl.BlockSpec((tr, width), lambda i: (i, 0))]
    args = [x, w, dy]
    if add is not None:
        in_specs.append(pl.BlockSpec((tr, width), lambda i: (i, 0)))
        args.append(add)
    return pl.pallas_call(
        body, name=name, grid=(t // tr,), in_specs=in_specs,
        out_specs=[pl.BlockSpec((tr, width), lambda i: (i, 0)), pl.BlockSpec((1, width), lambda i: (0, 0))],
        out_shape=[jax.ShapeDtypeStruct((t, width), out_dtype), jax.ShapeDtypeStruct((1, width), F32)],
        compiler_params=_cp("arbitrary"),
    )(*args)


def _shift_down(prev_halo, cur, j):
    if j == 0:
        return cur
    ext = jnp.concatenate([prev_halo, cur], axis=0)
    return pltpu.roll(ext, j, axis=0)[HALO:]


def _shift_up(cur, next_halo, j):
    if j == 0:
        return cur
    ext = jnp.concatenate([cur, next_halo], axis=0)
    return pltpu.roll(ext, ext.shape[0] - j, axis=0)[:cur.shape[0]]


def _conv_rows(prev, cur, w, b, kw):
    shifted = [cur]
    out = b + w[kw - 1:kw] * cur
    for j in range(1, kw):
        sh = _shift_down(prev, cur, j)
        shifted.append(sh)
        out = out + w[kw - 1 - j:kw - j] * sh
    return out, shifted


def _act_fwd(c, glu):
    if glu:
        half = c.shape[1] // 2
        return _silu(c[:, :half]) * c[:, half:]
    return _silu(c)


def _act_bwd(c, dout, glu):
    if glu:
        half = c.shape[1] // 2
        g, up = c[:, :half], c[:, half:]
        s = _sigmoid(g)
        gs = g * s
        return jnp.concatenate([dout * up * (s + gs * (1.0 - s)), dout * gs], axis=1)
    return dout * _dsilu(c)


def _conv_act_fwd(u, w, b, *, kw, glu, tc, coff, ncols, out_dtype, tr=256, name):
    t = u.shape[0]
    nb = ncols // tc
    oc = tc // 2 if glu else tc

    def body(u_ref, uh_ref, w_ref, b_ref, o_ref):
        prev = jnp.where(pl.program_id(0) == 0, 0.0, uh_ref[...])
        c, _ = _conv_rows(prev, u_ref[...], w_ref[...], b_ref[...], kw)
        o_ref[...] = _act_fwd(c, glu).astype(out_dtype)

    return pl.pallas_call(
        body, name=name, grid=(t // tr, nb),
        in_specs=[pl.BlockSpec((tr, tc), lambda i, j: (i, j + coff)),
                  pl.BlockSpec((HALO, tc), lambda i, j: (jnp.maximum(i * (tr // HALO) - 1, 0), j + coff)),
                  pl.BlockSpec((kw, tc), lambda i, j: (0, j)), pl.BlockSpec((1, tc), lambda i, j: (0, j))],
        out_specs=pl.BlockSpec((tr, oc), lambda i, j: (i, j)),
        out_shape=jax.ShapeDtypeStruct((t, nb * oc), out_dtype),
        compiler_params=_cp("parallel", "parallel"),
    )(u, u, w, b)


def _conv_act_bwd(u, w, b, dout, *, kw, glu, tc, coff, ncols, tr=256, name):
    t = u.shape[0]
    nb = ncols // tc
    nt = t // tr
    oc = tc // 2 if glu else tc

    def body(u_ref, up_ref, un_ref, d_ref, dn_ref, w_ref, b_ref, du_ref, dw_ref, db_ref):
        i = pl.program_id(1)
        cur, nxt, wv, bv = u_ref[...], un_ref[...], w_ref[...], b_ref[...]
        prev = jnp.where(i == 0, 0.0, up_ref[...])
        c_cur, shifted = _conv_rows(prev, cur, wv, bv, kw)
        c_nxt, _ = _conv_rows(cur[tr - HALO:], nxt, wv, bv, kw)
        d_cur = _act_bwd(c_cur, d_ref[...].astype(F32), glu)
        d_nxt = _act_bwd(c_nxt, jnp.where(i == nt - 1, 0.0, dn_ref[...].astype(F32)), glu)
        du = wv[kw - 1:kw] * d_cur
        for j in range(1, kw):
            du = du + wv[kw - 1 - j:kw - j] * _shift_up(d_cur, d_nxt, j)
        du_ref[...] = du.astype(BF16)

        @pl.when(i == 0)
        def _():
            dw_ref[...] = jnp.zeros_like(dw_ref)
            db_ref[...] = jnp.zeros_like(db_ref)

        db_ref[...] += jnp.sum(d_cur, axis=0, keepdims=True)
        dw_ref[...] += jnp.concatenate(
            [jnp.sum(d_cur * shifted[kw - 1 - k], axis=0, keepdims=True) for k in range(kw)], axis=0)

    nh = tr // HALO
    return pl.pallas_call(
        body, name=name, grid=(nb, nt),
        in_specs=[pl.BlockSpec((tr, tc), lambda j, i: (i, j + coff)),
                  pl.BlockSpec((HALO, tc), lambda j, i: (jnp.maximum(i * nh - 1, 0), j + coff)),
                  pl.BlockSpec((HALO, tc), lambda j, i: (jnp.minimum((i + 1) * nh, t // HALO - 1), j + coff)),
                  pl.BlockSpec((tr, oc), lambda j, i: (i, j)),
                  pl.BlockSpec((HALO, oc), lambda j, i: (jnp.minimum((i + 1) * nh, t // HALO - 1), j)),
                  pl.BlockSpec((kw, tc), lambda j, i: (0, j)), pl.BlockSpec((1, tc), lambda j, i: (0, j))],
        out_specs=[pl.BlockSpec((tr, tc), lambda j, i: (i, j)), pl.BlockSpec((kw, tc), lambda j, i: (0, j)),
                   pl.BlockSpec((1, tc), lambda j, i: (0, j))],
        out_shape=[jax.ShapeDtypeStruct((t, ncols), BF16), jax.ShapeDtypeStruct((kw, ncols), F32),
                   jax.ShapeDtypeStruct((1, ncols), F32)],
        compiler_params=_cp("parallel", "arbitrary"),
    )(u, u, u, dout, dout, w, b)


def _ple_fwd(x2, gl, pe, pw, *, tr=256, name):
    t, d = x2.shape

    def body(x_ref, gl_ref, pe_ref, pw_ref, o_ref):
        pv = pe_ref[...]
        r = lax.rsqrt(jnp.mean(pv * pv, axis=-1, keepdims=True) + NORM_EPS)
        o_ref[...] = x_ref[...] + _sigmoid(gl_ref[...]) * (pv * r * pw_ref[...])

    blk = pl.BlockSpec((tr, d), lambda i: (i, 0))
    return pl.pallas_call(
        body, name=name, grid=(t // tr,), in_specs=[blk, blk, blk, pl.BlockSpec((1, d), lambda i: (0, 0))],
        out_specs=blk, out_shape=jax.ShapeDtypeStruct((t, d), F32), compiler_params=_cp("parallel"),
    )(x2, gl, pe, pw)


def _ple_bwd(dx3, gl, pe, pw, *, tr=256, name):
    t, d = dx3.shape

    def body(dx_ref, gl_ref, pe_ref, pw_ref, dgl_ref, db_ref, dpe_ref, dpw_ref):
        dx, pv, pwv = dx_ref[...], pe_ref[...], pw_ref[...]
        gate = _sigmoid(gl_ref[...])
        r = lax.rsqrt(jnp.mean(pv * pv, axis=-1, keepdims=True) + NORM_EPS)
        ph = pv * r
        dgl = dx * (ph * pwv) * gate * (1.0 - gate)
        de = dx * gate
        g = de * pwv
        dgl_ref[...] = dgl.astype(BF16)
        dpe_ref[...] = (r * (g - ph * jnp.mean(g * ph, axis=-1, keepdims=True))).astype(BF16)

        @pl.when(pl.program_id(0) == 0)
        def _():
            db_ref[...] = jnp.zeros_like(db_ref)
            dpw_ref[...] = jnp.zeros_like(dpw_ref)

        db_ref[...] += jnp.sum(dgl, axis=0, keepdims=True)
        dpw_ref[...] += jnp.sum(de * ph, axis=0, keepdims=True)

    blk = pl.BlockSpec((tr, d), lambda i: (i, 0))
    row = pl.BlockSpec((1, d), lambda i: (0, 0))
    return pl.pallas_call(
        body, name=name, grid=(t // tr,), in_specs=[blk, blk, blk, row], out_specs=[blk, row, blk, row],
        out_shape=[jax.ShapeDtypeStruct((t, d), BF16), jax.ShapeDtypeStruct((1, d), F32),
                   jax.ShapeDtypeStruct((t, d), BF16), jax.ShapeDtypeStruct((1, d), F32)],
        compiler_params=_cp("arbitrary"),
    )(dx3, gl, pe, pw)


def _loss_head(x3, fw, target, *, tr=256, name):
    t, d = x3.shape

    def body(x_ref, w_ref, t_ref, l_ref, dx_ref, dw_ref):
        xv, wv = x_ref[...], w_ref[...]
        r = lax.rsqrt(jnp.mean(xv * xv, axis=-1, keepdims=True) + NORM_EPS)
        xh = xv * r
        err = xh * wv - t_ref[...]
        dy = err * (1.0 / d)
        g = dy * wv
        dx_ref[...] = r * (g - xh * jnp.mean(g * xh, axis=-1, keepdims=True))

        @pl.when(pl.program_id(0) == 0)
        def _():
            l_ref[...] = jnp.zeros_like(l_ref)
            dw_ref[...] = jnp.zeros_like(dw_ref)

        l_ref[...] += 0.5 * jnp.sum(jnp.mean(err * err, axis=-1, keepdims=True), axis=0, keepdims=True)
        dw_ref[...] += jnp.sum(dy * xh, axis=0, keepdims=True)

    blk = pl.BlockSpec((tr, d), lambda i: (i, 0))
    row = pl.BlockSpec((1, d), lambda i: (0, 0))
    return pl.pallas_call(
        body, name=name, grid=(t // tr,), in_specs=[blk, row, blk],
        out_specs=[pl.BlockSpec((1, 1), lambda i: (0, 0)), blk, row],
        out_shape=[jax.ShapeDtypeStruct((1, 1), F32), jax.ShapeDtypeStruct((t, d), F32),
                   jax.ShapeDtypeStruct((1, d), F32)],
        compiler_params=_cp("arbitrary"),
    )(x3, fw, target)


def _rope(blk, tab_ref):
    return blk * tab_ref[0] + pltpu.roll(blk, 96, axis=1) * tab_ref[1] + pltpu.roll(blk, 32, axis=1) * tab_ref[2]


def _unrope(g, tab_ref):
    return g * tab_ref[0] + pltpu.roll(g * tab_ref[1], 32, axis=1) + pltpu.roll(g * tab_ref[2], 96, axis=1)


def _mla_prep(q, kv, proj, tabs, *, tr=512, name):
    t = q.shape[0]

    def body(q_ref, kv_ref, kr_ref, tab_ref, qo_ref, ko_ref, vo_ref, vt_ref):
        qv, kvv = q_ref[...], kv_ref[...]
        qo_ref[0, :, :MLA_NOPE] = qv[:, :MLA_NOPE].astype(BF16)
        qo_ref[0, :, MLA_NOPE:] = _rope(qv[:, MLA_NOPE:], tab_ref).astype(BF16)
        ko_ref[0, :, :MLA_NOPE] = kvv[:, :MLA_NOPE].astype(BF16)
        ko_ref[0, :, MLA_NOPE:] = _rope(kr_ref[...], tab_ref).astype(BF16)
        vo_ref[0] = kvv[:, MLA_NOPE:].astype(BF16)
        for blk in range(tr // ATT_BLK):
            vt_ref[0, blk] = kvv[blk * ATT_BLK:(blk + 1) * ATT_BLK, MLA_NOPE:].T.astype(BF16)

    return pl.pallas_call(
        body, name=name, grid=(t // tr, MLA_HEADS),
        in_specs=[pl.BlockSpec((tr, MLA_QK_PAD), lambda i, h: (i, h)),
                  pl.BlockSpec((tr, MLA_NOPE + MLA_V), lambda i, h: (i, h)),
                  pl.BlockSpec((tr, LANES), lambda i, h: (i, OFF_KR // LANES)),
                  pl.BlockSpec((3, tr, LANES), lambda i, h: (0, i, 0))],
        out_specs=[pl.BlockSpec((1, tr, MLA_QK_PAD), lambda i, h: (h, i, 0)),
                   pl.BlockSpec((1, tr, MLA_QK_PAD), lambda i, h: (h, i, 0)),
                   pl.BlockSpec((1, tr, MLA_V), lambda i, h: (h, i, 0)),
                   pl.BlockSpec((1, tr // ATT_BLK, MLA_V, ATT_BLK), lambda i, h: (h, i, 0, 0))],
        out_shape=[jax.ShapeDtypeStruct((MLA_HEADS, t, MLA_QK_PAD), BF16),
                   jax.ShapeDtypeStruct((MLA_HEADS, t, MLA_QK_PAD), BF16),
                   jax.ShapeDtypeStruct((MLA_HEADS, t, MLA_V), BF16),
                   jax.ShapeDtypeStruct((MLA_HEADS, t // ATT_BLK, MLA_V, ATT_BLK), BF16)],
        compiler_params=_cp("parallel", "parallel"),
    )(q, kv, proj, tabs)


def _mla_unprep(dq3, dk3, dv3, tabs, *, tr=256, name):
    t = dq3.shape[1]

    def body(dq_ref, dk_ref, dv_ref, tab_ref, qo_ref, kvo_ref, kro_ref):
        kr = jnp.zeros((tr, LANES), F32)
        for h in range(MLA_HEADS):
            c0 = h * MLA_QK_PAD
            qo_ref[:, c0:c0 + MLA_NOPE] = dq_ref[h, :, :MLA_NOPE].astype(BF16)
            qo_ref[:, c0 + MLA_NOPE:c0 + MLA_QK_PAD] = _unrope(dq_ref[h, :, MLA_NOPE:], tab_ref).astype(BF16)
            kvo_ref[:, c0:c0 + MLA_NOPE] = dk_ref[h, :, :MLA_NOPE].astype(BF16)
            kvo_ref[:, c0 + MLA_NOPE:c0 + MLA_QK_PAD] = dv_ref[h].astype(BF16)
            kr = kr + dk_ref[h, :, MLA_NOPE:]
        kro_ref[...] = _unrope(kr, tab_ref).astype(BF16)

    return pl.pallas_call(
        body, name=name, grid=(t // tr,),
        in_specs=[pl.BlockSpec((MLA_HEADS, tr, MLA_QK_PAD), lambda i: (0, i, 0)),
                  pl.BlockSpec((MLA_HEADS, tr, MLA_QK_PAD), lambda i: (0, i, 0)),
                  pl.BlockSpec((MLA_HEADS, tr, MLA_V), lambda i: (0, i, 0)),
                  pl.BlockSpec((3, tr, LANES), lambda i: (0, i, 0))],
        out_specs=[pl.BlockSpec((tr, MLA_HEADS * MLA_QK_PAD), lambda i: (i, 0)),
                   pl.BlockSpec((tr, MLA_HEADS * MLA_QK_PAD), lambda i: (i, 0)),
                   pl.BlockSpec((tr, LANES), lambda i: (i, 0))],
        out_shape=[jax.ShapeDtypeStruct((t, MLA_HEADS * MLA_QK_PAD), BF16),
                   jax.ShapeDtypeStruct((t, MLA_HEADS * MLA_QK_PAD), BF16),
                   jax.ShapeDtypeStruct((t, LANES), BF16)],
        compiler_params=_cp("parallel"),
    )(dq3, dk3, dv3, tabs)


ATT_BLK = 512
ATT_SCALE = 1.0 / math.sqrt(MLA_NOPE + MLA_ROPE)
_NT = (((1,), (1,)), ((), ()))
_TN = (((0,), (0,)), ((), ()))


def _att_scores_t(k, q, diagonal):
    s = lax.dot_general(k, q, _NT, preferred_element_type=F32) * ATT_SCALE
    if not diagonal:
        return s
    key = lax.broadcasted_iota(jnp.int32, s.shape, 0)
    query = lax.broadcasted_iota(jnp.int32, s.shape, 1)
    return jnp.where((key >> 6) <= (query >> 6), s, NEG)


def _att_rows(i):
    return pl.ds(pl.multiple_of(i * ATT_BLK, ATT_BLK), ATT_BLK)


ATT_HEADS = 2


def _attn_fwd(q3, k3, vt4, *, name):
    t = q3.shape[1]
    nq = t // ATT_BLK

    def body(q_ref, k_ref, vt_ref, o_ref, lse_ref):
        qi = pl.program_id(1)
        qs = [q_ref[hh] for hh in range(ATT_HEADS)]

        def step(j, carry, diagonal=False):
            out = []
            for hh, (m, l, acc) in enumerate(carry):
                s = _att_scores_t(k_ref[hh, _att_rows(j), :], qs[hh], diagonal)
                m_new = jnp.maximum(m, jnp.max(s, axis=0, keepdims=True))
                p = jnp.exp(s - m_new)
                alpha = jnp.exp(m - m_new)
                l = alpha * l + jnp.sum(p, axis=0, keepdims=True)
                acc = alpha * acc + jnp.dot(vt_ref[hh, j], p.astype(BF16), preferred_element_type=F32)
                out.append((m_new, l, acc))
            return tuple(out)

        init = tuple((jnp.full((1, ATT_BLK), NEG, F32), jnp.zeros((1, ATT_BLK), F32),
                      jnp.zeros((MLA_V, ATT_BLK), F32)) for _ in range(ATT_HEADS))
        done = step(qi, lax.fori_loop(0, qi, step, init), diagonal=True)
        for hh, (m, l, acc) in enumerate(done):
            o_ref[:, hh * MLA_V:(hh + 1) * MLA_V] = (acc / l).T
            lse_ref[hh, 0] = m + jnp.log(l)

    return pl.pallas_call(
        body, name=name, grid=(MLA_HEADS // ATT_HEADS, nq),
        in_specs=[pl.BlockSpec((ATT_HEADS, ATT_BLK, MLA_QK_PAD), lambda h, i: (h, i, 0)),
                  pl.BlockSpec((ATT_HEADS, t, MLA_QK_PAD), lambda h, i: (h, 0, 0)),
                  pl.BlockSpec((ATT_HEADS, nq, MLA_V, ATT_BLK), lambda h, i: (h, 0, 0, 0))],
        out_specs=[pl.BlockSpec((ATT_BLK, ATT_HEADS * MLA_V), lambda h, i: (i, h)),
                   pl.BlockSpec((ATT_HEADS, 1, 1, ATT_BLK), lambda h, i: (h, i, 0, 0))],
        out_shape=[jax.ShapeDtypeStruct((t, MLA_HEADS * MLA_V), F32),
                   jax.ShapeDtypeStruct((MLA_HEADS, nq, 1, ATT_BLK), F32)],
        compiler_params=_cp("parallel", "parallel"),
    )(q3, k3, vt4)


def _attn_bwd(q3, k3, v3, o, dcat, lse, *, name):
    t = q3.shape[1]
    nq = t // ATT_BLK
    wide = ATT_HEADS * MLA_V

    def body(q_ref, k_ref, v_ref, o_ref, do_ref, lse_ref, dq_ref, dk_ref, dv_ref, delta_ref):
        kj = pl.program_id(1)

        @pl.when(kj == 0)
        def _():
            dq_ref[...] = jnp.zeros_like(dq_ref)
            ones = jnp.ones((HALO, MLA_V), F32)
            for i in range(nq):
                rows = pl.ds(i * ATT_BLK, ATT_BLK)
                prod = o_ref[rows, :] * do_ref[rows, :]
                for hh in range(ATT_HEADS):
                    delta_ref[hh, i] = lax.dot_general(ones, prod[:, hh * MLA_V:(hh + 1) * MLA_V], _NT, precision=HI,
                                                       preferred_element_type=F32)

        def step(i, carry, diagonal=False):
            rows = _att_rows(i)
            out = []
            for hh, (dk, dv) in enumerate(carry):
                k, v = k_ref[hh], v_ref[hh]
                q = q_ref[hh, rows, :]
                dob = do_ref[rows, hh * MLA_V:(hh + 1) * MLA_V].astype(BF16)
                p = jnp.exp(_att_scores_t(k, q, diagonal) - lse_ref[hh, i])
                dv = dv + jnp.dot(p.astype(BF16), dob, preferred_element_type=F32)
                dp = lax.dot_general(v, dob, _NT, preferred_element_type=F32)
                ds = (p * (dp - delta_ref[hh, i, 0:1, :]) * ATT_SCALE).astype(BF16)
                dk = dk + jnp.dot(ds, q, preferred_element_type=F32)
                dq_ref[hh, rows, :] += lax.dot_general(ds, k, _TN, preferred_element_type=F32)
                out.append((dk, dv))
            return tuple(out)

        init = tuple((jnp.zeros((ATT_BLK, MLA_QK_PAD), F32), jnp.zeros((ATT_BLK, MLA_V), F32))
                     for _ in range(ATT_HEADS))
        done = lax.fori_loop(kj + 1, nq, step, step(kj, init, diagonal=True))
        for hh, (dk, dv) in enumerate(done):
            dk_ref[hh] = dk
            dv_ref[hh] = dv

    return pl.pallas_call(
        body, name=name, grid=(MLA_HEADS // ATT_HEADS, nq),
        in_specs=[pl.BlockSpec((ATT_HEADS, t, MLA_QK_PAD), lambda h, j: (h, 0, 0)),
                  pl.BlockSpec((ATT_HEADS, ATT_BLK, MLA_QK_PAD), lambda h, j: (h, j, 0)),
                  pl.BlockSpec((ATT_HEADS, ATT_BLK, MLA_V), lambda h, j: (h, j, 0)),
                  pl.BlockSpec((t, wide), lambda h, j: (0, h)),
                  pl.BlockSpec((t, wide), lambda h, j: (0, MLA_HEADS // ATT_HEADS + h)),
                  pl.BlockSpec((ATT_HEADS, nq, 1, ATT_BLK), lambda h, j: (h, 0, 0, 0))],
        out_specs=[pl.BlockSpec((ATT_HEADS, t, MLA_QK_PAD), lambda h, j: (h, 0, 0)),
                   pl.BlockSpec((ATT_HEADS, ATT_BLK, MLA_QK_PAD), lambda h, j: (h, j, 0)),
                   pl.BlockSpec((ATT_HEADS, ATT_BLK, MLA_V), lambda h, j: (h, j, 0))],
        out_shape=[jax.ShapeDtypeStruct((MLA_HEADS, t, MLA_QK_PAD), F32),
                   jax.ShapeDtypeStruct((MLA_HEADS, t, MLA_QK_PAD), F32),
                   jax.ShapeDtypeStruct((MLA_HEADS, t, MLA_V), F32)],
        scratch_shapes=[pltpu.VMEM((ATT_HEADS, nq, HALO, ATT_BLK), F32)],
        compiler_params=_cp("parallel", "arbitrary"),
    )(q3, k3, v3, o, dcat, lse)


def _ssd_prep(proj, bias128, alog128, *, name):
    t = proj.shape[0]
    nc = t // CHUNK

    def body(raw_ref, b_ref, al_ref, dt_ref, cs_ref, a_ref):
        xv = raw_ref[...] + b_ref[...]
        dt = jnp.maximum(xv, 0.0) + jnp.log(1.0 + jnp.exp(-jnp.abs(xv)))
        a = -jnp.exp(al_ref[...])
        adt = (dt * a).reshape(nc, CHUNK, LANES)
        li = lax.broadcasted_iota(jnp.int32, (nc, CHUNK, CHUNK), 1)
        si = lax.broadcasted_iota(jnp.int32, (nc, CHUNK, CHUNK), 2)
        tril = jnp.where(si <= li, 1.0, 0.0).astype(F32)
        cs = lax.dot_general(tril, adt, (((2,), (1,)), ((0,), (0,))), precision=HI, preferred_element_type=F32)
        dt_ref[...] = dt
        cs_ref[...] = cs.reshape(t, LANES)
        a_ref[...] = a

    blk = pl.BlockSpec((t, LANES), lambda i: (0, 0))
    row = pl.BlockSpec((1, LANES), lambda i: (0, 0))
    return pl.pallas_call(
        body, name=name, grid=(1,),
        in_specs=[pl.BlockSpec((t, LANES), lambda i: (0, OFF_DT // LANES)), row, row],
        out_specs=[blk, blk, row],
        out_shape=[jax.ShapeDtypeStruct((t, LANES), F32), jax.ShapeDtypeStruct((t, LANES), F32),
                   jax.ShapeDtypeStruct((1, LANES), F32)],
        compiler_params=_cp("arbitrary"),
    )(proj, bias128, alog128)


def _ssd_prep_bwd(ddt128, dadt128, proj, bias128, dt128, a128, dd_h, *, name):
    t = proj.shape[0]

    def body(ddt_ref, dadt_ref, raw_ref, b_ref, dt_ref, a_ref, dd_ref, draw_ref, db_ref, dal_ref, dds_ref):
        draw = ddt_ref[...] * _sigmoid(raw_ref[...] + b_ref[...])
        draw_ref[...] = draw.astype(BF16)
        db_ref[...] = jnp.sum(draw, axis=0, keepdims=True)
        dal_ref[...] = jnp.sum(dadt_ref[...] * dt_ref[...], axis=0, keepdims=True) * a_ref[...]
        dds_ref[...] = jnp.sum(dd_ref[...], axis=-1, keepdims=True)

    blk = pl.BlockSpec((t, LANES), lambda i: (0, 0))
    row = pl.BlockSpec((1, LANES), lambda i: (0, 0))
    return pl.pallas_call(
        body, name=name, grid=(1,),
        in_specs=[blk, blk, pl.BlockSpec((t, LANES), lambda i: (0, OFF_DT // LANES)), row, blk, row,
                  pl.BlockSpec((SSD_HEADS, SSD_P), lambda i: (0, 0))],
        out_specs=[blk, row, row, pl.BlockSpec((SSD_HEADS, 1), lambda i: (0, 0))],
        out_shape=[jax.ShapeDtypeStruct((t, LANES), BF16), jax.ShapeDtypeStruct((1, LANES), F32),
                   jax.ShapeDtypeStruct((1, LANES), F32), jax.ShapeDtypeStruct((SSD_HEADS, 1), F32)],
        compiler_params=_cp("arbitrary"),
    )(ddt128, dadt128, proj, bias128, dt128, a128, dd_h)


def _bdot(a, b, ca, cb, precision=None):
    return lax.dot_general(a, b, (((ca,), (cb,)), ((0,), (0,))), precision=precision, preferred_element_type=F32)


def _head_matrices():
    eye, zero = jnp.eye(SSD_P, dtype=F32), jnp.zeros((SSD_P, SSD_P), F32)
    pick = jnp.stack([jnp.concatenate([eye, zero], axis=0), jnp.concatenate([zero, eye], axis=0)])
    return pick, pick.transpose(0, 2, 1)


def _move(x, sel):
    selb = sel.astype(BF16)
    hi = x.astype(BF16)
    rest = x - hi.astype(F32)
    mid = rest.astype(BF16)
    low = (rest - mid.astype(F32)).astype(BF16)
    out = jnp.dot(hi, selb, preferred_element_type=F32)
    out = out + jnp.dot(mid, selb, preferred_element_type=F32)
    return out + jnp.dot(low, selb, preferred_element_type=F32)


def _pick_head(pair_ref, pick_ref, h):
    return _move(pair_ref[...], pick_ref[h % 2])


def _place_head(out_ref, val, place_ref, h):
    wide = _move(val, place_ref[h % 2])

    @pl.when(h % 2 == 0)
    def _():
        out_ref[...] = wide

    @pl.when(h % 2 == 1)
    def _():
        out_ref[...] += wide


def _ssd_common(x2, dt_ref, cs_ref, csr_ref, b_ref, c_ref, nc):
    x = x2.reshape(nc, CHUNK, SSD_P)
    dt = dt_ref[0].reshape(nc, CHUNK, SSD_P)
    cs = cs_ref[0].reshape(nc, CHUNK, SSD_P)
    csr = csr_ref[0]
    bm = b_ref[...].reshape(nc, CHUNK, SSD_N).astype(BF16)
    cm = c_ref[...].reshape(nc, CHUNK, SSD_N).astype(BF16)
    li = lax.broadcasted_iota(jnp.int32, (nc, CHUNK, CHUNK), 1)
    si = lax.broadcasted_iota(jnp.int32, (nc, CHUNK, CHUNK), 2)
    lmat = jnp.exp(jnp.where(si <= li, cs - csr, NEG))
    g = _bdot(cm, bm, 2, 2)
    cs_last = jnp.sum(jnp.where(li == CHUNK - 1, cs, 0.0), axis=1, keepdims=True)
    xdt = x * dt
    dec = jnp.exp(cs_last - cs)
    return x, dt, cs, bm, cm, li, si, lmat, g, cs_last, xdt, dec


def _ssd_fwd(xbc, dt_h, cs_h, cs_row, dskip_h, *, name):
    t = xbc.shape[0]
    nc = t // CHUNK
    hpg = SSD_HEADS // SSD_GROUPS
    pick, place = _head_matrices()

    def body(xs_ref, dt_ref, cs_ref, csr_ref, b_ref, c_ref, dk_ref, pick_ref, place_ref, y_ref, st_ref, sc_ref, cd_ref):
        h = pl.program_id(0)
        x, dt, cs, bm, cm, li, si, lmat, g, cs_last, xdt, dec = _ssd_common(_pick_head(xs_ref, pick_ref, h), dt_ref,
                                                                           cs_ref, csr_ref, b_ref, c_ref, nc)
        yd = _bdot((g * lmat).astype(BF16), xdt.astype(BF16), 2, 1)
        sc_ref[...] = _bdot(bm, (dec * xdt).astype(BF16), 1, 1)
        cd_ref[...] = jnp.exp(cs_last)

        def step(c, s):
            st_ref[0, c] = s
            return s * cd_ref[c] + sc_ref[c]

        lax.fori_loop(0, nc, step, jnp.zeros((SSD_N, SSD_P), F32))
        yo = _bdot(cm, st_ref[0].astype(BF16), 2, 1) * jnp.exp(cs)
        _place_head(y_ref, (yd + yo + dk_ref[0] * x).reshape(t, SSD_P), place_ref, h)

    head = pl.BlockSpec((1, t, SSD_P), lambda h: (h, 0, 0))
    pair = pl.BlockSpec((t, 2 * SSD_P), lambda h: (0, h // 2))
    nxb = D_SSM // SSD_N
    return pl.pallas_call(
        body, name=name, grid=(SSD_HEADS,),
        in_specs=[pair, head, head, pl.BlockSpec((1, nc, 1, CHUNK), lambda h: (h, 0, 0, 0)),
                  pl.BlockSpec((t, SSD_N), lambda h: (0, nxb + h // hpg)),
                  pl.BlockSpec((t, SSD_N), lambda h: (0, nxb + SSD_GROUPS + h // hpg)),
                  pl.BlockSpec((1, 1, SSD_P), lambda h: (h, 0, 0)),
                  pl.BlockSpec((2, 2 * SSD_P, SSD_P), lambda h: (0, 0, 0)),
                  pl.BlockSpec((2, SSD_P, 2 * SSD_P), lambda h: (0, 0, 0))],
        out_specs=[pair, pl.BlockSpec((1, nc, SSD_N, SSD_P), lambda h: (h, 0, 0, 0))],
        out_shape=[jax.ShapeDtypeStruct((t, D_SSM), F32),
                   jax.ShapeDtypeStruct((SSD_HEADS, nc, SSD_N, SSD_P), F32)],
        scratch_shapes=[pltpu.VMEM((nc, SSD_N, SSD_P), F32), pltpu.VMEM((nc, 1, SSD_P), F32)],
        compiler_params=_cp("arbitrary"),
    )(xbc, dt_h, cs_h, cs_row, xbc, xbc, dskip_h, pick, place)


def _ssd_bwd(xbc, dt_h, cs_h, cs_row, dskip_h, a_h, states, dy, *, name):
    t = xbc.shape[0]
    nc = t // CHUNK
    hpg = SSD_HEADS // SSD_GROUPS
    pick, place = _head_matrices()

    def body(xs_ref, dt_ref, cs_ref, csr_ref, b_ref, c_ref, dk_ref, a_ref, st_ref, dy_ref, pick_ref, place_ref,
             dxs_ref, ddt_ref, dadt_ref, db_ref, dc_ref, dd_ref, dsl_ref, dsc_ref, cd_ref):
        h = pl.program_id(0) * hpg + pl.program_id(1)
        x, dt, cs, bm, cm, li, si, lmat, g, cs_last, xdt, dec = _ssd_common(_pick_head(xs_ref, pick_ref, h), dt_ref,
                                                                           cs_ref, csr_ref, b_ref, c_ref, nc)
        dy = _pick_head(dy_ref, pick_ref, h).reshape(nc, CHUNK, SSD_P)
        dyb = dy.astype(BF16)
        xdtb = xdt.astype(BF16)
        sprev = st_ref[0]
        sprevb = sprev.astype(BF16)
        cdec = jnp.exp(cs_last)
        ecs = jnp.exp(cs)
        dw = (ecs * dy).astype(BF16)
        wmat = _bdot(cm, sprevb, 2, 1)
        dcs = jnp.sum(dy * ecs * wmat, axis=2, keepdims=True)
        dcm = _bdot(dw, sprevb, 2, 2)
        dsl_ref[...] = _bdot(cm, dw, 1, 1)
        cd_ref[...] = cdec

        def step(k, ds):
            c = nc - 1 - k
            dsc_ref[c] = ds
            return ds * cd_ref[c] + dsl_ref[c]

        lax.fori_loop(0, nc, step, jnp.zeros((SSD_N, SSD_P), F32))
        dsc = dsc_ref[...]
        dscb = dsc.astype(BF16)
        d_last = jnp.sum(jnp.sum(dsc * sprev, axis=1, keepdims=True) * cdec, axis=2, keepdims=True)
        z = dec * xdt
        dbm = _bdot(z.astype(BF16), dscb, 2, 2)
        dz = _bdot(bm, dscb, 2, 1)
        dxdt = dec * dz
        t2 = jnp.sum(dz * z, axis=2, keepdims=True)
        dcs = dcs - t2
        d_last = d_last + jnp.sum(t2, axis=1, keepdims=True)
        m = g * lmat
        mb = m.astype(BF16)
        dm = _bdot(dyb, xdtb, 2, 2)
        dxdt = dxdt + _bdot(mb, dyb, 1, 1)
        dseg = dm * m
        dcs = dcs + jnp.sum(dseg, axis=2, keepdims=True)
        ones = jnp.ones((nc, CHUNK, SSD_P), F32)
        dcs = dcs - _bdot(dseg, ones, 1, 1, precision=HI)
        dg = (dm * lmat).astype(BF16)
        dcm = dcm + _bdot(dg, bm, 2, 1)
        dbm = dbm + _bdot(dg, cm, 1, 1)
        dcs = dcs + jnp.where(li[:, :, :SSD_P] == CHUNK - 1, d_last, 0.0)
        triu = jnp.where(li <= si, 1.0, 0.0).astype(F32)
        dadt = _bdot(triu, dcs, 2, 1, precision=HI)
        dk = dk_ref[0]
        _place_head(dxs_ref, (dxdt * dt + dk * dy).reshape(t, SSD_P), place_ref, h)
        ddt = jnp.sum(dxdt * x, axis=2, keepdims=True) + dadt * a_ref[0]
        mine = lax.broadcasted_iota(jnp.int32, (t, LANES), 1) == h

        @pl.when(h == 0)
        def _():
            ddt_ref[...] = jnp.zeros_like(ddt_ref)
            dadt_ref[...] = jnp.zeros_like(dadt_ref)

        ddt_ref[...] += jnp.where(mine, jnp.max(ddt, axis=2, keepdims=True).reshape(t, 1), 0.0)
        dadt_ref[...] += jnp.where(mine, jnp.max(dadt, axis=2, keepdims=True).reshape(t, 1), 0.0)
        dd_ref[0] = jnp.sum(jnp.sum(dy * x, axis=1, keepdims=True), axis=0)

        @pl.when(pl.program_id(1) == 0)
        def _():
            db_ref[...] = jnp.zeros_like(db_ref)
            dc_ref[...] = jnp.zeros_like(dc_ref)

        db_ref[...] += dbm.reshape(t, SSD_N)
        dc_ref[...] += dcm.reshape(t, SSD_N)

    head = pl.BlockSpec((1, t, SSD_P), lambda gi, hi: (gi * hpg + hi, 0, 0))
    pair = pl.BlockSpec((t, 2 * SSD_P), lambda gi, hi: (0, (gi * hpg + hi) // 2))
    grp = pl.BlockSpec((t, SSD_N), lambda gi, hi: (0, gi))
    lane = pl.BlockSpec((1, 1, SSD_P), lambda gi, hi: (gi * hpg + hi, 0, 0))
    rows = pl.BlockSpec((t, LANES), lambda gi, hi: (0, 0))
    nxb = D_SSM // SSD_N
    dxs, ddt, dadt, db, dc, dd = pl.pallas_call(
        body, name=name, grid=(SSD_GROUPS, hpg),
        in_specs=[pair, head, head, pl.BlockSpec((1, nc, 1, CHUNK), lambda gi, hi: (gi * hpg + hi, 0, 0, 0)),
                  pl.BlockSpec((t, SSD_N), lambda gi, hi: (0, nxb + gi)),
                  pl.BlockSpec((t, SSD_N), lambda gi, hi: (0, nxb + SSD_GROUPS + gi)), lane, lane,
                  pl.BlockSpec((1, nc, SSD_N, SSD_P), lambda gi, hi: (gi * hpg + hi, 0, 0, 0)), pair,
                  pl.BlockSpec((2, 2 * SSD_P, SSD_P), lambda gi, hi: (0, 0, 0)),
                  pl.BlockSpec((2, SSD_P, 2 * SSD_P), lambda gi, hi: (0, 0, 0))],
        out_specs=[pair, rows, rows, grp, grp, lane],
        out_shape=[jax.ShapeDtypeStruct((t, D_SSM), F32)] + [jax.ShapeDtypeStruct((t, LANES), F32)] * 2
        + [jax.ShapeDtypeStruct((t, SSD_GROUPS * SSD_N), F32)] * 2
        + [jax.ShapeDtypeStruct((SSD_HEADS, 1, SSD_P), F32)],
        scratch_shapes=[pltpu.VMEM((nc, SSD_N, SSD_P), F32), pltpu.VMEM((nc, SSD_N, SSD_P), F32),
                        pltpu.VMEM((nc, 1, SSD_P), F32)],
        compiler_params=_cp("arbitrary", "arbitrary"),
    )(xbc, dt_h, cs_h, cs_row, xbc, xbc, dskip_h, a_h, states, dy, pick, place)
    return jnp.concatenate([dxs, db, dc], axis=1), ddt, dadt, dd


def _ssd_gate_fwd(y, proj, w, *, tr=256, name):
    t = y.shape[0]
    gw = D_SSM // SSD_GROUPS

    def body(y_ref, z_ref, w_ref, o_ref):
        v = y_ref[...] * _silu(z_ref[...])
        for gi in range(SSD_GROUPS):
            vg = v[:, gi * gw:(gi + 1) * gw]
            r = lax.rsqrt(jnp.mean(vg * vg, axis=-1, keepdims=True) + NORM_EPS)
            o_ref[:, gi * gw:(gi + 1) * gw] = (vg * r * w_ref[:, gi * gw:(gi + 1) * gw]).astype(BF16)

    blk = pl.BlockSpec((tr, D_SSM), lambda i: (i, 0))
    return pl.pallas_call(
        body, name=name, grid=(t // tr,), in_specs=[blk, blk, pl.BlockSpec((1, D_SSM), lambda i: (0, 0))],
        out_specs=blk, out_shape=jax.ShapeDtypeStruct((t, D_SSM), BF16), compiler_params=_cp("parallel"),
    )(y, proj, w)


def _ssd_gate_bwd(y, proj, w, dcat, *, tr=256, name):
    t = y.shape[0]
    gw = D_SSM // SSD_GROUPS

    def body(y_ref, z_ref, w_ref, d_ref, dy_ref, dz_ref, dw_ref):
        yv, zv, dv = y_ref[...], z_ref[...], d_ref[...].astype(F32)
        sz = _silu(zv)
        v = yv * sz

        @pl.when(pl.program_id(0) == 0)
        def _():
            dw_ref[...] = jnp.zeros_like(dw_ref)

        for gi in range(SSD_GROUPS):
            sl = slice(gi * gw, (gi + 1) * gw)
            vg, dg = v[:, sl], dv[:, sl]
            r = lax.rsqrt(jnp.mean(vg * vg, axis=-1, keepdims=True) + NORM_EPS)
            vh = vg * r
            gg = dg * w_ref[:, sl]
            dvg = r * (gg - vh * jnp.mean(gg * vh, axis=-1, keepdims=True))
            dy_ref[:, sl] = dvg * sz[:, sl]
            dz_ref[:, sl] = (dvg * yv[:, sl] * _dsilu(zv[:, sl])).astype(BF16)
            dw_ref[:, sl] += jnp.sum(dg * vh, axis=0, keepdims=True)

    blk = pl.BlockSpec((tr, D_SSM), lambda i: (i, 0))
    row = pl.BlockSpec((1, D_SSM), lambda i: (0, 0))
    return pl.pallas_call(
        body, name=name, grid=(t // tr,), in_specs=[blk, blk, row, blk], out_specs=[blk, blk, row],
        out_shape=[jax.ShapeDtypeStruct((t, D_SSM), F32), jax.ShapeDtypeStruct((t, D_SSM), BF16),
                   jax.ShapeDtypeStruct((1, D_SSM), F32)],
        compiler_params=_cp("arbitrary"),
    )(y, proj, w, dcat)


def _pad_lanes(v):
    return jnp.pad(v, ((0, 0), (0, LANES - v.shape[1])))


def _per_head(v128, t):
    return jnp.broadcast_to(v128[:, :SSD_HEADS].T[:, :, None], (SSD_HEADS, t, SSD_P))


def _ssd_forward(proj, conv_w, conv_b, dt_bias, a_log, d_skip, ssd_norm_w):
    t = proj.shape[0]
    nc = t // CHUNK
    xbc = _conv_act_fwd(proj, conv_w, conv_b, kw=SSD_CONV, glu=False, tc=512, coff=OFF_XBC // 512,
                        ncols=SSD_CONV_DIM, out_dtype=F32, name="ssd_conv_fwd")
    bias128, alog128 = _pad_lanes(dt_bias), _pad_lanes(a_log)
    dt128, cs128, a128 = _ssd_prep(proj, bias128, alog128, name="ssd_prep")
    dt_h, cs_h = _per_head(dt128, t), _per_head(cs128, t)
    cs_row = cs128[:, :SSD_HEADS].T.reshape(SSD_HEADS, nc, 1, CHUNK)
    dskip_h = jnp.broadcast_to(d_skip[0][:, None, None], (SSD_HEADS, 1, SSD_P))
    a_h = jnp.broadcast_to(a128[0, :SSD_HEADS][:, None, None], (SSD_HEADS, 1, SSD_P))
    y, states = _ssd_fwd(xbc, dt_h, cs_h, cs_row, dskip_h, name="ssd_scan_fwd")
    y_ssd = _ssd_gate_fwd(y, proj, ssd_norm_w, name="ssd_gate_fwd")
    saved = (proj, conv_w, conv_b, ssd_norm_w, bias128, dt128, a128, dt_h, cs_h, cs_row, xbc, dskip_h, a_h, states, y)
    return y_ssd, saved


def _ssd_backward(saved, dcat):
    proj, conv_w, conv_b, ssd_norm_w, bias128, dt128, a128, dt_h, cs_h, cs_row, xbc, dskip_h, a_h, states, y = saved
    dy, dz, d_norm_w = _ssd_gate_bwd(y, proj, ssd_norm_w, dcat, name="ssd_gate_bwd")
    dxc, ddt128, dadt128, dd_h = _ssd_bwd(xbc, dt_h, cs_h, cs_row, dskip_h, a_h, states, dy, name="ssd_scan_bwd")
    dxbc, d_conv_w, d_conv_b = _conv_act_bwd(proj, conv_w, conv_b, dxc, kw=SSD_CONV, glu=False, tc=512,
                                             coff=OFF_XBC // 512, ncols=SSD_CONV_DIM, name="ssd_conv_bwd")
    d_raw, d_bias, d_alog, d_dskip = _ssd_prep_bwd(ddt128, dadt128, proj, bias128, dt128, a128,
                                                   dd_h.reshape(SSD_HEADS, SSD_P), name="ssd_prep_bwd")
    return (dz, dxbc, d_raw, d_norm_w, d_conv_w, d_conv_b, d_bias[:, :SSD_HEADS], d_alog[:, :SSD_HEADS],
            d_dskip.reshape(1, SSD_HEADS))


def _rope_tables(positions):
    inv_freq = ROPE_THETA ** (-jnp.arange(0, MLA_ROPE, 2, dtype=F32) / MLA_ROPE)
    ang = positions[0].astype(F32)[:, None] * inv_freq
    cos, sin = jnp.cos(ang), jnp.sin(ang)
    z = jnp.zeros_like(cos)
    return jnp.stack([jnp.concatenate([cos, cos, z, z], axis=1), jnp.concatenate([-sin, z, z, z], axis=1),
                      jnp.concatenate([z, sin, z, z], axis=1)])


def _mla_forward(proj, tabs, q_a_norm_w, wq_pad, kv_a_norm_w, wkv):
    qn = _rmsnorm_fwd(proj, q_a_norm_w, width=MLA_Q_RANK, cblk=OFF_QA // MLA_Q_RANK, name="q_a_norm")
    q = _matmul(qn, wq_pad, name="q_b_proj")
    kvn = _rmsnorm_fwd(proj, kv_a_norm_w, width=MLA_KV_RANK, cblk=OFF_CKV // MLA_KV_RANK, name="kv_a_norm")
    kv = _matmul(kvn, wkv, name="kv_b_proj")
    q3, k3, v3, vt4 = _mla_prep(q, kv, proj, tabs, name="mla_prep")
    o, lse = _attn_fwd(q3, k3, vt4, name="attn_fwd")
    return o, (proj, tabs, q_a_norm_w, wq_pad, kv_a_norm_w, wkv, qn, kvn, q3, k3, v3, o, lse)


def _mla_backward(saved, dcat):
    proj, tabs, q_a_norm_w, wq_pad, kv_a_norm_w, wkv, qn, kvn, q3, k3, v3, o, lse = saved
    dq3, dk3, dv3 = _attn_bwd(q3, k3, v3, o, dcat, lse, name="attn_bwd")
    dq, dkv, dkr = _mla_unprep(dq3, dk3, dv3, tabs, name="mla_unprep")
    d_wq = _matmul(qn, dq, ta=True, out_dtype=BF16, name="d_w_q_b")
    dqn = _matmul(dq, wq_pad, tb=True, name="d_qn")
    dq_a, d_qnw = _rmsnorm_bwd(proj, q_a_norm_w, dqn, width=MLA_Q_RANK, cblk=OFF_QA // MLA_Q_RANK, out_dtype=BF16,
                               name="q_a_norm_bwd")
    d_wkv = _matmul(kvn, dkv, ta=True, out_dtype=BF16, name="d_w_kv_b")
    dkvn = _matmul(dkv, wkv, tb=True, name="d_kvn")
    dckv, d_kvnw = _rmsnorm_bwd(proj, kv_a_norm_w, dkvn, width=MLA_KV_RANK, cblk=OFF_CKV // MLA_KV_RANK,
                                out_dtype=BF16, name="kv_a_norm_bwd")
    return dq_a, dckv, dkr, d_wq, d_wkv, d_qnw, d_kvnw


def _pad_w_q(w):
    r = w.shape[0]
    w3 = w.reshape(r, MLA_HEADS, MLA_NOPE + MLA_ROPE)
    return jnp.pad(w3, ((0, 0), (0, 0), (0, MLA_QK_PAD - MLA_NOPE - MLA_ROPE))).reshape(r, MLA_HEADS * MLA_QK_PAD)


def _unpad_w_q(w):
    r = w.shape[0]
    return w.reshape(r, MLA_HEADS, MLA_QK_PAD)[:, :, :MLA_NOPE + MLA_ROPE].reshape(r, MLA_HEADS * (MLA_NOPE + MLA_ROPE))


W_IN_SEGMENTS = ((0, D_SSM + SSD_CONV_DIM, 0), (D_SSM + SSD_CONV_DIM, D_SSM + SSD_CONV_DIM + SSD_HEADS, OFF_DT),
                 (D_SSM + SSD_CONV_DIM + SSD_HEADS, D_IN - MLA_ROPE, OFF_QA), (D_IN - MLA_ROPE, D_IN, OFF_KR))


def _pad_w_in_shards(g):
    n = g.shape[2]
    pieces, at = [], 0
    for lo, hi, start in sorted(W_IN_SEGMENTS, key=lambda seg: seg[2]):
        if start > at:
            pieces.append(jnp.zeros((g.shape[1], start - at), g.dtype))
        for j in range(N_DEV):
            a, b = max(lo, j * n), min(hi, (j + 1) * n)
            if a < b:
                pieces.append(g[j][:, a - j * n:b - j * n])
        at = start + hi - lo
    pieces.append(jnp.zeros((g.shape[1], D_IN_PAD - at), g.dtype))
    return jnp.concatenate(pieces, axis=1)


def _unpad_w_in_shards(w):
    n = D_IN // N_DEV
    shards = []
    for j in range(N_DEV):
        pieces = []
        for lo, hi, start in W_IN_SEGMENTS:
            a, b = max(lo, j * n), min(hi, (j + 1) * n)
            if a < b:
                pieces.append(w[:, start + a - lo:start + b - lo])
        shards.append(jnp.concatenate(pieces, axis=1) if len(pieces) > 1 else pieces[0])
    return jnp.stack(shards)


WEIGHTS = ['mix_norm_w', 'w_in', 'conv_w', 'conv_b', 'dt_bias', 'a_log', 'd_skip', 'ssd_norm_w', 'q_a_norm_w', 'w_q_b',
           'kv_a_norm_w', 'w_kv_b', 'w_out', 'ffn_norm_w', 'w_ffn_up', 'ffn_conv_w', 'ffn_conv_b', 'w_ffn_down',
           'ple_norm_w', 'w_ple_gate', 'b_ple_gate', 'w_ple_proj', 'ple_post_norm_w', 'final_norm_w']
BIG = ['w_in', 'w_q_b', 'w_kv_b', 'w_out', 'w_ffn_up', 'w_ffn_down', 'w_ple_gate', 'w_ple_proj']
COL_SHARDED = ('w_in', 'w_q_b', 'w_kv_b', 'w_ffn_up', 'w_ple_proj')
CONV = ['conv_w', 'ffn_conv_w']
REPL = [n for n in WEIGHTS if n not in BIG and n not in CONV]
FFN_INV = tuple(int(i) for i in np.argsort(FFN_PERM))


def _cat_cols(g):
    return jnp.concatenate([g[j] for j in range(N_DEV)], axis=1)


def _split_cols(w):
    n = w.shape[1] // N_DEV
    return jnp.stack([w[:, j * n:(j + 1) * n] for j in range(N_DEV)])


def _interleave(v):
    r = v.shape[0]
    return v.reshape(r, N_DEV, FFN_TC)[:, jnp.array(FFN_PERM)].reshape(r, N_DEV * FFN_TC)


def _deinterleave(v):
    r = v.shape[0]
    return v.reshape(r, N_DEV, FFN_TC)[:, jnp.array(FFN_INV)].reshape(r, N_DEV * FFN_TC)


def _assemble_weights(g):
    layout = {
        'w_in': _pad_w_in_shards,
        'w_q_b': lambda v: _pad_w_q(_cat_cols(v)),
        'w_kv_b': _cat_cols,
        'w_out': lambda v: v.reshape(D_MODEL, D_MODEL),
        'w_ffn_up': lambda v: v,
        'w_ffn_down': lambda v: v.reshape(D_FF, D_MODEL),
        'w_ple_gate': lambda v: v.reshape(D_MODEL, D_MODEL),
        'w_ple_proj': _cat_cols,
        'conv_w': _cat_cols,
        'ffn_conv_w': lambda v: _interleave(_cat_cols(v)),
    }
    return {n: layout[n](v) for n, v in g.items()}


WEIGHT_GROUPS = {'a': ['w_in', 'w_q_b', 'w_kv_b', 'conv_w'], 'b': ['w_out'],
                 'c': ['w_ffn_up', 'ffn_conv_w', 'w_ffn_down', 'w_ple_gate', 'w_ple_proj']}
GRAD_GROUPS = {'p': ['w_ple_proj', 'w_ple_gate', 'w_ffn_down'], 'r': ['w_ffn_up'], 's': ['w_out'],
               't': ['w_q_b', 'w_kv_b', 'w_in']}


def _ffn_perm(j):
    return (j % 2) * (N_DEV // 2) + j // 2


def _local_step(x, p, tabs, get_w, s, target, emit, relay, settle):
    t = x.shape[0]
    s = dict(s)
    half = D_MODEL // 2
    up_cols = 2 * D_FF
    ffn_conv_b = _interleave(s['ffn_conv_b'])
    w = dict(get_w('a', None))
    h = _rmsnorm_fwd(x, s['mix_norm_w'], width=D_MODEL, name="mix_norm")
    proj = _matmul(h, w['w_in'], name="in_proj")
    y_ssd, ssd_saved = _ssd_forward(proj, w['conv_w'], s['conv_b'], s['dt_bias'], s['a_log'], s['d_skip'],
                                    s['ssd_norm_w'])
    o, mla_saved = _mla_forward(proj, tabs, s['q_a_norm_w'], w['w_q_b'], s['kv_a_norm_w'], w['w_kv_b'])
    tk_o, tn_o = _tile(half, MM_TK), _tile(D_MODEL, MM_TILE)
    w.update(get_w('b', o))
    x1 = _matmul(y_ssd, w['w_out'], add=x, mnk=(t, D_MODEL, half), name="out_proj_ssd")
    x1 = _matmul(o, w['w_out'], add=x1, mnk=(t, D_MODEL, half), name="out_proj_mla",
                 b_spec=pl.BlockSpec((tk_o, tn_o), lambda i, j, kk: (kk + half // tk_o, j)))
    hf = _rmsnorm_fwd(x1, s['ffn_norm_w'], width=D_MODEL, name="ffn_norm")
    w.update(get_w('c', hf))
    tk_u = _tile(D_MODEL, MM_TK)
    u = _matmul(hf, w['w_ffn_up'], mnk=(t, up_cols, D_MODEL), tn=FFN_TC, name="ffn_up",
                b_spec=pl.BlockSpec((1, tk_u, FFN_TC), lambda i, j, kk: (_ffn_perm(j), kk, 0)))
    act = _conv_act_fwd(u, w['ffn_conv_w'], ffn_conv_b, kw=FFN_CONV, glu=True, tc=2 * FFN_TC, coff=0, ncols=up_cols,
                        out_dtype=BF16, name="ffn_act")
    x2 = _matmul(act, w['w_ffn_down'], add=x1, name="ffn_down")
    hp = _rmsnorm_fwd(x2, s['ple_norm_w'], width=D_MODEL, name="ple_norm")
    gl = _matmul(hp, w['w_ple_gate'], bias=s['b_ple_gate'], name="ple_gate")
    pe = _matmul(p, w['w_ple_proj'], name="ple_proj")
    x3 = _ple_fwd(x2, gl, pe, s['ple_post_norm_w'], name="ple_mix")
    loss, dx3, d_final = _loss_head(x3, s['final_norm_w'], target, name="loss_head")
    dgl, d_bgate, dpe, d_post = _ple_bwd(dx3, gl, pe, s['ple_post_norm_w'], name="ple_mix_bwd")
    d_wproj = _matmul(p, dpe, ta=True, out_dtype=BF16, name="d_w_ple_proj")
    d_wgate = _matmul(hp, dgl, ta=True, out_dtype=BF16, name="d_w_ple_gate")
    dhp = _matmul(dgl, w['w_ple_gate'], tb=True, name="d_ple_normed")
    dx2, d_plenorm = _rmsnorm_bwd(x2, s['ple_norm_w'], dhp, dx3, width=D_MODEL, name="ple_norm_bwd")
    dact = _matmul(dx2, w['w_ffn_down'], tb=True, name="d_ffn_act")
    d_wdown = _matmul(act, dx2, ta=True, out_dtype=BF16, name="d_w_ffn_down")
    zz = emit('p', {'w_ple_proj': _split_cols(d_wproj), 'w_ple_gate': d_wgate.reshape(N_DEV, D_MODEL // N_DEV, D_MODEL),
                    'w_ffn_down': d_wdown.reshape(N_DEV, D_FF // N_DEV, D_MODEL)})
    du, d_fconv_w, d_fconv_b = _conv_act_bwd(u, w['ffn_conv_w'], ffn_conv_b + zz, dact, kw=FFN_CONV, glu=True,
                                             tc=2 * FFN_TC, coff=0, ncols=up_cols, name="ffn_act_bwd")
    zz = zz + relay('p', du)
    tm_u = _tile(D_MODEL, MM_TILE)
    d_wup = _matmul(hf, du, ta=True, out_dtype=BF16, mnk=(D_MODEL, up_cols, t), tn=FFN_TC, name="d_w_ffn_up",
                    o_spec=pl.BlockSpec((1, tm_u, FFN_TC), lambda i, j, kk: (_ffn_perm(j), i, 0)),
                    o_shape=(N_DEV, D_MODEL, FFN_TC))
    zz = zz + emit('r', {'w_ffn_up': d_wup})
    zero_row = jnp.zeros((1, D_MODEL), F32)
    dhf = _matmul(du, w['w_ffn_up'], tb=True, mnk=(t, D_MODEL, up_cols), tk=FFN_TC, name="d_ffn_normed",
                  bias=zero_row + zz,
                  b_spec=pl.BlockSpec((1, tn_o, FFN_TC), lambda i, j, kk: (_ffn_perm(kk), j, 0)))
    zz = zz + relay('r', dhf) + settle('p')
    dx1, d_ffnnorm = _rmsnorm_bwd(x1, s['ffn_norm_w'] + zz, dhf, dx2, width=D_MODEL, name="ffn_norm_bwd")
    dcat = _matmul(dx1, w['w_out'], tb=True, name="d_mixed")
    d_wout = jnp.concatenate([_matmul(y_ssd, dx1, ta=True, out_dtype=BF16, name="d_w_out_ssd"),
                              _matmul(o, dx1, ta=True, out_dtype=BF16, name="d_w_out_mla")], axis=0)
    zz = zz + emit('s', {'w_out': d_wout.reshape(N_DEV, D_MODEL // N_DEV, D_MODEL)})
    ssd_saved = ssd_saved[:3] + (ssd_saved[3] + zz,) + ssd_saved[4:]
    dz, dxbc, d_raw, d_ssdnorm, d_conv_w, d_conv_b, d_dtb, d_alog, d_dskip = _ssd_backward(ssd_saved, dcat)
    zz = zz + relay('s', dz)
    mla_saved = mla_saved[:-1] + (mla_saved[-1] + zz,)
    dq_a, dckv, dkr, d_wq, d_wkv, d_qnorm, d_kvnorm = _mla_backward(mla_saved, dcat)
    d_raw = (d_raw + settle('r')).astype(BF16)
    dproj = jnp.concatenate([dz, dxbc, dq_a, dckv, dkr, d_raw], axis=1)
    d_win = _matmul(h, dproj, ta=True, out_dtype=BF16, name="d_w_in")
    zz = emit('t', {'w_in': _unpad_w_in_shards(d_win), 'w_q_b': _split_cols(_unpad_w_q(d_wq)),
                    'w_kv_b': _split_cols(d_wkv)}) + settle('s')
    dh = _matmul(dproj, w['w_in'], tb=True, bias=zero_row + zz, name="d_in_normed")
    zz = relay('t', dh)
    dx, d_mixnorm = _rmsnorm_bwd(x, s['mix_norm_w'] + zz, dh, dx1, width=D_MODEL, name="mix_norm_bwd")
    conv = {'conv_w': d_conv_w, 'ffn_conv_w': _deinterleave(d_fconv_w)}
    vec = {
        'mix_norm_w': d_mixnorm, 'conv_b': d_conv_b, 'dt_bias': d_dtb, 'a_log': d_alog, 'd_skip': d_dskip,
        'ssd_norm_w': d_ssdnorm, 'q_a_norm_w': d_qnorm, 'kv_a_norm_w': d_kvnorm, 'ffn_norm_w': d_ffnnorm,
        'ffn_conv_b': _deinterleave(d_fconv_b), 'ple_norm_w': d_plenorm, 'b_ple_gate': d_bgate,
        'ple_post_norm_w': d_post, 'final_norm_w': d_final,
    }
    return loss, dx, conv, vec


MESH = pl.DeviceIdType.MESH
FLIPS = ((0, 0, 1), (1, 0, 0), (0, 1, 0), (1, 1, 0), (1, 0, 1), (0, 1, 1), (1, 1, 1))


def _exchange(items, *, gather, name):
    n = len(items)

    def body(*refs):
        ins, outs = refs[:n], refs[n:2 * n]
        send_sems, recv_sems, local_sems = refs[2 * n:]
        x, y, c = lax.axis_index("x"), lax.axis_index("y"), lax.axis_index("c")
        me = 4 * x + 2 * y + c
        peers = [(jnp.where(fx, 1 - x, x), jnp.where(fy, 1 - y, y), jnp.where(fc, 1 - c, c)) for fx, fy, fc in FLIPS]
        slot = [4 * px + 2 * py + pc for px, py, pc in peers]
        local, sends = [], []
        for wi in range(n):
            cp = pltpu.make_async_copy(ins[wi] if gather else ins[wi].at[me], outs[wi].at[me], local_sems.at[wi])
            cp.start()
            local.append(cp)
            for k, peer in enumerate(peers):
                cp = pltpu.make_async_remote_copy(
                    src_ref=ins[wi] if gather else ins[wi].at[slot[k]], dst_ref=outs[wi].at[me],
                    send_sem=send_sems.at[k, wi], recv_sem=recv_sems.at[k, wi], device_id=peer, device_id_type=MESH)
                cp.start()
                sends.append(cp)
        for wi in range(n):
            for k, peer in enumerate(peers):
                pltpu.make_async_remote_copy(
                    src_ref=outs[wi].at[slot[k]], dst_ref=outs[wi].at[slot[k]], send_sem=send_sems.at[k, wi],
                    recv_sem=recv_sems.at[k, wi], device_id=peer, device_id_type=MESH).wait_recv()
        for cp in sends:
            cp.wait_send()
        for cp in local:
            cp.wait()

    hbm = pl.BlockSpec(memory_space=pltpu.HBM)
    out_shape = [jax.ShapeDtypeStruct(((N_DEV,) + v.shape) if gather else v.shape, v.dtype) for v in items]
    return pl.pallas_call(
        body, name=name, in_specs=[hbm] * n, out_specs=[hbm] * n, out_shape=out_shape,
        scratch_shapes=[pltpu.SemaphoreType.DMA((len(FLIPS), n)), pltpu.SemaphoreType.DMA((len(FLIPS), n)),
                        pltpu.SemaphoreType.DMA((n,))],
    )(*items)


HBM_SPEC = pl.BlockSpec(memory_space=pltpu.HBM)
SEM_SPEC = pl.BlockSpec(memory_space=pltpu.SEMAPHORE)
EFFECT = pltpu.SideEffectType.DATAFLOW_SIDE_EFFECTING


def _split_start(bufs, ncopies, plan, *, name):
    nb = len(bufs)

    def body(*refs):
        send_sems, recv_sems, token = refs[nb], refs[nb + 1], refs[2 * nb + 2]
        for i, (src, dst, peer, _) in enumerate(plan(refs[:nb])):
            pltpu.make_async_remote_copy(src_ref=src, dst_ref=dst, send_sem=send_sems.at[i], recv_sem=recv_sems.at[i],
                                         device_id=peer, device_id_type=MESH).start()
        token[...] = jnp.zeros_like(token)

    res = pl.pallas_call(
        body, name=name, in_specs=[HBM_SPEC] * nb,
        out_specs=[SEM_SPEC, SEM_SPEC] + [HBM_SPEC] * nb + [pl.BlockSpec(memory_space=pltpu.VMEM)],
        out_shape=[pltpu.SemaphoreType.DMA((ncopies,)), pltpu.SemaphoreType.DMA((ncopies,))]
        + [pltpu.HBM(v.shape, v.dtype) for v in bufs] + [jax.ShapeDtypeStruct((HALO, LANES), F32)],
        input_output_aliases={i: 2 + i for i in range(nb)},
        compiler_params=pltpu.CompilerParams(has_side_effects=EFFECT),
    )(*[pltpu.with_memory_space_constraint(v, pltpu.HBM) for v in bufs])
    return (res[0], res[1], list(res[2:2 + nb])), res[2 + nb]


def _split_wait(started, after, plan, local_plan, *, name):
    send_sems, recv_sems, bufs = started
    nb = len(bufs)
    nlocal = len(local_plan(bufs))

    def body(*refs):
        send_sems, recv_sems = refs[nb], refs[nb + 1]
        local_sems = refs[2 * nb + 3]
        local = []
        for j, (src, dst) in enumerate(local_plan(refs[:nb])):
            cp = pltpu.make_async_copy(src, dst, local_sems.at[j])
            cp.start()
            local.append(cp)
        for i, (src, _, peer, incoming) in enumerate(plan(refs[:nb])):
            cp = pltpu.make_async_remote_copy(src_ref=src, dst_ref=incoming, send_sem=send_sems.at[i],
                                              recv_sem=recv_sems.at[i], device_id=peer, device_id_type=MESH)
            cp.wait_send()
            cp.wait_recv()
        for cp in local:
            cp.wait()

    res = pl.pallas_call(
        body, name=name, in_specs=[HBM_SPEC] * nb + [SEM_SPEC, SEM_SPEC, pl.BlockSpec(memory_space=pl.ANY)],
        out_specs=[HBM_SPEC] * nb, out_shape=[pltpu.HBM(v.shape, v.dtype) for v in bufs],
        input_output_aliases={i: i for i in range(nb)},
        scratch_shapes=[pltpu.SemaphoreType.DMA((max(nlocal, 1),))],
        compiler_params=pltpu.CompilerParams(has_side_effects=EFFECT),
    )(*bufs, send_sems, recv_sems, after)
    return list(res)


def _place():
    x, y, c = lax.axis_index("x"), lax.axis_index("y"), lax.axis_index("c")
    others = [((1 - x, y, c), 2 * (1 - x) + y), ((x, 1 - y, c), 2 * x + 1 - y), ((1 - x, 1 - y, c), 2 * (1 - x) + 1 - y)]
    return 4 * x + 2 * y + c, 2 * x + y, c, (x, y, 1 - c), others


def _gather1_plan(n):
    def plan(refs):
        me, _, _, sibling, others = _place()
        out = []
        for wi in range(n):
            item, land = refs[wi], refs[n + wi]
            out.append((item, land.at[me], sibling, land.at[me + 1 - 2 * lax.axis_index("c")]))
            for peer, chip in others:
                out.append((item, land.at[me], peer, land.at[2 * chip + lax.axis_index("c")]))
        return out

    return plan


def _gather1_local(n):
    def plan(refs):
        me = _place()[0]
        return [(refs[wi], refs[n + wi].at[me]) for wi in range(n)]

    return plan


def _gather2_plan(n):
    def plan(refs):
        _, _, c, sibling, others = _place()
        out = []
        for wi in range(n):
            land = refs[wi]
            for _, chip in others:
                out.append((land.at[2 * chip + c], land.at[2 * chip + c], sibling, land.at[2 * chip + 1 - c]))
        return out

    return plan


def _gather_start(items, *, name):
    lands = [lax.empty((N_DEV,) + v.shape, v.dtype) for v in items]
    return _split_start(items + lands, 4 * len(items), _gather1_plan(len(items)), name=name)


def _gather_forward(started, after, *, name):
    n = len(started[2]) // 2
    bufs = _split_wait(started, after, _gather1_plan(n), _gather1_local(n), name=name + "_wait")
    return _split_start(bufs[n:], 3 * n, _gather2_plan(n), name=name + "_start")


def _gather_finish(started, after, *, name):
    n = len(started[2])
    return _split_wait(started, after, _gather2_plan(n), lambda refs: [], name=name)


def _handshake(peers):
    barrier = pltpu.get_barrier_semaphore()
    for peer in peers:
        pl.semaphore_signal(barrier, inc=1, device_id=peer, device_id_type=MESH)
    pl.semaphore_wait(barrier, len(peers))


def _remote(src, dst, send_sem, recv_sem, peer):
    return pltpu.make_async_remote_copy(src_ref=src, dst_ref=dst, send_sem=send_sem, recv_sem=recv_sem, device_id=peer,
                                        device_id_type=MESH)


def _sequencer_gather(items, *, collective_id, name):
    n = len(items)
    srcs = [jax.new_ref(v, memory_space=pltpu.MemorySpace.HBM) for v in items]
    lands = [jax.empty_ref(jax.ShapeDtypeStruct((N_DEV,) + v.shape, v.dtype), memory_space=pltpu.MemorySpace.HBM)
             for v in items]
    dma = pltpu.SemaphoreType.DMA

    @pl.kernel(mesh=plsc.ScalarSubcoreMesh(axis_name="sequencer", num_cores=1), name=name,
               scratch_types=(dma((4 * n,)), dma((4 * n,)), dma((3 * n,)), dma((3 * n,)), dma((n,))),
               compiler_params=pltpu.CompilerParams(collective_id=collective_id))
    def launch(send1, recv1, send2, recv2, local_sems):
        _, _, _, sibling, others = _place()
        _handshake([sibling] + [peer for peer, _ in others])
        hop1 = _gather1_plan(n)(srcs + lands)
        hop2 = _gather2_plan(n)(lands)
        local = [pltpu.make_async_copy(src, dst, local_sems.at[j])
                 for j, (src, dst) in enumerate(_gather1_local(n)(srcs + lands))]
        for cp in local:
            cp.start()
        for i, (src, dst, peer, _) in enumerate(hop1):
            _remote(src, dst, send1.at[i], recv1.at[i], peer).start()
        for wi in range(n):
            for j in range(3):
                i1, i2 = 4 * wi + 1 + j, 3 * wi + j
                src, _, peer, incoming = hop1[i1]
                _remote(src, incoming, send1.at[i1], recv1.at[i1], peer).wait_recv()
                src, dst, peer, _ = hop2[i2]
                _remote(src, dst, send2.at[i2], recv2.at[i2], peer).start()
        for wi in range(n):
            src, _, peer, incoming = hop1[4 * wi]
            _remote(src, incoming, send1.at[4 * wi], recv1.at[4 * wi], peer).wait_recv()
        for i, (src, _, peer, incoming) in enumerate(hop2):
            cp = _remote(src, incoming, send2.at[i], recv2.at[i], peer)
            cp.wait_send()
            cp.wait_recv()
        for i, (src, dst, peer, _) in enumerate(hop1):
            _remote(src, dst, send1.at[i], recv1.at[i], peer).wait_send()
        for cp in local:
            cp.wait()

    launch()
    return [land[...] for land in lands]


def _sequencer_exchange(sources, land_shapes, ncopies, plan, local_plan, peers, *, collective_id, name):
    srcs = [jax.new_ref(v, memory_space=pltpu.MemorySpace.HBM) for v in sources]
    lands = [jax.empty_ref(s, memory_space=pltpu.MemorySpace.HBM) for s in land_shapes]
    nlocal = len(local_plan(srcs + lands))
    dma = pltpu.SemaphoreType.DMA

    @pl.kernel(mesh=plsc.ScalarSubcoreMesh(axis_name="sequencer", num_cores=1), name=name,
               scratch_types=(dma((ncopies,)), dma((ncopies,)), dma((max(nlocal, 1),))),
               compiler_params=pltpu.CompilerParams(collective_id=collective_id))
    def launch(send_sems, recv_sems, local_sems):
        _handshake(peers(_place()))
        copies = plan(srcs + lands)
        local = [pltpu.make_async_copy(src, dst, local_sems.at[j])
                 for j, (src, dst) in enumerate(local_plan(srcs + lands))]
        for cp in local:
            cp.start()
        for i, (src, dst, peer, _) in enumerate(copies):
            _remote(src, dst, send_sems.at[i], recv_sems.at[i], peer).start()
        for i, (src, _, peer, incoming) in enumerate(copies):
            cp = _remote(src, incoming, send_sems.at[i], recv_sems.at[i], peer)
            cp.wait_send()
            cp.wait_recv()
        for cp in local:
            cp.wait()

    launch()
    return [land[...] for land in lands]


def _sequencer_scatter_hop2(sums, *, collective_id, name):
    n = len(sums)
    shapes = [jax.ShapeDtypeStruct(v.shape, v.dtype) for v in sums]
    return _sequencer_exchange(sums, shapes, 3 * n, _scatter2_plan(n), _scatter2_local(n),
                               lambda place: [peer for peer, _ in place[4]], collective_id=collective_id, name=name)


N_CHIP = N_DEV // 2


def _scatter1_plan(n):
    def plan(refs):
        _, _, c, sibling, _ = _place()
        out = []
        for wi in range(n):
            parts, half = refs[wi], refs[n + wi]
            for chip in range(N_CHIP):
                out.append((parts.at[2 * chip + 1 - c], half.at[chip], sibling, half.at[chip]))
        return out

    return plan


def _scatter2_plan(n):
    def plan(refs):
        _, my_chip, _, _, others = _place()
        out = []
        for wi in range(n):
            sums, recv = refs[wi], refs[n + wi]
            for peer, chip in others:
                out.append((sums.at[chip], recv.at[my_chip], peer, recv.at[chip]))
        return out

    return plan


def _scatter2_local(n):
    def plan(refs):
        my_chip = _place()[1]
        return [(refs[wi].at[my_chip], refs[n + wi].at[my_chip]) for wi in range(n)]

    return plan


def _pair_add(parts, half, core, *, name):
    _, r, c = parts.shape
    tr = max(d for d in range(HALO, 257, HALO) if r % d == 0) if r > 256 else r
    parts4 = parts.reshape(N_CHIP, 2, r, c)

    def body(core_ref, p_ref, h_ref, o_ref):
        o_ref[...] = (p_ref[:, 0].astype(F32) + h_ref[...].astype(F32)).astype(o_ref.dtype)

    return pl.pallas_call(
        body, name=name,
        grid_spec=pltpu.PrefetchScalarGridSpec(
            num_scalar_prefetch=1, grid=(r // tr,),
            in_specs=[pl.BlockSpec((N_CHIP, 1, tr, c), lambda i, core_ref: (0, core_ref[0], i, 0)),
                      pl.BlockSpec((N_CHIP, tr, c), lambda i, core_ref: (0, i, 0))],
            out_specs=pl.BlockSpec((N_CHIP, tr, c), lambda i, core_ref: (0, i, 0))),
        out_shape=jax.ShapeDtypeStruct((N_CHIP, r, c), parts.dtype), compiler_params=_cp("parallel"),
    )(core, parts4, half)


def _scatter_start(parts, *, name):
    halves = [lax.empty((N_CHIP,) + v.shape[1:], v.dtype) for v in parts]
    return _split_start(parts + halves, N_CHIP * len(parts), _scatter1_plan(len(parts)), name=name)


def _adamw(parts, w, m, v, *, name):
    r, c = w.shape
    nparts = parts.shape[0]
    tr = max(d for d in range(HALO, 129, HALO) if r % d == 0) if r > 128 else r

    def body(p_ref, w_ref, m_ref, v_ref, g_ref, d_ref, mo_ref, vo_ref):
        g = p_ref[0].astype(F32)
        for k in range(1, nparts):
            g = g + p_ref[k].astype(F32)
        mn = ADAM_B1 * m_ref[...] + (1.0 - ADAM_B1) * g
        vn = ADAM_B2 * v_ref[...] + (1.0 - ADAM_B2) * (g * g)
        m_hat = mn / (1.0 - ADAM_B1 ** ADAM_STEP)
        v_hat = vn / (1.0 - ADAM_B2 ** ADAM_STEP)
        g_ref[...] = g
        d_ref[...] = -ADAM_LR * (m_hat / (jnp.sqrt(v_hat) + ADAM_EPS) + ADAM_WD * w_ref[...])
        mo_ref[...] = mn
        vo_ref[...] = vn

    blk = pl.BlockSpec((tr, c), lambda i: (i, 0))
    return pl.pallas_call(
        body, name=name, grid=(r // tr,), in_specs=[pl.BlockSpec((nparts, tr, c), lambda i: (0, i, 0)), blk, blk, blk],
        out_specs=[blk] * 4, out_shape=[jax.ShapeDtypeStruct((r, c), F32)] * 4, compiler_params=_cp("parallel"),
    )(parts, w, m, v)


def _pack_rows(vs, rows):
    lead = vs[0].shape[:-1] if vs[0].ndim > 1 else ()
    flat = jnp.concatenate(vs, axis=-1)
    pad = rows * LANES - flat.shape[-1]
    flat = jnp.pad(flat, [(0, 0)] * len(lead) + [(0, pad)])
    return flat.reshape(lead + (rows, LANES))


def kernel(x, p, positions, mix_norm_w, w_in, conv_w, conv_b, dt_bias, a_log, d_skip, ssd_norm_w, q_a_norm_w, w_q_b, kv_a_norm_w, w_kv_b, w_out, ffn_norm_w, w_ffn_up, ffn_conv_w, ffn_conv_b, w_ffn_down, ple_norm_w, w_ple_gate, b_ple_gate, w_ple_proj, ple_post_norm_w, final_norm_w, loss_target, m_mix_norm_w, m_w_in, m_conv_w, m_conv_b, m_dt_bias, m_a_log, m_d_skip, m_ssd_norm_w, m_q_a_norm_w, m_w_q_b, m_kv_a_norm_w, m_w_kv_b, m_w_out, m_ffn_norm_w, m_w_ffn_up, m_ffn_conv_w, m_ffn_conv_b, m_w_ffn_down, m_ple_norm_w, m_w_ple_gate, m_b_ple_gate, m_w_ple_proj, m_ple_post_norm_w, m_final_norm_w, v_mix_norm_w, v_w_in, v_conv_w, v_conv_b, v_dt_bias, v_a_log, v_d_skip, v_ssd_norm_w, v_q_a_norm_w, v_w_q_b, v_kv_a_norm_w, v_w_kv_b, v_w_out, v_ffn_norm_w, v_w_ffn_up, v_ffn_conv_w, v_ffn_conv_b, v_w_ffn_down, v_ple_norm_w, v_w_ple_gate, v_b_ple_gate, v_w_ple_proj, v_ple_post_norm_w, v_final_norm_w):
    given = dict(locals())
    shapes = {n: given[n].shape for n in WEIGHTS}
    w2 = {n: given[n].reshape(given[n].shape[-2:] if n in BIG or n in CONV else (1, -1)) for n in WEIGHTS}
    m2 = {n: given['m_' + n].reshape(w2[n].shape) for n in WEIGHTS}
    v2 = {n: given['v_' + n].reshape(w2[n].shape) for n in WEIGHTS}
    me = 4 * lax.axis_index("x") + 2 * lax.axis_index("y") + lax.axis_index("c")

    core = lax.axis_index("c").astype(jnp.int32).reshape(1)

    def shards(grp, zero):
        return [(w2[n] + zero).astype(BF16) if n in BIG else w2[n] + zero for n in WEIGHT_GROUPS[grp]]

    first, token = _gather_start(shards('a', 0.0), name="gather_a_hop1")
    first, token = _gather_forward(first, token, name="gather_a_hop2")
    zero = token[0, 0]
    later = _sequencer_gather(shards('b', zero) + shards('c', zero), collective_id=1, name="gather_later")
    later = dict(zip(WEIGHT_GROUPS['b'] + WEIGHT_GROUPS['c'], later))

    def get_w(grp, after):
        if grp == 'a':
            lands = dict(zip(WEIGHT_GROUPS[grp], _gather_finish(first, token, name="gather_a_done")))
        else:
            lands = {n: later[n] for n in WEIGHT_GROUPS[grp]}
        return _assemble_weights(lands)

    scatters = {}

    hop_ids = {grp: 2 + 2 * i for i, grp in enumerate(GRAD_GROUPS)}

    def zero_of(arrays):
        return sum(v[(0,) * v.ndim].astype(F32) * 0.0 for v in arrays)

    def emit(grp, grads):
        scatters[grp], tok = _scatter_start([grads[n] for n in GRAD_GROUPS[grp]], name="scatter_" + grp + "_hop1")
        return tok[0, 0]

    def relay(grp, after):
        n = len(GRAD_GROUPS[grp])
        bufs = _split_wait(scatters[grp], after, _scatter1_plan(n), lambda refs: [], name="scatter_" + grp + "_hop1_wait")
        sums = [_pair_add(bufs[i], bufs[n + i], core, name="scatter_%s_add%d" % (grp, i)) for i in range(n)]
        scatters[grp] = _sequencer_scatter_hop2(sums, collective_id=hop_ids[grp] + 1, name="scatter_" + grp + "_hop2")
        return zero_of(sums)

    out_g, out_d, out_m, out_v = {}, {}, {}, {}

    def settle(grp):
        return zero_of(scatters[grp])

    def update(grp, behind=None):
        for n, parts in zip(GRAD_GROUPS[grp], scatters[grp]):
            wn = w2[n] if behind is None else w2[n] + behind
            out_g[n], out_d[n], out_m[n], out_v[n] = _adamw(parts, wn, m2[n], v2[n], name="adamw_" + n)

    vecs = {n: w2[n] for n in REPL}
    vecs['mix_norm_w'] = vecs['mix_norm_w'] + zero
    loss, dx, g_conv, g_vec = _local_step(x[0], p[0, 0], _rope_tables(positions), get_w, vecs, loss_target[0], emit,
                                          relay, settle)
    n_small = sum(g_vec[n].shape[1] for n in REPL) + sum(g_conv[n].size for n in CONV) + 1
    rows_small = -(-n_small // (LANES * HALO)) * HALO
    small = _pack_rows([g_vec[n] for n in REPL] + [g_conv[n].reshape(1, -1) for n in CONV] + [loss], rows_small)

    for grp in list(GRAD_GROUPS)[:-1]:
        update(grp)
    all_small = _exchange([small], gather=True, name="gather_small_grads")[0].reshape(N_DEV, rows_small * LANES)
    update(list(GRAD_GROUPS)[-1], zero_of([all_small]))
    pieces, off = [], 0
    for n in REPL:
        k = g_vec[n].shape[1]
        pieces.append(all_small[:, off:off + k])
        off += k
    for n in CONV:
        kw, cols = g_conv[n].shape
        full = all_small[:, off:off + kw * cols].reshape(N_DEV, kw, cols)
        mine = lax.dynamic_slice_in_dim(full, me * (cols // N_DEV), cols // N_DEV, axis=2)
        pieces.append(mine.reshape(N_DEV, kw * (cols // N_DEV)))
        off += kw * cols
    pieces.append(all_small[:, off:off + 1])
    small_names = REPL + CONV
    n_mine = sum(q.shape[1] for q in pieces)
    rows_mine = -(-n_mine // (LANES * HALO)) * HALO
    zero = jnp.zeros((1, 1), F32)
    packed = [_pack_rows([src[n].reshape(1, -1) for n in small_names] + [zero], rows_mine).reshape(rows_mine, LANES)
              for src in (w2, m2, v2)]
    sg, sd, sm, sv = _adamw(_pack_rows(pieces, rows_mine), *packed, name="adamw_small")
    off = 0
    for n in small_names:
        k = w2[n].size
        for dst, src in ((out_g, sg), (out_d, sd), (out_m, sm), (out_v, sv)):
            dst[n] = src.reshape(-1)[off:off + k].reshape(w2[n].shape)
        off += k
    total_loss = sg.reshape(-1)[off]

    outs = [total_loss, dx[None]]
    for res in (out_g, out_d, out_m, out_v):
        outs += [res[n].reshape(shapes[n]) for n in WEIGHTS]
    return tuple(outs)
```

```python
import math

import numpy as np
import jax
import jax.numpy as jnp
from jax import lax
from jax.experimental import pallas as pl
from jax.experimental.pallas import tpu as pltpu
from jax.experimental.pallas import tpu_sc as plsc

F32 = jnp.float32
BF16 = jnp.bfloat16
HI = lax.Precision.HIGHEST

D_MODEL = 2048
CHUNK = 64
D_SSM = 1024
SSD_P = 64
SSD_HEADS = 16
SSD_GROUPS = 2
SSD_N = 128
SSD_CONV = 4
SSD_CONV_DIM = D_SSM + 2 * SSD_GROUPS * SSD_N
MLA_HEADS = 8
MLA_NOPE = 128
MLA_ROPE = 64
MLA_V = 128
MLA_Q_RANK = 512
MLA_KV_RANK = 256
MLA_QK_PAD = 256
ROPE_THETA = 10000.0
D_FF = 5632
FFN_CONV = 3
PLE_DIM = 256
NORM_EPS = 1e-6
ADAM_LR, ADAM_B1, ADAM_B2, ADAM_EPS, ADAM_WD, ADAM_STEP = 0.001, 0.9, 0.999, 1e-08, 0.01, 10
N_DEV = 8

OFF_Z, OFF_XBC, OFF_QA, OFF_CKV, OFF_KR, OFF_DT, D_IN_PAD = 0, 1024, 2560, 3072, 3328, 3456, 3584
D_IN = 3408
LANES = 128
HALO = 8
VMEM_LIMIT = 56 * 1024 * 1024
FFN_TC = D_FF * 2 // N_DEV
FFN_PERM = (0, 4, 1, 5, 2, 6, 3, 7)
NEG = -1e30


def _cp(*sem):
    return pltpu.CompilerParams(dimension_semantics=tuple(sem), vmem_limit_bytes=VMEM_LIMIT)


def _tile(n, want):
    if n <= want:
        return n
    best = max(d for d in range(LANES, want + 1, LANES) if n % d == 0)
    return best


def _sigmoid(x):
    return 0.5 * (jnp.tanh(0.5 * x) + 1.0)


def _silu(x):
    return x * _sigmoid(x)


def _dsilu(x):
    s = _sigmoid(x)
    return s * (1.0 + x * (1.0 - s))


MM_TILE = 1408
MM_TK = 2816


def _matmul(a, b, *, ta=False, tb=False, out_dtype=F32, add=None, bias=None, tm=MM_TILE, tn=MM_TILE, tk=MM_TK, name,
            mnk=None, a_spec=None, b_spec=None, o_spec=None, o_shape=None):
    if mnk is None:
        m, k = (a.shape[1], a.shape[0]) if ta else a.shape
        n = b.shape[0] if tb else b.shape[1]
        assert k == (b.shape[1] if tb else b.shape[0])
    else:
        m, n, k = mnk
    tm, tn, tk = _tile(m, tm), _tile(n, tn), _tile(k, tk)
    nk = k // tk
    dims = (((0 if ta else 1,), (1 if tb else 0,)), ((), ()))

    def body(*refs):
        a_ref, b_ref = refs[0], refs[1]
        pos = 2
        add_ref = bias_ref = None
        if add is not None:
            add_ref = refs[pos]
            pos += 1
        if bias is not None:
            bias_ref = refs[pos]
            pos += 1
        o_ref = refs[pos]
        kk = pl.program_id(2)
        av = a_ref[...]
        bv = b_ref[...]
        av = av.reshape(av.shape[-2:]).astype(BF16)
        bv = bv.reshape(bv.shape[-2:]).astype(BF16)
        prod = lax.dot_general(av, bv, dims, preferred_element_type=F32)

        def finish(r):
            if bias_ref is not None:
                r = r + bias_ref[...]
            if add_ref is not None:
                r = r + add_ref[...].astype(F32)
            o_ref[...] = r.astype(out_dtype).reshape(o_ref.shape)

        if nk == 1:
            finish(prod)
        else:
            acc_ref = refs[pos + 1]

            @pl.when(kk == 0)
            def _():
                acc_ref[...] = prod

            @pl.when(kk > 0)
            def _():
                acc_ref[...] += prod

            @pl.when(kk == nk - 1)
            def _():
                finish(acc_ref[...])

    if a_spec is None:
        a_spec = (pl.BlockSpec((tk, tm), lambda i, j, kk: (kk, i)) if ta
                  else pl.BlockSpec((tm, tk), lambda i, j, kk: (i, kk)))
    if b_spec is None:
        b_spec = (pl.BlockSpec((tn, tk), lambda i, j, kk: (j, kk)) if tb
                  else pl.BlockSpec((tk, tn), lambda i, j, kk: (kk, j)))
    if o_spec is None:
        o_spec = pl.BlockSpec((tm, tn), lambda i, j, kk: (i, j))
    if o_shape is None:
        o_shape = (m, n)
    in_specs = [a_spec, b_spec]
    args = [a, b]
    if add is not None:
        in_specs.append(pl.BlockSpec((tm, tn), lambda i, j, kk: (i, j)))
        args.append(add)
    if bias is not None:
        in_specs.append(pl.BlockSpec((1, tn), lambda i, j, kk: (0, j)))
        args.append(bias)
    return pl.pallas_call(
        body, name=name, grid=(m // tm, n // tn, nk), in_specs=in_specs, out_specs=o_spec,
        out_shape=jax.ShapeDtypeStruct(o_shape, out_dtype),
        scratch_shapes=[pltpu.VMEM((tm, tn), F32)] if nk > 1 else [],
        compiler_params=_cp("parallel", "parallel", "arbitrary"),
    )(*args)


def _rmsnorm_fwd(x, w, *, width, cblk=0, out_dtype=BF16, tr=256, name):
    t = x.shape[0]

    def body(x_ref, w_ref, o_ref):
        xv = x_ref[...].astype(F32)
        r = lax.rsqrt(jnp.mean(xv * xv, axis=-1, keepdims=True) + NORM_EPS)
        o_ref[...] = (xv * r * w_ref[...]).astype(out_dtype)

    return pl.pallas_call(
        body, name=name, grid=(t // tr,),
        in_specs=[pl.BlockSpec((tr, width), lambda i: (i, cblk)), pl.BlockSpec((1, width), lambda i: (0, 0))],
        out_specs=pl.BlockSpec((tr, width), lambda i: (i, 0)),
        out_shape=jax.ShapeDtypeStruct((t, width), out_dtype),
        compiler_params=_cp("parallel"),
    )(x, w)


def _rmsnorm_bwd(x, w, dy, add=None, *, width, cblk=0, out_dtype=F32, tr=256, name):
    t = x.shape[0]

    def body(*refs):
        if add is None:
            x_ref, w_ref, dy_ref, dx_ref, dw_ref = refs
            add_ref = None
        else:
            x_ref, w_ref, dy_ref, add_ref, dx_ref, dw_ref = refs
        xv = x_ref[...].astype(F32)
        dyv = dy_ref[...].astype(F32)
        r = lax.rsqrt(jnp.mean(xv * xv, axis=-1, keepdims=True) + NORM_EPS)
        xh = xv * r
        g = dyv * w_ref[...]
        dx = r * (g - xh * jnp.mean(g * xh, axis=-1, keepdims=True))
        if add_ref is not None:
            dx = dx + add_ref[...].astype(F32)
        dx_ref[...] = dx.astype(out_dtype)

        @pl.when(pl.program_id(0) == 0)
        def _():
            dw_ref[...] = jnp.zeros_like(dw_ref)

        dw_ref[...] += jnp.sum(dyv * xh, axis=0, keepdims=True)

    in_specs = [pl.BlockSpec((tr, width), lambda i: (i, cblk)), pl.BlockSpec((1, width), lambda i: (0, 0)),
                pl.BlockSpec((tr, width), lambda i: (i, 0))]
    args = [x, w, dy]
    if add is not None:
        in_specs.append(pl.BlockSpec((tr, width), lambda i: (i, 0)))
        args.append(add)
    return pl.pallas_call(
        body, name=name, grid=(t // tr,), in_specs=in_specs,
        out_specs=[pl.BlockSpec((tr, width), lambda i: (i, 0)), pl.BlockSpec((1, width), lambda i: (0, 0))],
        out_shape=[jax.ShapeDtypeStruct((t, width), out_dtype), jax.ShapeDtypeStruct((1, width), F32)],
        compiler_params=_cp("arbitrary"),
    )(*args)


def _shift_down(prev_halo, cur, j):
    if j == 0:
        return cur
    ext = jnp.concatenate([prev_halo, cur], axis=0)
    return pltpu.roll(ext, j, axis=0)[HALO:]


def _shift_up(cur, next_halo, j):
    if j == 0:
        return cur
    ext = jnp.concatenate([cur, next_halo], axis=0)
    return pltpu.roll(ext, ext.shape[0] - j, axis=0)[:cur.shape[0]]


def _conv_rows(prev, cur, w, b, kw):
    shifted = [cur]
    out = b + w[kw - 1:kw] * cur
    for j in range(1, kw):
        sh = _shift_down(prev, cur, j)
        shifted.append(sh)
        out = out + w[kw - 1 - j:kw - j] * sh
    return out, shifted


def _act_fwd(c, glu):
    if glu:
        half = c.shape[1] // 2
        return _silu(c[:, :half]) * c[:, half:]
    return _silu(c)


def _act_bwd(c, dout, glu):
    if glu:
        half = c.shape[1] // 2
        g, up = c[:, :half], c[:, half:]
        s = _sigmoid(g)
        gs = g * s
        return jnp.concatenate([dout * up * (s + gs * (1.0 - s)), dout * gs], axis=1)
    return dout * _dsilu(c)


def _conv_act_fwd(u, w, b, *, kw, glu, tc, coff, ncols, out_dtype, tr=256, name):
    t = u.shape[0]
    nb = ncols // tc
    oc = tc // 2 if glu else tc

    def body(u_ref, uh_ref, w_ref, b_ref, o_ref):
        prev = jnp.where(pl.program_id(0) == 0, 0.0, uh_ref[...])
        c, _ = _conv_rows(prev, u_ref[...], w_ref[...], b_ref[...], kw)
        o_ref[...] = _act_fwd(c, glu).astype(out_dtype)

    return pl.pallas_call(
        body, name=name, grid=(t // tr, nb),
        in_specs=[pl.BlockSpec((tr, tc), lambda i, j: (i, j + coff)),
                  pl.BlockSpec((HALO, tc), lambda i, j: (jnp.maximum(i * (tr // HALO) - 1, 0), j + coff)),
                  pl.BlockSpec((kw, tc), lambda i, j: (0, j)), pl.BlockSpec((1, tc), lambda i, j: (0, j))],
        out_specs=pl.BlockSpec((tr, oc), lambda i, j: (i, j)),
        out_shape=jax.ShapeDtypeStruct((t, nb * oc), out_dtype),
        compiler_params=_cp("parallel", "parallel"),
    )(u, u, w, b)


def _conv_act_bwd(u, w, b, dout, *, kw, glu, tc, coff, ncols, tr=256, name):
    t = u.shape[0]
    nb = ncols // tc
    nt = t // tr
    oc = tc // 2 if glu else tc

    def body(u_ref, up_ref, un_ref, d_ref, dn_ref, w_ref, b_ref, du_ref, dw_ref, db_ref):
        i = pl.program_id(1)
        cur, nxt, wv, bv = u_ref[...], un_ref[...], w_ref[...], b_ref[...]
        prev = jnp.where(i == 0, 0.0, up_ref[...])
        c_cur, shifted = _conv_rows(prev, cur, wv, bv, kw)
        c_nxt, _ = _conv_rows(cur[tr - HALO:], nxt, wv, bv, kw)
        d_cur = _act_bwd(c_cur, d_ref[...].astype(F32), glu)
        d_nxt = _act_bwd(c_nxt, jnp.where(i == nt - 1, 0.0, dn_ref[...].astype(F32)), glu)
        du = wv[kw - 1:kw] * d_cur
        for j in range(1, kw):
            du = du + wv[kw - 1 - j:kw - j] * _shift_up(d_cur, d_nxt, j)
        du_ref[...] = du.astype(BF16)

        @pl.when(i == 0)
        def _():
            dw_ref[...] = jnp.zeros_like(dw_ref)
            db_ref[...] = jnp.zeros_like(db_ref)

        db_ref[...] += jnp.sum(d_cur, axis=0, keepdims=True)
        dw_ref[...] += jnp.concatenate(
            [jnp.sum(d_cur * shifted[kw - 1 - k], axis=0, keepdims=True) for k in range(kw)], axis=0)

    nh = tr // HALO
    return pl.pallas_call(
        body, name=name, grid=(nb, nt),
        in_specs=[pl.BlockSpec((tr, tc), lambda j, i: (i, j + coff)),
                  pl.BlockSpec((HALO, tc), lambda j, i: (jnp.maximum(i * nh - 1, 0), j + coff)),
                  pl.BlockSpec((HALO, tc), lambda j, i: (jnp.minimum((i + 1) * nh, t // HALO - 1), j + coff)),
                  pl.BlockSpec((tr, oc), lambda j, i: (i, j)),
                  pl.BlockSpec((HALO, oc), lambda j, i: (jnp.minimum((i + 1) * nh, t // HALO - 1), j)),
                  pl.BlockSpec((kw, tc), lambda j, i: (0, j)), pl.BlockSpec((1, tc), lambda j, i: (0, j))],
        out_specs=[pl.BlockSpec((tr, tc), lambda j, i: (i, j)), pl.BlockSpec((kw, tc), lambda j, i: (0, j)),
                   pl.BlockSpec((1, tc), lambda j, i: (0, j))],
        out_shape=[jax.ShapeDtypeStruct((t, ncols), BF16), jax.ShapeDtypeStruct((kw, ncols), F32),
                   jax.ShapeDtypeStruct((1, ncols), F32)],
        compiler_params=_cp("parallel", "arbitrary"),
    )(u, u, u, dout, dout, w, b)


def _ple_fwd(x2, gl, pe, pw, *, tr=256, name):
    t, d = x2.shape

    def body(x_ref, gl_ref, pe_ref, pw_ref, o_ref):
        pv = pe_ref[...]
        r = lax.rsqrt(jnp.mean(pv * pv, axis=-1, keepdims=True) + NORM_EPS)
        o_ref[...] = x_ref[...] + _sigmoid(gl_ref[...]) * (pv * r * pw_ref[...])

    blk = pl.BlockSpec((tr, d), lambda i: (i, 0))
    return pl.pallas_call(
        body, name=name, grid=(t // tr,), in_specs=[blk, blk, blk, pl.BlockSpec((1, d), lambda i: (0, 0))],
        out_specs=blk, out_shape=jax.ShapeDtypeStruct((t, d), F32), compiler_params=_cp("parallel"),
    )(x2, gl, pe, pw)


def _ple_bwd(dx3, gl, pe, pw, *, tr=256, name):
    t, d = dx3.shape

    def body(dx_ref, gl_ref, pe_ref, pw_ref, dgl_ref, db_ref, dpe_ref, dpw_ref):
        dx, pv, pwv = dx_ref[...], pe_ref[...], pw_ref[...]
        gate = _sigmoid(gl_ref[...])
        r = lax.rsqrt(jnp.mean(pv * pv, axis=-1, keepdims=True) + NORM_EPS)
        ph = pv * r
        dgl = dx * (ph * pwv) * gate * (1.0 - gate)
        de = dx * gate
        g = de * pwv
        dgl_ref[...] = dgl.astype(BF16)
        dpe_ref[...] = (r * (g - ph * jnp.mean(g * ph, axis=-1, keepdims=True))).astype(BF16)

        @pl.when(pl.program_id(0) == 0)
        def _():
            db_ref[...] = jnp.zeros_like(db_ref)
            dpw_ref[...] = jnp.zeros_like(dpw_ref)

        db_ref[...] += jnp.sum(dgl, axis=0, keepdims=True)
        dpw_ref[...] += jnp.sum(de * ph, axis=0, keepdims=True)

    blk = pl.BlockSpec((tr, d), lambda i: (i, 0))
    row = pl.BlockSpec((1, d), lambda i: (0, 0))
    return pl.pallas_call(
        body, name=name, grid=(t // tr,), in_specs=[blk, blk, blk, row], out_specs=[blk, row, blk, row],
        out_shape=[jax.ShapeDtypeStruct((t, d), BF16), jax.ShapeDtypeStruct((1, d), F32),
                   jax.ShapeDtypeStruct((t, d), BF16), jax.ShapeDtypeStruct((1, d), F32)],
        compiler_params=_cp("arbitrary"),
    )(dx3, gl, pe, pw)


def _loss_head(x3, fw, target, *, tr=256, name):
    t, d = x3.shape

    def body(x_ref, w_ref, t_ref, l_ref, dx_ref, dw_ref):
        xv, wv = x_ref[...], w_ref[...]
        r = lax.rsqrt(jnp.mean(xv * xv, axis=-1, keepdims=True) + NORM_EPS)
        xh = xv * r
        err = xh * wv - t_ref[...]
        dy = err * (1.0 / d)
        g = dy * wv
        dx_ref[...] = r * (g - xh * jnp.mean(g * xh, axis=-1, keepdims=True))

        @pl.when(pl.program_id(0) == 0)
        def _():
            l_ref[...] = jnp.zeros_like(l_ref)
            dw_ref[...] = jnp.zeros_like(dw_ref)

        l_ref[...] += 0.5 * jnp.sum(jnp.mean(err * err, axis=-1, keepdims=True), axis=0, keepdims=True)
        dw_ref[...] += jnp.sum(dy * xh, axis=0, keepdims=True)

    blk = pl.BlockSpec((tr, d), lambda i: (i, 0))
    row = pl.BlockSpec((1, d), lambda i: (0, 0))
    return pl.pallas_call(
        body, name=name, grid=(t // tr,), in_specs=[blk, row, blk],
        out_specs=[pl.BlockSpec((1, 1), lambda i: (0, 0)), blk, row],
        out_shape=[jax.ShapeDtypeStruct((1, 1), F32), jax.ShapeDtypeStruct((t, d), F32),
                   jax.ShapeDtypeStruct((1, d), F32)],
        compiler_params=_cp("arbitrary"),
    )(x3, fw, target)


def _rope(blk, tab_ref):
    return blk * tab_ref[0] + pltpu.roll(blk, 96, axis=1) * tab_ref[1] + pltpu.roll(blk, 32, axis=1) * tab_ref[2]


def _unrope(g, tab_ref):
    return g * tab_ref[0] + pltpu.roll(g * tab_ref[1], 32, axis=1) + pltpu.roll(g * tab_ref[2], 96, axis=1)


def _mla_prep(q, kv, proj, tabs, *, tr=512, name):
    t = q.shape[0]

    def body(q_ref, kv_ref, kr_ref, tab_ref, qo_ref, ko_ref, vo_ref, vt_ref):
        qv, kvv = q_ref[...], kv_ref[...]
        qo_ref[0, :, :MLA_NOPE] = qv[:, :MLA_NOPE].astype(BF16)
        qo_ref[0, :, MLA_NOPE:] = _rope(qv[:, MLA_NOPE:], tab_ref).astype(BF16)
        ko_ref[0, :, :MLA_NOPE] = kvv[:, :MLA_NOPE].astype(BF16)
        ko_ref[0, :, MLA_NOPE:] = _rope(kr_ref[...], tab_ref).astype(BF16)
        vo_ref[0] = kvv[:, MLA_NOPE:].astype(BF16)
        for blk in range(tr // ATT_BLK):
            vt_ref[0, blk] = kvv[blk * ATT_BLK:(blk + 1) * ATT_BLK, MLA_NOPE:].T.astype(BF16)

    return pl.pallas_call(
        body, name=name, grid=(t // tr, MLA_HEADS),
        in_specs=[pl.BlockSpec((tr, MLA_QK_PAD), lambda i, h: (i, h)),
                  pl.BlockSpec((tr, MLA_NOPE + MLA_V), lambda i, h: (i, h)),
                  pl.BlockSpec((tr, LANES), lambda i, h: (i, OFF_KR // LANES)),
                  pl.BlockSpec((3, tr, LANES), lambda i, h: (0, i, 0))],
        out_specs=[pl.BlockSpec((1, tr, MLA_QK_PAD), lambda i, h: (h, i, 0)),
                   pl.BlockSpec((1, tr, MLA_QK_PAD), lambda i, h: (h, i, 0)),
                   pl.BlockSpec((1, tr, MLA_V), lambda i, h: (h, i, 0)),
                   pl.BlockSpec((1, tr // ATT_BLK, MLA_V, ATT_BLK), lambda i, h: (h, i, 0, 0))],
        out_shape=[jax.ShapeDtypeStruct((MLA_HEADS, t, MLA_QK_PAD), BF16),
                   jax.ShapeDtypeStruct((MLA_HEADS, t, MLA_QK_PAD), BF16),
                   jax.ShapeDtypeStruct((MLA_HEADS, t, MLA_V), BF16),
                   jax.ShapeDtypeStruct((MLA_HEADS, t // ATT_BLK, MLA_V, ATT_BLK), BF16)],
        compiler_params=_cp("parallel", "parallel"),
    )(q, kv, proj, tabs)


def _mla_unprep(dq3, dk3, dv3, tabs, *, tr=256, name):
    t = dq3.shape[1]

    def body(dq_ref, dk_ref, dv_ref, tab_ref, qo_ref, kvo_ref, kro_ref):
        kr = jnp.zeros((tr, LANES), F32)
        for h in range(MLA_HEADS):
            c0 = h * MLA_QK_PAD
            qo_ref[:, c0:c0 + MLA_NOPE] = dq_ref[h, :, :MLA_NOPE].astype(BF16)
            qo_ref[:, c0 + MLA_NOPE:c0 + MLA_QK_PAD] = _unrope(dq_ref[h, :, MLA_NOPE:], tab_ref).astype(BF16)
            kvo_ref[:, c0:c0 + MLA_NOPE] = dk_ref[h, :, :MLA_NOPE].astype(BF16)
            kvo_ref[:, c0 + MLA_NOPE:c0 + MLA_QK_PAD] = dv_ref[h].astype(BF16)
            kr = kr + dk_ref[h, :, MLA_NOPE:]
        kro_ref[...] = _unrope(kr, tab_ref).astype(BF16)

    return pl.pallas_call(
        body, name=name, grid=(t // tr,),
        in_specs=[pl.BlockSpec((MLA_HEADS, tr, MLA_QK_PAD), lambda i: (0, i, 0)),
                  pl.BlockSpec((MLA_HEADS, tr, MLA_QK_PAD), lambda i: (0, i, 0)),
                  pl.BlockSpec((MLA_HEADS, tr, MLA_V), lambda i: (0, i, 0)),
                  pl.BlockSpec((3, tr, LANES), lambda i: (0, i, 0))],
        out_specs=[pl.BlockSpec((tr, MLA_HEADS * MLA_QK_PAD), lambda i: (i, 0)),
                   pl.BlockSpec((tr, MLA_HEADS * MLA_QK_PAD), lambda i: (i, 0)),
                   pl.BlockSpec((tr, LANES), lambda i: (i, 0))],
        out_shape=[jax.ShapeDtypeStruct((t, MLA_HEADS * MLA_QK_PAD), BF16),
                   jax.ShapeDtypeStruct((t, MLA_HEADS * MLA_QK_PAD), BF16),
                   jax.ShapeDtypeStruct((t, LANES), BF16)],
        compiler_params=_cp("parallel"),
    )(dq3, dk3, dv3, tabs)


ATT_BLK = 512
ATT_SCALE = 1.0 / math.sqrt(MLA_NOPE + MLA_ROPE)
_NT = (((1,), (1,)), ((), ()))
_TN = (((0,), (0,)), ((), ()))


def _att_scores_t(k, q, diagonal):
    s = lax.dot_general(k, q, _NT, preferred_element_type=F32) * ATT_SCALE
    if not diagonal:
        return s
    key = lax.broadcasted_iota(jnp.int32, s.shape, 0)
    query = lax.broadcasted_iota(jnp.int32, s.shape, 1)
    return jnp.where((key >> 6) <= (query >> 6), s, NEG)


def _att_rows(i):
    return pl.ds(pl.multiple_of(i * ATT_BLK, ATT_BLK), ATT_BLK)


ATT_HEADS = 2


def _attn_fwd(q3, k3, vt4, *, name):
    t = q3.shape[1]
    nq = t // ATT_BLK

    def body(q_ref, k_ref, vt_ref, o_ref, lse_ref):
        qi = pl.program_id(1)
        qs = [q_ref[hh] for hh in range(ATT_HEADS)]

        def step(j, carry, diagonal=False):
            out = []
            for hh, (m, l, acc) in enumerate(carry):
                s = _att_scores_t(k_ref[hh, _att_rows(j), :], qs[hh], diagonal)
                m_new = jnp.maximum(m, jnp.max(s, axis=0, keepdims=True))
                p = jnp.exp(s - m_new)
                alpha = jnp.exp(m - m_new)
                l = alpha * l + jnp.sum(p, axis=0, keepdims=True)
                acc = alpha * acc + jnp.dot(vt_ref[hh, j], p.astype(BF16), preferred_element_type=F32)
                out.append((m_new, l, acc))
            return tuple(out)

        init = tuple((jnp.full((1, ATT_BLK), NEG, F32), jnp.zeros((1, ATT_BLK), F32),
                      jnp.zeros((MLA_V, ATT_BLK), F32)) for _ in range(ATT_HEADS))
        done = step(qi, lax.fori_loop(0, qi, step, init), diagonal=True)
        for hh, (m, l, acc) in enumerate(done):
            o_ref[:, hh * MLA_V:(hh + 1) * MLA_V] = (acc / l).T
            lse_ref[hh, 0] = m + jnp.log(l)

    return pl.pallas_call(
        body, name=name, grid=(MLA_HEADS // ATT_HEADS, nq),
        in_specs=[pl.BlockSpec((ATT_HEADS, ATT_BLK, MLA_QK_PAD), lambda h, i: (h, i, 0)),
                  pl.BlockSpec((ATT_HEADS, t, MLA_QK_PAD), lambda h, i: (h, 0, 0)),
                  pl.BlockSpec((ATT_HEADS, nq, MLA_V, ATT_BLK), lambda h, i: (h, 0, 0, 0))],
        out_specs=[pl.BlockSpec((ATT_BLK, ATT_HEADS * MLA_V), lambda h, i: (i, h)),
                   pl.BlockSpec((ATT_HEADS, 1, 1, ATT_BLK), lambda h, i: (h, i, 0, 0))],
        out_shape=[jax.ShapeDtypeStruct((t, MLA_HEADS * MLA_V), F32),
                   jax.ShapeDtypeStruct((MLA_HEADS, nq, 1, ATT_BLK), F32)],
        compiler_params=_cp("parallel", "parallel"),
    )(q3, k3, vt4)


def _attn_bwd(q3, k3, v3, o, dcat, lse, *, name):
    t = q3.shape[1]
    nq = t // ATT_BLK
    wide = ATT_HEADS * MLA_V

    def body(q_ref, k_ref, v_ref, o_ref, do_ref, lse_ref, dq_ref, dk_ref, dv_ref, delta_ref):
        kj = pl.program_id(1)

        @pl.when(kj == 0)
        def _():
            dq_ref[...] = jnp.zeros_like(dq_ref)
            ones = jnp.ones((HALO, MLA_V), F32)
            for i in range(nq):
                rows = pl.ds(i * ATT_BLK, ATT_BLK)
                prod = o_ref[rows, :] * do_ref[rows, :]
                for hh in range(ATT_HEADS):
                    delta_ref[hh, i] = lax.dot_general(ones, prod[:, hh * MLA_V:(hh + 1) * MLA_V], _NT, precision=HI,
                                                       preferred_element_type=F32)

        def step(i, carry, diagonal=False):
            rows = _att_rows(i)
            out = []
            for hh, (dk, dv) in enumerate(carry):
                k, v = k_ref[hh], v_ref[hh]
                q = q_ref[hh, rows, :]
                dob = do_ref[rows, hh * MLA_V:(hh + 1) * MLA_V].astype(BF16)
                p = jnp.exp(_att_scores_t(k, q, diagonal) - lse_ref[hh, i])
                dv = dv + jnp.dot(p.astype(BF16), dob, preferred_element_type=F32)
                dp = lax.dot_general(v, dob, _NT, preferred_element_type=F32)
                ds = (p * (dp - delta_ref[hh, i, 0:1, :]) * ATT_SCALE).astype(BF16)
                dk = dk + jnp.dot(ds, q, preferred_element_type=F32)
                dq_ref[hh, rows, :] += lax.dot_general(ds, k, _TN, preferred_element_type=F32)
                out.append((dk, dv))
            return tuple(out)

        init = tuple((jnp.zeros((ATT_BLK, MLA_QK_PAD), F32), jnp.zeros((ATT_BLK, MLA_V), F32))
                     for _ in range(ATT_HEADS))
        done = lax.fori_loop(kj + 1, nq, step, step(kj, init, diagonal=True))
        for hh, (dk, dv) in enumerate(done):
            dk_ref[hh] = dk
            dv_ref[hh] = dv

    return pl.pallas_call(
        body, name=name, grid=(MLA_HEADS // ATT_HEADS, nq),
        in_specs=[pl.BlockSpec((ATT_HEADS, t, MLA_QK_PAD), lambda h, j: (h, 0, 0)),
                  pl.BlockSpec((ATT_HEADS, ATT_BLK, MLA_QK_PAD), lambda h, j: (h, j, 0)),
                  pl.BlockSpec((ATT_HEADS, ATT_BLK, MLA_V), lambda h, j: (h, j, 0)),
                  pl.BlockSpec((t, wide), lambda h, j: (0, h)),
                  pl.BlockSpec((t, wide), lambda h, j: (0, MLA_HEADS // ATT_HEADS + h)),
                  pl.BlockSpec((ATT_HEADS, nq, 1, ATT_BLK), lambda h, j: (h, 0, 0, 0))],
        out_specs=[pl.BlockSpec((ATT_HEADS, t, MLA_QK_PAD), lambda h, j: (h, 0, 0)),
                   pl.BlockSpec((ATT_HEADS, ATT_BLK, MLA_QK_PAD), lambda h, j: (h, j, 0)),
                   pl.BlockSpec((ATT_HEADS, ATT_BLK, MLA_V), lambda h, j: (h, j, 0))],
        out_shape=[jax.ShapeDtypeStruct((MLA_HEADS, t, MLA_QK_PAD), F32),
                   jax.ShapeDtypeStruct((MLA_HEADS, t, MLA_QK_PAD), F32),
                   jax.ShapeDtypeStruct((MLA_HEADS, t, MLA_V), F32)],
        scratch_shapes=[pltpu.VMEM((ATT_HEADS, nq, HALO, ATT_BLK), F32)],
        compiler_params=_cp("parallel", "arbitrary"),
    )(q3, k3, v3, o, dcat, lse)


def _ssd_prep(proj, bias128, alog128, *, name):
    t = proj.shape[0]
    nc = t // CHUNK

    def body(raw_ref, b_ref, al_ref, dt_ref, cs_ref, a_ref):
        xv = raw_ref[...] + b_ref[...]
        dt = jnp.maximum(xv, 0.0) + jnp.log(1.0 + jnp.exp(-jnp.abs(xv)))
        a = -jnp.exp(al_ref[...])
        adt = (dt * a).reshape(nc, CHUNK, LANES)
        li = lax.broadcasted_iota(jnp.int32, (nc, CHUNK, CHUNK), 1)
        si = lax.broadcasted_iota(jnp.int32, (nc, CHUNK, CHUNK), 2)
        tril = jnp.where(si <= li, 1.0, 0.0).astype(F32)
        cs = lax.dot_general(tril, adt, (((2,), (1,)), ((0,), (0,))), precision=HI, preferred_element_type=F32)
        dt_ref[...] = dt
        cs_ref[...] = cs.reshape(t, LANES)
        a_ref[...] = a

    blk = pl.BlockSpec((t, LANES), lambda i: (0, 0))
    row = pl.BlockSpec((1, LANES), lambda i: (0, 0))
    return pl.pallas_call(
        body, name=name, grid=(1,),
        in_specs=[pl.BlockSpec((t, LANES), lambda i: (0, OFF_DT // LANES)), row, row],
        out_specs=[blk, blk, row],
        out_shape=[jax.ShapeDtypeStruct((t, LANES), F32), jax.ShapeDtypeStruct((t, LANES), F32),
                   jax.ShapeDtypeStruct((1, LANES), F32)],
        compiler_params=_cp("arbitrary"),
    )(proj, bias128, alog128)


def _ssd_prep_bwd(ddt128, dadt128, proj, bias128, dt128, a128, dd_h, *, name):
    t = proj.shape[0]

    def body(ddt_ref, dadt_ref, raw_ref, b_ref, dt_ref, a_ref, dd_ref, draw_ref, db_ref, dal_ref, dds_ref):
        draw = ddt_ref[...] * _sigmoid(raw_ref[...] + b_ref[...])
        draw_ref[...] = draw.astype(BF16)
        db_ref[...] = jnp.sum(draw, axis=0, keepdims=True)
        dal_ref[...] = jnp.sum(dadt_ref[...] * dt_ref[...], axis=0, keepdims=True) * a_ref[...]
        dds_ref[...] = jnp.sum(dd_ref[...], axis=-1, keepdims=True)

    blk = pl.BlockSpec((t, LANES), lambda i: (0, 0))
    row = pl.BlockSpec((1, LANES), lambda i: (0, 0))
    return pl.pallas_call(
        body, name=name, grid=(1,),
        in_specs=[blk, blk, pl.BlockSpec((t, LANES), lambda i: (0, OFF_DT // LANES)), row, blk, row,
                  pl.BlockSpec((SSD_HEADS, SSD_P), lambda i: (0, 0))],
        out_specs=[blk, row, row, pl.BlockSpec((SSD_HEADS, 1), lambda i: (0, 0))],
        out_shape=[jax.ShapeDtypeStruct((t, LANES), BF16), jax.ShapeDtypeStruct((1, LANES), F32),
                   jax.ShapeDtypeStruct((1, LANES), F32), jax.ShapeDtypeStruct((SSD_HEADS, 1), F32)],
        compiler_params=_cp("arbitrary"),
    )(ddt128, dadt128, proj, bias128, dt128, a128, dd_h)


def _bdot(a, b, ca, cb, precision=None):
    return lax.dot_general(a, b, (((ca,), (cb,)), ((0,), (0,))), precision=precision, preferred_element_type=F32)


def _head_matrices():
    eye, zero = jnp.eye(SSD_P, dtype=F32), jnp.zeros((SSD_P, SSD_P), F32)
    pick = jnp.stack([jnp.concatenate([eye, zero], axis=0), jnp.concatenate([zero, eye], axis=0)])
    return pick, pick.transpose(0, 2, 1)


def _move(x, sel):
    selb = sel.astype(BF16)
    hi = x.astype(BF16)
    rest = x - hi.astype(F32)
    mid = rest.astype(BF16)
    low = (rest - mid.astype(F32)).astype(BF16)
    out = jnp.dot(hi, selb, preferred_element_type=F32)
    out = out + jnp.dot(mid, selb, preferred_element_type=F32)
    return out + jnp.dot(low, selb, preferred_element_type=F32)


def _pick_head(pair_ref, pick_ref, h):
    return _move(pair_ref[...], pick_ref[h % 2])


def _place_head(out_ref, val, place_ref, h):
    wide = _move(val, place_ref[h % 2])

    @pl.when(h % 2 == 0)
    def _():
        out_ref[...] = wide

    @pl.when(h % 2 == 1)
    def _():
        out_ref[...] += wide


def _ssd_common(x2, dt_ref, cs_ref, csr_ref, b_ref, c_ref, nc):
    x = x2.reshape(nc, CHUNK, SSD_P)
    dt = dt_ref[0].reshape(nc, CHUNK, SSD_P)
    cs = cs_ref[0].reshape(nc, CHUNK, SSD_P)
    csr = csr_ref[0]
    bm = b_ref[...].reshape(nc, CHUNK, SSD_N).astype(BF16)
    cm = c_ref[...].reshape(nc, CHUNK, SSD_N).astype(BF16)
    li = lax.broadcasted_iota(jnp.int32, (nc, CHUNK, CHUNK), 1)
    si = lax.broadcasted_iota(jnp.int32, (nc, CHUNK, CHUNK), 2)
    lmat = jnp.exp(jnp.where(si <= li, cs - csr, NEG))
    g = _bdot(cm, bm, 2, 2)
    cs_last = jnp.sum(jnp.where(li == CHUNK - 1, cs, 0.0), axis=1, keepdims=True)
    xdt = x * dt
    dec = jnp.exp(cs_last - cs)
    return x, dt, cs, bm, cm, li, si, lmat, g, cs_last, xdt, dec


def _ssd_fwd(xbc, dt_h, cs_h, cs_row, dskip_h, *, name):
    t = xbc.shape[0]
    nc = t // CHUNK
    hpg = SSD_HEADS // SSD_GROUPS
    pick, place = _head_matrices()

    def body(xs_ref, dt_ref, cs_ref, csr_ref, b_ref, c_ref, dk_ref, pick_ref, place_ref, y_ref, st_ref, sc_ref, cd_ref):
        h = pl.program_id(0)
        x, dt, cs, bm, cm, li, si, lmat, g, cs_last, xdt, dec = _ssd_common(_pick_head(xs_ref, pick_ref, h), dt_ref,
                                                                           cs_ref, csr_ref, b_ref, c_ref, nc)
        yd = _bdot((g * lmat).astype(BF16), xdt.astype(BF16), 2, 1)
        sc_ref[...] = _bdot(bm, (dec * xdt).astype(BF16), 1, 1)
        cd_ref[...] = jnp.exp(cs_last)

        def step(c, s):
            st_ref[0, c] = s
            return s * cd_ref[c] + sc_ref[c]

        lax.fori_loop(0, nc, step, jnp.zeros((SSD_N, SSD_P), F32))
        yo = _bdot(cm, st_ref[0].astype(BF16), 2, 1) * jnp.exp(cs)
        _place_head(y_ref, (yd + yo + dk_ref[0] * x).reshape(t, SSD_P), place_ref, h)

    head = pl.BlockSpec((1, t, SSD_P), lambda h: (h, 0, 0))
    pair = pl.BlockSpec((t, 2 * SSD_P), lambda h: (0, h // 2))
    nxb = D_SSM // SSD_N
    return pl.pallas_call(
        body, name=name, grid=(SSD_HEADS,),
        in_specs=[pair, head, head, pl.BlockSpec((1, nc, 1, CHUNK), lambda h: (h, 0, 0, 0)),
                  pl.BlockSpec((t, SSD_N), lambda h: (0, nxb + h // hpg)),
                  pl.BlockSpec((t, SSD_N), lambda h: (0, nxb + SSD_GROUPS + h // hpg)),
                  pl.BlockSpec((1, 1, SSD_P), lambda h: (h, 0, 0)),
                  pl.BlockSpec((2, 2 * SSD_P, SSD_P), lambda h: (0, 0, 0)),
                  pl.BlockSpec((2, SSD_P, 2 * SSD_P), lambda h: (0, 0, 0))],
        out_specs=[pair, pl.BlockSpec((1, nc, SSD_N, SSD_P), lambda h: (h, 0, 0, 0))],
        out_shape=[jax.ShapeDtypeStruct((t, D_SSM), F32),
                   jax.ShapeDtypeStruct((SSD_HEADS, nc, SSD_N, SSD_P), F32)],
        scratch_shapes=[pltpu.VMEM((nc, SSD_N, SSD_P), F32), pltpu.VMEM((nc, 1, SSD_P), F32)],
        compiler_params=_cp("arbitrary"),
    )(xbc, dt_h, cs_h, cs_row, xbc, xbc, dskip_h, pick, place)


def _ssd_bwd(xbc, dt_h, cs_h, cs_row, dskip_h, a_h, states, dy, *, name):
    t = xbc.shape[0]
    nc = t // CHUNK
    hpg = SSD_HEADS // SSD_GROUPS
    pick, place = _head_matrices()

    def body(xs_ref, dt_ref, cs_ref, csr_ref, b_ref, c_ref, dk_ref, a_ref, st_ref, dy_ref, pick_ref, place_ref,
             dxs_ref, ddt_ref, dadt_ref, db_ref, dc_ref, dd_ref, dsl_ref, dsc_ref, cd_ref):
        h = pl.program_id(0) * hpg + pl.program_id(1)
        x, dt, cs, bm, cm, li, si, lmat, g, cs_last, xdt, dec = _ssd_common(_pick_head(xs_ref, pick_ref, h), dt_ref,
                                                                           cs_ref, csr_ref, b_ref, c_ref, nc)
        dy = _pick_head(dy_ref, pick_ref, h).reshape(nc, CHUNK, SSD_P)
        dyb = dy.astype(BF16)
        xdtb = xdt.astype(BF16)
        sprev = st_ref[0]
        sprevb = sprev.astype(BF16)
        cdec = jnp.exp(cs_last)
        ecs = jnp.exp(cs)
        dw = (ecs * dy).astype(BF16)
        wmat = _bdot(cm, sprevb, 2, 1)
        dcs = jnp.sum(dy * ecs * wmat, axis=2, keepdims=True)
        dcm = _bdot(dw, sprevb, 2, 2)
        dsl_ref[...] = _bdot(cm, dw, 1, 1)
        cd_ref[...] = cdec

        def step(k, ds):
            c = nc - 1 - k
            dsc_ref[c] = ds
            return ds * cd_ref[c] + dsl_ref[c]

        lax.fori_loop(0, nc, step, jnp.zeros((SSD_N, SSD_P), F32))
        dsc = dsc_ref[...]
        dscb = dsc.astype(BF16)
        d_last = jnp.sum(jnp.sum(dsc * sprev, axis=1, keepdims=True) * cdec, axis=2, keepdims=True)
        z = dec * xdt
        dbm = _bdot(z.astype(BF16), dscb, 2, 2)
        dz = _bdot(bm, dscb, 2, 1)
        dxdt = dec * dz
        t2 = jnp.sum(dz * z, axis=2, keepdims=True)
        dcs = dcs - t2
        d_last = d_last + jnp.sum(t2, axis=1, keepdims=True)
        m = g * lmat
        mb = m.astype(BF16)
        dm = _bdot(dyb, xdtb, 2, 2)
        dxdt = dxdt + _bdot(mb, dyb, 1, 1)
        dseg = dm * m
        dcs = dcs + jnp.sum(dseg, axis=2, keepdims=True)
        ones = jnp.ones((nc, CHUNK, SSD_P), F32)
        dcs = dcs - _bdot(dseg, ones, 1, 1, precision=HI)
        dg = (dm * lmat).astype(BF16)
        dcm = dcm + _bdot(dg, bm, 2, 1)
        dbm = dbm + _bdot(dg, cm, 1, 1)
        dcs = dcs + jnp.where(li[:, :, :SSD_P] == CHUNK - 1, d_last, 0.0)
        triu = jnp.where(li <= si, 1.0, 0.0).astype(F32)
        dadt = _bdot(triu, dcs, 2, 1, precision=HI)
        dk = dk_ref[0]
        _place_head(dxs_ref, (dxdt * dt + dk * dy).reshape(t, SSD_P), place_ref, h)
        ddt = jnp.sum(dxdt * x, axis=2, keepdims=True) + dadt * a_ref[0]
        mine = lax.broadcasted_iota(jnp.int32, (t, LANES), 1) == h

        @pl.when(h == 0)
        def _():
            ddt_ref[...] = jnp.zeros_like(ddt_ref)
            dadt_ref[...] = jnp.zeros_like(dadt_ref)

        ddt_ref[...] += jnp.where(mine, jnp.max(ddt, axis=2, keepdims=True).reshape(t, 1), 0.0)
        dadt_ref[...] += jnp.where(mine, jnp.max(dadt, axis=2, keepdims=True).reshape(t, 1), 0.0)
        dd_ref[0] = jnp.sum(jnp.sum(dy * x, axis=1, keepdims=True), axis=0)

        @pl.when(pl.program_id(1) == 0)
        def _():
            db_ref[...] = jnp.zeros_like(db_ref)
            dc_ref[...] = jnp.zeros_like(dc_ref)

        db_ref[...] += dbm.reshape(t, SSD_N)
        dc_ref[...] += dcm.reshape(t, SSD_N)

    head = pl.BlockSpec((1, t, SSD_P), lambda gi, hi: (gi * hpg + hi, 0, 0))
    pair = pl.BlockSpec((t, 2 * SSD_P), lambda gi, hi: (0, (gi * hpg + hi) // 2))
    grp = pl.BlockSpec((t, SSD_N), lambda gi, hi: (0, gi))
    lane = pl.BlockSpec((1, 1, SSD_P), lambda gi, hi: (gi * hpg + hi, 0, 0))
    rows = pl.BlockSpec((t, LANES), lambda gi, hi: (0, 0))
    nxb = D_SSM // SSD_N
    dxs, ddt, dadt, db, dc, dd = pl.pallas_call(
        body, name=name, grid=(SSD_GROUPS, hpg),
        in_specs=[pair, head, head, pl.BlockSpec((1, nc, 1, CHUNK), lambda gi, hi: (gi * hpg + hi, 0, 0, 0)),
                  pl.BlockSpec((t, SSD_N), lambda gi, hi: (0, nxb + gi)),
                  pl.BlockSpec((t, SSD_N), lambda gi, hi: (0, nxb + SSD_GROUPS + gi)), lane, lane,
                  pl.BlockSpec((1, nc, SSD_N, SSD_P), lambda gi, hi: (gi * hpg + hi, 0, 0, 0)), pair,
                  pl.BlockSpec((2, 2 * SSD_P, SSD_P), lambda gi, hi: (0, 0, 0)),
                  pl.BlockSpec((2, SSD_P, 2 * SSD_P), lambda gi, hi: (0, 0, 0))],
        out_specs=[pair, rows, rows, grp, grp, lane],
        out_shape=[jax.ShapeDtypeStruct((t, D_SSM), F32)] + [jax.ShapeDtypeStruct((t, LANES), F32)] * 2
        + [jax.ShapeDtypeStruct((t, SSD_GROUPS * SSD_N), F32)] * 2
        + [jax.ShapeDtypeStruct((SSD_HEADS, 1, SSD_P), F32)],
        scratch_shapes=[pltpu.VMEM((nc, SSD_N, SSD_P), F32), pltpu.VMEM((nc, SSD_N, SSD_P), F32),
                        pltpu.VMEM((nc, 1, SSD_P), F32)],
        compiler_params=_cp("arbitrary", "arbitrary"),
    )(xbc, dt_h, cs_h, cs_row, xbc, xbc, dskip_h, a_h, states, dy, pick, place)
    return jnp.concatenate([dxs, db, dc], axis=1), ddt, dadt, dd


def _ssd_gate_fwd(y, proj, w, *, tr=256, name):
    t = y.shape[0]
    gw = D_SSM // SSD_GROUPS

    def body(y_ref, z_ref, w_ref, o_ref):
        v = y_ref[...] * _silu(z_ref[...])
        for gi in range(SSD_GROUPS):
            vg = v[:, gi * gw:(gi + 1) * gw]
            r = lax.rsqrt(jnp.mean(vg * vg, axis=-1, keepdims=True) + NORM_EPS)
            o_ref[:, gi * gw:(gi + 1) * gw] = (vg * r * w_ref[:, gi * gw:(gi + 1) * gw]).astype(BF16)

    blk = pl.BlockSpec((tr, D_SSM), lambda i: (i, 0))
    return pl.pallas_call(
        body, name=name, grid=(t // tr,), in_specs=[blk, blk, pl.BlockSpec((1, D_SSM), lambda i: (0, 0))],
        out_specs=blk, out_shape=jax.ShapeDtypeStruct((t, D_SSM), BF16), compiler_params=_cp("parallel"),
    )(y, proj, w)


def _ssd_gate_bwd(y, proj, w, dcat, *, tr=256, name):
    t = y.shape[0]
    gw = D_SSM // SSD_GROUPS

    def body(y_ref, z_ref, w_ref, d_ref, dy_ref, dz_ref, dw_ref):
        yv, zv, dv = y_ref[...], z_ref[...], d_ref[...].astype(F32)
        sz = _silu(zv)
        v = yv * sz

        @pl.when(pl.program_id(0) == 0)
        def _():
            dw_ref[...] = jnp.zeros_like(dw_ref)

        for gi in range(SSD_GROUPS):
            sl = slice(gi * gw, (gi + 1) * gw)
            vg, dg = v[:, sl], dv[:, sl]
            r = lax.rsqrt(jnp.mean(vg * vg, axis=-1, keepdims=True) + NORM_EPS)
            vh = vg * r
            gg = dg * w_ref[:, sl]
            dvg = r * (gg - vh * jnp.mean(gg * vh, axis=-1, keepdims=True))
            dy_ref[:, sl] = dvg * sz[:, sl]
            dz_ref[:, sl] = (dvg * yv[:, sl] * _dsilu(zv[:, sl])).astype(BF16)
            dw_ref[:, sl] += jnp.sum(dg * vh, axis=0, keepdims=True)

    blk = pl.BlockSpec((tr, D_SSM), lambda i: (i, 0))
    row = pl.BlockSpec((1, D_SSM), lambda i: (0, 0))
    return pl.pallas_call(
        body, name=name, grid=(t // tr,), in_specs=[blk, blk, row, blk], out_specs=[blk, blk, row],
        out_shape=[jax.ShapeDtypeStruct((t, D_SSM), F32), jax.ShapeDtypeStruct((t, D_SSM), BF16),
                   jax.ShapeDtypeStruct((1, D_SSM), F32)],
        compiler_params=_cp("arbitrary"),
    )(y, proj, w, dcat)


def _pad_lanes(v):
    return jnp.pad(v, ((0, 0), (0, LANES - v.shape[1])))


def _per_head(v128, t):
    return jnp.broadcast_to(v128[:, :SSD_HEADS].T[:, :, None], (SSD_HEADS, t, SSD_P))


def _ssd_forward(proj, conv_w, conv_b, dt_bias, a_log, d_skip, ssd_norm_w):
    t = proj.shape[0]
    nc = t // CHUNK
    xbc = _conv_act_fwd(proj, conv_w, conv_b, kw=SSD_CONV, glu=False, tc=512, coff=OFF_XBC // 512,
                        ncols=SSD_CONV_DIM, out_dtype=F32, name="ssd_conv_fwd")
    bias128, alog128 = _pad_lanes(dt_bias), _pad_lanes(a_log)
    dt128, cs128, a128 = _ssd_prep(proj, bias128, alog128, name="ssd_prep")
    dt_h, cs_h = _per_head(dt128, t), _per_head(cs128, t)
    cs_row = cs128[:, :SSD_HEADS].T.reshape(SSD_HEADS, nc, 1, CHUNK)
    dskip_h = jnp.broadcast_to(d_skip[0][:, None, None], (SSD_HEADS, 1, SSD_P))
    a_h = jnp.broadcast_to(a128[0, :SSD_HEADS][:, None, None], (SSD_HEADS, 1, SSD_P))
    y, states = _ssd_fwd(xbc, dt_h, cs_h, cs_row, dskip_h, name="ssd_scan_fwd")
    y_ssd = _ssd_gate_fwd(y, proj, ssd_norm_w, name="ssd_gate_fwd")
    saved = (proj, conv_w, conv_b, ssd_norm_w, bias128, dt128, a128, dt_h, cs_h, cs_row, xbc, dskip_h, a_h, states, y)
    return y_ssd, saved


def _ssd_backward(saved, dcat):
    proj, conv_w, conv_b, ssd_norm_w, bias128, dt128, a128, dt_h, cs_h, cs_row, xbc, dskip_h, a_h, states, y = saved
    dy, dz, d_norm_w = _ssd_gate_bwd(y, proj, ssd_norm_w, dcat, name="ssd_gate_bwd")
    dxc, ddt128, dadt128, dd_h = _ssd_bwd(xbc, dt_h, cs_h, cs_row, dskip_h, a_h, states, dy, name="ssd_scan_bwd")
    dxbc, d_conv_w, d_conv_b = _conv_act_bwd(proj, conv_w, conv_b, dxc, kw=SSD_CONV, glu=False, tc=512,
                                             coff=OFF_XBC // 512, ncols=SSD_CONV_DIM, name="ssd_conv_bwd")
    d_raw, d_bias, d_alog, d_dskip = _ssd_prep_bwd(ddt128, dadt128, proj, bias128, dt128, a128,
                                                   dd_h.reshape(SSD_HEADS, SSD_P), name="ssd_prep_bwd")
    return (dz, dxbc, d_raw, d_norm_w, d_conv_w, d_conv_b, d_bias[:, :SSD_HEADS], d_alog[:, :SSD_HEADS],
            d_dskip.reshape(1, SSD_HEADS))


def _rope_tables(positions):
    inv_freq = ROPE_THETA ** (-jnp.arange(0, MLA_ROPE, 2, dtype=F32) / MLA_ROPE)
    ang = positions[0].astype(F32)[:, None] * inv_freq
    cos, sin = jnp.cos(ang), jnp.sin(ang)
    z = jnp.zeros_like(cos)
    return jnp.stack([jnp.concatenate([cos, cos, z, z], axis=1), jnp.concatenate([-sin, z, z, z], axis=1),
                      jnp.concatenate([z, sin, z, z], axis=1)])


def _mla_forward(proj, tabs, q_a_norm_w, wq_pad, kv_a_norm_w, wkv):
    qn = _rmsnorm_fwd(proj, q_a_norm_w, width=MLA_Q_RANK, cblk=OFF_QA // MLA_Q_RANK, name="q_a_norm")
    q = _matmul(qn, wq_pad, name="q_b_proj")
    kvn = _rmsnorm_fwd(proj, kv_a_norm_w, width=MLA_KV_RANK, cblk=OFF_CKV // MLA_KV_RANK, name="kv_a_norm")
    kv = _matmul(kvn, wkv, name="kv_b_proj")
    q3, k3, v3, vt4 = _mla_prep(q, kv, proj, tabs, name="mla_prep")
    o, lse = _attn_fwd(q3, k3, vt4, name="attn_fwd")
    return o, (proj, tabs, q_a_norm_w, wq_pad, kv_a_norm_w, wkv, qn, kvn, q3, k3, v3, o, lse)


def _mla_backward(saved, dcat):
    proj, tabs, q_a_norm_w, wq_pad, kv_a_norm_w, wkv, qn, kvn, q3, k3, v3, o, lse = saved
    dq3, dk3, dv3 = _attn_bwd(q3, k3, v3, o, dcat, lse, name="attn_bwd")
    dq, dkv, dkr = _mla_unprep(dq3, dk3, dv3, tabs, name="mla_unprep")
    d_wq = _matmul(qn, dq, ta=True, out_dtype=BF16, name="d_w_q_b")
    dqn = _matmul(dq, wq_pad, tb=True, name="d_qn")
    dq_a, d_qnw = _rmsnorm_bwd(proj, q_a_norm_w, dqn, width=MLA_Q_RANK, cblk=OFF_QA // MLA_Q_RANK, out_dtype=BF16,
                               name="q_a_norm_bwd")
    d_wkv = _matmul(kvn, dkv, ta=True, out_dtype=BF16, name="d_w_kv_b")
    dkvn = _matmul(dkv, wkv, tb=True, name="d_kvn")
    dckv, d_kvnw = _rmsnorm_bwd(proj, kv_a_norm_w, dkvn, width=MLA_KV_RANK, cblk=OFF_CKV // MLA_KV_RANK,
                                out_dtype=BF16, name="kv_a_norm_bwd")
    return dq_a, dckv, dkr, d_wq, d_wkv, d_qnw, d_kvnw


def _pad_w_q(w):
    r = w.shape[0]
    w3 = w.reshape(r, MLA_HEADS, MLA_NOPE + MLA_ROPE)
    return jnp.pad(w3, ((0, 0), (0, 0), (0, MLA_QK_PAD - MLA_NOPE - MLA_ROPE))).reshape(r, MLA_HEADS * MLA_QK_PAD)


def _unpad_w_q(w):
    r = w.shape[0]
    return w.reshape(r, MLA_HEADS, MLA_QK_PAD)[:, :, :MLA_NOPE + MLA_ROPE].reshape(r, MLA_HEADS * (MLA_NOPE + MLA_ROPE))


W_IN_SEGMENTS = ((0, D_SSM + SSD_CONV_DIM, 0), (D_SSM + SSD_CONV_DIM, D_SSM + SSD_CONV_DIM + SSD_HEADS, OFF_DT),
                 (D_SSM + SSD_CONV_DIM + SSD_HEADS, D_IN - MLA_ROPE, OFF_QA), (D_IN - MLA_ROPE, D_IN, OFF_KR))


def _pad_w_in_shards(g):
    n = g.shape[2]
    pieces, at = [], 0
    for lo, hi, start in sorted(W_IN_SEGMENTS, key=lambda seg: seg[2]):
        if start > at:
            pieces.append(jnp.zeros((g.shape[1], start - at), g.dtype))
        for j in range(N_DEV):
            a, b = max(lo, j * n), min(hi, (j + 1) * n)
            if a < b:
                pieces.append(g[j][:, a - j * n:b - j * n])
        at = start + hi - lo
    pieces.append(jnp.zeros((g.shape[1], D_IN_PAD - at), g.dtype))
    return jnp.concatenate(pieces, axis=1)


def _unpad_w_in_shards(w):
    n = D_IN // N_DEV
    shards = []
    for j in range(N_DEV):
        pieces = []
        for lo, hi, start in W_IN_SEGMENTS:
            a, b = max(lo, j * n), min(hi, (j + 1) * n)
            if a < b:
                pieces.append(w[:, start + a - lo:start + b - lo])
        shards.append(jnp.concatenate(pieces, axis=1) if len(pieces) > 1 else pieces[0])
    return jnp.stack(shards)


WEIGHTS = ['mix_norm_w', 'w_in', 'conv_w', 'conv_b', 'dt_bias', 'a_log', 'd_skip', 'ssd_norm_w', 'q_a_norm_w', 'w_q_b',
           'kv_a_norm_w', 'w_kv_b', 'w_out', 'ffn_norm_w', 'w_ffn_up', 'ffn_conv_w', 'ffn_conv_b', 'w_ffn_down',
           'ple_norm_w', 'w_ple_gate', 'b_ple_gate', 'w_ple_proj', 'ple_post_norm_w', 'final_norm_w']
BIG = ['w_in', 'w_q_b', 'w_kv_b', 'w_out', 'w_ffn_up', 'w_ffn_down', 'w_ple_gate', 'w_ple_proj']
COL_SHARDED = ('w_in', 'w_q_b', 'w_kv_b', 'w_ffn_up', 'w_ple_proj')
CONV = ['conv_w', 'ffn_conv_w']
REPL = [n for n in WEIGHTS if n not in BIG and n not in CONV]
FFN_INV = tuple(int(i) for i in np.argsort(FFN_PERM))


def _cat_cols(g):
    return jnp.concatenate([g[j] for j in range(N_DEV)], axis=1)


def _split_cols(w):
    n = w.shape[1] // N_DEV
    return jnp.stack([w[:, j * n:(j + 1) * n] for j in range(N_DEV)])


def _interleave(v):
    r = v.shape[0]
    return v.reshape(r, N_DEV, FFN_TC)[:, jnp.array(FFN_PERM)].reshape(r, N_DEV * FFN_TC)


def _deinterleave(v):
    r = v.shape[0]
    return v.reshape(r, N_DEV, FFN_TC)[:, jnp.array(FFN_INV)].reshape(r, N_DEV * FFN_TC)


def _assemble_weights(g):
    layout = {
        'w_in': _pad_w_in_shards,
        'w_q_b': lambda v: _pad_w_q(_cat_cols(v)),
        'w_kv_b': _cat_cols,
        'w_out': lambda v: v.reshape(D_MODEL, D_MODEL),
        'w_ffn_up': lambda v: v,
        'w_ffn_down': lambda v: v.reshape(D_FF, D_MODEL),
        'w_ple_gate': lambda v: v.reshape(D_MODEL, D_MODEL),
        'w_ple_proj': _cat_cols,
        'conv_w': _cat_cols,
        'ffn_conv_w': lambda v: _interleave(_cat_cols(v)),
    }
    return {n: layout[n](v) for n, v in g.items()}


WEIGHT_GROUPS = {'a': ['w_in', 'w_q_b', 'w_kv_b', 'conv_w'], 'b': ['w_out', 'w_ffn_up', 'ffn_conv_w'],
                 'c': ['w_ffn_down', 'w_ple_gate', 'w_ple_proj']}
GRAD_GROUPS = {'p': ['w_ple_proj', 'w_ple_gate', 'w_ffn_down'], 'r': ['w_ffn_up'], 's': ['w_out'],
               't': ['w_q_b', 'w_kv_b', 'w_in']}


def _ffn_perm(j):
    return (j % 2) * (N_DEV // 2) + j // 2


def _local_step(x, p, tabs, get_w, s, target, emit, relay, settle):
    t = x.shape[0]
    s = dict(s)
    half = D_MODEL // 2
    up_cols = 2 * D_FF
    ffn_conv_b = _interleave(s['ffn_conv_b'])
    w = dict(get_w('a', None))
    h = _rmsnorm_fwd(x, s['mix_norm_w'], width=D_MODEL, name="mix_norm")
    proj = _matmul(h, w['w_in'], name="in_proj")
    y_ssd, ssd_saved = _ssd_forward(proj, w['conv_w'], s['conv_b'], s['dt_bias'], s['a_log'], s['d_skip'],
                                    s['ssd_norm_w'])
    o, mla_saved = _mla_forward(proj, tabs, s['q_a_norm_w'], w['w_q_b'], s['kv_a_norm_w'], w['w_kv_b'])
    tk_o, tn_o = _tile(half, MM_TK), _tile(D_MODEL, MM_TILE)
    w.update(get_w('b', o))
    x1 = _matmul(y_ssd, w['w_out'], add=x, mnk=(t, D_MODEL, half), name="out_proj_ssd")
    x1 = _matmul(o, w['w_out'], add=x1, mnk=(t, D_MODEL, half), name="out_proj_mla",
                 b_spec=pl.BlockSpec((tk_o, tn_o), lambda i, j, kk: (kk + half // tk_o, j)))
    hf = _rmsnorm_fwd(x1, s['ffn_norm_w'], width=D_MODEL, name="ffn_norm")
    tk_u = _tile(D_MODEL, MM_TK)
    u = _matmul(hf, w['w_ffn_up'], mnk=(t, up_cols, D_MODEL), tn=FFN_TC, name="ffn_up",
                b_spec=pl.BlockSpec((1, tk_u, FFN_TC), lambda i, j, kk: (_ffn_perm(j), kk, 0)))
    act = _conv_act_fwd(u, w['ffn_conv_w'], ffn_conv_b, kw=FFN_CONV, glu=True, tc=2 * FFN_TC, coff=0, ncols=up_cols,
                        out_dtype=BF16, name="ffn_act")
    w.update(get_w('c', act))
    x2 = _matmul(act, w['w_ffn_down'], add=x1, name="ffn_down")
    hp = _rmsnorm_fwd(x2, s['ple_norm_w'], width=D_MODEL, name="ple_norm")
    gl = _matmul(hp, w['w_ple_gate'], bias=s['b_ple_gate'], name="ple_gate")
    pe = _matmul(p, w['w_ple_proj'], name="ple_proj")
    x3 = _ple_fwd(x2, gl, pe, s['ple_post_norm_w'], name="ple_mix")
    loss, dx3, d_final = _loss_head(x3, s['final_norm_w'], target, name="loss_head")
    dgl, d_bgate, dpe, d_post = _ple_bwd(dx3, gl, pe, s['ple_post_norm_w'], name="ple_mix_bwd")
    d_wproj = _matmul(p, dpe, ta=True, out_dtype=BF16, name="d_w_ple_proj")
    d_wgate = _matmul(hp, dgl, ta=True, out_dtype=BF16, name="d_w_ple_gate")
    dhp = _matmul(dgl, w['w_ple_gate'], tb=True, name="d_ple_normed")
    dx2, d_plenorm = _rmsnorm_bwd(x2, s['ple_norm_w'], dhp, dx3, width=D_MODEL, name="ple_norm_bwd")
    dact = _matmul(dx2, w['w_ffn_down'], tb=True, name="d_ffn_act")
    d_wdown = _matmul(act, dx2, ta=True, out_dtype=BF16, name="d_w_ffn_down")
    zz = emit('p', {'w_ple_proj': _split_cols(d_wproj), 'w_ple_gate': d_wgate.reshape(N_DEV, D_MODEL // N_DEV, D_MODEL),
                    'w_ffn_down': d_wdown.reshape(N_DEV, D_FF // N_DEV, D_MODEL)})
    du, d_fconv_w, d_fconv_b = _conv_act_bwd(u, w['ffn_conv_w'], ffn_conv_b + zz, dact, kw=FFN_CONV, glu=True,
                                             tc=2 * FFN_TC, coff=0, ncols=up_cols, name="ffn_act_bwd")
    zz = zz + relay('p', du)
    tm_u = _tile(D_MODEL, MM_TILE)
    d_wup = _matmul(hf, du, ta=True, out_dtype=BF16, mnk=(D_MODEL, up_cols, t), tn=FFN_TC, name="d_w_ffn_up",
                    o_spec=pl.BlockSpec((1, tm_u, FFN_TC), lambda i, j, kk: (_ffn_perm(j), i, 0)),
                    o_shape=(N_DEV, D_MODEL, FFN_TC))
    zz = zz + emit('r', {'w_ffn_up': d_wup})
    zero_row = jnp.zeros((1, D_MODEL), F32)
    dhf = _matmul(du, w['w_ffn_up'], tb=True, mnk=(t, D_MODEL, up_cols), tk=FFN_TC, name="d_ffn_normed",
                  bias=zero_row + zz,
                  b_spec=pl.BlockSpec((1, tn_o, FFN_TC), lambda i, j, kk: (_ffn_perm(kk), j, 0)))
    zz = zz + relay('r', dhf) + settle('p')
    dx1, d_ffnnorm = _rmsnorm_bwd(x1, s['ffn_norm_w'] + zz, dhf, dx2, width=D_MODEL, name="ffn_norm_bwd")
    dcat = _matmul(dx1, w['w_out'], tb=True, name="d_mixed")
    d_wout = jnp.concatenate([_matmul(y_ssd, dx1, ta=True, out_dtype=BF16, name="d_w_out_ssd"),
                              _matmul(o, dx1, ta=True, out_dtype=BF16, name="d_w_out_mla")], axis=0)
    zz = zz + emit('s', {'w_out': d_wout.reshape(N_DEV, D_MODEL // N_DEV, D_MODEL)})
    ssd_saved = ssd_saved[:3] + (ssd_saved[3] + zz,) + ssd_saved[4:]
    dz, dxbc, d_raw, d_ssdnorm, d_conv_w, d_conv_b, d_dtb, d_alog, d_dskip = _ssd_backward(ssd_saved, dcat)
    zz = zz + relay('s', dz)
    mla_saved = mla_saved[:-1] + (mla_saved[-1] + zz,)
    dq_a, dckv, dkr, d_wq, d_wkv, d_qnorm, d_kvnorm = _mla_backward(mla_saved, dcat)
    d_raw = (d_raw + settle('r')).astype(BF16)
    dproj = jnp.concatenate([dz, dxbc, dq_a, dckv, dkr, d_raw], axis=1)
    d_win = _matmul(h, dproj, ta=True, out_dtype=BF16, name="d_w_in")
    zz = emit('t', {'w_in': _unpad_w_in_shards(d_win), 'w_q_b': _split_cols(_unpad_w_q(d_wq)),
                    'w_kv_b': _split_cols(d_wkv)}) + settle('s')
    dh = _matmul(dproj, w['w_in'], tb=True, bias=zero_row + zz, name="d_in_normed")
    zz = relay('t', dh)
    dx, d_mixnorm = _rmsnorm_bwd(x, s['mix_norm_w'] + zz, dh, dx1, width=D_MODEL, name="mix_norm_bwd")
    conv = {'conv_w': d_conv_w, 'ffn_conv_w': _deinterleave(d_fconv_w)}
    vec = {
        'mix_norm_w': d_mixnorm, 'conv_b': d_conv_b, 'dt_bias': d_dtb, 'a_log': d_alog, 'd_skip': d_dskip,
        'ssd_norm_w': d_ssdnorm, 'q_a_norm_w': d_qnorm, 'kv_a_norm_w': d_kvnorm, 'ffn_norm_w': d_ffnnorm,
        'ffn_conv_b': _deinterleave(d_fconv_b), 'ple_norm_w': d_plenorm, 'b_ple_gate': d_bgate,
        'ple_post_norm_w': d_post, 'final_norm_w': d_final,
    }
    return loss, dx, conv, vec


MESH = pl.DeviceIdType.MESH
FLIPS = ((0, 0, 1), (1, 0, 0), (0, 1, 0), (1, 1, 0), (1, 0, 1), (0, 1, 1), (1, 1, 1))


def _exchange(items, *, gather, name):
    n = len(items)

    def body(*refs):
        ins, outs = refs[:n], refs[n:2 * n]
        send_sems, recv_sems, local_sems = refs[2 * n:]
        x, y, c = lax.axis_index("x"), lax.axis_index("y"), lax.axis_index("c")
        me = 4 * x + 2 * y + c
        peers = [(jnp.where(fx, 1 - x, x), jnp.where(fy, 1 - y, y), jnp.where(fc, 1 - c, c)) for fx, fy, fc in FLIPS]
        slot = [4 * px + 2 * py + pc for px, py, pc in peers]
        local, sends = [], []
        for wi in range(n):
            cp = pltpu.make_async_copy(ins[wi] if gather else ins[wi].at[me], outs[wi].at[me], local_sems.at[wi])
            cp.start()
            local.append(cp)
            for k, peer in enumerate(peers):
                cp = pltpu.make_async_remote_copy(
                    src_ref=ins[wi] if gather else ins[wi].at[slot[k]], dst_ref=outs[wi].at[me],
                    send_sem=send_sems.at[k, wi], recv_sem=recv_sems.at[k, wi], device_id=peer, device_id_type=MESH)
                cp.start()
                sends.append(cp)
        for wi in range(n):
            for k, peer in enumerate(peers):
                pltpu.make_async_remote_copy(
                    src_ref=outs[wi].at[slot[k]], dst_ref=outs[wi].at[slot[k]], send_sem=send_sems.at[k, wi],
                    recv_sem=recv_sems.at[k, wi], device_id=peer, device_id_type=MESH).wait_recv()
        for cp in sends:
            cp.wait_send()
        for cp in local:
            cp.wait()

    hbm = pl.BlockSpec(memory_space=pltpu.HBM)
    out_shape = [jax.ShapeDtypeStruct(((N_DEV,) + v.shape) if gather else v.shape, v.dtype) for v in items]
    return pl.pallas_call(
        body, name=name, in_specs=[hbm] * n, out_specs=[hbm] * n, out_shape=out_shape,
        scratch_shapes=[pltpu.SemaphoreType.DMA((len(FLIPS), n)), pltpu.SemaphoreType.DMA((len(FLIPS), n)),
                        pltpu.SemaphoreType.DMA((n,))],
    )(*items)


HBM_SPEC = pl.BlockSpec(memory_space=pltpu.HBM)
SEM_SPEC = pl.BlockSpec(memory_space=pltpu.SEMAPHORE)
EFFECT = pltpu.SideEffectType.DATAFLOW_SIDE_EFFECTING


def _split_start(bufs, ncopies, plan, *, name):
    nb = len(bufs)

    def body(*refs):
        send_sems, recv_sems, token = refs[nb], refs[nb + 1], refs[2 * nb + 2]
        for i, (src, dst, peer, _) in enumerate(plan(refs[:nb])):
            pltpu.make_async_remote_copy(src_ref=src, dst_ref=dst, send_sem=send_sems.at[i], recv_sem=recv_sems.at[i],
                                         device_id=peer, device_id_type=MESH).start()
        token[...] = jnp.zeros_like(token)

    res = pl.pallas_call(
        body, name=name, in_specs=[HBM_SPEC] * nb,
        out_specs=[SEM_SPEC, SEM_SPEC] + [HBM_SPEC] * nb + [pl.BlockSpec(memory_space=pltpu.VMEM)],
        out_shape=[pltpu.SemaphoreType.DMA((ncopies,)), pltpu.SemaphoreType.DMA((ncopies,))]
        + [pltpu.HBM(v.shape, v.dtype) for v in bufs] + [jax.ShapeDtypeStruct((HALO, LANES), F32)],
        input_output_aliases={i: 2 + i for i in range(nb)},
        compiler_params=pltpu.CompilerParams(has_side_effects=EFFECT),
    )(*[pltpu.with_memory_space_constraint(v, pltpu.HBM) for v in bufs])
    return (res[0], res[1], list(res[2:2 + nb])), res[2 + nb]


def _split_wait(started, after, plan, local_plan, *, name):
    send_sems, recv_sems, bufs = started
    nb = len(bufs)
    nlocal = len(local_plan(bufs))

    def body(*refs):
        send_sems, recv_sems = refs[nb], refs[nb + 1]
        local_sems = refs[2 * nb + 3]
        local = []
        for j, (src, dst) in enumerate(local_plan(refs[:nb])):
            cp = pltpu.make_async_copy(src, dst, local_sems.at[j])
            cp.start()
            local.append(cp)
        for i, (src, _, peer, incoming) in enumerate(plan(refs[:nb])):
            cp = pltpu.make_async_remote_copy(src_ref=src, dst_ref=incoming, send_sem=send_sems.at[i],
                                              recv_sem=recv_sems.at[i], device_id=peer, device_id_type=MESH)
            cp.wait_send()
            cp.wait_recv()
        for cp in local:
            cp.wait()

    res = pl.pallas_call(
        body, name=name, in_specs=[HBM_SPEC] * nb + [SEM_SPEC, SEM_SPEC, pl.BlockSpec(memory_space=pl.ANY)],
        out_specs=[HBM_SPEC] * nb, out_shape=[pltpu.HBM(v.shape, v.dtype) for v in bufs],
        input_output_aliases={i: i for i in range(nb)},
        scratch_shapes=[pltpu.SemaphoreType.DMA((max(nlocal, 1),))],
        compiler_params=pltpu.CompilerParams(has_side_effects=EFFECT),
    )(*bufs, send_sems, recv_sems, after)
    return list(res)


def _place():
    x, y, c = lax.axis_index("x"), lax.axis_index("y"), lax.axis_index("c")
    others = [((1 - x, y, c), 2 * (1 - x) + y), ((x, 1 - y, c), 2 * x + 1 - y), ((1 - x, 1 - y, c), 2 * (1 - x) + 1 - y)]
    return 4 * x + 2 * y + c, 2 * x + y, c, (x, y, 1 - c), others


def _gather1_plan(n):
    def plan(refs):
        me, _, _, sibling, others = _place()
        out = []
        for wi in range(n):
            item, land = refs[wi], refs[n + wi]
            out.append((item, land.at[me], sibling, land.at[me + 1 - 2 * lax.axis_index("c")]))
            for peer, chip in others:
                out.append((item, land.at[me], peer, land.at[2 * chip + lax.axis_index("c")]))
        return out

    return plan


def _gather1_local(n):
    def plan(refs):
        me = _place()[0]
        return [(refs[wi], refs[n + wi].at[me]) for wi in range(n)]

    return plan


def _gather2_plan(n):
    def plan(refs):
        _, _, c, sibling, others = _place()
        out = []
        for wi in range(n):
            land = refs[wi]
            for _, chip in others:
                out.append((land.at[2 * chip + c], land.at[2 * chip + c], sibling, land.at[2 * chip + 1 - c]))
        return out

    return plan


def _gather_start(items, *, name):
    lands = [lax.empty((N_DEV,) + v.shape, v.dtype) for v in items]
    return _split_start(items + lands, 4 * len(items), _gather1_plan(len(items)), name=name)


def _gather_forward(started, after, *, name):
    n = len(started[2]) // 2
    bufs = _split_wait(started, after, _gather1_plan(n), _gather1_local(n), name=name + "_wait")
    return _split_start(bufs[n:], 3 * n, _gather2_plan(n), name=name + "_start")


def _gather_finish(started, after, *, name):
    n = len(started[2])
    return _split_wait(started, after, _gather2_plan(n), lambda refs: [], name=name)


def _handshake(peers):
    barrier = pltpu.get_barrier_semaphore()
    for peer in peers:
        pl.semaphore_signal(barrier, inc=1, device_id=peer, device_id_type=MESH)
    pl.semaphore_wait(barrier, len(peers))


def _remote(src, dst, send_sem, recv_sem, peer):
    return pltpu.make_async_remote_copy(src_ref=src, dst_ref=dst, send_sem=send_sem, recv_sem=recv_sem, device_id=peer,
                                        device_id_type=MESH)


def _sequencer_gather(items, *, collective_id, name):
    n = len(items)
    srcs = [jax.new_ref(v, memory_space=pltpu.MemorySpace.HBM) for v in items]
    lands = [jax.empty_ref(jax.ShapeDtypeStruct((N_DEV,) + v.shape, v.dtype), memory_space=pltpu.MemorySpace.HBM)
             for v in items]
    dma = pltpu.SemaphoreType.DMA

    @pl.kernel(mesh=plsc.ScalarSubcoreMesh(axis_name="sequencer", num_cores=1), name=name,
               scratch_types=(dma((4 * n,)), dma((4 * n,)), dma((3 * n,)), dma((3 * n,)), dma((n,))),
               compiler_params=pltpu.CompilerParams(collective_id=collective_id))
    def launch(send1, recv1, send2, recv2, local_sems):
        _, _, _, sibling, others = _place()
        _handshake([sibling] + [peer for peer, _ in others])
        hop1 = _gather1_plan(n)(srcs + lands)
        hop2 = _gather2_plan(n)(lands)
        local = [pltpu.make_async_copy(src, dst, local_sems.at[j])
                 for j, (src, dst) in enumerate(_gather1_local(n)(srcs + lands))]
        for cp in local:
            cp.start()
        for i, (src, dst, peer, _) in enumerate(hop1):
            _remote(src, dst, send1.at[i], recv1.at[i], peer).start()
        for wi in range(n):
            for j in range(3):
                i1, i2 = 4 * wi + 1 + j, 3 * wi + j
                src, _, peer, incoming = hop1[i1]
                _remote(src, incoming, send1.at[i1], recv1.at[i1], peer).wait_recv()
                src, dst, peer, _ = hop2[i2]
                _remote(src, dst, send2.at[i2], recv2.at[i2], peer).start()
        for wi in range(n):
            src, _, peer, incoming = hop1[4 * wi]
            _remote(src, incoming, send1.at[4 * wi], recv1.at[4 * wi], peer).wait_recv()
        for i, (src, _, peer, incoming) in enumerate(hop2):
            cp = _remote(src, incoming, send2.at[i], recv2.at[i], peer)
            cp.wait_send()
            cp.wait_recv()
        for i, (src, dst, peer, _) in enumerate(hop1):
            _remote(src, dst, send1.at[i], recv1.at[i], peer).wait_send()
        for cp in local:
            cp.wait()

    launch()
    return [land[...] for land in lands]


def _sequencer_exchange(sources, land_shapes, ncopies, plan, local_plan, peers, *, collective_id, name):
    srcs = [jax.new_ref(v, memory_space=pltpu.MemorySpace.HBM) for v in sources]
    lands = [jax.empty_ref(s, memory_space=pltpu.MemorySpace.HBM) for s in land_shapes]
    nlocal = len(local_plan(srcs + lands))
    dma = pltpu.SemaphoreType.DMA

    @pl.kernel(mesh=plsc.ScalarSubcoreMesh(axis_name="sequencer", num_cores=1), name=name,
               scratch_types=(dma((ncopies,)), dma((ncopies,)), dma((max(nlocal, 1),))),
               compiler_params=pltpu.CompilerParams(collective_id=collective_id))
    def launch(send_sems, recv_sems, local_sems):
        _handshake(peers(_place()))
        copies = plan(srcs + lands)
        local = [pltpu.make_async_copy(src, dst, local_sems.at[j])
                 for j, (src, dst) in enumerate(local_plan(srcs + lands))]
        for cp in local:
            cp.start()
        for i, (src, dst, peer, _) in enumerate(copies):
            _remote(src, dst, send_sems.at[i], recv_sems.at[i], peer).start()
        for i, (src, _, peer, incoming) in enumerate(copies):
            cp = _remote(src, incoming, send_sems.at[i], recv_sems.at[i], peer)
            cp.wait_send()
            cp.wait_recv()
        for cp in local:
            cp.wait()

    launch()
    return [land[...] for land in lands]


def _sequencer_scatter_hop2(sums, *, collective_id, name):
    n = len(sums)
    shapes = [jax.ShapeDtypeStruct(v.shape, v.dtype) for v in sums]
    return _sequencer_exchange(sums, shapes, 3 * n, _scatter2_plan(n), _scatter2_local(n),
                               lambda place: [peer for peer, _ in place[4]], collective_id=collective_id, name=name)


N_CHIP = N_DEV // 2


def _scatter1_plan(n):
    def plan(refs):
        _, _, c, sibling, _ = _place()
        out = []
        for wi in range(n):
            parts, half = refs[wi], refs[n + wi]
            for chip in range(N_CHIP):
                out.append((parts.at[2 * chip + 1 - c], half.at[chip], sibling, half.at[chip]))
        return out

    return plan


def _scatter2_plan(n):
    def plan(refs):
        _, my_chip, _, _, others = _place()
        out = []
        for wi in range(n):
            sums, recv = refs[wi], refs[n + wi]
            for peer, chip in others:
                out.append((sums.at[chip], recv.at[my_chip], peer, recv.at[chip]))
        return out

    return plan


def _scatter2_local(n):
    def plan(refs):
        my_chip = _place()[1]
        return [(refs[wi].at[my_chip], refs[n + wi].at[my_chip]) for wi in range(n)]

    return plan


def _pair_add(parts, half, core, *, name):
    _, r, c = parts.shape
    tr = max(d for d in range(HALO, 257, HALO) if r % d == 0) if r > 256 else r
    parts4 = parts.reshape(N_CHIP, 2, r, c)

    def body(core_ref, p_ref, h_ref, o_ref):
        o_ref[...] = (p_ref[:, 0].astype(F32) + h_ref[...].astype(F32)).astype(o_ref.dtype)

    return pl.pallas_call(
        body, name=name,
        grid_spec=pltpu.PrefetchScalarGridSpec(
            num_scalar_prefetch=1, grid=(r // tr,),
            in_specs=[pl.BlockSpec((N_CHIP, 1, tr, c), lambda i, core_ref: (0, core_ref[0], i, 0)),
                      pl.BlockSpec((N_CHIP, tr, c), lambda i, core_ref: (0, i, 0))],
            out_specs=pl.BlockSpec((N_CHIP, tr, c), lambda i, core_ref: (0, i, 0))),
        out_shape=jax.ShapeDtypeStruct((N_CHIP, r, c), parts.dtype), compiler_params=_cp("parallel"),
    )(core, parts4, half)


def _scatter_start(parts, *, name):
    halves = [lax.empty((N_CHIP,) + v.shape[1:], v.dtype) for v in parts]
    return _split_start(parts + halves, N_CHIP * len(parts), _scatter1_plan(len(parts)), name=name)


def _adamw(parts, w, m, v, *, name):
    r, c = w.shape
    nparts = parts.shape[0]
    tr = max(d for d in range(HALO, 129, HALO) if r % d == 0) if r > 128 else r

    def body(p_ref, w_ref, m_ref, v_ref, g_ref, d_ref, mo_ref, vo_ref):
        g = p_ref[0].astype(F32)
        for k in range(1, nparts):
            g = g + p_ref[k].astype(F32)
        mn = ADAM_B1 * m_ref[...] + (1.0 - ADAM_B1) * g
        vn = ADAM_B2 * v_ref[...] + (1.0 - ADAM_B2) * (g * g)
        m_hat = mn / (1.0 - ADAM_B1 ** ADAM_STEP)
        v_hat = vn / (1.0 - ADAM_B2 ** ADAM_STEP)
        g_ref[...] = g
        d_ref[...] = -ADAM_LR * (m_hat / (jnp.sqrt(v_hat) + ADAM_EPS) + ADAM_WD * w_ref[...])
        mo_ref[...] = mn
        vo_ref[...] = vn

    blk = pl.BlockSpec((tr, c), lambda i: (i, 0))
    return pl.pallas_call(
        body, name=name, grid=(r // tr,), in_specs=[pl.BlockSpec((nparts, tr, c), lambda i: (0, i, 0)), blk, blk, blk],
        out_specs=[blk] * 4, out_shape=[jax.ShapeDtypeStruct((r, c), F32)] * 4, compiler_params=_cp("parallel"),
    )(parts, w, m, v)


def _pack_rows(vs, rows):
    lead = vs[0].shape[:-1] if vs[0].ndim > 1 else ()
    flat = jnp.concatenate(vs, axis=-1)
    pad = rows * LANES - flat.shape[-1]
    flat = jnp.pad(flat, [(0, 0)] * len(lead) + [(0, pad)])
    return flat.reshape(lead + (rows, LANES))


def kernel(x, p, positions, mix_norm_w, w_in, conv_w, conv_b, dt_bias, a_log, d_skip, ssd_norm_w, q_a_norm_w, w_q_b, kv_a_norm_w, w_kv_b, w_out, ffn_norm_w, w_ffn_up, ffn_conv_w, ffn_conv_b, w_ffn_down, ple_norm_w, w_ple_gate, b_ple_gate, w_ple_proj, ple_post_norm_w, final_norm_w, loss_target, m_mix_norm_w, m_w_in, m_conv_w, m_conv_b, m_dt_bias, m_a_log, m_d_skip, m_ssd_norm_w, m_q_a_norm_w, m_w_q_b, m_kv_a_norm_w, m_w_kv_b, m_w_out, m_ffn_norm_w, m_w_ffn_up, m_ffn_conv_w, m_ffn_conv_b, m_w_ffn_down, m_ple_norm_w, m_w_ple_gate, m_b_ple_gate, m_w_ple_proj, m_ple_post_norm_w, m_final_norm_w, v_mix_norm_w, v_w_in, v_conv_w, v_conv_b, v_dt_bias, v_a_log, v_d_skip, v_ssd_norm_w, v_q_a_norm_w, v_w_q_b, v_kv_a_norm_w, v_w_kv_b, v_w_out, v_ffn_norm_w, v_w_ffn_up, v_ffn_conv_w, v_ffn_conv_b, v_w_ffn_down, v_ple_norm_w, v_w_ple_gate, v_b_ple_gate, v_w_ple_proj, v_ple_post_norm_w, v_final_norm_w):
    given = dict(locals())
    shapes = {n: given[n].shape for n in WEIGHTS}
    w2 = {n: given[n].reshape(given[n].shape[-2:] if n in BIG or n in CONV else (1, -1)) for n in WEIGHTS}
    m2 = {n: given['m_' + n].reshape(w2[n].shape) for n in WEIGHTS}
    v2 = {n: given['v_' + n].reshape(w2[n].shape) for n in WEIGHTS}
    me = 4 * lax.axis_index("x") + 2 * lax.axis_index("y") + lax.axis_index("c")

    core = lax.axis_index("c").astype(jnp.int32).reshape(1)

    def shards(grp, zero):
        return [(w2[n] + zero).astype(BF16) if n in BIG else w2[n] + zero for n in WEIGHT_GROUPS[grp]]

    first, token = _gather_start(shards('a', 0.0), name="gather_a_hop1")
    first, token = _gather_forward(first, token, name="gather_a_hop2")
    zero = token[0, 0]
    later = dict(zip(WEIGHT_GROUPS['b'], _sequencer_gather(shards('b', zero), collective_id=1, name="gather_b")))
    later.update(zip(WEIGHT_GROUPS['c'], _sequencer_gather(shards('c', zero), collective_id=2, name="gather_c")))

    def get_w(grp, after):
        if grp == 'a':
            lands = dict(zip(WEIGHT_GROUPS[grp], _gather_finish(first, token, name="gather_a_done")))
        else:
            lands = {n: later[n] for n in WEIGHT_GROUPS[grp]}
        return _assemble_weights(lands)

    scatters = {}

    hop_ids = {grp: 2 + 2 * i for i, grp in enumerate(GRAD_GROUPS)}

    def zero_of(arrays):
        return sum(v[(0,) * v.ndim].astype(F32) * 0.0 for v in arrays)

    def emit(grp, grads):
        scatters[grp], tok = _scatter_start([grads[n] for n in GRAD_GROUPS[grp]], name="scatter_" + grp + "_hop1")
        return tok[0, 0]

    def relay(grp, after):
        n = len(GRAD_GROUPS[grp])
        bufs = _split_wait(scatters[grp], after, _scatter1_plan(n), lambda refs: [], name="scatter_" + grp + "_hop1_wait")
        sums = [_pair_add(bufs[i], bufs[n + i], core, name="scatter_%s_add%d" % (grp, i)) for i in range(n)]
        scatters[grp] = _sequencer_scatter_hop2(sums, collective_id=hop_ids[grp] + 1, name="scatter_" + grp + "_hop2")
        return zero_of(sums)

    out_g, out_d, out_m, out_v = {}, {}, {}, {}

    def settle(grp):
        return zero_of(scatters[grp])

    def update(grp, behind=None):
        for n, parts in zip(GRAD_GROUPS[grp], scatters[grp]):
            wn = w2[n] if behind is None else w2[n] + behind
            out_g[n], out_d[n], out_m[n], out_v[n] = _adamw(parts, wn, m2[n], v2[n], name="adamw_" + n)

    vecs = {n: w2[n] for n in REPL}
    vecs['mix_norm_w'] = vecs['mix_norm_w'] + zero
    loss, dx, g_conv, g_vec = _local_step(x[0], p[0, 0], _rope_tables(positions), get_w, vecs, loss_target[0], emit,
                                          relay, settle)
    n_small = sum(g_vec[n].shape[1] for n in REPL) + sum(g_conv[n].size for n in CONV) + 1
    rows_small = -(-n_small // (LANES * HALO)) * HALO
    small = _pack_rows([g_vec[n] for n in REPL] + [g_conv[n].reshape(1, -1) for n in CONV] + [loss], rows_small)

    for grp in list(GRAD_GROUPS)[:-1]:
        update(grp)
    all_small = _exchange([small], gather=True, name="gather_small_grads")[0].reshape(N_DEV, rows_small * LANES)
    update(list(GRAD_GROUPS)[-1], zero_of([all_small]))
    pieces, off = [], 0
    for n in REPL:
        k = g_vec[n].shape[1]
        pieces.append(all_small[:, off:off + k])
        off += k
    for n in CONV:
        kw, cols = g_conv[n].shape
        full = all_small[:, off:off + kw * cols].reshape(N_DEV, kw, cols)
        mine = lax.dynamic_slice_in_dim(full, me * (cols // N_DEV), cols // N_DEV, axis=2)
        pieces.append(mine.reshape(N_DEV, kw * (cols // N_DEV)))
        off += kw * cols
    pieces.append(all_small[:, off:off + 1])
    small_names = REPL + CONV
    n_mine = sum(q.shape[1] for q in pieces)
    rows_mine = -(-n_mine // (LANES * HALO)) * HALO
    zero = jnp.zeros((1, 1), F32)
    packed = [_pack_rows([src[n].reshape(1, -1) for n in small_names] + [zero], rows_mine).reshape(rows_mine, LANES)
              for src in (w2, m2, v2)]
    sg, sd, sm, sv = _adamw(_pack_rows(pieces, rows_mine), *packed, name="adamw_small")
    off = 0
    for n in small_names:
        k = w2[n].size
        for dst, src in ((out_g, sg), (out_d, sd), (out_m, sm), (out_v, sv)):
            dst[n] = src.reshape(-1)[off:off + k].reshape(w2[n].shape)
        off += k
    total_loss = sg.reshape(-1)[off]

    outs = [total_loss, dx[None]]
    for res in (out_g, out_d, out_m, out_v):
        outs += [res[n].reshape(shapes[n]) for n in WEIGHTS]
    return tuple(outs)
```

```python
import math

import numpy as np
import jax
import jax.numpy as jnp
from jax import lax
from jax.experimental import pallas as pl
from jax.experimental.pallas import tpu as pltpu
from jax.experimental.pallas import tpu_sc as plsc

F32 = jnp.float32
BF16 = jnp.bfloat16
HI = lax.Precision.HIGHEST

D_MODEL = 2048
CHUNK = 64
D_SSM = 1024
SSD_P = 64
SSD_HEADS = 16
SSD_GROUPS = 2
SSD_N = 128
SSD_CONV = 4
SSD_CONV_DIM = D_SSM + 2 * SSD_GROUPS * SSD_N
MLA_HEADS = 8
MLA_NOPE = 128
MLA_ROPE = 64
MLA_V = 128
MLA_Q_RANK = 512
MLA_KV_RANK = 256
MLA_QK_PAD = 256
ROPE_THETA = 10000.0
D_FF = 5632
FFN_CONV = 3
PLE_DIM = 256
NORM_EPS = 1e-6
ADAM_LR, ADAM_B1, ADAM_B2, ADAM_EPS, ADAM_WD, ADAM_STEP = 0.001, 0.9, 0.999, 1e-08, 0.01, 10
N_DEV = 8

OFF_Z, OFF_XBC, OFF_QA, OFF_CKV, OFF_KR, OFF_DT, D_IN_PAD = 0, 1024, 2560, 3072, 3328, 3456, 3584
D_IN = 3408
LANES = 128
HALO = 8
VMEM_LIMIT = 56 * 1024 * 1024
FFN_TC = D_FF * 2 // N_DEV
FFN_PERM = (0, 4, 1, 5, 2, 6, 3, 7)
NEG = -1e30


def _cp(*sem):
    return pltpu.CompilerParams(dimension_semantics=tuple(sem), vmem_limit_bytes=VMEM_LIMIT)


def _tile(n, want):
    if n <= want:
        return n
    best = max(d for d in range(LANES, want + 1, LANES) if n % d == 0)
    return best


def _sigmoid(x):
    return 0.5 * (jnp.tanh(0.5 * x) + 1.0)


def _silu(x):
    return x * _sigmoid(x)


def _dsilu(x):
    s = _sigmoid(x)
    return s * (1.0 + x * (1.0 - s))


MM_TILE = 1408
MM_TK = 2816


def _matmul(a, b, *, ta=False, tb=False, out_dtype=F32, add=None, bias=None, tm=MM_TILE, tn=MM_TILE, tk=MM_TK, name,
            mnk=None, a_spec=None, b_spec=None, o_spec=None, o_shape=None):
    if mnk is None:
        m, k = (a.shape[1], a.shape[0]) if ta else a.shape
        n = b.shape[0] if tb else b.shape[1]
        assert k == (b.shape[1] if tb else b.shape[0])
    else:
        m, n, k = mnk
    tm, tn, tk = _tile(m, tm), _tile(n, tn), _tile(k, tk)
    nk = k // tk
    dims = (((0 if ta else 1,), (1 if tb else 0,)), ((), ()))

    def body(*refs):
        a_ref, b_ref = refs[0], refs[1]
        pos = 2
        add_ref = bias_ref = None
        if add is not None:
            add_ref = refs[pos]
            pos += 1
        if bias is not None:
            bias_ref = refs[pos]
            pos += 1
        o_ref = refs[pos]
        kk = pl.program_id(2)
        av = a_ref[...]
        bv = b_ref[...]
        av = av.reshape(av.shape[-2:]).astype(BF16)
        bv = bv.reshape(bv.shape[-2:]).astype(BF16)
        prod = lax.dot_general(av, bv, dims, preferred_element_type=F32)

        def finish(r):
            if bias_ref is not None:
                r = r + bias_ref[...]
            if add_ref is not None:
                r = r + add_ref[...].astype(F32)
            o_ref[...] = r.astype(out_dtype).reshape(o_ref.shape)

        if nk == 1:
            finish(prod)
        else:
            acc_ref = refs[pos + 1]

            @pl.when(kk == 0)
            def _():
                acc_ref[...] = prod

            @pl.when(kk > 0)
            def _():
                acc_ref[...] += prod

            @pl.when(kk == nk - 1)
            def _():
                finish(acc_ref[...])

    if a_spec is None:
        a_spec = (pl.BlockSpec((tk, tm), lambda i, j, kk: (kk, i)) if ta
                  else pl.BlockSpec((tm, tk), lambda i, j, kk: (i, kk)))
    if b_spec is None:
        b_spec = (pl.BlockSpec((tn, tk), lambda i, j, kk: (j, kk)) if tb
                  else pl.BlockSpec((tk, tn), lambda i, j, kk: (kk, j)))
    if o_spec is None:
        o_spec = pl.BlockSpec((tm, tn), lambda i, j, kk: (i, j))
    if o_shape is None:
        o_shape = (m, n)
    in_specs = [a_spec, b_spec]
    args = [a, b]
    if add is not None:
        in_specs.append(pl.BlockSpec((tm, tn), lambda i, j, kk: (i, j)))
        args.append(add)
    if bias is not None:
        in_specs.append(pl.BlockSpec((1, tn), lambda i, j, kk: (0, j)))
        args.append(bias)
    return pl.pallas_call(
        body, name=name, grid=(m // tm, n // tn, nk), in_specs=in_specs, out_specs=o_spec,
        out_shape=jax.ShapeDtypeStruct(o_shape, out_dtype),
        scratch_shapes=[pltpu.VMEM((tm, tn), F32)] if nk > 1 else [],
        compiler_params=_cp("parallel", "parallel", "arbitrary"),
    )(*args)


def _rmsnorm_fwd(x, w, *, width, cblk=0, out_dtype=BF16, tr=256, name):
    t = x.shape[0]

    def body(x_ref, w_ref, o_ref):
        xv = x_ref[...].astype(F32)
        r = lax.rsqrt(jnp.mean(xv * xv, axis=-1, keepdims=True) + NORM_EPS)
        o_ref[...] = (xv * r * w_ref[...]).astype(out_dtype)

    return pl.pallas_call(
        body, name=name, grid=(t // tr,),
        in_specs=[pl.BlockSpec((tr, width), lambda i: (i, cblk)), pl.BlockSpec((1, width), lambda i: (0, 0))],
        out_specs=pl.BlockSpec((tr, width), lambda i: (i, 0)),
        out_shape=jax.ShapeDtypeStruct((t, width), out_dtype),
        compiler_params=_cp("parallel"),
    )(x, w)


def _rmsnorm_bwd(x, w, dy, add=None, *, width, cblk=0, out_dtype=F32, tr=256, name):
    t = x.shape[0]

    def body(*refs):
        if add is None:
            x_ref, w_ref, dy_ref, dx_ref, dw_ref = refs
            add_ref = None
        else:
            x_ref, w_ref, dy_ref, add_ref, dx_ref, dw_ref = refs
        xv = x_ref[...].astype(F32)
        dyv = dy_ref[...].astype(F32)
        r = lax.rsqrt(jnp.mean(xv * xv, axis=-1, keepdims=True) + NORM_EPS)
        xh = xv * r
        g = dyv * w_ref[...]
        dx = r * (g - xh * jnp.mean(g * xh, axis=-1, keepdims=True))
        if add_ref is not None:
            dx = dx + add_ref[...].astype(F32)
        dx_ref[...] = dx.astype(out_dtype)

        @pl.when(pl.program_id(0) == 0)
        def _():
            dw_ref[...] = jnp.zeros_like(dw_ref)

        dw_ref[...] += jnp.sum(dyv * xh, axis=0, keepdims=True)

    in_specs = [pl.BlockSpec((tr, width), lambda i: (i, cblk)), pl.BlockSpec((1, width), lambda i: (0, 0)),
                pl.BlockSpec((tr, width), lambda i: (i, 0))]
    args = [x, w, dy]
    if add is not None:
        in_specs.append(pl.BlockSpec((tr, width), lambda i: (i, 0)))
        args.append(add)
    return pl.pallas_call(
        body, name=name, grid=(t // tr,), in_specs=in_specs,
        out_specs=[pl.BlockSpec((tr, width), lambda i: (i, 0)), pl.BlockSpec((1, width), lambda i: (0, 0))],
        out_shape=[jax.ShapeDtypeStruct((t, width), out_dtype), jax.ShapeDtypeStruct((1, width), F32)],
        compiler_params=_cp("arbitrary"),
    )(*args)


def _shift_down(prev_halo, cur, j):
    if j == 0:
        return cur
    ext = jnp.concatenate([prev_halo, cur], axis=0)
    return pltpu.roll(ext, j, axis=0)[HALO:]


def _shift_up(cur, next_halo, j):
    if j == 0:
        return cur
    ext = jnp.concatenate([cur, next_halo], axis=0)
    return pltpu.roll(ext, ext.shape[0] - j, axis=0)[:cur.shape[0]]


def _conv_rows(prev, cur, w, b, kw):
    shifted = [cur]
    out = b + w[kw - 1:kw] * cur
    for j in range(1, kw):
        sh = _shift_down(prev, cur, j)
        shifted.append(sh)
        out = out + w[kw - 1 - j:kw - j] * sh
    return out, shifted


def _act_fwd(c, glu):
    if glu:
        half = c.shape[1] // 2
        return _silu(c[:, :half]) * c[:, half:]
    return _silu(c)


def _act_bwd(c, dout, glu):
    if glu:
        half = c.shape[1] // 2
        g, up = c[:, :half], c[:, half:]
        s = _sigmoid(g)
        gs = g * s
        return jnp.concatenate([dout * up * (s + gs * (1.0 - s)), dout * gs], axis=1)
    return dout * _dsilu(c)


def _conv_act_fwd(u, w, b, *, kw, glu, tc, coff, ncols, out_dtype, tr=256, name):
    t = u.shape[0]
    nb = ncols // tc
    oc = tc // 2 if glu else tc

    def body(u_ref, uh_ref, w_ref, b_ref, o_ref):
        prev = jnp.where(pl.program_id(0) == 0, 0.0, uh_ref[...])
        c, _ = _conv_rows(prev, u_ref[...], w_ref[...], b_ref[...], kw)
        o_ref[...] = _act_fwd(c, glu).astype(out_dtype)

    return pl.pallas_call(
        body, name=name, grid=(t // tr, nb),
        in_specs=[pl.BlockSpec((tr, tc), lambda i, j: (i, j + coff)),
                  pl.BlockSpec((HALO, tc), lambda i, j: (jnp.maximum(i * (tr // HALO) - 1, 0), j + coff)),
                  pl.BlockSpec((kw, tc), lambda i, j: (0, j)), pl.BlockSpec((1, tc), lambda i, j: (0, j))],
        out_specs=pl.BlockSpec((tr, oc), lambda i, j: (i, j)),
        out_shape=jax.ShapeDtypeStruct((t, nb * oc), out_dtype),
        compiler_params=_cp("parallel", "parallel"),
    )(u, u, w, b)


def _conv_act_bwd(u, w, b, dout, *, kw, glu, tc, coff, ncols, tr=256, name):
    t = u.shape[0]
    nb = ncols // tc
    nt = t // tr
    oc = tc // 2 if glu else tc

    def body(u_ref, up_ref, un_ref, d_ref, dn_ref, w_ref, b_ref, du_ref, dw_ref, db_ref):
        i = pl.program_id(1)
        cur, nxt, wv, bv = u_ref[...], un_ref[...], w_ref[...], b_ref[...]
        prev = jnp.where(i == 0, 0.0, up_ref[...])
        c_cur, shifted = _conv_rows(prev, cur, wv, bv, kw)
        c_nxt, _ = _conv_rows(cur[tr - HALO:], nxt, wv, bv, kw)
        d_cur = _act_bwd(c_cur, d_ref[...].astype(F32), glu)
        d_nxt = _act_bwd(c_nxt, jnp.where(i == nt - 1, 0.0, dn_ref[...].astype(F32)), glu)
        du = wv[kw - 1:kw] * d_cur
        for j in range(1, kw):
            du = du + wv[kw - 1 - j:kw - j] * _shift_up(d_cur, d_nxt, j)
        du_ref[...] = du.astype(BF16)

        @pl.when(i == 0)
        def _():
            dw_ref[...] = jnp.zeros_like(dw_ref)
            db_ref[...] = jnp.zeros_like(db_ref)

        db_ref[...] += jnp.sum(d_cur, axis=0, keepdims=True)
        dw_ref[...] += jnp.concatenate(
            [jnp.sum(d_cur * shifted[kw - 1 - k], axis=0, keepdims=True) for k in range(kw)], axis=0)

    nh = tr // HALO
    return pl.pallas_call(
        body, name=name, grid=(nb, nt),
        in_specs=[pl.BlockSpec((tr, tc), lambda j, i: (i, j + coff)),
                  pl.BlockSpec((HALO, tc), lambda j, i: (jnp.maximum(i * nh - 1, 0), j + coff)),
                  pl.BlockSpec((HALO, tc), lambda j, i: (jnp.minimum((i + 1) * nh, t // HALO - 1), j + coff)),
                  pl.BlockSpec((tr, oc), lambda j, i: (i, j)),
                  pl.BlockSpec((HALO, oc), lambda j, i: (jnp.minimum((i + 1) * nh, t // HALO - 1), j)),
                  pl.BlockSpec((kw, tc), lambda j, i: (0, j)), pl.BlockSpec((1, tc), lambda j, i: (0, j))],
        out_specs=[pl.BlockSpec((tr, tc), lambda j, i: (i, j)), pl.BlockSpec((kw, tc), lambda j, i: (0, j)),
                   pl.BlockSpec((1, tc), lambda j, i: (0, j))],
        out_shape=[jax.ShapeDtypeStruct((t, ncols), BF16), jax.ShapeDtypeStruct((kw, ncols), F32),
                   jax.ShapeDtypeStruct((1, ncols), F32)],
        compiler_params=_cp("parallel", "arbitrary"),
    )(u, u, u, dout, dout, w, b)


def _ple_fwd(x2, gl, pe, pw, *, tr=256, name):
    t, d = x2.shape

    def body(x_ref, gl_ref, pe_ref, pw_ref, o_ref):
        pv = pe_ref[...]
        r = lax.rsqrt(jnp.mean(pv * pv, axis=-1, keepdims=True) + NORM_EPS)
        o_ref[...] = x_ref[...] + _sigmoid(gl_ref[...]) * (pv * r * pw_ref[...])

    blk = pl.BlockSpec((tr, d), lambda i: (i, 0))
    return pl.pallas_call(
        body, name=name, grid=(t // tr,), in_specs=[blk, blk, blk, pl.BlockSpec((1, d), lambda i: (0, 0))],
        out_specs=blk, out_shape=jax.ShapeDtypeStruct((t, d), F32), compiler_params=_cp("parallel"),
    )(x2, gl, pe, pw)


def _ple_bwd(dx3, gl, pe, pw, *, tr=256, name):
    t, d = dx3.shape

    def body(dx_ref, gl_ref, pe_ref, pw_ref, dgl_ref, db_ref, dpe_ref, dpw_ref):
        dx, pv, pwv = dx_ref[...], pe_ref[...], pw_ref[...]
        gate = _sigmoid(gl_ref[...])
        r = lax.rsqrt(jnp.mean(pv * pv, axis=-1, keepdims=True) + NORM_EPS)
        ph = pv * r
        dgl = dx * (ph * pwv) * gate * (1.0 - gate)
        de = dx * gate
        g = de * pwv
        dgl_ref[...] = dgl.astype(BF16)
        dpe_ref[...] = (r * (g - ph * jnp.mean(g * ph, axis=-1, keepdims=True))).astype(BF16)

        @pl.when(pl.program_id(0) == 0)
        def _():
            db_ref[...] = jnp.zeros_like(db_ref)
            dpw_ref[...] = jnp.zeros_like(dpw_ref)

        db_ref[...] += jnp.sum(dgl, axis=0, keepdims=True)
        dpw_ref[...] += jnp.sum(de * ph, axis=0, keepdims=True)

    blk = pl.BlockSpec((tr, d), lambda i: (i, 0))
    row = pl.BlockSpec((1, d), lambda i: (0, 0))
    return pl.pallas_call(
        body, name=name, grid=(t // tr,), in_specs=[blk, blk, blk, row], out_specs=[blk, row, blk, row],
        out_shape=[jax.ShapeDtypeStruct((t, d), BF16), jax.ShapeDtypeStruct((1, d), F32),
                   jax.ShapeDtypeStruct((t, d), BF16), jax.ShapeDtypeStruct((1, d), F32)],
        compiler_params=_cp("arbitrary"),
    )(dx3, gl, pe, pw)


def _loss_head(x3, fw, target, *, tr=256, name):
    t, d = x3.shape

    def body(x_ref, w_ref, t_ref, l_ref, dx_ref, dw_ref):
        xv, wv = x_ref[...], w_ref[...]
        r = lax.rsqrt(jnp.mean(xv * xv, axis=-1, keepdims=True) + NORM_EPS)
        xh = xv * r
        err = xh * wv - t_ref[...]
        dy = err * (1.0 / d)
        g = dy * wv
        dx_ref[...] = r * (g - xh * jnp.mean(g * xh, axis=-1, keepdims=True))

        @pl.when(pl.program_id(0) == 0)
        def _():
            l_ref[...] = jnp.zeros_like(l_ref)
            dw_ref[...] = jnp.zeros_like(dw_ref)

        l_ref[...] += 0.5 * jnp.sum(jnp.mean(err * err, axis=-1, keepdims=True), axis=0, keepdims=True)
        dw_ref[...] += jnp.sum(dy * xh, axis=0, keepdims=True)

    blk = pl.BlockSpec((tr, d), lambda i: (i, 0))
    row = pl.BlockSpec((1, d), lambda i: (0, 0))
    return pl.pallas_call(
        body, name=name, grid=(t // tr,), in_specs=[blk, row, blk],
        out_specs=[pl.BlockSpec((1, 1), lambda i: (0, 0)), blk, row],
        out_shape=[jax.ShapeDtypeStruct((1, 1), F32), jax.ShapeDtypeStruct((t, d), F32),
                   jax.ShapeDtypeStruct((1, d), F32)],
        compiler_params=_cp("arbitrary"),
    )(x3, fw, target)


def _rope(blk, tab_ref):
    return blk * tab_ref[0] + pltpu.roll(blk, 96, axis=1) * tab_ref[1] + pltpu.roll(blk, 32, axis=1) * tab_ref[2]


def _unrope(g, tab_ref):
    return g * tab_ref[0] + pltpu.roll(g * tab_ref[1], 32, axis=1) + pltpu.roll(g * tab_ref[2], 96, axis=1)


def _mla_prep(q, kv, proj, tabs, *, tr=512, name):
    t = q.shape[0]

    def body(q_ref, kv_ref, kr_ref, tab_ref, qo_ref, ko_ref, vo_ref, vt_ref):
        qv, kvv = q_ref[...], kv_ref[...]
        qo_ref[0, :, :MLA_NOPE] = qv[:, :MLA_NOPE].astype(BF16)
        qo_ref[0, :, MLA_NOPE:] = _rope(qv[:, MLA_NOPE:], tab_ref).astype(BF16)
        ko_ref[0, :, :MLA_NOPE] = kvv[:, :MLA_NOPE].astype(BF16)
        ko_ref[0, :, MLA_NOPE:] = _rope(kr_ref[...], tab_ref).astype(BF16)
        vo_ref[0] = kvv[:, MLA_NOPE:].astype(BF16)
        for blk in range(tr // ATT_BLK):
            vt_ref[0, blk] = kvv[blk * ATT_BLK:(blk + 1) * ATT_BLK, MLA_NOPE:].T.astype(BF16)

    return pl.pallas_call(
        body, name=name, grid=(t // tr, MLA_HEADS),
        in_specs=[pl.BlockSpec((tr, MLA_QK_PAD), lambda i, h: (i, h)),
                  pl.BlockSpec((tr, MLA_NOPE + MLA_V), lambda i, h: (i, h)),
                  pl.BlockSpec((tr, LANES), lambda i, h: (i, OFF_KR // LANES)),
                  pl.BlockSpec((3, tr, LANES), lambda i, h: (0, i, 0))],
        out_specs=[pl.BlockSpec((1, tr, MLA_QK_PAD), lambda i, h: (h, i, 0)),
                   pl.BlockSpec((1, tr, MLA_QK_PAD), lambda i, h: (h, i, 0)),
                   pl.BlockSpec((1, tr, MLA_V), lambda i, h: (h, i, 0)),
                   pl.BlockSpec((1, tr // ATT_BLK, MLA_V, ATT_BLK), lambda i, h: (h, i, 0, 0))],
        out_shape=[jax.ShapeDtypeStruct((MLA_HEADS, t, MLA_QK_PAD), BF16),
                   jax.ShapeDtypeStruct((MLA_HEADS, t, MLA_QK_PAD), BF16),
                   jax.ShapeDtypeStruct((MLA_HEADS, t, MLA_V), BF16),
                   jax.ShapeDtypeStruct((MLA_HEADS, t // ATT_BLK, MLA_V, ATT_BLK), BF16)],
        compiler_params=_cp("parallel", "parallel"),
    )(q, kv, proj, tabs)


def _mla_unprep(dq3, dk3, dv3, tabs, *, tr=256, name):
    t = dq3.shape[1]

    def body(dq_ref, dk_ref, dv_ref, tab_ref, qo_ref, kvo_ref, kro_ref):
        kr = jnp.zeros((tr, LANES), F32)
        for h in range(MLA_HEADS):
            c0 = h * MLA_QK_PAD
            qo_ref[:, c0:c0 + MLA_NOPE] = dq_ref[h, :, :MLA_NOPE].astype(BF16)
            qo_ref[:, c0 + MLA_NOPE:c0 + MLA_QK_PAD] = _unrope(dq_ref[h, :, MLA_NOPE:], tab_ref).astype(BF16)
            kvo_ref[:, c0:c0 + MLA_NOPE] = dk_ref[h, :, :MLA_NOPE].astype(BF16)
            kvo_ref[:, c0 + MLA_NOPE:c0 + MLA_QK_PAD] = dv_ref[h].astype(BF16)
            kr = kr + dk_ref[h, :, MLA_NOPE:]
        kro_ref[...] = _unrope(kr, tab_ref).astype(BF16)

    return pl.pallas_call(
        body, name=name, grid=(t // tr,),
        in_specs=[pl.BlockSpec((MLA_HEADS, tr, MLA_QK_PAD), lambda i: (0, i, 0)),
                  pl.BlockSpec((MLA_HEADS, tr, MLA_QK_PAD), lambda i: (0, i, 0)),
                  pl.BlockSpec((MLA_HEADS, tr, MLA_V), lambda i: (0, i, 0)),
                  pl.BlockSpec((3, tr, LANES), lambda i: (0, i, 0))],
        out_specs=[pl.BlockSpec((tr, MLA_HEADS * MLA_QK_PAD), lambda i: (i, 0)),
                   pl.BlockSpec((tr, MLA_HEADS * MLA_QK_PAD), lambda i: (i, 0)),
                   pl.BlockSpec((tr, LANES), lambda i: (i, 0))],
        out_shape=[jax.ShapeDtypeStruct((t, MLA_HEADS * MLA_QK_PAD), BF16),
                   jax.ShapeDtypeStruct((t, MLA_HEADS * MLA_QK_PAD), BF16),
                   jax.ShapeDtypeStruct((t, LANES), BF16)],
        compiler_params=_cp("parallel"),
    )(dq3, dk3, dv3, tabs)


ATT_BLK = 512
ATT_SCALE = 1.0 / math.sqrt(MLA_NOPE + MLA_ROPE)
_NT = (((1,), (1,)), ((), ()))
_TN = (((0,), (0,)), ((), ()))


def _att_scores_t(k, q, diagonal):
    s = lax.dot_general(k, q, _NT, preferred_element_type=F32) * ATT_SCALE
    if not diagonal:
        return s
    key = lax.broadcasted_iota(jnp.int32, s.shape, 0)
    query = lax.broadcasted_iota(jnp.int32, s.shape, 1)
    return jnp.where((key >> 6) <= (query >> 6), s, NEG)


def _att_rows(i):
    return pl.ds(pl.multiple_of(i * ATT_BLK, ATT_BLK), ATT_BLK)


ATT_HEADS = 2


def _attn_fwd(q3, k3, vt4, *, name):
    t = q3.shape[1]
    nq = t // ATT_BLK

    def body(q_ref, k_ref, vt_ref, o_ref, lse_ref):
        qi = pl.program_id(1)
        qs = [q_ref[hh] for hh in range(ATT_HEADS)]

        def step(j, carry, diagonal=False):
            out = []
            for hh, (m, l, acc) in enumerate(carry):
                s = _att_scores_t(k_ref[hh, _att_rows(j), :], qs[hh], diagonal)
                m_new = jnp.maximum(m, jnp.max(s, axis=0, keepdims=True))
                p = jnp.exp(s - m_new)
                alpha = jnp.exp(m - m_new)
                l = alpha * l + jnp.sum(p, axis=0, keepdims=True)
                acc = alpha * acc + jnp.dot(vt_ref[hh, j], p.astype(BF16), preferred_element_type=F32)
                out.append((m_new, l, acc))
            return tuple(out)

        init = tuple((jnp.full((1, ATT_BLK), NEG, F32), jnp.zeros((1, ATT_BLK), F32),
                      jnp.zeros((MLA_V, ATT_BLK), F32)) for _ in range(ATT_HEADS))
        done = step(qi, lax.fori_loop(0, qi, step, init), diagonal=True)
        for hh, (m, l, acc) in enumerate(done):
            o_ref[:, hh * MLA_V:(hh + 1) * MLA_V] = (acc / l).T
            lse_ref[hh, 0] = m + jnp.log(l)

    return pl.pallas_call(
        body, name=name, grid=(MLA_HEADS // ATT_HEADS, nq),
        in_specs=[pl.BlockSpec((ATT_HEADS, ATT_BLK, MLA_QK_PAD), lambda h, i: (h, i, 0)),
                  pl.BlockSpec((ATT_HEADS, t, MLA_QK_PAD), lambda h, i: (h, 0, 0)),
                  pl.BlockSpec((ATT_HEADS, nq, MLA_V, ATT_BLK), lambda h, i: (h, 0, 0, 0))],
        out_specs=[pl.BlockSpec((ATT_BLK, ATT_HEADS * MLA_V), lambda h, i: (i, h)),
                   pl.BlockSpec((ATT_HEADS, 1, 1, ATT_BLK), lambda h, i: (h, i, 0, 0))],
        out_shape=[jax.ShapeDtypeStruct((t, MLA_HEADS * MLA_V), F32),
                   jax.ShapeDtypeStruct((MLA_HEADS, nq, 1, ATT_BLK), F32)],
        compiler_params=_cp("parallel", "parallel"),
    )(q3, k3, vt4)


def _attn_bwd(q3, k3, v3, o, dcat, lse, *, name):
    t = q3.shape[1]
    nq = t // ATT_BLK
    wide = ATT_HEADS * MLA_V

    def body(q_ref, k_ref, v_ref, o_ref, do_ref, lse_ref, dq_ref, dk_ref, dv_ref, delta_ref):
        kj = pl.program_id(1)

        @pl.when(kj == 0)
        def _():
            dq_ref[...] = jnp.zeros_like(dq_ref)
            ones = jnp.ones((HALO, MLA_V), F32)
            for i in range(nq):
                rows = pl.ds(i * ATT_BLK, ATT_BLK)
                prod = o_ref[rows, :] * do_ref[rows, :]
                for hh in range(ATT_HEADS):
                    delta_ref[hh, i] = lax.dot_general(ones, prod[:, hh * MLA_V:(hh + 1) * MLA_V], _NT, precision=HI,
                                                       preferred_element_type=F32)

        def step(i, carry, diagonal=False):
            rows = _att_rows(i)
            out = []
            for hh, (dk, dv) in enumerate(carry):
                k, v = k_ref[hh], v_ref[hh]
                q = q_ref[hh, rows, :]
                dob = do_ref[rows, hh * MLA_V:(hh + 1) * MLA_V].astype(BF16)
                p = jnp.exp(_att_scores_t(k, q, diagonal) - lse_ref[hh, i])
                dv = dv + jnp.dot(p.astype(BF16), dob, preferred_element_type=F32)
                dp = lax.dot_general(v, dob, _NT, preferred_element_type=F32)
                ds = (p * (dp - delta_ref[hh, i, 0:1, :]) * ATT_SCALE).astype(BF16)
                dk = dk + jnp.dot(ds, q, preferred_element_type=F32)
                dq_ref[hh, rows, :] += lax.dot_general(ds, k, _TN, preferred_element_type=F32)
                out.append((dk, dv))
            return tuple(out)

        init = tuple((jnp.zeros((ATT_BLK, MLA_QK_PAD), F32), jnp.zeros((ATT_BLK, MLA_V), F32))
                     for _ in range(ATT_HEADS))
        done = lax.fori_loop(kj + 1, nq, step, step(kj, init, diagonal=True))
        for hh, (dk, dv) in enumerate(done):
            dk_ref[hh] = dk
            dv_ref[hh] = dv

    return pl.pallas_call(
        body, name=name, grid=(MLA_HEADS // ATT_HEADS, nq),
        in_specs=[pl.BlockSpec((ATT_HEADS, t, MLA_QK_PAD), lambda h, j: (h, 0, 0)),
                  pl.BlockSpec((ATT_HEADS, ATT_BLK, MLA_QK_PAD), lambda h, j: (h, j, 0)),
                  pl.BlockSpec((ATT_HEADS, ATT_BLK, MLA_V), lambda h, j: (h, j, 0)),
                  pl.BlockSpec((t, wide), lambda h, j: (0, h)),
                  pl.BlockSpec((t, wide), lambda h, j: (0, MLA_HEADS // ATT_HEADS + h)),
                  pl.BlockSpec((ATT_HEADS, nq, 1, ATT_BLK), lambda h, j: (h, 0, 0, 0))],
        out_specs=[pl.BlockSpec((ATT_HEADS, t, MLA_QK_PAD), lambda h, j: (h, 0, 0)),
                   pl.BlockSpec((ATT_HEADS, ATT_BLK, MLA_QK_PAD), lambda h, j: (h, j, 0)),
                   pl.BlockSpec((ATT_HEADS, ATT_BLK, MLA_V), lambda h, j: (h, j, 0))],
        out_shape=[jax.ShapeDtypeStruct((MLA_HEADS, t, MLA_QK_PAD), F32),
                   jax.ShapeDtypeStruct((MLA_HEADS, t, MLA_QK_PAD), F32),
                   jax.ShapeDtypeStruct((MLA_HEADS, t, MLA_V), F32)],
        scratch_shapes=[pltpu.VMEM((ATT_HEADS, nq, HALO, ATT_BLK), F32)],
        compiler_params=_cp("parallel", "arbitrary"),
    )(q3, k3, v3, o, dcat, lse)


def _ssd_prep(proj, bias128, alog128, *, name):
    t = proj.shape[0]
    nc = t // CHUNK

    def body(raw_ref, b_ref, al_ref, dt_ref, cs_ref, a_ref):
        xv = raw_ref[...] + b_ref[...]
        dt = jnp.maximum(xv, 0.0) + jnp.log(1.0 + jnp.exp(-jnp.abs(xv)))
        a = -jnp.exp(al_ref[...])
        adt = (dt * a).reshape(nc, CHUNK, LANES)
        li = lax.broadcasted_iota(jnp.int32, (nc, CHUNK, CHUNK), 1)
        si = lax.broadcasted_iota(jnp.int32, (nc, CHUNK, CHUNK), 2)
        tril = jnp.where(si <= li, 1.0, 0.0).astype(F32)
        cs = lax.dot_general(tril, adt, (((2,), (1,)), ((0,), (0,))), precision=HI, preferred_element_type=F32)
        dt_ref[...] = dt
        cs_ref[...] = cs.reshape(t, LANES)
        a_ref[...] = a

    blk = pl.BlockSpec((t, LANES), lambda i: (0, 0))
    row = pl.BlockSpec((1, LANES), lambda i: (0, 0))
    return pl.pallas_call(
        body, name=name, grid=(1,),
        in_specs=[pl.BlockSpec((t, LANES), lambda i: (0, OFF_DT // LANES)), row, row],
        out_specs=[blk, blk, row],
        out_shape=[jax.ShapeDtypeStruct((t, LANES), F32), jax.ShapeDtypeStruct((t, LANES), F32),
                   jax.ShapeDtypeStruct((1, LANES), F32)],
        compiler_params=_cp("arbitrary"),
    )(proj, bias128, alog128)


def _ssd_prep_bwd(ddt128, dadt128, proj, bias128, dt128, a128, dd_h, *, name):
    t = proj.shape[0]

    def body(ddt_ref, dadt_ref, raw_ref, b_ref, dt_ref, a_ref, dd_ref, draw_ref, db_ref, dal_ref, dds_ref):
        draw = ddt_ref[...] * _sigmoid(raw_ref[...] + b_ref[...])
        draw_ref[...] = draw.astype(BF16)
        db_ref[...] = jnp.sum(draw, axis=0, keepdims=True)
        dal_ref[...] = jnp.sum(dadt_ref[...] * dt_ref[...], axis=0, keepdims=True) * a_ref[...]
        dds_ref[...] = jnp.sum(dd_ref[...], axis=-1, keepdims=True)

    blk = pl.BlockSpec((t, LANES), lambda i: (0, 0))
    row = pl.BlockSpec((1, LANES), lambda i: (0, 0))
    return pl.pallas_call(
        body, name=name, grid=(1,),
        in_specs=[blk, blk, pl.BlockSpec((t, LANES), lambda i: (0, OFF_DT // LANES)), row, blk, row,
                  pl.BlockSpec((SSD_HEADS, SSD_P), lambda i: (0, 0))],
        out_specs=[blk, row, row, pl.BlockSpec((SSD_HEADS, 1), lambda i: (0, 0))],
        out_shape=[jax.ShapeDtypeStruct((t, LANES), BF16), jax.ShapeDtypeStruct((1, LANES), F32),
                   jax.ShapeDtypeStruct((1, LANES), F32), jax.ShapeDtypeStruct((SSD_HEADS, 1), F32)],
        compiler_params=_cp("arbitrary"),
    )(ddt128, dadt128, proj, bias128, dt128, a128, dd_h)


def _bdot(a, b, ca, cb, precision=None):
    return lax.dot_general(a, b, (((ca,), (cb,)), ((0,), (0,))), precision=precision, preferred_element_type=F32)


def _head_matrices():
    eye, zero = jnp.eye(SSD_P, dtype=F32), jnp.zeros((SSD_P, SSD_P), F32)
    pick = jnp.stack([jnp.concatenate([eye, zero], axis=0), jnp.concatenate([zero, eye], axis=0)])
    return pick, pick.transpose(0, 2, 1)


def _move(x, sel):
    selb = sel.astype(BF16)
    hi = x.astype(BF16)
    rest = x - hi.astype(F32)
    mid = rest.astype(BF16)
    low = (rest - mid.astype(F32)).astype(BF16)
    out = jnp.dot(hi, selb, preferred_element_type=F32)
    out = out + jnp.dot(mid, selb, preferred_element_type=F32)
    return out + jnp.dot(low, selb, preferred_element_type=F32)


def _pick_head(pair_ref, pick_ref, h):
    return _move(pair_ref[...], pick_ref[h % 2])


def _place_head(out_ref, val, place_ref, h):
    wide = _move(val, place_ref[h % 2])

    @pl.when(h % 2 == 0)
    def _():
        out_ref[...] = wide

    @pl.when(h % 2 == 1)
    def _():
        out_ref[...] += wide


def _ssd_common(x2, dt_ref, cs_ref, csr_ref, b_ref, c_ref, nc):
    x = x2.reshape(nc, CHUNK, SSD_P)
    dt = dt_ref[0].reshape(nc, CHUNK, SSD_P)
    cs = cs_ref[0].reshape(nc, CHUNK, SSD_P)
    csr = csr_ref[0]
    bm = b_ref[...].reshape(nc, CHUNK, SSD_N).astype(BF16)
    cm = c_ref[...].reshape(nc, CHUNK, SSD_N).astype(BF16)
    li = lax.broadcasted_iota(jnp.int32, (nc, CHUNK, CHUNK), 1)
    si = lax.broadcasted_iota(jnp.int32, (nc, CHUNK, CHUNK), 2)
    lmat = jnp.exp(jnp.where(si <= li, cs - csr, NEG))
    g = _bdot(cm, bm, 2, 2)
    cs_last = jnp.sum(jnp.where(li == CHUNK - 1, cs, 0.0), axis=1, keepdims=True)
    xdt = x * dt
    dec = jnp.exp(cs_last - cs)
    return x, dt, cs, bm, cm, li, si, lmat, g, cs_last, xdt, dec


def _ssd_fwd(xbc, dt_h, cs_h, cs_row, dskip_h, *, name):
    t = xbc.shape[0]
    nc = t // CHUNK
    hpg = SSD_HEADS // SSD_GROUPS
    pick, place = _head_matrices()

    def body(xs_ref, dt_ref, cs_ref, csr_ref, b_ref, c_ref, dk_ref, pick_ref, place_ref, y_ref, st_ref, sc_ref, cd_ref):
        h = pl.program_id(0)
        x, dt, cs, bm, cm, li, si, lmat, g, cs_last, xdt, dec = _ssd_common(_pick_head(xs_ref, pick_ref, h), dt_ref,
                                                                           cs_ref, csr_ref, b_ref, c_ref, nc)
        yd = _bdot((g * lmat).astype(BF16), xdt.astype(BF16), 2, 1)
        sc_ref[...] = _bdot(bm, (dec * xdt).astype(BF16), 1, 1)
        cd_ref[...] = jnp.exp(cs_last)

        def step(c, s):
            st_ref[0, c] = s
            return s * cd_ref[c] + sc_ref[c]

        lax.fori_loop(0, nc, step, jnp.zeros((SSD_N, SSD_P), F32))
        yo = _bdot(cm, st_ref[0].astype(BF16), 2, 1) * jnp.exp(cs)
        _place_head(y_ref, (yd + yo + dk_ref[0] * x).reshape(t, SSD_P), place_ref, h)

    head = pl.BlockSpec((1, t, SSD_P), lambda h: (h, 0, 0))
    pair = pl.BlockSpec((t, 2 * SSD_P), lambda h: (0, h // 2))
    nxb = D_SSM // SSD_N
    return pl.pallas_call(
        body, name=name, grid=(SSD_HEADS,),
        in_specs=[pair, head, head, pl.BlockSpec((1, nc, 1, CHUNK), lambda h: (h, 0, 0, 0)),
                  pl.BlockSpec((t, SSD_N), lambda h: (0, nxb + h // hpg)),
                  pl.BlockSpec((t, SSD_N), lambda h: (0, nxb + SSD_GROUPS + h // hpg)),
                  pl.BlockSpec((1, 1, SSD_P), lambda h: (h, 0, 0)),
                  pl.BlockSpec((2, 2 * SSD_P, SSD_P), lambda h: (0, 0, 0)),
                  pl.BlockSpec((2, SSD_P, 2 * SSD_P), lambda h: (0, 0, 0))],
        out_specs=[pair, pl.BlockSpec((1, nc, SSD_N, SSD_P), lambda h: (h, 0, 0, 0))],
        out_shape=[jax.ShapeDtypeStruct((t, D_SSM), F32),
                   jax.ShapeDtypeStruct((SSD_HEADS, nc, SSD_N, SSD_P), F32)],
        scratch_shapes=[pltpu.VMEM((nc, SSD_N, SSD_P), F32), pltpu.VMEM((nc, 1, SSD_P), F32)],
        compiler_params=_cp("arbitrary"),
    )(xbc, dt_h, cs_h, cs_row, xbc, xbc, dskip_h, pick, place)


def _ssd_bwd(xbc, dt_h, cs_h, cs_row, dskip_h, a_h, states, dy, *, name):
    t = xbc.shape[0]
    nc = t // CHUNK
    hpg = SSD_HEADS // SSD_GROUPS
    pick, place = _head_matrices()

    def body(xs_ref, dt_ref, cs_ref, csr_ref, b_ref, c_ref, dk_ref, a_ref, st_ref, dy_ref, pick_ref, place_ref,
             dxs_ref, ddt_ref, dadt_ref, db_ref, dc_ref, dd_ref, dsl_ref, dsc_ref, cd_ref):
        h = pl.program_id(0) * hpg + pl.program_id(1)
        x, dt, cs, bm, cm, li, si, lmat, g, cs_last, xdt, dec = _ssd_common(_pick_head(xs_ref, pick_ref, h), dt_ref,
                                                                           cs_ref, csr_ref, b_ref, c_ref, nc)
        dy = _pick_head(dy_ref, pick_ref, h).reshape(nc, CHUNK, SSD_P)
        dyb = dy.astype(BF16)
        xdtb = xdt.astype(BF16)
        sprev = st_ref[0]
        sprevb = sprev.astype(BF16)
        cdec = jnp.exp(cs_last)
        ecs = jnp.exp(cs)
        dw = (ecs * dy).astype(BF16)
        wmat = _bdot(cm, sprevb, 2, 1)
        dcs = jnp.sum(dy * ecs * wmat, axis=2, keepdims=True)
        dcm = _bdot(dw, sprevb, 2, 2)
        dsl_ref[...] = _bdot(cm, dw, 1, 1)
        cd_ref[...] = cdec

        def step(k, ds):
            c = nc - 1 - k
            dsc_ref[c] = ds
            return ds * cd_ref[c] + dsl_ref[c]

        lax.fori_loop(0, nc, step, jnp.zeros((SSD_N, SSD_P), F32))
        dsc = dsc_ref[...]
        dscb = dsc.astype(BF16)
        d_last = jnp.sum(jnp.sum(dsc * sprev, axis=1, keepdims=True) * cdec, axis=2, keepdims=True)
        z = dec * xdt
        dbm = _bdot(z.astype(BF16), dscb, 2, 2)
        dz = _bdot(bm, dscb, 2, 1)
        dxdt = dec * dz
        t2 = jnp.sum(dz * z, axis=2, keepdims=True)
        dcs = dcs - t2
        d_last = d_last + jnp.sum(t2, axis=1, keepdims=True)
        m = g * lmat
        mb = m.astype(BF16)
        dm = _bdot(dyb, xdtb, 2, 2)
        dxdt = dxdt + _bdot(mb, dyb, 1, 1)
        dseg = dm * m
        dcs = dcs + jnp.sum(dseg, axis=2, keepdims=True)
        ones = jnp.ones((nc, CHUNK, SSD_P), F32)
        dcs = dcs - _bdot(dseg, ones, 1, 1, precision=HI)
        dg = (dm * lmat).astype(BF16)
        dcm = dcm + _bdot(dg, bm, 2, 1)
        dbm = dbm + _bdot(dg, cm, 1, 1)
        dcs = dcs + jnp.where(li[:, :, :SSD_P] == CHUNK - 1, d_last, 0.0)
        triu = jnp.where(li <= si, 1.0, 0.0).astype(F32)
        dadt = _bdot(triu, dcs, 2, 1, precision=HI)
        dk = dk_ref[0]
        _place_head(dxs_ref, (dxdt * dt + dk * dy).reshape(t, SSD_P), place_ref, h)
        ddt = jnp.sum(dxdt * x, axis=2, keepdims=True) + dadt * a_ref[0]
        mine = lax.broadcasted_iota(jnp.int32, (t, LANES), 1) == h

        @pl.when(h == 0)
        def _():
            ddt_ref[...] = jnp.zeros_like(ddt_ref)
            dadt_ref[...] = jnp.zeros_like(dadt_ref)

        ddt_ref[...] += jnp.where(mine, jnp.max(ddt, axis=2, keepdims=True).reshape(t, 1), 0.0)
        dadt_ref[...] += jnp.where(mine, jnp.max(dadt, axis=2, keepdims=True).reshape(t, 1), 0.0)
        dd_ref[0] = jnp.sum(jnp.sum(dy * x, axis=1, keepdims=True), axis=0)

        @pl.when(pl.program_id(1) == 0)
        def _():
            db_ref[...] = jnp.zeros_like(db_ref)
            dc_ref[...] = jnp.zeros_like(dc_ref)

        db_ref[...] += dbm.reshape(t, SSD_N)
        dc_ref[...] += dcm.reshape(t, SSD_N)

    head = pl.BlockSpec((1, t, SSD_P), lambda gi, hi: (gi * hpg + hi, 0, 0))
    pair = pl.BlockSpec((t, 2 * SSD_P), lambda gi, hi: (0, (gi * hpg + hi) // 2))
    grp = pl.BlockSpec((t, SSD_N), lambda gi, hi: (0, gi))
    lane = pl.BlockSpec((1, 1, SSD_P), lambda gi, hi: (gi * hpg + hi, 0, 0))
    rows = pl.BlockSpec((t, LANES), lambda gi, hi: (0, 0))
    nxb = D_SSM // SSD_N
    dxs, ddt, dadt, db, dc, dd = pl.pallas_call(
        body, name=name, grid=(SSD_GROUPS, hpg),
        in_specs=[pair, head, head, pl.BlockSpec((1, nc, 1, CHUNK), lambda gi, hi: (gi * hpg + hi, 0, 0, 0)),
                  pl.BlockSpec((t, SSD_N), lambda gi, hi: (0, nxb + gi)),
                  pl.BlockSpec((t, SSD_N), lambda gi, hi: (0, nxb + SSD_GROUPS + gi)), lane, lane,
                  pl.BlockSpec((1, nc, SSD_N, SSD_P), lambda gi, hi: (gi * hpg + hi, 0, 0, 0)), pair,
                  pl.BlockSpec((2, 2 * SSD_P, SSD_P), lambda gi, hi: (0, 0, 0)),
                  pl.BlockSpec((2, SSD_P, 2 * SSD_P), lambda gi, hi: (0, 0, 0))],
        out_specs=[pair, rows, rows, grp, grp, lane],
        out_shape=[jax.ShapeDtypeStruct((t, D_SSM), F32)] + [jax.ShapeDtypeStruct((t, LANES), F32)] * 2
        + [jax.ShapeDtypeStruct((t, SSD_GROUPS * SSD_N), F32)] * 2
        + [jax.ShapeDtypeStruct((SSD_HEADS, 1, SSD_P), F32)],
        scratch_shapes=[pltpu.VMEM((nc, SSD_N, SSD_P), F32), pltpu.VMEM((nc, SSD_N, SSD_P), F32),
                        pltpu.VMEM((nc, 1, SSD_P), F32)],
        compiler_params=_cp("arbitrary", "arbitrary"),
    )(xbc, dt_h, cs_h, cs_row, xbc, xbc, dskip_h, a_h, states, dy, pick, place)
    return jnp.concatenate([dxs, db, dc], axis=1), ddt, dadt, dd


def _ssd_gate_fwd(y, proj, w, *, tr=256, name):
    t = y.shape[0]
    gw = D_SSM // SSD_GROUPS

    def body(y_ref, z_ref, w_ref, o_ref):
        v = y_ref[...] * _silu(z_ref[...])
        for gi in range(SSD_GROUPS):
            vg = v[:, gi * gw:(gi + 1) * gw]
            r = lax.rsqrt(jnp.mean(vg * vg, axis=-1, keepdims=True) + NORM_EPS)
            o_ref[:, gi * gw:(gi + 1) * gw] = (vg * r * w_ref[:, gi * gw:(gi + 1) * gw]).astype(BF16)

    blk = pl.BlockSpec((tr, D_SSM), lambda i: (i, 0))
    return pl.pallas_call(
        body, name=name, grid=(t // tr,), in_specs=[blk, blk, pl.BlockSpec((1, D_SSM), lambda i: (0, 0))],
        out_specs=blk, out_shape=jax.ShapeDtypeStruct((t, D_SSM), BF16), compiler_params=_cp("parallel"),
    )(y, proj, w)


def _ssd_gate_bwd(y, proj, w, dcat, *, tr=256, name):
    t = y.shape[0]
    gw = D_SSM // SSD_GROUPS

    def body(y_ref, z_ref, w_ref, d_ref, dy_ref, dz_ref, dw_ref):
        yv, zv, dv = y_ref[...], z_ref[...], d_ref[...].astype(F32)
        sz = _silu(zv)
        v = yv * sz

        @pl.when(pl.program_id(0) == 0)
        def _():
            dw_ref[...] = jnp.zeros_like(dw_ref)

        for gi in range(SSD_GROUPS):
            sl = slice(gi * gw, (gi + 1) * gw)
            vg, dg = v[:, sl], dv[:, sl]
            r = lax.rsqrt(jnp.mean(vg * vg, axis=-1, keepdims=True) + NORM_EPS)
            vh = vg * r
            gg = dg * w_ref[:, sl]
            dvg = r * (gg - vh * jnp.mean(gg * vh, axis=-1, keepdims=True))
            dy_ref[:, sl] = dvg * sz[:, sl]
            dz_ref[:, sl] = (dvg * yv[:, sl] * _dsilu(zv[:, sl])).astype(BF16)
            dw_ref[:, sl] += jnp.sum(dg * vh, axis=0, keepdims=True)

    blk = pl.BlockSpec((tr, D_SSM), lambda i: (i, 0))
    row = pl.BlockSpec((1, D_SSM), lambda i: (0, 0))
    return pl.pallas_call(
        body, name=name, grid=(t // tr,), in_specs=[blk, blk, row, blk], out_specs=[blk, blk, row],
        out_shape=[jax.ShapeDtypeStruct((t, D_SSM), F32), jax.ShapeDtypeStruct((t, D_SSM), BF16),
                   jax.ShapeDtypeStruct((1, D_SSM), F32)],
        compiler_params=_cp("arbitrary"),
    )(y, proj, w, dcat)


def _pad_lanes(v):
    return jnp.pad(v, ((0, 0), (0, LANES - v.shape[1])))


def _per_head(v128, t):
    return jnp.broadcast_to(v128[:, :SSD_HEADS].T[:, :, None], (SSD_HEADS, t, SSD_P))


def _ssd_forward(proj, conv_w, conv_b, dt_bias, a_log, d_skip, ssd_norm_w):
    t = proj.shape[0]
    nc = t // CHUNK
    xbc = _conv_act_fwd(proj, conv_w, conv_b, kw=SSD_CONV, glu=False, tc=512, coff=OFF_XBC // 512,
                        ncols=SSD_CONV_DIM, out_dtype=F32, name="ssd_conv_fwd")
    bias128, alog128 = _pad_lanes(dt_bias), _pad_lanes(a_log)
    dt128, cs128, a128 = _ssd_prep(proj, bias128, alog128, name="ssd_prep")
    dt_h, cs_h = _per_head(dt128, t), _per_head(cs128, t)
    cs_row = cs128[:, :SSD_HEADS].T.reshape(SSD_HEADS, nc, 1, CHUNK)
    dskip_h = jnp.broadcast_to(d_skip[0][:, None, None], (SSD_HEADS, 1, SSD_P))
    a_h = jnp.broadcast_to(a128[0, :SSD_HEADS][:, None, None], (SSD_HEADS, 1, SSD_P))
    y, states = _ssd_fwd(xbc, dt_h, cs_h, cs_row, dskip_h, name="ssd_scan_fwd")
    y_ssd = _ssd_gate_fwd(y, proj, ssd_norm_w, name="ssd_gate_fwd")
    saved = (proj, conv_w, conv_b, ssd_norm_w, bias128, dt128, a128, dt_h, cs_h, cs_row, xbc, dskip_h, a_h, states, y)
    return y_ssd, saved


def _ssd_backward(saved, dcat):
    proj, conv_w, conv_b, ssd_norm_w, bias128, dt128, a128, dt_h, cs_h, cs_row, xbc, dskip_h, a_h, states, y = saved
    dy, dz, d_norm_w = _ssd_gate_bwd(y, proj, ssd_norm_w, dcat, name="ssd_gate_bwd")
    dxc, ddt128, dadt128, dd_h = _ssd_bwd(xbc, dt_h, cs_h, cs_row, dskip_h, a_h, states, dy, name="ssd_scan_bwd")
    dxbc, d_conv_w, d_conv_b = _conv_act_bwd(proj, conv_w, conv_b, dxc, kw=SSD_CONV, glu=False, tc=512,
                                             coff=OFF_XBC // 512, ncols=SSD_CONV_DIM, name="ssd_conv_bwd")
    d_raw, d_bias, d_alog, d_dskip = _ssd_prep_bwd(ddt128, dadt128, proj, bias128, dt128, a128,
                                                   dd_h.reshape(SSD_HEADS, SSD_P), name="ssd_prep_bwd")
    return (dz, dxbc, d_raw, d_norm_w, d_conv_w, d_conv_b, d_bias[:, :SSD_HEADS], d_alog[:, :SSD_HEADS],
            d_dskip.reshape(1, SSD_HEADS))


def _rope_tables(positions):
    inv_freq = ROPE_THETA ** (-jnp.arange(0, MLA_ROPE, 2, dtype=F32) / MLA_ROPE)
    ang = positions[0].astype(F32)[:, None] * inv_freq
    cos, sin = jnp.cos(ang), jnp.sin(ang)
    z = jnp.zeros_like(cos)
    return jnp.stack([jnp.concatenate([cos, cos, z, z], axis=1), jnp.concatenate([-sin, z, z, z], axis=1),
                      jnp.concatenate([z, sin, z, z], axis=1)])


def _mla_forward(proj, tabs, q_a_norm_w, wq_pad, kv_a_norm_w, wkv):
    qn = _rmsnorm_fwd(proj, q_a_norm_w, width=MLA_Q_RANK, cblk=OFF_QA // MLA_Q_RANK, name="q_a_norm")
    q = _matmul(qn, wq_pad, name="q_b_proj")
    kvn = _rmsnorm_fwd(proj, kv_a_norm_w, width=MLA_KV_RANK, cblk=OFF_CKV // MLA_KV_RANK, name="kv_a_norm")
    kv = _matmul(kvn, wkv, name="kv_b_proj")
    q3, k3, v3, vt4 = _mla_prep(q, kv, proj, tabs, name="mla_prep")
    o, lse = _attn_fwd(q3, k3, vt4, name="attn_fwd")
    return o, (proj, tabs, q_a_norm_w, wq_pad, kv_a_norm_w, wkv, qn, kvn, q3, k3, v3, o, lse)


def _mla_backward(saved, dcat):
    proj, tabs, q_a_norm_w, wq_pad, kv_a_norm_w, wkv, qn, kvn, q3, k3, v3, o, lse = saved
    dq3, dk3, dv3 = _attn_bwd(q3, k3, v3, o, dcat, lse, name="attn_bwd")
    dq, dkv, dkr = _mla_unprep(dq3, dk3, dv3, tabs, name="mla_unprep")
    d_wq = _matmul(qn, dq, ta=True, out_dtype=BF16, name="d_w_q_b")
    dqn = _matmul(dq, wq_pad, tb=True, name="d_qn")
    dq_a, d_qnw = _rmsnorm_bwd(proj, q_a_norm_w, dqn, width=MLA_Q_RANK, cblk=OFF_QA // MLA_Q_RANK, out_dtype=BF16,
                               name="q_a_norm_bwd")
    d_wkv = _matmul(kvn, dkv, ta=True, out_dtype=BF16, name="d_w_kv_b")
    dkvn = _matmul(dkv, wkv, tb=True, name="d_kvn")
    dckv, d_kvnw = _rmsnorm_bwd(proj, kv_a_norm_w, dkvn, width=MLA_KV_RANK, cblk=OFF_CKV // MLA_KV_RANK,
                                out_dtype=BF16, name="kv_a_norm_bwd")
    return dq_a, dckv, dkr, d_wq, d_wkv, d_qnw, d_kvnw


def _pad_w_q(w):
    r = w.shape[0]
    w3 = w.reshape(r, MLA_HEADS, MLA_NOPE + MLA_ROPE)
    return jnp.pad(w3, ((0, 0), (0, 0), (0, MLA_QK_PAD - MLA_NOPE - MLA_ROPE))).reshape(r, MLA_HEADS * MLA_QK_PAD)


def _unpad_w_q(w):
    r = w.shape[0]
    return w.reshape(r, MLA_HEADS, MLA_QK_PAD)[:, :, :MLA_NOPE + MLA_ROPE].reshape(r, MLA_HEADS * (MLA_NOPE + MLA_ROPE))


W_IN_SEGMENTS = ((0, D_SSM + SSD_CONV_DIM, 0), (D_SSM + SSD_CONV_DIM, D_SSM + SSD_CONV_DIM + SSD_HEADS, OFF_DT),
                 (D_SSM + SSD_CONV_DIM + SSD_HEADS, D_IN - MLA_ROPE, OFF_QA), (D_IN - MLA_ROPE, D_IN, OFF_KR))


def _pad_w_in_shards(g):
    n = g.shape[2]
    pieces, at = [], 0
    for lo, hi, start in sorted(W_IN_SEGMENTS, key=lambda seg: seg[2]):
        if start > at:
            pieces.append(jnp.zeros((g.shape[1], start - at), g.dtype))
        for j in range(N_DEV):
            a, b = max(lo, j * n), min(hi, (j + 1) * n)
            if a < b:
                pieces.append(g[j][:, a - j * n:b - j * n])
        at = start + hi - lo
    pieces.append(jnp.zeros((g.shape[1], D_IN_PAD - at), g.dtype))
    return jnp.concatenate(pieces, axis=1)


def _unpad_w_in_shards(w):
    n = D_IN // N_DEV
    shards = []
    for j in range(N_DEV):
        pieces = []
        for lo, hi, start in W_IN_SEGMENTS:
            a, b = max(lo, j * n), min(hi, (j + 1) * n)
            if a < b:
                pieces.append(w[:, start + a - lo:start + b - lo])
        shards.append(jnp.concatenate(pieces, axis=1) if len(pieces) > 1 else pieces[0])
    return jnp.stack(shards)


WEIGHTS = ['mix_norm_w', 'w_in', 'conv_w', 'conv_b', 'dt_bias', 'a_log', 'd_skip', 'ssd_norm_w', 'q_a_norm_w', 'w_q_b',
           'kv_a_norm_w', 'w_kv_b', 'w_out', 'ffn_norm_w', 'w_ffn_up', 'ffn_conv_w', 'ffn_conv_b', 'w_ffn_down',
           'ple_norm_w', 'w_ple_gate', 'b_ple_gate', 'w_ple_proj', 'ple_post_norm_w', 'final_norm_w']
BIG = ['w_in', 'w_q_b', 'w_kv_b', 'w_out', 'w_ffn_up', 'w_ffn_down', 'w_ple_gate', 'w_ple_proj']
COL_SHARDED = ('w_in', 'w_q_b', 'w_kv_b', 'w_ffn_up', 'w_ple_proj')
CONV = ['conv_w', 'ffn_conv_w']
REPL = [n for n in WEIGHTS if n not in BIG and n not in CONV]
FFN_INV = tuple(int(i) for i in np.argsort(FFN_PERM))


def _cat_cols(g):
    return jnp.concatenate([g[j] for j in range(N_DEV)], axis=1)


def _split_cols(w):
    n = w.shape[1] // N_DEV
    return jnp.stack([w[:, j * n:(j + 1) * n] for j in range(N_DEV)])


def _interleave(v):
    r = v.shape[0]
    return v.reshape(r, N_DEV, FFN_TC)[:, jnp.array(FFN_PERM)].reshape(r, N_DEV * FFN_TC)


def _deinterleave(v):
    r = v.shape[0]
    return v.reshape(r, N_DEV, FFN_TC)[:, jnp.array(FFN_INV)].reshape(r, N_DEV * FFN_TC)


def _assemble_weights(g):
    layout = {
        'w_in': _pad_w_in_shards,
        'w_q_b': lambda v: _pad_w_q(_cat_cols(v)),
        'w_kv_b': _cat_cols,
        'w_out': lambda v: v.reshape(D_MODEL, D_MODEL),
        'w_ffn_up': lambda v: v,
        'w_ffn_down': lambda v: v.reshape(D_FF, D_MODEL),
        'w_ple_gate': lambda v: v.reshape(D_MODEL, D_MODEL),
        'w_ple_proj': _cat_cols,
        'conv_w': _cat_cols,
        'ffn_conv_w': lambda v: _interleave(_cat_cols(v)),
    }
    return {n: layout[n](v) for n, v in g.items()}


WEIGHT_GROUPS = {'a': ['w_in', 'w_q_b', 'w_kv_b', 'conv_w'], 'b': ['w_out', 'w_ffn_up', 'ffn_conv_w'],
                 'c': ['w_ffn_down', 'w_ple_gate', 'w_ple_proj']}
GRAD_GROUPS = {'p': ['w_ple_proj', 'w_ple_gate', 'w_ffn_down'], 'r': ['w_ffn_up'], 's': ['w_out'],
               't': ['w_q_b', 'w_kv_b', 'w_in']}


def _ffn_perm(j):
    return (j % 2) * (N_DEV // 2) + j // 2


def _local_step(x, p, tabs, get_w, s, target, emit, relay, settle):
    t = x.shape[0]
    s = dict(s)
    half = D_MODEL // 2
    up_cols = 2 * D_FF
    ffn_conv_b = _interleave(s['ffn_conv_b'])
    w = dict(get_w('a', None))
    h = _rmsnorm_fwd(x, s['mix_norm_w'], width=D_MODEL, name="mix_norm")
    proj = _matmul(h, w['w_in'], name="in_proj")
    y_ssd, ssd_saved = _ssd_forward(proj, w['conv_w'], s['conv_b'], s['dt_bias'], s['a_log'], s['d_skip'],
                                    s['ssd_norm_w'])
    o, mla_saved = _mla_forward(proj, tabs, s['q_a_norm_w'], w['w_q_b'], s['kv_a_norm_w'], w['w_kv_b'])
    tk_o, tn_o = _tile(half, MM_TK), _tile(D_MODEL, MM_TILE)
    w.update(get_w('b', o))
    x1 = _matmul(y_ssd, w['w_out'], add=x, mnk=(t, D_MODEL, half), name="out_proj_ssd")
    x1 = _matmul(o, w['w_out'], add=x1, mnk=(t, D_MODEL, half), name="out_proj_mla",
                 b_spec=pl.BlockSpec((tk_o, tn_o), lambda i, j, kk: (kk + half // tk_o, j)))
    hf = _rmsnorm_fwd(x1, s['ffn_norm_w'], width=D_MODEL, name="ffn_norm")
    tk_u = _tile(D_MODEL, MM_TK)
    u = _matmul(hf, w['w_ffn_up'], mnk=(t, up_cols, D_MODEL), tn=FFN_TC, name="ffn_up",
                b_spec=pl.BlockSpec((1, tk_u, FFN_TC), lambda i, j, kk: (_ffn_perm(j), kk, 0)))
    act = _conv_act_fwd(u, w['ffn_conv_w'], ffn_conv_b, kw=FFN_CONV, glu=True, tc=2 * FFN_TC, coff=0, ncols=up_cols,
                        out_dtype=BF16, name="ffn_act")
    w.update(get_w('c', act))
    x2 = _matmul(act, w['w_ffn_down'], add=x1, name="ffn_down")
    hp = _rmsnorm_fwd(x2, s['ple_norm_w'], width=D_MODEL, name="ple_norm")
    gl = _matmul(hp, w['w_ple_gate'], bias=s['b_ple_gate'], name="ple_gate")
    pe = _matmul(p, w['w_ple_proj'], name="ple_proj")
    x3 = _ple_fwd(x2, gl, pe, s['ple_post_norm_w'], name="ple_mix")
    loss, dx3, d_final = _loss_head(x3, s['final_norm_w'], target, name="loss_head")
    dgl, d_bgate, dpe, d_post = _ple_bwd(dx3, gl, pe, s['ple_post_norm_w'], name="ple_mix_bwd")
    d_wproj = _matmul(p, dpe, ta=True, out_dtype=BF16, name="d_w_ple_proj")
    d_wgate = _matmul(hp, dgl, ta=True, out_dtype=BF16, name="d_w_ple_gate")
    dhp = _matmul(dgl, w['w_ple_gate'], tb=True, name="d_ple_normed")
    dx2, d_plenorm = _rmsnorm_bwd(x2, s['ple_norm_w'], dhp, dx3, width=D_MODEL, name="ple_norm_bwd")
    dact = _matmul(dx2, w['w_ffn_down'], tb=True, name="d_ffn_act")
    d_wdown = _matmul(act, dx2, ta=True, out_dtype=BF16, name="d_w_ffn_down")
    zz = emit('p', {'w_ple_proj': _split_cols(d_wproj), 'w_ple_gate': d_wgate.reshape(N_DEV, D_MODEL // N_DEV, D_MODEL),
                    'w_ffn_down': d_wdown.reshape(N_DEV, D_FF // N_DEV, D_MODEL)})
    du, d_fconv_w, d_fconv_b = _conv_act_bwd(u, w['ffn_conv_w'], ffn_conv_b + zz, dact, kw=FFN_CONV, glu=True,
                                             tc=2 * FFN_TC, coff=0, ncols=up_cols, name="ffn_act_bwd")
    zz = zz + relay('p', du)
    tm_u = _tile(D_MODEL, MM_TILE)
    d_wup = _matmul(hf, du, ta=True, out_dtype=BF16, mnk=(D_MODEL, up_cols, t), tn=FFN_TC, name="d_w_ffn_up",
                    o_spec=pl.BlockSpec((1, tm_u, FFN_TC), lambda i, j, kk: (_ffn_perm(j), i, 0)),
                    o_shape=(N_DEV, D_MODEL, FFN_TC))
    zz = zz + emit('r', {'w_ffn_up': d_wup})
    zero_row = jnp.zeros((1, D_MODEL), F32)
    dhf = _matmul(du, w['w_ffn_up'], tb=True, mnk=(t, D_MODEL, up_cols), tk=FFN_TC, name="d_ffn_normed",
                  bias=zero_row + zz,
                  b_spec=pl.BlockSpec((1, tn_o, FFN_TC), lambda i, j, kk: (_ffn_perm(kk), j, 0)))
    zz = zz + relay('r', dhf) + settle('p')
    dx1, d_ffnnorm = _rmsnorm_bwd(x1, s['ffn_norm_w'] + zz, dhf, dx2, width=D_MODEL, name="ffn_norm_bwd")
    dcat = _matmul(dx1, w['w_out'], tb=True, name="d_mixed")
    d_wout = jnp.concatenate([_matmul(y_ssd, dx1, ta=True, out_dtype=BF16, name="d_w_out_ssd"),
                              _matmul(o, dx1, ta=True, out_dtype=BF16, name="d_w_out_mla")], axis=0)
    zz = zz + emit('s', {'w_out': d_wout.reshape(N_DEV, D_MODEL // N_DEV, D_MODEL)})
    ssd_saved = ssd_saved[:3] + (ssd_saved[3] + zz,) + ssd_saved[4:]
    dz, dxbc, d_raw, d_ssdnorm, d_conv_w, d_conv_b, d_dtb, d_alog, d_dskip = _ssd_backward(ssd_saved, dcat)
    zz = zz + relay('s', dz)
    mla_saved = mla_saved[:-1] + (mla_saved[-1] + zz,)
    dq_a, dckv, dkr, d_wq, d_wkv, d_qnorm, d_kvnorm = _mla_backward(mla_saved, dcat)
    d_raw = (d_raw + settle('r')).astype(BF16)
    dproj = jnp.concatenate([dz, dxbc, dq_a, dckv, dkr, d_raw], axis=1)
    d_win = _matmul(h, dproj, ta=True, out_dtype=BF16, name="d_w_in")
    zz = emit('t', {'w_in': _unpad_w_in_shards(d_win), 'w_q_b': _split_cols(_unpad_w_q(d_wq)),
                    'w_kv_b': _split_cols(d_wkv)}) + settle('s')
    dh = _matmul(dproj, w['w_in'], tb=True, bias=zero_row + zz, name="d_in_normed")
    zz = relay('t', dh)
    dx, d_mixnorm = _rmsnorm_bwd(x, s['mix_norm_w'] + zz, dh, dx1, width=D_MODEL, name="mix_norm_bwd")
    conv = {'conv_w': d_conv_w, 'ffn_conv_w': _deinterleave(d_fconv_w)}
    vec = {
        'mix_norm_w': d_mixnorm, 'conv_b': d_conv_b, 'dt_bias': d_dtb, 'a_log': d_alog, 'd_skip': d_dskip,
        'ssd_norm_w': d_ssdnorm, 'q_a_norm_w': d_qnorm, 'kv_a_norm_w': d_kvnorm, 'ffn_norm_w': d_ffnnorm,
        'ffn_conv_b': _deinterleave(d_fconv_b), 'ple_norm_w': d_plenorm, 'b_ple_gate': d_bgate,
        'ple_post_norm_w': d_post, 'final_norm_w': d_final,
    }
    return loss, dx, conv, vec


MESH = pl.DeviceIdType.MESH
FLIPS = ((0, 0, 1), (1, 0, 0), (0, 1, 0), (1, 1, 0), (1, 0, 1), (0, 1, 1), (1, 1, 1))


def _exchange(items, *, gather, name):
    n = len(items)

    def body(*refs):
        ins, outs = refs[:n], refs[n:2 * n]
        send_sems, recv_sems, local_sems = refs[2 * n:]
        x, y, c = lax.axis_index("x"), lax.axis_index("y"), lax.axis_index("c")
        me = 4 * x + 2 * y + c
        peers = [(jnp.where(fx, 1 - x, x), jnp.where(fy, 1 - y, y), jnp.where(fc, 1 - c, c)) for fx, fy, fc in FLIPS]
        slot = [4 * px + 2 * py + pc for px, py, pc in peers]
        local, sends = [], []
        for wi in range(n):
            cp = pltpu.make_async_copy(ins[wi] if gather else ins[wi].at[me], outs[wi].at[me], local_sems.at[wi])
            cp.start()
            local.append(cp)
            for k, peer in enumerate(peers):
                cp = pltpu.make_async_remote_copy(
                    src_ref=ins[wi] if gather else ins[wi].at[slot[k]], dst_ref=outs[wi].at[me],
                    send_sem=send_sems.at[k, wi], recv_sem=recv_sems.at[k, wi], device_id=peer, device_id_type=MESH)
                cp.start()
                sends.append(cp)
        for wi in range(n):
            for k, peer in enumerate(peers):
                pltpu.make_async_remote_copy(
                    src_ref=outs[wi].at[slot[k]], dst_ref=outs[wi].at[slot[k]], send_sem=send_sems.at[k, wi],
                    recv_sem=recv_sems.at[k, wi], device_id=peer, device_id_type=MESH).wait_recv()
        for cp in sends:
            cp.wait_send()
        for cp in local:
            cp.wait()

    hbm = pl.BlockSpec(memory_space=pltpu.HBM)
    out_shape = [jax.ShapeDtypeStruct(((N_DEV,) + v.shape) if gather else v.shape, v.dtype) for v in items]
    return pl.pallas_call(
        body, name=name, in_specs=[hbm] * n, out_specs=[hbm] * n, out_shape=out_shape,
        scratch_shapes=[pltpu.SemaphoreType.DMA((len(FLIPS), n)), pltpu.SemaphoreType.DMA((len(FLIPS), n)),
                        pltpu.SemaphoreType.DMA((n,))],
    )(*items)


HBM_SPEC = pl.BlockSpec(memory_space=pltpu.HBM)
SEM_SPEC = pl.BlockSpec(memory_space=pltpu.SEMAPHORE)
EFFECT = pltpu.SideEffectType.DATAFLOW_SIDE_EFFECTING


def _split_start(bufs, ncopies, plan, *, name):
    nb = len(bufs)

    def body(*refs):
        send_sems, recv_sems, token = refs[nb], refs[nb + 1], refs[2 * nb + 2]
        for i, (src, dst, peer, _) in enumerate(plan(refs[:nb])):
            pltpu.make_async_remote_copy(src_ref=src, dst_ref=dst, send_sem=send_sems.at[i], recv_sem=recv_sems.at[i],
                                         device_id=peer, device_id_type=MESH).start()
        token[...] = jnp.zeros_like(token)

    res = pl.pallas_call(
        body, name=name, in_specs=[HBM_SPEC] * nb,
        out_specs=[SEM_SPEC, SEM_SPEC] + [HBM_SPEC] * nb + [pl.BlockSpec(memory_space=pltpu.VMEM)],
        out_shape=[pltpu.SemaphoreType.DMA((ncopies,)), pltpu.SemaphoreType.DMA((ncopies,))]
        + [pltpu.HBM(v.shape, v.dtype) for v in bufs] + [jax.ShapeDtypeStruct((HALO, LANES), F32)],
        input_output_aliases={i: 2 + i for i in range(nb)},
        compiler_params=pltpu.CompilerParams(has_side_effects=EFFECT),
    )(*[pltpu.with_memory_space_constraint(v, pltpu.HBM) for v in bufs])
    return (res[0], res[1], list(res[2:2 + nb])), res[2 + nb]


def _split_wait(started, after, plan, local_plan, *, name):
    send_sems, recv_sems, bufs = started
    nb = len(bufs)
    nlocal = len(local_plan(bufs))

    def body(*refs):
        send_sems, recv_sems = refs[nb], refs[nb + 1]
        local_sems = refs[2 * nb + 3]
        local = []
        for j, (src, dst) in enumerate(local_plan(refs[:nb])):
            cp = pltpu.make_async_copy(src, dst, local_sems.at[j])
            cp.start()
            local.append(cp)
        for i, (src, _, peer, incoming) in enumerate(plan(refs[:nb])):
            cp = pltpu.make_async_remote_copy(src_ref=src, dst_ref=incoming, send_sem=send_sems.at[i],
                                              recv_sem=recv_sems.at[i], device_id=peer, device_id_type=MESH)
            cp.wait_send()
            cp.wait_recv()
        for cp in local:
            cp.wait()

    res = pl.pallas_call(
        body, name=name, in_specs=[HBM_SPEC] * nb + [SEM_SPEC, SEM_SPEC, pl.BlockSpec(memory_space=pl.ANY)],
        out_specs=[HBM_SPEC] * nb, out_shape=[pltpu.HBM(v.shape, v.dtype) for v in bufs],
        input_output_aliases={i: i for i in range(nb)},
        scratch_shapes=[pltpu.SemaphoreType.DMA((max(nlocal, 1),))],
        compiler_params=pltpu.CompilerParams(has_side_effects=EFFECT),
    )(*bufs, send_sems, recv_sems, after)
    return list(res)


def _hold(values, after, *, name):
    n = len(values)

    def body(*refs):
        del refs

    return list(pl.pallas_call(
        body, name=name, in_specs=[HBM_SPEC] * n + [pl.BlockSpec(memory_space=pl.ANY)], out_specs=[HBM_SPEC] * n,
        out_shape=[pltpu.HBM(v.shape, v.dtype) for v in values], input_output_aliases={i: i for i in range(n)},
    )(*values, after))


def _place():
    x, y, c = lax.axis_index("x"), lax.axis_index("y"), lax.axis_index("c")
    others = [((1 - x, y, c), 2 * (1 - x) + y), ((x, 1 - y, c), 2 * x + 1 - y), ((1 - x, 1 - y, c), 2 * (1 - x) + 1 - y)]
    return 4 * x + 2 * y + c, 2 * x + y, c, (x, y, 1 - c), others


def _gather1_plan(n):
    def plan(refs):
        me, _, _, sibling, others = _place()
        out = []
        for wi in range(n):
            item, land = refs[wi], refs[n + wi]
            out.append((item, land.at[me], sibling, land.at[me + 1 - 2 * lax.axis_index("c")]))
            for peer, chip in others:
                out.append((item, land.at[me], peer, land.at[2 * chip + lax.axis_index("c")]))
        return out

    return plan


def _gather1_local(n):
    def plan(refs):
        me = _place()[0]
        return [(refs[wi], refs[n + wi].at[me]) for wi in range(n)]

    return plan


def _gather2_plan(n):
    def plan(refs):
        _, _, c, sibling, others = _place()
        out = []
        for wi in range(n):
            land = refs[wi]
            for _, chip in others:
                out.append((land.at[2 * chip + c], land.at[2 * chip + c], sibling, land.at[2 * chip + 1 - c]))
        return out

    return plan


def _gather_start(items, *, name):
    lands = [lax.empty((N_DEV,) + v.shape, v.dtype) for v in items]
    return _split_start(items + lands, 4 * len(items), _gather1_plan(len(items)), name=name)


def _gather_forward(started, after, *, name):
    n = len(started[2]) // 2
    bufs = _split_wait(started, after, _gather1_plan(n), _gather1_local(n), name=name + "_wait")
    return _split_start(bufs[n:], 3 * n, _gather2_plan(n), name=name + "_start")


def _gather_finish(started, after, *, name):
    n = len(started[2])
    return _split_wait(started, after, _gather2_plan(n), lambda refs: [], name=name)


def _handshake(peers):
    barrier = pltpu.get_barrier_semaphore()
    for peer in peers:
        pl.semaphore_signal(barrier, inc=1, device_id=peer, device_id_type=MESH)
    pl.semaphore_wait(barrier, len(peers))


def _remote(src, dst, send_sem, recv_sem, peer):
    return pltpu.make_async_remote_copy(src_ref=src, dst_ref=dst, send_sem=send_sem, recv_sem=recv_sem, device_id=peer,
                                        device_id_type=MESH)


def _sequencer_gather(items, *, collective_id, name):
    n = len(items)
    srcs = [jax.new_ref(v, memory_space=pltpu.MemorySpace.HBM) for v in items]
    lands = [jax.empty_ref(jax.ShapeDtypeStruct((N_DEV,) + v.shape, v.dtype), memory_space=pltpu.MemorySpace.HBM)
             for v in items]
    dma = pltpu.SemaphoreType.DMA

    @pl.kernel(mesh=plsc.ScalarSubcoreMesh(axis_name="sequencer", num_cores=1), name=name,
               scratch_types=(dma((4 * n,)), dma((4 * n,)), dma((3 * n,)), dma((3 * n,)), dma((n,))),
               compiler_params=pltpu.CompilerParams(collective_id=collective_id))
    def launch(send1, recv1, send2, recv2, local_sems):
        _, _, _, sibling, others = _place()
        _handshake([sibling] + [peer for peer, _ in others])
        hop1 = _gather1_plan(n)(srcs + lands)
        hop2 = _gather2_plan(n)(lands)
        local = [pltpu.make_async_copy(src, dst, local_sems.at[j])
                 for j, (src, dst) in enumerate(_gather1_local(n)(srcs + lands))]
        for cp in local:
            cp.start()
        for i, (src, dst, peer, _) in enumerate(hop1):
            _remote(src, dst, send1.at[i], recv1.at[i], peer).start()
        for wi in range(n):
            for j in range(3):
                i1, i2 = 4 * wi + 1 + j, 3 * wi + j
                src, _, peer, incoming = hop1[i1]
                _remote(src, incoming, send1.at[i1], recv1.at[i1], peer).wait_recv()
                src, dst, peer, _ = hop2[i2]
                _remote(src, dst, send2.at[i2], recv2.at[i2], peer).start()
        for wi in range(n):
            src, _, peer, incoming = hop1[4 * wi]
            _remote(src, incoming, send1.at[4 * wi], recv1.at[4 * wi], peer).wait_recv()
        for i, (src, _, peer, incoming) in enumerate(hop2):
            cp = _remote(src, incoming, send2.at[i], recv2.at[i], peer)
            cp.wait_send()
            cp.wait_recv()
        for i, (src, dst, peer, _) in enumerate(hop1):
            _remote(src, dst, send1.at[i], recv1.at[i], peer).wait_send()
        for cp in local:
            cp.wait()

    launch()
    return [land[...] for land in lands]


def _sequencer_exchange(sources, land_shapes, ncopies, plan, local_plan, peers, *, collective_id, name):
    srcs = [jax.new_ref(v, memory_space=pltpu.MemorySpace.HBM) for v in sources]
    lands = [jax.empty_ref(s, memory_space=pltpu.MemorySpace.HBM) for s in land_shapes]
    nlocal = len(local_plan(srcs + lands))
    dma = pltpu.SemaphoreType.DMA

    @pl.kernel(mesh=plsc.ScalarSubcoreMesh(axis_name="sequencer", num_cores=1), name=name,
               scratch_types=(dma((ncopies,)), dma((ncopies,)), dma((max(nlocal, 1),))),
               compiler_params=pltpu.CompilerParams(collective_id=collective_id))
    def launch(send_sems, recv_sems, local_sems):
        _handshake(peers(_place()))
        copies = plan(srcs + lands)
        local = [pltpu.make_async_copy(src, dst, local_sems.at[j])
                 for j, (src, dst) in enumerate(local_plan(srcs + lands))]
        for cp in local:
            cp.start()
        for i, (src, dst, peer, _) in enumerate(copies):
            _remote(src, dst, send_sems.at[i], recv_sems.at[i], peer).start()
        for i, (src, _, peer, incoming) in enumerate(copies):
            cp = _remote(src, incoming, send_sems.at[i], recv_sems.at[i], peer)
            cp.wait_send()
            cp.wait_recv()
        for cp in local:
            cp.wait()

    launch()
    return [land[...] for land in lands]


def _sequencer_scatter_hop2(sums, *, collective_id, name):
    n = len(sums)
    shapes = [jax.ShapeDtypeStruct(v.shape, v.dtype) for v in sums]
    return _sequencer_exchange(sums, shapes, 3 * n, _scatter2_plan(n), _scatter2_local(n),
                               lambda place: [peer for peer, _ in place[4]], collective_id=collective_id, name=name)


N_CHIP = N_DEV // 2


def _scatter1_plan(n):
    def plan(refs):
        _, _, c, sibling, _ = _place()
        out = []
        for wi in range(n):
            parts, half = refs[wi], refs[n + wi]
            for chip in range(N_CHIP):
                out.append((parts.at[2 * chip + 1 - c], half.at[chip], sibling, half.at[chip]))
        return out

    return plan


def _scatter2_plan(n):
    def plan(refs):
        _, my_chip, _, _, others = _place()
        out = []
        for wi in range(n):
            sums, recv = refs[wi], refs[n + wi]
            for peer, chip in others:
                out.append((sums.at[chip], recv.at[my_chip], peer, recv.at[chip]))
        return out

    return plan


def _scatter2_local(n):
    def plan(refs):
        my_chip = _place()[1]
        return [(refs[wi].at[my_chip], refs[n + wi].at[my_chip]) for wi in range(n)]

    return plan


def _pair_add(parts, half, core, *, name):
    _, r, c = parts.shape
    tr = max(d for d in range(HALO, 257, HALO) if r % d == 0) if r > 256 else r
    parts4 = parts.reshape(N_CHIP, 2, r, c)

    def body(core_ref, p_ref, h_ref, o_ref):
        o_ref[...] = (p_ref[:, 0].astype(F32) + h_ref[...].astype(F32)).astype(o_ref.dtype)

    return pl.pallas_call(
        body, name=name,
        grid_spec=pltpu.PrefetchScalarGridSpec(
            num_scalar_prefetch=1, grid=(r // tr,),
            in_specs=[pl.BlockSpec((N_CHIP, 1, tr, c), lambda i, core_ref: (0, core_ref[0], i, 0)),
                      pl.BlockSpec((N_CHIP, tr, c), lambda i, core_ref: (0, i, 0))],
            out_specs=pl.BlockSpec((N_CHIP, tr, c), lambda i, core_ref: (0, i, 0))),
        out_shape=jax.ShapeDtypeStruct((N_CHIP, r, c), parts.dtype), compiler_params=_cp("parallel"),
    )(core, parts4, half)


def _scatter_start(parts, *, name):
    halves = [lax.empty((N_CHIP,) + v.shape[1:], v.dtype) for v in parts]
    return _split_start(parts + halves, N_CHIP * len(parts), _scatter1_plan(len(parts)), name=name)


def _adamw(parts, w, m, v, *, name):
    r, c = w.shape
    nparts = parts.shape[0]
    tr = max(d for d in range(HALO, 129, HALO) if r % d == 0) if r > 128 else r

    def body(p_ref, w_ref, m_ref, v_ref, g_ref, d_ref, mo_ref, vo_ref):
        g = p_ref[0].astype(F32)
        for k in range(1, nparts):
            g = g + p_ref[k].astype(F32)
        mn = ADAM_B1 * m_ref[...] + (1.0 - ADAM_B1) * g
        vn = ADAM_B2 * v_ref[...] + (1.0 - ADAM_B2) * (g * g)
        m_hat = mn / (1.0 - ADAM_B1 ** ADAM_STEP)
        v_hat = vn / (1.0 - ADAM_B2 ** ADAM_STEP)
        g_ref[...] = g
        d_ref[...] = -ADAM_LR * (m_hat / (jnp.sqrt(v_hat) + ADAM_EPS) + ADAM_WD * w_ref[...])
        mo_ref[...] = mn
        vo_ref[...] = vn

    blk = pl.BlockSpec((tr, c), lambda i: (i, 0))
    return pl.pallas_call(
        body, name=name, grid=(r // tr,), in_specs=[pl.BlockSpec((nparts, tr, c), lambda i: (0, i, 0)), blk, blk, blk],
        out_specs=[blk] * 4, out_shape=[jax.ShapeDtypeStruct((r, c), F32)] * 4, compiler_params=_cp("parallel"),
    )(parts, w, m, v)


def _pack_rows(vs, rows):
    lead = vs[0].shape[:-1] if vs[0].ndim > 1 else ()
    flat = jnp.concatenate(vs, axis=-1)
    pad = rows * LANES - flat.shape[-1]
    flat = jnp.pad(flat, [(0, 0)] * len(lead) + [(0, pad)])
    return flat.reshape(lead + (rows, LANES))


def kernel(x, p, positions, mix_norm_w, w_in, conv_w, conv_b, dt_bias, a_log, d_skip, ssd_norm_w, q_a_norm_w, w_q_b, kv_a_norm_w, w_kv_b, w_out, ffn_norm_w, w_ffn_up, ffn_conv_w, ffn_conv_b, w_ffn_down, ple_norm_w, w_ple_gate, b_ple_gate, w_ple_proj, ple_post_norm_w, final_norm_w, loss_target, m_mix_norm_w, m_w_in, m_conv_w, m_conv_b, m_dt_bias, m_a_log, m_d_skip, m_ssd_norm_w, m_q_a_norm_w, m_w_q_b, m_kv_a_norm_w, m_w_kv_b, m_w_out, m_ffn_norm_w, m_w_ffn_up, m_ffn_conv_w, m_ffn_conv_b, m_w_ffn_down, m_ple_norm_w, m_w_ple_gate, m_b_ple_gate, m_w_ple_proj, m_ple_post_norm_w, m_final_norm_w, v_mix_norm_w, v_w_in, v_conv_w, v_conv_b, v_dt_bias, v_a_log, v_d_skip, v_ssd_norm_w, v_q_a_norm_w, v_w_q_b, v_kv_a_norm_w, v_w_kv_b, v_w_out, v_ffn_norm_w, v_w_ffn_up, v_ffn_conv_w, v_ffn_conv_b, v_w_ffn_down, v_ple_norm_w, v_w_ple_gate, v_b_ple_gate, v_w_ple_proj, v_ple_post_norm_w, v_final_norm_w):
    given = dict(locals())
    shapes = {n: given[n].shape for n in WEIGHTS}
    w2 = {n: given[n].reshape(given[n].shape[-2:] if n in BIG or n in CONV else (1, -1)) for n in WEIGHTS}
    m2 = {n: given['m_' + n].reshape(w2[n].shape) for n in WEIGHTS}
    v2 = {n: given['v_' + n].reshape(w2[n].shape) for n in WEIGHTS}
    me = 4 * lax.axis_index("x") + 2 * lax.axis_index("y") + lax.axis_index("c")

    core = lax.axis_index("c").astype(jnp.int32).reshape(1)

    def shards(grp, zero):
        return [(w2[n] + zero).astype(BF16) if n in BIG else w2[n] + zero for n in WEIGHT_GROUPS[grp]]

    first, token = _gather_start(shards('a', 0.0), name="gather_a_hop1")
    first, token = _gather_forward(first, token, name="gather_a_hop2")
    zero = token[0, 0]
    later = dict(zip(WEIGHT_GROUPS['b'], _sequencer_gather(shards('b', zero), collective_id=1, name="gather_b")))
    later.update(zip(WEIGHT_GROUPS['c'], _sequencer_gather(shards('c', zero), collective_id=2, name="gather_c")))

    def get_w(grp, after):
        if grp == 'a':
            lands = dict(zip(WEIGHT_GROUPS[grp], _gather_finish(first, token, name="gather_a_done")))
        else:
            names = WEIGHT_GROUPS[grp]
            lands = dict(zip(names, _hold([later[n] for n in names], after, name="gather_" + grp + "_use")))
        return _assemble_weights(lands)

    scatters = {}

    hop_ids = {grp: 2 + 2 * i for i, grp in enumerate(GRAD_GROUPS)}

    def zero_of(arrays):
        return sum(v[(0,) * v.ndim].astype(F32) * 0.0 for v in arrays)

    def emit(grp, grads):
        scatters[grp], tok = _scatter_start([grads[n] for n in GRAD_GROUPS[grp]], name="scatter_" + grp + "_hop1")
        return tok[0, 0]

    def relay(grp, after):
        n = len(GRAD_GROUPS[grp])
        bufs = _split_wait(scatters[grp], after, _scatter1_plan(n), lambda refs: [], name="scatter_" + grp + "_hop1_wait")
        sums = [_pair_add(bufs[i], bufs[n + i], core, name="scatter_%s_add%d" % (grp, i)) for i in range(n)]
        scatters[grp] = _sequencer_scatter_hop2(sums, collective_id=hop_ids[grp] + 1, name="scatter_" + grp + "_hop2")
        return zero_of(sums)

    out_g, out_d, out_m, out_v = {}, {}, {}, {}

    def settle(grp):
        return zero_of(scatters[grp])

    def update(grp, behind=None):
        for n, parts in zip(GRAD_GROUPS[grp], scatters[grp]):
            wn = w2[n] if behind is None else w2[n] + behind
            out_g[n], out_d[n], out_m[n], out_v[n] = _adamw(parts, wn, m2[n], v2[n], name="adamw_" + n)

    vecs = {n: w2[n] for n in REPL}
    vecs['mix_norm_w'] = vecs['mix_norm_w'] + zero
    loss, dx, g_conv, g_vec = _local_step(x[0], p[0, 0], _rope_tables(positions), get_w, vecs, loss_target[0], emit,
                                          relay, settle)
    n_small = sum(g_vec[n].shape[1] for n in REPL) + sum(g_conv[n].size for n in CONV) + 1
    rows_small = -(-n_small // (LANES * HALO)) * HALO
    small = _pack_rows([g_vec[n] for n in REPL] + [g_conv[n].reshape(1, -1) for n in CONV] + [loss], rows_small)

    for grp in list(GRAD_GROUPS)[:-1]:
        update(grp)
    all_small = _exchange([small], gather=True, name="gather_small_grads")[0].reshape(N_DEV, rows_small * LANES)
    update(list(GRAD_GROUPS)[-1], zero_of([all_small]))
    pieces, off = [], 0
    for n in REPL:
        k = g_vec[n].shape[1]
        pieces.append(all_small[:, off:off + k])
        off += k
    for n in CONV:
        kw, cols = g_conv[n].shape
        full = all_small[:, off:off + kw * cols].reshape(N_DEV, kw, cols)
        mine = lax.dynamic_slice_in_dim(full, me * (cols // N_DEV), cols // N_DEV, axis=2)
        pieces.append(mine.reshape(N_DEV, kw * (cols // N_DEV)))
        off += kw * cols
    pieces.append(all_small[:, off:off + 1])
    small_names = REPL + CONV
    n_mine = sum(q.shape[1] for q in pieces)
    rows_mine = -(-n_mine // (LANES * HALO)) * HALO
    zero = jnp.zeros((1, 1), F32)
    packed = [_pack_rows([src[n].reshape(1, -1) for n in small_names] + [zero], rows_mine).reshape(rows_mine, LANES)
              for src in (w2, m2, v2)]
    sg, sd, sm, sv = _adamw(_pack_rows(pieces, rows_mine), *packed, name="adamw_small")
    off = 0
    for n in small_names:
        k = w2[n].size
        for dst, src in ((out_g, sg), (out_d, sd), (out_m, sm), (out_v, sv)):
            dst[n] = src.reshape(-1)[off:off + k].reshape(w2[n].shape)
        off += k
    total_loss = sg.reshape(-1)[off]

    outs = [total_loss, dx[None]]
    for res in (out_g, out_d, out_m, out_v):
        outs += [res[n].reshape(shapes[n]) for n in WEIGHTS]
    return tuple(outs)
```

```python
import math

import numpy as np
import jax
import jax.numpy as jnp
from jax import lax
from jax.experimental import pallas as pl
from jax.experimental.pallas import tpu as pltpu
from jax.experimental.pallas import tpu_sc as plsc

F32 = jnp.float32
BF16 = jnp.bfloat16
HI = lax.Precision.HIGHEST

D_MODEL = 2048
CHUNK = 64
D_SSM = 1024
SSD_P = 64
SSD_HEADS = 16
SSD_GROUPS = 2
SSD_N = 128
SSD_CONV = 4
SSD_CONV_DIM = D_SSM + 2 * SSD_GROUPS * SSD_N
MLA_HEADS = 8
MLA_NOPE = 128
MLA_ROPE = 64
MLA_V = 128
MLA_Q_RANK = 512
MLA_KV_RANK = 256
MLA_QK_PAD = 256
ROPE_THETA = 10000.0
D_FF = 5632
FFN_CONV = 3
PLE_DIM = 256
NORM_EPS = 1e-6
ADAM_LR, ADAM_B1, ADAM_B2, ADAM_EPS, ADAM_WD, ADAM_STEP = 0.001, 0.9, 0.999, 1e-08, 0.01, 10
N_DEV = 8

OFF_Z, OFF_XBC, OFF_QA, OFF_CKV, OFF_KR, OFF_DT, D_IN_PAD = 0, 1024, 2560, 3072, 3328, 3456, 3584
D_IN = 3408
LANES = 128
HALO = 8
VMEM_LIMIT = 56 * 1024 * 1024
FFN_TC = D_FF * 2 // N_DEV
FFN_PERM = (0, 4, 1, 5, 2, 6, 3, 7)
NEG = -1e30


def _cp(*sem):
    return pltpu.CompilerParams(dimension_semantics=tuple(sem), vmem_limit_bytes=VMEM_LIMIT)


def _tile(n, want):
    if n <= want:
        return n
    best = max(d for d in range(LANES, want + 1, LANES) if n % d == 0)
    return best


def _sigmoid(x):
    return 0.5 * (jnp.tanh(0.5 * x) + 1.0)


def _silu(x):
    return x * _sigmoid(x)


def _dsilu(x):
    s = _sigmoid(x)
    return s * (1.0 + x * (1.0 - s))


MM_TILE = 1408
MM_TK = 2816


def _matmul(a, b, *, ta=False, tb=False, out_dtype=F32, add=None, bias=None, tm=MM_TILE, tn=MM_TILE, tk=MM_TK, name,
            mnk=None, a_spec=None, b_spec=None, o_spec=None, o_shape=None):
    if mnk is None:
        m, k = (a.shape[1], a.shape[0]) if ta else a.shape
        n = b.shape[0] if tb else b.shape[1]
        assert k == (b.shape[1] if tb else b.shape[0])
    else:
        m, n, k = mnk
    tm, tn, tk = _tile(m, tm), _tile(n, tn), _tile(k, tk)
    nk = k // tk
    dims = (((0 if ta else 1,), (1 if tb else 0,)), ((), ()))

    def body(*refs):
        a_ref, b_ref = refs[0], refs[1]
        pos = 2
        add_ref = bias_ref = None
        if add is not None:
            add_ref = refs[pos]
            pos += 1
        if bias is not None:
            bias_ref = refs[pos]
            pos += 1
        o_ref = refs[pos]
        kk = pl.program_id(2)
        av = a_ref[...]
        bv = b_ref[...]
        av = av.reshape(av.shape[-2:]).astype(BF16)
        bv = bv.reshape(bv.shape[-2:]).astype(BF16)
        prod = lax.dot_general(av, bv, dims, preferred_element_type=F32)

        def finish(r):
            if bias_ref is not None:
                r = r + bias_ref[...]
            if add_ref is not None:
                r = r + add_ref[...].astype(F32)
            o_ref[...] = r.astype(out_dtype).reshape(o_ref.shape)

        if nk == 1:
            finish(prod)
        else:
            acc_ref = refs[pos + 1]

            @pl.when(kk == 0)
            def _():
                acc_ref[...] = prod

            @pl.when(kk > 0)
            def _():
                acc_ref[...] += prod

            @pl.when(kk == nk - 1)
            def _():
                finish(acc_ref[...])

    if a_spec is None:
        a_spec = (pl.BlockSpec((tk, tm), lambda i, j, kk: (kk, i)) if ta
                  else pl.BlockSpec((tm, tk), lambda i, j, kk: (i, kk)))
    if b_spec is None:
        b_spec = (pl.BlockSpec((tn, tk), lambda i, j, kk: (j, kk)) if tb
                  else pl.BlockSpec((tk, tn), lambda i, j, kk: (kk, j)))
    if o_spec is None:
        o_spec = pl.BlockSpec((tm, tn), lambda i, j, kk: (i, j))
    if o_shape is None:
        o_shape = (m, n)
    in_specs = [a_spec, b_spec]
    args = [a, b]
    if add is not None:
        in_specs.append(pl.BlockSpec((tm, tn), lambda i, j, kk: (i, j)))
        args.append(add)
    if bias is not None:
        in_specs.append(pl.BlockSpec((1, tn), lambda i, j, kk: (0, j)))
        args.append(bias)
    return pl.pallas_call(
        body, name=name, grid=(m // tm, n // tn, nk), in_specs=in_specs, out_specs=o_spec,
        out_shape=jax.ShapeDtypeStruct(o_shape, out_dtype),
        scratch_shapes=[pltpu.VMEM((tm, tn), F32)] if nk > 1 else [],
        compiler_params=_cp("parallel", "parallel", "arbitrary"),
    )(*args)


def _rmsnorm_fwd(x, w, *, width, cblk=0, out_dtype=BF16, tr=256, name):
    t = x.shape[0]

    def body(x_ref, w_ref, o_ref):
        xv = x_ref[...].astype(F32)
        r = lax.rsqrt(jnp.mean(xv * xv, axis=-1, keepdims=True) + NORM_EPS)
        o_ref[...] = (xv * r * w_ref[...]).astype(out_dtype)

    return pl.pallas_call(
        body, name=name, grid=(t // tr,),
        in_specs=[pl.BlockSpec((tr, width), lambda i: (i, cblk)), pl.BlockSpec((1, width), lambda i: (0, 0))],
        out_specs=pl.BlockSpec((tr, width), lambda i: (i, 0)),
        out_shape=jax.ShapeDtypeStruct((t, width), out_dtype),
        compiler_params=_cp("parallel"),
    )(x, w)


def _rmsnorm_bwd(x, w, dy, add=None, *, width, cblk=0, out_dtype=F32, also_bf16=False, tr=256, name):
    t = x.shape[0]

    def body(*refs):
        refs = list(refs)
        dxb_ref = refs.pop() if also_bf16 else None
        if add is None:
            x_ref, w_ref, dy_ref, dx_ref, dw_ref = refs
            add_ref = None
        else:
            x_ref, w_ref, dy_ref, add_ref, dx_ref, dw_ref = refs
        xv = x_ref[...].astype(F32)
        dyv = dy_ref[...].astype(F32)
        r = lax.rsqrt(jnp.mean(xv * xv, axis=-1, keepdims=True) + NORM_EPS)
        xh = xv * r
        g = dyv * w_ref[...]
        dx = r * (g - xh * jnp.mean(g * xh, axis=-1, keepdims=True))
        if add_ref is not None:
            dx = dx + add_ref[...].astype(F32)
        dx_ref[...] = dx.astype(out_dtype)
        if dxb_ref is not None:
            dxb_ref[...] = dx.astype(BF16)

        @pl.when(pl.program_id(0) == 0)
        def _():
            dw_ref[...] = jnp.zeros_like(dw_ref)

        dw_ref[...] += jnp.sum(dyv * xh, axis=0, keepdims=True)

    in_specs = [pl.BlockSpec((tr, width), lambda i: (i, cblk)), pl.BlockSpec((1, width), lambda i: (0, 0)),
                pl.BlockSpec((tr, width), lambda i: (i, 0))]
    args = [x, w, dy]
    if add is not None:
        in_specs.append(pl.BlockSpec((tr, width), lambda i: (i, 0)))
        args.append(add)
    blk = pl.BlockSpec((tr, width), lambda i: (i, 0))
    return pl.pallas_call(
        body, name=name, grid=(t // tr,), in_specs=in_specs,
        out_specs=[blk, pl.BlockSpec((1, width), lambda i: (0, 0))] + ([blk] if also_bf16 else []),
        out_shape=[jax.ShapeDtypeStruct((t, width), out_dtype), jax.ShapeDtypeStruct((1, width), F32)]
        + ([jax.ShapeDtypeStruct((t, width), BF16)] if also_bf16 else []),
        compiler_params=_cp("arbitrary"),
    )(*args)


def _shift_down(prev_halo, cur, j):
    if j == 0:
        return cur
    ext = jnp.concatenate([prev_halo, cur], axis=0)
    return pltpu.roll(ext, j, axis=0)[HALO:]


def _shift_up(cur, next_halo, j):
    if j == 0:
        return cur
    ext = jnp.concatenate([cur, next_halo], axis=0)
    return pltpu.roll(ext, ext.shape[0] - j, axis=0)[:cur.shape[0]]


def _conv_rows(prev, cur, w, b, kw):
    shifted = [cur]
    out = b + w[kw - 1:kw] * cur
    for j in range(1, kw):
        sh = _shift_down(prev, cur, j)
        shifted.append(sh)
        out = out + w[kw - 1 - j:kw - j] * sh
    return out, shifted


def _act_fwd(c, glu):
    if glu:
        half = c.shape[1] // 2
        return _silu(c[:, :half]) * c[:, half:]
    return _silu(c)


def _act_bwd(c, dout, glu):
    if glu:
        half = c.shape[1] // 2
        g, up = c[:, :half], c[:, half:]
        s = _sigmoid(g)
        gs = g * s
        return jnp.concatenate([dout * up * (s + gs * (1.0 - s)), dout * gs], axis=1)
    return dout * _dsilu(c)


def _conv_act_fwd(u, w, b, *, kw, glu, tc, coff, ncols, out_dtype, tr=256, name):
    t = u.shape[0]
    nb = ncols // tc
    oc = tc // 2 if glu else tc

    def body(u_ref, uh_ref, w_ref, b_ref, o_ref):
        prev = jnp.where(pl.program_id(0) == 0, 0.0, uh_ref[...])
        c, _ = _conv_rows(prev, u_ref[...], w_ref[...], b_ref[...], kw)
        o_ref[...] = _act_fwd(c, glu).astype(out_dtype)

    return pl.pallas_call(
        body, name=name, grid=(t // tr, nb),
        in_specs=[pl.BlockSpec((tr, tc), lambda i, j: (i, j + coff)),
                  pl.BlockSpec((HALO, tc), lambda i, j: (jnp.maximum(i * (tr // HALO) - 1, 0), j + coff)),
                  pl.BlockSpec((kw, tc), lambda i, j: (0, j)), pl.BlockSpec((1, tc), lambda i, j: (0, j))],
        out_specs=pl.BlockSpec((tr, oc), lambda i, j: (i, j)),
        out_shape=jax.ShapeDtypeStruct((t, nb * oc), out_dtype),
        compiler_params=_cp("parallel", "parallel"),
    )(u, u, w, b)


def _conv_act_bwd(u, w, b, dout, *, kw, glu, tc, coff, ncols, tr=256, name):
    t = u.shape[0]
    nb = ncols // tc
    nt = t // tr
    oc = tc // 2 if glu else tc

    def body(u_ref, up_ref, un_ref, d_ref, dn_ref, w_ref, b_ref, du_ref, dw_ref, db_ref):
        i = pl.program_id(1)
        cur, nxt, wv, bv = u_ref[...], un_ref[...], w_ref[...], b_ref[...]
        prev = jnp.where(i == 0, 0.0, up_ref[...])
        c_cur, shifted = _conv_rows(prev, cur, wv, bv, kw)
        c_nxt, _ = _conv_rows(cur[tr - HALO:], nxt, wv, bv, kw)
        d_cur = _act_bwd(c_cur, d_ref[...].astype(F32), glu)
        d_nxt = _act_bwd(c_nxt, jnp.where(i == nt - 1, 0.0, dn_ref[...].astype(F32)), glu)
        du = wv[kw - 1:kw] * d_cur
        for j in range(1, kw):
            du = du + wv[kw - 1 - j:kw - j] * _shift_up(d_cur, d_nxt, j)
        du_ref[...] = du.astype(BF16)

        @pl.when(i == 0)
        def _():
            dw_ref[...] = jnp.zeros_like(dw_ref)
            db_ref[...] = jnp.zeros_like(db_ref)

        db_ref[...] += jnp.sum(d_cur, axis=0, keepdims=True)
        dw_ref[...] += jnp.concatenate(
            [jnp.sum(d_cur * shifted[kw - 1 - k], axis=0, keepdims=True) for k in range(kw)], axis=0)

    nh = tr // HALO
    return pl.pallas_call(
        body, name=name, grid=(nb, nt),
        in_specs=[pl.BlockSpec((tr, tc), lambda j, i: (i, j + coff)),
                  pl.BlockSpec((HALO, tc), lambda j, i: (jnp.maximum(i * nh - 1, 0), j + coff)),
                  pl.BlockSpec((HALO, tc), lambda j, i: (jnp.minimum((i + 1) * nh, t // HALO - 1), j + coff)),
                  pl.BlockSpec((tr, oc), lambda j, i: (i, j)),
                  pl.BlockSpec((HALO, oc), lambda j, i: (jnp.minimum((i + 1) * nh, t // HALO - 1), j)),
                  pl.BlockSpec((kw, tc), lambda j, i: (0, j)), pl.BlockSpec((1, tc), lambda j, i: (0, j))],
        out_specs=[pl.BlockSpec((tr, tc), lambda j, i: (i, j)), pl.BlockSpec((kw, tc), lambda j, i: (0, j)),
                   pl.BlockSpec((1, tc), lambda j, i: (0, j))],
        out_shape=[jax.ShapeDtypeStruct((t, ncols), BF16), jax.ShapeDtypeStruct((kw, ncols), F32),
                   jax.ShapeDtypeStruct((1, ncols), F32)],
        compiler_params=_cp("parallel", "arbitrary"),
    )(u, u, u, dout, dout, w, b)


def _ple_fwd(x2, gl, pe, pw, *, tr=256, name):
    t, d = x2.shape

    def body(x_ref, gl_ref, pe_ref, pw_ref, o_ref):
        pv = pe_ref[...]
        r = lax.rsqrt(jnp.mean(pv * pv, axis=-1, keepdims=True) + NORM_EPS)
        o_ref[...] = x_ref[...] + _sigmoid(gl_ref[...]) * (pv * r * pw_ref[...])

    blk = pl.BlockSpec((tr, d), lambda i: (i, 0))
    return pl.pallas_call(
        body, name=name, grid=(t // tr,), in_specs=[blk, blk, blk, pl.BlockSpec((1, d), lambda i: (0, 0))],
        out_specs=blk, out_shape=jax.ShapeDtypeStruct((t, d), F32), compiler_params=_cp("parallel"),
    )(x2, gl, pe, pw)


def _ple_bwd(dx3, gl, pe, pw, *, tr=256, name):
    t, d = dx3.shape

    def body(dx_ref, gl_ref, pe_ref, pw_ref, dgl_ref, db_ref, dpe_ref, dpw_ref):
        dx, pv, pwv = dx_ref[...], pe_ref[...], pw_ref[...]
        gate = _sigmoid(gl_ref[...])
        r = lax.rsqrt(jnp.mean(pv * pv, axis=-1, keepdims=True) + NORM_EPS)
        ph = pv * r
        dgl = dx * (ph * pwv) * gate * (1.0 - gate)
        de = dx * gate
        g = de * pwv
        dgl_ref[...] = dgl.astype(BF16)
        dpe_ref[...] = (r * (g - ph * jnp.mean(g * ph, axis=-1, keepdims=True))).astype(BF16)

        @pl.when(pl.program_id(0) == 0)
        def _():
            db_ref[...] = jnp.zeros_like(db_ref)
            dpw_ref[...] = jnp.zeros_like(dpw_ref)

        db_ref[...] += jnp.sum(dgl, axis=0, keepdims=True)
        dpw_ref[...] += jnp.sum(de * ph, axis=0, keepdims=True)

    blk = pl.BlockSpec((tr, d), lambda i: (i, 0))
    row = pl.BlockSpec((1, d), lambda i: (0, 0))
    return pl.pallas_call(
        body, name=name, grid=(t // tr,), in_specs=[blk, blk, blk, row], out_specs=[blk, row, blk, row],
        out_shape=[jax.ShapeDtypeStruct((t, d), BF16), jax.ShapeDtypeStruct((1, d), F32),
                   jax.ShapeDtypeStruct((t, d), BF16), jax.ShapeDtypeStruct((1, d), F32)],
        compiler_params=_cp("arbitrary"),
    )(dx3, gl, pe, pw)


def _loss_head(x3, fw, target, *, tr=256, name):
    t, d = x3.shape

    def body(x_ref, w_ref, t_ref, l_ref, dx_ref, dw_ref):
        xv, wv = x_ref[...], w_ref[...]
        r = lax.rsqrt(jnp.mean(xv * xv, axis=-1, keepdims=True) + NORM_EPS)
        xh = xv * r
        err = xh * wv - t_ref[...]
        dy = err * (1.0 / d)
        g = dy * wv
        dx_ref[...] = r * (g - xh * jnp.mean(g * xh, axis=-1, keepdims=True))

        @pl.when(pl.program_id(0) == 0)
        def _():
            l_ref[...] = jnp.zeros_like(l_ref)
            dw_ref[...] = jnp.zeros_like(dw_ref)

        l_ref[...] += 0.5 * jnp.sum(jnp.mean(err * err, axis=-1, keepdims=True), axis=0, keepdims=True)
        dw_ref[...] += jnp.sum(dy * xh, axis=0, keepdims=True)

    blk = pl.BlockSpec((tr, d), lambda i: (i, 0))
    row = pl.BlockSpec((1, d), lambda i: (0, 0))
    return pl.pallas_call(
        body, name=name, grid=(t // tr,), in_specs=[blk, row, blk],
        out_specs=[pl.BlockSpec((1, 1), lambda i: (0, 0)), blk, row],
        out_shape=[jax.ShapeDtypeStruct((1, 1), F32), jax.ShapeDtypeStruct((t, d), F32),
                   jax.ShapeDtypeStruct((1, d), F32)],
        compiler_params=_cp("arbitrary"),
    )(x3, fw, target)


def _rope(blk, tab_ref):
    return blk * tab_ref[0] + pltpu.roll(blk, 96, axis=1) * tab_ref[1] + pltpu.roll(blk, 32, axis=1) * tab_ref[2]


def _unrope(g, tab_ref):
    return g * tab_ref[0] + pltpu.roll(g * tab_ref[1], 32, axis=1) + pltpu.roll(g * tab_ref[2], 96, axis=1)


def _mla_prep(q, kv, proj, tabs, *, tr=512, name):
    t = q.shape[0]

    def body(q_ref, kv_ref, kr_ref, tab_ref, qo_ref, ko_ref, vo_ref, vt_ref):
        qv, kvv = q_ref[...], kv_ref[...]
        qo_ref[0, :, :MLA_NOPE] = qv[:, :MLA_NOPE].astype(BF16)
        qo_ref[0, :, MLA_NOPE:] = _rope(qv[:, MLA_NOPE:], tab_ref).astype(BF16)
        ko_ref[0, :, :MLA_NOPE] = kvv[:, :MLA_NOPE].astype(BF16)
        ko_ref[0, :, MLA_NOPE:] = _rope(kr_ref[...], tab_ref).astype(BF16)
        vo_ref[0] = kvv[:, MLA_NOPE:].astype(BF16)
        for blk in range(tr // ATT_BLK):
            vt_ref[0, blk] = kvv[blk * ATT_BLK:(blk + 1) * ATT_BLK, MLA_NOPE:].T.astype(BF16)

    return pl.pallas_call(
        body, name=name, grid=(t // tr, MLA_HEADS),
        in_specs=[pl.BlockSpec((tr, MLA_QK_PAD), lambda i, h: (i, h)),
                  pl.BlockSpec((tr, MLA_NOPE + MLA_V), lambda i, h: (i, h)),
                  pl.BlockSpec((tr, LANES), lambda i, h: (i, OFF_KR // LANES)),
                  pl.BlockSpec((3, tr, LANES), lambda i, h: (0, i, 0))],
        out_specs=[pl.BlockSpec((1, tr, MLA_QK_PAD), lambda i, h: (h, i, 0)),
                   pl.BlockSpec((1, tr, MLA_QK_PAD), lambda i, h: (h, i, 0)),
                   pl.BlockSpec((1, tr, MLA_V), lambda i, h: (h, i, 0)),
                   pl.BlockSpec((1, tr // ATT_BLK, MLA_V, ATT_BLK), lambda i, h: (h, i, 0, 0))],
        out_shape=[jax.ShapeDtypeStruct((MLA_HEADS, t, MLA_QK_PAD), BF16),
                   jax.ShapeDtypeStruct((MLA_HEADS, t, MLA_QK_PAD), BF16),
                   jax.ShapeDtypeStruct((MLA_HEADS, t, MLA_V), BF16),
                   jax.ShapeDtypeStruct((MLA_HEADS, t // ATT_BLK, MLA_V, ATT_BLK), BF16)],
        compiler_params=_cp("parallel", "parallel"),
    )(q, kv, proj, tabs)


def _mla_unprep(dq3, dk3, dv3, tabs, *, tr=256, name):
    t = dq3.shape[1]

    def body(dq_ref, dk_ref, dv_ref, tab_ref, qo_ref, kvo_ref, kro_ref):
        kr = jnp.zeros((tr, LANES), F32)
        for h in range(MLA_HEADS):
            c0 = h * MLA_QK_PAD
            qo_ref[:, c0:c0 + MLA_NOPE] = dq_ref[h, :, :MLA_NOPE].astype(BF16)
            qo_ref[:, c0 + MLA_NOPE:c0 + MLA_QK_PAD] = _unrope(dq_ref[h, :, MLA_NOPE:], tab_ref).astype(BF16)
            kvo_ref[:, c0:c0 + MLA_NOPE] = dk_ref[h, :, :MLA_NOPE].astype(BF16)
            kvo_ref[:, c0 + MLA_NOPE:c0 + MLA_QK_PAD] = dv_ref[h].astype(BF16)
            kr = kr + dk_ref[h, :, MLA_NOPE:]
        kro_ref[...] = _unrope(kr, tab_ref).astype(BF16)

    return pl.pallas_call(
        body, name=name, grid=(t // tr,),
        in_specs=[pl.BlockSpec((MLA_HEADS, tr, MLA_QK_PAD), lambda i: (0, i, 0)),
                  pl.BlockSpec((MLA_HEADS, tr, MLA_QK_PAD), lambda i: (0, i, 0)),
                  pl.BlockSpec((MLA_HEADS, tr, MLA_V), lambda i: (0, i, 0)),
                  pl.BlockSpec((3, tr, LANES), lambda i: (0, i, 0))],
        out_specs=[pl.BlockSpec((tr, MLA_HEADS * MLA_QK_PAD), lambda i: (i, 0)),
                   pl.BlockSpec((tr, MLA_HEADS * MLA_QK_PAD), lambda i: (i, 0)),
                   pl.BlockSpec((tr, LANES), lambda i: (i, 0))],
        out_shape=[jax.ShapeDtypeStruct((t, MLA_HEADS * MLA_QK_PAD), BF16),
                   jax.ShapeDtypeStruct((t, MLA_HEADS * MLA_QK_PAD), BF16),
                   jax.ShapeDtypeStruct((t, LANES), BF16)],
        compiler_params=_cp("parallel"),
    )(dq3, dk3, dv3, tabs)


ATT_BLK = 512
ATT_SCALE = 1.0 / math.sqrt(MLA_NOPE + MLA_ROPE)
_NT = (((1,), (1,)), ((), ()))
_TN = (((0,), (0,)), ((), ()))


def _att_scores_t(k, q, diagonal):
    s = lax.dot_general(k, q, _NT, preferred_element_type=F32) * ATT_SCALE
    if not diagonal:
        return s
    key = lax.broadcasted_iota(jnp.int32, s.shape, 0)
    query = lax.broadcasted_iota(jnp.int32, s.shape, 1)
    return jnp.where((key >> 6) <= (query >> 6), s, NEG)


def _att_rows(i):
    return pl.ds(pl.multiple_of(i * ATT_BLK, ATT_BLK), ATT_BLK)


ATT_HEADS = 2


def _attn_fwd(q3, k3, vt4, *, name):
    t = q3.shape[1]
    nq = t // ATT_BLK

    def body(q_ref, k_ref, vt_ref, o_ref, lse_ref):
        qi = pl.program_id(1)
        qs = [q_ref[hh] for hh in range(ATT_HEADS)]

        def step(j, carry, diagonal=False):
            out = []
            for hh, (m, l, acc) in enumerate(carry):
                s = _att_scores_t(k_ref[hh, _att_rows(j), :], qs[hh], diagonal)
                m_new = jnp.maximum(m, jnp.max(s, axis=0, keepdims=True))
                p = jnp.exp(s - m_new)
                alpha = jnp.exp(m - m_new)
                l = alpha * l + jnp.sum(p, axis=0, keepdims=True)
                acc = alpha * acc + jnp.dot(vt_ref[hh, j], p.astype(BF16), preferred_element_type=F32)
                out.append((m_new, l, acc))
            return tuple(out)

        init = tuple((jnp.full((1, ATT_BLK), NEG, F32), jnp.zeros((1, ATT_BLK), F32),
                      jnp.zeros((MLA_V, ATT_BLK), F32)) for _ in range(ATT_HEADS))
        done = step(qi, lax.fori_loop(0, qi, step, init), diagonal=True)
        for hh, (m, l, acc) in enumerate(done):
            o_ref[:, hh * MLA_V:(hh + 1) * MLA_V] = (acc / l).T
            lse_ref[hh, 0] = m + jnp.log(l)

    return pl.pallas_call(
        body, name=name, grid=(MLA_HEADS // ATT_HEADS, nq),
        in_specs=[pl.BlockSpec((ATT_HEADS, ATT_BLK, MLA_QK_PAD), lambda h, i: (h, i, 0)),
                  pl.BlockSpec((ATT_HEADS, t, MLA_QK_PAD), lambda h, i: (h, 0, 0)),
                  pl.BlockSpec((ATT_HEADS, nq, MLA_V, ATT_BLK), lambda h, i: (h, 0, 0, 0))],
        out_specs=[pl.BlockSpec((ATT_BLK, ATT_HEADS * MLA_V), lambda h, i: (i, h)),
                   pl.BlockSpec((ATT_HEADS, 1, 1, ATT_BLK), lambda h, i: (h, i, 0, 0))],
        out_shape=[jax.ShapeDtypeStruct((t, MLA_HEADS * MLA_V), F32),
                   jax.ShapeDtypeStruct((MLA_HEADS, nq, 1, ATT_BLK), F32)],
        compiler_params=_cp("parallel", "parallel"),
    )(q3, k3, vt4)


def _attn_bwd(q3, k3, v3, o, dcat, lse, *, name):
    t = q3.shape[1]
    nq = t // ATT_BLK
    wide = ATT_HEADS * MLA_V

    def body(q_ref, k_ref, v_ref, o_ref, do_ref, lse_ref, dq_ref, dk_ref, dv_ref, delta_ref):
        kj = pl.program_id(1)

        @pl.when(kj == 0)
        def _():
            dq_ref[...] = jnp.zeros_like(dq_ref)
            ones = jnp.ones((HALO, MLA_V), F32)
            for i in range(nq):
                rows = pl.ds(i * ATT_BLK, ATT_BLK)
                prod = o_ref[rows, :] * do_ref[rows, :]
                for hh in range(ATT_HEADS):
                    delta_ref[hh, i] = lax.dot_general(ones, prod[:, hh * MLA_V:(hh + 1) * MLA_V], _NT, precision=HI,
                                                       preferred_element_type=F32)

        def step(i, carry, diagonal=False):
            rows = _att_rows(i)
            out = []
            for hh, (dk, dv) in enumerate(carry):
                k, v = k_ref[hh], v_ref[hh]
                q = q_ref[hh, rows, :]
                dob = do_ref[rows, hh * MLA_V:(hh + 1) * MLA_V].astype(BF16)
                p = jnp.exp(_att_scores_t(k, q, diagonal) - lse_ref[hh, i])
                dv = dv + jnp.dot(p.astype(BF16), dob, preferred_element_type=F32)
                dp = lax.dot_general(v, dob, _NT, preferred_element_type=F32)
                ds = (p * (dp - delta_ref[hh, i, 0:1, :]) * ATT_SCALE).astype(BF16)
                dk = dk + jnp.dot(ds, q, preferred_element_type=F32)
                dq_ref[hh, rows, :] += lax.dot_general(ds, k, _TN, preferred_element_type=F32)
                out.append((dk, dv))
            return tuple(out)

        init = tuple((jnp.zeros((ATT_BLK, MLA_QK_PAD), F32), jnp.zeros((ATT_BLK, MLA_V), F32))
                     for _ in range(ATT_HEADS))
        done = lax.fori_loop(kj + 1, nq, step, step(kj, init, diagonal=True))
        for hh, (dk, dv) in enumerate(done):
            dk_ref[hh] = dk
            dv_ref[hh] = dv

    return pl.pallas_call(
        body, name=name, grid=(MLA_HEADS // ATT_HEADS, nq),
        in_specs=[pl.BlockSpec((ATT_HEADS, t, MLA_QK_PAD), lambda h, j: (h, 0, 0)),
                  pl.BlockSpec((ATT_HEADS, ATT_BLK, MLA_QK_PAD), lambda h, j: (h, j, 0)),
                  pl.BlockSpec((ATT_HEADS, ATT_BLK, MLA_V), lambda h, j: (h, j, 0)),
                  pl.BlockSpec((t, wide), lambda h, j: (0, h)),
                  pl.BlockSpec((t, wide), lambda h, j: (0, MLA_HEADS // ATT_HEADS + h)),
                  pl.BlockSpec((ATT_HEADS, nq, 1, ATT_BLK), lambda h, j: (h, 0, 0, 0))],
        out_specs=[pl.BlockSpec((ATT_HEADS, t, MLA_QK_PAD), lambda h, j: (h, 0, 0)),
                   pl.BlockSpec((ATT_HEADS, ATT_BLK, MLA_QK_PAD), lambda h, j: (h, j, 0)),
                   pl.BlockSpec((ATT_HEADS, ATT_BLK, MLA_V), lambda h, j: (h, j, 0))],
        out_shape=[jax.ShapeDtypeStruct((MLA_HEADS, t, MLA_QK_PAD), F32),
                   jax.ShapeDtypeStruct((MLA_HEADS, t, MLA_QK_PAD), F32),
                   jax.ShapeDtypeStruct((MLA_HEADS, t, MLA_V), F32)],
        scratch_shapes=[pltpu.VMEM((ATT_HEADS, nq, HALO, ATT_BLK), F32)],
        compiler_params=_cp("parallel", "arbitrary"),
    )(q3, k3, v3, o, dcat, lse)


def _ssd_prep(proj, bias128, alog128, *, name):
    t = proj.shape[0]
    nc = t // CHUNK

    def body(raw_ref, b_ref, al_ref, dt_ref, cs_ref, a_ref):
        xv = raw_ref[...] + b_ref[...]
        dt = jnp.maximum(xv, 0.0) + jnp.log(1.0 + jnp.exp(-jnp.abs(xv)))
        a = -jnp.exp(al_ref[...])
        adt = (dt * a).reshape(nc, CHUNK, LANES)
        li = lax.broadcasted_iota(jnp.int32, (nc, CHUNK, CHUNK), 1)
        si = lax.broadcasted_iota(jnp.int32, (nc, CHUNK, CHUNK), 2)
        tril = jnp.where(si <= li, 1.0, 0.0).astype(F32)
        cs = lax.dot_general(tril, adt, (((2,), (1,)), ((0,), (0,))), precision=HI, preferred_element_type=F32)
        dt_ref[...] = dt
        cs_ref[...] = cs.reshape(t, LANES)
        a_ref[...] = a

    blk = pl.BlockSpec((t, LANES), lambda i: (0, 0))
    row = pl.BlockSpec((1, LANES), lambda i: (0, 0))
    return pl.pallas_call(
        body, name=name, grid=(1,),
        in_specs=[pl.BlockSpec((t, LANES), lambda i: (0, OFF_DT // LANES)), row, row],
        out_specs=[blk, blk, row],
        out_shape=[jax.ShapeDtypeStruct((t, LANES), F32), jax.ShapeDtypeStruct((t, LANES), F32),
                   jax.ShapeDtypeStruct((1, LANES), F32)],
        compiler_params=_cp("arbitrary"),
    )(proj, bias128, alog128)


def _ssd_prep_bwd(ddt128, dadt128, proj, bias128, dt128, a128, dd_h, *, name):
    t = proj.shape[0]

    def body(ddt_ref, dadt_ref, raw_ref, b_ref, dt_ref, a_ref, dd_ref, draw_ref, db_ref, dal_ref, dds_ref):
        draw = ddt_ref[...] * _sigmoid(raw_ref[...] + b_ref[...])
        draw_ref[...] = draw.astype(BF16)
        db_ref[...] = jnp.sum(draw, axis=0, keepdims=True)
        dal_ref[...] = jnp.sum(dadt_ref[...] * dt_ref[...], axis=0, keepdims=True) * a_ref[...]
        dds_ref[...] = jnp.sum(dd_ref[...], axis=-1, keepdims=True)

    blk = pl.BlockSpec((t, LANES), lambda i: (0, 0))
    row = pl.BlockSpec((1, LANES), lambda i: (0, 0))
    return pl.pallas_call(
        body, name=name, grid=(1,),
        in_specs=[blk, blk, pl.BlockSpec((t, LANES), lambda i: (0, OFF_DT // LANES)), row, blk, row,
                  pl.BlockSpec((SSD_HEADS, SSD_P), lambda i: (0, 0))],
        out_specs=[blk, row, row, pl.BlockSpec((SSD_HEADS, 1), lambda i: (0, 0))],
        out_shape=[jax.ShapeDtypeStruct((t, LANES), BF16), jax.ShapeDtypeStruct((1, LANES), F32),
                   jax.ShapeDtypeStruct((1, LANES), F32), jax.ShapeDtypeStruct((SSD_HEADS, 1), F32)],
        compiler_params=_cp("arbitrary"),
    )(ddt128, dadt128, proj, bias128, dt128, a128, dd_h)


def _bdot(a, b, ca, cb, precision=None):
    return lax.dot_general(a, b, (((ca,), (cb,)), ((0,), (0,))), precision=precision, preferred_element_type=F32)


def _head_matrices():
    eye, zero = jnp.eye(SSD_P, dtype=F32), jnp.zeros((SSD_P, SSD_P), F32)
    pick = jnp.stack([jnp.concatenate([eye, zero], axis=0), jnp.concatenate([zero, eye], axis=0)])
    return pick, pick.transpose(0, 2, 1)


def _move(x, sel):
    selb = sel.astype(BF16)
    hi = x.astype(BF16)
    rest = x - hi.astype(F32)
    mid = rest.astype(BF16)
    low = (rest - mid.astype(F32)).astype(BF16)
    out = jnp.dot(hi, selb, preferred_element_type=F32)
    out = out + jnp.dot(mid, selb, preferred_element_type=F32)
    return out + jnp.dot(low, selb, preferred_element_type=F32)


def _pick_head(pair_ref, pick_ref, h):
    return _move(pair_ref[...], pick_ref[h % 2])


def _place_head(out_ref, val, place_ref, h):
    wide = _move(val, place_ref[h % 2])

    @pl.when(h % 2 == 0)
    def _():
        out_ref[...] = wide

    @pl.when(h % 2 == 1)
    def _():
        out_ref[...] += wide


def _ssd_common(x2, dt_ref, cs_ref, csr_ref, b_ref, c_ref, nc):
    x = x2.reshape(nc, CHUNK, SSD_P)
    dt = dt_ref[0].reshape(nc, CHUNK, SSD_P)
    cs = cs_ref[0].reshape(nc, CHUNK, SSD_P)
    csr = csr_ref[0]
    bm = b_ref[...].reshape(nc, CHUNK, SSD_N).astype(BF16)
    cm = c_ref[...].reshape(nc, CHUNK, SSD_N).astype(BF16)
    li = lax.broadcasted_iota(jnp.int32, (nc, CHUNK, CHUNK), 1)
    si = lax.broadcasted_iota(jnp.int32, (nc, CHUNK, CHUNK), 2)
    lmat = jnp.exp(jnp.where(si <= li, cs - csr, NEG))
    g = _bdot(cm, bm, 2, 2)
    cs_last = jnp.sum(jnp.where(li == CHUNK - 1, cs, 0.0), axis=1, keepdims=True)
    xdt = x * dt
    dec = jnp.exp(cs_last - cs)
    return x, dt, cs, bm, cm, li, si, lmat, g, cs_last, xdt, dec


def _ssd_fwd(xbc, dt_h, cs_h, cs_row, dskip_h, *, name):
    t = xbc.shape[0]
    nc = t // CHUNK
    hpg = SSD_HEADS // SSD_GROUPS
    pick, place = _head_matrices()

    def body(xs_ref, dt_ref, cs_ref, csr_ref, b_ref, c_ref, dk_ref, pick_ref, place_ref, y_ref, st_ref, sc_ref, cd_ref):
        h = pl.program_id(0)
        x, dt, cs, bm, cm, li, si, lmat, g, cs_last, xdt, dec = _ssd_common(_pick_head(xs_ref, pick_ref, h), dt_ref,
                                                                           cs_ref, csr_ref, b_ref, c_ref, nc)
        yd = _bdot((g * lmat).astype(BF16), xdt.astype(BF16), 2, 1)
        sc_ref[...] = _bdot(bm, (dec * xdt).astype(BF16), 1, 1)
        cd_ref[...] = jnp.exp(cs_last)

        def step(c, s):
            st_ref[0, c] = s
            return s * cd_ref[c] + sc_ref[c]

        lax.fori_loop(0, nc, step, jnp.zeros((SSD_N, SSD_P), F32))
        yo = _bdot(cm, st_ref[0].astype(BF16), 2, 1) * jnp.exp(cs)
        _place_head(y_ref, (yd + yo + dk_ref[0] * x).reshape(t, SSD_P), place_ref, h)

    head = pl.BlockSpec((1, t, SSD_P), lambda h: (h, 0, 0))
    pair = pl.BlockSpec((t, 2 * SSD_P), lambda h: (0, h // 2))
    nxb = D_SSM // SSD_N
    return pl.pallas_call(
        body, name=name, grid=(SSD_HEADS,),
        in_specs=[pair, head, head, pl.BlockSpec((1, nc, 1, CHUNK), lambda h: (h, 0, 0, 0)),
                  pl.BlockSpec((t, SSD_N), lambda h: (0, nxb + h // hpg)),
                  pl.BlockSpec((t, SSD_N), lambda h: (0, nxb + SSD_GROUPS + h // hpg)),
                  pl.BlockSpec((1, 1, SSD_P), lambda h: (h, 0, 0)),
                  pl.BlockSpec((2, 2 * SSD_P, SSD_P), lambda h: (0, 0, 0)),
                  pl.BlockSpec((2, SSD_P, 2 * SSD_P), lambda h: (0, 0, 0))],
        out_specs=[pair, pl.BlockSpec((1, nc, SSD_N, SSD_P), lambda h: (h, 0, 0, 0))],
        out_shape=[jax.ShapeDtypeStruct((t, D_SSM), F32),
                   jax.ShapeDtypeStruct((SSD_HEADS, nc, SSD_N, SSD_P), F32)],
        scratch_shapes=[pltpu.VMEM((nc, SSD_N, SSD_P), F32), pltpu.VMEM((nc, 1, SSD_P), F32)],
        compiler_params=_cp("arbitrary"),
    )(xbc, dt_h, cs_h, cs_row, xbc, xbc, dskip_h, pick, place)


def _ssd_bwd(xbc, dt_h, cs_h, cs_row, dskip_h, a_h, states, dy, *, name):
    t = xbc.shape[0]
    nc = t // CHUNK
    hpg = SSD_HEADS // SSD_GROUPS
    pick, place = _head_matrices()

    def body(xs_ref, dt_ref, cs_ref, csr_ref, b_ref, c_ref, dk_ref, a_ref, st_ref, dy_ref, pick_ref, place_ref,
             dxs_ref, ddt_ref, dadt_ref, db_ref, dc_ref, dd_ref, dsl_ref, dsc_ref, cd_ref):
        h = pl.program_id(0) * hpg + pl.program_id(1)
        x, dt, cs, bm, cm, li, si, lmat, g, cs_last, xdt, dec = _ssd_common(_pick_head(xs_ref, pick_ref, h), dt_ref,
                                                                           cs_ref, csr_ref, b_ref, c_ref, nc)
        dy = _pick_head(dy_ref, pick_ref, h).reshape(nc, CHUNK, SSD_P)
        dyb = dy.astype(BF16)
        xdtb = xdt.astype(BF16)
        sprev = st_ref[0]
        sprevb = sprev.astype(BF16)
        cdec = jnp.exp(cs_last)
        ecs = jnp.exp(cs)
        dw = (ecs * dy).astype(BF16)
        wmat = _bdot(cm, sprevb, 2, 1)
        dcs = jnp.sum(dy * ecs * wmat, axis=2, keepdims=True)
        dcm = _bdot(dw, sprevb, 2, 2)
        dsl_ref[...] = _bdot(cm, dw, 1, 1)
        cd_ref[...] = cdec

        def step(k, ds):
            c = nc - 1 - k
            dsc_ref[c] = ds
            return ds * cd_ref[c] + dsl_ref[c]

        lax.fori_loop(0, nc, step, jnp.zeros((SSD_N, SSD_P), F32))
        dsc = dsc_ref[...]
        dscb = dsc.astype(BF16)
        d_last = jnp.sum(jnp.sum(dsc * sprev, axis=1, keepdims=True) * cdec, axis=2, keepdims=True)
        z = dec * xdt
        dbm = _bdot(z.astype(BF16), dscb, 2, 2)
        dz = _bdot(bm, dscb, 2, 1)
        dxdt = dec * dz
        t2 = jnp.sum(dz * z, axis=2, keepdims=True)
        dcs = dcs - t2
        d_last = d_last + jnp.sum(t2, axis=1, keepdims=True)
        m = g * lmat
        mb = m.astype(BF16)
        dm = _bdot(dyb, xdtb, 2, 2)
        dxdt = dxdt + _bdot(mb, dyb, 1, 1)
        dseg = dm * m
        dcs = dcs + jnp.sum(dseg, axis=2, keepdims=True)
        ones = jnp.ones((nc, CHUNK, SSD_P), F32)
        dcs = dcs - _bdot(dseg, ones, 1, 1, precision=HI)
        dg = (dm * lmat).astype(BF16)
        dcm = dcm + _bdot(dg, bm, 2, 1)
        dbm = dbm + _bdot(dg, cm, 1, 1)
        dcs = dcs + jnp.where(li[:, :, :SSD_P] == CHUNK - 1, d_last, 0.0)
        triu = jnp.where(li <= si, 1.0, 0.0).astype(F32)
        dadt = _bdot(triu, dcs, 2, 1, precision=HI)
        dk = dk_ref[0]
        _place_head(dxs_ref, (dxdt * dt + dk * dy).reshape(t, SSD_P), place_ref, h)
        ddt = jnp.sum(dxdt * x, axis=2, keepdims=True) + dadt * a_ref[0]
        mine = lax.broadcasted_iota(jnp.int32, (t, LANES), 1) == h

        @pl.when(h == 0)
        def _():
            ddt_ref[...] = jnp.zeros_like(ddt_ref)
            dadt_ref[...] = jnp.zeros_like(dadt_ref)

        ddt_ref[...] += jnp.where(mine, jnp.max(ddt, axis=2, keepdims=True).reshape(t, 1), 0.0)
        dadt_ref[...] += jnp.where(mine, jnp.max(dadt, axis=2, keepdims=True).reshape(t, 1), 0.0)
        dd_ref[0] = jnp.sum(jnp.sum(dy * x, axis=1, keepdims=True), axis=0)

        @pl.when(pl.program_id(1) == 0)
        def _():
            db_ref[...] = jnp.zeros_like(db_ref)
            dc_ref[...] = jnp.zeros_like(dc_ref)

        db_ref[...] += dbm.reshape(t, SSD_N)
        dc_ref[...] += dcm.reshape(t, SSD_N)

    head = pl.BlockSpec((1, t, SSD_P), lambda gi, hi: (gi * hpg + hi, 0, 0))
    pair = pl.BlockSpec((t, 2 * SSD_P), lambda gi, hi: (0, (gi * hpg + hi) // 2))
    grp = pl.BlockSpec((t, SSD_N), lambda gi, hi: (0, gi))
    lane = pl.BlockSpec((1, 1, SSD_P), lambda gi, hi: (gi * hpg + hi, 0, 0))
    rows = pl.BlockSpec((t, LANES), lambda gi, hi: (0, 0))
    nxb = D_SSM // SSD_N
    dxs, ddt, dadt, db, dc, dd = pl.pallas_call(
        body, name=name, grid=(SSD_GROUPS, hpg),
        in_specs=[pair, head, head, pl.BlockSpec((1, nc, 1, CHUNK), lambda gi, hi: (gi * hpg + hi, 0, 0, 0)),
                  pl.BlockSpec((t, SSD_N), lambda gi, hi: (0, nxb + gi)),
                  pl.BlockSpec((t, SSD_N), lambda gi, hi: (0, nxb + SSD_GROUPS + gi)), lane, lane,
                  pl.BlockSpec((1, nc, SSD_N, SSD_P), lambda gi, hi: (gi * hpg + hi, 0, 0, 0)), pair,
                  pl.BlockSpec((2, 2 * SSD_P, SSD_P), lambda gi, hi: (0, 0, 0)),
                  pl.BlockSpec((2, SSD_P, 2 * SSD_P), lambda gi, hi: (0, 0, 0))],
        out_specs=[pair, rows, rows, grp, grp, lane],
        out_shape=[jax.ShapeDtypeStruct((t, D_SSM), F32)] + [jax.ShapeDtypeStruct((t, LANES), F32)] * 2
        + [jax.ShapeDtypeStruct((t, SSD_GROUPS * SSD_N), F32)] * 2
        + [jax.ShapeDtypeStruct((SSD_HEADS, 1, SSD_P), F32)],
        scratch_shapes=[pltpu.VMEM((nc, SSD_N, SSD_P), F32), pltpu.VMEM((nc, SSD_N, SSD_P), F32),
                        pltpu.VMEM((nc, 1, SSD_P), F32)],
        compiler_params=_cp("arbitrary", "arbitrary"),
    )(xbc, dt_h, cs_h, cs_row, xbc, xbc, dskip_h, a_h, states, dy, pick, place)
    return jnp.concatenate([dxs, db, dc], axis=1), ddt, dadt, dd


def _ssd_gate_fwd(y, proj, w, *, tr=256, name):
    t = y.shape[0]
    gw = D_SSM // SSD_GROUPS

    def body(y_ref, z_ref, w_ref, o_ref):
        v = y_ref[...] * _silu(z_ref[...])
        for gi in range(SSD_GROUPS):
            vg = v[:, gi * gw:(gi + 1) * gw]
            r = lax.rsqrt(jnp.mean(vg * vg, axis=-1, keepdims=True) + NORM_EPS)
            o_ref[:, gi * gw:(gi + 1) * gw] = (vg * r * w_ref[:, gi * gw:(gi + 1) * gw]).astype(BF16)

    blk = pl.BlockSpec((tr, D_SSM), lambda i: (i, 0))
    return pl.pallas_call(
        body, name=name, grid=(t // tr,), in_specs=[blk, blk, pl.BlockSpec((1, D_SSM), lambda i: (0, 0))],
        out_specs=blk, out_shape=jax.ShapeDtypeStruct((t, D_SSM), BF16), compiler_params=_cp("parallel"),
    )(y, proj, w)


def _ssd_gate_bwd(y, proj, w, dcat, *, tr=256, name):
    t = y.shape[0]
    gw = D_SSM // SSD_GROUPS

    def body(y_ref, z_ref, w_ref, d_ref, dy_ref, dz_ref, dw_ref):
        yv, zv, dv = y_ref[...], z_ref[...], d_ref[...].astype(F32)
        sz = _silu(zv)
        v = yv * sz

        @pl.when(pl.program_id(0) == 0)
        def _():
            dw_ref[...] = jnp.zeros_like(dw_ref)

        for gi in range(SSD_GROUPS):
            sl = slice(gi * gw, (gi + 1) * gw)
            vg, dg = v[:, sl], dv[:, sl]
            r = lax.rsqrt(jnp.mean(vg * vg, axis=-1, keepdims=True) + NORM_EPS)
            vh = vg * r
            gg = dg * w_ref[:, sl]
            dvg = r * (gg - vh * jnp.mean(gg * vh, axis=-1, keepdims=True))
            dy_ref[:, sl] = dvg * sz[:, sl]
            dz_ref[:, sl] = (dvg * yv[:, sl] * _dsilu(zv[:, sl])).astype(BF16)
            dw_ref[:, sl] += jnp.sum(dg * vh, axis=0, keepdims=True)

    blk = pl.BlockSpec((tr, D_SSM), lambda i: (i, 0))
    row = pl.BlockSpec((1, D_SSM), lambda i: (0, 0))
    return pl.pallas_call(
        body, name=name, grid=(t // tr,), in_specs=[blk, blk, row, blk], out_specs=[blk, blk, row],
        out_shape=[jax.ShapeDtypeStruct((t, D_SSM), F32), jax.ShapeDtypeStruct((t, D_SSM), BF16),
                   jax.ShapeDtypeStruct((1, D_SSM), F32)],
        compiler_params=_cp("arbitrary"),
    )(y, proj, w, dcat)


def _pad_lanes(v):
    return jnp.pad(v, ((0, 0), (0, LANES - v.shape[1])))


def _per_head(v128, t):
    return jnp.broadcast_to(v128[:, :SSD_HEADS].T[:, :, None], (SSD_HEADS, t, SSD_P))


def _ssd_forward(proj, conv_w, conv_b, dt_bias, a_log, d_skip, ssd_norm_w):
    t = proj.shape[0]
    nc = t // CHUNK
    xbc = _conv_act_fwd(proj, conv_w, conv_b, kw=SSD_CONV, glu=False, tc=512, coff=OFF_XBC // 512,
                        ncols=SSD_CONV_DIM, out_dtype=F32, name="ssd_conv_fwd")
    bias128, alog128 = _pad_lanes(dt_bias), _pad_lanes(a_log)
    dt128, cs128, a128 = _ssd_prep(proj, bias128, alog128, name="ssd_prep")
    dt_h, cs_h = _per_head(dt128, t), _per_head(cs128, t)
    cs_row = cs128[:, :SSD_HEADS].T.reshape(SSD_HEADS, nc, 1, CHUNK)
    dskip_h = jnp.broadcast_to(d_skip[0][:, None, None], (SSD_HEADS, 1, SSD_P))
    a_h = jnp.broadcast_to(a128[0, :SSD_HEADS][:, None, None], (SSD_HEADS, 1, SSD_P))
    y, states = _ssd_fwd(xbc, dt_h, cs_h, cs_row, dskip_h, name="ssd_scan_fwd")
    y_ssd = _ssd_gate_fwd(y, proj, ssd_norm_w, name="ssd_gate_fwd")
    saved = (proj, conv_w, conv_b, ssd_norm_w, bias128, dt128, a128, dt_h, cs_h, cs_row, xbc, dskip_h, a_h, states, y)
    return y_ssd, saved


def _ssd_backward(saved, dcat):
    proj, conv_w, conv_b, ssd_norm_w, bias128, dt128, a128, dt_h, cs_h, cs_row, xbc, dskip_h, a_h, states, y = saved
    dy, dz, d_norm_w = _ssd_gate_bwd(y, proj, ssd_norm_w, dcat, name="ssd_gate_bwd")
    dxc, ddt128, dadt128, dd_h = _ssd_bwd(xbc, dt_h, cs_h, cs_row, dskip_h, a_h, states, dy, name="ssd_scan_bwd")
    dxbc, d_conv_w, d_conv_b = _conv_act_bwd(proj, conv_w, conv_b, dxc, kw=SSD_CONV, glu=False, tc=512,
                                             coff=OFF_XBC // 512, ncols=SSD_CONV_DIM, name="ssd_conv_bwd")
    d_raw, d_bias, d_alog, d_dskip = _ssd_prep_bwd(ddt128, dadt128, proj, bias128, dt128, a128,
                                                   dd_h.reshape(SSD_HEADS, SSD_P), name="ssd_prep_bwd")
    return (dz, dxbc, d_raw, d_norm_w, d_conv_w, d_conv_b, d_bias[:, :SSD_HEADS], d_alog[:, :SSD_HEADS],
            d_dskip.reshape(1, SSD_HEADS))


def _rope_tables(positions):
    inv_freq = ROPE_THETA ** (-jnp.arange(0, MLA_ROPE, 2, dtype=F32) / MLA_ROPE)
    ang = positions[0].astype(F32)[:, None] * inv_freq
    cos, sin = jnp.cos(ang), jnp.sin(ang)
    z = jnp.zeros_like(cos)
    return jnp.stack([jnp.concatenate([cos, cos, z, z], axis=1), jnp.concatenate([-sin, z, z, z], axis=1),
                      jnp.concatenate([z, sin, z, z], axis=1)])


def _mla_forward(proj, tabs, q_a_norm_w, wq_pad, kv_a_norm_w, wkv):
    qn = _rmsnorm_fwd(proj, q_a_norm_w, width=MLA_Q_RANK, cblk=OFF_QA // MLA_Q_RANK, name="q_a_norm")
    q = _matmul(qn, wq_pad, name="q_b_proj")
    kvn = _rmsnorm_fwd(proj, kv_a_norm_w, width=MLA_KV_RANK, cblk=OFF_CKV // MLA_KV_RANK, name="kv_a_norm")
    kv = _matmul(kvn, wkv, name="kv_b_proj")
    q3, k3, v3, vt4 = _mla_prep(q, kv, proj, tabs, name="mla_prep")
    o, lse = _attn_fwd(q3, k3, vt4, name="attn_fwd")
    return o, (proj, tabs, q_a_norm_w, wq_pad, kv_a_norm_w, wkv, qn, kvn, q3, k3, v3, o, lse)


def _mla_backward(saved, dcat):
    proj, tabs, q_a_norm_w, wq_pad, kv_a_norm_w, wkv, qn, kvn, q3, k3, v3, o, lse = saved
    dq3, dk3, dv3 = _attn_bwd(q3, k3, v3, o, dcat, lse, name="attn_bwd")
    dq, dkv, dkr = _mla_unprep(dq3, dk3, dv3, tabs, name="mla_unprep")
    d_wq = _matmul(qn, dq, ta=True, out_dtype=BF16, name="d_w_q_b")
    dqn = _matmul(dq, wq_pad, tb=True, name="d_qn")
    dq_a, d_qnw = _rmsnorm_bwd(proj, q_a_norm_w, dqn, width=MLA_Q_RANK, cblk=OFF_QA // MLA_Q_RANK, out_dtype=BF16,
                               name="q_a_norm_bwd")
    d_wkv = _matmul(kvn, dkv, ta=True, out_dtype=BF16, name="d_w_kv_b")
    dkvn = _matmul(dkv, wkv, tb=True, name="d_kvn")
    dckv, d_kvnw = _rmsnorm_bwd(proj, kv_a_norm_w, dkvn, width=MLA_KV_RANK, cblk=OFF_CKV // MLA_KV_RANK,
                                out_dtype=BF16, name="kv_a_norm_bwd")
    return dq_a, dckv, dkr, d_wq, d_wkv, d_qnw, d_kvnw


def _pad_w_q(w):
    r = w.shape[0]
    w3 = w.reshape(r, MLA_HEADS, MLA_NOPE + MLA_ROPE)
    return jnp.pad(w3, ((0, 0), (0, 0), (0, MLA_QK_PAD - MLA_NOPE - MLA_ROPE))).reshape(r, MLA_HEADS * MLA_QK_PAD)


def _unpad_w_q(w):
    r = w.shape[0]
    return w.reshape(r, MLA_HEADS, MLA_QK_PAD)[:, :, :MLA_NOPE + MLA_ROPE].reshape(r, MLA_HEADS * (MLA_NOPE + MLA_ROPE))


W_IN_SEGMENTS = ((0, D_SSM + SSD_CONV_DIM, 0), (D_SSM + SSD_CONV_DIM, D_SSM + SSD_CONV_DIM + SSD_HEADS, OFF_DT),
                 (D_SSM + SSD_CONV_DIM + SSD_HEADS, D_IN - MLA_ROPE, OFF_QA), (D_IN - MLA_ROPE, D_IN, OFF_KR))


def _pad_w_in_shards(g):
    n = g.shape[2]
    pieces, at = [], 0
    for lo, hi, start in sorted(W_IN_SEGMENTS, key=lambda seg: seg[2]):
        if start > at:
            pieces.append(jnp.zeros((g.shape[1], start - at), g.dtype))
        for j in range(N_DEV):
            a, b = max(lo, j * n), min(hi, (j + 1) * n)
            if a < b:
                pieces.append(g[j][:, a - j * n:b - j * n])
        at = start + hi - lo
    pieces.append(jnp.zeros((g.shape[1], D_IN_PAD - at), g.dtype))
    return jnp.concatenate(pieces, axis=1)


def _unpad_w_in_shards(w):
    n = D_IN // N_DEV
    shards = []
    for j in range(N_DEV):
        pieces = []
        for lo, hi, start in W_IN_SEGMENTS:
            a, b = max(lo, j * n), min(hi, (j + 1) * n)
            if a < b:
                pieces.append(w[:, start + a - lo:start + b - lo])
        shards.append(jnp.concatenate(pieces, axis=1) if len(pieces) > 1 else pieces[0])
    return jnp.stack(shards)


WEIGHTS = ['mix_norm_w', 'w_in', 'conv_w', 'conv_b', 'dt_bias', 'a_log', 'd_skip', 'ssd_norm_w', 'q_a_norm_w', 'w_q_b',
           'kv_a_norm_w', 'w_kv_b', 'w_out', 'ffn_norm_w', 'w_ffn_up', 'ffn_conv_w', 'ffn_conv_b', 'w_ffn_down',
           'ple_norm_w', 'w_ple_gate', 'b_ple_gate', 'w_ple_proj', 'ple_post_norm_w', 'final_norm_w']
BIG = ['w_in', 'w_q_b', 'w_kv_b', 'w_out', 'w_ffn_up', 'w_ffn_down', 'w_ple_gate', 'w_ple_proj']
COL_SHARDED = ('w_in', 'w_q_b', 'w_kv_b', 'w_ffn_up', 'w_ple_proj')
CONV = ['conv_w', 'ffn_conv_w']
REPL = [n for n in WEIGHTS if n not in BIG and n not in CONV]
FFN_INV = tuple(int(i) for i in np.argsort(FFN_PERM))


def _cat_cols(g):
    return jnp.concatenate([g[j] for j in range(N_DEV)], axis=1)


def _split_cols(w):
    n = w.shape[1] // N_DEV
    return jnp.stack([w[:, j * n:(j + 1) * n] for j in range(N_DEV)])


def _interleave(v):
    r = v.shape[0]
    return v.reshape(r, N_DEV, FFN_TC)[:, jnp.array(FFN_PERM)].reshape(r, N_DEV * FFN_TC)


def _deinterleave(v):
    r = v.shape[0]
    return v.reshape(r, N_DEV, FFN_TC)[:, jnp.array(FFN_INV)].reshape(r, N_DEV * FFN_TC)


def _assemble_weights(g):
    layout = {
        'w_in': _pad_w_in_shards,
        'w_q_b': lambda v: _pad_w_q(_cat_cols(v)),
        'w_kv_b': _cat_cols,
        'w_out': lambda v: v.reshape(D_MODEL, D_MODEL),
        'w_ffn_up': lambda v: v,
        'w_ffn_down': lambda v: v.reshape(D_FF, D_MODEL),
        'w_ple_gate': lambda v: v.reshape(D_MODEL, D_MODEL),
        'w_ple_proj': _cat_cols,
        'conv_w': _cat_cols,
        'ffn_conv_w': lambda v: _interleave(_cat_cols(v)),
    }
    return {n: layout[n](v) for n, v in g.items()}


WEIGHT_GROUPS = {'a': ['w_in', 'w_q_b', 'w_kv_b', 'conv_w'], 'b': ['w_out', 'w_ffn_up', 'ffn_conv_w'],
                 'c': ['w_ffn_down', 'w_ple_gate', 'w_ple_proj']}
GRAD_GROUPS = {'p': ['w_ple_proj', 'w_ple_gate', 'w_ffn_down'], 'r': ['w_ffn_up'], 's': ['w_out'],
               't': ['w_q_b', 'w_kv_b', 'w_in']}


def _ffn_perm(j):
    return (j % 2) * (N_DEV // 2) + j // 2


def _local_step(x, p, tabs, get_w, s, target, emit, relay, settle):
    t = x.shape[0]
    s = dict(s)
    half = D_MODEL // 2
    up_cols = 2 * D_FF
    ffn_conv_b = _interleave(s['ffn_conv_b'])
    w = dict(get_w('a', None))
    h = _rmsnorm_fwd(x, s['mix_norm_w'], width=D_MODEL, name="mix_norm")
    proj = _matmul(h, w['w_in'], name="in_proj")
    y_ssd, ssd_saved = _ssd_forward(proj, w['conv_w'], s['conv_b'], s['dt_bias'], s['a_log'], s['d_skip'],
                                    s['ssd_norm_w'])
    o, mla_saved = _mla_forward(proj, tabs, s['q_a_norm_w'], w['w_q_b'], s['kv_a_norm_w'], w['w_kv_b'])
    tk_o, tn_o = _tile(half, MM_TK), _tile(D_MODEL, MM_TILE)
    w.update(get_w('b', o))
    x1 = _matmul(y_ssd, w['w_out'], add=x, mnk=(t, D_MODEL, half), name="out_proj_ssd")
    x1 = _matmul(o, w['w_out'], add=x1, mnk=(t, D_MODEL, half), name="out_proj_mla",
                 b_spec=pl.BlockSpec((tk_o, tn_o), lambda i, j, kk: (kk + half // tk_o, j)))
    hf = _rmsnorm_fwd(x1, s['ffn_norm_w'], width=D_MODEL, name="ffn_norm")
    tk_u = _tile(D_MODEL, MM_TK)
    u = _matmul(hf, w['w_ffn_up'], mnk=(t, up_cols, D_MODEL), tn=FFN_TC, name="ffn_up",
                b_spec=pl.BlockSpec((1, tk_u, FFN_TC), lambda i, j, kk: (_ffn_perm(j), kk, 0)))
    act = _conv_act_fwd(u, w['ffn_conv_w'], ffn_conv_b, kw=FFN_CONV, glu=True, tc=2 * FFN_TC, coff=0, ncols=up_cols,
                        out_dtype=BF16, name="ffn_act")
    w.update(get_w('c', act))
    x2 = _matmul(act, w['w_ffn_down'], add=x1, name="ffn_down")
    hp = _rmsnorm_fwd(x2, s['ple_norm_w'], width=D_MODEL, name="ple_norm")
    gl = _matmul(hp, w['w_ple_gate'], bias=s['b_ple_gate'], name="ple_gate")
    pe = _matmul(p, w['w_ple_proj'], name="ple_proj")
    x3 = _ple_fwd(x2, gl, pe, s['ple_post_norm_w'], name="ple_mix")
    loss, dx3, d_final = _loss_head(x3, s['final_norm_w'], target, name="loss_head")
    dgl, d_bgate, dpe, d_post = _ple_bwd(dx3, gl, pe, s['ple_post_norm_w'], name="ple_mix_bwd")
    d_wproj = _matmul(p, dpe, ta=True, out_dtype=BF16, name="d_w_ple_proj")
    d_wgate = _matmul(hp, dgl, ta=True, out_dtype=BF16, name="d_w_ple_gate")
    dhp = _matmul(dgl, w['w_ple_gate'], tb=True, name="d_ple_normed")
    dx2, d_plenorm, dx2b = _rmsnorm_bwd(x2, s['ple_norm_w'], dhp, dx3, width=D_MODEL, also_bf16=True,
                                        name="ple_norm_bwd")
    dact = _matmul(dx2b, w['w_ffn_down'], tb=True, name="d_ffn_act")
    d_wdown = _matmul(act, dx2b, ta=True, out_dtype=BF16, name="d_w_ffn_down")
    zz = emit('p', {'w_ple_proj': _split_cols(d_wproj), 'w_ple_gate': d_wgate.reshape(N_DEV, D_MODEL // N_DEV, D_MODEL),
                    'w_ffn_down': d_wdown.reshape(N_DEV, D_FF // N_DEV, D_MODEL)})
    du, d_fconv_w, d_fconv_b = _conv_act_bwd(u, w['ffn_conv_w'], ffn_conv_b + zz, dact, kw=FFN_CONV, glu=True,
                                             tc=2 * FFN_TC, coff=0, ncols=up_cols, name="ffn_act_bwd")
    zz = zz + relay('p', du)
    tm_u = _tile(D_MODEL, MM_TILE)
    d_wup = _matmul(hf, du, ta=True, out_dtype=BF16, mnk=(D_MODEL, up_cols, t), tn=FFN_TC, name="d_w_ffn_up",
                    o_spec=pl.BlockSpec((1, tm_u, FFN_TC), lambda i, j, kk: (_ffn_perm(j), i, 0)),
                    o_shape=(N_DEV, D_MODEL, FFN_TC))
    zz = zz + emit('r', {'w_ffn_up': d_wup})
    zero_row = jnp.zeros((1, D_MODEL), F32)
    dhf = _matmul(du, w['w_ffn_up'], tb=True, mnk=(t, D_MODEL, up_cols), tm=t, tk=FFN_TC, name="d_ffn_normed",
                  bias=zero_row + zz,
                  b_spec=pl.BlockSpec((1, tn_o, FFN_TC), lambda i, j, kk: (_ffn_perm(kk), j, 0)))
    zz = zz + relay('r', dhf) + settle('p')
    dx1, d_ffnnorm, dx1b = _rmsnorm_bwd(x1, s['ffn_norm_w'] + zz, dhf, dx2, width=D_MODEL, also_bf16=True,
                                        name="ffn_norm_bwd")
    dcat = _matmul(dx1b, w['w_out'], tb=True, name="d_mixed")
    d_wout = jnp.concatenate([_matmul(y_ssd, dx1b, ta=True, out_dtype=BF16, name="d_w_out_ssd"),
                              _matmul(o, dx1b, ta=True, out_dtype=BF16, name="d_w_out_mla")], axis=0)
    zz = zz + emit('s', {'w_out': d_wout.reshape(N_DEV, D_MODEL // N_DEV, D_MODEL)})
    ssd_saved = ssd_saved[:3] + (ssd_saved[3] + zz,) + ssd_saved[4:]
    dz, dxbc, d_raw, d_ssdnorm, d_conv_w, d_conv_b, d_dtb, d_alog, d_dskip = _ssd_backward(ssd_saved, dcat)
    zz = zz + relay('s', dz)
    mla_saved = mla_saved[:-1] + (mla_saved[-1] + zz,)
    dq_a, dckv, dkr, d_wq, d_wkv, d_qnorm, d_kvnorm = _mla_backward(mla_saved, dcat)
    d_raw = (d_raw + settle('r')).astype(BF16)
    dproj = jnp.concatenate([dz, dxbc, dq_a, dckv, dkr, d_raw], axis=1)
    d_win = _matmul(h, dproj, ta=True, out_dtype=BF16, name="d_w_in")
    zz = emit('t', {'w_in': _unpad_w_in_shards(d_win), 'w_q_b': _split_cols(_unpad_w_q(d_wq)),
                    'w_kv_b': _split_cols(d_wkv)}) + settle('s')
    dh = _matmul(dproj, w['w_in'], tb=True, bias=zero_row + zz, name="d_in_normed")
    zz = relay('t', dh)
    dx, d_mixnorm = _rmsnorm_bwd(x, s['mix_norm_w'] + zz, dh, dx1, width=D_MODEL, name="mix_norm_bwd")
    conv = {'conv_w': d_conv_w, 'ffn_conv_w': _deinterleave(d_fconv_w)}
    vec = {
        'mix_norm_w': d_mixnorm, 'conv_b': d_conv_b, 'dt_bias': d_dtb, 'a_log': d_alog, 'd_skip': d_dskip,
        'ssd_norm_w': d_ssdnorm, 'q_a_norm_w': d_qnorm, 'kv_a_norm_w': d_kvnorm, 'ffn_norm_w': d_ffnnorm,
        'ffn_conv_b': _deinterleave(d_fconv_b), 'ple_norm_w': d_plenorm, 'b_ple_gate': d_bgate,
        'ple_post_norm_w': d_post, 'final_norm_w': d_final,
    }
    return loss, dx, conv, vec


MESH = pl.DeviceIdType.MESH
FLIPS = ((0, 0, 1), (1, 0, 0), (0, 1, 0), (1, 1, 0), (1, 0, 1), (0, 1, 1), (1, 1, 1))


def _exchange(items, *, gather, name):
    n = len(items)

    def body(*refs):
        ins, outs = refs[:n], refs[n:2 * n]
        send_sems, recv_sems, local_sems = refs[2 * n:]
        x, y, c = lax.axis_index("x"), lax.axis_index("y"), lax.axis_index("c")
        me = 4 * x + 2 * y + c
        peers = [(jnp.where(fx, 1 - x, x), jnp.where(fy, 1 - y, y), jnp.where(fc, 1 - c, c)) for fx, fy, fc in FLIPS]
        slot = [4 * px + 2 * py + pc for px, py, pc in peers]
        local, sends = [], []
        for wi in range(n):
            cp = pltpu.make_async_copy(ins[wi] if gather else ins[wi].at[me], outs[wi].at[me], local_sems.at[wi])
            cp.start()
            local.append(cp)
            for k, peer in enumerate(peers):
                cp = pltpu.make_async_remote_copy(
                    src_ref=ins[wi] if gather else ins[wi].at[slot[k]], dst_ref=outs[wi].at[me],
                    send_sem=send_sems.at[k, wi], recv_sem=recv_sems.at[k, wi], device_id=peer, device_id_type=MESH)
                cp.start()
                sends.append(cp)
        for wi in range(n):
            for k, peer in enumerate(peers):
                pltpu.make_async_remote_copy(
                    src_ref=outs[wi].at[slot[k]], dst_ref=outs[wi].at[slot[k]], send_sem=send_sems.at[k, wi],
                    recv_sem=recv_sems.at[k, wi], device_id=peer, device_id_type=MESH).wait_recv()
        for cp in sends:
            cp.wait_send()
        for cp in local:
            cp.wait()

    hbm = pl.BlockSpec(memory_space=pltpu.HBM)
    out_shape = [jax.ShapeDtypeStruct(((N_DEV,) + v.shape) if gather else v.shape, v.dtype) for v in items]
    return pl.pallas_call(
        body, name=name, in_specs=[hbm] * n, out_specs=[hbm] * n, out_shape=out_shape,
        scratch_shapes=[pltpu.SemaphoreType.DMA((len(FLIPS), n)), pltpu.SemaphoreType.DMA((len(FLIPS), n)),
                        pltpu.SemaphoreType.DMA((n,))],
    )(*items)


HBM_SPEC = pl.BlockSpec(memory_space=pltpu.HBM)
SEM_SPEC = pl.BlockSpec(memory_space=pltpu.SEMAPHORE)
EFFECT = pltpu.SideEffectType.DATAFLOW_SIDE_EFFECTING


def _split_start(bufs, ncopies, plan, *, name):
    nb = len(bufs)

    def body(*refs):
        send_sems, recv_sems, token = refs[nb], refs[nb + 1], refs[2 * nb + 2]
        for i, (src, dst, peer, _) in enumerate(plan(refs[:nb])):
            pltpu.make_async_remote_copy(src_ref=src, dst_ref=dst, send_sem=send_sems.at[i], recv_sem=recv_sems.at[i],
                                         device_id=peer, device_id_type=MESH).start()
        token[...] = jnp.zeros_like(token)

    res = pl.pallas_call(
        body, name=name, in_specs=[HBM_SPEC] * nb,
        out_specs=[SEM_SPEC, SEM_SPEC] + [HBM_SPEC] * nb + [pl.BlockSpec(memory_space=pltpu.VMEM)],
        out_shape=[pltpu.SemaphoreType.DMA((ncopies,)), pltpu.SemaphoreType.DMA((ncopies,))]
        + [pltpu.HBM(v.shape, v.dtype) for v in bufs] + [jax.ShapeDtypeStruct((HALO, LANES), F32)],
        input_output_aliases={i: 2 + i for i in range(nb)},
        compiler_params=pltpu.CompilerParams(has_side_effects=EFFECT),
    )(*[pltpu.with_memory_space_constraint(v, pltpu.HBM) for v in bufs])
    return (res[0], res[1], list(res[2:2 + nb])), res[2 + nb]


def _split_wait(started, after, plan, local_plan, *, name):
    send_sems, recv_sems, bufs = started
    nb = len(bufs)
    nlocal = len(local_plan(bufs))

    def body(*refs):
        send_sems, recv_sems = refs[nb], refs[nb + 1]
        local_sems = refs[2 * nb + 3]
        local = []
        for j, (src, dst) in enumerate(local_plan(refs[:nb])):
            cp = pltpu.make_async_copy(src, dst, local_sems.at[j])
            cp.start()
            local.append(cp)
        for i, (src, _, peer, incoming) in enumerate(plan(refs[:nb])):
            cp = pltpu.make_async_remote_copy(src_ref=src, dst_ref=incoming, send_sem=send_sems.at[i],
                                              recv_sem=recv_sems.at[i], device_id=peer, device_id_type=MESH)
            cp.wait_send()
            cp.wait_recv()
        for cp in local:
            cp.wait()

    res = pl.pallas_call(
        body, name=name, in_specs=[HBM_SPEC] * nb + [SEM_SPEC, SEM_SPEC, pl.BlockSpec(memory_space=pl.ANY)],
        out_specs=[HBM_SPEC] * nb, out_shape=[pltpu.HBM(v.shape, v.dtype) for v in bufs],
        input_output_aliases={i: i for i in range(nb)},
        scratch_shapes=[pltpu.SemaphoreType.DMA((max(nlocal, 1),))],
        compiler_params=pltpu.CompilerParams(has_side_effects=EFFECT),
    )(*bufs, send_sems, recv_sems, after)
    return list(res)


def _hold(values, after, *, name):
    n = len(values)

    def body(*refs):
        del refs

    return list(pl.pallas_call(
        body, name=name, in_specs=[HBM_SPEC] * n + [pl.BlockSpec(memory_space=pl.ANY)], out_specs=[HBM_SPEC] * n,
        out_shape=[pltpu.HBM(v.shape, v.dtype) for v in values], input_output_aliases={i: i for i in range(n)},
    )(*values, after))


def _place():
    x, y, c = lax.axis_index("x"), lax.axis_index("y"), lax.axis_index("c")
    others = [((1 - x, y, c), 2 * (1 - x) + y), ((x, 1 - y, c), 2 * x + 1 - y), ((1 - x, 1 - y, c), 2 * (1 - x) + 1 - y)]
    return 4 * x + 2 * y + c, 2 * x + y, c, (x, y, 1 - c), others


def _gather1_plan(n):
    def plan(refs):
        me, _, _, sibling, others = _place()
        out = []
        for wi in range(n):
            item, land = refs[wi], refs[n + wi]
            out.append((item, land.at[me], sibling, land.at[me + 1 - 2 * lax.axis_index("c")]))
            for peer, chip in others:
                out.append((item, land.at[me], peer, land.at[2 * chip + lax.axis_index("c")]))
        return out

    return plan


def _gather1_local(n):
    def plan(refs):
        me = _place()[0]
        return [(refs[wi], refs[n + wi].at[me]) for wi in range(n)]

    return plan


def _gather2_plan(n):
    def plan(refs):
        _, _, c, sibling, others = _place()
        out = []
        for wi in range(n):
            land = refs[wi]
            for _, chip in others:
                out.append((land.at[2 * chip + c], land.at[2 * chip + c], sibling, land.at[2 * chip + 1 - c]))
        return out

    return plan


def _gather_start(items, *, name):
    lands = [lax.empty((N_DEV,) + v.shape, v.dtype) for v in items]
    return _split_start(items + lands, 4 * len(items), _gather1_plan(len(items)), name=name)


def _gather_forward(started, after, *, name):
    n = len(started[2]) // 2
    bufs = _split_wait(started, after, _gather1_plan(n), _gather1_local(n), name=name + "_wait")
    return _split_start(bufs[n:], 3 * n, _gather2_plan(n), name=name + "_start")


def _gather_finish(started, after, *, name):
    n = len(started[2])
    return _split_wait(started, after, _gather2_plan(n), lambda refs: [], name=name)


def _handshake(peers):
    barrier = pltpu.get_barrier_semaphore()
    for peer in peers:
        pl.semaphore_signal(barrier, inc=1, device_id=peer, device_id_type=MESH)
    pl.semaphore_wait(barrier, len(peers))


def _remote(src, dst, send_sem, recv_sem, peer):
    return pltpu.make_async_remote_copy(src_ref=src, dst_ref=dst, send_sem=send_sem, recv_sem=recv_sem, device_id=peer,
                                        device_id_type=MESH)


def _sequencer_gather(items, *, collective_id, name):
    n = len(items)
    srcs = [jax.new_ref(v, memory_space=pltpu.MemorySpace.HBM) for v in items]
    lands = [jax.empty_ref(jax.ShapeDtypeStruct((N_DEV,) + v.shape, v.dtype), memory_space=pltpu.MemorySpace.HBM)
             for v in items]
    dma = pltpu.SemaphoreType.DMA

    @pl.kernel(mesh=plsc.ScalarSubcoreMesh(axis_name="sequencer", num_cores=1), name=name,
               scratch_types=(dma((4 * n,)), dma((4 * n,)), dma((3 * n,)), dma((3 * n,)), dma((n,))),
               compiler_params=pltpu.CompilerParams(collective_id=collective_id))
    def launch(send1, recv1, send2, recv2, local_sems):
        _, _, _, sibling, others = _place()
        _handshake([sibling] + [peer for peer, _ in others])
        hop1 = _gather1_plan(n)(srcs + lands)
        hop2 = _gather2_plan(n)(lands)
        local = [pltpu.make_async_copy(src, dst, local_sems.at[j])
                 for j, (src, dst) in enumerate(_gather1_local(n)(srcs + lands))]
        for cp in local:
            cp.start()
        for i, (src, dst, peer, _) in enumerate(hop1):
            _remote(src, dst, send1.at[i], recv1.at[i], peer).start()
        for wi in range(n):
            for j in range(3):
                i1, i2 = 4 * wi + 1 + j, 3 * wi + j
                src, _, peer, incoming = hop1[i1]
                _remote(src, incoming, send1.at[i1], recv1.at[i1], peer).wait_recv()
                src, dst, peer, _ = hop2[i2]
                _remote(src, dst, send2.at[i2], recv2.at[i2], peer).start()
        for wi in range(n):
            src, _, peer, incoming = hop1[4 * wi]
            _remote(src, incoming, send1.at[4 * wi], recv1.at[4 * wi], peer).wait_recv()
        for i, (src, _, peer, incoming) in enumerate(hop2):
            cp = _remote(src, incoming, send2.at[i], recv2.at[i], peer)
            cp.wait_send()
            cp.wait_recv()
        for i, (src, dst, peer, _) in enumerate(hop1):
            _remote(src, dst, send1.at[i], recv1.at[i], peer).wait_send()
        for cp in local:
            cp.wait()

    launch()
    return [land[...] for land in lands]


def _sequencer_exchange(sources, land_shapes, ncopies, plan, local_plan, peers, *, collective_id, name):
    srcs = [jax.new_ref(v, memory_space=pltpu.MemorySpace.HBM) for v in sources]
    lands = [jax.empty_ref(s, memory_space=pltpu.MemorySpace.HBM) for s in land_shapes]
    nlocal = len(local_plan(srcs + lands))
    dma = pltpu.SemaphoreType.DMA

    @pl.kernel(mesh=plsc.ScalarSubcoreMesh(axis_name="sequencer", num_cores=1), name=name,
               scratch_types=(dma((ncopies,)), dma((ncopies,)), dma((max(nlocal, 1),))),
               compiler_params=pltpu.CompilerParams(collective_id=collective_id))
    def launch(send_sems, recv_sems, local_sems):
        _handshake(peers(_place()))
        copies = plan(srcs + lands)
        local = [pltpu.make_async_copy(src, dst, local_sems.at[j])
                 for j, (src, dst) in enumerate(local_plan(srcs + lands))]
        for cp in local:
            cp.start()
        for i, (src, dst, peer, _) in enumerate(copies):
            _remote(src, dst, send_sems.at[i], recv_sems.at[i], peer).start()
        for i, (src, _, peer, incoming) in enumerate(copies):
            cp = _remote(src, incoming, send_sems.at[i], recv_sems.at[i], peer)
            cp.wait_send()
            cp.wait_recv()
        for cp in local:
            cp.wait()

    launch()
    return [land[...] for land in lands]


def _sequencer_scatter_hop2(sums, *, collective_id, name):
    n = len(sums)
    shapes = [jax.ShapeDtypeStruct(v.shape, v.dtype) for v in sums]
    return _sequencer_exchange(sums, shapes, 3 * n, _scatter2_plan(n), _scatter2_local(n),
                               lambda place: [peer for peer, _ in place[4]], collective_id=collective_id, name=name)


N_CHIP = N_DEV // 2


def _scatter1_plan(n):
    def plan(refs):
        _, _, c, sibling, _ = _place()
        out = []
        for wi in range(n):
            parts, half = refs[wi], refs[n + wi]
            for chip in range(N_CHIP):
                out.append((parts.at[2 * chip + 1 - c], half.at[chip], sibling, half.at[chip]))
        return out

    return plan


def _scatter2_plan(n):
    def plan(refs):
        _, my_chip, _, _, others = _place()
        out = []
        for wi in range(n):
            sums, recv = refs[wi], refs[n + wi]
            for peer, chip in others:
                out.append((sums.at[chip], recv.at[my_chip], peer, recv.at[chip]))
        return out

    return plan


def _scatter2_local(n):
    def plan(refs):
        my_chip = _place()[1]
        return [(refs[wi].at[my_chip], refs[n + wi].at[my_chip]) for wi in range(n)]

    return plan


def _pair_add(parts, half, core, *, name):
    _, r, c = parts.shape
    tr = max(d for d in range(HALO, 257, HALO) if r % d == 0) if r > 256 else r
    parts4 = parts.reshape(N_CHIP, 2, r, c)

    def body(core_ref, p_ref, h_ref, o_ref):
        o_ref[...] = (p_ref[:, 0].astype(F32) + h_ref[...].astype(F32)).astype(o_ref.dtype)

    return pl.pallas_call(
        body, name=name,
        grid_spec=pltpu.PrefetchScalarGridSpec(
            num_scalar_prefetch=1, grid=(r // tr,),
            in_specs=[pl.BlockSpec((N_CHIP, 1, tr, c), lambda i, core_ref: (0, core_ref[0], i, 0)),
                      pl.BlockSpec((N_CHIP, tr, c), lambda i, core_ref: (0, i, 0))],
            out_specs=pl.BlockSpec((N_CHIP, tr, c), lambda i, core_ref: (0, i, 0))),
        out_shape=jax.ShapeDtypeStruct((N_CHIP, r, c), parts.dtype), compiler_params=_cp("parallel"),
    )(core, parts4, half)


def _scatter_start(parts, *, name):
    halves = [lax.empty((N_CHIP,) + v.shape[1:], v.dtype) for v in parts]
    return _split_start(parts + halves, N_CHIP * len(parts), _scatter1_plan(len(parts)), name=name)


def _adamw(parts, w, m, v, *, name):
    r, c = w.shape
    nparts = parts.shape[0]
    tr = max(d for d in range(HALO, 129, HALO) if r % d == 0) if r > 128 else r

    def body(p_ref, w_ref, m_ref, v_ref, g_ref, d_ref, mo_ref, vo_ref):
        g = p_ref[0].astype(F32)
        for k in range(1, nparts):
            g = g + p_ref[k].astype(F32)
        mn = ADAM_B1 * m_ref[...] + (1.0 - ADAM_B1) * g
        vn = ADAM_B2 * v_ref[...] + (1.0 - ADAM_B2) * (g * g)
        m_hat = mn / (1.0 - ADAM_B1 ** ADAM_STEP)
        v_hat = vn / (1.0 - ADAM_B2 ** ADAM_STEP)
        g_ref[...] = g
        d_ref[...] = -ADAM_LR * (m_hat / (jnp.sqrt(v_hat) + ADAM_EPS) + ADAM_WD * w_ref[...])
        mo_ref[...] = mn
        vo_ref[...] = vn

    blk = pl.BlockSpec((tr, c), lambda i: (i, 0))
    return pl.pallas_call(
        body, name=name, grid=(r // tr,), in_specs=[pl.BlockSpec((nparts, tr, c), lambda i: (0, i, 0)), blk, blk, blk],
        out_specs=[blk] * 4, out_shape=[jax.ShapeDtypeStruct((r, c), F32)] * 4, compiler_params=_cp("parallel"),
    )(parts, w, m, v)


def _pack_rows(vs, rows):
    lead = vs[0].shape[:-1] if vs[0].ndim > 1 else ()
    flat = jnp.concatenate(vs, axis=-1)
    pad = rows * LANES - flat.shape[-1]
    flat = jnp.pad(flat, [(0, 0)] * len(lead) + [(0, pad)])
    return flat.reshape(lead + (rows, LANES))


def kernel(x, p, positions, mix_norm_w, w_in, conv_w, conv_b, dt_bias, a_log, d_skip, ssd_norm_w, q_a_norm_w, w_q_b, kv_a_norm_w, w_kv_b, w_out, ffn_norm_w, w_ffn_up, ffn_conv_w, ffn_conv_b, w_ffn_down, ple_norm_w, w_ple_gate, b_ple_gate, w_ple_proj, ple_post_norm_w, final_norm_w, loss_target, m_mix_norm_w, m_w_in, m_conv_w, m_conv_b, m_dt_bias, m_a_log, m_d_skip, m_ssd_norm_w, m_q_a_norm_w, m_w_q_b, m_kv_a_norm_w, m_w_kv_b, m_w_out, m_ffn_norm_w, m_w_ffn_up, m_ffn_conv_w, m_ffn_conv_b, m_w_ffn_down, m_ple_norm_w, m_w_ple_gate, m_b_ple_gate, m_w_ple_proj, m_ple_post_norm_w, m_final_norm_w, v_mix_norm_w, v_w_in, v_conv_w, v_conv_b, v_dt_bias, v_a_log, v_d_skip, v_ssd_norm_w, v_q_a_norm_w, v_w_q_b, v_kv_a_norm_w, v_w_kv_b, v_w_out, v_ffn_norm_w, v_w_ffn_up, v_ffn_conv_w, v_ffn_conv_b, v_w_ffn_down, v_ple_norm_w, v_w_ple_gate, v_b_ple_gate, v_w_ple_proj, v_ple_post_norm_w, v_final_norm_w):
    given = dict(locals())
    shapes = {n: given[n].shape for n in WEIGHTS}
    w2 = {n: given[n].reshape(given[n].shape[-2:] if n in BIG or n in CONV else (1, -1)) for n in WEIGHTS}
    m2 = {n: given['m_' + n].reshape(w2[n].shape) for n in WEIGHTS}
    v2 = {n: given['v_' + n].reshape(w2[n].shape) for n in WEIGHTS}
    me = 4 * lax.axis_index("x") + 2 * lax.axis_index("y") + lax.axis_index("c")

    core = lax.axis_index("c").astype(jnp.int32).reshape(1)

    def shards(grp, zero):
        return [(w2[n] + zero).astype(BF16) if n in BIG else w2[n] + zero for n in WEIGHT_GROUPS[grp]]

    first, token = _gather_start(shards('a', 0.0), name="gather_a_hop1")
    first, token = _gather_forward(first, token, name="gather_a_hop2")
    zero = token[0, 0]
    later = dict(zip(WEIGHT_GROUPS['b'], _sequencer_gather(shards('b', zero), collective_id=1, name="gather_b")))
    later.update(zip(WEIGHT_GROUPS['c'], _sequencer_gather(shards('c', zero), collective_id=2, name="gather_c")))

    def get_w(grp, after):
        if grp == 'a':
            lands = dict(zip(WEIGHT_GROUPS[grp], _gather_finish(first, token, name="gather_a_done")))
        else:
            names = WEIGHT_GROUPS[grp]
            lands = dict(zip(names, _hold([later[n] for n in names], after, name="gather_" + grp + "_use")))
        return _assemble_weights(lands)

    scatters = {}

    hop_ids = {grp: 2 + 2 * i for i, grp in enumerate(GRAD_GROUPS)}

    def zero_of(arrays):
        return sum(v[(0,) * v.ndim].astype(F32) * 0.0 for v in arrays)

    def emit(grp, grads):
        scatters[grp], tok = _scatter_start([grads[n] for n in GRAD_GROUPS[grp]], name="scatter_" + grp + "_hop1")
        return tok[0, 0]

    def relay(grp, after):
        n = len(GRAD_GROUPS[grp])
        bufs = _split_wait(scatters[grp], after, _scatter1_plan(n), lambda refs: [], name="scatter_" + grp + "_hop1_wait")
        sums = [_pair_add(bufs[i], bufs[n + i], core, name="scatter_%s_add%d" % (grp, i)) for i in range(n)]
        scatters[grp] = _sequencer_scatter_hop2(sums, collective_id=hop_ids[grp] + 1, name="scatter_" + grp + "_hop2")
        return zero_of(sums)

    out_g, out_d, out_m, out_v = {}, {}, {}, {}

    def settle(grp):
        return zero_of(scatters[grp])

    def update(grp, behind=None):
        for n, parts in zip(GRAD_GROUPS[grp], scatters[grp]):
            wn = w2[n] if behind is None else w2[n] + behind
            out_g[n], out_d[n], out_m[n], out_v[n] = _adamw(parts, wn, m2[n], v2[n], name="adamw_" + n)

    vecs = {n: w2[n] for n in REPL}
    vecs['mix_norm_w'] = vecs['mix_norm_w'] + zero
    loss, dx, g_conv, g_vec = _local_step(x[0], p[0, 0], _rope_tables(positions), get_w, vecs, loss_target[0], emit,
                                          relay, settle)
    n_small = sum(g_vec[n].shape[1] for n in REPL) + sum(g_conv[n].size for n in CONV) + 1
    rows_small = -(-n_small // (LANES * HALO)) * HALO
    small = _pack_rows([g_vec[n] for n in REPL] + [g_conv[n].reshape(1, -1) for n in CONV] + [loss], rows_small)

    for grp in list(GRAD_GROUPS)[:-1]:
        update(grp)
    all_small = _exchange([small], gather=True, name="gather_small_grads")[0].reshape(N_DEV, rows_small * LANES)
    update(list(GRAD_GROUPS)[-1], zero_of([all_small]))
    pieces, off = [], 0
    for n in REPL:
        k = g_vec[n].shape[1]
        pieces.append(all_small[:, off:off + k])
        off += k
    for n in CONV:
        kw, cols = g_conv[n].shape
        full = all_small[:, off:off + kw * cols].reshape(N_DEV, kw, cols)
        mine = lax.dynamic_slice_in_dim(full, me * (cols // N_DEV), cols // N_DEV, axis=2)
        pieces.append(mine.reshape(N_DEV, kw * (cols // N_DEV)))
        off += kw * cols
    pieces.append(all_small[:, off:off + 1])
    small_names = REPL + CONV
    n_mine = sum(q.shape[1] for q in pieces)
    rows_mine = -(-n_mine // (LANES * HALO)) * HALO
    zero = jnp.zeros((1, 1), F32)
    packed = [_pack_rows([src[n].reshape(1, -1) for n in small_names] + [zero], rows_mine).reshape(rows_mine, LANES)
              for src in (w2, m2, v2)]
    sg, sd, sm, sv = _adamw(_pack_rows(pieces, rows_mine), *packed, name="adamw_small")
    off = 0
    for n in small_names:
        k = w2[n].size
        for dst, src in ((out_g, sg), (out_d, sd), (out_m, sm), (out_v, sv)):
            dst[n] = src.reshape(-1)[off:off + k].reshape(w2[n].shape)
        off += k
    total_loss = sg.reshape(-1)[off]

    outs = [total_loss, dx[None]]
    for res in (out_g, out_d, out_m, out_v):
        outs += [res[n].reshape(shapes[n]) for n in WEIGHTS]
    return tuple(outs)
```

```python
import math

import numpy as np
import jax
import jax.numpy as jnp
from jax import lax
from jax.experimental import pallas as pl
from jax.experimental.pallas import tpu as pltpu
from jax.experimental.pallas import tpu_sc as plsc

F32 = jnp.float32
BF16 = jnp.bfloat16
HI = lax.Precision.HIGHEST

D_MODEL = 2048
CHUNK = 64
D_SSM = 1024
SSD_P = 64
SSD_HEADS = 16
SSD_GROUPS = 2
SSD_N = 128
SSD_CONV = 4
SSD_CONV_DIM = D_SSM + 2 * SSD_GROUPS * SSD_N
MLA_HEADS = 8
MLA_NOPE = 128
MLA_ROPE = 64
MLA_V = 128
MLA_Q_RANK = 512
MLA_KV_RANK = 256
MLA_QK_PAD = 256
ROPE_THETA = 10000.0
D_FF = 5632
FFN_CONV = 3
PLE_DIM = 256
NORM_EPS = 1e-6
ADAM_LR, ADAM_B1, ADAM_B2, ADAM_EPS, ADAM_WD, ADAM_STEP = 0.001, 0.9, 0.999, 1e-08, 0.01, 10
N_DEV = 8

OFF_Z, OFF_XBC, OFF_QA, OFF_CKV, OFF_KR, OFF_DT, D_IN_PAD = 0, 1024, 2560, 3072, 3328, 3456, 3584
D_IN = 3408
LANES = 128
HALO = 8
VMEM_LIMIT = 56 * 1024 * 1024
FFN_TC = D_FF * 2 // N_DEV
FFN_PERM = (0, 4, 1, 5, 2, 6, 3, 7)
NEG = -1e30


def _cp(*sem):
    return pltpu.CompilerParams(dimension_semantics=tuple(sem), vmem_limit_bytes=VMEM_LIMIT)


def _tile(n, want):
    if n <= want:
        return n
    best = max(d for d in range(LANES, want + 1, LANES) if n % d == 0)
    return best


def _sigmoid(x):
    return 0.5 * (jnp.tanh(0.5 * x) + 1.0)


def _silu(x):
    return x * _sigmoid(x)


def _dsilu(x):
    s = _sigmoid(x)
    return s * (1.0 + x * (1.0 - s))


MM_TILE = 1408
MM_TK = 2816


def _matmul(a, b, *, ta=False, tb=False, out_dtype=F32, add=None, bias=None, tm=MM_TILE, tn=MM_TILE, tk=MM_TK, name,
            mnk=None, a_spec=None, b_spec=None, o_spec=None, o_shape=None):
    if mnk is None:
        m, k = (a.shape[1], a.shape[0]) if ta else a.shape
        n = b.shape[0] if tb else b.shape[1]
        assert k == (b.shape[1] if tb else b.shape[0])
    else:
        m, n, k = mnk
    tm, tn, tk = _tile(m, tm), _tile(n, tn), _tile(k, tk)
    nk = k // tk
    dims = (((0 if ta else 1,), (1 if tb else 0,)), ((), ()))

    def body(*refs):
        a_ref, b_ref = refs[0], refs[1]
        pos = 2
        add_ref = bias_ref = None
        if add is not None:
            add_ref = refs[pos]
            pos += 1
        if bias is not None:
            bias_ref = refs[pos]
            pos += 1
        o_ref = refs[pos]
        kk = pl.program_id(2)
        av = a_ref[...]
        bv = b_ref[...]
        av = av.reshape(av.shape[-2:]).astype(BF16)
        bv = bv.reshape(bv.shape[-2:]).astype(BF16)
        prod = lax.dot_general(av, bv, dims, preferred_element_type=F32)

        def finish(r):
            if bias_ref is not None:
                r = r + bias_ref[...]
            if add_ref is not None:
                r = r + add_ref[...].astype(F32)
            o_ref[...] = r.astype(out_dtype).reshape(o_ref.shape)

        if nk == 1:
            finish(prod)
        else:
            acc_ref = refs[pos + 1]

            @pl.when(kk == 0)
            def _():
                acc_ref[...] = prod

            @pl.when(kk > 0)
            def _():
                acc_ref[...] += prod

            @pl.when(kk == nk - 1)
            def _():
                finish(acc_ref[...])

    if a_spec is None:
        a_spec = (pl.BlockSpec((tk, tm), lambda i, j, kk: (kk, i)) if ta
                  else pl.BlockSpec((tm, tk), lambda i, j, kk: (i, kk)))
    if b_spec is None:
        b_spec = (pl.BlockSpec((tn, tk), lambda i, j, kk: (j, kk)) if tb
                  else pl.BlockSpec((tk, tn), lambda i, j, kk: (kk, j)))
    if o_spec is None:
        o_spec = pl.BlockSpec((tm, tn), lambda i, j, kk: (i, j))
    if o_shape is None:
        o_shape = (m, n)
    in_specs = [a_spec, b_spec]
    args = [a, b]
    if add is not None:
        in_specs.append(pl.BlockSpec((tm, tn), lambda i, j, kk: (i, j)))
        args.append(add)
    if bias is not None:
        in_specs.append(pl.BlockSpec((1, tn), lambda i, j, kk: (0, j)))
        args.append(bias)
    return pl.pallas_call(
        body, name=name, grid=(m // tm, n // tn, nk), in_specs=in_specs, out_specs=o_spec,
        out_shape=jax.ShapeDtypeStruct(o_shape, out_dtype),
        scratch_shapes=[pltpu.VMEM((tm, tn), F32)] if nk > 1 else [],
        compiler_params=_cp("parallel", "parallel", "arbitrary"),
    )(*args)


def _rmsnorm_fwd(x, w, *, width, cblk=0, out_dtype=BF16, tr=256, name):
    t = x.shape[0]

    def body(x_ref, w_ref, o_ref):
        xv = x_ref[...].astype(F32)
        r = lax.rsqrt(jnp.mean(xv * xv, axis=-1, keepdims=True) + NORM_EPS)
        o_ref[...] = (xv * r * w_ref[...]).astype(out_dtype)

    return pl.pallas_call(
        body, name=name, grid=(t // tr,),
        in_specs=[pl.BlockSpec((tr, width), lambda i: (i, cblk)), pl.BlockSpec((1, width), lambda i: (0, 0))],
        out_specs=pl.BlockSpec((tr, width), lambda i: (i, 0)),
        out_shape=jax.ShapeDtypeStruct((t, width), out_dtype),
        compiler_params=_cp("parallel"),
    )(x, w)


def _rmsnorm_bwd(x, w, dy, add=None, *, width, cblk=0, out_dtype=F32, also_bf16=False, tr=256, name):
    t = x.shape[0]

    def body(*refs):
        refs = list(refs)
        dxb_ref = refs.pop() if also_bf16 else None
        if add is None:
            x_ref, w_ref, dy_ref, dx_ref, dw_ref = refs
            add_ref = None
        else:
            x_ref, w_ref, dy_ref, add_ref, dx_ref, dw_ref = refs
        xv = x_ref[...].astype(F32)
        dyv = dy_ref[...].astype(F32)
        r = lax.rsqrt(jnp.mean(xv * xv, axis=-1, keepdims=True) + NORM_EPS)
        xh = xv * r
        g = dyv * w_ref[...]
        dx = r * (g - xh * jnp.mean(g * xh, axis=-1, keepdims=True))
        if add_ref is not None:
            dx = dx + add_ref[...].astype(F32)
        dx_ref[...] = dx.astype(out_dtype)
        if dxb_ref is not None:
            dxb_ref[...] = dx.astype(BF16)

        @pl.when(pl.program_id(0) == 0)
        def _():
            dw_ref[...] = jnp.zeros_like(dw_ref)

        dw_ref[...] += jnp.sum(dyv * xh, axis=0, keepdims=True)

    in_specs = [pl.BlockSpec((tr, width), lambda i: (i, cblk)), pl.BlockSpec((1, width), lambda i: (0, 0)),
                pl.BlockSpec((tr, width), lambda i: (i, 0))]
    args = [x, w, dy]
    if add is not None:
        in_specs.append(pl.BlockSpec((tr, width), lambda i: (i, 0)))
        args.append(add)
    blk = pl.BlockSpec((tr, width), lambda i: (i, 0))
    return pl.pallas_call(
        body, name=name, grid=(t // tr,), in_specs=in_specs,
        out_specs=[blk, pl.BlockSpec((1, width), lambda i: (0, 0))] + ([blk] if also_bf16 else []),
        out_shape=[jax.ShapeDtypeStruct((t, width), out_dtype), jax.ShapeDtypeStruct((1, width), F32)]
        + ([jax.ShapeDtypeStruct((t, width), BF16)] if also_bf16 else []),
        compiler_params=_cp("arbitrary"),
    )(*args)


def _shift_down(prev_halo, cur, j):
    if j == 0:
        return cur
    ext = jnp.concatenate([prev_halo, cur], axis=0)
    return pltpu.roll(ext, j, axis=0)[HALO:]


def _shift_up(cur, next_halo, j):
    if j == 0:
        return cur
    ext = jnp.concatenate([cur, next_halo], axis=0)
    return pltpu.roll(ext, ext.shape[0] - j, axis=0)[:cur.shape[0]]


def _conv_rows(prev, cur, w, b, kw):
    shifted = [cur]
    out = b + w[kw - 1:kw] * cur
    for j in range(1, kw):
        sh = _shift_down(prev, cur, j)
        shifted.append(sh)
        out = out + w[kw - 1 - j:kw - j] * sh
    return out, shifted


def _act_fwd(c, glu):
    if glu:
        half = c.shape[1] // 2
        return _silu(c[:, :half]) * c[:, half:]
    return _silu(c)


def _act_bwd(c, dout, glu):
    if glu:
        half = c.shape[1] // 2
        g, up = c[:, :half], c[:, half:]
        s = _sigmoid(g)
        gs = g * s
        return jnp.concatenate([dout * up * (s + gs * (1.0 - s)), dout * gs], axis=1)
    return dout * _dsilu(c)


def _conv_act_fwd(u, w, b, *, kw, glu, tc, coff, ncols, out_dtype, tr=256, name):
    t = u.shape[0]
    nb = ncols // tc
    oc = tc // 2 if glu else tc

    def body(u_ref, uh_ref, w_ref, b_ref, o_ref):
        prev = jnp.where(pl.program_id(0) == 0, 0.0, uh_ref[...])
        c, _ = _conv_rows(prev, u_ref[...], w_ref[...], b_ref[...], kw)
        o_ref[...] = _act_fwd(c, glu).astype(out_dtype)

    return pl.pallas_call(
        body, name=name, grid=(t // tr, nb),
        in_specs=[pl.BlockSpec((tr, tc), lambda i, j: (i, j + coff)),
                  pl.BlockSpec((HALO, tc), lambda i, j: (jnp.maximum(i * (tr // HALO) - 1, 0), j + coff)),
                  pl.BlockSpec((kw, tc), lambda i, j: (0, j)), pl.BlockSpec((1, tc), lambda i, j: (0, j))],
        out_specs=pl.BlockSpec((tr, oc), lambda i, j: (i, j)),
        out_shape=jax.ShapeDtypeStruct((t, nb * oc), out_dtype),
        compiler_params=_cp("parallel", "parallel"),
    )(u, u, w, b)


def _conv_act_bwd(u, w, b, dout, *, kw, glu, tc, coff, ncols, tr=256, name):
    t = u.shape[0]
    nb = ncols // tc
    nt = t // tr
    oc = tc // 2 if glu else tc

    def body(u_ref, up_ref, un_ref, d_ref, dn_ref, w_ref, b_ref, du_ref, dw_ref, db_ref):
        i = pl.program_id(1)
        cur, nxt, wv, bv = u_ref[...], un_ref[...], w_ref[...], b_ref[...]
        prev = jnp.where(i == 0, 0.0, up_ref[...])
        c_cur, shifted = _conv_rows(prev, cur, wv, bv, kw)
        c_nxt, _ = _conv_rows(cur[tr - HALO:], nxt, wv, bv, kw)
        d_cur = _act_bwd(c_cur, d_ref[...].astype(F32), glu)
        d_nxt = _act_bwd(c_nxt, jnp.where(i == nt - 1, 0.0, dn_ref[...].astype(F32)), glu)
        du = wv[kw - 1:kw] * d_cur
        for j in range(1, kw):
            du = du + wv[kw - 1 - j:kw - j] * _shift_up(d_cur, d_nxt, j)
        du_ref[...] = du.astype(BF16)

        @pl.when(i == 0)
        def _():
            dw_ref[...] = jnp.zeros_like(dw_ref)
            db_ref[...] = jnp.zeros_like(db_ref)

        db_ref[...] += jnp.sum(d_cur, axis=0, keepdims=True)
        dw_ref[...] += jnp.concatenate(
            [jnp.sum(d_cur * shifted[kw - 1 - k], axis=0, keepdims=True) for k in range(kw)], axis=0)

    nh = tr // HALO
    return pl.pallas_call(
        body, name=name, grid=(nb, nt),
        in_specs=[pl.BlockSpec((tr, tc), lambda j, i: (i, j + coff)),
                  pl.BlockSpec((HALO, tc), lambda j, i: (jnp.maximum(i * nh - 1, 0), j + coff)),
                  pl.BlockSpec((HALO, tc), lambda j, i: (jnp.minimum((i + 1) * nh, t // HALO - 1), j + coff)),
                  pl.BlockSpec((tr, oc), lambda j, i: (i, j)),
                  pl.BlockSpec((HALO, oc), lambda j, i: (jnp.minimum((i + 1) * nh, t // HALO - 1), j)),
                  pl.BlockSpec((kw, tc), lambda j, i: (0, j)), pl.BlockSpec((1, tc), lambda j, i: (0, j))],
        out_specs=[pl.BlockSpec((tr, tc), lambda j, i: (i, j)), pl.BlockSpec((kw, tc), lambda j, i: (0, j)),
                   pl.BlockSpec((1, tc), lambda j, i: (0, j))],
        out_shape=[jax.ShapeDtypeStruct((t, ncols), BF16), jax.ShapeDtypeStruct((kw, ncols), F32),
                   jax.ShapeDtypeStruct((1, ncols), F32)],
        compiler_params=_cp("parallel", "arbitrary"),
    )(u, u, u, dout, dout, w, b)


def _ple_loss(x2, gl, pe, pw, fw, target, *, tr=256, name):
    t, d = x2.shape

    def body(x_ref, gl_ref, pe_ref, pw_ref, fw_ref, t_ref, l_ref, dx_ref, dfw_ref, dgl_ref, db_ref, dpe_ref, dpw_ref):
        pv, pwv, wv = pe_ref[...], pw_ref[...], fw_ref[...]
        gate = _sigmoid(gl_ref[...])
        rp = lax.rsqrt(jnp.mean(pv * pv, axis=-1, keepdims=True) + NORM_EPS)
        ph = pv * rp
        e = ph * pwv
        x3 = x_ref[...] + gate * e
        r = lax.rsqrt(jnp.mean(x3 * x3, axis=-1, keepdims=True) + NORM_EPS)
        xh = x3 * r
        err = xh * wv - t_ref[...]
        dy = err * (1.0 / d)
        g = dy * wv
        dx = r * (g - xh * jnp.mean(g * xh, axis=-1, keepdims=True))
        dx_ref[...] = dx
        dgl = dx * e * gate * (1.0 - gate)
        de = dx * gate
        gg = de * pwv
        dgl_ref[...] = dgl.astype(BF16)
        dpe_ref[...] = (rp * (gg - ph * jnp.mean(gg * ph, axis=-1, keepdims=True))).astype(BF16)

        @pl.when(pl.program_id(0) == 0)
        def _():
            for ref in (l_ref, dfw_ref, db_ref, dpw_ref):
                ref[...] = jnp.zeros_like(ref)

        l_ref[...] += 0.5 * jnp.sum(jnp.mean(err * err, axis=-1, keepdims=True), axis=0, keepdims=True)
        dfw_ref[...] += jnp.sum(dy * xh, axis=0, keepdims=True)
        db_ref[...] += jnp.sum(dgl, axis=0, keepdims=True)
        dpw_ref[...] += jnp.sum(de * ph, axis=0, keepdims=True)

    blk = pl.BlockSpec((tr, d), lambda i: (i, 0))
    row = pl.BlockSpec((1, d), lambda i: (0, 0))
    rowf = jax.ShapeDtypeStruct((1, d), F32)
    return pl.pallas_call(
        body, name=name, grid=(t // tr,), in_specs=[blk, blk, blk, row, row, blk],
        out_specs=[pl.BlockSpec((1, 1), lambda i: (0, 0)), blk, row, blk, row, blk, row],
        out_shape=[jax.ShapeDtypeStruct((1, 1), F32), jax.ShapeDtypeStruct((t, d), F32), rowf,
                   jax.ShapeDtypeStruct((t, d), BF16), rowf, jax.ShapeDtypeStruct((t, d), BF16), rowf],
        compiler_params=_cp("arbitrary"),
    )(x2, gl, pe, pw, fw, target)


def _rope(blk, tab_ref):
    return blk * tab_ref[0] + pltpu.roll(blk, 96, axis=1) * tab_ref[1] + pltpu.roll(blk, 32, axis=1) * tab_ref[2]


def _unrope(g, tab_ref):
    return g * tab_ref[0] + pltpu.roll(g * tab_ref[1], 32, axis=1) + pltpu.roll(g * tab_ref[2], 96, axis=1)


def _mla_prep(q, kv, proj, tabs, *, tr=512, name):
    t = q.shape[0]

    def body(q_ref, kv_ref, kr_ref, tab_ref, qo_ref, ko_ref, vo_ref, vt_ref):
        qv, kvv = q_ref[...], kv_ref[...]
        qo_ref[0, :, :MLA_NOPE] = qv[:, :MLA_NOPE].astype(BF16)
        qo_ref[0, :, MLA_NOPE:] = _rope(qv[:, MLA_NOPE:], tab_ref).astype(BF16)
        ko_ref[0, :, :MLA_NOPE] = kvv[:, :MLA_NOPE].astype(BF16)
        ko_ref[0, :, MLA_NOPE:] = _rope(kr_ref[...], tab_ref).astype(BF16)
        vo_ref[0] = kvv[:, MLA_NOPE:].astype(BF16)
        for blk in range(tr // ATT_BLK):
            vt_ref[0, blk] = kvv[blk * ATT_BLK:(blk + 1) * ATT_BLK, MLA_NOPE:].T.astype(BF16)

    return pl.pallas_call(
        body, name=name, grid=(t // tr, MLA_HEADS),
        in_specs=[pl.BlockSpec((tr, MLA_QK_PAD), lambda i, h: (i, h)),
                  pl.BlockSpec((tr, MLA_NOPE + MLA_V), lambda i, h: (i, h)),
                  pl.BlockSpec((tr, LANES), lambda i, h: (i, OFF_KR // LANES)),
                  pl.BlockSpec((3, tr, LANES), lambda i, h: (0, i, 0))],
        out_specs=[pl.BlockSpec((1, tr, MLA_QK_PAD), lambda i, h: (h, i, 0)),
                   pl.BlockSpec((1, tr, MLA_QK_PAD), lambda i, h: (h, i, 0)),
                   pl.BlockSpec((1, tr, MLA_V), lambda i, h: (h, i, 0)),
                   pl.BlockSpec((1, tr // ATT_BLK, MLA_V, ATT_BLK), lambda i, h: (h, i, 0, 0))],
        out_shape=[jax.ShapeDtypeStruct((MLA_HEADS, t, MLA_QK_PAD), BF16),
                   jax.ShapeDtypeStruct((MLA_HEADS, t, MLA_QK_PAD), BF16),
                   jax.ShapeDtypeStruct((MLA_HEADS, t, MLA_V), BF16),
                   jax.ShapeDtypeStruct((MLA_HEADS, t // ATT_BLK, MLA_V, ATT_BLK), BF16)],
        compiler_params=_cp("parallel", "parallel"),
    )(q, kv, proj, tabs)


def _mla_unprep(dq3, dk3, dv3, tabs, *, tr=256, name):
    t = dq3.shape[1]

    def body(dq_ref, dk_ref, dv_ref, tab_ref, qo_ref, kvo_ref, kro_ref):
        kr = jnp.zeros((tr, LANES), F32)
        for h in range(MLA_HEADS):
            c0 = h * MLA_QK_PAD
            qo_ref[:, c0:c0 + MLA_NOPE] = dq_ref[h, :, :MLA_NOPE].astype(BF16)
            qo_ref[:, c0 + MLA_NOPE:c0 + MLA_QK_PAD] = _unrope(dq_ref[h, :, MLA_NOPE:], tab_ref).astype(BF16)
            kvo_ref[:, c0:c0 + MLA_NOPE] = dk_ref[h, :, :MLA_NOPE].astype(BF16)
            kvo_ref[:, c0 + MLA_NOPE:c0 + MLA_QK_PAD] = dv_ref[h].astype(BF16)
            kr = kr + dk_ref[h, :, MLA_NOPE:]
        kro_ref[...] = _unrope(kr, tab_ref).astype(BF16)

    return pl.pallas_call(
        body, name=name, grid=(t // tr,),
        in_specs=[pl.BlockSpec((MLA_HEADS, tr, MLA_QK_PAD), lambda i: (0, i, 0)),
                  pl.BlockSpec((MLA_HEADS, tr, MLA_QK_PAD), lambda i: (0, i, 0)),
                  pl.BlockSpec((MLA_HEADS, tr, MLA_V), lambda i: (0, i, 0)),
                  pl.BlockSpec((3, tr, LANES), lambda i: (0, i, 0))],
        out_specs=[pl.BlockSpec((tr, MLA_HEADS * MLA_QK_PAD), lambda i: (i, 0)),
                   pl.BlockSpec((tr, MLA_HEADS * MLA_QK_PAD), lambda i: (i, 0)),
                   pl.BlockSpec((tr, LANES), lambda i: (i, 0))],
        out_shape=[jax.ShapeDtypeStruct((t, MLA_HEADS * MLA_QK_PAD), BF16),
                   jax.ShapeDtypeStruct((t, MLA_HEADS * MLA_QK_PAD), BF16),
                   jax.ShapeDtypeStruct((t, LANES), BF16)],
        compiler_params=_cp("parallel"),
    )(dq3, dk3, dv3, tabs)


ATT_BLK = 512
ATT_SCALE = 1.0 / math.sqrt(MLA_NOPE + MLA_ROPE)
_NT = (((1,), (1,)), ((), ()))
_TN = (((0,), (0,)), ((), ()))


def _att_scores_t(k, q, diagonal):
    s = lax.dot_general(k, q, _NT, preferred_element_type=F32) * ATT_SCALE
    if not diagonal:
        return s
    key = lax.broadcasted_iota(jnp.int32, s.shape, 0)
    query = lax.broadcasted_iota(jnp.int32, s.shape, 1)
    return jnp.where((key >> 6) <= (query >> 6), s, NEG)


def _att_rows(i):
    return pl.ds(pl.multiple_of(i * ATT_BLK, ATT_BLK), ATT_BLK)


ATT_HEADS = 2


def _attn_fwd(q3, k3, vt4, *, name):
    t = q3.shape[1]
    nq = t // ATT_BLK

    def body(q_ref, k_ref, vt_ref, o_ref, lse_ref):
        qi = pl.program_id(1)
        qs = [q_ref[hh] for hh in range(ATT_HEADS)]

        def step(j, carry, diagonal=False):
            out = []
            for hh, (m, l, acc) in enumerate(carry):
                s = _att_scores_t(k_ref[hh, _att_rows(j), :], qs[hh], diagonal)
                m_new = jnp.maximum(m, jnp.max(s, axis=0, keepdims=True))
                p = jnp.exp(s - m_new)
                alpha = jnp.exp(m - m_new)
                l = alpha * l + jnp.sum(p, axis=0, keepdims=True)
                acc = alpha * acc + jnp.dot(vt_ref[hh, j], p.astype(BF16), preferred_element_type=F32)
                out.append((m_new, l, acc))
            return tuple(out)

        init = tuple((jnp.full((1, ATT_BLK), NEG, F32), jnp.zeros((1, ATT_BLK), F32),
                      jnp.zeros((MLA_V, ATT_BLK), F32)) for _ in range(ATT_HEADS))
        done = step(qi, lax.fori_loop(0, qi, step, init), diagonal=True)
        for hh, (m, l, acc) in enumerate(done):
            o_ref[:, hh * MLA_V:(hh + 1) * MLA_V] = (acc / l).T
            lse_ref[hh, 0] = m + jnp.log(l)

    return pl.pallas_call(
        body, name=name, grid=(MLA_HEADS // ATT_HEADS, nq),
        in_specs=[pl.BlockSpec((ATT_HEADS, ATT_BLK, MLA_QK_PAD), lambda h, i: (h, i, 0)),
                  pl.BlockSpec((ATT_HEADS, t, MLA_QK_PAD), lambda h, i: (h, 0, 0)),
                  pl.BlockSpec((ATT_HEADS, nq, MLA_V, ATT_BLK), lambda h, i: (h, 0, 0, 0))],
        out_specs=[pl.BlockSpec((ATT_BLK, ATT_HEADS * MLA_V), lambda h, i: (i, h)),
                   pl.BlockSpec((ATT_HEADS, 1, 1, ATT_BLK), lambda h, i: (h, i, 0, 0))],
        out_shape=[jax.ShapeDtypeStruct((t, MLA_HEADS * MLA_V), F32),
                   jax.ShapeDtypeStruct((MLA_HEADS, nq, 1, ATT_BLK), F32)],
        compiler_params=_cp("parallel", "parallel"),
    )(q3, k3, vt4)


def _attn_bwd(q3, k3, v3, o, dcat, lse, *, name):
    t = q3.shape[1]
    nq = t // ATT_BLK
    wide = ATT_HEADS * MLA_V

    def body(q_ref, k_ref, v_ref, o_ref, do_ref, lse_ref, dq_ref, dk_ref, dv_ref, delta_ref):
        kj = pl.program_id(1)

        @pl.when(kj == 0)
        def _():
            dq_ref[...] = jnp.zeros_like(dq_ref)
            ones = jnp.ones((HALO, MLA_V), F32)
            for i in range(nq):
                rows = pl.ds(i * ATT_BLK, ATT_BLK)
                prod = o_ref[rows, :] * do_ref[rows, :]
                for hh in range(ATT_HEADS):
                    delta_ref[hh, i] = lax.dot_general(ones, prod[:, hh * MLA_V:(hh + 1) * MLA_V], _NT, precision=HI,
                                                       preferred_element_type=F32)

        def step(i, carry, diagonal=False):
            rows = _att_rows(i)
            out = []
            for hh, (dk, dv) in enumerate(carry):
                k, v = k_ref[hh], v_ref[hh]
                q = q_ref[hh, rows, :]
                dob = do_ref[rows, hh * MLA_V:(hh + 1) * MLA_V].astype(BF16)
                p = jnp.exp(_att_scores_t(k, q, diagonal) - lse_ref[hh, i])
                dv = dv + jnp.dot(p.astype(BF16), dob, preferred_element_type=F32)
                dp = lax.dot_general(v, dob, _NT, preferred_element_type=F32)
                ds = (p * (dp - delta_ref[hh, i, 0:1, :]) * ATT_SCALE).astype(BF16)
                dk = dk + jnp.dot(ds, q, preferred_element_type=F32)
                dq_ref[hh, rows, :] += lax.dot_general(ds, k, _TN, preferred_element_type=F32)
                out.append((dk, dv))
            return tuple(out)

        init = tuple((jnp.zeros((ATT_BLK, MLA_QK_PAD), F32), jnp.zeros((ATT_BLK, MLA_V), F32))
                     for _ in range(ATT_HEADS))
        done = lax.fori_loop(kj + 1, nq, step, step(kj, init, diagonal=True))
        for hh, (dk, dv) in enumerate(done):
            dk_ref[hh] = dk
            dv_ref[hh] = dv

    return pl.pallas_call(
        body, name=name, grid=(MLA_HEADS // ATT_HEADS, nq),
        in_specs=[pl.BlockSpec((ATT_HEADS, t, MLA_QK_PAD), lambda h, j: (h, 0, 0)),
                  pl.BlockSpec((ATT_HEADS, ATT_BLK, MLA_QK_PAD), lambda h, j: (h, j, 0)),
                  pl.BlockSpec((ATT_HEADS, ATT_BLK, MLA_V), lambda h, j: (h, j, 0)),
                  pl.BlockSpec((t, wide), lambda h, j: (0, h)),
                  pl.BlockSpec((t, wide), lambda h, j: (0, MLA_HEADS // ATT_HEADS + h)),
                  pl.BlockSpec((ATT_HEADS, nq, 1, ATT_BLK), lambda h, j: (h, 0, 0, 0))],
        out_specs=[pl.BlockSpec((ATT_HEADS, t, MLA_QK_PAD), lambda h, j: (h, 0, 0)),
                   pl.BlockSpec((ATT_HEADS, ATT_BLK, MLA_QK_PAD), lambda h, j: (h, j, 0)),
                   pl.BlockSpec((ATT_HEADS, ATT_BLK, MLA_V), lambda h, j: (h, j, 0))],
        out_shape=[jax.ShapeDtypeStruct((MLA_HEADS, t, MLA_QK_PAD), F32),
                   jax.ShapeDtypeStruct((MLA_HEADS, t, MLA_QK_PAD), F32),
                   jax.ShapeDtypeStruct((MLA_HEADS, t, MLA_V), F32)],
        scratch_shapes=[pltpu.VMEM((ATT_HEADS, nq, HALO, ATT_BLK), F32)],
        compiler_params=_cp("parallel", "arbitrary"),
    )(q3, k3, v3, o, dcat, lse)


def _ssd_prep(proj, bias128, alog128, *, name):
    t = proj.shape[0]
    nc = t // CHUNK

    def body(raw_ref, b_ref, al_ref, dt_ref, cs_ref, a_ref):
        xv = raw_ref[...] + b_ref[...]
        dt = jnp.maximum(xv, 0.0) + jnp.log(1.0 + jnp.exp(-jnp.abs(xv)))
        a = -jnp.exp(al_ref[...])
        adt = (dt * a).reshape(nc, CHUNK, LANES)
        li = lax.broadcasted_iota(jnp.int32, (nc, CHUNK, CHUNK), 1)
        si = lax.broadcasted_iota(jnp.int32, (nc, CHUNK, CHUNK), 2)
        tril = jnp.where(si <= li, 1.0, 0.0).astype(F32)
        cs = lax.dot_general(tril, adt, (((2,), (1,)), ((0,), (0,))), precision=HI, preferred_element_type=F32)
        dt_ref[...] = dt
        cs_ref[...] = cs.reshape(t, LANES)
        a_ref[...] = a

    blk = pl.BlockSpec((t, LANES), lambda i: (0, 0))
    row = pl.BlockSpec((1, LANES), lambda i: (0, 0))
    return pl.pallas_call(
        body, name=name, grid=(1,),
        in_specs=[pl.BlockSpec((t, LANES), lambda i: (0, OFF_DT // LANES)), row, row],
        out_specs=[blk, blk, row],
        out_shape=[jax.ShapeDtypeStruct((t, LANES), F32), jax.ShapeDtypeStruct((t, LANES), F32),
                   jax.ShapeDtypeStruct((1, LANES), F32)],
        compiler_params=_cp("arbitrary"),
    )(proj, bias128, alog128)


def _ssd_prep_bwd(ddt128, dadt128, proj, bias128, dt128, a128, dd_h, *, name):
    t = proj.shape[0]

    def body(ddt_ref, dadt_ref, raw_ref, b_ref, dt_ref, a_ref, dd_ref, draw_ref, db_ref, dal_ref, dds_ref):
        draw = ddt_ref[...] * _sigmoid(raw_ref[...] + b_ref[...])
        draw_ref[...] = draw.astype(BF16)
        db_ref[...] = jnp.sum(draw, axis=0, keepdims=True)
        dal_ref[...] = jnp.sum(dadt_ref[...] * dt_ref[...], axis=0, keepdims=True) * a_ref[...]
        dds_ref[...] = jnp.sum(dd_ref[...], axis=-1, keepdims=True)

    blk = pl.BlockSpec((t, LANES), lambda i: (0, 0))
    row = pl.BlockSpec((1, LANES), lambda i: (0, 0))
    return pl.pallas_call(
        body, name=name, grid=(1,),
        in_specs=[blk, blk, pl.BlockSpec((t, LANES), lambda i: (0, OFF_DT // LANES)), row, blk, row,
                  pl.BlockSpec((SSD_HEADS, SSD_P), lambda i: (0, 0))],
        out_specs=[blk, row, row, pl.BlockSpec((SSD_HEADS, 1), lambda i: (0, 0))],
        out_shape=[jax.ShapeDtypeStruct((t, LANES), BF16), jax.ShapeDtypeStruct((1, LANES), F32),
                   jax.ShapeDtypeStruct((1, LANES), F32), jax.ShapeDtypeStruct((SSD_HEADS, 1), F32)],
        compiler_params=_cp("arbitrary"),
    )(ddt128, dadt128, proj, bias128, dt128, a128, dd_h)


def _bdot(a, b, ca, cb, precision=None):
    return lax.dot_general(a, b, (((ca,), (cb,)), ((0,), (0,))), precision=precision, preferred_element_type=F32)


def _head_matrices():
    eye, zero = jnp.eye(SSD_P, dtype=F32), jnp.zeros((SSD_P, SSD_P), F32)
    pick = jnp.stack([jnp.concatenate([eye, zero], axis=0), jnp.concatenate([zero, eye], axis=0)])
    return pick, pick.transpose(0, 2, 1)


def _move(x, sel):
    selb = sel.astype(BF16)
    hi = x.astype(BF16)
    rest = x - hi.astype(F32)
    mid = rest.astype(BF16)
    low = (rest - mid.astype(F32)).astype(BF16)
    out = jnp.dot(hi, selb, preferred_element_type=F32)
    out = out + jnp.dot(mid, selb, preferred_element_type=F32)
    return out + jnp.dot(low, selb, preferred_element_type=F32)


def _pick_head(pair_ref, pick_ref, h):
    return _move(pair_ref[...], pick_ref[h % 2])


def _place_head(out_ref, val, place_ref, h):
    wide = _move(val, place_ref[h % 2])

    @pl.when(h % 2 == 0)
    def _():
        out_ref[...] = wide

    @pl.when(h % 2 == 1)
    def _():
        out_ref[...] += wide


def _ssd_common(x2, dt_ref, cs_ref, csr_ref, b_ref, c_ref, nc):
    x = x2.reshape(nc, CHUNK, SSD_P)
    dt = dt_ref[0].reshape(nc, CHUNK, SSD_P)
    cs = cs_ref[0].reshape(nc, CHUNK, SSD_P)
    csr = csr_ref[0]
    bm = b_ref[...].reshape(nc, CHUNK, SSD_N).astype(BF16)
    cm = c_ref[...].reshape(nc, CHUNK, SSD_N).astype(BF16)
    li = lax.broadcasted_iota(jnp.int32, (nc, CHUNK, CHUNK), 1)
    si = lax.broadcasted_iota(jnp.int32, (nc, CHUNK, CHUNK), 2)
    lmat = jnp.exp(jnp.where(si <= li, cs - csr, NEG))
    g = _bdot(cm, bm, 2, 2)
    cs_last = jnp.sum(jnp.where(li == CHUNK - 1, cs, 0.0), axis=1, keepdims=True)
    xdt = x * dt
    dec = jnp.exp(cs_last - cs)
    return x, dt, cs, bm, cm, li, si, lmat, g, cs_last, xdt, dec


def _ssd_fwd(xbc, dt_h, cs_h, cs_row, dskip_h, *, name):
    t = xbc.shape[0]
    nc = t // CHUNK
    hpg = SSD_HEADS // SSD_GROUPS
    pick, place = _head_matrices()

    def body(xs_ref, dt_ref, cs_ref, csr_ref, b_ref, c_ref, dk_ref, pick_ref, place_ref, y_ref, st_ref, sc_ref, cd_ref):
        h = pl.program_id(0)
        x, dt, cs, bm, cm, li, si, lmat, g, cs_last, xdt, dec = _ssd_common(_pick_head(xs_ref, pick_ref, h), dt_ref,
                                                                           cs_ref, csr_ref, b_ref, c_ref, nc)
        yd = _bdot((g * lmat).astype(BF16), xdt.astype(BF16), 2, 1)
        sc_ref[...] = _bdot(bm, (dec * xdt).astype(BF16), 1, 1)
        cd_ref[...] = jnp.exp(cs_last)

        def step(c, s):
            st_ref[0, c] = s
            return s * cd_ref[c] + sc_ref[c]

        lax.fori_loop(0, nc, step, jnp.zeros((SSD_N, SSD_P), F32))
        yo = _bdot(cm, st_ref[0].astype(BF16), 2, 1) * jnp.exp(cs)
        _place_head(y_ref, (yd + yo + dk_ref[0] * x).reshape(t, SSD_P), place_ref, h)

    head = pl.BlockSpec((1, t, SSD_P), lambda h: (h, 0, 0))
    pair = pl.BlockSpec((t, 2 * SSD_P), lambda h: (0, h // 2))
    nxb = D_SSM // SSD_N
    return pl.pallas_call(
        body, name=name, grid=(SSD_HEADS,),
        in_specs=[pair, head, head, pl.BlockSpec((1, nc, 1, CHUNK), lambda h: (h, 0, 0, 0)),
                  pl.BlockSpec((t, SSD_N), lambda h: (0, nxb + h // hpg)),
                  pl.BlockSpec((t, SSD_N), lambda h: (0, nxb + SSD_GROUPS + h // hpg)),
                  pl.BlockSpec((1, 1, SSD_P), lambda h: (h, 0, 0)),
                  pl.BlockSpec((2, 2 * SSD_P, SSD_P), lambda h: (0, 0, 0)),
                  pl.BlockSpec((2, SSD_P, 2 * SSD_P), lambda h: (0, 0, 0))],
        out_specs=[pair, pl.BlockSpec((1, nc, SSD_N, SSD_P), lambda h: (h, 0, 0, 0))],
        out_shape=[jax.ShapeDtypeStruct((t, D_SSM), F32),
                   jax.ShapeDtypeStruct((SSD_HEADS, nc, SSD_N, SSD_P), F32)],
        scratch_shapes=[pltpu.VMEM((nc, SSD_N, SSD_P), F32), pltpu.VMEM((nc, 1, SSD_P), F32)],
        compiler_params=_cp("arbitrary"),
    )(xbc, dt_h, cs_h, cs_row, xbc, xbc, dskip_h, pick, place)


def _ssd_bwd(xbc, dt_h, cs_h, cs_row, dskip_h, a_h, states, dy, *, name):
    t = xbc.shape[0]
    nc = t // CHUNK
    hpg = SSD_HEADS // SSD_GROUPS
    pick, place = _head_matrices()

    def body(xs_ref, dt_ref, cs_ref, csr_ref, b_ref, c_ref, dk_ref, a_ref, st_ref, dy_ref, pick_ref, place_ref,
             dxs_ref, ddt_ref, dadt_ref, db_ref, dc_ref, dd_ref, dsl_ref, dsc_ref, cd_ref):
        h = pl.program_id(0) * hpg + pl.program_id(1)
        x, dt, cs, bm, cm, li, si, lmat, g, cs_last, xdt, dec = _ssd_common(_pick_head(xs_ref, pick_ref, h), dt_ref,
                                                                           cs_ref, csr_ref, b_ref, c_ref, nc)
        dy = _pick_head(dy_ref, pick_ref, h).reshape(nc, CHUNK, SSD_P)
        dyb = dy.astype(BF16)
        xdtb = xdt.astype(BF16)
        sprev = st_ref[0]
        sprevb = sprev.astype(BF16)
        cdec = jnp.exp(cs_last)
        ecs = jnp.exp(cs)
        dw = (ecs * dy).astype(BF16)
        wmat = _bdot(cm, sprevb, 2, 1)
        dcs = jnp.sum(dy * ecs * wmat, axis=2, keepdims=True)
        dcm = _bdot(dw, sprevb, 2, 2)
        dsl_ref[...] = _bdot(cm, dw, 1, 1)
        cd_ref[...] = cdec

        def step(k, ds):
            c = nc - 1 - k
            dsc_ref[c] = ds
            return ds * cd_ref[c] + dsl_ref[c]

        lax.fori_loop(0, nc, step, jnp.zeros((SSD_N, SSD_P), F32))
        dsc = dsc_ref[...]
        dscb = dsc.astype(BF16)
        d_last = jnp.sum(jnp.sum(dsc * sprev, axis=1, keepdims=True) * cdec, axis=2, keepdims=True)
        z = dec * xdt
        dbm = _bdot(z.astype(BF16), dscb, 2, 2)
        dz = _bdot(bm, dscb, 2, 1)
        dxdt = dec * dz
        t2 = jnp.sum(dz * z, axis=2, keepdims=True)
        dcs = dcs - t2
        d_last = d_last + jnp.sum(t2, axis=1, keepdims=True)
        m = g * lmat
        mb = m.astype(BF16)
        dm = _bdot(dyb, xdtb, 2, 2)
        dxdt = dxdt + _bdot(mb, dyb, 1, 1)
        dseg = dm * m
        dcs = dcs + jnp.sum(dseg, axis=2, keepdims=True)
        ones = jnp.ones((nc, CHUNK, SSD_P), F32)
        dcs = dcs - _bdot(dseg, ones, 1, 1, precision=HI)
        dg = (dm * lmat).astype(BF16)
        dcm = dcm + _bdot(dg, bm, 2, 1)
        dbm = dbm + _bdot(dg, cm, 1, 1)
        dcs = dcs + jnp.where(li[:, :, :SSD_P] == CHUNK - 1, d_last, 0.0)
        triu = jnp.where(li <= si, 1.0, 0.0).astype(F32)
        dadt = _bdot(triu, dcs, 2, 1, precision=HI)
        dk = dk_ref[0]
        _place_head(dxs_ref, (dxdt * dt + dk * dy).reshape(t, SSD_P), place_ref, h)
        ddt = jnp.sum(dxdt * x, axis=2, keepdims=True) + dadt * a_ref[0]
        mine = lax.broadcasted_iota(jnp.int32, (t, LANES), 1) == h

        @pl.when(h == 0)
        def _():
            ddt_ref[...] = jnp.zeros_like(ddt_ref)
            dadt_ref[...] = jnp.zeros_like(dadt_ref)

        ddt_ref[...] += jnp.where(mine, jnp.max(ddt, axis=2, keepdims=True).reshape(t, 1), 0.0)
        dadt_ref[...] += jnp.where(mine, jnp.max(dadt, axis=2, keepdims=True).reshape(t, 1), 0.0)
        dd_ref[0] = jnp.sum(jnp.sum(dy * x, axis=1, keepdims=True), axis=0)

        @pl.when(pl.program_id(1) == 0)
        def _():
            db_ref[...] = jnp.zeros_like(db_ref)
            dc_ref[...] = jnp.zeros_like(dc_ref)

        db_ref[...] += dbm.reshape(t, SSD_N)
        dc_ref[...] += dcm.reshape(t, SSD_N)

    head = pl.BlockSpec((1, t, SSD_P), lambda gi, hi: (gi * hpg + hi, 0, 0))
    pair = pl.BlockSpec((t, 2 * SSD_P), lambda gi, hi: (0, (gi * hpg + hi) // 2))
    grp = pl.BlockSpec((t, SSD_N), lambda gi, hi: (0, gi))
    lane = pl.BlockSpec((1, 1, SSD_P), lambda gi, hi: (gi * hpg + hi, 0, 0))
    rows = pl.BlockSpec((t, LANES), lambda gi, hi: (0, 0))
    nxb = D_SSM // SSD_N
    dxs, ddt, dadt, db, dc, dd = pl.pallas_call(
        body, name=name, grid=(SSD_GROUPS, hpg),
        in_specs=[pair, head, head, pl.BlockSpec((1, nc, 1, CHUNK), lambda gi, hi: (gi * hpg + hi, 0, 0, 0)),
                  pl.BlockSpec((t, SSD_N), lambda gi, hi: (0, nxb + gi)),
                  pl.BlockSpec((t, SSD_N), lambda gi, hi: (0, nxb + SSD_GROUPS + gi)), lane, lane,
                  pl.BlockSpec((1, nc, SSD_N, SSD_P), lambda gi, hi: (gi * hpg + hi, 0, 0, 0)), pair,
                  pl.BlockSpec((2, 2 * SSD_P, SSD_P), lambda gi, hi: (0, 0, 0)),
                  pl.BlockSpec((2, SSD_P, 2 * SSD_P), lambda gi, hi: (0, 0, 0))],
        out_specs=[pair, rows, rows, grp, grp, lane],
        out_shape=[jax.ShapeDtypeStruct((t, D_SSM), F32)] + [jax.ShapeDtypeStruct((t, LANES), F32)] * 2
        + [jax.ShapeDtypeStruct((t, SSD_GROUPS * SSD_N), F32)] * 2
        + [jax.ShapeDtypeStruct((SSD_HEADS, 1, SSD_P), F32)],
        scratch_shapes=[pltpu.VMEM((nc, SSD_N, SSD_P), F32), pltpu.VMEM((nc, SSD_N, SSD_P), F32),
                        pltpu.VMEM((nc, 1, SSD_P), F32)],
        compiler_params=_cp("arbitrary", "arbitrary"),
    )(xbc, dt_h, cs_h, cs_row, xbc, xbc, dskip_h, a_h, states, dy, pick, place)
    return jnp.concatenate([dxs, db, dc], axis=1), ddt, dadt, dd


def _ssd_gate_fwd(y, proj, w, *, tr=256, name):
    t = y.shape[0]
    gw = D_SSM // SSD_GROUPS

    def body(y_ref, z_ref, w_ref, o_ref):
        v = y_ref[...] * _silu(z_ref[...])
        for gi in range(SSD_GROUPS):
            vg = v[:, gi * gw:(gi + 1) * gw]
            r = lax.rsqrt(jnp.mean(vg * vg, axis=-1, keepdims=True) + NORM_EPS)
            o_ref[:, gi * gw:(gi + 1) * gw] = (vg * r * w_ref[:, gi * gw:(gi + 1) * gw]).astype(BF16)

    blk = pl.BlockSpec((tr, D_SSM), lambda i: (i, 0))
    return pl.pallas_call(
        body, name=name, grid=(t // tr,), in_specs=[blk, blk, pl.BlockSpec((1, D_SSM), lambda i: (0, 0))],
        out_specs=blk, out_shape=jax.ShapeDtypeStruct((t, D_SSM), BF16), compiler_params=_cp("parallel"),
    )(y, proj, w)


def _ssd_gate_bwd(y, proj, w, dcat, *, tr=256, name):
    t = y.shape[0]
    gw = D_SSM // SSD_GROUPS

    def body(y_ref, z_ref, w_ref, d_ref, dy_ref, dz_ref, dw_ref):
        yv, zv, dv = y_ref[...], z_ref[...], d_ref[...].astype(F32)
        sz = _silu(zv)
        v = yv * sz

        @pl.when(pl.program_id(0) == 0)
        def _():
            dw_ref[...] = jnp.zeros_like(dw_ref)

        for gi in range(SSD_GROUPS):
            sl = slice(gi * gw, (gi + 1) * gw)
            vg, dg = v[:, sl], dv[:, sl]
            r = lax.rsqrt(jnp.mean(vg * vg, axis=-1, keepdims=True) + NORM_EPS)
            vh = vg * r
            gg = dg * w_ref[:, sl]
            dvg = r * (gg - vh * jnp.mean(gg * vh, axis=-1, keepdims=True))
            dy_ref[:, sl] = dvg * sz[:, sl]
            dz_ref[:, sl] = (dvg * yv[:, sl] * _dsilu(zv[:, sl])).astype(BF16)
            dw_ref[:, sl] += jnp.sum(dg * vh, axis=0, keepdims=True)

    blk = pl.BlockSpec((tr, D_SSM), lambda i: (i, 0))
    row = pl.BlockSpec((1, D_SSM), lambda i: (0, 0))
    return pl.pallas_call(
        body, name=name, grid=(t // tr,), in_specs=[blk, blk, row, blk], out_specs=[blk, blk, row],
        out_shape=[jax.ShapeDtypeStruct((t, D_SSM), F32), jax.ShapeDtypeStruct((t, D_SSM), BF16),
                   jax.ShapeDtypeStruct((1, D_SSM), F32)],
        compiler_params=_cp("arbitrary"),
    )(y, proj, w, dcat)


def _pad_lanes(v):
    return jnp.pad(v, ((0, 0), (0, LANES - v.shape[1])))


def _per_head(v128, t):
    return jnp.broadcast_to(v128[:, :SSD_HEADS].T[:, :, None], (SSD_HEADS, t, SSD_P))


def _ssd_forward(proj, conv_w, conv_b, dt_bias, a_log, d_skip, ssd_norm_w):
    t = proj.shape[0]
    nc = t // CHUNK
    xbc = _conv_act_fwd(proj, conv_w, conv_b, kw=SSD_CONV, glu=False, tc=512, coff=OFF_XBC // 512,
                        ncols=SSD_CONV_DIM, out_dtype=F32, name="ssd_conv_fwd")
    bias128, alog128 = _pad_lanes(dt_bias), _pad_lanes(a_log)
    dt128, cs128, a128 = _ssd_prep(proj, bias128, alog128, name="ssd_prep")
    dt_h, cs_h = _per_head(dt128, t), _per_head(cs128, t)
    cs_row = cs128[:, :SSD_HEADS].T.reshape(SSD_HEADS, nc, 1, CHUNK)
    dskip_h = jnp.broadcast_to(d_skip[0][:, None, None], (SSD_HEADS, 1, SSD_P))
    a_h = jnp.broadcast_to(a128[0, :SSD_HEADS][:, None, None], (SSD_HEADS, 1, SSD_P))
    y, states = _ssd_fwd(xbc, dt_h, cs_h, cs_row, dskip_h, name="ssd_scan_fwd")
    y_ssd = _ssd_gate_fwd(y, proj, ssd_norm_w, name="ssd_gate_fwd")
    saved = (proj, conv_w, conv_b, ssd_norm_w, bias128, dt128, a128, dt_h, cs_h, cs_row, xbc, dskip_h, a_h, states, y)
    return y_ssd, saved


def _ssd_backward(saved, dcat):
    proj, conv_w, conv_b, ssd_norm_w, bias128, dt128, a128, dt_h, cs_h, cs_row, xbc, dskip_h, a_h, states, y = saved
    dy, dz, d_norm_w = _ssd_gate_bwd(y, proj, ssd_norm_w, dcat, name="ssd_gate_bwd")
    dxc, ddt128, dadt128, dd_h = _ssd_bwd(xbc, dt_h, cs_h, cs_row, dskip_h, a_h, states, dy, name="ssd_scan_bwd")
    dxbc, d_conv_w, d_conv_b = _conv_act_bwd(proj, conv_w, conv_b, dxc, kw=SSD_CONV, glu=False, tc=512,
                                             coff=OFF_XBC // 512, ncols=SSD_CONV_DIM, name="ssd_conv_bwd")
    d_raw, d_bias, d_alog, d_dskip = _ssd_prep_bwd(ddt128, dadt128, proj, bias128, dt128, a128,
                                                   dd_h.reshape(SSD_HEADS, SSD_P), name="ssd_prep_bwd")
    return (dz, dxbc, d_raw, d_norm_w, d_conv_w, d_conv_b, d_bias[:, :SSD_HEADS], d_alog[:, :SSD_HEADS],
            d_dskip.reshape(1, SSD_HEADS))


def _rope_tables(positions):
    inv_freq = ROPE_THETA ** (-jnp.arange(0, MLA_ROPE, 2, dtype=F32) / MLA_ROPE)
    ang = positions[0].astype(F32)[:, None] * inv_freq
    cos, sin = jnp.cos(ang), jnp.sin(ang)
    z = jnp.zeros_like(cos)
    return jnp.stack([jnp.concatenate([cos, cos, z, z], axis=1), jnp.concatenate([-sin, z, z, z], axis=1),
                      jnp.concatenate([z, sin, z, z], axis=1)])


def _mla_forward(proj, tabs, q_a_norm_w, wq_pad, kv_a_norm_w, wkv):
    qn = _rmsnorm_fwd(proj, q_a_norm_w, width=MLA_Q_RANK, cblk=OFF_QA // MLA_Q_RANK, name="q_a_norm")
    q = _matmul(qn, wq_pad, name="q_b_proj")
    kvn = _rmsnorm_fwd(proj, kv_a_norm_w, width=MLA_KV_RANK, cblk=OFF_CKV // MLA_KV_RANK, name="kv_a_norm")
    kv = _matmul(kvn, wkv, name="kv_b_proj")
    q3, k3, v3, vt4 = _mla_prep(q, kv, proj, tabs, name="mla_prep")
    o, lse = _attn_fwd(q3, k3, vt4, name="attn_fwd")
    return o, (proj, tabs, q_a_norm_w, wq_pad, kv_a_norm_w, wkv, qn, kvn, q3, k3, v3, o, lse)


def _mla_backward(saved, dcat):
    proj, tabs, q_a_norm_w, wq_pad, kv_a_norm_w, wkv, qn, kvn, q3, k3, v3, o, lse = saved
    dq3, dk3, dv3 = _attn_bwd(q3, k3, v3, o, dcat, lse, name="attn_bwd")
    dq, dkv, dkr = _mla_unprep(dq3, dk3, dv3, tabs, name="mla_unprep")
    d_wq = _matmul(qn, dq, ta=True, out_dtype=BF16, name="d_w_q_b")
    dqn = _matmul(dq, wq_pad, tb=True, name="d_qn")
    dq_a, d_qnw = _rmsnorm_bwd(proj, q_a_norm_w, dqn, width=MLA_Q_RANK, cblk=OFF_QA // MLA_Q_RANK, out_dtype=BF16,
                               name="q_a_norm_bwd")
    d_wkv = _matmul(kvn, dkv, ta=True, out_dtype=BF16, name="d_w_kv_b")
    dkvn = _matmul(dkv, wkv, tb=True, name="d_kvn")
    dckv, d_kvnw = _rmsnorm_bwd(proj, kv_a_norm_w, dkvn, width=MLA_KV_RANK, cblk=OFF_CKV // MLA_KV_RANK,
                                out_dtype=BF16, name="kv_a_norm_bwd")
    return dq_a, dckv, dkr, d_wq, d_wkv, d_qnw, d_kvnw


def _pad_w_q(w):
    r = w.shape[0]
    w3 = w.reshape(r, MLA_HEADS, MLA_NOPE + MLA_ROPE)
    return jnp.pad(w3, ((0, 0), (0, 0), (0, MLA_QK_PAD - MLA_NOPE - MLA_ROPE))).reshape(r, MLA_HEADS * MLA_QK_PAD)


def _unpad_w_q(w):
    r = w.shape[0]
    return w.reshape(r, MLA_HEADS, MLA_QK_PAD)[:, :, :MLA_NOPE + MLA_ROPE].reshape(r, MLA_HEADS * (MLA_NOPE + MLA_ROPE))


W_IN_SEGMENTS = ((0, D_SSM + SSD_CONV_DIM, 0), (D_SSM + SSD_CONV_DIM, D_SSM + SSD_CONV_DIM + SSD_HEADS, OFF_DT),
                 (D_SSM + SSD_CONV_DIM + SSD_HEADS, D_IN - MLA_ROPE, OFF_QA), (D_IN - MLA_ROPE, D_IN, OFF_KR))


def _pad_w_in_shards(g):
    n = g.shape[2]
    pieces, at = [], 0
    for lo, hi, start in sorted(W_IN_SEGMENTS, key=lambda seg: seg[2]):
        if start > at:
            pieces.append(jnp.zeros((g.shape[1], start - at), g.dtype))
        for j in range(N_DEV):
            a, b = max(lo, j * n), min(hi, (j + 1) * n)
            if a < b:
                pieces.append(g[j][:, a - j * n:b - j * n])
        at = start + hi - lo
    pieces.append(jnp.zeros((g.shape[1], D_IN_PAD - at), g.dtype))
    return jnp.concatenate(pieces, axis=1)


def _unpad_w_in_shards(w):
    n = D_IN // N_DEV
    shards = []
    for j in range(N_DEV):
        pieces = []
        for lo, hi, start in W_IN_SEGMENTS:
            a, b = max(lo, j * n), min(hi, (j + 1) * n)
            if a < b:
                pieces.append(w[:, start + a - lo:start + b - lo])
        shards.append(jnp.concatenate(pieces, axis=1) if len(pieces) > 1 else pieces[0])
    return jnp.stack(shards)


WEIGHTS = ['mix_norm_w', 'w_in', 'conv_w', 'conv_b', 'dt_bias', 'a_log', 'd_skip', 'ssd_norm_w', 'q_a_norm_w', 'w_q_b',
           'kv_a_norm_w', 'w_kv_b', 'w_out', 'ffn_norm_w', 'w_ffn_up', 'ffn_conv_w', 'ffn_conv_b', 'w_ffn_down',
           'ple_norm_w', 'w_ple_gate', 'b_ple_gate', 'w_ple_proj', 'ple_post_norm_w', 'final_norm_w']
BIG = ['w_in', 'w_q_b', 'w_kv_b', 'w_out', 'w_ffn_up', 'w_ffn_down', 'w_ple_gate', 'w_ple_proj']
COL_SHARDED = ('w_in', 'w_q_b', 'w_kv_b', 'w_ffn_up', 'w_ple_proj')
CONV = ['conv_w', 'ffn_conv_w']
REPL = [n for n in WEIGHTS if n not in BIG and n not in CONV]
FFN_INV = tuple(int(i) for i in np.argsort(FFN_PERM))


def _cat_cols(g):
    return jnp.concatenate([g[j] for j in range(N_DEV)], axis=1)


def _split_cols(w):
    n = w.shape[1] // N_DEV
    return jnp.stack([w[:, j * n:(j + 1) * n] for j in range(N_DEV)])


def _interleave(v):
    r = v.shape[0]
    return v.reshape(r, N_DEV, FFN_TC)[:, jnp.array(FFN_PERM)].reshape(r, N_DEV * FFN_TC)


def _deinterleave(v):
    r = v.shape[0]
    return v.reshape(r, N_DEV, FFN_TC)[:, jnp.array(FFN_INV)].reshape(r, N_DEV * FFN_TC)


def _assemble_weights(g):
    layout = {
        'w_in': _pad_w_in_shards,
        'w_q_b': lambda v: _pad_w_q(_cat_cols(v)),
        'w_kv_b': _cat_cols,
        'w_out': lambda v: v.reshape(D_MODEL, D_MODEL),
        'w_ffn_up': lambda v: v,
        'w_ffn_down': lambda v: v.reshape(D_FF, D_MODEL),
        'w_ple_gate': lambda v: v.reshape(D_MODEL, D_MODEL),
        'w_ple_proj': _cat_cols,
        'conv_w': _cat_cols,
        'ffn_conv_w': lambda v: _interleave(_cat_cols(v)),
    }
    return {n: layout[n](v) for n, v in g.items()}


WEIGHT_GROUPS = {'a': ['w_in', 'w_q_b', 'w_kv_b', 'conv_w'], 'b': ['w_out', 'w_ffn_up', 'ffn_conv_w'],
                 'c': ['w_ffn_down', 'w_ple_gate', 'w_ple_proj']}
GRAD_GROUPS = {'p': ['w_ple_proj', 'w_ple_gate', 'w_ffn_down'], 'r': ['w_ffn_up'], 's': ['w_out'],
               't': ['w_q_b', 'w_kv_b', 'w_in']}


def _ffn_perm(j):
    return (j % 2) * (N_DEV // 2) + j // 2


def _local_step(x, p, tabs, get_w, s, target, emit, relay, settle):
    t = x.shape[0]
    s = dict(s)
    half = D_MODEL // 2
    up_cols = 2 * D_FF
    ffn_conv_b = _interleave(s['ffn_conv_b'])
    w = dict(get_w('a', None))
    h = _rmsnorm_fwd(x, s['mix_norm_w'], width=D_MODEL, name="mix_norm")
    proj = _matmul(h, w['w_in'], name="in_proj")
    y_ssd, ssd_saved = _ssd_forward(proj, w['conv_w'], s['conv_b'], s['dt_bias'], s['a_log'], s['d_skip'],
                                    s['ssd_norm_w'])
    o, mla_saved = _mla_forward(proj, tabs, s['q_a_norm_w'], w['w_q_b'], s['kv_a_norm_w'], w['w_kv_b'])
    tk_o, tn_o = _tile(half, MM_TK), _tile(D_MODEL, MM_TILE)
    w.update(get_w('b', o))
    x1 = _matmul(y_ssd, w['w_out'], add=x, mnk=(t, D_MODEL, half), name="out_proj_ssd")
    x1 = _matmul(o, w['w_out'], add=x1, mnk=(t, D_MODEL, half), name="out_proj_mla",
                 b_spec=pl.BlockSpec((tk_o, tn_o), lambda i, j, kk: (kk + half // tk_o, j)))
    hf = _rmsnorm_fwd(x1, s['ffn_norm_w'], width=D_MODEL, name="ffn_norm")
    tk_u = _tile(D_MODEL, MM_TK)
    u = _matmul(hf, w['w_ffn_up'], mnk=(t, up_cols, D_MODEL), tn=FFN_TC, name="ffn_up",
                b_spec=pl.BlockSpec((1, tk_u, FFN_TC), lambda i, j, kk: (_ffn_perm(j), kk, 0)))
    act = _conv_act_fwd(u, w['ffn_conv_w'], ffn_conv_b, kw=FFN_CONV, glu=True, tc=2 * FFN_TC, coff=0, ncols=up_cols,
                        out_dtype=BF16, name="ffn_act")
    w.update(get_w('c', act))
    x2 = _matmul(act, w['w_ffn_down'], add=x1, name="ffn_down")
    hp = _rmsnorm_fwd(x2, s['ple_norm_w'], width=D_MODEL, name="ple_norm")
    gl = _matmul(hp, w['w_ple_gate'], bias=s['b_ple_gate'], name="ple_gate")
    pe = _matmul(p, w['w_ple_proj'], name="ple_proj")
    loss, dx3, d_final, dgl, d_bgate, dpe, d_post = _ple_loss(x2, gl, pe, s['ple_post_norm_w'], s['final_norm_w'],
                                                              target, name="ple_loss")
    d_wproj = _matmul(p, dpe, ta=True, out_dtype=BF16, name="d_w_ple_proj")
    d_wgate = _matmul(hp, dgl, ta=True, out_dtype=BF16, name="d_w_ple_gate")
    dhp = _matmul(dgl, w['w_ple_gate'], tb=True, name="d_ple_normed")
    dx2, d_plenorm, dx2b = _rmsnorm_bwd(x2, s['ple_norm_w'], dhp, dx3, width=D_MODEL, also_bf16=True,
                                        name="ple_norm_bwd")
    dact = _matmul(dx2b, w['w_ffn_down'], tb=True, name="d_ffn_act")
    d_wdown = _matmul(act, dx2b, ta=True, out_dtype=BF16, name="d_w_ffn_down")
    zz = emit('p', {'w_ple_proj': _split_cols(d_wproj), 'w_ple_gate': d_wgate.reshape(N_DEV, D_MODEL // N_DEV, D_MODEL),
                    'w_ffn_down': d_wdown.reshape(N_DEV, D_FF // N_DEV, D_MODEL)})
    du, d_fconv_w, d_fconv_b = _conv_act_bwd(u, w['ffn_conv_w'], ffn_conv_b + zz, dact, kw=FFN_CONV, glu=True,
                                             tc=2 * FFN_TC, coff=0, ncols=up_cols, name="ffn_act_bwd")
    zz = zz + relay('p', du)
    tm_u = _tile(D_MODEL, MM_TILE)
    d_wup = _matmul(hf, du, ta=True, out_dtype=BF16, mnk=(D_MODEL, up_cols, t), tn=FFN_TC, name="d_w_ffn_up",
                    o_spec=pl.BlockSpec((1, tm_u, FFN_TC), lambda i, j, kk: (_ffn_perm(j), i, 0)),
                    o_shape=(N_DEV, D_MODEL, FFN_TC))
    zz = zz + emit('r', {'w_ffn_up': d_wup})
    zero_row = jnp.zeros((1, D_MODEL), F32)
    dhf = _matmul(du, w['w_ffn_up'], tb=True, mnk=(t, D_MODEL, up_cols), tm=t, tk=FFN_TC, name="d_ffn_normed",
                  bias=zero_row + zz,
                  b_spec=pl.BlockSpec((1, tn_o, FFN_TC), lambda i, j, kk: (_ffn_perm(kk), j, 0)))
    zz = zz + relay('r', dhf) + settle('p')
    dx1, d_ffnnorm, dx1b = _rmsnorm_bwd(x1, s['ffn_norm_w'] + zz, dhf, dx2, width=D_MODEL, also_bf16=True,
                                        name="ffn_norm_bwd")
    dcat = _matmul(dx1b, w['w_out'], tb=True, name="d_mixed")
    d_wout = jnp.concatenate([_matmul(y_ssd, dx1b, ta=True, out_dtype=BF16, name="d_w_out_ssd"),
                              _matmul(o, dx1b, ta=True, out_dtype=BF16, name="d_w_out_mla")], axis=0)
    zz = zz + emit('s', {'w_out': d_wout.reshape(N_DEV, D_MODEL // N_DEV, D_MODEL)})
    ssd_saved = ssd_saved[:3] + (ssd_saved[3] + zz,) + ssd_saved[4:]
    dz, dxbc, d_raw, d_ssdnorm, d_conv_w, d_conv_b, d_dtb, d_alog, d_dskip = _ssd_backward(ssd_saved, dcat)
    zz = zz + relay('s', dz)
    mla_saved = mla_saved[:-1] + (mla_saved[-1] + zz,)
    dq_a, dckv, dkr, d_wq, d_wkv, d_qnorm, d_kvnorm = _mla_backward(mla_saved, dcat)
    d_raw = (d_raw + settle('r')).astype(BF16)
    dproj = jnp.concatenate([dz, dxbc, dq_a, dckv, dkr, d_raw], axis=1)
    d_win = _matmul(h, dproj, ta=True, out_dtype=BF16, name="d_w_in")
    zz = emit('t', {'w_in': _unpad_w_in_shards(d_win), 'w_q_b': _split_cols(_unpad_w_q(d_wq)),
                    'w_kv_b': _split_cols(d_wkv)}) + settle('s')
    dh = _matmul(dproj, w['w_in'], tb=True, bias=zero_row + zz, name="d_in_normed")
    zz = relay('t', dh)
    dx, d_mixnorm = _rmsnorm_bwd(x, s['mix_norm_w'] + zz, dh, dx1, width=D_MODEL, name="mix_norm_bwd")
    conv = {'conv_w': d_conv_w, 'ffn_conv_w': _deinterleave(d_fconv_w)}
    vec = {
        'mix_norm_w': d_mixnorm, 'conv_b': d_conv_b, 'dt_bias': d_dtb, 'a_log': d_alog, 'd_skip': d_dskip,
        'ssd_norm_w': d_ssdnorm, 'q_a_norm_w': d_qnorm, 'kv_a_norm_w': d_kvnorm, 'ffn_norm_w': d_ffnnorm,
        'ffn_conv_b': _deinterleave(d_fconv_b), 'ple_norm_w': d_plenorm, 'b_ple_gate': d_bgate,
        'ple_post_norm_w': d_post, 'final_norm_w': d_final,
    }
    return loss, dx, conv, vec


MESH = pl.DeviceIdType.MESH
FLIPS = ((0, 0, 1), (1, 0, 0), (0, 1, 0), (1, 1, 0), (1, 0, 1), (0, 1, 1), (1, 1, 1))


def _exchange(items, *, gather, name):
    n = len(items)

    def body(*refs):
        ins, outs = refs[:n], refs[n:2 * n]
        send_sems, recv_sems, local_sems = refs[2 * n:]
        x, y, c = lax.axis_index("x"), lax.axis_index("y"), lax.axis_index("c")
        me = 4 * x + 2 * y + c
        peers = [(jnp.where(fx, 1 - x, x), jnp.where(fy, 1 - y, y), jnp.where(fc, 1 - c, c)) for fx, fy, fc in FLIPS]
        slot = [4 * px + 2 * py + pc for px, py, pc in peers]
        local, sends = [], []
        for wi in range(n):
            cp = pltpu.make_async_copy(ins[wi] if gather else ins[wi].at[me], outs[wi].at[me], local_sems.at[wi])
            cp.start()
            local.append(cp)
            for k, peer in enumerate(peers):
                cp = pltpu.make_async_remote_copy(
                    src_ref=ins[wi] if gather else ins[wi].at[slot[k]], dst_ref=outs[wi].at[me],
                    send_sem=send_sems.at[k, wi], recv_sem=recv_sems.at[k, wi], device_id=peer, device_id_type=MESH)
                cp.start()
                sends.append(cp)
        for wi in range(n):
            for k, peer in enumerate(peers):
                pltpu.make_async_remote_copy(
                    src_ref=outs[wi].at[slot[k]], dst_ref=outs[wi].at[slot[k]], send_sem=send_sems.at[k, wi],
                    recv_sem=recv_sems.at[k, wi], device_id=peer, device_id_type=MESH).wait_recv()
        for cp in sends:
            cp.wait_send()
        for cp in local:
            cp.wait()

    hbm = pl.BlockSpec(memory_space=pltpu.HBM)
    out_shape = [jax.ShapeDtypeStruct(((N_DEV,) + v.shape) if gather else v.shape, v.dtype) for v in items]
    return pl.pallas_call(
        body, name=name, in_specs=[hbm] * n, out_specs=[hbm] * n, out_shape=out_shape,
        scratch_shapes=[pltpu.SemaphoreType.DMA((len(FLIPS), n)), pltpu.SemaphoreType.DMA((len(FLIPS), n)),
                        pltpu.SemaphoreType.DMA((n,))],
    )(*items)


HBM_SPEC = pl.BlockSpec(memory_space=pltpu.HBM)
SEM_SPEC = pl.BlockSpec(memory_space=pltpu.SEMAPHORE)
EFFECT = pltpu.SideEffectType.DATAFLOW_SIDE_EFFECTING


def _split_start(bufs, ncopies, plan, *, name):
    nb = len(bufs)

    def body(*refs):
        send_sems, recv_sems, token = refs[nb], refs[nb + 1], refs[2 * nb + 2]
        for i, (src, dst, peer, _) in enumerate(plan(refs[:nb])):
            pltpu.make_async_remote_copy(src_ref=src, dst_ref=dst, send_sem=send_sems.at[i], recv_sem=recv_sems.at[i],
                                         device_id=peer, device_id_type=MESH).start()
        token[...] = jnp.zeros_like(token)

    res = pl.pallas_call(
        body, name=name, in_specs=[HBM_SPEC] * nb,
        out_specs=[SEM_SPEC, SEM_SPEC] + [HBM_SPEC] * nb + [pl.BlockSpec(memory_space=pltpu.VMEM)],
        out_shape=[pltpu.SemaphoreType.DMA((ncopies,)), pltpu.SemaphoreType.DMA((ncopies,))]
        + [pltpu.HBM(v.shape, v.dtype) for v in bufs] + [jax.ShapeDtypeStruct((HALO, LANES), F32)],
        input_output_aliases={i: 2 + i for i in range(nb)},
        compiler_params=pltpu.CompilerParams(has_side_effects=EFFECT),
    )(*[pltpu.with_memory_space_constraint(v, pltpu.HBM) for v in bufs])
    return (res[0], res[1], list(res[2:2 + nb])), res[2 + nb]


def _split_wait(started, after, plan, local_plan, *, name):
    send_sems, recv_sems, bufs = started
    nb = len(bufs)
    nlocal = len(local_plan(bufs))

    def body(*refs):
        send_sems, recv_sems = refs[nb], refs[nb + 1]
        local_sems = refs[2 * nb + 3]
        local = []
        for j, (src, dst) in enumerate(local_plan(refs[:nb])):
            cp = pltpu.make_async_copy(src, dst, local_sems.at[j])
            cp.start()
            local.append(cp)
        for i, (src, _, peer, incoming) in enumerate(plan(refs[:nb])):
            cp = pltpu.make_async_remote_copy(src_ref=src, dst_ref=incoming, send_sem=send_sems.at[i],
                                              recv_sem=recv_sems.at[i], device_id=peer, device_id_type=MESH)
            cp.wait_send()
            cp.wait_recv()
        for cp in local:
            cp.wait()

    res = pl.pallas_call(
        body, name=name, in_specs=[HBM_SPEC] * nb + [SEM_SPEC, SEM_SPEC, pl.BlockSpec(memory_space=pl.ANY)],
        out_specs=[HBM_SPEC] * nb, out_shape=[pltpu.HBM(v.shape, v.dtype) for v in bufs],
        input_output_aliases={i: i for i in range(nb)},
        scratch_shapes=[pltpu.SemaphoreType.DMA((max(nlocal, 1),))],
        compiler_params=pltpu.CompilerParams(has_side_effects=EFFECT),
    )(*bufs, send_sems, recv_sems, after)
    return list(res)


def _hold(values, after, *, name):
    n = len(values)

    def body(*refs):
        del refs

    return list(pl.pallas_call(
        body, name=name, in_specs=[HBM_SPEC] * n + [pl.BlockSpec(memory_space=pl.ANY)], out_specs=[HBM_SPEC] * n,
        out_shape=[pltpu.HBM(v.shape, v.dtype) for v in values], input_output_aliases={i: i for i in range(n)},
    )(*values, after))


def _place():
    x, y, c = lax.axis_index("x"), lax.axis_index("y"), lax.axis_index("c")
    others = [((1 - x, y, c), 2 * (1 - x) + y), ((x, 1 - y, c), 2 * x + 1 - y), ((1 - x, 1 - y, c), 2 * (1 - x) + 1 - y)]
    return 4 * x + 2 * y + c, 2 * x + y, c, (x, y, 1 - c), others


def _gather1_plan(n):
    def plan(refs):
        me, _, _, sibling, others = _place()
        out = []
        for wi in range(n):
            item, land = refs[wi], refs[n + wi]
            out.append((item, land.at[me], sibling, land.at[me + 1 - 2 * lax.axis_index("c")]))
            for peer, chip in others:
                out.append((item, land.at[me], peer, land.at[2 * chip + lax.axis_index("c")]))
        return out

    return plan


def _gather1_local(n):
    def plan(refs):
        me = _place()[0]
        return [(refs[wi], refs[n + wi].at[me]) for wi in range(n)]

    return plan


def _gather2_plan(n):
    def plan(refs):
        _, _, c, sibling, others = _place()
        out = []
        for wi in range(n):
            land = refs[wi]
            for _, chip in others:
                out.append((land.at[2 * chip + c], land.at[2 * chip + c], sibling, land.at[2 * chip + 1 - c]))
        return out

    return plan


def _gather_start(items, *, name):
    lands = [lax.empty((N_DEV,) + v.shape, v.dtype) for v in items]
    return _split_start(items + lands, 4 * len(items), _gather1_plan(len(items)), name=name)


def _gather_forward(started, after, *, name):
    n = len(started[2]) // 2
    bufs = _split_wait(started, after, _gather1_plan(n), _gather1_local(n), name=name + "_wait")
    return _split_start(bufs[n:], 3 * n, _gather2_plan(n), name=name + "_start")


def _gather_finish(started, after, *, name):
    n = len(started[2])
    return _split_wait(started, after, _gather2_plan(n), lambda refs: [], name=name)


def _handshake(peers):
    barrier = pltpu.get_barrier_semaphore()
    for peer in peers:
        pl.semaphore_signal(barrier, inc=1, device_id=peer, device_id_type=MESH)
    pl.semaphore_wait(barrier, len(peers))


def _remote(src, dst, send_sem, recv_sem, peer):
    return pltpu.make_async_remote_copy(src_ref=src, dst_ref=dst, send_sem=send_sem, recv_sem=recv_sem, device_id=peer,
                                        device_id_type=MESH)


def _sequencer_gather(items, *, collective_id, name):
    n = len(items)
    srcs = [jax.new_ref(v, memory_space=pltpu.MemorySpace.HBM) for v in items]
    lands = [jax.empty_ref(jax.ShapeDtypeStruct((N_DEV,) + v.shape, v.dtype), memory_space=pltpu.MemorySpace.HBM)
             for v in items]
    dma = pltpu.SemaphoreType.DMA

    @pl.kernel(mesh=plsc.ScalarSubcoreMesh(axis_name="sequencer", num_cores=1), name=name,
               scratch_types=(dma((4 * n,)), dma((4 * n,)), dma((3 * n,)), dma((3 * n,)), dma((n,))),
               compiler_params=pltpu.CompilerParams(collective_id=collective_id))
    def launch(send1, recv1, send2, recv2, local_sems):
        _, _, _, sibling, others = _place()
        _handshake([sibling] + [peer for peer, _ in others])
        hop1 = _gather1_plan(n)(srcs + lands)
        hop2 = _gather2_plan(n)(lands)
        local = [pltpu.make_async_copy(src, dst, local_sems.at[j])
                 for j, (src, dst) in enumerate(_gather1_local(n)(srcs + lands))]
        for cp in local:
            cp.start()
        for i, (src, dst, peer, _) in enumerate(hop1):
            _remote(src, dst, send1.at[i], recv1.at[i], peer).start()
        for wi in range(n):
            for j in range(3):
                i1, i2 = 4 * wi + 1 + j, 3 * wi + j
                src, _, peer, incoming = hop1[i1]
                _remote(src, incoming, send1.at[i1], recv1.at[i1], peer).wait_recv()
                src, dst, peer, _ = hop2[i2]
                _remote(src, dst, send2.at[i2], recv2.at[i2], peer).start()
        for wi in range(n):
            src, _, peer, incoming = hop1[4 * wi]
            _remote(src, incoming, send1.at[4 * wi], recv1.at[4 * wi], peer).wait_recv()
        for i, (src, _, peer, incoming) in enumerate(hop2):
            cp = _remote(src, incoming, send2.at[i], recv2.at[i], peer)
            cp.wait_send()
            cp.wait_recv()
        for i, (src, dst, peer, _) in enumerate(hop1):
            _remote(src, dst, send1.at[i], recv1.at[i], peer).wait_send()
        for cp in local:
            cp.wait()

    launch()
    return [land[...] for land in lands]


def _sequencer_exchange(sources, land_shapes, ncopies, plan, local_plan, peers, *, collective_id, name):
    srcs = [jax.new_ref(v, memory_space=pltpu.MemorySpace.HBM) for v in sources]
    lands = [jax.empty_ref(s, memory_space=pltpu.MemorySpace.HBM) for s in land_shapes]
    nlocal = len(local_plan(srcs + lands))
    dma = pltpu.SemaphoreType.DMA

    @pl.kernel(mesh=plsc.ScalarSubcoreMesh(axis_name="sequencer", num_cores=1), name=name,
               scratch_types=(dma((ncopies,)), dma((ncopies,)), dma((max(nlocal, 1),))),
               compiler_params=pltpu.CompilerParams(collective_id=collective_id))
    def launch(send_sems, recv_sems, local_sems):
        _handshake(peers(_place()))
        copies = plan(srcs + lands)
        local = [pltpu.make_async_copy(src, dst, local_sems.at[j])
                 for j, (src, dst) in enumerate(local_plan(srcs + lands))]
        for cp in local:
            cp.start()
        for i, (src, dst, peer, _) in enumerate(copies):
            _remote(src, dst, send_sems.at[i], recv_sems.at[i], peer).start()
        for i, (src, _, peer, incoming) in enumerate(copies):
            cp = _remote(src, incoming, send_sems.at[i], recv_sems.at[i], peer)
            cp.wait_send()
            cp.wait_recv()
        for cp in local:
            cp.wait()

    launch()
    return [land[...] for land in lands]


def _sequencer_scatter_hop2(sums, *, collective_id, name):
    n = len(sums)
    shapes = [jax.ShapeDtypeStruct(v.shape, v.dtype) for v in sums]
    return _sequencer_exchange(sums, shapes, 3 * n, _scatter2_plan(n), _scatter2_local(n),
                               lambda place: [peer for peer, _ in place[4]], collective_id=collective_id, name=name)


N_CHIP = N_DEV // 2


def _scatter1_plan(n):
    def plan(refs):
        _, _, c, sibling, _ = _place()
        out = []
        for wi in range(n):
            parts, half = refs[wi], refs[n + wi]
            for chip in range(N_CHIP):
                out.append((parts.at[2 * chip + 1 - c], half.at[chip], sibling, half.at[chip]))
        return out

    return plan


def _scatter2_plan(n):
    def plan(refs):
        _, my_chip, _, _, others = _place()
        out = []
        for wi in range(n):
            sums, recv = refs[wi], refs[n + wi]
            for peer, chip in others:
                out.append((sums.at[chip], recv.at[my_chip], peer, recv.at[chip]))
        return out

    return plan


def _scatter2_local(n):
    def plan(refs):
        my_chip = _place()[1]
        return [(refs[wi].at[my_chip], refs[n + wi].at[my_chip]) for wi in range(n)]

    return plan


def _pair_add(parts, half, core, *, name):
    _, r, c = parts.shape
    tr = max(d for d in range(HALO, 257, HALO) if r % d == 0) if r > 256 else r
    parts4 = parts.reshape(N_CHIP, 2, r, c)

    def body(core_ref, p_ref, h_ref, o_ref):
        o_ref[...] = (p_ref[:, 0].astype(F32) + h_ref[...].astype(F32)).astype(o_ref.dtype)

    return pl.pallas_call(
        body, name=name,
        grid_spec=pltpu.PrefetchScalarGridSpec(
            num_scalar_prefetch=1, grid=(r // tr,),
            in_specs=[pl.BlockSpec((N_CHIP, 1, tr, c), lambda i, core_ref: (0, core_ref[0], i, 0)),
                      pl.BlockSpec((N_CHIP, tr, c), lambda i, core_ref: (0, i, 0))],
            out_specs=pl.BlockSpec((N_CHIP, tr, c), lambda i, core_ref: (0, i, 0))),
        out_shape=jax.ShapeDtypeStruct((N_CHIP, r, c), parts.dtype), compiler_params=_cp("parallel"),
    )(core, parts4, half)


def _scatter_start(parts, *, name):
    halves = [lax.empty((N_CHIP,) + v.shape[1:], v.dtype) for v in parts]
    return _split_start(parts + halves, N_CHIP * len(parts), _scatter1_plan(len(parts)), name=name)


def _adamw(parts, w, m, v, *, name):
    r, c = w.shape
    nparts = parts.shape[0]
    tr = max(d for d in range(HALO, 129, HALO) if r % d == 0) if r > 128 else r

    def body(p_ref, w_ref, m_ref, v_ref, g_ref, d_ref, mo_ref, vo_ref):
        g = p_ref[0].astype(F32)
        for k in range(1, nparts):
            g = g + p_ref[k].astype(F32)
        mn = ADAM_B1 * m_ref[...] + (1.0 - ADAM_B1) * g
        vn = ADAM_B2 * v_ref[...] + (1.0 - ADAM_B2) * (g * g)
        m_hat = mn / (1.0 - ADAM_B1 ** ADAM_STEP)
        v_hat = vn / (1.0 - ADAM_B2 ** ADAM_STEP)
        g_ref[...] = g
        d_ref[...] = -ADAM_LR * (m_hat / (jnp.sqrt(v_hat) + ADAM_EPS) + ADAM_WD * w_ref[...])
        mo_ref[...] = mn
        vo_ref[...] = vn

    blk = pl.BlockSpec((tr, c), lambda i: (i, 0))
    return pl.pallas_call(
        body, name=name, grid=(r // tr,), in_specs=[pl.BlockSpec((nparts, tr, c), lambda i: (0, i, 0)), blk, blk, blk],
        out_specs=[blk] * 4, out_shape=[jax.ShapeDtypeStruct((r, c), F32)] * 4, compiler_params=_cp("parallel"),
    )(parts, w, m, v)


def _pack_rows(vs, rows):
    lead = vs[0].shape[:-1] if vs[0].ndim > 1 else ()
    flat = jnp.concatenate(vs, axis=-1)
    pad = rows * LANES - flat.shape[-1]
    flat = jnp.pad(flat, [(0, 0)] * len(lead) + [(0, pad)])
    return flat.reshape(lead + (rows, LANES))


def kernel(x, p, positions, mix_norm_w, w_in, conv_w, conv_b, dt_bias, a_log, d_skip, ssd_norm_w, q_a_norm_w, w_q_b, kv_a_norm_w, w_kv_b, w_out, ffn_norm_w, w_ffn_up, ffn_conv_w, ffn_conv_b, w_ffn_down, ple_norm_w, w_ple_gate, b_ple_gate, w_ple_proj, ple_post_norm_w, final_norm_w, loss_target, m_mix_norm_w, m_w_in, m_conv_w, m_conv_b, m_dt_bias, m_a_log, m_d_skip, m_ssd_norm_w, m_q_a_norm_w, m_w_q_b, m_kv_a_norm_w, m_w_kv_b, m_w_out, m_ffn_norm_w, m_w_ffn_up, m_ffn_conv_w, m_ffn_conv_b, m_w_ffn_down, m_ple_norm_w, m_w_ple_gate, m_b_ple_gate, m_w_ple_proj, m_ple_post_norm_w, m_final_norm_w, v_mix_norm_w, v_w_in, v_conv_w, v_conv_b, v_dt_bias, v_a_log, v_d_skip, v_ssd_norm_w, v_q_a_norm_w, v_w_q_b, v_kv_a_norm_w, v_w_kv_b, v_w_out, v_ffn_norm_w, v_w_ffn_up, v_ffn_conv_w, v_ffn_conv_b, v_w_ffn_down, v_ple_norm_w, v_w_ple_gate, v_b_ple_gate, v_w_ple_proj, v_ple_post_norm_w, v_final_norm_w):
    given = dict(locals())
    shapes = {n: given[n].shape for n in WEIGHTS}
    w2 = {n: given[n].reshape(given[n].shape[-2:] if n in BIG or n in CONV else (1, -1)) for n in WEIGHTS}
    m2 = {n: given['m_' + n].reshape(w2[n].shape) for n in WEIGHTS}
    v2 = {n: given['v_' + n].reshape(w2[n].shape) for n in WEIGHTS}
    me = 4 * lax.axis_index("x") + 2 * lax.axis_index("y") + lax.axis_index("c")

    core = lax.axis_index("c").astype(jnp.int32).reshape(1)

    def shards(grp, zero):
        return [(w2[n] + zero).astype(BF16) if n in BIG else w2[n] + zero for n in WEIGHT_GROUPS[grp]]

    first, token = _gather_start(shards('a', 0.0), name="gather_a_hop1")
    first, token = _gather_forward(first, token, name="gather_a_hop2")
    zero = token[0, 0]
    later = dict(zip(WEIGHT_GROUPS['b'], _sequencer_gather(shards('b', zero), collective_id=1, name="gather_b")))
    later.update(zip(WEIGHT_GROUPS['c'], _sequencer_gather(shards('c', zero), collective_id=2, name="gather_c")))

    def get_w(grp, after):
        if grp == 'a':
            lands = dict(zip(WEIGHT_GROUPS[grp], _gather_finish(first, token, name="gather_a_done")))
        else:
            names = WEIGHT_GROUPS[grp]
            lands = dict(zip(names, _hold([later[n] for n in names], after, name="gather_" + grp + "_use")))
        return _assemble_weights(lands)

    scatters = {}

    hop_ids = {grp: 2 + 2 * i for i, grp in enumerate(GRAD_GROUPS)}

    def zero_of(arrays):
        return sum(v[(0,) * v.ndim].astype(F32) * 0.0 for v in arrays)

    def emit(grp, grads):
        scatters[grp], tok = _scatter_start([grads[n] for n in GRAD_GROUPS[grp]], name="scatter_" + grp + "_hop1")
        return tok[0, 0]

    def relay(grp, after):
        n = len(GRAD_GROUPS[grp])
        bufs = _split_wait(scatters[grp], after, _scatter1_plan(n), lambda refs: [], name="scatter_" + grp + "_hop1_wait")
        sums = [_pair_add(bufs[i], bufs[n + i], core, name="scatter_%s_add%d" % (grp, i)) for i in range(n)]
        scatters[grp] = _sequencer_scatter_hop2(sums, collective_id=hop_ids[grp] + 1, name="scatter_" + grp + "_hop2")
        return zero_of(sums)

    out_g, out_d, out_m, out_v = {}, {}, {}, {}

    def settle(grp):
        return zero_of(scatters[grp])

    def update(grp, behind=None):
        for n, parts in zip(GRAD_GROUPS[grp], scatters[grp]):
            wn = w2[n] if behind is None else w2[n] + behind
            out_g[n], out_d[n], out_m[n], out_v[n] = _adamw(parts, wn, m2[n], v2[n], name="adamw_" + n)

    vecs = {n: w2[n] for n in REPL}
    vecs['mix_norm_w'] = vecs['mix_norm_w'] + zero
    loss, dx, g_conv, g_vec = _local_step(x[0], p[0, 0], _rope_tables(positions), get_w, vecs, loss_target[0], emit,
                                          relay, settle)
    n_small = sum(g_vec[n].shape[1] for n in REPL) + sum(g_conv[n].size for n in CONV) + 1
    rows_small = -(-n_small // (LANES * HALO)) * HALO
    small = _pack_rows([g_vec[n] for n in REPL] + [g_conv[n].reshape(1, -1) for n in CONV] + [loss], rows_small)

    for grp in list(GRAD_GROUPS)[:-1]:
        update(grp)
    all_small = _exchange([small], gather=True, name="gather_small_grads")[0].reshape(N_DEV, rows_small * LANES)
    update(list(GRAD_GROUPS)[-1], zero_of([all_small]))
    pieces, off = [], 0
    for n in REPL:
        k = g_vec[n].shape[1]
        pieces.append(all_small[:, off:off + k])
        off += k
    for n in CONV:
        kw, cols = g_conv[n].shape
        full = all_small[:, off:off + kw * cols].reshape(N_DEV, kw, cols)
        mine = lax.dynamic_slice_in_dim(full, me * (cols // N_DEV), cols // N_DEV, axis=2)
        pieces.append(mine.reshape(N_DEV, kw * (cols // N_DEV)))
        off += kw * cols
    pieces.append(all_small[:, off:off + 1])
    small_names = REPL + CONV
    n_mine = sum(q.shape[1] for q in pieces)
    rows_mine = -(-n_mine // (LANES * HALO)) * HALO
    zero = jnp.zeros((1, 1), F32)
    packed = [_pack_rows([src[n].reshape(1, -1) for n in small_names] + [zero], rows_mine).reshape(rows_mine, LANES)
              for src in (w2, m2, v2)]
    sg, sd, sm, sv = _adamw(_pack_rows(pieces, rows_mine), *packed, name="adamw_small")
    off = 0
    for n in small_names:
        k = w2[n].size
        for dst, src in ((out_g, sg), (out_d, sd), (out_m, sm), (out_v, sv)):
            dst[n] = src.reshape(-1)[off:off + k].reshape(w2[n].shape)
        off += k
    total_loss = sg.reshape(-1)[off]

    outs = [total_loss, dx[None]]
    for res in (out_g, out_d, out_m, out_v):
        outs += [res[n].reshape(shapes[n]) for n in WEIGHTS]
    return tuple(outs)
```

```python
import math

import numpy as np
import jax
import jax.numpy as jnp
from jax import lax
from jax.experimental import pallas as pl
from jax.experimental.pallas import tpu as pltpu
from jax.experimental.pallas import tpu_sc as plsc

F32 = jnp.float32
BF16 = jnp.bfloat16
HI = lax.Precision.HIGHEST

D_MODEL = 2048
CHUNK = 64
D_SSM = 1024
SSD_P = 64
SSD_HEADS = 16
SSD_GROUPS = 2
SSD_N = 128
SSD_CONV = 4
SSD_CONV_DIM = D_SSM + 2 * SSD_GROUPS * SSD_N
MLA_HEADS = 8
MLA_NOPE = 128
MLA_ROPE = 64
MLA_V = 128
MLA_Q_RANK = 512
MLA_KV_RANK = 256
MLA_QK_PAD = 256
ROPE_THETA = 10000.0
D_FF = 5632
FFN_CONV = 3
PLE_DIM = 256
NORM_EPS = 1e-6
ADAM_LR, ADAM_B1, ADAM_B2, ADAM_EPS, ADAM_WD, ADAM_STEP = 0.001, 0.9, 0.999, 1e-08, 0.01, 10
N_DEV = 8

OFF_Z, OFF_XBC, OFF_QA, OFF_CKV, OFF_KR, OFF_DT, D_IN_PAD = 0, 1024, 2560, 3072, 3328, 3456, 3584
D_IN = 3408
LANES = 128
HALO = 8
VMEM_LIMIT = 56 * 1024 * 1024
FFN_TC = D_FF * 2 // N_DEV
FFN_PERM = (0, 4, 1, 5, 2, 6, 3, 7)
NEG = -1e30


def _cp(*sem):
    return pltpu.CompilerParams(dimension_semantics=tuple(sem), vmem_limit_bytes=VMEM_LIMIT)


def _tile(n, want):
    if n <= want:
        return n
    best = max(d for d in range(LANES, want + 1, LANES) if n % d == 0)
    return best


def _sigmoid(x):
    return 0.5 * (jnp.tanh(0.5 * x) + 1.0)


def _silu(x):
    return x * _sigmoid(x)


def _dsilu(x):
    s = _sigmoid(x)
    return s * (1.0 + x * (1.0 - s))


MM_TILE = 1408
MM_TK = 2816


def _matmul(a, b, *, ta=False, tb=False, out_dtype=F32, add=None, bias=None, tm=MM_TILE, tn=MM_TILE, tk=MM_TK, name,
            mnk=None, a_spec=None, b_spec=None, o_spec=None, o_shape=None):
    if mnk is None:
        m, k = (a.shape[1], a.shape[0]) if ta else a.shape
        n = b.shape[0] if tb else b.shape[1]
        assert k == (b.shape[1] if tb else b.shape[0])
    else:
        m, n, k = mnk
    tm, tn, tk = _tile(m, tm), _tile(n, tn), _tile(k, tk)
    nk = k // tk
    dims = (((0 if ta else 1,), (1 if tb else 0,)), ((), ()))

    def body(*refs):
        a_ref, b_ref = refs[0], refs[1]
        pos = 2
        add_ref = bias_ref = None
        if add is not None:
            add_ref = refs[pos]
            pos += 1
        if bias is not None:
            bias_ref = refs[pos]
            pos += 1
        o_ref = refs[pos]
        kk = pl.program_id(2)
        av = a_ref[...]
        bv = b_ref[...]
        av = av.reshape(av.shape[-2:]).astype(BF16)
        bv = bv.reshape(bv.shape[-2:]).astype(BF16)
        prod = lax.dot_general(av, bv, dims, preferred_element_type=F32)

        def finish(r):
            if bias_ref is not None:
                r = r + bias_ref[...]
            if add_ref is not None:
                r = r + add_ref[...].astype(F32)
            o_ref[...] = r.astype(out_dtype).reshape(o_ref.shape)

        if nk == 1:
            finish(prod)
        else:
            acc_ref = refs[pos + 1]

            @pl.when(kk == 0)
            def _():
                acc_ref[...] = prod

            @pl.when(kk > 0)
            def _():
                acc_ref[...] += prod

            @pl.when(kk == nk - 1)
            def _():
                finish(acc_ref[...])

    if a_spec is None:
        a_spec = (pl.BlockSpec((tk, tm), lambda i, j, kk: (kk, i)) if ta
                  else pl.BlockSpec((tm, tk), lambda i, j, kk: (i, kk)))
    if b_spec is None:
        b_spec = (pl.BlockSpec((tn, tk), lambda i, j, kk: (j, kk)) if tb
                  else pl.BlockSpec((tk, tn), lambda i, j, kk: (kk, j)))
    if o_spec is None:
        o_spec = pl.BlockSpec((tm, tn), lambda i, j, kk: (i, j))
    if o_shape is None:
        o_shape = (m, n)
    in_specs = [a_spec, b_spec]
    args = [a, b]
    if add is not None:
        in_specs.append(pl.BlockSpec((tm, tn), lambda i, j, kk: (i, j)))
        args.append(add)
    if bias is not None:
        in_specs.append(pl.BlockSpec((1, tn), lambda i, j, kk: (0, j)))
        args.append(bias)
    return pl.pallas_call(
        body, name=name, grid=(m // tm, n // tn, nk), in_specs=in_specs, out_specs=o_spec,
        out_shape=jax.ShapeDtypeStruct(o_shape, out_dtype),
        scratch_shapes=[pltpu.VMEM((tm, tn), F32)] if nk > 1 else [],
        compiler_params=_cp("parallel", "parallel", "arbitrary"),
    )(*args)


def _rmsnorm_fwd(x, w, *, width, cblk=0, out_dtype=BF16, tr=256, name):
    t = x.shape[0]

    def body(x_ref, w_ref, o_ref):
        xv = x_ref[...].astype(F32)
        r = lax.rsqrt(jnp.mean(xv * xv, axis=-1, keepdims=True) + NORM_EPS)
        o_ref[...] = (xv * r * w_ref[...]).astype(out_dtype)

    return pl.pallas_call(
        body, name=name, grid=(t // tr,),
        in_specs=[pl.BlockSpec((tr, width), lambda i: (i, cblk)), pl.BlockSpec((1, width), lambda i: (0, 0))],
        out_specs=pl.BlockSpec((tr, width), lambda i: (i, 0)),
        out_shape=jax.ShapeDtypeStruct((t, width), out_dtype),
        compiler_params=_cp("parallel"),
    )(x, w)


def _rmsnorm_bwd(x, w, dy, add=None, *, width, cblk=0, out_dtype=F32, also_bf16=False, tr=256, name):
    t = x.shape[0]

    def body(*refs):
        refs = list(refs)
        dxb_ref = refs.pop() if also_bf16 else None
        if add is None:
            x_ref, w_ref, dy_ref, dx_ref, dw_ref = refs
            add_ref = None
        else:
            x_ref, w_ref, dy_ref, add_ref, dx_ref, dw_ref = refs
        xv = x_ref[...].astype(F32)
        dyv = dy_ref[...].astype(F32)
        r = lax.rsqrt(jnp.mean(xv * xv, axis=-1, keepdims=True) + NORM_EPS)
        xh = xv * r
        g = dyv * w_ref[...]
        dx = r * (g - xh * jnp.mean(g * xh, axis=-1, keepdims=True))
        if add_ref is not None:
            dx = dx + add_ref[...].astype(F32)
        dx_ref[...] = dx.astype(out_dtype)
        if dxb_ref is not None:
            dxb_ref[...] = dx.astype(BF16)

        @pl.when(pl.program_id(0) == 0)
        def _():
            dw_ref[...] = jnp.zeros_like(dw_ref)

        dw_ref[...] += jnp.sum(dyv * xh, axis=0, keepdims=True)

    in_specs = [pl.BlockSpec((tr, width), lambda i: (i, cblk)), pl.BlockSpec((1, width), lambda i: (0, 0)),
                pl.BlockSpec((tr, width), lambda i: (i, 0))]
    args = [x, w, dy]
    if add is not None:
        in_specs.append(pl.BlockSpec((tr, width), lambda i: (i, 0)))
        args.append(add)
    blk = pl.BlockSpec((tr, width), lambda i: (i, 0))
    return pl.pallas_call(
        body, name=name, grid=(t // tr,), in_specs=in_specs,
        out_specs=[blk, pl.BlockSpec((1, width), lambda i: (0, 0))] + ([blk] if also_bf16 else []),
        out_shape=[jax.ShapeDtypeStruct((t, width), out_dtype), jax.ShapeDtypeStruct((1, width), F32)]
        + ([jax.ShapeDtypeStruct((t, width), BF16)] if also_bf16 else []),
        compiler_params=_cp("arbitrary"),
    )(*args)


def _shift_down(prev_halo, cur, j):
    if j == 0:
        return cur
    ext = jnp.concatenate([prev_halo, cur], axis=0)
    return pltpu.roll(ext, j, axis=0)[HALO:]


def _shift_up(cur, next_halo, j):
    if j == 0:
        return cur
    ext = jnp.concatenate([cur, next_halo], axis=0)
    return pltpu.roll(ext, ext.shape[0] - j, axis=0)[:cur.shape[0]]


def _conv_rows(prev, cur, w, b, kw):
    shifted = [cur]
    out = b + w[kw - 1:kw] * cur
    for j in range(1, kw):
        sh = _shift_down(prev, cur, j)
        shifted.append(sh)
        out = out + w[kw - 1 - j:kw - j] * sh
    return out, shifted


def _act_fwd(c, glu):
    if glu:
        half = c.shape[1] // 2
        return _silu(c[:, :half]) * c[:, half:]
    return _silu(c)


def _act_bwd(c, dout, glu):
    if glu:
        half = c.shape[1] // 2
        g, up = c[:, :half], c[:, half:]
        s = _sigmoid(g)
        gs = g * s
        return jnp.concatenate([dout * up * (s + gs * (1.0 - s)), dout * gs], axis=1)
    return dout * _dsilu(c)


def _conv_act_fwd(u, w, b, *, kw, glu, tc, coff, ncols, out_dtype, tr=256, name):
    t = u.shape[0]
    nb = ncols // tc
    oc = tc // 2 if glu else tc

    def body(u_ref, uh_ref, w_ref, b_ref, o_ref):
        prev = jnp.where(pl.program_id(0) == 0, 0.0, uh_ref[...])
        c, _ = _conv_rows(prev, u_ref[...], w_ref[...], b_ref[...], kw)
        o_ref[...] = _act_fwd(c, glu).astype(out_dtype)

    return pl.pallas_call(
        body, name=name, grid=(t // tr, nb),
        in_specs=[pl.BlockSpec((tr, tc), lambda i, j: (i, j + coff)),
                  pl.BlockSpec((HALO, tc), lambda i, j: (jnp.maximum(i * (tr // HALO) - 1, 0), j + coff)),
                  pl.BlockSpec((kw, tc), lambda i, j: (0, j)), pl.BlockSpec((1, tc), lambda i, j: (0, j))],
        out_specs=pl.BlockSpec((tr, oc), lambda i, j: (i, j)),
        out_shape=jax.ShapeDtypeStruct((t, nb * oc), out_dtype),
        compiler_params=_cp("parallel", "parallel"),
    )(u, u, w, b)


def _conv_act_bwd(u, w, b, dout, *, kw, glu, tc, coff, ncols, tr=256, name):
    t = u.shape[0]
    nb = ncols // tc
    nt = t // tr
    oc = tc // 2 if glu else tc

    def body(u_ref, up_ref, un_ref, d_ref, dn_ref, w_ref, b_ref, du_ref, dw_ref, db_ref):
        i = pl.program_id(1)
        cur, nxt, wv, bv = u_ref[...], un_ref[...], w_ref[...], b_ref[...]
        prev = jnp.where(i == 0, 0.0, up_ref[...])
        c_cur, shifted = _conv_rows(prev, cur, wv, bv, kw)
        c_nxt, _ = _conv_rows(cur[tr - HALO:], nxt, wv, bv, kw)
        d_cur = _act_bwd(c_cur, d_ref[...].astype(F32), glu)
        d_nxt = _act_bwd(c_nxt, jnp.where(i == nt - 1, 0.0, dn_ref[...].astype(F32)), glu)
        du = wv[kw - 1:kw] * d_cur
        for j in range(1, kw):
            du = du + wv[kw - 1 - j:kw - j] * _shift_up(d_cur, d_nxt, j)
        du_ref[...] = du.astype(BF16)

        @pl.when(i == 0)
        def _():
            dw_ref[...] = jnp.zeros_like(dw_ref)
            db_ref[...] = jnp.zeros_like(db_ref)

        db_ref[...] += jnp.sum(d_cur, axis=0, keepdims=True)
        dw_ref[...] += jnp.concatenate(
            [jnp.sum(d_cur * shifted[kw - 1 - k], axis=0, keepdims=True) for k in range(kw)], axis=0)

    nh = tr // HALO
    return pl.pallas_call(
        body, name=name, grid=(nb, nt),
        in_specs=[pl.BlockSpec((tr, tc), lambda j, i: (i, j + coff)),
                  pl.BlockSpec((HALO, tc), lambda j, i: (jnp.maximum(i * nh - 1, 0), j + coff)),
                  pl.BlockSpec((HALO, tc), lambda j, i: (jnp.minimum((i + 1) * nh, t // HALO - 1), j + coff)),
                  pl.BlockSpec((tr, oc), lambda j, i: (i, j)),
                  pl.BlockSpec((HALO, oc), lambda j, i: (jnp.minimum((i + 1) * nh, t // HALO - 1), j)),
                  pl.BlockSpec((kw, tc), lambda j, i: (0, j)), pl.BlockSpec((1, tc), lambda j, i: (0, j))],
        out_specs=[pl.BlockSpec((tr, tc), lambda j, i: (i, j)), pl.BlockSpec((kw, tc), lambda j, i: (0, j)),
                   pl.BlockSpec((1, tc), lambda j, i: (0, j))],
        out_shape=[jax.ShapeDtypeStruct((t, ncols), BF16), jax.ShapeDtypeStruct((kw, ncols), F32),
                   jax.ShapeDtypeStruct((1, ncols), F32)],
        compiler_params=_cp("parallel", "arbitrary"),
    )(u, u, u, dout, dout, w, b)


def _ple_loss(x2, gl, pe, pw, fw, target, *, tr=256, name):
    t, d = x2.shape

    def body(x_ref, gl_ref, pe_ref, pw_ref, fw_ref, t_ref, l_ref, dx_ref, dfw_ref, dgl_ref, db_ref, dpe_ref, dpw_ref):
        pv, pwv, wv = pe_ref[...], pw_ref[...], fw_ref[...]
        gate = _sigmoid(gl_ref[...])
        rp = lax.rsqrt(jnp.mean(pv * pv, axis=-1, keepdims=True) + NORM_EPS)
        ph = pv * rp
        e = ph * pwv
        x3 = x_ref[...] + gate * e
        r = lax.rsqrt(jnp.mean(x3 * x3, axis=-1, keepdims=True) + NORM_EPS)
        xh = x3 * r
        err = xh * wv - t_ref[...]
        dy = err * (1.0 / d)
        g = dy * wv
        dx = r * (g - xh * jnp.mean(g * xh, axis=-1, keepdims=True))
        dx_ref[...] = dx
        dgl = dx * e * gate * (1.0 - gate)
        de = dx * gate
        gg = de * pwv
        dgl_ref[...] = dgl.astype(BF16)
        dpe_ref[...] = (rp * (gg - ph * jnp.mean(gg * ph, axis=-1, keepdims=True))).astype(BF16)

        @pl.when(pl.program_id(0) == 0)
        def _():
            for ref in (l_ref, dfw_ref, db_ref, dpw_ref):
                ref[...] = jnp.zeros_like(ref)

        l_ref[...] += 0.5 * jnp.sum(jnp.mean(err * err, axis=-1, keepdims=True), axis=0, keepdims=True)
        dfw_ref[...] += jnp.sum(dy * xh, axis=0, keepdims=True)
        db_ref[...] += jnp.sum(dgl, axis=0, keepdims=True)
        dpw_ref[...] += jnp.sum(de * ph, axis=0, keepdims=True)

    blk = pl.BlockSpec((tr, d), lambda i: (i, 0))
    row = pl.BlockSpec((1, d), lambda i: (0, 0))
    rowf = jax.ShapeDtypeStruct((1, d), F32)
    return pl.pallas_call(
        body, name=name, grid=(t // tr,), in_specs=[blk, blk, blk, row, row, blk],
        out_specs=[pl.BlockSpec((1, 1), lambda i: (0, 0)), blk, row, blk, row, blk, row],
        out_shape=[jax.ShapeDtypeStruct((1, 1), F32), jax.ShapeDtypeStruct((t, d), F32), rowf,
                   jax.ShapeDtypeStruct((t, d), BF16), rowf, jax.ShapeDtypeStruct((t, d), BF16), rowf],
        compiler_params=_cp("arbitrary"),
    )(x2, gl, pe, pw, fw, target)


def _rope(blk, tab_ref):
    return blk * tab_ref[0] + pltpu.roll(blk, 96, axis=1) * tab_ref[1] + pltpu.roll(blk, 32, axis=1) * tab_ref[2]


def _unrope(g, tab_ref):
    return g * tab_ref[0] + pltpu.roll(g * tab_ref[1], 32, axis=1) + pltpu.roll(g * tab_ref[2], 96, axis=1)


def _mla_prep(q, kv, proj, tabs, *, tr=512, name):
    t = q.shape[0]

    def body(q_ref, kv_ref, kr_ref, tab_ref, qo_ref, ko_ref, vo_ref, vt_ref):
        qv, kvv = q_ref[...], kv_ref[...]
        qo_ref[0, :, :MLA_NOPE] = qv[:, :MLA_NOPE].astype(BF16)
        qo_ref[0, :, MLA_NOPE:] = _rope(qv[:, MLA_NOPE:], tab_ref).astype(BF16)
        ko_ref[0, :, :MLA_NOPE] = kvv[:, :MLA_NOPE].astype(BF16)
        ko_ref[0, :, MLA_NOPE:] = _rope(kr_ref[...], tab_ref).astype(BF16)
        vo_ref[0] = kvv[:, MLA_NOPE:].astype(BF16)
        for blk in range(tr // ATT_BLK):
            vt_ref[0, blk] = kvv[blk * ATT_BLK:(blk + 1) * ATT_BLK, MLA_NOPE:].T.astype(BF16)

    return pl.pallas_call(
        body, name=name, grid=(t // tr, MLA_HEADS),
        in_specs=[pl.BlockSpec((tr, MLA_QK_PAD), lambda i, h: (i, h)),
                  pl.BlockSpec((tr, MLA_NOPE + MLA_V), lambda i, h: (i, h)),
                  pl.BlockSpec((tr, LANES), lambda i, h: (i, OFF_KR // LANES)),
                  pl.BlockSpec((3, tr, LANES), lambda i, h: (0, i, 0))],
        out_specs=[pl.BlockSpec((1, tr, MLA_QK_PAD), lambda i, h: (h, i, 0)),
                   pl.BlockSpec((1, tr, MLA_QK_PAD), lambda i, h: (h, i, 0)),
                   pl.BlockSpec((1, tr, MLA_V), lambda i, h: (h, i, 0)),
                   pl.BlockSpec((1, tr // ATT_BLK, MLA_V, ATT_BLK), lambda i, h: (h, i, 0, 0))],
        out_shape=[jax.ShapeDtypeStruct((MLA_HEADS, t, MLA_QK_PAD), BF16),
                   jax.ShapeDtypeStruct((MLA_HEADS, t, MLA_QK_PAD), BF16),
                   jax.ShapeDtypeStruct((MLA_HEADS, t, MLA_V), BF16),
                   jax.ShapeDtypeStruct((MLA_HEADS, t // ATT_BLK, MLA_V, ATT_BLK), BF16)],
        compiler_params=_cp("parallel", "parallel"),
    )(q, kv, proj, tabs)


def _mla_unprep(dq3, dk3, dv3, tabs, *, tr=256, name):
    t = dq3.shape[1]

    def body(dq_ref, dk_ref, dv_ref, tab_ref, qo_ref, kvo_ref, kro_ref):
        kr = jnp.zeros((tr, LANES), F32)
        for h in range(MLA_HEADS):
            c0 = h * MLA_QK_PAD
            qo_ref[:, c0:c0 + MLA_NOPE] = dq_ref[h, :, :MLA_NOPE].astype(BF16)
            qo_ref[:, c0 + MLA_NOPE:c0 + MLA_QK_PAD] = _unrope(dq_ref[h, :, MLA_NOPE:], tab_ref).astype(BF16)
            kvo_ref[:, c0:c0 + MLA_NOPE] = dk_ref[h, :, :MLA_NOPE].astype(BF16)
            kvo_ref[:, c0 + MLA_NOPE:c0 + MLA_QK_PAD] = dv_ref[h].astype(BF16)
            kr = kr + dk_ref[h, :, MLA_NOPE:]
        kro_ref[...] = _unrope(kr, tab_ref).astype(BF16)

    return pl.pallas_call(
        body, name=name, grid=(t // tr,),
        in_specs=[pl.BlockSpec((MLA_HEADS, tr, MLA_QK_PAD), lambda i: (0, i, 0)),
                  pl.BlockSpec((MLA_HEADS, tr, MLA_QK_PAD), lambda i: (0, i, 0)),
                  pl.BlockSpec((MLA_HEADS, tr, MLA_V), lambda i: (0, i, 0)),
                  pl.BlockSpec((3, tr, LANES), lambda i: (0, i, 0))],
        out_specs=[pl.BlockSpec((tr, MLA_HEADS * MLA_QK_PAD), lambda i: (i, 0)),
                   pl.BlockSpec((tr, MLA_HEADS * MLA_QK_PAD), lambda i: (i, 0)),
                   pl.BlockSpec((tr, LANES), lambda i: (i, 0))],
        out_shape=[jax.ShapeDtypeStruct((t, MLA_HEADS * MLA_QK_PAD), BF16),
                   jax.ShapeDtypeStruct((t, MLA_HEADS * MLA_QK_PAD), BF16),
                   jax.ShapeDtypeStruct((t, LANES), BF16)],
        compiler_params=_cp("parallel"),
    )(dq3, dk3, dv3, tabs)


ATT_BLK = 512
ATT_SCALE = 1.0 / math.sqrt(MLA_NOPE + MLA_ROPE)
_NT = (((1,), (1,)), ((), ()))
_TN = (((0,), (0,)), ((), ()))


def _att_scores_t(k, q, diagonal):
    s = lax.dot_general(k, q, _NT, preferred_element_type=F32) * ATT_SCALE
    if not diagonal:
        return s
    key = lax.broadcasted_iota(jnp.int32, s.shape, 0)
    query = lax.broadcasted_iota(jnp.int32, s.shape, 1)
    return jnp.where((key >> 6) <= (query >> 6), s, NEG)


def _att_rows(i):
    return pl.ds(pl.multiple_of(i * ATT_BLK, ATT_BLK), ATT_BLK)


ATT_HEADS = 2


def _attn_fwd(q3, k3, vt4, *, name):
    t = q3.shape[1]
    nq = t // ATT_BLK

    def body(q_ref, k_ref, vt_ref, o_ref, lse_ref):
        qi = pl.program_id(1)
        qs = [q_ref[hh] for hh in range(ATT_HEADS)]

        def step(j, carry, diagonal=False):
            out = []
            for hh, (m, l, acc) in enumerate(carry):
                s = _att_scores_t(k_ref[hh, _att_rows(j), :], qs[hh], diagonal)
                m_new = jnp.maximum(m, jnp.max(s, axis=0, keepdims=True))
                p = jnp.exp(s - m_new)
                alpha = jnp.exp(m - m_new)
                l = alpha * l + jnp.sum(p, axis=0, keepdims=True)
                acc = alpha * acc + jnp.dot(vt_ref[hh, j], p.astype(BF16), preferred_element_type=F32)
                out.append((m_new, l, acc))
            return tuple(out)

        init = tuple((jnp.full((1, ATT_BLK), NEG, F32), jnp.zeros((1, ATT_BLK), F32),
                      jnp.zeros((MLA_V, ATT_BLK), F32)) for _ in range(ATT_HEADS))
        done = step(qi, lax.fori_loop(0, qi, step, init), diagonal=True)
        for hh, (m, l, acc) in enumerate(done):
            o_ref[:, hh * MLA_V:(hh + 1) * MLA_V] = (acc / l).T
            lse_ref[hh, 0] = m + jnp.log(l)

    return pl.pallas_call(
        body, name=name, grid=(MLA_HEADS // ATT_HEADS, nq),
        in_specs=[pl.BlockSpec((ATT_HEADS, ATT_BLK, MLA_QK_PAD), lambda h, i: (h, i, 0)),
                  pl.BlockSpec((ATT_HEADS, t, MLA_QK_PAD), lambda h, i: (h, 0, 0)),
                  pl.BlockSpec((ATT_HEADS, nq, MLA_V, ATT_BLK), lambda h, i: (h, 0, 0, 0))],
        out_specs=[pl.BlockSpec((ATT_BLK, ATT_HEADS * MLA_V), lambda h, i: (i, h)),
                   pl.BlockSpec((ATT_HEADS, 1, 1, ATT_BLK), lambda h, i: (h, i, 0, 0))],
        out_shape=[jax.ShapeDtypeStruct((t, MLA_HEADS * MLA_V), F32),
                   jax.ShapeDtypeStruct((MLA_HEADS, nq, 1, ATT_BLK), F32)],
        compiler_params=_cp("parallel", "parallel"),
    )(q3, k3, vt4)


def _attn_bwd(q3, k3, v3, o, dcat, lse, *, name):
    t = q3.shape[1]
    nq = t // ATT_BLK
    wide = ATT_HEADS * MLA_V

    def body(q_ref, k_ref, v_ref, o_ref, do_ref, lse_ref, dq_ref, dk_ref, dv_ref, delta_ref):
        kj = pl.program_id(1)

        @pl.when(kj == 0)
        def _():
            dq_ref[...] = jnp.zeros_like(dq_ref)
            ones = jnp.ones((HALO, MLA_V), F32)
            for i in range(nq):
                rows = pl.ds(i * ATT_BLK, ATT_BLK)
                prod = o_ref[rows, :] * do_ref[rows, :]
                for hh in range(ATT_HEADS):
                    delta_ref[hh, i] = lax.dot_general(ones, prod[:, hh * MLA_V:(hh + 1) * MLA_V], _NT, precision=HI,
                                                       preferred_element_type=F32)

        def step(i, carry, diagonal=False):
            rows = _att_rows(i)
            out = []
            for hh, (dk, dv) in enumerate(carry):
                k, v = k_ref[hh], v_ref[hh]
                q = q_ref[hh, rows, :]
                dob = do_ref[rows, hh * MLA_V:(hh + 1) * MLA_V].astype(BF16)
                p = jnp.exp(_att_scores_t(k, q, diagonal) - lse_ref[hh, i])
                dv = dv + jnp.dot(p.astype(BF16), dob, preferred_element_type=F32)
                dp = lax.dot_general(v, dob, _NT, preferred_element_type=F32)
                ds = (p * (dp - delta_ref[hh, i, 0:1, :]) * ATT_SCALE).astype(BF16)
                dk = dk + jnp.dot(ds, q, preferred_element_type=F32)
                dq_ref[hh, rows, :] += lax.dot_general(ds, k, _TN, preferred_element_type=F32)
                out.append((dk, dv))
            return tuple(out)

        init = tuple((jnp.zeros((ATT_BLK, MLA_QK_PAD), F32), jnp.zeros((ATT_BLK, MLA_V), F32))
                     for _ in range(ATT_HEADS))
        done = lax.fori_loop(kj + 1, nq, step, step(kj, init, diagonal=True))
        for hh, (dk, dv) in enumerate(done):
            dk_ref[hh] = dk
            dv_ref[hh] = dv

    return pl.pallas_call(
        body, name=name, grid=(MLA_HEADS // ATT_HEADS, nq),
        in_specs=[pl.BlockSpec((ATT_HEADS, t, MLA_QK_PAD), lambda h, j: (h, 0, 0)),
                  pl.BlockSpec((ATT_HEADS, ATT_BLK, MLA_QK_PAD), lambda h, j: (h, j, 0)),
                  pl.BlockSpec((ATT_HEADS, ATT_BLK, MLA_V), lambda h, j: (h, j, 0)),
                  pl.BlockSpec((t, wide), lambda h, j: (0, h)),
                  pl.BlockSpec((t, wide), lambda h, j: (0, MLA_HEADS // ATT_HEADS + h)),
                  pl.BlockSpec((ATT_HEADS, nq, 1, ATT_BLK), lambda h, j: (h, 0, 0, 0))],
        out_specs=[pl.BlockSpec((ATT_HEADS, t, MLA_QK_PAD), lambda h, j: (h, 0, 0)),
                   pl.BlockSpec((ATT_HEADS, ATT_BLK, MLA_QK_PAD), lambda h, j: (h, j, 0)),
                   pl.BlockSpec((ATT_HEADS, ATT_BLK, MLA_V), lambda h, j: (h, j, 0))],
        out_shape=[jax.ShapeDtypeStruct((MLA_HEADS, t, MLA_QK_PAD), F32),
                   jax.ShapeDtypeStruct((MLA_HEADS, t, MLA_QK_PAD), F32),
                   jax.ShapeDtypeStruct((MLA_HEADS, t, MLA_V), F32)],
        scratch_shapes=[pltpu.VMEM((ATT_HEADS, nq, HALO, ATT_BLK), F32)],
        compiler_params=_cp("parallel", "arbitrary"),
    )(q3, k3, v3, o, dcat, lse)


def _ssd_prep(proj, bias128, alog128, *, name):
    t = proj.shape[0]
    nc = t // CHUNK

    def body(raw_ref, b_ref, al_ref, dt_ref, cs_ref, a_ref):
        xv = raw_ref[...] + b_ref[...]
        dt = jnp.maximum(xv, 0.0) + jnp.log(1.0 + jnp.exp(-jnp.abs(xv)))
        a = -jnp.exp(al_ref[...])
        adt = (dt * a).reshape(nc, CHUNK, LANES)
        li = lax.broadcasted_iota(jnp.int32, (nc, CHUNK, CHUNK), 1)
        si = lax.broadcasted_iota(jnp.int32, (nc, CHUNK, CHUNK), 2)
        tril = jnp.where(si <= li, 1.0, 0.0).astype(F32)
        cs = lax.dot_general(tril, adt, (((2,), (1,)), ((0,), (0,))), precision=HI, preferred_element_type=F32)
        dt_ref[...] = dt
        cs_ref[...] = cs.reshape(t, LANES)
        a_ref[...] = a

    blk = pl.BlockSpec((t, LANES), lambda i: (0, 0))
    row = pl.BlockSpec((1, LANES), lambda i: (0, 0))
    return pl.pallas_call(
        body, name=name, grid=(1,),
        in_specs=[pl.BlockSpec((t, LANES), lambda i: (0, OFF_DT // LANES)), row, row],
        out_specs=[blk, blk, row],
        out_shape=[jax.ShapeDtypeStruct((t, LANES), F32), jax.ShapeDtypeStruct((t, LANES), F32),
                   jax.ShapeDtypeStruct((1, LANES), F32)],
        compiler_params=_cp("arbitrary"),
    )(proj, bias128, alog128)


def _ssd_prep_bwd(ddt128, dadt128, proj, bias128, dt128, a128, dd_h, *, name):
    t = proj.shape[0]

    def body(ddt_ref, dadt_ref, raw_ref, b_ref, dt_ref, a_ref, dd_ref, draw_ref, db_ref, dal_ref, dds_ref):
        draw = ddt_ref[...] * _sigmoid(raw_ref[...] + b_ref[...])
        draw_ref[...] = draw.astype(BF16)
        db_ref[...] = jnp.sum(draw, axis=0, keepdims=True)
        dal_ref[...] = jnp.sum(dadt_ref[...] * dt_ref[...], axis=0, keepdims=True) * a_ref[...]
        dds_ref[...] = jnp.sum(dd_ref[...], axis=-1, keepdims=True)

    blk = pl.BlockSpec((t, LANES), lambda i: (0, 0))
    row = pl.BlockSpec((1, LANES), lambda i: (0, 0))
    return pl.pallas_call(
        body, name=name, grid=(1,),
        in_specs=[blk, blk, pl.BlockSpec((t, LANES), lambda i: (0, OFF_DT // LANES)), row, blk, row,
                  pl.BlockSpec((SSD_HEADS, SSD_P), lambda i: (0, 0))],
        out_specs=[blk, row, row, pl.BlockSpec((SSD_HEADS, 1), lambda i: (0, 0))],
        out_shape=[jax.ShapeDtypeStruct((t, LANES), BF16), jax.ShapeDtypeStruct((1, LANES), F32),
                   jax.ShapeDtypeStruct((1, LANES), F32), jax.ShapeDtypeStruct((SSD_HEADS, 1), F32)],
        compiler_params=_cp("arbitrary"),
    )(ddt128, dadt128, proj, bias128, dt128, a128, dd_h)


def _bdot(a, b, ca, cb, precision=None):
    return lax.dot_general(a, b, (((ca,), (cb,)), ((0,), (0,))), precision=precision, preferred_element_type=F32)


def _pieces(x):
    hi = x.astype(BF16)
    rest = x - hi.astype(F32)
    mid = rest.astype(BF16)
    return hi, mid, (rest - mid.astype(F32)).astype(BF16)


def _bdot_sum(a, b, ca, cb, split):
    other = (b if split == 0 else a).astype(BF16)
    out = None
    for piece in _pieces(a if split == 0 else b):
        term = _bdot(piece, other, ca, cb) if split == 0 else _bdot(other, piece, ca, cb)
        out = term if out is None else out + term
    return out


def _head_matrices():
    eye, zero = jnp.eye(SSD_P, dtype=F32), jnp.zeros((SSD_P, SSD_P), F32)
    pick = jnp.stack([jnp.concatenate([eye, zero], axis=0), jnp.concatenate([zero, eye], axis=0)])
    return pick, pick.transpose(0, 2, 1)


def _move(x, sel):
    selb = sel.astype(BF16)
    hi = x.astype(BF16)
    rest = x - hi.astype(F32)
    mid = rest.astype(BF16)
    low = (rest - mid.astype(F32)).astype(BF16)
    out = jnp.dot(hi, selb, preferred_element_type=F32)
    out = out + jnp.dot(mid, selb, preferred_element_type=F32)
    return out + jnp.dot(low, selb, preferred_element_type=F32)


def _pick_head(pair_ref, pick_ref, h):
    return _move(pair_ref[...], pick_ref[h % 2])


def _place_head(out_ref, val, place_ref, h):
    wide = _move(val, place_ref[h % 2])

    @pl.when(h % 2 == 0)
    def _():
        out_ref[...] = wide

    @pl.when(h % 2 == 1)
    def _():
        out_ref[...] += wide


def _ssd_common(x2, dt_ref, cs_ref, csr_ref, b_ref, c_ref, nc):
    x = x2.reshape(nc, CHUNK, SSD_P)
    dt = dt_ref[0].reshape(nc, CHUNK, SSD_P)
    cs = cs_ref[0].reshape(nc, CHUNK, SSD_P)
    csr = csr_ref[0]
    bm = b_ref[...].reshape(nc, CHUNK, SSD_N).astype(BF16)
    cm = c_ref[...].reshape(nc, CHUNK, SSD_N).astype(BF16)
    li = lax.broadcasted_iota(jnp.int32, (nc, CHUNK, CHUNK), 1)
    si = lax.broadcasted_iota(jnp.int32, (nc, CHUNK, CHUNK), 2)
    lmat = jnp.exp(jnp.where(si <= li, cs - csr, NEG))
    g = _bdot(cm, bm, 2, 2)
    cs_last = jnp.sum(jnp.where(li == CHUNK - 1, cs, 0.0), axis=1, keepdims=True)
    xdt = x * dt
    dec = jnp.exp(cs_last - cs)
    return x, dt, cs, bm, cm, li, si, lmat, g, cs_last, xdt, dec


def _ssd_fwd(xbc, dt_h, cs_h, cs_row, dskip_h, *, name):
    t = xbc.shape[0]
    nc = t // CHUNK
    hpg = SSD_HEADS // SSD_GROUPS
    pick, place = _head_matrices()

    def body(xs_ref, dt_ref, cs_ref, csr_ref, b_ref, c_ref, dk_ref, pick_ref, place_ref, y_ref, st_ref, sc_ref, cd_ref):
        h = pl.program_id(0)
        x, dt, cs, bm, cm, li, si, lmat, g, cs_last, xdt, dec = _ssd_common(_pick_head(xs_ref, pick_ref, h), dt_ref,
                                                                           cs_ref, csr_ref, b_ref, c_ref, nc)
        yd = _bdot((g * lmat).astype(BF16), xdt.astype(BF16), 2, 1)
        sc_ref[...] = _bdot(bm, (dec * xdt).astype(BF16), 1, 1)
        cd_ref[...] = jnp.exp(cs_last)

        def step(c, s):
            st_ref[0, c] = s
            return s * cd_ref[c] + sc_ref[c]

        lax.fori_loop(0, nc, step, jnp.zeros((SSD_N, SSD_P), F32))
        yo = _bdot(cm, st_ref[0].astype(BF16), 2, 1) * jnp.exp(cs)
        _place_head(y_ref, (yd + yo + dk_ref[0] * x).reshape(t, SSD_P), place_ref, h)

    head = pl.BlockSpec((1, t, SSD_P), lambda h: (h, 0, 0))
    pair = pl.BlockSpec((t, 2 * SSD_P), lambda h: (0, h // 2))
    nxb = D_SSM // SSD_N
    return pl.pallas_call(
        body, name=name, grid=(SSD_HEADS,),
        in_specs=[pair, head, head, pl.BlockSpec((1, nc, 1, CHUNK), lambda h: (h, 0, 0, 0)),
                  pl.BlockSpec((t, SSD_N), lambda h: (0, nxb + h // hpg)),
                  pl.BlockSpec((t, SSD_N), lambda h: (0, nxb + SSD_GROUPS + h // hpg)),
                  pl.BlockSpec((1, 1, SSD_P), lambda h: (h, 0, 0)),
                  pl.BlockSpec((2, 2 * SSD_P, SSD_P), lambda h: (0, 0, 0)),
                  pl.BlockSpec((2, SSD_P, 2 * SSD_P), lambda h: (0, 0, 0))],
        out_specs=[pair, pl.BlockSpec((1, nc, SSD_N, SSD_P), lambda h: (h, 0, 0, 0))],
        out_shape=[jax.ShapeDtypeStruct((t, D_SSM), F32),
                   jax.ShapeDtypeStruct((SSD_HEADS, nc, SSD_N, SSD_P), F32)],
        scratch_shapes=[pltpu.VMEM((nc, SSD_N, SSD_P), F32), pltpu.VMEM((nc, 1, SSD_P), F32)],
        compiler_params=_cp("arbitrary"),
    )(xbc, dt_h, cs_h, cs_row, xbc, xbc, dskip_h, pick, place)


def _ssd_bwd(xbc, dt_h, cs_h, cs_row, dskip_h, a_h, states, dy, *, name):
    t = xbc.shape[0]
    nc = t // CHUNK
    hpg = SSD_HEADS // SSD_GROUPS
    pick, place = _head_matrices()

    def body(xs_ref, dt_ref, cs_ref, csr_ref, b_ref, c_ref, dk_ref, a_ref, st_ref, dy_ref, pick_ref, place_ref,
             dxs_ref, ddt_ref, dadt_ref, db_ref, dc_ref, dd_ref, dsl_ref, dsc_ref, cd_ref):
        h = pl.program_id(0) * hpg + pl.program_id(1)
        x, dt, cs, bm, cm, li, si, lmat, g, cs_last, xdt, dec = _ssd_common(_pick_head(xs_ref, pick_ref, h), dt_ref,
                                                                           cs_ref, csr_ref, b_ref, c_ref, nc)
        dy = _pick_head(dy_ref, pick_ref, h).reshape(nc, CHUNK, SSD_P)
        dyb = dy.astype(BF16)
        xdtb = xdt.astype(BF16)
        sprev = st_ref[0]
        sprevb = sprev.astype(BF16)
        cdec = jnp.exp(cs_last)
        ecs = jnp.exp(cs)
        dw = (ecs * dy).astype(BF16)
        wmat = _bdot(cm, sprevb, 2, 1)
        dcs = jnp.sum(dy * ecs * wmat, axis=2, keepdims=True)
        dcm = _bdot(dw, sprevb, 2, 2)
        dsl_ref[...] = _bdot(cm, dw, 1, 1)
        cd_ref[...] = cdec

        def step(k, ds):
            c = nc - 1 - k
            dsc_ref[c] = ds
            return ds * cd_ref[c] + dsl_ref[c]

        lax.fori_loop(0, nc, step, jnp.zeros((SSD_N, SSD_P), F32))
        dsc = dsc_ref[...]
        dscb = dsc.astype(BF16)
        d_last = jnp.sum(jnp.sum(dsc * sprev, axis=1, keepdims=True) * cdec, axis=2, keepdims=True)
        z = dec * xdt
        dbm = _bdot(z.astype(BF16), dscb, 2, 2)
        dz = _bdot(bm, dscb, 2, 1)
        dxdt = dec * dz
        t2 = jnp.sum(dz * z, axis=2, keepdims=True)
        dcs = dcs - t2
        d_last = d_last + jnp.sum(t2, axis=1, keepdims=True)
        m = g * lmat
        mb = m.astype(BF16)
        dm = _bdot(dyb, xdtb, 2, 2)
        dxdt = dxdt + _bdot(mb, dyb, 1, 1)
        dseg = dm * m
        dcs = dcs + jnp.sum(dseg, axis=2, keepdims=True)
        ones = jnp.ones((nc, CHUNK, SSD_P), F32)
        dcs = dcs - _bdot_sum(dseg, ones, 1, 1, 0)
        dg = (dm * lmat).astype(BF16)
        dcm = dcm + _bdot(dg, bm, 2, 1)
        dbm = dbm + _bdot(dg, cm, 1, 1)
        dcs = dcs + jnp.where(li[:, :, :SSD_P] == CHUNK - 1, d_last, 0.0)
        triu = jnp.where(li <= si, 1.0, 0.0).astype(F32)
        dadt = _bdot_sum(triu, dcs, 2, 1, 1)
        dk = dk_ref[0]
        _place_head(dxs_ref, (dxdt * dt + dk * dy).reshape(t, SSD_P), place_ref, h)
        ddt = jnp.sum(dxdt * x, axis=2, keepdims=True) + dadt * a_ref[0]
        mine = lax.broadcasted_iota(jnp.int32, (t, LANES), 1) == h

        @pl.when(h == 0)
        def _():
            ddt_ref[...] = jnp.zeros_like(ddt_ref)
            dadt_ref[...] = jnp.zeros_like(dadt_ref)

        ddt_ref[...] += jnp.where(mine, jnp.max(ddt, axis=2, keepdims=True).reshape(t, 1), 0.0)
        dadt_ref[...] += jnp.where(mine, jnp.max(dadt, axis=2, keepdims=True).reshape(t, 1), 0.0)
        dd_ref[0] = jnp.sum(jnp.sum(dy * x, axis=1, keepdims=True), axis=0)

        @pl.when(pl.program_id(1) == 0)
        def _():
            db_ref[...] = jnp.zeros_like(db_ref)
            dc_ref[...] = jnp.zeros_like(dc_ref)

        db_ref[...] += dbm.reshape(t, SSD_N)
        dc_ref[...] += dcm.reshape(t, SSD_N)

    head = pl.BlockSpec((1, t, SSD_P), lambda gi, hi: (gi * hpg + hi, 0, 0))
    pair = pl.BlockSpec((t, 2 * SSD_P), lambda gi, hi: (0, (gi * hpg + hi) // 2))
    grp = pl.BlockSpec((t, SSD_N), lambda gi, hi: (0, gi))
    lane = pl.BlockSpec((1, 1, SSD_P), lambda gi, hi: (gi * hpg + hi, 0, 0))
    rows = pl.BlockSpec((t, LANES), lambda gi, hi: (0, 0))
    nxb = D_SSM // SSD_N
    dxs, ddt, dadt, db, dc, dd = pl.pallas_call(
        body, name=name, grid=(SSD_GROUPS, hpg),
        in_specs=[pair, head, head, pl.BlockSpec((1, nc, 1, CHUNK), lambda gi, hi: (gi * hpg + hi, 0, 0, 0)),
                  pl.BlockSpec((t, SSD_N), lambda gi, hi: (0, nxb + gi)),
                  pl.BlockSpec((t, SSD_N), lambda gi, hi: (0, nxb + SSD_GROUPS + gi)), lane, lane,
                  pl.BlockSpec((1, nc, SSD_N, SSD_P), lambda gi, hi: (gi * hpg + hi, 0, 0, 0)), pair,
                  pl.BlockSpec((2, 2 * SSD_P, SSD_P), lambda gi, hi: (0, 0, 0)),
                  pl.BlockSpec((2, SSD_P, 2 * SSD_P), lambda gi, hi: (0, 0, 0))],
        out_specs=[pair, rows, rows, grp, grp, lane],
        out_shape=[jax.ShapeDtypeStruct((t, D_SSM), F32)] + [jax.ShapeDtypeStruct((t, LANES), F32)] * 2
        + [jax.ShapeDtypeStruct((t, SSD_GROUPS * SSD_N), F32)] * 2
        + [jax.ShapeDtypeStruct((SSD_HEADS, 1, SSD_P), F32)],
        scratch_shapes=[pltpu.VMEM((nc, SSD_N, SSD_P), F32), pltpu.VMEM((nc, SSD_N, SSD_P), F32),
                        pltpu.VMEM((nc, 1, SSD_P), F32)],
        compiler_params=_cp("arbitrary", "arbitrary"),
    )(xbc, dt_h, cs_h, cs_row, xbc, xbc, dskip_h, a_h, states, dy, pick, place)
    return jnp.concatenate([dxs, db, dc], axis=1), ddt, dadt, dd


def _ssd_gate_fwd(y, proj, w, *, tr=256, name):
    t = y.shape[0]
    gw = D_SSM // SSD_GROUPS

    def body(y_ref, z_ref, w_ref, o_ref):
        v = y_ref[...] * _silu(z_ref[...])
        for gi in range(SSD_GROUPS):
            vg = v[:, gi * gw:(gi + 1) * gw]
            r = lax.rsqrt(jnp.mean(vg * vg, axis=-1, keepdims=True) + NORM_EPS)
            o_ref[:, gi * gw:(gi + 1) * gw] = (vg * r * w_ref[:, gi * gw:(gi + 1) * gw]).astype(BF16)

    blk = pl.BlockSpec((tr, D_SSM), lambda i: (i, 0))
    return pl.pallas_call(
        body, name=name, grid=(t // tr,), in_specs=[blk, blk, pl.BlockSpec((1, D_SSM), lambda i: (0, 0))],
        out_specs=blk, out_shape=jax.ShapeDtypeStruct((t, D_SSM), BF16), compiler_params=_cp("parallel"),
    )(y, proj, w)


def _ssd_gate_bwd(y, proj, w, dcat, *, tr=256, name):
    t = y.shape[0]
    gw = D_SSM // SSD_GROUPS

    def body(y_ref, z_ref, w_ref, d_ref, dy_ref, dz_ref, dw_ref):
        yv, zv, dv = y_ref[...], z_ref[...], d_ref[...].astype(F32)
        sz = _silu(zv)
        v = yv * sz

        @pl.when(pl.program_id(0) == 0)
        def _():
            dw_ref[...] = jnp.zeros_like(dw_ref)

        for gi in range(SSD_GROUPS):
            sl = slice(gi * gw, (gi + 1) * gw)
            vg, dg = v[:, sl], dv[:, sl]
            r = lax.rsqrt(jnp.mean(vg * vg, axis=-1, keepdims=True) + NORM_EPS)
            vh = vg * r
            gg = dg * w_ref[:, sl]
            dvg = r * (gg - vh * jnp.mean(gg * vh, axis=-1, keepdims=True))
            dy_ref[:, sl] = dvg * sz[:, sl]
            dz_ref[:, sl] = (dvg * yv[:, sl] * _dsilu(zv[:, sl])).astype(BF16)
            dw_ref[:, sl] += jnp.sum(dg * vh, axis=0, keepdims=True)

    blk = pl.BlockSpec((tr, D_SSM), lambda i: (i, 0))
    row = pl.BlockSpec((1, D_SSM), lambda i: (0, 0))
    return pl.pallas_call(
        body, name=name, grid=(t // tr,), in_specs=[blk, blk, row, blk], out_specs=[blk, blk, row],
        out_shape=[jax.ShapeDtypeStruct((t, D_SSM), F32), jax.ShapeDtypeStruct((t, D_SSM), BF16),
                   jax.ShapeDtypeStruct((1, D_SSM), F32)],
        compiler_params=_cp("arbitrary"),
    )(y, proj, w, dcat)


def _pad_lanes(v):
    return jnp.pad(v, ((0, 0), (0, LANES - v.shape[1])))


def _per_head(v128, t):
    return jnp.broadcast_to(v128[:, :SSD_HEADS].T[:, :, None], (SSD_HEADS, t, SSD_P))


def _ssd_forward(proj, conv_w, conv_b, dt_bias, a_log, d_skip, ssd_norm_w):
    t = proj.shape[0]
    nc = t // CHUNK
    xbc = _conv_act_fwd(proj, conv_w, conv_b, kw=SSD_CONV, glu=False, tc=512, coff=OFF_XBC // 512,
                        ncols=SSD_CONV_DIM, out_dtype=F32, name="ssd_conv_fwd")
    bias128, alog128 = _pad_lanes(dt_bias), _pad_lanes(a_log)
    dt128, cs128, a128 = _ssd_prep(proj, bias128, alog128, name="ssd_prep")
    dt_h, cs_h = _per_head(dt128, t), _per_head(cs128, t)
    cs_row = cs128[:, :SSD_HEADS].T.reshape(SSD_HEADS, nc, 1, CHUNK)
    dskip_h = jnp.broadcast_to(d_skip[0][:, None, None], (SSD_HEADS, 1, SSD_P))
    a_h = jnp.broadcast_to(a128[0, :SSD_HEADS][:, None, None], (SSD_HEADS, 1, SSD_P))
    y, states = _ssd_fwd(xbc, dt_h, cs_h, cs_row, dskip_h, name="ssd_scan_fwd")
    y_ssd = _ssd_gate_fwd(y, proj, ssd_norm_w, name="ssd_gate_fwd")
    saved = (proj, conv_w, conv_b, ssd_norm_w, bias128, dt128, a128, dt_h, cs_h, cs_row, xbc, dskip_h, a_h, states, y)
    return y_ssd, saved


def _ssd_backward(saved, dcat):
    proj, conv_w, conv_b, ssd_norm_w, bias128, dt128, a128, dt_h, cs_h, cs_row, xbc, dskip_h, a_h, states, y = saved
    dy, dz, d_norm_w = _ssd_gate_bwd(y, proj, ssd_norm_w, dcat, name="ssd_gate_bwd")
    dxc, ddt128, dadt128, dd_h = _ssd_bwd(xbc, dt_h, cs_h, cs_row, dskip_h, a_h, states, dy, name="ssd_scan_bwd")
    dxbc, d_conv_w, d_conv_b = _conv_act_bwd(proj, conv_w, conv_b, dxc, kw=SSD_CONV, glu=False, tc=512,
                                             coff=OFF_XBC // 512, ncols=SSD_CONV_DIM, name="ssd_conv_bwd")
    d_raw, d_bias, d_alog, d_dskip = _ssd_prep_bwd(ddt128, dadt128, proj, bias128, dt128, a128,
                                                   dd_h.reshape(SSD_HEADS, SSD_P), name="ssd_prep_bwd")
    return (dz, dxbc, d_raw, d_norm_w, d_conv_w, d_conv_b, d_bias[:, :SSD_HEADS], d_alog[:, :SSD_HEADS],
            d_dskip.reshape(1, SSD_HEADS))


def _rope_tables(positions):
    inv_freq = ROPE_THETA ** (-jnp.arange(0, MLA_ROPE, 2, dtype=F32) / MLA_ROPE)
    ang = positions[0].astype(F32)[:, None] * inv_freq
    cos, sin = jnp.cos(ang), jnp.sin(ang)
    z = jnp.zeros_like(cos)
    return jnp.stack([jnp.concatenate([cos, cos, z, z], axis=1), jnp.concatenate([-sin, z, z, z], axis=1),
                      jnp.concatenate([z, sin, z, z], axis=1)])


def _mla_forward(proj, tabs, q_a_norm_w, wq_pad, kv_a_norm_w, wkv):
    qn = _rmsnorm_fwd(proj, q_a_norm_w, width=MLA_Q_RANK, cblk=OFF_QA // MLA_Q_RANK, name="q_a_norm")
    q = _matmul(qn, wq_pad, name="q_b_proj")
    kvn = _rmsnorm_fwd(proj, kv_a_norm_w, width=MLA_KV_RANK, cblk=OFF_CKV // MLA_KV_RANK, name="kv_a_norm")
    kv = _matmul(kvn, wkv, name="kv_b_proj")
    q3, k3, v3, vt4 = _mla_prep(q, kv, proj, tabs, name="mla_prep")
    o, lse = _attn_fwd(q3, k3, vt4, name="attn_fwd")
    return o, (proj, tabs, q_a_norm_w, wq_pad, kv_a_norm_w, wkv, qn, kvn, q3, k3, v3, o, lse)


def _mla_backward(saved, dcat):
    proj, tabs, q_a_norm_w, wq_pad, kv_a_norm_w, wkv, qn, kvn, q3, k3, v3, o, lse = saved
    dq3, dk3, dv3 = _attn_bwd(q3, k3, v3, o, dcat, lse, name="attn_bwd")
    dq, dkv, dkr = _mla_unprep(dq3, dk3, dv3, tabs, name="mla_unprep")
    d_wq = _matmul(qn, dq, ta=True, out_dtype=BF16, name="d_w_q_b")
    dqn = _matmul(dq, wq_pad, tb=True, name="d_qn")
    dq_a, d_qnw = _rmsnorm_bwd(proj, q_a_norm_w, dqn, width=MLA_Q_RANK, cblk=OFF_QA // MLA_Q_RANK, out_dtype=BF16,
                               name="q_a_norm_bwd")
    d_wkv = _matmul(kvn, dkv, ta=True, out_dtype=BF16, name="d_w_kv_b")
    dkvn = _matmul(dkv, wkv, tb=True, name="d_kvn")
    dckv, d_kvnw = _rmsnorm_bwd(proj, kv_a_norm_w, dkvn, width=MLA_KV_RANK, cblk=OFF_CKV // MLA_KV_RANK,
                                out_dtype=BF16, name="kv_a_norm_bwd")
    return dq_a, dckv, dkr, d_wq, d_wkv, d_qnw, d_kvnw


def _pad_w_q(w):
    r = w.shape[0]
    w3 = w.reshape(r, MLA_HEADS, MLA_NOPE + MLA_ROPE)
    return jnp.pad(w3, ((0, 0), (0, 0), (0, MLA_QK_PAD - MLA_NOPE - MLA_ROPE))).reshape(r, MLA_HEADS * MLA_QK_PAD)


def _unpad_w_q(w):
    r = w.shape[0]
    return w.reshape(r, MLA_HEADS, MLA_QK_PAD)[:, :, :MLA_NOPE + MLA_ROPE].reshape(r, MLA_HEADS * (MLA_NOPE + MLA_ROPE))


W_IN_SEGMENTS = ((0, D_SSM + SSD_CONV_DIM, 0), (D_SSM + SSD_CONV_DIM, D_SSM + SSD_CONV_DIM + SSD_HEADS, OFF_DT),
                 (D_SSM + SSD_CONV_DIM + SSD_HEADS, D_IN - MLA_ROPE, OFF_QA), (D_IN - MLA_ROPE, D_IN, OFF_KR))


def _pad_w_in_shards(g):
    n = g.shape[2]
    pieces, at = [], 0
    for lo, hi, start in sorted(W_IN_SEGMENTS, key=lambda seg: seg[2]):
        if start > at:
            pieces.append(jnp.zeros((g.shape[1], start - at), g.dtype))
        for j in range(N_DEV):
            a, b = max(lo, j * n), min(hi, (j + 1) * n)
            if a < b:
                pieces.append(g[j][:, a - j * n:b - j * n])
        at = start + hi - lo
    pieces.append(jnp.zeros((g.shape[1], D_IN_PAD - at), g.dtype))
    return jnp.concatenate(pieces, axis=1)


def _unpad_w_in_shards(w):
    n = D_IN // N_DEV
    shards = []
    for j in range(N_DEV):
        pieces = []
        for lo, hi, start in W_IN_SEGMENTS:
            a, b = max(lo, j * n), min(hi, (j + 1) * n)
            if a < b:
                pieces.append(w[:, start + a - lo:start + b - lo])
        shards.append(jnp.concatenate(pieces, axis=1) if len(pieces) > 1 else pieces[0])
    return jnp.stack(shards)


WEIGHTS = ['mix_norm_w', 'w_in', 'conv_w', 'conv_b', 'dt_bias', 'a_log', 'd_skip', 'ssd_norm_w', 'q_a_norm_w', 'w_q_b',
           'kv_a_norm_w', 'w_kv_b', 'w_out', 'ffn_norm_w', 'w_ffn_up', 'ffn_conv_w', 'ffn_conv_b', 'w_ffn_down',
           'ple_norm_w', 'w_ple_gate', 'b_ple_gate', 'w_ple_proj', 'ple_post_norm_w', 'final_norm_w']
BIG = ['w_in', 'w_q_b', 'w_kv_b', 'w_out', 'w_ffn_up', 'w_ffn_down', 'w_ple_gate', 'w_ple_proj']
COL_SHARDED = ('w_in', 'w_q_b', 'w_kv_b', 'w_ffn_up', 'w_ple_proj')
CONV = ['conv_w', 'ffn_conv_w']
REPL = [n for n in WEIGHTS if n not in BIG and n not in CONV]
FFN_INV = tuple(int(i) for i in np.argsort(FFN_PERM))


def _cat_cols(g):
    return jnp.concatenate([g[j] for j in range(N_DEV)], axis=1)


def _split_cols(w):
    n = w.shape[1] // N_DEV
    return jnp.stack([w[:, j * n:(j + 1) * n] for j in range(N_DEV)])


def _interleave(v):
    r = v.shape[0]
    return v.reshape(r, N_DEV, FFN_TC)[:, jnp.array(FFN_PERM)].reshape(r, N_DEV * FFN_TC)


def _deinterleave(v):
    r = v.shape[0]
    return v.reshape(r, N_DEV, FFN_TC)[:, jnp.array(FFN_INV)].reshape(r, N_DEV * FFN_TC)


def _assemble_weights(g):
    layout = {
        'w_in': _pad_w_in_shards,
        'w_q_b': lambda v: _pad_w_q(_cat_cols(v)),
        'w_kv_b': _cat_cols,
        'w_out': lambda v: v.reshape(D_MODEL, D_MODEL),
        'w_ffn_up': lambda v: v,
        'w_ffn_down': lambda v: v.reshape(D_FF, D_MODEL),
        'w_ple_gate': lambda v: v.reshape(D_MODEL, D_MODEL),
        'w_ple_proj': _cat_cols,
        'conv_w': _cat_cols,
        'ffn_conv_w': lambda v: _interleave(_cat_cols(v)),
    }
    return {n: layout[n](v) for n, v in g.items()}


WEIGHT_GROUPS = {'a': ['w_in', 'w_q_b', 'w_kv_b', 'conv_w'], 'b': ['w_out', 'w_ffn_up', 'ffn_conv_w'],
                 'c': ['w_ffn_down', 'w_ple_gate', 'w_ple_proj']}
GRAD_GROUPS = {'p': ['w_ple_proj', 'w_ple_gate', 'w_ffn_down'], 'r': ['w_ffn_up'], 's': ['w_out'],
               't': ['w_q_b', 'w_kv_b', 'w_in']}


def _ffn_perm(j):
    return (j % 2) * (N_DEV // 2) + j // 2


def _local_step(x, p, tabs, get_w, s, target, emit, relay, settle):
    t = x.shape[0]
    s = dict(s)
    half = D_MODEL // 2
    up_cols = 2 * D_FF
    ffn_conv_b = _interleave(s['ffn_conv_b'])
    w = dict(get_w('a', None))
    h = _rmsnorm_fwd(x, s['mix_norm_w'], width=D_MODEL, name="mix_norm")
    proj = _matmul(h, w['w_in'], name="in_proj")
    y_ssd, ssd_saved = _ssd_forward(proj, w['conv_w'], s['conv_b'], s['dt_bias'], s['a_log'], s['d_skip'],
                                    s['ssd_norm_w'])
    o, mla_saved = _mla_forward(proj, tabs, s['q_a_norm_w'], w['w_q_b'], s['kv_a_norm_w'], w['w_kv_b'])
    tk_o, tn_o = _tile(half, MM_TK), _tile(D_MODEL, MM_TILE)
    w.update(get_w('b', o))
    x1 = _matmul(y_ssd, w['w_out'], add=x, mnk=(t, D_MODEL, half), name="out_proj_ssd")
    x1 = _matmul(o, w['w_out'], add=x1, mnk=(t, D_MODEL, half), name="out_proj_mla",
                 b_spec=pl.BlockSpec((tk_o, tn_o), lambda i, j, kk: (kk + half // tk_o, j)))
    hf = _rmsnorm_fwd(x1, s['ffn_norm_w'], width=D_MODEL, name="ffn_norm")
    tk_u = _tile(D_MODEL, MM_TK)
    u = _matmul(hf, w['w_ffn_up'], mnk=(t, up_cols, D_MODEL), tn=FFN_TC, name="ffn_up",
                b_spec=pl.BlockSpec((1, tk_u, FFN_TC), lambda i, j, kk: (_ffn_perm(j), kk, 0)))
    act = _conv_act_fwd(u, w['ffn_conv_w'], ffn_conv_b, kw=FFN_CONV, glu=True, tc=2 * FFN_TC, coff=0, ncols=up_cols,
                        out_dtype=BF16, name="ffn_act")
    w.update(get_w('c', act))
    x2 = _matmul(act, w['w_ffn_down'], add=x1, name="ffn_down")
    hp = _rmsnorm_fwd(x2, s['ple_norm_w'], width=D_MODEL, name="ple_norm")
    gl = _matmul(hp, w['w_ple_gate'], bias=s['b_ple_gate'], name="ple_gate")
    pe = _matmul(p, w['w_ple_proj'], name="ple_proj")
    loss, dx3, d_final, dgl, d_bgate, dpe, d_post = _ple_loss(x2, gl, pe, s['ple_post_norm_w'], s['final_norm_w'],
                                                              target, name="ple_loss")
    d_wproj = _matmul(p, dpe, ta=True, out_dtype=BF16, name="d_w_ple_proj")
    d_wgate = _matmul(hp, dgl, ta=True, out_dtype=BF16, name="d_w_ple_gate")
    dhp = _matmul(dgl, w['w_ple_gate'], tb=True, name="d_ple_normed")
    dx2, d_plenorm, dx2b = _rmsnorm_bwd(x2, s['ple_norm_w'], dhp, dx3, width=D_MODEL, also_bf16=True,
                                        name="ple_norm_bwd")
    dact = _matmul(dx2b, w['w_ffn_down'], tb=True, name="d_ffn_act")
    d_wdown = _matmul(act, dx2b, ta=True, out_dtype=BF16, name="d_w_ffn_down")
    zz = emit('p', {'w_ple_proj': _split_cols(d_wproj), 'w_ple_gate': d_wgate.reshape(N_DEV, D_MODEL // N_DEV, D_MODEL),
                    'w_ffn_down': d_wdown.reshape(N_DEV, D_FF // N_DEV, D_MODEL)})
    du, d_fconv_w, d_fconv_b = _conv_act_bwd(u, w['ffn_conv_w'], ffn_conv_b + zz, dact, kw=FFN_CONV, glu=True,
                                             tc=2 * FFN_TC, coff=0, ncols=up_cols, name="ffn_act_bwd")
    zz = zz + relay('p', du)
    tm_u = _tile(D_MODEL, MM_TILE)
    d_wup = _matmul(hf, du, ta=True, out_dtype=BF16, mnk=(D_MODEL, up_cols, t), tn=FFN_TC, name="d_w_ffn_up",
                    o_spec=pl.BlockSpec((1, tm_u, FFN_TC), lambda i, j, kk: (_ffn_perm(j), i, 0)),
                    o_shape=(N_DEV, D_MODEL, FFN_TC))
    zz = zz + emit('r', {'w_ffn_up': d_wup})
    zero_row = jnp.zeros((1, D_MODEL), F32)
    dhf = _matmul(du, w['w_ffn_up'], tb=True, mnk=(t, D_MODEL, up_cols), tm=t, tk=FFN_TC, name="d_ffn_normed",
                  bias=zero_row + zz,
                  b_spec=pl.BlockSpec((1, tn_o, FFN_TC), lambda i, j, kk: (_ffn_perm(kk), j, 0)))
    zz = zz + relay('r', dhf) + settle('p')
    dx1, d_ffnnorm, dx1b = _rmsnorm_bwd(x1, s['ffn_norm_w'] + zz, dhf, dx2, width=D_MODEL, also_bf16=True,
                                        name="ffn_norm_bwd")
    dcat = _matmul(dx1b, w['w_out'], tb=True, name="d_mixed")
    d_wout = jnp.concatenate([_matmul(y_ssd, dx1b, ta=True, out_dtype=BF16, name="d_w_out_ssd"),
                              _matmul(o, dx1b, ta=True, out_dtype=BF16, name="d_w_out_mla")], axis=0)
    zz = zz + emit('s', {'w_out': d_wout.reshape(N_DEV, D_MODEL // N_DEV, D_MODEL)})
    ssd_saved = ssd_saved[:3] + (ssd_saved[3] + zz,) + ssd_saved[4:]
    dz, dxbc, d_raw, d_ssdnorm, d_conv_w, d_conv_b, d_dtb, d_alog, d_dskip = _ssd_backward(ssd_saved, dcat)
    zz = zz + relay('s', dz)
    mla_saved = mla_saved[:-1] + (mla_saved[-1] + zz,)
    dq_a, dckv, dkr, d_wq, d_wkv, d_qnorm, d_kvnorm = _mla_backward(mla_saved, dcat)
    d_raw = (d_raw + settle('r')).astype(BF16)
    dproj = jnp.concatenate([dz, dxbc, dq_a, dckv, dkr, d_raw], axis=1)
    d_win = _matmul(h, dproj, ta=True, out_dtype=BF16, name="d_w_in")
    zz = emit('t', {'w_in': _unpad_w_in_shards(d_win), 'w_q_b': _split_cols(_unpad_w_q(d_wq)),
                    'w_kv_b': _split_cols(d_wkv)}) + settle('s')
    dh = _matmul(dproj, w['w_in'], tb=True, bias=zero_row + zz, name="d_in_normed")
    zz = relay('t', dh)
    dx, d_mixnorm = _rmsnorm_bwd(x, s['mix_norm_w'] + zz, dh, dx1, width=D_MODEL, name="mix_norm_bwd")
    conv = {'conv_w': d_conv_w, 'ffn_conv_w': _deinterleave(d_fconv_w)}
    vec = {
        'mix_norm_w': d_mixnorm, 'conv_b': d_conv_b, 'dt_bias': d_dtb, 'a_log': d_alog, 'd_skip': d_dskip,
        'ssd_norm_w': d_ssdnorm, 'q_a_norm_w': d_qnorm, 'kv_a_norm_w': d_kvnorm, 'ffn_norm_w': d_ffnnorm,
        'ffn_conv_b': _deinterleave(d_fconv_b), 'ple_norm_w': d_plenorm, 'b_ple_gate': d_bgate,
        'ple_post_norm_w': d_post, 'final_norm_w': d_final,
    }
    return loss, dx, conv, vec


MESH = pl.DeviceIdType.MESH
FLIPS = ((0, 0, 1), (1, 0, 0), (0, 1, 0), (1, 1, 0), (1, 0, 1), (0, 1, 1), (1, 1, 1))


def _exchange(items, *, gather, name):
    n = len(items)

    def body(*refs):
        ins, outs = refs[:n], refs[n:2 * n]
        send_sems, recv_sems, local_sems = refs[2 * n:]
        x, y, c = lax.axis_index("x"), lax.axis_index("y"), lax.axis_index("c")
        me = 4 * x + 2 * y + c
        peers = [(jnp.where(fx, 1 - x, x), jnp.where(fy, 1 - y, y), jnp.where(fc, 1 - c, c)) for fx, fy, fc in FLIPS]
        slot = [4 * px + 2 * py + pc for px, py, pc in peers]
        local, sends = [], []
        for wi in range(n):
            cp = pltpu.make_async_copy(ins[wi] if gather else ins[wi].at[me], outs[wi].at[me], local_sems.at[wi])
            cp.start()
            local.append(cp)
            for k, peer in enumerate(peers):
                cp = pltpu.make_async_remote_copy(
                    src_ref=ins[wi] if gather else ins[wi].at[slot[k]], dst_ref=outs[wi].at[me],
                    send_sem=send_sems.at[k, wi], recv_sem=recv_sems.at[k, wi], device_id=peer, device_id_type=MESH)
                cp.start()
                sends.append(cp)
        for wi in range(n):
            for k, peer in enumerate(peers):
                pltpu.make_async_remote_copy(
                    src_ref=outs[wi].at[slot[k]], dst_ref=outs[wi].at[slot[k]], send_sem=send_sems.at[k, wi],
                    recv_sem=recv_sems.at[k, wi], device_id=peer, device_id_type=MESH).wait_recv()
        for cp in sends:
            cp.wait_send()
        for cp in local:
            cp.wait()

    hbm = pl.BlockSpec(memory_space=pltpu.HBM)
    out_shape = [jax.ShapeDtypeStruct(((N_DEV,) + v.shape) if gather else v.shape, v.dtype) for v in items]
    return pl.pallas_call(
        body, name=name, in_specs=[hbm] * n, out_specs=[hbm] * n, out_shape=out_shape,
        scratch_shapes=[pltpu.SemaphoreType.DMA((len(FLIPS), n)), pltpu.SemaphoreType.DMA((len(FLIPS), n)),
                        pltpu.SemaphoreType.DMA((n,))],
    )(*items)


HBM_SPEC = pl.BlockSpec(memory_space=pltpu.HBM)
SEM_SPEC = pl.BlockSpec(memory_space=pltpu.SEMAPHORE)
EFFECT = pltpu.SideEffectType.DATAFLOW_SIDE_EFFECTING


def _split_start(bufs, ncopies, plan, *, name):
    nb = len(bufs)

    def body(*refs):
        send_sems, recv_sems, token = refs[nb], refs[nb + 1], refs[2 * nb + 2]
        for i, (src, dst, peer, _) in enumerate(plan(refs[:nb])):
            pltpu.make_async_remote_copy(src_ref=src, dst_ref=dst, send_sem=send_sems.at[i], recv_sem=recv_sems.at[i],
                                         device_id=peer, device_id_type=MESH).start()
        token[...] = jnp.zeros_like(token)

    res = pl.pallas_call(
        body, name=name, in_specs=[HBM_SPEC] * nb,
        out_specs=[SEM_SPEC, SEM_SPEC] + [HBM_SPEC] * nb + [pl.BlockSpec(memory_space=pltpu.VMEM)],
        out_shape=[pltpu.SemaphoreType.DMA((ncopies,)), pltpu.SemaphoreType.DMA((ncopies,))]
        + [pltpu.HBM(v.shape, v.dtype) for v in bufs] + [jax.ShapeDtypeStruct((HALO, LANES), F32)],
        input_output_aliases={i: 2 + i for i in range(nb)},
        compiler_params=pltpu.CompilerParams(has_side_effects=EFFECT),
    )(*[pltpu.with_memory_space_constraint(v, pltpu.HBM) for v in bufs])
    return (res[0], res[1], list(res[2:2 + nb])), res[2 + nb]


def _split_wait(started, after, plan, local_plan, *, name):
    send_sems, recv_sems, bufs = started
    nb = len(bufs)
    nlocal = len(local_plan(bufs))

    def body(*refs):
        send_sems, recv_sems = refs[nb], refs[nb + 1]
        local_sems = refs[2 * nb + 3]
        local = []
        for j, (src, dst) in enumerate(local_plan(refs[:nb])):
            cp = pltpu.make_async_copy(src, dst, local_sems.at[j])
            cp.start()
            local.append(cp)
        for i, (src, _, peer, incoming) in enumerate(plan(refs[:nb])):
            cp = pltpu.make_async_remote_copy(src_ref=src, dst_ref=incoming, send_sem=send_sems.at[i],
                                              recv_sem=recv_sems.at[i], device_id=peer, device_id_type=MESH)
            cp.wait_send()
            cp.wait_recv()
        for cp in local:
            cp.wait()

    res = pl.pallas_call(
        body, name=name, in_specs=[HBM_SPEC] * nb + [SEM_SPEC, SEM_SPEC, pl.BlockSpec(memory_space=pl.ANY)],
        out_specs=[HBM_SPEC] * nb, out_shape=[pltpu.HBM(v.shape, v.dtype) for v in bufs],
        input_output_aliases={i: i for i in range(nb)},
        scratch_shapes=[pltpu.SemaphoreType.DMA((max(nlocal, 1),))],
        compiler_params=pltpu.CompilerParams(has_side_effects=EFFECT),
    )(*bufs, send_sems, recv_sems, after)
    return list(res)


def _hold(values, after, *, name):
    n = len(values)

    def body(*refs):
        del refs

    return list(pl.pallas_call(
        body, name=name, in_specs=[HBM_SPEC] * n + [pl.BlockSpec(memory_space=pl.ANY)], out_specs=[HBM_SPEC] * n,
        out_shape=[pltpu.HBM(v.shape, v.dtype) for v in values], input_output_aliases={i: i for i in range(n)},
    )(*values, after))


def _place():
    x, y, c = lax.axis_index("x"), lax.axis_index("y"), lax.axis_index("c")
    others = [((1 - x, y, c), 2 * (1 - x) + y), ((x, 1 - y, c), 2 * x + 1 - y), ((1 - x, 1 - y, c), 2 * (1 - x) + 1 - y)]
    return 4 * x + 2 * y + c, 2 * x + y, c, (x, y, 1 - c), others


def _gather1_plan(n):
    def plan(refs):
        me, _, _, sibling, others = _place()
        out = []
        for wi in range(n):
            item, land = refs[wi], refs[n + wi]
            out.append((item, land.at[me], sibling, land.at[me + 1 - 2 * lax.axis_index("c")]))
            for peer, chip in others:
                out.append((item, land.at[me], peer, land.at[2 * chip + lax.axis_index("c")]))
        return out

    return plan


def _gather1_local(n):
    def plan(refs):
        me = _place()[0]
        return [(refs[wi], refs[n + wi].at[me]) for wi in range(n)]

    return plan


def _gather2_plan(n):
    def plan(refs):
        _, _, c, sibling, others = _place()
        out = []
        for wi in range(n):
            land = refs[wi]
            for _, chip in others:
                out.append((land.at[2 * chip + c], land.at[2 * chip + c], sibling, land.at[2 * chip + 1 - c]))
        return out

    return plan


def _gather_start(items, *, name):
    lands = [lax.empty((N_DEV,) + v.shape, v.dtype) for v in items]
    return _split_start(items + lands, 4 * len(items), _gather1_plan(len(items)), name=name)


def _gather_forward(started, after, *, name):
    n = len(started[2]) // 2
    bufs = _split_wait(started, after, _gather1_plan(n), _gather1_local(n), name=name + "_wait")
    return _split_start(bufs[n:], 3 * n, _gather2_plan(n), name=name + "_start")


def _gather_finish(started, after, *, name):
    n = len(started[2])
    return _split_wait(started, after, _gather2_plan(n), lambda refs: [], name=name)


def _handshake(peers):
    barrier = pltpu.get_barrier_semaphore()
    for peer in peers:
        pl.semaphore_signal(barrier, inc=1, device_id=peer, device_id_type=MESH)
    pl.semaphore_wait(barrier, len(peers))


def _remote(src, dst, send_sem, recv_sem, peer):
    return pltpu.make_async_remote_copy(src_ref=src, dst_ref=dst, send_sem=send_sem, recv_sem=recv_sem, device_id=peer,
                                        device_id_type=MESH)


def _sequencer_gather(items, *, collective_id, name):
    n = len(items)
    srcs = [jax.new_ref(v, memory_space=pltpu.MemorySpace.HBM) for v in items]
    lands = [jax.empty_ref(jax.ShapeDtypeStruct((N_DEV,) + v.shape, v.dtype), memory_space=pltpu.MemorySpace.HBM)
             for v in items]
    dma = pltpu.SemaphoreType.DMA

    @pl.kernel(mesh=plsc.ScalarSubcoreMesh(axis_name="sequencer", num_cores=1), name=name,
               scratch_types=(dma((4 * n,)), dma((4 * n,)), dma((3 * n,)), dma((3 * n,)), dma((n,))),
               compiler_params=pltpu.CompilerParams(collective_id=collective_id))
    def launch(send1, recv1, send2, recv2, local_sems):
        _, _, _, sibling, others = _place()
        _handshake([sibling] + [peer for peer, _ in others])
        hop1 = _gather1_plan(n)(srcs + lands)
        hop2 = _gather2_plan(n)(lands)
        local = [pltpu.make_async_copy(src, dst, local_sems.at[j])
                 for j, (src, dst) in enumerate(_gather1_local(n)(srcs + lands))]
        for cp in local:
            cp.start()
        for i, (src, dst, peer, _) in enumerate(hop1):
            _remote(src, dst, send1.at[i], recv1.at[i], peer).start()
        for wi in range(n):
            for j in range(3):
                i1, i2 = 4 * wi + 1 + j, 3 * wi + j
                src, _, peer, incoming = hop1[i1]
                _remote(src, incoming, send1.at[i1], recv1.at[i1], peer).wait_recv()
                src, dst, peer, _ = hop2[i2]
                _remote(src, dst, send2.at[i2], recv2.at[i2], peer).start()
        for wi in range(n):
            src, _, peer, incoming = hop1[4 * wi]
            _remote(src, incoming, send1.at[4 * wi], recv1.at[4 * wi], peer).wait_recv()
        for i, (src, _, peer, incoming) in enumerate(hop2):
            cp = _remote(src, incoming, send2.at[i], recv2.at[i], peer)
            cp.wait_send()
            cp.wait_recv()
        for i, (src, dst, peer, _) in enumerate(hop1):
            _remote(src, dst, send1.at[i], recv1.at[i], peer).wait_send()
        for cp in local:
            cp.wait()

    launch()
    return [land[...] for land in lands]


def _sequencer_exchange(sources, land_shapes, ncopies, plan, local_plan, peers, *, collective_id, name):
    srcs = [jax.new_ref(v, memory_space=pltpu.MemorySpace.HBM) for v in sources]
    lands = [jax.empty_ref(s, memory_space=pltpu.MemorySpace.HBM) for s in land_shapes]
    nlocal = len(local_plan(srcs + lands))
    dma = pltpu.SemaphoreType.DMA

    @pl.kernel(mesh=plsc.ScalarSubcoreMesh(axis_name="sequencer", num_cores=1), name=name,
               scratch_types=(dma((ncopies,)), dma((ncopies,)), dma((max(nlocal, 1),))),
               compiler_params=pltpu.CompilerParams(collective_id=collective_id))
    def launch(send_sems, recv_sems, local_sems):
        _handshake(peers(_place()))
        copies = plan(srcs + lands)
        local = [pltpu.make_async_copy(src, dst, local_sems.at[j])
                 for j, (src, dst) in enumerate(local_plan(srcs + lands))]
        for cp in local:
            cp.start()
        for i, (src, dst, peer, _) in enumerate(copies):
            _remote(src, dst, send_sems.at[i], recv_sems.at[i], peer).start()
        for i, (src, _, peer, incoming) in enumerate(copies):
            cp = _remote(src, incoming, send_sems.at[i], recv_sems.at[i], peer)
            cp.wait_send()
            cp.wait_recv()
        for cp in local:
            cp.wait()

    launch()
    return [land[...] for land in lands]


def _sequencer_scatter_hop2(sums, *, collective_id, name):
    n = len(sums)
    shapes = [jax.ShapeDtypeStruct(v.shape, v.dtype) for v in sums]
    return _sequencer_exchange(sums, shapes, 3 * n, _scatter2_plan(n), _scatter2_local(n),
                               lambda place: [peer for peer, _ in place[4]], collective_id=collective_id, name=name)


N_CHIP = N_DEV // 2


def _scatter1_plan(n):
    def plan(refs):
        _, _, c, sibling, _ = _place()
        out = []
        for wi in range(n):
            parts, half = refs[wi], refs[n + wi]
            for chip in range(N_CHIP):
                out.append((parts.at[2 * chip + 1 - c], half.at[chip], sibling, half.at[chip]))
        return out

    return plan


def _scatter2_plan(n):
    def plan(refs):
        _, my_chip, _, _, others = _place()
        out = []
        for wi in range(n):
            sums, recv = refs[wi], refs[n + wi]
            for peer, chip in others:
                out.append((sums.at[chip], recv.at[my_chip], peer, recv.at[chip]))
        return out

    return plan


def _scatter2_local(n):
    def plan(refs):
        my_chip = _place()[1]
        return [(refs[wi].at[my_chip], refs[n + wi].at[my_chip]) for wi in range(n)]

    return plan


def _pair_add(parts, half, core, *, name):
    _, r, c = parts.shape
    tr = max(d for d in range(HALO, 257, HALO) if r % d == 0) if r > 256 else r
    parts4 = parts.reshape(N_CHIP, 2, r, c)

    def body(core_ref, p_ref, h_ref, o_ref):
        o_ref[...] = (p_ref[:, 0].astype(F32) + h_ref[...].astype(F32)).astype(o_ref.dtype)

    return pl.pallas_call(
        body, name=name,
        grid_spec=pltpu.PrefetchScalarGridSpec(
            num_scalar_prefetch=1, grid=(r // tr,),
            in_specs=[pl.BlockSpec((N_CHIP, 1, tr, c), lambda i, core_ref: (0, core_ref[0], i, 0)),
                      pl.BlockSpec((N_CHIP, tr, c), lambda i, core_ref: (0, i, 0))],
            out_specs=pl.BlockSpec((N_CHIP, tr, c), lambda i, core_ref: (0, i, 0))),
        out_shape=jax.ShapeDtypeStruct((N_CHIP, r, c), parts.dtype), compiler_params=_cp("parallel"),
    )(core, parts4, half)


def _scatter_start(parts, *, name):
    halves = [lax.empty((N_CHIP,) + v.shape[1:], v.dtype) for v in parts]
    return _split_start(parts + halves, N_CHIP * len(parts), _scatter1_plan(len(parts)), name=name)


def _adamw(parts, w, m, v, *, name):
    r, c = w.shape
    nparts = parts.shape[0]
    tr = max(d for d in range(HALO, 129, HALO) if r % d == 0) if r > 128 else r

    def body(p_ref, w_ref, m_ref, v_ref, g_ref, d_ref, mo_ref, vo_ref):
        g = p_ref[0].astype(F32)
        for k in range(1, nparts):
            g = g + p_ref[k].astype(F32)
        mn = ADAM_B1 * m_ref[...] + (1.0 - ADAM_B1) * g
        vn = ADAM_B2 * v_ref[...] + (1.0 - ADAM_B2) * (g * g)
        m_hat = mn / (1.0 - ADAM_B1 ** ADAM_STEP)
        v_hat = vn / (1.0 - ADAM_B2 ** ADAM_STEP)
        g_ref[...] = g
        d_ref[...] = -ADAM_LR * (m_hat / (jnp.sqrt(v_hat) + ADAM_EPS) + ADAM_WD * w_ref[...])
        mo_ref[...] = mn
        vo_ref[...] = vn

    blk = pl.BlockSpec((tr, c), lambda i: (i, 0))
    return pl.pallas_call(
        body, name=name, grid=(r // tr,), in_specs=[pl.BlockSpec((nparts, tr, c), lambda i: (0, i, 0)), blk, blk, blk],
        out_specs=[blk] * 4, out_shape=[jax.ShapeDtypeStruct((r, c), F32)] * 4, compiler_params=_cp("parallel"),
    )(parts, w, m, v)


def _pack_rows(vs, rows):
    lead = vs[0].shape[:-1] if vs[0].ndim > 1 else ()
    flat = jnp.concatenate(vs, axis=-1)
    pad = rows * LANES - flat.shape[-1]
    flat = jnp.pad(flat, [(0, 0)] * len(lead) + [(0, pad)])
    return flat.reshape(lead + (rows, LANES))


def kernel(x, p, positions, mix_norm_w, w_in, conv_w, conv_b, dt_bias, a_log, d_skip, ssd_norm_w, q_a_norm_w, w_q_b, kv_a_norm_w, w_kv_b, w_out, ffn_norm_w, w_ffn_up, ffn_conv_w, ffn_conv_b, w_ffn_down, ple_norm_w, w_ple_gate, b_ple_gate, w_ple_proj, ple_post_norm_w, final_norm_w, loss_target, m_mix_norm_w, m_w_in, m_conv_w, m_conv_b, m_dt_bias, m_a_log, m_d_skip, m_ssd_norm_w, m_q_a_norm_w, m_w_q_b, m_kv_a_norm_w, m_w_kv_b, m_w_out, m_ffn_norm_w, m_w_ffn_up, m_ffn_conv_w, m_ffn_conv_b, m_w_ffn_down, m_ple_norm_w, m_w_ple_gate, m_b_ple_gate, m_w_ple_proj, m_ple_post_norm_w, m_final_norm_w, v_mix_norm_w, v_w_in, v_conv_w, v_conv_b, v_dt_bias, v_a_log, v_d_skip, v_ssd_norm_w, v_q_a_norm_w, v_w_q_b, v_kv_a_norm_w, v_w_kv_b, v_w_out, v_ffn_norm_w, v_w_ffn_up, v_ffn_conv_w, v_ffn_conv_b, v_w_ffn_down, v_ple_norm_w, v_w_ple_gate, v_b_ple_gate, v_w_ple_proj, v_ple_post_norm_w, v_final_norm_w):
    given = dict(locals())
    shapes = {n: given[n].shape for n in WEIGHTS}
    w2 = {n: given[n].reshape(given[n].shape[-2:] if n in BIG or n in CONV else (1, -1)) for n in WEIGHTS}
    m2 = {n: given['m_' + n].reshape(w2[n].shape) for n in WEIGHTS}
    v2 = {n: given['v_' + n].reshape(w2[n].shape) for n in WEIGHTS}
    me = 4 * lax.axis_index("x") + 2 * lax.axis_index("y") + lax.axis_index("c")

    core = lax.axis_index("c").astype(jnp.int32).reshape(1)

    def shards(grp, zero):
        return [(w2[n] + zero).astype(BF16) if n in BIG else w2[n] + zero for n in WEIGHT_GROUPS[grp]]

    first, token = _gather_start(shards('a', 0.0), name="gather_a_hop1")
    first, token = _gather_forward(first, token, name="gather_a_hop2")
    zero = token[0, 0]
    later = dict(zip(WEIGHT_GROUPS['b'], _sequencer_gather(shards('b', zero), collective_id=1, name="gather_b")))
    later.update(zip(WEIGHT_GROUPS['c'], _sequencer_gather(shards('c', zero), collective_id=2, name="gather_c")))

    def get_w(grp, after):
        if grp == 'a':
            lands = dict(zip(WEIGHT_GROUPS[grp], _gather_finish(first, token, name="gather_a_done")))
        else:
            names = WEIGHT_GROUPS[grp]
            lands = dict(zip(names, _hold([later[n] for n in names], after, name="gather_" + grp + "_use")))
        return _assemble_weights(lands)

    scatters = {}

    hop_ids = {grp: 2 + 2 * i for i, grp in enumerate(GRAD_GROUPS)}

    def zero_of(arrays):
        return sum(v[(0,) * v.ndim].astype(F32) * 0.0 for v in arrays)

    def emit(grp, grads):
        scatters[grp], tok = _scatter_start([grads[n] for n in GRAD_GROUPS[grp]], name="scatter_" + grp + "_hop1")
        return tok[0, 0]

    def relay(grp, after):
        n = len(GRAD_GROUPS[grp])
        bufs = _split_wait(scatters[grp], after, _scatter1_plan(n), lambda refs: [], name="scatter_" + grp + "_hop1_wait")
        sums = [_pair_add(bufs[i], bufs[n + i], core, name="scatter_%s_add%d" % (grp, i)) for i in range(n)]
        scatters[grp] = _sequencer_scatter_hop2(sums, collective_id=hop_ids[grp] + 1, name="scatter_" + grp + "_hop2")
        return zero_of(sums)

    out_g, out_d, out_m, out_v = {}, {}, {}, {}

    def settle(grp):
        return zero_of(scatters[grp])

    def update(grp, behind=None):
        for n, parts in zip(GRAD_GROUPS[grp], scatters[grp]):
            wn = w2[n] if behind is None else w2[n] + behind
            out_g[n], out_d[n], out_m[n], out_v[n] = _adamw(parts, wn, m2[n], v2[n], name="adamw_" + n)

    vecs = {n: w2[n] for n in REPL}
    vecs['mix_norm_w'] = vecs['mix_norm_w'] + zero
    loss, dx, g_conv, g_vec = _local_step(x[0], p[0, 0], _rope_tables(positions), get_w, vecs, loss_target[0], emit,
                                          relay, settle)
    n_small = sum(g_vec[n].shape[1] for n in REPL) + sum(g_conv[n].size for n in CONV) + 1
    rows_small = -(-n_small // (LANES * HALO)) * HALO
    small = _pack_rows([g_vec[n] for n in REPL] + [g_conv[n].reshape(1, -1) for n in CONV] + [loss], rows_small)

    for grp in list(GRAD_GROUPS)[:-1]:
        update(grp)
    all_small = _exchange([small], gather=True, name="gather_small_grads")[0].reshape(N_DEV, rows_small * LANES)
    update(list(GRAD_GROUPS)[-1], zero_of([all_small]))
    pieces, off = [], 0
    for n in REPL:
        k = g_vec[n].shape[1]
        pieces.append(all_small[:, off:off + k])
        off += k
    for n in CONV:
        kw, cols = g_conv[n].shape
        full = all_small[:, off:off + kw * cols].reshape(N_DEV, kw, cols)
        mine = lax.dynamic_slice_in_dim(full, me * (cols // N_DEV), cols // N_DEV, axis=2)
        pieces.append(mine.reshape(N_DEV, kw * (cols // N_DEV)))
        off += kw * cols
    pieces.append(all_small[:, off:off + 1])
    small_names = REPL + CONV
    n_mine = sum(q.shape[1] for q in pieces)
    rows_mine = -(-n_mine // (LANES * HALO)) * HALO
    zero = jnp.zeros((1, 1), F32)
    packed = [_pack_rows([src[n].reshape(1, -1) for n in small_names] + [zero], rows_mine).reshape(rows_mine, LANES)
              for src in (w2, m2, v2)]
    sg, sd, sm, sv = _adamw(_pack_rows(pieces, rows_mine), *packed, name="adamw_small")
    off = 0
    for n in small_names:
        k = w2[n].size
        for dst, src in ((out_g, sg), (out_d, sd), (out_m, sm), (out_v, sv)):
            dst[n] = src.reshape(-1)[off:off + k].reshape(w2[n].shape)
        off += k
    total_loss = sg.reshape(-1)[off]

    outs = [total_loss, dx[None]]
    for res in (out_g, out_d, out_m, out_v):
        outs += [res[n].reshape(shapes[n]) for n in WEIGHTS]
    return tuple(outs)
```

```python
import math

import numpy as np
import jax
import jax.numpy as jnp
from jax import lax
from jax.experimental import pallas as pl
from jax.experimental.pallas import tpu as pltpu
from jax.experimental.pallas import tpu_sc as plsc

F32 = jnp.float32
BF16 = jnp.bfloat16
HI = lax.Precision.HIGHEST

D_MODEL = 2048
CHUNK = 64
D_SSM = 1024
SSD_P = 64
SSD_HEADS = 16
SSD_GROUPS = 2
SSD_N = 128
SSD_CONV = 4
SSD_CONV_DIM = D_SSM + 2 * SSD_GROUPS * SSD_N
MLA_HEADS = 8
MLA_NOPE = 128
MLA_ROPE = 64
MLA_V = 128
MLA_Q_RANK = 512
MLA_KV_RANK = 256
MLA_QK_PAD = 256
ROPE_THETA = 10000.0
D_FF = 5632
FFN_CONV = 3
PLE_DIM = 256
NORM_EPS = 1e-6
ADAM_LR, ADAM_B1, ADAM_B2, ADAM_EPS, ADAM_WD, ADAM_STEP = 0.001, 0.9, 0.999, 1e-08, 0.01, 10
N_DEV = 8

OFF_Z, OFF_XBC, OFF_QA, OFF_CKV, OFF_KR, OFF_DT, D_IN_PAD = 0, 1024, 2560, 3072, 3328, 3456, 3584
D_IN = 3408
LANES = 128
HALO = 8
VMEM_LIMIT = 56 * 1024 * 1024
FFN_TC = D_FF * 2 // N_DEV
FFN_PERM = (0, 4, 1, 5, 2, 6, 3, 7)
NEG = -1e30


def _cp(*sem):
    return pltpu.CompilerParams(dimension_semantics=tuple(sem), vmem_limit_bytes=VMEM_LIMIT)


def _tile(n, want):
    if n <= want:
        return n
    best = max(d for d in range(LANES, want + 1, LANES) if n % d == 0)
    return best


def _sigmoid(x):
    return 0.5 * (jnp.tanh(0.5 * x) + 1.0)


def _silu(x):
    return x * _sigmoid(x)


def _dsilu(x):
    s = _sigmoid(x)
    return s * (1.0 + x * (1.0 - s))


MM_TILE = 1408
MM_TK = 2816


def _matmul(a, b, *, ta=False, tb=False, out_dtype=F32, add=None, bias=None, tm=MM_TILE, tn=MM_TILE, tk=MM_TK, name,
            mnk=None, a_spec=None, b_spec=None, o_spec=None, o_shape=None):
    if mnk is None:
        m, k = (a.shape[1], a.shape[0]) if ta else a.shape
        n = b.shape[0] if tb else b.shape[1]
        assert k == (b.shape[1] if tb else b.shape[0])
    else:
        m, n, k = mnk
    tm, tn, tk = _tile(m, tm), _tile(n, tn), _tile(k, tk)
    nk = k // tk
    dims = (((0 if ta else 1,), (1 if tb else 0,)), ((), ()))

    def body(*refs):
        a_ref, b_ref = refs[0], refs[1]
        pos = 2
        add_ref = bias_ref = None
        if add is not None:
            add_ref = refs[pos]
            pos += 1
        if bias is not None:
            bias_ref = refs[pos]
            pos += 1
        o_ref = refs[pos]
        kk = pl.program_id(2)
        av = a_ref[...]
        bv = b_ref[...]
        av = av.reshape(av.shape[-2:]).astype(BF16)
        bv = bv.reshape(bv.shape[-2:]).astype(BF16)
        prod = lax.dot_general(av, bv, dims, preferred_element_type=F32)

        def finish(r):
            if bias_ref is not None:
                r = r + bias_ref[...]
            if add_ref is not None:
                r = r + add_ref[...].astype(F32)
            o_ref[...] = r.astype(out_dtype).reshape(o_ref.shape)

        if nk == 1:
            finish(prod)
        else:
            acc_ref = refs[pos + 1]

            @pl.when(kk == 0)
            def _():
                acc_ref[...] = prod

            @pl.when(kk > 0)
            def _():
                acc_ref[...] += prod

            @pl.when(kk == nk - 1)
            def _():
                finish(acc_ref[...])

    if a_spec is None:
        a_spec = (pl.BlockSpec((tk, tm), lambda i, j, kk: (kk, i)) if ta
                  else pl.BlockSpec((tm, tk), lambda i, j, kk: (i, kk)))
    if b_spec is None:
        b_spec = (pl.BlockSpec((tn, tk), lambda i, j, kk: (j, kk)) if tb
                  else pl.BlockSpec((tk, tn), lambda i, j, kk: (kk, j)))
    if o_spec is None:
        o_spec = pl.BlockSpec((tm, tn), lambda i, j, kk: (i, j))
    if o_shape is None:
        o_shape = (m, n)
    in_specs = [a_spec, b_spec]
    args = [a, b]
    if add is not None:
        in_specs.append(pl.BlockSpec((tm, tn), lambda i, j, kk: (i, j)))
        args.append(add)
    if bias is not None:
        in_specs.append(pl.BlockSpec((1, tn), lambda i, j, kk: (0, j)))
        args.append(bias)
    return pl.pallas_call(
        body, name=name, grid=(m // tm, n // tn, nk), in_specs=in_specs, out_specs=o_spec,
        out_shape=jax.ShapeDtypeStruct(o_shape, out_dtype),
        scratch_shapes=[pltpu.VMEM((tm, tn), F32)] if nk > 1 else [],
        compiler_params=_cp("parallel", "parallel", "arbitrary"),
    )(*args)


def _rmsnorm_fwd(x, w, *, width, cblk=0, out_dtype=BF16, tr=256, name):
    t = x.shape[0]

    def body(x_ref, w_ref, o_ref):
        xv = x_ref[...].astype(F32)
        r = lax.rsqrt(jnp.mean(xv * xv, axis=-1, keepdims=True) + NORM_EPS)
        o_ref[...] = (xv * r * w_ref[...]).astype(out_dtype)

    return pl.pallas_call(
        body, name=name, grid=(t // tr,),
        in_specs=[pl.BlockSpec((tr, width), lambda i: (i, cblk)), pl.BlockSpec((1, width), lambda i: (0, 0))],
        out_specs=pl.BlockSpec((tr, width), lambda i: (i, 0)),
        out_shape=jax.ShapeDtypeStruct((t, width), out_dtype),
        compiler_params=_cp("parallel"),
    )(x, w)


def _rmsnorm_bwd(x, w, dy, add=None, *, width, cblk=0, out_dtype=F32, also_bf16=False, tr=256, name):
    t = x.shape[0]

    def body(*refs):
        refs = list(refs)
        dxb_ref = refs.pop() if also_bf16 else None
        if add is None:
            x_ref, w_ref, dy_ref, dx_ref, dw_ref = refs
            add_ref = None
        else:
            x_ref, w_ref, dy_ref, add_ref, dx_ref, dw_ref = refs
        xv = x_ref[...].astype(F32)
        dyv = dy_ref[...].astype(F32)
        r = lax.rsqrt(jnp.mean(xv * xv, axis=-1, keepdims=True) + NORM_EPS)
        xh = xv * r
        g = dyv * w_ref[...]
        dx = r * (g - xh * jnp.mean(g * xh, axis=-1, keepdims=True))
        if add_ref is not None:
            dx = dx + add_ref[...].astype(F32)
        dx_ref[...] = dx.astype(out_dtype)
        if dxb_ref is not None:
            dxb_ref[...] = dx.astype(BF16)

        @pl.when(pl.program_id(0) == 0)
        def _():
            dw_ref[...] = jnp.zeros_like(dw_ref)

        dw_ref[...] += jnp.sum(dyv * xh, axis=0, keepdims=True)

    in_specs = [pl.BlockSpec((tr, width), lambda i: (i, cblk)), pl.BlockSpec((1, width), lambda i: (0, 0)),
                pl.BlockSpec((tr, width), lambda i: (i, 0))]
    args = [x, w, dy]
    if add is not None:
        in_specs.append(pl.BlockSpec((tr, width), lambda i: (i, 0)))
        args.append(add)
    blk = pl.BlockSpec((tr, width), lambda i: (i, 0))
    return pl.pallas_call(
        body, name=name, grid=(t // tr,), in_specs=in_specs,
        out_specs=[blk, pl.BlockSpec((1, width), lambda i: (0, 0))] + ([blk] if also_bf16 else []),
        out_shape=[jax.ShapeDtypeStruct((t, width), out_dtype), jax.ShapeDtypeStruct((1, width), F32)]
        + ([jax.ShapeDtypeStruct((t, width), BF16)] if also_bf16 else []),
        compiler_params=_cp("arbitrary"),
    )(*args)


def _shift_down(prev_halo, cur, j):
    if j == 0:
        return cur
    ext = jnp.concatenate([prev_halo, cur], axis=0)
    return pltpu.roll(ext, j, axis=0)[HALO:]


def _shift_up(cur, next_halo, j):
    if j == 0:
        return cur
    ext = jnp.concatenate([cur, next_halo], axis=0)
    return pltpu.roll(ext, ext.shape[0] - j, axis=0)[:cur.shape[0]]


def _conv_rows(prev, cur, w, b, kw):
    shifted = [cur]
    out = b + w[kw - 1:kw] * cur
    for j in range(1, kw):
        sh = _shift_down(prev, cur, j)
        shifted.append(sh)
        out = out + w[kw - 1 - j:kw - j] * sh
    return out, shifted


def _act_fwd(c, glu):
    if glu:
        half = c.shape[1] // 2
        return _silu(c[:, :half]) * c[:, half:]
    return _silu(c)


def _act_bwd(c, dout, glu):
    if glu:
        half = c.shape[1] // 2
        g, up = c[:, :half], c[:, half:]
        s = _sigmoid(g)
        gs = g * s
        return jnp.concatenate([dout * up * (s + gs * (1.0 - s)), dout * gs], axis=1)
    return dout * _dsilu(c)


def _conv_act_fwd(u, w, b, *, kw, glu, tc, coff, ncols, out_dtype, tr=256, name):
    t = u.shape[0]
    nb = ncols // tc
    oc = tc // 2 if glu else tc

    def body(u_ref, uh_ref, w_ref, b_ref, o_ref):
        prev = jnp.where(pl.program_id(0) == 0, 0.0, uh_ref[...])
        c, _ = _conv_rows(prev, u_ref[...], w_ref[...], b_ref[...], kw)
        o_ref[...] = _act_fwd(c, glu).astype(out_dtype)

    return pl.pallas_call(
        body, name=name, grid=(t // tr, nb),
        in_specs=[pl.BlockSpec((tr, tc), lambda i, j: (i, j + coff)),
                  pl.BlockSpec((HALO, tc), lambda i, j: (jnp.maximum(i * (tr // HALO) - 1, 0), j + coff)),
                  pl.BlockSpec((kw, tc), lambda i, j: (0, j)), pl.BlockSpec((1, tc), lambda i, j: (0, j))],
        out_specs=pl.BlockSpec((tr, oc), lambda i, j: (i, j)),
        out_shape=jax.ShapeDtypeStruct((t, nb * oc), out_dtype),
        compiler_params=_cp("parallel", "parallel"),
    )(u, u, w, b)


def _conv_act_bwd(u, w, b, dout, *, kw, glu, tc, coff, ncols, tr=256, name):
    t = u.shape[0]
    nb = ncols // tc
    nt = t // tr
    oc = tc // 2 if glu else tc

    def body(u_ref, up_ref, un_ref, d_ref, dn_ref, w_ref, b_ref, du_ref, dw_ref, db_ref):
        i = pl.program_id(1)
        cur, nxt, wv, bv = u_ref[...], un_ref[...], w_ref[...], b_ref[...]
        prev = jnp.where(i == 0, 0.0, up_ref[...])
        c_cur, shifted = _conv_rows(prev, cur, wv, bv, kw)
        c_nxt, _ = _conv_rows(cur[tr - HALO:], nxt, wv, bv, kw)
        d_cur = _act_bwd(c_cur, d_ref[...].astype(F32), glu)
        d_nxt = _act_bwd(c_nxt, jnp.where(i == nt - 1, 0.0, dn_ref[...].astype(F32)), glu)
        du = wv[kw - 1:kw] * d_cur
        for j in range(1, kw):
            du = du + wv[kw - 1 - j:kw - j] * _shift_up(d_cur, d_nxt, j)
        du_ref[...] = du.astype(BF16)

        @pl.when(i == 0)
        def _():
            dw_ref[...] = jnp.zeros_like(dw_ref)
            db_ref[...] = jnp.zeros_like(db_ref)

        db_ref[...] += jnp.sum(d_cur, axis=0, keepdims=True)
        dw_ref[...] += jnp.concatenate(
            [jnp.sum(d_cur * shifted[kw - 1 - k], axis=0, keepdims=True) for k in range(kw)], axis=0)

    nh = tr // HALO
    return pl.pallas_call(
        body, name=name, grid=(nb, nt),
        in_specs=[pl.BlockSpec((tr, tc), lambda j, i: (i, j + coff)),
                  pl.BlockSpec((HALO, tc), lambda j, i: (jnp.maximum(i * nh - 1, 0), j + coff)),
                  pl.BlockSpec((HALO, tc), lambda j, i: (jnp.minimum((i + 1) * nh, t // HALO - 1), j + coff)),
                  pl.BlockSpec((tr, oc), lambda j, i: (i, j)),
                  pl.BlockSpec((HALO, oc), lambda j, i: (jnp.minimum((i + 1) * nh, t // HALO - 1), j)),
                  pl.BlockSpec((kw, tc), lambda j, i: (0, j)), pl.BlockSpec((1, tc), lambda j, i: (0, j))],
        out_specs=[pl.BlockSpec((tr, tc), lambda j, i: (i, j)), pl.BlockSpec((kw, tc), lambda j, i: (0, j)),
                   pl.BlockSpec((1, tc), lambda j, i: (0, j))],
        out_shape=[jax.ShapeDtypeStruct((t, ncols), BF16), jax.ShapeDtypeStruct((kw, ncols), F32),
                   jax.ShapeDtypeStruct((1, ncols), F32)],
        compiler_params=_cp("parallel", "arbitrary"),
    )(u, u, u, dout, dout, w, b)


def _ple_loss(x2, gl, pe, pw, fw, target, *, tr=256, name):
    t, d = x2.shape

    def body(x_ref, gl_ref, pe_ref, pw_ref, fw_ref, t_ref, l_ref, dx_ref, dfw_ref, dgl_ref, db_ref, dpe_ref, dpw_ref):
        pv, pwv, wv = pe_ref[...], pw_ref[...], fw_ref[...]
        gate = _sigmoid(gl_ref[...])
        rp = lax.rsqrt(jnp.mean(pv * pv, axis=-1, keepdims=True) + NORM_EPS)
        ph = pv * rp
        e = ph * pwv
        x3 = x_ref[...] + gate * e
        r = lax.rsqrt(jnp.mean(x3 * x3, axis=-1, keepdims=True) + NORM_EPS)
        xh = x3 * r
        err = xh * wv - t_ref[...]
        dy = err * (1.0 / d)
        g = dy * wv
        dx = r * (g - xh * jnp.mean(g * xh, axis=-1, keepdims=True))
        dx_ref[...] = dx
        dgl = dx * e * gate * (1.0 - gate)
        de = dx * gate
        gg = de * pwv
        dgl_ref[...] = dgl.astype(BF16)
        dpe_ref[...] = (rp * (gg - ph * jnp.mean(gg * ph, axis=-1, keepdims=True))).astype(BF16)

        @pl.when(pl.program_id(0) == 0)
        def _():
            for ref in (l_ref, dfw_ref, db_ref, dpw_ref):
                ref[...] = jnp.zeros_like(ref)

        l_ref[...] += 0.5 * jnp.sum(jnp.mean(err * err, axis=-1, keepdims=True), axis=0, keepdims=True)
        dfw_ref[...] += jnp.sum(dy * xh, axis=0, keepdims=True)
        db_ref[...] += jnp.sum(dgl, axis=0, keepdims=True)
        dpw_ref[...] += jnp.sum(de * ph, axis=0, keepdims=True)

    blk = pl.BlockSpec((tr, d), lambda i: (i, 0))
    row = pl.BlockSpec((1, d), lambda i: (0, 0))
    rowf = jax.ShapeDtypeStruct((1, d), F32)
    return pl.pallas_call(
        body, name=name, grid=(t // tr,), in_specs=[blk, blk, blk, row, row, blk],
        out_specs=[pl.BlockSpec((1, 1), lambda i: (0, 0)), blk, row, blk, row, blk, row],
        out_shape=[jax.ShapeDtypeStruct((1, 1), F32), jax.ShapeDtypeStruct((t, d), F32), rowf,
                   jax.ShapeDtypeStruct((t, d), BF16), rowf, jax.ShapeDtypeStruct((t, d), BF16), rowf],
        compiler_params=_cp("arbitrary"),
    )(x2, gl, pe, pw, fw, target)


def _rope(blk, tab_ref):
    return blk * tab_ref[0] + pltpu.roll(blk, 96, axis=1) * tab_ref[1] + pltpu.roll(blk, 32, axis=1) * tab_ref[2]


def _unrope(g, tab_ref):
    return g * tab_ref[0] + pltpu.roll(g * tab_ref[1], 32, axis=1) + pltpu.roll(g * tab_ref[2], 96, axis=1)


def _mla_prep(q, kv, proj, tabs, *, tr=512, name):
    t = q.shape[0]

    def body(q_ref, kv_ref, kr_ref, tab_ref, qo_ref, ko_ref, vo_ref, vt_ref):
        qv, kvv = q_ref[...], kv_ref[...]
        qo_ref[0, :, :MLA_NOPE] = qv[:, :MLA_NOPE].astype(BF16)
        qo_ref[0, :, MLA_NOPE:] = _rope(qv[:, MLA_NOPE:], tab_ref).astype(BF16)
        ko_ref[0, :, :MLA_NOPE] = kvv[:, :MLA_NOPE].astype(BF16)
        ko_ref[0, :, MLA_NOPE:] = _rope(kr_ref[...], tab_ref).astype(BF16)
        vo_ref[0] = kvv[:, MLA_NOPE:].astype(BF16)
        for blk in range(tr // ATT_BLK):
            vt_ref[0, blk] = kvv[blk * ATT_BLK:(blk + 1) * ATT_BLK, MLA_NOPE:].T.astype(BF16)

    return pl.pallas_call(
        body, name=name, grid=(t // tr, MLA_HEADS),
        in_specs=[pl.BlockSpec((tr, MLA_QK_PAD), lambda i, h: (i, h)),
                  pl.BlockSpec((tr, MLA_NOPE + MLA_V), lambda i, h: (i, h)),
                  pl.BlockSpec((tr, LANES), lambda i, h: (i, OFF_KR // LANES)),
                  pl.BlockSpec((3, tr, LANES), lambda i, h: (0, i, 0))],
        out_specs=[pl.BlockSpec((1, tr, MLA_QK_PAD), lambda i, h: (h, i, 0)),
                   pl.BlockSpec((1, tr, MLA_QK_PAD), lambda i, h: (h, i, 0)),
                   pl.BlockSpec((1, tr, MLA_V), lambda i, h: (h, i, 0)),
                   pl.BlockSpec((1, tr // ATT_BLK, MLA_V, ATT_BLK), lambda i, h: (h, i, 0, 0))],
        out_shape=[jax.ShapeDtypeStruct((MLA_HEADS, t, MLA_QK_PAD), BF16),
                   jax.ShapeDtypeStruct((MLA_HEADS, t, MLA_QK_PAD), BF16),
                   jax.ShapeDtypeStruct((MLA_HEADS, t, MLA_V), BF16),
                   jax.ShapeDtypeStruct((MLA_HEADS, t // ATT_BLK, MLA_V, ATT_BLK), BF16)],
        compiler_params=_cp("parallel", "parallel"),
    )(q, kv, proj, tabs)


def _mla_unprep(dq3, dk3, dv3, tabs, *, tr=256, name):
    t = dq3.shape[1]

    def body(dq_ref, dk_ref, dv_ref, tab_ref, qo_ref, kvo_ref, kro_ref):
        kr = jnp.zeros((tr, LANES), F32)
        for h in range(MLA_HEADS):
            c0 = h * MLA_QK_PAD
            qo_ref[:, c0:c0 + MLA_NOPE] = dq_ref[h, :, :MLA_NOPE].astype(BF16)
            qo_ref[:, c0 + MLA_NOPE:c0 + MLA_QK_PAD] = _unrope(dq_ref[h, :, MLA_NOPE:], tab_ref).astype(BF16)
            kvo_ref[:, c0:c0 + MLA_NOPE] = dk_ref[h, :, :MLA_NOPE].astype(BF16)
            kvo_ref[:, c0 + MLA_NOPE:c0 + MLA_QK_PAD] = dv_ref[h].astype(BF16)
            kr = kr + dk_ref[h, :, MLA_NOPE:]
        kro_ref[...] = _unrope(kr, tab_ref).astype(BF16)

    return pl.pallas_call(
        body, name=name, grid=(t // tr,),
        in_specs=[pl.BlockSpec((MLA_HEADS, tr, MLA_QK_PAD), lambda i: (0, i, 0)),
                  pl.BlockSpec((MLA_HEADS, tr, MLA_QK_PAD), lambda i: (0, i, 0)),
                  pl.BlockSpec((MLA_HEADS, tr, MLA_V), lambda i: (0, i, 0)),
                  pl.BlockSpec((3, tr, LANES), lambda i: (0, i, 0))],
        out_specs=[pl.BlockSpec((tr, MLA_HEADS * MLA_QK_PAD), lambda i: (i, 0)),
                   pl.BlockSpec((tr, MLA_HEADS * MLA_QK_PAD), lambda i: (i, 0)),
                   pl.BlockSpec((tr, LANES), lambda i: (i, 0))],
        out_shape=[jax.ShapeDtypeStruct((t, MLA_HEADS * MLA_QK_PAD), BF16),
                   jax.ShapeDtypeStruct((t, MLA_HEADS * MLA_QK_PAD), BF16),
                   jax.ShapeDtypeStruct((t, LANES), BF16)],
        compiler_params=_cp("parallel"),
    )(dq3, dk3, dv3, tabs)


ATT_BLK = 512
ATT_SCALE = 1.0 / math.sqrt(MLA_NOPE + MLA_ROPE)
_NT = (((1,), (1,)), ((), ()))
_TN = (((0,), (0,)), ((), ()))


def _att_scores_t(k, q, diagonal):
    s = lax.dot_general(k, q, _NT, preferred_element_type=F32) * ATT_SCALE
    if not diagonal:
        return s
    key = lax.broadcasted_iota(jnp.int32, s.shape, 0)
    query = lax.broadcasted_iota(jnp.int32, s.shape, 1)
    return jnp.where((key >> 6) <= (query >> 6), s, NEG)


def _att_rows(i):
    return pl.ds(pl.multiple_of(i * ATT_BLK, ATT_BLK), ATT_BLK)


ATT_HEADS = 2


def _attn_fwd(q3, k3, vt4, *, name):
    t = q3.shape[1]
    nq = t // ATT_BLK

    def body(q_ref, k_ref, vt_ref, o_ref, lse_ref):
        qi = pl.program_id(1)
        qs = [q_ref[hh] for hh in range(ATT_HEADS)]

        def step(j, carry, diagonal=False):
            out = []
            for hh, (m, l, acc) in enumerate(carry):
                s = _att_scores_t(k_ref[hh, _att_rows(j), :], qs[hh], diagonal)
                m_new = jnp.maximum(m, jnp.max(s, axis=0, keepdims=True))
                p = jnp.exp(s - m_new)
                alpha = jnp.exp(m - m_new)
                l = alpha * l + jnp.sum(p, axis=0, keepdims=True)
                acc = alpha * acc + jnp.dot(vt_ref[hh, j], p.astype(BF16), preferred_element_type=F32)
                out.append((m_new, l, acc))
            return tuple(out)

        init = tuple((jnp.full((1, ATT_BLK), NEG, F32), jnp.zeros((1, ATT_BLK), F32),
                      jnp.zeros((MLA_V, ATT_BLK), F32)) for _ in range(ATT_HEADS))
        done = step(qi, lax.fori_loop(0, qi, step, init), diagonal=True)
        for hh, (m, l, acc) in enumerate(done):
            o_ref[:, hh * MLA_V:(hh + 1) * MLA_V] = (acc / l).T
            lse_ref[hh, 0] = m + jnp.log(l)

    return pl.pallas_call(
        body, name=name, grid=(MLA_HEADS // ATT_HEADS, nq),
        in_specs=[pl.BlockSpec((ATT_HEADS, ATT_BLK, MLA_QK_PAD), lambda h, i: (h, i, 0)),
                  pl.BlockSpec((ATT_HEADS, t, MLA_QK_PAD), lambda h, i: (h, 0, 0)),
                  pl.BlockSpec((ATT_HEADS, nq, MLA_V, ATT_BLK), lambda h, i: (h, 0, 0, 0))],
        out_specs=[pl.BlockSpec((ATT_BLK, ATT_HEADS * MLA_V), lambda h, i: (i, h)),
                   pl.BlockSpec((ATT_HEADS, 1, 1, ATT_BLK), lambda h, i: (h, i, 0, 0))],
        out_shape=[jax.ShapeDtypeStruct((t, MLA_HEADS * MLA_V), F32),
                   jax.ShapeDtypeStruct((MLA_HEADS, nq, 1, ATT_BLK), F32)],
        compiler_params=_cp("parallel", "parallel"),
    )(q3, k3, vt4)


def _attn_bwd(q3, k3, v3, o, dcat, lse, *, name):
    t = q3.shape[1]
    nq = t // ATT_BLK
    wide = ATT_HEADS * MLA_V

    def body(q_ref, k_ref, v_ref, o_ref, do_ref, lse_ref, dq_ref, dk_ref, dv_ref, delta_ref):
        kj = pl.program_id(1)

        @pl.when(kj == 0)
        def _():
            dq_ref[...] = jnp.zeros_like(dq_ref)
            ones = jnp.ones((HALO, MLA_V), F32)
            for i in range(nq):
                rows = pl.ds(i * ATT_BLK, ATT_BLK)
                prod = o_ref[rows, :] * do_ref[rows, :]
                for hh in range(ATT_HEADS):
                    delta_ref[hh, i] = lax.dot_general(ones, prod[:, hh * MLA_V:(hh + 1) * MLA_V], _NT, precision=HI,
                                                       preferred_element_type=F32)

        def step(i, carry, diagonal=False):
            rows = _att_rows(i)
            out = []
            for hh, (dk, dv) in enumerate(carry):
                k, v = k_ref[hh], v_ref[hh]
                q = q_ref[hh, rows, :]
                dob = do_ref[rows, hh * MLA_V:(hh + 1) * MLA_V].astype(BF16)
                p = jnp.exp(_att_scores_t(k, q, diagonal) - lse_ref[hh, i])
                dv = dv + jnp.dot(p.astype(BF16), dob, preferred_element_type=F32)
                dp = lax.dot_general(v, dob, _NT, preferred_element_type=F32)
                ds = (p * (dp - delta_ref[hh, i, 0:1, :]) * ATT_SCALE).astype(BF16)
                dk = dk + jnp.dot(ds, q, preferred_element_type=F32)
                dq_ref[hh, rows, :] += lax.dot_general(ds, k, _TN, preferred_element_type=F32)
                out.append((dk, dv))
            return tuple(out)

        init = tuple((jnp.zeros((ATT_BLK, MLA_QK_PAD), F32), jnp.zeros((ATT_BLK, MLA_V), F32))
                     for _ in range(ATT_HEADS))
        done = lax.fori_loop(kj + 1, nq, step, step(kj, init, diagonal=True))
        for hh, (dk, dv) in enumerate(done):
            dk_ref[hh] = dk
            dv_ref[hh] = dv

    return pl.pallas_call(
        body, name=name, grid=(MLA_HEADS // ATT_HEADS, nq),
        in_specs=[pl.BlockSpec((ATT_HEADS, t, MLA_QK_PAD), lambda h, j: (h, 0, 0)),
                  pl.BlockSpec((ATT_HEADS, ATT_BLK, MLA_QK_PAD), lambda h, j: (h, j, 0)),
                  pl.BlockSpec((ATT_HEADS, ATT_BLK, MLA_V), lambda h, j: (h, j, 0)),
                  pl.BlockSpec((t, wide), lambda h, j: (0, h)),
                  pl.BlockSpec((t, wide), lambda h, j: (0, MLA_HEADS // ATT_HEADS + h)),
                  pl.BlockSpec((ATT_HEADS, nq, 1, ATT_BLK), lambda h, j: (h, 0, 0, 0))],
        out_specs=[pl.BlockSpec((ATT_HEADS, t, MLA_QK_PAD), lambda h, j: (h, 0, 0)),
                   pl.BlockSpec((ATT_HEADS, ATT_BLK, MLA_QK_PAD), lambda h, j: (h, j, 0)),
                   pl.BlockSpec((ATT_HEADS, ATT_BLK, MLA_V), lambda h, j: (h, j, 0))],
        out_shape=[jax.ShapeDtypeStruct((MLA_HEADS, t, MLA_QK_PAD), F32),
                   jax.ShapeDtypeStruct((MLA_HEADS, t, MLA_QK_PAD), F32),
                   jax.ShapeDtypeStruct((MLA_HEADS, t, MLA_V), F32)],
        scratch_shapes=[pltpu.VMEM((ATT_HEADS, nq, HALO, ATT_BLK), F32)],
        compiler_params=_cp("parallel", "arbitrary"),
    )(q3, k3, v3, o, dcat, lse)


def _ssd_prep(proj, bias128, alog128, *, name):
    t = proj.shape[0]
    nc = t // CHUNK

    def body(raw_ref, b_ref, al_ref, dt_ref, cs_ref, a_ref):
        xv = raw_ref[...] + b_ref[...]
        dt = jnp.maximum(xv, 0.0) + jnp.log(1.0 + jnp.exp(-jnp.abs(xv)))
        a = -jnp.exp(al_ref[...])
        adt = (dt * a).reshape(nc, CHUNK, LANES)
        li = lax.broadcasted_iota(jnp.int32, (nc, CHUNK, CHUNK), 1)
        si = lax.broadcasted_iota(jnp.int32, (nc, CHUNK, CHUNK), 2)
        tril = jnp.where(si <= li, 1.0, 0.0).astype(F32)
        cs = lax.dot_general(tril, adt, (((2,), (1,)), ((0,), (0,))), precision=HI, preferred_element_type=F32)
        dt_ref[...] = dt
        cs_ref[...] = cs.reshape(t, LANES)
        a_ref[...] = a

    blk = pl.BlockSpec((t, LANES), lambda i: (0, 0))
    row = pl.BlockSpec((1, LANES), lambda i: (0, 0))
    return pl.pallas_call(
        body, name=name, grid=(1,),
        in_specs=[pl.BlockSpec((t, LANES), lambda i: (0, OFF_DT // LANES)), row, row],
        out_specs=[blk, blk, row],
        out_shape=[jax.ShapeDtypeStruct((t, LANES), F32), jax.ShapeDtypeStruct((t, LANES), F32),
                   jax.ShapeDtypeStruct((1, LANES), F32)],
        compiler_params=_cp("arbitrary"),
    )(proj, bias128, alog128)


def _ssd_prep_bwd(ddt128, dadt128, proj, bias128, dt128, a128, dd_h, *, name):
    t = proj.shape[0]

    def body(ddt_ref, dadt_ref, raw_ref, b_ref, dt_ref, a_ref, dd_ref, draw_ref, db_ref, dal_ref, dds_ref):
        draw = ddt_ref[...] * _sigmoid(raw_ref[...] + b_ref[...])
        draw_ref[...] = draw.astype(BF16)
        db_ref[...] = jnp.sum(draw, axis=0, keepdims=True)
        dal_ref[...] = jnp.sum(dadt_ref[...] * dt_ref[...], axis=0, keepdims=True) * a_ref[...]
        dds_ref[...] = jnp.sum(dd_ref[...], axis=-1, keepdims=True)

    blk = pl.BlockSpec((t, LANES), lambda i: (0, 0))
    row = pl.BlockSpec((1, LANES), lambda i: (0, 0))
    return pl.pallas_call(
        body, name=name, grid=(1,),
        in_specs=[blk, blk, pl.BlockSpec((t, LANES), lambda i: (0, OFF_DT // LANES)), row, blk, row,
                  pl.BlockSpec((SSD_HEADS, SSD_P), lambda i: (0, 0))],
        out_specs=[blk, row, row, pl.BlockSpec((SSD_HEADS, 1), lambda i: (0, 0))],
        out_shape=[jax.ShapeDtypeStruct((t, LANES), BF16), jax.ShapeDtypeStruct((1, LANES), F32),
                   jax.ShapeDtypeStruct((1, LANES), F32), jax.ShapeDtypeStruct((SSD_HEADS, 1), F32)],
        compiler_params=_cp("arbitrary"),
    )(ddt128, dadt128, proj, bias128, dt128, a128, dd_h)


def _bdot(a, b, ca, cb, precision=None):
    return lax.dot_general(a, b, (((ca,), (cb,)), ((0,), (0,))), precision=precision, preferred_element_type=F32)


def _pieces(x):
    hi = x.astype(BF16)
    rest = x - hi.astype(F32)
    mid = rest.astype(BF16)
    return hi, mid, (rest - mid.astype(F32)).astype(BF16)


def _bdot_sum(a, b, ca, cb, split):
    other = (b if split == 0 else a).astype(BF16)
    out = None
    for piece in _pieces(a if split == 0 else b):
        term = _bdot(piece, other, ca, cb) if split == 0 else _bdot(other, piece, ca, cb)
        out = term if out is None else out + term
    return out


def _head_matrices():
    eye, zero = jnp.eye(SSD_P, dtype=F32), jnp.zeros((SSD_P, SSD_P), F32)
    pick = jnp.stack([jnp.concatenate([eye, zero], axis=0), jnp.concatenate([zero, eye], axis=0)])
    return pick, pick.transpose(0, 2, 1)


def _move(x, sel):
    selb = sel.astype(BF16)
    hi = x.astype(BF16)
    rest = x - hi.astype(F32)
    mid = rest.astype(BF16)
    low = (rest - mid.astype(F32)).astype(BF16)
    out = jnp.dot(hi, selb, preferred_element_type=F32)
    out = out + jnp.dot(mid, selb, preferred_element_type=F32)
    return out + jnp.dot(low, selb, preferred_element_type=F32)


def _pick_head(pair_ref, pick_ref, h):
    return _move(pair_ref[...], pick_ref[h % 2])


def _place_head(out_ref, val, place_ref, h):
    wide = _move(val, place_ref[h % 2])

    @pl.when(h % 2 == 0)
    def _():
        out_ref[...] = wide

    @pl.when(h % 2 == 1)
    def _():
        out_ref[...] += wide


def _ssd_common(x2, dt_ref, cs_ref, csr_ref, b_ref, c_ref, nc):
    x = x2.reshape(nc, CHUNK, SSD_P)
    dt = dt_ref[0].reshape(nc, CHUNK, SSD_P)
    cs = cs_ref[0].reshape(nc, CHUNK, SSD_P)
    csr = csr_ref[0]
    bm = b_ref[...].reshape(nc, CHUNK, SSD_N).astype(BF16)
    cm = c_ref[...].reshape(nc, CHUNK, SSD_N).astype(BF16)
    li = lax.broadcasted_iota(jnp.int32, (nc, CHUNK, CHUNK), 1)
    si = lax.broadcasted_iota(jnp.int32, (nc, CHUNK, CHUNK), 2)
    lmat = jnp.exp(jnp.where(si <= li, cs - csr, NEG))
    g = _bdot(cm, bm, 2, 2)
    cs_last = jnp.sum(jnp.where(li == CHUNK - 1, cs, 0.0), axis=1, keepdims=True)
    xdt = x * dt
    dec = jnp.exp(cs_last - cs)
    return x, dt, cs, bm, cm, li, si, lmat, g, cs_last, xdt, dec


def _ssd_fwd(xbc, dt_h, cs_h, cs_row, dskip_h, *, name):
    t = xbc.shape[0]
    nc = t // CHUNK
    hpg = SSD_HEADS // SSD_GROUPS
    pick, place = _head_matrices()

    def body(xs_ref, dt_ref, cs_ref, csr_ref, b_ref, c_ref, dk_ref, pick_ref, place_ref, y_ref, st_ref, sc_ref, cd_ref):
        h = pl.program_id(0)
        x, dt, cs, bm, cm, li, si, lmat, g, cs_last, xdt, dec = _ssd_common(_pick_head(xs_ref, pick_ref, h), dt_ref,
                                                                           cs_ref, csr_ref, b_ref, c_ref, nc)
        yd = _bdot((g * lmat).astype(BF16), xdt.astype(BF16), 2, 1)
        sc_ref[...] = _bdot(bm, (dec * xdt).astype(BF16), 1, 1)
        cd_ref[...] = jnp.exp(cs_last)

        def step(c, s):
            st_ref[0, c] = s
            return s * cd_ref[c] + sc_ref[c]

        lax.fori_loop(0, nc, step, jnp.zeros((SSD_N, SSD_P), F32))
        yo = _bdot(cm, st_ref[0].astype(BF16), 2, 1) * jnp.exp(cs)
        _place_head(y_ref, (yd + yo + dk_ref[0] * x).reshape(t, SSD_P), place_ref, h)

    head = pl.BlockSpec((1, t, SSD_P), lambda h: (h, 0, 0))
    pair = pl.BlockSpec((t, 2 * SSD_P), lambda h: (0, h // 2))
    nxb = D_SSM // SSD_N
    return pl.pallas_call(
        body, name=name, grid=(SSD_HEADS,),
        in_specs=[pair, head, head, pl.BlockSpec((1, nc, 1, CHUNK), lambda h: (h, 0, 0, 0)),
                  pl.BlockSpec((t, SSD_N), lambda h: (0, nxb + h // hpg)),
                  pl.BlockSpec((t, SSD_N), lambda h: (0, nxb + SSD_GROUPS + h // hpg)),
                  pl.BlockSpec((1, 1, SSD_P), lambda h: (h, 0, 0)),
                  pl.BlockSpec((2, 2 * SSD_P, SSD_P), lambda h: (0, 0, 0)),
                  pl.BlockSpec((2, SSD_P, 2 * SSD_P), lambda h: (0, 0, 0))],
        out_specs=[pair, pl.BlockSpec((1, nc, SSD_N, SSD_P), lambda h: (h, 0, 0, 0))],
        out_shape=[jax.ShapeDtypeStruct((t, D_SSM), F32),
                   jax.ShapeDtypeStruct((SSD_HEADS, nc, SSD_N, SSD_P), F32)],
        scratch_shapes=[pltpu.VMEM((nc, SSD_N, SSD_P), F32), pltpu.VMEM((nc, 1, SSD_P), F32)],
        compiler_params=_cp("arbitrary"),
    )(xbc, dt_h, cs_h, cs_row, xbc, xbc, dskip_h, pick, place)


def _ssd_bwd(xbc, dt_h, cs_h, cs_row, dskip_h, a_h, states, dy, *, name):
    t = xbc.shape[0]
    nc = t // CHUNK
    hpg = SSD_HEADS // SSD_GROUPS
    pick, place = _head_matrices()

    def body(xs_ref, dt_ref, cs_ref, csr_ref, b_ref, c_ref, dk_ref, a_ref, st_ref, dy_ref, pick_ref, place_ref,
             dxs_ref, ddt_ref, dadt_ref, db_ref, dc_ref, dd_ref, dsl_ref, dsc_ref, cd_ref):
        h = pl.program_id(0) * hpg + pl.program_id(1)
        x, dt, cs, bm, cm, li, si, lmat, g, cs_last, xdt, dec = _ssd_common(_pick_head(xs_ref, pick_ref, h), dt_ref,
                                                                           cs_ref, csr_ref, b_ref, c_ref, nc)
        dy = _pick_head(dy_ref, pick_ref, h).reshape(nc, CHUNK, SSD_P)
        dyb = dy.astype(BF16)
        xdtb = xdt.astype(BF16)
        sprev = st_ref[0]
        sprevb = sprev.astype(BF16)
        cdec = jnp.exp(cs_last)
        ecs = jnp.exp(cs)
        dw = (ecs * dy).astype(BF16)
        wmat = _bdot(cm, sprevb, 2, 1)
        dcs = jnp.sum(dy * ecs * wmat, axis=2, keepdims=True)
        dcm = _bdot(dw, sprevb, 2, 2)
        dsl_ref[...] = _bdot(cm, dw, 1, 1)
        cd_ref[...] = cdec

        def step(k, ds):
            c = nc - 1 - k
            dsc_ref[c] = ds
            return ds * cd_ref[c] + dsl_ref[c]

        lax.fori_loop(0, nc, step, jnp.zeros((SSD_N, SSD_P), F32))
        dsc = dsc_ref[...]
        dscb = dsc.astype(BF16)
        d_last = jnp.sum(jnp.sum(dsc * sprev, axis=1, keepdims=True) * cdec, axis=2, keepdims=True)
        z = dec * xdt
        dbm = _bdot(z.astype(BF16), dscb, 2, 2)
        dz = _bdot(bm, dscb, 2, 1)
        dxdt = dec * dz
        t2 = jnp.sum(dz * z, axis=2, keepdims=True)
        dcs = dcs - t2
        d_last = d_last + jnp.sum(t2, axis=1, keepdims=True)
        m = g * lmat
        mb = m.astype(BF16)
        dm = _bdot(dyb, xdtb, 2, 2)
        dxdt = dxdt + _bdot(mb, dyb, 1, 1)
        dseg = dm * m
        dcs = dcs + jnp.sum(dseg, axis=2, keepdims=True)
        ones = jnp.ones((nc, CHUNK, SSD_P), F32)
        dcs = dcs - _bdot_sum(dseg, ones, 1, 1, 0)
        dg = (dm * lmat).astype(BF16)
        dcm = dcm + _bdot(dg, bm, 2, 1)
        dbm = dbm + _bdot(dg, cm, 1, 1)
        dcs = dcs + jnp.where(li[:, :, :SSD_P] == CHUNK - 1, d_last, 0.0)
        triu = jnp.where(li <= si, 1.0, 0.0).astype(F32)
        dadt = _bdot_sum(triu, dcs, 2, 1, 1)
        dk = dk_ref[0]
        _place_head(dxs_ref, (dxdt * dt + dk * dy).reshape(t, SSD_P), place_ref, h)
        ddt = jnp.sum(dxdt * x, axis=2, keepdims=True) + dadt * a_ref[0]
        mine = lax.broadcasted_iota(jnp.int32, (t, LANES), 1) == h

        @pl.when(h == 0)
        def _():
            ddt_ref[...] = jnp.zeros_like(ddt_ref)
            dadt_ref[...] = jnp.zeros_like(dadt_ref)

        ddt_ref[...] += jnp.where(mine, jnp.max(ddt, axis=2, keepdims=True).reshape(t, 1), 0.0)
        dadt_ref[...] += jnp.where(mine, jnp.max(dadt, axis=2, keepdims=True).reshape(t, 1), 0.0)
        dd_ref[0] = jnp.sum(jnp.sum(dy * x, axis=1, keepdims=True), axis=0)

        @pl.when(pl.program_id(1) == 0)
        def _():
            db_ref[...] = jnp.zeros_like(db_ref)
            dc_ref[...] = jnp.zeros_like(dc_ref)

        db_ref[...] += dbm.reshape(t, SSD_N)
        dc_ref[...] += dcm.reshape(t, SSD_N)

    head = pl.BlockSpec((1, t, SSD_P), lambda gi, hi: (gi * hpg + hi, 0, 0))
    pair = pl.BlockSpec((t, 2 * SSD_P), lambda gi, hi: (0, (gi * hpg + hi) // 2))
    grp = pl.BlockSpec((t, SSD_N), lambda gi, hi: (0, gi))
    lane = pl.BlockSpec((1, 1, SSD_P), lambda gi, hi: (gi * hpg + hi, 0, 0))
    rows = pl.BlockSpec((t, LANES), lambda gi, hi: (0, 0))
    nxb = D_SSM // SSD_N
    dxs, ddt, dadt, db, dc, dd = pl.pallas_call(
        body, name=name, grid=(SSD_GROUPS, hpg),
        in_specs=[pair, head, head, pl.BlockSpec((1, nc, 1, CHUNK), lambda gi, hi: (gi * hpg + hi, 0, 0, 0)),
                  pl.BlockSpec((t, SSD_N), lambda gi, hi: (0, nxb + gi)),
                  pl.BlockSpec((t, SSD_N), lambda gi, hi: (0, nxb + SSD_GROUPS + gi)), lane, lane,
                  pl.BlockSpec((1, nc, SSD_N, SSD_P), lambda gi, hi: (gi * hpg + hi, 0, 0, 0)), pair,
                  pl.BlockSpec((2, 2 * SSD_P, SSD_P), lambda gi, hi: (0, 0, 0)),
                  pl.BlockSpec((2, SSD_P, 2 * SSD_P), lambda gi, hi: (0, 0, 0))],
        out_specs=[pair, rows, rows, grp, grp, lane],
        out_shape=[jax.ShapeDtypeStruct((t, D_SSM), F32)] + [jax.ShapeDtypeStruct((t, LANES), F32)] * 2
        + [jax.ShapeDtypeStruct((t, SSD_GROUPS * SSD_N), F32)] * 2
        + [jax.ShapeDtypeStruct((SSD_HEADS, 1, SSD_P), F32)],
        scratch_shapes=[pltpu.VMEM((nc, SSD_N, SSD_P), F32), pltpu.VMEM((nc, SSD_N, SSD_P), F32),
                        pltpu.VMEM((nc, 1, SSD_P), F32)],
        compiler_params=_cp("arbitrary", "arbitrary"),
    )(xbc, dt_h, cs_h, cs_row, xbc, xbc, dskip_h, a_h, states, dy, pick, place)
    return jnp.concatenate([dxs, db, dc], axis=1), ddt, dadt, dd


def _ssd_gate_fwd(y, proj, w, *, tr=256, name):
    t = y.shape[0]
    gw = D_SSM // SSD_GROUPS

    def body(y_ref, z_ref, w_ref, o_ref):
        v = y_ref[...] * _silu(z_ref[...])
        for gi in range(SSD_GROUPS):
            vg = v[:, gi * gw:(gi + 1) * gw]
            r = lax.rsqrt(jnp.mean(vg * vg, axis=-1, keepdims=True) + NORM_EPS)
            o_ref[:, gi * gw:(gi + 1) * gw] = (vg * r * w_ref[:, gi * gw:(gi + 1) * gw]).astype(BF16)

    blk = pl.BlockSpec((tr, D_SSM), lambda i: (i, 0))
    return pl.pallas_call(
        body, name=name, grid=(t // tr,), in_specs=[blk, blk, pl.BlockSpec((1, D_SSM), lambda i: (0, 0))],
        out_specs=blk, out_shape=jax.ShapeDtypeStruct((t, D_SSM), BF16), compiler_params=_cp("parallel"),
    )(y, proj, w)


def _ssd_gate_bwd(y, proj, w, dcat, *, tr=256, name):
    t = y.shape[0]
    gw = D_SSM // SSD_GROUPS

    def body(y_ref, z_ref, w_ref, d_ref, dy_ref, dz_ref, dw_ref):
        yv, zv, dv = y_ref[...], z_ref[...], d_ref[...].astype(F32)
        sz = _silu(zv)
        v = yv * sz

        @pl.when(pl.program_id(0) == 0)
        def _():
            dw_ref[...] = jnp.zeros_like(dw_ref)

        for gi in range(SSD_GROUPS):
            sl = slice(gi * gw, (gi + 1) * gw)
            vg, dg = v[:, sl], dv[:, sl]
            r = lax.rsqrt(jnp.mean(vg * vg, axis=-1, keepdims=True) + NORM_EPS)
            vh = vg * r
            gg = dg * w_ref[:, sl]
            dvg = r * (gg - vh * jnp.mean(gg * vh, axis=-1, keepdims=True))
            dy_ref[:, sl] = dvg * sz[:, sl]
            dz_ref[:, sl] = (dvg * yv[:, sl] * _dsilu(zv[:, sl])).astype(BF16)
            dw_ref[:, sl] += jnp.sum(dg * vh, axis=0, keepdims=True)

    blk = pl.BlockSpec((tr, D_SSM), lambda i: (i, 0))
    row = pl.BlockSpec((1, D_SSM), lambda i: (0, 0))
    return pl.pallas_call(
        body, name=name, grid=(t // tr,), in_specs=[blk, blk, row, blk], out_specs=[blk, blk, row],
        out_shape=[jax.ShapeDtypeStruct((t, D_SSM), F32), jax.ShapeDtypeStruct((t, D_SSM), BF16),
                   jax.ShapeDtypeStruct((1, D_SSM), F32)],
        compiler_params=_cp("arbitrary"),
    )(y, proj, w, dcat)


def _pad_lanes(v):
    return jnp.pad(v, ((0, 0), (0, LANES - v.shape[1])))


def _per_head(v128, t):
    return jnp.broadcast_to(v128[:, :SSD_HEADS].T[:, :, None], (SSD_HEADS, t, SSD_P))


def _ssd_forward(proj, conv_w, conv_b, dt_bias, a_log, d_skip, ssd_norm_w):
    t = proj.shape[0]
    nc = t // CHUNK
    xbc = _conv_act_fwd(proj, conv_w, conv_b, kw=SSD_CONV, glu=False, tc=512, coff=OFF_XBC // 512,
                        ncols=SSD_CONV_DIM, out_dtype=F32, name="ssd_conv_fwd")
    bias128, alog128 = _pad_lanes(dt_bias), _pad_lanes(a_log)
    dt128, cs128, a128 = _ssd_prep(proj, bias128, alog128, name="ssd_prep")
    dt_h, cs_h = _per_head(dt128, t), _per_head(cs128, t)
    cs_row = cs128[:, :SSD_HEADS].T.reshape(SSD_HEADS, nc, 1, CHUNK)
    dskip_h = jnp.broadcast_to(d_skip[0][:, None, None], (SSD_HEADS, 1, SSD_P))
    a_h = jnp.broadcast_to(a128[0, :SSD_HEADS][:, None, None], (SSD_HEADS, 1, SSD_P))
    y, states = _ssd_fwd(xbc, dt_h, cs_h, cs_row, dskip_h, name="ssd_scan_fwd")
    y_ssd = _ssd_gate_fwd(y, proj, ssd_norm_w, name="ssd_gate_fwd")
    saved = (proj, conv_w, conv_b, ssd_norm_w, bias128, dt128, a128, dt_h, cs_h, cs_row, xbc, dskip_h, a_h, states, y)
    return y_ssd, saved


def _ssd_backward(saved, dcat):
    proj, conv_w, conv_b, ssd_norm_w, bias128, dt128, a128, dt_h, cs_h, cs_row, xbc, dskip_h, a_h, states, y = saved
    dy, dz, d_norm_w = _ssd_gate_bwd(y, proj, ssd_norm_w, dcat, name="ssd_gate_bwd")
    dxc, ddt128, dadt128, dd_h = _ssd_bwd(xbc, dt_h, cs_h, cs_row, dskip_h, a_h, states, dy, name="ssd_scan_bwd")
    dxbc, d_conv_w, d_conv_b = _conv_act_bwd(proj, conv_w, conv_b, dxc, kw=SSD_CONV, glu=False, tc=512,
                                             coff=OFF_XBC // 512, ncols=SSD_CONV_DIM, name="ssd_conv_bwd")
    d_raw, d_bias, d_alog, d_dskip = _ssd_prep_bwd(ddt128, dadt128, proj, bias128, dt128, a128,
                                                   dd_h.reshape(SSD_HEADS, SSD_P), name="ssd_prep_bwd")
    return (dz, dxbc, d_raw, d_norm_w, d_conv_w, d_conv_b, d_bias[:, :SSD_HEADS], d_alog[:, :SSD_HEADS],
            d_dskip.reshape(1, SSD_HEADS))


def _rope_tables(positions):
    inv_freq = ROPE_THETA ** (-jnp.arange(0, MLA_ROPE, 2, dtype=F32) / MLA_ROPE)
    ang = positions[0].astype(F32)[:, None] * inv_freq
    cos, sin = jnp.cos(ang), jnp.sin(ang)
    z = jnp.zeros_like(cos)
    return jnp.stack([jnp.concatenate([cos, cos, z, z], axis=1), jnp.concatenate([-sin, z, z, z], axis=1),
                      jnp.concatenate([z, sin, z, z], axis=1)])


def _mla_forward(proj, tabs, q_a_norm_w, wq_pad, kv_a_norm_w, wkv):
    qn = _rmsnorm_fwd(proj, q_a_norm_w, width=MLA_Q_RANK, cblk=OFF_QA // MLA_Q_RANK, name="q_a_norm")
    q = _matmul(qn, wq_pad, name="q_b_proj")
    kvn = _rmsnorm_fwd(proj, kv_a_norm_w, width=MLA_KV_RANK, cblk=OFF_CKV // MLA_KV_RANK, name="kv_a_norm")
    kv = _matmul(kvn, wkv, name="kv_b_proj")
    q3, k3, v3, vt4 = _mla_prep(q, kv, proj, tabs, name="mla_prep")
    o, lse = _attn_fwd(q3, k3, vt4, name="attn_fwd")
    return o, (proj, tabs, q_a_norm_w, wq_pad, kv_a_norm_w, wkv, qn, kvn, q3, k3, v3, o, lse)


def _mla_backward(saved, dcat):
    proj, tabs, q_a_norm_w, wq_pad, kv_a_norm_w, wkv, qn, kvn, q3, k3, v3, o, lse = saved
    dq3, dk3, dv3 = _attn_bwd(q3, k3, v3, o, dcat, lse, name="attn_bwd")
    dq, dkv, dkr = _mla_unprep(dq3, dk3, dv3, tabs, name="mla_unprep")
    d_wq = _matmul(qn, dq, ta=True, out_dtype=BF16, name="d_w_q_b")
    dqn = _matmul(dq, wq_pad, tb=True, name="d_qn")
    dq_a, d_qnw = _rmsnorm_bwd(proj, q_a_norm_w, dqn, width=MLA_Q_RANK, cblk=OFF_QA // MLA_Q_RANK, out_dtype=BF16,
                               name="q_a_norm_bwd")
    d_wkv = _matmul(kvn, dkv, ta=True, out_dtype=BF16, name="d_w_kv_b")
    dkvn = _matmul(dkv, wkv, tb=True, name="d_kvn")
    dckv, d_kvnw = _rmsnorm_bwd(proj, kv_a_norm_w, dkvn, width=MLA_KV_RANK, cblk=OFF_CKV // MLA_KV_RANK,
                                out_dtype=BF16, name="kv_a_norm_bwd")
    return dq_a, dckv, dkr, d_wq, d_wkv, d_qnw, d_kvnw


def _pad_w_q(w):
    r = w.shape[0]
    w3 = w.reshape(r, MLA_HEADS, MLA_NOPE + MLA_ROPE)
    return jnp.pad(w3, ((0, 0), (0, 0), (0, MLA_QK_PAD - MLA_NOPE - MLA_ROPE))).reshape(r, MLA_HEADS * MLA_QK_PAD)


def _unpad_w_q(w):
    r = w.shape[0]
    return w.reshape(r, MLA_HEADS, MLA_QK_PAD)[:, :, :MLA_NOPE + MLA_ROPE].reshape(r, MLA_HEADS * (MLA_NOPE + MLA_ROPE))


W_IN_SEGMENTS = ((0, D_SSM + SSD_CONV_DIM, 0), (D_SSM + SSD_CONV_DIM, D_SSM + SSD_CONV_DIM + SSD_HEADS, OFF_DT),
                 (D_SSM + SSD_CONV_DIM + SSD_HEADS, D_IN - MLA_ROPE, OFF_QA), (D_IN - MLA_ROPE, D_IN, OFF_KR))


def _pad_w_in_shards(g):
    n = g.shape[2]
    pieces, at = [], 0
    for lo, hi, start in sorted(W_IN_SEGMENTS, key=lambda seg: seg[2]):
        if start > at:
            pieces.append(jnp.zeros((g.shape[1], start - at), g.dtype))
        for j in range(N_DEV):
            a, b = max(lo, j * n), min(hi, (j + 1) * n)
            if a < b:
                pieces.append(g[j][:, a - j * n:b - j * n])
        at = start + hi - lo
    pieces.append(jnp.zeros((g.shape[1], D_IN_PAD - at), g.dtype))
    return jnp.concatenate(pieces, axis=1)


def _unpad_w_in_shards(w):
    n = D_IN // N_DEV
    shards = []
    for j in range(N_DEV):
        pieces = []
        for lo, hi, start in W_IN_SEGMENTS:
            a, b = max(lo, j * n), min(hi, (j + 1) * n)
            if a < b:
                pieces.append(w[:, start + a - lo:start + b - lo])
        shards.append(jnp.concatenate(pieces, axis=1) if len(pieces) > 1 else pieces[0])
    return jnp.stack(shards)


WEIGHTS = ['mix_norm_w', 'w_in', 'conv_w', 'conv_b', 'dt_bias', 'a_log', 'd_skip', 'ssd_norm_w', 'q_a_norm_w', 'w_q_b',
           'kv_a_norm_w', 'w_kv_b', 'w_out', 'ffn_norm_w', 'w_ffn_up', 'ffn_conv_w', 'ffn_conv_b', 'w_ffn_down',
           'ple_norm_w', 'w_ple_gate', 'b_ple_gate', 'w_ple_proj', 'ple_post_norm_w', 'final_norm_w']
BIG = ['w_in', 'w_q_b', 'w_kv_b', 'w_out', 'w_ffn_up', 'w_ffn_down', 'w_ple_gate', 'w_ple_proj']
COL_SHARDED = ('w_in', 'w_q_b', 'w_kv_b', 'w_ffn_up', 'w_ple_proj')
CONV = ['conv_w', 'ffn_conv_w']
REPL = [n for n in WEIGHTS if n not in BIG and n not in CONV]
FFN_INV = tuple(int(i) for i in np.argsort(FFN_PERM))


def _cat_cols(g):
    return jnp.concatenate([g[j] for j in range(N_DEV)], axis=1)


def _split_cols(w):
    n = w.shape[1] // N_DEV
    return jnp.stack([w[:, j * n:(j + 1) * n] for j in range(N_DEV)])


def _interleave(v):
    r = v.shape[0]
    return v.reshape(r, N_DEV, FFN_TC)[:, jnp.array(FFN_PERM)].reshape(r, N_DEV * FFN_TC)


def _deinterleave(v):
    r = v.shape[0]
    return v.reshape(r, N_DEV, FFN_TC)[:, jnp.array(FFN_INV)].reshape(r, N_DEV * FFN_TC)


def _assemble_weights(g):
    layout = {
        'w_in': _pad_w_in_shards,
        'w_q_b': lambda v: _pad_w_q(_cat_cols(v)),
        'w_kv_b': _cat_cols,
        'w_out': lambda v: v.reshape(D_MODEL, D_MODEL),
        'w_ffn_up': lambda v: v,
        'w_ffn_down': lambda v: v.reshape(D_FF, D_MODEL),
        'w_ple_gate': lambda v: v.reshape(D_MODEL, D_MODEL),
        'w_ple_proj': _cat_cols,
        'conv_w': _cat_cols,
        'ffn_conv_w': lambda v: _interleave(_cat_cols(v)),
    }
    return {n: layout[n](v) for n, v in g.items()}


WEIGHT_GROUPS = {'a': ['w_in', 'w_q_b', 'w_kv_b', 'conv_w'], 'b': ['w_out', 'w_ffn_up', 'ffn_conv_w'],
                 'c': ['w_ffn_down', 'w_ple_gate', 'w_ple_proj']}
GRAD_GROUPS = {'p': ['w_ple_proj', 'w_ple_gate', 'w_ffn_down'], 'r': ['w_ffn_up'], 's': ['w_out'],
               't': ['w_q_b', 'w_kv_b', 'w_in']}


def _ffn_perm(j):
    return (j % 2) * (N_DEV // 2) + j // 2


def _local_step(x, p, tabs, get_w, s, target, emit, relay, settle):
    t = x.shape[0]
    s = dict(s)
    half = D_MODEL // 2
    up_cols = 2 * D_FF
    ffn_conv_b = _interleave(s['ffn_conv_b'])
    w = dict(get_w('a', None))
    h = _rmsnorm_fwd(x, s['mix_norm_w'], width=D_MODEL, name="mix_norm")
    proj = _matmul(h, w['w_in'], tm=t, name="in_proj")
    y_ssd, ssd_saved = _ssd_forward(proj, w['conv_w'], s['conv_b'], s['dt_bias'], s['a_log'], s['d_skip'],
                                    s['ssd_norm_w'])
    o, mla_saved = _mla_forward(proj, tabs, s['q_a_norm_w'], w['w_q_b'], s['kv_a_norm_w'], w['w_kv_b'])
    tk_o, tn_o = _tile(half, MM_TK), _tile(D_MODEL, MM_TILE)
    w.update(get_w('b', o))
    x1 = _matmul(y_ssd, w['w_out'], add=x, mnk=(t, D_MODEL, half), name="out_proj_ssd")
    x1 = _matmul(o, w['w_out'], add=x1, mnk=(t, D_MODEL, half), name="out_proj_mla",
                 b_spec=pl.BlockSpec((tk_o, tn_o), lambda i, j, kk: (kk + half // tk_o, j)))
    hf = _rmsnorm_fwd(x1, s['ffn_norm_w'], width=D_MODEL, name="ffn_norm")
    tk_u = _tile(D_MODEL, MM_TK)
    u = _matmul(hf, w['w_ffn_up'], mnk=(t, up_cols, D_MODEL), tn=FFN_TC, name="ffn_up",
                b_spec=pl.BlockSpec((1, tk_u, FFN_TC), lambda i, j, kk: (_ffn_perm(j), kk, 0)))
    act = _conv_act_fwd(u, w['ffn_conv_w'], ffn_conv_b, kw=FFN_CONV, glu=True, tc=2 * FFN_TC, coff=0, ncols=up_cols,
                        out_dtype=BF16, name="ffn_act")
    w.update(get_w('c', act))
    x2 = _matmul(act, w['w_ffn_down'], add=x1, name="ffn_down")
    hp = _rmsnorm_fwd(x2, s['ple_norm_w'], width=D_MODEL, name="ple_norm")
    gl = _matmul(hp, w['w_ple_gate'], bias=s['b_ple_gate'], tm=t, name="ple_gate")
    pe = _matmul(p, w['w_ple_proj'], name="ple_proj")
    loss, dx3, d_final, dgl, d_bgate, dpe, d_post = _ple_loss(x2, gl, pe, s['ple_post_norm_w'], s['final_norm_w'],
                                                              target, name="ple_loss")
    d_wproj = _matmul(p, dpe, ta=True, out_dtype=BF16, name="d_w_ple_proj")
    d_wgate = _matmul(hp, dgl, ta=True, out_dtype=BF16, tm=D_MODEL, name="d_w_ple_gate")
    dhp = _matmul(dgl, w['w_ple_gate'], tb=True, tm=t, name="d_ple_normed")
    dx2, d_plenorm, dx2b = _rmsnorm_bwd(x2, s['ple_norm_w'], dhp, dx3, width=D_MODEL, also_bf16=True,
                                        name="ple_norm_bwd")
    dact = _matmul(dx2b, w['w_ffn_down'], tb=True, name="d_ffn_act")
    d_wdown = _matmul(act, dx2b, ta=True, out_dtype=BF16, name="d_w_ffn_down")
    zz = emit('p', {'w_ple_proj': _split_cols(d_wproj), 'w_ple_gate': d_wgate.reshape(N_DEV, D_MODEL // N_DEV, D_MODEL),
                    'w_ffn_down': d_wdown.reshape(N_DEV, D_FF // N_DEV, D_MODEL)})
    du, d_fconv_w, d_fconv_b = _conv_act_bwd(u, w['ffn_conv_w'], ffn_conv_b + zz, dact, kw=FFN_CONV, glu=True,
                                             tc=2 * FFN_TC, coff=0, ncols=up_cols, name="ffn_act_bwd")
    zz = zz + relay('p', du)
    tm_u = D_MODEL
    d_wup = _matmul(hf, du, ta=True, out_dtype=BF16, mnk=(D_MODEL, up_cols, t), tm=tm_u, tn=FFN_TC, name="d_w_ffn_up",
                    o_spec=pl.BlockSpec((1, tm_u, FFN_TC), lambda i, j, kk: (_ffn_perm(j), i, 0)),
                    o_shape=(N_DEV, D_MODEL, FFN_TC))
    zz = zz + emit('r', {'w_ffn_up': d_wup})
    zero_row = jnp.zeros((1, D_MODEL), F32)
    dhf = _matmul(du, w['w_ffn_up'], tb=True, mnk=(t, D_MODEL, up_cols), tm=t, tk=FFN_TC, name="d_ffn_normed",
                  bias=zero_row + zz,
                  b_spec=pl.BlockSpec((1, tn_o, FFN_TC), lambda i, j, kk: (_ffn_perm(kk), j, 0)))
    zz = zz + relay('r', dhf) + settle('p')
    dx1, d_ffnnorm, dx1b = _rmsnorm_bwd(x1, s['ffn_norm_w'] + zz, dhf, dx2, width=D_MODEL, also_bf16=True,
                                        name="ffn_norm_bwd")
    dcat = _matmul(dx1b, w['w_out'], tb=True, name="d_mixed")
    d_wout = jnp.concatenate([_matmul(y_ssd, dx1b, ta=True, out_dtype=BF16, name="d_w_out_ssd"),
                              _matmul(o, dx1b, ta=True, out_dtype=BF16, name="d_w_out_mla")], axis=0)
    zz = zz + emit('s', {'w_out': d_wout.reshape(N_DEV, D_MODEL // N_DEV, D_MODEL)})
    ssd_saved = ssd_saved[:3] + (ssd_saved[3] + zz,) + ssd_saved[4:]
    dz, dxbc, d_raw, d_ssdnorm, d_conv_w, d_conv_b, d_dtb, d_alog, d_dskip = _ssd_backward(ssd_saved, dcat)
    zz = zz + relay('s', dz)
    mla_saved = mla_saved[:-1] + (mla_saved[-1] + zz,)
    dq_a, dckv, dkr, d_wq, d_wkv, d_qnorm, d_kvnorm = _mla_backward(mla_saved, dcat)
    d_raw = (d_raw + settle('r')).astype(BF16)
    dproj = jnp.concatenate([dz, dxbc, dq_a, dckv, dkr, d_raw], axis=1)
    d_win = _matmul(h, dproj, ta=True, out_dtype=BF16, tm=D_MODEL, name="d_w_in")
    zz = emit('t', {'w_in': _unpad_w_in_shards(d_win), 'w_q_b': _split_cols(_unpad_w_q(d_wq)),
                    'w_kv_b': _split_cols(d_wkv)}) + settle('s')
    dh = _matmul(dproj, w['w_in'], tb=True, bias=zero_row + zz, name="d_in_normed")
    zz = relay('t', dh)
    dx, d_mixnorm = _rmsnorm_bwd(x, s['mix_norm_w'] + zz, dh, dx1, width=D_MODEL, name="mix_norm_bwd")
    conv = {'conv_w': d_conv_w, 'ffn_conv_w': _deinterleave(d_fconv_w)}
    vec = {
        'mix_norm_w': d_mixnorm, 'conv_b': d_conv_b, 'dt_bias': d_dtb, 'a_log': d_alog, 'd_skip': d_dskip,
        'ssd_norm_w': d_ssdnorm, 'q_a_norm_w': d_qnorm, 'kv_a_norm_w': d_kvnorm, 'ffn_norm_w': d_ffnnorm,
        'ffn_conv_b': _deinterleave(d_fconv_b), 'ple_norm_w': d_plenorm, 'b_ple_gate': d_bgate,
        'ple_post_norm_w': d_post, 'final_norm_w': d_final,
    }
    return loss, dx, conv, vec


MESH = pl.DeviceIdType.MESH
FLIPS = ((0, 0, 1), (1, 0, 0), (0, 1, 0), (1, 1, 0), (1, 0, 1), (0, 1, 1), (1, 1, 1))


def _exchange(items, *, gather, name):
    n = len(items)

    def body(*refs):
        ins, outs = refs[:n], refs[n:2 * n]
        send_sems, recv_sems, local_sems = refs[2 * n:]
        x, y, c = lax.axis_index("x"), lax.axis_index("y"), lax.axis_index("c")
        me = 4 * x + 2 * y + c
        peers = [(jnp.where(fx, 1 - x, x), jnp.where(fy, 1 - y, y), jnp.where(fc, 1 - c, c)) for fx, fy, fc in FLIPS]
        slot = [4 * px + 2 * py + pc for px, py, pc in peers]
        local, sends = [], []
        for wi in range(n):
            cp = pltpu.make_async_copy(ins[wi] if gather else ins[wi].at[me], outs[wi].at[me], local_sems.at[wi])
            cp.start()
            local.append(cp)
            for k, peer in enumerate(peers):
                cp = pltpu.make_async_remote_copy(
                    src_ref=ins[wi] if gather else ins[wi].at[slot[k]], dst_ref=outs[wi].at[me],
                    send_sem=send_sems.at[k, wi], recv_sem=recv_sems.at[k, wi], device_id=peer, device_id_type=MESH)
                cp.start()
                sends.append(cp)
        for wi in range(n):
            for k, peer in enumerate(peers):
                pltpu.make_async_remote_copy(
                    src_ref=outs[wi].at[slot[k]], dst_ref=outs[wi].at[slot[k]], send_sem=send_sems.at[k, wi],
                    recv_sem=recv_sems.at[k, wi], device_id=peer, device_id_type=MESH).wait_recv()
        for cp in sends:
            cp.wait_send()
        for cp in local:
            cp.wait()

    hbm = pl.BlockSpec(memory_space=pltpu.HBM)
    out_shape = [jax.ShapeDtypeStruct(((N_DEV,) + v.shape) if gather else v.shape, v.dtype) for v in items]
    return pl.pallas_call(
        body, name=name, in_specs=[hbm] * n, out_specs=[hbm] * n, out_shape=out_shape,
        scratch_shapes=[pltpu.SemaphoreType.DMA((len(FLIPS), n)), pltpu.SemaphoreType.DMA((len(FLIPS), n)),
                        pltpu.SemaphoreType.DMA((n,))],
    )(*items)


HBM_SPEC = pl.BlockSpec(memory_space=pltpu.HBM)
SEM_SPEC = pl.BlockSpec(memory_space=pltpu.SEMAPHORE)
EFFECT = pltpu.SideEffectType.DATAFLOW_SIDE_EFFECTING


def _split_start(bufs, ncopies, plan, *, name):
    nb = len(bufs)

    def body(*refs):
        send_sems, recv_sems, token = refs[nb], refs[nb + 1], refs[2 * nb + 2]
        for i, (src, dst, peer, _) in enumerate(plan(refs[:nb])):
            pltpu.make_async_remote_copy(src_ref=src, dst_ref=dst, send_sem=send_sems.at[i], recv_sem=recv_sems.at[i],
                                         device_id=peer, device_id_type=MESH).start()
        token[...] = jnp.zeros_like(token)

    res = pl.pallas_call(
        body, name=name, in_specs=[HBM_SPEC] * nb,
        out_specs=[SEM_SPEC, SEM_SPEC] + [HBM_SPEC] * nb + [pl.BlockSpec(memory_space=pltpu.VMEM)],
        out_shape=[pltpu.SemaphoreType.DMA((ncopies,)), pltpu.SemaphoreType.DMA((ncopies,))]
        + [pltpu.HBM(v.shape, v.dtype) for v in bufs] + [jax.ShapeDtypeStruct((HALO, LANES), F32)],
        input_output_aliases={i: 2 + i for i in range(nb)},
        compiler_params=pltpu.CompilerParams(has_side_effects=EFFECT),
    )(*[pltpu.with_memory_space_constraint(v, pltpu.HBM) for v in bufs])
    return (res[0], res[1], list(res[2:2 + nb])), res[2 + nb]


def _split_wait(started, after, plan, local_plan, *, name):
    send_sems, recv_sems, bufs = started
    nb = len(bufs)
    nlocal = len(local_plan(bufs))

    def body(*refs):
        send_sems, recv_sems = refs[nb], refs[nb + 1]
        local_sems = refs[2 * nb + 3]
        local = []
        for j, (src, dst) in enumerate(local_plan(refs[:nb])):
            cp = pltpu.make_async_copy(src, dst, local_sems.at[j])
            cp.start()
            local.append(cp)
        for i, (src, _, peer, incoming) in enumerate(plan(refs[:nb])):
            cp = pltpu.make_async_remote_copy(src_ref=src, dst_ref=incoming, send_sem=send_sems.at[i],
                                              recv_sem=recv_sems.at[i], device_id=peer, device_id_type=MESH)
            cp.wait_send()
            cp.wait_recv()
        for cp in local:
            cp.wait()

    res = pl.pallas_call(
        body, name=name, in_specs=[HBM_SPEC] * nb + [SEM_SPEC, SEM_SPEC, pl.BlockSpec(memory_space=pl.ANY)],
        out_specs=[HBM_SPEC] * nb, out_shape=[pltpu.HBM(v.shape, v.dtype) for v in bufs],
        input_output_aliases={i: i for i in range(nb)},
        scratch_shapes=[pltpu.SemaphoreType.DMA((max(nlocal, 1),))],
        compiler_params=pltpu.CompilerParams(has_side_effects=EFFECT),
    )(*bufs, send_sems, recv_sems, after)
    return list(res)


def _hold(values, after, *, name):
    n = len(values)

    def body(*refs):
        del refs

    return list(pl.pallas_call(
        body, name=name, in_specs=[HBM_SPEC] * n + [pl.BlockSpec(memory_space=pl.ANY)], out_specs=[HBM_SPEC] * n,
        out_shape=[pltpu.HBM(v.shape, v.dtype) for v in values], input_output_aliases={i: i for i in range(n)},
    )(*values, after))


def _place():
    x, y, c = lax.axis_index("x"), lax.axis_index("y"), lax.axis_index("c")
    others = [((1 - x, y, c), 2 * (1 - x) + y), ((x, 1 - y, c), 2 * x + 1 - y), ((1 - x, 1 - y, c), 2 * (1 - x) + 1 - y)]
    return 4 * x + 2 * y + c, 2 * x + y, c, (x, y, 1 - c), others


def _gather1_plan(n):
    def plan(refs):
        me, _, _, sibling, others = _place()
        out = []
        for wi in range(n):
            item, land = refs[wi], refs[n + wi]
            out.append((item, land.at[me], sibling, land.at[me + 1 - 2 * lax.axis_index("c")]))
            for peer, chip in others:
                out.append((item, land.at[me], peer, land.at[2 * chip + lax.axis_index("c")]))
        return out

    return plan


def _gather1_local(n):
    def plan(refs):
        me = _place()[0]
        return [(refs[wi], refs[n + wi].at[me]) for wi in range(n)]

    return plan


def _gather2_plan(n):
    def plan(refs):
        _, _, c, sibling, others = _place()
        out = []
        for wi in range(n):
            land = refs[wi]
            for _, chip in others:
                out.append((land.at[2 * chip + c], land.at[2 * chip + c], sibling, land.at[2 * chip + 1 - c]))
        return out

    return plan


def _gather_start(items, *, name):
    lands = [lax.empty((N_DEV,) + v.shape, v.dtype) for v in items]
    return _split_start(items + lands, 4 * len(items), _gather1_plan(len(items)), name=name)


def _gather_forward(started, after, *, name):
    n = len(started[2]) // 2
    bufs = _split_wait(started, after, _gather1_plan(n), _gather1_local(n), name=name + "_wait")
    return _split_start(bufs[n:], 3 * n, _gather2_plan(n), name=name + "_start")


def _gather_finish(started, after, *, name):
    n = len(started[2])
    return _split_wait(started, after, _gather2_plan(n), lambda refs: [], name=name)


def _handshake(peers):
    barrier = pltpu.get_barrier_semaphore()
    for peer in peers:
        pl.semaphore_signal(barrier, inc=1, device_id=peer, device_id_type=MESH)
    pl.semaphore_wait(barrier, len(peers))


def _remote(src, dst, send_sem, recv_sem, peer):
    return pltpu.make_async_remote_copy(src_ref=src, dst_ref=dst, send_sem=send_sem, recv_sem=recv_sem, device_id=peer,
                                        device_id_type=MESH)


def _sequencer_gather(items, *, collective_id, name):
    n = len(items)
    srcs = [jax.new_ref(v, memory_space=pltpu.MemorySpace.HBM) for v in items]
    lands = [jax.empty_ref(jax.ShapeDtypeStruct((N_DEV,) + v.shape, v.dtype), memory_space=pltpu.MemorySpace.HBM)
             for v in items]
    dma = pltpu.SemaphoreType.DMA

    @pl.kernel(mesh=plsc.ScalarSubcoreMesh(axis_name="sequencer", num_cores=1), name=name,
               scratch_types=(dma((4 * n,)), dma((4 * n,)), dma((3 * n,)), dma((3 * n,)), dma((n,))),
               compiler_params=pltpu.CompilerParams(collective_id=collective_id))
    def launch(send1, recv1, send2, recv2, local_sems):
        _, _, _, sibling, others = _place()
        _handshake([sibling] + [peer for peer, _ in others])
        hop1 = _gather1_plan(n)(srcs + lands)
        hop2 = _gather2_plan(n)(lands)
        local = [pltpu.make_async_copy(src, dst, local_sems.at[j])
                 for j, (src, dst) in enumerate(_gather1_local(n)(srcs + lands))]
        for cp in local:
            cp.start()
        for i, (src, dst, peer, _) in enumerate(hop1):
            _remote(src, dst, send1.at[i], recv1.at[i], peer).start()
        for wi in range(n):
            for j in range(3):
                i1, i2 = 4 * wi + 1 + j, 3 * wi + j
                src, _, peer, incoming = hop1[i1]
                _remote(src, incoming, send1.at[i1], recv1.at[i1], peer).wait_recv()
                src, dst, peer, _ = hop2[i2]
                _remote(src, dst, send2.at[i2], recv2.at[i2], peer).start()
        for wi in range(n):
            src, _, peer, incoming = hop1[4 * wi]
            _remote(src, incoming, send1.at[4 * wi], recv1.at[4 * wi], peer).wait_recv()
        for i, (src, _, peer, incoming) in enumerate(hop2):
            cp = _remote(src, incoming, send2.at[i], recv2.at[i], peer)
            cp.wait_send()
            cp.wait_recv()
        for i, (src, dst, peer, _) in enumerate(hop1):
            _remote(src, dst, send1.at[i], recv1.at[i], peer).wait_send()
        for cp in local:
            cp.wait()

    launch()
    return [land[...] for land in lands]


def _sequencer_exchange(sources, land_shapes, ncopies, plan, local_plan, peers, *, collective_id, name):
    srcs = [jax.new_ref(v, memory_space=pltpu.MemorySpace.HBM) for v in sources]
    lands = [jax.empty_ref(s, memory_space=pltpu.MemorySpace.HBM) for s in land_shapes]
    nlocal = len(local_plan(srcs + lands))
    dma = pltpu.SemaphoreType.DMA

    @pl.kernel(mesh=plsc.ScalarSubcoreMesh(axis_name="sequencer", num_cores=1), name=name,
               scratch_types=(dma((ncopies,)), dma((ncopies,)), dma((max(nlocal, 1),))),
               compiler_params=pltpu.CompilerParams(collective_id=collective_id))
    def launch(send_sems, recv_sems, local_sems):
        _handshake(peers(_place()))
        copies = plan(srcs + lands)
        local = [pltpu.make_async_copy(src, dst, local_sems.at[j])
                 for j, (src, dst) in enumerate(local_plan(srcs + lands))]
        for cp in local:
            cp.start()
        for i, (src, dst, peer, _) in enumerate(copies):
            _remote(src, dst, send_sems.at[i], recv_sems.at[i], peer).start()
        for i, (src, _, peer, incoming) in enumerate(copies):
            cp = _remote(src, incoming, send_sems.at[i], recv_sems.at[i], peer)
            cp.wait_send()
            cp.wait_recv()
        for cp in local:
            cp.wait()

    launch()
    return [land[...] for land in lands]


def _sequencer_scatter_hop2(sums, *, collective_id, name):
    n = len(sums)
    shapes = [jax.ShapeDtypeStruct(v.shape, v.dtype) for v in sums]
    return _sequencer_exchange(sums, shapes, 3 * n, _scatter2_plan(n), _scatter2_local(n),
                               lambda place: [peer for peer, _ in place[4]], collective_id=collective_id, name=name)


N_CHIP = N_DEV // 2


def _scatter1_plan(n):
    def plan(refs):
        _, _, c, sibling, _ = _place()
        out = []
        for wi in range(n):
            parts, half = refs[wi], refs[n + wi]
            for chip in range(N_CHIP):
                out.append((parts.at[2 * chip + 1 - c], half.at[chip], sibling, half.at[chip]))
        return out

    return plan


def _scatter2_plan(n):
    def plan(refs):
        _, my_chip, _, _, others = _place()
        out = []
        for wi in range(n):
            sums, recv = refs[wi], refs[n + wi]
            for peer, chip in others:
                out.append((sums.at[chip], recv.at[my_chip], peer, recv.at[chip]))
        return out

    return plan


def _scatter2_local(n):
    def plan(refs):
        my_chip = _place()[1]
        return [(refs[wi].at[my_chip], refs[n + wi].at[my_chip]) for wi in range(n)]

    return plan


def _pair_add(parts, half, core, *, name):
    _, r, c = parts.shape
    tr = max(d for d in range(HALO, 257, HALO) if r % d == 0) if r > 256 else r
    parts4 = parts.reshape(N_CHIP, 2, r, c)

    def body(core_ref, p_ref, h_ref, o_ref):
        o_ref[...] = (p_ref[:, 0].astype(F32) + h_ref[...].astype(F32)).astype(o_ref.dtype)

    return pl.pallas_call(
        body, name=name,
        grid_spec=pltpu.PrefetchScalarGridSpec(
            num_scalar_prefetch=1, grid=(r // tr,),
            in_specs=[pl.BlockSpec((N_CHIP, 1, tr, c), lambda i, core_ref: (0, core_ref[0], i, 0)),
                      pl.BlockSpec((N_CHIP, tr, c), lambda i, core_ref: (0, i, 0))],
            out_specs=pl.BlockSpec((N_CHIP, tr, c), lambda i, core_ref: (0, i, 0))),
        out_shape=jax.ShapeDtypeStruct((N_CHIP, r, c), parts.dtype), compiler_params=_cp("parallel"),
    )(core, parts4, half)


def _scatter_start(parts, *, name):
    halves = [lax.empty((N_CHIP,) + v.shape[1:], v.dtype) for v in parts]
    return _split_start(parts + halves, N_CHIP * len(parts), _scatter1_plan(len(parts)), name=name)


def _adamw(parts, w, m, v, *, name):
    r, c = w.shape
    nparts = parts.shape[0]
    tr = max(d for d in range(HALO, 129, HALO) if r % d == 0) if r > 128 else r

    def body(p_ref, w_ref, m_ref, v_ref, g_ref, d_ref, mo_ref, vo_ref):
        g = p_ref[0].astype(F32)
        for k in range(1, nparts):
            g = g + p_ref[k].astype(F32)
        mn = ADAM_B1 * m_ref[...] + (1.0 - ADAM_B1) * g
        vn = ADAM_B2 * v_ref[...] + (1.0 - ADAM_B2) * (g * g)
        m_hat = mn / (1.0 - ADAM_B1 ** ADAM_STEP)
        v_hat = vn / (1.0 - ADAM_B2 ** ADAM_STEP)
        g_ref[...] = g
        d_ref[...] = -ADAM_LR * (m_hat / (jnp.sqrt(v_hat) + ADAM_EPS) + ADAM_WD * w_ref[...])
        mo_ref[...] = mn
        vo_ref[...] = vn

    blk = pl.BlockSpec((tr, c), lambda i: (i, 0))
    return pl.pallas_call(
        body, name=name, grid=(r // tr,), in_specs=[pl.BlockSpec((nparts, tr, c), lambda i: (0, i, 0)), blk, blk, blk],
        out_specs=[blk] * 4, out_shape=[jax.ShapeDtypeStruct((r, c), F32)] * 4, compiler_params=_cp("parallel"),
    )(parts, w, m, v)


def _pack_rows(vs, rows):
    lead = vs[0].shape[:-1] if vs[0].ndim > 1 else ()
    flat = jnp.concatenate(vs, axis=-1)
    pad = rows * LANES - flat.shape[-1]
    flat = jnp.pad(flat, [(0, 0)] * len(lead) + [(0, pad)])
    return flat.reshape(lead + (rows, LANES))


def kernel(x, p, positions, mix_norm_w, w_in, conv_w, conv_b, dt_bias, a_log, d_skip, ssd_norm_w, q_a_norm_w, w_q_b, kv_a_norm_w, w_kv_b, w_out, ffn_norm_w, w_ffn_up, ffn_conv_w, ffn_conv_b, w_ffn_down, ple_norm_w, w_ple_gate, b_ple_gate, w_ple_proj, ple_post_norm_w, final_norm_w, loss_target, m_mix_norm_w, m_w_in, m_conv_w, m_conv_b, m_dt_bias, m_a_log, m_d_skip, m_ssd_norm_w, m_q_a_norm_w, m_w_q_b, m_kv_a_norm_w, m_w_kv_b, m_w_out, m_ffn_norm_w, m_w_ffn_up, m_ffn_conv_w, m_ffn_conv_b, m_w_ffn_down, m_ple_norm_w, m_w_ple_gate, m_b_ple_gate, m_w_ple_proj, m_ple_post_norm_w, m_final_norm_w, v_mix_norm_w, v_w_in, v_conv_w, v_conv_b, v_dt_bias, v_a_log, v_d_skip, v_ssd_norm_w, v_q_a_norm_w, v_w_q_b, v_kv_a_norm_w, v_w_kv_b, v_w_out, v_ffn_norm_w, v_w_ffn_up, v_ffn_conv_w, v_ffn_conv_b, v_w_ffn_down, v_ple_norm_w, v_w_ple_gate, v_b_ple_gate, v_w_ple_proj, v_ple_post_norm_w, v_final_norm_w):
    given = dict(locals())
    shapes = {n: given[n].shape for n in WEIGHTS}
    w2 = {n: given[n].reshape(given[n].shape[-2:] if n in BIG or n in CONV else (1, -1)) for n in WEIGHTS}
    m2 = {n: given['m_' + n].reshape(w2[n].shape) for n in WEIGHTS}
    v2 = {n: given['v_' + n].reshape(w2[n].shape) for n in WEIGHTS}
    me = 4 * lax.axis_index("x") + 2 * lax.axis_index("y") + lax.axis_index("c")

    core = lax.axis_index("c").astype(jnp.int32).reshape(1)

    def shards(grp, zero):
        return [(w2[n] + zero).astype(BF16) if n in BIG else w2[n] + zero for n in WEIGHT_GROUPS[grp]]

    first, token = _gather_start(shards('a', 0.0), name="gather_a_hop1")
    first, token = _gather_forward(first, token, name="gather_a_hop2")
    zero = token[0, 0]
    later = dict(zip(WEIGHT_GROUPS['b'], _sequencer_gather(shards('b', zero), collective_id=1, name="gather_b")))
    later.update(zip(WEIGHT_GROUPS['c'], _sequencer_gather(shards('c', zero), collective_id=2, name="gather_c")))

    def get_w(grp, after):
        if grp == 'a':
            lands = dict(zip(WEIGHT_GROUPS[grp], _gather_finish(first, token, name="gather_a_done")))
        else:
            names = WEIGHT_GROUPS[grp]
            lands = dict(zip(names, _hold([later[n] for n in names], after, name="gather_" + grp + "_use")))
        return _assemble_weights(lands)

    scatters = {}

    hop_ids = {grp: 2 + 2 * i for i, grp in enumerate(GRAD_GROUPS)}

    def zero_of(arrays):
        return sum(v[(0,) * v.ndim].astype(F32) * 0.0 for v in arrays)

    def emit(grp, grads):
        scatters[grp], tok = _scatter_start([grads[n] for n in GRAD_GROUPS[grp]], name="scatter_" + grp + "_hop1")
        return tok[0, 0]

    def relay(grp, after):
        n = len(GRAD_GROUPS[grp])
        bufs = _split_wait(scatters[grp], after, _scatter1_plan(n), lambda refs: [], name="scatter_" + grp + "_hop1_wait")
        sums = [_pair_add(bufs[i], bufs[n + i], core, name="scatter_%s_add%d" % (grp, i)) for i in range(n)]
        scatters[grp] = _sequencer_scatter_hop2(sums, collective_id=hop_ids[grp] + 1, name="scatter_" + grp + "_hop2")
        return zero_of(sums)

    out_g, out_d, out_m, out_v = {}, {}, {}, {}

    def settle(grp):
        return zero_of(scatters[grp])

    def update(grp, behind=None):
        for n, parts in zip(GRAD_GROUPS[grp], scatters[grp]):
            wn = w2[n] if behind is None else w2[n] + behind
            out_g[n], out_d[n], out_m[n], out_v[n] = _adamw(parts, wn, m2[n], v2[n], name="adamw_" + n)

    vecs = {n: w2[n] for n in REPL}
    vecs['mix_norm_w'] = vecs['mix_norm_w'] + zero
    loss, dx, g_conv, g_vec = _local_step(x[0], p[0, 0], _rope_tables(positions), get_w, vecs, loss_target[0], emit,
                                          relay, settle)
    n_small = sum(g_vec[n].shape[1] for n in REPL) + sum(g_conv[n].size for n in CONV) + 1
    rows_small = -(-n_small // (LANES * HALO)) * HALO
    small = _pack_rows([g_vec[n] for n in REPL] + [g_conv[n].reshape(1, -1) for n in CONV] + [loss], rows_small)

    for grp in list(GRAD_GROUPS)[:-1]:
        update(grp)
    all_small = _exchange([small], gather=True, name="gather_small_grads")[0].reshape(N_DEV, rows_small * LANES)
    update(list(GRAD_GROUPS)[-1], zero_of([all_small]))
    pieces, off = [], 0
    for n in REPL:
        k = g_vec[n].shape[1]
        pieces.append(all_small[:, off:off + k])
        off += k
    for n in CONV:
        kw, cols = g_conv[n].shape
        full = all_small[:, off:off + kw * cols].reshape(N_DEV, kw, cols)
        mine = lax.dynamic_slice_in_dim(full, me * (cols // N_DEV), cols // N_DEV, axis=2)
        pieces.append(mine.reshape(N_DEV, kw * (cols // N_DEV)))
        off += kw * cols
    pieces.append(all_small[:, off:off + 1])
    small_names = REPL + CONV
    n_mine = sum(q.shape[1] for q in pieces)
    rows_mine = -(-n_mine // (LANES * HALO)) * HALO
    zero = jnp.zeros((1, 1), F32)
    packed = [_pack_rows([src[n].reshape(1, -1) for n in small_names] + [zero], rows_mine).reshape(rows_mine, LANES)
              for src in (w2, m2, v2)]
    sg, sd, sm, sv = _adamw(_pack_rows(pieces, rows_mine), *packed, name="adamw_small")
    off = 0
    for n in small_names:
        k = w2[n].size
        for dst, src in ((out_g, sg), (out_d, sd), (out_m, sm), (out_v, sv)):
            dst[n] = src.reshape(-1)[off:off + k].reshape(w2[n].shape)
        off += k
    total_loss = sg.reshape(-1)[off]

    outs = [total_loss, dx[None]]
    for res in (out_g, out_d, out_m, out_v):
        outs += [res[n].reshape(shapes[n]) for n in WEIGHTS]
    return tuple(outs)
```

```python
import math

import numpy as np
import jax
import jax.numpy as jnp
from jax import lax
from jax.experimental import pallas as pl
from jax.experimental.pallas import tpu as pltpu
from jax.experimental.pallas import tpu_sc as plsc

F32 = jnp.float32
BF16 = jnp.bfloat16
HI = lax.Precision.HIGHEST

D_MODEL = 2048
CHUNK = 64
D_SSM = 1024
SSD_P = 64
SSD_HEADS = 16
SSD_GROUPS = 2
SSD_N = 128
SSD_CONV = 4
SSD_CONV_DIM = D_SSM + 2 * SSD_GROUPS * SSD_N
MLA_HEADS = 8
MLA_NOPE = 128
MLA_ROPE = 64
MLA_V = 128
MLA_Q_RANK = 512
MLA_KV_RANK = 256
MLA_QK_PAD = 256
ROPE_THETA = 10000.0
D_FF = 5632
FFN_CONV = 3
PLE_DIM = 256
NORM_EPS = 1e-6
ADAM_LR, ADAM_B1, ADAM_B2, ADAM_EPS, ADAM_WD, ADAM_STEP = 0.001, 0.9, 0.999, 1e-08, 0.01, 10
N_DEV = 8

OFF_Z, OFF_XBC, OFF_QA, OFF_CKV, OFF_KR, OFF_DT, D_IN_PAD = 0, 1024, 2560, 3072, 3328, 3456, 3584
D_IN = 3408
LANES = 128
HALO = 8
VMEM_LIMIT = 56 * 1024 * 1024
FFN_TC = D_FF * 2 // N_DEV
FFN_PERM = (0, 4, 1, 5, 2, 6, 3, 7)
NEG = -1e30


def _cp(*sem):
    return pltpu.CompilerParams(dimension_semantics=tuple(sem), vmem_limit_bytes=VMEM_LIMIT)


def _tile(n, want):
    if n <= want:
        return n
    best = max(d for d in range(LANES, want + 1, LANES) if n % d == 0)
    return best


def _sigmoid(x):
    return 0.5 * (jnp.tanh(0.5 * x) + 1.0)


def _silu(x):
    return x * _sigmoid(x)


def _dsilu(x):
    s = _sigmoid(x)
    return s * (1.0 + x * (1.0 - s))


MM_TILE = 1408
MM_TK = 2816


def _matmul(a, b, *, ta=False, tb=False, out_dtype=F32, add=None, bias=None, tm=MM_TILE, tn=MM_TILE, tk=MM_TK, name,
            mnk=None, a_spec=None, b_spec=None, o_spec=None, o_shape=None):
    if mnk is None:
        m, k = (a.shape[1], a.shape[0]) if ta else a.shape
        n = b.shape[0] if tb else b.shape[1]
        assert k == (b.shape[1] if tb else b.shape[0])
    else:
        m, n, k = mnk
    tm, tn, tk = _tile(m, tm), _tile(n, tn), _tile(k, tk)
    nk = k // tk
    dims = (((0 if ta else 1,), (1 if tb else 0,)), ((), ()))

    def body(*refs):
        a_ref, b_ref = refs[0], refs[1]
        pos = 2
        add_ref = bias_ref = None
        if add is not None:
            add_ref = refs[pos]
            pos += 1
        if bias is not None:
            bias_ref = refs[pos]
            pos += 1
        o_ref = refs[pos]
        kk = pl.program_id(2)
        av = a_ref[...]
        bv = b_ref[...]
        av = av.reshape(av.shape[-2:]).astype(BF16)
        bv = bv.reshape(bv.shape[-2:]).astype(BF16)
        prod = lax.dot_general(av, bv, dims, preferred_element_type=F32)

        def finish(r):
            if bias_ref is not None:
                r = r + bias_ref[...]
            if add_ref is not None:
                r = r + add_ref[...].astype(F32)
            o_ref[...] = r.astype(out_dtype).reshape(o_ref.shape)

        if nk == 1:
            finish(prod)
        else:
            acc_ref = refs[pos + 1]

            @pl.when(kk == 0)
            def _():
                acc_ref[...] = prod

            @pl.when(kk > 0)
            def _():
                acc_ref[...] += prod

            @pl.when(kk == nk - 1)
            def _():
                finish(acc_ref[...])

    if a_spec is None:
        a_spec = (pl.BlockSpec((tk, tm), lambda i, j, kk: (kk, i)) if ta
                  else pl.BlockSpec((tm, tk), lambda i, j, kk: (i, kk)))
    if b_spec is None:
        b_spec = (pl.BlockSpec((tn, tk), lambda i, j, kk: (j, kk)) if tb
                  else pl.BlockSpec((tk, tn), lambda i, j, kk: (kk, j)))
    if o_spec is None:
        o_spec = pl.BlockSpec((tm, tn), lambda i, j, kk: (i, j))
    if o_shape is None:
        o_shape = (m, n)
    in_specs = [a_spec, b_spec]
    args = [a, b]
    if add is not None:
        in_specs.append(pl.BlockSpec((tm, tn), lambda i, j, kk: (i, j)))
        args.append(add)
    if bias is not None:
        in_specs.append(pl.BlockSpec((1, tn), lambda i, j, kk: (0, j)))
        args.append(bias)
    return pl.pallas_call(
        body, name=name, grid=(m // tm, n // tn, nk), in_specs=in_specs, out_specs=o_spec,
        out_shape=jax.ShapeDtypeStruct(o_shape, out_dtype),
        scratch_shapes=[pltpu.VMEM((tm, tn), F32)] if nk > 1 else [],
        compiler_params=_cp("parallel", "parallel", "arbitrary"),
    )(*args)


def _rmsnorm_fwd(x, w, *, width, cblk=0, out_dtype=BF16, tr=256, name):
    t = x.shape[0]

    def body(x_ref, w_ref, o_ref):
        xv = x_ref[...].astype(F32)
        r = lax.rsqrt(jnp.mean(xv * xv, axis=-1, keepdims=True) + NORM_EPS)
        o_ref[...] = (xv * r * w_ref[...]).astype(out_dtype)

    return pl.pallas_call(
        body, name=name, grid=(t // tr,),
        in_specs=[pl.BlockSpec((tr, width), lambda i: (i, cblk)), pl.BlockSpec((1, width), lambda i: (0, 0))],
        out_specs=pl.BlockSpec((tr, width), lambda i: (i, 0)),
        out_shape=jax.ShapeDtypeStruct((t, width), out_dtype),
        compiler_params=_cp("parallel"),
    )(x, w)


def _rmsnorm_bwd(x, w, dy, add=None, *, width, cblk=0, out_dtype=F32, also_bf16=False, tr=256, name):
    t = x.shape[0]

    def body(*refs):
        refs = list(refs)
        dxb_ref = refs.pop() if also_bf16 else None
        if add is None:
            x_ref, w_ref, dy_ref, dx_ref, dw_ref = refs
            add_ref = None
        else:
            x_ref, w_ref, dy_ref, add_ref, dx_ref, dw_ref = refs
        xv = x_ref[...].astype(F32)
        dyv = dy_ref[...].astype(F32)
        r = lax.rsqrt(jnp.mean(xv * xv, axis=-1, keepdims=True) + NORM_EPS)
        xh = xv * r
        g = dyv * w_ref[...]
        dx = r * (g - xh * jnp.mean(g * xh, axis=-1, keepdims=True))
        if add_ref is not None:
            dx = dx + add_ref[...].astype(F32)
        dx_ref[...] = dx.astype(out_dtype)
        if dxb_ref is not None:
            dxb_ref[...] = dx.astype(BF16)

        @pl.when(pl.program_id(0) == 0)
        def _():
            dw_ref[...] = jnp.zeros_like(dw_ref)

        dw_ref[...] += jnp.sum(dyv * xh, axis=0, keepdims=True)

    in_specs = [pl.BlockSpec((tr, width), lambda i: (i, cblk)), pl.BlockSpec((1, width), lambda i: (0, 0)),
                pl.BlockSpec((tr, width), lambda i: (i, 0))]
    args = [x, w, dy]
    if add is not None:
        in_specs.append(pl.BlockSpec((tr, width), lambda i: (i, 0)))
        args.append(add)
    blk = pl.BlockSpec((tr, width), lambda i: (i, 0))
    return pl.pallas_call(
        body, name=name, grid=(t // tr,), in_specs=in_specs,
        out_specs=[blk, pl.BlockSpec((1, width), lambda i: (0, 0))] + ([blk] if also_bf16 else []),
        out_shape=[jax.ShapeDtypeStruct((t, width), out_dtype), jax.ShapeDtypeStruct((1, width), F32)]
        + ([jax.ShapeDtypeStruct((t, width), BF16)] if also_bf16 else []),
        compiler_params=_cp("arbitrary"),
    )(*args)


def _shift_down(prev_halo, cur, j):
    if j == 0:
        return cur
    ext = jnp.concatenate([prev_halo, cur], axis=0)
    return pltpu.roll(ext, j, axis=0)[HALO:]


def _shift_up(cur, next_halo, j):
    if j == 0:
        return cur
    ext = jnp.concatenate([cur, next_halo], axis=0)
    return pltpu.roll(ext, ext.shape[0] - j, axis=0)[:cur.shape[0]]


def _conv_rows(prev, cur, w, b, kw):
    shifted = [cur]
    out = b + w[kw - 1:kw] * cur
    for j in range(1, kw):
        sh = _shift_down(prev, cur, j)
        shifted.append(sh)
        out = out + w[kw - 1 - j:kw - j] * sh
    return out, shifted


def _act_fwd(c, glu):
    if glu:
        half = c.shape[1] // 2
        return _silu(c[:, :half]) * c[:, half:]
    return _silu(c)


def _act_bwd(c, dout, glu):
    if glu:
        half = c.shape[1] // 2
        g, up = c[:, :half], c[:, half:]
        s = _sigmoid(g)
        gs = g * s
        return jnp.concatenate([dout * up * (s + gs * (1.0 - s)), dout * gs], axis=1)
    return dout * _dsilu(c)


def _conv_act_fwd(u, w, b, *, kw, glu, tc, coff, ncols, out_dtype, tr=256, name):
    t = u.shape[0]
    nb = ncols // tc
    oc = tc // 2 if glu else tc

    def body(u_ref, uh_ref, w_ref, b_ref, o_ref):
        prev = jnp.where(pl.program_id(0) == 0, 0.0, uh_ref[...])
        c, _ = _conv_rows(prev, u_ref[...], w_ref[...], b_ref[...], kw)
        o_ref[...] = _act_fwd(c, glu).astype(out_dtype)

    return pl.pallas_call(
        body, name=name, grid=(t // tr, nb),
        in_specs=[pl.BlockSpec((tr, tc), lambda i, j: (i, j + coff)),
                  pl.BlockSpec((HALO, tc), lambda i, j: (jnp.maximum(i * (tr // HALO) - 1, 0), j + coff)),
                  pl.BlockSpec((kw, tc), lambda i, j: (0, j)), pl.BlockSpec((1, tc), lambda i, j: (0, j))],
        out_specs=pl.BlockSpec((tr, oc), lambda i, j: (i, j)),
        out_shape=jax.ShapeDtypeStruct((t, nb * oc), out_dtype),
        compiler_params=_cp("parallel", "parallel"),
    )(u, u, w, b)


def _conv_act_bwd(u, w, b, dout, *, kw, glu, tc, coff, ncols, tr=256, name):
    t = u.shape[0]
    nb = ncols // tc
    nt = t // tr
    oc = tc // 2 if glu else tc

    def body(u_ref, up_ref, un_ref, d_ref, dn_ref, w_ref, b_ref, du_ref, dw_ref, db_ref):
        i = pl.program_id(1)
        cur, nxt, wv, bv = u_ref[...], un_ref[...], w_ref[...], b_ref[...]
        prev = jnp.where(i == 0, 0.0, up_ref[...])
        c_cur, shifted = _conv_rows(prev, cur, wv, bv, kw)
        c_nxt, _ = _conv_rows(cur[tr - HALO:], nxt, wv, bv, kw)
        d_cur = _act_bwd(c_cur, d_ref[...].astype(F32), glu)
        d_nxt = _act_bwd(c_nxt, jnp.where(i == nt - 1, 0.0, dn_ref[...].astype(F32)), glu)
        du = wv[kw - 1:kw] * d_cur
        for j in range(1, kw):
            du = du + wv[kw - 1 - j:kw - j] * _shift_up(d_cur, d_nxt, j)
        du_ref[...] = du.astype(BF16)

        @pl.when(i == 0)
        def _():
            dw_ref[...] = jnp.zeros_like(dw_ref)
            db_ref[...] = jnp.zeros_like(db_ref)

        db_ref[...] += jnp.sum(d_cur, axis=0, keepdims=True)
        dw_ref[...] += jnp.concatenate(
            [jnp.sum(d_cur * shifted[kw - 1 - k], axis=0, keepdims=True) for k in range(kw)], axis=0)

    nh = tr // HALO
    return pl.pallas_call(
        body, name=name, grid=(nb, nt),
        in_specs=[pl.BlockSpec((tr, tc), lambda j, i: (i, j + coff)),
                  pl.BlockSpec((HALO, tc), lambda j, i: (jnp.maximum(i * nh - 1, 0), j + coff)),
                  pl.BlockSpec((HALO, tc), lambda j, i: (jnp.minimum((i + 1) * nh, t // HALO - 1), j + coff)),
                  pl.BlockSpec((tr, oc), lambda j, i: (i, j)),
                  pl.BlockSpec((HALO, oc), lambda j, i: (jnp.minimum((i + 1) * nh, t // HALO - 1), j)),
                  pl.BlockSpec((kw, tc), lambda j, i: (0, j)), pl.BlockSpec((1, tc), lambda j, i: (0, j))],
        out_specs=[pl.BlockSpec((tr, tc), lambda j, i: (i, j)), pl.BlockSpec((kw, tc), lambda j, i: (0, j)),
                   pl.BlockSpec((1, tc), lambda j, i: (0, j))],
        out_shape=[jax.ShapeDtypeStruct((t, ncols), BF16), jax.ShapeDtypeStruct((kw, ncols), F32),
                   jax.ShapeDtypeStruct((1, ncols), F32)],
        compiler_params=_cp("parallel", "arbitrary"),
    )(u, u, u, dout, dout, w, b)


def _ple_loss(x2, gl, pe, pw, fw, target, *, tr=256, name):
    t, d = x2.shape

    def body(x_ref, gl_ref, pe_ref, pw_ref, fw_ref, t_ref, l_ref, dx_ref, dfw_ref, dgl_ref, db_ref, dpe_ref, dpw_ref):
        pv, pwv, wv = pe_ref[...], pw_ref[...], fw_ref[...]
        gate = _sigmoid(gl_ref[...])
        rp = lax.rsqrt(jnp.mean(pv * pv, axis=-1, keepdims=True) + NORM_EPS)
        ph = pv * rp
        e = ph * pwv
        x3 = x_ref[...] + gate * e
        r = lax.rsqrt(jnp.mean(x3 * x3, axis=-1, keepdims=True) + NORM_EPS)
        xh = x3 * r
        err = xh * wv - t_ref[...]
        dy = err * (1.0 / d)
        g = dy * wv
        dx = r * (g - xh * jnp.mean(g * xh, axis=-1, keepdims=True))
        dx_ref[...] = dx
        dgl = dx * e * gate * (1.0 - gate)
        de = dx * gate
        gg = de * pwv
        dgl_ref[...] = dgl.astype(BF16)
        dpe_ref[...] = (rp * (gg - ph * jnp.mean(gg * ph, axis=-1, keepdims=True))).astype(BF16)

        @pl.when(pl.program_id(0) == 0)
        def _():
            for ref in (l_ref, dfw_ref, db_ref, dpw_ref):
                ref[...] = jnp.zeros_like(ref)

        l_ref[...] += 0.5 * jnp.sum(jnp.mean(err * err, axis=-1, keepdims=True), axis=0, keepdims=True)
        dfw_ref[...] += jnp.sum(dy * xh, axis=0, keepdims=True)
        db_ref[...] += jnp.sum(dgl, axis=0, keepdims=True)
        dpw_ref[...] += jnp.sum(de * ph, axis=0, keepdims=True)

    blk = pl.BlockSpec((tr, d), lambda i: (i, 0))
    row = pl.BlockSpec((1, d), lambda i: (0, 0))
    rowf = jax.ShapeDtypeStruct((1, d), F32)
    return pl.pallas_call(
        body, name=name, grid=(t // tr,), in_specs=[blk, blk, blk, row, row, blk],
        out_specs=[pl.BlockSpec((1, 1), lambda i: (0, 0)), blk, row, blk, row, blk, row],
        out_shape=[jax.ShapeDtypeStruct((1, 1), F32), jax.ShapeDtypeStruct((t, d), F32), rowf,
                   jax.ShapeDtypeStruct((t, d), BF16), rowf, jax.ShapeDtypeStruct((t, d), BF16), rowf],
        compiler_params=_cp("arbitrary"),
    )(x2, gl, pe, pw, fw, target)


def _rope(blk, tab_ref):
    return blk * tab_ref[0] + pltpu.roll(blk, 96, axis=1) * tab_ref[1] + pltpu.roll(blk, 32, axis=1) * tab_ref[2]


def _unrope(g, tab_ref):
    return g * tab_ref[0] + pltpu.roll(g * tab_ref[1], 32, axis=1) + pltpu.roll(g * tab_ref[2], 96, axis=1)


def _mla_prep(q, kv, proj, tabs, *, tr=512, name):
    t = q.shape[0]

    def body(q_ref, kv_ref, kr_ref, tab_ref, qo_ref, ko_ref, vo_ref, vt_ref):
        qv, kvv = q_ref[...], kv_ref[...]
        qo_ref[0, :, :MLA_NOPE] = qv[:, :MLA_NOPE].astype(BF16)
        qo_ref[0, :, MLA_NOPE:] = _rope(qv[:, MLA_NOPE:], tab_ref).astype(BF16)
        ko_ref[0, :, :MLA_NOPE] = kvv[:, :MLA_NOPE].astype(BF16)
        ko_ref[0, :, MLA_NOPE:] = _rope(kr_ref[...], tab_ref).astype(BF16)
        vo_ref[0] = kvv[:, MLA_NOPE:].astype(BF16)
        for blk in range(tr // ATT_BLK):
            vt_ref[0, blk] = kvv[blk * ATT_BLK:(blk + 1) * ATT_BLK, MLA_NOPE:].T.astype(BF16)

    return pl.pallas_call(
        body, name=name, grid=(t // tr, MLA_HEADS),
        in_specs=[pl.BlockSpec((tr, MLA_QK_PAD), lambda i, h: (i, h)),
                  pl.BlockSpec((tr, MLA_NOPE + MLA_V), lambda i, h: (i, h)),
                  pl.BlockSpec((tr, LANES), lambda i, h: (i, OFF_KR // LANES)),
                  pl.BlockSpec((3, tr, LANES), lambda i, h: (0, i, 0))],
        out_specs=[pl.BlockSpec((1, tr, MLA_QK_PAD), lambda i, h: (h, i, 0)),
                   pl.BlockSpec((1, tr, MLA_QK_PAD), lambda i, h: (h, i, 0)),
                   pl.BlockSpec((1, tr, MLA_V), lambda i, h: (h, i, 0)),
                   pl.BlockSpec((1, tr // ATT_BLK, MLA_V, ATT_BLK), lambda i, h: (h, i, 0, 0))],
        out_shape=[jax.ShapeDtypeStruct((MLA_HEADS, t, MLA_QK_PAD), BF16),
                   jax.ShapeDtypeStruct((MLA_HEADS, t, MLA_QK_PAD), BF16),
                   jax.ShapeDtypeStruct((MLA_HEADS, t, MLA_V), BF16),
                   jax.ShapeDtypeStruct((MLA_HEADS, t // ATT_BLK, MLA_V, ATT_BLK), BF16)],
        compiler_params=_cp("parallel", "parallel"),
    )(q, kv, proj, tabs)


def _mla_unprep(dq3, dk3, dv3, tabs, *, tr=256, name):
    t = dq3.shape[1]

    def body(dq_ref, dk_ref, dv_ref, tab_ref, qo_ref, kvo_ref, kro_ref):
        kr = jnp.zeros((tr, LANES), F32)
        for h in range(MLA_HEADS):
            c0 = h * MLA_QK_PAD
            qo_ref[:, c0:c0 + MLA_NOPE] = dq_ref[h, :, :MLA_NOPE].astype(BF16)
            qo_ref[:, c0 + MLA_NOPE:c0 + MLA_QK_PAD] = _unrope(dq_ref[h, :, MLA_NOPE:], tab_ref).astype(BF16)
            kvo_ref[:, c0:c0 + MLA_NOPE] = dk_ref[h, :, :MLA_NOPE].astype(BF16)
            kvo_ref[:, c0 + MLA_NOPE:c0 + MLA_QK_PAD] = dv_ref[h].astype(BF16)
            kr = kr + dk_ref[h, :, MLA_NOPE:]
        kro_ref[...] = _unrope(kr, tab_ref).astype(BF16)

    return pl.pallas_call(
        body, name=name, grid=(t // tr,),
        in_specs=[pl.BlockSpec((MLA_HEADS, tr, MLA_QK_PAD), lambda i: (0, i, 0)),
                  pl.BlockSpec((MLA_HEADS, tr, MLA_QK_PAD), lambda i: (0, i, 0)),
                  pl.BlockSpec((MLA_HEADS, tr, MLA_V), lambda i: (0, i, 0)),
                  pl.BlockSpec((3, tr, LANES), lambda i: (0, i, 0))],
        out_specs=[pl.BlockSpec((tr, MLA_HEADS * MLA_QK_PAD), lambda i: (i, 0)),
                   pl.BlockSpec((tr, MLA_HEADS * MLA_QK_PAD), lambda i: (i, 0)),
                   pl.BlockSpec((tr, LANES), lambda i: (i, 0))],
        out_shape=[jax.ShapeDtypeStruct((t, MLA_HEADS * MLA_QK_PAD), BF16),
                   jax.ShapeDtypeStruct((t, MLA_HEADS * MLA_QK_PAD), BF16),
                   jax.ShapeDtypeStruct((t, LANES), BF16)],
        compiler_params=_cp("parallel"),
    )(dq3, dk3, dv3, tabs)


ATT_BLK = 512
ATT_SCALE = 1.0 / math.sqrt(MLA_NOPE + MLA_ROPE)
_NT = (((1,), (1,)), ((), ()))
_TN = (((0,), (0,)), ((), ()))


def _att_scores_t(k, q, diagonal):
    s = lax.dot_general(k, q, _NT, preferred_element_type=F32) * ATT_SCALE
    if not diagonal:
        return s
    key = lax.broadcasted_iota(jnp.int32, s.shape, 0)
    query = lax.broadcasted_iota(jnp.int32, s.shape, 1)
    return jnp.where((key >> 6) <= (query >> 6), s, NEG)


def _att_rows(i):
    return pl.ds(pl.multiple_of(i * ATT_BLK, ATT_BLK), ATT_BLK)


ATT_HEADS = 2


def _attn_fwd(q3, k3, vt4, *, name):
    t = q3.shape[1]
    nq = t // ATT_BLK

    def body(q_ref, k_ref, vt_ref, o_ref, lse_ref):
        qi = pl.program_id(1)
        qs = [q_ref[hh] for hh in range(ATT_HEADS)]

        def step(j, carry, diagonal=False):
            out = []
            for hh, (m, l, acc) in enumerate(carry):
                s = _att_scores_t(k_ref[hh, _att_rows(j), :], qs[hh], diagonal)
                m_new = jnp.maximum(m, jnp.max(s, axis=0, keepdims=True))
                p = jnp.exp(s - m_new)
                alpha = jnp.exp(m - m_new)
                l = alpha * l + jnp.sum(p, axis=0, keepdims=True)
                acc = alpha * acc + jnp.dot(vt_ref[hh, j], p.astype(BF16), preferred_element_type=F32)
                out.append((m_new, l, acc))
            return tuple(out)

        init = tuple((jnp.full((1, ATT_BLK), NEG, F32), jnp.zeros((1, ATT_BLK), F32),
                      jnp.zeros((MLA_V, ATT_BLK), F32)) for _ in range(ATT_HEADS))
        done = step(qi, lax.fori_loop(0, qi, step, init), diagonal=True)
        for hh, (m, l, acc) in enumerate(done):
            o_ref[:, hh * MLA_V:(hh + 1) * MLA_V] = (acc / l).T
            lse_ref[hh, 0] = m + jnp.log(l)

    return pl.pallas_call(
        body, name=name, grid=(MLA_HEADS // ATT_HEADS, nq),
        in_specs=[pl.BlockSpec((ATT_HEADS, ATT_BLK, MLA_QK_PAD), lambda h, i: (h, i, 0)),
                  pl.BlockSpec((ATT_HEADS, t, MLA_QK_PAD), lambda h, i: (h, 0, 0)),
                  pl.BlockSpec((ATT_HEADS, nq, MLA_V, ATT_BLK), lambda h, i: (h, 0, 0, 0))],
        out_specs=[pl.BlockSpec((ATT_BLK, ATT_HEADS * MLA_V), lambda h, i: (i, h)),
                   pl.BlockSpec((ATT_HEADS, 1, 1, ATT_BLK), lambda h, i: (h, i, 0, 0))],
        out_shape=[jax.ShapeDtypeStruct((t, MLA_HEADS * MLA_V), F32),
                   jax.ShapeDtypeStruct((MLA_HEADS, nq, 1, ATT_BLK), F32)],
        compiler_params=_cp("parallel", "parallel"),
    )(q3, k3, vt4)


def _attn_bwd(q3, k3, v3, o, dcat, lse, *, name):
    t = q3.shape[1]
    nq = t // ATT_BLK
    wide = ATT_HEADS * MLA_V

    def body(q_ref, k_ref, v_ref, o_ref, do_ref, lse_ref, dq_ref, dk_ref, dv_ref, delta_ref):
        kj = pl.program_id(1)

        @pl.when(kj == 0)
        def _():
            dq_ref[...] = jnp.zeros_like(dq_ref)
            ones = jnp.ones((HALO, MLA_V), F32)
            for i in range(nq):
                rows = pl.ds(i * ATT_BLK, ATT_BLK)
                prod = o_ref[rows, :] * do_ref[rows, :]
                for hh in range(ATT_HEADS):
                    delta_ref[hh, i] = lax.dot_general(ones, prod[:, hh * MLA_V:(hh + 1) * MLA_V], _NT, precision=HI,
                                                       preferred_element_type=F32)

        def step(i, carry, diagonal=False):
            rows = _att_rows(i)
            out = []
            for hh, (dk, dv) in enumerate(carry):
                k, v = k_ref[hh], v_ref[hh]
                q = q_ref[hh, rows, :]
                dob = do_ref[rows, hh * MLA_V:(hh + 1) * MLA_V].astype(BF16)
                p = jnp.exp(_att_scores_t(k, q, diagonal) - lse_ref[hh, i])
                dv = dv + jnp.dot(p.astype(BF16), dob, preferred_element_type=F32)
                dp = lax.dot_general(v, dob, _NT, preferred_element_type=F32)
                ds = (p * (dp - delta_ref[hh, i, 0:1, :]) * ATT_SCALE).astype(BF16)
                dk = dk + jnp.dot(ds, q, preferred_element_type=F32)
                dq_ref[hh, rows, :] += lax.dot_general(ds, k, _TN, preferred_element_type=F32)
                out.append((dk, dv))
            return tuple(out)

        init = tuple((jnp.zeros((ATT_BLK, MLA_QK_PAD), F32), jnp.zeros((ATT_BLK, MLA_V), F32))
                     for _ in range(ATT_HEADS))
        done = lax.fori_loop(kj + 1, nq, step, step(kj, init, diagonal=True))
        for hh, (dk, dv) in enumerate(done):
            dk_ref[hh] = dk
            dv_ref[hh] = dv

    return pl.pallas_call(
        body, name=name, grid=(MLA_HEADS // ATT_HEADS, nq),
        in_specs=[pl.BlockSpec((ATT_HEADS, t, MLA_QK_PAD), lambda h, j: (h, 0, 0)),
                  pl.BlockSpec((ATT_HEADS, ATT_BLK, MLA_QK_PAD), lambda h, j: (h, j, 0)),
                  pl.BlockSpec((ATT_HEADS, ATT_BLK, MLA_V), lambda h, j: (h, j, 0)),
                  pl.BlockSpec((t, wide), lambda h, j: (0, h)),
                  pl.BlockSpec((t, wide), lambda h, j: (0, MLA_HEADS // ATT_HEADS + h)),
                  pl.BlockSpec((ATT_HEADS, nq, 1, ATT_BLK), lambda h, j: (h, 0, 0, 0))],
        out_specs=[pl.BlockSpec((ATT_HEADS, t, MLA_QK_PAD), lambda h, j: (h, 0, 0)),
                   pl.BlockSpec((ATT_HEADS, ATT_BLK, MLA_QK_PAD), lambda h, j: (h, j, 0)),
                   pl.BlockSpec((ATT_HEADS, ATT_BLK, MLA_V), lambda h, j: (h, j, 0))],
        out_shape=[jax.ShapeDtypeStruct((MLA_HEADS, t, MLA_QK_PAD), F32),
                   jax.ShapeDtypeStruct((MLA_HEADS, t, MLA_QK_PAD), F32),
                   jax.ShapeDtypeStruct((MLA_HEADS, t, MLA_V), F32)],
        scratch_shapes=[pltpu.VMEM((ATT_HEADS, nq, HALO, ATT_BLK), F32)],
        compiler_params=_cp("parallel", "arbitrary"),
    )(q3, k3, v3, o, dcat, lse)


def _ssd_prep(proj, bias128, alog128, *, name):
    t = proj.shape[0]
    nc = t // CHUNK

    def body(raw_ref, b_ref, al_ref, dt_ref, cs_ref, a_ref):
        xv = raw_ref[...] + b_ref[...]
        dt = jnp.maximum(xv, 0.0) + jnp.log(1.0 + jnp.exp(-jnp.abs(xv)))
        a = -jnp.exp(al_ref[...])
        adt = (dt * a).reshape(nc, CHUNK, LANES)
        li = lax.broadcasted_iota(jnp.int32, (nc, CHUNK, CHUNK), 1)
        si = lax.broadcasted_iota(jnp.int32, (nc, CHUNK, CHUNK), 2)
        tril = jnp.where(si <= li, 1.0, 0.0).astype(F32)
        cs = lax.dot_general(tril, adt, (((2,), (1,)), ((0,), (0,))), precision=HI, preferred_element_type=F32)
        dt_ref[...] = dt
        cs_ref[...] = cs.reshape(t, LANES)
        a_ref[...] = a

    blk = pl.BlockSpec((t, LANES), lambda i: (0, 0))
    row = pl.BlockSpec((1, LANES), lambda i: (0, 0))
    return pl.pallas_call(
        body, name=name, grid=(1,),
        in_specs=[pl.BlockSpec((t, LANES), lambda i: (0, OFF_DT // LANES)), row, row],
        out_specs=[blk, blk, row],
        out_shape=[jax.ShapeDtypeStruct((t, LANES), F32), jax.ShapeDtypeStruct((t, LANES), F32),
                   jax.ShapeDtypeStruct((1, LANES), F32)],
        compiler_params=_cp("arbitrary"),
    )(proj, bias128, alog128)


def _ssd_prep_bwd(ddt128, dadt128, proj, bias128, dt128, a128, dd_h, *, name):
    t = proj.shape[0]

    def body(ddt_ref, dadt_ref, raw_ref, b_ref, dt_ref, a_ref, dd_ref, draw_ref, db_ref, dal_ref, dds_ref):
        draw = ddt_ref[...] * _sigmoid(raw_ref[...] + b_ref[...])
        draw_ref[...] = draw.astype(BF16)
        db_ref[...] = jnp.sum(draw, axis=0, keepdims=True)
        dal_ref[...] = jnp.sum(dadt_ref[...] * dt_ref[...], axis=0, keepdims=True) * a_ref[...]
        dds_ref[...] = jnp.sum(dd_ref[...], axis=-1, keepdims=True)

    blk = pl.BlockSpec((t, LANES), lambda i: (0, 0))
    row = pl.BlockSpec((1, LANES), lambda i: (0, 0))
    return pl.pallas_call(
        body, name=name, grid=(1,),
        in_specs=[blk, blk, pl.BlockSpec((t, LANES), lambda i: (0, OFF_DT // LANES)), row, blk, row,
                  pl.BlockSpec((SSD_HEADS, SSD_P), lambda i: (0, 0))],
        out_specs=[blk, row, row, pl.BlockSpec((SSD_HEADS, 1), lambda i: (0, 0))],
        out_shape=[jax.ShapeDtypeStruct((t, LANES), BF16), jax.ShapeDtypeStruct((1, LANES), F32),
                   jax.ShapeDtypeStruct((1, LANES), F32), jax.ShapeDtypeStruct((SSD_HEADS, 1), F32)],
        compiler_params=_cp("arbitrary"),
    )(ddt128, dadt128, proj, bias128, dt128, a128, dd_h)


def _bdot(a, b, ca, cb, precision=None):
    return lax.dot_general(a, b, (((ca,), (cb,)), ((0,), (0,))), precision=precision, preferred_element_type=F32)


def _pieces(x):
    hi = x.astype(BF16)
    rest = x - hi.astype(F32)
    mid = rest.astype(BF16)
    return hi, mid, (rest - mid.astype(F32)).astype(BF16)


def _bdot_sum(a, b, ca, cb, split):
    other = (b if split == 0 else a).astype(BF16)
    out = None
    for piece in _pieces(a if split == 0 else b):
        term = _bdot(piece, other, ca, cb) if split == 0 else _bdot(other, piece, ca, cb)
        out = term if out is None else out + term
    return out


def _head_matrices():
    eye, zero = jnp.eye(SSD_P, dtype=F32), jnp.zeros((SSD_P, SSD_P), F32)
    pick = jnp.stack([jnp.concatenate([eye, zero], axis=0), jnp.concatenate([zero, eye], axis=0)])
    return pick, pick.transpose(0, 2, 1)


def _move(x, sel):
    selb = sel.astype(BF16)
    hi = x.astype(BF16)
    rest = x - hi.astype(F32)
    mid = rest.astype(BF16)
    low = (rest - mid.astype(F32)).astype(BF16)
    out = jnp.dot(hi, selb, preferred_element_type=F32)
    out = out + jnp.dot(mid, selb, preferred_element_type=F32)
    return out + jnp.dot(low, selb, preferred_element_type=F32)


def _pick_head(pair_ref, pick_ref, h):
    return _move(pair_ref[...], pick_ref[h % 2])


def _place_head(out_ref, val, place_ref, h):
    wide = _move(val, place_ref[h % 2])

    @pl.when(h % 2 == 0)
    def _():
        out_ref[...] = wide

    @pl.when(h % 2 == 1)
    def _():
        out_ref[...] += wide


def _ssd_common(x2, dt_ref, cs_ref, csr_ref, b_ref, c_ref, nc):
    x = x2.reshape(nc, CHUNK, SSD_P)
    dt = dt_ref[0].reshape(nc, CHUNK, SSD_P)
    cs = cs_ref[0].reshape(nc, CHUNK, SSD_P)
    csr = csr_ref[0]
    bm = b_ref[...].reshape(nc, CHUNK, SSD_N).astype(BF16)
    cm = c_ref[...].reshape(nc, CHUNK, SSD_N).astype(BF16)
    li = lax.broadcasted_iota(jnp.int32, (nc, CHUNK, CHUNK), 1)
    si = lax.broadcasted_iota(jnp.int32, (nc, CHUNK, CHUNK), 2)
    lmat = jnp.exp(jnp.where(si <= li, cs - csr, NEG))
    g = _bdot(cm, bm, 2, 2)
    cs_last = jnp.sum(jnp.where(li == CHUNK - 1, cs, 0.0), axis=1, keepdims=True)
    xdt = x * dt
    dec = jnp.exp(cs_last - cs)
    return x, dt, cs, bm, cm, li, si, lmat, g, cs_last, xdt, dec


def _ssd_fwd(xbc, dt_h, cs_h, cs_row, dskip_h, *, name):
    t = xbc.shape[0]
    nc = t // CHUNK
    hpg = SSD_HEADS // SSD_GROUPS
    pick, place = _head_matrices()

    def body(xs_ref, dt_ref, cs_ref, csr_ref, b_ref, c_ref, dk_ref, pick_ref, place_ref, y_ref, st_ref, sc_ref, cd_ref):
        h = pl.program_id(0)
        x, dt, cs, bm, cm, li, si, lmat, g, cs_last, xdt, dec = _ssd_common(_pick_head(xs_ref, pick_ref, h), dt_ref,
                                                                           cs_ref, csr_ref, b_ref, c_ref, nc)
        yd = _bdot((g * lmat).astype(BF16), xdt.astype(BF16), 2, 1)
        sc_ref[...] = _bdot(bm, (dec * xdt).astype(BF16), 1, 1)
        cd_ref[...] = jnp.exp(cs_last)

        def step(c, s):
            st_ref[0, c] = s
            return s * cd_ref[c] + sc_ref[c]

        lax.fori_loop(0, nc, step, jnp.zeros((SSD_N, SSD_P), F32))
        yo = _bdot(cm, st_ref[0].astype(BF16), 2, 1) * jnp.exp(cs)
        _place_head(y_ref, (yd + yo + dk_ref[0] * x).reshape(t, SSD_P), place_ref, h)

    head = pl.BlockSpec((1, t, SSD_P), lambda h: (h, 0, 0))
    pair = pl.BlockSpec((t, 2 * SSD_P), lambda h: (0, h // 2))
    nxb = D_SSM // SSD_N
    return pl.pallas_call(
        body, name=name, grid=(SSD_HEADS,),
        in_specs=[pair, head, head, pl.BlockSpec((1, nc, 1, CHUNK), lambda h: (h, 0, 0, 0)),
                  pl.BlockSpec((t, SSD_N), lambda h: (0, nxb + h // hpg)),
                  pl.BlockSpec((t, SSD_N), lambda h: (0, nxb + SSD_GROUPS + h // hpg)),
                  pl.BlockSpec((1, 1, SSD_P), lambda h: (h, 0, 0)),
                  pl.BlockSpec((2, 2 * SSD_P, SSD_P), lambda h: (0, 0, 0)),
                  pl.BlockSpec((2, SSD_P, 2 * SSD_P), lambda h: (0, 0, 0))],
        out_specs=[pair, pl.BlockSpec((1, nc, SSD_N, SSD_P), lambda h: (h, 0, 0, 0))],
        out_shape=[jax.ShapeDtypeStruct((t, D_SSM), F32),
                   jax.ShapeDtypeStruct((SSD_HEADS, nc, SSD_N, SSD_P), F32)],
        scratch_shapes=[pltpu.VMEM((nc, SSD_N, SSD_P), F32), pltpu.VMEM((nc, 1, SSD_P), F32)],
        compiler_params=_cp("arbitrary"),
    )(xbc, dt_h, cs_h, cs_row, xbc, xbc, dskip_h, pick, place)


def _ssd_bwd(xbc, dt_h, cs_h, cs_row, dskip_h, a_h, states, dy, *, name):
    t = xbc.shape[0]
    nc = t // CHUNK
    hpg = SSD_HEADS // SSD_GROUPS
    pick, place = _head_matrices()

    def body(xs_ref, dt_ref, cs_ref, csr_ref, b_ref, c_ref, dk_ref, a_ref, st_ref, dy_ref, pick_ref, place_ref,
             dxs_ref, ddt_ref, dadt_ref, db_ref, dc_ref, dd_ref, dsl_ref, dsc_ref, cd_ref):
        h = pl.program_id(0) * hpg + pl.program_id(1)
        x, dt, cs, bm, cm, li, si, lmat, g, cs_last, xdt, dec = _ssd_common(_pick_head(xs_ref, pick_ref, h), dt_ref,
                                                                           cs_ref, csr_ref, b_ref, c_ref, nc)
        dy = _pick_head(dy_ref, pick_ref, h).reshape(nc, CHUNK, SSD_P)
        dyb = dy.astype(BF16)
        xdtb = xdt.astype(BF16)
        sprev = st_ref[0]
        sprevb = sprev.astype(BF16)
        cdec = jnp.exp(cs_last)
        ecs = jnp.exp(cs)
        dw = (ecs * dy).astype(BF16)
        wmat = _bdot(cm, sprevb, 2, 1)
        dcs = jnp.sum(dy * ecs * wmat, axis=2, keepdims=True)
        dcm = _bdot(dw, sprevb, 2, 2)
        dsl_ref[...] = _bdot(cm, dw, 1, 1)
        cd_ref[...] = cdec

        def step(k, ds):
            c = nc - 1 - k
            dsc_ref[c] = ds
            return ds * cd_ref[c] + dsl_ref[c]

        lax.fori_loop(0, nc, step, jnp.zeros((SSD_N, SSD_P), F32))
        dsc = dsc_ref[...]
        dscb = dsc.astype(BF16)
        d_last = jnp.sum(jnp.sum(dsc * sprev, axis=1, keepdims=True) * cdec, axis=2, keepdims=True)
        z = dec * xdt
        dbm = _bdot(z.astype(BF16), dscb, 2, 2)
        dz = _bdot(bm, dscb, 2, 1)
        dxdt = dec * dz
        t2 = jnp.sum(dz * z, axis=2, keepdims=True)
        dcs = dcs - t2
        d_last = d_last + jnp.sum(t2, axis=1, keepdims=True)
        m = g * lmat
        mb = m.astype(BF16)
        dm = _bdot(dyb, xdtb, 2, 2)
        dxdt = dxdt + _bdot(mb, dyb, 1, 1)
        dseg = dm * m
        dcs = dcs + jnp.sum(dseg, axis=2, keepdims=True)
        ones = jnp.ones((nc, CHUNK, SSD_P), F32)
        dcs = dcs - _bdot_sum(dseg, ones, 1, 1, 0)
        dg = (dm * lmat).astype(BF16)
        dcm = dcm + _bdot(dg, bm, 2, 1)
        dbm = dbm + _bdot(dg, cm, 1, 1)
        dcs = dcs + jnp.where(li[:, :, :SSD_P] == CHUNK - 1, d_last, 0.0)
        triu = jnp.where(li <= si, 1.0, 0.0).astype(F32)
        dadt = _bdot_sum(triu, dcs, 2, 1, 1)
        dk = dk_ref[0]
        _place_head(dxs_ref, (dxdt * dt + dk * dy).reshape(t, SSD_P), place_ref, h)
        ddt = jnp.sum(dxdt * x, axis=2, keepdims=True) + dadt * a_ref[0]
        mine = lax.broadcasted_iota(jnp.int32, (t, LANES), 1) == h

        @pl.when(h == 0)
        def _():
            ddt_ref[...] = jnp.zeros_like(ddt_ref)
            dadt_ref[...] = jnp.zeros_like(dadt_ref)

        ddt_ref[...] += jnp.where(mine, jnp.max(ddt, axis=2, keepdims=True).reshape(t, 1), 0.0)
        dadt_ref[...] += jnp.where(mine, jnp.max(dadt, axis=2, keepdims=True).reshape(t, 1), 0.0)
        dd_ref[0] = jnp.sum(jnp.sum(dy * x, axis=1, keepdims=True), axis=0)

        @pl.when(pl.program_id(1) == 0)
        def _():
            db_ref[...] = jnp.zeros_like(db_ref)
            dc_ref[...] = jnp.zeros_like(dc_ref)

        db_ref[...] += dbm.reshape(t, SSD_N)
        dc_ref[...] += dcm.reshape(t, SSD_N)

    head = pl.BlockSpec((1, t, SSD_P), lambda gi, hi: (gi * hpg + hi, 0, 0))
    pair = pl.BlockSpec((t, 2 * SSD_P), lambda gi, hi: (0, (gi * hpg + hi) // 2))
    grp = pl.BlockSpec((t, SSD_N), lambda gi, hi: (0, gi))
    lane = pl.BlockSpec((1, 1, SSD_P), lambda gi, hi: (gi * hpg + hi, 0, 0))
    rows = pl.BlockSpec((t, LANES), lambda gi, hi: (0, 0))
    nxb = D_SSM // SSD_N
    dxs, ddt, dadt, db, dc, dd = pl.pallas_call(
        body, name=name, grid=(SSD_GROUPS, hpg),
        in_specs=[pair, head, head, pl.BlockSpec((1, nc, 1, CHUNK), lambda gi, hi: (gi * hpg + hi, 0, 0, 0)),
                  pl.BlockSpec((t, SSD_N), lambda gi, hi: (0, nxb + gi)),
                  pl.BlockSpec((t, SSD_N), lambda gi, hi: (0, nxb + SSD_GROUPS + gi)), lane, lane,
                  pl.BlockSpec((1, nc, SSD_N, SSD_P), lambda gi, hi: (gi * hpg + hi, 0, 0, 0)), pair,
                  pl.BlockSpec((2, 2 * SSD_P, SSD_P), lambda gi, hi: (0, 0, 0)),
                  pl.BlockSpec((2, SSD_P, 2 * SSD_P), lambda gi, hi: (0, 0, 0))],
        out_specs=[pair, rows, rows, grp, grp, lane],
        out_shape=[jax.ShapeDtypeStruct((t, D_SSM), F32)] + [jax.ShapeDtypeStruct((t, LANES), F32)] * 2
        + [jax.ShapeDtypeStruct((t, SSD_GROUPS * SSD_N), F32)] * 2
        + [jax.ShapeDtypeStruct((SSD_HEADS, 1, SSD_P), F32)],
        scratch_shapes=[pltpu.VMEM((nc, SSD_N, SSD_P), F32), pltpu.VMEM((nc, SSD_N, SSD_P), F32),
                        pltpu.VMEM((nc, 1, SSD_P), F32)],
        compiler_params=_cp("arbitrary", "arbitrary"),
    )(xbc, dt_h, cs_h, cs_row, xbc, xbc, dskip_h, a_h, states, dy, pick, place)
    return jnp.concatenate([dxs, db, dc], axis=1), ddt, dadt, dd


def _ssd_gate_fwd(y, proj, w, *, tr=256, name):
    t = y.shape[0]
    gw = D_SSM // SSD_GROUPS

    def body(y_ref, z_ref, w_ref, o_ref):
        v = y_ref[...] * _silu(z_ref[...])
        for gi in range(SSD_GROUPS):
            vg = v[:, gi * gw:(gi + 1) * gw]
            r = lax.rsqrt(jnp.mean(vg * vg, axis=-1, keepdims=True) + NORM_EPS)
            o_ref[:, gi * gw:(gi + 1) * gw] = (vg * r * w_ref[:, gi * gw:(gi + 1) * gw]).astype(BF16)

    blk = pl.BlockSpec((tr, D_SSM), lambda i: (i, 0))
    return pl.pallas_call(
        body, name=name, grid=(t // tr,), in_specs=[blk, blk, pl.BlockSpec((1, D_SSM), lambda i: (0, 0))],
        out_specs=blk, out_shape=jax.ShapeDtypeStruct((t, D_SSM), BF16), compiler_params=_cp("parallel"),
    )(y, proj, w)


def _ssd_gate_bwd(y, proj, w, dcat, *, tr=256, name):
    t = y.shape[0]
    gw = D_SSM // SSD_GROUPS

    def body(y_ref, z_ref, w_ref, d_ref, dy_ref, dz_ref, dw_ref):
        yv, zv, dv = y_ref[...], z_ref[...], d_ref[...].astype(F32)
        sz = _silu(zv)
        v = yv * sz

        @pl.when(pl.program_id(0) == 0)
        def _():
            dw_ref[...] = jnp.zeros_like(dw_ref)

        for gi in range(SSD_GROUPS):
            sl = slice(gi * gw, (gi + 1) * gw)
            vg, dg = v[:, sl], dv[:, sl]
            r = lax.rsqrt(jnp.mean(vg * vg, axis=-1, keepdims=True) + NORM_EPS)
            vh = vg * r
            gg = dg * w_ref[:, sl]
            dvg = r * (gg - vh * jnp.mean(gg * vh, axis=-1, keepdims=True))
            dy_ref[:, sl] = dvg * sz[:, sl]
            dz_ref[:, sl] = (dvg * yv[:, sl] * _dsilu(zv[:, sl])).astype(BF16)
            dw_ref[:, sl] += jnp.sum(dg * vh, axis=0, keepdims=True)

    blk = pl.BlockSpec((tr, D_SSM), lambda i: (i, 0))
    row = pl.BlockSpec((1, D_SSM), lambda i: (0, 0))
    return pl.pallas_call(
        body, name=name, grid=(t // tr,), in_specs=[blk, blk, row, blk], out_specs=[blk, blk, row],
        out_shape=[jax.ShapeDtypeStruct((t, D_SSM), F32), jax.ShapeDtypeStruct((t, D_SSM), BF16),
                   jax.ShapeDtypeStruct((1, D_SSM), F32)],
        compiler_params=_cp("arbitrary"),
    )(y, proj, w, dcat)


def _pad_lanes(v):
    return jnp.pad(v, ((0, 0), (0, LANES - v.shape[1])))


def _per_head(v128, t):
    return jnp.broadcast_to(v128[:, :SSD_HEADS].T[:, :, None], (SSD_HEADS, t, SSD_P))


def _ssd_forward(proj, conv_w, conv_b, dt_bias, a_log, d_skip, ssd_norm_w):
    t = proj.shape[0]
    nc = t // CHUNK
    xbc = _conv_act_fwd(proj, conv_w, conv_b, kw=SSD_CONV, glu=False, tc=512, coff=OFF_XBC // 512,
                        ncols=SSD_CONV_DIM, out_dtype=F32, name="ssd_conv_fwd")
    bias128, alog128 = _pad_lanes(dt_bias), _pad_lanes(a_log)
    dt128, cs128, a128 = _ssd_prep(proj, bias128, alog128, name="ssd_prep")
    dt_h, cs_h = _per_head(dt128, t), _per_head(cs128, t)
    cs_row = cs128[:, :SSD_HEADS].T.reshape(SSD_HEADS, nc, 1, CHUNK)
    dskip_h = jnp.broadcast_to(d_skip[0][:, None, None], (SSD_HEADS, 1, SSD_P))
    a_h = jnp.broadcast_to(a128[0, :SSD_HEADS][:, None, None], (SSD_HEADS, 1, SSD_P))
    y, states = _ssd_fwd(xbc, dt_h, cs_h, cs_row, dskip_h, name="ssd_scan_fwd")
    y_ssd = _ssd_gate_fwd(y, proj, ssd_norm_w, name="ssd_gate_fwd")
    saved = (proj, conv_w, conv_b, ssd_norm_w, bias128, dt128, a128, dt_h, cs_h, cs_row, xbc, dskip_h, a_h, states, y)
    return y_ssd, saved


def _ssd_backward(saved, dcat):
    proj, conv_w, conv_b, ssd_norm_w, bias128, dt128, a128, dt_h, cs_h, cs_row, xbc, dskip_h, a_h, states, y = saved
    dy, dz, d_norm_w = _ssd_gate_bwd(y, proj, ssd_norm_w, dcat, name="ssd_gate_bwd")
    dxc, ddt128, dadt128, dd_h = _ssd_bwd(xbc, dt_h, cs_h, cs_row, dskip_h, a_h, states, dy, name="ssd_scan_bwd")
    dxbc, d_conv_w, d_conv_b = _conv_act_bwd(proj, conv_w, conv_b, dxc, kw=SSD_CONV, glu=False, tc=512,
                                             coff=OFF_XBC // 512, ncols=SSD_CONV_DIM, name="ssd_conv_bwd")
    d_raw, d_bias, d_alog, d_dskip = _ssd_prep_bwd(ddt128, dadt128, proj, bias128, dt128, a128,
                                                   dd_h.reshape(SSD_HEADS, SSD_P), name="ssd_prep_bwd")
    return (dz, dxbc, d_raw, d_norm_w, d_conv_w, d_conv_b, d_bias[:, :SSD_HEADS], d_alog[:, :SSD_HEADS],
            d_dskip.reshape(1, SSD_HEADS))


def _rope_tables(positions):
    inv_freq = ROPE_THETA ** (-jnp.arange(0, MLA_ROPE, 2, dtype=F32) / MLA_ROPE)
    ang = positions[0].astype(F32)[:, None] * inv_freq
    cos, sin = jnp.cos(ang), jnp.sin(ang)
    z = jnp.zeros_like(cos)
    return jnp.stack([jnp.concatenate([cos, cos, z, z], axis=1), jnp.concatenate([-sin, z, z, z], axis=1),
                      jnp.concatenate([z, sin, z, z], axis=1)])


def _latent_norms(proj, q_w, kv_w, *, tr=256, name):
    t = proj.shape[0]

    def body(q_ref, kv_ref, qw_ref, kvw_ref, qo_ref, kvo_ref):
        for x_ref, w_ref, o_ref in ((q_ref, qw_ref, qo_ref), (kv_ref, kvw_ref, kvo_ref)):
            xv = x_ref[...]
            r = lax.rsqrt(jnp.mean(xv * xv, axis=-1, keepdims=True) + NORM_EPS)
            o_ref[...] = (xv * r * w_ref[...]).astype(BF16)

    return pl.pallas_call(
        body, name=name, grid=(t // tr,),
        in_specs=[pl.BlockSpec((tr, MLA_Q_RANK), lambda i: (i, OFF_QA // MLA_Q_RANK)),
                  pl.BlockSpec((tr, MLA_KV_RANK), lambda i: (i, OFF_CKV // MLA_KV_RANK)),
                  pl.BlockSpec((1, MLA_Q_RANK), lambda i: (0, 0)), pl.BlockSpec((1, MLA_KV_RANK), lambda i: (0, 0))],
        out_specs=[pl.BlockSpec((tr, MLA_Q_RANK), lambda i: (i, 0)), pl.BlockSpec((tr, MLA_KV_RANK), lambda i: (i, 0))],
        out_shape=[jax.ShapeDtypeStruct((t, MLA_Q_RANK), BF16), jax.ShapeDtypeStruct((t, MLA_KV_RANK), BF16)],
        compiler_params=_cp("parallel"),
    )(proj, proj, q_w, kv_w)


def _mla_forward(proj, tabs, q_a_norm_w, wq_pad, kv_a_norm_w, wkv):
    qn, kvn = _latent_norms(proj, q_a_norm_w, kv_a_norm_w, name="latent_norms")
    q = _matmul(qn, wq_pad, name="q_b_proj")
    kv = _matmul(kvn, wkv, name="kv_b_proj")
    q3, k3, v3, vt4 = _mla_prep(q, kv, proj, tabs, name="mla_prep")
    o, lse = _attn_fwd(q3, k3, vt4, name="attn_fwd")
    return o, (proj, tabs, q_a_norm_w, wq_pad, kv_a_norm_w, wkv, qn, kvn, q3, k3, v3, o, lse)


def _mla_backward(saved, dcat):
    proj, tabs, q_a_norm_w, wq_pad, kv_a_norm_w, wkv, qn, kvn, q3, k3, v3, o, lse = saved
    dq3, dk3, dv3 = _attn_bwd(q3, k3, v3, o, dcat, lse, name="attn_bwd")
    dq, dkv, dkr = _mla_unprep(dq3, dk3, dv3, tabs, name="mla_unprep")
    d_wq = _matmul(qn, dq, ta=True, out_dtype=BF16, name="d_w_q_b")
    dqn = _matmul(dq, wq_pad, tb=True, name="d_qn")
    dq_a, d_qnw = _rmsnorm_bwd(proj, q_a_norm_w, dqn, width=MLA_Q_RANK, cblk=OFF_QA // MLA_Q_RANK, out_dtype=BF16,
                               name="q_a_norm_bwd")
    d_wkv = _matmul(kvn, dkv, ta=True, out_dtype=BF16, name="d_w_kv_b")
    dkvn = _matmul(dkv, wkv, tb=True, name="d_kvn")
    dckv, d_kvnw = _rmsnorm_bwd(proj, kv_a_norm_w, dkvn, width=MLA_KV_RANK, cblk=OFF_CKV // MLA_KV_RANK,
                                out_dtype=BF16, name="kv_a_norm_bwd")
    return dq_a, dckv, dkr, d_wq, d_wkv, d_qnw, d_kvnw


def _pad_w_q(w):
    r = w.shape[0]
    w3 = w.reshape(r, MLA_HEADS, MLA_NOPE + MLA_ROPE)
    return jnp.pad(w3, ((0, 0), (0, 0), (0, MLA_QK_PAD - MLA_NOPE - MLA_ROPE))).reshape(r, MLA_HEADS * MLA_QK_PAD)


def _unpad_w_q(w):
    r = w.shape[0]
    return w.reshape(r, MLA_HEADS, MLA_QK_PAD)[:, :, :MLA_NOPE + MLA_ROPE].reshape(r, MLA_HEADS * (MLA_NOPE + MLA_ROPE))


W_IN_SEGMENTS = ((0, D_SSM + SSD_CONV_DIM, 0), (D_SSM + SSD_CONV_DIM, D_SSM + SSD_CONV_DIM + SSD_HEADS, OFF_DT),
                 (D_SSM + SSD_CONV_DIM + SSD_HEADS, D_IN - MLA_ROPE, OFF_QA), (D_IN - MLA_ROPE, D_IN, OFF_KR))


def _pad_w_in_shards(g):
    n = g.shape[2]
    pieces, at = [], 0
    for lo, hi, start in sorted(W_IN_SEGMENTS, key=lambda seg: seg[2]):
        if start > at:
            pieces.append(jnp.zeros((g.shape[1], start - at), g.dtype))
        for j in range(N_DEV):
            a, b = max(lo, j * n), min(hi, (j + 1) * n)
            if a < b:
                pieces.append(g[j][:, a - j * n:b - j * n])
        at = start + hi - lo
    pieces.append(jnp.zeros((g.shape[1], D_IN_PAD - at), g.dtype))
    return jnp.concatenate(pieces, axis=1)


def _unpad_w_in_shards(w):
    n = D_IN // N_DEV
    shards = []
    for j in range(N_DEV):
        pieces = []
        for lo, hi, start in W_IN_SEGMENTS:
            a, b = max(lo, j * n), min(hi, (j + 1) * n)
            if a < b:
                pieces.append(w[:, start + a - lo:start + b - lo])
        shards.append(jnp.concatenate(pieces, axis=1) if len(pieces) > 1 else pieces[0])
    return jnp.stack(shards)


WEIGHTS = ['mix_norm_w', 'w_in', 'conv_w', 'conv_b', 'dt_bias', 'a_log', 'd_skip', 'ssd_norm_w', 'q_a_norm_w', 'w_q_b',
           'kv_a_norm_w', 'w_kv_b', 'w_out', 'ffn_norm_w', 'w_ffn_up', 'ffn_conv_w', 'ffn_conv_b', 'w_ffn_down',
           'ple_norm_w', 'w_ple_gate', 'b_ple_gate', 'w_ple_proj', 'ple_post_norm_w', 'final_norm_w']
BIG = ['w_in', 'w_q_b', 'w_kv_b', 'w_out', 'w_ffn_up', 'w_ffn_down', 'w_ple_gate', 'w_ple_proj']
COL_SHARDED = ('w_in', 'w_q_b', 'w_kv_b', 'w_ffn_up', 'w_ple_proj')
CONV = ['conv_w', 'ffn_conv_w']
REPL = [n for n in WEIGHTS if n not in BIG and n not in CONV]
FFN_INV = tuple(int(i) for i in np.argsort(FFN_PERM))


def _cat_cols(g):
    return jnp.concatenate([g[j] for j in range(N_DEV)], axis=1)


def _split_cols(w):
    n = w.shape[1] // N_DEV
    return jnp.stack([w[:, j * n:(j + 1) * n] for j in range(N_DEV)])


def _interleave(v):
    r = v.shape[0]
    return v.reshape(r, N_DEV, FFN_TC)[:, jnp.array(FFN_PERM)].reshape(r, N_DEV * FFN_TC)


def _deinterleave(v):
    r = v.shape[0]
    return v.reshape(r, N_DEV, FFN_TC)[:, jnp.array(FFN_INV)].reshape(r, N_DEV * FFN_TC)


def _assemble_weights(g):
    layout = {
        'w_in': _pad_w_in_shards,
        'w_q_b': lambda v: _pad_w_q(_cat_cols(v)),
        'w_kv_b': _cat_cols,
        'w_out': lambda v: v.reshape(D_MODEL, D_MODEL),
        'w_ffn_up': lambda v: v,
        'w_ffn_down': lambda v: v.reshape(D_FF, D_MODEL),
        'w_ple_gate': lambda v: v.reshape(D_MODEL, D_MODEL),
        'w_ple_proj': _cat_cols,
        'conv_w': _cat_cols,
        'ffn_conv_w': lambda v: _interleave(_cat_cols(v)),
    }
    return {n: layout[n](v) for n, v in g.items()}


WEIGHT_GROUPS = {'a': ['w_in', 'w_q_b', 'w_kv_b', 'conv_w'], 'b': ['w_out', 'w_ffn_up', 'ffn_conv_w'],
                 'c': ['w_ffn_down', 'w_ple_gate', 'w_ple_proj']}
GRAD_GROUPS = {'p': ['w_ple_proj', 'w_ple_gate', 'w_ffn_down'], 'r': ['w_ffn_up'], 's': ['w_out'],
               't': ['w_q_b', 'w_kv_b', 'w_in']}


def _ffn_perm(j):
    return (j % 2) * (N_DEV // 2) + j // 2


def _local_step(x, p, tabs, get_w, s, target, emit, relay, settle):
    t = x.shape[0]
    s = dict(s)
    half = D_MODEL // 2
    up_cols = 2 * D_FF
    ffn_conv_b = _interleave(s['ffn_conv_b'])
    w = dict(get_w('a', None))
    h = _rmsnorm_fwd(x, s['mix_norm_w'], width=D_MODEL, name="mix_norm")
    proj = _matmul(h, w['w_in'], name="in_proj")
    y_ssd, ssd_saved = _ssd_forward(proj, w['conv_w'], s['conv_b'], s['dt_bias'], s['a_log'], s['d_skip'],
                                    s['ssd_norm_w'])
    o, mla_saved = _mla_forward(proj, tabs, s['q_a_norm_w'], w['w_q_b'], s['kv_a_norm_w'], w['w_kv_b'])
    tk_o, tn_o = _tile(half, MM_TK), _tile(D_MODEL, MM_TILE)
    w.update(get_w('b', o))
    x1 = _matmul(y_ssd, w['w_out'], add=x, mnk=(t, D_MODEL, half), name="out_proj_ssd")
    x1 = _matmul(o, w['w_out'], add=x1, mnk=(t, D_MODEL, half), name="out_proj_mla",
                 b_spec=pl.BlockSpec((tk_o, tn_o), lambda i, j, kk: (kk + half // tk_o, j)))
    hf = _rmsnorm_fwd(x1, s['ffn_norm_w'], width=D_MODEL, name="ffn_norm")
    tk_u = _tile(D_MODEL, MM_TK)
    u = _matmul(hf, w['w_ffn_up'], mnk=(t, up_cols, D_MODEL), tn=FFN_TC, name="ffn_up",
                b_spec=pl.BlockSpec((1, tk_u, FFN_TC), lambda i, j, kk: (_ffn_perm(j), kk, 0)))
    act = _conv_act_fwd(u, w['ffn_conv_w'], ffn_conv_b, kw=FFN_CONV, glu=True, tc=2 * FFN_TC, coff=0, ncols=up_cols,
                        out_dtype=BF16, name="ffn_act")
    w.update(get_w('c', act))
    x2 = _matmul(act, w['w_ffn_down'], add=x1, name="ffn_down")
    hp = _rmsnorm_fwd(x2, s['ple_norm_w'], width=D_MODEL, name="ple_norm")
    gl = _matmul(hp, w['w_ple_gate'], bias=s['b_ple_gate'], name="ple_gate")
    pe = _matmul(p, w['w_ple_proj'], name="ple_proj")
    loss, dx3, d_final, dgl, d_bgate, dpe, d_post = _ple_loss(x2, gl, pe, s['ple_post_norm_w'], s['final_norm_w'],
                                                              target, name="ple_loss")
    d_wproj = _matmul(p, dpe, ta=True, out_dtype=BF16, name="d_w_ple_proj")
    d_wgate = _matmul(hp, dgl, ta=True, out_dtype=BF16, name="d_w_ple_gate")
    dhp = _matmul(dgl, w['w_ple_gate'], tb=True, name="d_ple_normed")
    dx2, d_plenorm, dx2b = _rmsnorm_bwd(x2, s['ple_norm_w'], dhp, dx3, width=D_MODEL, also_bf16=True,
                                        name="ple_norm_bwd")
    dact = _matmul(dx2b, w['w_ffn_down'], tb=True, name="d_ffn_act")
    d_wdown = _matmul(act, dx2b, ta=True, out_dtype=BF16, name="d_w_ffn_down")
    zz = emit('p', {'w_ple_proj': _split_cols(d_wproj), 'w_ple_gate': d_wgate.reshape(N_DEV, D_MODEL // N_DEV, D_MODEL),
                    'w_ffn_down': d_wdown.reshape(N_DEV, D_FF // N_DEV, D_MODEL)})
    du, d_fconv_w, d_fconv_b = _conv_act_bwd(u, w['ffn_conv_w'], ffn_conv_b + zz, dact, kw=FFN_CONV, glu=True,
                                             tc=2 * FFN_TC, coff=0, ncols=up_cols, name="ffn_act_bwd")
    zz = zz + relay('p', du)
    tm_u = _tile(D_MODEL, MM_TILE)
    d_wup = _matmul(hf, du, ta=True, out_dtype=BF16, mnk=(D_MODEL, up_cols, t), tn=FFN_TC, name="d_w_ffn_up",
                    o_spec=pl.BlockSpec((1, tm_u, FFN_TC), lambda i, j, kk: (_ffn_perm(j), i, 0)),
                    o_shape=(N_DEV, D_MODEL, FFN_TC))
    zz = zz + emit('r', {'w_ffn_up': d_wup})
    zero_row = jnp.zeros((1, D_MODEL), F32)
    dhf = _matmul(du, w['w_ffn_up'], tb=True, mnk=(t, D_MODEL, up_cols), tm=t, tk=FFN_TC, name="d_ffn_normed",
                  bias=zero_row + zz,
                  b_spec=pl.BlockSpec((1, tn_o, FFN_TC), lambda i, j, kk: (_ffn_perm(kk), j, 0)))
    zz = zz + relay('r', dhf) + settle('p')
    dx1, d_ffnnorm, dx1b = _rmsnorm_bwd(x1, s['ffn_norm_w'] + zz, dhf, dx2, width=D_MODEL, also_bf16=True,
                                        name="ffn_norm_bwd")
    dcat = _matmul(dx1b, w['w_out'], tb=True, name="d_mixed")
    d_wout = jnp.concatenate([_matmul(y_ssd, dx1b, ta=True, out_dtype=BF16, name="d_w_out_ssd"),
                              _matmul(o, dx1b, ta=True, out_dtype=BF16, name="d_w_out_mla")], axis=0)
    zz = zz + emit('s', {'w_out': d_wout.reshape(N_DEV, D_MODEL // N_DEV, D_MODEL)})
    ssd_saved = ssd_saved[:3] + (ssd_saved[3] + zz,) + ssd_saved[4:]
    dz, dxbc, d_raw, d_ssdnorm, d_conv_w, d_conv_b, d_dtb, d_alog, d_dskip = _ssd_backward(ssd_saved, dcat)
    zz = zz + relay('s', dz)
    mla_saved = mla_saved[:-1] + (mla_saved[-1] + zz,)
    dq_a, dckv, dkr, d_wq, d_wkv, d_qnorm, d_kvnorm = _mla_backward(mla_saved, dcat)
    d_raw = (d_raw + settle('r')).astype(BF16)
    dproj = jnp.concatenate([dz, dxbc, dq_a, dckv, dkr, d_raw], axis=1)
    d_win = _matmul(h, dproj, ta=True, out_dtype=BF16, name="d_w_in")
    zz = emit('t', {'w_in': _unpad_w_in_shards(d_win), 'w_q_b': _split_cols(_unpad_w_q(d_wq)),
                    'w_kv_b': _split_cols(d_wkv)}) + settle('s')
    dh = _matmul(dproj, w['w_in'], tb=True, bias=zero_row + zz, name="d_in_normed")
    zz = relay('t', dh)
    dx, d_mixnorm = _rmsnorm_bwd(x, s['mix_norm_w'] + zz, dh, dx1, width=D_MODEL, name="mix_norm_bwd")
    conv = {'conv_w': d_conv_w, 'ffn_conv_w': _deinterleave(d_fconv_w)}
    vec = {
        'mix_norm_w': d_mixnorm, 'conv_b': d_conv_b, 'dt_bias': d_dtb, 'a_log': d_alog, 'd_skip': d_dskip,
        'ssd_norm_w': d_ssdnorm, 'q_a_norm_w': d_qnorm, 'kv_a_norm_w': d_kvnorm, 'ffn_norm_w': d_ffnnorm,
        'ffn_conv_b': _deinterleave(d_fconv_b), 'ple_norm_w': d_plenorm, 'b_ple_gate': d_bgate,
        'ple_post_norm_w': d_post, 'final_norm_w': d_final,
    }
    return loss, dx, conv, vec


MESH = pl.DeviceIdType.MESH
FLIPS = ((0, 0, 1), (1, 0, 0), (0, 1, 0), (1, 1, 0), (1, 0, 1), (0, 1, 1), (1, 1, 1))


def _exchange(items, *, gather, name):
    n = len(items)

    def body(*refs):
        ins, outs = refs[:n], refs[n:2 * n]
        send_sems, recv_sems, local_sems = refs[2 * n:]
        x, y, c = lax.axis_index("x"), lax.axis_index("y"), lax.axis_index("c")
        me = 4 * x + 2 * y + c
        peers = [(jnp.where(fx, 1 - x, x), jnp.where(fy, 1 - y, y), jnp.where(fc, 1 - c, c)) for fx, fy, fc in FLIPS]
        slot = [4 * px + 2 * py + pc for px, py, pc in peers]
        local, sends = [], []
        for wi in range(n):
            cp = pltpu.make_async_copy(ins[wi] if gather else ins[wi].at[me], outs[wi].at[me], local_sems.at[wi])
            cp.start()
            local.append(cp)
            for k, peer in enumerate(peers):
                cp = pltpu.make_async_remote_copy(
                    src_ref=ins[wi] if gather else ins[wi].at[slot[k]], dst_ref=outs[wi].at[me],
                    send_sem=send_sems.at[k, wi], recv_sem=recv_sems.at[k, wi], device_id=peer, device_id_type=MESH)
                cp.start()
                sends.append(cp)
        for wi in range(n):
            for k, peer in enumerate(peers):
                pltpu.make_async_remote_copy(
                    src_ref=outs[wi].at[slot[k]], dst_ref=outs[wi].at[slot[k]], send_sem=send_sems.at[k, wi],
                    recv_sem=recv_sems.at[k, wi], device_id=peer, device_id_type=MESH).wait_recv()
        for cp in sends:
            cp.wait_send()
        for cp in local:
            cp.wait()

    hbm = pl.BlockSpec(memory_space=pltpu.HBM)
    out_shape = [jax.ShapeDtypeStruct(((N_DEV,) + v.shape) if gather else v.shape, v.dtype) for v in items]
    return pl.pallas_call(
        body, name=name, in_specs=[hbm] * n, out_specs=[hbm] * n, out_shape=out_shape,
        scratch_shapes=[pltpu.SemaphoreType.DMA((len(FLIPS), n)), pltpu.SemaphoreType.DMA((len(FLIPS), n)),
                        pltpu.SemaphoreType.DMA((n,))],
    )(*items)


HBM_SPEC = pl.BlockSpec(memory_space=pltpu.HBM)
SEM_SPEC = pl.BlockSpec(memory_space=pltpu.SEMAPHORE)
EFFECT = pltpu.SideEffectType.DATAFLOW_SIDE_EFFECTING


def _split_start(bufs, ncopies, plan, *, name):
    nb = len(bufs)

    def body(*refs):
        send_sems, recv_sems, token = refs[nb], refs[nb + 1], refs[2 * nb + 2]
        for i, (src, dst, peer, _) in enumerate(plan(refs[:nb])):
            pltpu.make_async_remote_copy(src_ref=src, dst_ref=dst, send_sem=send_sems.at[i], recv_sem=recv_sems.at[i],
                                         device_id=peer, device_id_type=MESH).start()
        token[...] = jnp.zeros_like(token)

    res = pl.pallas_call(
        body, name=name, in_specs=[HBM_SPEC] * nb,
        out_specs=[SEM_SPEC, SEM_SPEC] + [HBM_SPEC] * nb + [pl.BlockSpec(memory_space=pltpu.VMEM)],
        out_shape=[pltpu.SemaphoreType.DMA((ncopies,)), pltpu.SemaphoreType.DMA((ncopies,))]
        + [pltpu.HBM(v.shape, v.dtype) for v in bufs] + [jax.ShapeDtypeStruct((HALO, LANES), F32)],
        input_output_aliases={i: 2 + i for i in range(nb)},
        compiler_params=pltpu.CompilerParams(has_side_effects=EFFECT),
    )(*[pltpu.with_memory_space_constraint(v, pltpu.HBM) for v in bufs])
    return (res[0], res[1], list(res[2:2 + nb])), res[2 + nb]


def _split_wait(started, after, plan, local_plan, *, name):
    send_sems, recv_sems, bufs = started
    nb = len(bufs)
    nlocal = len(local_plan(bufs))

    def body(*refs):
        send_sems, recv_sems = refs[nb], refs[nb + 1]
        local_sems = refs[2 * nb + 3]
        local = []
        for j, (src, dst) in enumerate(local_plan(refs[:nb])):
            cp = pltpu.make_async_copy(src, dst, local_sems.at[j])
            cp.start()
            local.append(cp)
        for i, (src, _, peer, incoming) in enumerate(plan(refs[:nb])):
            cp = pltpu.make_async_remote_copy(src_ref=src, dst_ref=incoming, send_sem=send_sems.at[i],
                                              recv_sem=recv_sems.at[i], device_id=peer, device_id_type=MESH)
            cp.wait_send()
            cp.wait_recv()
        for cp in local:
            cp.wait()

    res = pl.pallas_call(
        body, name=name, in_specs=[HBM_SPEC] * nb + [SEM_SPEC, SEM_SPEC, pl.BlockSpec(memory_space=pl.ANY)],
        out_specs=[HBM_SPEC] * nb, out_shape=[pltpu.HBM(v.shape, v.dtype) for v in bufs],
        input_output_aliases={i: i for i in range(nb)},
        scratch_shapes=[pltpu.SemaphoreType.DMA((max(nlocal, 1),))],
        compiler_params=pltpu.CompilerParams(has_side_effects=EFFECT),
    )(*bufs, send_sems, recv_sems, after)
    return list(res)


def _hold(values, after, *, name):
    n = len(values)

    def body(*refs):
        del refs

    return list(pl.pallas_call(
        body, name=name, in_specs=[HBM_SPEC] * n + [pl.BlockSpec(memory_space=pl.ANY)], out_specs=[HBM_SPEC] * n,
        out_shape=[pltpu.HBM(v.shape, v.dtype) for v in values], input_output_aliases={i: i for i in range(n)},
    )(*values, after))


def _place():
    x, y, c = lax.axis_index("x"), lax.axis_index("y"), lax.axis_index("c")
    others = [((1 - x, y, c), 2 * (1 - x) + y), ((x, 1 - y, c), 2 * x + 1 - y), ((1 - x, 1 - y, c), 2 * (1 - x) + 1 - y)]
    return 4 * x + 2 * y + c, 2 * x + y, c, (x, y, 1 - c), others


def _gather1_plan(n):
    def plan(refs):
        me, _, _, sibling, others = _place()
        out = []
        for wi in range(n):
            item, land = refs[wi], refs[n + wi]
            out.append((item, land.at[me], sibling, land.at[me + 1 - 2 * lax.axis_index("c")]))
            for peer, chip in others:
                out.append((item, land.at[me], peer, land.at[2 * chip + lax.axis_index("c")]))
        return out

    return plan


def _gather1_local(n):
    def plan(refs):
        me = _place()[0]
        return [(refs[wi], refs[n + wi].at[me]) for wi in range(n)]

    return plan


def _gather2_plan(n):
    def plan(refs):
        _, _, c, sibling, others = _place()
        out = []
        for wi in range(n):
            land = refs[wi]
            for _, chip in others:
                out.append((land.at[2 * chip + c], land.at[2 * chip + c], sibling, land.at[2 * chip + 1 - c]))
        return out

    return plan


def _gather_start(items, *, name):
    lands = [lax.empty((N_DEV,) + v.shape, v.dtype) for v in items]
    return _split_start(items + lands, 4 * len(items), _gather1_plan(len(items)), name=name)


def _gather_forward(started, after, *, name):
    n = len(started[2]) // 2
    bufs = _split_wait(started, after, _gather1_plan(n), _gather1_local(n), name=name + "_wait")
    return _split_start(bufs[n:], 3 * n, _gather2_plan(n), name=name + "_start")


def _gather_finish(started, after, *, name):
    n = len(started[2])
    return _split_wait(started, after, _gather2_plan(n), lambda refs: [], name=name)


def _handshake(peers):
    barrier = pltpu.get_barrier_semaphore()
    for peer in peers:
        pl.semaphore_signal(barrier, inc=1, device_id=peer, device_id_type=MESH)
    pl.semaphore_wait(barrier, len(peers))


def _remote(src, dst, send_sem, recv_sem, peer):
    return pltpu.make_async_remote_copy(src_ref=src, dst_ref=dst, send_sem=send_sem, recv_sem=recv_sem, device_id=peer,
                                        device_id_type=MESH)


def _sequencer_gather(items, *, collective_id, name):
    n = len(items)
    srcs = [jax.new_ref(v, memory_space=pltpu.MemorySpace.HBM) for v in items]
    lands = [jax.empty_ref(jax.ShapeDtypeStruct((N_DEV,) + v.shape, v.dtype), memory_space=pltpu.MemorySpace.HBM)
             for v in items]
    dma = pltpu.SemaphoreType.DMA

    @pl.kernel(mesh=plsc.ScalarSubcoreMesh(axis_name="sequencer", num_cores=1), name=name,
               scratch_types=(dma((4 * n,)), dma((4 * n,)), dma((3 * n,)), dma((3 * n,)), dma((n,))),
               compiler_params=pltpu.CompilerParams(collective_id=collective_id))
    def launch(send1, recv1, send2, recv2, local_sems):
        _, _, _, sibling, others = _place()
        _handshake([sibling] + [peer for peer, _ in others])
        hop1 = _gather1_plan(n)(srcs + lands)
        hop2 = _gather2_plan(n)(lands)
        local = [pltpu.make_async_copy(src, dst, local_sems.at[j])
                 for j, (src, dst) in enumerate(_gather1_local(n)(srcs + lands))]
        for cp in local:
            cp.start()
        for i, (src, dst, peer, _) in enumerate(hop1):
            _remote(src, dst, send1.at[i], recv1.at[i], peer).start()
        for wi in range(n):
            for j in range(3):
                i1, i2 = 4 * wi + 1 + j, 3 * wi + j
                src, _, peer, incoming = hop1[i1]
                _remote(src, incoming, send1.at[i1], recv1.at[i1], peer).wait_recv()
                src, dst, peer, _ = hop2[i2]
                _remote(src, dst, send2.at[i2], recv2.at[i2], peer).start()
        for wi in range(n):
            src, _, peer, incoming = hop1[4 * wi]
            _remote(src, incoming, send1.at[4 * wi], recv1.at[4 * wi], peer).wait_recv()
        for i, (src, _, peer, incoming) in enumerate(hop2):
            cp = _remote(src, incoming, send2.at[i], recv2.at[i], peer)
            cp.wait_send()
            cp.wait_recv()
        for i, (src, dst, peer, _) in enumerate(hop1):
            _remote(src, dst, send1.at[i], recv1.at[i], peer).wait_send()
        for cp in local:
            cp.wait()

    launch()
    return [land[...] for land in lands]


def _sequencer_exchange(sources, land_shapes, ncopies, plan, local_plan, peers, *, collective_id, name):
    srcs = [jax.new_ref(v, memory_space=pltpu.MemorySpace.HBM) for v in sources]
    lands = [jax.empty_ref(s, memory_space=pltpu.MemorySpace.HBM) for s in land_shapes]
    nlocal = len(local_plan(srcs + lands))
    dma = pltpu.SemaphoreType.DMA

    @pl.kernel(mesh=plsc.ScalarSubcoreMesh(axis_name="sequencer", num_cores=1), name=name,
               scratch_types=(dma((ncopies,)), dma((ncopies,)), dma((max(nlocal, 1),))),
               compiler_params=pltpu.CompilerParams(collective_id=collective_id))
    def launch(send_sems, recv_sems, local_sems):
        _handshake(peers(_place()))
        copies = plan(srcs + lands)
        local = [pltpu.make_async_copy(src, dst, local_sems.at[j])
                 for j, (src, dst) in enumerate(local_plan(srcs + lands))]
        for cp in local:
            cp.start()
        for i, (src, dst, peer, _) in enumerate(copies):
            _remote(src, dst, send_sems.at[i], recv_sems.at[i], peer).start()
        for i, (src, _, peer, incoming) in enumerate(copies):
            cp = _remote(src, incoming, send_sems.at[i], recv_sems.at[i], peer)
            cp.wait_send()
            cp.wait_recv()
        for cp in local:
            cp.wait()

    launch()
    return [land[...] for land in lands]


def _sequencer_scatter_hop2(sums, *, collective_id, name):
    n = len(sums)
    shapes = [jax.ShapeDtypeStruct(v.shape, v.dtype) for v in sums]
    return _sequencer_exchange(sums, shapes, 3 * n, _scatter2_plan(n), _scatter2_local(n),
                               lambda place: [peer for peer, _ in place[4]], collective_id=collective_id, name=name)


N_CHIP = N_DEV // 2


def _scatter1_plan(n):
    def plan(refs):
        _, _, c, sibling, _ = _place()
        out = []
        for wi in range(n):
            parts, half = refs[wi], refs[n + wi]
            for chip in range(N_CHIP):
                out.append((parts.at[2 * chip + 1 - c], half.at[chip], sibling, half.at[chip]))
        return out

    return plan


def _scatter2_plan(n):
    def plan(refs):
        _, my_chip, _, _, others = _place()
        out = []
        for wi in range(n):
            sums, recv = refs[wi], refs[n + wi]
            for peer, chip in others:
                out.append((sums.at[chip], recv.at[my_chip], peer, recv.at[chip]))
        return out

    return plan


def _scatter2_local(n):
    def plan(refs):
        my_chip = _place()[1]
        return [(refs[wi].at[my_chip], refs[n + wi].at[my_chip]) for wi in range(n)]

    return plan


def _pair_add(parts, half, core, *, name):
    _, r, c = parts.shape
    tr = max(d for d in range(HALO, 257, HALO) if r % d == 0) if r > 256 else r
    parts4 = parts.reshape(N_CHIP, 2, r, c)

    def body(core_ref, p_ref, h_ref, o_ref):
        o_ref[...] = (p_ref[:, 0].astype(F32) + h_ref[...].astype(F32)).astype(o_ref.dtype)

    return pl.pallas_call(
        body, name=name,
        grid_spec=pltpu.PrefetchScalarGridSpec(
            num_scalar_prefetch=1, grid=(r // tr,),
            in_specs=[pl.BlockSpec((N_CHIP, 1, tr, c), lambda i, core_ref: (0, core_ref[0], i, 0)),
                      pl.BlockSpec((N_CHIP, tr, c), lambda i, core_ref: (0, i, 0))],
            out_specs=pl.BlockSpec((N_CHIP, tr, c), lambda i, core_ref: (0, i, 0))),
        out_shape=jax.ShapeDtypeStruct((N_CHIP, r, c), parts.dtype), compiler_params=_cp("parallel"),
    )(core, parts4, half)


def _scatter_start(parts, *, name):
    halves = [lax.empty((N_CHIP,) + v.shape[1:], v.dtype) for v in parts]
    return _split_start(parts + halves, N_CHIP * len(parts), _scatter1_plan(len(parts)), name=name)


def _adamw(parts, w, m, v, *, name):
    r, c = w.shape
    nparts = parts.shape[0]
    tr = max(d for d in range(HALO, 129, HALO) if r % d == 0) if r > 128 else r

    def body(p_ref, w_ref, m_ref, v_ref, g_ref, d_ref, mo_ref, vo_ref):
        g = p_ref[0].astype(F32)
        for k in range(1, nparts):
            g = g + p_ref[k].astype(F32)
        mn = ADAM_B1 * m_ref[...] + (1.0 - ADAM_B1) * g
        vn = ADAM_B2 * v_ref[...] + (1.0 - ADAM_B2) * (g * g)
        m_hat = mn / (1.0 - ADAM_B1 ** ADAM_STEP)
        v_hat = vn / (1.0 - ADAM_B2 ** ADAM_STEP)
        g_ref[...] = g
        d_ref[...] = -ADAM_LR * (m_hat / (jnp.sqrt(v_hat) + ADAM_EPS) + ADAM_WD * w_ref[...])
        mo_ref[...] = mn
        vo_ref[...] = vn

    blk = pl.BlockSpec((tr, c), lambda i: (i, 0))
    return pl.pallas_call(
        body, name=name, grid=(r // tr,), in_specs=[pl.BlockSpec((nparts, tr, c), lambda i: (0, i, 0)), blk, blk, blk],
        out_specs=[blk] * 4, out_shape=[jax.ShapeDtypeStruct((r, c), F32)] * 4, compiler_params=_cp("parallel"),
    )(parts, w, m, v)


def _pack_rows(vs, rows):
    lead = vs[0].shape[:-1] if vs[0].ndim > 1 else ()
    flat = jnp.concatenate(vs, axis=-1)
    pad = rows * LANES - flat.shape[-1]
    flat = jnp.pad(flat, [(0, 0)] * len(lead) + [(0, pad)])
    return flat.reshape(lead + (rows, LANES))


def kernel(x, p, positions, mix_norm_w, w_in, conv_w, conv_b, dt_bias, a_log, d_skip, ssd_norm_w, q_a_norm_w, w_q_b, kv_a_norm_w, w_kv_b, w_out, ffn_norm_w, w_ffn_up, ffn_conv_w, ffn_conv_b, w_ffn_down, ple_norm_w, w_ple_gate, b_ple_gate, w_ple_proj, ple_post_norm_w, final_norm_w, loss_target, m_mix_norm_w, m_w_in, m_conv_w, m_conv_b, m_dt_bias, m_a_log, m_d_skip, m_ssd_norm_w, m_q_a_norm_w, m_w_q_b, m_kv_a_norm_w, m_w_kv_b, m_w_out, m_ffn_norm_w, m_w_ffn_up, m_ffn_conv_w, m_ffn_conv_b, m_w_ffn_down, m_ple_norm_w, m_w_ple_gate, m_b_ple_gate, m_w_ple_proj, m_ple_post_norm_w, m_final_norm_w, v_mix_norm_w, v_w_in, v_conv_w, v_conv_b, v_dt_bias, v_a_log, v_d_skip, v_ssd_norm_w, v_q_a_norm_w, v_w_q_b, v_kv_a_norm_w, v_w_kv_b, v_w_out, v_ffn_norm_w, v_w_ffn_up, v_ffn_conv_w, v_ffn_conv_b, v_w_ffn_down, v_ple_norm_w, v_w_ple_gate, v_b_ple_gate, v_w_ple_proj, v_ple_post_norm_w, v_final_norm_w):
    given = dict(locals())
    shapes = {n: given[n].shape for n in WEIGHTS}
    w2 = {n: given[n].reshape(given[n].shape[-2:] if n in BIG or n in CONV else (1, -1)) for n in WEIGHTS}
    m2 = {n: given['m_' + n].reshape(w2[n].shape) for n in WEIGHTS}
    v2 = {n: given['v_' + n].reshape(w2[n].shape) for n in WEIGHTS}
    me = 4 * lax.axis_index("x") + 2 * lax.axis_index("y") + lax.axis_index("c")

    core = lax.axis_index("c").astype(jnp.int32).reshape(1)

    def shards(grp, zero):
        return [(w2[n] + zero).astype(BF16) if n in BIG else w2[n] + zero for n in WEIGHT_GROUPS[grp]]

    first, token = _gather_start(shards('a', 0.0), name="gather_a_hop1")
    first, token = _gather_forward(first, token, name="gather_a_hop2")
    zero = token[0, 0]
    later = dict(zip(WEIGHT_GROUPS['b'], _sequencer_gather(shards('b', zero), collective_id=1, name="gather_b")))
    later.update(zip(WEIGHT_GROUPS['c'], _sequencer_gather(shards('c', zero), collective_id=2, name="gather_c")))

    def get_w(grp, after):
        if grp == 'a':
            lands = dict(zip(WEIGHT_GROUPS[grp], _gather_finish(first, token, name="gather_a_done")))
        else:
            names = WEIGHT_GROUPS[grp]
            lands = dict(zip(names, _hold([later[n] for n in names], after, name="gather_" + grp + "_use")))
        return _assemble_weights(lands)

    scatters = {}

    hop_ids = {grp: 2 + 2 * i for i, grp in enumerate(GRAD_GROUPS)}

    def zero_of(arrays):
        return sum(v[(0,) * v.ndim].astype(F32) * 0.0 for v in arrays)

    def emit(grp, grads):
        scatters[grp], tok = _scatter_start([grads[n] for n in GRAD_GROUPS[grp]], name="scatter_" + grp + "_hop1")
        return tok[0, 0]

    def relay(grp, after):
        n = len(GRAD_GROUPS[grp])
        bufs = _split_wait(scatters[grp], after, _scatter1_plan(n), lambda refs: [], name="scatter_" + grp + "_hop1_wait")
        sums = [_pair_add(bufs[i], bufs[n + i], core, name="scatter_%s_add%d" % (grp, i)) for i in range(n)]
        scatters[grp] = _sequencer_scatter_hop2(sums, collective_id=hop_ids[grp] + 1, name="scatter_" + grp + "_hop2")
        return zero_of(sums)

    out_g, out_d, out_m, out_v = {}, {}, {}, {}

    def settle(grp):
        return zero_of(scatters[grp])

    def update(grp, behind=None):
        for n, parts in zip(GRAD_GROUPS[grp], scatters[grp]):
            wn = w2[n] if behind is None else w2[n] + behind
            out_g[n], out_d[n], out_m[n], out_v[n] = _adamw(parts, wn, m2[n], v2[n], name="adamw_" + n)

    vecs = {n: w2[n] for n in REPL}
    vecs['mix_norm_w'] = vecs['mix_norm_w'] + zero
    loss, dx, g_conv, g_vec = _local_step(x[0], p[0, 0], _rope_tables(positions), get_w, vecs, loss_target[0], emit,
                                          relay, settle)
    n_small = sum(g_vec[n].shape[1] for n in REPL) + sum(g_conv[n].size for n in CONV) + 1
    rows_small = -(-n_small // (LANES * HALO)) * HALO
    small = _pack_rows([g_vec[n] for n in REPL] + [g_conv[n].reshape(1, -1) for n in CONV] + [loss], rows_small)

    for grp in list(GRAD_GROUPS)[:-1]:
        update(grp)
    all_small = _exchange([small], gather=True, name="gather_small_grads")[0].reshape(N_DEV, rows_small * LANES)
    update(list(GRAD_GROUPS)[-1], zero_of([all_small]))
    pieces, off = [], 0
    for n in REPL:
        k = g_vec[n].shape[1]
        pieces.append(all_small[:, off:off + k])
        off += k
    for n in CONV:
        kw, cols = g_conv[n].shape
        full = all_small[:, off:off + kw * cols].reshape(N_DEV, kw, cols)
        mine = lax.dynamic_slice_in_dim(full, me * (cols // N_DEV), cols // N_DEV, axis=2)
        pieces.append(mine.reshape(N_DEV, kw * (cols // N_DEV)))
        off += kw * cols
    pieces.append(all_small[:, off:off + 1])
    small_names = REPL + CONV
    n_mine = sum(q.shape[1] for q in pieces)
    rows_mine = -(-n_mine // (LANES * HALO)) * HALO
    zero = jnp.zeros((1, 1), F32)
    packed = [_pack_rows([src[n].reshape(1, -1) for n in small_names] + [zero], rows_mine).reshape(rows_mine, LANES)
              for src in (w2, m2, v2)]
    sg, sd, sm, sv = _adamw(_pack_rows(pieces, rows_mine), *packed, name="adamw_small")
    off = 0
    for n in small_names:
        k = w2[n].size
        for dst, src in ((out_g, sg), (out_d, sd), (out_m, sm), (out_v, sv)):
            dst[n] = src.reshape(-1)[off:off + k].reshape(w2[n].shape)
        off += k
    total_loss = sg.reshape(-1)[off]

    outs = [total_loss, dx[None]]
    for res in (out_g, out_d, out_m, out_v):
        outs += [res[n].reshape(shapes[n]) for n in WEIGHTS]
    return tuple(outs)
```

```python
import math

import numpy as np
import jax
import jax.numpy as jnp
from jax import lax
from jax.experimental import pallas as pl
from jax.experimental.pallas import tpu as pltpu
from jax.experimental.pallas import tpu_sc as plsc

F32 = jnp.float32
BF16 = jnp.bfloat16
HI = lax.Precision.HIGHEST

D_MODEL = 2048
CHUNK = 64
D_SSM = 1024
SSD_P = 64
SSD_HEADS = 16
SSD_GROUPS = 2
SSD_N = 128
SSD_CONV = 4
SSD_CONV_DIM = D_SSM + 2 * SSD_GROUPS * SSD_N
MLA_HEADS = 8
MLA_NOPE = 128
MLA_ROPE = 64
MLA_V = 128
MLA_Q_RANK = 512
MLA_KV_RANK = 256
MLA_QK_PAD = 256
ROPE_THETA = 10000.0
D_FF = 5632
FFN_CONV = 3
PLE_DIM = 256
NORM_EPS = 1e-6
ADAM_LR, ADAM_B1, ADAM_B2, ADAM_EPS, ADAM_WD, ADAM_STEP = 0.001, 0.9, 0.999, 1e-08, 0.01, 10
N_DEV = 8

OFF_Z, OFF_XBC, OFF_QA, OFF_CKV, OFF_KR, OFF_DT, D_IN_PAD = 0, 1024, 2560, 3072, 3328, 3456, 3584
D_IN = 3408
LANES = 128
HALO = 8
VMEM_LIMIT = 56 * 1024 * 1024
FFN_TC = D_FF * 2 // N_DEV
FFN_PERM = (0, 4, 1, 5, 2, 6, 3, 7)
NEG = -1e30


def _cp(*sem):
    return pltpu.CompilerParams(dimension_semantics=tuple(sem), vmem_limit_bytes=VMEM_LIMIT)


def _tile(n, want):
    if n <= want:
        return n
    best = max(d for d in range(LANES, want + 1, LANES) if n % d == 0)
    return best


def _sigmoid(x):
    return 0.5 * (jnp.tanh(0.5 * x) + 1.0)


def _silu(x):
    return x * _sigmoid(x)


def _dsilu(x):
    s = _sigmoid(x)
    return s * (1.0 + x * (1.0 - s))


MM_TILE = 1408
MM_TK = 2816


def _matmul(a, b, *, ta=False, tb=False, out_dtype=F32, add=None, bias=None, tm=MM_TILE, tn=MM_TILE, tk=MM_TK, name,
            mnk=None, a_spec=None, b_spec=None, o_spec=None, o_shape=None):
    if mnk is None:
        m, k = (a.shape[1], a.shape[0]) if ta else a.shape
        n = b.shape[0] if tb else b.shape[1]
        assert k == (b.shape[1] if tb else b.shape[0])
    else:
        m, n, k = mnk
    tm, tn, tk = _tile(m, tm), _tile(n, tn), _tile(k, tk)
    nk = k // tk
    dims = (((0 if ta else 1,), (1 if tb else 0,)), ((), ()))

    def body(*refs):
        a_ref, b_ref = refs[0], refs[1]
        pos = 2
        add_ref = bias_ref = None
        if add is not None:
            add_ref = refs[pos]
            pos += 1
        if bias is not None:
            bias_ref = refs[pos]
            pos += 1
        o_ref = refs[pos]
        kk = pl.program_id(2)
        av = a_ref[...]
        bv = b_ref[...]
        av = av.reshape(av.shape[-2:]).astype(BF16)
        bv = bv.reshape(bv.shape[-2:]).astype(BF16)
        prod = lax.dot_general(av, bv, dims, preferred_element_type=F32)

        def finish(r):
            if bias_ref is not None:
                r = r + bias_ref[...]
            if add_ref is not None:
                r = r + add_ref[...].astype(F32)
            o_ref[...] = r.astype(out_dtype).reshape(o_ref.shape)

        if nk == 1:
            finish(prod)
        else:
            acc_ref = refs[pos + 1]

            @pl.when(kk == 0)
            def _():
                acc_ref[...] = prod

            @pl.when(kk > 0)
            def _():
                acc_ref[...] += prod

            @pl.when(kk == nk - 1)
            def _():
                finish(acc_ref[...])

    if a_spec is None:
        a_spec = (pl.BlockSpec((tk, tm), lambda i, j, kk: (kk, i)) if ta
                  else pl.BlockSpec((tm, tk), lambda i, j, kk: (i, kk)))
    if b_spec is None:
        b_spec = (pl.BlockSpec((tn, tk), lambda i, j, kk: (j, kk)) if tb
                  else pl.BlockSpec((tk, tn), lambda i, j, kk: (kk, j)))
    if o_spec is None:
        o_spec = pl.BlockSpec((tm, tn), lambda i, j, kk: (i, j))
    if o_shape is None:
        o_shape = (m, n)
    in_specs = [a_spec, b_spec]
    args = [a, b]
    if add is not None:
        in_specs.append(pl.BlockSpec((tm, tn), lambda i, j, kk: (i, j)))
        args.append(add)
    if bias is not None:
        in_specs.append(pl.BlockSpec((1, tn), lambda i, j, kk: (0, j)))
        args.append(bias)
    return pl.pallas_call(
        body, name=name, grid=(m // tm, n // tn, nk), in_specs=in_specs, out_specs=o_spec,
        out_shape=jax.ShapeDtypeStruct(o_shape, out_dtype),
        scratch_shapes=[pltpu.VMEM((tm, tn), F32)] if nk > 1 else [],
        compiler_params=_cp("parallel", "parallel", "arbitrary"),
    )(*args)


def _rmsnorm_fwd(x, w, *, width, cblk=0, out_dtype=BF16, tr=256, name):
    t = x.shape[0]

    def body(x_ref, w_ref, o_ref):
        xv = x_ref[...].astype(F32)
        r = lax.rsqrt(jnp.mean(xv * xv, axis=-1, keepdims=True) + NORM_EPS)
        o_ref[...] = (xv * r * w_ref[...]).astype(out_dtype)

    return pl.pallas_call(
        body, name=name, grid=(t // tr,),
        in_specs=[pl.BlockSpec((tr, width), lambda i: (i, cblk)), pl.BlockSpec((1, width), lambda i: (0, 0))],
        out_specs=pl.BlockSpec((tr, width), lambda i: (i, 0)),
        out_shape=jax.ShapeDtypeStruct((t, width), out_dtype),
        compiler_params=_cp("parallel"),
    )(x, w)


def _rmsnorm_bwd(x, w, dy, add=None, *, width, cblk=0, out_dtype=F32, also_bf16=False, tr=256, name):
    t = x.shape[0]

    def body(*refs):
        refs = list(refs)
        dxb_ref = refs.pop() if also_bf16 else None
        if add is None:
            x_ref, w_ref, dy_ref, dx_ref, dw_ref = refs
            add_ref = None
        else:
            x_ref, w_ref, dy_ref, add_ref, dx_ref, dw_ref = refs
        xv = x_ref[...].astype(F32)
        dyv = dy_ref[...].astype(F32)
        r = lax.rsqrt(jnp.mean(xv * xv, axis=-1, keepdims=True) + NORM_EPS)
        xh = xv * r
        g = dyv * w_ref[...]
        dx = r * (g - xh * jnp.mean(g * xh, axis=-1, keepdims=True))
        if add_ref is not None:
            dx = dx + add_ref[...].astype(F32)
        dx_ref[...] = dx.astype(out_dtype)
        if dxb_ref is not None:
            dxb_ref[...] = dx.astype(BF16)

        @pl.when(pl.program_id(0) == 0)
        def _():
            dw_ref[...] = jnp.zeros_like(dw_ref)

        dw_ref[...] += jnp.sum(dyv * xh, axis=0, keepdims=True)

    in_specs = [pl.BlockSpec((tr, width), lambda i: (i, cblk)), pl.BlockSpec((1, width), lambda i: (0, 0)),
                pl.BlockSpec((tr, width), lambda i: (i, 0))]
    args = [x, w, dy]
    if add is not None:
        in_specs.append(pl.BlockSpec((tr, width), lambda i: (i, 0)))
        args.append(add)
    blk = pl.BlockSpec((tr, width), lambda i: (i, 0))
    return pl.pallas_call(
        body, name=name, grid=(t // tr,), in_specs=in_specs,
        out_specs=[blk, pl.BlockSpec((1, width), lambda i: (0, 0))] + ([blk] if also_bf16 else []),
        out_shape=[jax.ShapeDtypeStruct((t, width), out_dtype), jax.ShapeDtypeStruct((1, width), F32)]
        + ([jax.ShapeDtypeStruct((t, width), BF16)] if also_bf16 else []),
        compiler_params=_cp("arbitrary"),
    )(*args)


def _shift_down(prev_halo, cur, j):
    if j == 0:
        return cur
    ext = jnp.concatenate([prev_halo, cur], axis=0)
    return pltpu.roll(ext, j, axis=0)[HALO:]


def _shift_up(cur, next_halo, j):
    if j == 0:
        return cur
    ext = jnp.concatenate([cur, next_halo], axis=0)
    return pltpu.roll(ext, ext.shape[0] - j, axis=0)[:cur.shape[0]]


def _conv_rows(prev, cur, w, b, kw):
    shifted = [cur]
    out = b + w[kw - 1:kw] * cur
    for j in range(1, kw):
        sh = _shift_down(prev, cur, j)
        shifted.append(sh)
        out = out + w[kw - 1 - j:kw - j] * sh
    return out, shifted


def _act_fwd(c, glu):
    if glu:
        half = c.shape[1] // 2
        return _silu(c[:, :half]) * c[:, half:]
    return _silu(c)


def _act_bwd(c, dout, glu):
    if glu:
        half = c.shape[1] // 2
        g, up = c[:, :half], c[:, half:]
        s = _sigmoid(g)
        gs = g * s
        return jnp.concatenate([dout * up * (s + gs * (1.0 - s)), dout * gs], axis=1)
    return dout * _dsilu(c)


def _conv_act_fwd(u, w, b, *, kw, glu, tc, coff, ncols, out_dtype, tr=256, name):
    t = u.shape[0]
    nb = ncols // tc
    oc = tc // 2 if glu else tc

    def body(u_ref, uh_ref, w_ref, b_ref, o_ref):
        prev = jnp.where(pl.program_id(0) == 0, 0.0, uh_ref[...])
        c, _ = _conv_rows(prev, u_ref[...], w_ref[...], b_ref[...], kw)
        o_ref[...] = _act_fwd(c, glu).astype(out_dtype)

    return pl.pallas_call(
        body, name=name, grid=(t // tr, nb),
        in_specs=[pl.BlockSpec((tr, tc), lambda i, j: (i, j + coff)),
                  pl.BlockSpec((HALO, tc), lambda i, j: (jnp.maximum(i * (tr // HALO) - 1, 0), j + coff)),
                  pl.BlockSpec((kw, tc), lambda i, j: (0, j)), pl.BlockSpec((1, tc), lambda i, j: (0, j))],
        out_specs=pl.BlockSpec((tr, oc), lambda i, j: (i, j)),
        out_shape=jax.ShapeDtypeStruct((t, nb * oc), out_dtype),
        compiler_params=_cp("parallel", "parallel"),
    )(u, u, w, b)


def _conv_act_bwd(u, w, b, dout, *, kw, glu, tc, coff, ncols, tr=256, name):
    t = u.shape[0]
    nb = ncols // tc
    nt = t // tr
    oc = tc // 2 if glu else tc

    def body(u_ref, up_ref, un_ref, d_ref, dn_ref, w_ref, b_ref, du_ref, dw_ref, db_ref):
        i = pl.program_id(1)
        cur, nxt, wv, bv = u_ref[...], un_ref[...], w_ref[...], b_ref[...]
        prev = jnp.where(i == 0, 0.0, up_ref[...])
        c_cur, shifted = _conv_rows(prev, cur, wv, bv, kw)
        c_nxt, _ = _conv_rows(cur[tr - HALO:], nxt, wv, bv, kw)
        d_cur = _act_bwd(c_cur, d_ref[...].astype(F32), glu)
        d_nxt = _act_bwd(c_nxt, jnp.where(i == nt - 1, 0.0, dn_ref[...].astype(F32)), glu)
        du = wv[kw - 1:kw] * d_cur
        for j in range(1, kw):
            du = du + wv[kw - 1 - j:kw - j] * _shift_up(d_cur, d_nxt, j)
        du_ref[...] = du.astype(BF16)

        @pl.when(i == 0)
        def _():
            dw_ref[...] = jnp.zeros_like(dw_ref)
            db_ref[...] = jnp.zeros_like(db_ref)

        db_ref[...] += jnp.sum(d_cur, axis=0, keepdims=True)
        dw_ref[...] += jnp.concatenate(
            [jnp.sum(d_cur * shifted[kw - 1 - k], axis=0, keepdims=True) for k in range(kw)], axis=0)

    nh = tr // HALO
    return pl.pallas_call(
        body, name=name, grid=(nb, nt),
        in_specs=[pl.BlockSpec((tr, tc), lambda j, i: (i, j + coff)),
                  pl.BlockSpec((HALO, tc), lambda j, i: (jnp.maximum(i * nh - 1, 0), j + coff)),
                  pl.BlockSpec((HALO, tc), lambda j, i: (jnp.minimum((i + 1) * nh, t // HALO - 1), j + coff)),
                  pl.BlockSpec((tr, oc), lambda j, i: (i, j)),
                  pl.BlockSpec((HALO, oc), lambda j, i: (jnp.minimum((i + 1) * nh, t // HALO - 1), j)),
                  pl.BlockSpec((kw, tc), lambda j, i: (0, j)), pl.BlockSpec((1, tc), lambda j, i: (0, j))],
        out_specs=[pl.BlockSpec((tr, tc), lambda j, i: (i, j)), pl.BlockSpec((kw, tc), lambda j, i: (0, j)),
                   pl.BlockSpec((1, tc), lambda j, i: (0, j))],
        out_shape=[jax.ShapeDtypeStruct((t, ncols), BF16), jax.ShapeDtypeStruct((kw, ncols), F32),
                   jax.ShapeDtypeStruct((1, ncols), F32)],
        compiler_params=_cp("parallel", "arbitrary"),
    )(u, u, u, dout, dout, w, b)


def _ple_loss(x2, gl, pe, pw, fw, target, *, tr=256, name):
    t, d = x2.shape

    def body(x_ref, gl_ref, pe_ref, pw_ref, fw_ref, t_ref, l_ref, dx_ref, dfw_ref, dgl_ref, db_ref, dpe_ref, dpw_ref):
        pv, pwv, wv = pe_ref[...], pw_ref[...], fw_ref[...]
        gate = _sigmoid(gl_ref[...])
        rp = lax.rsqrt(jnp.mean(pv * pv, axis=-1, keepdims=True) + NORM_EPS)
        ph = pv * rp
        e = ph * pwv
        x3 = x_ref[...] + gate * e
        r = lax.rsqrt(jnp.mean(x3 * x3, axis=-1, keepdims=True) + NORM_EPS)
        xh = x3 * r
        err = xh * wv - t_ref[...]
        dy = err * (1.0 / d)
        g = dy * wv
        dx = r * (g - xh * jnp.mean(g * xh, axis=-1, keepdims=True))
        dx_ref[...] = dx
        dgl = dx * e * gate * (1.0 - gate)
        de = dx * gate
        gg = de * pwv
        dgl_ref[...] = dgl.astype(BF16)
        dpe_ref[...] = (rp * (gg - ph * jnp.mean(gg * ph, axis=-1, keepdims=True))).astype(BF16)

        @pl.when(pl.program_id(0) == 0)
        def _():
            for ref in (l_ref, dfw_ref, db_ref, dpw_ref):
                ref[...] = jnp.zeros_like(ref)

        l_ref[...] += 0.5 * jnp.sum(jnp.mean(err * err, axis=-1, keepdims=True), axis=0, keepdims=True)
        dfw_ref[...] += jnp.sum(dy * xh, axis=0, keepdims=True)
        db_ref[...] += jnp.sum(dgl, axis=0, keepdims=True)
        dpw_ref[...] += jnp.sum(de * ph, axis=0, keepdims=True)

    blk = pl.BlockSpec((tr, d), lambda i: (i, 0))
    row = pl.BlockSpec((1, d), lambda i: (0, 0))
    rowf = jax.ShapeDtypeStruct((1, d), F32)
    return pl.pallas_call(
        body, name=name, grid=(t // tr,), in_specs=[blk, blk, blk, row, row, blk],
        out_specs=[pl.BlockSpec((1, 1), lambda i: (0, 0)), blk, row, blk, row, blk, row],
        out_shape=[jax.ShapeDtypeStruct((1, 1), F32), jax.ShapeDtypeStruct((t, d), F32), rowf,
                   jax.ShapeDtypeStruct((t, d), BF16), rowf, jax.ShapeDtypeStruct((t, d), BF16), rowf],
        compiler_params=_cp("arbitrary"),
    )(x2, gl, pe, pw, fw, target)


def _rope(blk, tab_ref):
    return blk * tab_ref[0] + pltpu.roll(blk, 96, axis=1) * tab_ref[1] + pltpu.roll(blk, 32, axis=1) * tab_ref[2]


def _unrope(g, tab_ref):
    return g * tab_ref[0] + pltpu.roll(g * tab_ref[1], 32, axis=1) + pltpu.roll(g * tab_ref[2], 96, axis=1)


def _mla_prep(q, kv, proj, tabs, *, tr=512, name):
    t = q.shape[0]

    def body(q_ref, kv_ref, kr_ref, tab_ref, qo_ref, ko_ref, vo_ref, vt_ref):
        qv, kvv = q_ref[...], kv_ref[...]
        qo_ref[0, :, :MLA_NOPE] = qv[:, :MLA_NOPE].astype(BF16)
        qo_ref[0, :, MLA_NOPE:] = _rope(qv[:, MLA_NOPE:], tab_ref).astype(BF16)
        ko_ref[0, :, :MLA_NOPE] = kvv[:, :MLA_NOPE].astype(BF16)
        ko_ref[0, :, MLA_NOPE:] = _rope(kr_ref[...], tab_ref).astype(BF16)
        vo_ref[0] = kvv[:, MLA_NOPE:].astype(BF16)
        for blk in range(tr // ATT_BLK):
            vt_ref[0, blk] = kvv[blk * ATT_BLK:(blk + 1) * ATT_BLK, MLA_NOPE:].T.astype(BF16)

    return pl.pallas_call(
        body, name=name, grid=(t // tr, MLA_HEADS),
        in_specs=[pl.BlockSpec((tr, MLA_QK_PAD), lambda i, h: (i, h)),
                  pl.BlockSpec((tr, MLA_NOPE + MLA_V), lambda i, h: (i, h)),
                  pl.BlockSpec((tr, LANES), lambda i, h: (i, OFF_KR // LANES)),
                  pl.BlockSpec((3, tr, LANES), lambda i, h: (0, i, 0))],
        out_specs=[pl.BlockSpec((1, tr, MLA_QK_PAD), lambda i, h: (h, i, 0)),
                   pl.BlockSpec((1, tr, MLA_QK_PAD), lambda i, h: (h, i, 0)),
                   pl.BlockSpec((1, tr, MLA_V), lambda i, h: (h, i, 0)),
                   pl.BlockSpec((1, tr // ATT_BLK, MLA_V, ATT_BLK), lambda i, h: (h, i, 0, 0))],
        out_shape=[jax.ShapeDtypeStruct((MLA_HEADS, t, MLA_QK_PAD), BF16),
                   jax.ShapeDtypeStruct((MLA_HEADS, t, MLA_QK_PAD), BF16),
                   jax.ShapeDtypeStruct((MLA_HEADS, t, MLA_V), BF16),
                   jax.ShapeDtypeStruct((MLA_HEADS, t // ATT_BLK, MLA_V, ATT_BLK), BF16)],
        compiler_params=_cp("parallel", "parallel"),
    )(q, kv, proj, tabs)


def _mla_unprep(dq3, dk3, dv3, tabs, *, tr=256, name):
    t = dq3.shape[1]

    def body(dq_ref, dk_ref, dv_ref, tab_ref, qo_ref, kvo_ref, kro_ref):
        kr = jnp.zeros((tr, LANES), F32)
        for h in range(MLA_HEADS):
            c0 = h * MLA_QK_PAD
            qo_ref[:, c0:c0 + MLA_NOPE] = dq_ref[h, :, :MLA_NOPE].astype(BF16)
            qo_ref[:, c0 + MLA_NOPE:c0 + MLA_QK_PAD] = _unrope(dq_ref[h, :, MLA_NOPE:], tab_ref).astype(BF16)
            kvo_ref[:, c0:c0 + MLA_NOPE] = dk_ref[h, :, :MLA_NOPE].astype(BF16)
            kvo_ref[:, c0 + MLA_NOPE:c0 + MLA_QK_PAD] = dv_ref[h].astype(BF16)
            kr = kr + dk_ref[h, :, MLA_NOPE:]
        kro_ref[...] = _unrope(kr, tab_ref).astype(BF16)

    return pl.pallas_call(
        body, name=name, grid=(t // tr,),
        in_specs=[pl.BlockSpec((MLA_HEADS, tr, MLA_QK_PAD), lambda i: (0, i, 0)),
                  pl.BlockSpec((MLA_HEADS, tr, MLA_QK_PAD), lambda i: (0, i, 0)),
                  pl.BlockSpec((MLA_HEADS, tr, MLA_V), lambda i: (0, i, 0)),
                  pl.BlockSpec((3, tr, LANES), lambda i: (0, i, 0))],
        out_specs=[pl.BlockSpec((tr, MLA_HEADS * MLA_QK_PAD), lambda i: (i, 0)),
                   pl.BlockSpec((tr, MLA_HEADS * MLA_QK_PAD), lambda i: (i, 0)),
                   pl.BlockSpec((tr, LANES), lambda i: (i, 0))],
        out_shape=[jax.ShapeDtypeStruct((t, MLA_HEADS * MLA_QK_PAD), BF16),
                   jax.ShapeDtypeStruct((t, MLA_HEADS * MLA_QK_PAD), BF16),
                   jax.ShapeDtypeStruct((t, LANES), BF16)],
        compiler_params=_cp("parallel"),
    )(dq3, dk3, dv3, tabs)


ATT_BLK = 512
ATT_SCALE = 1.0 / math.sqrt(MLA_NOPE + MLA_ROPE)
_NT = (((1,), (1,)), ((), ()))
_TN = (((0,), (0,)), ((), ()))


def _att_scores_t(k, q, diagonal):
    s = lax.dot_general(k, q, _NT, preferred_element_type=F32) * ATT_SCALE
    if not diagonal:
        return s
    key = lax.broadcasted_iota(jnp.int32, s.shape, 0)
    query = lax.broadcasted_iota(jnp.int32, s.shape, 1)
    return jnp.where((key >> 6) <= (query >> 6), s, NEG)


def _att_rows(i):
    return pl.ds(pl.multiple_of(i * ATT_BLK, ATT_BLK), ATT_BLK)


ATT_HEADS = 2


def _attn_fwd(q3, k3, vt4, *, name):
    t = q3.shape[1]
    nq = t // ATT_BLK

    def body(q_ref, k_ref, vt_ref, o_ref, lse_ref):
        qi = pl.program_id(1)
        qs = [q_ref[hh] for hh in range(ATT_HEADS)]

        def step(j, carry, diagonal=False):
            out = []
            for hh, (m, l, acc) in enumerate(carry):
                s = _att_scores_t(k_ref[hh, _att_rows(j), :], qs[hh], diagonal)
                m_new = jnp.maximum(m, jnp.max(s, axis=0, keepdims=True))
                p = jnp.exp(s - m_new)
                alpha = jnp.exp(m - m_new)
                l = alpha * l + jnp.sum(p, axis=0, keepdims=True)
                acc = alpha * acc + jnp.dot(vt_ref[hh, j], p.astype(BF16), preferred_element_type=F32)
                out.append((m_new, l, acc))
            return tuple(out)

        init = tuple((jnp.full((1, ATT_BLK), NEG, F32), jnp.zeros((1, ATT_BLK), F32),
                      jnp.zeros((MLA_V, ATT_BLK), F32)) for _ in range(ATT_HEADS))
        done = step(qi, lax.fori_loop(0, qi, step, init), diagonal=True)
        for hh, (m, l, acc) in enumerate(done):
            o_ref[:, hh * MLA_V:(hh + 1) * MLA_V] = (acc / l).T
            lse_ref[hh, 0] = m + jnp.log(l)

    return pl.pallas_call(
        body, name=name, grid=(MLA_HEADS // ATT_HEADS, nq),
        in_specs=[pl.BlockSpec((ATT_HEADS, ATT_BLK, MLA_QK_PAD), lambda h, i: (h, i, 0)),
                  pl.BlockSpec((ATT_HEADS, t, MLA_QK_PAD), lambda h, i: (h, 0, 0)),
                  pl.BlockSpec((ATT_HEADS, nq, MLA_V, ATT_BLK), lambda h, i: (h, 0, 0, 0))],
        out_specs=[pl.BlockSpec((ATT_BLK, ATT_HEADS * MLA_V), lambda h, i: (i, h)),
                   pl.BlockSpec((ATT_HEADS, 1, 1, ATT_BLK), lambda h, i: (h, i, 0, 0))],
        out_shape=[jax.ShapeDtypeStruct((t, MLA_HEADS * MLA_V), F32),
                   jax.ShapeDtypeStruct((MLA_HEADS, nq, 1, ATT_BLK), F32)],
        compiler_params=_cp("parallel", "parallel"),
    )(q3, k3, vt4)


def _attn_bwd(q3, k3, v3, o, dcat, lse, *, name):
    t = q3.shape[1]
    nq = t // ATT_BLK
    wide = ATT_HEADS * MLA_V

    def body(q_ref, k_ref, v_ref, o_ref, do_ref, lse_ref, dq_ref, dk_ref, dv_ref, delta_ref):
        kj = pl.program_id(1)

        @pl.when(kj == 0)
        def _():
            dq_ref[...] = jnp.zeros_like(dq_ref)
            ones = jnp.ones((HALO, MLA_V), F32)
            for i in range(nq):
                rows = pl.ds(i * ATT_BLK, ATT_BLK)
                prod = o_ref[rows, :] * do_ref[rows, :]
                for hh in range(ATT_HEADS):
                    delta_ref[hh, i] = lax.dot_general(ones, prod[:, hh * MLA_V:(hh + 1) * MLA_V], _NT, precision=HI,
                                                       preferred_element_type=F32)

        def step(i, carry, diagonal=False):
            rows = _att_rows(i)
            out = []
            for hh, (dk, dv) in enumerate(carry):
                k, v = k_ref[hh], v_ref[hh]
                q = q_ref[hh, rows, :]
                dob = do_ref[rows, hh * MLA_V:(hh + 1) * MLA_V].astype(BF16)
                p = jnp.exp(_att_scores_t(k, q, diagonal) - lse_ref[hh, i])
                dv = dv + jnp.dot(p.astype(BF16), dob, preferred_element_type=F32)
                dp = lax.dot_general(v, dob, _NT, preferred_element_type=F32)
                ds = (p * (dp - delta_ref[hh, i, 0:1, :]) * ATT_SCALE).astype(BF16)
                dk = dk + jnp.dot(ds, q, preferred_element_type=F32)
                dq_ref[hh, rows, :] += lax.dot_general(ds, k, _TN, preferred_element_type=F32)
                out.append((dk, dv))
            return tuple(out)

        init = tuple((jnp.zeros((ATT_BLK, MLA_QK_PAD), F32), jnp.zeros((ATT_BLK, MLA_V), F32))
                     for _ in range(ATT_HEADS))
        done = lax.fori_loop(kj + 1, nq, step, step(kj, init, diagonal=True))
        for hh, (dk, dv) in enumerate(done):
            dk_ref[hh] = dk
            dv_ref[hh] = dv

    return pl.pallas_call(
        body, name=name, grid=(MLA_HEADS // ATT_HEADS, nq),
        in_specs=[pl.BlockSpec((ATT_HEADS, t, MLA_QK_PAD), lambda h, j: (h, 0, 0)),
                  pl.BlockSpec((ATT_HEADS, ATT_BLK, MLA_QK_PAD), lambda h, j: (h, j, 0)),
                  pl.BlockSpec((ATT_HEADS, ATT_BLK, MLA_V), lambda h, j: (h, j, 0)),
                  pl.BlockSpec((t, wide), lambda h, j: (0, h)),
                  pl.BlockSpec((t, wide), lambda h, j: (0, MLA_HEADS // ATT_HEADS + h)),
                  pl.BlockSpec((ATT_HEADS, nq, 1, ATT_BLK), lambda h, j: (h, 0, 0, 0))],
        out_specs=[pl.BlockSpec((ATT_HEADS, t, MLA_QK_PAD), lambda h, j: (h, 0, 0)),
                   pl.BlockSpec((ATT_HEADS, ATT_BLK, MLA_QK_PAD), lambda h, j: (h, j, 0)),
                   pl.BlockSpec((ATT_HEADS, ATT_BLK, MLA_V), lambda h, j: (h, j, 0))],
        out_shape=[jax.ShapeDtypeStruct((MLA_HEADS, t, MLA_QK_PAD), F32),
                   jax.ShapeDtypeStruct((MLA_HEADS, t, MLA_QK_PAD), F32),
                   jax.ShapeDtypeStruct((MLA_HEADS, t, MLA_V), F32)],
        scratch_shapes=[pltpu.VMEM((ATT_HEADS, nq, HALO, ATT_BLK), F32)],
        compiler_params=_cp("parallel", "arbitrary"),
    )(q3, k3, v3, o, dcat, lse)


def _ssd_prep(proj, bias128, alog128, *, name):
    t = proj.shape[0]
    nc = t // CHUNK

    def body(raw_ref, b_ref, al_ref, dt_ref, cs_ref, a_ref):
        xv = raw_ref[...] + b_ref[...]
        dt = jnp.maximum(xv, 0.0) + jnp.log(1.0 + jnp.exp(-jnp.abs(xv)))
        a = -jnp.exp(al_ref[...])
        adt = (dt * a).reshape(nc, CHUNK, LANES)
        li = lax.broadcasted_iota(jnp.int32, (nc, CHUNK, CHUNK), 1)
        si = lax.broadcasted_iota(jnp.int32, (nc, CHUNK, CHUNK), 2)
        tril = jnp.where(si <= li, 1.0, 0.0).astype(F32)
        cs = lax.dot_general(tril, adt, (((2,), (1,)), ((0,), (0,))), precision=HI, preferred_element_type=F32)
        dt_ref[...] = dt
        cs_ref[...] = cs.reshape(t, LANES)
        a_ref[...] = a

    blk = pl.BlockSpec((t, LANES), lambda i: (0, 0))
    row = pl.BlockSpec((1, LANES), lambda i: (0, 0))
    return pl.pallas_call(
        body, name=name, grid=(1,),
        in_specs=[pl.BlockSpec((t, LANES), lambda i: (0, OFF_DT // LANES)), row, row],
        out_specs=[blk, blk, row],
        out_shape=[jax.ShapeDtypeStruct((t, LANES), F32), jax.ShapeDtypeStruct((t, LANES), F32),
                   jax.ShapeDtypeStruct((1, LANES), F32)],
        compiler_params=_cp("arbitrary"),
    )(proj, bias128, alog128)


def _ssd_prep_bwd(ddt128, dadt128, proj, bias128, dt128, a128, dd_h, *, name):
    t = proj.shape[0]

    def body(ddt_ref, dadt_ref, raw_ref, b_ref, dt_ref, a_ref, dd_ref, draw_ref, db_ref, dal_ref, dds_ref):
        draw = ddt_ref[...] * _sigmoid(raw_ref[...] + b_ref[...])
        draw_ref[...] = draw.astype(BF16)
        db_ref[...] = jnp.sum(draw, axis=0, keepdims=True)
        dal_ref[...] = jnp.sum(dadt_ref[...] * dt_ref[...], axis=0, keepdims=True) * a_ref[...]
        dds_ref[...] = jnp.sum(dd_ref[...], axis=-1, keepdims=True)

    blk = pl.BlockSpec((t, LANES), lambda i: (0, 0))
    row = pl.BlockSpec((1, LANES), lambda i: (0, 0))
    return pl.pallas_call(
        body, name=name, grid=(1,),
        in_specs=[blk, blk, pl.BlockSpec((t, LANES), lambda i: (0, OFF_DT // LANES)), row, blk, row,
                  pl.BlockSpec((SSD_HEADS, SSD_P), lambda i: (0, 0))],
        out_specs=[blk, row, row, pl.BlockSpec((SSD_HEADS, 1), lambda i: (0, 0))],
        out_shape=[jax.ShapeDtypeStruct((t, LANES), BF16), jax.ShapeDtypeStruct((1, LANES), F32),
                   jax.ShapeDtypeStruct((1, LANES), F32), jax.ShapeDtypeStruct((SSD_HEADS, 1), F32)],
        compiler_params=_cp("arbitrary"),
    )(ddt128, dadt128, proj, bias128, dt128, a128, dd_h)


def _bdot(a, b, ca, cb, precision=None):
    return lax.dot_general(a, b, (((ca,), (cb,)), ((0,), (0,))), precision=precision, preferred_element_type=F32)


def _pieces(x):
    hi = x.astype(BF16)
    rest = x - hi.astype(F32)
    mid = rest.astype(BF16)
    return hi, mid, (rest - mid.astype(F32)).astype(BF16)


def _bdot_sum(a, b, ca, cb, split):
    other = (b if split == 0 else a).astype(BF16)
    out = None
    for piece in _pieces(a if split == 0 else b):
        term = _bdot(piece, other, ca, cb) if split == 0 else _bdot(other, piece, ca, cb)
        out = term if out is None else out + term
    return out


def _head_matrices():
    eye, zero = jnp.eye(SSD_P, dtype=F32), jnp.zeros((SSD_P, SSD_P), F32)
    pick = jnp.stack([jnp.concatenate([eye, zero], axis=0), jnp.concatenate([zero, eye], axis=0)])
    return pick, pick.transpose(0, 2, 1)


def _move(x, sel):
    selb = sel.astype(BF16)
    hi = x.astype(BF16)
    rest = x - hi.astype(F32)
    mid = rest.astype(BF16)
    low = (rest - mid.astype(F32)).astype(BF16)
    out = jnp.dot(hi, selb, preferred_element_type=F32)
    out = out + jnp.dot(mid, selb, preferred_element_type=F32)
    return out + jnp.dot(low, selb, preferred_element_type=F32)


def _pick_head(pair_ref, pick_ref, h):
    return _move(pair_ref[...], pick_ref[h % 2])


def _place_head(out_ref, val, place_ref, h):
    wide = _move(val, place_ref[h % 2])

    @pl.when(h % 2 == 0)
    def _():
        out_ref[...] = wide

    @pl.when(h % 2 == 1)
    def _():
        out_ref[...] += wide


def _ssd_common(x2, dt_ref, cs_ref, csr_ref, b_ref, c_ref, nc):
    x = x2.reshape(nc, CHUNK, SSD_P)
    dt = dt_ref[0].reshape(nc, CHUNK, SSD_P)
    cs = cs_ref[0].reshape(nc, CHUNK, SSD_P)
    csr = csr_ref[0]
    bm = b_ref[...].reshape(nc, CHUNK, SSD_N).astype(BF16)
    cm = c_ref[...].reshape(nc, CHUNK, SSD_N).astype(BF16)
    li = lax.broadcasted_iota(jnp.int32, (nc, CHUNK, CHUNK), 1)
    si = lax.broadcasted_iota(jnp.int32, (nc, CHUNK, CHUNK), 2)
    lmat = jnp.exp(jnp.where(si <= li, cs - csr, NEG))
    g = _bdot(cm, bm, 2, 2)
    cs_last = jnp.sum(jnp.where(li == CHUNK - 1, cs, 0.0), axis=1, keepdims=True)
    xdt = x * dt
    dec = jnp.exp(cs_last - cs)
    return x, dt, cs, bm, cm, li, si, lmat, g, cs_last, xdt, dec


def _ssd_fwd(xbc, dt_h, cs_h, cs_row, dskip_h, *, name):
    t = xbc.shape[0]
    nc = t // CHUNK
    hpg = SSD_HEADS // SSD_GROUPS
    pick, place = _head_matrices()

    def body(xs_ref, dt_ref, cs_ref, csr_ref, b_ref, c_ref, dk_ref, pick_ref, place_ref, y_ref, st_ref, sc_ref, cd_ref):
        h = pl.program_id(0)
        x, dt, cs, bm, cm, li, si, lmat, g, cs_last, xdt, dec = _ssd_common(_pick_head(xs_ref, pick_ref, h), dt_ref,
                                                                           cs_ref, csr_ref, b_ref, c_ref, nc)
        yd = _bdot((g * lmat).astype(BF16), xdt.astype(BF16), 2, 1)
        sc_ref[...] = _bdot(bm, (dec * xdt).astype(BF16), 1, 1)
        cd_ref[...] = jnp.exp(cs_last)

        def step(c, s):
            st_ref[0, c] = s
            return s * cd_ref[c] + sc_ref[c]

        lax.fori_loop(0, nc, step, jnp.zeros((SSD_N, SSD_P), F32))
        yo = _bdot(cm, st_ref[0].astype(BF16), 2, 1) * jnp.exp(cs)
        _place_head(y_ref, (yd + yo + dk_ref[0] * x).reshape(t, SSD_P), place_ref, h)

    head = pl.BlockSpec((1, t, SSD_P), lambda h: (h, 0, 0))
    pair = pl.BlockSpec((t, 2 * SSD_P), lambda h: (0, h // 2))
    nxb = D_SSM // SSD_N
    return pl.pallas_call(
        body, name=name, grid=(SSD_HEADS,),
        in_specs=[pair, head, head, pl.BlockSpec((1, nc, 1, CHUNK), lambda h: (h, 0, 0, 0)),
                  pl.BlockSpec((t, SSD_N), lambda h: (0, nxb + h // hpg)),
                  pl.BlockSpec((t, SSD_N), lambda h: (0, nxb + SSD_GROUPS + h // hpg)),
                  pl.BlockSpec((1, 1, SSD_P), lambda h: (h, 0, 0)),
                  pl.BlockSpec((2, 2 * SSD_P, SSD_P), lambda h: (0, 0, 0)),
                  pl.BlockSpec((2, SSD_P, 2 * SSD_P), lambda h: (0, 0, 0))],
        out_specs=[pair, pl.BlockSpec((1, nc, SSD_N, SSD_P), lambda h: (h, 0, 0, 0))],
        out_shape=[jax.ShapeDtypeStruct((t, D_SSM), F32),
                   jax.ShapeDtypeStruct((SSD_HEADS, nc, SSD_N, SSD_P), F32)],
        scratch_shapes=[pltpu.VMEM((nc, SSD_N, SSD_P), F32), pltpu.VMEM((nc, 1, SSD_P), F32)],
        compiler_params=_cp("arbitrary"),
    )(xbc, dt_h, cs_h, cs_row, xbc, xbc, dskip_h, pick, place)


def _ssd_bwd(xbc, dt_h, cs_h, cs_row, dskip_h, a_h, states, dy, *, name):
    t = xbc.shape[0]
    nc = t // CHUNK
    hpg = SSD_HEADS // SSD_GROUPS
    pick, place = _head_matrices()

    def body(xs_ref, dt_ref, cs_ref, csr_ref, b_ref, c_ref, dk_ref, a_ref, st_ref, dy_ref, pick_ref, place_ref,
             dxs_ref, ddt_ref, dadt_ref, db_ref, dc_ref, dd_ref, dsl_ref, dsc_ref, cd_ref):
        h = pl.program_id(0) * hpg + pl.program_id(1)
        x, dt, cs, bm, cm, li, si, lmat, g, cs_last, xdt, dec = _ssd_common(_pick_head(xs_ref, pick_ref, h), dt_ref,
                                                                           cs_ref, csr_ref, b_ref, c_ref, nc)
        dy = _pick_head(dy_ref, pick_ref, h).reshape(nc, CHUNK, SSD_P)
        dyb = dy.astype(BF16)
        xdtb = xdt.astype(BF16)
        sprev = st_ref[0]
        sprevb = sprev.astype(BF16)
        cdec = jnp.exp(cs_last)
        ecs = jnp.exp(cs)
        dw = (ecs * dy).astype(BF16)
        wmat = _bdot(cm, sprevb, 2, 1)
        dcs = jnp.sum(dy * ecs * wmat, axis=2, keepdims=True)
        dcm = _bdot(dw, sprevb, 2, 2)
        dsl_ref[...] = _bdot(cm, dw, 1, 1)
        cd_ref[...] = cdec

        def step(k, ds):
            c = nc - 1 - k
            dsc_ref[c] = ds
            return ds * cd_ref[c] + dsl_ref[c]

        lax.fori_loop(0, nc, step, jnp.zeros((SSD_N, SSD_P), F32))
        dsc = dsc_ref[...]
        dscb = dsc.astype(BF16)
        d_last = jnp.sum(jnp.sum(dsc * sprev, axis=1, keepdims=True) * cdec, axis=2, keepdims=True)
        z = dec * xdt
        dbm = _bdot(z.astype(BF16), dscb, 2, 2)
        dz = _bdot(bm, dscb, 2, 1)
        dxdt = dec * dz
        t2 = jnp.sum(dz * z, axis=2, keepdims=True)
        dcs = dcs - t2
        d_last = d_last + jnp.sum(t2, axis=1, keepdims=True)
        m = g * lmat
        mb = m.astype(BF16)
        dm = _bdot(dyb, xdtb, 2, 2)
        dxdt = dxdt + _bdot(mb, dyb, 1, 1)
        dseg = dm * m
        dcs = dcs + jnp.sum(dseg, axis=2, keepdims=True)
        ones = jnp.ones((nc, CHUNK, SSD_P), F32)
        dcs = dcs - _bdot_sum(dseg, ones, 1, 1, 0)
        dg = (dm * lmat).astype(BF16)
        dcm = dcm + _bdot(dg, bm, 2, 1)
        dbm = dbm + _bdot(dg, cm, 1, 1)
        dcs = dcs + jnp.where(li[:, :, :SSD_P] == CHUNK - 1, d_last, 0.0)
        triu = jnp.where(li <= si, 1.0, 0.0).astype(F32)
        dadt = _bdot_sum(triu, dcs, 2, 1, 1)
        dk = dk_ref[0]
        _place_head(dxs_ref, (dxdt * dt + dk * dy).reshape(t, SSD_P), place_ref, h)
        ddt = jnp.sum(dxdt * x, axis=2, keepdims=True) + dadt * a_ref[0]
        mine = lax.broadcasted_iota(jnp.int32, (t, LANES), 1) == h

        @pl.when(h == 0)
        def _():
            ddt_ref[...] = jnp.zeros_like(ddt_ref)
            dadt_ref[...] = jnp.zeros_like(dadt_ref)

        ddt_ref[...] += jnp.where(mine, jnp.max(ddt, axis=2, keepdims=True).reshape(t, 1), 0.0)
        dadt_ref[...] += jnp.where(mine, jnp.max(dadt, axis=2, keepdims=True).reshape(t, 1), 0.0)
        dd_ref[0] = jnp.sum(jnp.sum(dy * x, axis=1, keepdims=True), axis=0)

        @pl.when(pl.program_id(1) == 0)
        def _():
            db_ref[...] = jnp.zeros_like(db_ref)
            dc_ref[...] = jnp.zeros_like(dc_ref)

        db_ref[...] += dbm.reshape(t, SSD_N)
        dc_ref[...] += dcm.reshape(t, SSD_N)

    head = pl.BlockSpec((1, t, SSD_P), lambda gi, hi: (gi * hpg + hi, 0, 0))
    pair = pl.BlockSpec((t, 2 * SSD_P), lambda gi, hi: (0, (gi * hpg + hi) // 2))
    grp = pl.BlockSpec((t, SSD_N), lambda gi, hi: (0, gi))
    lane = pl.BlockSpec((1, 1, SSD_P), lambda gi, hi: (gi * hpg + hi, 0, 0))
    rows = pl.BlockSpec((t, LANES), lambda gi, hi: (0, 0))
    nxb = D_SSM // SSD_N
    dxs, ddt, dadt, db, dc, dd = pl.pallas_call(
        body, name=name, grid=(SSD_GROUPS, hpg),
        in_specs=[pair, head, head, pl.BlockSpec((1, nc, 1, CHUNK), lambda gi, hi: (gi * hpg + hi, 0, 0, 0)),
                  pl.BlockSpec((t, SSD_N), lambda gi, hi: (0, nxb + gi)),
                  pl.BlockSpec((t, SSD_N), lambda gi, hi: (0, nxb + SSD_GROUPS + gi)), lane, lane,
                  pl.BlockSpec((1, nc, SSD_N, SSD_P), lambda gi, hi: (gi * hpg + hi, 0, 0, 0)), pair,
                  pl.BlockSpec((2, 2 * SSD_P, SSD_P), lambda gi, hi: (0, 0, 0)),
                  pl.BlockSpec((2, SSD_P, 2 * SSD_P), lambda gi, hi: (0, 0, 0))],
        out_specs=[pair, rows, rows, grp, grp, lane],
        out_shape=[jax.ShapeDtypeStruct((t, D_SSM), F32)] + [jax.ShapeDtypeStruct((t, LANES), F32)] * 2
        + [jax.ShapeDtypeStruct((t, SSD_GROUPS * SSD_N), F32)] * 2
        + [jax.ShapeDtypeStruct((SSD_HEADS, 1, SSD_P), F32)],
        scratch_shapes=[pltpu.VMEM((nc, SSD_N, SSD_P), F32), pltpu.VMEM((nc, SSD_N, SSD_P), F32),
                        pltpu.VMEM((nc, 1, SSD_P), F32)],
        compiler_params=_cp("arbitrary", "arbitrary"),
    )(xbc, dt_h, cs_h, cs_row, xbc, xbc, dskip_h, a_h, states, dy, pick, place)
    return jnp.concatenate([dxs, db, dc], axis=1), ddt, dadt, dd


def _ssd_gate_fwd(y, proj, w, *, tr=256, name):
    t = y.shape[0]
    gw = D_SSM // SSD_GROUPS

    def body(y_ref, z_ref, w_ref, o_ref):
        v = y_ref[...] * _silu(z_ref[...])
        for gi in range(SSD_GROUPS):
            vg = v[:, gi * gw:(gi + 1) * gw]
            r = lax.rsqrt(jnp.mean(vg * vg, axis=-1, keepdims=True) + NORM_EPS)
            o_ref[:, gi * gw:(gi + 1) * gw] = (vg * r * w_ref[:, gi * gw:(gi + 1) * gw]).astype(BF16)

    blk = pl.BlockSpec((tr, D_SSM), lambda i: (i, 0))
    return pl.pallas_call(
        body, name=name, grid=(t // tr,), in_specs=[blk, blk, pl.BlockSpec((1, D_SSM), lambda i: (0, 0))],
        out_specs=blk, out_shape=jax.ShapeDtypeStruct((t, D_SSM), BF16), compiler_params=_cp("parallel"),
    )(y, proj, w)


def _ssd_gate_bwd(y, proj, w, dcat, *, tr=256, name):
    t = y.shape[0]
    gw = D_SSM // SSD_GROUPS

    def body(y_ref, z_ref, w_ref, d_ref, dy_ref, dz_ref, dw_ref):
        yv, zv, dv = y_ref[...], z_ref[...], d_ref[...].astype(F32)
        sz = _silu(zv)
        v = yv * sz

        @pl.when(pl.program_id(0) == 0)
        def _():
            dw_ref[...] = jnp.zeros_like(dw_ref)

        for gi in range(SSD_GROUPS):
            sl = slice(gi * gw, (gi + 1) * gw)
            vg, dg = v[:, sl], dv[:, sl]
            r = lax.rsqrt(jnp.mean(vg * vg, axis=-1, keepdims=True) + NORM_EPS)
            vh = vg * r
            gg = dg * w_ref[:, sl]
            dvg = r * (gg - vh * jnp.mean(gg * vh, axis=-1, keepdims=True))
            dy_ref[:, sl] = dvg * sz[:, sl]
            dz_ref[:, sl] = (dvg * yv[:, sl] * _dsilu(zv[:, sl])).astype(BF16)
            dw_ref[:, sl] += jnp.sum(dg * vh, axis=0, keepdims=True)

    blk = pl.BlockSpec((tr, D_SSM), lambda i: (i, 0))
    row = pl.BlockSpec((1, D_SSM), lambda i: (0, 0))
    return pl.pallas_call(
        body, name=name, grid=(t // tr,), in_specs=[blk, blk, row, blk], out_specs=[blk, blk, row],
        out_shape=[jax.ShapeDtypeStruct((t, D_SSM), F32), jax.ShapeDtypeStruct((t, D_SSM), BF16),
                   jax.ShapeDtypeStruct((1, D_SSM), F32)],
        compiler_params=_cp("arbitrary"),
    )(y, proj, w, dcat)


def _pad_lanes(v):
    return jnp.pad(v, ((0, 0), (0, LANES - v.shape[1])))


def _per_head(v128, t):
    return jnp.broadcast_to(v128[:, :SSD_HEADS].T[:, :, None], (SSD_HEADS, t, SSD_P))


def _ssd_forward(proj, conv_w, conv_b, dt_bias, a_log, d_skip, ssd_norm_w):
    t = proj.shape[0]
    nc = t // CHUNK
    xbc = _conv_act_fwd(proj, conv_w, conv_b, kw=SSD_CONV, glu=False, tc=512, coff=OFF_XBC // 512,
                        ncols=SSD_CONV_DIM, out_dtype=F32, name="ssd_conv_fwd")
    bias128, alog128 = _pad_lanes(dt_bias), _pad_lanes(a_log)
    dt128, cs128, a128 = _ssd_prep(proj, bias128, alog128, name="ssd_prep")
    dt_h, cs_h = _per_head(dt128, t), _per_head(cs128, t)
    cs_row = cs128[:, :SSD_HEADS].T.reshape(SSD_HEADS, nc, 1, CHUNK)
    dskip_h = jnp.broadcast_to(d_skip[0][:, None, None], (SSD_HEADS, 1, SSD_P))
    a_h = jnp.broadcast_to(a128[0, :SSD_HEADS][:, None, None], (SSD_HEADS, 1, SSD_P))
    y, states = _ssd_fwd(xbc, dt_h, cs_h, cs_row, dskip_h, name="ssd_scan_fwd")
    y_ssd = _ssd_gate_fwd(y, proj, ssd_norm_w, name="ssd_gate_fwd")
    saved = (proj, conv_w, conv_b, ssd_norm_w, bias128, dt128, a128, dt_h, cs_h, cs_row, xbc, dskip_h, a_h, states, y)
    return y_ssd, saved


def _ssd_backward(saved, dcat):
    proj, conv_w, conv_b, ssd_norm_w, bias128, dt128, a128, dt_h, cs_h, cs_row, xbc, dskip_h, a_h, states, y = saved
    dy, dz, d_norm_w = _ssd_gate_bwd(y, proj, ssd_norm_w, dcat, name="ssd_gate_bwd")
    dxc, ddt128, dadt128, dd_h = _ssd_bwd(xbc, dt_h, cs_h, cs_row, dskip_h, a_h, states, dy, name="ssd_scan_bwd")
    dxbc, d_conv_w, d_conv_b = _conv_act_bwd(proj, conv_w, conv_b, dxc, kw=SSD_CONV, glu=False, tc=512,
                                             coff=OFF_XBC // 512, ncols=SSD_CONV_DIM, name="ssd_conv_bwd")
    d_raw, d_bias, d_alog, d_dskip = _ssd_prep_bwd(ddt128, dadt128, proj, bias128, dt128, a128,
                                                   dd_h.reshape(SSD_HEADS, SSD_P), name="ssd_prep_bwd")
    return (dz, dxbc, d_raw, d_norm_w, d_conv_w, d_conv_b, d_bias[:, :SSD_HEADS], d_alog[:, :SSD_HEADS],
            d_dskip.reshape(1, SSD_HEADS))


def _rope_tables(positions):
    inv_freq = ROPE_THETA ** (-jnp.arange(0, MLA_ROPE, 2, dtype=F32) / MLA_ROPE)
    ang = positions[0].astype(F32)[:, None] * inv_freq
    cos, sin = jnp.cos(ang), jnp.sin(ang)
    z = jnp.zeros_like(cos)
    return jnp.stack([jnp.concatenate([cos, cos, z, z], axis=1), jnp.concatenate([-sin, z, z, z], axis=1),
                      jnp.concatenate([z, sin, z, z], axis=1)])


def _latent_norms(proj, q_w, kv_w, *, tr=256, name):
    t = proj.shape[0]

    def body(q_ref, kv_ref, qw_ref, kvw_ref, qo_ref, kvo_ref):
        for x_ref, w_ref, o_ref in ((q_ref, qw_ref, qo_ref), (kv_ref, kvw_ref, kvo_ref)):
            xv = x_ref[...]
            r = lax.rsqrt(jnp.mean(xv * xv, axis=-1, keepdims=True) + NORM_EPS)
            o_ref[...] = (xv * r * w_ref[...]).astype(BF16)

    return pl.pallas_call(
        body, name=name, grid=(t // tr,),
        in_specs=[pl.BlockSpec((tr, MLA_Q_RANK), lambda i: (i, OFF_QA // MLA_Q_RANK)),
                  pl.BlockSpec((tr, MLA_KV_RANK), lambda i: (i, OFF_CKV // MLA_KV_RANK)),
                  pl.BlockSpec((1, MLA_Q_RANK), lambda i: (0, 0)), pl.BlockSpec((1, MLA_KV_RANK), lambda i: (0, 0))],
        out_specs=[pl.BlockSpec((tr, MLA_Q_RANK), lambda i: (i, 0)), pl.BlockSpec((tr, MLA_KV_RANK), lambda i: (i, 0))],
        out_shape=[jax.ShapeDtypeStruct((t, MLA_Q_RANK), BF16), jax.ShapeDtypeStruct((t, MLA_KV_RANK), BF16)],
        compiler_params=_cp("parallel"),
    )(proj, proj, q_w, kv_w)


def _latent_norms_bwd(proj, q_w, kv_w, dqn, dkvn, *, tr=256, name):
    t = proj.shape[0]

    def body(q_ref, kv_ref, qw_ref, kvw_ref, dq_ref, dkv_ref, dqa_ref, dqw_ref, dckv_ref, dkvw_ref):
        first = pl.program_id(0) == 0
        for x_ref, w_ref, dy_ref, dx_ref, dw_ref in ((q_ref, qw_ref, dq_ref, dqa_ref, dqw_ref),
                                                     (kv_ref, kvw_ref, dkv_ref, dckv_ref, dkvw_ref)):
            xv, dyv = x_ref[...], dy_ref[...]
            r = lax.rsqrt(jnp.mean(xv * xv, axis=-1, keepdims=True) + NORM_EPS)
            xh = xv * r
            g = dyv * w_ref[...]
            dx_ref[...] = (r * (g - xh * jnp.mean(g * xh, axis=-1, keepdims=True))).astype(BF16)

            @pl.when(first)
            def _(dw_ref=dw_ref):
                dw_ref[...] = jnp.zeros_like(dw_ref)

            dw_ref[...] += jnp.sum(dyv * xh, axis=0, keepdims=True)

    def rows(width, cblk=0):
        return pl.BlockSpec((tr, width), lambda i: (i, cblk))

    def row(width):
        return pl.BlockSpec((1, width), lambda i: (0, 0))

    return pl.pallas_call(
        body, name=name, grid=(t // tr,),
        in_specs=[rows(MLA_Q_RANK, OFF_QA // MLA_Q_RANK), rows(MLA_KV_RANK, OFF_CKV // MLA_KV_RANK), row(MLA_Q_RANK),
                  row(MLA_KV_RANK), rows(MLA_Q_RANK), rows(MLA_KV_RANK)],
        out_specs=[rows(MLA_Q_RANK), row(MLA_Q_RANK), rows(MLA_KV_RANK), row(MLA_KV_RANK)],
        out_shape=[jax.ShapeDtypeStruct((t, MLA_Q_RANK), BF16), jax.ShapeDtypeStruct((1, MLA_Q_RANK), F32),
                   jax.ShapeDtypeStruct((t, MLA_KV_RANK), BF16), jax.ShapeDtypeStruct((1, MLA_KV_RANK), F32)],
        compiler_params=_cp("arbitrary"),
    )(proj, proj, q_w, kv_w, dqn, dkvn)


def _mla_forward(proj, tabs, q_a_norm_w, wq_pad, kv_a_norm_w, wkv):
    qn, kvn = _latent_norms(proj, q_a_norm_w, kv_a_norm_w, name="latent_norms")
    q = _matmul(qn, wq_pad, name="q_b_proj")
    kv = _matmul(kvn, wkv, name="kv_b_proj")
    q3, k3, v3, vt4 = _mla_prep(q, kv, proj, tabs, name="mla_prep")
    o, lse = _attn_fwd(q3, k3, vt4, name="attn_fwd")
    return o, (proj, tabs, q_a_norm_w, wq_pad, kv_a_norm_w, wkv, qn, kvn, q3, k3, v3, o, lse)


def _mla_backward(saved, dcat):
    proj, tabs, q_a_norm_w, wq_pad, kv_a_norm_w, wkv, qn, kvn, q3, k3, v3, o, lse = saved
    dq3, dk3, dv3 = _attn_bwd(q3, k3, v3, o, dcat, lse, name="attn_bwd")
    dq, dkv, dkr = _mla_unprep(dq3, dk3, dv3, tabs, name="mla_unprep")
    d_wq = _matmul(qn, dq, ta=True, out_dtype=BF16, name="d_w_q_b")
    dqn = _matmul(dq, wq_pad, tb=True, name="d_qn")
    d_wkv = _matmul(kvn, dkv, ta=True, out_dtype=BF16, name="d_w_kv_b")
    dkvn = _matmul(dkv, wkv, tb=True, name="d_kvn")
    dq_a, d_qnw, dckv, d_kvnw = _latent_norms_bwd(proj, q_a_norm_w, kv_a_norm_w, dqn, dkvn, name="latent_norms_bwd")
    return dq_a, dckv, dkr, d_wq, d_wkv, d_qnw, d_kvnw


def _pad_w_q(w):
    r = w.shape[0]
    w3 = w.reshape(r, MLA_HEADS, MLA_NOPE + MLA_ROPE)
    return jnp.pad(w3, ((0, 0), (0, 0), (0, MLA_QK_PAD - MLA_NOPE - MLA_ROPE))).reshape(r, MLA_HEADS * MLA_QK_PAD)


def _unpad_w_q(w):
    r = w.shape[0]
    return w.reshape(r, MLA_HEADS, MLA_QK_PAD)[:, :, :MLA_NOPE + MLA_ROPE].reshape(r, MLA_HEADS * (MLA_NOPE + MLA_ROPE))


W_IN_SEGMENTS = ((0, D_SSM + SSD_CONV_DIM, 0), (D_SSM + SSD_CONV_DIM, D_SSM + SSD_CONV_DIM + SSD_HEADS, OFF_DT),
                 (D_SSM + SSD_CONV_DIM + SSD_HEADS, D_IN - MLA_ROPE, OFF_QA), (D_IN - MLA_ROPE, D_IN, OFF_KR))


def _pad_w_in_shards(g):
    n = g.shape[2]
    pieces, at = [], 0
    for lo, hi, start in sorted(W_IN_SEGMENTS, key=lambda seg: seg[2]):
        if start > at:
            pieces.append(jnp.zeros((g.shape[1], start - at), g.dtype))
        for j in range(N_DEV):
            a, b = max(lo, j * n), min(hi, (j + 1) * n)
            if a < b:
                pieces.append(g[j][:, a - j * n:b - j * n])
        at = start + hi - lo
    pieces.append(jnp.zeros((g.shape[1], D_IN_PAD - at), g.dtype))
    return jnp.concatenate(pieces, axis=1)


def _unpad_w_in_shards(w):
    n = D_IN // N_DEV
    shards = []
    for j in range(N_DEV):
        pieces = []
        for lo, hi, start in W_IN_SEGMENTS:
            a, b = max(lo, j * n), min(hi, (j + 1) * n)
            if a < b:
                pieces.append(w[:, start + a - lo:start + b - lo])
        shards.append(jnp.concatenate(pieces, axis=1) if len(pieces) > 1 else pieces[0])
    return jnp.stack(shards)


WEIGHTS = ['mix_norm_w', 'w_in', 'conv_w', 'conv_b', 'dt_bias', 'a_log', 'd_skip', 'ssd_norm_w', 'q_a_norm_w', 'w_q_b',
           'kv_a_norm_w', 'w_kv_b', 'w_out', 'ffn_norm_w', 'w_ffn_up', 'ffn_conv_w', 'ffn_conv_b', 'w_ffn_down',
           'ple_norm_w', 'w_ple_gate', 'b_ple_gate', 'w_ple_proj', 'ple_post_norm_w', 'final_norm_w']
BIG = ['w_in', 'w_q_b', 'w_kv_b', 'w_out', 'w_ffn_up', 'w_ffn_down', 'w_ple_gate', 'w_ple_proj']
COL_SHARDED = ('w_in', 'w_q_b', 'w_kv_b', 'w_ffn_up', 'w_ple_proj')
CONV = ['conv_w', 'ffn_conv_w']
REPL = [n for n in WEIGHTS if n not in BIG and n not in CONV]
FFN_INV = tuple(int(i) for i in np.argsort(FFN_PERM))


def _cat_cols(g):
    return jnp.concatenate([g[j] for j in range(N_DEV)], axis=1)


def _split_cols(w):
    n = w.shape[1] // N_DEV
    return jnp.stack([w[:, j * n:(j + 1) * n] for j in range(N_DEV)])


def _interleave(v):
    r = v.shape[0]
    return v.reshape(r, N_DEV, FFN_TC)[:, jnp.array(FFN_PERM)].reshape(r, N_DEV * FFN_TC)


def _deinterleave(v):
    r = v.shape[0]
    return v.reshape(r, N_DEV, FFN_TC)[:, jnp.array(FFN_INV)].reshape(r, N_DEV * FFN_TC)


def _assemble_weights(g):
    layout = {
        'w_in': _pad_w_in_shards,
        'w_q_b': lambda v: _pad_w_q(_cat_cols(v)),
        'w_kv_b': _cat_cols,
        'w_out': lambda v: v.reshape(D_MODEL, D_MODEL),
        'w_ffn_up': lambda v: v,
        'w_ffn_down': lambda v: v.reshape(D_FF, D_MODEL),
        'w_ple_gate': lambda v: v.reshape(D_MODEL, D_MODEL),
        'w_ple_proj': _cat_cols,
        'conv_w': _cat_cols,
        'ffn_conv_w': lambda v: _interleave(_cat_cols(v)),
    }
    return {n: layout[n](v) for n, v in g.items()}


WEIGHT_GROUPS = {'a': ['w_in', 'w_q_b', 'w_kv_b', 'conv_w'], 'b': ['w_out', 'w_ffn_up', 'ffn_conv_w'],
                 'c': ['w_ffn_down', 'w_ple_gate', 'w_ple_proj']}
GRAD_GROUPS = {'p': ['w_ple_proj', 'w_ple_gate', 'w_ffn_down'], 'r': ['w_ffn_up'], 's': ['w_out'],
               't': ['w_q_b', 'w_kv_b', 'w_in']}


def _ffn_perm(j):
    return (j % 2) * (N_DEV // 2) + j // 2


def _local_step(x, p, tabs, get_w, s, target, emit, relay, settle):
    t = x.shape[0]
    s = dict(s)
    half = D_MODEL // 2
    up_cols = 2 * D_FF
    ffn_conv_b = _interleave(s['ffn_conv_b'])
    w = dict(get_w('a', None))
    h = _rmsnorm_fwd(x, s['mix_norm_w'], width=D_MODEL, name="mix_norm")
    proj = _matmul(h, w['w_in'], name="in_proj")
    y_ssd, ssd_saved = _ssd_forward(proj, w['conv_w'], s['conv_b'], s['dt_bias'], s['a_log'], s['d_skip'],
                                    s['ssd_norm_w'])
    o, mla_saved = _mla_forward(proj, tabs, s['q_a_norm_w'], w['w_q_b'], s['kv_a_norm_w'], w['w_kv_b'])
    tk_o, tn_o = _tile(half, MM_TK), _tile(D_MODEL, MM_TILE)
    w.update(get_w('b', o))
    x1 = _matmul(y_ssd, w['w_out'], add=x, mnk=(t, D_MODEL, half), name="out_proj_ssd")
    x1 = _matmul(o, w['w_out'], add=x1, mnk=(t, D_MODEL, half), name="out_proj_mla",
                 b_spec=pl.BlockSpec((tk_o, tn_o), lambda i, j, kk: (kk + half // tk_o, j)))
    hf = _rmsnorm_fwd(x1, s['ffn_norm_w'], width=D_MODEL, name="ffn_norm")
    tk_u = _tile(D_MODEL, MM_TK)
    u = _matmul(hf, w['w_ffn_up'], mnk=(t, up_cols, D_MODEL), tn=FFN_TC, name="ffn_up",
                b_spec=pl.BlockSpec((1, tk_u, FFN_TC), lambda i, j, kk: (_ffn_perm(j), kk, 0)))
    act = _conv_act_fwd(u, w['ffn_conv_w'], ffn_conv_b, kw=FFN_CONV, glu=True, tc=2 * FFN_TC, coff=0, ncols=up_cols,
                        out_dtype=BF16, name="ffn_act")
    w.update(get_w('c', act))
    x2 = _matmul(act, w['w_ffn_down'], add=x1, name="ffn_down")
    hp = _rmsnorm_fwd(x2, s['ple_norm_w'], width=D_MODEL, name="ple_norm")
    gl = _matmul(hp, w['w_ple_gate'], bias=s['b_ple_gate'], name="ple_gate")
    pe = _matmul(p, w['w_ple_proj'], name="ple_proj")
    loss, dx3, d_final, dgl, d_bgate, dpe, d_post = _ple_loss(x2, gl, pe, s['ple_post_norm_w'], s['final_norm_w'],
                                                              target, name="ple_loss")
    d_wproj = _matmul(p, dpe, ta=True, out_dtype=BF16, name="d_w_ple_proj")
    d_wgate = _matmul(hp, dgl, ta=True, out_dtype=BF16, name="d_w_ple_gate")
    dhp = _matmul(dgl, w['w_ple_gate'], tb=True, name="d_ple_normed")
    dx2, d_plenorm, dx2b = _rmsnorm_bwd(x2, s['ple_norm_w'], dhp, dx3, width=D_MODEL, also_bf16=True,
                                        name="ple_norm_bwd")
    dact = _matmul(dx2b, w['w_ffn_down'], tb=True, name="d_ffn_act")
    d_wdown = _matmul(act, dx2b, ta=True, out_dtype=BF16, name="d_w_ffn_down")
    zz = emit('p', {'w_ple_proj': _split_cols(d_wproj), 'w_ple_gate': d_wgate.reshape(N_DEV, D_MODEL // N_DEV, D_MODEL),
                    'w_ffn_down': d_wdown.reshape(N_DEV, D_FF // N_DEV, D_MODEL)})
    du, d_fconv_w, d_fconv_b = _conv_act_bwd(u, w['ffn_conv_w'], ffn_conv_b + zz, dact, kw=FFN_CONV, glu=True,
                                             tc=2 * FFN_TC, coff=0, ncols=up_cols, name="ffn_act_bwd")
    zz = zz + relay('p', du)
    tm_u = _tile(D_MODEL, MM_TILE)
    d_wup = _matmul(hf, du, ta=True, out_dtype=BF16, mnk=(D_MODEL, up_cols, t), tn=FFN_TC, name="d_w_ffn_up",
                    o_spec=pl.BlockSpec((1, tm_u, FFN_TC), lambda i, j, kk: (_ffn_perm(j), i, 0)),
                    o_shape=(N_DEV, D_MODEL, FFN_TC))
    zz = zz + emit('r', {'w_ffn_up': d_wup})
    zero_row = jnp.zeros((1, D_MODEL), F32)
    dhf = _matmul(du, w['w_ffn_up'], tb=True, mnk=(t, D_MODEL, up_cols), tm=t, tk=FFN_TC, name="d_ffn_normed",
                  bias=zero_row + zz,
                  b_spec=pl.BlockSpec((1, tn_o, FFN_TC), lambda i, j, kk: (_ffn_perm(kk), j, 0)))
    zz = zz + relay('r', dhf) + settle('p')
    dx1, d_ffnnorm, dx1b = _rmsnorm_bwd(x1, s['ffn_norm_w'] + zz, dhf, dx2, width=D_MODEL, also_bf16=True,
                                        name="ffn_norm_bwd")
    dcat = _matmul(dx1b, w['w_out'], tb=True, name="d_mixed")
    d_wout = jnp.concatenate([_matmul(y_ssd, dx1b, ta=True, out_dtype=BF16, name="d_w_out_ssd"),
                              _matmul(o, dx1b, ta=True, out_dtype=BF16, name="d_w_out_mla")], axis=0)
    zz = zz + emit('s', {'w_out': d_wout.reshape(N_DEV, D_MODEL // N_DEV, D_MODEL)})
    ssd_saved = ssd_saved[:3] + (ssd_saved[3] + zz,) + ssd_saved[4:]
    dz, dxbc, d_raw, d_ssdnorm, d_conv_w, d_conv_b, d_dtb, d_alog, d_dskip = _ssd_backward(ssd_saved, dcat)
    zz = zz + relay('s', dz)
    mla_saved = mla_saved[:-1] + (mla_saved[-1] + zz,)
    dq_a, dckv, dkr, d_wq, d_wkv, d_qnorm, d_kvnorm = _mla_backward(mla_saved, dcat)
    d_raw = (d_raw + settle('r')).astype(BF16)
    dproj = jnp.concatenate([dz, dxbc, dq_a, dckv, dkr, d_raw], axis=1)
    d_win = _matmul(h, dproj, ta=True, out_dtype=BF16, name="d_w_in")
    zz = emit('t', {'w_in': _unpad_w_in_shards(d_win), 'w_q_b': _split_cols(_unpad_w_q(d_wq)),
                    'w_kv_b': _split_cols(d_wkv)}) + settle('s')
    dh = _matmul(dproj, w['w_in'], tb=True, bias=zero_row + zz, name="d_in_normed")
    zz = relay('t', dh)
    dx, d_mixnorm = _rmsnorm_bwd(x, s['mix_norm_w'] + zz, dh, dx1, width=D_MODEL, name="mix_norm_bwd")
    conv = {'conv_w': d_conv_w, 'ffn_conv_w': _deinterleave(d_fconv_w)}
    vec = {
        'mix_norm_w': d_mixnorm, 'conv_b': d_conv_b, 'dt_bias': d_dtb, 'a_log': d_alog, 'd_skip': d_dskip,
        'ssd_norm_w': d_ssdnorm, 'q_a_norm_w': d_qnorm, 'kv_a_norm_w': d_kvnorm, 'ffn_norm_w': d_ffnnorm,
        'ffn_conv_b': _deinterleave(d_fconv_b), 'ple_norm_w': d_plenorm, 'b_ple_gate': d_bgate,
        'ple_post_norm_w': d_post, 'final_norm_w': d_final,
    }
    return loss, dx, conv, vec


MESH = pl.DeviceIdType.MESH
FLIPS = ((0, 0, 1), (1, 0, 0), (0, 1, 0), (1, 1, 0), (1, 0, 1), (0, 1, 1), (1, 1, 1))


def _exchange(items, *, gather, name):
    n = len(items)

    def body(*refs):
        ins, outs = refs[:n], refs[n:2 * n]
        send_sems, recv_sems, local_sems = refs[2 * n:]
        x, y, c = lax.axis_index("x"), lax.axis_index("y"), lax.axis_index("c")
        me = 4 * x + 2 * y + c
        peers = [(jnp.where(fx, 1 - x, x), jnp.where(fy, 1 - y, y), jnp.where(fc, 1 - c, c)) for fx, fy, fc in FLIPS]
        slot = [4 * px + 2 * py + pc for px, py, pc in peers]
        local, sends = [], []
        for wi in range(n):
            cp = pltpu.make_async_copy(ins[wi] if gather else ins[wi].at[me], outs[wi].at[me], local_sems.at[wi])
            cp.start()
            local.append(cp)
            for k, peer in enumerate(peers):
                cp = pltpu.make_async_remote_copy(
                    src_ref=ins[wi] if gather else ins[wi].at[slot[k]], dst_ref=outs[wi].at[me],
                    send_sem=send_sems.at[k, wi], recv_sem=recv_sems.at[k, wi], device_id=peer, device_id_type=MESH)
                cp.start()
                sends.append(cp)
        for wi in range(n):
            for k, peer in enumerate(peers):
                pltpu.make_async_remote_copy(
                    src_ref=outs[wi].at[slot[k]], dst_ref=outs[wi].at[slot[k]], send_sem=send_sems.at[k, wi],
                    recv_sem=recv_sems.at[k, wi], device_id=peer, device_id_type=MESH).wait_recv()
        for cp in sends:
            cp.wait_send()
        for cp in local:
            cp.wait()

    hbm = pl.BlockSpec(memory_space=pltpu.HBM)
    out_shape = [jax.ShapeDtypeStruct(((N_DEV,) + v.shape) if gather else v.shape, v.dtype) for v in items]
    return pl.pallas_call(
        body, name=name, in_specs=[hbm] * n, out_specs=[hbm] * n, out_shape=out_shape,
        scratch_shapes=[pltpu.SemaphoreType.DMA((len(FLIPS), n)), pltpu.SemaphoreType.DMA((len(FLIPS), n)),
                        pltpu.SemaphoreType.DMA((n,))],
    )(*items)


HBM_SPEC = pl.BlockSpec(memory_space=pltpu.HBM)
SEM_SPEC = pl.BlockSpec(memory_space=pltpu.SEMAPHORE)
EFFECT = pltpu.SideEffectType.DATAFLOW_SIDE_EFFECTING


def _split_start(bufs, ncopies, plan, *, name):
    nb = len(bufs)

    def body(*refs):
        send_sems, recv_sems, token = refs[nb], refs[nb + 1], refs[2 * nb + 2]
        for i, (src, dst, peer, _) in enumerate(plan(refs[:nb])):
            pltpu.make_async_remote_copy(src_ref=src, dst_ref=dst, send_sem=send_sems.at[i], recv_sem=recv_sems.at[i],
                                         device_id=peer, device_id_type=MESH).start()
        token[...] = jnp.zeros_like(token)

    res = pl.pallas_call(
        body, name=name, in_specs=[HBM_SPEC] * nb,
        out_specs=[SEM_SPEC, SEM_SPEC] + [HBM_SPEC] * nb + [pl.BlockSpec(memory_space=pltpu.VMEM)],
        out_shape=[pltpu.SemaphoreType.DMA((ncopies,)), pltpu.SemaphoreType.DMA((ncopies,))]
        + [pltpu.HBM(v.shape, v.dtype) for v in bufs] + [jax.ShapeDtypeStruct((HALO, LANES), F32)],
        input_output_aliases={i: 2 + i for i in range(nb)},
        compiler_params=pltpu.CompilerParams(has_side_effects=EFFECT),
    )(*[pltpu.with_memory_space_constraint(v, pltpu.HBM) for v in bufs])
    return (res[0], res[1], list(res[2:2 + nb])), res[2 + nb]


def _split_wait(started, after, plan, local_plan, *, name):
    send_sems, recv_sems, bufs = started
    nb = len(bufs)
    nlocal = len(local_plan(bufs))

    def body(*refs):
        send_sems, recv_sems = refs[nb], refs[nb + 1]
        local_sems = refs[2 * nb + 3]
        local = []
        for j, (src, dst) in enumerate(local_plan(refs[:nb])):
            cp = pltpu.make_async_copy(src, dst, local_sems.at[j])
            cp.start()
            local.append(cp)
        for i, (src, _, peer, incoming) in enumerate(plan(refs[:nb])):
            cp = pltpu.make_async_remote_copy(src_ref=src, dst_ref=incoming, send_sem=send_sems.at[i],
                                              recv_sem=recv_sems.at[i], device_id=peer, device_id_type=MESH)
            cp.wait_send()
            cp.wait_recv()
        for cp in local:
            cp.wait()

    res = pl.pallas_call(
        body, name=name, in_specs=[HBM_SPEC] * nb + [SEM_SPEC, SEM_SPEC, pl.BlockSpec(memory_space=pl.ANY)],
        out_specs=[HBM_SPEC] * nb, out_shape=[pltpu.HBM(v.shape, v.dtype) for v in bufs],
        input_output_aliases={i: i for i in range(nb)},
        scratch_shapes=[pltpu.SemaphoreType.DMA((max(nlocal, 1),))],
        compiler_params=pltpu.CompilerParams(has_side_effects=EFFECT),
    )(*bufs, send_sems, recv_sems, after)
    return list(res)


def _hold(values, after, *, name):
    n = len(values)

    def body(*refs):
        del refs

    return list(pl.pallas_call(
        body, name=name, in_specs=[HBM_SPEC] * n + [pl.BlockSpec(memory_space=pl.ANY)], out_specs=[HBM_SPEC] * n,
        out_shape=[pltpu.HBM(v.shape, v.dtype) for v in values], input_output_aliases={i: i for i in range(n)},
    )(*values, after))


def _place():
    x, y, c = lax.axis_index("x"), lax.axis_index("y"), lax.axis_index("c")
    others = [((1 - x, y, c), 2 * (1 - x) + y), ((x, 1 - y, c), 2 * x + 1 - y), ((1 - x, 1 - y, c), 2 * (1 - x) + 1 - y)]
    return 4 * x + 2 * y + c, 2 * x + y, c, (x, y, 1 - c), others


def _gather1_plan(n):
    def plan(refs):
        me, _, _, sibling, others = _place()
        out = []
        for wi in range(n):
            item, land = refs[wi], refs[n + wi]
            out.append((item, land.at[me], sibling, land.at[me + 1 - 2 * lax.axis_index("c")]))
            for peer, chip in others:
                out.append((item, land.at[me], peer, land.at[2 * chip + lax.axis_index("c")]))
        return out

    return plan


def _gather1_local(n):
    def plan(refs):
        me = _place()[0]
        return [(refs[wi], refs[n + wi].at[me]) for wi in range(n)]

    return plan


def _gather2_plan(n):
    def plan(refs):
        _, _, c, sibling, others = _place()
        out = []
        for wi in range(n):
            land = refs[wi]
            for _, chip in others:
                out.append((land.at[2 * chip + c], land.at[2 * chip + c], sibling, land.at[2 * chip + 1 - c]))
        return out

    return plan


def _gather_start(items, *, name):
    lands = [lax.empty((N_DEV,) + v.shape, v.dtype) for v in items]
    return _split_start(items + lands, 4 * len(items), _gather1_plan(len(items)), name=name)


def _gather_forward(started, after, *, name):
    n = len(started[2]) // 2
    bufs = _split_wait(started, after, _gather1_plan(n), _gather1_local(n), name=name + "_wait")
    return _split_start(bufs[n:], 3 * n, _gather2_plan(n), name=name + "_start")


def _gather_finish(started, after, *, name):
    n = len(started[2])
    return _split_wait(started, after, _gather2_plan(n), lambda refs: [], name=name)


def _handshake(peers):
    barrier = pltpu.get_barrier_semaphore()
    for peer in peers:
        pl.semaphore_signal(barrier, inc=1, device_id=peer, device_id_type=MESH)
    pl.semaphore_wait(barrier, len(peers))


def _remote(src, dst, send_sem, recv_sem, peer):
    return pltpu.make_async_remote_copy(src_ref=src, dst_ref=dst, send_sem=send_sem, recv_sem=recv_sem, device_id=peer,
                                        device_id_type=MESH)


def _sequencer_gather(items, *, collective_id, name):
    n = len(items)
    srcs = [jax.new_ref(v, memory_space=pltpu.MemorySpace.HBM) for v in items]
    lands = [jax.empty_ref(jax.ShapeDtypeStruct((N_DEV,) + v.shape, v.dtype), memory_space=pltpu.MemorySpace.HBM)
             for v in items]
    dma = pltpu.SemaphoreType.DMA

    @pl.kernel(mesh=plsc.ScalarSubcoreMesh(axis_name="sequencer", num_cores=1), name=name,
               scratch_types=(dma((4 * n,)), dma((4 * n,)), dma((3 * n,)), dma((3 * n,)), dma((n,))),
               compiler_params=pltpu.CompilerParams(collective_id=collective_id))
    def launch(send1, recv1, send2, recv2, local_sems):
        _, _, _, sibling, others = _place()
        _handshake([sibling] + [peer for peer, _ in others])
        hop1 = _gather1_plan(n)(srcs + lands)
        hop2 = _gather2_plan(n)(lands)
        local = [pltpu.make_async_copy(src, dst, local_sems.at[j])
                 for j, (src, dst) in enumerate(_gather1_local(n)(srcs + lands))]
        for cp in local:
            cp.start()
        for i, (src, dst, peer, _) in enumerate(hop1):
            _remote(src, dst, send1.at[i], recv1.at[i], peer).start()
        for wi in range(n):
            for j in range(3):
                i1, i2 = 4 * wi + 1 + j, 3 * wi + j
                src, _, peer, incoming = hop1[i1]
                _remote(src, incoming, send1.at[i1], recv1.at[i1], peer).wait_recv()
                src, dst, peer, _ = hop2[i2]
                _remote(src, dst, send2.at[i2], recv2.at[i2], peer).start()
        for wi in range(n):
            src, _, peer, incoming = hop1[4 * wi]
            _remote(src, incoming, send1.at[4 * wi], recv1.at[4 * wi], peer).wait_recv()
        for i, (src, _, peer, incoming) in enumerate(hop2):
            cp = _remote(src, incoming, send2.at[i], recv2.at[i], peer)
            cp.wait_send()
            cp.wait_recv()
        for i, (src, dst, peer, _) in enumerate(hop1):
            _remote(src, dst, send1.at[i], recv1.at[i], peer).wait_send()
        for cp in local:
            cp.wait()

    launch()
    return [land[...] for land in lands]


def _sequencer_exchange(sources, land_shapes, ncopies, plan, local_plan, peers, *, collective_id, name):
    srcs = [jax.new_ref(v, memory_space=pltpu.MemorySpace.HBM) for v in sources]
    lands = [jax.empty_ref(s, memory_space=pltpu.MemorySpace.HBM) for s in land_shapes]
    nlocal = len(local_plan(srcs + lands))
    dma = pltpu.SemaphoreType.DMA

    @pl.kernel(mesh=plsc.ScalarSubcoreMesh(axis_name="sequencer", num_cores=1), name=name,
               scratch_types=(dma((ncopies,)), dma((ncopies,)), dma((max(nlocal, 1),))),
               compiler_params=pltpu.CompilerParams(collective_id=collective_id))
    def launch(send_sems, recv_sems, local_sems):
        _handshake(peers(_place()))
        copies = plan(srcs + lands)
        local = [pltpu.make_async_copy(src, dst, local_sems.at[j])
                 for j, (src, dst) in enumerate(local_plan(srcs + lands))]
        for cp in local:
            cp.start()
        for i, (src, dst, peer, _) in enumerate(copies):
            _remote(src, dst, send_sems.at[i], recv_sems.at[i], peer).start()
        for i, (src, _, peer, incoming) in enumerate(copies):
            cp = _remote(src, incoming, send_sems.at[i], recv_sems.at[i], peer)
            cp.wait_send()
            cp.wait_recv()
        for cp in local:
            cp.wait()

    launch()
    return [land[...] for land in lands]


def _sequencer_scatter_hop2(sums, *, collective_id, name):
    n = len(sums)
    shapes = [jax.ShapeDtypeStruct(v.shape, v.dtype) for v in sums]
    return _sequencer_exchange(sums, shapes, 3 * n, _scatter2_plan(n), _scatter2_local(n),
                               lambda place: [peer for peer, _ in place[4]], collective_id=collective_id, name=name)


N_CHIP = N_DEV // 2


def _scatter1_plan(n):
    def plan(refs):
        _, _, c, sibling, _ = _place()
        out = []
        for wi in range(n):
            parts, half = refs[wi], refs[n + wi]
            for chip in range(N_CHIP):
                out.append((parts.at[2 * chip + 1 - c], half.at[chip], sibling, half.at[chip]))
        return out

    return plan


def _scatter2_plan(n):
    def plan(refs):
        _, my_chip, _, _, others = _place()
        out = []
        for wi in range(n):
            sums, recv = refs[wi], refs[n + wi]
            for peer, chip in others:
                out.append((sums.at[chip], recv.at[my_chip], peer, recv.at[chip]))
        return out

    return plan


def _scatter2_local(n):
    def plan(refs):
        my_chip = _place()[1]
        return [(refs[wi].at[my_chip], refs[n + wi].at[my_chip]) for wi in range(n)]

    return plan


def _pair_add(parts, half, core, *, name):
    _, r, c = parts.shape
    tr = max(d for d in range(HALO, 257, HALO) if r % d == 0) if r > 256 else r
    parts4 = parts.reshape(N_CHIP, 2, r, c)

    def body(core_ref, p_ref, h_ref, o_ref):
        o_ref[...] = (p_ref[:, 0].astype(F32) + h_ref[...].astype(F32)).astype(o_ref.dtype)

    return pl.pallas_call(
        body, name=name,
        grid_spec=pltpu.PrefetchScalarGridSpec(
            num_scalar_prefetch=1, grid=(r // tr,),
            in_specs=[pl.BlockSpec((N_CHIP, 1, tr, c), lambda i, core_ref: (0, core_ref[0], i, 0)),
                      pl.BlockSpec((N_CHIP, tr, c), lambda i, core_ref: (0, i, 0))],
            out_specs=pl.BlockSpec((N_CHIP, tr, c), lambda i, core_ref: (0, i, 0))),
        out_shape=jax.ShapeDtypeStruct((N_CHIP, r, c), parts.dtype), compiler_params=_cp("parallel"),
    )(core, parts4, half)


def _scatter_start(parts, *, name):
    halves = [lax.empty((N_CHIP,) + v.shape[1:], v.dtype) for v in parts]
    return _split_start(parts + halves, N_CHIP * len(parts), _scatter1_plan(len(parts)), name=name)


def _adamw(parts, w, m, v, *, name):
    r, c = w.shape
    nparts = parts.shape[0]
    tr = max(d for d in range(HALO, 129, HALO) if r % d == 0) if r > 128 else r

    def body(p_ref, w_ref, m_ref, v_ref, g_ref, d_ref, mo_ref, vo_ref):
        g = p_ref[0].astype(F32)
        for k in range(1, nparts):
            g = g + p_ref[k].astype(F32)
        mn = ADAM_B1 * m_ref[...] + (1.0 - ADAM_B1) * g
        vn = ADAM_B2 * v_ref[...] + (1.0 - ADAM_B2) * (g * g)
        m_hat = mn / (1.0 - ADAM_B1 ** ADAM_STEP)
        v_hat = vn / (1.0 - ADAM_B2 ** ADAM_STEP)
        g_ref[...] = g
        d_ref[...] = -ADAM_LR * (m_hat / (jnp.sqrt(v_hat) + ADAM_EPS) + ADAM_WD * w_ref[...])
        mo_ref[...] = mn
        vo_ref[...] = vn

    blk = pl.BlockSpec((tr, c), lambda i: (i, 0))
    return pl.pallas_call(
        body, name=name, grid=(r // tr,), in_specs=[pl.BlockSpec((nparts, tr, c), lambda i: (0, i, 0)), blk, blk, blk],
        out_specs=[blk] * 4, out_shape=[jax.ShapeDtypeStruct((r, c), F32)] * 4, compiler_params=_cp("parallel"),
    )(parts, w, m, v)


def _pack_rows(vs, rows):
    lead = vs[0].shape[:-1] if vs[0].ndim > 1 else ()
    flat = jnp.concatenate(vs, axis=-1)
    pad = rows * LANES - flat.shape[-1]
    flat = jnp.pad(flat, [(0, 0)] * len(lead) + [(0, pad)])
    return flat.reshape(lead + (rows, LANES))


def kernel(x, p, positions, mix_norm_w, w_in, conv_w, conv_b, dt_bias, a_log, d_skip, ssd_norm_w, q_a_norm_w, w_q_b, kv_a_norm_w, w_kv_b, w_out, ffn_norm_w, w_ffn_up, ffn_conv_w, ffn_conv_b, w_ffn_down, ple_norm_w, w_ple_gate, b_ple_gate, w_ple_proj, ple_post_norm_w, final_norm_w, loss_target, m_mix_norm_w, m_w_in, m_conv_w, m_conv_b, m_dt_bias, m_a_log, m_d_skip, m_ssd_norm_w, m_q_a_norm_w, m_w_q_b, m_kv_a_norm_w, m_w_kv_b, m_w_out, m_ffn_norm_w, m_w_ffn_up, m_ffn_conv_w, m_ffn_conv_b, m_w_ffn_down, m_ple_norm_w, m_w_ple_gate, m_b_ple_gate, m_w_ple_proj, m_ple_post_norm_w, m_final_norm_w, v_mix_norm_w, v_w_in, v_conv_w, v_conv_b, v_dt_bias, v_a_log, v_d_skip, v_ssd_norm_w, v_q_a_norm_w, v_w_q_b, v_kv_a_norm_w, v_w_kv_b, v_w_out, v_ffn_norm_w, v_w_ffn_up, v_ffn_conv_w, v_ffn_conv_b, v_w_ffn_down, v_ple_norm_w, v_w_ple_gate, v_b_ple_gate, v_w_ple_proj, v_ple_post_norm_w, v_final_norm_w):
    given = dict(locals())
    shapes = {n: given[n].shape for n in WEIGHTS}
    w2 = {n: given[n].reshape(given[n].shape[-2:] if n in BIG or n in CONV else (1, -1)) for n in WEIGHTS}
    m2 = {n: given['m_' + n].reshape(w2[n].shape) for n in WEIGHTS}
    v2 = {n: given['v_' + n].reshape(w2[n].shape) for n in WEIGHTS}
    me = 4 * lax.axis_index("x") + 2 * lax.axis_index("y") + lax.axis_index("c")

    core = lax.axis_index("c").astype(jnp.int32).reshape(1)

    def shards(grp, zero):
        return [(w2[n] + zero).astype(BF16) if n in BIG else w2[n] + zero for n in WEIGHT_GROUPS[grp]]

    first, token = _gather_start(shards('a', 0.0), name="gather_a_hop1")
    first, token = _gather_forward(first, token, name="gather_a_hop2")
    zero = token[0, 0]
    later = dict(zip(WEIGHT_GROUPS['b'], _sequencer_gather(shards('b', zero), collective_id=1, name="gather_b")))
    later.update(zip(WEIGHT_GROUPS['c'], _sequencer_gather(shards('c', zero), collective_id=2, name="gather_c")))

    def get_w(grp, after):
        if grp == 'a':
            lands = dict(zip(WEIGHT_GROUPS[grp], _gather_finish(first, token, name="gather_a_done")))
        else:
            names = WEIGHT_GROUPS[grp]
            lands = dict(zip(names, _hold([later[n] for n in names], after, name="gather_" + grp + "_use")))
        return _assemble_weights(lands)

    scatters = {}

    hop_ids = {grp: 2 + 2 * i for i, grp in enumerate(GRAD_GROUPS)}

    def zero_of(arrays):
        return sum(v[(0,) * v.ndim].astype(F32) * 0.0 for v in arrays)

    def emit(grp, grads):
        scatters[grp], tok = _scatter_start([grads[n] for n in GRAD_GROUPS[grp]], name="scatter_" + grp + "_hop1")
        return tok[0, 0]

    def relay(grp, after):
        n = len(GRAD_GROUPS[grp])
        bufs = _split_wait(scatters[grp], after, _scatter1_plan(n), lambda refs: [], name="scatter_" + grp + "_hop1_wait")
        sums = [_pair_add(bufs[i], bufs[n + i], core, name="scatter_%s_add%d" % (grp, i)) for i in range(n)]
        scatters[grp] = _sequencer_scatter_hop2(sums, collective_id=hop_ids[grp] + 1, name="scatter_" + grp + "_hop2")
        return zero_of(sums)

    out_g, out_d, out_m, out_v = {}, {}, {}, {}

    def settle(grp):
        return zero_of(scatters[grp])

    def update(grp, behind=None):
        for n, parts in zip(GRAD_GROUPS[grp], scatters[grp]):
            wn = w2[n] if behind is None else w2[n] + behind
            out_g[n], out_d[n], out_m[n], out_v[n] = _adamw(parts, wn, m2[n], v2[n], name="adamw_" + n)

    vecs = {n: w2[n] for n in REPL}
    vecs['mix_norm_w'] = vecs['mix_norm_w'] + zero
    loss, dx, g_conv, g_vec = _local_step(x[0], p[0, 0], _rope_tables(positions), get_w, vecs, loss_target[0], emit,
                                          relay, settle)
    n_small = sum(g_vec[n].shape[1] for n in REPL) + sum(g_conv[n].size for n in CONV) + 1
    rows_small = -(-n_small // (LANES * HALO)) * HALO
    small = _pack_rows([g_vec[n] for n in REPL] + [g_conv[n].reshape(1, -1) for n in CONV] + [loss], rows_small)

    for grp in list(GRAD_GROUPS)[:-1]:
        update(grp)
    all_small = _exchange([small], gather=True, name="gather_small_grads")[0].reshape(N_DEV, rows_small * LANES)
    update(list(GRAD_GROUPS)[-1], zero_of([all_small]))
    pieces, off = [], 0
    for n in REPL:
        k = g_vec[n].shape[1]
        pieces.append(all_small[:, off:off + k])
        off += k
    for n in CONV:
        kw, cols = g_conv[n].shape
        full = all_small[:, off:off + kw * cols].reshape(N_DEV, kw, cols)
        mine = lax.dynamic_slice_in_dim(full, me * (cols // N_DEV), cols // N_DEV, axis=2)
        pieces.append(mine.reshape(N_DEV, kw * (cols // N_DEV)))
        off += kw * cols
    pieces.append(all_small[:, off:off + 1])
    small_names = REPL + CONV
    n_mine = sum(q.shape[1] for q in pieces)
    rows_mine = -(-n_mine // (LANES * HALO)) * HALO
    zero = jnp.zeros((1, 1), F32)
    packed = [_pack_rows([src[n].reshape(1, -1) for n in small_names] + [zero], rows_mine).reshape(rows_mine, LANES)
              for src in (w2, m2, v2)]
    sg, sd, sm, sv = _adamw(_pack_rows(pieces, rows_mine), *packed, name="adamw_small")
    off = 0
    for n in small_names:
        k = w2[n].size
        for dst, src in ((out_g, sg), (out_d, sd), (out_m, sm), (out_v, sv)):
            dst[n] = src.reshape(-1)[off:off + k].reshape(w2[n].shape)
        off += k
    total_loss = sg.reshape(-1)[off]

    outs = [total_loss, dx[None]]
    for res in (out_g, out_d, out_m, out_v):
        outs += [res[n].reshape(shapes[n]) for n in WEIGHTS]
    return tuple(outs)
```

```python
import math

import numpy as np
import jax
import jax.numpy as jnp
from jax import lax
from jax.experimental import pallas as pl
from jax.experimental.pallas import tpu as pltpu
from jax.experimental.pallas import tpu_sc as plsc

F32 = jnp.float32
BF16 = jnp.bfloat16
HI = lax.Precision.HIGHEST

D_MODEL = 2048
CHUNK = 64
D_SSM = 1024
SSD_P = 64
SSD_HEADS = 16
SSD_GROUPS = 2
SSD_N = 128
SSD_CONV = 4
SSD_CONV_DIM = D_SSM + 2 * SSD_GROUPS * SSD_N
MLA_HEADS = 8
MLA_NOPE = 128
MLA_ROPE = 64
MLA_V = 128
MLA_Q_RANK = 512
MLA_KV_RANK = 256
MLA_QK_PAD = 256
ROPE_THETA = 10000.0
D_FF = 5632
FFN_CONV = 3
PLE_DIM = 256
NORM_EPS = 1e-6
ADAM_LR, ADAM_B1, ADAM_B2, ADAM_EPS, ADAM_WD, ADAM_STEP = 0.001, 0.9, 0.999, 1e-08, 0.01, 10
N_DEV = 8

OFF_Z, OFF_XBC, OFF_QA, OFF_CKV, OFF_KR, OFF_DT, D_IN_PAD = 0, 1024, 2560, 3072, 3328, 3456, 3584
D_IN = 3408
LANES = 128
HALO = 8
VMEM_LIMIT = 56 * 1024 * 1024
FFN_TC = D_FF * 2 // N_DEV
FFN_PERM = (0, 4, 1, 5, 2, 6, 3, 7)
NEG = -1e30


def _cp(*sem):
    return pltpu.CompilerParams(dimension_semantics=tuple(sem), vmem_limit_bytes=VMEM_LIMIT)


def _tile(n, want):
    if n <= want:
        return n
    best = max(d for d in range(LANES, want + 1, LANES) if n % d == 0)
    return best


def _sigmoid(x):
    return 0.5 * (jnp.tanh(0.5 * x) + 1.0)


def _silu(x):
    return x * _sigmoid(x)


def _dsilu(x):
    s = _sigmoid(x)
    return s * (1.0 + x * (1.0 - s))


MM_TILE = 1408
MM_TK = 2816


def _matmul(a, b, *, ta=False, tb=False, out_dtype=F32, add=None, bias=None, tm=MM_TILE, tn=MM_TILE, tk=MM_TK, name,
            mnk=None, a_spec=None, b_spec=None, o_spec=None, o_shape=None):
    if mnk is None:
        m, k = (a.shape[1], a.shape[0]) if ta else a.shape
        n = b.shape[0] if tb else b.shape[1]
        assert k == (b.shape[1] if tb else b.shape[0])
    else:
        m, n, k = mnk
    tm, tn, tk = _tile(m, tm), _tile(n, tn), _tile(k, tk)
    nk = k // tk
    dims = (((0 if ta else 1,), (1 if tb else 0,)), ((), ()))

    def body(*refs):
        a_ref, b_ref = refs[0], refs[1]
        pos = 2
        add_ref = bias_ref = None
        if add is not None:
            add_ref = refs[pos]
            pos += 1
        if bias is not None:
            bias_ref = refs[pos]
            pos += 1
        o_ref = refs[pos]
        kk = pl.program_id(2)
        av = a_ref[...]
        bv = b_ref[...]
        av = av.reshape(av.shape[-2:]).astype(BF16)
        bv = bv.reshape(bv.shape[-2:]).astype(BF16)
        prod = lax.dot_general(av, bv, dims, preferred_element_type=F32)

        def finish(r):
            if bias_ref is not None:
                r = r + bias_ref[...]
            if add_ref is not None:
                r = r + add_ref[...].astype(F32)
            o_ref[...] = r.astype(out_dtype).reshape(o_ref.shape)

        if nk == 1:
            finish(prod)
        else:
            acc_ref = refs[pos + 1]

            @pl.when(kk == 0)
            def _():
                acc_ref[...] = prod

            @pl.when(kk > 0)
            def _():
                acc_ref[...] += prod

            @pl.when(kk == nk - 1)
            def _():
                finish(acc_ref[...])

    if a_spec is None:
        a_spec = (pl.BlockSpec((tk, tm), lambda i, j, kk: (kk, i)) if ta
                  else pl.BlockSpec((tm, tk), lambda i, j, kk: (i, kk)))
    if b_spec is None:
        b_spec = (pl.BlockSpec((tn, tk), lambda i, j, kk: (j, kk)) if tb
                  else pl.BlockSpec((tk, tn), lambda i, j, kk: (kk, j)))
    if o_spec is None:
        o_spec = pl.BlockSpec((tm, tn), lambda i, j, kk: (i, j))
    if o_shape is None:
        o_shape = (m, n)
    in_specs = [a_spec, b_spec]
    args = [a, b]
    if add is not None:
        in_specs.append(pl.BlockSpec((tm, tn), lambda i, j, kk: (i, j)))
        args.append(add)
    if bias is not None:
        in_specs.append(pl.BlockSpec((1, tn), lambda i, j, kk: (0, j)))
        args.append(bias)
    return pl.pallas_call(
        body, name=name, grid=(m // tm, n // tn, nk), in_specs=in_specs, out_specs=o_spec,
        out_shape=jax.ShapeDtypeStruct(o_shape, out_dtype),
        scratch_shapes=[pltpu.VMEM((tm, tn), F32)] if nk > 1 else [],
        compiler_params=_cp("parallel", "parallel", "arbitrary"),
    )(*args)


def _rmsnorm_fwd(x, w, *, width, cblk=0, out_dtype=BF16, tr=256, name):
    t = x.shape[0]

    def body(x_ref, w_ref, o_ref):
        xv = x_ref[...].astype(F32)
        r = lax.rsqrt(jnp.mean(xv * xv, axis=-1, keepdims=True) + NORM_EPS)
        o_ref[...] = (xv * r * w_ref[...]).astype(out_dtype)

    return pl.pallas_call(
        body, name=name, grid=(t // tr,),
        in_specs=[pl.BlockSpec((tr, width), lambda i: (i, cblk)), pl.BlockSpec((1, width), lambda i: (0, 0))],
        out_specs=pl.BlockSpec((tr, width), lambda i: (i, 0)),
        out_shape=jax.ShapeDtypeStruct((t, width), out_dtype),
        compiler_params=_cp("parallel"),
    )(x, w)


def _rmsnorm_bwd(x, w, dy, add=None, *, width, cblk=0, out_dtype=F32, also_bf16=False, tr=256, name):
    t = x.shape[0]

    def body(*refs):
        refs = list(refs)
        dxb_ref = refs.pop() if also_bf16 else None
        if add is None:
            x_ref, w_ref, dy_ref, dx_ref, dw_ref = refs
            add_ref = None
        else:
            x_ref, w_ref, dy_ref, add_ref, dx_ref, dw_ref = refs
        xv = x_ref[...].astype(F32)
        dyv = dy_ref[...].astype(F32)
        r = lax.rsqrt(jnp.mean(xv * xv, axis=-1, keepdims=True) + NORM_EPS)
        xh = xv * r
        g = dyv * w_ref[...]
        dx = r * (g - xh * jnp.mean(g * xh, axis=-1, keepdims=True))
        if add_ref is not None:
            dx = dx + add_ref[...].astype(F32)
        dx_ref[...] = dx.astype(out_dtype)
        if dxb_ref is not None:
            dxb_ref[...] = dx.astype(BF16)

        @pl.when(pl.program_id(0) == 0)
        def _():
            dw_ref[...] = jnp.zeros_like(dw_ref)

        dw_ref[...] += jnp.sum(dyv * xh, axis=0, keepdims=True)

    in_specs = [pl.BlockSpec((tr, width), lambda i: (i, cblk)), pl.BlockSpec((1, width), lambda i: (0, 0)),
                pl.BlockSpec((tr, width), lambda i: (i, 0))]
    args = [x, w, dy]
    if add is not None:
        in_specs.append(pl.BlockSpec((tr, width), lambda i: (i, 0)))
        args.append(add)
    blk = pl.BlockSpec((tr, width), lambda i: (i, 0))
    return pl.pallas_call(
        body, name=name, grid=(t // tr,), in_specs=in_specs,
        out_specs=[blk, pl.BlockSpec((1, width), lambda i: (0, 0))] + ([blk] if also_bf16 else []),
        out_shape=[jax.ShapeDtypeStruct((t, width), out_dtype), jax.ShapeDtypeStruct((1, width), F32)]
        + ([jax.ShapeDtypeStruct((t, width), BF16)] if also_bf16 else []),
        compiler_params=_cp("arbitrary"),
    )(*args)


def _shift_down(prev_halo, cur, j):
    if j == 0:
        return cur
    ext = jnp.concatenate([prev_halo, cur], axis=0)
    return pltpu.roll(ext, j, axis=0)[HALO:]


def _shift_up(cur, next_halo, j):
    if j == 0:
        return cur
    ext = jnp.concatenate([cur, next_halo], axis=0)
    return pltpu.roll(ext, ext.shape[0] - j, axis=0)[:cur.shape[0]]


def _conv_rows(prev, cur, w, b, kw):
    shifted = [cur]
    out = b + w[kw - 1:kw] * cur
    for j in range(1, kw):
        sh = _shift_down(prev, cur, j)
        shifted.append(sh)
        out = out + w[kw - 1 - j:kw - j] * sh
    return out, shifted


def _act_fwd(c, glu):
    if glu:
        half = c.shape[1] // 2
        return _silu(c[:, :half]) * c[:, half:]
    return _silu(c)


def _act_bwd(c, dout, glu):
    if glu:
        half = c.shape[1] // 2
        g, up = c[:, :half], c[:, half:]
        s = _sigmoid(g)
        gs = g * s
        return jnp.concatenate([dout * up * (s + gs * (1.0 - s)), dout * gs], axis=1)
    return dout * _dsilu(c)


def _conv_act_fwd(u, w, b, *, kw, glu, tc, coff, ncols, out_dtype, tr=256, name):
    t = u.shape[0]
    nb = ncols // tc
    oc = tc // 2 if glu else tc

    def body(u_ref, uh_ref, w_ref, b_ref, o_ref):
        prev = jnp.where(pl.program_id(0) == 0, 0.0, uh_ref[...])
        c, _ = _conv_rows(prev, u_ref[...], w_ref[...], b_ref[...], kw)
        o_ref[...] = _act_fwd(c, glu).astype(out_dtype)

    return pl.pallas_call(
        body, name=name, grid=(t // tr, nb),
        in_specs=[pl.BlockSpec((tr, tc), lambda i, j: (i, j + coff)),
                  pl.BlockSpec((HALO, tc), lambda i, j: (jnp.maximum(i * (tr // HALO) - 1, 0), j + coff)),
                  pl.BlockSpec((kw, tc), lambda i, j: (0, j)), pl.BlockSpec((1, tc), lambda i, j: (0, j))],
        out_specs=pl.BlockSpec((tr, oc), lambda i, j: (i, j)),
        out_shape=jax.ShapeDtypeStruct((t, nb * oc), out_dtype),
        compiler_params=_cp("parallel", "parallel"),
    )(u, u, w, b)


def _conv_act_bwd(u, w, b, dout, *, kw, glu, tc, coff, ncols, tr=256, name):
    t = u.shape[0]
    nb = ncols // tc
    nt = t // tr
    oc = tc // 2 if glu else tc

    def body(u_ref, up_ref, un_ref, d_ref, dn_ref, w_ref, b_ref, du_ref, dw_ref, db_ref):
        i = pl.program_id(1)
        cur, nxt, wv, bv = u_ref[...], un_ref[...], w_ref[...], b_ref[...]
        prev = jnp.where(i == 0, 0.0, up_ref[...])
        c_cur, shifted = _conv_rows(prev, cur, wv, bv, kw)
        c_nxt, _ = _conv_rows(cur[tr - HALO:], nxt, wv, bv, kw)
        d_cur = _act_bwd(c_cur, d_ref[...].astype(F32), glu)
        d_nxt = _act_bwd(c_nxt, jnp.where(i == nt - 1, 0.0, dn_ref[...].astype(F32)), glu)
        du = wv[kw - 1:kw] * d_cur
        for j in range(1, kw):
            du = du + wv[kw - 1 - j:kw - j] * _shift_up(d_cur, d_nxt, j)
        du_ref[...] = du.astype(BF16)

        @pl.when(i == 0)
        def _():
            dw_ref[...] = jnp.zeros_like(dw_ref)
            db_ref[...] = jnp.zeros_like(db_ref)

        db_ref[...] += jnp.sum(d_cur, axis=0, keepdims=True)
        dw_ref[...] += jnp.concatenate(
            [jnp.sum(d_cur * shifted[kw - 1 - k], axis=0, keepdims=True) for k in range(kw)], axis=0)

    nh = tr // HALO
    return pl.pallas_call(
        body, name=name, grid=(nb, nt),
        in_specs=[pl.BlockSpec((tr, tc), lambda j, i: (i, j + coff)),
                  pl.BlockSpec((HALO, tc), lambda j, i: (jnp.maximum(i * nh - 1, 0), j + coff)),
                  pl.BlockSpec((HALO, tc), lambda j, i: (jnp.minimum((i + 1) * nh, t // HALO - 1), j + coff)),
                  pl.BlockSpec((tr, oc), lambda j, i: (i, j)),
                  pl.BlockSpec((HALO, oc), lambda j, i: (jnp.minimum((i + 1) * nh, t // HALO - 1), j)),
                  pl.BlockSpec((kw, tc), lambda j, i: (0, j)), pl.BlockSpec((1, tc), lambda j, i: (0, j))],
        out_specs=[pl.BlockSpec((tr, tc), lambda j, i: (i, j)), pl.BlockSpec((kw, tc), lambda j, i: (0, j)),
                   pl.BlockSpec((1, tc), lambda j, i: (0, j))],
        out_shape=[jax.ShapeDtypeStruct((t, ncols), BF16), jax.ShapeDtypeStruct((kw, ncols), F32),
                   jax.ShapeDtypeStruct((1, ncols), F32)],
        compiler_params=_cp("parallel", "arbitrary"),
    )(u, u, u, dout, dout, w, b)


def _ple_loss(x2, gl, pe, pw, fw, target, *, tr=256, name):
    t, d = x2.shape

    def body(x_ref, gl_ref, pe_ref, pw_ref, fw_ref, t_ref, l_ref, dx_ref, dfw_ref, dgl_ref, db_ref, dpe_ref, dpw_ref):
        pv, pwv, wv = pe_ref[...], pw_ref[...], fw_ref[...]
        gate = _sigmoid(gl_ref[...])
        rp = lax.rsqrt(jnp.mean(pv * pv, axis=-1, keepdims=True) + NORM_EPS)
        ph = pv * rp
        e = ph * pwv
        x3 = x_ref[...] + gate * e
        r = lax.rsqrt(jnp.mean(x3 * x3, axis=-1, keepdims=True) + NORM_EPS)
        xh = x3 * r
        err = xh * wv - t_ref[...]
        dy = err * (1.0 / d)
        g = dy * wv
        dx = r * (g - xh * jnp.mean(g * xh, axis=-1, keepdims=True))
        dx_ref[...] = dx
        dgl = dx * e * gate * (1.0 - gate)
        de = dx * gate
        gg = de * pwv
        dgl_ref[...] = dgl.astype(BF16)
        dpe_ref[...] = (rp * (gg - ph * jnp.mean(gg * ph, axis=-1, keepdims=True))).astype(BF16)

        @pl.when(pl.program_id(0) == 0)
        def _():
            for ref in (l_ref, dfw_ref, db_ref, dpw_ref):
                ref[...] = jnp.zeros_like(ref)

        l_ref[...] += 0.5 * jnp.sum(jnp.mean(err * err, axis=-1, keepdims=True), axis=0, keepdims=True)
        dfw_ref[...] += jnp.sum(dy * xh, axis=0, keepdims=True)
        db_ref[...] += jnp.sum(dgl, axis=0, keepdims=True)
        dpw_ref[...] += jnp.sum(de * ph, axis=0, keepdims=True)

    blk = pl.BlockSpec((tr, d), lambda i: (i, 0))
    row = pl.BlockSpec((1, d), lambda i: (0, 0))
    rowf = jax.ShapeDtypeStruct((1, d), F32)
    return pl.pallas_call(
        body, name=name, grid=(t // tr,), in_specs=[blk, blk, blk, row, row, blk],
        out_specs=[pl.BlockSpec((1, 1), lambda i: (0, 0)), blk, row, blk, row, blk, row],
        out_shape=[jax.ShapeDtypeStruct((1, 1), F32), jax.ShapeDtypeStruct((t, d), F32), rowf,
                   jax.ShapeDtypeStruct((t, d), BF16), rowf, jax.ShapeDtypeStruct((t, d), BF16), rowf],
        compiler_params=_cp("arbitrary"),
    )(x2, gl, pe, pw, fw, target)


def _rope(blk, tab_ref):
    return blk * tab_ref[0] + pltpu.roll(blk, 96, axis=1) * tab_ref[1] + pltpu.roll(blk, 32, axis=1) * tab_ref[2]


def _unrope(g, tab_ref):
    return g * tab_ref[0] + pltpu.roll(g * tab_ref[1], 32, axis=1) + pltpu.roll(g * tab_ref[2], 96, axis=1)


def _mla_prep(q, kv, proj, tabs, *, tr=512, name):
    t = q.shape[0]

    def body(q_ref, kv_ref, kr_ref, tab_ref, qo_ref, ko_ref, vo_ref, vt_ref):
        qv, kvv = q_ref[...], kv_ref[...]
        qo_ref[0, :, :MLA_NOPE] = qv[:, :MLA_NOPE].astype(BF16)
        qo_ref[0, :, MLA_NOPE:] = _rope(qv[:, MLA_NOPE:], tab_ref).astype(BF16)
        ko_ref[0, :, :MLA_NOPE] = kvv[:, :MLA_NOPE].astype(BF16)
        ko_ref[0, :, MLA_NOPE:] = _rope(kr_ref[...], tab_ref).astype(BF16)
        vo_ref[0] = kvv[:, MLA_NOPE:].astype(BF16)
        for blk in range(tr // ATT_BLK):
            vt_ref[0, blk] = kvv[blk * ATT_BLK:(blk + 1) * ATT_BLK, MLA_NOPE:].T.astype(BF16)

    return pl.pallas_call(
        body, name=name, grid=(t // tr, MLA_HEADS),
        in_specs=[pl.BlockSpec((tr, MLA_QK_PAD), lambda i, h: (i, h)),
                  pl.BlockSpec((tr, MLA_NOPE + MLA_V), lambda i, h: (i, h)),
                  pl.BlockSpec((tr, LANES), lambda i, h: (i, OFF_KR // LANES)),
                  pl.BlockSpec((3, tr, LANES), lambda i, h: (0, i, 0))],
        out_specs=[pl.BlockSpec((1, tr, MLA_QK_PAD), lambda i, h: (h, i, 0)),
                   pl.BlockSpec((1, tr, MLA_QK_PAD), lambda i, h: (h, i, 0)),
                   pl.BlockSpec((1, tr, MLA_V), lambda i, h: (h, i, 0)),
                   pl.BlockSpec((1, tr // ATT_BLK, MLA_V, ATT_BLK), lambda i, h: (h, i, 0, 0))],
        out_shape=[jax.ShapeDtypeStruct((MLA_HEADS, t, MLA_QK_PAD), BF16),
                   jax.ShapeDtypeStruct((MLA_HEADS, t, MLA_QK_PAD), BF16),
                   jax.ShapeDtypeStruct((MLA_HEADS, t, MLA_V), BF16),
                   jax.ShapeDtypeStruct((MLA_HEADS, t // ATT_BLK, MLA_V, ATT_BLK), BF16)],
        compiler_params=_cp("parallel", "parallel"),
    )(q, kv, proj, tabs)


def _mla_unprep(dq3, dk3, dv3, tabs, *, tr=256, name):
    t = dq3.shape[1]

    def body(dq_ref, dk_ref, dv_ref, tab_ref, qo_ref, kvo_ref, kro_ref):
        kr = jnp.zeros((tr, LANES), F32)
        for h in range(MLA_HEADS):
            c0 = h * MLA_QK_PAD
            qo_ref[:, c0:c0 + MLA_NOPE] = dq_ref[h, :, :MLA_NOPE].astype(BF16)
            qo_ref[:, c0 + MLA_NOPE:c0 + MLA_QK_PAD] = _unrope(dq_ref[h, :, MLA_NOPE:], tab_ref).astype(BF16)
            kvo_ref[:, c0:c0 + MLA_NOPE] = dk_ref[h, :, :MLA_NOPE].astype(BF16)
            kvo_ref[:, c0 + MLA_NOPE:c0 + MLA_QK_PAD] = dv_ref[h].astype(BF16)
            kr = kr + dk_ref[h, :, MLA_NOPE:]
        kro_ref[...] = _unrope(kr, tab_ref).astype(BF16)

    return pl.pallas_call(
        body, name=name, grid=(t // tr,),
        in_specs=[pl.BlockSpec((MLA_HEADS, tr, MLA_QK_PAD), lambda i: (0, i, 0)),
                  pl.BlockSpec((MLA_HEADS, tr, MLA_QK_PAD), lambda i: (0, i, 0)),
                  pl.BlockSpec((MLA_HEADS, tr, MLA_V), lambda i: (0, i, 0)),
                  pl.BlockSpec((3, tr, LANES), lambda i: (0, i, 0))],
        out_specs=[pl.BlockSpec((tr, MLA_HEADS * MLA_QK_PAD), lambda i: (i, 0)),
                   pl.BlockSpec((tr, MLA_HEADS * MLA_QK_PAD), lambda i: (i, 0)),
                   pl.BlockSpec((tr, LANES), lambda i: (i, 0))],
        out_shape=[jax.ShapeDtypeStruct((t, MLA_HEADS * MLA_QK_PAD), BF16),
                   jax.ShapeDtypeStruct((t, MLA_HEADS * MLA_QK_PAD), BF16),
                   jax.ShapeDtypeStruct((t, LANES), BF16)],
        compiler_params=_cp("parallel"),
    )(dq3, dk3, dv3, tabs)


ATT_BLK = 512
ATT_SCALE = 1.0 / math.sqrt(MLA_NOPE + MLA_ROPE)
_NT = (((1,), (1,)), ((), ()))
_TN = (((0,), (0,)), ((), ()))


def _att_scores_t(k, q, diagonal):
    s = lax.dot_general(k, q, _NT, preferred_element_type=F32) * ATT_SCALE
    if not diagonal:
        return s
    key = lax.broadcasted_iota(jnp.int32, s.shape, 0)
    query = lax.broadcasted_iota(jnp.int32, s.shape, 1)
    return jnp.where((key >> 6) <= (query >> 6), s, NEG)


def _att_rows(i):
    return pl.ds(pl.multiple_of(i * ATT_BLK, ATT_BLK), ATT_BLK)


ATT_HEADS = 2


def _attn_fwd(q3, k3, vt4, *, name):
    t = q3.shape[1]
    nq = t // ATT_BLK

    def body(q_ref, k_ref, vt_ref, o_ref, lse_ref):
        qi = pl.program_id(1)
        qs = [q_ref[hh] for hh in range(ATT_HEADS)]

        def step(j, carry, diagonal=False):
            out = []
            for hh, (m, l, acc) in enumerate(carry):
                s = _att_scores_t(k_ref[hh, _att_rows(j), :], qs[hh], diagonal)
                m_new = jnp.maximum(m, jnp.max(s, axis=0, keepdims=True))
                p = jnp.exp(s - m_new)
                alpha = jnp.exp(m - m_new)
                l = alpha * l + jnp.sum(p, axis=0, keepdims=True)
                acc = alpha * acc + jnp.dot(vt_ref[hh, j], p.astype(BF16), preferred_element_type=F32)
                out.append((m_new, l, acc))
            return tuple(out)

        init = tuple((jnp.full((1, ATT_BLK), NEG, F32), jnp.zeros((1, ATT_BLK), F32),
                      jnp.zeros((MLA_V, ATT_BLK), F32)) for _ in range(ATT_HEADS))
        done = step(qi, lax.fori_loop(0, qi, step, init), diagonal=True)
        for hh, (m, l, acc) in enumerate(done):
            o_ref[:, hh * MLA_V:(hh + 1) * MLA_V] = (acc / l).T
            lse_ref[hh, 0] = m + jnp.log(l)

    return pl.pallas_call(
        body, name=name, grid=(MLA_HEADS // ATT_HEADS, nq),
        in_specs=[pl.BlockSpec((ATT_HEADS, ATT_BLK, MLA_QK_PAD), lambda h, i: (h, i, 0)),
                  pl.BlockSpec((ATT_HEADS, t, MLA_QK_PAD), lambda h, i: (h, 0, 0)),
                  pl.BlockSpec((ATT_HEADS, nq, MLA_V, ATT_BLK), lambda h, i: (h, 0, 0, 0))],
        out_specs=[pl.BlockSpec((ATT_BLK, ATT_HEADS * MLA_V), lambda h, i: (i, h)),
                   pl.BlockSpec((ATT_HEADS, 1, 1, ATT_BLK), lambda h, i: (h, i, 0, 0))],
        out_shape=[jax.ShapeDtypeStruct((t, MLA_HEADS * MLA_V), F32),
                   jax.ShapeDtypeStruct((MLA_HEADS, nq, 1, ATT_BLK), F32)],
        compiler_params=_cp("parallel", "parallel"),
    )(q3, k3, vt4)


def _attn_bwd(q3, k3, v3, o, dcat, lse, *, name):
    t = q3.shape[1]
    nq = t // ATT_BLK
    wide = ATT_HEADS * MLA_V

    def body(q_ref, k_ref, v_ref, o_ref, do_ref, lse_ref, dq_ref, dk_ref, dv_ref, delta_ref):
        kj = pl.program_id(1)

        @pl.when(kj == 0)
        def _():
            dq_ref[...] = jnp.zeros_like(dq_ref)
            ones = jnp.ones((HALO, MLA_V), F32)
            for i in range(nq):
                rows = pl.ds(i * ATT_BLK, ATT_BLK)
                prod = o_ref[rows, :] * do_ref[rows, :]
                for hh in range(ATT_HEADS):
                    delta_ref[hh, i] = lax.dot_general(ones, prod[:, hh * MLA_V:(hh + 1) * MLA_V], _NT, precision=HI,
                                                       preferred_element_type=F32)

        def step(i, carry, diagonal=False):
            rows = _att_rows(i)
            out = []
            for hh, (dk, dv) in enumerate(carry):
                k, v = k_ref[hh], v_ref[hh]
                q = q_ref[hh, rows, :]
                dob = do_ref[rows, hh * MLA_V:(hh + 1) * MLA_V].astype(BF16)
                p = jnp.exp(_att_scores_t(k, q, diagonal) - lse_ref[hh, i])
                dv = dv + jnp.dot(p.astype(BF16), dob, preferred_element_type=F32)
                dp = lax.dot_general(v, dob, _NT, preferred_element_type=F32)
                ds = (p * (dp - delta_ref[hh, i, 0:1, :]) * ATT_SCALE).astype(BF16)
                dk = dk + jnp.dot(ds, q, preferred_element_type=F32)
                dq_ref[hh, rows, :] += lax.dot_general(ds, k, _TN, preferred_element_type=F32)
                out.append((dk, dv))
            return tuple(out)

        init = tuple((jnp.zeros((ATT_BLK, MLA_QK_PAD), F32), jnp.zeros((ATT_BLK, MLA_V), F32))
                     for _ in range(ATT_HEADS))
        done = lax.fori_loop(kj + 1, nq, step, step(kj, init, diagonal=True))
        for hh, (dk, dv) in enumerate(done):
            dk_ref[hh] = dk
            dv_ref[hh] = dv

    return pl.pallas_call(
        body, name=name, grid=(MLA_HEADS // ATT_HEADS, nq),
        in_specs=[pl.BlockSpec((ATT_HEADS, t, MLA_QK_PAD), lambda h, j: (h, 0, 0)),
                  pl.BlockSpec((ATT_HEADS, ATT_BLK, MLA_QK_PAD), lambda h, j: (h, j, 0)),
                  pl.BlockSpec((ATT_HEADS, ATT_BLK, MLA_V), lambda h, j: (h, j, 0)),
                  pl.BlockSpec((t, wide), lambda h, j: (0, h)),
                  pl.BlockSpec((t, wide), lambda h, j: (0, MLA_HEADS // ATT_HEADS + h)),
                  pl.BlockSpec((ATT_HEADS, nq, 1, ATT_BLK), lambda h, j: (h, 0, 0, 0))],
        out_specs=[pl.BlockSpec((ATT_HEADS, t, MLA_QK_PAD), lambda h, j: (h, 0, 0)),
                   pl.BlockSpec((ATT_HEADS, ATT_BLK, MLA_QK_PAD), lambda h, j: (h, j, 0)),
                   pl.BlockSpec((ATT_HEADS, ATT_BLK, MLA_V), lambda h, j: (h, j, 0))],
        out_shape=[jax.ShapeDtypeStruct((MLA_HEADS, t, MLA_QK_PAD), F32),
                   jax.ShapeDtypeStruct((MLA_HEADS, t, MLA_QK_PAD), F32),
                   jax.ShapeDtypeStruct((MLA_HEADS, t, MLA_V), F32)],
        scratch_shapes=[pltpu.VMEM((ATT_HEADS, nq, HALO, ATT_BLK), F32)],
        compiler_params=_cp("parallel", "arbitrary"),
    )(q3, k3, v3, o, dcat, lse)


def _ssd_prep(proj, bias128, alog128, *, name):
    t = proj.shape[0]
    nc = t // CHUNK

    def body(raw_ref, b_ref, al_ref, dt_ref, cs_ref, a_ref):
        xv = raw_ref[...] + b_ref[...]
        dt = jnp.maximum(xv, 0.0) + jnp.log(1.0 + jnp.exp(-jnp.abs(xv)))
        a = -jnp.exp(al_ref[...])
        adt = (dt * a).reshape(nc, CHUNK, LANES)
        li = lax.broadcasted_iota(jnp.int32, (nc, CHUNK, CHUNK), 1)
        si = lax.broadcasted_iota(jnp.int32, (nc, CHUNK, CHUNK), 2)
        tril = jnp.where(si <= li, 1.0, 0.0).astype(F32)
        cs = lax.dot_general(tril, adt, (((2,), (1,)), ((0,), (0,))), precision=HI, preferred_element_type=F32)
        dt_ref[...] = dt
        cs_ref[...] = cs.reshape(t, LANES)
        a_ref[...] = a

    blk = pl.BlockSpec((t, LANES), lambda i: (0, 0))
    row = pl.BlockSpec((1, LANES), lambda i: (0, 0))
    return pl.pallas_call(
        body, name=name, grid=(1,),
        in_specs=[pl.BlockSpec((t, LANES), lambda i: (0, OFF_DT // LANES)), row, row],
        out_specs=[blk, blk, row],
        out_shape=[jax.ShapeDtypeStruct((t, LANES), F32), jax.ShapeDtypeStruct((t, LANES), F32),
                   jax.ShapeDtypeStruct((1, LANES), F32)],
        compiler_params=_cp("arbitrary"),
    )(proj, bias128, alog128)


def _ssd_prep_bwd(ddt128, dadt128, proj, bias128, dt128, a128, dd_h, *, name):
    t = proj.shape[0]

    def body(ddt_ref, dadt_ref, raw_ref, b_ref, dt_ref, a_ref, dd_ref, draw_ref, db_ref, dal_ref, dds_ref):
        draw = ddt_ref[...] * _sigmoid(raw_ref[...] + b_ref[...])
        draw_ref[...] = draw.astype(BF16)
        db_ref[...] = jnp.sum(draw, axis=0, keepdims=True)
        dal_ref[...] = jnp.sum(dadt_ref[...] * dt_ref[...], axis=0, keepdims=True) * a_ref[...]
        dds_ref[...] = jnp.sum(dd_ref[...], axis=-1, keepdims=True)

    blk = pl.BlockSpec((t, LANES), lambda i: (0, 0))
    row = pl.BlockSpec((1, LANES), lambda i: (0, 0))
    return pl.pallas_call(
        body, name=name, grid=(1,),
        in_specs=[blk, blk, pl.BlockSpec((t, LANES), lambda i: (0, OFF_DT // LANES)), row, blk, row,
                  pl.BlockSpec((SSD_HEADS, SSD_P), lambda i: (0, 0))],
        out_specs=[blk, row, row, pl.BlockSpec((SSD_HEADS, 1), lambda i: (0, 0))],
        out_shape=[jax.ShapeDtypeStruct((t, LANES), BF16), jax.ShapeDtypeStruct((1, LANES), F32),
                   jax.ShapeDtypeStruct((1, LANES), F32), jax.ShapeDtypeStruct((SSD_HEADS, 1), F32)],
        compiler_params=_cp("arbitrary"),
    )(ddt128, dadt128, proj, bias128, dt128, a128, dd_h)


def _bdot(a, b, ca, cb, precision=None):
    return lax.dot_general(a, b, (((ca,), (cb,)), ((0,), (0,))), precision=precision, preferred_element_type=F32)


def _pieces(x):
    hi = x.astype(BF16)
    rest = x - hi.astype(F32)
    mid = rest.astype(BF16)
    return hi, mid, (rest - mid.astype(F32)).astype(BF16)


def _bdot_sum(a, b, ca, cb, split):
    other = (b if split == 0 else a).astype(BF16)
    out = None
    for piece in _pieces(a if split == 0 else b):
        term = _bdot(piece, other, ca, cb) if split == 0 else _bdot(other, piece, ca, cb)
        out = term if out is None else out + term
    return out


def _head_matrices():
    eye, zero = jnp.eye(SSD_P, dtype=F32), jnp.zeros((SSD_P, SSD_P), F32)
    pick = jnp.stack([jnp.concatenate([eye, zero], axis=0), jnp.concatenate([zero, eye], axis=0)])
    return pick, pick.transpose(0, 2, 1)


def _move(x, sel):
    selb = sel.astype(BF16)
    hi = x.astype(BF16)
    rest = x - hi.astype(F32)
    mid = rest.astype(BF16)
    low = (rest - mid.astype(F32)).astype(BF16)
    out = jnp.dot(hi, selb, preferred_element_type=F32)
    out = out + jnp.dot(mid, selb, preferred_element_type=F32)
    return out + jnp.dot(low, selb, preferred_element_type=F32)


def _pick_head(pair_ref, pick_ref, h):
    return _move(pair_ref[...], pick_ref[h % 2])


def _place_head(out_ref, val, place_ref, h):
    wide = _move(val, place_ref[h % 2])

    @pl.when(h % 2 == 0)
    def _():
        out_ref[...] = wide

    @pl.when(h % 2 == 1)
    def _():
        out_ref[...] += wide


def _head_column(ref, h, nc):
    v = ref[...]
    mine = lax.broadcasted_iota(jnp.int32, v.shape, 1) == h
    return jnp.sum(jnp.where(mine, v, 0.0), axis=1, keepdims=True).reshape(nc, CHUNK, 1)


def _ssd_common(x2, dt_ref, cs_ref, csr_ref, b_ref, c_ref, nc, h):
    x = x2.reshape(nc, CHUNK, SSD_P)
    dt = _head_column(dt_ref, h, nc)
    cs = _head_column(cs_ref, h, nc)
    csr = csr_ref[0]
    bm = b_ref[...].reshape(nc, CHUNK, SSD_N).astype(BF16)
    cm = c_ref[...].reshape(nc, CHUNK, SSD_N).astype(BF16)
    li = lax.broadcasted_iota(jnp.int32, (nc, CHUNK, CHUNK), 1)
    si = lax.broadcasted_iota(jnp.int32, (nc, CHUNK, CHUNK), 2)
    lmat = jnp.exp(jnp.where(si <= li, cs - csr, NEG))
    g = _bdot(cm, bm, 2, 2)
    cs_last = jnp.sum(jnp.where(li == CHUNK - 1, cs, 0.0), axis=1, keepdims=True)
    xdt = x * dt
    dec = jnp.exp(cs_last - cs)
    return x, dt, cs, bm, cm, li, si, lmat, g, cs_last, xdt, dec


def _ssd_fwd(xbc, dt128, cs128, cs_row, dskip_h, *, name):
    t = xbc.shape[0]
    nc = t // CHUNK
    hpg = SSD_HEADS // SSD_GROUPS
    pick, place = _head_matrices()

    def body(xs_ref, dt_ref, cs_ref, csr_ref, b_ref, c_ref, dk_ref, pick_ref, place_ref, y_ref, st_ref, sc_ref, cd_ref):
        h = pl.program_id(0)
        x, dt, cs, bm, cm, li, si, lmat, g, cs_last, xdt, dec = _ssd_common(_pick_head(xs_ref, pick_ref, h), dt_ref,
                                                                           cs_ref, csr_ref, b_ref, c_ref, nc, h)
        yd = _bdot((g * lmat).astype(BF16), xdt.astype(BF16), 2, 1)
        sc_ref[...] = _bdot(bm, (dec * xdt).astype(BF16), 1, 1)
        cd_ref[...] = jnp.exp(cs_last)

        def step(c, s):
            st_ref[0, c] = s
            return s * cd_ref[c] + sc_ref[c]

        lax.fori_loop(0, nc, step, jnp.zeros((SSD_N, SSD_P), F32))
        yo = _bdot(cm, st_ref[0].astype(BF16), 2, 1) * jnp.exp(cs)
        _place_head(y_ref, (yd + yo + dk_ref[0] * x).reshape(t, SSD_P), place_ref, h)

    lanes = pl.BlockSpec((t, LANES), lambda h: (0, 0))
    pair = pl.BlockSpec((t, 2 * SSD_P), lambda h: (0, h // 2))
    nxb = D_SSM // SSD_N
    return pl.pallas_call(
        body, name=name, grid=(SSD_HEADS,),
        in_specs=[pair, lanes, lanes, pl.BlockSpec((1, nc, 1, CHUNK), lambda h: (h, 0, 0, 0)),
                  pl.BlockSpec((t, SSD_N), lambda h: (0, nxb + h // hpg)),
                  pl.BlockSpec((t, SSD_N), lambda h: (0, nxb + SSD_GROUPS + h // hpg)),
                  pl.BlockSpec((1, 1, SSD_P), lambda h: (h, 0, 0)),
                  pl.BlockSpec((2, 2 * SSD_P, SSD_P), lambda h: (0, 0, 0)),
                  pl.BlockSpec((2, SSD_P, 2 * SSD_P), lambda h: (0, 0, 0))],
        out_specs=[pair, pl.BlockSpec((1, nc, SSD_N, SSD_P), lambda h: (h, 0, 0, 0))],
        out_shape=[jax.ShapeDtypeStruct((t, D_SSM), F32),
                   jax.ShapeDtypeStruct((SSD_HEADS, nc, SSD_N, SSD_P), F32)],
        scratch_shapes=[pltpu.VMEM((nc, SSD_N, SSD_P), F32), pltpu.VMEM((nc, 1, SSD_P), F32)],
        compiler_params=_cp("arbitrary"),
    )(xbc, dt128, cs128, cs_row, xbc, xbc, dskip_h, pick, place)


def _ssd_bwd(xbc, dt128, cs128, cs_row, dskip_h, a_h, states, dy, *, name):
    t = xbc.shape[0]
    nc = t // CHUNK
    hpg = SSD_HEADS // SSD_GROUPS
    pick, place = _head_matrices()

    def body(xs_ref, dt_ref, cs_ref, csr_ref, b_ref, c_ref, dk_ref, a_ref, st_ref, dy_ref, pick_ref, place_ref,
             dxs_ref, ddt_ref, dadt_ref, db_ref, dc_ref, dd_ref, dsl_ref, dsc_ref, cd_ref):
        h = pl.program_id(0) * hpg + pl.program_id(1)
        x, dt, cs, bm, cm, li, si, lmat, g, cs_last, xdt, dec = _ssd_common(_pick_head(xs_ref, pick_ref, h), dt_ref,
                                                                           cs_ref, csr_ref, b_ref, c_ref, nc, h)
        dy = _pick_head(dy_ref, pick_ref, h).reshape(nc, CHUNK, SSD_P)
        dyb = dy.astype(BF16)
        xdtb = xdt.astype(BF16)
        sprev = st_ref[0]
        sprevb = sprev.astype(BF16)
        cdec = jnp.exp(cs_last)
        ecs = jnp.exp(cs)
        dw = (ecs * dy).astype(BF16)
        wmat = _bdot(cm, sprevb, 2, 1)
        dcs = jnp.sum(dy * ecs * wmat, axis=2, keepdims=True)
        dcm = _bdot(dw, sprevb, 2, 2)
        dsl_ref[...] = _bdot(cm, dw, 1, 1)
        cd_ref[...] = cdec

        def step(k, ds):
            c = nc - 1 - k
            dsc_ref[c] = ds
            return ds * cd_ref[c] + dsl_ref[c]

        lax.fori_loop(0, nc, step, jnp.zeros((SSD_N, SSD_P), F32))
        dsc = dsc_ref[...]
        dscb = dsc.astype(BF16)
        d_last = jnp.sum(jnp.sum(dsc * sprev, axis=1, keepdims=True) * cdec, axis=2, keepdims=True)
        z = dec * xdt
        dbm = _bdot(z.astype(BF16), dscb, 2, 2)
        dz = _bdot(bm, dscb, 2, 1)
        dxdt = dec * dz
        t2 = jnp.sum(dz * z, axis=2, keepdims=True)
        dcs = dcs - t2
        d_last = d_last + jnp.sum(t2, axis=1, keepdims=True)
        m = g * lmat
        mb = m.astype(BF16)
        dm = _bdot(dyb, xdtb, 2, 2)
        dxdt = dxdt + _bdot(mb, dyb, 1, 1)
        dseg = dm * m
        dcs = dcs + jnp.sum(dseg, axis=2, keepdims=True)
        ones = jnp.ones((nc, CHUNK, SSD_P), F32)
        dcs = dcs - _bdot_sum(dseg, ones, 1, 1, 0)
        dg = (dm * lmat).astype(BF16)
        dcm = dcm + _bdot(dg, bm, 2, 1)
        dbm = dbm + _bdot(dg, cm, 1, 1)
        dcs = dcs + jnp.where(li[:, :, :SSD_P] == CHUNK - 1, d_last, 0.0)
        triu = jnp.where(li <= si, 1.0, 0.0).astype(F32)
        dadt = _bdot_sum(triu, dcs, 2, 1, 1)
        dk = dk_ref[0]
        _place_head(dxs_ref, (dxdt * dt + dk * dy).reshape(t, SSD_P), place_ref, h)
        ddt = jnp.sum(dxdt * x, axis=2, keepdims=True) + dadt * a_ref[0]
        mine = lax.broadcasted_iota(jnp.int32, (t, LANES), 1) == h

        @pl.when(h == 0)
        def _():
            ddt_ref[...] = jnp.zeros_like(ddt_ref)
            dadt_ref[...] = jnp.zeros_like(dadt_ref)

        ddt_ref[...] += jnp.where(mine, jnp.max(ddt, axis=2, keepdims=True).reshape(t, 1), 0.0)
        dadt_ref[...] += jnp.where(mine, jnp.max(dadt, axis=2, keepdims=True).reshape(t, 1), 0.0)
        dd_ref[0] = jnp.sum(jnp.sum(dy * x, axis=1, keepdims=True), axis=0)

        @pl.when(pl.program_id(1) == 0)
        def _():
            db_ref[...] = jnp.zeros_like(db_ref)
            dc_ref[...] = jnp.zeros_like(dc_ref)

        db_ref[...] += dbm.reshape(t, SSD_N)
        dc_ref[...] += dcm.reshape(t, SSD_N)

    head = pl.BlockSpec((1, t, SSD_P), lambda gi, hi: (gi * hpg + hi, 0, 0))
    pair = pl.BlockSpec((t, 2 * SSD_P), lambda gi, hi: (0, (gi * hpg + hi) // 2))
    grp = pl.BlockSpec((t, SSD_N), lambda gi, hi: (0, gi))
    lane = pl.BlockSpec((1, 1, SSD_P), lambda gi, hi: (gi * hpg + hi, 0, 0))
    rows = pl.BlockSpec((t, LANES), lambda gi, hi: (0, 0))
    nxb = D_SSM // SSD_N
    dxs, ddt, dadt, db, dc, dd = pl.pallas_call(
        body, name=name, grid=(SSD_GROUPS, hpg),
        in_specs=[pair, rows, rows, pl.BlockSpec((1, nc, 1, CHUNK), lambda gi, hi: (gi * hpg + hi, 0, 0, 0)),
                  pl.BlockSpec((t, SSD_N), lambda gi, hi: (0, nxb + gi)),
                  pl.BlockSpec((t, SSD_N), lambda gi, hi: (0, nxb + SSD_GROUPS + gi)), lane, lane,
                  pl.BlockSpec((1, nc, SSD_N, SSD_P), lambda gi, hi: (gi * hpg + hi, 0, 0, 0)), pair,
                  pl.BlockSpec((2, 2 * SSD_P, SSD_P), lambda gi, hi: (0, 0, 0)),
                  pl.BlockSpec((2, SSD_P, 2 * SSD_P), lambda gi, hi: (0, 0, 0))],
        out_specs=[pair, rows, rows, grp, grp, lane],
        out_shape=[jax.ShapeDtypeStruct((t, D_SSM), F32)] + [jax.ShapeDtypeStruct((t, LANES), F32)] * 2
        + [jax.ShapeDtypeStruct((t, SSD_GROUPS * SSD_N), F32)] * 2
        + [jax.ShapeDtypeStruct((SSD_HEADS, 1, SSD_P), F32)],
        scratch_shapes=[pltpu.VMEM((nc, SSD_N, SSD_P), F32), pltpu.VMEM((nc, SSD_N, SSD_P), F32),
                        pltpu.VMEM((nc, 1, SSD_P), F32)],
        compiler_params=_cp("arbitrary", "arbitrary"),
    )(xbc, dt128, cs128, cs_row, xbc, xbc, dskip_h, a_h, states, dy, pick, place)
    return jnp.concatenate([dxs, db, dc], axis=1), ddt, dadt, dd


def _ssd_gate_fwd(y, proj, w, *, tr=256, name):
    t = y.shape[0]
    gw = D_SSM // SSD_GROUPS

    def body(y_ref, z_ref, w_ref, o_ref):
        v = y_ref[...] * _silu(z_ref[...])
        for gi in range(SSD_GROUPS):
            vg = v[:, gi * gw:(gi + 1) * gw]
            r = lax.rsqrt(jnp.mean(vg * vg, axis=-1, keepdims=True) + NORM_EPS)
            o_ref[:, gi * gw:(gi + 1) * gw] = (vg * r * w_ref[:, gi * gw:(gi + 1) * gw]).astype(BF16)

    blk = pl.BlockSpec((tr, D_SSM), lambda i: (i, 0))
    return pl.pallas_call(
        body, name=name, grid=(t // tr,), in_specs=[blk, blk, pl.BlockSpec((1, D_SSM), lambda i: (0, 0))],
        out_specs=blk, out_shape=jax.ShapeDtypeStruct((t, D_SSM), BF16), compiler_params=_cp("parallel"),
    )(y, proj, w)


def _ssd_gate_bwd(y, proj, w, dcat, *, tr=256, name):
    t = y.shape[0]
    gw = D_SSM // SSD_GROUPS

    def body(y_ref, z_ref, w_ref, d_ref, dy_ref, dz_ref, dw_ref):
        yv, zv, dv = y_ref[...], z_ref[...], d_ref[...].astype(F32)
        sz = _silu(zv)
        v = yv * sz

        @pl.when(pl.program_id(0) == 0)
        def _():
            dw_ref[...] = jnp.zeros_like(dw_ref)

        for gi in range(SSD_GROUPS):
            sl = slice(gi * gw, (gi + 1) * gw)
            vg, dg = v[:, sl], dv[:, sl]
            r = lax.rsqrt(jnp.mean(vg * vg, axis=-1, keepdims=True) + NORM_EPS)
            vh = vg * r
            gg = dg * w_ref[:, sl]
            dvg = r * (gg - vh * jnp.mean(gg * vh, axis=-1, keepdims=True))
            dy_ref[:, sl] = dvg * sz[:, sl]
            dz_ref[:, sl] = (dvg * yv[:, sl] * _dsilu(zv[:, sl])).astype(BF16)
            dw_ref[:, sl] += jnp.sum(dg * vh, axis=0, keepdims=True)

    blk = pl.BlockSpec((tr, D_SSM), lambda i: (i, 0))
    row = pl.BlockSpec((1, D_SSM), lambda i: (0, 0))
    return pl.pallas_call(
        body, name=name, grid=(t // tr,), in_specs=[blk, blk, row, blk], out_specs=[blk, blk, row],
        out_shape=[jax.ShapeDtypeStruct((t, D_SSM), F32), jax.ShapeDtypeStruct((t, D_SSM), BF16),
                   jax.ShapeDtypeStruct((1, D_SSM), F32)],
        compiler_params=_cp("arbitrary"),
    )(y, proj, w, dcat)


def _pad_lanes(v):
    return jnp.pad(v, ((0, 0), (0, LANES - v.shape[1])))


def _per_head(v128, t):
    return jnp.broadcast_to(v128[:, :SSD_HEADS].T[:, :, None], (SSD_HEADS, t, SSD_P))


def _ssd_forward(proj, conv_w, conv_b, dt_bias, a_log, d_skip, ssd_norm_w):
    t = proj.shape[0]
    nc = t // CHUNK
    xbc = _conv_act_fwd(proj, conv_w, conv_b, kw=SSD_CONV, glu=False, tc=512, coff=OFF_XBC // 512,
                        ncols=SSD_CONV_DIM, out_dtype=F32, name="ssd_conv_fwd")
    bias128, alog128 = _pad_lanes(dt_bias), _pad_lanes(a_log)
    dt128, cs128, a128 = _ssd_prep(proj, bias128, alog128, name="ssd_prep")
    cs_row = cs128[:, :SSD_HEADS].T.reshape(SSD_HEADS, nc, 1, CHUNK)
    dskip_h = jnp.broadcast_to(d_skip[0][:, None, None], (SSD_HEADS, 1, SSD_P))
    a_h = jnp.broadcast_to(a128[0, :SSD_HEADS][:, None, None], (SSD_HEADS, 1, SSD_P))
    y, states = _ssd_fwd(xbc, dt128, cs128, cs_row, dskip_h, name="ssd_scan_fwd")
    y_ssd = _ssd_gate_fwd(y, proj, ssd_norm_w, name="ssd_gate_fwd")
    saved = (proj, conv_w, conv_b, ssd_norm_w, bias128, dt128, a128, cs128, cs_row, xbc, dskip_h, a_h, states, y)
    return y_ssd, saved


def _ssd_backward(saved, dcat):
    proj, conv_w, conv_b, ssd_norm_w, bias128, dt128, a128, cs128, cs_row, xbc, dskip_h, a_h, states, y = saved
    dy, dz, d_norm_w = _ssd_gate_bwd(y, proj, ssd_norm_w, dcat, name="ssd_gate_bwd")
    dxc, ddt128, dadt128, dd_h = _ssd_bwd(xbc, dt128, cs128, cs_row, dskip_h, a_h, states, dy, name="ssd_scan_bwd")
    dxbc, d_conv_w, d_conv_b = _conv_act_bwd(proj, conv_w, conv_b, dxc, kw=SSD_CONV, glu=False, tc=512,
                                             coff=OFF_XBC // 512, ncols=SSD_CONV_DIM, name="ssd_conv_bwd")
    d_raw, d_bias, d_alog, d_dskip = _ssd_prep_bwd(ddt128, dadt128, proj, bias128, dt128, a128,
                                                   dd_h.reshape(SSD_HEADS, SSD_P), name="ssd_prep_bwd")
    return (dz, dxbc, d_raw, d_norm_w, d_conv_w, d_conv_b, d_bias[:, :SSD_HEADS], d_alog[:, :SSD_HEADS],
            d_dskip.reshape(1, SSD_HEADS))


def _rope_tables(positions):
    inv_freq = ROPE_THETA ** (-jnp.arange(0, MLA_ROPE, 2, dtype=F32) / MLA_ROPE)
    ang = positions[0].astype(F32)[:, None] * inv_freq
    cos, sin = jnp.cos(ang), jnp.sin(ang)
    z = jnp.zeros_like(cos)
    return jnp.stack([jnp.concatenate([cos, cos, z, z], axis=1), jnp.concatenate([-sin, z, z, z], axis=1),
                      jnp.concatenate([z, sin, z, z], axis=1)])


def _latent_norms(proj, q_w, kv_w, *, tr=256, name):
    t = proj.shape[0]

    def body(q_ref, kv_ref, qw_ref, kvw_ref, qo_ref, kvo_ref):
        for x_ref, w_ref, o_ref in ((q_ref, qw_ref, qo_ref), (kv_ref, kvw_ref, kvo_ref)):
            xv = x_ref[...]
            r = lax.rsqrt(jnp.mean(xv * xv, axis=-1, keepdims=True) + NORM_EPS)
            o_ref[...] = (xv * r * w_ref[...]).astype(BF16)

    return pl.pallas_call(
        body, name=name, grid=(t // tr,),
        in_specs=[pl.BlockSpec((tr, MLA_Q_RANK), lambda i: (i, OFF_QA // MLA_Q_RANK)),
                  pl.BlockSpec((tr, MLA_KV_RANK), lambda i: (i, OFF_CKV // MLA_KV_RANK)),
                  pl.BlockSpec((1, MLA_Q_RANK), lambda i: (0, 0)), pl.BlockSpec((1, MLA_KV_RANK), lambda i: (0, 0))],
        out_specs=[pl.BlockSpec((tr, MLA_Q_RANK), lambda i: (i, 0)), pl.BlockSpec((tr, MLA_KV_RANK), lambda i: (i, 0))],
        out_shape=[jax.ShapeDtypeStruct((t, MLA_Q_RANK), BF16), jax.ShapeDtypeStruct((t, MLA_KV_RANK), BF16)],
        compiler_params=_cp("parallel"),
    )(proj, proj, q_w, kv_w)


def _latent_norms_bwd(proj, q_w, kv_w, dqn, dkvn, *, tr=256, name):
    t = proj.shape[0]

    def body(q_ref, kv_ref, qw_ref, kvw_ref, dq_ref, dkv_ref, dqa_ref, dqw_ref, dckv_ref, dkvw_ref):
        first = pl.program_id(0) == 0
        for x_ref, w_ref, dy_ref, dx_ref, dw_ref in ((q_ref, qw_ref, dq_ref, dqa_ref, dqw_ref),
                                                     (kv_ref, kvw_ref, dkv_ref, dckv_ref, dkvw_ref)):
            xv, dyv = x_ref[...], dy_ref[...]
            r = lax.rsqrt(jnp.mean(xv * xv, axis=-1, keepdims=True) + NORM_EPS)
            xh = xv * r
            g = dyv * w_ref[...]
            dx_ref[...] = (r * (g - xh * jnp.mean(g * xh, axis=-1, keepdims=True))).astype(BF16)

            @pl.when(first)
            def _(dw_ref=dw_ref):
                dw_ref[...] = jnp.zeros_like(dw_ref)

            dw_ref[...] += jnp.sum(dyv * xh, axis=0, keepdims=True)

    def rows(width, cblk=0):
        return pl.BlockSpec((tr, width), lambda i: (i, cblk))

    def row(width):
        return pl.BlockSpec((1, width), lambda i: (0, 0))

    return pl.pallas_call(
        body, name=name, grid=(t // tr,),
        in_specs=[rows(MLA_Q_RANK, OFF_QA // MLA_Q_RANK), rows(MLA_KV_RANK, OFF_CKV // MLA_KV_RANK), row(MLA_Q_RANK),
                  row(MLA_KV_RANK), rows(MLA_Q_RANK), rows(MLA_KV_RANK)],
        out_specs=[rows(MLA_Q_RANK), row(MLA_Q_RANK), rows(MLA_KV_RANK), row(MLA_KV_RANK)],
        out_shape=[jax.ShapeDtypeStruct((t, MLA_Q_RANK), BF16), jax.ShapeDtypeStruct((1, MLA_Q_RANK), F32),
                   jax.ShapeDtypeStruct((t, MLA_KV_RANK), BF16), jax.ShapeDtypeStruct((1, MLA_KV_RANK), F32)],
        compiler_params=_cp("arbitrary"),
    )(proj, proj, q_w, kv_w, dqn, dkvn)


def _mla_forward(proj, tabs, q_a_norm_w, wq_pad, kv_a_norm_w, wkv):
    qn, kvn = _latent_norms(proj, q_a_norm_w, kv_a_norm_w, name="latent_norms")
    q = _matmul(qn, wq_pad, name="q_b_proj")
    kv = _matmul(kvn, wkv, name="kv_b_proj")
    q3, k3, v3, vt4 = _mla_prep(q, kv, proj, tabs, name="mla_prep")
    o, lse = _attn_fwd(q3, k3, vt4, name="attn_fwd")
    return o, (proj, tabs, q_a_norm_w, wq_pad, kv_a_norm_w, wkv, qn, kvn, q3, k3, v3, o, lse)


def _mla_backward(saved, dcat):
    proj, tabs, q_a_norm_w, wq_pad, kv_a_norm_w, wkv, qn, kvn, q3, k3, v3, o, lse = saved
    dq3, dk3, dv3 = _attn_bwd(q3, k3, v3, o, dcat, lse, name="attn_bwd")
    dq, dkv, dkr = _mla_unprep(dq3, dk3, dv3, tabs, name="mla_unprep")
    d_wq = _matmul(qn, dq, ta=True, out_dtype=BF16, name="d_w_q_b")
    dqn = _matmul(dq, wq_pad, tb=True, name="d_qn")
    d_wkv = _matmul(kvn, dkv, ta=True, out_dtype=BF16, name="d_w_kv_b")
    dkvn = _matmul(dkv, wkv, tb=True, name="d_kvn")
    dq_a, d_qnw, dckv, d_kvnw = _latent_norms_bwd(proj, q_a_norm_w, kv_a_norm_w, dqn, dkvn, name="latent_norms_bwd")
    return dq_a, dckv, dkr, d_wq, d_wkv, d_qnw, d_kvnw


def _pad_w_q(w):
    r = w.shape[0]
    w3 = w.reshape(r, MLA_HEADS, MLA_NOPE + MLA_ROPE)
    return jnp.pad(w3, ((0, 0), (0, 0), (0, MLA_QK_PAD - MLA_NOPE - MLA_ROPE))).reshape(r, MLA_HEADS * MLA_QK_PAD)


def _unpad_w_q(w):
    r = w.shape[0]
    return w.reshape(r, MLA_HEADS, MLA_QK_PAD)[:, :, :MLA_NOPE + MLA_ROPE].reshape(r, MLA_HEADS * (MLA_NOPE + MLA_ROPE))


W_IN_SEGMENTS = ((0, D_SSM + SSD_CONV_DIM, 0), (D_SSM + SSD_CONV_DIM, D_SSM + SSD_CONV_DIM + SSD_HEADS, OFF_DT),
                 (D_SSM + SSD_CONV_DIM + SSD_HEADS, D_IN - MLA_ROPE, OFF_QA), (D_IN - MLA_ROPE, D_IN, OFF_KR))


def _pad_w_in_shards(g):
    n = g.shape[2]
    pieces, at = [], 0
    for lo, hi, start in sorted(W_IN_SEGMENTS, key=lambda seg: seg[2]):
        if start > at:
            pieces.append(jnp.zeros((g.shape[1], start - at), g.dtype))
        for j in range(N_DEV):
            a, b = max(lo, j * n), min(hi, (j + 1) * n)
            if a < b:
                pieces.append(g[j][:, a - j * n:b - j * n])
        at = start + hi - lo
    pieces.append(jnp.zeros((g.shape[1], D_IN_PAD - at), g.dtype))
    return jnp.concatenate(pieces, axis=1)


def _unpad_w_in_shards(w):
    n = D_IN // N_DEV
    shards = []
    for j in range(N_DEV):
        pieces = []
        for lo, hi, start in W_IN_SEGMENTS:
            a, b = max(lo, j * n), min(hi, (j + 1) * n)
            if a < b:
                pieces.append(w[:, start + a - lo:start + b - lo])
        shards.append(jnp.concatenate(pieces, axis=1) if len(pieces) > 1 else pieces[0])
    return jnp.stack(shards)


WEIGHTS = ['mix_norm_w', 'w_in', 'conv_w', 'conv_b', 'dt_bias', 'a_log', 'd_skip', 'ssd_norm_w', 'q_a_norm_w', 'w_q_b',
           'kv_a_norm_w', 'w_kv_b', 'w_out', 'ffn_norm_w', 'w_ffn_up', 'ffn_conv_w', 'ffn_conv_b', 'w_ffn_down',
           'ple_norm_w', 'w_ple_gate', 'b_ple_gate', 'w_ple_proj', 'ple_post_norm_w', 'final_norm_w']
BIG = ['w_in', 'w_q_b', 'w_kv_b', 'w_out', 'w_ffn_up', 'w_ffn_down', 'w_ple_gate', 'w_ple_proj']
COL_SHARDED = ('w_in', 'w_q_b', 'w_kv_b', 'w_ffn_up', 'w_ple_proj')
CONV = ['conv_w', 'ffn_conv_w']
REPL = [n for n in WEIGHTS if n not in BIG and n not in CONV]
FFN_INV = tuple(int(i) for i in np.argsort(FFN_PERM))


def _cat_cols(g):
    return jnp.concatenate([g[j] for j in range(N_DEV)], axis=1)


def _split_cols(w):
    n = w.shape[1] // N_DEV
    return jnp.stack([w[:, j * n:(j + 1) * n] for j in range(N_DEV)])


def _interleave(v):
    r = v.shape[0]
    return v.reshape(r, N_DEV, FFN_TC)[:, jnp.array(FFN_PERM)].reshape(r, N_DEV * FFN_TC)


def _deinterleave(v):
    r = v.shape[0]
    return v.reshape(r, N_DEV, FFN_TC)[:, jnp.array(FFN_INV)].reshape(r, N_DEV * FFN_TC)


def _assemble_weights(g):
    layout = {
        'w_in': _pad_w_in_shards,
        'w_q_b': lambda v: _pad_w_q(_cat_cols(v)),
        'w_kv_b': _cat_cols,
        'w_out': lambda v: v.reshape(D_MODEL, D_MODEL),
        'w_ffn_up': lambda v: v,
        'w_ffn_down': lambda v: v.reshape(D_FF, D_MODEL),
        'w_ple_gate': lambda v: v.reshape(D_MODEL, D_MODEL),
        'w_ple_proj': _cat_cols,
        'conv_w': _cat_cols,
        'ffn_conv_w': lambda v: _interleave(_cat_cols(v)),
    }
    return {n: layout[n](v) for n, v in g.items()}


WEIGHT_GROUPS = {'a': ['w_in', 'w_q_b', 'w_kv_b', 'conv_w'], 'b': ['w_out', 'w_ffn_up', 'ffn_conv_w'],
                 'c': ['w_ffn_down', 'w_ple_gate', 'w_ple_proj']}
GRAD_GROUPS = {'p': ['w_ple_proj', 'w_ple_gate', 'w_ffn_down'], 'r': ['w_ffn_up'], 's': ['w_out'],
               't': ['w_q_b', 'w_kv_b', 'w_in']}


def _ffn_perm(j):
    return (j % 2) * (N_DEV // 2) + j // 2


def _local_step(x, p, tabs, get_w, s, target, emit, relay, settle):
    t = x.shape[0]
    s = dict(s)
    half = D_MODEL // 2
    up_cols = 2 * D_FF
    ffn_conv_b = _interleave(s['ffn_conv_b'])
    w = dict(get_w('a', None))
    h = _rmsnorm_fwd(x, s['mix_norm_w'], width=D_MODEL, name="mix_norm")
    proj = _matmul(h, w['w_in'], name="in_proj")
    y_ssd, ssd_saved = _ssd_forward(proj, w['conv_w'], s['conv_b'], s['dt_bias'], s['a_log'], s['d_skip'],
                                    s['ssd_norm_w'])
    o, mla_saved = _mla_forward(proj, tabs, s['q_a_norm_w'], w['w_q_b'], s['kv_a_norm_w'], w['w_kv_b'])
    tk_o, tn_o = _tile(half, MM_TK), _tile(D_MODEL, MM_TILE)
    w.update(get_w('b', o))
    x1 = _matmul(y_ssd, w['w_out'], add=x, mnk=(t, D_MODEL, half), name="out_proj_ssd")
    x1 = _matmul(o, w['w_out'], add=x1, mnk=(t, D_MODEL, half), name="out_proj_mla",
                 b_spec=pl.BlockSpec((tk_o, tn_o), lambda i, j, kk: (kk + half // tk_o, j)))
    hf = _rmsnorm_fwd(x1, s['ffn_norm_w'], width=D_MODEL, name="ffn_norm")
    tk_u = _tile(D_MODEL, MM_TK)
    u = _matmul(hf, w['w_ffn_up'], mnk=(t, up_cols, D_MODEL), tn=FFN_TC, name="ffn_up",
                b_spec=pl.BlockSpec((1, tk_u, FFN_TC), lambda i, j, kk: (_ffn_perm(j), kk, 0)))
    act = _conv_act_fwd(u, w['ffn_conv_w'], ffn_conv_b, kw=FFN_CONV, glu=True, tc=2 * FFN_TC, coff=0, ncols=up_cols,
                        out_dtype=BF16, name="ffn_act")
    w.update(get_w('c', act))
    x2 = _matmul(act, w['w_ffn_down'], add=x1, name="ffn_down")
    hp = _rmsnorm_fwd(x2, s['ple_norm_w'], width=D_MODEL, name="ple_norm")
    gl = _matmul(hp, w['w_ple_gate'], bias=s['b_ple_gate'], name="ple_gate")
    pe = _matmul(p, w['w_ple_proj'], name="ple_proj")
    loss, dx3, d_final, dgl, d_bgate, dpe, d_post = _ple_loss(x2, gl, pe, s['ple_post_norm_w'], s['final_norm_w'],
                                                              target, name="ple_loss")
    d_wproj = _matmul(p, dpe, ta=True, out_dtype=BF16, name="d_w_ple_proj")
    d_wgate = _matmul(hp, dgl, ta=True, out_dtype=BF16, name="d_w_ple_gate")
    dhp = _matmul(dgl, w['w_ple_gate'], tb=True, name="d_ple_normed")
    dx2, d_plenorm, dx2b = _rmsnorm_bwd(x2, s['ple_norm_w'], dhp, dx3, width=D_MODEL, also_bf16=True,
                                        name="ple_norm_bwd")
    dact = _matmul(dx2b, w['w_ffn_down'], tb=True, name="d_ffn_act")
    d_wdown = _matmul(act, dx2b, ta=True, out_dtype=BF16, name="d_w_ffn_down")
    zz = emit('p', {'w_ple_proj': _split_cols(d_wproj), 'w_ple_gate': d_wgate.reshape(N_DEV, D_MODEL // N_DEV, D_MODEL),
                    'w_ffn_down': d_wdown.reshape(N_DEV, D_FF // N_DEV, D_MODEL)})
    du, d_fconv_w, d_fconv_b = _conv_act_bwd(u, w['ffn_conv_w'], ffn_conv_b + zz, dact, kw=FFN_CONV, glu=True,
                                             tc=2 * FFN_TC, coff=0, ncols=up_cols, name="ffn_act_bwd")
    zz = zz + relay('p', du)
    tm_u = _tile(D_MODEL, MM_TILE)
    d_wup = _matmul(hf, du, ta=True, out_dtype=BF16, mnk=(D_MODEL, up_cols, t), tn=FFN_TC, name="d_w_ffn_up",
                    o_spec=pl.BlockSpec((1, tm_u, FFN_TC), lambda i, j, kk: (_ffn_perm(j), i, 0)),
                    o_shape=(N_DEV, D_MODEL, FFN_TC))
    zz = zz + emit('r', {'w_ffn_up': d_wup})
    zero_row = jnp.zeros((1, D_MODEL), F32)
    dhf = _matmul(du, w['w_ffn_up'], tb=True, mnk=(t, D_MODEL, up_cols), tm=t, tk=FFN_TC, name="d_ffn_normed",
                  bias=zero_row + zz,
                  b_spec=pl.BlockSpec((1, tn_o, FFN_TC), lambda i, j, kk: (_ffn_perm(kk), j, 0)))
    zz = zz + relay('r', dhf) + settle('p')
    dx1, d_ffnnorm, dx1b = _rmsnorm_bwd(x1, s['ffn_norm_w'] + zz, dhf, dx2, width=D_MODEL, also_bf16=True,
                                        name="ffn_norm_bwd")
    dcat = _matmul(dx1b, w['w_out'], tb=True, name="d_mixed")
    d_wout = jnp.concatenate([_matmul(y_ssd, dx1b, ta=True, out_dtype=BF16, name="d_w_out_ssd"),
                              _matmul(o, dx1b, ta=True, out_dtype=BF16, name="d_w_out_mla")], axis=0)
    zz = zz + emit('s', {'w_out': d_wout.reshape(N_DEV, D_MODEL // N_DEV, D_MODEL)})
    ssd_saved = ssd_saved[:3] + (ssd_saved[3] + zz,) + ssd_saved[4:]
    dz, dxbc, d_raw, d_ssdnorm, d_conv_w, d_conv_b, d_dtb, d_alog, d_dskip = _ssd_backward(ssd_saved, dcat)
    zz = zz + relay('s', dz)
    mla_saved = mla_saved[:-1] + (mla_saved[-1] + zz,)
    dq_a, dckv, dkr, d_wq, d_wkv, d_qnorm, d_kvnorm = _mla_backward(mla_saved, dcat)
    d_raw = (d_raw + settle('r')).astype(BF16)
    dproj = jnp.concatenate([dz, dxbc, dq_a, dckv, dkr, d_raw], axis=1)
    d_win = _matmul(h, dproj, ta=True, out_dtype=BF16, name="d_w_in")
    zz = emit('t', {'w_in': _unpad_w_in_shards(d_win), 'w_q_b': _split_cols(_unpad_w_q(d_wq)),
                    'w_kv_b': _split_cols(d_wkv)}) + settle('s')
    dh = _matmul(dproj, w['w_in'], tb=True, bias=zero_row + zz, name="d_in_normed")
    zz = relay('t', dh)
    dx, d_mixnorm = _rmsnorm_bwd(x, s['mix_norm_w'] + zz, dh, dx1, width=D_MODEL, name="mix_norm_bwd")
    conv = {'conv_w': d_conv_w, 'ffn_conv_w': _deinterleave(d_fconv_w)}
    vec = {
        'mix_norm_w': d_mixnorm, 'conv_b': d_conv_b, 'dt_bias': d_dtb, 'a_log': d_alog, 'd_skip': d_dskip,
        'ssd_norm_w': d_ssdnorm, 'q_a_norm_w': d_qnorm, 'kv_a_norm_w': d_kvnorm, 'ffn_norm_w': d_ffnnorm,
        'ffn_conv_b': _deinterleave(d_fconv_b), 'ple_norm_w': d_plenorm, 'b_ple_gate': d_bgate,
        'ple_post_norm_w': d_post, 'final_norm_w': d_final,
    }
    return loss, dx, conv, vec


MESH = pl.DeviceIdType.MESH
FLIPS = ((0, 0, 1), (1, 0, 0), (0, 1, 0), (1, 1, 0), (1, 0, 1), (0, 1, 1), (1, 1, 1))


def _exchange(items, *, gather, name):
    n = len(items)

    def body(*refs):
        ins, outs = refs[:n], refs[n:2 * n]
        send_sems, recv_sems, local_sems = refs[2 * n:]
        x, y, c = lax.axis_index("x"), lax.axis_index("y"), lax.axis_index("c")
        me = 4 * x + 2 * y + c
        peers = [(jnp.where(fx, 1 - x, x), jnp.where(fy, 1 - y, y), jnp.where(fc, 1 - c, c)) for fx, fy, fc in FLIPS]
        slot = [4 * px + 2 * py + pc for px, py, pc in peers]
        local, sends = [], []
        for wi in range(n):
            cp = pltpu.make_async_copy(ins[wi] if gather else ins[wi].at[me], outs[wi].at[me], local_sems.at[wi])
            cp.start()
            local.append(cp)
            for k, peer in enumerate(peers):
                cp = pltpu.make_async_remote_copy(
                    src_ref=ins[wi] if gather else ins[wi].at[slot[k]], dst_ref=outs[wi].at[me],
                    send_sem=send_sems.at[k, wi], recv_sem=recv_sems.at[k, wi], device_id=peer, device_id_type=MESH)
                cp.start()
                sends.append(cp)
        for wi in range(n):
            for k, peer in enumerate(peers):
                pltpu.make_async_remote_copy(
                    src_ref=outs[wi].at[slot[k]], dst_ref=outs[wi].at[slot[k]], send_sem=send_sems.at[k, wi],
                    recv_sem=recv_sems.at[k, wi], device_id=peer, device_id_type=MESH).wait_recv()
        for cp in sends:
            cp.wait_send()
        for cp in local:
            cp.wait()

    hbm = pl.BlockSpec(memory_space=pltpu.HBM)
    out_shape = [jax.ShapeDtypeStruct(((N_DEV,) + v.shape) if gather else v.shape, v.dtype) for v in items]
    return pl.pallas_call(
        body, name=name, in_specs=[hbm] * n, out_specs=[hbm] * n, out_shape=out_shape,
        scratch_shapes=[pltpu.SemaphoreType.DMA((len(FLIPS), n)), pltpu.SemaphoreType.DMA((len(FLIPS), n)),
                        pltpu.SemaphoreType.DMA((n,))],
    )(*items)


HBM_SPEC = pl.BlockSpec(memory_space=pltpu.HBM)
SEM_SPEC = pl.BlockSpec(memory_space=pltpu.SEMAPHORE)
EFFECT = pltpu.SideEffectType.DATAFLOW_SIDE_EFFECTING


def _split_start(bufs, ncopies, plan, *, name):
    nb = len(bufs)

    def body(*refs):
        send_sems, recv_sems, token = refs[nb], refs[nb + 1], refs[2 * nb + 2]
        for i, (src, dst, peer, _) in enumerate(plan(refs[:nb])):
            pltpu.make_async_remote_copy(src_ref=src, dst_ref=dst, send_sem=send_sems.at[i], recv_sem=recv_sems.at[i],
                                         device_id=peer, device_id_type=MESH).start()
        token[...] = jnp.zeros_like(token)

    res = pl.pallas_call(
        body, name=name, in_specs=[HBM_SPEC] * nb,
        out_specs=[SEM_SPEC, SEM_SPEC] + [HBM_SPEC] * nb + [pl.BlockSpec(memory_space=pltpu.VMEM)],
        out_shape=[pltpu.SemaphoreType.DMA((ncopies,)), pltpu.SemaphoreType.DMA((ncopies,))]
        + [pltpu.HBM(v.shape, v.dtype) for v in bufs] + [jax.ShapeDtypeStruct((HALO, LANES), F32)],
        input_output_aliases={i: 2 + i for i in range(nb)},
        compiler_params=pltpu.CompilerParams(has_side_effects=EFFECT),
    )(*[pltpu.with_memory_space_constraint(v, pltpu.HBM) for v in bufs])
    return (res[0], res[1], list(res[2:2 + nb])), res[2 + nb]


def _split_wait(started, after, plan, local_plan, *, name):
    send_sems, recv_sems, bufs = started
    nb = len(bufs)
    nlocal = len(local_plan(bufs))

    def body(*refs):
        send_sems, recv_sems = refs[nb], refs[nb + 1]
        local_sems = refs[2 * nb + 3]
        local = []
        for j, (src, dst) in enumerate(local_plan(refs[:nb])):
            cp = pltpu.make_async_copy(src, dst, local_sems.at[j])
            cp.start()
            local.append(cp)
        for i, (src, _, peer, incoming) in enumerate(plan(refs[:nb])):
            cp = pltpu.make_async_remote_copy(src_ref=src, dst_ref=incoming, send_sem=send_sems.at[i],
                                              recv_sem=recv_sems.at[i], device_id=peer, device_id_type=MESH)
            cp.wait_send()
            cp.wait_recv()
        for cp in local:
            cp.wait()

    res = pl.pallas_call(
        body, name=name, in_specs=[HBM_SPEC] * nb + [SEM_SPEC, SEM_SPEC, pl.BlockSpec(memory_space=pl.ANY)],
        out_specs=[HBM_SPEC] * nb, out_shape=[pltpu.HBM(v.shape, v.dtype) for v in bufs],
        input_output_aliases={i: i for i in range(nb)},
        scratch_shapes=[pltpu.SemaphoreType.DMA((max(nlocal, 1),))],
        compiler_params=pltpu.CompilerParams(has_side_effects=EFFECT),
    )(*bufs, send_sems, recv_sems, after)
    return list(res)


def _hold(values, after, *, name):
    n = len(values)

    def body(*refs):
        del refs

    return list(pl.pallas_call(
        body, name=name, in_specs=[HBM_SPEC] * n + [pl.BlockSpec(memory_space=pl.ANY)], out_specs=[HBM_SPEC] * n,
        out_shape=[pltpu.HBM(v.shape, v.dtype) for v in values], input_output_aliases={i: i for i in range(n)},
    )(*values, after))


def _place():
    x, y, c = lax.axis_index("x"), lax.axis_index("y"), lax.axis_index("c")
    others = [((1 - x, y, c), 2 * (1 - x) + y), ((x, 1 - y, c), 2 * x + 1 - y), ((1 - x, 1 - y, c), 2 * (1 - x) + 1 - y)]
    return 4 * x + 2 * y + c, 2 * x + y, c, (x, y, 1 - c), others


def _gather1_plan(n):
    def plan(refs):
        me, _, _, sibling, others = _place()
        out = []
        for wi in range(n):
            item, land = refs[wi], refs[n + wi]
            out.append((item, land.at[me], sibling, land.at[me + 1 - 2 * lax.axis_index("c")]))
            for peer, chip in others:
                out.append((item, land.at[me], peer, land.at[2 * chip + lax.axis_index("c")]))
        return out

    return plan


def _gather1_local(n):
    def plan(refs):
        me = _place()[0]
        return [(refs[wi], refs[n + wi].at[me]) for wi in range(n)]

    return plan


def _gather2_plan(n):
    def plan(refs):
        _, _, c, sibling, others = _place()
        out = []
        for wi in range(n):
            land = refs[wi]
            for _, chip in others:
                out.append((land.at[2 * chip + c], land.at[2 * chip + c], sibling, land.at[2 * chip + 1 - c]))
        return out

    return plan


def _gather_start(items, *, name):
    lands = [lax.empty((N_DEV,) + v.shape, v.dtype) for v in items]
    return _split_start(items + lands, 4 * len(items), _gather1_plan(len(items)), name=name)


def _gather_forward(started, after, *, name):
    n = len(started[2]) // 2
    bufs = _split_wait(started, after, _gather1_plan(n), _gather1_local(n), name=name + "_wait")
    return _split_start(bufs[n:], 3 * n, _gather2_plan(n), name=name + "_start")


def _gather_finish(started, after, *, name):
    n = len(started[2])
    return _split_wait(started, after, _gather2_plan(n), lambda refs: [], name=name)


def _handshake(peers):
    barrier = pltpu.get_barrier_semaphore()
    for peer in peers:
        pl.semaphore_signal(barrier, inc=1, device_id=peer, device_id_type=MESH)
    pl.semaphore_wait(barrier, len(peers))


def _remote(src, dst, send_sem, recv_sem, peer):
    return pltpu.make_async_remote_copy(src_ref=src, dst_ref=dst, send_sem=send_sem, recv_sem=recv_sem, device_id=peer,
                                        device_id_type=MESH)


def _sequencer_gather(items, *, collective_id, name):
    n = len(items)
    srcs = [jax.new_ref(v, memory_space=pltpu.MemorySpace.HBM) for v in items]
    lands = [jax.empty_ref(jax.ShapeDtypeStruct((N_DEV,) + v.shape, v.dtype), memory_space=pltpu.MemorySpace.HBM)
             for v in items]
    dma = pltpu.SemaphoreType.DMA

    @pl.kernel(mesh=plsc.ScalarSubcoreMesh(axis_name="sequencer", num_cores=1), name=name,
               scratch_types=(dma((4 * n,)), dma((4 * n,)), dma((3 * n,)), dma((3 * n,)), dma((n,))),
               compiler_params=pltpu.CompilerParams(collective_id=collective_id))
    def launch(send1, recv1, send2, recv2, local_sems):
        _, _, _, sibling, others = _place()
        _handshake([sibling] + [peer for peer, _ in others])
        hop1 = _gather1_plan(n)(srcs + lands)
        hop2 = _gather2_plan(n)(lands)
        local = [pltpu.make_async_copy(src, dst, local_sems.at[j])
                 for j, (src, dst) in enumerate(_gather1_local(n)(srcs + lands))]
        for cp in local:
            cp.start()
        for i, (src, dst, peer, _) in enumerate(hop1):
            _remote(src, dst, send1.at[i], recv1.at[i], peer).start()
        for wi in range(n):
            for j in range(3):
                i1, i2 = 4 * wi + 1 + j, 3 * wi + j
                src, _, peer, incoming = hop1[i1]
                _remote(src, incoming, send1.at[i1], recv1.at[i1], peer).wait_recv()
                src, dst, peer, _ = hop2[i2]
                _remote(src, dst, send2.at[i2], recv2.at[i2], peer).start()
        for wi in range(n):
            src, _, peer, incoming = hop1[4 * wi]
            _remote(src, incoming, send1.at[4 * wi], recv1.at[4 * wi], peer).wait_recv()
        for i, (src, _, peer, incoming) in enumerate(hop2):
            cp = _remote(src, incoming, send2.at[i], recv2.at[i], peer)
            cp.wait_send()
            cp.wait_recv()
        for i, (src, dst, peer, _) in enumerate(hop1):
            _remote(src, dst, send1.at[i], recv1.at[i], peer).wait_send()
        for cp in local:
            cp.wait()

    launch()
    return [land[...] for land in lands]


def _sequencer_exchange(sources, land_shapes, ncopies, plan, local_plan, peers, *, collective_id, name):
    srcs = [jax.new_ref(v, memory_space=pltpu.MemorySpace.HBM) for v in sources]
    lands = [jax.empty_ref(s, memory_space=pltpu.MemorySpace.HBM) for s in land_shapes]
    nlocal = len(local_plan(srcs + lands))
    dma = pltpu.SemaphoreType.DMA

    @pl.kernel(mesh=plsc.ScalarSubcoreMesh(axis_name="sequencer", num_cores=1), name=name,
               scratch_types=(dma((ncopies,)), dma((ncopies,)), dma((max(nlocal, 1),))),
               compiler_params=pltpu.CompilerParams(collective_id=collective_id))
    def launch(send_sems, recv_sems, local_sems):
        _handshake(peers(_place()))
        copies = plan(srcs + lands)
        local = [pltpu.make_async_copy(src, dst, local_sems.at[j])
                 for j, (src, dst) in enumerate(local_plan(srcs + lands))]
        for cp in local:
            cp.start()
        for i, (src, dst, peer, _) in enumerate(copies):
            _remote(src, dst, send_sems.at[i], recv_sems.at[i], peer).start()
        for i, (src, _, peer, incoming) in enumerate(copies):
            cp = _remote(src, incoming, send_sems.at[i], recv_sems.at[i], peer)
            cp.wait_send()
            cp.wait_recv()
        for cp in local:
            cp.wait()

    launch()
    return [land[...] for land in lands]


def _sequencer_scatter_hop2(sums, *, collective_id, name):
    n = len(sums)
    shapes = [jax.ShapeDtypeStruct(v.shape, v.dtype) for v in sums]
    return _sequencer_exchange(sums, shapes, 3 * n, _scatter2_plan(n), _scatter2_local(n),
                               lambda place: [peer for peer, _ in place[4]], collective_id=collective_id, name=name)


N_CHIP = N_DEV // 2


def _scatter1_plan(n):
    def plan(refs):
        _, _, c, sibling, _ = _place()
        out = []
        for wi in range(n):
            parts, half = refs[wi], refs[n + wi]
            for chip in range(N_CHIP):
                out.append((parts.at[2 * chip + 1 - c], half.at[chip], sibling, half.at[chip]))
        return out

    return plan


def _scatter2_plan(n):
    def plan(refs):
        _, my_chip, _, _, others = _place()
        out = []
        for wi in range(n):
            sums, recv = refs[wi], refs[n + wi]
            for peer, chip in others:
                out.append((sums.at[chip], recv.at[my_chip], peer, recv.at[chip]))
        return out

    return plan


def _scatter2_local(n):
    def plan(refs):
        my_chip = _place()[1]
        return [(refs[wi].at[my_chip], refs[n + wi].at[my_chip]) for wi in range(n)]

    return plan


def _pair_add(parts, half, core, *, name):
    _, r, c = parts.shape
    tr = max(d for d in range(HALO, 257, HALO) if r % d == 0) if r > 256 else r
    parts4 = parts.reshape(N_CHIP, 2, r, c)

    def body(core_ref, p_ref, h_ref, o_ref):
        o_ref[...] = (p_ref[:, 0].astype(F32) + h_ref[...].astype(F32)).astype(o_ref.dtype)

    return pl.pallas_call(
        body, name=name,
        grid_spec=pltpu.PrefetchScalarGridSpec(
            num_scalar_prefetch=1, grid=(r // tr,),
            in_specs=[pl.BlockSpec((N_CHIP, 1, tr, c), lambda i, core_ref: (0, core_ref[0], i, 0)),
                      pl.BlockSpec((N_CHIP, tr, c), lambda i, core_ref: (0, i, 0))],
            out_specs=pl.BlockSpec((N_CHIP, tr, c), lambda i, core_ref: (0, i, 0))),
        out_shape=jax.ShapeDtypeStruct((N_CHIP, r, c), parts.dtype), compiler_params=_cp("parallel"),
    )(core, parts4, half)


def _scatter_start(parts, *, name):
    halves = [lax.empty((N_CHIP,) + v.shape[1:], v.dtype) for v in parts]
    return _split_start(parts + halves, N_CHIP * len(parts), _scatter1_plan(len(parts)), name=name)


def _adamw(parts, w, m, v, *, name):
    r, c = w.shape
    nparts = parts.shape[0]
    tr = max(d for d in range(HALO, 129, HALO) if r % d == 0) if r > 128 else r

    def body(p_ref, w_ref, m_ref, v_ref, g_ref, d_ref, mo_ref, vo_ref):
        g = p_ref[0].astype(F32)
        for k in range(1, nparts):
            g = g + p_ref[k].astype(F32)
        mn = ADAM_B1 * m_ref[...] + (1.0 - ADAM_B1) * g
        vn = ADAM_B2 * v_ref[...] + (1.0 - ADAM_B2) * (g * g)
        m_hat = mn / (1.0 - ADAM_B1 ** ADAM_STEP)
        v_hat = vn / (1.0 - ADAM_B2 ** ADAM_STEP)
        g_ref[...] = g
        d_ref[...] = -ADAM_LR * (m_hat / (jnp.sqrt(v_hat) + ADAM_EPS) + ADAM_WD * w_ref[...])
        mo_ref[...] = mn
        vo_ref[...] = vn

    blk = pl.BlockSpec((tr, c), lambda i: (i, 0))
    return pl.pallas_call(
        body, name=name, grid=(r // tr,), in_specs=[pl.BlockSpec((nparts, tr, c), lambda i: (0, i, 0)), blk, blk, blk],
        out_specs=[blk] * 4, out_shape=[jax.ShapeDtypeStruct((r, c), F32)] * 4, compiler_params=_cp("parallel"),
    )(parts, w, m, v)


def _pack_rows(vs, rows):
    lead = vs[0].shape[:-1] if vs[0].ndim > 1 else ()
    flat = jnp.concatenate(vs, axis=-1)
    pad = rows * LANES - flat.shape[-1]
    flat = jnp.pad(flat, [(0, 0)] * len(lead) + [(0, pad)])
    return flat.reshape(lead + (rows, LANES))


def kernel(x, p, positions, mix_norm_w, w_in, conv_w, conv_b, dt_bias, a_log, d_skip, ssd_norm_w, q_a_norm_w, w_q_b, kv_a_norm_w, w_kv_b, w_out, ffn_norm_w, w_ffn_up, ffn_conv_w, ffn_conv_b, w_ffn_down, ple_norm_w, w_ple_gate, b_ple_gate, w_ple_proj, ple_post_norm_w, final_norm_w, loss_target, m_mix_norm_w, m_w_in, m_conv_w, m_conv_b, m_dt_bias, m_a_log, m_d_skip, m_ssd_norm_w, m_q_a_norm_w, m_w_q_b, m_kv_a_norm_w, m_w_kv_b, m_w_out, m_ffn_norm_w, m_w_ffn_up, m_ffn_conv_w, m_ffn_conv_b, m_w_ffn_down, m_ple_norm_w, m_w_ple_gate, m_b_ple_gate, m_w_ple_proj, m_ple_post_norm_w, m_final_norm_w, v_mix_norm_w, v_w_in, v_conv_w, v_conv_b, v_dt_bias, v_a_log, v_d_skip, v_ssd_norm_w, v_q_a_norm_w, v_w_q_b, v_kv_a_norm_w, v_w_kv_b, v_w_out, v_ffn_norm_w, v_w_ffn_up, v_ffn_conv_w, v_ffn_conv_b, v_w_ffn_down, v_ple_norm_w, v_w_ple_gate, v_b_ple_gate, v_w_ple_proj, v_ple_post_norm_w, v_final_norm_w):
    given = dict(locals())
    shapes = {n: given[n].shape for n in WEIGHTS}
    w2 = {n: given[n].reshape(given[n].shape[-2:] if n in BIG or n in CONV else (1, -1)) for n in WEIGHTS}
    m2 = {n: given['m_' + n].reshape(w2[n].shape) for n in WEIGHTS}
    v2 = {n: given['v_' + n].reshape(w2[n].shape) for n in WEIGHTS}
    me = 4 * lax.axis_index("x") + 2 * lax.axis_index("y") + lax.axis_index("c")

    core = lax.axis_index("c").astype(jnp.int32).reshape(1)

    def shards(grp, zero):
        return [(w2[n] + zero).astype(BF16) if n in BIG else w2[n] + zero for n in WEIGHT_GROUPS[grp]]

    first, token = _gather_start(shards('a', 0.0), name="gather_a_hop1")
    first, token = _gather_forward(first, token, name="gather_a_hop2")
    zero = token[0, 0]
    later = dict(zip(WEIGHT_GROUPS['b'], _sequencer_gather(shards('b', zero), collective_id=1, name="gather_b")))
    later.update(zip(WEIGHT_GROUPS['c'], _sequencer_gather(shards('c', zero), collective_id=2, name="gather_c")))

    def get_w(grp, after):
        if grp == 'a':
            lands = dict(zip(WEIGHT_GROUPS[grp], _gather_finish(first, token, name="gather_a_done")))
        else:
            names = WEIGHT_GROUPS[grp]
            lands = dict(zip(names, _hold([later[n] for n in names], after, name="gather_" + grp + "_use")))
        return _assemble_weights(lands)

    scatters = {}

    hop_ids = {grp: 2 + 2 * i for i, grp in enumerate(GRAD_GROUPS)}

    def zero_of(arrays):
        return sum(v[(0,) * v.ndim].astype(F32) * 0.0 for v in arrays)

    def emit(grp, grads):
        scatters[grp], tok = _scatter_start([grads[n] for n in GRAD_GROUPS[grp]], name="scatter_" + grp + "_hop1")
        return tok[0, 0]

    def relay(grp, after):
        n = len(GRAD_GROUPS[grp])
        bufs = _split_wait(scatters[grp], after, _scatter1_plan(n), lambda refs: [], name="scatter_" + grp + "_hop1_wait")
        sums = [_pair_add(bufs[i], bufs[n + i], core, name="scatter_%s_add%d" % (grp, i)) for i in range(n)]
        scatters[grp] = _sequencer_scatter_hop2(sums, collective_id=hop_ids[grp] + 1, name="scatter_" + grp + "_hop2")
        return zero_of(sums)

    out_g, out_d, out_m, out_v = {}, {}, {}, {}

    def settle(grp):
        return zero_of(scatters[grp])

    def update(grp, behind=None):
        for n, parts in zip(GRAD_GROUPS[grp], scatters[grp]):
            wn = w2[n] if behind is None else w2[n] + behind
            out_g[n], out_d[n], out_m[n], out_v[n] = _adamw(parts, wn, m2[n], v2[n], name="adamw_" + n)

    vecs = {n: w2[n] for n in REPL}
    vecs['mix_norm_w'] = vecs['mix_norm_w'] + zero
    loss, dx, g_conv, g_vec = _local_step(x[0], p[0, 0], _rope_tables(positions), get_w, vecs, loss_target[0], emit,
                                          relay, settle)
    n_small = sum(g_vec[n].shape[1] for n in REPL) + sum(g_conv[n].size for n in CONV) + 1
    rows_small = -(-n_small // (LANES * HALO)) * HALO
    small = _pack_rows([g_vec[n] for n in REPL] + [g_conv[n].reshape(1, -1) for n in CONV] + [loss], rows_small)

    for grp in list(GRAD_GROUPS)[:-1]:
        update(grp)
    all_small = _exchange([small], gather=True, name="gather_small_grads")[0].reshape(N_DEV, rows_small * LANES)
    update(list(GRAD_GROUPS)[-1], zero_of([all_small]))
    pieces, off = [], 0
    for n in REPL:
        k = g_vec[n].shape[1]
        pieces.append(all_small[:, off:off + k])
        off += k
    for n in CONV:
        kw, cols = g_conv[n].shape
        full = all_small[:, off:off + kw * cols].reshape(N_DEV, kw, cols)
        mine = lax.dynamic_slice_in_dim(full, me * (cols // N_DEV), cols // N_DEV, axis=2)
        pieces.append(mine.reshape(N_DEV, kw * (cols // N_DEV)))
        off += kw * cols
    pieces.append(all_small[:, off:off + 1])
    small_names = REPL + CONV
    n_mine = sum(q.shape[1] for q in pieces)
    rows_mine = -(-n_mine // (LANES * HALO)) * HALO
    zero = jnp.zeros((1, 1), F32)
    packed = [_pack_rows([src[n].reshape(1, -1) for n in small_names] + [zero], rows_mine).reshape(rows_mine, LANES)
              for src in (w2, m2, v2)]
    sg, sd, sm, sv = _adamw(_pack_rows(pieces, rows_mine), *packed, name="adamw_small")
    off = 0
    for n in small_names:
        k = w2[n].size
        for dst, src in ((out_g, sg), (out_d, sd), (out_m, sm), (out_v, sv)):
            dst[n] = src.reshape(-1)[off:off + k].reshape(w2[n].shape)
        off += k
    total_loss = sg.reshape(-1)[off]

    outs = [total_loss, dx[None]]
    for res in (out_g, out_d, out_m, out_v):
        outs += [res[n].reshape(shapes[n]) for n in WEIGHTS]
    return tuple(outs)
```

```python
import math

import numpy as np
import jax
import jax.numpy as jnp
from jax import lax
from jax.experimental import pallas as pl
from jax.experimental.pallas import tpu as pltpu
from jax.experimental.pallas import tpu_sc as plsc

F32 = jnp.float32
BF16 = jnp.bfloat16
HI = lax.Precision.HIGHEST

D_MODEL = 2048
CHUNK = 64
D_SSM = 1024
SSD_P = 64
SSD_HEADS = 16
SSD_GROUPS = 2
SSD_N = 128
SSD_CONV = 4
SSD_CONV_DIM = D_SSM + 2 * SSD_GROUPS * SSD_N
MLA_HEADS = 8
MLA_NOPE = 128
MLA_ROPE = 64
MLA_V = 128
MLA_Q_RANK = 512
MLA_KV_RANK = 256
MLA_QK_PAD = 256
ROPE_THETA = 10000.0
D_FF = 5632
FFN_CONV = 3
PLE_DIM = 256
NORM_EPS = 1e-6
ADAM_LR, ADAM_B1, ADAM_B2, ADAM_EPS, ADAM_WD, ADAM_STEP = 0.001, 0.9, 0.999, 1e-08, 0.01, 10
N_DEV = 8

OFF_Z, OFF_XBC, OFF_QA, OFF_CKV, OFF_KR, OFF_DT, D_IN_PAD = 0, 1024, 2560, 3072, 3328, 3456, 3584
D_IN = 3408
LANES = 128
HALO = 8
VMEM_LIMIT = 56 * 1024 * 1024
FFN_TC = D_FF * 2 // N_DEV
FFN_PERM = (0, 4, 1, 5, 2, 6, 3, 7)
NEG = -1e30


def _cp(*sem):
    return pltpu.CompilerParams(dimension_semantics=tuple(sem), vmem_limit_bytes=VMEM_LIMIT)


def _tile(n, want):
    if n <= want:
        return n
    best = max(d for d in range(LANES, want + 1, LANES) if n % d == 0)
    return best


def _sigmoid(x):
    return 0.5 * (jnp.tanh(0.5 * x) + 1.0)


def _silu(x):
    return x * _sigmoid(x)


def _dsilu(x):
    s = _sigmoid(x)
    return s * (1.0 + x * (1.0 - s))


MM_TILE = 1408
MM_TK = 2816


def _matmul(a, b, *, ta=False, tb=False, out_dtype=F32, add=None, bias=None, tm=MM_TILE, tn=MM_TILE, tk=MM_TK, name,
            mnk=None, a_spec=None, b_spec=None, o_spec=None, o_shape=None):
    if mnk is None:
        m, k = (a.shape[1], a.shape[0]) if ta else a.shape
        n = b.shape[0] if tb else b.shape[1]
        assert k == (b.shape[1] if tb else b.shape[0])
    else:
        m, n, k = mnk
    tm, tn, tk = _tile(m, tm), _tile(n, tn), _tile(k, tk)
    nk = k // tk
    dims = (((0 if ta else 1,), (1 if tb else 0,)), ((), ()))

    def body(*refs):
        a_ref, b_ref = refs[0], refs[1]
        pos = 2
        add_ref = bias_ref = None
        if add is not None:
            add_ref = refs[pos]
            pos += 1
        if bias is not None:
            bias_ref = refs[pos]
            pos += 1
        o_ref = refs[pos]
        kk = pl.program_id(2)
        av = a_ref[...]
        bv = b_ref[...]
        av = av.reshape(av.shape[-2:]).astype(BF16)
        bv = bv.reshape(bv.shape[-2:]).astype(BF16)
        prod = lax.dot_general(av, bv, dims, preferred_element_type=F32)

        def finish(r):
            if bias_ref is not None:
                r = r + bias_ref[...]
            if add_ref is not None:
                r = r + add_ref[...].astype(F32)
            o_ref[...] = r.astype(out_dtype).reshape(o_ref.shape)

        if nk == 1:
            finish(prod)
        else:
            acc_ref = refs[pos + 1]

            @pl.when(kk == 0)
            def _():
                acc_ref[...] = prod

            @pl.when(kk > 0)
            def _():
                acc_ref[...] += prod

            @pl.when(kk == nk - 1)
            def _():
                finish(acc_ref[...])

    if a_spec is None:
        a_spec = (pl.BlockSpec((tk, tm), lambda i, j, kk: (kk, i)) if ta
                  else pl.BlockSpec((tm, tk), lambda i, j, kk: (i, kk)))
    if b_spec is None:
        b_spec = (pl.BlockSpec((tn, tk), lambda i, j, kk: (j, kk)) if tb
                  else pl.BlockSpec((tk, tn), lambda i, j, kk: (kk, j)))
    if o_spec is None:
        o_spec = pl.BlockSpec((tm, tn), lambda i, j, kk: (i, j))
    if o_shape is None:
        o_shape = (m, n)
    in_specs = [a_spec, b_spec]
    args = [a, b]
    if add is not None:
        in_specs.append(pl.BlockSpec((tm, tn), lambda i, j, kk: (i, j)))
        args.append(add)
    if bias is not None:
        in_specs.append(pl.BlockSpec((1, tn), lambda i, j, kk: (0, j)))
        args.append(bias)
    return pl.pallas_call(
        body, name=name, grid=(m // tm, n // tn, nk), in_specs=in_specs, out_specs=o_spec,
        out_shape=jax.ShapeDtypeStruct(o_shape, out_dtype),
        scratch_shapes=[pltpu.VMEM((tm, tn), F32)] if nk > 1 else [],
        compiler_params=_cp("parallel", "parallel", "arbitrary"),
    )(*args)


def _rmsnorm_fwd(x, w, *, width, cblk=0, out_dtype=BF16, tr=256, name):
    t = x.shape[0]

    def body(x_ref, w_ref, o_ref):
        xv = x_ref[...].astype(F32)
        r = lax.rsqrt(jnp.mean(xv * xv, axis=-1, keepdims=True) + NORM_EPS)
        o_ref[...] = (xv * r * w_ref[...]).astype(out_dtype)

    return pl.pallas_call(
        body, name=name, grid=(t // tr,),
        in_specs=[pl.BlockSpec((tr, width), lambda i: (i, cblk)), pl.BlockSpec((1, width), lambda i: (0, 0))],
        out_specs=pl.BlockSpec((tr, width), lambda i: (i, 0)),
        out_shape=jax.ShapeDtypeStruct((t, width), out_dtype),
        compiler_params=_cp("parallel"),
    )(x, w)


def _rmsnorm_bwd(x, w, dy, add=None, *, width, cblk=0, out_dtype=F32, also_bf16=False, tr=256, name):
    t = x.shape[0]

    def body(*refs):
        refs = list(refs)
        dxb_ref = refs.pop() if also_bf16 else None
        if add is None:
            x_ref, w_ref, dy_ref, dx_ref, dw_ref = refs
            add_ref = None
        else:
            x_ref, w_ref, dy_ref, add_ref, dx_ref, dw_ref = refs
        xv = x_ref[...].astype(F32)
        dyv = dy_ref[...].astype(F32)
        r = lax.rsqrt(jnp.mean(xv * xv, axis=-1, keepdims=True) + NORM_EPS)
        xh = xv * r
        g = dyv * w_ref[...]
        dx = r * (g - xh * jnp.mean(g * xh, axis=-1, keepdims=True))
        if add_ref is not None:
            dx = dx + add_ref[...].astype(F32)
        dx_ref[...] = dx.astype(out_dtype)
        if dxb_ref is not None:
            dxb_ref[...] = dx.astype(BF16)

        @pl.when(pl.program_id(0) == 0)
        def _():
            dw_ref[...] = jnp.zeros_like(dw_ref)

        dw_ref[...] += jnp.sum(dyv * xh, axis=0, keepdims=True)

    in_specs = [pl.BlockSpec((tr, width), lambda i: (i, cblk)), pl.BlockSpec((1, width), lambda i: (0, 0)),
                pl.BlockSpec((tr, width), lambda i: (i, 0))]
    args = [x, w, dy]
    if add is not None:
        in_specs.append(pl.BlockSpec((tr, width), lambda i: (i, 0)))
        args.append(add)
    blk = pl.BlockSpec((tr, width), lambda i: (i, 0))
    return pl.pallas_call(
        body, name=name, grid=(t // tr,), in_specs=in_specs,
        out_specs=[blk, pl.BlockSpec((1, width), lambda i: (0, 0))] + ([blk] if also_bf16 else []),
        out_shape=[jax.ShapeDtypeStruct((t, width), out_dtype), jax.ShapeDtypeStruct((1, width), F32)]
        + ([jax.ShapeDtypeStruct((t, width), BF16)] if also_bf16 else []),
        compiler_params=_cp("arbitrary"),
    )(*args)


def _shift_down(prev_halo, cur, j):
    if j == 0:
        return cur
    ext = jnp.concatenate([prev_halo, cur], axis=0)
    return pltpu.roll(ext, j, axis=0)[HALO:]


def _shift_up(cur, next_halo, j):
    if j == 0:
        return cur
    ext = jnp.concatenate([cur, next_halo], axis=0)
    return pltpu.roll(ext, ext.shape[0] - j, axis=0)[:cur.shape[0]]


def _conv_rows(prev, cur, w, b, kw):
    shifted = [cur]
    out = b + w[kw - 1:kw] * cur
    for j in range(1, kw):
        sh = _shift_down(prev, cur, j)
        shifted.append(sh)
        out = out + w[kw - 1 - j:kw - j] * sh
    return out, shifted


def _act_fwd(c, glu):
    if glu:
        half = c.shape[1] // 2
        return _silu(c[:, :half]) * c[:, half:]
    return _silu(c)


def _act_bwd(c, dout, glu):
    if glu:
        half = c.shape[1] // 2
        g, up = c[:, :half], c[:, half:]
        s = _sigmoid(g)
        gs = g * s
        return jnp.concatenate([dout * up * (s + gs * (1.0 - s)), dout * gs], axis=1)
    return dout * _dsilu(c)


def _conv_act_fwd(u, w, b, *, kw, glu, tc, coff, ncols, out_dtype, tr=256, name):
    t = u.shape[0]
    nb = ncols // tc
    oc = tc // 2 if glu else tc

    def body(u_ref, uh_ref, w_ref, b_ref, o_ref):
        prev = jnp.where(pl.program_id(0) == 0, 0.0, uh_ref[...])
        c, _ = _conv_rows(prev, u_ref[...], w_ref[...], b_ref[...], kw)
        o_ref[...] = _act_fwd(c, glu).astype(out_dtype)

    return pl.pallas_call(
        body, name=name, grid=(t // tr, nb),
        in_specs=[pl.BlockSpec((tr, tc), lambda i, j: (i, j + coff)),
                  pl.BlockSpec((HALO, tc), lambda i, j: (jnp.maximum(i * (tr // HALO) - 1, 0), j + coff)),
                  pl.BlockSpec((kw, tc), lambda i, j: (0, j)), pl.BlockSpec((1, tc), lambda i, j: (0, j))],
        out_specs=pl.BlockSpec((tr, oc), lambda i, j: (i, j)),
        out_shape=jax.ShapeDtypeStruct((t, nb * oc), out_dtype),
        compiler_params=_cp("parallel", "parallel"),
    )(u, u, w, b)


def _conv_act_bwd(u, w, b, dout, *, kw, glu, tc, coff, ncols, tr=256, name):
    t = u.shape[0]
    nb = ncols // tc
    nt = t // tr
    oc = tc // 2 if glu else tc

    def body(u_ref, up_ref, un_ref, d_ref, dn_ref, w_ref, b_ref, du_ref, dw_ref, db_ref):
        i = pl.program_id(1)
        cur, nxt, wv, bv = u_ref[...], un_ref[...], w_ref[...], b_ref[...]
        prev = jnp.where(i == 0, 0.0, up_ref[...])
        c_cur, shifted = _conv_rows(prev, cur, wv, bv, kw)
        c_nxt, _ = _conv_rows(cur[tr - HALO:], nxt, wv, bv, kw)
        d_cur = _act_bwd(c_cur, d_ref[...].astype(F32), glu)
        d_nxt = _act_bwd(c_nxt, jnp.where(i == nt - 1, 0.0, dn_ref[...].astype(F32)), glu)
        du = wv[kw - 1:kw] * d_cur
        for j in range(1, kw):
            du = du + wv[kw - 1 - j:kw - j] * _shift_up(d_cur, d_nxt, j)
        du_ref[...] = du.astype(BF16)

        @pl.when(i == 0)
        def _():
            dw_ref[...] = jnp.zeros_like(dw_ref)
            db_ref[...] = jnp.zeros_like(db_ref)

        db_ref[...] += jnp.sum(d_cur, axis=0, keepdims=True)
        dw_ref[...] += jnp.concatenate(
            [jnp.sum(d_cur * shifted[kw - 1 - k], axis=0, keepdims=True) for k in range(kw)], axis=0)

    nh = tr // HALO
    return pl.pallas_call(
        body, name=name, grid=(nb, nt),
        in_specs=[pl.BlockSpec((tr, tc), lambda j, i: (i, j + coff)),
                  pl.BlockSpec((HALO, tc), lambda j, i: (jnp.maximum(i * nh - 1, 0), j + coff)),
                  pl.BlockSpec((HALO, tc), lambda j, i: (jnp.minimum((i + 1) * nh, t // HALO - 1), j + coff)),
                  pl.BlockSpec((tr, oc), lambda j, i: (i, j)),
                  pl.BlockSpec((HALO, oc), lambda j, i: (jnp.minimum((i + 1) * nh, t // HALO - 1), j)),
                  pl.BlockSpec((kw, tc), lambda j, i: (0, j)), pl.BlockSpec((1, tc), lambda j, i: (0, j))],
        out_specs=[pl.BlockSpec((tr, tc), lambda j, i: (i, j)), pl.BlockSpec((kw, tc), lambda j, i: (0, j)),
                   pl.BlockSpec((1, tc), lambda j, i: (0, j))],
        out_shape=[jax.ShapeDtypeStruct((t, ncols), BF16), jax.ShapeDtypeStruct((kw, ncols), F32),
                   jax.ShapeDtypeStruct((1, ncols), F32)],
        compiler_params=_cp("parallel", "arbitrary"),
    )(u, u, u, dout, dout, w, b)


def _ple_loss(x2, gl, pe, pw, fw, target, *, tr=256, name):
    t, d = x2.shape

    def body(x_ref, gl_ref, pe_ref, pw_ref, fw_ref, t_ref, l_ref, dx_ref, dfw_ref, dgl_ref, db_ref, dpe_ref, dpw_ref):
        pv, pwv, wv = pe_ref[...], pw_ref[...], fw_ref[...]
        gate = _sigmoid(gl_ref[...])
        rp = lax.rsqrt(jnp.mean(pv * pv, axis=-1, keepdims=True) + NORM_EPS)
        ph = pv * rp
        e = ph * pwv
        x3 = x_ref[...] + gate * e
        r = lax.rsqrt(jnp.mean(x3 * x3, axis=-1, keepdims=True) + NORM_EPS)
        xh = x3 * r
        err = xh * wv - t_ref[...]
        dy = err * (1.0 / d)
        g = dy * wv
        dx = r * (g - xh * jnp.mean(g * xh, axis=-1, keepdims=True))
        dx_ref[...] = dx
        dgl = dx * e * gate * (1.0 - gate)
        de = dx * gate
        gg = de * pwv
        dgl_ref[...] = dgl.astype(BF16)
        dpe_ref[...] = (rp * (gg - ph * jnp.mean(gg * ph, axis=-1, keepdims=True))).astype(BF16)

        @pl.when(pl.program_id(0) == 0)
        def _():
            for ref in (l_ref, dfw_ref, db_ref, dpw_ref):
                ref[...] = jnp.zeros_like(ref)

        l_ref[...] += 0.5 * jnp.sum(jnp.mean(err * err, axis=-1, keepdims=True), axis=0, keepdims=True)
        dfw_ref[...] += jnp.sum(dy * xh, axis=0, keepdims=True)
        db_ref[...] += jnp.sum(dgl, axis=0, keepdims=True)
        dpw_ref[...] += jnp.sum(de * ph, axis=0, keepdims=True)

    blk = pl.BlockSpec((tr, d), lambda i: (i, 0))
    row = pl.BlockSpec((1, d), lambda i: (0, 0))
    rowf = jax.ShapeDtypeStruct((1, d), F32)
    return pl.pallas_call(
        body, name=name, grid=(t // tr,), in_specs=[blk, blk, blk, row, row, blk],
        out_specs=[pl.BlockSpec((1, 1), lambda i: (0, 0)), blk, row, blk, row, blk, row],
        out_shape=[jax.ShapeDtypeStruct((1, 1), F32), jax.ShapeDtypeStruct((t, d), F32), rowf,
                   jax.ShapeDtypeStruct((t, d), BF16), rowf, jax.ShapeDtypeStruct((t, d), BF16), rowf],
        compiler_params=_cp("arbitrary"),
    )(x2, gl, pe, pw, fw, target)


def _rope(blk, tab_ref):
    return blk * tab_ref[0] + pltpu.roll(blk, 96, axis=1) * tab_ref[1] + pltpu.roll(blk, 32, axis=1) * tab_ref[2]


def _unrope(g, tab_ref):
    return g * tab_ref[0] + pltpu.roll(g * tab_ref[1], 32, axis=1) + pltpu.roll(g * tab_ref[2], 96, axis=1)


def _mla_prep(q, kv, proj, tabs, *, tr=512, name):
    t = q.shape[0]

    def body(q_ref, kv_ref, kr_ref, tab_ref, qo_ref, ko_ref, vo_ref, vt_ref):
        qv, kvv = q_ref[...], kv_ref[...]
        qo_ref[0, :, :MLA_NOPE] = qv[:, :MLA_NOPE].astype(BF16)
        qo_ref[0, :, MLA_NOPE:] = _rope(qv[:, MLA_NOPE:], tab_ref).astype(BF16)
        ko_ref[0, :, :MLA_NOPE] = kvv[:, :MLA_NOPE].astype(BF16)
        ko_ref[0, :, MLA_NOPE:] = _rope(kr_ref[...], tab_ref).astype(BF16)
        vo_ref[0] = kvv[:, MLA_NOPE:].astype(BF16)
        for blk in range(tr // ATT_BLK):
            vt_ref[0, blk] = kvv[blk * ATT_BLK:(blk + 1) * ATT_BLK, MLA_NOPE:].T.astype(BF16)

    return pl.pallas_call(
        body, name=name, grid=(t // tr, MLA_HEADS),
        in_specs=[pl.BlockSpec((tr, MLA_QK_PAD), lambda i, h: (i, h)),
                  pl.BlockSpec((tr, MLA_NOPE + MLA_V), lambda i, h: (i, h)),
                  pl.BlockSpec((tr, LANES), lambda i, h: (i, OFF_KR // LANES)),
                  pl.BlockSpec((3, tr, LANES), lambda i, h: (0, i, 0))],
        out_specs=[pl.BlockSpec((1, tr, MLA_QK_PAD), lambda i, h: (h, i, 0)),
                   pl.BlockSpec((1, tr, MLA_QK_PAD), lambda i, h: (h, i, 0)),
                   pl.BlockSpec((1, tr, MLA_V), lambda i, h: (h, i, 0)),
                   pl.BlockSpec((1, tr // ATT_BLK, MLA_V, ATT_BLK), lambda i, h: (h, i, 0, 0))],
        out_shape=[jax.ShapeDtypeStruct((MLA_HEADS, t, MLA_QK_PAD), BF16),
                   jax.ShapeDtypeStruct((MLA_HEADS, t, MLA_QK_PAD), BF16),
                   jax.ShapeDtypeStruct((MLA_HEADS, t, MLA_V), BF16),
                   jax.ShapeDtypeStruct((MLA_HEADS, t // ATT_BLK, MLA_V, ATT_BLK), BF16)],
        compiler_params=_cp("parallel", "parallel"),
    )(q, kv, proj, tabs)


def _mla_unprep(dq3, dk3, dv3, tabs, *, tr=256, name):
    t = dq3.shape[1]

    def body(dq_ref, dk_ref, dv_ref, tab_ref, qo_ref, kvo_ref, kro_ref):
        kr = jnp.zeros((tr, LANES), F32)
        for h in range(MLA_HEADS):
            c0 = h * MLA_QK_PAD
            qo_ref[:, c0:c0 + MLA_NOPE] = dq_ref[h, :, :MLA_NOPE].astype(BF16)
            qo_ref[:, c0 + MLA_NOPE:c0 + MLA_QK_PAD] = _unrope(dq_ref[h, :, MLA_NOPE:], tab_ref).astype(BF16)
            kvo_ref[:, c0:c0 + MLA_NOPE] = dk_ref[h, :, :MLA_NOPE].astype(BF16)
            kvo_ref[:, c0 + MLA_NOPE:c0 + MLA_QK_PAD] = dv_ref[h].astype(BF16)
            kr = kr + dk_ref[h, :, MLA_NOPE:]
        kro_ref[...] = _unrope(kr, tab_ref).astype(BF16)

    return pl.pallas_call(
        body, name=name, grid=(t // tr,),
        in_specs=[pl.BlockSpec((MLA_HEADS, tr, MLA_QK_PAD), lambda i: (0, i, 0)),
                  pl.BlockSpec((MLA_HEADS, tr, MLA_QK_PAD), lambda i: (0, i, 0)),
                  pl.BlockSpec((MLA_HEADS, tr, MLA_V), lambda i: (0, i, 0)),
                  pl.BlockSpec((3, tr, LANES), lambda i: (0, i, 0))],
        out_specs=[pl.BlockSpec((tr, MLA_HEADS * MLA_QK_PAD), lambda i: (i, 0)),
                   pl.BlockSpec((tr, MLA_HEADS * MLA_QK_PAD), lambda i: (i, 0)),
                   pl.BlockSpec((tr, LANES), lambda i: (i, 0))],
        out_shape=[jax.ShapeDtypeStruct((t, MLA_HEADS * MLA_QK_PAD), BF16),
                   jax.ShapeDtypeStruct((t, MLA_HEADS * MLA_QK_PAD), BF16),
                   jax.ShapeDtypeStruct((t, LANES), BF16)],
        compiler_params=_cp("parallel"),
    )(dq3, dk3, dv3, tabs)


ATT_BLK = 512
ATT_SCALE = 1.0 / math.sqrt(MLA_NOPE + MLA_ROPE)
_NT = (((1,), (1,)), ((), ()))
_TN = (((0,), (0,)), ((), ()))


def _att_scores_t(k, q, diagonal):
    s = lax.dot_general(k, q, _NT, preferred_element_type=F32) * ATT_SCALE
    if not diagonal:
        return s
    key = lax.broadcasted_iota(jnp.int32, s.shape, 0)
    query = lax.broadcasted_iota(jnp.int32, s.shape, 1)
    return jnp.where((key >> 6) <= (query >> 6), s, NEG)


def _att_rows(i):
    return pl.ds(pl.multiple_of(i * ATT_BLK, ATT_BLK), ATT_BLK)


ATT_HEADS = 2


def _attn_fwd(q3, k3, vt4, *, name):
    t = q3.shape[1]
    nq = t // ATT_BLK

    def body(q_ref, k_ref, vt_ref, o_ref, lse_ref):
        qi = pl.program_id(1)
        qs = [q_ref[hh] for hh in range(ATT_HEADS)]

        def step(j, carry, diagonal=False):
            out = []
            for hh, (m, l, acc) in enumerate(carry):
                s = _att_scores_t(k_ref[hh, _att_rows(j), :], qs[hh], diagonal)
                m_new = jnp.maximum(m, jnp.max(s, axis=0, keepdims=True))
                p = jnp.exp(s - m_new)
                alpha = jnp.exp(m - m_new)
                l = alpha * l + jnp.sum(p, axis=0, keepdims=True)
                acc = alpha * acc + jnp.dot(vt_ref[hh, j], p.astype(BF16), preferred_element_type=F32)
                out.append((m_new, l, acc))
            return tuple(out)

        init = tuple((jnp.full((1, ATT_BLK), NEG, F32), jnp.zeros((1, ATT_BLK), F32),
                      jnp.zeros((MLA_V, ATT_BLK), F32)) for _ in range(ATT_HEADS))
        done = step(qi, lax.fori_loop(0, qi, step, init), diagonal=True)
        for hh, (m, l, acc) in enumerate(done):
            o_ref[:, hh * MLA_V:(hh + 1) * MLA_V] = (acc / l).T
            lse_ref[hh, 0] = m + jnp.log(l)

    return pl.pallas_call(
        body, name=name, grid=(MLA_HEADS // ATT_HEADS, nq),
        in_specs=[pl.BlockSpec((ATT_HEADS, ATT_BLK, MLA_QK_PAD), lambda h, i: (h, i, 0)),
                  pl.BlockSpec((ATT_HEADS, t, MLA_QK_PAD), lambda h, i: (h, 0, 0)),
                  pl.BlockSpec((ATT_HEADS, nq, MLA_V, ATT_BLK), lambda h, i: (h, 0, 0, 0))],
        out_specs=[pl.BlockSpec((ATT_BLK, ATT_HEADS * MLA_V), lambda h, i: (i, h)),
                   pl.BlockSpec((ATT_HEADS, 1, 1, ATT_BLK), lambda h, i: (h, i, 0, 0))],
        out_shape=[jax.ShapeDtypeStruct((t, MLA_HEADS * MLA_V), F32),
                   jax.ShapeDtypeStruct((MLA_HEADS, nq, 1, ATT_BLK), F32)],
        compiler_params=_cp("parallel", "parallel"),
    )(q3, k3, vt4)


def _attn_bwd(q3, k3, v3, o, dcat, lse, *, name):
    t = q3.shape[1]
    nq = t // ATT_BLK
    wide = ATT_HEADS * MLA_V

    def body(q_ref, k_ref, v_ref, o_ref, do_ref, lse_ref, dq_ref, dk_ref, dv_ref, delta_ref):
        kj = pl.program_id(1)

        @pl.when(kj == 0)
        def _():
            dq_ref[...] = jnp.zeros_like(dq_ref)
            ones = jnp.ones((HALO, MLA_V), F32)
            for i in range(nq):
                rows = pl.ds(i * ATT_BLK, ATT_BLK)
                prod = o_ref[rows, :] * do_ref[rows, :]
                for hh in range(ATT_HEADS):
                    delta_ref[hh, i] = lax.dot_general(ones, prod[:, hh * MLA_V:(hh + 1) * MLA_V], _NT, precision=HI,
                                                       preferred_element_type=F32)

        def step(i, carry, diagonal=False):
            rows = _att_rows(i)
            out = []
            for hh, (dk, dv) in enumerate(carry):
                k, v = k_ref[hh], v_ref[hh]
                q = q_ref[hh, rows, :]
                dob = do_ref[rows, hh * MLA_V:(hh + 1) * MLA_V].astype(BF16)
                p = jnp.exp(_att_scores_t(k, q, diagonal) - lse_ref[hh, i])
                dv = dv + jnp.dot(p.astype(BF16), dob, preferred_element_type=F32)
                dp = lax.dot_general(v, dob, _NT, preferred_element_type=F32)
                ds = (p * (dp - delta_ref[hh, i, 0:1, :]) * ATT_SCALE).astype(BF16)
                dk = dk + jnp.dot(ds, q, preferred_element_type=F32)
                dq_ref[hh, rows, :] += lax.dot_general(ds, k, _TN, preferred_element_type=F32)
                out.append((dk, dv))
            return tuple(out)

        init = tuple((jnp.zeros((ATT_BLK, MLA_QK_PAD), F32), jnp.zeros((ATT_BLK, MLA_V), F32))
                     for _ in range(ATT_HEADS))
        done = lax.fori_loop(kj + 1, nq, step, step(kj, init, diagonal=True))
        for hh, (dk, dv) in enumerate(done):
            dk_ref[hh] = dk
            dv_ref[hh] = dv

    return pl.pallas_call(
        body, name=name, grid=(MLA_HEADS // ATT_HEADS, nq),
        in_specs=[pl.BlockSpec((ATT_HEADS, t, MLA_QK_PAD), lambda h, j: (h, 0, 0)),
                  pl.BlockSpec((ATT_HEADS, ATT_BLK, MLA_QK_PAD), lambda h, j: (h, j, 0)),
                  pl.BlockSpec((ATT_HEADS, ATT_BLK, MLA_V), lambda h, j: (h, j, 0)),
                  pl.BlockSpec((t, wide), lambda h, j: (0, h)),
                  pl.BlockSpec((t, wide), lambda h, j: (0, MLA_HEADS // ATT_HEADS + h)),
                  pl.BlockSpec((ATT_HEADS, nq, 1, ATT_BLK), lambda h, j: (h, 0, 0, 0))],
        out_specs=[pl.BlockSpec((ATT_HEADS, t, MLA_QK_PAD), lambda h, j: (h, 0, 0)),
                   pl.BlockSpec((ATT_HEADS, ATT_BLK, MLA_QK_PAD), lambda h, j: (h, j, 0)),
                   pl.BlockSpec((ATT_HEADS, ATT_BLK, MLA_V), lambda h, j: (h, j, 0))],
        out_shape=[jax.ShapeDtypeStruct((MLA_HEADS, t, MLA_QK_PAD), F32),
                   jax.ShapeDtypeStruct((MLA_HEADS, t, MLA_QK_PAD), F32),
                   jax.ShapeDtypeStruct((MLA_HEADS, t, MLA_V), F32)],
        scratch_shapes=[pltpu.VMEM((ATT_HEADS, nq, HALO, ATT_BLK), F32)],
        compiler_params=_cp("parallel", "arbitrary"),
    )(q3, k3, v3, o, dcat, lse)


def _ssd_prep(proj, bias128, alog128, *, name):
    t = proj.shape[0]
    nc = t // CHUNK

    def body(raw_ref, b_ref, al_ref, dt_ref, cs_ref, a_ref):
        xv = raw_ref[...] + b_ref[...]
        dt = jnp.maximum(xv, 0.0) + jnp.log(1.0 + jnp.exp(-jnp.abs(xv)))
        a = -jnp.exp(al_ref[...])
        adt = (dt * a).reshape(nc, CHUNK, LANES)
        li = lax.broadcasted_iota(jnp.int32, (nc, CHUNK, CHUNK), 1)
        si = lax.broadcasted_iota(jnp.int32, (nc, CHUNK, CHUNK), 2)
        tril = jnp.where(si <= li, 1.0, 0.0).astype(F32)
        cs = lax.dot_general(tril, adt, (((2,), (1,)), ((0,), (0,))), precision=HI, preferred_element_type=F32)
        dt_ref[...] = dt
        cs_ref[...] = cs.reshape(t, LANES)
        a_ref[...] = a

    blk = pl.BlockSpec((t, LANES), lambda i: (0, 0))
    row = pl.BlockSpec((1, LANES), lambda i: (0, 0))
    return pl.pallas_call(
        body, name=name, grid=(1,),
        in_specs=[pl.BlockSpec((t, LANES), lambda i: (0, OFF_DT // LANES)), row, row],
        out_specs=[blk, blk, row],
        out_shape=[jax.ShapeDtypeStruct((t, LANES), F32), jax.ShapeDtypeStruct((t, LANES), F32),
                   jax.ShapeDtypeStruct((1, LANES), F32)],
        compiler_params=_cp("arbitrary"),
    )(proj, bias128, alog128)


def _ssd_prep_bwd(ddt128, dadt128, proj, bias128, dt128, a128, dd_h, *, name):
    t = proj.shape[0]

    def body(ddt_ref, dadt_ref, raw_ref, b_ref, dt_ref, a_ref, dd_ref, draw_ref, db_ref, dal_ref, dds_ref):
        draw = ddt_ref[...] * _sigmoid(raw_ref[...] + b_ref[...])
        draw_ref[...] = draw.astype(BF16)
        db_ref[...] = jnp.sum(draw, axis=0, keepdims=True)
        dal_ref[...] = jnp.sum(dadt_ref[...] * dt_ref[...], axis=0, keepdims=True) * a_ref[...]
        dds_ref[...] = jnp.sum(dd_ref[...], axis=-1, keepdims=True)

    blk = pl.BlockSpec((t, LANES), lambda i: (0, 0))
    row = pl.BlockSpec((1, LANES), lambda i: (0, 0))
    return pl.pallas_call(
        body, name=name, grid=(1,),
        in_specs=[blk, blk, pl.BlockSpec((t, LANES), lambda i: (0, OFF_DT // LANES)), row, blk, row,
                  pl.BlockSpec((SSD_HEADS, SSD_P), lambda i: (0, 0))],
        out_specs=[blk, row, row, pl.BlockSpec((SSD_HEADS, 1), lambda i: (0, 0))],
        out_shape=[jax.ShapeDtypeStruct((t, LANES), BF16), jax.ShapeDtypeStruct((1, LANES), F32),
                   jax.ShapeDtypeStruct((1, LANES), F32), jax.ShapeDtypeStruct((SSD_HEADS, 1), F32)],
        compiler_params=_cp("arbitrary"),
    )(ddt128, dadt128, proj, bias128, dt128, a128, dd_h)


def _bdot(a, b, ca, cb, precision=None):
    return lax.dot_general(a, b, (((ca,), (cb,)), ((0,), (0,))), precision=precision, preferred_element_type=F32)


def _pieces(x):
    hi = x.astype(BF16)
    rest = x - hi.astype(F32)
    mid = rest.astype(BF16)
    return hi, mid, (rest - mid.astype(F32)).astype(BF16)


def _bdot_sum(a, b, ca, cb, split):
    other = (b if split == 0 else a).astype(BF16)
    out = None
    for piece in _pieces(a if split == 0 else b):
        term = _bdot(piece, other, ca, cb) if split == 0 else _bdot(other, piece, ca, cb)
        out = term if out is None else out + term
    return out


def _head_matrices():
    eye, zero = jnp.eye(SSD_P, dtype=F32), jnp.zeros((SSD_P, SSD_P), F32)
    pick = jnp.stack([jnp.concatenate([eye, zero], axis=0), jnp.concatenate([zero, eye], axis=0)])
    return pick, pick.transpose(0, 2, 1)


def _move(x, sel):
    selb = sel.astype(BF16)
    hi = x.astype(BF16)
    rest = x - hi.astype(F32)
    mid = rest.astype(BF16)
    low = (rest - mid.astype(F32)).astype(BF16)
    out = jnp.dot(hi, selb, preferred_element_type=F32)
    out = out + jnp.dot(mid, selb, preferred_element_type=F32)
    return out + jnp.dot(low, selb, preferred_element_type=F32)


def _pick_head(pair_ref, pick_ref, h):
    return _move(pair_ref[...], pick_ref[h % 2])


def _place_head(out_ref, val, place_ref, h):
    wide = _move(val, place_ref[h % 2])

    @pl.when(h % 2 == 0)
    def _():
        out_ref[...] = wide

    @pl.when(h % 2 == 1)
    def _():
        out_ref[...] += wide


def _head_column(ref, h, nc):
    v = ref[...]
    mine = lax.broadcasted_iota(jnp.int32, v.shape, 1) == h
    return jnp.sum(jnp.where(mine, v, 0.0), axis=1, keepdims=True).reshape(nc, CHUNK, 1)


def _ssd_common(x2, dt_ref, cs_ref, csr_ref, b_ref, c_ref, nc, h):
    x = x2.reshape(nc, CHUNK, SSD_P)
    dt = _head_column(dt_ref, h, nc)
    cs = _head_column(cs_ref, h, nc)
    csr = csr_ref[0]
    bm = b_ref[...].reshape(nc, CHUNK, SSD_N).astype(BF16)
    cm = c_ref[...].reshape(nc, CHUNK, SSD_N).astype(BF16)
    li = lax.broadcasted_iota(jnp.int32, (nc, CHUNK, CHUNK), 1)
    si = lax.broadcasted_iota(jnp.int32, (nc, CHUNK, CHUNK), 2)
    lmat = jnp.exp(jnp.where(si <= li, cs - csr, NEG))
    g = _bdot(cm, bm, 2, 2)
    cs_last = jnp.sum(jnp.where(li == CHUNK - 1, cs, 0.0), axis=1, keepdims=True)
    xdt = x * dt
    dec = jnp.exp(cs_last - cs)
    return x, dt, cs, bm, cm, li, si, lmat, g, cs_last, xdt, dec


def _ssd_fwd(xbc, dt128, cs128, cs_row, dskip_h, *, name):
    t = xbc.shape[0]
    nc = t // CHUNK
    hpg = SSD_HEADS // SSD_GROUPS
    pick, place = _head_matrices()

    def body(xs_ref, dt_ref, cs_ref, csr_ref, b_ref, c_ref, dk_ref, pick_ref, place_ref, y_ref, st_ref, sc_ref, cd_ref):
        h = pl.program_id(0)
        x, dt, cs, bm, cm, li, si, lmat, g, cs_last, xdt, dec = _ssd_common(_pick_head(xs_ref, pick_ref, h), dt_ref,
                                                                           cs_ref, csr_ref, b_ref, c_ref, nc, h)
        yd = _bdot((g * lmat).astype(BF16), xdt.astype(BF16), 2, 1)
        sc_ref[...] = _bdot(bm, (dec * xdt).astype(BF16), 1, 1)
        cd_ref[...] = jnp.exp(cs_last)

        def step(c, s):
            st_ref[0, c] = s
            return s * cd_ref[c] + sc_ref[c]

        lax.fori_loop(0, nc, step, jnp.zeros((SSD_N, SSD_P), F32))
        yo = _bdot(cm, st_ref[0].astype(BF16), 2, 1) * jnp.exp(cs)
        _place_head(y_ref, (yd + yo + dk_ref[0] * x).reshape(t, SSD_P), place_ref, h)

    lanes = pl.BlockSpec((t, LANES), lambda h: (0, 0))
    pair = pl.BlockSpec((t, 2 * SSD_P), lambda h: (0, h // 2))
    nxb = D_SSM // SSD_N
    return pl.pallas_call(
        body, name=name, grid=(SSD_HEADS,),
        in_specs=[pair, lanes, lanes, pl.BlockSpec((1, nc, 1, CHUNK), lambda h: (h, 0, 0, 0)),
                  pl.BlockSpec((t, SSD_N), lambda h: (0, nxb + h // hpg)),
                  pl.BlockSpec((t, SSD_N), lambda h: (0, nxb + SSD_GROUPS + h // hpg)),
                  pl.BlockSpec((1, 1, SSD_P), lambda h: (h, 0, 0)),
                  pl.BlockSpec((2, 2 * SSD_P, SSD_P), lambda h: (0, 0, 0)),
                  pl.BlockSpec((2, SSD_P, 2 * SSD_P), lambda h: (0, 0, 0))],
        out_specs=[pair, pl.BlockSpec((1, nc, SSD_N, SSD_P), lambda h: (h, 0, 0, 0))],
        out_shape=[jax.ShapeDtypeStruct((t, D_SSM), F32),
                   jax.ShapeDtypeStruct((SSD_HEADS, nc, SSD_N, SSD_P), F32)],
        scratch_shapes=[pltpu.VMEM((nc, SSD_N, SSD_P), F32), pltpu.VMEM((nc, 1, SSD_P), F32)],
        compiler_params=_cp("arbitrary"),
    )(xbc, dt128, cs128, cs_row, xbc, xbc, dskip_h, pick, place)


def _ssd_bwd(xbc, dt128, cs128, cs_row, dskip_h, a_h, states, dy, *, name):
    t = xbc.shape[0]
    nc = t // CHUNK
    hpg = SSD_HEADS // SSD_GROUPS
    pick, place = _head_matrices()

    def body(xs_ref, dt_ref, cs_ref, csr_ref, b_ref, c_ref, dk_ref, a_ref, st_ref, dy_ref, pick_ref, place_ref,
             dxs_ref, ddt_ref, dadt_ref, db_ref, dc_ref, dd_ref, dsl_ref, dsc_ref, cd_ref):
        h = pl.program_id(0) * hpg + pl.program_id(1)
        x, dt, cs, bm, cm, li, si, lmat, g, cs_last, xdt, dec = _ssd_common(_pick_head(xs_ref, pick_ref, h), dt_ref,
                                                                           cs_ref, csr_ref, b_ref, c_ref, nc, h)
        dy = _pick_head(dy_ref, pick_ref, h).reshape(nc, CHUNK, SSD_P)
        dyb = dy.astype(BF16)
        xdtb = xdt.astype(BF16)
        sprev = st_ref[0]
        sprevb = sprev.astype(BF16)
        cdec = jnp.exp(cs_last)
        ecs = jnp.exp(cs)
        dw = (ecs * dy).astype(BF16)
        wmat = _bdot(cm, sprevb, 2, 1)
        dcs = jnp.sum(dy * ecs * wmat, axis=2, keepdims=True)
        dcm = _bdot(dw, sprevb, 2, 2)
        dsl_ref[...] = _bdot(cm, dw, 1, 1)
        cd_ref[...] = cdec

        def step(k, ds):
            c = nc - 1 - k
            dsc_ref[c] = ds
            return ds * cd_ref[c] + dsl_ref[c]

        lax.fori_loop(0, nc, step, jnp.zeros((SSD_N, SSD_P), F32))
        dsc = dsc_ref[...]
        dscb = dsc.astype(BF16)
        d_last = jnp.sum(jnp.sum(dsc * sprev, axis=1, keepdims=True) * cdec, axis=2, keepdims=True)
        z = dec * xdt
        dbm = _bdot(z.astype(BF16), dscb, 2, 2)
        dz = _bdot(bm, dscb, 2, 1)
        dxdt = dec * dz
        t2 = jnp.sum(dz * z, axis=2, keepdims=True)
        dcs = dcs - t2
        d_last = d_last + jnp.sum(t2, axis=1, keepdims=True)
        m = g * lmat
        mb = m.astype(BF16)
        dm = _bdot(dyb, xdtb, 2, 2)
        dxdt = dxdt + _bdot(mb, dyb, 1, 1)
        dseg = dm * m
        dcs = dcs + jnp.sum(dseg, axis=2, keepdims=True)
        ones = jnp.ones((nc, CHUNK, SSD_P), F32)
        dcs = dcs - _bdot_sum(dseg, ones, 1, 1, 0)
        dg = (dm * lmat).astype(BF16)
        dcm = dcm + _bdot(dg, bm, 2, 1)
        dbm = dbm + _bdot(dg, cm, 1, 1)
        dcs = dcs + jnp.where(li[:, :, :SSD_P] == CHUNK - 1, d_last, 0.0)
        triu = jnp.where(li <= si, 1.0, 0.0).astype(F32)
        dadt = _bdot_sum(triu, dcs, 2, 1, 1)
        dk = dk_ref[0]
        _place_head(dxs_ref, (dxdt * dt + dk * dy).reshape(t, SSD_P), place_ref, h)
        ddt = jnp.sum(dxdt * x, axis=2, keepdims=True) + dadt * a_ref[0]
        mine = lax.broadcasted_iota(jnp.int32, (t, LANES), 1) == h

        @pl.when(h == 0)
        def _():
            ddt_ref[...] = jnp.zeros_like(ddt_ref)
            dadt_ref[...] = jnp.zeros_like(dadt_ref)

        ddt_ref[...] += jnp.where(mine, jnp.max(ddt, axis=2, keepdims=True).reshape(t, 1), 0.0)
        dadt_ref[...] += jnp.where(mine, jnp.max(dadt, axis=2, keepdims=True).reshape(t, 1), 0.0)
        dd_ref[0] = jnp.sum(jnp.sum(dy * x, axis=1, keepdims=True), axis=0)

        @pl.when(pl.program_id(1) == 0)
        def _():
            db_ref[...] = jnp.zeros_like(db_ref)
            dc_ref[...] = jnp.zeros_like(dc_ref)

        db_ref[...] += dbm.reshape(t, SSD_N)
        dc_ref[...] += dcm.reshape(t, SSD_N)

    head = pl.BlockSpec((1, t, SSD_P), lambda gi, hi: (gi * hpg + hi, 0, 0))
    pair = pl.BlockSpec((t, 2 * SSD_P), lambda gi, hi: (0, (gi * hpg + hi) // 2))
    grp = pl.BlockSpec((t, SSD_N), lambda gi, hi: (0, gi))
    lane = pl.BlockSpec((1, 1, SSD_P), lambda gi, hi: (gi * hpg + hi, 0, 0))
    rows = pl.BlockSpec((t, LANES), lambda gi, hi: (0, 0))
    nxb = D_SSM // SSD_N
    dxs, ddt, dadt, db, dc, dd = pl.pallas_call(
        body, name=name, grid=(SSD_GROUPS, hpg),
        in_specs=[pair, rows, rows, pl.BlockSpec((1, nc, 1, CHUNK), lambda gi, hi: (gi * hpg + hi, 0, 0, 0)),
                  pl.BlockSpec((t, SSD_N), lambda gi, hi: (0, nxb + gi)),
                  pl.BlockSpec((t, SSD_N), lambda gi, hi: (0, nxb + SSD_GROUPS + gi)), lane, lane,
                  pl.BlockSpec((1, nc, SSD_N, SSD_P), lambda gi, hi: (gi * hpg + hi, 0, 0, 0)), pair,
                  pl.BlockSpec((2, 2 * SSD_P, SSD_P), lambda gi, hi: (0, 0, 0)),
                  pl.BlockSpec((2, SSD_P, 2 * SSD_P), lambda gi, hi: (0, 0, 0))],
        out_specs=[pair, rows, rows, grp, grp, lane],
        out_shape=[jax.ShapeDtypeStruct((t, D_SSM), F32)] + [jax.ShapeDtypeStruct((t, LANES), F32)] * 2
        + [jax.ShapeDtypeStruct((t, SSD_GROUPS * SSD_N), F32)] * 2
        + [jax.ShapeDtypeStruct((SSD_HEADS, 1, SSD_P), F32)],
        scratch_shapes=[pltpu.VMEM((nc, SSD_N, SSD_P), F32), pltpu.VMEM((nc, SSD_N, SSD_P), F32),
                        pltpu.VMEM((nc, 1, SSD_P), F32)],
        compiler_params=_cp("arbitrary", "arbitrary"),
    )(xbc, dt128, cs128, cs_row, xbc, xbc, dskip_h, a_h, states, dy, pick, place)
    return jnp.concatenate([dxs, db, dc], axis=1), ddt, dadt, dd


def _ssd_gate_fwd(y, proj, w, *, tr=256, name):
    t = y.shape[0]
    gw = D_SSM // SSD_GROUPS

    def body(y_ref, z_ref, w_ref, o_ref):
        v = y_ref[...] * _silu(z_ref[...])
        for gi in range(SSD_GROUPS):
            vg = v[:, gi * gw:(gi + 1) * gw]
            r = lax.rsqrt(jnp.mean(vg * vg, axis=-1, keepdims=True) + NORM_EPS)
            o_ref[:, gi * gw:(gi + 1) * gw] = (vg * r * w_ref[:, gi * gw:(gi + 1) * gw]).astype(BF16)

    blk = pl.BlockSpec((tr, D_SSM), lambda i: (i, 0))
    return pl.pallas_call(
        body, name=name, grid=(t // tr,), in_specs=[blk, blk, pl.BlockSpec((1, D_SSM), lambda i: (0, 0))],
        out_specs=blk, out_shape=jax.ShapeDtypeStruct((t, D_SSM), BF16), compiler_params=_cp("parallel"),
    )(y, proj, w)


def _ssd_gate_bwd(y, proj, w, dcat, *, tr=256, name):
    t = y.shape[0]
    gw = D_SSM // SSD_GROUPS

    def body(y_ref, z_ref, w_ref, d_ref, dy_ref, dz_ref, dw_ref):
        yv, zv, dv = y_ref[...], z_ref[...], d_ref[...].astype(F32)
        sz = _silu(zv)
        v = yv * sz

        @pl.when(pl.program_id(0) == 0)
        def _():
            dw_ref[...] = jnp.zeros_like(dw_ref)

        for gi in range(SSD_GROUPS):
            sl = slice(gi * gw, (gi + 1) * gw)
            vg, dg = v[:, sl], dv[:, sl]
            r = lax.rsqrt(jnp.mean(vg * vg, axis=-1, keepdims=True) + NORM_EPS)
            vh = vg * r
            gg = dg * w_ref[:, sl]
            dvg = r * (gg - vh * jnp.mean(gg * vh, axis=-1, keepdims=True))
            dy_ref[:, sl] = dvg * sz[:, sl]
            dz_ref[:, sl] = (dvg * yv[:, sl] * _dsilu(zv[:, sl])).astype(BF16)
            dw_ref[:, sl] += jnp.sum(dg * vh, axis=0, keepdims=True)

    blk = pl.BlockSpec((tr, D_SSM), lambda i: (i, 0))
    row = pl.BlockSpec((1, D_SSM), lambda i: (0, 0))
    return pl.pallas_call(
        body, name=name, grid=(t // tr,), in_specs=[blk, blk, row, blk], out_specs=[blk, blk, row],
        out_shape=[jax.ShapeDtypeStruct((t, D_SSM), F32), jax.ShapeDtypeStruct((t, D_SSM), BF16),
                   jax.ShapeDtypeStruct((1, D_SSM), F32)],
        compiler_params=_cp("arbitrary"),
    )(y, proj, w, dcat)


def _pad_lanes(v):
    return jnp.pad(v, ((0, 0), (0, LANES - v.shape[1])))


def _per_head(v128, t):
    return jnp.broadcast_to(v128[:, :SSD_HEADS].T[:, :, None], (SSD_HEADS, t, SSD_P))


def _ssd_forward(proj, conv_w, conv_b, dt_bias, a_log, d_skip, ssd_norm_w):
    t = proj.shape[0]
    nc = t // CHUNK
    xbc = _conv_act_fwd(proj, conv_w, conv_b, kw=SSD_CONV, glu=False, tc=512, coff=OFF_XBC // 512,
                        ncols=SSD_CONV_DIM, out_dtype=F32, name="ssd_conv_fwd")
    bias128, alog128 = _pad_lanes(dt_bias), _pad_lanes(a_log)
    dt128, cs128, a128 = _ssd_prep(proj, bias128, alog128, name="ssd_prep")
    cs_row = cs128[:, :SSD_HEADS].T.reshape(SSD_HEADS, nc, 1, CHUNK)
    dskip_h = jnp.broadcast_to(d_skip[0][:, None, None], (SSD_HEADS, 1, SSD_P))
    a_h = jnp.broadcast_to(a128[0, :SSD_HEADS][:, None, None], (SSD_HEADS, 1, SSD_P))
    y, states = _ssd_fwd(xbc, dt128, cs128, cs_row, dskip_h, name="ssd_scan_fwd")
    y_ssd = _ssd_gate_fwd(y, proj, ssd_norm_w, name="ssd_gate_fwd")
    saved = (proj, conv_w, conv_b, ssd_norm_w, bias128, dt128, a128, cs128, cs_row, xbc, dskip_h, a_h, states, y)
    return y_ssd, saved


def _ssd_backward(saved, dcat):
    proj, conv_w, conv_b, ssd_norm_w, bias128, dt128, a128, cs128, cs_row, xbc, dskip_h, a_h, states, y = saved
    dy, dz, d_norm_w = _ssd_gate_bwd(y, proj, ssd_norm_w, dcat, name="ssd_gate_bwd")
    dxc, ddt128, dadt128, dd_h = _ssd_bwd(xbc, dt128, cs128, cs_row, dskip_h, a_h, states, dy, name="ssd_scan_bwd")
    dxbc, d_conv_w, d_conv_b = _conv_act_bwd(proj, conv_w, conv_b, dxc, kw=SSD_CONV, glu=False, tc=512,
                                             coff=OFF_XBC // 512, ncols=SSD_CONV_DIM, name="ssd_conv_bwd")
    d_raw, d_bias, d_alog, d_dskip = _ssd_prep_bwd(ddt128, dadt128, proj, bias128, dt128, a128,
                                                   dd_h.reshape(SSD_HEADS, SSD_P), name="ssd_prep_bwd")
    return (dz, dxbc, d_raw, d_norm_w, d_conv_w, d_conv_b, d_bias[:, :SSD_HEADS], d_alog[:, :SSD_HEADS],
            d_dskip.reshape(1, SSD_HEADS))


def _rope_tables(positions):
    inv_freq = ROPE_THETA ** (-jnp.arange(0, MLA_ROPE, 2, dtype=F32) / MLA_ROPE)
    ang = positions[0].astype(F32)[:, None] * inv_freq
    cos, sin = jnp.cos(ang), jnp.sin(ang)
    z = jnp.zeros_like(cos)
    return jnp.stack([jnp.concatenate([cos, cos, z, z], axis=1), jnp.concatenate([-sin, z, z, z], axis=1),
                      jnp.concatenate([z, sin, z, z], axis=1)])


def _latent_norms(proj, q_w, kv_w, *, tr=256, name):
    t = proj.shape[0]

    def body(q_ref, kv_ref, qw_ref, kvw_ref, qo_ref, kvo_ref):
        for x_ref, w_ref, o_ref in ((q_ref, qw_ref, qo_ref), (kv_ref, kvw_ref, kvo_ref)):
            xv = x_ref[...]
            r = lax.rsqrt(jnp.mean(xv * xv, axis=-1, keepdims=True) + NORM_EPS)
            o_ref[...] = (xv * r * w_ref[...]).astype(BF16)

    return pl.pallas_call(
        body, name=name, grid=(t // tr,),
        in_specs=[pl.BlockSpec((tr, MLA_Q_RANK), lambda i: (i, OFF_QA // MLA_Q_RANK)),
                  pl.BlockSpec((tr, MLA_KV_RANK), lambda i: (i, OFF_CKV // MLA_KV_RANK)),
                  pl.BlockSpec((1, MLA_Q_RANK), lambda i: (0, 0)), pl.BlockSpec((1, MLA_KV_RANK), lambda i: (0, 0))],
        out_specs=[pl.BlockSpec((tr, MLA_Q_RANK), lambda i: (i, 0)), pl.BlockSpec((tr, MLA_KV_RANK), lambda i: (i, 0))],
        out_shape=[jax.ShapeDtypeStruct((t, MLA_Q_RANK), BF16), jax.ShapeDtypeStruct((t, MLA_KV_RANK), BF16)],
        compiler_params=_cp("parallel"),
    )(proj, proj, q_w, kv_w)


def _latent_norms_bwd(proj, q_w, kv_w, dqn, dkvn, *, tr=256, name):
    t = proj.shape[0]

    def body(q_ref, kv_ref, qw_ref, kvw_ref, dq_ref, dkv_ref, dqa_ref, dqw_ref, dckv_ref, dkvw_ref):
        first = pl.program_id(0) == 0
        for x_ref, w_ref, dy_ref, dx_ref, dw_ref in ((q_ref, qw_ref, dq_ref, dqa_ref, dqw_ref),
                                                     (kv_ref, kvw_ref, dkv_ref, dckv_ref, dkvw_ref)):
            xv, dyv = x_ref[...], dy_ref[...]
            r = lax.rsqrt(jnp.mean(xv * xv, axis=-1, keepdims=True) + NORM_EPS)
            xh = xv * r
            g = dyv * w_ref[...]
            dx_ref[...] = (r * (g - xh * jnp.mean(g * xh, axis=-1, keepdims=True))).astype(BF16)

            @pl.when(first)
            def _(dw_ref=dw_ref):
                dw_ref[...] = jnp.zeros_like(dw_ref)

            dw_ref[...] += jnp.sum(dyv * xh, axis=0, keepdims=True)

    def rows(width, cblk=0):
        return pl.BlockSpec((tr, width), lambda i: (i, cblk))

    def row(width):
        return pl.BlockSpec((1, width), lambda i: (0, 0))

    return pl.pallas_call(
        body, name=name, grid=(t // tr,),
        in_specs=[rows(MLA_Q_RANK, OFF_QA // MLA_Q_RANK), rows(MLA_KV_RANK, OFF_CKV // MLA_KV_RANK), row(MLA_Q_RANK),
                  row(MLA_KV_RANK), rows(MLA_Q_RANK), rows(MLA_KV_RANK)],
        out_specs=[rows(MLA_Q_RANK), row(MLA_Q_RANK), rows(MLA_KV_RANK), row(MLA_KV_RANK)],
        out_shape=[jax.ShapeDtypeStruct((t, MLA_Q_RANK), BF16), jax.ShapeDtypeStruct((1, MLA_Q_RANK), F32),
                   jax.ShapeDtypeStruct((t, MLA_KV_RANK), BF16), jax.ShapeDtypeStruct((1, MLA_KV_RANK), F32)],
        compiler_params=_cp("arbitrary"),
    )(proj, proj, q_w, kv_w, dqn, dkvn)


def _mla_forward(proj, tabs, q_a_norm_w, wq_pad, kv_a_norm_w, wkv):
    qn, kvn = _latent_norms(proj, q_a_norm_w, kv_a_norm_w, name="latent_norms")
    q = _matmul(qn, wq_pad, name="q_b_proj")
    kv = _matmul(kvn, wkv, name="kv_b_proj")
    q3, k3, v3, vt4 = _mla_prep(q, kv, proj, tabs, name="mla_prep")
    o, lse = _attn_fwd(q3, k3, vt4, name="attn_fwd")
    return o, (proj, tabs, q_a_norm_w, wq_pad, kv_a_norm_w, wkv, qn, kvn, q3, k3, v3, o, lse)


def _mla_backward(saved, dcat):
    proj, tabs, q_a_norm_w, wq_pad, kv_a_norm_w, wkv, qn, kvn, q3, k3, v3, o, lse = saved
    dq3, dk3, dv3 = _attn_bwd(q3, k3, v3, o, dcat, lse, name="attn_bwd")
    dq, dkv, dkr = _mla_unprep(dq3, dk3, dv3, tabs, name="mla_unprep")
    d_wq = _matmul(qn, dq, ta=True, out_dtype=BF16, name="d_w_q_b")
    dqn = _matmul(dq, wq_pad, tb=True, name="d_qn")
    d_wkv = _matmul(kvn, dkv, ta=True, out_dtype=BF16, name="d_w_kv_b")
    dkvn = _matmul(dkv, wkv, tb=True, name="d_kvn")
    dq_a, d_qnw, dckv, d_kvnw = _latent_norms_bwd(proj, q_a_norm_w, kv_a_norm_w, dqn, dkvn, name="latent_norms_bwd")
    return dq_a, dckv, dkr, d_wq, d_wkv, d_qnw, d_kvnw


def _pad_w_q(w):
    r = w.shape[0]
    w3 = w.reshape(r, MLA_HEADS, MLA_NOPE + MLA_ROPE)
    return jnp.pad(w3, ((0, 0), (0, 0), (0, MLA_QK_PAD - MLA_NOPE - MLA_ROPE))).reshape(r, MLA_HEADS * MLA_QK_PAD)


def _unpad_w_q(w):
    r = w.shape[0]
    return w.reshape(r, MLA_HEADS, MLA_QK_PAD)[:, :, :MLA_NOPE + MLA_ROPE].reshape(r, MLA_HEADS * (MLA_NOPE + MLA_ROPE))


W_IN_SEGMENTS = ((0, D_SSM + SSD_CONV_DIM, 0), (D_SSM + SSD_CONV_DIM, D_SSM + SSD_CONV_DIM + SSD_HEADS, OFF_DT),
                 (D_SSM + SSD_CONV_DIM + SSD_HEADS, D_IN - MLA_ROPE, OFF_QA), (D_IN - MLA_ROPE, D_IN, OFF_KR))


def _pad_w_in_shards(g):
    n = g.shape[2]
    pieces, at = [], 0
    for lo, hi, start in sorted(W_IN_SEGMENTS, key=lambda seg: seg[2]):
        if start > at:
            pieces.append(jnp.zeros((g.shape[1], start - at), g.dtype))
        for j in range(N_DEV):
            a, b = max(lo, j * n), min(hi, (j + 1) * n)
            if a < b:
                pieces.append(g[j][:, a - j * n:b - j * n])
        at = start + hi - lo
    pieces.append(jnp.zeros((g.shape[1], D_IN_PAD - at), g.dtype))
    return jnp.concatenate(pieces, axis=1)


def _unpad_w_in_shards(w):
    n = D_IN // N_DEV
    shards = []
    for j in range(N_DEV):
        pieces = []
        for lo, hi, start in W_IN_SEGMENTS:
            a, b = max(lo, j * n), min(hi, (j + 1) * n)
            if a < b:
                pieces.append(w[:, start + a - lo:start + b - lo])
        shards.append(jnp.concatenate(pieces, axis=1) if len(pieces) > 1 else pieces[0])
    return jnp.stack(shards)


WEIGHTS = ['mix_norm_w', 'w_in', 'conv_w', 'conv_b', 'dt_bias', 'a_log', 'd_skip', 'ssd_norm_w', 'q_a_norm_w', 'w_q_b',
           'kv_a_norm_w', 'w_kv_b', 'w_out', 'ffn_norm_w', 'w_ffn_up', 'ffn_conv_w', 'ffn_conv_b', 'w_ffn_down',
           'ple_norm_w', 'w_ple_gate', 'b_ple_gate', 'w_ple_proj', 'ple_post_norm_w', 'final_norm_w']
BIG = ['w_in', 'w_q_b', 'w_kv_b', 'w_out', 'w_ffn_up', 'w_ffn_down', 'w_ple_gate', 'w_ple_proj']
COL_SHARDED = ('w_in', 'w_q_b', 'w_kv_b', 'w_ffn_up', 'w_ple_proj')
CONV = ['conv_w', 'ffn_conv_w']
REPL = [n for n in WEIGHTS if n not in BIG and n not in CONV]
FFN_INV = tuple(int(i) for i in np.argsort(FFN_PERM))


def _cat_cols(g):
    return jnp.concatenate([g[j] for j in range(N_DEV)], axis=1)


def _split_cols(w):
    n = w.shape[1] // N_DEV
    return jnp.stack([w[:, j * n:(j + 1) * n] for j in range(N_DEV)])


def _interleave(v):
    r = v.shape[0]
    return v.reshape(r, N_DEV, FFN_TC)[:, jnp.array(FFN_PERM)].reshape(r, N_DEV * FFN_TC)


def _deinterleave(v):
    r = v.shape[0]
    return v.reshape(r, N_DEV, FFN_TC)[:, jnp.array(FFN_INV)].reshape(r, N_DEV * FFN_TC)


def _assemble_weights(g):
    layout = {
        'w_in': _pad_w_in_shards,
        'w_q_b': lambda v: _pad_w_q(_cat_cols(v)),
        'w_kv_b': _cat_cols,
        'w_out': lambda v: v.reshape(D_MODEL, D_MODEL),
        'w_ffn_up': lambda v: v,
        'w_ffn_down': lambda v: v.reshape(D_FF, D_MODEL),
        'w_ple_gate': lambda v: v.reshape(D_MODEL, D_MODEL),
        'w_ple_proj': _cat_cols,
        'conv_w': _cat_cols,
        'ffn_conv_w': lambda v: _interleave(_cat_cols(v)),
    }
    return {n: layout[n](v) for n, v in g.items()}


WEIGHT_GROUPS = {'a': ['w_in', 'w_q_b', 'w_kv_b', 'conv_w'], 'b': ['w_out', 'w_ffn_up', 'ffn_conv_w'],
                 'c': ['w_ffn_down', 'w_ple_gate', 'w_ple_proj']}
GRAD_GROUPS = {'p': ['w_ple_proj', 'w_ple_gate', 'w_ffn_down'], 'r': ['w_ffn_up'], 's': ['w_out'],
               't': ['w_q_b', 'w_kv_b', 'w_in']}


def _ffn_perm(j):
    return (j % 2) * (N_DEV // 2) + j // 2


def _local_step(x, p, tabs, get_w, s, target, emit, relay, settle):
    t = x.shape[0]
    s = dict(s)
    half = D_MODEL // 2
    up_cols = 2 * D_FF
    ffn_conv_b = _interleave(s['ffn_conv_b'])
    w = dict(get_w('a', None))
    h = _rmsnorm_fwd(x, s['mix_norm_w'], width=D_MODEL, name="mix_norm")
    proj = _matmul(h, w['w_in'], name="in_proj")
    y_ssd, ssd_saved = _ssd_forward(proj, w['conv_w'], s['conv_b'], s['dt_bias'], s['a_log'], s['d_skip'],
                                    s['ssd_norm_w'])
    o, mla_saved = _mla_forward(proj, tabs, s['q_a_norm_w'], w['w_q_b'], s['kv_a_norm_w'], w['w_kv_b'])
    tk_o, tn_o = _tile(half, MM_TK), _tile(D_MODEL, MM_TILE)
    w.update(get_w('b', o))
    x1 = _matmul(y_ssd, w['w_out'], add=x, mnk=(t, D_MODEL, half), name="out_proj_ssd")
    x1 = _matmul(o, w['w_out'], add=x1, mnk=(t, D_MODEL, half), name="out_proj_mla",
                 b_spec=pl.BlockSpec((tk_o, tn_o), lambda i, j, kk: (kk + half // tk_o, j)))
    hf = _rmsnorm_fwd(x1, s['ffn_norm_w'], width=D_MODEL, name="ffn_norm")
    tk_u = _tile(D_MODEL, MM_TK)
    u = _matmul(hf, w['w_ffn_up'], mnk=(t, up_cols, D_MODEL), tn=FFN_TC, name="ffn_up",
                b_spec=pl.BlockSpec((1, tk_u, FFN_TC), lambda i, j, kk: (_ffn_perm(j), kk, 0)))
    act = _conv_act_fwd(u, w['ffn_conv_w'], ffn_conv_b, kw=FFN_CONV, glu=True, tc=2 * FFN_TC, coff=0, ncols=up_cols,
                        out_dtype=BF16, name="ffn_act")
    w.update(get_w('c', act))
    x2 = _matmul(act, w['w_ffn_down'], add=x1, name="ffn_down")
    hp = _rmsnorm_fwd(x2, s['ple_norm_w'], width=D_MODEL, name="ple_norm")
    gl = _matmul(hp, w['w_ple_gate'], bias=s['b_ple_gate'], name="ple_gate")
    pe = _matmul(p, w['w_ple_proj'], name="ple_proj")
    loss, dx3, d_final, dgl, d_bgate, dpe, d_post = _ple_loss(x2, gl, pe, s['ple_post_norm_w'], s['final_norm_w'],
                                                              target, name="ple_loss")
    d_wproj = _matmul(p, dpe, ta=True, out_dtype=BF16, name="d_w_ple_proj")
    d_wgate = _matmul(hp, dgl, ta=True, out_dtype=BF16, name="d_w_ple_gate")
    dhp = _matmul(dgl, w['w_ple_gate'], tb=True, name="d_ple_normed")
    dx2, d_plenorm, dx2b = _rmsnorm_bwd(x2, s['ple_norm_w'], dhp, dx3, width=D_MODEL, also_bf16=True,
                                        name="ple_norm_bwd")
    dact = _matmul(dx2b, w['w_ffn_down'], tb=True, name="d_ffn_act")
    d_wdown = _matmul(act, dx2b, ta=True, out_dtype=BF16, name="d_w_ffn_down")
    zz = emit('p', {'w_ple_proj': _split_cols(d_wproj), 'w_ple_gate': d_wgate.reshape(N_DEV, D_MODEL // N_DEV, D_MODEL),
                    'w_ffn_down': d_wdown.reshape(N_DEV, D_FF // N_DEV, D_MODEL)})
    du, d_fconv_w, d_fconv_b = _conv_act_bwd(u, w['ffn_conv_w'], ffn_conv_b + zz, dact, kw=FFN_CONV, glu=True,
                                             tc=2 * FFN_TC, coff=0, ncols=up_cols, tr=128, name="ffn_act_bwd")
    zz = zz + relay('p', du)
    tm_u = _tile(D_MODEL, MM_TILE)
    d_wup = _matmul(hf, du, ta=True, out_dtype=BF16, mnk=(D_MODEL, up_cols, t), tn=FFN_TC, name="d_w_ffn_up",
                    o_spec=pl.BlockSpec((1, tm_u, FFN_TC), lambda i, j, kk: (_ffn_perm(j), i, 0)),
                    o_shape=(N_DEV, D_MODEL, FFN_TC))
    zz = zz + emit('r', {'w_ffn_up': d_wup})
    zero_row = jnp.zeros((1, D_MODEL), F32)
    dhf = _matmul(du, w['w_ffn_up'], tb=True, mnk=(t, D_MODEL, up_cols), tm=t, tk=FFN_TC, name="d_ffn_normed",
                  bias=zero_row + zz,
                  b_spec=pl.BlockSpec((1, tn_o, FFN_TC), lambda i, j, kk: (_ffn_perm(kk), j, 0)))
    zz = zz + relay('r', dhf) + settle('p')
    dx1, d_ffnnorm, dx1b = _rmsnorm_bwd(x1, s['ffn_norm_w'] + zz, dhf, dx2, width=D_MODEL, also_bf16=True,
                                        name="ffn_norm_bwd")
    dcat = _matmul(dx1b, w['w_out'], tb=True, name="d_mixed")
    d_wout = jnp.concatenate([_matmul(y_ssd, dx1b, ta=True, out_dtype=BF16, name="d_w_out_ssd"),
                              _matmul(o, dx1b, ta=True, out_dtype=BF16, name="d_w_out_mla")], axis=0)
    zz = zz + emit('s', {'w_out': d_wout.reshape(N_DEV, D_MODEL // N_DEV, D_MODEL)})
    ssd_saved = ssd_saved[:3] + (ssd_saved[3] + zz,) + ssd_saved[4:]
    dz, dxbc, d_raw, d_ssdnorm, d_conv_w, d_conv_b, d_dtb, d_alog, d_dskip = _ssd_backward(ssd_saved, dcat)
    zz = zz + relay('s', dz)
    mla_saved = mla_saved[:-1] + (mla_saved[-1] + zz,)
    dq_a, dckv, dkr, d_wq, d_wkv, d_qnorm, d_kvnorm = _mla_backward(mla_saved, dcat)
    d_raw = (d_raw + settle('r')).astype(BF16)
    dproj = jnp.concatenate([dz, dxbc, dq_a, dckv, dkr, d_raw], axis=1)
    d_win = _matmul(h, dproj, ta=True, out_dtype=BF16, name="d_w_in")
    zz = emit('t', {'w_in': _unpad_w_in_shards(d_win), 'w_q_b': _split_cols(_unpad_w_q(d_wq)),
                    'w_kv_b': _split_cols(d_wkv)}) + settle('s')
    dh = _matmul(dproj, w['w_in'], tb=True, bias=zero_row + zz, name="d_in_normed")
    zz = relay('t', dh)
    dx, d_mixnorm = _rmsnorm_bwd(x, s['mix_norm_w'] + zz, dh, dx1, width=D_MODEL, name="mix_norm_bwd")
    conv = {'conv_w': d_conv_w, 'ffn_conv_w': _deinterleave(d_fconv_w)}
    vec = {
        'mix_norm_w': d_mixnorm, 'conv_b': d_conv_b, 'dt_bias': d_dtb, 'a_log': d_alog, 'd_skip': d_dskip,
        'ssd_norm_w': d_ssdnorm, 'q_a_norm_w': d_qnorm, 'kv_a_norm_w': d_kvnorm, 'ffn_norm_w': d_ffnnorm,
        'ffn_conv_b': _deinterleave(d_fconv_b), 'ple_norm_w': d_plenorm, 'b_ple_gate': d_bgate,
        'ple_post_norm_w': d_post, 'final_norm_w': d_final,
    }
    return loss, dx, conv, vec


MESH = pl.DeviceIdType.MESH
FLIPS = ((0, 0, 1), (1, 0, 0), (0, 1, 0), (1, 1, 0), (1, 0, 1), (0, 1, 1), (1, 1, 1))


def _exchange(items, *, gather, name):
    n = len(items)

    def body(*refs):
        ins, outs = refs[:n], refs[n:2 * n]
        send_sems, recv_sems, local_sems = refs[2 * n:]
        x, y, c = lax.axis_index("x"), lax.axis_index("y"), lax.axis_index("c")
        me = 4 * x + 2 * y + c
        peers = [(jnp.where(fx, 1 - x, x), jnp.where(fy, 1 - y, y), jnp.where(fc, 1 - c, c)) for fx, fy, fc in FLIPS]
        slot = [4 * px + 2 * py + pc for px, py, pc in peers]
        local, sends = [], []
        for wi in range(n):
            cp = pltpu.make_async_copy(ins[wi] if gather else ins[wi].at[me], outs[wi].at[me], local_sems.at[wi])
            cp.start()
            local.append(cp)
            for k, peer in enumerate(peers):
                cp = pltpu.make_async_remote_copy(
                    src_ref=ins[wi] if gather else ins[wi].at[slot[k]], dst_ref=outs[wi].at[me],
                    send_sem=send_sems.at[k, wi], recv_sem=recv_sems.at[k, wi], device_id=peer, device_id_type=MESH)
                cp.start()
                sends.append(cp)
        for wi in range(n):
            for k, peer in enumerate(peers):
                pltpu.make_async_remote_copy(
                    src_ref=outs[wi].at[slot[k]], dst_ref=outs[wi].at[slot[k]], send_sem=send_sems.at[k, wi],
                    recv_sem=recv_sems.at[k, wi], device_id=peer, device_id_type=MESH).wait_recv()
        for cp in sends:
            cp.wait_send()
        for cp in local:
            cp.wait()

    hbm = pl.BlockSpec(memory_space=pltpu.HBM)
    out_shape = [jax.ShapeDtypeStruct(((N_DEV,) + v.shape) if gather else v.shape, v.dtype) for v in items]
    return pl.pallas_call(
        body, name=name, in_specs=[hbm] * n, out_specs=[hbm] * n, out_shape=out_shape,
        scratch_shapes=[pltpu.SemaphoreType.DMA((len(FLIPS), n)), pltpu.SemaphoreType.DMA((len(FLIPS), n)),
                        pltpu.SemaphoreType.DMA((n,))],
    )(*items)


HBM_SPEC = pl.BlockSpec(memory_space=pltpu.HBM)
SEM_SPEC = pl.BlockSpec(memory_space=pltpu.SEMAPHORE)
EFFECT = pltpu.SideEffectType.DATAFLOW_SIDE_EFFECTING


def _split_start(bufs, ncopies, plan, *, name):
    nb = len(bufs)

    def body(*refs):
        send_sems, recv_sems, token = refs[nb], refs[nb + 1], refs[2 * nb + 2]
        for i, (src, dst, peer, _) in enumerate(plan(refs[:nb])):
            pltpu.make_async_remote_copy(src_ref=src, dst_ref=dst, send_sem=send_sems.at[i], recv_sem=recv_sems.at[i],
                                         device_id=peer, device_id_type=MESH).start()
        token[...] = jnp.zeros_like(token)

    res = pl.pallas_call(
        body, name=name, in_specs=[HBM_SPEC] * nb,
        out_specs=[SEM_SPEC, SEM_SPEC] + [HBM_SPEC] * nb + [pl.BlockSpec(memory_space=pltpu.VMEM)],
        out_shape=[pltpu.SemaphoreType.DMA((ncopies,)), pltpu.SemaphoreType.DMA((ncopies,))]
        + [pltpu.HBM(v.shape, v.dtype) for v in bufs] + [jax.ShapeDtypeStruct((HALO, LANES), F32)],
        input_output_aliases={i: 2 + i for i in range(nb)},
        compiler_params=pltpu.CompilerParams(has_side_effects=EFFECT),
    )(*[pltpu.with_memory_space_constraint(v, pltpu.HBM) for v in bufs])
    return (res[0], res[1], list(res[2:2 + nb])), res[2 + nb]


def _split_wait(started, after, plan, local_plan, *, name):
    send_sems, recv_sems, bufs = started
    nb = len(bufs)
    nlocal = len(local_plan(bufs))

    def body(*refs):
        send_sems, recv_sems = refs[nb], refs[nb + 1]
        local_sems = refs[2 * nb + 3]
        local = []
        for j, (src, dst) in enumerate(local_plan(refs[:nb])):
            cp = pltpu.make_async_copy(src, dst, local_sems.at[j])
            cp.start()
            local.append(cp)
        for i, (src, _, peer, incoming) in enumerate(plan(refs[:nb])):
            cp = pltpu.make_async_remote_copy(src_ref=src, dst_ref=incoming, send_sem=send_sems.at[i],
                                              recv_sem=recv_sems.at[i], device_id=peer, device_id_type=MESH)
            cp.wait_send()
            cp.wait_recv()
        for cp in local:
            cp.wait()

    res = pl.pallas_call(
        body, name=name, in_specs=[HBM_SPEC] * nb + [SEM_SPEC, SEM_SPEC, pl.BlockSpec(memory_space=pl.ANY)],
        out_specs=[HBM_SPEC] * nb, out_shape=[pltpu.HBM(v.shape, v.dtype) for v in bufs],
        input_output_aliases={i: i for i in range(nb)},
        scratch_shapes=[pltpu.SemaphoreType.DMA((max(nlocal, 1),))],
        compiler_params=pltpu.CompilerParams(has_side_effects=EFFECT),
    )(*bufs, send_sems, recv_sems, after)
    return list(res)


def _hold(values, after, *, name):
    n = len(values)

    def body(*refs):
        del refs

    return list(pl.pallas_call(
        body, name=name, in_specs=[HBM_SPEC] * n + [pl.BlockSpec(memory_space=pl.ANY)], out_specs=[HBM_SPEC] * n,
        out_shape=[pltpu.HBM(v.shape, v.dtype) for v in values], input_output_aliases={i: i for i in range(n)},
    )(*values, after))


def _place():
    x, y, c = lax.axis_index("x"), lax.axis_index("y"), lax.axis_index("c")
    others = [((1 - x, y, c), 2 * (1 - x) + y), ((x, 1 - y, c), 2 * x + 1 - y), ((1 - x, 1 - y, c), 2 * (1 - x) + 1 - y)]
    return 4 * x + 2 * y + c, 2 * x + y, c, (x, y, 1 - c), others


def _gather1_plan(n):
    def plan(refs):
        me, _, _, sibling, others = _place()
        out = []
        for wi in range(n):
            item, land = refs[wi], refs[n + wi]
            out.append((item, land.at[me], sibling, land.at[me + 1 - 2 * lax.axis_index("c")]))
            for peer, chip in others:
                out.append((item, land.at[me], peer, land.at[2 * chip + lax.axis_index("c")]))
        return out

    return plan


def _gather1_local(n):
    def plan(refs):
        me = _place()[0]
        return [(refs[wi], refs[n + wi].at[me]) for wi in range(n)]

    return plan


def _gather2_plan(n):
    def plan(refs):
        _, _, c, sibling, others = _place()
        out = []
        for wi in range(n):
            land = refs[wi]
            for _, chip in others:
                out.append((land.at[2 * chip + c], land.at[2 * chip + c], sibling, land.at[2 * chip + 1 - c]))
        return out

    return plan


def _gather_start(items, *, name):
    lands = [lax.empty((N_DEV,) + v.shape, v.dtype) for v in items]
    return _split_start(items + lands, 4 * len(items), _gather1_plan(len(items)), name=name)


def _gather_forward(started, after, *, name):
    n = len(started[2]) // 2
    bufs = _split_wait(started, after, _gather1_plan(n), _gather1_local(n), name=name + "_wait")
    return _split_start(bufs[n:], 3 * n, _gather2_plan(n), name=name + "_start")


def _gather_finish(started, after, *, name):
    n = len(started[2])
    return _split_wait(started, after, _gather2_plan(n), lambda refs: [], name=name)


def _handshake(peers):
    barrier = pltpu.get_barrier_semaphore()
    for peer in peers:
        pl.semaphore_signal(barrier, inc=1, device_id=peer, device_id_type=MESH)
    pl.semaphore_wait(barrier, len(peers))


def _remote(src, dst, send_sem, recv_sem, peer):
    return pltpu.make_async_remote_copy(src_ref=src, dst_ref=dst, send_sem=send_sem, recv_sem=recv_sem, device_id=peer,
                                        device_id_type=MESH)


def _sequencer_gather(items, *, collective_id, name):
    n = len(items)
    srcs = [jax.new_ref(v, memory_space=pltpu.MemorySpace.HBM) for v in items]
    lands = [jax.empty_ref(jax.ShapeDtypeStruct((N_DEV,) + v.shape, v.dtype), memory_space=pltpu.MemorySpace.HBM)
             for v in items]
    dma = pltpu.SemaphoreType.DMA

    @pl.kernel(mesh=plsc.ScalarSubcoreMesh(axis_name="sequencer", num_cores=1), name=name,
               scratch_types=(dma((4 * n,)), dma((4 * n,)), dma((3 * n,)), dma((3 * n,)), dma((n,))),
               compiler_params=pltpu.CompilerParams(collective_id=collective_id))
    def launch(send1, recv1, send2, recv2, local_sems):
        _, _, _, sibling, others = _place()
        _handshake([sibling] + [peer for peer, _ in others])
        hop1 = _gather1_plan(n)(srcs + lands)
        hop2 = _gather2_plan(n)(lands)
        local = [pltpu.make_async_copy(src, dst, local_sems.at[j])
                 for j, (src, dst) in enumerate(_gather1_local(n)(srcs + lands))]
        for cp in local:
            cp.start()
        for i, (src, dst, peer, _) in enumerate(hop1):
            _remote(src, dst, send1.at[i], recv1.at[i], peer).start()
        for wi in range(n):
            for j in range(3):
                i1, i2 = 4 * wi + 1 + j, 3 * wi + j
                src, _, peer, incoming = hop1[i1]
                _remote(src, incoming, send1.at[i1], recv1.at[i1], peer).wait_recv()
                src, dst, peer, _ = hop2[i2]
                _remote(src, dst, send2.at[i2], recv2.at[i2], peer).start()
        for wi in range(n):
            src, _, peer, incoming = hop1[4 * wi]
            _remote(src, incoming, send1.at[4 * wi], recv1.at[4 * wi], peer).wait_recv()
        for i, (src, _, peer, incoming) in enumerate(hop2):
            cp = _remote(src, incoming, send2.at[i], recv2.at[i], peer)
            cp.wait_send()
            cp.wait_recv()
        for i, (src, dst, peer, _) in enumerate(hop1):
            _remote(src, dst, send1.at[i], recv1.at[i], peer).wait_send()
        for cp in local:
            cp.wait()

    launch()
    return [land[...] for land in lands]


def _sequencer_exchange(sources, land_shapes, ncopies, plan, local_plan, peers, *, collective_id, name):
    srcs = [jax.new_ref(v, memory_space=pltpu.MemorySpace.HBM) for v in sources]
    lands = [jax.empty_ref(s, memory_space=pltpu.MemorySpace.HBM) for s in land_shapes]
    nlocal = len(local_plan(srcs + lands))
    dma = pltpu.SemaphoreType.DMA

    @pl.kernel(mesh=plsc.ScalarSubcoreMesh(axis_name="sequencer", num_cores=1), name=name,
               scratch_types=(dma((ncopies,)), dma((ncopies,)), dma((max(nlocal, 1),))),
               compiler_params=pltpu.CompilerParams(collective_id=collective_id))
    def launch(send_sems, recv_sems, local_sems):
        _handshake(peers(_place()))
        copies = plan(srcs + lands)
        local = [pltpu.make_async_copy(src, dst, local_sems.at[j])
                 for j, (src, dst) in enumerate(local_plan(srcs + lands))]
        for cp in local:
            cp.start()
        for i, (src, dst, peer, _) in enumerate(copies):
            _remote(src, dst, send_sems.at[i], recv_sems.at[i], peer).start()
        for i, (src, _, peer, incoming) in enumerate(copies):
            cp = _remote(src, incoming, send_sems.at[i], recv_sems.at[i], peer)
            cp.wait_send()
            cp.wait_recv()
        for cp in local:
            cp.wait()

    launch()
    return [land[...] for land in lands]


def _sequencer_scatter_hop2(sums, *, collective_id, name):
    n = len(sums)
    shapes = [jax.ShapeDtypeStruct(v.shape, v.dtype) for v in sums]
    return _sequencer_exchange(sums, shapes, 3 * n, _scatter2_plan(n), _scatter2_local(n),
                               lambda place: [peer for peer, _ in place[4]], collective_id=collective_id, name=name)


N_CHIP = N_DEV // 2


def _scatter1_plan(n):
    def plan(refs):
        _, _, c, sibling, _ = _place()
        out = []
        for wi in range(n):
            parts, half = refs[wi], refs[n + wi]
            for chip in range(N_CHIP):
                out.append((parts.at[2 * chip + 1 - c], half.at[chip], sibling, half.at[chip]))
        return out

    return plan


def _scatter2_plan(n):
    def plan(refs):
        _, my_chip, _, _, others = _place()
        out = []
        for wi in range(n):
            sums, recv = refs[wi], refs[n + wi]
            for peer, chip in others:
                out.append((sums.at[chip], recv.at[my_chip], peer, recv.at[chip]))
        return out

    return plan


def _scatter2_local(n):
    def plan(refs):
        my_chip = _place()[1]
        return [(refs[wi].at[my_chip], refs[n + wi].at[my_chip]) for wi in range(n)]

    return plan


def _pair_add(parts, half, core, *, name):
    _, r, c = parts.shape
    tr = max(d for d in range(HALO, 257, HALO) if r % d == 0) if r > 256 else r
    parts4 = parts.reshape(N_CHIP, 2, r, c)

    def body(core_ref, p_ref, h_ref, o_ref):
        o_ref[...] = (p_ref[:, 0].astype(F32) + h_ref[...].astype(F32)).astype(o_ref.dtype)

    return pl.pallas_call(
        body, name=name,
        grid_spec=pltpu.PrefetchScalarGridSpec(
            num_scalar_prefetch=1, grid=(r // tr,),
            in_specs=[pl.BlockSpec((N_CHIP, 1, tr, c), lambda i, core_ref: (0, core_ref[0], i, 0)),
                      pl.BlockSpec((N_CHIP, tr, c), lambda i, core_ref: (0, i, 0))],
            out_specs=pl.BlockSpec((N_CHIP, tr, c), lambda i, core_ref: (0, i, 0))),
        out_shape=jax.ShapeDtypeStruct((N_CHIP, r, c), parts.dtype), compiler_params=_cp("parallel"),
    )(core, parts4, half)


def _scatter_start(parts, *, name):
    halves = [lax.empty((N_CHIP,) + v.shape[1:], v.dtype) for v in parts]
    return _split_start(parts + halves, N_CHIP * len(parts), _scatter1_plan(len(parts)), name=name)


def _adamw(parts, w, m, v, *, name):
    r, c = w.shape
    nparts = parts.shape[0]
    tr = max(d for d in range(HALO, 129, HALO) if r % d == 0) if r > 128 else r

    def body(p_ref, w_ref, m_ref, v_ref, g_ref, d_ref, mo_ref, vo_ref):
        g = p_ref[0].astype(F32)
        for k in range(1, nparts):
            g = g + p_ref[k].astype(F32)
        mn = ADAM_B1 * m_ref[...] + (1.0 - ADAM_B1) * g
        vn = ADAM_B2 * v_ref[...] + (1.0 - ADAM_B2) * (g * g)
        m_hat = mn / (1.0 - ADAM_B1 ** ADAM_STEP)
        v_hat = vn / (1.0 - ADAM_B2 ** ADAM_STEP)
        g_ref[...] = g
        d_ref[...] = -ADAM_LR * (m_hat / (jnp.sqrt(v_hat) + ADAM_EPS) + ADAM_WD * w_ref[...])
        mo_ref[...] = mn
        vo_ref[...] = vn

    blk = pl.BlockSpec((tr, c), lambda i: (i, 0))
    return pl.pallas_call(
        body, name=name, grid=(r // tr,), in_specs=[pl.BlockSpec((nparts, tr, c), lambda i: (0, i, 0)), blk, blk, blk],
        out_specs=[blk] * 4, out_shape=[jax.ShapeDtypeStruct((r, c), F32)] * 4, compiler_params=_cp("parallel"),
    )(parts, w, m, v)


def _pack_rows(vs, rows):
    lead = vs[0].shape[:-1] if vs[0].ndim > 1 else ()
    flat = jnp.concatenate(vs, axis=-1)
    pad = rows * LANES - flat.shape[-1]
    flat = jnp.pad(flat, [(0, 0)] * len(lead) + [(0, pad)])
    return flat.reshape(lead + (rows, LANES))


def kernel(x, p, positions, mix_norm_w, w_in, conv_w, conv_b, dt_bias, a_log, d_skip, ssd_norm_w, q_a_norm_w, w_q_b, kv_a_norm_w, w_kv_b, w_out, ffn_norm_w, w_ffn_up, ffn_conv_w, ffn_conv_b, w_ffn_down, ple_norm_w, w_ple_gate, b_ple_gate, w_ple_proj, ple_post_norm_w, final_norm_w, loss_target, m_mix_norm_w, m_w_in, m_conv_w, m_conv_b, m_dt_bias, m_a_log, m_d_skip, m_ssd_norm_w, m_q_a_norm_w, m_w_q_b, m_kv_a_norm_w, m_w_kv_b, m_w_out, m_ffn_norm_w, m_w_ffn_up, m_ffn_conv_w, m_ffn_conv_b, m_w_ffn_down, m_ple_norm_w, m_w_ple_gate, m_b_ple_gate, m_w_ple_proj, m_ple_post_norm_w, m_final_norm_w, v_mix_norm_w, v_w_in, v_conv_w, v_conv_b, v_dt_bias, v_a_log, v_d_skip, v_ssd_norm_w, v_q_a_norm_w, v_w_q_b, v_kv_a_norm_w, v_w_kv_b, v_w_out, v_ffn_norm_w, v_w_ffn_up, v_ffn_conv_w, v_ffn_conv_b, v_w_ffn_down, v_ple_norm_w, v_w_ple_gate, v_b_ple_gate, v_w_ple_proj, v_ple_post_norm_w, v_final_norm_w):
    given = dict(locals())
    shapes = {n: given[n].shape for n in WEIGHTS}
    w2 = {n: given[n].reshape(given[n].shape[-2:] if n in BIG or n in CONV else (1, -1)) for n in WEIGHTS}
    m2 = {n: given['m_' + n].reshape(w2[n].shape) for n in WEIGHTS}
    v2 = {n: given['v_' + n].reshape(w2[n].shape) for n in WEIGHTS}
    me = 4 * lax.axis_index("x") + 2 * lax.axis_index("y") + lax.axis_index("c")

    core = lax.axis_index("c").astype(jnp.int32).reshape(1)

    def shards(grp, zero):
        return [(w2[n] + zero).astype(BF16) if n in BIG else w2[n] + zero for n in WEIGHT_GROUPS[grp]]

    first, token = _gather_start(shards('a', 0.0), name="gather_a_hop1")
    first, token = _gather_forward(first, token, name="gather_a_hop2")
    zero = token[0, 0]
    later = dict(zip(WEIGHT_GROUPS['b'], _sequencer_gather(shards('b', zero), collective_id=1, name="gather_b")))
    later.update(zip(WEIGHT_GROUPS['c'], _sequencer_gather(shards('c', zero), collective_id=2, name="gather_c")))

    def get_w(grp, after):
        if grp == 'a':
            lands = dict(zip(WEIGHT_GROUPS[grp], _gather_finish(first, token, name="gather_a_done")))
        else:
            names = WEIGHT_GROUPS[grp]
            lands = dict(zip(names, _hold([later[n] for n in names], after, name="gather_" + grp + "_use")))
        return _assemble_weights(lands)

    scatters = {}

    hop_ids = {grp: 2 + 2 * i for i, grp in enumerate(GRAD_GROUPS)}

    def zero_of(arrays):
        return sum(v[(0,) * v.ndim].astype(F32) * 0.0 for v in arrays)

    def emit(grp, grads):
        scatters[grp], tok = _scatter_start([grads[n] for n in GRAD_GROUPS[grp]], name="scatter_" + grp + "_hop1")
        return tok[0, 0]

    def relay(grp, after):
        n = len(GRAD_GROUPS[grp])
        bufs = _split_wait(scatters[grp], after, _scatter1_plan(n), lambda refs: [], name="scatter_" + grp + "_hop1_wait")
        sums = [_pair_add(bufs[i], bufs[n + i], core, name="scatter_%s_add%d" % (grp, i)) for i in range(n)]
        scatters[grp] = _sequencer_scatter_hop2(sums, collective_id=hop_ids[grp] + 1, name="scatter_" + grp + "_hop2")
        return zero_of(sums)

    out_g, out_d, out_m, out_v = {}, {}, {}, {}

    def settle(grp):
        return zero_of(scatters[grp])

    def update(grp, behind=None):
        for n, parts in zip(GRAD_GROUPS[grp], scatters[grp]):
            wn = w2[n] if behind is None else w2[n] + behind
            out_g[n], out_d[n], out_m[n], out_v[n] = _adamw(parts, wn, m2[n], v2[n], name="adamw_" + n)

    vecs = {n: w2[n] for n in REPL}
    vecs['mix_norm_w'] = vecs['mix_norm_w'] + zero
    loss, dx, g_conv, g_vec = _local_step(x[0], p[0, 0], _rope_tables(positions), get_w, vecs, loss_target[0], emit,
                                          relay, settle)
    n_small = sum(g_vec[n].shape[1] for n in REPL) + sum(g_conv[n].size for n in CONV) + 1
    rows_small = -(-n_small // (LANES * HALO)) * HALO
    small = _pack_rows([g_vec[n] for n in REPL] + [g_conv[n].reshape(1, -1) for n in CONV] + [loss], rows_small)

    for grp in list(GRAD_GROUPS)[:-1]:
        update(grp)
    all_small = _exchange([small], gather=True, name="gather_small_grads")[0].reshape(N_DEV, rows_small * LANES)
    update(list(GRAD_GROUPS)[-1], zero_of([all_small]))
    pieces, off = [], 0
    for n in REPL:
        k = g_vec[n].shape[1]
        pieces.append(all_small[:, off:off + k])
        off += k
    for n in CONV:
        kw, cols = g_conv[n].shape
        full = all_small[:, off:off + kw * cols].reshape(N_DEV, kw, cols)
        mine = lax.dynamic_slice_in_dim(full, me * (cols // N_DEV), cols // N_DEV, axis=2)
        pieces.append(mine.reshape(N_DEV, kw * (cols // N_DEV)))
        off += kw * cols
    pieces.append(all_small[:, off:off + 1])
    small_names = REPL + CONV
    n_mine = sum(q.shape[1] for q in pieces)
    rows_mine = -(-n_mine // (LANES * HALO)) * HALO
    zero = jnp.zeros((1, 1), F32)
    packed = [_pack_rows([src[n].reshape(1, -1) for n in small_names] + [zero], rows_mine).reshape(rows_mine, LANES)
              for src in (w2, m2, v2)]
    sg, sd, sm, sv = _adamw(_pack_rows(pieces, rows_mine), *packed, name="adamw_small")
    off = 0
    for n in small_names:
        k = w2[n].size
        for dst, src in ((out_g, sg), (out_d, sd), (out_m, sm), (out_v, sv)):
            dst[n] = src.reshape(-1)[off:off + k].reshape(w2[n].shape)
        off += k
    total_loss = sg.reshape(-1)[off]

    outs = [total_loss, dx[None]]
    for res in (out_g, out_d, out_m, out_v):
        outs += [res[n].reshape(shapes[n]) for n in WEIGHTS]
    return tuple(outs)
```

```python
import math

import numpy as np
import jax
import jax.numpy as jnp
from jax import lax
from jax.experimental import pallas as pl
from jax.experimental.pallas import tpu as pltpu
from jax.experimental.pallas import tpu_sc as plsc

F32 = jnp.float32
BF16 = jnp.bfloat16
HI = lax.Precision.HIGHEST

D_MODEL = 2048
CHUNK = 64
D_SSM = 1024
SSD_P = 64
SSD_HEADS = 16
SSD_GROUPS = 2
SSD_N = 128
SSD_CONV = 4
SSD_CONV_DIM = D_SSM + 2 * SSD_GROUPS * SSD_N
MLA_HEADS = 8
MLA_NOPE = 128
MLA_ROPE = 64
MLA_V = 128
MLA_Q_RANK = 512
MLA_KV_RANK = 256
MLA_QK_PAD = 256
ROPE_THETA = 10000.0
D_FF = 5632
FFN_CONV = 3
PLE_DIM = 256
NORM_EPS = 1e-6
ADAM_LR, ADAM_B1, ADAM_B2, ADAM_EPS, ADAM_WD, ADAM_STEP = 0.001, 0.9, 0.999, 1e-08, 0.01, 10
N_DEV = 8

OFF_Z, OFF_XBC, OFF_QA, OFF_CKV, OFF_KR, OFF_DT, D_IN_PAD = 0, 1024, 2560, 3072, 3328, 3456, 3584
D_IN = 3408
LANES = 128
HALO = 8
VMEM_LIMIT = 56 * 1024 * 1024
FFN_TC = D_FF * 2 // N_DEV
FFN_PERM = (0, 4, 1, 5, 2, 6, 3, 7)
NEG = -1e30


def _cp(*sem):
    return pltpu.CompilerParams(dimension_semantics=tuple(sem), vmem_limit_bytes=VMEM_LIMIT)


def _tile(n, want):
    if n <= want:
        return n
    best = max(d for d in range(LANES, want + 1, LANES) if n % d == 0)
    return best


def _sigmoid(x):
    return 0.5 * (jnp.tanh(0.5 * x) + 1.0)


def _silu(x):
    return x * _sigmoid(x)


def _dsilu(x):
    s = _sigmoid(x)
    return s * (1.0 + x * (1.0 - s))


MM_TILE = 1408
MM_TK = 2816


def _matmul(a, b, *, ta=False, tb=False, out_dtype=F32, add=None, bias=None, tm=MM_TILE, tn=MM_TILE, tk=MM_TK, name,
            mnk=None, a_spec=None, b_spec=None, o_spec=None, o_shape=None):
    if mnk is None:
        m, k = (a.shape[1], a.shape[0]) if ta else a.shape
        n = b.shape[0] if tb else b.shape[1]
        assert k == (b.shape[1] if tb else b.shape[0])
    else:
        m, n, k = mnk
    tm, tn, tk = _tile(m, tm), _tile(n, tn), _tile(k, tk)
    nk = k // tk
    dims = (((0 if ta else 1,), (1 if tb else 0,)), ((), ()))

    def body(*refs):
        a_ref, b_ref = refs[0], refs[1]
        pos = 2
        add_ref = bias_ref = None
        if add is not None:
            add_ref = refs[pos]
            pos += 1
        if bias is not None:
            bias_ref = refs[pos]
            pos += 1
        o_ref = refs[pos]
        kk = pl.program_id(2)
        av = a_ref[...]
        bv = b_ref[...]
        av = av.reshape(av.shape[-2:]).astype(BF16)
        bv = bv.reshape(bv.shape[-2:]).astype(BF16)
        prod = lax.dot_general(av, bv, dims, preferred_element_type=F32)

        def finish(r):
            if bias_ref is not None:
                r = r + bias_ref[...]
            if add_ref is not None:
                r = r + add_ref[...].astype(F32)
            o_ref[...] = r.astype(out_dtype).reshape(o_ref.shape)

        if nk == 1:
            finish(prod)
        else:
            acc_ref = refs[pos + 1]

            @pl.when(kk == 0)
            def _():
                acc_ref[...] = prod

            @pl.when(kk > 0)
            def _():
                acc_ref[...] += prod

            @pl.when(kk == nk - 1)
            def _():
                finish(acc_ref[...])

    if a_spec is None:
        a_spec = (pl.BlockSpec((tk, tm), lambda i, j, kk: (kk, i)) if ta
                  else pl.BlockSpec((tm, tk), lambda i, j, kk: (i, kk)))
    if b_spec is None:
        b_spec = (pl.BlockSpec((tn, tk), lambda i, j, kk: (j, kk)) if tb
                  else pl.BlockSpec((tk, tn), lambda i, j, kk: (kk, j)))
    if o_spec is None:
        o_spec = pl.BlockSpec((tm, tn), lambda i, j, kk: (i, j))
    if o_shape is None:
        o_shape = (m, n)
    in_specs = [a_spec, b_spec]
    args = [a, b]
    if add is not None:
        in_specs.append(pl.BlockSpec((tm, tn), lambda i, j, kk: (i, j)))
        args.append(add)
    if bias is not None:
        in_specs.append(pl.BlockSpec((1, tn), lambda i, j, kk: (0, j)))
        args.append(bias)
    return pl.pallas_call(
        body, name=name, grid=(m // tm, n // tn, nk), in_specs=in_specs, out_specs=o_spec,
        out_shape=jax.ShapeDtypeStruct(o_shape, out_dtype),
        scratch_shapes=[pltpu.VMEM((tm, tn), F32)] if nk > 1 else [],
        compiler_params=_cp("parallel", "parallel", "arbitrary"),
    )(*args)


def _rmsnorm_fwd(x, w, *, width, cblk=0, out_dtype=BF16, tr=256, name):
    t = x.shape[0]

    def body(x_ref, w_ref, o_ref):
        xv = x_ref[...].astype(F32)
        r = lax.rsqrt(jnp.mean(xv * xv, axis=-1, keepdims=True) + NORM_EPS)
        o_ref[...] = (xv * r * w_ref[...]).astype(out_dtype)

    return pl.pallas_call(
        body, name=name, grid=(t // tr,),
        in_specs=[pl.BlockSpec((tr, width), lambda i: (i, cblk)), pl.BlockSpec((1, width), lambda i: (0, 0))],
        out_specs=pl.BlockSpec((tr, width), lambda i: (i, 0)),
        out_shape=jax.ShapeDtypeStruct((t, width), out_dtype),
        compiler_params=_cp("parallel"),
    )(x, w)


def _rmsnorm_bwd(x, w, dy, add=None, *, width, cblk=0, out_dtype=F32, also_bf16=False, tr=256, name):
    t = x.shape[0]

    def body(*refs):
        refs = list(refs)
        dxb_ref = refs.pop() if also_bf16 else None
        if add is None:
            x_ref, w_ref, dy_ref, dx_ref, dw_ref = refs
            add_ref = None
        else:
            x_ref, w_ref, dy_ref, add_ref, dx_ref, dw_ref = refs
        xv = x_ref[...].astype(F32)
        dyv = dy_ref[...].astype(F32)
        r = lax.rsqrt(jnp.mean(xv * xv, axis=-1, keepdims=True) + NORM_EPS)
        xh = xv * r
        g = dyv * w_ref[...]
        dx = r * (g - xh * jnp.mean(g * xh, axis=-1, keepdims=True))
        if add_ref is not None:
            dx = dx + add_ref[...].astype(F32)
        dx_ref[...] = dx.astype(out_dtype)
        if dxb_ref is not None:
            dxb_ref[...] = dx.astype(BF16)

        @pl.when(pl.program_id(0) == 0)
        def _():
            dw_ref[...] = jnp.zeros_like(dw_ref)

        dw_ref[...] += jnp.sum(dyv * xh, axis=0, keepdims=True)

    in_specs = [pl.BlockSpec((tr, width), lambda i: (i, cblk)), pl.BlockSpec((1, width), lambda i: (0, 0)),
                pl.BlockSpec((tr, width), lambda i: (i, 0))]
    args = [x, w, dy]
    if add is not None:
        in_specs.append(pl.BlockSpec((tr, width), lambda i: (i, 0)))
        args.append(add)
    blk = pl.BlockSpec((tr, width), lambda i: (i, 0))
    return pl.pallas_call(
        body, name=name, grid=(t // tr,), in_specs=in_specs,
        out_specs=[blk, pl.BlockSpec((1, width), lambda i: (0, 0))] + ([blk] if also_bf16 else []),
        out_shape=[jax.ShapeDtypeStruct((t, width), out_dtype), jax.ShapeDtypeStruct((1, width), F32)]
        + ([jax.ShapeDtypeStruct((t, width), BF16)] if also_bf16 else []),
        compiler_params=_cp("arbitrary"),
    )(*args)


def _shift_down(prev_halo, cur, j):
    if j == 0:
        return cur
    ext = jnp.concatenate([prev_halo, cur], axis=0)
    return pltpu.roll(ext, j, axis=0)[HALO:]


def _shift_up(cur, next_halo, j):
    if j == 0:
        return cur
    ext = jnp.concatenate([cur, next_halo], axis=0)
    return pltpu.roll(ext, ext.shape[0] - j, axis=0)[:cur.shape[0]]


def _conv_rows(prev, cur, w, b, kw):
    shifted = [cur]
    out = b + w[kw - 1:kw] * cur
    for j in range(1, kw):
        sh = _shift_down(prev, cur, j)
        shifted.append(sh)
        out = out + w[kw - 1 - j:kw - j] * sh
    return out, shifted


def _act_fwd(c, glu):
    if glu:
        half = c.shape[1] // 2
        return _silu(c[:, :half]) * c[:, half:]
    return _silu(c)


def _act_bwd(c, dout, glu):
    if glu:
        half = c.shape[1] // 2
        g, up = c[:, :half], c[:, half:]
        s = _sigmoid(g)
        gs = g * s
        return jnp.concatenate([dout * up * (s + gs * (1.0 - s)), dout * gs], axis=1)
    return dout * _dsilu(c)


def _conv_act_fwd(u, w, b, *, kw, glu, tc, coff, ncols, out_dtype, tr=256, name):
    t = u.shape[0]
    nb = ncols // tc
    oc = tc // 2 if glu else tc

    def body(u_ref, uh_ref, w_ref, b_ref, o_ref):
        prev = jnp.where(pl.program_id(0) == 0, 0.0, uh_ref[...])
        c, _ = _conv_rows(prev, u_ref[...], w_ref[...], b_ref[...], kw)
        o_ref[...] = _act_fwd(c, glu).astype(out_dtype)

    return pl.pallas_call(
        body, name=name, grid=(t // tr, nb),
        in_specs=[pl.BlockSpec((tr, tc), lambda i, j: (i, j + coff)),
                  pl.BlockSpec((HALO, tc), lambda i, j: (jnp.maximum(i * (tr // HALO) - 1, 0), j + coff)),
                  pl.BlockSpec((kw, tc), lambda i, j: (0, j)), pl.BlockSpec((1, tc), lambda i, j: (0, j))],
        out_specs=pl.BlockSpec((tr, oc), lambda i, j: (i, j)),
        out_shape=jax.ShapeDtypeStruct((t, nb * oc), out_dtype),
        compiler_params=_cp("parallel", "parallel"),
    )(u, u, w, b)


def _conv_act_bwd(u, w, b, dout, *, kw, glu, tc, coff, ncols, tr=256, name):
    t = u.shape[0]
    nb = ncols // tc
    nt = t // tr
    oc = tc // 2 if glu else tc

    def body(u_ref, up_ref, un_ref, d_ref, dn_ref, w_ref, b_ref, du_ref, dw_ref, db_ref):
        i = pl.program_id(1)
        cur, nxt, wv, bv = u_ref[...], un_ref[...], w_ref[...], b_ref[...]
        prev = jnp.where(i == 0, 0.0, up_ref[...])
        c_cur, shifted = _conv_rows(prev, cur, wv, bv, kw)
        c_nxt, _ = _conv_rows(cur[tr - HALO:], nxt, wv, bv, kw)
        d_cur = _act_bwd(c_cur, d_ref[...].astype(F32), glu)
        d_nxt = _act_bwd(c_nxt, jnp.where(i == nt - 1, 0.0, dn_ref[...].astype(F32)), glu)
        du = wv[kw - 1:kw] * d_cur
        for j in range(1, kw):
            du = du + wv[kw - 1 - j:kw - j] * _shift_up(d_cur, d_nxt, j)
        du_ref[...] = du.astype(BF16)

        @pl.when(i == 0)
        def _():
            dw_ref[...] = jnp.zeros_like(dw_ref)
            db_ref[...] = jnp.zeros_like(db_ref)

        db_ref[...] += jnp.sum(d_cur, axis=0, keepdims=True)
        dw_ref[...] += jnp.concatenate(
            [jnp.sum(d_cur * shifted[kw - 1 - k], axis=0, keepdims=True) for k in range(kw)], axis=0)

    nh = tr // HALO
    return pl.pallas_call(
        body, name=name, grid=(nb, nt),
        in_specs=[pl.BlockSpec((tr, tc), lambda j, i: (i, j + coff)),
                  pl.BlockSpec((HALO, tc), lambda j, i: (jnp.maximum(i * nh - 1, 0), j + coff)),
                  pl.BlockSpec((HALO, tc), lambda j, i: (jnp.minimum((i + 1) * nh, t // HALO - 1), j + coff)),
                  pl.BlockSpec((tr, oc), lambda j, i: (i, j)),
                  pl.BlockSpec((HALO, oc), lambda j, i: (jnp.minimum((i + 1) * nh, t // HALO - 1), j)),
                  pl.BlockSpec((kw, tc), lambda j, i: (0, j)), pl.BlockSpec((1, tc), lambda j, i: (0, j))],
        out_specs=[pl.BlockSpec((tr, tc), lambda j, i: (i, j)), pl.BlockSpec((kw, tc), lambda j, i: (0, j)),
                   pl.BlockSpec((1, tc), lambda j, i: (0, j))],
        out_shape=[jax.ShapeDtypeStruct((t, ncols), BF16), jax.ShapeDtypeStruct((kw, ncols), F32),
                   jax.ShapeDtypeStruct((1, ncols), F32)],
        compiler_params=_cp("parallel", "arbitrary"),
    )(u, u, u, dout, dout, w, b)


def _ple_loss(x2, gl, pe, pw, fw, target, *, tr=256, name):
    t, d = x2.shape

    def body(x_ref, gl_ref, pe_ref, pw_ref, fw_ref, t_ref, l_ref, dx_ref, dfw_ref, dgl_ref, db_ref, dpe_ref, dpw_ref):
        pv, pwv, wv = pe_ref[...], pw_ref[...], fw_ref[...]
        gate = _sigmoid(gl_ref[...])
        rp = lax.rsqrt(jnp.mean(pv * pv, axis=-1, keepdims=True) + NORM_EPS)
        ph = pv * rp
        e = ph * pwv
        x3 = x_ref[...] + gate * e
        r = lax.rsqrt(jnp.mean(x3 * x3, axis=-1, keepdims=True) + NORM_EPS)
        xh = x3 * r
        err = xh * wv - t_ref[...]
        dy = err * (1.0 / d)
        g = dy * wv
        dx = r * (g - xh * jnp.mean(g * xh, axis=-1, keepdims=True))
        dx_ref[...] = dx
        dgl = dx * e * gate * (1.0 - gate)
        de = dx * gate
        gg = de * pwv
        dgl_ref[...] = dgl.astype(BF16)
        dpe_ref[...] = (rp * (gg - ph * jnp.mean(gg * ph, axis=-1, keepdims=True))).astype(BF16)

        @pl.when(pl.program_id(0) == 0)
        def _():
            for ref in (l_ref, dfw_ref, db_ref, dpw_ref):
                ref[...] = jnp.zeros_like(ref)

        l_ref[...] += 0.5 * jnp.sum(jnp.mean(err * err, axis=-1, keepdims=True), axis=0, keepdims=True)
        dfw_ref[...] += jnp.sum(dy * xh, axis=0, keepdims=True)
        db_ref[...] += jnp.sum(dgl, axis=0, keepdims=True)
        dpw_ref[...] += jnp.sum(de * ph, axis=0, keepdims=True)

    blk = pl.BlockSpec((tr, d), lambda i: (i, 0))
    row = pl.BlockSpec((1, d), lambda i: (0, 0))
    rowf = jax.ShapeDtypeStruct((1, d), F32)
    return pl.pallas_call(
        body, name=name, grid=(t // tr,), in_specs=[blk, blk, blk, row, row, blk],
        out_specs=[pl.BlockSpec((1, 1), lambda i: (0, 0)), blk, row, blk, row, blk, row],
        out_shape=[jax.ShapeDtypeStruct((1, 1), F32), jax.ShapeDtypeStruct((t, d), F32), rowf,
                   jax.ShapeDtypeStruct((t, d), BF16), rowf, jax.ShapeDtypeStruct((t, d), BF16), rowf],
        compiler_params=_cp("arbitrary"),
    )(x2, gl, pe, pw, fw, target)


def _rope(blk, tab_ref):
    return blk * tab_ref[0] + pltpu.roll(blk, 96, axis=1) * tab_ref[1] + pltpu.roll(blk, 32, axis=1) * tab_ref[2]


def _unrope(g, tab_ref):
    return g * tab_ref[0] + pltpu.roll(g * tab_ref[1], 32, axis=1) + pltpu.roll(g * tab_ref[2], 96, axis=1)


def _mla_prep(q, kv, proj, tabs, *, tr=512, name):
    t = q.shape[0]

    def body(q_ref, kv_ref, kr_ref, tab_ref, qo_ref, ko_ref, vo_ref, vt_ref):
        qv, kvv = q_ref[...], kv_ref[...]
        qo_ref[0, :, :MLA_NOPE] = qv[:, :MLA_NOPE].astype(BF16)
        qo_ref[0, :, MLA_NOPE:] = _rope(qv[:, MLA_NOPE:], tab_ref).astype(BF16)
        ko_ref[0, :, :MLA_NOPE] = kvv[:, :MLA_NOPE].astype(BF16)
        ko_ref[0, :, MLA_NOPE:] = _rope(kr_ref[...], tab_ref).astype(BF16)
        vo_ref[0] = kvv[:, MLA_NOPE:].astype(BF16)
        for blk in range(tr // ATT_BLK):
            vt_ref[0, blk] = kvv[blk * ATT_BLK:(blk + 1) * ATT_BLK, MLA_NOPE:].T.astype(BF16)

    return pl.pallas_call(
        body, name=name, grid=(t // tr, MLA_HEADS),
        in_specs=[pl.BlockSpec((tr, MLA_QK_PAD), lambda i, h: (i, h)),
                  pl.BlockSpec((tr, MLA_NOPE + MLA_V), lambda i, h: (i, h)),
                  pl.BlockSpec((tr, LANES), lambda i, h: (i, OFF_KR // LANES)),
                  pl.BlockSpec((3, tr, LANES), lambda i, h: (0, i, 0))],
        out_specs=[pl.BlockSpec((1, tr, MLA_QK_PAD), lambda i, h: (h, i, 0)),
                   pl.BlockSpec((1, tr, MLA_QK_PAD), lambda i, h: (h, i, 0)),
                   pl.BlockSpec((1, tr, MLA_V), lambda i, h: (h, i, 0)),
                   pl.BlockSpec((1, tr // ATT_BLK, MLA_V, ATT_BLK), lambda i, h: (h, i, 0, 0))],
        out_shape=[jax.ShapeDtypeStruct((MLA_HEADS, t, MLA_QK_PAD), BF16),
                   jax.ShapeDtypeStruct((MLA_HEADS, t, MLA_QK_PAD), BF16),
                   jax.ShapeDtypeStruct((MLA_HEADS, t, MLA_V), BF16),
                   jax.ShapeDtypeStruct((MLA_HEADS, t // ATT_BLK, MLA_V, ATT_BLK), BF16)],
        compiler_params=_cp("parallel", "parallel"),
    )(q, kv, proj, tabs)


def _mla_unprep(dq3, dk3, dv3, tabs, *, tr=256, name):
    t = dq3.shape[1]

    def body(dq_ref, dk_ref, dv_ref, tab_ref, qo_ref, kvo_ref, kro_ref):
        kr = jnp.zeros((tr, LANES), F32)
        for h in range(MLA_HEADS):
            c0 = h * MLA_QK_PAD
            qo_ref[:, c0:c0 + MLA_NOPE] = dq_ref[h, :, :MLA_NOPE].astype(BF16)
            qo_ref[:, c0 + MLA_NOPE:c0 + MLA_QK_PAD] = _unrope(dq_ref[h, :, MLA_NOPE:], tab_ref).astype(BF16)
            kvo_ref[:, c0:c0 + MLA_NOPE] = dk_ref[h, :, :MLA_NOPE].astype(BF16)
            kvo_ref[:, c0 + MLA_NOPE:c0 + MLA_QK_PAD] = dv_ref[h].astype(BF16)
            kr = kr + dk_ref[h, :, MLA_NOPE:]
        kro_ref[...] = _unrope(kr, tab_ref).astype(BF16)

    return pl.pallas_call(
        body, name=name, grid=(t // tr,),
        in_specs=[pl.BlockSpec((MLA_HEADS, tr, MLA_QK_PAD), lambda i: (0, i, 0)),
                  pl.BlockSpec((MLA_HEADS, tr, MLA_QK_PAD), lambda i: (0, i, 0)),
                  pl.BlockSpec((MLA_HEADS, tr, MLA_V), lambda i: (0, i, 0)),
                  pl.BlockSpec((3, tr, LANES), lambda i: (0, i, 0))],
        out_specs=[pl.BlockSpec((tr, MLA_HEADS * MLA_QK_PAD), lambda i: (i, 0)),
                   pl.BlockSpec((tr, MLA_HEADS * MLA_QK_PAD), lambda i: (i, 0)),
                   pl.BlockSpec((tr, LANES), lambda i: (i, 0))],
        out_shape=[jax.ShapeDtypeStruct((t, MLA_HEADS * MLA_QK_PAD), BF16),
                   jax.ShapeDtypeStruct((t, MLA_HEADS * MLA_QK_PAD), BF16),
                   jax.ShapeDtypeStruct((t, LANES), BF16)],
        compiler_params=_cp("parallel"),
    )(dq3, dk3, dv3, tabs)


ATT_BLK = 512
ATT_SCALE = 1.0 / math.sqrt(MLA_NOPE + MLA_ROPE)
_NT = (((1,), (1,)), ((), ()))
_TN = (((0,), (0,)), ((), ()))


def _att_scores_t(k, q, diagonal):
    s = lax.dot_general(k, q, _NT, preferred_element_type=F32) * ATT_SCALE
    if not diagonal:
        return s
    key = lax.broadcasted_iota(jnp.int32, s.shape, 0)
    query = lax.broadcasted_iota(jnp.int32, s.shape, 1)
    return jnp.where((key >> 6) <= (query >> 6), s, NEG)


def _att_rows(i):
    return pl.ds(pl.multiple_of(i * ATT_BLK, ATT_BLK), ATT_BLK)


ATT_HEADS = 2


def _attn_fwd(q3, k3, vt4, *, name):
    t = q3.shape[1]
    nq = t // ATT_BLK

    def body(q_ref, k_ref, vt_ref, o_ref, lse_ref):
        qi = pl.program_id(1)
        qs = [q_ref[hh] for hh in range(ATT_HEADS)]

        def step(j, carry, diagonal=False):
            out = []
            for hh, (m, l, acc) in enumerate(carry):
                s = _att_scores_t(k_ref[hh, _att_rows(j), :], qs[hh], diagonal)
                m_new = jnp.maximum(m, jnp.max(s, axis=0, keepdims=True))
                p = jnp.exp(s - m_new)
                alpha = jnp.exp(m - m_new)
                l = alpha * l + jnp.sum(p, axis=0, keepdims=True)
                acc = alpha * acc + jnp.dot(vt_ref[hh, j], p.astype(BF16), preferred_element_type=F32)
                out.append((m_new, l, acc))
            return tuple(out)

        init = tuple((jnp.full((1, ATT_BLK), NEG, F32), jnp.zeros((1, ATT_BLK), F32),
                      jnp.zeros((MLA_V, ATT_BLK), F32)) for _ in range(ATT_HEADS))
        done = step(qi, lax.fori_loop(0, qi, step, init), diagonal=True)
        for hh, (m, l, acc) in enumerate(done):
            o_ref[:, hh * MLA_V:(hh + 1) * MLA_V] = (acc / l).T
            lse_ref[hh, 0] = m + jnp.log(l)

    return pl.pallas_call(
        body, name=name, grid=(MLA_HEADS // ATT_HEADS, nq),
        in_specs=[pl.BlockSpec((ATT_HEADS, ATT_BLK, MLA_QK_PAD), lambda h, i: (h, i, 0)),
                  pl.BlockSpec((ATT_HEADS, t, MLA_QK_PAD), lambda h, i: (h, 0, 0)),
                  pl.BlockSpec((ATT_HEADS, nq, MLA_V, ATT_BLK), lambda h, i: (h, 0, 0, 0))],
        out_specs=[pl.BlockSpec((ATT_BLK, ATT_HEADS * MLA_V), lambda h, i: (i, h)),
                   pl.BlockSpec((ATT_HEADS, 1, 1, ATT_BLK), lambda h, i: (h, i, 0, 0))],
        out_shape=[jax.ShapeDtypeStruct((t, MLA_HEADS * MLA_V), F32),
                   jax.ShapeDtypeStruct((MLA_HEADS, nq, 1, ATT_BLK), F32)],
        compiler_params=_cp("parallel", "parallel"),
    )(q3, k3, vt4)


def _attn_bwd(q3, k3, v3, o, dcat, lse, *, name):
    t = q3.shape[1]
    nq = t // ATT_BLK
    wide = ATT_HEADS * MLA_V

    def body(q_ref, k_ref, v_ref, o_ref, do_ref, lse_ref, dq_ref, dk_ref, dv_ref, delta_ref):
        kj = pl.program_id(1)

        @pl.when(kj == 0)
        def _():
            dq_ref[...] = jnp.zeros_like(dq_ref)
            ones = jnp.ones((HALO, MLA_V), F32)
            for i in range(nq):
                rows = pl.ds(i * ATT_BLK, ATT_BLK)
                prod = o_ref[rows, :] * do_ref[rows, :]
                for hh in range(ATT_HEADS):
                    delta_ref[hh, i] = lax.dot_general(ones, prod[:, hh * MLA_V:(hh + 1) * MLA_V], _NT, precision=HI,
                                                       preferred_element_type=F32)

        def step(i, carry, diagonal=False):
            rows = _att_rows(i)
            out = []
            for hh, (dk, dv) in enumerate(carry):
                k, v = k_ref[hh], v_ref[hh]
                q = q_ref[hh, rows, :]
                dob = do_ref[rows, hh * MLA_V:(hh + 1) * MLA_V].astype(BF16)
                p = jnp.exp(_att_scores_t(k, q, diagonal) - lse_ref[hh, i])
                dv = dv + jnp.dot(p.astype(BF16), dob, preferred_element_type=F32)
                dp = lax.dot_general(v, dob, _NT, preferred_element_type=F32)
                ds = (p * (dp - delta_ref[hh, i, 0:1, :]) * ATT_SCALE).astype(BF16)
                dk = dk + jnp.dot(ds, q, preferred_element_type=F32)
                dq_ref[hh, rows, :] += lax.dot_general(ds, k, _TN, preferred_element_type=F32)
                out.append((dk, dv))
            return tuple(out)

        init = tuple((jnp.zeros((ATT_BLK, MLA_QK_PAD), F32), jnp.zeros((ATT_BLK, MLA_V), F32))
                     for _ in range(ATT_HEADS))
        done = lax.fori_loop(kj + 1, nq, step, step(kj, init, diagonal=True))
        for hh, (dk, dv) in enumerate(done):
            dk_ref[hh] = dk
            dv_ref[hh] = dv

    return pl.pallas_call(
        body, name=name, grid=(MLA_HEADS // ATT_HEADS, nq),
        in_specs=[pl.BlockSpec((ATT_HEADS, t, MLA_QK_PAD), lambda h, j: (h, 0, 0)),
                  pl.BlockSpec((ATT_HEADS, ATT_BLK, MLA_QK_PAD), lambda h, j: (h, j, 0)),
                  pl.BlockSpec((ATT_HEADS, ATT_BLK, MLA_V), lambda h, j: (h, j, 0)),
                  pl.BlockSpec((t, wide), lambda h, j: (0, h)),
                  pl.BlockSpec((t, wide), lambda h, j: (0, MLA_HEADS // ATT_HEADS + h)),
                  pl.BlockSpec((ATT_HEADS, nq, 1, ATT_BLK), lambda h, j: (h, 0, 0, 0))],
        out_specs=[pl.BlockSpec((ATT_HEADS, t, MLA_QK_PAD), lambda h, j: (h, 0, 0)),
                   pl.BlockSpec((ATT_HEADS, ATT_BLK, MLA_QK_PAD), lambda h, j: (h, j, 0)),
                   pl.BlockSpec((ATT_HEADS, ATT_BLK, MLA_V), lambda h, j: (h, j, 0))],
        out_shape=[jax.ShapeDtypeStruct((MLA_HEADS, t, MLA_QK_PAD), F32),
                   jax.ShapeDtypeStruct((MLA_HEADS, t, MLA_QK_PAD), F32),
                   jax.ShapeDtypeStruct((MLA_HEADS, t, MLA_V), F32)],
        scratch_shapes=[pltpu.VMEM((ATT_HEADS, nq, HALO, ATT_BLK), F32)],
        compiler_params=_cp("parallel", "arbitrary"),
    )(q3, k3, v3, o, dcat, lse)


def _ssd_prep(proj, bias128, alog128, *, name):
    t = proj.shape[0]
    nc = t // CHUNK

    def body(raw_ref, b_ref, al_ref, dt_ref, cs_ref, a_ref):
        xv = raw_ref[...] + b_ref[...]
        dt = jnp.maximum(xv, 0.0) + jnp.log(1.0 + jnp.exp(-jnp.abs(xv)))
        a = -jnp.exp(al_ref[...])
        adt = (dt * a).reshape(nc, CHUNK, LANES)
        li = lax.broadcasted_iota(jnp.int32, (nc, CHUNK, CHUNK), 1)
        si = lax.broadcasted_iota(jnp.int32, (nc, CHUNK, CHUNK), 2)
        tril = jnp.where(si <= li, 1.0, 0.0).astype(F32)
        cs = lax.dot_general(tril, adt, (((2,), (1,)), ((0,), (0,))), precision=HI, preferred_element_type=F32)
        dt_ref[...] = dt
        cs_ref[...] = cs.reshape(t, LANES)
        a_ref[...] = a

    blk = pl.BlockSpec((t, LANES), lambda i: (0, 0))
    row = pl.BlockSpec((1, LANES), lambda i: (0, 0))
    return pl.pallas_call(
        body, name=name, grid=(1,),
        in_specs=[pl.BlockSpec((t, LANES), lambda i: (0, OFF_DT // LANES)), row, row],
        out_specs=[blk, blk, row],
        out_shape=[jax.ShapeDtypeStruct((t, LANES), F32), jax.ShapeDtypeStruct((t, LANES), F32),
                   jax.ShapeDtypeStruct((1, LANES), F32)],
        compiler_params=_cp("arbitrary"),
    )(proj, bias128, alog128)


def _ssd_prep_bwd(ddt128, dadt128, proj, bias128, dt128, a128, dd_h, *, name):
    t = proj.shape[0]

    def body(ddt_ref, dadt_ref, raw_ref, b_ref, dt_ref, a_ref, dd_ref, draw_ref, db_ref, dal_ref, dds_ref):
        draw = ddt_ref[...] * _sigmoid(raw_ref[...] + b_ref[...])
        draw_ref[...] = draw.astype(BF16)
        db_ref[...] = jnp.sum(draw, axis=0, keepdims=True)
        dal_ref[...] = jnp.sum(dadt_ref[...] * dt_ref[...], axis=0, keepdims=True) * a_ref[...]
        dds_ref[...] = jnp.sum(dd_ref[...], axis=-1, keepdims=True)

    blk = pl.BlockSpec((t, LANES), lambda i: (0, 0))
    row = pl.BlockSpec((1, LANES), lambda i: (0, 0))
    return pl.pallas_call(
        body, name=name, grid=(1,),
        in_specs=[blk, blk, pl.BlockSpec((t, LANES), lambda i: (0, OFF_DT // LANES)), row, blk, row,
                  pl.BlockSpec((SSD_HEADS, SSD_P), lambda i: (0, 0))],
        out_specs=[blk, row, row, pl.BlockSpec((SSD_HEADS, 1), lambda i: (0, 0))],
        out_shape=[jax.ShapeDtypeStruct((t, LANES), BF16), jax.ShapeDtypeStruct((1, LANES), F32),
                   jax.ShapeDtypeStruct((1, LANES), F32), jax.ShapeDtypeStruct((SSD_HEADS, 1), F32)],
        compiler_params=_cp("arbitrary"),
    )(ddt128, dadt128, proj, bias128, dt128, a128, dd_h)


def _bdot(a, b, ca, cb, precision=None):
    return lax.dot_general(a, b, (((ca,), (cb,)), ((0,), (0,))), precision=precision, preferred_element_type=F32)


def _pieces(x):
    hi = x.astype(BF16)
    rest = x - hi.astype(F32)
    mid = rest.astype(BF16)
    return hi, mid, (rest - mid.astype(F32)).astype(BF16)


def _bdot_sum(a, b, ca, cb, split):
    other = (b if split == 0 else a).astype(BF16)
    out = None
    for piece in _pieces(a if split == 0 else b):
        term = _bdot(piece, other, ca, cb) if split == 0 else _bdot(other, piece, ca, cb)
        out = term if out is None else out + term
    return out


def _head_matrices():
    eye, zero = jnp.eye(SSD_P, dtype=F32), jnp.zeros((SSD_P, SSD_P), F32)
    pick = jnp.stack([jnp.concatenate([eye, zero], axis=0), jnp.concatenate([zero, eye], axis=0)])
    return pick, pick.transpose(0, 2, 1)


def _move(x, sel):
    selb = sel.astype(BF16)
    hi = x.astype(BF16)
    rest = x - hi.astype(F32)
    mid = rest.astype(BF16)
    low = (rest - mid.astype(F32)).astype(BF16)
    out = jnp.dot(hi, selb, preferred_element_type=F32)
    out = out + jnp.dot(mid, selb, preferred_element_type=F32)
    return out + jnp.dot(low, selb, preferred_element_type=F32)


def _pick_head(pair_ref, pick_ref, h):
    return _move(pair_ref[...], pick_ref[h % 2])


def _place_head(out_ref, val, place_ref, h):
    wide = _move(val, place_ref[h % 2])

    @pl.when(h % 2 == 0)
    def _():
        out_ref[...] = wide

    @pl.when(h % 2 == 1)
    def _():
        out_ref[...] += wide


def _head_column(ref, h, nc):
    v = ref[...]
    mine = lax.broadcasted_iota(jnp.int32, v.shape, 1) == h
    return jnp.sum(jnp.where(mine, v, 0.0), axis=1, keepdims=True).reshape(nc, CHUNK, 1)


def _ssd_common(x2, dt_ref, cs_ref, csr_ref, b_ref, c_ref, nc, h):
    x = x2.reshape(nc, CHUNK, SSD_P)
    dt = _head_column(dt_ref, h, nc)
    cs = _head_column(cs_ref, h, nc)
    csr = csr_ref[0]
    bm = b_ref[...].reshape(nc, CHUNK, SSD_N).astype(BF16)
    cm = c_ref[...].reshape(nc, CHUNK, SSD_N).astype(BF16)
    li = lax.broadcasted_iota(jnp.int32, (nc, CHUNK, CHUNK), 1)
    si = lax.broadcasted_iota(jnp.int32, (nc, CHUNK, CHUNK), 2)
    lmat = jnp.exp(jnp.where(si <= li, cs - csr, NEG))
    g = _bdot(cm, bm, 2, 2)
    cs_last = jnp.sum(jnp.where(li == CHUNK - 1, cs, 0.0), axis=1, keepdims=True)
    xdt = x * dt
    dec = jnp.exp(cs_last - cs)
    return x, dt, cs, bm, cm, li, si, lmat, g, cs_last, xdt, dec


def _ssd_fwd(xbc, dt128, cs128, cs_row, dskip_h, *, name):
    t = xbc.shape[0]
    nc = t // CHUNK
    hpg = SSD_HEADS // SSD_GROUPS
    pick, place = _head_matrices()

    def body(xs_ref, dt_ref, cs_ref, csr_ref, b_ref, c_ref, dk_ref, pick_ref, place_ref, y_ref, st_ref, sc_ref, cd_ref):
        h = pl.program_id(0)
        x, dt, cs, bm, cm, li, si, lmat, g, cs_last, xdt, dec = _ssd_common(_pick_head(xs_ref, pick_ref, h), dt_ref,
                                                                           cs_ref, csr_ref, b_ref, c_ref, nc, h)
        yd = _bdot((g * lmat).astype(BF16), xdt.astype(BF16), 2, 1)
        sc_ref[...] = _bdot(bm, (dec * xdt).astype(BF16), 1, 1)
        cd_ref[...] = jnp.exp(cs_last)

        def step(c, s):
            st_ref[0, c] = s
            return s * cd_ref[c] + sc_ref[c]

        lax.fori_loop(0, nc, step, jnp.zeros((SSD_N, SSD_P), F32))
        yo = _bdot(cm, st_ref[0].astype(BF16), 2, 1) * jnp.exp(cs)
        _place_head(y_ref, (yd + yo + dk_ref[0] * x).reshape(t, SSD_P), place_ref, h)

    lanes = pl.BlockSpec((t, LANES), lambda h: (0, 0))
    pair = pl.BlockSpec((t, 2 * SSD_P), lambda h: (0, h // 2))
    nxb = D_SSM // SSD_N
    return pl.pallas_call(
        body, name=name, grid=(SSD_HEADS,),
        in_specs=[pair, lanes, lanes, pl.BlockSpec((1, nc, 1, CHUNK), lambda h: (h, 0, 0, 0)),
                  pl.BlockSpec((t, SSD_N), lambda h: (0, nxb + h // hpg)),
                  pl.BlockSpec((t, SSD_N), lambda h: (0, nxb + SSD_GROUPS + h // hpg)),
                  pl.BlockSpec((1, 1, SSD_P), lambda h: (h, 0, 0)),
                  pl.BlockSpec((2, 2 * SSD_P, SSD_P), lambda h: (0, 0, 0)),
                  pl.BlockSpec((2, SSD_P, 2 * SSD_P), lambda h: (0, 0, 0))],
        out_specs=[pair, pl.BlockSpec((1, nc, SSD_N, SSD_P), lambda h: (h, 0, 0, 0))],
        out_shape=[jax.ShapeDtypeStruct((t, D_SSM), F32),
                   jax.ShapeDtypeStruct((SSD_HEADS, nc, SSD_N, SSD_P), F32)],
        scratch_shapes=[pltpu.VMEM((nc, SSD_N, SSD_P), F32), pltpu.VMEM((nc, 1, SSD_P), F32)],
        compiler_params=_cp("arbitrary"),
    )(xbc, dt128, cs128, cs_row, xbc, xbc, dskip_h, pick, place)


def _ssd_bwd(xbc, dt128, cs128, cs_row, dskip_h, a_h, states, dy, *, name):
    t = xbc.shape[0]
    nc = t // CHUNK
    hpg = SSD_HEADS // SSD_GROUPS
    pick, place = _head_matrices()

    def body(xs_ref, dt_ref, cs_ref, csr_ref, b_ref, c_ref, dk_ref, a_ref, st_ref, dy_ref, pick_ref, place_ref,
             dxs_ref, ddt_ref, dadt_ref, db_ref, dc_ref, dd_ref, dsl_ref, dsc_ref, cd_ref):
        h = pl.program_id(0) * hpg + pl.program_id(1)
        x, dt, cs, bm, cm, li, si, lmat, g, cs_last, xdt, dec = _ssd_common(_pick_head(xs_ref, pick_ref, h), dt_ref,
                                                                           cs_ref, csr_ref, b_ref, c_ref, nc, h)
        dy = _pick_head(dy_ref, pick_ref, h).reshape(nc, CHUNK, SSD_P)
        dyb = dy.astype(BF16)
        xdtb = xdt.astype(BF16)
        sprev = st_ref[0]
        sprevb = sprev.astype(BF16)
        cdec = jnp.exp(cs_last)
        ecs = jnp.exp(cs)
        dw = (ecs * dy).astype(BF16)
        wmat = _bdot(cm, sprevb, 2, 1)
        dcs = jnp.sum(dy * ecs * wmat, axis=2, keepdims=True)
        dcm = _bdot(dw, sprevb, 2, 2)
        dsl_ref[...] = _bdot(cm, dw, 1, 1)
        cd_ref[...] = cdec

        def step(k, ds):
            c = nc - 1 - k
            dsc_ref[c] = ds
            return ds * cd_ref[c] + dsl_ref[c]

        lax.fori_loop(0, nc, step, jnp.zeros((SSD_N, SSD_P), F32))
        dsc = dsc_ref[...]
        dscb = dsc.astype(BF16)
        d_last = jnp.sum(jnp.sum(dsc * sprev, axis=1, keepdims=True) * cdec, axis=2, keepdims=True)
        z = dec * xdt
        dbm = _bdot(z.astype(BF16), dscb, 2, 2)
        dz = _bdot(bm, dscb, 2, 1)
        dxdt = dec * dz
        t2 = jnp.sum(dz * z, axis=2, keepdims=True)
        dcs = dcs - t2
        d_last = d_last + jnp.sum(t2, axis=1, keepdims=True)
        m = g * lmat
        mb = m.astype(BF16)
        dm = _bdot(dyb, xdtb, 2, 2)
        dxdt = dxdt + _bdot(mb, dyb, 1, 1)
        dseg = dm * m
        dcs = dcs + jnp.sum(dseg, axis=2, keepdims=True)
        ones = jnp.ones((nc, CHUNK, SSD_P), F32)
        dcs = dcs - _bdot_sum(dseg, ones, 1, 1, 0)
        dg = (dm * lmat).astype(BF16)
        dcm = dcm + _bdot(dg, bm, 2, 1)
        dbm = dbm + _bdot(dg, cm, 1, 1)
        dcs = dcs + jnp.where(li[:, :, :SSD_P] == CHUNK - 1, d_last, 0.0)
        triu = jnp.where(li <= si, 1.0, 0.0).astype(F32)
        dadt = _bdot_sum(triu, dcs, 2, 1, 1)
        dk = dk_ref[0]
        _place_head(dxs_ref, (dxdt * dt + dk * dy).reshape(t, SSD_P), place_ref, h)
        ddt = jnp.sum(dxdt * x, axis=2, keepdims=True) + dadt * a_ref[0]
        mine = lax.broadcasted_iota(jnp.int32, (t, LANES), 1) == h

        @pl.when(h == 0)
        def _():
            ddt_ref[...] = jnp.zeros_like(ddt_ref)
            dadt_ref[...] = jnp.zeros_like(dadt_ref)

        ddt_ref[...] += jnp.where(mine, jnp.max(ddt, axis=2, keepdims=True).reshape(t, 1), 0.0)
        dadt_ref[...] += jnp.where(mine, jnp.max(dadt, axis=2, keepdims=True).reshape(t, 1), 0.0)
        dd_ref[0] = jnp.sum(jnp.sum(dy * x, axis=1, keepdims=True), axis=0)

        @pl.when(pl.program_id(1) == 0)
        def _():
            db_ref[...] = jnp.zeros_like(db_ref)
            dc_ref[...] = jnp.zeros_like(dc_ref)

        db_ref[...] += dbm.reshape(t, SSD_N)
        dc_ref[...] += dcm.reshape(t, SSD_N)

    head = pl.BlockSpec((1, t, SSD_P), lambda gi, hi: (gi * hpg + hi, 0, 0))
    pair = pl.BlockSpec((t, 2 * SSD_P), lambda gi, hi: (0, (gi * hpg + hi) // 2))
    grp = pl.BlockSpec((t, SSD_N), lambda gi, hi: (0, gi))
    lane = pl.BlockSpec((1, 1, SSD_P), lambda gi, hi: (gi * hpg + hi, 0, 0))
    rows = pl.BlockSpec((t, LANES), lambda gi, hi: (0, 0))
    nxb = D_SSM // SSD_N
    dxs, ddt, dadt, db, dc, dd = pl.pallas_call(
        body, name=name, grid=(SSD_GROUPS, hpg),
        in_specs=[pair, rows, rows, pl.BlockSpec((1, nc, 1, CHUNK), lambda gi, hi: (gi * hpg + hi, 0, 0, 0)),
                  pl.BlockSpec((t, SSD_N), lambda gi, hi: (0, nxb + gi)),
                  pl.BlockSpec((t, SSD_N), lambda gi, hi: (0, nxb + SSD_GROUPS + gi)), lane, lane,
                  pl.BlockSpec((1, nc, SSD_N, SSD_P), lambda gi, hi: (gi * hpg + hi, 0, 0, 0)), pair,
                  pl.BlockSpec((2, 2 * SSD_P, SSD_P), lambda gi, hi: (0, 0, 0)),
                  pl.BlockSpec((2, SSD_P, 2 * SSD_P), lambda gi, hi: (0, 0, 0))],
        out_specs=[pair, rows, rows, grp, grp, lane],
        out_shape=[jax.ShapeDtypeStruct((t, D_SSM), F32)] + [jax.ShapeDtypeStruct((t, LANES), F32)] * 2
        + [jax.ShapeDtypeStruct((t, SSD_GROUPS * SSD_N), F32)] * 2
        + [jax.ShapeDtypeStruct((SSD_HEADS, 1, SSD_P), F32)],
        scratch_shapes=[pltpu.VMEM((nc, SSD_N, SSD_P), F32), pltpu.VMEM((nc, SSD_N, SSD_P), F32),
                        pltpu.VMEM((nc, 1, SSD_P), F32)],
        compiler_params=_cp("arbitrary", "arbitrary"),
    )(xbc, dt128, cs128, cs_row, xbc, xbc, dskip_h, a_h, states, dy, pick, place)
    return jnp.concatenate([dxs, db, dc], axis=1), ddt, dadt, dd


def _ssd_gate_fwd(y, proj, w, *, tr=256, name):
    t = y.shape[0]
    gw = D_SSM // SSD_GROUPS

    def body(y_ref, z_ref, w_ref, o_ref):
        v = y_ref[...] * _silu(z_ref[...])
        for gi in range(SSD_GROUPS):
            vg = v[:, gi * gw:(gi + 1) * gw]
            r = lax.rsqrt(jnp.mean(vg * vg, axis=-1, keepdims=True) + NORM_EPS)
            o_ref[:, gi * gw:(gi + 1) * gw] = (vg * r * w_ref[:, gi * gw:(gi + 1) * gw]).astype(BF16)

    blk = pl.BlockSpec((tr, D_SSM), lambda i: (i, 0))
    return pl.pallas_call(
        body, name=name, grid=(t // tr,), in_specs=[blk, blk, pl.BlockSpec((1, D_SSM), lambda i: (0, 0))],
        out_specs=blk, out_shape=jax.ShapeDtypeStruct((t, D_SSM), BF16), compiler_params=_cp("parallel"),
    )(y, proj, w)


def _ssd_gate_bwd(y, proj, w, dcat, *, tr=256, name):
    t = y.shape[0]
    gw = D_SSM // SSD_GROUPS

    def body(y_ref, z_ref, w_ref, d_ref, dy_ref, dz_ref, dw_ref):
        yv, zv, dv = y_ref[...], z_ref[...], d_ref[...].astype(F32)
        sz = _silu(zv)
        v = yv * sz

        @pl.when(pl.program_id(0) == 0)
        def _():
            dw_ref[...] = jnp.zeros_like(dw_ref)

        for gi in range(SSD_GROUPS):
            sl = slice(gi * gw, (gi + 1) * gw)
            vg, dg = v[:, sl], dv[:, sl]
            r = lax.rsqrt(jnp.mean(vg * vg, axis=-1, keepdims=True) + NORM_EPS)
            vh = vg * r
            gg = dg * w_ref[:, sl]
            dvg = r * (gg - vh * jnp.mean(gg * vh, axis=-1, keepdims=True))
            dy_ref[:, sl] = dvg * sz[:, sl]
            dz_ref[:, sl] = (dvg * yv[:, sl] * _dsilu(zv[:, sl])).astype(BF16)
            dw_ref[:, sl] += jnp.sum(dg * vh, axis=0, keepdims=True)

    blk = pl.BlockSpec((tr, D_SSM), lambda i: (i, 0))
    row = pl.BlockSpec((1, D_SSM), lambda i: (0, 0))
    return pl.pallas_call(
        body, name=name, grid=(t // tr,), in_specs=[blk, blk, row, blk], out_specs=[blk, blk, row],
        out_shape=[jax.ShapeDtypeStruct((t, D_SSM), F32), jax.ShapeDtypeStruct((t, D_SSM), BF16),
                   jax.ShapeDtypeStruct((1, D_SSM), F32)],
        compiler_params=_cp("arbitrary"),
    )(y, proj, w, dcat)


def _pad_lanes(v):
    return jnp.pad(v, ((0, 0), (0, LANES - v.shape[1])))


def _per_head(v128, t):
    return jnp.broadcast_to(v128[:, :SSD_HEADS].T[:, :, None], (SSD_HEADS, t, SSD_P))


def _ssd_forward(proj, conv_w, conv_b, dt_bias, a_log, d_skip, ssd_norm_w):
    t = proj.shape[0]
    nc = t // CHUNK
    xbc = _conv_act_fwd(proj, conv_w, conv_b, kw=SSD_CONV, glu=False, tc=512, coff=OFF_XBC // 512,
                        ncols=SSD_CONV_DIM, out_dtype=F32, name="ssd_conv_fwd")
    bias128, alog128 = _pad_lanes(dt_bias), _pad_lanes(a_log)
    dt128, cs128, a128 = _ssd_prep(proj, bias128, alog128, name="ssd_prep")
    cs_row = cs128[:, :SSD_HEADS].T.reshape(SSD_HEADS, nc, 1, CHUNK)
    dskip_h = jnp.broadcast_to(d_skip[0][:, None, None], (SSD_HEADS, 1, SSD_P))
    a_h = jnp.broadcast_to(a128[0, :SSD_HEADS][:, None, None], (SSD_HEADS, 1, SSD_P))
    y, states = _ssd_fwd(xbc, dt128, cs128, cs_row, dskip_h, name="ssd_scan_fwd")
    y_ssd = _ssd_gate_fwd(y, proj, ssd_norm_w, name="ssd_gate_fwd")
    saved = (proj, conv_w, conv_b, ssd_norm_w, bias128, dt128, a128, cs128, cs_row, xbc, dskip_h, a_h, states, y)
    return y_ssd, saved


def _ssd_backward(saved, dcat):
    proj, conv_w, conv_b, ssd_norm_w, bias128, dt128, a128, cs128, cs_row, xbc, dskip_h, a_h, states, y = saved
    dy, dz, d_norm_w = _ssd_gate_bwd(y, proj, ssd_norm_w, dcat, name="ssd_gate_bwd")
    dxc, ddt128, dadt128, dd_h = _ssd_bwd(xbc, dt128, cs128, cs_row, dskip_h, a_h, states, dy, name="ssd_scan_bwd")
    dxbc, d_conv_w, d_conv_b = _conv_act_bwd(proj, conv_w, conv_b, dxc, kw=SSD_CONV, glu=False, tc=512,
                                             coff=OFF_XBC // 512, ncols=SSD_CONV_DIM, name="ssd_conv_bwd")
    d_raw, d_bias, d_alog, d_dskip = _ssd_prep_bwd(ddt128, dadt128, proj, bias128, dt128, a128,
                                                   dd_h.reshape(SSD_HEADS, SSD_P), name="ssd_prep_bwd")
    return (dz, dxbc, d_raw, d_norm_w, d_conv_w, d_conv_b, d_bias[:, :SSD_HEADS], d_alog[:, :SSD_HEADS],
            d_dskip.reshape(1, SSD_HEADS))


def _rope_tables(positions):
    inv_freq = ROPE_THETA ** (-jnp.arange(0, MLA_ROPE, 2, dtype=F32) / MLA_ROPE)
    ang = positions[0].astype(F32)[:, None] * inv_freq
    cos, sin = jnp.cos(ang), jnp.sin(ang)
    z = jnp.zeros_like(cos)
    return jnp.stack([jnp.concatenate([cos, cos, z, z], axis=1), jnp.concatenate([-sin, z, z, z], axis=1),
                      jnp.concatenate([z, sin, z, z], axis=1)])


def _latent_norms(proj, q_w, kv_w, *, tr=256, name):
    t = proj.shape[0]

    def body(q_ref, kv_ref, qw_ref, kvw_ref, qo_ref, kvo_ref):
        for x_ref, w_ref, o_ref in ((q_ref, qw_ref, qo_ref), (kv_ref, kvw_ref, kvo_ref)):
            xv = x_ref[...]
            r = lax.rsqrt(jnp.mean(xv * xv, axis=-1, keepdims=True) + NORM_EPS)
            o_ref[...] = (xv * r * w_ref[...]).astype(BF16)

    return pl.pallas_call(
        body, name=name, grid=(t // tr,),
        in_specs=[pl.BlockSpec((tr, MLA_Q_RANK), lambda i: (i, OFF_QA // MLA_Q_RANK)),
                  pl.BlockSpec((tr, MLA_KV_RANK), lambda i: (i, OFF_CKV // MLA_KV_RANK)),
                  pl.BlockSpec((1, MLA_Q_RANK), lambda i: (0, 0)), pl.BlockSpec((1, MLA_KV_RANK), lambda i: (0, 0))],
        out_specs=[pl.BlockSpec((tr, MLA_Q_RANK), lambda i: (i, 0)), pl.BlockSpec((tr, MLA_KV_RANK), lambda i: (i, 0))],
        out_shape=[jax.ShapeDtypeStruct((t, MLA_Q_RANK), BF16), jax.ShapeDtypeStruct((t, MLA_KV_RANK), BF16)],
        compiler_params=_cp("parallel"),
    )(proj, proj, q_w, kv_w)


def _latent_norms_bwd(proj, q_w, kv_w, dqn, dkvn, *, tr=256, name):
    t = proj.shape[0]

    def body(q_ref, kv_ref, qw_ref, kvw_ref, dq_ref, dkv_ref, dqa_ref, dqw_ref, dckv_ref, dkvw_ref):
        first = pl.program_id(0) == 0
        for x_ref, w_ref, dy_ref, dx_ref, dw_ref in ((q_ref, qw_ref, dq_ref, dqa_ref, dqw_ref),
                                                     (kv_ref, kvw_ref, dkv_ref, dckv_ref, dkvw_ref)):
            xv, dyv = x_ref[...], dy_ref[...]
            r = lax.rsqrt(jnp.mean(xv * xv, axis=-1, keepdims=True) + NORM_EPS)
            xh = xv * r
            g = dyv * w_ref[...]
            dx_ref[...] = (r * (g - xh * jnp.mean(g * xh, axis=-1, keepdims=True))).astype(BF16)

            @pl.when(first)
            def _(dw_ref=dw_ref):
                dw_ref[...] = jnp.zeros_like(dw_ref)

            dw_ref[...] += jnp.sum(dyv * xh, axis=0, keepdims=True)

    def rows(width, cblk=0):
        return pl.BlockSpec((tr, width), lambda i: (i, cblk))

    def row(width):
        return pl.BlockSpec((1, width), lambda i: (0, 0))

    return pl.pallas_call(
        body, name=name, grid=(t // tr,),
        in_specs=[rows(MLA_Q_RANK, OFF_QA // MLA_Q_RANK), rows(MLA_KV_RANK, OFF_CKV // MLA_KV_RANK), row(MLA_Q_RANK),
                  row(MLA_KV_RANK), rows(MLA_Q_RANK), rows(MLA_KV_RANK)],
        out_specs=[rows(MLA_Q_RANK), row(MLA_Q_RANK), rows(MLA_KV_RANK), row(MLA_KV_RANK)],
        out_shape=[jax.ShapeDtypeStruct((t, MLA_Q_RANK), BF16), jax.ShapeDtypeStruct((1, MLA_Q_RANK), F32),
                   jax.ShapeDtypeStruct((t, MLA_KV_RANK), BF16), jax.ShapeDtypeStruct((1, MLA_KV_RANK), F32)],
        compiler_params=_cp("arbitrary"),
    )(proj, proj, q_w, kv_w, dqn, dkvn)


def _mla_forward(proj, tabs, q_a_norm_w, wq_pad, kv_a_norm_w, wkv):
    qn, kvn = _latent_norms(proj, q_a_norm_w, kv_a_norm_w, name="latent_norms")
    q = _matmul(qn, wq_pad, name="q_b_proj")
    kv = _matmul(kvn, wkv, name="kv_b_proj")
    q3, k3, v3, vt4 = _mla_prep(q, kv, proj, tabs, name="mla_prep")
    o, lse = _attn_fwd(q3, k3, vt4, name="attn_fwd")
    return o, (proj, tabs, q_a_norm_w, wq_pad, kv_a_norm_w, wkv, qn, kvn, q3, k3, v3, o, lse)


def _mla_backward(saved, dcat):
    proj, tabs, q_a_norm_w, wq_pad, kv_a_norm_w, wkv, qn, kvn, q3, k3, v3, o, lse = saved
    dq3, dk3, dv3 = _attn_bwd(q3, k3, v3, o, dcat, lse, name="attn_bwd")
    dq, dkv, dkr = _mla_unprep(dq3, dk3, dv3, tabs, name="mla_unprep")
    d_wq = _matmul(qn, dq, ta=True, out_dtype=BF16, name="d_w_q_b")
    dqn = _matmul(dq, wq_pad, tb=True, name="d_qn")
    d_wkv = _matmul(kvn, dkv, ta=True, out_dtype=BF16, name="d_w_kv_b")
    dkvn = _matmul(dkv, wkv, tb=True, name="d_kvn")
    dq_a, d_qnw, dckv, d_kvnw = _latent_norms_bwd(proj, q_a_norm_w, kv_a_norm_w, dqn, dkvn, name="latent_norms_bwd")
    return dq_a, dckv, dkr, d_wq, d_wkv, d_qnw, d_kvnw


def _pad_w_q(w):
    r = w.shape[0]
    w3 = w.reshape(r, MLA_HEADS, MLA_NOPE + MLA_ROPE)
    return jnp.pad(w3, ((0, 0), (0, 0), (0, MLA_QK_PAD - MLA_NOPE - MLA_ROPE))).reshape(r, MLA_HEADS * MLA_QK_PAD)


def _unpad_w_q(w):
    r = w.shape[0]
    return w.reshape(r, MLA_HEADS, MLA_QK_PAD)[:, :, :MLA_NOPE + MLA_ROPE].reshape(r, MLA_HEADS * (MLA_NOPE + MLA_ROPE))


W_IN_SEGMENTS = ((0, D_SSM + SSD_CONV_DIM, 0), (D_SSM + SSD_CONV_DIM, D_SSM + SSD_CONV_DIM + SSD_HEADS, OFF_DT),
                 (D_SSM + SSD_CONV_DIM + SSD_HEADS, D_IN - MLA_ROPE, OFF_QA), (D_IN - MLA_ROPE, D_IN, OFF_KR))


def _pad_w_in_shards(g):
    n = g.shape[2]
    pieces, at = [], 0
    for lo, hi, start in sorted(W_IN_SEGMENTS, key=lambda seg: seg[2]):
        if start > at:
            pieces.append(jnp.zeros((g.shape[1], start - at), g.dtype))
        for j in range(N_DEV):
            a, b = max(lo, j * n), min(hi, (j + 1) * n)
            if a < b:
                pieces.append(g[j][:, a - j * n:b - j * n])
        at = start + hi - lo
    pieces.append(jnp.zeros((g.shape[1], D_IN_PAD - at), g.dtype))
    return jnp.concatenate(pieces, axis=1)


def _unpad_w_in_shards(w):
    n = D_IN // N_DEV
    shards = []
    for j in range(N_DEV):
        pieces = []
        for lo, hi, start in W_IN_SEGMENTS:
            a, b = max(lo, j * n), min(hi, (j + 1) * n)
            if a < b:
                pieces.append(w[:, start + a - lo:start + b - lo])
        shards.append(jnp.concatenate(pieces, axis=1) if len(pieces) > 1 else pieces[0])
    return jnp.stack(shards)


WEIGHTS = ['mix_norm_w', 'w_in', 'conv_w', 'conv_b', 'dt_bias', 'a_log', 'd_skip', 'ssd_norm_w', 'q_a_norm_w', 'w_q_b',
           'kv_a_norm_w', 'w_kv_b', 'w_out', 'ffn_norm_w', 'w_ffn_up', 'ffn_conv_w', 'ffn_conv_b', 'w_ffn_down',
           'ple_norm_w', 'w_ple_gate', 'b_ple_gate', 'w_ple_proj', 'ple_post_norm_w', 'final_norm_w']
BIG = ['w_in', 'w_q_b', 'w_kv_b', 'w_out', 'w_ffn_up', 'w_ffn_down', 'w_ple_gate', 'w_ple_proj']
COL_SHARDED = ('w_in', 'w_q_b', 'w_kv_b', 'w_ffn_up', 'w_ple_proj')
CONV = ['conv_w', 'ffn_conv_w']
REPL = [n for n in WEIGHTS if n not in BIG and n not in CONV]
FFN_INV = tuple(int(i) for i in np.argsort(FFN_PERM))


def _cat_cols(g):
    return jnp.concatenate([g[j] for j in range(N_DEV)], axis=1)


def _split_cols(w):
    n = w.shape[1] // N_DEV
    return jnp.stack([w[:, j * n:(j + 1) * n] for j in range(N_DEV)])


def _interleave(v):
    r = v.shape[0]
    return v.reshape(r, N_DEV, FFN_TC)[:, jnp.array(FFN_PERM)].reshape(r, N_DEV * FFN_TC)


def _deinterleave(v):
    r = v.shape[0]
    return v.reshape(r, N_DEV, FFN_TC)[:, jnp.array(FFN_INV)].reshape(r, N_DEV * FFN_TC)


def _assemble_weights(g):
    layout = {
        'w_in': _pad_w_in_shards,
        'w_q_b': lambda v: _pad_w_q(_cat_cols(v)),
        'w_kv_b': _cat_cols,
        'w_out': lambda v: v.reshape(D_MODEL, D_MODEL),
        'w_ffn_up': lambda v: v,
        'w_ffn_down': lambda v: v.reshape(D_FF, D_MODEL),
        'w_ple_gate': lambda v: v.reshape(D_MODEL, D_MODEL),
        'w_ple_proj': _cat_cols,
        'conv_w': _cat_cols,
        'ffn_conv_w': lambda v: _interleave(_cat_cols(v)),
    }
    return {n: layout[n](v) for n, v in g.items()}


WEIGHT_GROUPS = {'a': ['w_in', 'w_q_b', 'w_kv_b', 'conv_w'], 'b': ['w_out', 'w_ffn_up', 'ffn_conv_w'],
                 'c': ['w_ffn_down', 'w_ple_gate', 'w_ple_proj']}
GRAD_GROUPS = {'p': ['w_ple_proj', 'w_ple_gate', 'w_ffn_down'], 'r': ['w_ffn_up'], 's': ['w_out'],
               't': ['w_q_b', 'w_kv_b', 'w_in']}


def _ffn_perm(j):
    return (j % 2) * (N_DEV // 2) + j // 2


def _local_step(x, p, tabs, get_w, s, target, emit, relay, settle):
    t = x.shape[0]
    s = dict(s)
    half = D_MODEL // 2
    up_cols = 2 * D_FF
    ffn_conv_b = _interleave(s['ffn_conv_b'])
    w = dict(get_w('a', None))
    h = _rmsnorm_fwd(x, s['mix_norm_w'], width=D_MODEL, name="mix_norm")
    proj = _matmul(h, w['w_in'], name="in_proj")
    y_ssd, ssd_saved = _ssd_forward(proj, w['conv_w'], s['conv_b'], s['dt_bias'], s['a_log'], s['d_skip'],
                                    s['ssd_norm_w'])
    o, mla_saved = _mla_forward(proj, tabs, s['q_a_norm_w'], w['w_q_b'], s['kv_a_norm_w'], w['w_kv_b'])
    tk_o, tn_o = _tile(half, MM_TK), _tile(D_MODEL, MM_TILE)
    w.update(get_w('b', o))
    x1 = _matmul(y_ssd, w['w_out'], add=x, mnk=(t, D_MODEL, half), name="out_proj_ssd")
    x1 = _matmul(o, w['w_out'], add=x1, mnk=(t, D_MODEL, half), name="out_proj_mla",
                 b_spec=pl.BlockSpec((tk_o, tn_o), lambda i, j, kk: (kk + half // tk_o, j)))
    hf = _rmsnorm_fwd(x1, s['ffn_norm_w'], width=D_MODEL, name="ffn_norm")
    tk_u = _tile(D_MODEL, MM_TK)
    u = _matmul(hf, w['w_ffn_up'], mnk=(t, up_cols, D_MODEL), tn=FFN_TC, name="ffn_up",
                b_spec=pl.BlockSpec((1, tk_u, FFN_TC), lambda i, j, kk: (_ffn_perm(j), kk, 0)))
    act = _conv_act_fwd(u, w['ffn_conv_w'], ffn_conv_b, kw=FFN_CONV, glu=True, tc=2 * FFN_TC, coff=0, ncols=up_cols,
                        out_dtype=BF16, tr=128, name="ffn_act")
    w.update(get_w('c', act))
    x2 = _matmul(act, w['w_ffn_down'], add=x1, name="ffn_down")
    hp = _rmsnorm_fwd(x2, s['ple_norm_w'], width=D_MODEL, name="ple_norm")
    gl = _matmul(hp, w['w_ple_gate'], bias=s['b_ple_gate'], name="ple_gate")
    pe = _matmul(p, w['w_ple_proj'], name="ple_proj")
    loss, dx3, d_final, dgl, d_bgate, dpe, d_post = _ple_loss(x2, gl, pe, s['ple_post_norm_w'], s['final_norm_w'],
                                                              target, name="ple_loss")
    d_wproj = _matmul(p, dpe, ta=True, out_dtype=BF16, name="d_w_ple_proj")
    d_wgate = _matmul(hp, dgl, ta=True, out_dtype=BF16, name="d_w_ple_gate")
    dhp = _matmul(dgl, w['w_ple_gate'], tb=True, name="d_ple_normed")
    dx2, d_plenorm, dx2b = _rmsnorm_bwd(x2, s['ple_norm_w'], dhp, dx3, width=D_MODEL, also_bf16=True,
                                        name="ple_norm_bwd")
    dact = _matmul(dx2b, w['w_ffn_down'], tb=True, name="d_ffn_act")
    d_wdown = _matmul(act, dx2b, ta=True, out_dtype=BF16, name="d_w_ffn_down")
    zz = emit('p', {'w_ple_proj': _split_cols(d_wproj), 'w_ple_gate': d_wgate.reshape(N_DEV, D_MODEL // N_DEV, D_MODEL),
                    'w_ffn_down': d_wdown.reshape(N_DEV, D_FF // N_DEV, D_MODEL)})
    du, d_fconv_w, d_fconv_b = _conv_act_bwd(u, w['ffn_conv_w'], ffn_conv_b + zz, dact, kw=FFN_CONV, glu=True,
                                             tc=2 * FFN_TC, coff=0, ncols=up_cols, tr=64, name="ffn_act_bwd")
    zz = zz + relay('p', du)
    tm_u = _tile(D_MODEL, MM_TILE)
    d_wup = _matmul(hf, du, ta=True, out_dtype=BF16, mnk=(D_MODEL, up_cols, t), tn=FFN_TC, name="d_w_ffn_up",
                    o_spec=pl.BlockSpec((1, tm_u, FFN_TC), lambda i, j, kk: (_ffn_perm(j), i, 0)),
                    o_shape=(N_DEV, D_MODEL, FFN_TC))
    zz = zz + emit('r', {'w_ffn_up': d_wup})
    zero_row = jnp.zeros((1, D_MODEL), F32)
    dhf = _matmul(du, w['w_ffn_up'], tb=True, mnk=(t, D_MODEL, up_cols), tm=t, tk=FFN_TC, name="d_ffn_normed",
                  bias=zero_row + zz,
                  b_spec=pl.BlockSpec((1, tn_o, FFN_TC), lambda i, j, kk: (_ffn_perm(kk), j, 0)))
    zz = zz + relay('r', dhf) + settle('p')
    dx1, d_ffnnorm, dx1b = _rmsnorm_bwd(x1, s['ffn_norm_w'] + zz, dhf, dx2, width=D_MODEL, also_bf16=True,
                                        name="ffn_norm_bwd")
    dcat = _matmul(dx1b, w['w_out'], tb=True, name="d_mixed")
    d_wout = jnp.concatenate([_matmul(y_ssd, dx1b, ta=True, out_dtype=BF16, name="d_w_out_ssd"),
                              _matmul(o, dx1b, ta=True, out_dtype=BF16, name="d_w_out_mla")], axis=0)
    zz = zz + emit('s', {'w_out': d_wout.reshape(N_DEV, D_MODEL // N_DEV, D_MODEL)})
    ssd_saved = ssd_saved[:3] + (ssd_saved[3] + zz,) + ssd_saved[4:]
    dz, dxbc, d_raw, d_ssdnorm, d_conv_w, d_conv_b, d_dtb, d_alog, d_dskip = _ssd_backward(ssd_saved, dcat)
    zz = zz + relay('s', dz)
    mla_saved = mla_saved[:-1] + (mla_saved[-1] + zz,)
    dq_a, dckv, dkr, d_wq, d_wkv, d_qnorm, d_kvnorm = _mla_backward(mla_saved, dcat)
    d_raw = (d_raw + settle('r')).astype(BF16)
    dproj = jnp.concatenate([dz, dxbc, dq_a, dckv, dkr, d_raw], axis=1)
    d_win = _matmul(h, dproj, ta=True, out_dtype=BF16, name="d_w_in")
    zz = emit('t', {'w_in': _unpad_w_in_shards(d_win), 'w_q_b': _split_cols(_unpad_w_q(d_wq)),
                    'w_kv_b': _split_cols(d_wkv)}) + settle('s')
    dh = _matmul(dproj, w['w_in'], tb=True, bias=zero_row + zz, name="d_in_normed")
    zz = relay('t', dh)
    dx, d_mixnorm = _rmsnorm_bwd(x, s['mix_norm_w'] + zz, dh, dx1, width=D_MODEL, name="mix_norm_bwd")
    conv = {'conv_w': d_conv_w, 'ffn_conv_w': _deinterleave(d_fconv_w)}
    vec = {
        'mix_norm_w': d_mixnorm, 'conv_b': d_conv_b, 'dt_bias': d_dtb, 'a_log': d_alog, 'd_skip': d_dskip,
        'ssd_norm_w': d_ssdnorm, 'q_a_norm_w': d_qnorm, 'kv_a_norm_w': d_kvnorm, 'ffn_norm_w': d_ffnnorm,
        'ffn_conv_b': _deinterleave(d_fconv_b), 'ple_norm_w': d_plenorm, 'b_ple_gate': d_bgate,
        'ple_post_norm_w': d_post, 'final_norm_w': d_final,
    }
    return loss, dx, conv, vec


MESH = pl.DeviceIdType.MESH
FLIPS = ((0, 0, 1), (1, 0, 0), (0, 1, 0), (1, 1, 0), (1, 0, 1), (0, 1, 1), (1, 1, 1))


def _exchange(items, *, gather, name):
    n = len(items)

    def body(*refs):
        ins, outs = refs[:n], refs[n:2 * n]
        send_sems, recv_sems, local_sems = refs[2 * n:]
        x, y, c = lax.axis_index("x"), lax.axis_index("y"), lax.axis_index("c")
        me = 4 * x + 2 * y + c
        peers = [(jnp.where(fx, 1 - x, x), jnp.where(fy, 1 - y, y), jnp.where(fc, 1 - c, c)) for fx, fy, fc in FLIPS]
        slot = [4 * px + 2 * py + pc for px, py, pc in peers]
        local, sends = [], []
        for wi in range(n):
            cp = pltpu.make_async_copy(ins[wi] if gather else ins[wi].at[me], outs[wi].at[me], local_sems.at[wi])
            cp.start()
            local.append(cp)
            for k, peer in enumerate(peers):
                cp = pltpu.make_async_remote_copy(
                    src_ref=ins[wi] if gather else ins[wi].at[slot[k]], dst_ref=outs[wi].at[me],
                    send_sem=send_sems.at[k, wi], recv_sem=recv_sems.at[k, wi], device_id=peer, device_id_type=MESH)
                cp.start()
                sends.append(cp)
        for wi in range(n):
            for k, peer in enumerate(peers):
                pltpu.make_async_remote_copy(
                    src_ref=outs[wi].at[slot[k]], dst_ref=outs[wi].at[slot[k]], send_sem=send_sems.at[k, wi],
                    recv_sem=recv_sems.at[k, wi], device_id=peer, device_id_type=MESH).wait_recv()
        for cp in sends:
            cp.wait_send()
        for cp in local:
            cp.wait()

    hbm = pl.BlockSpec(memory_space=pltpu.HBM)
    out_shape = [jax.ShapeDtypeStruct(((N_DEV,) + v.shape) if gather else v.shape, v.dtype) for v in items]
    return pl.pallas_call(
        body, name=name, in_specs=[hbm] * n, out_specs=[hbm] * n, out_shape=out_shape,
        scratch_shapes=[pltpu.SemaphoreType.DMA((len(FLIPS), n)), pltpu.SemaphoreType.DMA((len(FLIPS), n)),
                        pltpu.SemaphoreType.DMA((n,))],
    )(*items)


HBM_SPEC = pl.BlockSpec(memory_space=pltpu.HBM)
SEM_SPEC = pl.BlockSpec(memory_space=pltpu.SEMAPHORE)
EFFECT = pltpu.SideEffectType.DATAFLOW_SIDE_EFFECTING


def _split_start(bufs, ncopies, plan, *, name):
    nb = len(bufs)

    def body(*refs):
        send_sems, recv_sems, token = refs[nb], refs[nb + 1], refs[2 * nb + 2]
        for i, (src, dst, peer, _) in enumerate(plan(refs[:nb])):
            pltpu.make_async_remote_copy(src_ref=src, dst_ref=dst, send_sem=send_sems.at[i], recv_sem=recv_sems.at[i],
                                         device_id=peer, device_id_type=MESH).start()
        token[...] = jnp.zeros_like(token)

    res = pl.pallas_call(
        body, name=name, in_specs=[HBM_SPEC] * nb,
        out_specs=[SEM_SPEC, SEM_SPEC] + [HBM_SPEC] * nb + [pl.BlockSpec(memory_space=pltpu.VMEM)],
        out_shape=[pltpu.SemaphoreType.DMA((ncopies,)), pltpu.SemaphoreType.DMA((ncopies,))]
        + [pltpu.HBM(v.shape, v.dtype) for v in bufs] + [jax.ShapeDtypeStruct((HALO, LANES), F32)],
        input_output_aliases={i: 2 + i for i in range(nb)},
        compiler_params=pltpu.CompilerParams(has_side_effects=EFFECT),
    )(*[pltpu.with_memory_space_constraint(v, pltpu.HBM) for v in bufs])
    return (res[0], res[1], list(res[2:2 + nb])), res[2 + nb]


def _split_wait(started, after, plan, local_plan, *, name):
    send_sems, recv_sems, bufs = started
    nb = len(bufs)
    nlocal = len(local_plan(bufs))

    def body(*refs):
        send_sems, recv_sems = refs[nb], refs[nb + 1]
        local_sems = refs[2 * nb + 3]
        local = []
        for j, (src, dst) in enumerate(local_plan(refs[:nb])):
            cp = pltpu.make_async_copy(src, dst, local_sems.at[j])
            cp.start()
            local.append(cp)
        for i, (src, _, peer, incoming) in enumerate(plan(refs[:nb])):
            cp = pltpu.make_async_remote_copy(src_ref=src, dst_ref=incoming, send_sem=send_sems.at[i],
                                              recv_sem=recv_sems.at[i], device_id=peer, device_id_type=MESH)
            cp.wait_send()
            cp.wait_recv()
        for cp in local:
            cp.wait()

    res = pl.pallas_call(
        body, name=name, in_specs=[HBM_SPEC] * nb + [SEM_SPEC, SEM_SPEC, pl.BlockSpec(memory_space=pl.ANY)],
        out_specs=[HBM_SPEC] * nb, out_shape=[pltpu.HBM(v.shape, v.dtype) for v in bufs],
        input_output_aliases={i: i for i in range(nb)},
        scratch_shapes=[pltpu.SemaphoreType.DMA((max(nlocal, 1),))],
        compiler_params=pltpu.CompilerParams(has_side_effects=EFFECT),
    )(*bufs, send_sems, recv_sems, after)
    return list(res)


def _hold(values, after, *, name):
    n = len(values)

    def body(*refs):
        del refs

    return list(pl.pallas_call(
        body, name=name, in_specs=[HBM_SPEC] * n + [pl.BlockSpec(memory_space=pl.ANY)], out_specs=[HBM_SPEC] * n,
        out_shape=[pltpu.HBM(v.shape, v.dtype) for v in values], input_output_aliases={i: i for i in range(n)},
    )(*values, after))


def _place():
    x, y, c = lax.axis_index("x"), lax.axis_index("y"), lax.axis_index("c")
    others = [((1 - x, y, c), 2 * (1 - x) + y), ((x, 1 - y, c), 2 * x + 1 - y), ((1 - x, 1 - y, c), 2 * (1 - x) + 1 - y)]
    return 4 * x + 2 * y + c, 2 * x + y, c, (x, y, 1 - c), others


def _gather1_plan(n):
    def plan(refs):
        me, _, _, sibling, others = _place()
        out = []
        for wi in range(n):
            item, land = refs[wi], refs[n + wi]
            out.append((item, land.at[me], sibling, land.at[me + 1 - 2 * lax.axis_index("c")]))
            for peer, chip in others:
                out.append((item, land.at[me], peer, land.at[2 * chip + lax.axis_index("c")]))
        return out

    return plan


def _gather1_local(n):
    def plan(refs):
        me = _place()[0]
        return [(refs[wi], refs[n + wi].at[me]) for wi in range(n)]

    return plan


def _gather2_plan(n):
    def plan(refs):
        _, _, c, sibling, others = _place()
        out = []
        for wi in range(n):
            land = refs[wi]
            for _, chip in others:
                out.append((land.at[2 * chip + c], land.at[2 * chip + c], sibling, land.at[2 * chip + 1 - c]))
        return out

    return plan


def _gather_start(items, *, name):
    lands = [lax.empty((N_DEV,) + v.shape, v.dtype) for v in items]
    return _split_start(items + lands, 4 * len(items), _gather1_plan(len(items)), name=name)


def _gather_forward(started, after, *, name):
    n = len(started[2]) // 2
    bufs = _split_wait(started, after, _gather1_plan(n), _gather1_local(n), name=name + "_wait")
    return _split_start(bufs[n:], 3 * n, _gather2_plan(n), name=name + "_start")


def _gather_finish(started, after, *, name):
    n = len(started[2])
    return _split_wait(started, after, _gather2_plan(n), lambda refs: [], name=name)


def _handshake(peers):
    barrier = pltpu.get_barrier_semaphore()
    for peer in peers:
        pl.semaphore_signal(barrier, inc=1, device_id=peer, device_id_type=MESH)
    pl.semaphore_wait(barrier, len(peers))


def _remote(src, dst, send_sem, recv_sem, peer):
    return pltpu.make_async_remote_copy(src_ref=src, dst_ref=dst, send_sem=send_sem, recv_sem=recv_sem, device_id=peer,
                                        device_id_type=MESH)


def _sequencer_gather(items, *, collective_id, name):
    n = len(items)
    srcs = [jax.new_ref(v, memory_space=pltpu.MemorySpace.HBM) for v in items]
    lands = [jax.empty_ref(jax.ShapeDtypeStruct((N_DEV,) + v.shape, v.dtype), memory_space=pltpu.MemorySpace.HBM)
             for v in items]
    dma = pltpu.SemaphoreType.DMA

    @pl.kernel(mesh=plsc.ScalarSubcoreMesh(axis_name="sequencer", num_cores=1), name=name,
               scratch_types=(dma((4 * n,)), dma((4 * n,)), dma((3 * n,)), dma((3 * n,)), dma((n,))),
               compiler_params=pltpu.CompilerParams(collective_id=collective_id))
    def launch(send1, recv1, send2, recv2, local_sems):
        _, _, _, sibling, others = _place()
        _handshake([sibling] + [peer for peer, _ in others])
        hop1 = _gather1_plan(n)(srcs + lands)
        hop2 = _gather2_plan(n)(lands)
        local = [pltpu.make_async_copy(src, dst, local_sems.at[j])
                 for j, (src, dst) in enumerate(_gather1_local(n)(srcs + lands))]
        for cp in local:
            cp.start()
        for i, (src, dst, peer, _) in enumerate(hop1):
            _remote(src, dst, send1.at[i], recv1.at[i], peer).start()
        for wi in range(n):
            for j in range(3):
                i1, i2 = 4 * wi + 1 + j, 3 * wi + j
                src, _, peer, incoming = hop1[i1]
                _remote(src, incoming, send1.at[i1], recv1.at[i1], peer).wait_recv()
                src, dst, peer, _ = hop2[i2]
                _remote(src, dst, send2.at[i2], recv2.at[i2], peer).start()
        for wi in range(n):
            src, _, peer, incoming = hop1[4 * wi]
            _remote(src, incoming, send1.at[4 * wi], recv1.at[4 * wi], peer).wait_recv()
        for i, (src, _, peer, incoming) in enumerate(hop2):
            cp = _remote(src, incoming, send2.at[i], recv2.at[i], peer)
            cp.wait_send()
            cp.wait_recv()
        for i, (src, dst, peer, _) in enumerate(hop1):
            _remote(src, dst, send1.at[i], recv1.at[i], peer).wait_send()
        for cp in local:
            cp.wait()

    launch()
    return [land[...] for land in lands]


def _sequencer_exchange(sources, land_shapes, ncopies, plan, local_plan, peers, *, collective_id, name):
    srcs = [jax.new_ref(v, memory_space=pltpu.MemorySpace.HBM) for v in sources]
    lands = [jax.empty_ref(s, memory_space=pltpu.MemorySpace.HBM) for s in land_shapes]
    nlocal = len(local_plan(srcs + lands))
    dma = pltpu.SemaphoreType.DMA

    @pl.kernel(mesh=plsc.ScalarSubcoreMesh(axis_name="sequencer", num_cores=1), name=name,
               scratch_types=(dma((ncopies,)), dma((ncopies,)), dma((max(nlocal, 1),))),
               compiler_params=pltpu.CompilerParams(collective_id=collective_id))
    def launch(send_sems, recv_sems, local_sems):
        _handshake(peers(_place()))
        copies = plan(srcs + lands)
        local = [pltpu.make_async_copy(src, dst, local_sems.at[j])
                 for j, (src, dst) in enumerate(local_plan(srcs + lands))]
        for cp in local:
            cp.start()
        for i, (src, dst, peer, _) in enumerate(copies):
            _remote(src, dst, send_sems.at[i], recv_sems.at[i], peer).start()
        for i, (src, _, peer, incoming) in enumerate(copies):
            cp = _remote(src, incoming, send_sems.at[i], recv_sems.at[i], peer)
            cp.wait_send()
            cp.wait_recv()
        for cp in local:
            cp.wait()

    launch()
    return [land[...] for land in lands]


def _sequencer_scatter_hop2(sums, *, collective_id, name):
    n = len(sums)
    shapes = [jax.ShapeDtypeStruct(v.shape, v.dtype) for v in sums]
    return _sequencer_exchange(sums, shapes, 3 * n, _scatter2_plan(n), _scatter2_local(n),
                               lambda place: [peer for peer, _ in place[4]], collective_id=collective_id, name=name)


N_CHIP = N_DEV // 2


def _scatter1_plan(n):
    def plan(refs):
        _, _, c, sibling, _ = _place()
        out = []
        for wi in range(n):
            parts, half = refs[wi], refs[n + wi]
            for chip in range(N_CHIP):
                out.append((parts.at[2 * chip + 1 - c], half.at[chip], sibling, half.at[chip]))
        return out

    return plan


def _scatter2_plan(n):
    def plan(refs):
        _, my_chip, _, _, others = _place()
        out = []
        for wi in range(n):
            sums, recv = refs[wi], refs[n + wi]
            for peer, chip in others:
                out.append((sums.at[chip], recv.at[my_chip], peer, recv.at[chip]))
        return out

    return plan


def _scatter2_local(n):
    def plan(refs):
        my_chip = _place()[1]
        return [(refs[wi].at[my_chip], refs[n + wi].at[my_chip]) for wi in range(n)]

    return plan


def _pair_add(parts, half, core, *, name):
    _, r, c = parts.shape
    tr = max(d for d in range(HALO, 257, HALO) if r % d == 0) if r > 256 else r
    parts4 = parts.reshape(N_CHIP, 2, r, c)

    def body(core_ref, p_ref, h_ref, o_ref):
        o_ref[...] = (p_ref[:, 0].astype(F32) + h_ref[...].astype(F32)).astype(o_ref.dtype)

    return pl.pallas_call(
        body, name=name,
        grid_spec=pltpu.PrefetchScalarGridSpec(
            num_scalar_prefetch=1, grid=(r // tr,),
            in_specs=[pl.BlockSpec((N_CHIP, 1, tr, c), lambda i, core_ref: (0, core_ref[0], i, 0)),
                      pl.BlockSpec((N_CHIP, tr, c), lambda i, core_ref: (0, i, 0))],
            out_specs=pl.BlockSpec((N_CHIP, tr, c), lambda i, core_ref: (0, i, 0))),
        out_shape=jax.ShapeDtypeStruct((N_CHIP, r, c), parts.dtype), compiler_params=_cp("parallel"),
    )(core, parts4, half)


def _scatter_start(parts, *, name):
    halves = [lax.empty((N_CHIP,) + v.shape[1:], v.dtype) for v in parts]
    return _split_start(parts + halves, N_CHIP * len(parts), _scatter1_plan(len(parts)), name=name)


def _adamw(parts, w, m, v, *, name):
    r, c = w.shape
    nparts = parts.shape[0]
    tr = max(d for d in range(HALO, 129, HALO) if r % d == 0) if r > 128 else r

    def body(p_ref, w_ref, m_ref, v_ref, g_ref, d_ref, mo_ref, vo_ref):
        g = p_ref[0].astype(F32)
        for k in range(1, nparts):
            g = g + p_ref[k].astype(F32)
        mn = ADAM_B1 * m_ref[...] + (1.0 - ADAM_B1) * g
        vn = ADAM_B2 * v_ref[...] + (1.0 - ADAM_B2) * (g * g)
        m_hat = mn / (1.0 - ADAM_B1 ** ADAM_STEP)
        v_hat = vn / (1.0 - ADAM_B2 ** ADAM_STEP)
        g_ref[...] = g
        d_ref[...] = -ADAM_LR * (m_hat / (jnp.sqrt(v_hat) + ADAM_EPS) + ADAM_WD * w_ref[...])
        mo_ref[...] = mn
        vo_ref[...] = vn

    blk = pl.BlockSpec((tr, c), lambda i: (i, 0))
    return pl.pallas_call(
        body, name=name, grid=(r // tr,), in_specs=[pl.BlockSpec((nparts, tr, c), lambda i: (0, i, 0)), blk, blk, blk],
        out_specs=[blk] * 4, out_shape=[jax.ShapeDtypeStruct((r, c), F32)] * 4, compiler_params=_cp("parallel"),
    )(parts, w, m, v)


def _pack_rows(vs, rows):
    lead = vs[0].shape[:-1] if vs[0].ndim > 1 else ()
    flat = jnp.concatenate(vs, axis=-1)
    pad = rows * LANES - flat.shape[-1]
    flat = jnp.pad(flat, [(0, 0)] * len(lead) + [(0, pad)])
    return flat.reshape(lead + (rows, LANES))


def kernel(x, p, positions, mix_norm_w, w_in, conv_w, conv_b, dt_bias, a_log, d_skip, ssd_norm_w, q_a_norm_w, w_q_b, kv_a_norm_w, w_kv_b, w_out, ffn_norm_w, w_ffn_up, ffn_conv_w, ffn_conv_b, w_ffn_down, ple_norm_w, w_ple_gate, b_ple_gate, w_ple_proj, ple_post_norm_w, final_norm_w, loss_target, m_mix_norm_w, m_w_in, m_conv_w, m_conv_b, m_dt_bias, m_a_log, m_d_skip, m_ssd_norm_w, m_q_a_norm_w, m_w_q_b, m_kv_a_norm_w, m_w_kv_b, m_w_out, m_ffn_norm_w, m_w_ffn_up, m_ffn_conv_w, m_ffn_conv_b, m_w_ffn_down, m_ple_norm_w, m_w_ple_gate, m_b_ple_gate, m_w_ple_proj, m_ple_post_norm_w, m_final_norm_w, v_mix_norm_w, v_w_in, v_conv_w, v_conv_b, v_dt_bias, v_a_log, v_d_skip, v_ssd_norm_w, v_q_a_norm_w, v_w_q_b, v_kv_a_norm_w, v_w_kv_b, v_w_out, v_ffn_norm_w, v_w_ffn_up, v_ffn_conv_w, v_ffn_conv_b, v_w_ffn_down, v_ple_norm_w, v_w_ple_gate, v_b_ple_gate, v_w_ple_proj, v_ple_post_norm_w, v_final_norm_w):
    given = dict(locals())
    shapes = {n: given[n].shape for n in WEIGHTS}
    w2 = {n: given[n].reshape(given[n].shape[-2:] if n in BIG or n in CONV else (1, -1)) for n in WEIGHTS}
    m2 = {n: given['m_' + n].reshape(w2[n].shape) for n in WEIGHTS}
    v2 = {n: given['v_' + n].reshape(w2[n].shape) for n in WEIGHTS}
    me = 4 * lax.axis_index("x") + 2 * lax.axis_index("y") + lax.axis_index("c")

    core = lax.axis_index("c").astype(jnp.int32).reshape(1)

    def shards(grp, zero):
        return [(w2[n] + zero).astype(BF16) if n in BIG else w2[n] + zero for n in WEIGHT_GROUPS[grp]]

    first, token = _gather_start(shards('a', 0.0), name="gather_a_hop1")
    first, token = _gather_forward(first, token, name="gather_a_hop2")
    zero = token[0, 0]
    later = dict(zip(WEIGHT_GROUPS['b'], _sequencer_gather(shards('b', zero), collective_id=1, name="gather_b")))
    later.update(zip(WEIGHT_GROUPS['c'], _sequencer_gather(shards('c', zero), collective_id=2, name="gather_c")))

    def get_w(grp, after):
        if grp == 'a':
            lands = dict(zip(WEIGHT_GROUPS[grp], _gather_finish(first, token, name="gather_a_done")))
        else:
            names = WEIGHT_GROUPS[grp]
            lands = dict(zip(names, _hold([later[n] for n in names], after, name="gather_" + grp + "_use")))
        return _assemble_weights(lands)

    scatters = {}

    hop_ids = {grp: 2 + 2 * i for i, grp in enumerate(GRAD_GROUPS)}

    def zero_of(arrays):
        return sum(v[(0,) * v.ndim].astype(F32) * 0.0 for v in arrays)

    def emit(grp, grads):
        scatters[grp], tok = _scatter_start([grads[n] for n in GRAD_GROUPS[grp]], name="scatter_" + grp + "_hop1")
        return tok[0, 0]

    def relay(grp, after):
        n = len(GRAD_GROUPS[grp])
        bufs = _split_wait(scatters[grp], after, _scatter1_plan(n), lambda refs: [], name="scatter_" + grp + "_hop1_wait")
        sums = [_pair_add(bufs[i], bufs[n + i], core, name="scatter_%s_add%d" % (grp, i)) for i in range(n)]
        scatters[grp] = _sequencer_scatter_hop2(sums, collective_id=hop_ids[grp] + 1, name="scatter_" + grp + "_hop2")
        return zero_of(sums)

    out_g, out_d, out_m, out_v = {}, {}, {}, {}

    def settle(grp):
        return zero_of(scatters[grp])

    def update(grp, behind=None):
        for n, parts in zip(GRAD_GROUPS[grp], scatters[grp]):
            wn = w2[n] if behind is None else w2[n] + behind
            out_g[n], out_d[n], out_m[n], out_v[n] = _adamw(parts, wn, m2[n], v2[n], name="adamw_" + n)

    vecs = {n: w2[n] for n in REPL}
    vecs['mix_norm_w'] = vecs['mix_norm_w'] + zero
    loss, dx, g_conv, g_vec = _local_step(x[0], p[0, 0], _rope_tables(positions), get_w, vecs, loss_target[0], emit,
                                          relay, settle)
    n_small = sum(g_vec[n].shape[1] for n in REPL) + sum(g_conv[n].size for n in CONV) + 1
    rows_small = -(-n_small // (LANES * HALO)) * HALO
    small = _pack_rows([g_vec[n] for n in REPL] + [g_conv[n].reshape(1, -1) for n in CONV] + [loss], rows_small)

    for grp in list(GRAD_GROUPS)[:-1]:
        update(grp)
    all_small = _exchange([small], gather=True, name="gather_small_grads")[0].reshape(N_DEV, rows_small * LANES)
    update(list(GRAD_GROUPS)[-1], zero_of([all_small]))
    pieces, off = [], 0
    for n in REPL:
        k = g_vec[n].shape[1]
        pieces.append(all_small[:, off:off + k])
        off += k
    for n in CONV:
        kw, cols = g_conv[n].shape
        full = all_small[:, off:off + kw * cols].reshape(N_DEV, kw, cols)
        mine = lax.dynamic_slice_in_dim(full, me * (cols // N_DEV), cols // N_DEV, axis=2)
        pieces.append(mine.reshape(N_DEV, kw * (cols // N_DEV)))
        off += kw * cols
    pieces.append(all_small[:, off:off + 1])
    small_names = REPL + CONV
    n_mine = sum(q.shape[1] for q in pieces)
    rows_mine = -(-n_mine // (LANES * HALO)) * HALO
    zero = jnp.zeros((1, 1), F32)
    packed = [_pack_rows([src[n].reshape(1, -1) for n in small_names] + [zero], rows_mine).reshape(rows_mine, LANES)
              for src in (w2, m2, v2)]
    sg, sd, sm, sv = _adamw(_pack_rows(pieces, rows_mine), *packed, name="adamw_small")
    off = 0
    for n in small_names:
        k = w2[n].size
        for dst, src in ((out_g, sg), (out_d, sd), (out_m, sm), (out_v, sv)):
            dst[n] = src.reshape(-1)[off:off + k].reshape(w2[n].shape)
        off += k
    total_loss = sg.reshape(-1)[off]

    outs = [total_loss, dx[None]]
    for res in (out_g, out_d, out_m, out_v):
        outs += [res[n].reshape(shapes[n]) for n in WEIGHTS]
    return tuple(outs)
```
